```python
import jax, jax.numpy as jnp
from jax import lax
import numpy as np

D_MODEL = 1024
BATCH = 16
SEQ = 2048
DEPTH = 1

N_META = 16
EPS = 1e-6
SSD_HEADS = 16
SSD_HEAD_DIM = 64
SSD_INNER = SSD_HEADS * SSD_HEAD_DIM
SSD_GROUPS = 4
SSD_HPG = SSD_HEADS // SSD_GROUPS
SSD_STATE = 128
SSD_CONV = 4
SSD_CHUNK = 128
SSD_CONV_CH = SSD_INNER + 2 * SSD_GROUPS * SSD_STATE
HG_WIDTH = 1024
HG_EXPAND = 128
HG_HEADS = HG_WIDTH // HG_EXPAND
HG_HEAD_I = HG_WIDTH // HG_HEADS
HG_CHUNK = 16
D_FF = 2816
IN_SIZES = (SSD_INNER, SSD_CONV_CH, SSD_HEADS, HG_WIDTH, HG_WIDTH, HG_WIDTH, HG_WIDTH, D_MODEL, D_MODEL)
IN_TOTAL = sum(IN_SIZES)

kernel_name = "hybrid_ssd_hgrn2_macaron_block"


def _split_points():
    pts, acc = [], 0
    for s in IN_SIZES[:-1]:
        acc += s
        pts.append(acc)
    return pts


def rmsnorm(x, w):
    xf = x.astype(jnp.float32)
    y = xf * lax.rsqrt(jnp.mean(xf * xf, axis=-1, keepdims=True) + EPS)
    return (y * w.astype(jnp.float32)).astype(x.dtype)


def swiglu(x, w_gu, w_down):
    g, u = jnp.split(x @ w_gu, 2, axis=-1)
    return (jax.nn.silu(g) * u) @ w_down


def causal_depthwise_conv(x, w, b):
    y = lax.conv_general_dilated(x, w[:, None, :], window_strides=(1,), padding=[(w.shape[0] - 1, 0)],
                                 dimension_numbers=("NWC", "WIO", "NWC"), feature_group_count=x.shape[-1])
    return y + b


def segsum_exp(a):
    T = a.shape[-1]
    cs = jnp.cumsum(a, axis=-1)
    mask = jnp.tril(jnp.ones((T, T), dtype=bool))
    return jnp.exp(jnp.where(mask, cs[..., :, None] - cs[..., None, :], -jnp.inf))


def ssd_mixer(z, xbc, dt_raw, conv_w, conv_b, dt_bias, a_log, d_skip, norm_w):
    f32 = jnp.float32
    Bsz, L, _ = z.shape
    G, R, P, N, Q = SSD_GROUPS, SSD_HPG, SSD_HEAD_DIM, SSD_STATE, SSD_CHUNK
    xbc = jax.nn.silu(causal_depthwise_conv(xbc, conv_w, conv_b)).astype(f32)
    xs, Bm, Cm = jnp.split(xbc, [SSD_INNER, SSD_INNER + G * N], axis=-1)
    dt = jax.nn.softplus(dt_raw.astype(f32) + dt_bias.astype(f32))
    A = -jnp.exp(a_log.astype(f32))
    pad = (-L) % Q
    padf = lambda t: jnp.pad(t, ((0, 0), (pad, 0)) + ((0, 0),) * (t.ndim - 2))
    Lp = L + pad
    nc = Lp // Q
    x4 = padf(xs).reshape(Bsz, nc, Q, G, R, P)
    dtp = padf(dt).reshape(Bsz, nc, Q, SSD_HEADS)
    Bc = padf(Bm).reshape(Bsz, nc, Q, G, N)
    Cc = padf(Cm).reshape(Bsz, nc, Q, G, N)
    X = x4 * dtp.reshape(Bsz, nc, Q, G, R)[..., None]
    a = (dtp * A).transpose(0, 3, 1, 2)
    a_cs = jnp.cumsum(a, axis=-1)
    Lmat = segsum_exp(a).reshape(Bsz, G, R, nc, Q, Q)
    CB = jnp.einsum("bclgn,bcsgn->bcgls", Cc, Bc)
    y_diag = jnp.einsum("bcgls,bgrcls,bcsgrp->bclgrp", CB, Lmat, X)
    decay_states = jnp.exp(a_cs[..., -1:] - a_cs).reshape(Bsz, G, R, nc, Q)
    states = jnp.einsum("bclgn,bgrcl,bclgrp->cbgrpn", Bc, decay_states, X)
    chunk_decay = jnp.moveaxis(jnp.exp(a_cs[..., -1]).reshape(Bsz, G, R, nc), -1, 0)

    def step(hs, inp):
        s, dec = inp
        return hs * dec[..., None, None] + s, hs

    _, prev = lax.scan(step, jnp.zeros((Bsz, G, R, P, N), f32), (states, chunk_decay))
    y_off = jnp.einsum("bclgn,cbgrpn,bgrcl->bclgrp", Cc, prev, jnp.exp(a_cs).reshape(Bsz, G, R, nc, Q))
    y = y_diag + y_off + x4 * d_skip.astype(f32).reshape(G, R)[:, :, None]
    y = y.reshape(Bsz, Lp, SSD_INNER)[:, pad:]
    yg = (y * jax.nn.silu(z.astype(f32))).reshape(Bsz, L, G, SSD_INNER // G)
    yg = yg * lax.rsqrt(jnp.mean(yg * yg, axis=-1, keepdims=True) + EPS)
    return (yg.reshape(Bsz, L, SSD_INNER) * norm_w.astype(f32)).astype(z.dtype)


def hgrn2_mixer(q, f_logit, i_in, g_out, lb, norm_w):
    f32 = jnp.float32
    Bsz, L, _ = q.shape
    H, K, V, C = HG_HEADS, HG_EXPAND, HG_HEAD_I, HG_CHUNK
    nc = L // C
    f = lb + (1.0 - lb) * jax.nn.sigmoid(f_logit.astype(f32))
    chunked = lambda t, d: jnp.moveaxis(t.reshape(Bsz, nc, C, H, d), 1, 0)
    qs = chunked(jax.nn.silu(q.astype(f32)), K)
    ks = chunked(1.0 - f, K)
    vs = chunked(i_in.astype(f32), V)
    gs = chunked(jnp.log(f), K)
    tri = jnp.tril(jnp.ones((C, C), dtype=bool))[None, :, :, None, None]

    def step(S, inp):
        qc, kc, vc, gc = inp
        Gc = jnp.cumsum(gc, axis=1)
        o_inter = jnp.einsum("blhk,bhkv->blhv", qc * jnp.exp(Gc), S)
        dec = jnp.exp(jnp.where(tri, Gc[:, :, None] - Gc[:, None, :], -jnp.inf))
        att = jnp.einsum("blhk,bshk,blshk->bhls", qc, kc, dec)
        o = o_inter + jnp.einsum("bhls,bshv->blhv", att, vc)
        G_last = Gc[:, -1]
        S_new = jnp.exp(G_last)[..., None] * S + jnp.einsum(
            "bshk,bshv->bhkv", kc * jnp.exp(G_last[:, None] - Gc), vc)
        return S_new, o

    _, o = lax.scan(step, jnp.zeros((Bsz, H, K, V), f32), (qs, ks, vs, gs))
    o = jnp.moveaxis(o, 0, 1).reshape(Bsz, L, H, V)
    o = o * lax.rsqrt(jnp.mean(o * o, axis=-1, keepdims=True) + EPS) * norm_w.astype(f32).reshape(H, V)
    o = o.reshape(Bsz, L, HG_WIDTH) * jax.nn.silu(g_out.astype(f32))
    return o.astype(q.dtype)


def _fwd_setup_inputs(seed: int = 0) -> dict:
    key = jax.random.key(seed)
    ks = jax.random.split(key, 24)
    nrm = lambda k, shape, s: jax.random.normal(k, shape, jnp.float32) * s
    gain = lambda k, shape: 1.0 + 0.02 * jax.random.normal(k, shape, jnp.float32)
    dt0 = jnp.exp(jax.random.uniform(ks[9], (DEPTH, SSD_HEADS), jnp.float32, np.log(1e-3), np.log(1e-1)))
    return {
        "x": nrm(ks[0], (BATCH, SEQ, D_MODEL), 1.0),
        "meta_tokens": nrm(ks[1], (N_META, D_MODEL), 1.0),
        "ffn1_norm": gain(ks[2], (DEPTH, D_MODEL)),
        "ffn1_w_gu": nrm(ks[3], (DEPTH, D_MODEL, 2 * D_FF), D_MODEL ** -0.5),
        "ffn1_w_down": nrm(ks[4], (DEPTH, D_FF, D_MODEL), D_FF ** -0.5),
        "mix_norm": gain(ks[5], (DEPTH, D_MODEL)),
        "w_in": nrm(ks[6], (DEPTH, D_MODEL, IN_TOTAL), D_MODEL ** -0.5),
        "ssd_conv_w": nrm(ks[7], (DEPTH, SSD_CONV, SSD_CONV_CH), SSD_CONV ** -0.5),
        "ssd_conv_b": nrm(ks[8], (DEPTH, SSD_CONV_CH), 0.01),
        "ssd_dt_bias": dt0 + jnp.log(-jnp.expm1(-dt0)),
        "ssd_a_log": jnp.log(jax.random.uniform(ks[10], (DEPTH, SSD_HEADS), jnp.float32, 1.0, 16.0)),
        "ssd_d": 1.0 + 0.1 * jax.random.normal(ks[11], (DEPTH, SSD_HEADS), jnp.float32),
        "ssd_norm": gain(ks[12], (DEPTH, SSD_INNER)),
        "hg_lower_bound": 1.0 + 0.1 * jax.random.normal(ks[13], (DEPTH + 1, HG_WIDTH), jnp.float32),
        "hg_norm": gain(ks[14], (DEPTH, HG_WIDTH)),
        "w_branch_a": nrm(ks[15], (DEPTH, SSD_INNER, D_MODEL), SSD_INNER ** -0.5),
        "w_branch_b": nrm(ks[16], (DEPTH, HG_WIDTH, D_MODEL), HG_WIDTH ** -0.5),
        "w_out": nrm(ks[17], (DEPTH, D_MODEL, D_MODEL), D_MODEL ** -0.5),
        "ffn2_norm": gain(ks[18], (DEPTH, D_MODEL)),
        "ffn2_w_gu": nrm(ks[19], (DEPTH, D_MODEL, 2 * D_FF), D_MODEL ** -0.5),
        "ffn2_w_down": nrm(ks[20], (DEPTH, D_FF, D_MODEL), D_FF ** -0.5),
        "final_norm": gain(ks[21], (D_MODEL,)),
    }


def _fwd_reference(x, meta_tokens, ffn1_norm, ffn1_w_gu, ffn1_w_down, mix_norm, w_in, ssd_conv_w, ssd_conv_b,
              ssd_dt_bias, ssd_a_log, ssd_d, ssd_norm, hg_lower_bound, hg_norm, w_branch_a, w_branch_b,
              w_out, ffn2_norm, ffn2_w_gu, ffn2_w_down, final_norm):
    Bsz = x.shape[0]
    meta = jnp.broadcast_to(meta_tokens[None].astype(x.dtype), (Bsz, N_META, D_MODEL))
    h = jnp.concatenate([meta, x], axis=1)
    lb_all = jnp.cumsum(jax.nn.softmax(hg_lower_bound.astype(jnp.float32), axis=0), axis=0)
    splits = _split_points()
    for l in range(DEPTH):
        h = h + 0.5 * swiglu(rmsnorm(h, ffn1_norm[l]), ffn1_w_gu[l], ffn1_w_down[l])
        u = rmsnorm(h, mix_norm[l])
        z, xbc, dt_raw, q, f_logit, i_in, g_out, gate_a, gate_b = jnp.split(u @ w_in[l], splits, axis=-1)
        y_a = ssd_mixer(z, xbc, dt_raw, ssd_conv_w[l], ssd_conv_b[l], ssd_dt_bias[l], ssd_a_log[l],
                        ssd_d[l], ssd_norm[l])
        y_b = hgrn2_mixer(q, f_logit, i_in, g_out, lb_all[l], hg_norm[l])
        merged = jax.nn.sigmoid(gate_a) * (y_a @ w_branch_a[l]) + jax.nn.sigmoid(gate_b) * (y_b @ w_branch_b[l])
        h = h + merged @ w_out[l]
        h = h + 0.5 * swiglu(rmsnorm(h, ffn2_norm[l]), ffn2_w_gu[l], ffn2_w_down[l])
    h = rmsnorm(h, final_norm)
    return h[:, N_META:]


import jax as _jax
import jax.numpy as _jnp

TWIN_FORMAT = 'train_step'
FWD_PARAMS = ['x', 'meta_tokens', 'ffn1_norm', 'ffn1_w_gu', 'ffn1_w_down', 'mix_norm', 'w_in', 'ssd_conv_w', 'ssd_conv_b', 'ssd_dt_bias', 'ssd_a_log', 'ssd_d', 'ssd_norm', 'hg_lower_bound', 'hg_norm', 'w_branch_a', 'w_branch_b', 'w_out', 'ffn2_norm', 'ffn2_w_gu', 'ffn2_w_down', 'final_norm']
TWIN_WEIGHTS = ['meta_tokens', 'ffn1_norm', 'ffn1_w_gu', 'ffn1_w_down', 'mix_norm', 'w_in', 'ssd_conv_w', 'ssd_conv_b', 'ssd_dt_bias', 'ssd_a_log', 'ssd_d', 'ssd_norm', 'hg_lower_bound', 'hg_norm', 'w_branch_a', 'w_branch_b', 'w_out', 'ffn2_norm', 'ffn2_w_gu', 'ffn2_w_down', 'final_norm']
TWIN_DIFF_INPUT = 'x'
TWIN_INPUTS = ['x', 'meta_tokens', 'ffn1_norm', 'ffn1_w_gu', 'ffn1_w_down', 'mix_norm', 'w_in', 'ssd_conv_w', 'ssd_conv_b', 'ssd_dt_bias', 'ssd_a_log', 'ssd_d', 'ssd_norm', 'hg_lower_bound', 'hg_norm', 'w_branch_a', 'w_branch_b', 'w_out', 'ffn2_norm', 'ffn2_w_gu', 'ffn2_w_down', 'final_norm', 'loss_target', 'm_meta_tokens', 'm_ffn1_norm', 'm_ffn1_w_gu', 'm_ffn1_w_down', 'm_mix_norm', 'm_w_in', 'm_ssd_conv_w', 'm_ssd_conv_b', 'm_ssd_dt_bias', 'm_ssd_a_log', 'm_ssd_d', 'm_ssd_norm', 'm_hg_lower_bound', 'm_hg_norm', 'm_w_branch_a', 'm_w_branch_b', 'm_w_out', 'm_ffn2_norm', 'm_ffn2_w_gu', 'm_ffn2_w_down', 'm_final_norm', 'v_meta_tokens', 'v_ffn1_norm', 'v_ffn1_w_gu', 'v_ffn1_w_down', 'v_mix_norm', 'v_w_in', 'v_ssd_conv_w', 'v_ssd_conv_b', 'v_ssd_dt_bias', 'v_ssd_a_log', 'v_ssd_d', 'v_ssd_norm', 'v_hg_lower_bound', 'v_hg_norm', 'v_w_branch_a', 'v_w_branch_b', 'v_w_out', 'v_ffn2_norm', 'v_ffn2_w_gu', 'v_ffn2_w_down', 'v_final_norm']
TWIN_OUTPUTS = ['loss', 'grad_x', 'grad_meta_tokens', 'grad_ffn1_norm', 'grad_ffn1_w_gu', 'grad_ffn1_w_down', 'grad_mix_norm', 'grad_w_in', 'grad_ssd_conv_w', 'grad_ssd_conv_b', 'grad_ssd_dt_bias', 'grad_ssd_a_log', 'grad_ssd_d', 'grad_ssd_norm', 'grad_hg_lower_bound', 'grad_hg_norm', 'grad_w_branch_a', 'grad_w_branch_b', 'grad_w_out', 'grad_ffn2_norm', 'grad_ffn2_w_gu', 'grad_ffn2_w_down', 'grad_final_norm', 'delta_meta_tokens', 'delta_ffn1_norm', 'delta_ffn1_w_gu', 'delta_ffn1_w_down', 'delta_mix_norm', 'delta_w_in', 'delta_ssd_conv_w', 'delta_ssd_conv_b', 'delta_ssd_dt_bias', 'delta_ssd_a_log', 'delta_ssd_d', 'delta_ssd_norm', 'delta_hg_lower_bound', 'delta_hg_norm', 'delta_w_branch_a', 'delta_w_branch_b', 'delta_w_out', 'delta_ffn2_norm', 'delta_ffn2_w_gu', 'delta_ffn2_w_down', 'delta_final_norm', 'new_m_meta_tokens', 'new_m_ffn1_norm', 'new_m_ffn1_w_gu', 'new_m_ffn1_w_down', 'new_m_mix_norm', 'new_m_w_in', 'new_m_ssd_conv_w', 'new_m_ssd_conv_b', 'new_m_ssd_dt_bias', 'new_m_ssd_a_log', 'new_m_ssd_d', 'new_m_ssd_norm', 'new_m_hg_lower_bound', 'new_m_hg_norm', 'new_m_w_branch_a', 'new_m_w_branch_b', 'new_m_w_out', 'new_m_ffn2_norm', 'new_m_ffn2_w_gu', 'new_m_ffn2_w_down', 'new_m_final_norm', 'new_v_meta_tokens', 'new_v_ffn1_norm', 'new_v_ffn1_w_gu', 'new_v_ffn1_w_down', 'new_v_mix_norm', 'new_v_w_in', 'new_v_ssd_conv_w', 'new_v_ssd_conv_b', 'new_v_ssd_dt_bias', 'new_v_ssd_a_log', 'new_v_ssd_d', 'new_v_ssd_norm', 'new_v_hg_lower_bound', 'new_v_hg_norm', 'new_v_w_branch_a', 'new_v_w_branch_b', 'new_v_w_out', 'new_v_ffn2_norm', 'new_v_ffn2_w_gu', 'new_v_ffn2_w_down', 'new_v_final_norm']
TWIN_LEAF_KINDS = {'loss': 'loss', 'grad_x': 'grad_x', 'grad_meta_tokens': 'grad_w', 'grad_ffn1_norm': 'grad_w', 'grad_ffn1_w_gu': 'grad_w', 'grad_ffn1_w_down': 'grad_w', 'grad_mix_norm': 'grad_w', 'grad_w_in': 'grad_w', 'grad_ssd_conv_w': 'grad_w', 'grad_ssd_conv_b': 'grad_w', 'grad_ssd_dt_bias': 'grad_w', 'grad_ssd_a_log': 'grad_w', 'grad_ssd_d': 'grad_w', 'grad_ssd_norm': 'grad_w', 'grad_hg_lower_bound': 'grad_w', 'grad_hg_norm': 'grad_w', 'grad_w_branch_a': 'grad_w', 'grad_w_branch_b': 'grad_w', 'grad_w_out': 'grad_w', 'grad_ffn2_norm': 'grad_w', 'grad_ffn2_w_gu': 'grad_w', 'grad_ffn2_w_down': 'grad_w', 'grad_final_norm': 'grad_w', 'delta_meta_tokens': 'delta_w', 'delta_ffn1_norm': 'delta_w', 'delta_ffn1_w_gu': 'delta_w', 'delta_ffn1_w_down': 'delta_w', 'delta_mix_norm': 'delta_w', 'delta_w_in': 'delta_w', 'delta_ssd_conv_w': 'delta_w', 'delta_ssd_conv_b': 'delta_w', 'delta_ssd_dt_bias': 'delta_w', 'delta_ssd_a_log': 'delta_w', 'delta_ssd_d': 'delta_w', 'delta_ssd_norm': 'delta_w', 'delta_hg_lower_bound': 'delta_w', 'delta_hg_norm': 'delta_w', 'delta_w_branch_a': 'delta_w', 'delta_w_branch_b': 'delta_w', 'delta_w_out': 'delta_w', 'delta_ffn2_norm': 'delta_w', 'delta_ffn2_w_gu': 'delta_w', 'delta_ffn2_w_down': 'delta_w', 'delta_final_norm': 'delta_w', 'new_m_meta_tokens': 'new_m', 'new_m_ffn1_norm': 'new_m', 'new_m_ffn1_w_gu': 'new_m', 'new_m_ffn1_w_down': 'new_m', 'new_m_mix_norm': 'new_m', 'new_m_w_in': 'new_m', 'new_m_ssd_conv_w': 'new_m', 'new_m_ssd_conv_b': 'new_m', 'new_m_ssd_dt_bias': 'new_m', 'new_m_ssd_a_log': 'new_m', 'new_m_ssd_d': 'new_m', 'new_m_ssd_norm': 'new_m', 'new_m_hg_lower_bound': 'new_m', 'new_m_hg_norm': 'new_m', 'new_m_w_branch_a': 'new_m', 'new_m_w_branch_b': 'new_m', 'new_m_w_out': 'new_m', 'new_m_ffn2_norm': 'new_m', 'new_m_ffn2_w_gu': 'new_m', 'new_m_ffn2_w_down': 'new_m', 'new_m_final_norm': 'new_m', 'new_v_meta_tokens': 'new_v', 'new_v_ffn1_norm': 'new_v', 'new_v_ffn1_w_gu': 'new_v', 'new_v_ffn1_w_down': 'new_v', 'new_v_mix_norm': 'new_v', 'new_v_w_in': 'new_v', 'new_v_ssd_conv_w': 'new_v', 'new_v_ssd_conv_b': 'new_v', 'new_v_ssd_dt_bias': 'new_v', 'new_v_ssd_a_log': 'new_v', 'new_v_ssd_d': 'new_v', 'new_v_ssd_norm': 'new_v', 'new_v_hg_lower_bound': 'new_v', 'new_v_hg_norm': 'new_v', 'new_v_w_branch_a': 'new_v', 'new_v_w_branch_b': 'new_v', 'new_v_w_out': 'new_v', 'new_v_ffn2_norm': 'new_v', 'new_v_ffn2_w_gu': 'new_v', 'new_v_ffn2_w_down': 'new_v', 'new_v_final_norm': 'new_v'}


def _forward(args):
    return _fwd_reference(*[args[k] for k in FWD_PARAMS])


def _output_shape():
    out = _jax.eval_shape(lambda: _forward(_fwd_setup_inputs(0)))
    return out.shape, out.dtype

N_MICROBATCH = 1
ADAM_LR = 0.001
ADAM_B1 = 0.9
ADAM_B2 = 0.999
ADAM_EPS = 1e-08
ADAM_WD = 0.01
ADAM_STEP = 10
PER_EXAMPLE_BATCH_AXIS = {'x': 0, 'loss_target': 0}
SHARED_INPUTS = []
_WEIGHT_DTYPES = {'meta_tokens': _jnp.float32, 'ffn1_norm': _jnp.float32, 'ffn1_w_gu': _jnp.float32, 'ffn1_w_down': _jnp.float32, 'mix_norm': _jnp.float32, 'w_in': _jnp.float32, 'ssd_conv_w': _jnp.float32, 'ssd_conv_b': _jnp.float32, 'ssd_dt_bias': _jnp.float32, 'ssd_a_log': _jnp.float32, 'ssd_d': _jnp.float32, 'ssd_norm': _jnp.float32, 'hg_lower_bound': _jnp.float32, 'hg_norm': _jnp.float32, 'w_branch_a': _jnp.float32, 'w_branch_b': _jnp.float32, 'w_out': _jnp.float32, 'ffn2_norm': _jnp.float32, 'ffn2_w_gu': _jnp.float32, 'ffn2_w_down': _jnp.float32, 'final_norm': _jnp.float32}
MOMENT_SCALE = {'meta_tokens': 2.544295e-03, 'ffn1_norm': 8.832074e-02, 'ffn1_w_gu': 3.752476e-02, 'ffn1_w_down': 6.133012e-02, 'mix_norm': 1.467097e-01, 'w_in': 4.886541e-02, 'ssd_conv_w': 6.225123e-02, 'ssd_conv_b': 8.075882e-02, 'ssd_dt_bias': 4.156354e-01, 'ssd_a_log': 3.465659e-01, 'ssd_d': 6.733745e-01, 'ssd_norm': 8.096054e-02, 'hg_lower_bound': 4.546841e-03, 'hg_norm': 4.956272e-02, 'w_branch_a': 8.142990e-02, 'w_branch_b': 4.837480e-02, 'w_out': 9.493777e-02, 'ffn2_norm': 6.220967e-02, 'ffn2_w_gu': 2.590583e-02, 'ffn2_w_down': 4.234451e-02, 'final_norm': 3.200092e+01}


def _to_microbatches(a, axis):
    t = _jnp.moveaxis(a, axis, 0)
    t = t.reshape((N_MICROBATCH, t.shape[0] // N_MICROBATCH) + t.shape[1:])
    return _jnp.moveaxis(t, 1, axis + 1)


def setup_inputs(seed: int = 0) -> dict:
    inp = _fwd_setup_inputs(seed)
    key = _jax.random.fold_in(_jax.random.key(seed), 7919)
    shape, _ = _output_shape()
    out = dict(inp)
    out["loss_target"] = _jax.random.normal(_jax.random.fold_in(key, 0), shape, _jnp.float32)
    for i, name in enumerate(TWIN_WEIGHTS):
        w = inp[name].astype(_jnp.float32)
        if MOMENT_SCALE is None:
            s = _jnp.sqrt(_jnp.mean(_jnp.square(w)) + 1e-30)
        else:
            s = MOMENT_SCALE[name]
        km, kv = _jax.random.split(_jax.random.fold_in(key, i + 1))
        out[name] = w
        out["m_" + name] = s * _jax.random.normal(km, w.shape, _jnp.float32)
        out["v_" + name] = (s * s) * _jax.random.uniform(kv, w.shape, _jnp.float32, 0.5, 1.5)
    if N_MICROBATCH > 1:
        for name, axis in PER_EXAMPLE_BATCH_AXIS.items():
            out[name] = _to_microbatches(out[name], axis)
    return {'x': out['x'], 'meta_tokens': out['meta_tokens'], 'ffn1_norm': out['ffn1_norm'], 'ffn1_w_gu': out['ffn1_w_gu'], 'ffn1_w_down': out['ffn1_w_down'], 'mix_norm': out['mix_norm'], 'w_in': out['w_in'], 'ssd_conv_w': out['ssd_conv_w'], 'ssd_conv_b': out['ssd_conv_b'], 'ssd_dt_bias': out['ssd_dt_bias'], 'ssd_a_log': out['ssd_a_log'], 'ssd_d': out['ssd_d'], 'ssd_norm': out['ssd_norm'], 'hg_lower_bound': out['hg_lower_bound'], 'hg_norm': out['hg_norm'], 'w_branch_a': out['w_branch_a'], 'w_branch_b': out['w_branch_b'], 'w_out': out['w_out'], 'ffn2_norm': out['ffn2_norm'], 'ffn2_w_gu': out['ffn2_w_gu'], 'ffn2_w_down': out['ffn2_w_down'], 'final_norm': out['final_norm'], 'loss_target': out['loss_target'], 'm_meta_tokens': out['m_meta_tokens'], 'm_ffn1_norm': out['m_ffn1_norm'], 'm_ffn1_w_gu': out['m_ffn1_w_gu'], 'm_ffn1_w_down': out['m_ffn1_w_down'], 'm_mix_norm': out['m_mix_norm'], 'm_w_in': out['m_w_in'], 'm_ssd_conv_w': out['m_ssd_conv_w'], 'm_ssd_conv_b': out['m_ssd_conv_b'], 'm_ssd_dt_bias': out['m_ssd_dt_bias'], 'm_ssd_a_log': out['m_ssd_a_log'], 'm_ssd_d': out['m_ssd_d'], 'm_ssd_norm': out['m_ssd_norm'], 'm_hg_lower_bound': out['m_hg_lower_bound'], 'm_hg_norm': out['m_hg_norm'], 'm_w_branch_a': out['m_w_branch_a'], 'm_w_branch_b': out['m_w_branch_b'], 'm_w_out': out['m_w_out'], 'm_ffn2_norm': out['m_ffn2_norm'], 'm_ffn2_w_gu': out['m_ffn2_w_gu'], 'm_ffn2_w_down': out['m_ffn2_w_down'], 'm_final_norm': out['m_final_norm'], 'v_meta_tokens': out['v_meta_tokens'], 'v_ffn1_norm': out['v_ffn1_norm'], 'v_ffn1_w_gu': out['v_ffn1_w_gu'], 'v_ffn1_w_down': out['v_ffn1_w_down'], 'v_mix_norm': out['v_mix_norm'], 'v_w_in': out['v_w_in'], 'v_ssd_conv_w': out['v_ssd_conv_w'], 'v_ssd_conv_b': out['v_ssd_conv_b'], 'v_ssd_dt_bias': out['v_ssd_dt_bias'], 'v_ssd_a_log': out['v_ssd_a_log'], 'v_ssd_d': out['v_ssd_d'], 'v_ssd_norm': out['v_ssd_norm'], 'v_hg_lower_bound': out['v_hg_lower_bound'], 'v_hg_norm': out['v_hg_norm'], 'v_w_branch_a': out['v_w_branch_a'], 'v_w_branch_b': out['v_w_branch_b'], 'v_w_out': out['v_w_out'], 'v_ffn2_norm': out['v_ffn2_norm'], 'v_ffn2_w_gu': out['v_ffn2_w_gu'], 'v_ffn2_w_down': out['v_ffn2_w_down'], 'v_final_norm': out['v_final_norm']}


def _loss(weights, diff, rest, loss_target):
    with _jax.named_scope("forward"):
        args = {**rest, TWIN_DIFF_INPUT: diff, **{k: w.astype(_WEIGHT_DTYPES[k]) for k, w in weights.items()}}
        y = _forward(args)
    with _jax.named_scope("loss_head"):
        err = _jnp.square(y.astype(_jnp.float32) - loss_target)
        return 0.5 * _jnp.sum(_jnp.mean(err, axis=-1)) if err.ndim else 0.5 * err


def _adamw(w, g, m, v):
    m = ADAM_B1 * m + (1.0 - ADAM_B1) * g
    v = ADAM_B2 * v + (1.0 - ADAM_B2) * _jnp.square(g)
    m_hat = m / (1.0 - ADAM_B1 ** ADAM_STEP)
    v_hat = v / (1.0 - ADAM_B2 ** ADAM_STEP)
    delta = -ADAM_LR * (m_hat / (_jnp.sqrt(v_hat) + ADAM_EPS) + ADAM_WD * w)
    return delta, m, v


def reference(x, meta_tokens, ffn1_norm, ffn1_w_gu, ffn1_w_down, mix_norm, w_in, ssd_conv_w, ssd_conv_b, ssd_dt_bias, ssd_a_log, ssd_d, ssd_norm, hg_lower_bound, hg_norm, w_branch_a, w_branch_b, w_out, ffn2_norm, ffn2_w_gu, ffn2_w_down, final_norm, loss_target, m_meta_tokens, m_ffn1_norm, m_ffn1_w_gu, m_ffn1_w_down, m_mix_norm, m_w_in, m_ssd_conv_w, m_ssd_conv_b, m_ssd_dt_bias, m_ssd_a_log, m_ssd_d, m_ssd_norm, m_hg_lower_bound, m_hg_norm, m_w_branch_a, m_w_branch_b, m_w_out, m_ffn2_norm, m_ffn2_w_gu, m_ffn2_w_down, m_final_norm, v_meta_tokens, v_ffn1_norm, v_ffn1_w_gu, v_ffn1_w_down, v_mix_norm, v_w_in, v_ssd_conv_w, v_ssd_conv_b, v_ssd_dt_bias, v_ssd_a_log, v_ssd_d, v_ssd_norm, v_hg_lower_bound, v_hg_norm, v_w_branch_a, v_w_branch_b, v_w_out, v_ffn2_norm, v_ffn2_w_gu, v_ffn2_w_down, v_final_norm):
    given = dict(x=x, meta_tokens=meta_tokens, ffn1_norm=ffn1_norm, ffn1_w_gu=ffn1_w_gu, ffn1_w_down=ffn1_w_down, mix_norm=mix_norm, w_in=w_in, ssd_conv_w=ssd_conv_w, ssd_conv_b=ssd_conv_b, ssd_dt_bias=ssd_dt_bias, ssd_a_log=ssd_a_log, ssd_d=ssd_d, ssd_norm=ssd_norm, hg_lower_bound=hg_lower_bound, hg_norm=hg_norm, w_branch_a=w_branch_a, w_branch_b=w_branch_b, w_out=w_out, ffn2_norm=ffn2_norm, ffn2_w_gu=ffn2_w_gu, ffn2_w_down=ffn2_w_down, final_norm=final_norm, loss_target=loss_target, m_meta_tokens=m_meta_tokens, m_ffn1_norm=m_ffn1_norm, m_ffn1_w_gu=m_ffn1_w_gu, m_ffn1_w_down=m_ffn1_w_down, m_mix_norm=m_mix_norm, m_w_in=m_w_in, m_ssd_conv_w=m_ssd_conv_w, m_ssd_conv_b=m_ssd_conv_b, m_ssd_dt_bias=m_ssd_dt_bias, m_ssd_a_log=m_ssd_a_log, m_ssd_d=m_ssd_d, m_ssd_norm=m_ssd_norm, m_hg_lower_bound=m_hg_lower_bound, m_hg_norm=m_hg_norm, m_w_branch_a=m_w_branch_a, m_w_branch_b=m_w_branch_b, m_w_out=m_w_out, m_ffn2_norm=m_ffn2_norm, m_ffn2_w_gu=m_ffn2_w_gu, m_ffn2_w_down=m_ffn2_w_down, m_final_norm=m_final_norm, v_meta_tokens=v_meta_tokens, v_ffn1_norm=v_ffn1_norm, v_ffn1_w_gu=v_ffn1_w_gu, v_ffn1_w_down=v_ffn1_w_down, v_mix_norm=v_mix_norm, v_w_in=v_w_in, v_ssd_conv_w=v_ssd_conv_w, v_ssd_conv_b=v_ssd_conv_b, v_ssd_dt_bias=v_ssd_dt_bias, v_ssd_a_log=v_ssd_a_log, v_ssd_d=v_ssd_d, v_ssd_norm=v_ssd_norm, v_hg_lower_bound=v_hg_lower_bound, v_hg_norm=v_hg_norm, v_w_branch_a=v_w_branch_a, v_w_branch_b=v_w_branch_b, v_w_out=v_w_out, v_ffn2_norm=v_ffn2_norm, v_ffn2_w_gu=v_ffn2_w_gu, v_ffn2_w_down=v_ffn2_w_down, v_final_norm=v_final_norm)
    weights = {n: given[n] for n in TWIN_WEIGHTS}
    shared = {n: given[n] for n in SHARED_INPUTS}
    per_example = {n: given[n] for n in ['x']}
    grad_fn = _jax.value_and_grad(_loss, argnums=(0, 1))

    def one_microbatch(ex, loss_target):
        ex = dict(ex)
        diff = ex.pop(TWIN_DIFF_INPUT)
        return grad_fn(weights, diff, {**shared, **ex}, loss_target)

    if N_MICROBATCH == 1:
        loss, (grad_w, grad_x) = one_microbatch(per_example, given["loss_target"])
    else:
        def body(carry, xs):
            loss_sum, grad_sum = carry
            l_k, (gw_k, gx_k) = one_microbatch(xs[0], xs[1])
            with _jax.named_scope("update"):
                return (loss_sum + l_k, _jax.tree.map(_jnp.add, grad_sum, gw_k)), gx_k

        init = (_jnp.zeros((), _jnp.float32), _jax.tree.map(_jnp.zeros_like, weights))
        (loss, grad_w), grad_x = _jax.lax.scan(body, init, (per_example, given["loss_target"]))
    with _jax.named_scope("update"):
        delta_w, new_m, new_v = {}, {}, {}
        for n in TWIN_WEIGHTS:
            delta_w[n], new_m[n], new_v[n] = _adamw(weights[n], grad_w[n], given["m_" + n], given["v_" + n])
    return (loss, grad_x, *[grad_w[n] for n in TWIN_WEIGHTS], *[delta_w[n] for n in TWIN_WEIGHTS],
            *[new_m[n] for n in TWIN_WEIGHTS], *[new_v[n] for n in TWIN_WEIGHTS])
```

```python
import functools

import jax
import jax.numpy as jnp
from jax import lax
from jax.experimental import pallas as pl
from jax.experimental.pallas import tpu as pltpu

F32 = jnp.float32
BF16 = jnp.bfloat16
HIGHEST = lax.Precision.HIGHEST
MESH = pl.DeviceIdType.MESH

D_MODEL = 1024
N_META = 16
EPS = 1e-6
SSD_HEADS = 16
SSD_HEAD_DIM = 64
SSD_INNER = 1024
SSD_GROUPS = 4
SSD_STATE = 128
SSD_CONV = 4
SSD_CONV_CH = 2048
HG_HEADS = 8
HG_SUB = 32
CHUNK = 128
D_FF = 2816
N_CHIPS = 4
IN_SIZES = (1024, 2048, 16, 1024, 1024, 1024, 1024, 1024, 1024)
ADAM_LR = 0.001
ADAM_B1 = 0.9
ADAM_B2 = 0.999
ADAM_EPS = 1e-08
ADAM_WD = 0.01
ADAM_STEP = 10
VMEM_LIMIT = 56 * 1024 * 1024


def _cparams(sem=None):
    return pltpu.CompilerParams(dimension_semantics=sem, vmem_limit_bytes=VMEM_LIMIT)


def _pick(n, cands):
    for c in cands:
        if n % c == 0:
            return c
    return n


def _dg(a, b, ca, cb):
    return lax.dot_general(a.astype(BF16), b.astype(BF16), (((ca,), (cb,)), ((), ())), preferred_element_type=F32)


@jax.custom_vjp
def _mm(a, b):
    return _dg(a, b, 1, 0)


def _mm_fwd(a, b):
    return _dg(a, b, 1, 0), (a, b)


def _mm_bwd(r, g):
    a, b = r
    return _dg(g, b, 1, 1), _dg(a, g, 0, 0)


_mm.defvjp(_mm_fwd, _mm_bwd)


@jax.custom_vjp
def _mm_nt(a, b):
    return _dg(a, b, 1, 1)


def _mm_nt_fwd(a, b):
    return _dg(a, b, 1, 1), (a, b)


def _mm_nt_bwd(r, g):
    a, b = r
    return _dg(g, b, 1, 0), _dg(g, a, 0, 0)


_mm_nt.defvjp(_mm_nt_fwd, _mm_nt_bwd)


@jax.custom_vjp
def _mm_tn(a, b):
    return _dg(a, b, 0, 0)


def _mm_tn_fwd(a, b):
    return _dg(a, b, 0, 0), (a, b)


def _mm_tn_bwd(r, g):
    a, b = r
    return _dg(b, g, 1, 1), _dg(a, g, 1, 0)


_mm_tn.defvjp(_mm_tn_fwd, _mm_tn_bwd)


def _silu(x):
    return x * jax.nn.sigmoid(x)


def _softplus(x):
    return jnp.maximum(x, 0.0) + jnp.log(1.0 + jnp.exp(-jnp.abs(x)))


def _tril(n):
    ri = lax.broadcasted_iota(jnp.int32, (n, n), 0)
    ci = lax.broadcasted_iota(jnp.int32, (n, n), 1)
    return ri >= ci


def _row_of(m, r):
    sub = lax.broadcasted_iota(jnp.int32, (m.shape[0], 1), 0)
    return jnp.sum(jnp.where(sub == r, m, 0.0), axis=0, keepdims=True)


def _col_of(m, c):
    lane = lax.broadcasted_iota(jnp.int32, (1, m.shape[1]), 1)
    return jnp.sum(jnp.where(lane == c, m, 0.0), axis=1, keepdims=True)


def _matmul(a, b, *, mode, out_dtype, name, alpha=1.0, res=None, tm=None, tn=None, tk=None, out_groups=None):
    b3 = b.ndim == 3
    if mode == "nn":
        M, K = a.shape
        G = b.shape[0] if b3 else 1
        Ng = b.shape[-1]
        N = G * Ng
    elif mode == "nt":
        M, K = a.shape
        G = b.shape[0] if b3 else 1
        N = b.shape[-2]
        Kg = b.shape[-1]
        assert G * Kg == K
    else:
        K, M = a.shape
        N = b.shape[1]
        G = out_groups or 1
        Ng = N // G
    if mode == "tn":
        tm = tm or _pick(M, (1408, 1024, 512, 256, 128))
        tk = tk or _pick(K, (544, 256, 128))
        tn = tn or _pick(Ng, (1408, 1024, 512, 256, 128))
    else:
        tm = tm or _pick(M, (1088, 544, 256, 128))
        if mode == "nn":
            tn = tn or _pick(Ng, (1408, 512, 256, 128))
            tk = K
        else:
            tn = tn or _pick(N, (512, 256, 128))
            tk = tk or (Kg if b3 else K)
    nm, nn_, nk = M // tm, N // tn, K // tk
    assert nm * tm == M and nn_ * tn == N and nk * tk == K, (name, M, N, K, tm, tn, tk)

    if mode == "nn":
        a_spec = pl.BlockSpec((tm, tk), lambda i, j, k: (i, k))
        if b3:
            ns = Ng // tn
            b_spec = pl.BlockSpec((None, tk, tn), lambda i, j, k: (j // ns, k, j % ns))
        else:
            b_spec = pl.BlockSpec((tk, tn), lambda i, j, k: (k, j))
        ca, cb = 1, 0
    elif mode == "nt":
        a_spec = pl.BlockSpec((tm, tk), lambda i, j, k: (i, k))
        if b3:
            ks = Kg // tk
            b_spec = pl.BlockSpec((None, tn, tk), lambda i, j, k: (k // ks, j, k % ks))
        else:
            b_spec = pl.BlockSpec((tn, tk), lambda i, j, k: (j, k))
        ca, cb = 1, 1
    else:
        a_spec = pl.BlockSpec((tk, tm), lambda i, j, k: (k, i))
        b_spec = pl.BlockSpec((tk, tn), lambda i, j, k: (k, j))
        ca, cb = 0, 0
    if mode == "tn" and G > 1:
        ns = Ng // tn
        o_spec = pl.BlockSpec((None, tm, tn), lambda i, j, k: (j // ns, i, j % ns))
        out_shape = jax.ShapeDtypeStruct((G, M, Ng), out_dtype)
    else:
        o_spec = pl.BlockSpec((tm, tn), lambda i, j, k: (i, j))
        out_shape = jax.ShapeDtypeStruct((M, N), out_dtype)
    in_specs = [a_spec, b_spec]
    args = [a, b]
    if res is not None:
        in_specs.append(pl.BlockSpec((tm, tn), lambda i, j, k: (i, j)))
        args.append(res)
    has_res = res is not None

    def body(*refs):
        a_ref, b_ref = refs[0], refs[1]
        o_ref, acc_ref = refs[-2], refs[-1]
        k = pl.program_id(2)

        @pl.when(k == 0)
        def _():
            acc_ref[...] = jnp.zeros_like(acc_ref)

        acc_ref[...] += _dg(a_ref[...], b_ref[...], ca, cb)

        @pl.when(k == nk - 1)
        def _():
            o = acc_ref[...]
            if alpha != 1.0:
                o = o * alpha
            if has_res:
                o = o + refs[2][...]
            o_ref[...] = o.astype(o_ref.dtype)

    return pl.pallas_call(
        body, grid=(nm, nn_, nk), in_specs=in_specs, out_specs=o_spec, out_shape=out_shape,
        scratch_shapes=[pltpu.VMEM((tm, tn), F32)], name=name,
        compiler_params=_cparams(("parallel", "parallel", "arbitrary")),
    )(*args)


def _rms_fn(h, w):
    r = lax.rsqrt(jnp.mean(h * h, axis=-1, keepdims=True) + EPS)
    return h * r * w


def _swiglu_fn(gu):
    g = gu[:, :D_FF].astype(F32)
    u = gu[:, D_FF:].astype(F32)
    return _silu(g) * u


def _merge_fn(pa, pb, gates):
    return jax.nn.sigmoid(gates[:, :D_MODEL]) * pa + jax.nn.sigmoid(gates[:, D_MODEL:]) * pb


def _rows_call(body, *, rows, tr, ins, outs, accs=(), name):
    n = rows // tr
    assert n * tr == rows

    def spec(x):
        if isinstance(x, tuple):
            shp = x[1].shape
            return pl.BlockSpec(shp, lambda i: (0,) * len(shp))
        return pl.BlockSpec((tr, x.shape[1]), lambda i: (i, 0))

    in_specs = [spec(x) for x in ins]
    args = [x[1] if isinstance(x, tuple) else x for x in ins]
    out_specs = [spec(x) for x in outs] + [pl.BlockSpec(x.shape, lambda i: (0,) * len(x.shape)) for x in accs]
    out_shape = [x[1] if isinstance(x, tuple) else x for x in outs] + list(accs)
    return pl.pallas_call(
        body, grid=(n,), in_specs=in_specs, out_specs=out_specs, out_shape=out_shape, name=name,
        compiler_params=_cparams(("arbitrary",)),
    )(*args)


def _acc_rows(ref, val):
    @pl.when(pl.program_id(0) == 0)
    def _():
        ref[...] = jnp.zeros_like(ref)

    ref[0:1, :] += val


def _rms_fwd(h, w, name):
    def body(h_ref, w_ref, o_ref):
        o_ref[...] = _rms_fn(h_ref[...], w_ref[...]).astype(o_ref.dtype)

    R = h.shape[0]
    return _rows_call(body, rows=R, tr=_pick(R, (256, 128)), ins=[h, ("full", w)],
                      outs=[jax.ShapeDtypeStruct(h.shape, BF16)], name=name)[0]


def _rms_bwd(h, w, dn, dres, name):
    def body(h_ref, w_ref, dn_ref, dres_ref, dh_ref, dw_ref):
        _, vjp = jax.vjp(_rms_fn, h_ref[...], w_ref[...])
        dh, dw = vjp(dn_ref[...].astype(F32))
        dh_ref[...] = dh + dres_ref[...]
        _acc_rows(dw_ref, dw)

    R = h.shape[0]
    return _rows_call(body, rows=R, tr=_pick(R, (256, 128)), ins=[h, ("full", w), dn, dres],
                      outs=[jax.ShapeDtypeStruct(h.shape, F32)], accs=[jax.ShapeDtypeStruct((8, D_MODEL), F32)], name=name)


def _swiglu_fwd(gu, name):
    def body(gu_ref, o_ref):
        o_ref[...] = _swiglu_fn(gu_ref[...]).astype(o_ref.dtype)

    R = gu.shape[0]
    return _rows_call(body, rows=R, tr=_pick(R, (256, 128)), ins=[gu],
                      outs=[jax.ShapeDtypeStruct((R, D_FF), BF16)], name=name)[0]


def _swiglu_bwd(gu, da, name):
    def body(gu_ref, da_ref, o_ref):
        _, vjp = jax.vjp(_swiglu_fn, gu_ref[...].astype(F32))
        (dgu,) = vjp(da_ref[...].astype(F32))
        o_ref[...] = dgu.astype(o_ref.dtype)

    R = gu.shape[0]
    return _rows_call(body, rows=R, tr=_pick(R, (256, 128)), ins=[gu, da],
                      outs=[jax.ShapeDtypeStruct(gu.shape, BF16)], name=name)[0]


def _merge_fwd(pa, pb, gates, name):
    def body(pa_ref, pb_ref, g_ref, o_ref):
        o_ref[...] = _merge_fn(pa_ref[...], pb_ref[...], g_ref[...]).astype(o_ref.dtype)

    R = pa.shape[0]
    return _rows_call(body, rows=R, tr=_pick(R, (256, 128)), ins=[pa, pb, gates],
                      outs=[jax.ShapeDtypeStruct(pa.shape, BF16)], name=name)[0]


def _merge_bwd(pa, pb, gates, dm, name):
    def body(pa_ref, pb_ref, g_ref, dm_ref, dpa_ref, dpb_ref, dg_ref):
        _, vjp = jax.vjp(_merge_fn, pa_ref[...], pb_ref[...], g_ref[...])
        dpa, dpb, dg = vjp(dm_ref[...].astype(F32))
        dpa_ref[...] = dpa.astype(dpa_ref.dtype)
        dpb_ref[...] = dpb.astype(dpb_ref.dtype)
        dg_ref[...] = dg.astype(dg_ref.dtype)

    R = pa.shape[0]
    return _rows_call(body, rows=R, tr=_pick(R, (256, 128)), ins=[pa, pb, gates, dm],
                      outs=[jax.ShapeDtypeStruct(pa.shape, BF16), jax.ShapeDtypeStruct(pa.shape, BF16),
                            jax.ShapeDtypeStruct(gates.shape, BF16)], name=name)


def _loss_head(h3, w, target, nseq, name):
    Tp = h3.shape[0] // nseq
    nc = Tp // CHUNK

    def fn(h, w_, t, valid):
        y = _rms_fn(h, w_)
        e = (y - t) * valid
        return 0.5 * jnp.sum(jnp.mean(e * e, axis=-1, keepdims=True))

    def body(h_ref, w_ref, t_ref, loss_ref, dh_ref, dw_ref):
        b, c = pl.program_id(0), pl.program_id(1)
        valid = (c >= 1).astype(F32)
        t = t_ref[...]
        loss, vjp = jax.vjp(lambda h, w_: fn(h, w_, t, valid), h_ref[...], w_ref[...])
        dh, dw = vjp(jnp.ones((), F32))
        dh_ref[...] = dh

        @pl.when((b == 0) & (c == 0))
        def _():
            loss_ref[...] = jnp.zeros_like(loss_ref)
            dw_ref[...] = jnp.zeros_like(dw_ref)

        loss_ref[...] += jnp.full(loss_ref.shape, loss, F32)
        dw_ref[0:1, :] += dw

    return pl.pallas_call(
        body, grid=(nseq, nc),
        in_specs=[pl.BlockSpec((CHUNK, D_MODEL), lambda b, c: (b * nc + c, 0)),
                  pl.BlockSpec((1, D_MODEL), lambda b, c: (0, 0)),
                  pl.BlockSpec((None, CHUNK, D_MODEL), lambda b, c: (b, jnp.maximum(c - 1, 0), 0))],
        out_specs=[pl.BlockSpec((8, 128), lambda b, c: (0, 0)),
                   pl.BlockSpec((CHUNK, D_MODEL), lambda b, c: (b * nc + c, 0)),
                   pl.BlockSpec((8, D_MODEL), lambda b, c: (0, 0))],
        out_shape=[jax.ShapeDtypeStruct((8, 128), F32), jax.ShapeDtypeStruct(h3.shape, F32),
                   jax.ShapeDtypeStruct((8, D_MODEL), F32)],
        name=name, compiler_params=_cparams(("arbitrary", "arbitrary")),
    )(h3, w, target)


CONV_TILE = 512
CONV_HALO = 8


def _conv_fwd(xbc, w, b, pad, name):
    B, Tp, C = xbc.shape
    nch = Tp // CHUNK

    def body(x_ref, w_ref, b_ref, o_ref, xp):
        xp[0:CONV_HALO, :] = jnp.zeros((CONV_HALO, CONV_TILE), F32)
        xp[CONV_HALO:, :] = x_ref[...]
        for c in range(nch):
            acc = jnp.zeros((CHUNK, CONV_TILE), F32) + b_ref[...]
            for k in range(SSD_CONV):
                acc = acc + w_ref[k:k + 1, :] * xp[pl.ds(CONV_HALO + CHUNK * c - (SSD_CONV - 1) + k, CHUNK), :]
            row = CHUNK * c + lax.broadcasted_iota(jnp.int32, (CHUNK, 1), 0)
            o_ref[pl.ds(CHUNK * c, CHUNK), :] = jnp.where(row >= pad, _silu(acc), 0.0)

    return pl.pallas_call(
        body, grid=(B, C // CONV_TILE),
        in_specs=[pl.BlockSpec((None, Tp, CONV_TILE), lambda i, j: (i, 0, j)),
                  pl.BlockSpec((SSD_CONV, CONV_TILE), lambda i, j: (0, j)),
                  pl.BlockSpec((1, CONV_TILE), lambda i, j: (0, j))],
        out_specs=pl.BlockSpec((None, Tp, CONV_TILE), lambda i, j: (i, 0, j)),
        out_shape=jax.ShapeDtypeStruct(xbc.shape, F32),
        scratch_shapes=[pltpu.VMEM((Tp + CONV_HALO, CONV_TILE), F32)],
        name=name, compiler_params=_cparams(("arbitrary", "arbitrary")),
    )(xbc, w, b)


def _conv_bwd(xbc, w, b, dact, pad, name):
    B, Tp, C = xbc.shape
    nch = Tp // CHUNK

    def body(x_ref, w_ref, b_ref, da_ref, dx_ref, dw_ref, db_ref, xp, dp):
        bi = pl.program_id(1)
        xp[0:CONV_HALO, :] = jnp.zeros((CONV_HALO, CONV_TILE), F32)
        xp[CONV_HALO:, :] = x_ref[...]
        dp[pl.ds(Tp, CONV_HALO), :] = jnp.zeros((CONV_HALO, CONV_TILE), F32)
        dws = [jnp.zeros((1, CONV_TILE), F32) for _ in range(SSD_CONV)]
        dbs = jnp.zeros((1, CONV_TILE), F32)
        for c in range(nch):
            xs = [xp[pl.ds(CONV_HALO + CHUNK * c - (SSD_CONV - 1) + k, CHUNK), :] for k in range(SSD_CONV)]
            acc = jnp.zeros((CHUNK, CONV_TILE), F32) + b_ref[...]
            for k in range(SSD_CONV):
                acc = acc + w_ref[k:k + 1, :] * xs[k]
            row = CHUNK * c + lax.broadcasted_iota(jnp.int32, (CHUNK, 1), 0)
            sg = jax.nn.sigmoid(acc)
            dpre = jnp.where(row >= pad, da_ref[pl.ds(CHUNK * c, CHUNK), :] * (sg * (1.0 + acc * (1.0 - sg))), 0.0)
            dp[pl.ds(CHUNK * c, CHUNK), :] = dpre
            dbs = dbs + jnp.sum(dpre, axis=0, keepdims=True)
            for k in range(SSD_CONV):
                dws[k] = dws[k] + jnp.sum(dpre * xs[k], axis=0, keepdims=True)
        for c in range(nch):
            acc = jnp.zeros((CHUNK, CONV_TILE), F32)
            for k in range(SSD_CONV):
                acc = acc + w_ref[k:k + 1, :] * dp[pl.ds(CHUNK * c + (SSD_CONV - 1) - k, CHUNK), :]
            dx_ref[pl.ds(CHUNK * c, CHUNK), :] = acc

        @pl.when(bi == 0)
        def _():
            dw_ref[...] = jnp.zeros_like(dw_ref)
            db_ref[...] = jnp.zeros_like(db_ref)

        for k in range(SSD_CONV):
            dw_ref[k:k + 1, :] += dws[k]
        db_ref[0:1, :] += dbs

    return pl.pallas_call(
        body, grid=(C // CONV_TILE, B),
        in_specs=[pl.BlockSpec((None, Tp, CONV_TILE), lambda j, i: (i, 0, j)),
                  pl.BlockSpec((SSD_CONV, CONV_TILE), lambda j, i: (0, j)),
                  pl.BlockSpec((1, CONV_TILE), lambda j, i: (0, j)),
                  pl.BlockSpec((None, Tp, CONV_TILE), lambda j, i: (i, 0, j))],
        out_specs=[pl.BlockSpec((None, Tp, CONV_TILE), lambda j, i: (i, 0, j)),
                   pl.BlockSpec((8, CONV_TILE), lambda j, i: (0, j)),
                   pl.BlockSpec((8, CONV_TILE), lambda j, i: (0, j))],
        out_shape=[jax.ShapeDtypeStruct(xbc.shape, F32), jax.ShapeDtypeStruct((8, C), F32),
                   jax.ShapeDtypeStruct((8, C), F32)],
        scratch_shapes=[pltpu.VMEM((Tp + CONV_HALO, CONV_TILE), F32), pltpu.VMEM((Tp + CONV_HALO, CONV_TILE), F32)],
        name=name, compiler_params=_cparams(("arbitrary", "arbitrary")),
    )(xbc, w, b, dact)


def _ssd_chunk(xs, bm, cm, dtr, z, state, dt_bias, a_log, dskip, norm_w, valid):
    Q = xs.shape[0]
    lane = lax.broadcasted_iota(jnp.int32, (1, 128), 1)
    dt = jnp.where(lane < SSD_HEADS, _softplus(dtr + dt_bias), 0.0) * valid
    a = dt * (-jnp.exp(a_log))
    tril = _tril(Q)
    cs = jnp.dot(tril.astype(F32), a, precision=HIGHEST)
    cs_t = cs.T
    cs_end = _row_of(cs, Q - 1)
    low = lane < SSD_HEAD_DIM
    low_rows = lax.broadcasted_iota(jnp.int32, (128, 1), 0) < SSD_HEAD_DIM
    ys, new_state = [], []
    for g in range(SSD_GROUPS):
        bg = bm[:, 128 * g:128 * (g + 1)]
        cg = cm[:, 128 * g:128 * (g + 1)]
        cb = _mm_nt(cg, bg)
        for pr in range(2):
            p = 2 * g + pr
            h0, h1 = 2 * p, 2 * p + 1
            xp = xs[:, 128 * p:128 * (p + 1)]
            c0, c1 = _col_of(cs, h0), _col_of(cs, h1)
            e0, e1 = _col_of(cs_end, h0), _col_of(cs_end, h1)
            xd = xp * jnp.where(low, _col_of(dt, h0), _col_of(dt, h1))
            l0 = jnp.exp(jnp.where(tril, c0 - _row_of(cs_t, h0), -1e30))
            l1 = jnp.exp(jnp.where(tril, c1 - _row_of(cs_t, h1), -1e30))
            y_diag = jnp.where(low, _mm(cb * l0, xd), _mm(cb * l1, xd))
            to_end = jnp.where(low, jnp.exp(e0 - c0), jnp.exp(e1 - c1))
            sp = state[128 * p:128 * (p + 1), :]
            y_off = _mm_nt(cg, sp) * jnp.where(low, jnp.exp(c0), jnp.exp(c1))
            new_state.append(sp * jnp.where(low_rows, jnp.exp(e0), jnp.exp(e1)) + _mm_tn(xd * to_end, bg))
            ys.append(y_diag + y_off + xp * jnp.where(low, _col_of(dskip, h0), _col_of(dskip, h1)))
    y = jnp.concatenate(ys, axis=1) * _silu(z)
    gw = SSD_INNER // SSD_GROUPS
    outs = []
    for g in range(SSD_GROUPS):
        blk = y[:, gw * g:gw * (g + 1)]
        outs.append(blk * lax.rsqrt(jnp.mean(blk * blk, axis=-1, keepdims=True) + EPS))
    return jnp.concatenate(outs, axis=1) * norm_w, jnp.concatenate(new_state, axis=0)


def _valid_rows(c, pad):
    row = c * CHUNK + lax.broadcasted_iota(jnp.int32, (CHUNK, 1), 0)
    return (row >= pad).astype(F32)


def _ssd_fwd(xact, dtr, z, dt_bias, a_log, dskip, norm_w, pad, name):
    B, Tp, _ = xact.shape
    nc = Tp // CHUNK

    def body(xs_ref, bm_ref, cm_ref, dt_ref, z_ref, db_ref, al_ref, ds_ref, nw_ref, y_ref, save_ref, st):
        c = pl.program_id(1)

        @pl.when(c == 0)
        def _():
            st[...] = jnp.zeros_like(st)

        s0 = st[...]
        save_ref[...] = s0
        y, s1 = _ssd_chunk(xs_ref[...], bm_ref[...], cm_ref[...], dt_ref[...], z_ref[...], s0, db_ref[...],
                           al_ref[...], ds_ref[...], nw_ref[...], _valid_rows(c, pad))
        y_ref[...] = y.astype(y_ref.dtype)
        st[...] = s1

    row = lambda w, off=0: pl.BlockSpec((None, CHUNK, w), lambda b, c: (b, c, off))
    par = lambda w: pl.BlockSpec((1, w), lambda b, c: (0, 0))
    return pl.pallas_call(
        body, grid=(B, nc),
        in_specs=[row(1024, 0), row(512, 2), row(512, 3), row(128), row(1024), par(128), par(128), par(128), par(1024)],
        out_specs=[row(1024), pl.BlockSpec((None, None, 1024, 128), lambda b, c: (b, c, 0, 0))],
        out_shape=[jax.ShapeDtypeStruct((B, Tp, SSD_INNER), BF16), jax.ShapeDtypeStruct((B, nc, 1024, 128), F32)],
        scratch_shapes=[pltpu.VMEM((1024, 128), F32)],
        name=name, compiler_params=_cparams(("arbitrary", "arbitrary")),
    )(xact, xact, xact, dtr, z, dt_bias, a_log, dskip, norm_w)


def _ssd_bwd(xact, dtr, z, dt_bias, a_log, dskip, norm_w, saved, dy, pad, name):
    B, Tp, _ = xact.shape
    nc = Tp // CHUNK

    def body(xs_ref, bm_ref, cm_ref, dt_ref, z_ref, db_ref, al_ref, ds_ref, nw_ref, sv_ref, dy_ref,
             dxs_ref, dbm_ref, dcm_ref, ddt_ref, dz_ref, dpar_ref, dnw_ref, dst):
        b, i = pl.program_id(0), pl.program_id(1)
        c = nc - 1 - i

        @pl.when(i == 0)
        def _():
            dst[...] = jnp.zeros_like(dst)

        valid = _valid_rows(c, pad)
        fn = lambda *a: _ssd_chunk(*a, valid)
        _, vjp = jax.vjp(fn, xs_ref[...], bm_ref[...], cm_ref[...], dt_ref[...], z_ref[...], sv_ref[...],
                         db_ref[...], al_ref[...], ds_ref[...], nw_ref[...])
        dxs, dbm, dcm, ddt, dz, dstate, ddb, dal, dds, dnw = vjp((dy_ref[...].astype(F32), dst[...]))
        dxs_ref[...] = dxs
        dbm_ref[...] = dbm
        dcm_ref[...] = dcm
        ddt_ref[...] = ddt
        dz_ref[...] = dz
        dst[...] = dstate

        @pl.when((b == 0) & (i == 0))
        def _():
            dpar_ref[...] = jnp.zeros_like(dpar_ref)
            dnw_ref[...] = jnp.zeros_like(dnw_ref)

        dpar_ref[0:1, :] += ddb
        dpar_ref[1:2, :] += dal
        dpar_ref[2:3, :] += dds
        dnw_ref[0:1, :] += dnw

    row = lambda w, off=0: pl.BlockSpec((None, CHUNK, w), lambda b, i: (b, nc - 1 - i, off))
    par = lambda w: pl.BlockSpec((1, w), lambda b, i: (0, 0))
    acc = lambda w: pl.BlockSpec((8, w), lambda b, i: (0, 0))
    outs = pl.pallas_call(
        body, grid=(B, nc),
        in_specs=[row(1024, 0), row(512, 2), row(512, 3), row(128), row(1024), par(128), par(128), par(128), par(1024),
                  pl.BlockSpec((None, None, 1024, 128), lambda b, i: (b, nc - 1 - i, 0, 0)), row(1024)],
        out_specs=[row(1024), row(512), row(512), row(128), row(1024), acc(128), acc(1024)],
        out_shape=[jax.ShapeDtypeStruct((B, Tp, 1024), F32), jax.ShapeDtypeStruct((B, Tp, 512), F32),
                   jax.ShapeDtypeStruct((B, Tp, 512), F32), jax.ShapeDtypeStruct((B, Tp, 128), F32),
                   jax.ShapeDtypeStruct((B, Tp, 1024), F32), jax.ShapeDtypeStruct((8, 128), F32),
                   jax.ShapeDtypeStruct((8, 1024), F32)],
        scratch_shapes=[pltpu.VMEM((1024, 128), F32)],
        name=name, compiler_params=_cparams(("arbitrary", "arbitrary")),
    )(xact, xact, xact, dtr, z, dt_bias, a_log, dskip, norm_w, saved, dy)
    return outs


def _hg_chunk(qr, fr, ir, gr, state_t, p0, p1, norm_w, valid):
    Q = qr.shape[0]
    lb = jax.nn.sigmoid(p0 - p1)
    f = lb + (1.0 - lb) * jax.nn.sigmoid(fr)
    k = 1.0 - f
    q = _silu(qr)
    v = ir * valid
    cum = jnp.dot(_tril(Q).astype(F32), jnp.log(f), precision=HIGHEST)
    cum_end = _row_of(cum, Q - 1)
    o_inter = _mm_nt(q * jnp.exp(cum), state_t)
    tril_s = _tril(HG_SUB)
    outs = []
    for i in range(Q // HG_SUB):
        lo, hi = HG_SUB * i, HG_SUB * (i + 1)
        qi, ci, ki, vi = q[lo:hi], cum[lo:hi], k[lo:hi], v[lo:hi]
        mid = _row_of(cum, lo + HG_SUB // 2 - 1)
        att = jnp.where(tril_s, _mm_nt(qi * jnp.exp(ci - mid), ki * jnp.exp(mid - ci)), 0.0)
        oi = _mm(att, vi)
        if i > 0:
            start = _row_of(cum, lo - 1)
            att_prev = _mm_nt(qi * jnp.exp(ci - start), k[:lo] * jnp.exp(start - cum[:lo]))
            oi = oi + _mm(att_prev, v[:lo])
        outs.append(oi)
    o = o_inter + jnp.concatenate(outs, axis=0)
    new_state_t = state_t * jnp.exp(cum_end) + _mm_tn(v, k * jnp.exp(cum_end - cum))
    o = o * lax.rsqrt(jnp.mean(o * o, axis=-1, keepdims=True) + EPS) * norm_w
    return o * _silu(gr), new_state_t


def _hg_fwd(qfig, lbh, nwh, pad, name):
    B, Tp, _ = qfig.shape
    nc = Tp // CHUNK

    def body(q_ref, f_ref, i_ref, g_ref, lb_ref, nw_ref, y_ref, save_ref, st):
        c = pl.program_id(2)

        @pl.when(c == 0)
        def _():
            st[...] = jnp.zeros_like(st)

        s0 = st[...]
        save_ref[...] = s0
        y, s1 = _hg_chunk(q_ref[...], f_ref[...], i_ref[...], g_ref[...], s0, lb_ref[0:1, :], lb_ref[1:2, :],
                          nw_ref[...], _valid_rows(c, pad))
        y_ref[...] = y.astype(y_ref.dtype)
        st[...] = s1

    part = lambda k: pl.BlockSpec((None, CHUNK, 128), lambda h, b, c: (b, c, k * HG_HEADS + h))
    return pl.pallas_call(
        body, grid=(HG_HEADS, B, nc),
        in_specs=[part(0), part(1), part(2), part(3),
                  pl.BlockSpec((None, 2, 128), lambda h, b, c: (h, 0, 0)),
                  pl.BlockSpec((None, 1, 128), lambda h, b, c: (h, 0, 0))],
        out_specs=[pl.BlockSpec((None, CHUNK, 128), lambda h, b, c: (b, c, h)),
                   pl.BlockSpec((None, None, None, 128, 128), lambda h, b, c: (h, b, c, 0, 0))],
        out_shape=[jax.ShapeDtypeStruct((B, Tp, 1024), BF16), jax.ShapeDtypeStruct((HG_HEADS, B, nc, 128, 128), F32)],
        scratch_shapes=[pltpu.VMEM((128, 128), F32)],
        name=name, compiler_params=_cparams(("arbitrary", "arbitrary", "arbitrary")),
    )(qfig, qfig, qfig, qfig, lbh, nwh)


def _hg_bwd(qfig, lbh, nwh, saved, dy, pad, name):
    B, Tp, _ = qfig.shape
    nc = Tp // CHUNK

    def body(q_ref, f_ref, i_ref, g_ref, lb_ref, nw_ref, sv_ref, dy_ref,
             dq_ref, df_ref, di_ref, dg_ref, dlb_ref, dnw_ref, dst):
        b, i = pl.program_id(1), pl.program_id(2)
        c = nc - 1 - i

        @pl.when(i == 0)
        def _():
            dst[...] = jnp.zeros_like(dst)

        valid = _valid_rows(c, pad)
        fn = lambda *a: _hg_chunk(*a, valid)
        _, vjp = jax.vjp(fn, q_ref[...], f_ref[...], i_ref[...], g_ref[...], sv_ref[...], lb_ref[0:1, :],
                         lb_ref[1:2, :], nw_ref[...])
        dq, df, di, dg, dstate, dp0, dp1, dnw = vjp((dy_ref[...].astype(F32), dst[...]))
        dq_ref[...] = dq.astype(dq_ref.dtype)
        df_ref[...] = df.astype(df_ref.dtype)
        di_ref[...] = di.astype(di_ref.dtype)
        dg_ref[...] = dg.astype(dg_ref.dtype)
        dst[...] = dstate

        @pl.when((b == 0) & (i == 0))
        def _():
            dlb_ref[...] = jnp.zeros_like(dlb_ref)
            dnw_ref[...] = jnp.zeros_like(dnw_ref)

        dlb_ref[0:1, :] += dp0
        dlb_ref[1:2, :] += dp1
        dnw_ref[0:1, :] += dnw

    part = lambda k: pl.BlockSpec((None, CHUNK, 128), lambda h, b, i: (b, nc - 1 - i, k * HG_HEADS + h))
    head = pl.BlockSpec((None, CHUNK, 128), lambda h, b, i: (b, nc - 1 - i, h))
    acc = pl.BlockSpec((None, 8, 128), lambda h, b, i: (h, 0, 0))
    dsh = jax.ShapeDtypeStruct((B, Tp, 1024), BF16)
    return pl.pallas_call(
        body, grid=(HG_HEADS, B, nc),
        in_specs=[part(0), part(1), part(2), part(3),
                  pl.BlockSpec((None, 2, 128), lambda h, b, i: (h, 0, 0)),
                  pl.BlockSpec((None, 1, 128), lambda h, b, i: (h, 0, 0)),
                  pl.BlockSpec((None, None, None, 128, 128), lambda h, b, i: (h, b, nc - 1 - i, 0, 0)), head],
        out_specs=[head, head, head, head, acc, acc],
        out_shape=[dsh, dsh, dsh, dsh, jax.ShapeDtypeStruct((HG_HEADS, 8, 128), F32),
                   jax.ShapeDtypeStruct((HG_HEADS, 8, 128), F32)],
        scratch_shapes=[pltpu.VMEM((128, 128), F32)],
        name=name, compiler_params=_cparams(("arbitrary", "arbitrary", "arbitrary")),
    )(qfig, qfig, qfig, qfig, lbh, nwh, saved, dy)


def _adamw(w, g, m, v, name):
    R, C = w.shape
    tr = _pick(R, (256, 176, 128, 64, 8)) if R > 256 else R

    def body(w_ref, g_ref, m_ref, v_ref, d_ref, mo_ref, vo_ref):
        g_ = g_ref[...]
        m_ = ADAM_B1 * m_ref[...] + (1.0 - ADAM_B1) * g_
        v_ = ADAM_B2 * v_ref[...] + (1.0 - ADAM_B2) * (g_ * g_)
        m_hat = m_ / (1.0 - ADAM_B1 ** ADAM_STEP)
        v_hat = v_ / (1.0 - ADAM_B2 ** ADAM_STEP)
        d_ref[...] = -ADAM_LR * (m_hat / (jnp.sqrt(v_hat) + ADAM_EPS) + ADAM_WD * w_ref[...])
        mo_ref[...] = m_
        vo_ref[...] = v_

    sp = pl.BlockSpec((tr, C), lambda i: (i, 0))
    sh = jax.ShapeDtypeStruct((R, C), F32)
    return pl.pallas_call(body, grid=(R // tr,), in_specs=[sp] * 4, out_specs=[sp] * 3, out_shape=[sh] * 3,
                          name=name, compiler_params=_cparams(("arbitrary",)))(w, g, m, v)


def _ffn_fwd(h, norm_w, w_gu, w_down, tag):
    n = _rms_fwd(h, norm_w, f"{tag}_norm")
    gu = _matmul(n, w_gu, mode="nn", out_dtype=BF16, name=f"{tag}_gu")
    a = _swiglu_fwd(gu, f"{tag}_act")
    out = _matmul(a, w_down, mode="nn", out_dtype=F32, alpha=0.5, res=h, name=f"{tag}_down")
    return out, (n, gu, a)


def _ffn_bwd(h, norm_w, w_gu, w_down, saved, dout, tag):
    n, gu, a = saved
    da = _matmul(dout, w_down, mode="nt", out_dtype=BF16, alpha=0.5, name=f"{tag}_d_act")
    dw_down = _matmul(a, dout, mode="tn", out_dtype=F32, alpha=0.5, name=f"{tag}_dw_down")
    dgu = _swiglu_bwd(gu, da, f"{tag}_d_gu")
    dn = _matmul(dgu, w_gu, mode="nt", out_dtype=F32, name=f"{tag}_d_norm")
    dw_gu = _matmul(n, dgu, mode="tn", out_dtype=F32, out_groups=N_CHIPS, name=f"{tag}_dw_gu")
    dh, dnw = _rms_bwd(h, norm_w, dn, dout, f"{tag}_d_in")
    return dh, dnw, dw_gu, dw_down


IN_NAMES = ("z", "xbc", "dt", "q", "f", "i", "g", "gates")


def _split_w_in(w_in_full):
    pts = [0]
    for s in IN_SIZES:
        pts.append(pts[-1] + s)
    sl = lambda i, j: w_in_full[:, pts[i]:pts[j]]
    return {"z": sl(0, 1), "xbc": sl(1, 2), "dt": jnp.pad(sl(2, 3), ((0, 0), (0, 128 - SSD_HEADS))),
            "qfig": sl(3, 7), "gates": sl(7, 9)}


def _local_step(x, target, W):
    B, S, _ = x.shape
    T = N_META + S
    pad = (-T) % CHUNK
    Tp = T + pad
    assert pad + N_META == CHUNK
    R = B * Tp
    meta = jnp.broadcast_to(W["meta_tokens"][None], (B, N_META, D_MODEL))
    h0 = jnp.concatenate([jnp.zeros((B, pad, D_MODEL), F32), meta, x], axis=1).reshape(R, D_MODEL)

    h1, sv1 = _ffn_fwd(h0, W["ffn1_norm"], W["ffn1_w_gu"], W["ffn1_w_down"], "ffn1")
    um = _rms_fwd(h1, W["mix_norm"], "mix_norm")
    wi = W["w_in"]
    z = _matmul(um, wi["z"], mode="nn", out_dtype=F32, name="in_z")
    xbc = _matmul(um, wi["xbc"], mode="nn", out_dtype=F32, name="in_xbc")
    dtr = _matmul(um, wi["dt"], mode="nn", out_dtype=F32, name="in_dt")
    qfig = _matmul(um, wi["qfig"], mode="nn", out_dtype=F32, name="in_qfig")
    gates = _matmul(um, wi["gates"], mode="nn", out_dtype=F32, name="in_gates")

    r3 = lambda t: t.reshape(B, Tp, t.shape[-1])
    lane_pad = lambda t: jnp.pad(t, ((0, 0), (0, 128 - t.shape[1])))
    dt_bias, a_log, dskip = lane_pad(W["ssd_dt_bias"]), lane_pad(W["ssd_a_log"]), lane_pad(W["ssd_d"])
    xact = _conv_fwd(r3(xbc), W["ssd_conv_w"], W["ssd_conv_b"], pad, "conv_fwd")
    ya, ssd_saved = _ssd_fwd(xact, r3(dtr), r3(z), dt_bias, a_log, dskip, W["ssd_norm"], pad, "ssd_fwd")
    lbh = W["hg_lower_bound"].reshape(2, HG_HEADS, 128).transpose(1, 0, 2)
    nwh = W["hg_norm"].reshape(HG_HEADS, 1, 128)
    yb, hg_saved = _hg_fwd(r3(qfig), lbh, nwh, pad, "hg_fwd")
    ya2, yb2 = ya.reshape(R, -1), yb.reshape(R, -1)
    pa = _matmul(ya2, W["w_branch_a"], mode="nn", out_dtype=F32, name="branch_a")
    pb = _matmul(yb2, W["w_branch_b"], mode="nn", out_dtype=F32, name="branch_b")
    mg = _merge_fwd(pa, pb, gates, "merge")
    h2 = _matmul(mg, W["w_out"], mode="nn", out_dtype=F32, res=h1, name="mix_out")
    h3, sv2 = _ffn_fwd(h2, W["ffn2_norm"], W["ffn2_w_gu"], W["ffn2_w_down"], "ffn2")

    loss, dh3, d_final = _loss_head(h3, W["final_norm"].reshape(1, D_MODEL), target, B, "loss_head")

    G = {"final_norm": d_final[0]}
    dh2, dnw, G["ffn2_w_gu"], G["ffn2_w_down"] = _ffn_bwd(h2, W["ffn2_norm"], W["ffn2_w_gu"], W["ffn2_w_down"], sv2, dh3, "ffn2")
    G["ffn2_norm"] = dnw[0:1]
    dmg = _matmul(dh2, W["w_out"], mode="nt", out_dtype=BF16, name="d_merge")
    G["w_out"] = _matmul(mg, dh2, mode="tn", out_dtype=F32, name="dw_out")
    dpa, dpb, dgates = _merge_bwd(pa, pb, gates, dmg, "merge_bwd")
    dya = _matmul(dpa, W["w_branch_a"], mode="nt", out_dtype=BF16, name="d_ya")
    dyb = _matmul(dpb, W["w_branch_b"], mode="nt", out_dtype=BF16, name="d_yb")
    G["w_branch_a"] = _matmul(ya2, dpa, mode="tn", out_dtype=F32, name="dw_branch_a")
    G["w_branch_b"] = _matmul(yb2, dpb, mode="tn", out_dtype=F32, name="dw_branch_b")

    dxs, dbm, dcm, ddtr, dz, dpar, dnw = _ssd_bwd(xact, r3(dtr), r3(z), dt_bias, a_log, dskip, W["ssd_norm"], ssd_saved,
                                                r3(dya), pad, "ssd_bwd")
    G["ssd_dt_bias"], G["ssd_a_log"], G["ssd_d"] = dpar[0:1, :SSD_HEADS], dpar[1:2, :SSD_HEADS], dpar[2:3, :SSD_HEADS]
    G["ssd_norm"] = dnw[0:1]
    dxact = jnp.concatenate([dxs, dbm, dcm], axis=-1)
    dxbc, dcw, dcb = _conv_bwd(r3(xbc), W["ssd_conv_w"], W["ssd_conv_b"], dxact, pad, "conv_bwd")
    G["ssd_conv_w"], G["ssd_conv_b"] = dcw[0:SSD_CONV], dcb[0:1]
    dq, df, di, dg, dlb, dhn = _hg_bwd(r3(qfig), lbh, nwh, hg_saved, r3(dyb), pad, "hg_bwd")
    G["hg_lower_bound"] = dlb[:, 0:2, :].transpose(1, 0, 2).reshape(2, D_MODEL)
    G["hg_norm"] = dhn[:, 0, :].reshape(1, D_MODEL)

    r2 = lambda t: t.reshape(R, t.shape[-1])
    dqfig = jnp.concatenate([dq, df, di, dg], axis=-1)
    pieces = [("z", r2(dz)), ("xbc", r2(dxbc)), ("dt", r2(ddtr)), ("qfig", r2(dqfig)), ("gates", dgates)]
    dum = None
    dwi = {}
    for nm, dpiece in pieces:
        dum = _matmul(dpiece, wi[nm], mode="nt", out_dtype=F32, res=dum, name=f"d_mix_{nm}")
        dwi[nm] = _matmul(um, dpiece, mode="tn", out_dtype=F32, name=f"dw_in_{nm}")
    G["w_in"] = jnp.concatenate([dwi["z"], dwi["xbc"], dwi["dt"][:, :SSD_HEADS], dwi["qfig"], dwi["gates"]], axis=1)
    dh1, dnw = _rms_bwd(h1, W["mix_norm"], dum, dh2, "mix_norm_bwd")
    G["mix_norm"] = dnw[0:1]
    dh0, dnw, G["ffn1_w_gu"], G["ffn1_w_down"] = _ffn_bwd(h0, W["ffn1_norm"], W["ffn1_w_gu"], W["ffn1_w_down"], sv1, dh1, "ffn1")
    G["ffn1_norm"] = dnw[0:1]
    dh0 = dh0.reshape(B, Tp, D_MODEL)
    G["meta_tokens"] = jnp.sum(dh0[:, pad:CHUNK], axis=0)
    return loss, dh0[:, CHUNK:], G


ANY = pl.BlockSpec(memory_space=pl.ANY)


def _place():
    return lax.axis_index("x"), lax.axis_index("y"), lax.axis_index("c")


def _other_chips(x, y):
    return [(1 - x, y), (x, 1 - y), (1 - x, 1 - y)]


def _remote(src, dst, ssem, rsem, dev):
    return pltpu.make_async_remote_copy(src_ref=src, dst_ref=dst, send_sem=ssem, recv_sem=rsem,
                                        device_id=dev, device_id_type=MESH)


def _exchange8(buf, reduce, name):
    n, w = buf.shape

    def body(x_ref, *rest):
        if reduce:
            red_ref, out_ref, ssem, rsem = rest
        else:
            out_ref, ssem, rsem = rest
        x, y, c = _place()
        me = 4 * x + 2 * y + c
        out_ref[me] = x_ref[...]
        copies = []
        for k in range(1, 8):
            px = 1 - x if (k >> 2) & 1 else x
            py = 1 - y if (k >> 1) & 1 else y
            pc = 1 - c if k & 1 else c
            cp = _remote(x_ref, out_ref.at[me], ssem.at[k - 1], rsem.at[k - 1], (px, py, pc))
            cp.start()
            copies.append((cp, 4 * px + 2 * py + pc))
        for k, (cp, peer) in enumerate(copies):
            _remote(x_ref, out_ref.at[peer], ssem.at[k], rsem.at[k], (x, y, c)).wait_recv()
        for cp, _ in copies:
            cp.wait_send()
        if reduce:
            acc = out_ref[0]
            for d in range(1, 8):
                acc = acc + out_ref[d]
            red_ref[...] = acc

    vm = pl.BlockSpec(memory_space=pltpu.VMEM)
    g_shape = jax.ShapeDtypeStruct((8, n, w), F32)
    if reduce:
        out_shape, out_specs, scratch = [jax.ShapeDtypeStruct((n, w), F32)], [vm], [pltpu.VMEM((8, n, w), F32)]
    else:
        out_shape, out_specs, scratch = [g_shape], [vm], []
    return pl.pallas_call(
        body, in_specs=[vm], out_specs=out_specs, out_shape=out_shape,
        scratch_shapes=scratch + [pltpu.SemaphoreType.DMA((7,)), pltpu.SemaphoreType.DMA((7,))], name=name,
    )(buf)[0]


def _gather_big(shards, name):
    n = len(shards)
    half = [s.shape[0] // 2 for s in shards]

    def body(*refs):
        sh, full = refs[:n], refs[n:2 * n]
        lsem, ssem, rsem, fssem, frsem = refs[2 * n:]
        x, y, c = _place()
        q = 2 * x + y
        chips = _other_chips(x, y)
        piece = lambda s, qq, cc: full[s].at[qq, pl.ds(cc * half[s], half[s])]
        started = []
        for s in range(n):
            cp = pltpu.make_async_copy(sh[s], full[s].at[q], lsem.at[s])
            cp.start()
            started.append(cp)
        sends = []
        for j, (px, py) in enumerate(chips):
            for s in range(n):
                cp = _remote(sh[s].at[pl.ds(c * half[s], half[s])], piece(s, q, c), ssem.at[s, j], rsem.at[s, j], (px, py, c))
                cp.start()
                sends.append(cp)
        for j, (px, py) in enumerate(chips):
            for s in range(n):
                got = piece(s, 2 * px + py, c)
                _remote(got, got, ssem.at[s, j], rsem.at[s, j], (px, py, c)).wait_recv()
                cp = _remote(got, got, fssem.at[s, j], frsem.at[s, j], (x, y, 1 - c))
                cp.start()
                sends.append(cp)
        for j, (px, py) in enumerate(chips):
            for s in range(n):
                got = piece(s, 2 * px + py, 1 - c)
                _remote(got, got, fssem.at[s, j], frsem.at[s, j], (x, y, 1 - c)).wait_recv()
        for cp in sends:
            cp.wait_send()
        for cp in started:
            cp.wait()

    sems = [pltpu.SemaphoreType.DMA((n,))] + [pltpu.SemaphoreType.DMA((n, 3))] * 4
    return pl.pallas_call(
        body, in_specs=[ANY] * n, out_specs=[ANY] * n,
        out_shape=[jax.ShapeDtypeStruct((N_CHIPS,) + s.shape, s.dtype) for s in shards],
        scratch_shapes=sems, name=name,
    )(*shards)


def _pair_swap(parts, name):
    n = len(parts)
    half = [p.shape[1] // 2 for p in parts]

    def body(*refs):
        src, got = refs[:n], refs[n:2 * n]
        ssem, rsem = refs[2 * n:]
        x, y, c = _place()
        copies = []
        for s in range(n):
            cp = _remote(src[s].at[pl.ds(0, N_CHIPS), pl.ds((1 - c) * half[s], half[s])], got[s], ssem.at[s], rsem.at[s], (x, y, 1 - c))
            cp.start()
            copies.append(cp)
        for cp in copies:
            cp.wait_recv()
        for cp in copies:
            cp.wait_send()

    return pl.pallas_call(
        body, in_specs=[ANY] * n, out_specs=[ANY] * n,
        out_shape=[jax.ShapeDtypeStruct((N_CHIPS, h, p.shape[2]), p.dtype) for p, h in zip(parts, half)],
        scratch_shapes=[pltpu.SemaphoreType.DMA((n,)), pltpu.SemaphoreType.DMA((n,))], name=name,
    )(*parts)


def _to_owners(sums, name):
    n = len(sums)

    def body(*refs):
        src, got = refs[:n], refs[n:2 * n]
        lsem, ssem, rsem = refs[2 * n:]
        x, y, c = _place()
        q = 2 * x + y
        chips = _other_chips(x, y)
        started, sends = [], []
        for s in range(n):
            cp = pltpu.make_async_copy(src[s].at[q], got[s].at[q], lsem.at[s])
            cp.start()
            started.append(cp)
        for j, (px, py) in enumerate(chips):
            for s in range(n):
                cp = _remote(src[s].at[2 * px + py], got[s].at[q], ssem.at[s, j], rsem.at[s, j], (px, py, c))
                cp.start()
                sends.append(cp)
        for j, (px, py) in enumerate(chips):
            for s in range(n):
                slot = got[s].at[2 * px + py]
                _remote(slot, slot, ssem.at[s, j], rsem.at[s, j], (px, py, c)).wait_recv()
        for cp in sends:
            cp.wait_send()
        for cp in started:
            cp.wait()

    return pl.pallas_call(
        body, in_specs=[ANY] * n, out_specs=[ANY] * n,
        out_shape=[jax.ShapeDtypeStruct(s.shape, s.dtype) for s in sums],
        scratch_shapes=[pltpu.SemaphoreType.DMA((n,)), pltpu.SemaphoreType.DMA((n, 3)), pltpu.SemaphoreType.DMA((n, 3))],
        name=name,
    )(*sums)


def _pair_join(halves, name):
    n = len(halves)

    def body(*refs):
        src, out = refs[:n], refs[n:2 * n]
        lsem, ssem, rsem = refs[2 * n:]
        x, y, c = _place()
        started, sends = [], []
        for s in range(n):
            h = halves[s].shape[0]
            mine = out[s].at[pl.ds(c * h, h)]
            cp = pltpu.make_async_copy(src[s], mine, lsem.at[s])
            cp.start()
            started.append(cp)
            cp = _remote(src[s], mine, ssem.at[s], rsem.at[s], (x, y, 1 - c))
            cp.start()
            sends.append(cp)
        for s in range(n):
            h = halves[s].shape[0]
            theirs = out[s].at[pl.ds((1 - c) * h, h)]
            _remote(src[s], theirs, ssem.at[s], rsem.at[s], (x, y, 1 - c)).wait_recv()
        for cp in sends:
            cp.wait_send()
        for cp in started:
            cp.wait()

    return pl.pallas_call(
        body, in_specs=[ANY] * n, out_specs=[ANY] * n,
        out_shape=[jax.ShapeDtypeStruct((2 * h.shape[0], h.shape[1]), h.dtype) for h in halves],
        scratch_shapes=[pltpu.SemaphoreType.DMA((n,))] * 3, name=name,
    )(*halves)


def _row_tile(h):
    return _pick(h, (256, 272, 128, 8))


def _add_pair(part, got, c, name):
    _, h, w = got.shape
    tr = _row_tile(h)
    nt = h // tr

    def body(c_ref, p_ref, g_ref, o_ref):
        o_ref[...] = (p_ref[...] + g_ref[...].astype(F32)).astype(o_ref.dtype)

    return pl.pallas_call(
        body,
        grid_spec=pltpu.PrefetchScalarGridSpec(
            num_scalar_prefetch=1, grid=(N_CHIPS, nt),
            in_specs=[pl.BlockSpec((None, tr, w), lambda q, i, c_ref: (q, c_ref[0] * nt + i, 0)),
                      pl.BlockSpec((None, tr, w), lambda q, i, c_ref: (q, i, 0))],
            out_specs=pl.BlockSpec((None, tr, w), lambda q, i, c_ref: (q, i, 0))),
        out_shape=jax.ShapeDtypeStruct(got.shape, F32), name=name,
        compiler_params=_cparams(("arbitrary", "arbitrary")),
    )(c.reshape(1).astype(jnp.int32), part, got)


def _sum_chips(slots, name):
    _, h, w = slots.shape
    tr = _row_tile(h)

    def body(s_ref, o_ref):
        o_ref[...] = ((s_ref[0] + s_ref[1]) + s_ref[2]) + s_ref[3]

    return pl.pallas_call(
        body, grid=(h // tr,), in_specs=[pl.BlockSpec((N_CHIPS, tr, w), lambda i: (0, i, 0))],
        out_specs=pl.BlockSpec((tr, w), lambda i: (i, 0)), out_shape=jax.ShapeDtypeStruct((h, w), F32), name=name,
        compiler_params=_cparams(("arbitrary",)),
    )(slots)


def _reduce_to_owners(parts, c):
    got = _pair_swap(parts, "grad_pair_swap")
    sums = [_add_pair(p, g, c, f"grad_pair_add{i}") for i, (p, g) in enumerate(zip(parts, got))]
    slots = _to_owners(sums, "grad_to_owners")
    halves = [_sum_chips(s, f"grad_sum_chips{i}") for i, s in enumerate(slots)]
    return _pair_join(halves, "grad_pair_join")


WEIGHTS = ("meta_tokens", "ffn1_norm", "ffn1_w_gu", "ffn1_w_down", "mix_norm", "w_in", "ssd_conv_w", "ssd_conv_b",
           "ssd_dt_bias", "ssd_a_log", "ssd_d", "ssd_norm", "hg_lower_bound", "hg_norm", "w_branch_a", "w_branch_b",
           "w_out", "ffn2_norm", "ffn2_w_gu", "ffn2_w_down", "final_norm")
BIG = ("ffn1_w_gu", "ffn1_w_down", "w_in", "w_branch_a", "w_branch_b", "w_out", "ffn2_w_gu", "ffn2_w_down")
ROW_SHARDED = ("ffn1_w_down", "ffn2_w_down", "w_branch_a", "w_branch_b", "w_out")
SMALL = tuple(n for n in WEIGHTS if n not in BIG)
SMALL_ROWS = 24


def _rows1024(a):
    flat = a.reshape(-1)
    n = -(-flat.shape[0] // 1024) * 1024
    return jnp.pad(flat, (0, n - flat.shape[0])).reshape(-1, 1024)


def _pack_small(d):
    rows = jnp.concatenate([_rows1024(d[n]) for n in SMALL], axis=0)
    return jnp.pad(rows, ((0, SMALL_ROWS - rows.shape[0]), (0, 0)))


def _unpack_small(packed, like):
    out, r = {}, 0
    for n in SMALL:
        size = like[n].size
        nr = -(-size // 1024)
        out[n] = packed[r:r + nr].reshape(-1)[:size].reshape(like[n].shape)
        r += nr
    return out


def kernel(x, meta_tokens, ffn1_norm, ffn1_w_gu, ffn1_w_down, mix_norm, w_in, ssd_conv_w, ssd_conv_b, ssd_dt_bias, ssd_a_log, ssd_d, ssd_norm, hg_lower_bound, hg_norm, w_branch_a, w_branch_b, w_out, ffn2_norm, ffn2_w_gu, ffn2_w_down, final_norm, loss_target, m_meta_tokens, m_ffn1_norm, m_ffn1_w_gu, m_ffn1_w_down, m_mix_norm, m_w_in, m_ssd_conv_w, m_ssd_conv_b, m_ssd_dt_bias, m_ssd_a_log, m_ssd_d, m_ssd_norm, m_hg_lower_bound, m_hg_norm, m_w_branch_a, m_w_branch_b, m_w_out, m_ffn2_norm, m_ffn2_w_gu, m_ffn2_w_down, m_final_norm, v_meta_tokens, v_ffn1_norm, v_ffn1_w_gu, v_ffn1_w_down, v_mix_norm, v_w_in, v_ssd_conv_w, v_ssd_conv_b, v_ssd_dt_bias, v_ssd_a_log, v_ssd_d, v_ssd_norm, v_hg_lower_bound, v_hg_norm, v_w_branch_a, v_w_branch_b, v_w_out, v_ffn2_norm, v_ffn2_w_gu, v_ffn2_w_down, v_final_norm):
    P = dict(zip(WEIGHTS, (meta_tokens, ffn1_norm, ffn1_w_gu, ffn1_w_down, mix_norm, w_in, ssd_conv_w, ssd_conv_b, ssd_dt_bias, ssd_a_log, ssd_d, ssd_norm, hg_lower_bound, hg_norm, w_branch_a, w_branch_b, w_out, ffn2_norm, ffn2_w_gu, ffn2_w_down, final_norm)))
    M = dict(zip(WEIGHTS, (m_meta_tokens, m_ffn1_norm, m_ffn1_w_gu, m_ffn1_w_down, m_mix_norm, m_w_in, m_ssd_conv_w, m_ssd_conv_b, m_ssd_dt_bias, m_ssd_a_log, m_ssd_d, m_ssd_norm, m_hg_lower_bound, m_hg_norm, m_w_branch_a, m_w_branch_b, m_w_out, m_ffn2_norm, m_ffn2_w_gu, m_ffn2_w_down, m_final_norm)))
    V = dict(zip(WEIGHTS, (v_meta_tokens, v_ffn1_norm, v_ffn1_w_gu, v_ffn1_w_down, v_mix_norm, v_w_in, v_ssd_conv_w, v_ssd_conv_b, v_ssd_dt_bias, v_ssd_a_log, v_ssd_d, v_ssd_norm, v_hg_lower_bound, v_hg_norm, v_w_branch_a, v_w_branch_b, v_w_out, v_ffn2_norm, v_ffn2_w_gu, v_ffn2_w_down, v_final_norm)))
    cx, cy, cc = _place()
    q = 2 * cx + cy

    mine = jnp.concatenate([meta_tokens.reshape(4, 1024), ssd_conv_w.reshape(2, 1024), jnp.zeros((2, 1024), F32)], axis=0)
    every = _exchange8(mine, False, "gather_small")
    meta_full = jnp.concatenate([every[2 * k, 0:4].reshape(N_META, 256) for k in range(N_CHIPS)], axis=1)
    conv_w_full = jnp.concatenate([every[2 * k, 4:6].reshape(SSD_CONV, 512) for k in range(N_CHIPS)], axis=1)

    rows = jnp.concatenate([P[n][0] for n in ROW_SHARDED], axis=0).astype(BF16)
    gu1, gu2, w_in_all, rows_all = _gather_big(
        [ffn1_w_gu[0].astype(BF16), ffn2_w_gu[0].astype(BF16), w_in[0].astype(BF16), rows], "gather_weights")
    W = {n: P[n] for n in SMALL}
    W["meta_tokens"], W["ssd_conv_w"] = meta_full, conv_w_full
    W["ffn1_w_gu"], W["ffn2_w_gu"] = gu1, gu2
    W["w_in"] = _split_w_in(w_in_all.transpose(1, 0, 2).reshape(D_MODEL, -1))
    r = 0
    for n in ROW_SHARDED:
        nr = P[n].shape[1]
        W[n] = rows_all[:, r:r + nr].reshape(N_CHIPS * nr, D_MODEL)
        r += nr

    loss8, grad_x, G = _local_step(x, loss_target, W)

    small = jnp.concatenate(
        [G["meta_tokens"]] + [_rows1024(G[n]) for n in SMALL if n != "meta_tokens"] + [_rows1024(loss8[0:1, 0:1])], axis=0)
    small = jnp.pad(small, ((0, 40 - small.shape[0]), (0, 0)))
    small = _exchange8(small, True, "reduce_small")
    Gs = {"meta_tokens": small[0:N_META]}
    r = N_META
    for n in SMALL:
        if n == "meta_tokens":
            continue
        nr = -(-G[n].size // 1024)
        Gs[n] = small[r:r + nr].reshape(-1)[:G[n].size].reshape(G[n].shape)
        r += nr
    loss = small[r, 0]
    Gs["meta_tokens"] = lax.dynamic_slice(Gs["meta_tokens"], (0, 256 * q), (N_META, 256))
    Gs["ssd_conv_w"] = lax.dynamic_slice(Gs["ssd_conv_w"], (0, 512 * q), (SSD_CONV, 512))[None]
    Gs = {n: Gs[n].reshape(P[n].shape) for n in SMALL}

    w_in_parts = G["w_in"].reshape(D_MODEL, N_CHIPS, -1).transpose(1, 0, 2)
    row_parts = jnp.concatenate([G[n].reshape(N_CHIPS, -1, D_MODEL) for n in ROW_SHARDED], axis=1)
    g_gu1, g_gu2, g_w_in, g_rows = _reduce_to_owners([G["ffn1_w_gu"], G["ffn2_w_gu"], w_in_parts, row_parts], cc)
    Gb = {"ffn1_w_gu": g_gu1, "ffn2_w_gu": g_gu2, "w_in": g_w_in}
    r = 0
    for n in ROW_SHARDED:
        nr = P[n].shape[1]
        Gb[n] = g_rows[r:r + nr]
        r += nr

    grads, delta, new_m, new_v = dict(Gs), {}, {}, {}
    d_s, m_s, v_s = _adamw(_pack_small(P), _pack_small(Gs), _pack_small(M), _pack_small(V), "adamw_small")
    delta.update(_unpack_small(d_s, P))
    new_m.update(_unpack_small(m_s, P))
    new_v.update(_unpack_small(v_s, P))
    for n in BIG:
        d_, m_, v_ = _adamw(P[n][0], Gb[n], M[n][0], V[n][0], f"adamw_{n}")
        grads[n], delta[n], new_m[n], new_v[n] = Gb[n][None], d_[None], m_[None], v_[None]
    return (loss, grad_x, *[grads[n] for n in WEIGHTS], *[delta[n] for n in WEIGHTS],
            *[new_m[n] for n in WEIGHTS], *[new_v[n] for n in WEIGHTS])
```

```python
import functools

import jax
import jax.numpy as jnp
from jax import lax
from jax.experimental import pallas as pl
from jax.experimental.pallas import tpu as pltpu

F32 = jnp.float32
BF16 = jnp.bfloat16
HIGHEST = lax.Precision.HIGHEST
MESH = pl.DeviceIdType.MESH

D_MODEL = 1024
N_META = 16
EPS = 1e-6
SSD_HEADS = 16
SSD_HEAD_DIM = 64
SSD_INNER = 1024
SSD_GROUPS = 4
SSD_STATE = 128
SSD_CONV = 4
SSD_CONV_CH = 2048
HG_HEADS = 8
HG_SUB = 32
CHUNK = 128
D_FF = 2816
N_CHIPS = 4
IN_SIZES = (1024, 2048, 16, 1024, 1024, 1024, 1024, 1024, 1024)
ADAM_LR = 0.001
ADAM_B1 = 0.9
ADAM_B2 = 0.999
ADAM_EPS = 1e-08
ADAM_WD = 0.01
ADAM_STEP = 10
VMEM_LIMIT = 56 * 1024 * 1024


def _cparams(sem=None):
    return pltpu.CompilerParams(dimension_semantics=sem, vmem_limit_bytes=VMEM_LIMIT)


def _pick(n, cands):
    for c in cands:
        if n % c == 0:
            return c
    return n


def _dg(a, b, ca, cb):
    return lax.dot_general(a.astype(BF16), b.astype(BF16), (((ca,), (cb,)), ((), ())), preferred_element_type=F32)


@jax.custom_vjp
def _mm(a, b):
    return _dg(a, b, 1, 0)


def _mm_fwd(a, b):
    return _dg(a, b, 1, 0), (a, b)


def _mm_bwd(r, g):
    a, b = r
    return _dg(g, b, 1, 1), _dg(a, g, 0, 0)


_mm.defvjp(_mm_fwd, _mm_bwd)


@jax.custom_vjp
def _mm_nt(a, b):
    return _dg(a, b, 1, 1)


def _mm_nt_fwd(a, b):
    return _dg(a, b, 1, 1), (a, b)


def _mm_nt_bwd(r, g):
    a, b = r
    return _dg(g, b, 1, 0), _dg(g, a, 0, 0)


_mm_nt.defvjp(_mm_nt_fwd, _mm_nt_bwd)


@jax.custom_vjp
def _mm_tn(a, b):
    return _dg(a, b, 0, 0)


def _mm_tn_fwd(a, b):
    return _dg(a, b, 0, 0), (a, b)


def _mm_tn_bwd(r, g):
    a, b = r
    return _dg(b, g, 1, 1), _dg(a, g, 1, 0)


_mm_tn.defvjp(_mm_tn_fwd, _mm_tn_bwd)


def _silu(x):
    return x * jax.nn.sigmoid(x)


def _softplus(x):
    return jnp.maximum(x, 0.0) + jnp.log(1.0 + jnp.exp(-jnp.abs(x)))


def _tril(n):
    ri = lax.broadcasted_iota(jnp.int32, (n, n), 0)
    ci = lax.broadcasted_iota(jnp.int32, (n, n), 1)
    return ri >= ci


def _row_of(m, r):
    sub = lax.broadcasted_iota(jnp.int32, (m.shape[0], 1), 0)
    return jnp.sum(jnp.where(sub == r, m, 0.0), axis=0, keepdims=True)


def _col_of(m, c):
    lane = lax.broadcasted_iota(jnp.int32, (1, m.shape[1]), 1)
    return jnp.sum(jnp.where(lane == c, m, 0.0), axis=1, keepdims=True)


def _matmul(a, b, *, mode, out_dtype, name, alpha=1.0, res=None, tm=None, tn=None, tk=None, out_groups=None):
    b3 = b.ndim == 3
    if mode == "nn":
        M, K = a.shape
        G = b.shape[0] if b3 else 1
        Ng = b.shape[-1]
        N = G * Ng
    elif mode == "nt":
        M, K = a.shape
        G = b.shape[0] if b3 else 1
        N = b.shape[-2]
        Kg = b.shape[-1]
        assert G * Kg == K
    else:
        K, M = a.shape
        N = b.shape[1]
        G = out_groups or 1
        Ng = N // G
    if mode == "tn":
        tm = tm or _pick(M, (1408, 1024, 512, 256, 128))
        tk = tk or _pick(K, (544, 256, 128))
        tn = tn or _pick(Ng, (1408, 1024, 512, 256, 128))
    else:
        tm = tm or _pick(M, (1088, 544, 256, 128))
        if mode == "nn":
            tn = tn or _pick(Ng, (1408, 512, 256, 128))
            tk = K
        else:
            tn = tn or _pick(N, (512, 256, 128))
            tk = tk or (Kg if b3 else K)
    nm, nn_, nk = M // tm, N // tn, K // tk
    assert nm * tm == M and nn_ * tn == N and nk * tk == K, (name, M, N, K, tm, tn, tk)

    if mode == "nn":
        a_spec = pl.BlockSpec((tm, tk), lambda i, j, k: (i, k))
        if b3:
            ns = Ng // tn
            b_spec = pl.BlockSpec((None, tk, tn), lambda i, j, k: (j // ns, k, j % ns))
        else:
            b_spec = pl.BlockSpec((tk, tn), lambda i, j, k: (k, j))
        ca, cb = 1, 0
    elif mode == "nt":
        a_spec = pl.BlockSpec((tm, tk), lambda i, j, k: (i, k))
        if b3:
            ks = Kg // tk
            b_spec = pl.BlockSpec((None, tn, tk), lambda i, j, k: (k // ks, j, k % ks))
        else:
            b_spec = pl.BlockSpec((tn, tk), lambda i, j, k: (j, k))
        ca, cb = 1, 1
    else:
        a_spec = pl.BlockSpec((tk, tm), lambda i, j, k: (k, i))
        b_spec = pl.BlockSpec((tk, tn), lambda i, j, k: (k, j))
        ca, cb = 0, 0
    if mode == "tn" and G > 1:
        ns = Ng // tn
        o_spec = pl.BlockSpec((None, tm, tn), lambda i, j, k: (j // ns, i, j % ns))
        out_shape = jax.ShapeDtypeStruct((G, M, Ng), out_dtype)
    else:
        o_spec = pl.BlockSpec((tm, tn), lambda i, j, k: (i, j))
        out_shape = jax.ShapeDtypeStruct((M, N), out_dtype)
    in_specs = [a_spec, b_spec]
    args = [a, b]
    if res is not None:
        in_specs.append(pl.BlockSpec((tm, tn), lambda i, j, k: (i, j)))
        args.append(res)
    has_res = res is not None

    def body(*refs):
        a_ref, b_ref = refs[0], refs[1]
        o_ref, acc_ref = refs[-2], refs[-1]
        k = pl.program_id(2)

        @pl.when(k == 0)
        def _():
            acc_ref[...] = jnp.zeros_like(acc_ref)

        acc_ref[...] += _dg(a_ref[...], b_ref[...], ca, cb)

        @pl.when(k == nk - 1)
        def _():
            o = acc_ref[...]
            if alpha != 1.0:
                o = o * alpha
            if has_res:
                o = o + refs[2][...]
            o_ref[...] = o.astype(o_ref.dtype)

    return pl.pallas_call(
        body, grid=(nm, nn_, nk), in_specs=in_specs, out_specs=o_spec, out_shape=out_shape,
        scratch_shapes=[pltpu.VMEM((tm, tn), F32)], name=name,
        compiler_params=_cparams(("parallel", "parallel", "arbitrary")),
    )(*args)


def _rms_fn(h, w):
    r = lax.rsqrt(jnp.mean(h * h, axis=-1, keepdims=True) + EPS)
    return h * r * w


def _swiglu_fn(gu):
    g = gu[:, :D_FF].astype(F32)
    u = gu[:, D_FF:].astype(F32)
    return _silu(g) * u


def _merge_fn(pa, pb, gates):
    return jax.nn.sigmoid(gates[:, :D_MODEL]) * pa + jax.nn.sigmoid(gates[:, D_MODEL:]) * pb


def _rows_call(body, *, rows, tr, ins, outs, accs=(), name):
    n = rows // tr
    assert n * tr == rows

    def spec(x):
        if isinstance(x, tuple):
            shp = x[1].shape
            return pl.BlockSpec(shp, lambda i: (0,) * len(shp))
        return pl.BlockSpec((tr, x.shape[1]), lambda i: (i, 0))

    in_specs = [spec(x) for x in ins]
    args = [x[1] if isinstance(x, tuple) else x for x in ins]
    out_specs = [spec(x) for x in outs] + [pl.BlockSpec(x.shape, lambda i: (0,) * len(x.shape)) for x in accs]
    out_shape = [x[1] if isinstance(x, tuple) else x for x in outs] + list(accs)
    return pl.pallas_call(
        body, grid=(n,), in_specs=in_specs, out_specs=out_specs, out_shape=out_shape, name=name,
        compiler_params=_cparams(("arbitrary",)),
    )(*args)


def _acc_rows(ref, val):
    @pl.when(pl.program_id(0) == 0)
    def _():
        ref[...] = jnp.zeros_like(ref)

    ref[0:1, :] += val


def _rms_fwd(h, w, name):
    def body(h_ref, w_ref, o_ref):
        o_ref[...] = _rms_fn(h_ref[...], w_ref[...]).astype(o_ref.dtype)

    R = h.shape[0]
    return _rows_call(body, rows=R, tr=_pick(R, (256, 128)), ins=[h, ("full", w)],
                      outs=[jax.ShapeDtypeStruct(h.shape, BF16)], name=name)[0]


def _rms_bwd(h, w, dn, dres, name):
    def body(h_ref, w_ref, dn_ref, dres_ref, dh_ref, dw_ref):
        _, vjp = jax.vjp(_rms_fn, h_ref[...], w_ref[...])
        dh, dw = vjp(dn_ref[...].astype(F32))
        dh_ref[...] = dh + dres_ref[...]
        _acc_rows(dw_ref, dw)

    R = h.shape[0]
    return _rows_call(body, rows=R, tr=_pick(R, (256, 128)), ins=[h, ("full", w), dn, dres],
                      outs=[jax.ShapeDtypeStruct(h.shape, F32)], accs=[jax.ShapeDtypeStruct((8, D_MODEL), F32)], name=name)


def _swiglu_fwd(gu, name):
    def body(gu_ref, o_ref):
        o_ref[...] = _swiglu_fn(gu_ref[...]).astype(o_ref.dtype)

    R = gu.shape[0]
    return _rows_call(body, rows=R, tr=_pick(R, (256, 128)), ins=[gu],
                      outs=[jax.ShapeDtypeStruct((R, D_FF), BF16)], name=name)[0]


def _swiglu_bwd(gu, da, name):
    def body(gu_ref, da_ref, o_ref):
        _, vjp = jax.vjp(_swiglu_fn, gu_ref[...].astype(F32))
        (dgu,) = vjp(da_ref[...].astype(F32))
        o_ref[...] = dgu.astype(o_ref.dtype)

    R = gu.shape[0]
    return _rows_call(body, rows=R, tr=_pick(R, (256, 128)), ins=[gu, da],
                      outs=[jax.ShapeDtypeStruct(gu.shape, BF16)], name=name)[0]


def _merge_fwd(pa, pb, gates, name):
    def body(pa_ref, pb_ref, g_ref, o_ref):
        o_ref[...] = _merge_fn(pa_ref[...], pb_ref[...], g_ref[...]).astype(o_ref.dtype)

    R = pa.shape[0]
    return _rows_call(body, rows=R, tr=_pick(R, (256, 128)), ins=[pa, pb, gates],
                      outs=[jax.ShapeDtypeStruct(pa.shape, BF16)], name=name)[0]


def _merge_bwd(pa, pb, gates, dm, name):
    def body(pa_ref, pb_ref, g_ref, dm_ref, dpa_ref, dpb_ref, dg_ref):
        _, vjp = jax.vjp(_merge_fn, pa_ref[...], pb_ref[...], g_ref[...])
        dpa, dpb, dg = vjp(dm_ref[...].astype(F32))
        dpa_ref[...] = dpa.astype(dpa_ref.dtype)
        dpb_ref[...] = dpb.astype(dpb_ref.dtype)
        dg_ref[...] = dg.astype(dg_ref.dtype)

    R = pa.shape[0]
    return _rows_call(body, rows=R, tr=_pick(R, (256, 128)), ins=[pa, pb, gates, dm],
                      outs=[jax.ShapeDtypeStruct(pa.shape, BF16), jax.ShapeDtypeStruct(pa.shape, BF16),
                            jax.ShapeDtypeStruct(gates.shape, BF16)], name=name)


def _loss_head(h3, w, target, nseq, name):
    Tp = h3.shape[0] // nseq
    nc = Tp // CHUNK

    def fn(h, w_, t, valid):
        y = _rms_fn(h, w_)
        e = (y - t) * valid
        return 0.5 * jnp.sum(jnp.mean(e * e, axis=-1, keepdims=True))

    def body(h_ref, w_ref, t_ref, loss_ref, dh_ref, dw_ref):
        b, c = pl.program_id(0), pl.program_id(1)
        valid = (c >= 1).astype(F32)
        t = t_ref[...]
        loss, vjp = jax.vjp(lambda h, w_: fn(h, w_, t, valid), h_ref[...], w_ref[...])
        dh, dw = vjp(jnp.ones((), F32))
        dh_ref[...] = dh

        @pl.when((b == 0) & (c == 0))
        def _():
            loss_ref[...] = jnp.zeros_like(loss_ref)
            dw_ref[...] = jnp.zeros_like(dw_ref)

        loss_ref[...] += jnp.full(loss_ref.shape, loss, F32)
        dw_ref[0:1, :] += dw

    return pl.pallas_call(
        body, grid=(nseq, nc),
        in_specs=[pl.BlockSpec((CHUNK, D_MODEL), lambda b, c: (b * nc + c, 0)),
                  pl.BlockSpec((1, D_MODEL), lambda b, c: (0, 0)),
                  pl.BlockSpec((None, CHUNK, D_MODEL), lambda b, c: (b, jnp.maximum(c - 1, 0), 0))],
        out_specs=[pl.BlockSpec((8, 128), lambda b, c: (0, 0)),
                   pl.BlockSpec((CHUNK, D_MODEL), lambda b, c: (b * nc + c, 0)),
                   pl.BlockSpec((8, D_MODEL), lambda b, c: (0, 0))],
        out_shape=[jax.ShapeDtypeStruct((8, 128), F32), jax.ShapeDtypeStruct(h3.shape, F32),
                   jax.ShapeDtypeStruct((8, D_MODEL), F32)],
        name=name, compiler_params=_cparams(("arbitrary", "arbitrary")),
    )(h3, w, target)


CONV_TILE = 512
CONV_HALO = 8


def _conv_fwd(xbc, w, b, pad, name):
    B, Tp, C = xbc.shape
    nch = Tp // CHUNK

    def body(x_ref, w_ref, b_ref, o_ref, xp):
        xp[0:CONV_HALO, :] = jnp.zeros((CONV_HALO, CONV_TILE), F32)
        xp[CONV_HALO:, :] = x_ref[...]
        for c in range(nch):
            acc = jnp.zeros((CHUNK, CONV_TILE), F32) + b_ref[...]
            for k in range(SSD_CONV):
                acc = acc + w_ref[k:k + 1, :] * xp[pl.ds(CONV_HALO + CHUNK * c - (SSD_CONV - 1) + k, CHUNK), :]
            row = CHUNK * c + lax.broadcasted_iota(jnp.int32, (CHUNK, 1), 0)
            o_ref[pl.ds(CHUNK * c, CHUNK), :] = jnp.where(row >= pad, _silu(acc), 0.0)

    return pl.pallas_call(
        body, grid=(B, C // CONV_TILE),
        in_specs=[pl.BlockSpec((None, Tp, CONV_TILE), lambda i, j: (i, 0, j)),
                  pl.BlockSpec((SSD_CONV, CONV_TILE), lambda i, j: (0, j)),
                  pl.BlockSpec((1, CONV_TILE), lambda i, j: (0, j))],
        out_specs=pl.BlockSpec((None, Tp, CONV_TILE), lambda i, j: (i, 0, j)),
        out_shape=jax.ShapeDtypeStruct(xbc.shape, F32),
        scratch_shapes=[pltpu.VMEM((Tp + CONV_HALO, CONV_TILE), F32)],
        name=name, compiler_params=_cparams(("arbitrary", "arbitrary")),
    )(xbc, w, b)


def _conv_bwd(xbc, w, b, dact, pad, name):
    B, Tp, C = xbc.shape
    nch = Tp // CHUNK

    def body(x_ref, w_ref, b_ref, da_ref, dx_ref, dw_ref, db_ref, xp, dp):
        bi = pl.program_id(1)
        xp[0:CONV_HALO, :] = jnp.zeros((CONV_HALO, CONV_TILE), F32)
        xp[CONV_HALO:, :] = x_ref[...]
        dp[pl.ds(Tp, CONV_HALO), :] = jnp.zeros((CONV_HALO, CONV_TILE), F32)
        dws = [jnp.zeros((1, CONV_TILE), F32) for _ in range(SSD_CONV)]
        dbs = jnp.zeros((1, CONV_TILE), F32)
        for c in range(nch):
            xs = [xp[pl.ds(CONV_HALO + CHUNK * c - (SSD_CONV - 1) + k, CHUNK), :] for k in range(SSD_CONV)]
            acc = jnp.zeros((CHUNK, CONV_TILE), F32) + b_ref[...]
            for k in range(SSD_CONV):
                acc = acc + w_ref[k:k + 1, :] * xs[k]
            row = CHUNK * c + lax.broadcasted_iota(jnp.int32, (CHUNK, 1), 0)
            sg = jax.nn.sigmoid(acc)
            dpre = jnp.where(row >= pad, da_ref[pl.ds(CHUNK * c, CHUNK), :] * (sg * (1.0 + acc * (1.0 - sg))), 0.0)
            dp[pl.ds(CHUNK * c, CHUNK), :] = dpre
            dbs = dbs + jnp.sum(dpre, axis=0, keepdims=True)
            for k in range(SSD_CONV):
                dws[k] = dws[k] + jnp.sum(dpre * xs[k], axis=0, keepdims=True)
        for c in range(nch):
            acc = jnp.zeros((CHUNK, CONV_TILE), F32)
            for k in range(SSD_CONV):
                acc = acc + w_ref[k:k + 1, :] * dp[pl.ds(CHUNK * c + (SSD_CONV - 1) - k, CHUNK), :]
            dx_ref[pl.ds(CHUNK * c, CHUNK), :] = acc

        @pl.when(bi == 0)
        def _():
            dw_ref[...] = jnp.zeros_like(dw_ref)
            db_ref[...] = jnp.zeros_like(db_ref)

        for k in range(SSD_CONV):
            dw_ref[k:k + 1, :] += dws[k]
        db_ref[0:1, :] += dbs

    return pl.pallas_call(
        body, grid=(C // CONV_TILE, B),
        in_specs=[pl.BlockSpec((None, Tp, CONV_TILE), lambda j, i: (i, 0, j)),
                  pl.BlockSpec((SSD_CONV, CONV_TILE), lambda j, i: (0, j)),
                  pl.BlockSpec((1, CONV_TILE), lambda j, i: (0, j)),
                  pl.BlockSpec((None, Tp, CONV_TILE), lambda j, i: (i, 0, j))],
        out_specs=[pl.BlockSpec((None, Tp, CONV_TILE), lambda j, i: (i, 0, j)),
                   pl.BlockSpec((8, CONV_TILE), lambda j, i: (0, j)),
                   pl.BlockSpec((8, CONV_TILE), lambda j, i: (0, j))],
        out_shape=[jax.ShapeDtypeStruct(xbc.shape, F32), jax.ShapeDtypeStruct((8, C), F32),
                   jax.ShapeDtypeStruct((8, C), F32)],
        scratch_shapes=[pltpu.VMEM((Tp + CONV_HALO, CONV_TILE), F32), pltpu.VMEM((Tp + CONV_HALO, CONV_TILE), F32)],
        name=name, compiler_params=_cparams(("arbitrary", "arbitrary")),
    )(xbc, w, b, dact)


def _ssd_chunk(xs, bm, cm, dtr, z, state, dt_bias, a_log, dskip, norm_w, valid):
    Q = xs.shape[0]
    lane = lax.broadcasted_iota(jnp.int32, (1, 128), 1)
    dt = jnp.where(lane < SSD_HEADS, _softplus(dtr + dt_bias), 0.0) * valid
    a = dt * (-jnp.exp(a_log))
    tril = _tril(Q)
    cs = jnp.dot(tril.astype(F32), a, precision=HIGHEST)
    cs_t = cs.T
    cs_end = _row_of(cs, Q - 1)
    low = lane < SSD_HEAD_DIM
    low_rows = lax.broadcasted_iota(jnp.int32, (128, 1), 0) < SSD_HEAD_DIM
    ys, new_state = [], []
    for g in range(SSD_GROUPS):
        bg = bm[:, 128 * g:128 * (g + 1)]
        cg = cm[:, 128 * g:128 * (g + 1)]
        cb = _mm_nt(cg, bg)
        for pr in range(2):
            p = 2 * g + pr
            h0, h1 = 2 * p, 2 * p + 1
            xp = xs[:, 128 * p:128 * (p + 1)]
            c0, c1 = _col_of(cs, h0), _col_of(cs, h1)
            e0, e1 = _col_of(cs_end, h0), _col_of(cs_end, h1)
            xd = xp * jnp.where(low, _col_of(dt, h0), _col_of(dt, h1))
            l0 = jnp.exp(jnp.where(tril, c0 - _row_of(cs_t, h0), -1e30))
            l1 = jnp.exp(jnp.where(tril, c1 - _row_of(cs_t, h1), -1e30))
            y_diag = jnp.where(low, _mm(cb * l0, xd), _mm(cb * l1, xd))
            to_end = jnp.where(low, jnp.exp(e0 - c0), jnp.exp(e1 - c1))
            sp = state[128 * p:128 * (p + 1), :]
            y_off = _mm_nt(cg, sp) * jnp.where(low, jnp.exp(c0), jnp.exp(c1))
            new_state.append(sp * jnp.where(low_rows, jnp.exp(e0), jnp.exp(e1)) + _mm_tn(xd * to_end, bg))
            ys.append(y_diag + y_off + xp * jnp.where(low, _col_of(dskip, h0), _col_of(dskip, h1)))
    y = jnp.concatenate(ys, axis=1) * _silu(z)
    gw = SSD_INNER // SSD_GROUPS
    outs = []
    for g in range(SSD_GROUPS):
        blk = y[:, gw * g:gw * (g + 1)]
        outs.append(blk * lax.rsqrt(jnp.mean(blk * blk, axis=-1, keepdims=True) + EPS))
    return jnp.concatenate(outs, axis=1) * norm_w, jnp.concatenate(new_state, axis=0)


def _valid_rows(c, pad):
    row = c * CHUNK + lax.broadcasted_iota(jnp.int32, (CHUNK, 1), 0)
    return (row >= pad).astype(F32)


def _ssd_fwd(xact, dtr, z, dt_bias, a_log, dskip, norm_w, pad, name):
    B, Tp, _ = xact.shape
    nc = Tp // CHUNK

    def body(xs_ref, bm_ref, cm_ref, dt_ref, z_ref, db_ref, al_ref, ds_ref, nw_ref, y_ref, save_ref, st):
        c = pl.program_id(1)

        @pl.when(c == 0)
        def _():
            st[...] = jnp.zeros_like(st)

        s0 = st[...]
        save_ref[...] = s0
        y, s1 = _ssd_chunk(xs_ref[...], bm_ref[...], cm_ref[...], dt_ref[...], z_ref[...], s0, db_ref[...],
                           al_ref[...], ds_ref[...], nw_ref[...], _valid_rows(c, pad))
        y_ref[...] = y.astype(y_ref.dtype)
        st[...] = s1

    row = lambda w, off=0: pl.BlockSpec((None, CHUNK, w), lambda b, c: (b, c, off))
    par = lambda w: pl.BlockSpec((1, w), lambda b, c: (0, 0))
    return pl.pallas_call(
        body, grid=(B, nc),
        in_specs=[row(1024, 0), row(512, 2), row(512, 3), row(128), row(1024), par(128), par(128), par(128), par(1024)],
        out_specs=[row(1024), pl.BlockSpec((None, None, 1024, 128), lambda b, c: (b, c, 0, 0))],
        out_shape=[jax.ShapeDtypeStruct((B, Tp, SSD_INNER), BF16), jax.ShapeDtypeStruct((B, nc, 1024, 128), F32)],
        scratch_shapes=[pltpu.VMEM((1024, 128), F32)],
        name=name, compiler_params=_cparams(("arbitrary", "arbitrary")),
    )(xact, xact, xact, dtr, z, dt_bias, a_log, dskip, norm_w)


def _ssd_bwd(xact, dtr, z, dt_bias, a_log, dskip, norm_w, saved, dy, pad, name):
    B, Tp, _ = xact.shape
    nc = Tp // CHUNK

    def body(xs_ref, bm_ref, cm_ref, dt_ref, z_ref, db_ref, al_ref, ds_ref, nw_ref, sv_ref, dy_ref,
             dx_ref, ddt_ref, dz_ref, dpar_ref, dnw_ref, dst):
        b, i = pl.program_id(0), pl.program_id(1)
        c = nc - 1 - i

        @pl.when(i == 0)
        def _():
            dst[...] = jnp.zeros_like(dst)

        valid = _valid_rows(c, pad)
        fn = lambda *a: _ssd_chunk(*a, valid)
        _, vjp = jax.vjp(fn, xs_ref[...], bm_ref[...], cm_ref[...], dt_ref[...], z_ref[...], sv_ref[...],
                         db_ref[...], al_ref[...], ds_ref[...], nw_ref[...])
        dxs, dbm, dcm, ddt, dz, dstate, ddb, dal, dds, dnw = vjp((dy_ref[...].astype(F32), dst[...]))
        dx_ref[:, 0:1024] = dxs
        dx_ref[:, 1024:1536] = dbm
        dx_ref[:, 1536:2048] = dcm
        ddt_ref[...] = ddt
        dz_ref[...] = dz
        dst[...] = dstate

        @pl.when((b == 0) & (i == 0))
        def _():
            dpar_ref[...] = jnp.zeros_like(dpar_ref)
            dnw_ref[...] = jnp.zeros_like(dnw_ref)

        dpar_ref[0:1, :] += ddb
        dpar_ref[1:2, :] += dal
        dpar_ref[2:3, :] += dds
        dnw_ref[0:1, :] += dnw

    row = lambda w, off=0: pl.BlockSpec((None, CHUNK, w), lambda b, i: (b, nc - 1 - i, off))
    par = lambda w: pl.BlockSpec((1, w), lambda b, i: (0, 0))
    acc = lambda w: pl.BlockSpec((8, w), lambda b, i: (0, 0))
    outs = pl.pallas_call(
        body, grid=(B, nc),
        in_specs=[row(1024, 0), row(512, 2), row(512, 3), row(128), row(1024), par(128), par(128), par(128), par(1024),
                  pl.BlockSpec((None, None, 1024, 128), lambda b, i: (b, nc - 1 - i, 0, 0)), row(1024)],
        out_specs=[row(2048), row(128), row(1024), acc(128), acc(1024)],
        out_shape=[jax.ShapeDtypeStruct((B, Tp, 2048), F32), jax.ShapeDtypeStruct((B, Tp, 128), F32),
                   jax.ShapeDtypeStruct((B, Tp, 1024), F32), jax.ShapeDtypeStruct((8, 128), F32),
                   jax.ShapeDtypeStruct((8, 1024), F32)],
        scratch_shapes=[pltpu.VMEM((1024, 128), F32)],
        name=name, compiler_params=_cparams(("arbitrary", "arbitrary")),
    )(xact, xact, xact, dtr, z, dt_bias, a_log, dskip, norm_w, saved, dy)
    return outs


def _hg_chunk(qr, fr, ir, gr, state_t, p0, p1, norm_w, valid):
    Q = qr.shape[0]
    lb = jax.nn.sigmoid(p0 - p1)
    f = lb + (1.0 - lb) * jax.nn.sigmoid(fr)
    k = 1.0 - f
    q = _silu(qr)
    v = ir * valid
    cum = jnp.dot(_tril(Q).astype(F32), jnp.log(f), precision=HIGHEST)
    cum_end = _row_of(cum, Q - 1)
    o_inter = _mm_nt(q * jnp.exp(cum), state_t)
    tril_s = _tril(HG_SUB)
    outs = []
    for i in range(Q // HG_SUB):
        lo, hi = HG_SUB * i, HG_SUB * (i + 1)
        qi, ci, ki, vi = q[lo:hi], cum[lo:hi], k[lo:hi], v[lo:hi]
        mid = _row_of(cum, lo + HG_SUB // 2 - 1)
        att = jnp.where(tril_s, _mm_nt(qi * jnp.exp(ci - mid), ki * jnp.exp(mid - ci)), 0.0)
        oi = _mm(att, vi)
        if i > 0:
            start = _row_of(cum, lo - 1)
            att_prev = _mm_nt(qi * jnp.exp(ci - start), k[:lo] * jnp.exp(start - cum[:lo]))
            oi = oi + _mm(att_prev, v[:lo])
        outs.append(oi)
    o = o_inter + jnp.concatenate(outs, axis=0)
    new_state_t = state_t * jnp.exp(cum_end) + _mm_tn(v, k * jnp.exp(cum_end - cum))
    o = o * lax.rsqrt(jnp.mean(o * o, axis=-1, keepdims=True) + EPS) * norm_w
    return o * _silu(gr), new_state_t


HG_PER_STEP = 2
HG_COLS = 4 * 128


def _hg_fwd(qfig, lbh, nwh, pad, name):
    B, Tp, _ = qfig.shape
    nc = Tp // CHUNK
    hp = HG_PER_STEP

    def body(x_ref, lb_ref, nw_ref, y_ref, save_ref, st):
        c = pl.program_id(1)

        @pl.when(c == 0)
        def _():
            st[...] = jnp.zeros_like(st)

        valid = _valid_rows(c, pad)
        for j in range(hp):
            for b in range(B):
                s0 = st[j, b]
                save_ref[j, b] = s0
                col = lambda k: x_ref[b, :, HG_COLS * j + 128 * k:HG_COLS * j + 128 * (k + 1)]
                y, s1 = _hg_chunk(col(0), col(1), col(2), col(3), s0, lb_ref[j, 0:1, :], lb_ref[j, 1:2, :], nw_ref[j], valid)
                y_ref[b, :, 128 * j:128 * (j + 1)] = y.astype(y_ref.dtype)
                st[j, b] = s1

    return pl.pallas_call(
        body, grid=(HG_HEADS // hp, nc),
        in_specs=[pl.BlockSpec((B, CHUNK, HG_COLS * hp), lambda h, c: (0, c, h)),
                  pl.BlockSpec((hp, 2, 128), lambda h, c: (h, 0, 0)),
                  pl.BlockSpec((hp, 1, 128), lambda h, c: (h, 0, 0))],
        out_specs=[pl.BlockSpec((B, CHUNK, 128 * hp), lambda h, c: (0, c, h)),
                   pl.BlockSpec((hp, B, None, 128, 128), lambda h, c: (h, 0, c, 0, 0))],
        out_shape=[jax.ShapeDtypeStruct((B, Tp, 1024), BF16), jax.ShapeDtypeStruct((HG_HEADS, B, nc, 128, 128), F32)],
        scratch_shapes=[pltpu.VMEM((hp, B, 128, 128), F32)],
        name=name, compiler_params=_cparams(("arbitrary", "arbitrary")),
    )(qfig, lbh, nwh)


def _hg_bwd(qfig, lbh, nwh, saved, dy, pad, name):
    B, Tp, _ = qfig.shape
    nc = Tp // CHUNK
    hp = HG_PER_STEP

    def body(x_ref, lb_ref, nw_ref, sv_ref, dy_ref, dx_ref, dlb_ref, dnw_ref, dst):
        i = pl.program_id(1)
        c = nc - 1 - i

        @pl.when(i == 0)
        def _():
            dst[...] = jnp.zeros_like(dst)
            dlb_ref[...] = jnp.zeros_like(dlb_ref)
            dnw_ref[...] = jnp.zeros_like(dnw_ref)

        valid = _valid_rows(c, pad)
        fn = lambda *a: _hg_chunk(*a, valid)
        for j in range(hp):
            for b in range(B):
                col = lambda k: x_ref[b, :, HG_COLS * j + 128 * k:HG_COLS * j + 128 * (k + 1)]
                _, vjp = jax.vjp(fn, col(0), col(1), col(2), col(3), sv_ref[j, b], lb_ref[j, 0:1, :], lb_ref[j, 1:2, :], nw_ref[j])
                d4 = vjp((dy_ref[b, :, 128 * j:128 * (j + 1)].astype(F32), dst[j, b]))
                for k in range(4):
                    dx_ref[b, :, HG_COLS * j + 128 * k:HG_COLS * j + 128 * (k + 1)] = d4[k].astype(dx_ref.dtype)
                dst[j, b] = d4[4]
                dlb_ref[j, 0:1, :] += d4[5]
                dlb_ref[j, 1:2, :] += d4[6]
                dnw_ref[j, 0:1, :] += d4[7]

    acc = pl.BlockSpec((hp, 8, 128), lambda h, i: (h, 0, 0))
    return pl.pallas_call(
        body, grid=(HG_HEADS // hp, nc),
        in_specs=[pl.BlockSpec((B, CHUNK, HG_COLS * hp), lambda h, i: (0, nc - 1 - i, h)),
                  pl.BlockSpec((hp, 2, 128), lambda h, i: (h, 0, 0)),
                  pl.BlockSpec((hp, 1, 128), lambda h, i: (h, 0, 0)),
                  pl.BlockSpec((hp, B, None, 128, 128), lambda h, i: (h, 0, nc - 1 - i, 0, 0)),
                  pl.BlockSpec((B, CHUNK, 128 * hp), lambda h, i: (0, nc - 1 - i, h))],
        out_specs=[pl.BlockSpec((B, CHUNK, HG_COLS * hp), lambda h, i: (0, nc - 1 - i, h)), acc, acc],
        out_shape=[jax.ShapeDtypeStruct((B, Tp, 4096), BF16), jax.ShapeDtypeStruct((HG_HEADS, 8, 128), F32),
                   jax.ShapeDtypeStruct((HG_HEADS, 8, 128), F32)],
        scratch_shapes=[pltpu.VMEM((hp, B, 128, 128), F32)],
        name=name, compiler_params=_cparams(("arbitrary", "arbitrary")),
    )(qfig, lbh, nwh, saved, dy)


def _adamw(w, g, m, v, name):
    R, C = w.shape
    tr = _pick(R, (256, 176, 128, 64, 8)) if R > 256 else R

    def body(w_ref, g_ref, m_ref, v_ref, d_ref, mo_ref, vo_ref):
        g_ = g_ref[...]
        m_ = ADAM_B1 * m_ref[...] + (1.0 - ADAM_B1) * g_
        v_ = ADAM_B2 * v_ref[...] + (1.0 - ADAM_B2) * (g_ * g_)
        m_hat = m_ / (1.0 - ADAM_B1 ** ADAM_STEP)
        v_hat = v_ / (1.0 - ADAM_B2 ** ADAM_STEP)
        d_ref[...] = -ADAM_LR * (m_hat / (jnp.sqrt(v_hat) + ADAM_EPS) + ADAM_WD * w_ref[...])
        mo_ref[...] = m_
        vo_ref[...] = v_

    sp = pl.BlockSpec((tr, C), lambda i: (i, 0))
    sh = jax.ShapeDtypeStruct((R, C), F32)
    return pl.pallas_call(body, grid=(R // tr,), in_specs=[sp] * 4, out_specs=[sp] * 3, out_shape=[sh] * 3,
                          name=name, compiler_params=_cparams(("arbitrary",)))(w, g, m, v)


def _ffn_fwd(h, norm_w, w_gu, w_down, tag):
    n = _rms_fwd(h, norm_w, f"{tag}_norm")
    gu = _matmul(n, w_gu, mode="nn", out_dtype=BF16, name=f"{tag}_gu")
    a = _swiglu_fwd(gu, f"{tag}_act")
    out = _matmul(a, w_down, mode="nn", out_dtype=F32, alpha=0.5, res=h, name=f"{tag}_down")
    return out, (n, gu, a)


def _ffn_bwd(h, norm_w, w_gu, w_down, saved, dout, tag):
    n, gu, a = saved
    da = _matmul(dout, w_down, mode="nt", out_dtype=BF16, alpha=0.5, name=f"{tag}_d_act")
    dw_down = _matmul(a, dout, mode="tn", out_dtype=F32, alpha=0.5, name=f"{tag}_dw_down")
    dgu = _swiglu_bwd(gu, da, f"{tag}_d_gu")
    dn = _matmul(dgu, w_gu, mode="nt", out_dtype=F32, name=f"{tag}_d_norm")
    dw_gu = _matmul(n, dgu, mode="tn", out_dtype=F32, out_groups=N_CHIPS, name=f"{tag}_dw_gu")
    dh, dnw = _rms_bwd(h, norm_w, dn, dout, f"{tag}_d_in")
    return dh, dnw, dw_gu, dw_down


IN_NAMES = ("z", "xbc", "dt", "q", "f", "i", "g", "gates")


def _split_w_in(w_in_full):
    pts = [0]
    for s in IN_SIZES:
        pts.append(pts[-1] + s)
    sl = lambda i, j: w_in_full[:, pts[i]:pts[j]]
    qfig = sl(3, 7).reshape(D_MODEL, 4, HG_HEADS, 128).transpose(0, 2, 1, 3).reshape(D_MODEL, 4 * D_MODEL)
    return {"z": sl(0, 1), "xbc": sl(1, 2), "dt": jnp.pad(sl(2, 3), ((0, 0), (0, 128 - SSD_HEADS))),
            "qfig": qfig, "gates": sl(7, 9)}


def _local_step(x, target, W):
    B, S, _ = x.shape
    T = N_META + S
    pad = (-T) % CHUNK
    Tp = T + pad
    assert pad + N_META == CHUNK
    R = B * Tp
    meta = jnp.broadcast_to(W["meta_tokens"][None], (B, N_META, D_MODEL))
    h0 = jnp.concatenate([jnp.zeros((B, pad, D_MODEL), F32), meta, x], axis=1).reshape(R, D_MODEL)

    h1, sv1 = _ffn_fwd(h0, W["ffn1_norm"], W["ffn1_w_gu"], W["ffn1_w_down"], "ffn1")
    um = _rms_fwd(h1, W["mix_norm"], "mix_norm")
    wi = W["w_in"]
    z = _matmul(um, wi["z"], mode="nn", out_dtype=F32, name="in_z")
    xbc = _matmul(um, wi["xbc"], mode="nn", out_dtype=F32, name="in_xbc")
    dtr = _matmul(um, wi["dt"], mode="nn", out_dtype=F32, name="in_dt")
    qfig = _matmul(um, wi["qfig"], mode="nn", out_dtype=F32, name="in_qfig")
    gates = _matmul(um, wi["gates"], mode="nn", out_dtype=F32, name="in_gates")

    r3 = lambda t: t.reshape(B, Tp, t.shape[-1])
    lane_pad = lambda t: jnp.pad(t, ((0, 0), (0, 128 - t.shape[1])))
    dt_bias, a_log, dskip = lane_pad(W["ssd_dt_bias"]), lane_pad(W["ssd_a_log"]), lane_pad(W["ssd_d"])
    xact = _conv_fwd(r3(xbc), W["ssd_conv_w"], W["ssd_conv_b"], pad, "conv_fwd")
    ya, ssd_saved = _ssd_fwd(xact, r3(dtr), r3(z), dt_bias, a_log, dskip, W["ssd_norm"], pad, "ssd_fwd")
    lbh = W["hg_lower_bound"].reshape(2, HG_HEADS, 128).transpose(1, 0, 2)
    nwh = W["hg_norm"].reshape(HG_HEADS, 1, 128)
    yb, hg_saved = _hg_fwd(r3(qfig), lbh, nwh, pad, "hg_fwd")
    ya2, yb2 = ya.reshape(R, -1), yb.reshape(R, -1)
    pa = _matmul(ya2, W["w_branch_a"], mode="nn", out_dtype=F32, name="branch_a")
    pb = _matmul(yb2, W["w_branch_b"], mode="nn", out_dtype=F32, name="branch_b")
    mg = _merge_fwd(pa, pb, gates, "merge")
    h2 = _matmul(mg, W["w_out"], mode="nn", out_dtype=F32, res=h1, name="mix_out")
    h3, sv2 = _ffn_fwd(h2, W["ffn2_norm"], W["ffn2_w_gu"], W["ffn2_w_down"], "ffn2")

    loss, dh3, d_final = _loss_head(h3, W["final_norm"].reshape(1, D_MODEL), target, B, "loss_head")

    G = {"final_norm": d_final[0]}
    dh2, dnw, G["ffn2_w_gu"], G["ffn2_w_down"] = _ffn_bwd(h2, W["ffn2_norm"], W["ffn2_w_gu"], W["ffn2_w_down"], sv2, dh3, "ffn2")
    G["ffn2_norm"] = dnw[0:1]
    dmg = _matmul(dh2, W["w_out"], mode="nt", out_dtype=BF16, name="d_merge")
    G["w_out"] = _matmul(mg, dh2, mode="tn", out_dtype=F32, name="dw_out")
    dpa, dpb, dgates = _merge_bwd(pa, pb, gates, dmg, "merge_bwd")
    dya = _matmul(dpa, W["w_branch_a"], mode="nt", out_dtype=BF16, name="d_ya")
    dyb = _matmul(dpb, W["w_branch_b"], mode="nt", out_dtype=BF16, name="d_yb")
    G["w_branch_a"] = _matmul(ya2, dpa, mode="tn", out_dtype=F32, name="dw_branch_a")
    G["w_branch_b"] = _matmul(yb2, dpb, mode="tn", out_dtype=F32, name="dw_branch_b")

    dxact, ddtr, dz, dpar, dnw = _ssd_bwd(xact, r3(dtr), r3(z), dt_bias, a_log, dskip, W["ssd_norm"], ssd_saved,
                                          r3(dya), pad, "ssd_bwd")
    G["ssd_dt_bias"], G["ssd_a_log"], G["ssd_d"] = dpar[0:1, :SSD_HEADS], dpar[1:2, :SSD_HEADS], dpar[2:3, :SSD_HEADS]
    G["ssd_norm"] = dnw[0:1]
    dxbc, dcw, dcb = _conv_bwd(r3(xbc), W["ssd_conv_w"], W["ssd_conv_b"], dxact, pad, "conv_bwd")
    G["ssd_conv_w"], G["ssd_conv_b"] = dcw[0:SSD_CONV], dcb[0:1]
    dqfig, dlb, dhn = _hg_bwd(r3(qfig), lbh, nwh, hg_saved, r3(dyb), pad, "hg_bwd")
    G["hg_lower_bound"] = dlb[:, 0:2, :].transpose(1, 0, 2).reshape(2, D_MODEL)
    G["hg_norm"] = dhn[:, 0, :].reshape(1, D_MODEL)

    r2 = lambda t: t.reshape(R, t.shape[-1])
    pieces = [("z", r2(dz)), ("xbc", r2(dxbc)), ("dt", r2(ddtr)), ("qfig", r2(dqfig)), ("gates", dgates)]
    dum = None
    dwi = {}
    for nm, dpiece in pieces:
        dum = _matmul(dpiece, wi[nm], mode="nt", out_dtype=F32, res=dum, name=f"d_mix_{nm}")
        dwi[nm] = _matmul(um, dpiece, mode="tn", out_dtype=F32, name=f"dw_in_{nm}")
    dw_qfig = dwi["qfig"].reshape(D_MODEL, HG_HEADS, 4, 128).transpose(0, 2, 1, 3).reshape(D_MODEL, 4 * D_MODEL)
    G["w_in"] = jnp.concatenate([dwi["z"], dwi["xbc"], dwi["dt"][:, :SSD_HEADS], dw_qfig, dwi["gates"]], axis=1)
    dh1, dnw = _rms_bwd(h1, W["mix_norm"], dum, dh2, "mix_norm_bwd")
    G["mix_norm"] = dnw[0:1]
    dh0, dnw, G["ffn1_w_gu"], G["ffn1_w_down"] = _ffn_bwd(h0, W["ffn1_norm"], W["ffn1_w_gu"], W["ffn1_w_down"], sv1, dh1, "ffn1")
    G["ffn1_norm"] = dnw[0:1]
    dh0 = dh0.reshape(B, Tp, D_MODEL)
    G["meta_tokens"] = jnp.sum(dh0[:, pad:CHUNK], axis=0)
    return loss, dh0[:, CHUNK:], G


ANY = pl.BlockSpec(memory_space=pl.ANY)


def _place():
    return lax.axis_index("x"), lax.axis_index("y"), lax.axis_index("c")


def _other_chips(x, y):
    return [(1 - x, y), (x, 1 - y), (1 - x, 1 - y)]


def _remote(src, dst, ssem, rsem, dev):
    return pltpu.make_async_remote_copy(src_ref=src, dst_ref=dst, send_sem=ssem, recv_sem=rsem,
                                        device_id=dev, device_id_type=MESH)


def _exchange8(buf, reduce, name):
    n, w = buf.shape

    def body(x_ref, *rest):
        if reduce:
            red_ref, out_ref, ssem, rsem = rest
        else:
            out_ref, ssem, rsem = rest
        x, y, c = _place()
        me = 4 * x + 2 * y + c
        out_ref[me] = x_ref[...]
        copies = []
        for k in range(1, 8):
            px = 1 - x if (k >> 2) & 1 else x
            py = 1 - y if (k >> 1) & 1 else y
            pc = 1 - c if k & 1 else c
            cp = _remote(x_ref, out_ref.at[me], ssem.at[k - 1], rsem.at[k - 1], (px, py, pc))
            cp.start()
            copies.append((cp, 4 * px + 2 * py + pc))
        for k, (cp, peer) in enumerate(copies):
            _remote(x_ref, out_ref.at[peer], ssem.at[k], rsem.at[k], (x, y, c)).wait_recv()
        for cp, _ in copies:
            cp.wait_send()
        if reduce:
            acc = out_ref[0]
            for d in range(1, 8):
                acc = acc + out_ref[d]
            red_ref[...] = acc

    vm = pl.BlockSpec(memory_space=pltpu.VMEM)
    g_shape = jax.ShapeDtypeStruct((8, n, w), F32)
    if reduce:
        out_shape, out_specs, scratch = [jax.ShapeDtypeStruct((n, w), F32)], [vm], [pltpu.VMEM((8, n, w), F32)]
    else:
        out_shape, out_specs, scratch = [g_shape], [vm], []
    return pl.pallas_call(
        body, in_specs=[vm], out_specs=out_specs, out_shape=out_shape,
        scratch_shapes=scratch + [pltpu.SemaphoreType.DMA((7,)), pltpu.SemaphoreType.DMA((7,))], name=name,
    )(buf)[0]


def _gather_big(shards, name):
    n = len(shards)
    half = [s.shape[0] // 2 for s in shards]

    def body(*refs):
        sh, full = refs[:n], refs[n:2 * n]
        lsem, ssem, rsem, fssem, frsem = refs[2 * n:]
        x, y, c = _place()
        q = 2 * x + y
        chips = _other_chips(x, y)
        piece = lambda s, qq, cc: full[s].at[qq, pl.ds(cc * half[s], half[s])]
        started = []
        for s in range(n):
            cp = pltpu.make_async_copy(sh[s], full[s].at[q], lsem.at[s])
            cp.start()
            started.append(cp)
        sends = []
        for j, (px, py) in enumerate(chips):
            for s in range(n):
                cp = _remote(sh[s].at[pl.ds(c * half[s], half[s])], piece(s, q, c), ssem.at[s, j], rsem.at[s, j], (px, py, c))
                cp.start()
                sends.append(cp)
        for j, (px, py) in enumerate(chips):
            for s in range(n):
                got = piece(s, 2 * px + py, c)
                _remote(got, got, ssem.at[s, j], rsem.at[s, j], (px, py, c)).wait_recv()
                cp = _remote(got, got, fssem.at[s, j], frsem.at[s, j], (x, y, 1 - c))
                cp.start()
                sends.append(cp)
        for j, (px, py) in enumerate(chips):
            for s in range(n):
                got = piece(s, 2 * px + py, 1 - c)
                _remote(got, got, fssem.at[s, j], frsem.at[s, j], (x, y, 1 - c)).wait_recv()
        for cp in sends:
            cp.wait_send()
        for cp in started:
            cp.wait()

    sems = [pltpu.SemaphoreType.DMA((n,))] + [pltpu.SemaphoreType.DMA((n, 3))] * 4
    return pl.pallas_call(
        body, in_specs=[ANY] * n, out_specs=[ANY] * n,
        out_shape=[jax.ShapeDtypeStruct((N_CHIPS,) + s.shape, s.dtype) for s in shards],
        scratch_shapes=sems, name=name,
    )(*shards)


def _pair_swap(parts, name):
    n = len(parts)
    half = [p.shape[1] // 2 for p in parts]

    def body(*refs):
        src, got = refs[:n], refs[n:2 * n]
        ssem, rsem = refs[2 * n:]
        x, y, c = _place()
        copies = []
        for s in range(n):
            cp = _remote(src[s].at[pl.ds(0, N_CHIPS), pl.ds((1 - c) * half[s], half[s])], got[s], ssem.at[s], rsem.at[s], (x, y, 1 - c))
            cp.start()
            copies.append(cp)
        for cp in copies:
            cp.wait_recv()
        for cp in copies:
            cp.wait_send()

    return pl.pallas_call(
        body, in_specs=[ANY] * n, out_specs=[ANY] * n,
        out_shape=[jax.ShapeDtypeStruct((N_CHIPS, h, p.shape[2]), p.dtype) for p, h in zip(parts, half)],
        scratch_shapes=[pltpu.SemaphoreType.DMA((n,)), pltpu.SemaphoreType.DMA((n,))], name=name,
    )(*parts)


def _to_owners(sums, name):
    n = len(sums)

    def body(*refs):
        src, got = refs[:n], refs[n:2 * n]
        lsem, ssem, rsem = refs[2 * n:]
        x, y, c = _place()
        q = 2 * x + y
        chips = _other_chips(x, y)
        started, sends = [], []
        for s in range(n):
            cp = pltpu.make_async_copy(src[s].at[q], got[s].at[q], lsem.at[s])
            cp.start()
            started.append(cp)
        for j, (px, py) in enumerate(chips):
            for s in range(n):
                cp = _remote(src[s].at[2 * px + py], got[s].at[q], ssem.at[s, j], rsem.at[s, j], (px, py, c))
                cp.start()
                sends.append(cp)
        for j, (px, py) in enumerate(chips):
            for s in range(n):
                slot = got[s].at[2 * px + py]
                _remote(slot, slot, ssem.at[s, j], rsem.at[s, j], (px, py, c)).wait_recv()
        for cp in sends:
            cp.wait_send()
        for cp in started:
            cp.wait()

    return pl.pallas_call(
        body, in_specs=[ANY] * n, out_specs=[ANY] * n,
        out_shape=[jax.ShapeDtypeStruct(s.shape, s.dtype) for s in sums],
        scratch_shapes=[pltpu.SemaphoreType.DMA((n,)), pltpu.SemaphoreType.DMA((n, 3)), pltpu.SemaphoreType.DMA((n, 3))],
        name=name,
    )(*sums)


def _pair_join(halves, name):
    n = len(halves)

    def body(*refs):
        src, out = refs[:n], refs[n:2 * n]
        lsem, ssem, rsem = refs[2 * n:]
        x, y, c = _place()
        started, sends = [], []
        for s in range(n):
            h = halves[s].shape[0]
            mine = out[s].at[pl.ds(c * h, h)]
            cp = pltpu.make_async_copy(src[s], mine, lsem.at[s])
            cp.start()
            started.append(cp)
            cp = _remote(src[s], mine, ssem.at[s], rsem.at[s], (x, y, 1 - c))
            cp.start()
            sends.append(cp)
        for s in range(n):
            h = halves[s].shape[0]
            theirs = out[s].at[pl.ds((1 - c) * h, h)]
            _remote(src[s], theirs, ssem.at[s], rsem.at[s], (x, y, 1 - c)).wait_recv()
        for cp in sends:
            cp.wait_send()
        for cp in started:
            cp.wait()

    return pl.pallas_call(
        body, in_specs=[ANY] * n, out_specs=[ANY] * n,
        out_shape=[jax.ShapeDtypeStruct((2 * h.shape[0], h.shape[1]), h.dtype) for h in halves],
        scratch_shapes=[pltpu.SemaphoreType.DMA((n,))] * 3, name=name,
    )(*halves)


WIRE = BF16


def _row_tile(h):
    return _pick(h, (256, 272, 128, 16))


def _add_pair(part, got, c, name):
    _, h, w = got.shape
    tr = _row_tile(h)
    nt = h // tr

    def body(c_ref, p_ref, g_ref, o_ref):
        o_ref[...] = (p_ref[...] + g_ref[...].astype(F32)).astype(o_ref.dtype)

    return pl.pallas_call(
        body,
        grid_spec=pltpu.PrefetchScalarGridSpec(
            num_scalar_prefetch=1, grid=(N_CHIPS, nt),
            in_specs=[pl.BlockSpec((None, tr, w), lambda q, i, c_ref: (q, c_ref[0] * nt + i, 0)),
                      pl.BlockSpec((None, tr, w), lambda q, i, c_ref: (q, i, 0))],
            out_specs=pl.BlockSpec((None, tr, w), lambda q, i, c_ref: (q, i, 0))),
        out_shape=jax.ShapeDtypeStruct(got.shape, WIRE), name=name,
        compiler_params=_cparams(("arbitrary", "arbitrary")),
    )(c.reshape(1).astype(jnp.int32), part, got)


def _sum_chips(slots, name):
    _, h, w = slots.shape
    tr = _row_tile(h)

    def body(s_ref, o_ref):
        o_ref[...] = ((s_ref[0].astype(F32) + s_ref[1].astype(F32)) + s_ref[2].astype(F32)) + s_ref[3].astype(F32)

    return pl.pallas_call(
        body, grid=(h // tr,), in_specs=[pl.BlockSpec((N_CHIPS, tr, w), lambda i: (0, i, 0))],
        out_specs=pl.BlockSpec((tr, w), lambda i: (i, 0)), out_shape=jax.ShapeDtypeStruct((h, w), F32), name=name,
        compiler_params=_cparams(("arbitrary",)),
    )(slots)


def _reduce_to_owners(parts, c):
    got = _pair_swap(parts, "grad_pair_swap")
    sums = [_add_pair(p, g, c, f"grad_pair_add{i}") for i, (p, g) in enumerate(zip(parts, got))]
    slots = _to_owners(sums, "grad_to_owners")
    halves = [_sum_chips(s, f"grad_sum_chips{i}") for i, s in enumerate(slots)]
    return _pair_join(halves, "grad_pair_join")


WEIGHTS = ("meta_tokens", "ffn1_norm", "ffn1_w_gu", "ffn1_w_down", "mix_norm", "w_in", "ssd_conv_w", "ssd_conv_b",
           "ssd_dt_bias", "ssd_a_log", "ssd_d", "ssd_norm", "hg_lower_bound", "hg_norm", "w_branch_a", "w_branch_b",
           "w_out", "ffn2_norm", "ffn2_w_gu", "ffn2_w_down", "final_norm")
BIG = ("ffn1_w_gu", "ffn1_w_down", "w_in", "w_branch_a", "w_branch_b", "w_out", "ffn2_w_gu", "ffn2_w_down")
ROW_SHARDED = ("ffn1_w_down", "ffn2_w_down", "w_branch_a", "w_branch_b", "w_out")
SMALL = tuple(n for n in WEIGHTS if n not in BIG)
SMALL_ROWS = 24


def _rows1024(a):
    flat = a.reshape(-1)
    n = -(-flat.shape[0] // 1024) * 1024
    return jnp.pad(flat, (0, n - flat.shape[0])).reshape(-1, 1024)


def _pack_small(d):
    rows = jnp.concatenate([_rows1024(d[n]) for n in SMALL], axis=0)
    return jnp.pad(rows, ((0, SMALL_ROWS - rows.shape[0]), (0, 0)))


def _unpack_small(packed, like):
    out, r = {}, 0
    for n in SMALL:
        size = like[n].size
        nr = -(-size // 1024)
        out[n] = packed[r:r + nr].reshape(-1)[:size].reshape(like[n].shape)
        r += nr
    return out


def kernel(x, meta_tokens, ffn1_norm, ffn1_w_gu, ffn1_w_down, mix_norm, w_in, ssd_conv_w, ssd_conv_b, ssd_dt_bias, ssd_a_log, ssd_d, ssd_norm, hg_lower_bound, hg_norm, w_branch_a, w_branch_b, w_out, ffn2_norm, ffn2_w_gu, ffn2_w_down, final_norm, loss_target, m_meta_tokens, m_ffn1_norm, m_ffn1_w_gu, m_ffn1_w_down, m_mix_norm, m_w_in, m_ssd_conv_w, m_ssd_conv_b, m_ssd_dt_bias, m_ssd_a_log, m_ssd_d, m_ssd_norm, m_hg_lower_bound, m_hg_norm, m_w_branch_a, m_w_branch_b, m_w_out, m_ffn2_norm, m_ffn2_w_gu, m_ffn2_w_down, m_final_norm, v_meta_tokens, v_ffn1_norm, v_ffn1_w_gu, v_ffn1_w_down, v_mix_norm, v_w_in, v_ssd_conv_w, v_ssd_conv_b, v_ssd_dt_bias, v_ssd_a_log, v_ssd_d, v_ssd_norm, v_hg_lower_bound, v_hg_norm, v_w_branch_a, v_w_branch_b, v_w_out, v_ffn2_norm, v_ffn2_w_gu, v_ffn2_w_down, v_final_norm):
    P = dict(zip(WEIGHTS, (meta_tokens, ffn1_norm, ffn1_w_gu, ffn1_w_down, mix_norm, w_in, ssd_conv_w, ssd_conv_b, ssd_dt_bias, ssd_a_log, ssd_d, ssd_norm, hg_lower_bound, hg_norm, w_branch_a, w_branch_b, w_out, ffn2_norm, ffn2_w_gu, ffn2_w_down, final_norm)))
    M = dict(zip(WEIGHTS, (m_meta_tokens, m_ffn1_norm, m_ffn1_w_gu, m_ffn1_w_down, m_mix_norm, m_w_in, m_ssd_conv_w, m_ssd_conv_b, m_ssd_dt_bias, m_ssd_a_log, m_ssd_d, m_ssd_norm, m_hg_lower_bound, m_hg_norm, m_w_branch_a, m_w_branch_b, m_w_out, m_ffn2_norm, m_ffn2_w_gu, m_ffn2_w_down, m_final_norm)))
    V = dict(zip(WEIGHTS, (v_meta_tokens, v_ffn1_norm, v_ffn1_w_gu, v_ffn1_w_down, v_mix_norm, v_w_in, v_ssd_conv_w, v_ssd_conv_b, v_ssd_dt_bias, v_ssd_a_log, v_ssd_d, v_ssd_norm, v_hg_lower_bound, v_hg_norm, v_w_branch_a, v_w_branch_b, v_w_out, v_ffn2_norm, v_ffn2_w_gu, v_ffn2_w_down, v_final_norm)))
    cx, cy, cc = _place()
    q = 2 * cx + cy

    mine = jnp.concatenate([meta_tokens.reshape(4, 1024), ssd_conv_w.reshape(2, 1024), jnp.zeros((2, 1024), F32)], axis=0)
    every = _exchange8(mine, False, "gather_small")
    meta_full = jnp.concatenate([every[2 * k, 0:4].reshape(N_META, 256) for k in range(N_CHIPS)], axis=1)
    conv_w_full = jnp.concatenate([every[2 * k, 4:6].reshape(SSD_CONV, 512) for k in range(N_CHIPS)], axis=1)

    rows = jnp.concatenate([P[n][0] for n in ROW_SHARDED], axis=0).astype(BF16)
    gu1, gu2, w_in_all, rows_all = _gather_big(
        [ffn1_w_gu[0].astype(BF16), ffn2_w_gu[0].astype(BF16), w_in[0].astype(BF16), rows], "gather_weights")
    W = {n: P[n] for n in SMALL}
    W["meta_tokens"], W["ssd_conv_w"] = meta_full, conv_w_full
    W["ffn1_w_gu"], W["ffn2_w_gu"] = gu1, gu2
    W["w_in"] = _split_w_in(w_in_all.transpose(1, 0, 2).reshape(D_MODEL, -1))
    r = 0
    for n in ROW_SHARDED:
        nr = P[n].shape[1]
        W[n] = rows_all[:, r:r + nr].reshape(N_CHIPS * nr, D_MODEL)
        r += nr

    loss8, grad_x, G = _local_step(x, loss_target, W)

    small = jnp.concatenate(
        [G["meta_tokens"]] + [_rows1024(G[n]) for n in SMALL if n != "meta_tokens"] + [_rows1024(loss8[0:1, 0:1])], axis=0)
    small = jnp.pad(small, ((0, 40 - small.shape[0]), (0, 0)))
    small = _exchange8(small, True, "reduce_small")
    Gs = {"meta_tokens": small[0:N_META]}
    r = N_META
    for n in SMALL:
        if n == "meta_tokens":
            continue
        nr = -(-G[n].size // 1024)
        Gs[n] = small[r:r + nr].reshape(-1)[:G[n].size].reshape(G[n].shape)
        r += nr
    loss = small[r, 0]
    Gs["meta_tokens"] = lax.dynamic_slice(Gs["meta_tokens"], (0, 256 * q), (N_META, 256))
    Gs["ssd_conv_w"] = lax.dynamic_slice(Gs["ssd_conv_w"], (0, 512 * q), (SSD_CONV, 512))[None]
    Gs = {n: Gs[n].reshape(P[n].shape) for n in SMALL}

    w_in_parts = G["w_in"].reshape(D_MODEL, N_CHIPS, -1).transpose(1, 0, 2)
    row_parts = jnp.concatenate([G[n].reshape(N_CHIPS, -1, D_MODEL) for n in ROW_SHARDED], axis=1)
    g_gu1, g_gu2, g_w_in, g_rows = _reduce_to_owners([G["ffn1_w_gu"], G["ffn2_w_gu"], w_in_parts, row_parts], cc)
    Gb = {"ffn1_w_gu": g_gu1, "ffn2_w_gu": g_gu2, "w_in": g_w_in}
    r = 0
    for n in ROW_SHARDED:
        nr = P[n].shape[1]
        Gb[n] = g_rows[r:r + nr]
        r += nr

    grads, delta, new_m, new_v = dict(Gs), {}, {}, {}
    d_s, m_s, v_s = _adamw(_pack_small(P), _pack_small(Gs), _pack_small(M), _pack_small(V), "adamw_small")
    delta.update(_unpack_small(d_s, P))
    new_m.update(_unpack_small(m_s, P))
    new_v.update(_unpack_small(v_s, P))
    for n in BIG:
        d_, m_, v_ = _adamw(P[n][0], Gb[n], M[n][0], V[n][0], f"adamw_{n}")
        grads[n], delta[n], new_m[n], new_v[n] = Gb[n][None], d_[None], m_[None], v_[None]
    return (loss, grad_x, *[grads[n] for n in WEIGHTS], *[delta[n] for n in WEIGHTS],
            *[new_m[n] for n in WEIGHTS], *[new_v[n] for n in WEIGHTS])
```

```python
import functools

import jax
import jax.numpy as jnp
from jax import lax
from jax.experimental import pallas as pl
from jax.experimental.pallas import tpu as pltpu

F32 = jnp.float32
BF16 = jnp.bfloat16
HIGHEST = lax.Precision.HIGHEST
MESH = pl.DeviceIdType.MESH

D_MODEL = 1024
N_META = 16
EPS = 1e-6
SSD_HEADS = 16
SSD_HEAD_DIM = 64
SSD_INNER = 1024
SSD_GROUPS = 4
SSD_STATE = 128
SSD_CONV = 4
SSD_CONV_CH = 2048
HG_HEADS = 8
HG_SUB = 32
CHUNK = 128
D_FF = 2816
N_CHIPS = 4
IN_SIZES = (1024, 2048, 16, 1024, 1024, 1024, 1024, 1024, 1024)
ADAM_LR = 0.001
ADAM_B1 = 0.9
ADAM_B2 = 0.999
ADAM_EPS = 1e-08
ADAM_WD = 0.01
ADAM_STEP = 10
VMEM_LIMIT = 56 * 1024 * 1024


def _cparams(sem=None):
    return pltpu.CompilerParams(dimension_semantics=sem, vmem_limit_bytes=VMEM_LIMIT)


def _pick(n, cands):
    for c in cands:
        if n % c == 0:
            return c
    return n


def _dg(a, b, ca, cb):
    return lax.dot_general(a.astype(BF16), b.astype(BF16), (((ca,), (cb,)), ((), ())), preferred_element_type=F32)


@jax.custom_vjp
def _mm(a, b):
    return _dg(a, b, 1, 0)


def _mm_fwd(a, b):
    return _dg(a, b, 1, 0), (a, b)


def _mm_bwd(r, g):
    a, b = r
    return _dg(g, b, 1, 1), _dg(a, g, 0, 0)


_mm.defvjp(_mm_fwd, _mm_bwd)


@jax.custom_vjp
def _mm_nt(a, b):
    return _dg(a, b, 1, 1)


def _mm_nt_fwd(a, b):
    return _dg(a, b, 1, 1), (a, b)


def _mm_nt_bwd(r, g):
    a, b = r
    return _dg(g, b, 1, 0), _dg(g, a, 0, 0)


_mm_nt.defvjp(_mm_nt_fwd, _mm_nt_bwd)


@jax.custom_vjp
def _mm_tn(a, b):
    return _dg(a, b, 0, 0)


def _mm_tn_fwd(a, b):
    return _dg(a, b, 0, 0), (a, b)


def _mm_tn_bwd(r, g):
    a, b = r
    return _dg(b, g, 1, 1), _dg(a, g, 1, 0)


_mm_tn.defvjp(_mm_tn_fwd, _mm_tn_bwd)


def _silu(x):
    return x * jax.nn.sigmoid(x)


def _softplus(x):
    return jnp.maximum(x, 0.0) + jnp.log(1.0 + jnp.exp(-jnp.abs(x)))


def _tril(n):
    ri = lax.broadcasted_iota(jnp.int32, (n, n), 0)
    ci = lax.broadcasted_iota(jnp.int32, (n, n), 1)
    return ri >= ci


def _row_of(m, r):
    sub = lax.broadcasted_iota(jnp.int32, (m.shape[0], 1), 0)
    return jnp.sum(jnp.where(sub == r, m, 0.0), axis=0, keepdims=True)


def _col_of(m, c):
    lane = lax.broadcasted_iota(jnp.int32, (1, m.shape[1]), 1)
    return jnp.sum(jnp.where(lane == c, m, 0.0), axis=1, keepdims=True)


def _matmul(a, b, *, mode, out_dtype, name, alpha=1.0, res=None, tm=None, tn=None, tk=None, out_groups=None):
    b3 = b.ndim == 3
    if mode == "nn":
        M, K = a.shape
        G = b.shape[0] if b3 else 1
        Ng = b.shape[-1]
        N = G * Ng
    elif mode == "nt":
        M, K = a.shape
        G = b.shape[0] if b3 else 1
        N = b.shape[-2]
        Kg = b.shape[-1]
        assert G * Kg == K
    else:
        K, M = a.shape
        N = b.shape[1]
        G = out_groups or 1
        Ng = N // G
    if mode == "tn":
        tm = tm or _pick(M, (1408, 1024, 512, 256, 128))
        tk = tk or _pick(K, (544, 256, 128))
        tn = tn or _pick(Ng, (1408, 1024, 512, 256, 128))
    else:
        tm = tm or _pick(M, (1088, 544, 256, 128))
        if mode == "nn":
            tn = tn or _pick(Ng, (1408, 512, 256, 128))
            tk = K
        else:
            tn = tn or _pick(N, (512, 256, 128))
            tk = tk or (Kg if b3 else K)
    nm, nn_, nk = M // tm, N // tn, K // tk
    assert nm * tm == M and nn_ * tn == N and nk * tk == K, (name, M, N, K, tm, tn, tk)

    if mode == "nn":
        a_spec = pl.BlockSpec((tm, tk), lambda i, j, k: (i, k))
        if b3:
            ns = Ng // tn
            b_spec = pl.BlockSpec((None, tk, tn), lambda i, j, k: (j // ns, k, j % ns))
        else:
            b_spec = pl.BlockSpec((tk, tn), lambda i, j, k: (k, j))
        ca, cb = 1, 0
    elif mode == "nt":
        a_spec = pl.BlockSpec((tm, tk), lambda i, j, k: (i, k))
        if b3:
            ks = Kg // tk
            b_spec = pl.BlockSpec((None, tn, tk), lambda i, j, k: (k // ks, j, k % ks))
        else:
            b_spec = pl.BlockSpec((tn, tk), lambda i, j, k: (j, k))
        ca, cb = 1, 1
    else:
        a_spec = pl.BlockSpec((tk, tm), lambda i, j, k: (k, i))
        b_spec = pl.BlockSpec((tk, tn), lambda i, j, k: (k, j))
        ca, cb = 0, 0
    if mode == "tn" and G > 1:
        ns = Ng // tn
        o_spec = pl.BlockSpec((None, tm, tn), lambda i, j, k: (j // ns, i, j % ns))
        out_shape = jax.ShapeDtypeStruct((G, M, Ng), out_dtype)
    else:
        o_spec = pl.BlockSpec((tm, tn), lambda i, j, k: (i, j))
        out_shape = jax.ShapeDtypeStruct((M, N), out_dtype)
    in_specs = [a_spec, b_spec]
    args = [a, b]
    if res is not None:
        in_specs.append(pl.BlockSpec((tm, tn), lambda i, j, k: (i, j)))
        args.append(res)
    has_res = res is not None

    def body(*refs):
        a_ref, b_ref = refs[0], refs[1]
        o_ref, acc_ref = refs[-2], refs[-1]
        k = pl.program_id(2)

        @pl.when(k == 0)
        def _():
            acc_ref[...] = jnp.zeros_like(acc_ref)

        acc_ref[...] += _dg(a_ref[...], b_ref[...], ca, cb)

        @pl.when(k == nk - 1)
        def _():
            o = acc_ref[...]
            if alpha != 1.0:
                o = o * alpha
            if has_res:
                o = o + refs[2][...]
            o_ref[...] = o.astype(o_ref.dtype)

    return pl.pallas_call(
        body, grid=(nm, nn_, nk), in_specs=in_specs, out_specs=o_spec, out_shape=out_shape,
        scratch_shapes=[pltpu.VMEM((tm, tn), F32)], name=name,
        compiler_params=_cparams(("parallel", "parallel", "arbitrary")),
    )(*args)


def _rms_fn(h, w):
    r = lax.rsqrt(jnp.mean(h * h, axis=-1, keepdims=True) + EPS)
    return h * r * w


def _swiglu_fn(gu):
    g = gu[:, :D_FF].astype(F32)
    u = gu[:, D_FF:].astype(F32)
    return _silu(g) * u


def _merge_fn(pa, pb, gates):
    return jax.nn.sigmoid(gates[:, :D_MODEL]) * pa + jax.nn.sigmoid(gates[:, D_MODEL:]) * pb


def _rows_call(body, *, rows, tr, ins, outs, accs=(), name):
    n = rows // tr
    assert n * tr == rows

    def spec(x):
        if isinstance(x, tuple):
            shp = x[1].shape
            return pl.BlockSpec(shp, lambda i: (0,) * len(shp))
        return pl.BlockSpec((tr, x.shape[1]), lambda i: (i, 0))

    in_specs = [spec(x) for x in ins]
    args = [x[1] if isinstance(x, tuple) else x for x in ins]
    out_specs = [spec(x) for x in outs] + [pl.BlockSpec(x.shape, lambda i: (0,) * len(x.shape)) for x in accs]
    out_shape = [x[1] if isinstance(x, tuple) else x for x in outs] + list(accs)
    return pl.pallas_call(
        body, grid=(n,), in_specs=in_specs, out_specs=out_specs, out_shape=out_shape, name=name,
        compiler_params=_cparams(("arbitrary",)),
    )(*args)


def _acc_rows(ref, val):
    @pl.when(pl.program_id(0) == 0)
    def _():
        ref[...] = jnp.zeros_like(ref)

    ref[0:1, :] += val


def _rms_fwd(h, w, name):
    def body(h_ref, w_ref, o_ref):
        o_ref[...] = _rms_fn(h_ref[...], w_ref[...]).astype(o_ref.dtype)

    R = h.shape[0]
    return _rows_call(body, rows=R, tr=_pick(R, (256, 128)), ins=[h, ("full", w)],
                      outs=[jax.ShapeDtypeStruct(h.shape, BF16)], name=name)[0]


def _rms_bwd(h, w, dn, dres, name):
    def body(h_ref, w_ref, dn_ref, dres_ref, dh_ref, dw_ref):
        _, vjp = jax.vjp(_rms_fn, h_ref[...], w_ref[...])
        dh, dw = vjp(dn_ref[...].astype(F32))
        dh_ref[...] = dh + dres_ref[...]
        _acc_rows(dw_ref, dw)

    R = h.shape[0]
    return _rows_call(body, rows=R, tr=_pick(R, (256, 128)), ins=[h, ("full", w), dn, dres],
                      outs=[jax.ShapeDtypeStruct(h.shape, F32)], accs=[jax.ShapeDtypeStruct((8, D_MODEL), F32)], name=name)


def _swiglu_fwd(gu, name):
    def body(gu_ref, o_ref):
        o_ref[...] = _swiglu_fn(gu_ref[...]).astype(o_ref.dtype)

    R = gu.shape[0]
    return _rows_call(body, rows=R, tr=_pick(R, (256, 128)), ins=[gu],
                      outs=[jax.ShapeDtypeStruct((R, D_FF), BF16)], name=name)[0]


def _swiglu_bwd(gu, da, name):
    def body(gu_ref, da_ref, o_ref):
        _, vjp = jax.vjp(_swiglu_fn, gu_ref[...].astype(F32))
        (dgu,) = vjp(da_ref[...].astype(F32))
        o_ref[...] = dgu.astype(o_ref.dtype)

    R = gu.shape[0]
    return _rows_call(body, rows=R, tr=_pick(R, (256, 128)), ins=[gu, da],
                      outs=[jax.ShapeDtypeStruct(gu.shape, BF16)], name=name)[0]


def _merge_fwd(pa, pb, gates, name):
    def body(pa_ref, pb_ref, g_ref, o_ref):
        o_ref[...] = _merge_fn(pa_ref[...], pb_ref[...], g_ref[...]).astype(o_ref.dtype)

    R = pa.shape[0]
    return _rows_call(body, rows=R, tr=_pick(R, (256, 128)), ins=[pa, pb, gates],
                      outs=[jax.ShapeDtypeStruct(pa.shape, BF16)], name=name)[0]


def _merge_bwd(pa, pb, gates, dm, name):
    def body(pa_ref, pb_ref, g_ref, dm_ref, dpa_ref, dpb_ref, dg_ref):
        _, vjp = jax.vjp(_merge_fn, pa_ref[...], pb_ref[...], g_ref[...])
        dpa, dpb, dg = vjp(dm_ref[...].astype(F32))
        dpa_ref[...] = dpa.astype(dpa_ref.dtype)
        dpb_ref[...] = dpb.astype(dpb_ref.dtype)
        dg_ref[...] = dg.astype(dg_ref.dtype)

    R = pa.shape[0]
    return _rows_call(body, rows=R, tr=_pick(R, (256, 128)), ins=[pa, pb, gates, dm],
                      outs=[jax.ShapeDtypeStruct(pa.shape, BF16), jax.ShapeDtypeStruct(pa.shape, BF16),
                            jax.ShapeDtypeStruct(gates.shape, BF16)], name=name)


def _loss_head(h3, w, target, nseq, name):
    Tp = h3.shape[0] // nseq
    nc = Tp // CHUNK

    def fn(h, w_, t, valid):
        y = _rms_fn(h, w_)
        e = (y - t) * valid
        return 0.5 * jnp.sum(jnp.mean(e * e, axis=-1, keepdims=True))

    def body(h_ref, w_ref, t_ref, loss_ref, dh_ref, dw_ref):
        b, c = pl.program_id(0), pl.program_id(1)
        valid = (c >= 1).astype(F32)
        t = t_ref[...]
        loss, vjp = jax.vjp(lambda h, w_: fn(h, w_, t, valid), h_ref[...], w_ref[...])
        dh, dw = vjp(jnp.ones((), F32))
        dh_ref[...] = dh

        @pl.when((b == 0) & (c == 0))
        def _():
            loss_ref[...] = jnp.zeros_like(loss_ref)
            dw_ref[...] = jnp.zeros_like(dw_ref)

        loss_ref[...] += jnp.full(loss_ref.shape, loss, F32)
        dw_ref[0:1, :] += dw

    return pl.pallas_call(
        body, grid=(nseq, nc),
        in_specs=[pl.BlockSpec((CHUNK, D_MODEL), lambda b, c: (b * nc + c, 0)),
                  pl.BlockSpec((1, D_MODEL), lambda b, c: (0, 0)),
                  pl.BlockSpec((None, CHUNK, D_MODEL), lambda b, c: (b, jnp.maximum(c - 1, 0), 0))],
        out_specs=[pl.BlockSpec((8, 128), lambda b, c: (0, 0)),
                   pl.BlockSpec((CHUNK, D_MODEL), lambda b, c: (b * nc + c, 0)),
                   pl.BlockSpec((8, D_MODEL), lambda b, c: (0, 0))],
        out_shape=[jax.ShapeDtypeStruct((8, 128), F32), jax.ShapeDtypeStruct(h3.shape, F32),
                   jax.ShapeDtypeStruct((8, D_MODEL), F32)],
        name=name, compiler_params=_cparams(("arbitrary", "arbitrary")),
    )(h3, w, target)


CONV_TILE = 512
CONV_HALO = 8


def _conv_fwd(xbc, w, b, pad, name):
    B, Tp, C = xbc.shape
    nch = Tp // CHUNK

    def body(x_ref, w_ref, b_ref, o_ref, xp):
        xp[0:CONV_HALO, :] = jnp.zeros((CONV_HALO, CONV_TILE), F32)
        xp[CONV_HALO:, :] = x_ref[...]
        for c in range(nch):
            acc = jnp.zeros((CHUNK, CONV_TILE), F32) + b_ref[...]
            for k in range(SSD_CONV):
                acc = acc + w_ref[k:k + 1, :] * xp[pl.ds(CONV_HALO + CHUNK * c - (SSD_CONV - 1) + k, CHUNK), :]
            row = CHUNK * c + lax.broadcasted_iota(jnp.int32, (CHUNK, 1), 0)
            o_ref[pl.ds(CHUNK * c, CHUNK), :] = jnp.where(row >= pad, _silu(acc), 0.0)

    return pl.pallas_call(
        body, grid=(B, C // CONV_TILE),
        in_specs=[pl.BlockSpec((None, Tp, CONV_TILE), lambda i, j: (i, 0, j)),
                  pl.BlockSpec((SSD_CONV, CONV_TILE), lambda i, j: (0, j)),
                  pl.BlockSpec((1, CONV_TILE), lambda i, j: (0, j))],
        out_specs=pl.BlockSpec((None, Tp, CONV_TILE), lambda i, j: (i, 0, j)),
        out_shape=jax.ShapeDtypeStruct(xbc.shape, F32),
        scratch_shapes=[pltpu.VMEM((Tp + CONV_HALO, CONV_TILE), F32)],
        name=name, compiler_params=_cparams(("arbitrary", "arbitrary")),
    )(xbc, w, b)


def _conv_bwd(xbc, w, b, dact, pad, name):
    B, Tp, C = xbc.shape
    nch = Tp // CHUNK

    def body(x_ref, w_ref, b_ref, da_ref, dx_ref, dw_ref, db_ref, xp, dp):
        bi = pl.program_id(1)
        xp[0:CONV_HALO, :] = jnp.zeros((CONV_HALO, CONV_TILE), F32)
        xp[CONV_HALO:, :] = x_ref[...]
        dp[pl.ds(Tp, CONV_HALO), :] = jnp.zeros((CONV_HALO, CONV_TILE), F32)
        dws = [jnp.zeros((1, CONV_TILE), F32) for _ in range(SSD_CONV)]
        dbs = jnp.zeros((1, CONV_TILE), F32)
        for c in range(nch):
            xs = [xp[pl.ds(CONV_HALO + CHUNK * c - (SSD_CONV - 1) + k, CHUNK), :] for k in range(SSD_CONV)]
            acc = jnp.zeros((CHUNK, CONV_TILE), F32) + b_ref[...]
            for k in range(SSD_CONV):
                acc = acc + w_ref[k:k + 1, :] * xs[k]
            row = CHUNK * c + lax.broadcasted_iota(jnp.int32, (CHUNK, 1), 0)
            sg = jax.nn.sigmoid(acc)
            dpre = jnp.where(row >= pad, da_ref[pl.ds(CHUNK * c, CHUNK), :] * (sg * (1.0 + acc * (1.0 - sg))), 0.0)
            dp[pl.ds(CHUNK * c, CHUNK), :] = dpre
            dbs = dbs + jnp.sum(dpre, axis=0, keepdims=True)
            for k in range(SSD_CONV):
                dws[k] = dws[k] + jnp.sum(dpre * xs[k], axis=0, keepdims=True)
        for c in range(nch):
            acc = jnp.zeros((CHUNK, CONV_TILE), F32)
            for k in range(SSD_CONV):
                acc = acc + w_ref[k:k + 1, :] * dp[pl.ds(CHUNK * c + (SSD_CONV - 1) - k, CHUNK), :]
            dx_ref[pl.ds(CHUNK * c, CHUNK), :] = acc

        @pl.when(bi == 0)
        def _():
            dw_ref[...] = jnp.zeros_like(dw_ref)
            db_ref[...] = jnp.zeros_like(db_ref)

        for k in range(SSD_CONV):
            dw_ref[k:k + 1, :] += dws[k]
        db_ref[0:1, :] += dbs

    return pl.pallas_call(
        body, grid=(C // CONV_TILE, B),
        in_specs=[pl.BlockSpec((None, Tp, CONV_TILE), lambda j, i: (i, 0, j)),
                  pl.BlockSpec((SSD_CONV, CONV_TILE), lambda j, i: (0, j)),
                  pl.BlockSpec((1, CONV_TILE), lambda j, i: (0, j)),
                  pl.BlockSpec((None, Tp, CONV_TILE), lambda j, i: (i, 0, j))],
        out_specs=[pl.BlockSpec((None, Tp, CONV_TILE), lambda j, i: (i, 0, j)),
                   pl.BlockSpec((8, CONV_TILE), lambda j, i: (0, j)),
                   pl.BlockSpec((8, CONV_TILE), lambda j, i: (0, j))],
        out_shape=[jax.ShapeDtypeStruct(xbc.shape, F32), jax.ShapeDtypeStruct((8, C), F32),
                   jax.ShapeDtypeStruct((8, C), F32)],
        scratch_shapes=[pltpu.VMEM((Tp + CONV_HALO, CONV_TILE), F32), pltpu.VMEM((Tp + CONV_HALO, CONV_TILE), F32)],
        name=name, compiler_params=_cparams(("arbitrary", "arbitrary")),
    )(xbc, w, b, dact)


def _ssd_chunk(xs, bm, cm, dtr, z, state, dt_bias, a_log, dskip, norm_w, valid):
    Q = xs.shape[0]
    lane = lax.broadcasted_iota(jnp.int32, (1, 128), 1)
    dt = jnp.where(lane < SSD_HEADS, _softplus(dtr + dt_bias), 0.0) * valid
    a = dt * (-jnp.exp(a_log))
    tril = _tril(Q)
    cs = jnp.dot(tril.astype(F32), a, precision=HIGHEST)
    cs_t = cs.T
    cs_end = _row_of(cs, Q - 1)
    low = lane < SSD_HEAD_DIM
    low_rows = lax.broadcasted_iota(jnp.int32, (128, 1), 0) < SSD_HEAD_DIM
    ys, new_state = [], []
    for g in range(SSD_GROUPS):
        bg = bm[:, 128 * g:128 * (g + 1)]
        cg = cm[:, 128 * g:128 * (g + 1)]
        cb = _mm_nt(cg, bg)
        for pr in range(2):
            p = 2 * g + pr
            h0, h1 = 2 * p, 2 * p + 1
            xp = xs[:, 128 * p:128 * (p + 1)]
            c0, c1 = _col_of(cs, h0), _col_of(cs, h1)
            e0, e1 = _col_of(cs_end, h0), _col_of(cs_end, h1)
            xd = xp * jnp.where(low, _col_of(dt, h0), _col_of(dt, h1))
            l0 = jnp.exp(jnp.where(tril, c0 - _row_of(cs_t, h0), -1e30))
            l1 = jnp.exp(jnp.where(tril, c1 - _row_of(cs_t, h1), -1e30))
            y_diag = jnp.where(low, _mm(cb * l0, xd), _mm(cb * l1, xd))
            to_end = jnp.where(low, jnp.exp(e0 - c0), jnp.exp(e1 - c1))
            sp = state[128 * p:128 * (p + 1), :]
            y_off = _mm_nt(cg, sp) * jnp.where(low, jnp.exp(c0), jnp.exp(c1))
            new_state.append(sp * jnp.where(low_rows, jnp.exp(e0), jnp.exp(e1)) + _mm_tn(xd * to_end, bg))
            ys.append(y_diag + y_off + xp * jnp.where(low, _col_of(dskip, h0), _col_of(dskip, h1)))
    y = jnp.concatenate(ys, axis=1) * _silu(z)
    gw = SSD_INNER // SSD_GROUPS
    outs = []
    for g in range(SSD_GROUPS):
        blk = y[:, gw * g:gw * (g + 1)]
        outs.append(blk * lax.rsqrt(jnp.mean(blk * blk, axis=-1, keepdims=True) + EPS))
    return jnp.concatenate(outs, axis=1) * norm_w, jnp.concatenate(new_state, axis=0)


def _valid_rows(c, pad):
    row = c * CHUNK + lax.broadcasted_iota(jnp.int32, (CHUNK, 1), 0)
    return (row >= pad).astype(F32)


def _ssd_fwd(xact, dtr, z, dt_bias, a_log, dskip, norm_w, pad, name):
    B, Tp, _ = xact.shape
    nc = Tp // CHUNK

    def body(xs_ref, bm_ref, cm_ref, dt_ref, z_ref, db_ref, al_ref, ds_ref, nw_ref, y_ref, save_ref, st):
        c = pl.program_id(1)

        @pl.when(c == 0)
        def _():
            st[...] = jnp.zeros_like(st)

        s0 = st[...]
        save_ref[...] = s0
        y, s1 = _ssd_chunk(xs_ref[...], bm_ref[...], cm_ref[...], dt_ref[...], z_ref[...], s0, db_ref[...],
                           al_ref[...], ds_ref[...], nw_ref[...], _valid_rows(c, pad))
        y_ref[...] = y.astype(y_ref.dtype)
        st[...] = s1

    row = lambda w, off=0: pl.BlockSpec((None, CHUNK, w), lambda b, c: (b, c, off))
    par = lambda w: pl.BlockSpec((1, w), lambda b, c: (0, 0))
    return pl.pallas_call(
        body, grid=(B, nc),
        in_specs=[row(1024, 0), row(512, 2), row(512, 3), row(128), row(1024), par(128), par(128), par(128), par(1024)],
        out_specs=[row(1024), pl.BlockSpec((None, None, 1024, 128), lambda b, c: (b, c, 0, 0))],
        out_shape=[jax.ShapeDtypeStruct((B, Tp, SSD_INNER), BF16), jax.ShapeDtypeStruct((B, nc, 1024, 128), F32)],
        scratch_shapes=[pltpu.VMEM((1024, 128), F32)],
        name=name, compiler_params=_cparams(("arbitrary", "arbitrary")),
    )(xact, xact, xact, dtr, z, dt_bias, a_log, dskip, norm_w)


def _ssd_bwd(xact, dtr, z, dt_bias, a_log, dskip, norm_w, saved, dy, pad, name):
    B, Tp, _ = xact.shape
    nc = Tp // CHUNK

    def body(xs_ref, bm_ref, cm_ref, dt_ref, z_ref, db_ref, al_ref, ds_ref, nw_ref, sv_ref, dy_ref,
             dx_ref, ddt_ref, dz_ref, dpar_ref, dnw_ref, dst):
        b, i = pl.program_id(0), pl.program_id(1)
        c = nc - 1 - i

        @pl.when(i == 0)
        def _():
            dst[...] = jnp.zeros_like(dst)

        valid = _valid_rows(c, pad)
        fn = lambda *a: _ssd_chunk(*a, valid)
        _, vjp = jax.vjp(fn, xs_ref[...], bm_ref[...], cm_ref[...], dt_ref[...], z_ref[...], sv_ref[...],
                         db_ref[...], al_ref[...], ds_ref[...], nw_ref[...])
        dxs, dbm, dcm, ddt, dz, dstate, ddb, dal, dds, dnw = vjp((dy_ref[...].astype(F32), dst[...]))
        dx_ref[:, 0:1024] = dxs
        dx_ref[:, 1024:1536] = dbm
        dx_ref[:, 1536:2048] = dcm
        ddt_ref[...] = ddt
        dz_ref[...] = dz
        dst[...] = dstate

        @pl.when((b == 0) & (i == 0))
        def _():
            dpar_ref[...] = jnp.zeros_like(dpar_ref)
            dnw_ref[...] = jnp.zeros_like(dnw_ref)

        dpar_ref[0:1, :] += ddb
        dpar_ref[1:2, :] += dal
        dpar_ref[2:3, :] += dds
        dnw_ref[0:1, :] += dnw

    row = lambda w, off=0: pl.BlockSpec((None, CHUNK, w), lambda b, i: (b, nc - 1 - i, off))
    par = lambda w: pl.BlockSpec((1, w), lambda b, i: (0, 0))
    acc = lambda w: pl.BlockSpec((8, w), lambda b, i: (0, 0))
    outs = pl.pallas_call(
        body, grid=(B, nc),
        in_specs=[row(1024, 0), row(512, 2), row(512, 3), row(128), row(1024), par(128), par(128), par(128), par(1024),
                  pl.BlockSpec((None, None, 1024, 128), lambda b, i: (b, nc - 1 - i, 0, 0)), row(1024)],
        out_specs=[row(2048), row(128), row(1024), acc(128), acc(1024)],
        out_shape=[jax.ShapeDtypeStruct((B, Tp, 2048), F32), jax.ShapeDtypeStruct((B, Tp, 128), F32),
                   jax.ShapeDtypeStruct((B, Tp, 1024), F32), jax.ShapeDtypeStruct((8, 128), F32),
                   jax.ShapeDtypeStruct((8, 1024), F32)],
        scratch_shapes=[pltpu.VMEM((1024, 128), F32)],
        name=name, compiler_params=_cparams(("arbitrary", "arbitrary")),
    )(xact, xact, xact, dtr, z, dt_bias, a_log, dskip, norm_w, saved, dy)
    return outs


def _hg_chunk(qr, fr, ir, gr, state_t, p0, p1, norm_w, valid):
    Q = qr.shape[0]
    lb = jax.nn.sigmoid(p0 - p1)
    f = lb + (1.0 - lb) * jax.nn.sigmoid(fr)
    k = 1.0 - f
    q = _silu(qr)
    v = ir * valid
    cum = jnp.dot(_tril(Q).astype(F32), jnp.log(f), precision=HIGHEST)
    cum_end = _row_of(cum, Q - 1)
    o_inter = _mm_nt(q * jnp.exp(cum), state_t)
    tril_s = _tril(HG_SUB)
    outs = []
    for i in range(Q // HG_SUB):
        lo, hi = HG_SUB * i, HG_SUB * (i + 1)
        qi, ci, ki, vi = q[lo:hi], cum[lo:hi], k[lo:hi], v[lo:hi]
        mid = _row_of(cum, lo + HG_SUB // 2 - 1)
        att = jnp.where(tril_s, _mm_nt(qi * jnp.exp(ci - mid), ki * jnp.exp(mid - ci)), 0.0)
        oi = _mm(att, vi)
        if i > 0:
            start = _row_of(cum, lo - 1)
            att_prev = _mm_nt(qi * jnp.exp(ci - start), k[:lo] * jnp.exp(start - cum[:lo]))
            oi = oi + _mm(att_prev, v[:lo])
        outs.append(oi)
    o = o_inter + jnp.concatenate(outs, axis=0)
    new_state_t = state_t * jnp.exp(cum_end) + _mm_tn(v, k * jnp.exp(cum_end - cum))
    o = o * lax.rsqrt(jnp.mean(o * o, axis=-1, keepdims=True) + EPS) * norm_w
    return o * _silu(gr), new_state_t


HG_PER_STEP = 2
HG_COLS = 4 * 128


def _hg_fwd(qfig, lbh, nwh, pad, name):
    B, Tp, _ = qfig.shape
    nc = Tp // CHUNK
    hp = HG_PER_STEP

    def body(x_ref, lb_ref, nw_ref, y_ref, save_ref, st):
        c = pl.program_id(1)

        @pl.when(c == 0)
        def _():
            st[...] = jnp.zeros_like(st)

        valid = _valid_rows(c, pad)
        for j in range(hp):
            for b in range(B):
                s0 = st[j, b]
                save_ref[j, b] = s0
                col = lambda k: x_ref[b, :, HG_COLS * j + 128 * k:HG_COLS * j + 128 * (k + 1)]
                y, s1 = _hg_chunk(col(0), col(1), col(2), col(3), s0, lb_ref[j, 0:1, :], lb_ref[j, 1:2, :], nw_ref[j], valid)
                y_ref[b, :, 128 * j:128 * (j + 1)] = y.astype(y_ref.dtype)
                st[j, b] = s1

    return pl.pallas_call(
        body, grid=(HG_HEADS // hp, nc),
        in_specs=[pl.BlockSpec((B, CHUNK, HG_COLS * hp), lambda h, c: (0, c, h)),
                  pl.BlockSpec((hp, 2, 128), lambda h, c: (h, 0, 0)),
                  pl.BlockSpec((hp, 1, 128), lambda h, c: (h, 0, 0))],
        out_specs=[pl.BlockSpec((B, CHUNK, 128 * hp), lambda h, c: (0, c, h)),
                   pl.BlockSpec((hp, B, None, 128, 128), lambda h, c: (h, 0, c, 0, 0))],
        out_shape=[jax.ShapeDtypeStruct((B, Tp, 1024), BF16), jax.ShapeDtypeStruct((HG_HEADS, B, nc, 128, 128), F32)],
        scratch_shapes=[pltpu.VMEM((hp, B, 128, 128), F32)],
        name=name, compiler_params=_cparams(("arbitrary", "arbitrary")),
    )(qfig, lbh, nwh)


def _hg_bwd(qfig, lbh, nwh, saved, dy, pad, name):
    B, Tp, _ = qfig.shape
    nc = Tp // CHUNK
    hp = HG_PER_STEP

    def body(x_ref, lb_ref, nw_ref, sv_ref, dy_ref, dx_ref, dlb_ref, dnw_ref, dst):
        i = pl.program_id(1)
        c = nc - 1 - i

        @pl.when(i == 0)
        def _():
            dst[...] = jnp.zeros_like(dst)
            dlb_ref[...] = jnp.zeros_like(dlb_ref)
            dnw_ref[...] = jnp.zeros_like(dnw_ref)

        valid = _valid_rows(c, pad)
        fn = lambda *a: _hg_chunk(*a, valid)
        for j in range(hp):
            for b in range(B):
                col = lambda k: x_ref[b, :, HG_COLS * j + 128 * k:HG_COLS * j + 128 * (k + 1)]
                _, vjp = jax.vjp(fn, col(0), col(1), col(2), col(3), sv_ref[j, b], lb_ref[j, 0:1, :], lb_ref[j, 1:2, :], nw_ref[j])
                d4 = vjp((dy_ref[b, :, 128 * j:128 * (j + 1)].astype(F32), dst[j, b]))
                for k in range(4):
                    dx_ref[b, :, HG_COLS * j + 128 * k:HG_COLS * j + 128 * (k + 1)] = d4[k].astype(dx_ref.dtype)
                dst[j, b] = d4[4]
                dlb_ref[j, 0:1, :] += d4[5]
                dlb_ref[j, 1:2, :] += d4[6]
                dnw_ref[j, 0:1, :] += d4[7]

    acc = pl.BlockSpec((hp, 8, 128), lambda h, i: (h, 0, 0))
    return pl.pallas_call(
        body, grid=(HG_HEADS // hp, nc),
        in_specs=[pl.BlockSpec((B, CHUNK, HG_COLS * hp), lambda h, i: (0, nc - 1 - i, h)),
                  pl.BlockSpec((hp, 2, 128), lambda h, i: (h, 0, 0)),
                  pl.BlockSpec((hp, 1, 128), lambda h, i: (h, 0, 0)),
                  pl.BlockSpec((hp, B, None, 128, 128), lambda h, i: (h, 0, nc - 1 - i, 0, 0)),
                  pl.BlockSpec((B, CHUNK, 128 * hp), lambda h, i: (0, nc - 1 - i, h))],
        out_specs=[pl.BlockSpec((B, CHUNK, HG_COLS * hp), lambda h, i: (0, nc - 1 - i, h)), acc, acc],
        out_shape=[jax.ShapeDtypeStruct((B, Tp, 4096), BF16), jax.ShapeDtypeStruct((HG_HEADS, 8, 128), F32),
                   jax.ShapeDtypeStruct((HG_HEADS, 8, 128), F32)],
        scratch_shapes=[pltpu.VMEM((hp, B, 128, 128), F32)],
        name=name, compiler_params=_cparams(("arbitrary", "arbitrary")),
    )(qfig, lbh, nwh, saved, dy)


def _adamw(w, g, m, v, name):
    R, C = w.shape
    tr = _pick(R, (256, 176, 128, 64, 8)) if R > 256 else R

    def body(w_ref, g_ref, m_ref, v_ref, d_ref, mo_ref, vo_ref):
        g_ = g_ref[...]
        m_ = ADAM_B1 * m_ref[...] + (1.0 - ADAM_B1) * g_
        v_ = ADAM_B2 * v_ref[...] + (1.0 - ADAM_B2) * (g_ * g_)
        m_hat = m_ / (1.0 - ADAM_B1 ** ADAM_STEP)
        v_hat = v_ / (1.0 - ADAM_B2 ** ADAM_STEP)
        d_ref[...] = -ADAM_LR * (m_hat / (jnp.sqrt(v_hat) + ADAM_EPS) + ADAM_WD * w_ref[...])
        mo_ref[...] = m_
        vo_ref[...] = v_

    sp = pl.BlockSpec((tr, C), lambda i: (i, 0))
    sh = jax.ShapeDtypeStruct((R, C), F32)
    return pl.pallas_call(body, grid=(R // tr,), in_specs=[sp] * 4, out_specs=[sp] * 3, out_shape=[sh] * 3,
                          name=name, compiler_params=_cparams(("arbitrary",)))(w, g, m, v)


def _ffn_fwd(h, norm_w, w_gu, w_down, tag):
    n = _rms_fwd(h, norm_w, f"{tag}_norm")
    gu = _matmul(n, w_gu, mode="nn", out_dtype=BF16, name=f"{tag}_gu")
    a = _swiglu_fwd(gu, f"{tag}_act")
    out = _matmul(a, w_down, mode="nn", out_dtype=F32, alpha=0.5, res=h, name=f"{tag}_down")
    return out, (n, gu, a)


def _ffn_bwd(h, norm_w, w_gu, w_down, saved, dout, tag):
    n, gu, a = saved
    da = _matmul(dout, w_down, mode="nt", out_dtype=BF16, alpha=0.5, name=f"{tag}_d_act")
    dw_down = _matmul(a, dout, mode="tn", out_dtype=F32, alpha=0.5, name=f"{tag}_dw_down")
    dgu = _swiglu_bwd(gu, da, f"{tag}_d_gu")
    dn = _matmul(dgu, w_gu, mode="nt", out_dtype=F32, name=f"{tag}_d_norm")
    dw_gu = _matmul(n, dgu, mode="tn", out_dtype=F32, out_groups=N_CHIPS, name=f"{tag}_dw_gu")
    dh, dnw = _rms_bwd(h, norm_w, dn, dout, f"{tag}_d_in")
    return dh, dnw, dw_gu, dw_down


IN_NAMES = ("z", "xbc", "dt", "q", "f", "i", "g", "gates")


def _split_w_in(w_in_full):
    pts = [0]
    for s in IN_SIZES:
        pts.append(pts[-1] + s)
    sl = lambda i, j: w_in_full[:, pts[i]:pts[j]]
    qfig = sl(3, 7).reshape(D_MODEL, 4, HG_HEADS, 128).transpose(0, 2, 1, 3).reshape(D_MODEL, 4 * D_MODEL)
    return {"z": sl(0, 1), "xbc": sl(1, 2), "dt": jnp.pad(sl(2, 3), ((0, 0), (0, 128 - SSD_HEADS))),
            "qfig": qfig, "gates": sl(7, 9)}


def _local_step(x, target, W):
    B, S, _ = x.shape
    T = N_META + S
    pad = (-T) % CHUNK
    Tp = T + pad
    assert pad + N_META == CHUNK
    R = B * Tp
    meta = jnp.broadcast_to(W["meta_tokens"][None], (B, N_META, D_MODEL))
    h0 = jnp.concatenate([jnp.zeros((B, pad, D_MODEL), F32), meta, x], axis=1).reshape(R, D_MODEL)

    h1, sv1 = _ffn_fwd(h0, W["ffn1_norm"], W["ffn1_w_gu"], W["ffn1_w_down"], "ffn1")
    um = _rms_fwd(h1, W["mix_norm"], "mix_norm")
    wi = W["w_in"]
    z = _matmul(um, wi["z"], mode="nn", out_dtype=F32, name="in_z")
    xbc = _matmul(um, wi["xbc"], mode="nn", out_dtype=F32, name="in_xbc")
    dtr = _matmul(um, wi["dt"], mode="nn", out_dtype=F32, name="in_dt")
    qfig = _matmul(um, wi["qfig"], mode="nn", out_dtype=F32, name="in_qfig")
    gates = _matmul(um, wi["gates"], mode="nn", out_dtype=F32, name="in_gates")

    r3 = lambda t: t.reshape(B, Tp, t.shape[-1])
    lane_pad = lambda t: jnp.pad(t, ((0, 0), (0, 128 - t.shape[1])))
    dt_bias, a_log, dskip = lane_pad(W["ssd_dt_bias"]), lane_pad(W["ssd_a_log"]), lane_pad(W["ssd_d"])
    xact = _conv_fwd(r3(xbc), W["ssd_conv_w"], W["ssd_conv_b"], pad, "conv_fwd")
    ya, ssd_saved = _ssd_fwd(xact, r3(dtr), r3(z), dt_bias, a_log, dskip, W["ssd_norm"], pad, "ssd_fwd")
    lbh = W["hg_lower_bound"].reshape(2, HG_HEADS, 128).transpose(1, 0, 2)
    nwh = W["hg_norm"].reshape(HG_HEADS, 1, 128)
    yb, hg_saved = _hg_fwd(r3(qfig), lbh, nwh, pad, "hg_fwd")
    ya2, yb2 = ya.reshape(R, -1), yb.reshape(R, -1)
    pa = _matmul(ya2, W["w_branch_a"], mode="nn", out_dtype=F32, name="branch_a")
    pb = _matmul(yb2, W["w_branch_b"], mode="nn", out_dtype=F32, name="branch_b")
    mg = _merge_fwd(pa, pb, gates, "merge")
    h2 = _matmul(mg, W["w_out"], mode="nn", out_dtype=F32, res=h1, name="mix_out")
    h3, sv2 = _ffn_fwd(h2, W["ffn2_norm"], W["ffn2_w_gu"], W["ffn2_w_down"], "ffn2")

    loss, dh3, d_final = _loss_head(h3, W["final_norm"].reshape(1, D_MODEL), target, B, "loss_head")

    G = {"final_norm": d_final[0]}
    dh2, dnw, G["ffn2_w_gu"], G["ffn2_w_down"] = _ffn_bwd(h2, W["ffn2_norm"], W["ffn2_w_gu"], W["ffn2_w_down"], sv2, dh3, "ffn2")
    G["ffn2_norm"] = dnw[0:1]
    dmg = _matmul(dh2, W["w_out"], mode="nt", out_dtype=BF16, name="d_merge")
    G["w_out"] = _matmul(mg, dh2, mode="tn", out_dtype=F32, name="dw_out")
    dpa, dpb, dgates = _merge_bwd(pa, pb, gates, dmg, "merge_bwd")
    dya = _matmul(dpa, W["w_branch_a"], mode="nt", out_dtype=BF16, name="d_ya")
    dyb = _matmul(dpb, W["w_branch_b"], mode="nt", out_dtype=BF16, name="d_yb")
    G["w_branch_a"] = _matmul(ya2, dpa, mode="tn", out_dtype=F32, name="dw_branch_a")
    G["w_branch_b"] = _matmul(yb2, dpb, mode="tn", out_dtype=F32, name="dw_branch_b")

    dxact, ddtr, dz, dpar, dnw = _ssd_bwd(xact, r3(dtr), r3(z), dt_bias, a_log, dskip, W["ssd_norm"], ssd_saved,
                                          r3(dya), pad, "ssd_bwd")
    G["ssd_dt_bias"], G["ssd_a_log"], G["ssd_d"] = dpar[0:1, :SSD_HEADS], dpar[1:2, :SSD_HEADS], dpar[2:3, :SSD_HEADS]
    G["ssd_norm"] = dnw[0:1]
    dxbc, dcw, dcb = _conv_bwd(r3(xbc), W["ssd_conv_w"], W["ssd_conv_b"], dxact, pad, "conv_bwd")
    G["ssd_conv_w"], G["ssd_conv_b"] = dcw[0:SSD_CONV], dcb[0:1]
    dqfig, dlb, dhn = _hg_bwd(r3(qfig), lbh, nwh, hg_saved, r3(dyb), pad, "hg_bwd")
    G["hg_lower_bound"] = dlb[:, 0:2, :].transpose(1, 0, 2).reshape(2, D_MODEL)
    G["hg_norm"] = dhn[:, 0, :].reshape(1, D_MODEL)

    r2 = lambda t: t.reshape(R, t.shape[-1])
    pieces = [("z", r2(dz)), ("xbc", r2(dxbc)), ("dt", r2(ddtr)), ("qfig", r2(dqfig)), ("gates", dgates)]
    dum = None
    dwi = {}
    for nm, dpiece in pieces:
        dum = _matmul(dpiece, wi[nm], mode="nt", out_dtype=F32, res=dum, name=f"d_mix_{nm}")
        dwi[nm] = _matmul(um, dpiece, mode="tn", out_dtype=F32, name=f"dw_in_{nm}")
    dw_qfig = dwi["qfig"].reshape(D_MODEL, HG_HEADS, 4, 128).transpose(0, 2, 1, 3).reshape(D_MODEL, 4 * D_MODEL)
    G["w_in"] = jnp.concatenate([dwi["z"], dwi["xbc"], dwi["dt"][:, :SSD_HEADS], dw_qfig, dwi["gates"]], axis=1)
    dh1, dnw = _rms_bwd(h1, W["mix_norm"], dum, dh2, "mix_norm_bwd")
    G["mix_norm"] = dnw[0:1]
    dh0, dnw, G["ffn1_w_gu"], G["ffn1_w_down"] = _ffn_bwd(h0, W["ffn1_norm"], W["ffn1_w_gu"], W["ffn1_w_down"], sv1, dh1, "ffn1")
    G["ffn1_norm"] = dnw[0:1]
    dh0 = dh0.reshape(B, Tp, D_MODEL)
    G["meta_tokens"] = jnp.sum(dh0[:, pad:CHUNK], axis=0)
    return loss, dh0[:, CHUNK:], G


ANY = pl.BlockSpec(memory_space=pl.ANY)


def _place():
    return lax.axis_index("x"), lax.axis_index("y"), lax.axis_index("c")


def _other_chips(x, y):
    return [(1 - x, y), (x, 1 - y), (1 - x, 1 - y)]


def _remote(src, dst, ssem, rsem, dev):
    return pltpu.make_async_remote_copy(src_ref=src, dst_ref=dst, send_sem=ssem, recv_sem=rsem,
                                        device_id=dev, device_id_type=MESH)


def _exchange8(buf, reduce, name):
    n, w = buf.shape

    def body(x_ref, *rest):
        if reduce:
            red_ref, out_ref, ssem, rsem = rest
        else:
            out_ref, ssem, rsem = rest
        x, y, c = _place()
        me = 4 * x + 2 * y + c
        out_ref[me] = x_ref[...]
        copies = []
        for k in range(1, 8):
            px = 1 - x if (k >> 2) & 1 else x
            py = 1 - y if (k >> 1) & 1 else y
            pc = 1 - c if k & 1 else c
            cp = _remote(x_ref, out_ref.at[me], ssem.at[k - 1], rsem.at[k - 1], (px, py, pc))
            cp.start()
            copies.append((cp, 4 * px + 2 * py + pc))
        for k, (cp, peer) in enumerate(copies):
            _remote(x_ref, out_ref.at[peer], ssem.at[k], rsem.at[k], (x, y, c)).wait_recv()
        for cp, _ in copies:
            cp.wait_send()
        if reduce:
            acc = out_ref[0]
            for d in range(1, 8):
                acc = acc + out_ref[d]
            red_ref[...] = acc

    vm = pl.BlockSpec(memory_space=pltpu.VMEM)
    g_shape = jax.ShapeDtypeStruct((8, n, w), F32)
    if reduce:
        out_shape, out_specs, scratch = [jax.ShapeDtypeStruct((n, w), F32)], [vm], [pltpu.VMEM((8, n, w), F32)]
    else:
        out_shape, out_specs, scratch = [g_shape], [vm], []
    return pl.pallas_call(
        body, in_specs=[vm], out_specs=out_specs, out_shape=out_shape,
        scratch_shapes=scratch + [pltpu.SemaphoreType.DMA((7,)), pltpu.SemaphoreType.DMA((7,))], name=name,
    )(buf)[0]


def _gather_big(blocks, name):
    n = len(blocks)
    half = [s.shape[1] // 2 for s in blocks]

    def body(*refs):
        full = refs[n:2 * n]
        ssem, rsem, fssem, frsem = refs[2 * n:]
        x, y, c = _place()
        q = 2 * x + y
        chips = _other_chips(x, y)
        piece = lambda s, qq, cc: full[s].at[qq, pl.ds(cc * half[s], half[s])]
        sends = []
        for j, (px, py) in enumerate(chips):
            for s in range(n):
                cp = _remote(piece(s, q, c), piece(s, q, c), ssem.at[s, j], rsem.at[s, j], (px, py, c))
                cp.start()
                sends.append(cp)
        for j, (px, py) in enumerate(chips):
            for s in range(n):
                got = piece(s, 2 * px + py, c)
                _remote(got, got, ssem.at[s, j], rsem.at[s, j], (px, py, c)).wait_recv()
                cp = _remote(got, got, fssem.at[s, j], frsem.at[s, j], (x, y, 1 - c))
                cp.start()
                sends.append(cp)
        for j, (px, py) in enumerate(chips):
            for s in range(n):
                got = piece(s, 2 * px + py, 1 - c)
                _remote(got, got, fssem.at[s, j], frsem.at[s, j], (x, y, 1 - c)).wait_recv()
        for cp in sends:
            cp.wait_send()

    return pl.pallas_call(
        body, in_specs=[ANY] * n, out_specs=[ANY] * n,
        out_shape=[jax.ShapeDtypeStruct(s.shape, s.dtype) for s in blocks],
        input_output_aliases={s: s for s in range(n)},
        scratch_shapes=[pltpu.SemaphoreType.DMA((n, 3))] * 4, name=name,
    )(*blocks)


def _pair_swap(parts, name):
    n = len(parts)
    half = [p.shape[1] // 2 for p in parts]

    def body(*refs):
        src, got = refs[:n], refs[n:2 * n]
        ssem, rsem = refs[2 * n:]
        x, y, c = _place()
        copies = []
        for s in range(n):
            cp = _remote(src[s].at[pl.ds(0, N_CHIPS), pl.ds((1 - c) * half[s], half[s])], got[s], ssem.at[s], rsem.at[s], (x, y, 1 - c))
            cp.start()
            copies.append(cp)
        for cp in copies:
            cp.wait_recv()
        for cp in copies:
            cp.wait_send()

    return pl.pallas_call(
        body, in_specs=[ANY] * n, out_specs=[ANY] * n,
        out_shape=[jax.ShapeDtypeStruct((N_CHIPS, h, p.shape[2]), p.dtype) for p, h in zip(parts, half)],
        scratch_shapes=[pltpu.SemaphoreType.DMA((n,)), pltpu.SemaphoreType.DMA((n,))], name=name,
    )(*parts)


def _to_owners(sums, name):
    n = len(sums)

    def body(*refs):
        src, got = refs[:n], refs[n:2 * n]
        lsem, ssem, rsem = refs[2 * n:]
        x, y, c = _place()
        q = 2 * x + y
        chips = _other_chips(x, y)
        started, sends = [], []
        for s in range(n):
            cp = pltpu.make_async_copy(src[s].at[q], got[s].at[q], lsem.at[s])
            cp.start()
            started.append(cp)
        for j, (px, py) in enumerate(chips):
            for s in range(n):
                cp = _remote(src[s].at[2 * px + py], got[s].at[q], ssem.at[s, j], rsem.at[s, j], (px, py, c))
                cp.start()
                sends.append(cp)
        for j, (px, py) in enumerate(chips):
            for s in range(n):
                slot = got[s].at[2 * px + py]
                _remote(slot, slot, ssem.at[s, j], rsem.at[s, j], (px, py, c)).wait_recv()
        for cp in sends:
            cp.wait_send()
        for cp in started:
            cp.wait()

    return pl.pallas_call(
        body, in_specs=[ANY] * n, out_specs=[ANY] * n,
        out_shape=[jax.ShapeDtypeStruct(s.shape, s.dtype) for s in sums],
        scratch_shapes=[pltpu.SemaphoreType.DMA((n,)), pltpu.SemaphoreType.DMA((n, 3)), pltpu.SemaphoreType.DMA((n, 3))],
        name=name,
    )(*sums)


def _pair_join(blocks, name):
    n = len(blocks)

    def body(*refs):
        out = refs[n:2 * n]
        ssem, rsem = refs[2 * n:]
        x, y, c = _place()
        sends = []
        for s in range(n):
            h = blocks[s].shape[0] // 2
            mine = out[s].at[pl.ds(c * h, h)]
            cp = _remote(mine, mine, ssem.at[s], rsem.at[s], (x, y, 1 - c))
            cp.start()
            sends.append(cp)
        for s in range(n):
            h = blocks[s].shape[0] // 2
            theirs = out[s].at[pl.ds((1 - c) * h, h)]
            _remote(theirs, theirs, ssem.at[s], rsem.at[s], (x, y, 1 - c)).wait_recv()
        for cp in sends:
            cp.wait_send()

    return pl.pallas_call(
        body, in_specs=[ANY] * n, out_specs=[ANY] * n,
        out_shape=[jax.ShapeDtypeStruct(b.shape, b.dtype) for b in blocks],
        input_output_aliases={s: s for s in range(n)},
        scratch_shapes=[pltpu.SemaphoreType.DMA((n,))] * 2, name=name,
    )(*blocks)


WIRE = BF16


def _row_tile(h):
    return _pick(h, (256, 272, 128, 16))


def _add_pair(part, got, c, name):
    _, h, w = got.shape
    tr = _row_tile(h)
    nt = h // tr

    def body(c_ref, p_ref, g_ref, o_ref):
        o_ref[...] = (p_ref[...] + g_ref[...].astype(F32)).astype(o_ref.dtype)

    return pl.pallas_call(
        body,
        grid_spec=pltpu.PrefetchScalarGridSpec(
            num_scalar_prefetch=1, grid=(N_CHIPS, nt),
            in_specs=[pl.BlockSpec((None, tr, w), lambda q, i, c_ref: (q, c_ref[0] * nt + i, 0)),
                      pl.BlockSpec((None, tr, w), lambda q, i, c_ref: (q, i, 0))],
            out_specs=pl.BlockSpec((None, tr, w), lambda q, i, c_ref: (q, i, 0))),
        out_shape=jax.ShapeDtypeStruct(got.shape, WIRE), name=name,
        compiler_params=_cparams(("arbitrary", "arbitrary")),
    )(c.reshape(1).astype(jnp.int32), part, got)


def _sum_chips(slots, c, name):
    _, h, w = slots.shape
    tr = _row_tile(h)
    nt = h // tr

    def body(c_ref, s_ref, o_ref):
        o_ref[...] = ((s_ref[0].astype(F32) + s_ref[1].astype(F32)) + s_ref[2].astype(F32)) + s_ref[3].astype(F32)

    return pl.pallas_call(
        body,
        grid_spec=pltpu.PrefetchScalarGridSpec(
            num_scalar_prefetch=1, grid=(nt,),
            in_specs=[pl.BlockSpec((N_CHIPS, tr, w), lambda i, c_ref: (0, i, 0))],
            out_specs=pl.BlockSpec((tr, w), lambda i, c_ref: (c_ref[0] * nt + i, 0))),
        out_shape=jax.ShapeDtypeStruct((2 * h, w), F32), name=name,
        compiler_params=_cparams(("arbitrary",)),
    )(c.reshape(1).astype(jnp.int32), slots)


def _reduce_to_owners(parts, c):
    got = _pair_swap(parts, "grad_pair_swap")
    sums = [_add_pair(p, g, c, f"grad_pair_add{i}") for i, (p, g) in enumerate(zip(parts, got))]
    slots = _to_owners(sums, "grad_to_owners")
    blocks = [_sum_chips(s, c, f"grad_sum_chips{i}") for i, s in enumerate(slots)]
    return _pair_join(blocks, "grad_pair_join")


WEIGHTS = ("meta_tokens", "ffn1_norm", "ffn1_w_gu", "ffn1_w_down", "mix_norm", "w_in", "ssd_conv_w", "ssd_conv_b",
           "ssd_dt_bias", "ssd_a_log", "ssd_d", "ssd_norm", "hg_lower_bound", "hg_norm", "w_branch_a", "w_branch_b",
           "w_out", "ffn2_norm", "ffn2_w_gu", "ffn2_w_down", "final_norm")
BIG = ("ffn1_w_gu", "ffn1_w_down", "w_in", "w_branch_a", "w_branch_b", "w_out", "ffn2_w_gu", "ffn2_w_down")
ROW_SHARDED = ("ffn1_w_down", "ffn2_w_down", "w_branch_a", "w_branch_b", "w_out")
SMALL = tuple(n for n in WEIGHTS if n not in BIG)
SMALL_ROWS = 24


def _rows1024(a):
    flat = a.reshape(-1)
    n = -(-flat.shape[0] // 1024) * 1024
    return jnp.pad(flat, (0, n - flat.shape[0])).reshape(-1, 1024)


def _pack_small(d):
    rows = jnp.concatenate([_rows1024(d[n]) for n in SMALL], axis=0)
    return jnp.pad(rows, ((0, SMALL_ROWS - rows.shape[0]), (0, 0)))


def _unpack_small(packed, like):
    out, r = {}, 0
    for n in SMALL:
        size = like[n].size
        nr = -(-size // 1024)
        out[n] = packed[r:r + nr].reshape(-1)[:size].reshape(like[n].shape)
        r += nr
    return out


def kernel(x, meta_tokens, ffn1_norm, ffn1_w_gu, ffn1_w_down, mix_norm, w_in, ssd_conv_w, ssd_conv_b, ssd_dt_bias, ssd_a_log, ssd_d, ssd_norm, hg_lower_bound, hg_norm, w_branch_a, w_branch_b, w_out, ffn2_norm, ffn2_w_gu, ffn2_w_down, final_norm, loss_target, m_meta_tokens, m_ffn1_norm, m_ffn1_w_gu, m_ffn1_w_down, m_mix_norm, m_w_in, m_ssd_conv_w, m_ssd_conv_b, m_ssd_dt_bias, m_ssd_a_log, m_ssd_d, m_ssd_norm, m_hg_lower_bound, m_hg_norm, m_w_branch_a, m_w_branch_b, m_w_out, m_ffn2_norm, m_ffn2_w_gu, m_ffn2_w_down, m_final_norm, v_meta_tokens, v_ffn1_norm, v_ffn1_w_gu, v_ffn1_w_down, v_mix_norm, v_w_in, v_ssd_conv_w, v_ssd_conv_b, v_ssd_dt_bias, v_ssd_a_log, v_ssd_d, v_ssd_norm, v_hg_lower_bound, v_hg_norm, v_w_branch_a, v_w_branch_b, v_w_out, v_ffn2_norm, v_ffn2_w_gu, v_ffn2_w_down, v_final_norm):
    P = dict(zip(WEIGHTS, (meta_tokens, ffn1_norm, ffn1_w_gu, ffn1_w_down, mix_norm, w_in, ssd_conv_w, ssd_conv_b, ssd_dt_bias, ssd_a_log, ssd_d, ssd_norm, hg_lower_bound, hg_norm, w_branch_a, w_branch_b, w_out, ffn2_norm, ffn2_w_gu, ffn2_w_down, final_norm)))
    M = dict(zip(WEIGHTS, (m_meta_tokens, m_ffn1_norm, m_ffn1_w_gu, m_ffn1_w_down, m_mix_norm, m_w_in, m_ssd_conv_w, m_ssd_conv_b, m_ssd_dt_bias, m_ssd_a_log, m_ssd_d, m_ssd_norm, m_hg_lower_bound, m_hg_norm, m_w_branch_a, m_w_branch_b, m_w_out, m_ffn2_norm, m_ffn2_w_gu, m_ffn2_w_down, m_final_norm)))
    V = dict(zip(WEIGHTS, (v_meta_tokens, v_ffn1_norm, v_ffn1_w_gu, v_ffn1_w_down, v_mix_norm, v_w_in, v_ssd_conv_w, v_ssd_conv_b, v_ssd_dt_bias, v_ssd_a_log, v_ssd_d, v_ssd_norm, v_hg_lower_bound, v_hg_norm, v_w_branch_a, v_w_branch_b, v_w_out, v_ffn2_norm, v_ffn2_w_gu, v_ffn2_w_down, v_final_norm)))
    cx, cy, cc = _place()
    q = 2 * cx + cy

    mine = jnp.concatenate([meta_tokens.reshape(4, 1024), ssd_conv_w.reshape(2, 1024), jnp.zeros((2, 1024), F32)], axis=0)
    every = _exchange8(mine, False, "gather_small")
    meta_full = jnp.concatenate([every[2 * k, 0:4].reshape(N_META, 256) for k in range(N_CHIPS)], axis=1)
    conv_w_full = jnp.concatenate([every[2 * k, 4:6].reshape(SSD_CONV, 512) for k in range(N_CHIPS)], axis=1)

    rows = jnp.concatenate([P[n][0] for n in ROW_SHARDED], axis=0).astype(BF16)
    in_slot = lambda s: lax.dynamic_update_slice(lax.empty((N_CHIPS,) + s.shape, BF16), s.astype(BF16)[None], (q, 0, 0))
    gu1, gu2, w_in_all, rows_all = _gather_big(
        [in_slot(ffn1_w_gu[0]), in_slot(ffn2_w_gu[0]), in_slot(w_in[0]), in_slot(rows)], "gather_weights")
    W = {n: P[n] for n in SMALL}
    W["meta_tokens"], W["ssd_conv_w"] = meta_full, conv_w_full
    W["ffn1_w_gu"], W["ffn2_w_gu"] = gu1, gu2
    W["w_in"] = _split_w_in(w_in_all.transpose(1, 0, 2).reshape(D_MODEL, -1))
    r = 0
    for n in ROW_SHARDED:
        nr = P[n].shape[1]
        W[n] = rows_all[:, r:r + nr].reshape(N_CHIPS * nr, D_MODEL)
        r += nr

    loss8, grad_x, G = _local_step(x, loss_target, W)

    small = jnp.concatenate(
        [G["meta_tokens"]] + [_rows1024(G[n]) for n in SMALL if n != "meta_tokens"] + [_rows1024(loss8[0:1, 0:1])], axis=0)
    small = jnp.pad(small, ((0, 40 - small.shape[0]), (0, 0)))
    small = _exchange8(small, True, "reduce_small")
    Gs = {"meta_tokens": small[0:N_META]}
    r = N_META
    for n in SMALL:
        if n == "meta_tokens":
            continue
        nr = -(-G[n].size // 1024)
        Gs[n] = small[r:r + nr].reshape(-1)[:G[n].size].reshape(G[n].shape)
        r += nr
    loss = small[r, 0]
    Gs["meta_tokens"] = lax.dynamic_slice(Gs["meta_tokens"], (0, 256 * q), (N_META, 256))
    Gs["ssd_conv_w"] = lax.dynamic_slice(Gs["ssd_conv_w"], (0, 512 * q), (SSD_CONV, 512))[None]
    Gs = {n: Gs[n].reshape(P[n].shape) for n in SMALL}

    w_in_parts = G["w_in"].reshape(D_MODEL, N_CHIPS, -1).transpose(1, 0, 2)
    row_parts = jnp.concatenate([G[n].reshape(N_CHIPS, -1, D_MODEL) for n in ROW_SHARDED], axis=1)
    g_gu1, g_gu2, g_w_in, g_rows = _reduce_to_owners([G["ffn1_w_gu"], G["ffn2_w_gu"], w_in_parts, row_parts], cc)
    Gb = {"ffn1_w_gu": g_gu1, "ffn2_w_gu": g_gu2, "w_in": g_w_in}
    r = 0
    for n in ROW_SHARDED:
        nr = P[n].shape[1]
        Gb[n] = g_rows[r:r + nr]
        r += nr

    grads, delta, new_m, new_v = dict(Gs), {}, {}, {}
    d_s, m_s, v_s = _adamw(_pack_small(P), _pack_small(Gs), _pack_small(M), _pack_small(V), "adamw_small")
    delta.update(_unpack_small(d_s, P))
    new_m.update(_unpack_small(m_s, P))
    new_v.update(_unpack_small(v_s, P))
    for n in BIG:
        d_, m_, v_ = _adamw(P[n][0], Gb[n], M[n][0], V[n][0], f"adamw_{n}")
        grads[n], delta[n], new_m[n], new_v[n] = Gb[n][None], d_[None], m_[None], v_[None]
    return (loss, grad_x, *[grads[n] for n in WEIGHTS], *[delta[n] for n in WEIGHTS],
            *[new_m[n] for n in WEIGHTS], *[new_v[n] for n in WEIGHTS])
```

```python
import functools

import jax
import jax.numpy as jnp
from jax import lax
from jax.experimental import pallas as pl
from jax.experimental.pallas import tpu as pltpu

F32 = jnp.float32
BF16 = jnp.bfloat16
HIGHEST = lax.Precision.HIGHEST
MESH = pl.DeviceIdType.MESH

D_MODEL = 1024
N_META = 16
EPS = 1e-6
SSD_HEADS = 16
SSD_HEAD_DIM = 64
SSD_INNER = 1024
SSD_GROUPS = 4
SSD_STATE = 128
SSD_CONV = 4
SSD_CONV_CH = 2048
HG_HEADS = 8
HG_SUB = 32
CHUNK = 128
D_FF = 2816
N_CHIPS = 4
IN_SIZES = (1024, 2048, 16, 1024, 1024, 1024, 1024, 1024, 1024)
ADAM_LR = 0.001
ADAM_B1 = 0.9
ADAM_B2 = 0.999
ADAM_EPS = 1e-08
ADAM_WD = 0.01
ADAM_STEP = 10
VMEM_LIMIT = 56 * 1024 * 1024


def _cparams(sem=None):
    return pltpu.CompilerParams(dimension_semantics=sem, vmem_limit_bytes=VMEM_LIMIT)


def _pick(n, cands):
    for c in cands:
        if n % c == 0:
            return c
    return n


def _dg(a, b, ca, cb):
    return lax.dot_general(a.astype(BF16), b.astype(BF16), (((ca,), (cb,)), ((), ())), preferred_element_type=F32)


@jax.custom_vjp
def _mm(a, b):
    return _dg(a, b, 1, 0)


def _mm_fwd(a, b):
    return _dg(a, b, 1, 0), (a, b)


def _mm_bwd(r, g):
    a, b = r
    return _dg(g, b, 1, 1), _dg(a, g, 0, 0)


_mm.defvjp(_mm_fwd, _mm_bwd)


@jax.custom_vjp
def _mm_nt(a, b):
    return _dg(a, b, 1, 1)


def _mm_nt_fwd(a, b):
    return _dg(a, b, 1, 1), (a, b)


def _mm_nt_bwd(r, g):
    a, b = r
    return _dg(g, b, 1, 0), _dg(g, a, 0, 0)


_mm_nt.defvjp(_mm_nt_fwd, _mm_nt_bwd)


@jax.custom_vjp
def _mm_tn(a, b):
    return _dg(a, b, 0, 0)


def _mm_tn_fwd(a, b):
    return _dg(a, b, 0, 0), (a, b)


def _mm_tn_bwd(r, g):
    a, b = r
    return _dg(b, g, 1, 1), _dg(a, g, 1, 0)


_mm_tn.defvjp(_mm_tn_fwd, _mm_tn_bwd)


def _silu(x):
    return x * jax.nn.sigmoid(x)


def _softplus(x):
    return jnp.maximum(x, 0.0) + jnp.log(1.0 + jnp.exp(-jnp.abs(x)))


def _tril(n):
    ri = lax.broadcasted_iota(jnp.int32, (n, n), 0)
    ci = lax.broadcasted_iota(jnp.int32, (n, n), 1)
    return ri >= ci


def _row_of(m, r):
    sub = lax.broadcasted_iota(jnp.int32, (m.shape[0], 1), 0)
    return jnp.sum(jnp.where(sub == r, m, 0.0), axis=0, keepdims=True)


def _col_of(m, c):
    lane = lax.broadcasted_iota(jnp.int32, (1, m.shape[1]), 1)
    return jnp.sum(jnp.where(lane == c, m, 0.0), axis=1, keepdims=True)


def _matmul(a, b, *, mode, out_dtype, name, alpha=1.0, res=None, tm=None, tn=None, tk=None, out_groups=None):
    b3 = b.ndim == 3
    if mode == "nn":
        M, K = a.shape
        G = b.shape[0] if b3 else 1
        Ng = b.shape[-1]
        N = G * Ng
    elif mode == "nt":
        M, K = a.shape
        G = b.shape[0] if b3 else 1
        N = b.shape[-2]
        Kg = b.shape[-1]
        assert G * Kg == K
    else:
        K, M = a.shape
        N = b.shape[1]
        G = out_groups or 1
        Ng = N // G
    if mode == "tn":
        tm = tm or _pick(M, (1408, 1024, 512, 256, 128))
        tk = tk or _pick(K, (544, 256, 128))
        tn = tn or _pick(Ng, (1408, 1024, 512, 256, 128))
    else:
        tm = tm or _pick(M, (1088, 544, 256, 128))
        if mode == "nn":
            tn = tn or _pick(Ng, (1408, 512, 256, 128))
            tk = K
        else:
            tn = tn or _pick(N, (512, 256, 128))
            tk = tk or (Kg if b3 else K)
    nm, nn_, nk = M // tm, N // tn, K // tk
    assert nm * tm == M and nn_ * tn == N and nk * tk == K, (name, M, N, K, tm, tn, tk)

    if mode == "nn":
        a_spec = pl.BlockSpec((tm, tk), lambda i, j, k: (i, k))
        if b3:
            ns = Ng // tn
            b_spec = pl.BlockSpec((None, tk, tn), lambda i, j, k: (j // ns, k, j % ns))
        else:
            b_spec = pl.BlockSpec((tk, tn), lambda i, j, k: (k, j))
        ca, cb = 1, 0
    elif mode == "nt":
        a_spec = pl.BlockSpec((tm, tk), lambda i, j, k: (i, k))
        if b3:
            ks = Kg // tk
            b_spec = pl.BlockSpec((None, tn, tk), lambda i, j, k: (k // ks, j, k % ks))
        else:
            b_spec = pl.BlockSpec((tn, tk), lambda i, j, k: (j, k))
        ca, cb = 1, 1
    else:
        a_spec = pl.BlockSpec((tk, tm), lambda i, j, k: (k, i))
        b_spec = pl.BlockSpec((tk, tn), lambda i, j, k: (k, j))
        ca, cb = 0, 0
    if mode == "tn" and G > 1:
        ns = Ng // tn
        o_spec = pl.BlockSpec((None, tm, tn), lambda i, j, k: (j // ns, i, j % ns))
        out_shape = jax.ShapeDtypeStruct((G, M, Ng), out_dtype)
    else:
        o_spec = pl.BlockSpec((tm, tn), lambda i, j, k: (i, j))
        out_shape = jax.ShapeDtypeStruct((M, N), out_dtype)
    in_specs = [a_spec, b_spec]
    args = [a, b]
    if res is not None:
        in_specs.append(pl.BlockSpec((tm, tn), lambda i, j, k: (i, j)))
        args.append(res)
    has_res = res is not None

    def body(*refs):
        a_ref, b_ref = refs[0], refs[1]
        o_ref, acc_ref = refs[-2], refs[-1]
        k = pl.program_id(2)

        @pl.when(k == 0)
        def _():
            acc_ref[...] = jnp.zeros_like(acc_ref)

        acc_ref[...] += _dg(a_ref[...], b_ref[...], ca, cb)

        @pl.when(k == nk - 1)
        def _():
            o = acc_ref[...]
            if alpha != 1.0:
                o = o * alpha
            if has_res:
                o = o + refs[2][...]
            o_ref[...] = o.astype(o_ref.dtype)

    return pl.pallas_call(
        body, grid=(nm, nn_, nk), in_specs=in_specs, out_specs=o_spec, out_shape=out_shape,
        scratch_shapes=[pltpu.VMEM((tm, tn), F32)], name=name,
        compiler_params=_cparams(("parallel", "parallel", "arbitrary")),
    )(*args)


def _rms_fn(h, w):
    r = lax.rsqrt(jnp.mean(h * h, axis=-1, keepdims=True) + EPS)
    return h * r * w


def _swiglu_fn(gu):
    g = gu[:, :D_FF].astype(F32)
    u = gu[:, D_FF:].astype(F32)
    return _silu(g) * u


def _merge_fn(pa, pb, gates):
    return jax.nn.sigmoid(gates[:, :D_MODEL]) * pa + jax.nn.sigmoid(gates[:, D_MODEL:]) * pb


def _rows_call(body, *, rows, tr, ins, outs, accs=(), name):
    n = rows // tr
    assert n * tr == rows

    def spec(x):
        if isinstance(x, tuple):
            shp = x[1].shape
            return pl.BlockSpec(shp, lambda i: (0,) * len(shp))
        return pl.BlockSpec((tr, x.shape[1]), lambda i: (i, 0))

    in_specs = [spec(x) for x in ins]
    args = [x[1] if isinstance(x, tuple) else x for x in ins]
    out_specs = [spec(x) for x in outs] + [pl.BlockSpec(x.shape, lambda i: (0,) * len(x.shape)) for x in accs]
    out_shape = [x[1] if isinstance(x, tuple) else x for x in outs] + list(accs)
    return pl.pallas_call(
        body, grid=(n,), in_specs=in_specs, out_specs=out_specs, out_shape=out_shape, name=name,
        compiler_params=_cparams(("arbitrary",)),
    )(*args)


def _acc_rows(ref, val):
    @pl.when(pl.program_id(0) == 0)
    def _():
        ref[...] = jnp.zeros_like(ref)

    ref[0:1, :] += val


def _rms_fwd(h, w, name):
    def body(h_ref, w_ref, o_ref):
        o_ref[...] = _rms_fn(h_ref[...], w_ref[...]).astype(o_ref.dtype)

    R = h.shape[0]
    return _rows_call(body, rows=R, tr=_pick(R, (256, 128)), ins=[h, ("full", w)],
                      outs=[jax.ShapeDtypeStruct(h.shape, BF16)], name=name)[0]


def _rms_bwd(h, w, dn, dres, name):
    def body(h_ref, w_ref, dn_ref, dres_ref, dh_ref, dw_ref):
        _, vjp = jax.vjp(_rms_fn, h_ref[...], w_ref[...])
        dh, dw = vjp(dn_ref[...].astype(F32))
        dh_ref[...] = dh + dres_ref[...]
        _acc_rows(dw_ref, dw)

    R = h.shape[0]
    return _rows_call(body, rows=R, tr=_pick(R, (256, 128)), ins=[h, ("full", w), dn, dres],
                      outs=[jax.ShapeDtypeStruct(h.shape, F32)], accs=[jax.ShapeDtypeStruct((8, D_MODEL), F32)], name=name)


def _swiglu_fwd(gu, name):
    def body(gu_ref, o_ref):
        o_ref[...] = _swiglu_fn(gu_ref[...]).astype(o_ref.dtype)

    R = gu.shape[0]
    return _rows_call(body, rows=R, tr=_pick(R, (256, 128)), ins=[gu],
                      outs=[jax.ShapeDtypeStruct((R, D_FF), BF16)], name=name)[0]


def _swiglu_bwd(gu, da, name):
    def body(gu_ref, da_ref, o_ref):
        _, vjp = jax.vjp(_swiglu_fn, gu_ref[...].astype(F32))
        (dgu,) = vjp(da_ref[...].astype(F32))
        o_ref[...] = dgu.astype(o_ref.dtype)

    R = gu.shape[0]
    return _rows_call(body, rows=R, tr=_pick(R, (256, 128)), ins=[gu, da],
                      outs=[jax.ShapeDtypeStruct(gu.shape, BF16)], name=name)[0]


def _merge_fwd(pa, pb, gates, name):
    def body(pa_ref, pb_ref, g_ref, o_ref):
        o_ref[...] = _merge_fn(pa_ref[...], pb_ref[...], g_ref[...]).astype(o_ref.dtype)

    R = pa.shape[0]
    return _rows_call(body, rows=R, tr=_pick(R, (256, 128)), ins=[pa, pb, gates],
                      outs=[jax.ShapeDtypeStruct(pa.shape, BF16)], name=name)[0]


def _merge_bwd(pa, pb, gates, dm, name):
    def body(pa_ref, pb_ref, g_ref, dm_ref, dpa_ref, dpb_ref, dg_ref):
        _, vjp = jax.vjp(_merge_fn, pa_ref[...], pb_ref[...], g_ref[...])
        dpa, dpb, dg = vjp(dm_ref[...].astype(F32))
        dpa_ref[...] = dpa.astype(dpa_ref.dtype)
        dpb_ref[...] = dpb.astype(dpb_ref.dtype)
        dg_ref[...] = dg.astype(dg_ref.dtype)

    R = pa.shape[0]
    return _rows_call(body, rows=R, tr=_pick(R, (256, 128)), ins=[pa, pb, gates, dm],
                      outs=[jax.ShapeDtypeStruct(pa.shape, BF16), jax.ShapeDtypeStruct(pa.shape, BF16),
                            jax.ShapeDtypeStruct(gates.shape, BF16)], name=name)


def _loss_head(h3, w, target, nseq, name):
    Tp = h3.shape[0] // nseq
    nc = Tp // CHUNK

    def fn(h, w_, t, valid):
        y = _rms_fn(h, w_)
        e = (y - t) * valid
        return 0.5 * jnp.sum(jnp.mean(e * e, axis=-1, keepdims=True))

    def body(h_ref, w_ref, t_ref, loss_ref, dh_ref, dw_ref):
        b, c = pl.program_id(0), pl.program_id(1)
        valid = (c >= 1).astype(F32)
        t = t_ref[...]
        loss, vjp = jax.vjp(lambda h, w_: fn(h, w_, t, valid), h_ref[...], w_ref[...])
        dh, dw = vjp(jnp.ones((), F32))
        dh_ref[...] = dh

        @pl.when((b == 0) & (c == 0))
        def _():
            loss_ref[...] = jnp.zeros_like(loss_ref)
            dw_ref[...] = jnp.zeros_like(dw_ref)

        loss_ref[...] += jnp.full(loss_ref.shape, loss, F32)
        dw_ref[0:1, :] += dw

    return pl.pallas_call(
        body, grid=(nseq, nc),
        in_specs=[pl.BlockSpec((CHUNK, D_MODEL), lambda b, c: (b * nc + c, 0)),
                  pl.BlockSpec((1, D_MODEL), lambda b, c: (0, 0)),
                  pl.BlockSpec((None, CHUNK, D_MODEL), lambda b, c: (b, jnp.maximum(c - 1, 0), 0))],
        out_specs=[pl.BlockSpec((8, 128), lambda b, c: (0, 0)),
                   pl.BlockSpec((CHUNK, D_MODEL), lambda b, c: (b * nc + c, 0)),
                   pl.BlockSpec((8, D_MODEL), lambda b, c: (0, 0))],
        out_shape=[jax.ShapeDtypeStruct((8, 128), F32), jax.ShapeDtypeStruct(h3.shape, F32),
                   jax.ShapeDtypeStruct((8, D_MODEL), F32)],
        name=name, compiler_params=_cparams(("arbitrary", "arbitrary")),
    )(h3, w, target)


CONV_TILE = 512
CONV_HALO = 8


def _conv_fwd(xbc, w, b, pad, name):
    B, Tp, C = xbc.shape
    nch = Tp // CHUNK

    def body(x_ref, w_ref, b_ref, o_ref, xp):
        xp[0:CONV_HALO, :] = jnp.zeros((CONV_HALO, CONV_TILE), F32)
        xp[CONV_HALO:, :] = x_ref[...]
        for c in range(nch):
            acc = jnp.zeros((CHUNK, CONV_TILE), F32) + b_ref[...]
            for k in range(SSD_CONV):
                acc = acc + w_ref[k:k + 1, :] * xp[pl.ds(CONV_HALO + CHUNK * c - (SSD_CONV - 1) + k, CHUNK), :]
            row = CHUNK * c + lax.broadcasted_iota(jnp.int32, (CHUNK, 1), 0)
            o_ref[pl.ds(CHUNK * c, CHUNK), :] = jnp.where(row >= pad, _silu(acc), 0.0)

    return pl.pallas_call(
        body, grid=(B, C // CONV_TILE),
        in_specs=[pl.BlockSpec((None, Tp, CONV_TILE), lambda i, j: (i, 0, j)),
                  pl.BlockSpec((SSD_CONV, CONV_TILE), lambda i, j: (0, j)),
                  pl.BlockSpec((1, CONV_TILE), lambda i, j: (0, j))],
        out_specs=pl.BlockSpec((None, Tp, CONV_TILE), lambda i, j: (i, 0, j)),
        out_shape=jax.ShapeDtypeStruct(xbc.shape, F32),
        scratch_shapes=[pltpu.VMEM((Tp + CONV_HALO, CONV_TILE), F32)],
        name=name, compiler_params=_cparams(("arbitrary", "arbitrary")),
    )(xbc, w, b)


def _conv_bwd(xbc, w, b, dact, pad, name):
    B, Tp, C = xbc.shape
    nch = Tp // CHUNK

    def body(x_ref, w_ref, b_ref, da_ref, dx_ref, dw_ref, db_ref, xp, dp):
        bi = pl.program_id(1)
        xp[0:CONV_HALO, :] = jnp.zeros((CONV_HALO, CONV_TILE), F32)
        xp[CONV_HALO:, :] = x_ref[...]
        dp[pl.ds(Tp, CONV_HALO), :] = jnp.zeros((CONV_HALO, CONV_TILE), F32)
        dws = [jnp.zeros((1, CONV_TILE), F32) for _ in range(SSD_CONV)]
        dbs = jnp.zeros((1, CONV_TILE), F32)
        for c in range(nch):
            xs = [xp[pl.ds(CONV_HALO + CHUNK * c - (SSD_CONV - 1) + k, CHUNK), :] for k in range(SSD_CONV)]
            acc = jnp.zeros((CHUNK, CONV_TILE), F32) + b_ref[...]
            for k in range(SSD_CONV):
                acc = acc + w_ref[k:k + 1, :] * xs[k]
            row = CHUNK * c + lax.broadcasted_iota(jnp.int32, (CHUNK, 1), 0)
            sg = jax.nn.sigmoid(acc)
            dpre = jnp.where(row >= pad, da_ref[pl.ds(CHUNK * c, CHUNK), :] * (sg * (1.0 + acc * (1.0 - sg))), 0.0)
            dp[pl.ds(CHUNK * c, CHUNK), :] = dpre
            dbs = dbs + jnp.sum(dpre, axis=0, keepdims=True)
            for k in range(SSD_CONV):
                dws[k] = dws[k] + jnp.sum(dpre * xs[k], axis=0, keepdims=True)
        for c in range(nch):
            acc = jnp.zeros((CHUNK, CONV_TILE), F32)
            for k in range(SSD_CONV):
                acc = acc + w_ref[k:k + 1, :] * dp[pl.ds(CHUNK * c + (SSD_CONV - 1) - k, CHUNK), :]
            dx_ref[pl.ds(CHUNK * c, CHUNK), :] = acc

        @pl.when(bi == 0)
        def _():
            dw_ref[...] = jnp.zeros_like(dw_ref)
            db_ref[...] = jnp.zeros_like(db_ref)

        for k in range(SSD_CONV):
            dw_ref[k:k + 1, :] += dws[k]
        db_ref[0:1, :] += dbs

    return pl.pallas_call(
        body, grid=(C // CONV_TILE, B),
        in_specs=[pl.BlockSpec((None, Tp, CONV_TILE), lambda j, i: (i, 0, j)),
                  pl.BlockSpec((SSD_CONV, CONV_TILE), lambda j, i: (0, j)),
                  pl.BlockSpec((1, CONV_TILE), lambda j, i: (0, j)),
                  pl.BlockSpec((None, Tp, CONV_TILE), lambda j, i: (i, 0, j))],
        out_specs=[pl.BlockSpec((None, Tp, CONV_TILE), lambda j, i: (i, 0, j)),
                   pl.BlockSpec((8, CONV_TILE), lambda j, i: (0, j)),
                   pl.BlockSpec((8, CONV_TILE), lambda j, i: (0, j))],
        out_shape=[jax.ShapeDtypeStruct(xbc.shape, F32), jax.ShapeDtypeStruct((8, C), F32),
                   jax.ShapeDtypeStruct((8, C), F32)],
        scratch_shapes=[pltpu.VMEM((Tp + CONV_HALO, CONV_TILE), F32), pltpu.VMEM((Tp + CONV_HALO, CONV_TILE), F32)],
        name=name, compiler_params=_cparams(("arbitrary", "arbitrary")),
    )(xbc, w, b, dact)


def _ssd_chunk(xs, bm, cm, dtr, z, state, dt_bias, a_log, dskip, norm_w, valid):
    Q = xs.shape[0]
    lane = lax.broadcasted_iota(jnp.int32, (1, 128), 1)
    dt = jnp.where(lane < SSD_HEADS, _softplus(dtr + dt_bias), 0.0) * valid
    a = dt * (-jnp.exp(a_log))
    tril = _tril(Q)
    cs = jnp.dot(tril.astype(F32), a, precision=HIGHEST)
    cs_t = cs.T
    cs_end = _row_of(cs, Q - 1)
    low = lane < SSD_HEAD_DIM
    low_rows = lax.broadcasted_iota(jnp.int32, (128, 1), 0) < SSD_HEAD_DIM
    ys, new_state = [], []
    for g in range(SSD_GROUPS):
        bg = bm[:, 128 * g:128 * (g + 1)]
        cg = cm[:, 128 * g:128 * (g + 1)]
        cb = _mm_nt(cg, bg)
        for pr in range(2):
            p = 2 * g + pr
            h0, h1 = 2 * p, 2 * p + 1
            xp = xs[:, 128 * p:128 * (p + 1)]
            c0, c1 = _col_of(cs, h0), _col_of(cs, h1)
            e0, e1 = _col_of(cs_end, h0), _col_of(cs_end, h1)
            xd = xp * jnp.where(low, _col_of(dt, h0), _col_of(dt, h1))
            l0 = jnp.exp(jnp.where(tril, c0 - _row_of(cs_t, h0), -1e30))
            l1 = jnp.exp(jnp.where(tril, c1 - _row_of(cs_t, h1), -1e30))
            y_diag = jnp.where(low, _mm(cb * l0, xd), _mm(cb * l1, xd))
            to_end = jnp.where(low, jnp.exp(e0 - c0), jnp.exp(e1 - c1))
            sp = state[128 * p:128 * (p + 1), :]
            y_off = _mm_nt(cg, sp) * jnp.where(low, jnp.exp(c0), jnp.exp(c1))
            new_state.append(sp * jnp.where(low_rows, jnp.exp(e0), jnp.exp(e1)) + _mm_tn(xd * to_end, bg))
            ys.append(y_diag + y_off + xp * jnp.where(low, _col_of(dskip, h0), _col_of(dskip, h1)))
    y = jnp.concatenate(ys, axis=1) * _silu(z)
    gw = SSD_INNER // SSD_GROUPS
    outs = []
    for g in range(SSD_GROUPS):
        blk = y[:, gw * g:gw * (g + 1)]
        outs.append(blk * lax.rsqrt(jnp.mean(blk * blk, axis=-1, keepdims=True) + EPS))
    return jnp.concatenate(outs, axis=1) * norm_w, jnp.concatenate(new_state, axis=0)


def _valid_rows(c, pad):
    row = c * CHUNK + lax.broadcasted_iota(jnp.int32, (CHUNK, 1), 0)
    return (row >= pad).astype(F32)


def _ssd_fwd(xact, dtr, z, dt_bias, a_log, dskip, norm_w, pad, name):
    B, Tp, _ = xact.shape
    nc = Tp // CHUNK

    def body(xs_ref, bm_ref, cm_ref, dt_ref, z_ref, db_ref, al_ref, ds_ref, nw_ref, y_ref, save_ref, st):
        c = pl.program_id(1)

        @pl.when(c == 0)
        def _():
            st[...] = jnp.zeros_like(st)

        s0 = st[...]
        save_ref[...] = s0
        y, s1 = _ssd_chunk(xs_ref[...], bm_ref[...], cm_ref[...], dt_ref[...], z_ref[...], s0, db_ref[...],
                           al_ref[...], ds_ref[...], nw_ref[...], _valid_rows(c, pad))
        y_ref[...] = y.astype(y_ref.dtype)
        st[...] = s1

    row = lambda w, off=0: pl.BlockSpec((None, CHUNK, w), lambda b, c: (b, c, off))
    par = lambda w: pl.BlockSpec((1, w), lambda b, c: (0, 0))
    return pl.pallas_call(
        body, grid=(B, nc),
        in_specs=[row(1024, 0), row(512, 2), row(512, 3), row(128), row(1024), par(128), par(128), par(128), par(1024)],
        out_specs=[row(1024), pl.BlockSpec((None, None, 1024, 128), lambda b, c: (b, c, 0, 0))],
        out_shape=[jax.ShapeDtypeStruct((B, Tp, SSD_INNER), BF16), jax.ShapeDtypeStruct((B, nc, 1024, 128), F32)],
        scratch_shapes=[pltpu.VMEM((1024, 128), F32)],
        name=name, compiler_params=_cparams(("arbitrary", "arbitrary")),
    )(xact, xact, xact, dtr, z, dt_bias, a_log, dskip, norm_w)


def _ssd_bwd(xact, dtr, z, dt_bias, a_log, dskip, norm_w, saved, dy, pad, name):
    B, Tp, _ = xact.shape
    nc = Tp // CHUNK

    def body(xs_ref, bm_ref, cm_ref, dt_ref, z_ref, db_ref, al_ref, ds_ref, nw_ref, sv_ref, dy_ref,
             dx_ref, ddt_ref, dz_ref, dpar_ref, dnw_ref, dst):
        b, i = pl.program_id(0), pl.program_id(1)
        c = nc - 1 - i

        @pl.when(i == 0)
        def _():
            dst[...] = jnp.zeros_like(dst)

        valid = _valid_rows(c, pad)
        fn = lambda *a: _ssd_chunk(*a, valid)
        _, vjp = jax.vjp(fn, xs_ref[...], bm_ref[...], cm_ref[...], dt_ref[...], z_ref[...], sv_ref[...],
                         db_ref[...], al_ref[...], ds_ref[...], nw_ref[...])
        dxs, dbm, dcm, ddt, dz, dstate, ddb, dal, dds, dnw = vjp((dy_ref[...].astype(F32), dst[...]))
        dx_ref[:, 0:1024] = dxs
        dx_ref[:, 1024:1536] = dbm
        dx_ref[:, 1536:2048] = dcm
        ddt_ref[...] = ddt
        dz_ref[...] = dz
        dst[...] = dstate

        @pl.when((b == 0) & (i == 0))
        def _():
            dpar_ref[...] = jnp.zeros_like(dpar_ref)
            dnw_ref[...] = jnp.zeros_like(dnw_ref)

        dpar_ref[0:1, :] += ddb
        dpar_ref[1:2, :] += dal
        dpar_ref[2:3, :] += dds
        dnw_ref[0:1, :] += dnw

    row = lambda w, off=0: pl.BlockSpec((None, CHUNK, w), lambda b, i: (b, nc - 1 - i, off))
    par = lambda w: pl.BlockSpec((1, w), lambda b, i: (0, 0))
    acc = lambda w: pl.BlockSpec((8, w), lambda b, i: (0, 0))
    outs = pl.pallas_call(
        body, grid=(B, nc),
        in_specs=[row(1024, 0), row(512, 2), row(512, 3), row(128), row(1024), par(128), par(128), par(128), par(1024),
                  pl.BlockSpec((None, None, 1024, 128), lambda b, i: (b, nc - 1 - i, 0, 0)), row(1024)],
        out_specs=[row(2048), row(128), row(1024), acc(128), acc(1024)],
        out_shape=[jax.ShapeDtypeStruct((B, Tp, 2048), F32), jax.ShapeDtypeStruct((B, Tp, 128), F32),
                   jax.ShapeDtypeStruct((B, Tp, 1024), F32), jax.ShapeDtypeStruct((8, 128), F32),
                   jax.ShapeDtypeStruct((8, 1024), F32)],
        scratch_shapes=[pltpu.VMEM((1024, 128), F32)],
        name=name, compiler_params=_cparams(("arbitrary", "arbitrary")),
    )(xact, xact, xact, dtr, z, dt_bias, a_log, dskip, norm_w, saved, dy)
    return outs


def _hg_chunk(qr, fr, ir, gr, state_t, p0, p1, norm_w, valid):
    Q = qr.shape[0]
    lb = jax.nn.sigmoid(p0 - p1)
    f = lb + (1.0 - lb) * jax.nn.sigmoid(fr)
    k = 1.0 - f
    q = _silu(qr)
    v = ir * valid
    cum = jnp.dot(_tril(Q).astype(F32), jnp.log(f), precision=HIGHEST)
    cum_end = _row_of(cum, Q - 1)
    o_inter = _mm_nt(q * jnp.exp(cum), state_t)
    nblk = Q // HG_SUB
    row = lax.broadcasted_iota(jnp.int32, (Q, 1), 0)
    ri = lax.broadcasted_iota(jnp.int32, (Q, Q), 0)
    ci = lax.broadcasted_iota(jnp.int32, (Q, Q), 1)
    mids = jnp.concatenate([jnp.broadcast_to(_row_of(cum, HG_SUB * i + HG_SUB // 2 - 1), (HG_SUB, cum.shape[1]))
                            for i in range(nblk)], axis=0)
    sh = HG_SUB.bit_length() - 1
    same = (jnp.right_shift(ri, sh) == jnp.right_shift(ci, sh)) & (ri >= ci)
    att = jnp.where(same, _mm_nt(q * jnp.exp(cum - mids), k * jnp.exp(mids - cum)), 0.0)
    for i in range(1, nblk):
        lo = HG_SUB * i
        start = _row_of(cum, lo - 1)
        qa = q * jnp.exp(jnp.where((row >= lo) & (row < lo + HG_SUB), cum - start, -1e30))
        ka = k * jnp.exp(jnp.where(row < lo, start - cum, -1e30))
        att = att + _mm_nt(qa, ka)
    o = o_inter + _mm(att, v)
    new_state_t = state_t * jnp.exp(cum_end) + _mm_tn(v, k * jnp.exp(cum_end - cum))
    o = o * lax.rsqrt(jnp.mean(o * o, axis=-1, keepdims=True) + EPS) * norm_w
    return o * _silu(gr), new_state_t


HG_PER_STEP = 4
HG_COLS = 4 * 128


def _hg_fwd(qfig, lbh, nwh, pad, name):
    B, Tp, _ = qfig.shape
    nc = Tp // CHUNK
    hp = HG_PER_STEP

    def body(x_ref, lb_ref, nw_ref, y_ref, save_ref, st):
        c = pl.program_id(1)

        @pl.when(c == 0)
        def _():
            st[...] = jnp.zeros_like(st)

        valid = _valid_rows(c, pad)
        for j in range(hp):
            for b in range(B):
                s0 = st[j, b]
                save_ref[j, b] = s0
                col = lambda k: x_ref[b, :, HG_COLS * j + 128 * k:HG_COLS * j + 128 * (k + 1)]
                y, s1 = _hg_chunk(col(0), col(1), col(2), col(3), s0, lb_ref[j, 0:1, :], lb_ref[j, 1:2, :], nw_ref[j], valid)
                y_ref[b, :, 128 * j:128 * (j + 1)] = y.astype(y_ref.dtype)
                st[j, b] = s1

    return pl.pallas_call(
        body, grid=(HG_HEADS // hp, nc),
        in_specs=[pl.BlockSpec((B, CHUNK, HG_COLS * hp), lambda h, c: (0, c, h)),
                  pl.BlockSpec((hp, 2, 128), lambda h, c: (h, 0, 0)),
                  pl.BlockSpec((hp, 1, 128), lambda h, c: (h, 0, 0))],
        out_specs=[pl.BlockSpec((B, CHUNK, 128 * hp), lambda h, c: (0, c, h)),
                   pl.BlockSpec((hp, B, None, 128, 128), lambda h, c: (h, 0, c, 0, 0))],
        out_shape=[jax.ShapeDtypeStruct((B, Tp, 1024), BF16), jax.ShapeDtypeStruct((HG_HEADS, B, nc, 128, 128), F32)],
        scratch_shapes=[pltpu.VMEM((hp, B, 128, 128), F32)],
        name=name, compiler_params=_cparams(("arbitrary", "arbitrary")),
    )(qfig, lbh, nwh)


def _hg_bwd(qfig, lbh, nwh, saved, dy, pad, name):
    B, Tp, _ = qfig.shape
    nc = Tp // CHUNK
    hp = HG_PER_STEP

    def body(x_ref, lb_ref, nw_ref, sv_ref, dy_ref, dx_ref, dlb_ref, dnw_ref, dst):
        i = pl.program_id(1)
        c = nc - 1 - i

        @pl.when(i == 0)
        def _():
            dst[...] = jnp.zeros_like(dst)
            dlb_ref[...] = jnp.zeros_like(dlb_ref)
            dnw_ref[...] = jnp.zeros_like(dnw_ref)

        valid = _valid_rows(c, pad)
        fn = lambda *a: _hg_chunk(*a, valid)
        for j in range(hp):
            for b in range(B):
                col = lambda k: x_ref[b, :, HG_COLS * j + 128 * k:HG_COLS * j + 128 * (k + 1)]
                _, vjp = jax.vjp(fn, col(0), col(1), col(2), col(3), sv_ref[j, b], lb_ref[j, 0:1, :], lb_ref[j, 1:2, :], nw_ref[j])
                d4 = vjp((dy_ref[b, :, 128 * j:128 * (j + 1)].astype(F32), dst[j, b]))
                for k in range(4):
                    dx_ref[b, :, HG_COLS * j + 128 * k:HG_COLS * j + 128 * (k + 1)] = d4[k].astype(dx_ref.dtype)
                dst[j, b] = d4[4]
                dlb_ref[j, 0:1, :] += d4[5]
                dlb_ref[j, 1:2, :] += d4[6]
                dnw_ref[j, 0:1, :] += d4[7]

    acc = pl.BlockSpec((hp, 8, 128), lambda h, i: (h, 0, 0))
    return pl.pallas_call(
        body, grid=(HG_HEADS // hp, nc),
        in_specs=[pl.BlockSpec((B, CHUNK, HG_COLS * hp), lambda h, i: (0, nc - 1 - i, h)),
                  pl.BlockSpec((hp, 2, 128), lambda h, i: (h, 0, 0)),
                  pl.BlockSpec((hp, 1, 128), lambda h, i: (h, 0, 0)),
                  pl.BlockSpec((hp, B, None, 128, 128), lambda h, i: (h, 0, nc - 1 - i, 0, 0)),
                  pl.BlockSpec((B, CHUNK, 128 * hp), lambda h, i: (0, nc - 1 - i, h))],
        out_specs=[pl.BlockSpec((B, CHUNK, HG_COLS * hp), lambda h, i: (0, nc - 1 - i, h)), acc, acc],
        out_shape=[jax.ShapeDtypeStruct((B, Tp, 4096), BF16), jax.ShapeDtypeStruct((HG_HEADS, 8, 128), F32),
                   jax.ShapeDtypeStruct((HG_HEADS, 8, 128), F32)],
        scratch_shapes=[pltpu.VMEM((hp, B, 128, 128), F32)],
        name=name, compiler_params=_cparams(("arbitrary", "arbitrary")),
    )(qfig, lbh, nwh, saved, dy)


def _adamw(w, g, m, v, name):
    R, C = w.shape
    tr = _pick(R, (256, 176, 128, 64, 8)) if R > 256 else R

    def body(w_ref, g_ref, m_ref, v_ref, d_ref, mo_ref, vo_ref):
        g_ = g_ref[...]
        m_ = ADAM_B1 * m_ref[...] + (1.0 - ADAM_B1) * g_
        v_ = ADAM_B2 * v_ref[...] + (1.0 - ADAM_B2) * (g_ * g_)
        m_hat = m_ / (1.0 - ADAM_B1 ** ADAM_STEP)
        v_hat = v_ / (1.0 - ADAM_B2 ** ADAM_STEP)
        d_ref[...] = -ADAM_LR * (m_hat / (jnp.sqrt(v_hat) + ADAM_EPS) + ADAM_WD * w_ref[...])
        mo_ref[...] = m_
        vo_ref[...] = v_

    sp = pl.BlockSpec((tr, C), lambda i: (i, 0))
    sh = jax.ShapeDtypeStruct((R, C), F32)
    return pl.pallas_call(body, grid=(R // tr,), in_specs=[sp] * 4, out_specs=[sp] * 3, out_shape=[sh] * 3,
                          name=name, compiler_params=_cparams(("arbitrary",)))(w, g, m, v)


def _ffn_fwd(h, norm_w, w_gu, w_down, tag):
    n = _rms_fwd(h, norm_w, f"{tag}_norm")
    gu = _matmul(n, w_gu, mode="nn", out_dtype=BF16, name=f"{tag}_gu")
    a = _swiglu_fwd(gu, f"{tag}_act")
    out = _matmul(a, w_down, mode="nn", out_dtype=F32, alpha=0.5, res=h, name=f"{tag}_down")
    return out, (n, gu, a)


def _ffn_bwd(h, norm_w, w_gu, w_down, saved, dout, tag):
    n, gu, a = saved
    da = _matmul(dout, w_down, mode="nt", out_dtype=BF16, alpha=0.5, name=f"{tag}_d_act")
    dw_down = _matmul(a, dout, mode="tn", out_dtype=F32, alpha=0.5, name=f"{tag}_dw_down")
    dgu = _swiglu_bwd(gu, da, f"{tag}_d_gu")
    dn = _matmul(dgu, w_gu, mode="nt", out_dtype=F32, name=f"{tag}_d_norm")
    dw_gu = _matmul(n, dgu, mode="tn", out_dtype=F32, out_groups=N_CHIPS, name=f"{tag}_dw_gu")
    dh, dnw = _rms_bwd(h, norm_w, dn, dout, f"{tag}_d_in")
    return dh, dnw, dw_gu, dw_down


IN_NAMES = ("z", "xbc", "dt", "q", "f", "i", "g", "gates")


def _split_w_in(w_in_full):
    pts = [0]
    for s in IN_SIZES:
        pts.append(pts[-1] + s)
    sl = lambda i, j: w_in_full[:, pts[i]:pts[j]]
    qfig = sl(3, 7).reshape(D_MODEL, 4, HG_HEADS, 128).transpose(0, 2, 1, 3).reshape(D_MODEL, 4 * D_MODEL)
    return {"z": sl(0, 1), "xbc": sl(1, 2), "dt": jnp.pad(sl(2, 3), ((0, 0), (0, 128 - SSD_HEADS))),
            "qfig": qfig, "gates": sl(7, 9)}


def _local_step(x, target, W):
    B, S, _ = x.shape
    T = N_META + S
    pad = (-T) % CHUNK
    Tp = T + pad
    assert pad + N_META == CHUNK
    R = B * Tp
    meta = jnp.broadcast_to(W["meta_tokens"][None], (B, N_META, D_MODEL))
    h0 = jnp.concatenate([jnp.zeros((B, pad, D_MODEL), F32), meta, x], axis=1).reshape(R, D_MODEL)

    h1, sv1 = _ffn_fwd(h0, W["ffn1_norm"], W["ffn1_w_gu"], W["ffn1_w_down"], "ffn1")
    um = _rms_fwd(h1, W["mix_norm"], "mix_norm")
    wi = W["w_in"]
    z = _matmul(um, wi["z"], mode="nn", out_dtype=F32, name="in_z")
    xbc = _matmul(um, wi["xbc"], mode="nn", out_dtype=F32, name="in_xbc")
    dtr = _matmul(um, wi["dt"], mode="nn", out_dtype=F32, name="in_dt")
    qfig = _matmul(um, wi["qfig"], mode="nn", out_dtype=F32, name="in_qfig")
    gates = _matmul(um, wi["gates"], mode="nn", out_dtype=F32, name="in_gates")

    r3 = lambda t: t.reshape(B, Tp, t.shape[-1])
    lane_pad = lambda t: jnp.pad(t, ((0, 0), (0, 128 - t.shape[1])))
    dt_bias, a_log, dskip = lane_pad(W["ssd_dt_bias"]), lane_pad(W["ssd_a_log"]), lane_pad(W["ssd_d"])
    xact = _conv_fwd(r3(xbc), W["ssd_conv_w"], W["ssd_conv_b"], pad, "conv_fwd")
    ya, ssd_saved = _ssd_fwd(xact, r3(dtr), r3(z), dt_bias, a_log, dskip, W["ssd_norm"], pad, "ssd_fwd")
    lbh = W["hg_lower_bound"].reshape(2, HG_HEADS, 128).transpose(1, 0, 2)
    nwh = W["hg_norm"].reshape(HG_HEADS, 1, 128)
    yb, hg_saved = _hg_fwd(r3(qfig), lbh, nwh, pad, "hg_fwd")
    ya2, yb2 = ya.reshape(R, -1), yb.reshape(R, -1)
    pa = _matmul(ya2, W["w_branch_a"], mode="nn", out_dtype=F32, name="branch_a")
    pb = _matmul(yb2, W["w_branch_b"], mode="nn", out_dtype=F32, name="branch_b")
    mg = _merge_fwd(pa, pb, gates, "merge")
    h2 = _matmul(mg, W["w_out"], mode="nn", out_dtype=F32, res=h1, name="mix_out")
    h3, sv2 = _ffn_fwd(h2, W["ffn2_norm"], W["ffn2_w_gu"], W["ffn2_w_down"], "ffn2")

    loss, dh3, d_final = _loss_head(h3, W["final_norm"].reshape(1, D_MODEL), target, B, "loss_head")

    G = {"final_norm": d_final[0]}
    dh2, dnw, G["ffn2_w_gu"], G["ffn2_w_down"] = _ffn_bwd(h2, W["ffn2_norm"], W["ffn2_w_gu"], W["ffn2_w_down"], sv2, dh3, "ffn2")
    G["ffn2_norm"] = dnw[0:1]
    dmg = _matmul(dh2, W["w_out"], mode="nt", out_dtype=BF16, name="d_merge")
    G["w_out"] = _matmul(mg, dh2, mode="tn", out_dtype=F32, name="dw_out")
    dpa, dpb, dgates = _merge_bwd(pa, pb, gates, dmg, "merge_bwd")
    dya = _matmul(dpa, W["w_branch_a"], mode="nt", out_dtype=BF16, name="d_ya")
    dyb = _matmul(dpb, W["w_branch_b"], mode="nt", out_dtype=BF16, name="d_yb")
    G["w_branch_a"] = _matmul(ya2, dpa, mode="tn", out_dtype=F32, name="dw_branch_a")
    G["w_branch_b"] = _matmul(yb2, dpb, mode="tn", out_dtype=F32, name="dw_branch_b")

    dxact, ddtr, dz, dpar, dnw = _ssd_bwd(xact, r3(dtr), r3(z), dt_bias, a_log, dskip, W["ssd_norm"], ssd_saved,
                                          r3(dya), pad, "ssd_bwd")
    G["ssd_dt_bias"], G["ssd_a_log"], G["ssd_d"] = dpar[0:1, :SSD_HEADS], dpar[1:2, :SSD_HEADS], dpar[2:3, :SSD_HEADS]
    G["ssd_norm"] = dnw[0:1]
    dxbc, dcw, dcb = _conv_bwd(r3(xbc), W["ssd_conv_w"], W["ssd_conv_b"], dxact, pad, "conv_bwd")
    G["ssd_conv_w"], G["ssd_conv_b"] = dcw[0:SSD_CONV], dcb[0:1]
    dqfig, dlb, dhn = _hg_bwd(r3(qfig), lbh, nwh, hg_saved, r3(dyb), pad, "hg_bwd")
    G["hg_lower_bound"] = dlb[:, 0:2, :].transpose(1, 0, 2).reshape(2, D_MODEL)
    G["hg_norm"] = dhn[:, 0, :].reshape(1, D_MODEL)

    r2 = lambda t: t.reshape(R, t.shape[-1])
    pieces = [("z", r2(dz)), ("xbc", r2(dxbc)), ("dt", r2(ddtr)), ("qfig", r2(dqfig)), ("gates", dgates)]
    dum = None
    dwi = {}
    for nm, dpiece in pieces:
        dum = _matmul(dpiece, wi[nm], mode="nt", out_dtype=F32, res=dum, name=f"d_mix_{nm}")
        dwi[nm] = _matmul(um, dpiece, mode="tn", out_dtype=F32, name=f"dw_in_{nm}")
    dw_qfig = dwi["qfig"].reshape(D_MODEL, HG_HEADS, 4, 128).transpose(0, 2, 1, 3).reshape(D_MODEL, 4 * D_MODEL)
    G["w_in"] = jnp.concatenate([dwi["z"], dwi["xbc"], dwi["dt"][:, :SSD_HEADS], dw_qfig, dwi["gates"]], axis=1)
    dh1, dnw = _rms_bwd(h1, W["mix_norm"], dum, dh2, "mix_norm_bwd")
    G["mix_norm"] = dnw[0:1]
    dh0, dnw, G["ffn1_w_gu"], G["ffn1_w_down"] = _ffn_bwd(h0, W["ffn1_norm"], W["ffn1_w_gu"], W["ffn1_w_down"], sv1, dh1, "ffn1")
    G["ffn1_norm"] = dnw[0:1]
    dh0 = dh0.reshape(B, Tp, D_MODEL)
    G["meta_tokens"] = jnp.sum(dh0[:, pad:CHUNK], axis=0)
    return loss, dh0[:, CHUNK:], G


ANY = pl.BlockSpec(memory_space=pl.ANY)


def _place():
    return lax.axis_index("x"), lax.axis_index("y"), lax.axis_index("c")


def _other_chips(x, y):
    return [(1 - x, y), (x, 1 - y), (1 - x, 1 - y)]


def _remote(src, dst, ssem, rsem, dev):
    return pltpu.make_async_remote_copy(src_ref=src, dst_ref=dst, send_sem=ssem, recv_sem=rsem,
                                        device_id=dev, device_id_type=MESH)


def _exchange8(buf, reduce, name):
    n, w = buf.shape

    def body(x_ref, *rest):
        if reduce:
            red_ref, out_ref, ssem, rsem = rest
        else:
            out_ref, ssem, rsem = rest
        x, y, c = _place()
        me = 4 * x + 2 * y + c
        out_ref[me] = x_ref[...]
        copies = []
        for k in range(1, 8):
            px = 1 - x if (k >> 2) & 1 else x
            py = 1 - y if (k >> 1) & 1 else y
            pc = 1 - c if k & 1 else c
            cp = _remote(x_ref, out_ref.at[me], ssem.at[k - 1], rsem.at[k - 1], (px, py, pc))
            cp.start()
            copies.append((cp, 4 * px + 2 * py + pc))
        for k, (cp, peer) in enumerate(copies):
            _remote(x_ref, out_ref.at[peer], ssem.at[k], rsem.at[k], (x, y, c)).wait_recv()
        for cp, _ in copies:
            cp.wait_send()
        if reduce:
            acc = out_ref[0]
            for d in range(1, 8):
                acc = acc + out_ref[d]
            red_ref[...] = acc

    vm = pl.BlockSpec(memory_space=pltpu.VMEM)
    g_shape = jax.ShapeDtypeStruct((8, n, w), F32)
    if reduce:
        out_shape, out_specs, scratch = [jax.ShapeDtypeStruct((n, w), F32)], [vm], [pltpu.VMEM((8, n, w), F32)]
    else:
        out_shape, out_specs, scratch = [g_shape], [vm], []
    return pl.pallas_call(
        body, in_specs=[vm], out_specs=out_specs, out_shape=out_shape,
        scratch_shapes=scratch + [pltpu.SemaphoreType.DMA((7,)), pltpu.SemaphoreType.DMA((7,))], name=name,
    )(buf)[0]


def _gather_big(blocks, name):
    n = len(blocks)
    half = [s.shape[1] // 2 for s in blocks]

    def body(*refs):
        full = refs[n:2 * n]
        ssem, rsem, fssem, frsem = refs[2 * n:]
        x, y, c = _place()
        q = 2 * x + y
        chips = _other_chips(x, y)
        piece = lambda s, qq, cc: full[s].at[qq, pl.ds(cc * half[s], half[s])]
        sends = []
        for j, (px, py) in enumerate(chips):
            for s in range(n):
                cp = _remote(piece(s, q, c), piece(s, q, c), ssem.at[s, j], rsem.at[s, j], (px, py, c))
                cp.start()
                sends.append(cp)
        for j, (px, py) in enumerate(chips):
            for s in range(n):
                got = piece(s, 2 * px + py, c)
                _remote(got, got, ssem.at[s, j], rsem.at[s, j], (px, py, c)).wait_recv()
                cp = _remote(got, got, fssem.at[s, j], frsem.at[s, j], (x, y, 1 - c))
                cp.start()
                sends.append(cp)
        for j, (px, py) in enumerate(chips):
            for s in range(n):
                got = piece(s, 2 * px + py, 1 - c)
                _remote(got, got, fssem.at[s, j], frsem.at[s, j], (x, y, 1 - c)).wait_recv()
        for cp in sends:
            cp.wait_send()

    return pl.pallas_call(
        body, in_specs=[ANY] * n, out_specs=[ANY] * n,
        out_shape=[jax.ShapeDtypeStruct(s.shape, s.dtype) for s in blocks],
        input_output_aliases={s: s for s in range(n)},
        scratch_shapes=[pltpu.SemaphoreType.DMA((n, 3))] * 4, name=name,
    )(*blocks)


def _pair_swap(parts, name):
    n = len(parts)
    half = [p.shape[1] // 2 for p in parts]

    def body(*refs):
        src, got = refs[:n], refs[n:2 * n]
        ssem, rsem = refs[2 * n:]
        x, y, c = _place()
        copies = []
        for s in range(n):
            cp = _remote(src[s].at[pl.ds(0, N_CHIPS), pl.ds((1 - c) * half[s], half[s])], got[s], ssem.at[s], rsem.at[s], (x, y, 1 - c))
            cp.start()
            copies.append(cp)
        for cp in copies:
            cp.wait_recv()
        for cp in copies:
            cp.wait_send()

    return pl.pallas_call(
        body, in_specs=[ANY] * n, out_specs=[ANY] * n,
        out_shape=[jax.ShapeDtypeStruct((N_CHIPS, h, p.shape[2]), p.dtype) for p, h in zip(parts, half)],
        scratch_shapes=[pltpu.SemaphoreType.DMA((n,)), pltpu.SemaphoreType.DMA((n,))], name=name,
    )(*parts)


def _to_owners(sums, name):
    n = len(sums)

    def body(*refs):
        src, got = refs[:n], refs[n:2 * n]
        lsem, ssem, rsem = refs[2 * n:]
        x, y, c = _place()
        q = 2 * x + y
        chips = _other_chips(x, y)
        started, sends = [], []
        for s in range(n):
            cp = pltpu.make_async_copy(src[s].at[q], got[s].at[q], lsem.at[s])
            cp.start()
            started.append(cp)
        for j, (px, py) in enumerate(chips):
            for s in range(n):
                cp = _remote(src[s].at[2 * px + py], got[s].at[q], ssem.at[s, j], rsem.at[s, j], (px, py, c))
                cp.start()
                sends.append(cp)
        for j, (px, py) in enumerate(chips):
            for s in range(n):
                slot = got[s].at[2 * px + py]
                _remote(slot, slot, ssem.at[s, j], rsem.at[s, j], (px, py, c)).wait_recv()
        for cp in sends:
            cp.wait_send()
        for cp in started:
            cp.wait()

    return pl.pallas_call(
        body, in_specs=[ANY] * n, out_specs=[ANY] * n,
        out_shape=[jax.ShapeDtypeStruct(s.shape, s.dtype) for s in sums],
        scratch_shapes=[pltpu.SemaphoreType.DMA((n,)), pltpu.SemaphoreType.DMA((n, 3)), pltpu.SemaphoreType.DMA((n, 3))],
        name=name,
    )(*sums)


def _pair_join(blocks, name):
    n = len(blocks)

    def body(*refs):
        out = refs[n:2 * n]
        ssem, rsem = refs[2 * n:]
        x, y, c = _place()
        sends = []
        for s in range(n):
            h = blocks[s].shape[0] // 2
            mine = out[s].at[pl.ds(c * h, h)]
            cp = _remote(mine, mine, ssem.at[s], rsem.at[s], (x, y, 1 - c))
            cp.start()
            sends.append(cp)
        for s in range(n):
            h = blocks[s].shape[0] // 2
            theirs = out[s].at[pl.ds((1 - c) * h, h)]
            _remote(theirs, theirs, ssem.at[s], rsem.at[s], (x, y, 1 - c)).wait_recv()
        for cp in sends:
            cp.wait_send()

    return pl.pallas_call(
        body, in_specs=[ANY] * n, out_specs=[ANY] * n,
        out_shape=[jax.ShapeDtypeStruct(b.shape, b.dtype) for b in blocks],
        input_output_aliases={s: s for s in range(n)},
        scratch_shapes=[pltpu.SemaphoreType.DMA((n,))] * 2, name=name,
    )(*blocks)


WIRE = BF16


def _row_tile(h):
    return _pick(h, (256, 272, 128, 16))


def _add_pair(part, got, c, name):
    _, h, w = got.shape
    tr = _row_tile(h)
    nt = h // tr

    def body(c_ref, p_ref, g_ref, o_ref):
        o_ref[...] = (p_ref[...] + g_ref[...].astype(F32)).astype(o_ref.dtype)

    return pl.pallas_call(
        body,
        grid_spec=pltpu.PrefetchScalarGridSpec(
            num_scalar_prefetch=1, grid=(N_CHIPS, nt),
            in_specs=[pl.BlockSpec((None, tr, w), lambda q, i, c_ref: (q, c_ref[0] * nt + i, 0)),
                      pl.BlockSpec((None, tr, w), lambda q, i, c_ref: (q, i, 0))],
            out_specs=pl.BlockSpec((None, tr, w), lambda q, i, c_ref: (q, i, 0))),
        out_shape=jax.ShapeDtypeStruct(got.shape, WIRE), name=name,
        compiler_params=_cparams(("arbitrary", "arbitrary")),
    )(c.reshape(1).astype(jnp.int32), part, got)


def _sum_chips(slots, c, name):
    _, h, w = slots.shape
    tr = _row_tile(h)
    nt = h // tr

    def body(c_ref, s_ref, o_ref):
        o_ref[...] = ((s_ref[0].astype(F32) + s_ref[1].astype(F32)) + s_ref[2].astype(F32)) + s_ref[3].astype(F32)

    return pl.pallas_call(
        body,
        grid_spec=pltpu.PrefetchScalarGridSpec(
            num_scalar_prefetch=1, grid=(nt,),
            in_specs=[pl.BlockSpec((N_CHIPS, tr, w), lambda i, c_ref: (0, i, 0))],
            out_specs=pl.BlockSpec((tr, w), lambda i, c_ref: (c_ref[0] * nt + i, 0))),
        out_shape=jax.ShapeDtypeStruct((2 * h, w), F32), name=name,
        compiler_params=_cparams(("arbitrary",)),
    )(c.reshape(1).astype(jnp.int32), slots)


def _reduce_to_owners(parts, c):
    got = _pair_swap(parts, "grad_pair_swap")
    sums = [_add_pair(p, g, c, f"grad_pair_add{i}") for i, (p, g) in enumerate(zip(parts, got))]
    slots = _to_owners(sums, "grad_to_owners")
    blocks = [_sum_chips(s, c, f"grad_sum_chips{i}") for i, s in enumerate(slots)]
    return _pair_join(blocks, "grad_pair_join")


WEIGHTS = ("meta_tokens", "ffn1_norm", "ffn1_w_gu", "ffn1_w_down", "mix_norm", "w_in", "ssd_conv_w", "ssd_conv_b",
           "ssd_dt_bias", "ssd_a_log", "ssd_d", "ssd_norm", "hg_lower_bound", "hg_norm", "w_branch_a", "w_branch_b",
           "w_out", "ffn2_norm", "ffn2_w_gu", "ffn2_w_down", "final_norm")
BIG = ("ffn1_w_gu", "ffn1_w_down", "w_in", "w_branch_a", "w_branch_b", "w_out", "ffn2_w_gu", "ffn2_w_down")
ROW_SHARDED = ("ffn1_w_down", "ffn2_w_down", "w_branch_a", "w_branch_b", "w_out")
SMALL = tuple(n for n in WEIGHTS if n not in BIG)
SMALL_ROWS = 24


def _rows1024(a):
    flat = a.reshape(-1)
    n = -(-flat.shape[0] // 1024) * 1024
    return jnp.pad(flat, (0, n - flat.shape[0])).reshape(-1, 1024)


def _pack_small(d):
    rows = jnp.concatenate([_rows1024(d[n]) for n in SMALL], axis=0)
    return jnp.pad(rows, ((0, SMALL_ROWS - rows.shape[0]), (0, 0)))


def _unpack_small(packed, like):
    out, r = {}, 0
    for n in SMALL:
        size = like[n].size
        nr = -(-size // 1024)
        out[n] = packed[r:r + nr].reshape(-1)[:size].reshape(like[n].shape)
        r += nr
    return out


def kernel(x, meta_tokens, ffn1_norm, ffn1_w_gu, ffn1_w_down, mix_norm, w_in, ssd_conv_w, ssd_conv_b, ssd_dt_bias, ssd_a_log, ssd_d, ssd_norm, hg_lower_bound, hg_norm, w_branch_a, w_branch_b, w_out, ffn2_norm, ffn2_w_gu, ffn2_w_down, final_norm, loss_target, m_meta_tokens, m_ffn1_norm, m_ffn1_w_gu, m_ffn1_w_down, m_mix_norm, m_w_in, m_ssd_conv_w, m_ssd_conv_b, m_ssd_dt_bias, m_ssd_a_log, m_ssd_d, m_ssd_norm, m_hg_lower_bound, m_hg_norm, m_w_branch_a, m_w_branch_b, m_w_out, m_ffn2_norm, m_ffn2_w_gu, m_ffn2_w_down, m_final_norm, v_meta_tokens, v_ffn1_norm, v_ffn1_w_gu, v_ffn1_w_down, v_mix_norm, v_w_in, v_ssd_conv_w, v_ssd_conv_b, v_ssd_dt_bias, v_ssd_a_log, v_ssd_d, v_ssd_norm, v_hg_lower_bound, v_hg_norm, v_w_branch_a, v_w_branch_b, v_w_out, v_ffn2_norm, v_ffn2_w_gu, v_ffn2_w_down, v_final_norm):
    P = dict(zip(WEIGHTS, (meta_tokens, ffn1_norm, ffn1_w_gu, ffn1_w_down, mix_norm, w_in, ssd_conv_w, ssd_conv_b, ssd_dt_bias, ssd_a_log, ssd_d, ssd_norm, hg_lower_bound, hg_norm, w_branch_a, w_branch_b, w_out, ffn2_norm, ffn2_w_gu, ffn2_w_down, final_norm)))
    M = dict(zip(WEIGHTS, (m_meta_tokens, m_ffn1_norm, m_ffn1_w_gu, m_ffn1_w_down, m_mix_norm, m_w_in, m_ssd_conv_w, m_ssd_conv_b, m_ssd_dt_bias, m_ssd_a_log, m_ssd_d, m_ssd_norm, m_hg_lower_bound, m_hg_norm, m_w_branch_a, m_w_branch_b, m_w_out, m_ffn2_norm, m_ffn2_w_gu, m_ffn2_w_down, m_final_norm)))
    V = dict(zip(WEIGHTS, (v_meta_tokens, v_ffn1_norm, v_ffn1_w_gu, v_ffn1_w_down, v_mix_norm, v_w_in, v_ssd_conv_w, v_ssd_conv_b, v_ssd_dt_bias, v_ssd_a_log, v_ssd_d, v_ssd_norm, v_hg_lower_bound, v_hg_norm, v_w_branch_a, v_w_branch_b, v_w_out, v_ffn2_norm, v_ffn2_w_gu, v_ffn2_w_down, v_final_norm)))
    cx, cy, cc = _place()
    q = 2 * cx + cy

    mine = jnp.concatenate([meta_tokens.reshape(4, 1024), ssd_conv_w.reshape(2, 1024), jnp.zeros((2, 1024), F32)], axis=0)
    every = _exchange8(mine, False, "gather_small")
    meta_full = jnp.concatenate([every[2 * k, 0:4].reshape(N_META, 256) for k in range(N_CHIPS)], axis=1)
    conv_w_full = jnp.concatenate([every[2 * k, 4:6].reshape(SSD_CONV, 512) for k in range(N_CHIPS)], axis=1)

    rows = jnp.concatenate([P[n][0] for n in ROW_SHARDED], axis=0).astype(BF16)
    in_slot = lambda s: lax.dynamic_update_slice(lax.empty((N_CHIPS,) + s.shape, BF16), s.astype(BF16)[None], (q, 0, 0))
    gu1, gu2, w_in_all, rows_all = _gather_big(
        [in_slot(ffn1_w_gu[0]), in_slot(ffn2_w_gu[0]), in_slot(w_in[0]), in_slot(rows)], "gather_weights")
    W = {n: P[n] for n in SMALL}
    W["meta_tokens"], W["ssd_conv_w"] = meta_full, conv_w_full
    W["ffn1_w_gu"], W["ffn2_w_gu"] = gu1, gu2
    W["w_in"] = _split_w_in(w_in_all.transpose(1, 0, 2).reshape(D_MODEL, -1))
    r = 0
    for n in ROW_SHARDED:
        nr = P[n].shape[1]
        W[n] = rows_all[:, r:r + nr].reshape(N_CHIPS * nr, D_MODEL)
        r += nr

    loss8, grad_x, G = _local_step(x, loss_target, W)

    small = jnp.concatenate(
        [G["meta_tokens"]] + [_rows1024(G[n]) for n in SMALL if n != "meta_tokens"] + [_rows1024(loss8[0:1, 0:1])], axis=0)
    small = jnp.pad(small, ((0, 40 - small.shape[0]), (0, 0)))
    small = _exchange8(small, True, "reduce_small")
    Gs = {"meta_tokens": small[0:N_META]}
    r = N_META
    for n in SMALL:
        if n == "meta_tokens":
            continue
        nr = -(-G[n].size // 1024)
        Gs[n] = small[r:r + nr].reshape(-1)[:G[n].size].reshape(G[n].shape)
        r += nr
    loss = small[r, 0]
    Gs["meta_tokens"] = lax.dynamic_slice(Gs["meta_tokens"], (0, 256 * q), (N_META, 256))
    Gs["ssd_conv_w"] = lax.dynamic_slice(Gs["ssd_conv_w"], (0, 512 * q), (SSD_CONV, 512))[None]
    Gs = {n: Gs[n].reshape(P[n].shape) for n in SMALL}

    w_in_parts = G["w_in"].reshape(D_MODEL, N_CHIPS, -1).transpose(1, 0, 2)
    row_parts = jnp.concatenate([G[n].reshape(N_CHIPS, -1, D_MODEL) for n in ROW_SHARDED], axis=1)
    g_gu1, g_gu2, g_w_in, g_rows = _reduce_to_owners([G["ffn1_w_gu"], G["ffn2_w_gu"], w_in_parts, row_parts], cc)
    Gb = {"ffn1_w_gu": g_gu1, "ffn2_w_gu": g_gu2, "w_in": g_w_in}
    r = 0
    for n in ROW_SHARDED:
        nr = P[n].shape[1]
        Gb[n] = g_rows[r:r + nr]
        r += nr

    grads, delta, new_m, new_v = dict(Gs), {}, {}, {}
    d_s, m_s, v_s = _adamw(_pack_small(P), _pack_small(Gs), _pack_small(M), _pack_small(V), "adamw_small")
    delta.update(_unpack_small(d_s, P))
    new_m.update(_unpack_small(m_s, P))
    new_v.update(_unpack_small(v_s, P))
    for n in BIG:
        d_, m_, v_ = _adamw(P[n][0], Gb[n], M[n][0], V[n][0], f"adamw_{n}")
        grads[n], delta[n], new_m[n], new_v[n] = Gb[n][None], d_[None], m_[None], v_[None]
    return (loss, grad_x, *[grads[n] for n in WEIGHTS], *[delta[n] for n in WEIGHTS],
            *[new_m[n] for n in WEIGHTS], *[new_v[n] for n in WEIGHTS])
```

```python
import functools

import jax
import jax.numpy as jnp
from jax import lax
from jax.experimental import pallas as pl
from jax.experimental.pallas import tpu as pltpu

F32 = jnp.float32
BF16 = jnp.bfloat16
HIGHEST = lax.Precision.HIGHEST
MESH = pl.DeviceIdType.MESH

D_MODEL = 1024
N_META = 16
EPS = 1e-6
SSD_HEADS = 16
SSD_HEAD_DIM = 64
SSD_INNER = 1024
SSD_GROUPS = 4
SSD_STATE = 128
SSD_CONV = 4
SSD_CONV_CH = 2048
HG_HEADS = 8
HG_SUB = 32
CHUNK = 128
D_FF = 2816
N_CHIPS = 4
IN_SIZES = (1024, 2048, 16, 1024, 1024, 1024, 1024, 1024, 1024)
ADAM_LR = 0.001
ADAM_B1 = 0.9
ADAM_B2 = 0.999
ADAM_EPS = 1e-08
ADAM_WD = 0.01
ADAM_STEP = 10
VMEM_LIMIT = 56 * 1024 * 1024
MATMUL_BLOCK_BYTES = 42 * 1024 * 1024


def _cparams(sem=None):
    return pltpu.CompilerParams(dimension_semantics=sem, vmem_limit_bytes=VMEM_LIMIT)


def _pick(n, cands):
    for c in cands:
        if n % c == 0:
            return c
    return n


def _dg(a, b, ca, cb):
    return lax.dot_general(a.astype(BF16), b.astype(BF16), (((ca,), (cb,)), ((), ())), preferred_element_type=F32)


@jax.custom_vjp
def _mm(a, b):
    return _dg(a, b, 1, 0)


def _mm_fwd(a, b):
    return _dg(a, b, 1, 0), (a, b)


def _mm_bwd(r, g):
    a, b = r
    return _dg(g, b, 1, 1), _dg(a, g, 0, 0)


_mm.defvjp(_mm_fwd, _mm_bwd)


@jax.custom_vjp
def _mm_nt(a, b):
    return _dg(a, b, 1, 1)


def _mm_nt_fwd(a, b):
    return _dg(a, b, 1, 1), (a, b)


def _mm_nt_bwd(r, g):
    a, b = r
    return _dg(g, b, 1, 0), _dg(g, a, 0, 0)


_mm_nt.defvjp(_mm_nt_fwd, _mm_nt_bwd)


@jax.custom_vjp
def _mm_tn(a, b):
    return _dg(a, b, 0, 0)


def _mm_tn_fwd(a, b):
    return _dg(a, b, 0, 0), (a, b)


def _mm_tn_bwd(r, g):
    a, b = r
    return _dg(b, g, 1, 1), _dg(a, g, 1, 0)


_mm_tn.defvjp(_mm_tn_fwd, _mm_tn_bwd)


def _silu(x):
    return x * jax.nn.sigmoid(x)


def _softplus(x):
    return jnp.maximum(x, 0.0) + jnp.log(1.0 + jnp.exp(-jnp.abs(x)))


def _tril(n):
    ri = lax.broadcasted_iota(jnp.int32, (n, n), 0)
    ci = lax.broadcasted_iota(jnp.int32, (n, n), 1)
    return ri >= ci


def _row_of(m, r):
    sub = lax.broadcasted_iota(jnp.int32, (m.shape[0], 1), 0)
    return jnp.sum(jnp.where(sub == r, m, 0.0), axis=0, keepdims=True)


def _col_of(m, c):
    lane = lax.broadcasted_iota(jnp.int32, (1, m.shape[1]), 1)
    return jnp.sum(jnp.where(lane == c, m, 0.0), axis=1, keepdims=True)


def _matmul(a, b, *, mode, out_dtype, name, alpha=1.0, res=None, tm=None, tn=None, tk=None, out_groups=None):
    b3 = b.ndim == 3
    if mode == "nn":
        M, K = a.shape
        G = b.shape[0] if b3 else 1
        Ng = b.shape[-1]
        N = G * Ng
    elif mode == "nt":
        M, K = a.shape
        G = b.shape[0] if b3 else 1
        N = b.shape[-2]
        Kg = b.shape[-1]
        assert G * Kg == K
    else:
        K, M = a.shape
        N = b.shape[1]
        G = out_groups or 1
        Ng = N // G
    if mode == "tn":
        tk = tk or K
        cands = (1408, 1024, 512, 256, 128)
        fits = [(m_ * n_, m_, n_) for m_ in cands if M % m_ == 0 for n_ in cands if Ng % n_ == 0
                if 2 * (tk * m_ * a.dtype.itemsize + tk * n_ * b.dtype.itemsize + m_ * n_ * 4) <= MATMUL_BLOCK_BYTES]
        _, tm_fit, tn_fit = max(fits)
        tm, tn = tm or tm_fit, tn or tn_fit
    else:
        tm = tm or _pick(M, (1088, 544, 256, 128))
        if mode == "nn":
            tn = tn or _pick(Ng, (1408, 512, 256, 128))
            tk = K
        else:
            tn = tn or _pick(N, (1408, 512, 256, 128))
            tk = tk or (Kg if b3 else K)
    nm, nn_, nk = M // tm, N // tn, K // tk
    assert nm * tm == M and nn_ * tn == N and nk * tk == K, (name, M, N, K, tm, tn, tk)

    if mode == "nn":
        a_spec = pl.BlockSpec((tm, tk), lambda i, j, k: (i, k))
        if b3:
            ns = Ng // tn
            b_spec = pl.BlockSpec((None, tk, tn), lambda i, j, k: (j // ns, k, j % ns))
        else:
            b_spec = pl.BlockSpec((tk, tn), lambda i, j, k: (k, j))
        ca, cb = 1, 0
    elif mode == "nt":
        a_spec = pl.BlockSpec((tm, tk), lambda i, j, k: (i, k))
        if b3:
            ks = Kg // tk
            b_spec = pl.BlockSpec((None, tn, tk), lambda i, j, k: (k // ks, j, k % ks))
        else:
            b_spec = pl.BlockSpec((tn, tk), lambda i, j, k: (j, k))
        ca, cb = 1, 1
    else:
        a_spec = pl.BlockSpec((tk, tm), lambda i, j, k: (k, i))
        b_spec = pl.BlockSpec((tk, tn), lambda i, j, k: (k, j))
        ca, cb = 0, 0
    if mode == "tn" and G > 1:
        ns = Ng // tn
        o_spec = pl.BlockSpec((None, tm, tn), lambda i, j, k: (j // ns, i, j % ns))
        out_shape = jax.ShapeDtypeStruct((G, M, Ng), out_dtype)
    else:
        o_spec = pl.BlockSpec((tm, tn), lambda i, j, k: (i, j))
        out_shape = jax.ShapeDtypeStruct((M, N), out_dtype)
    in_specs = [a_spec, b_spec]
    args = [a, b]
    if res is not None:
        in_specs.append(pl.BlockSpec((tm, tn), lambda i, j, k: (i, j)))
        args.append(res)
    has_res = res is not None

    def finish(refs, o):
        if alpha != 1.0:
            o = o * alpha
        if has_res:
            o = o + refs[2][...]
        return o

    def body_one(*refs):
        o_ref = refs[-1]
        o_ref[...] = finish(refs, _dg(refs[0][...], refs[1][...], ca, cb)).astype(o_ref.dtype)

    def body_acc(*refs):
        a_ref, b_ref = refs[0], refs[1]
        o_ref, acc_ref = refs[-2], refs[-1]
        k = pl.program_id(2)

        @pl.when(k == 0)
        def _():
            acc_ref[...] = jnp.zeros_like(acc_ref)

        acc_ref[...] += _dg(a_ref[...], b_ref[...], ca, cb)

        @pl.when(k == nk - 1)
        def _():
            o_ref[...] = finish(refs, acc_ref[...]).astype(o_ref.dtype)

    return pl.pallas_call(
        body_one if nk == 1 else body_acc, grid=(nm, nn_, nk), in_specs=in_specs, out_specs=o_spec, out_shape=out_shape,
        scratch_shapes=[] if nk == 1 else [pltpu.VMEM((tm, tn), F32)], name=name,
        compiler_params=_cparams(("parallel", "parallel", "arbitrary")),
    )(*args)


def _rms_fn(h, w):
    r = lax.rsqrt(jnp.mean(h * h, axis=-1, keepdims=True) + EPS)
    return h * r * w


def _swiglu_fn(gu):
    g = gu[:, :D_FF].astype(F32)
    u = gu[:, D_FF:].astype(F32)
    return _silu(g) * u


def _merge_fn(pa, pb, gates):
    return jax.nn.sigmoid(gates[:, :D_MODEL]) * pa + jax.nn.sigmoid(gates[:, D_MODEL:]) * pb


def _rows_call(body, *, rows, tr, ins, outs, accs=(), name):
    n = rows // tr
    assert n * tr == rows

    def spec(x):
        if isinstance(x, tuple):
            shp = x[1].shape
            return pl.BlockSpec(shp, lambda i: (0,) * len(shp))
        return pl.BlockSpec((tr, x.shape[1]), lambda i: (i, 0))

    in_specs = [spec(x) for x in ins]
    args = [x[1] if isinstance(x, tuple) else x for x in ins]
    out_specs = [spec(x) for x in outs] + [pl.BlockSpec(x.shape, lambda i: (0,) * len(x.shape)) for x in accs]
    out_shape = [x[1] if isinstance(x, tuple) else x for x in outs] + list(accs)
    return pl.pallas_call(
        body, grid=(n,), in_specs=in_specs, out_specs=out_specs, out_shape=out_shape, name=name,
        compiler_params=_cparams(("arbitrary",)),
    )(*args)


def _acc_rows(ref, val):
    @pl.when(pl.program_id(0) == 0)
    def _():
        ref[...] = jnp.zeros_like(ref)

    ref[0:1, :] += val


def _rms_fwd(h, w, name):
    def body(h_ref, w_ref, o_ref):
        o_ref[...] = _rms_fn(h_ref[...], w_ref[...]).astype(o_ref.dtype)

    R = h.shape[0]
    return _rows_call(body, rows=R, tr=_pick(R, (256, 128)), ins=[h, ("full", w)],
                      outs=[jax.ShapeDtypeStruct(h.shape, BF16)], name=name)[0]


def _rms_bwd(h, w, dn, dres, name):
    def body(h_ref, w_ref, dn_ref, dres_ref, dh_ref, dw_ref):
        _, vjp = jax.vjp(_rms_fn, h_ref[...], w_ref[...])
        dh, dw = vjp(dn_ref[...].astype(F32))
        dh_ref[...] = dh + dres_ref[...]
        _acc_rows(dw_ref, dw)

    R = h.shape[0]
    return _rows_call(body, rows=R, tr=_pick(R, (256, 128)), ins=[h, ("full", w), dn, dres],
                      outs=[jax.ShapeDtypeStruct(h.shape, F32)], accs=[jax.ShapeDtypeStruct((8, D_MODEL), F32)], name=name)


def _swiglu_fwd(gu, name):
    def body(gu_ref, o_ref):
        o_ref[...] = _swiglu_fn(gu_ref[...]).astype(o_ref.dtype)

    R = gu.shape[0]
    return _rows_call(body, rows=R, tr=_pick(R, (256, 128)), ins=[gu],
                      outs=[jax.ShapeDtypeStruct((R, D_FF), BF16)], name=name)[0]


def _swiglu_bwd(gu, da, name):
    def body(gu_ref, da_ref, o_ref):
        _, vjp = jax.vjp(_swiglu_fn, gu_ref[...].astype(F32))
        (dgu,) = vjp(da_ref[...].astype(F32))
        o_ref[...] = dgu.astype(o_ref.dtype)

    R = gu.shape[0]
    return _rows_call(body, rows=R, tr=_pick(R, (256, 128)), ins=[gu, da],
                      outs=[jax.ShapeDtypeStruct(gu.shape, BF16)], name=name)[0]


def _merge_fwd(pa, pb, gates, name):
    def body(pa_ref, pb_ref, g_ref, o_ref):
        o_ref[...] = _merge_fn(pa_ref[...], pb_ref[...], g_ref[...]).astype(o_ref.dtype)

    R = pa.shape[0]
    return _rows_call(body, rows=R, tr=_pick(R, (256, 128)), ins=[pa, pb, gates],
                      outs=[jax.ShapeDtypeStruct(pa.shape, BF16)], name=name)[0]


def _merge_bwd(pa, pb, gates, dm, name):
    def body(pa_ref, pb_ref, g_ref, dm_ref, dpa_ref, dpb_ref, dg_ref):
        _, vjp = jax.vjp(_merge_fn, pa_ref[...], pb_ref[...], g_ref[...])
        dpa, dpb, dg = vjp(dm_ref[...].astype(F32))
        dpa_ref[...] = dpa.astype(dpa_ref.dtype)
        dpb_ref[...] = dpb.astype(dpb_ref.dtype)
        dg_ref[...] = dg.astype(dg_ref.dtype)

    R = pa.shape[0]
    return _rows_call(body, rows=R, tr=_pick(R, (256, 128)), ins=[pa, pb, gates, dm],
                      outs=[jax.ShapeDtypeStruct(pa.shape, BF16), jax.ShapeDtypeStruct(pa.shape, BF16),
                            jax.ShapeDtypeStruct(gates.shape, BF16)], name=name)


def _loss_head(h3, w, target, nseq, name):
    Tp = h3.shape[0] // nseq
    nc = Tp // CHUNK

    def fn(h, w_, t, valid):
        y = _rms_fn(h, w_)
        e = (y - t) * valid
        return 0.5 * jnp.sum(jnp.mean(e * e, axis=-1, keepdims=True))

    def body(h_ref, w_ref, t_ref, loss_ref, dh_ref, dw_ref):
        b, c = pl.program_id(0), pl.program_id(1)
        valid = (c >= 1).astype(F32)
        t = t_ref[...]
        loss, vjp = jax.vjp(lambda h, w_: fn(h, w_, t, valid), h_ref[...], w_ref[...])
        dh, dw = vjp(jnp.ones((), F32))
        dh_ref[...] = dh

        @pl.when((b == 0) & (c == 0))
        def _():
            loss_ref[...] = jnp.zeros_like(loss_ref)
            dw_ref[...] = jnp.zeros_like(dw_ref)

        loss_ref[...] += jnp.full(loss_ref.shape, loss, F32)
        dw_ref[0:1, :] += dw

    return pl.pallas_call(
        body, grid=(nseq, nc),
        in_specs=[pl.BlockSpec((CHUNK, D_MODEL), lambda b, c: (b * nc + c, 0)),
                  pl.BlockSpec((1, D_MODEL), lambda b, c: (0, 0)),
                  pl.BlockSpec((None, CHUNK, D_MODEL), lambda b, c: (b, jnp.maximum(c - 1, 0), 0))],
        out_specs=[pl.BlockSpec((8, 128), lambda b, c: (0, 0)),
                   pl.BlockSpec((CHUNK, D_MODEL), lambda b, c: (b * nc + c, 0)),
                   pl.BlockSpec((8, D_MODEL), lambda b, c: (0, 0))],
        out_shape=[jax.ShapeDtypeStruct((8, 128), F32), jax.ShapeDtypeStruct(h3.shape, F32),
                   jax.ShapeDtypeStruct((8, D_MODEL), F32)],
        name=name, compiler_params=_cparams(("arbitrary", "arbitrary")),
    )(h3, w, target)


CONV_TILE = 512
CONV_HALO = 8


def _conv_fwd(xbc, w, b, pad, name):
    B, Tp, C = xbc.shape
    nch = Tp // CHUNK

    def body(x_ref, w_ref, b_ref, o_ref, xp):
        xp[0:CONV_HALO, :] = jnp.zeros((CONV_HALO, CONV_TILE), F32)
        xp[CONV_HALO:, :] = x_ref[...]
        for c in range(nch):
            acc = jnp.zeros((CHUNK, CONV_TILE), F32) + b_ref[...]
            for k in range(SSD_CONV):
                acc = acc + w_ref[k:k + 1, :] * xp[pl.ds(CONV_HALO + CHUNK * c - (SSD_CONV - 1) + k, CHUNK), :]
            row = CHUNK * c + lax.broadcasted_iota(jnp.int32, (CHUNK, 1), 0)
            o_ref[pl.ds(CHUNK * c, CHUNK), :] = jnp.where(row >= pad, _silu(acc), 0.0)

    return pl.pallas_call(
        body, grid=(B, C // CONV_TILE),
        in_specs=[pl.BlockSpec((None, Tp, CONV_TILE), lambda i, j: (i, 0, j)),
                  pl.BlockSpec((SSD_CONV, CONV_TILE), lambda i, j: (0, j)),
                  pl.BlockSpec((1, CONV_TILE), lambda i, j: (0, j))],
        out_specs=pl.BlockSpec((None, Tp, CONV_TILE), lambda i, j: (i, 0, j)),
        out_shape=jax.ShapeDtypeStruct(xbc.shape, F32),
        scratch_shapes=[pltpu.VMEM((Tp + CONV_HALO, CONV_TILE), F32)],
        name=name, compiler_params=_cparams(("arbitrary", "arbitrary")),
    )(xbc, w, b)


def _conv_bwd(xbc, w, b, dact, pad, name):
    B, Tp, C = xbc.shape
    nch = Tp // CHUNK

    def body(x_ref, w_ref, b_ref, da_ref, dx_ref, dw_ref, db_ref, xp, dp):
        bi = pl.program_id(1)
        xp[0:CONV_HALO, :] = jnp.zeros((CONV_HALO, CONV_TILE), F32)
        xp[CONV_HALO:, :] = x_ref[...]
        dp[pl.ds(Tp, CONV_HALO), :] = jnp.zeros((CONV_HALO, CONV_TILE), F32)
        dws = [jnp.zeros((1, CONV_TILE), F32) for _ in range(SSD_CONV)]
        dbs = jnp.zeros((1, CONV_TILE), F32)
        for c in range(nch):
            xs = [xp[pl.ds(CONV_HALO + CHUNK * c - (SSD_CONV - 1) + k, CHUNK), :] for k in range(SSD_CONV)]
            acc = jnp.zeros((CHUNK, CONV_TILE), F32) + b_ref[...]
            for k in range(SSD_CONV):
                acc = acc + w_ref[k:k + 1, :] * xs[k]
            row = CHUNK * c + lax.broadcasted_iota(jnp.int32, (CHUNK, 1), 0)
            sg = jax.nn.sigmoid(acc)
            dpre = jnp.where(row >= pad, da_ref[pl.ds(CHUNK * c, CHUNK), :] * (sg * (1.0 + acc * (1.0 - sg))), 0.0)
            dp[pl.ds(CHUNK * c, CHUNK), :] = dpre
            dbs = dbs + jnp.sum(dpre, axis=0, keepdims=True)
            for k in range(SSD_CONV):
                dws[k] = dws[k] + jnp.sum(dpre * xs[k], axis=0, keepdims=True)
        for c in range(nch):
            acc = jnp.zeros((CHUNK, CONV_TILE), F32)
            for k in range(SSD_CONV):
                acc = acc + w_ref[k:k + 1, :] * dp[pl.ds(CHUNK * c + (SSD_CONV - 1) - k, CHUNK), :]
            dx_ref[pl.ds(CHUNK * c, CHUNK), :] = acc

        @pl.when(bi == 0)
        def _():
            dw_ref[...] = jnp.zeros_like(dw_ref)
            db_ref[...] = jnp.zeros_like(db_ref)

        for k in range(SSD_CONV):
            dw_ref[k:k + 1, :] += dws[k]
        db_ref[0:1, :] += dbs

    return pl.pallas_call(
        body, grid=(C // CONV_TILE, B),
        in_specs=[pl.BlockSpec((None, Tp, CONV_TILE), lambda j, i: (i, 0, j)),
                  pl.BlockSpec((SSD_CONV, CONV_TILE), lambda j, i: (0, j)),
                  pl.BlockSpec((1, CONV_TILE), lambda j, i: (0, j)),
                  pl.BlockSpec((None, Tp, CONV_TILE), lambda j, i: (i, 0, j))],
        out_specs=[pl.BlockSpec((None, Tp, CONV_TILE), lambda j, i: (i, 0, j)),
                   pl.BlockSpec((8, CONV_TILE), lambda j, i: (0, j)),
                   pl.BlockSpec((8, CONV_TILE), lambda j, i: (0, j))],
        out_shape=[jax.ShapeDtypeStruct(xbc.shape, F32), jax.ShapeDtypeStruct((8, C), F32),
                   jax.ShapeDtypeStruct((8, C), F32)],
        scratch_shapes=[pltpu.VMEM((Tp + CONV_HALO, CONV_TILE), F32), pltpu.VMEM((Tp + CONV_HALO, CONV_TILE), F32)],
        name=name, compiler_params=_cparams(("arbitrary", "arbitrary")),
    )(xbc, w, b, dact)


def _ssd_chunk(xs, bm, cm, dtr, z, state, dt_bias, a_log, dskip, norm_w, valid):
    Q = xs.shape[0]
    lane = lax.broadcasted_iota(jnp.int32, (1, 128), 1)
    dt = jnp.where(lane < SSD_HEADS, _softplus(dtr + dt_bias), 0.0) * valid
    a = dt * (-jnp.exp(a_log))
    tril = _tril(Q)
    cs = jnp.dot(tril.astype(F32), a, precision=HIGHEST)
    cs_t = cs.T
    cs_end = _row_of(cs, Q - 1)
    low = lane < SSD_HEAD_DIM
    low_rows = lax.broadcasted_iota(jnp.int32, (128, 1), 0) < SSD_HEAD_DIM
    ys, new_state = [], []
    for g in range(SSD_GROUPS):
        bg = bm[:, 128 * g:128 * (g + 1)]
        cg = cm[:, 128 * g:128 * (g + 1)]
        cb = _mm_nt(cg, bg)
        for pr in range(2):
            p = 2 * g + pr
            h0, h1 = 2 * p, 2 * p + 1
            xp = xs[:, 128 * p:128 * (p + 1)]
            c0, c1 = _col_of(cs, h0), _col_of(cs, h1)
            e0, e1 = _col_of(cs_end, h0), _col_of(cs_end, h1)
            xd = xp * jnp.where(low, _col_of(dt, h0), _col_of(dt, h1))
            l0 = jnp.exp(jnp.where(tril, c0 - _row_of(cs_t, h0), -1e30))
            l1 = jnp.exp(jnp.where(tril, c1 - _row_of(cs_t, h1), -1e30))
            y_diag = jnp.where(low, _mm(cb * l0, xd), _mm(cb * l1, xd))
            to_end = jnp.where(low, jnp.exp(e0 - c0), jnp.exp(e1 - c1))
            sp = state[128 * p:128 * (p + 1), :]
            y_off = _mm_nt(cg, sp) * jnp.where(low, jnp.exp(c0), jnp.exp(c1))
            new_state.append(sp * jnp.where(low_rows, jnp.exp(e0), jnp.exp(e1)) + _mm_tn(xd * to_end, bg))
            ys.append(y_diag + y_off + xp * jnp.where(low, _col_of(dskip, h0), _col_of(dskip, h1)))
    y = jnp.concatenate(ys, axis=1) * _silu(z)
    gw = SSD_INNER // SSD_GROUPS
    outs = []
    for g in range(SSD_GROUPS):
        blk = y[:, gw * g:gw * (g + 1)]
        outs.append(blk * lax.rsqrt(jnp.mean(blk * blk, axis=-1, keepdims=True) + EPS))
    return jnp.concatenate(outs, axis=1) * norm_w, jnp.concatenate(new_state, axis=0)


def _valid_rows(c, pad):
    row = c * CHUNK + lax.broadcasted_iota(jnp.int32, (CHUNK, 1), 0)
    return (row >= pad).astype(F32)


def _ssd_fwd(xact, dtr, z, dt_bias, a_log, dskip, norm_w, pad, name):
    B, Tp, _ = xact.shape
    nc = Tp // CHUNK

    def body(xs_ref, bm_ref, cm_ref, dt_ref, z_ref, db_ref, al_ref, ds_ref, nw_ref, y_ref, save_ref, st):
        c = pl.program_id(1)

        @pl.when(c == 0)
        def _():
            st[...] = jnp.zeros_like(st)

        s0 = st[...]
        save_ref[...] = s0
        y, s1 = _ssd_chunk(xs_ref[...], bm_ref[...], cm_ref[...], dt_ref[...], z_ref[...], s0, db_ref[...],
                           al_ref[...], ds_ref[...], nw_ref[...], _valid_rows(c, pad))
        y_ref[...] = y.astype(y_ref.dtype)
        st[...] = s1

    row = lambda w, off=0: pl.BlockSpec((None, CHUNK, w), lambda b, c: (b, c, off))
    par = lambda w: pl.BlockSpec((1, w), lambda b, c: (0, 0))
    return pl.pallas_call(
        body, grid=(B, nc),
        in_specs=[row(1024, 0), row(512, 2), row(512, 3), row(128), row(1024), par(128), par(128), par(128), par(1024)],
        out_specs=[row(1024), pl.BlockSpec((None, None, 1024, 128), lambda b, c: (b, c, 0, 0))],
        out_shape=[jax.ShapeDtypeStruct((B, Tp, SSD_INNER), BF16), jax.ShapeDtypeStruct((B, nc, 1024, 128), F32)],
        scratch_shapes=[pltpu.VMEM((1024, 128), F32)],
        name=name, compiler_params=_cparams(("arbitrary", "arbitrary")),
    )(xact, xact, xact, dtr, z, dt_bias, a_log, dskip, norm_w)


def _ssd_bwd(xact, dtr, z, dt_bias, a_log, dskip, norm_w, saved, dy, pad, name):
    B, Tp, _ = xact.shape
    nc = Tp // CHUNK

    def body(xs_ref, bm_ref, cm_ref, dt_ref, z_ref, db_ref, al_ref, ds_ref, nw_ref, sv_ref, dy_ref,
             dx_ref, ddt_ref, dz_ref, dpar_ref, dnw_ref, dst):
        b, i = pl.program_id(0), pl.program_id(1)
        c = nc - 1 - i

        @pl.when(i == 0)
        def _():
            dst[...] = jnp.zeros_like(dst)

        valid = _valid_rows(c, pad)
        fn = lambda *a: _ssd_chunk(*a, valid)
        _, vjp = jax.vjp(fn, xs_ref[...], bm_ref[...], cm_ref[...], dt_ref[...], z_ref[...], sv_ref[...],
                         db_ref[...], al_ref[...], ds_ref[...], nw_ref[...])
        dxs, dbm, dcm, ddt, dz, dstate, ddb, dal, dds, dnw = vjp((dy_ref[...].astype(F32), dst[...]))
        dx_ref[:, 0:1024] = dxs
        dx_ref[:, 1024:1536] = dbm
        dx_ref[:, 1536:2048] = dcm
        ddt_ref[...] = ddt
        dz_ref[...] = dz
        dst[...] = dstate

        @pl.when((b == 0) & (i == 0))
        def _():
            dpar_ref[...] = jnp.zeros_like(dpar_ref)
            dnw_ref[...] = jnp.zeros_like(dnw_ref)

        dpar_ref[0:1, :] += ddb
        dpar_ref[1:2, :] += dal
        dpar_ref[2:3, :] += dds
        dnw_ref[0:1, :] += dnw

    row = lambda w, off=0: pl.BlockSpec((None, CHUNK, w), lambda b, i: (b, nc - 1 - i, off))
    par = lambda w: pl.BlockSpec((1, w), lambda b, i: (0, 0))
    acc = lambda w: pl.BlockSpec((8, w), lambda b, i: (0, 0))
    outs = pl.pallas_call(
        body, grid=(B, nc),
        in_specs=[row(1024, 0), row(512, 2), row(512, 3), row(128), row(1024), par(128), par(128), par(128), par(1024),
                  pl.BlockSpec((None, None, 1024, 128), lambda b, i: (b, nc - 1 - i, 0, 0)), row(1024)],
        out_specs=[row(2048), row(128), row(1024), acc(128), acc(1024)],
        out_shape=[jax.ShapeDtypeStruct((B, Tp, 2048), F32), jax.ShapeDtypeStruct((B, Tp, 128), F32),
                   jax.ShapeDtypeStruct((B, Tp, 1024), F32), jax.ShapeDtypeStruct((8, 128), F32),
                   jax.ShapeDtypeStruct((8, 1024), F32)],
        scratch_shapes=[pltpu.VMEM((1024, 128), F32)],
        name=name, compiler_params=_cparams(("arbitrary", "arbitrary")),
    )(xact, xact, xact, dtr, z, dt_bias, a_log, dskip, norm_w, saved, dy)
    return outs


def _hg_chunk(qr, fr, ir, gr, state_t, p0, p1, norm_w, valid):
    Q = qr.shape[0]
    lb = jax.nn.sigmoid(p0 - p1)
    f = lb + (1.0 - lb) * jax.nn.sigmoid(fr)
    k = 1.0 - f
    q = _silu(qr)
    v = ir * valid
    cum = jnp.dot(_tril(Q).astype(F32), jnp.log(f), precision=HIGHEST)
    cum_end = _row_of(cum, Q - 1)
    o_inter = _mm_nt(q * jnp.exp(cum), state_t)
    nblk = Q // HG_SUB
    row = lax.broadcasted_iota(jnp.int32, (Q, 1), 0)
    ri = lax.broadcasted_iota(jnp.int32, (Q, Q), 0)
    ci = lax.broadcasted_iota(jnp.int32, (Q, Q), 1)
    mids = jnp.concatenate([jnp.broadcast_to(_row_of(cum, HG_SUB * i + HG_SUB // 2 - 1), (HG_SUB, cum.shape[1]))
                            for i in range(nblk)], axis=0)
    sh = HG_SUB.bit_length() - 1
    same = (jnp.right_shift(ri, sh) == jnp.right_shift(ci, sh)) & (ri >= ci)
    att = jnp.where(same, _mm_nt(q * jnp.exp(cum - mids), k * jnp.exp(mids - cum)), 0.0)
    for i in range(1, nblk):
        lo = HG_SUB * i
        start = _row_of(cum, lo - 1)
        qa = q * jnp.exp(jnp.where((row >= lo) & (row < lo + HG_SUB), cum - start, -1e30))
        ka = k * jnp.exp(jnp.where(row < lo, start - cum, -1e30))
        att = att + _mm_nt(qa, ka)
    o = o_inter + _mm(att, v)
    new_state_t = state_t * jnp.exp(cum_end) + _mm_tn(v, k * jnp.exp(cum_end - cum))
    o = o * lax.rsqrt(jnp.mean(o * o, axis=-1, keepdims=True) + EPS) * norm_w
    return o * _silu(gr), new_state_t


HG_PER_STEP = 4
HG_COLS = 4 * 128


def _hg_fwd(qfig, lbh, nwh, pad, name):
    B, Tp, _ = qfig.shape
    nc = Tp // CHUNK
    hp = HG_PER_STEP

    def body(x_ref, lb_ref, nw_ref, y_ref, save_ref, st):
        c = pl.program_id(1)

        @pl.when(c == 0)
        def _():
            st[...] = jnp.zeros_like(st)

        valid = _valid_rows(c, pad)
        for j in range(hp):
            for b in range(B):
                s0 = st[j, b]
                save_ref[j, b] = s0
                col = lambda k: x_ref[b, :, HG_COLS * j + 128 * k:HG_COLS * j + 128 * (k + 1)]
                y, s1 = _hg_chunk(col(0), col(1), col(2), col(3), s0, lb_ref[j, 0:1, :], lb_ref[j, 1:2, :], nw_ref[j], valid)
                y_ref[b, :, 128 * j:128 * (j + 1)] = y.astype(y_ref.dtype)
                st[j, b] = s1

    return pl.pallas_call(
        body, grid=(HG_HEADS // hp, nc),
        in_specs=[pl.BlockSpec((B, CHUNK, HG_COLS * hp), lambda h, c: (0, c, h)),
                  pl.BlockSpec((hp, 2, 128), lambda h, c: (h, 0, 0)),
                  pl.BlockSpec((hp, 1, 128), lambda h, c: (h, 0, 0))],
        out_specs=[pl.BlockSpec((B, CHUNK, 128 * hp), lambda h, c: (0, c, h)),
                   pl.BlockSpec((hp, B, None, 128, 128), lambda h, c: (h, 0, c, 0, 0))],
        out_shape=[jax.ShapeDtypeStruct((B, Tp, 1024), BF16), jax.ShapeDtypeStruct((HG_HEADS, B, nc, 128, 128), F32)],
        scratch_shapes=[pltpu.VMEM((hp, B, 128, 128), F32)],
        name=name, compiler_params=_cparams(("arbitrary", "arbitrary")),
    )(qfig, lbh, nwh)


def _hg_bwd(qfig, lbh, nwh, saved, dy, pad, name):
    B, Tp, _ = qfig.shape
    nc = Tp // CHUNK
    hp = HG_PER_STEP

    def body(x_ref, lb_ref, nw_ref, sv_ref, dy_ref, dx_ref, dlb_ref, dnw_ref, dst):
        i = pl.program_id(1)
        c = nc - 1 - i

        @pl.when(i == 0)
        def _():
            dst[...] = jnp.zeros_like(dst)
            dlb_ref[...] = jnp.zeros_like(dlb_ref)
            dnw_ref[...] = jnp.zeros_like(dnw_ref)

        valid = _valid_rows(c, pad)
        fn = lambda *a: _hg_chunk(*a, valid)
        for j in range(hp):
            for b in range(B):
                col = lambda k: x_ref[b, :, HG_COLS * j + 128 * k:HG_COLS * j + 128 * (k + 1)]
                _, vjp = jax.vjp(fn, col(0), col(1), col(2), col(3), sv_ref[j, b], lb_ref[j, 0:1, :], lb_ref[j, 1:2, :], nw_ref[j])
                d4 = vjp((dy_ref[b, :, 128 * j:128 * (j + 1)].astype(F32), dst[j, b]))
                for k in range(4):
                    dx_ref[b, :, HG_COLS * j + 128 * k:HG_COLS * j + 128 * (k + 1)] = d4[k].astype(dx_ref.dtype)
                dst[j, b] = d4[4]
                dlb_ref[j, 0:1, :] += d4[5]
                dlb_ref[j, 1:2, :] += d4[6]
                dnw_ref[j, 0:1, :] += d4[7]

    acc = pl.BlockSpec((hp, 8, 128), lambda h, i: (h, 0, 0))
    return pl.pallas_call(
        body, grid=(HG_HEADS // hp, nc),
        in_specs=[pl.BlockSpec((B, CHUNK, HG_COLS * hp), lambda h, i: (0, nc - 1 - i, h)),
                  pl.BlockSpec((hp, 2, 128), lambda h, i: (h, 0, 0)),
                  pl.BlockSpec((hp, 1, 128), lambda h, i: (h, 0, 0)),
                  pl.BlockSpec((hp, B, None, 128, 128), lambda h, i: (h, 0, nc - 1 - i, 0, 0)),
                  pl.BlockSpec((B, CHUNK, 128 * hp), lambda h, i: (0, nc - 1 - i, h))],
        out_specs=[pl.BlockSpec((B, CHUNK, HG_COLS * hp), lambda h, i: (0, nc - 1 - i, h)), acc, acc],
        out_shape=[jax.ShapeDtypeStruct((B, Tp, 4096), BF16), jax.ShapeDtypeStruct((HG_HEADS, 8, 128), F32),
                   jax.ShapeDtypeStruct((HG_HEADS, 8, 128), F32)],
        scratch_shapes=[pltpu.VMEM((hp, B, 128, 128), F32)],
        name=name, compiler_params=_cparams(("arbitrary", "arbitrary")),
    )(qfig, lbh, nwh, saved, dy)


def _adamw(w, g, m, v, name):
    R, C = w.shape
    tr = _pick(R, (256, 176, 128, 64, 8)) if R > 256 else R

    def body(w_ref, g_ref, m_ref, v_ref, d_ref, mo_ref, vo_ref):
        g_ = g_ref[...]
        m_ = ADAM_B1 * m_ref[...] + (1.0 - ADAM_B1) * g_
        v_ = ADAM_B2 * v_ref[...] + (1.0 - ADAM_B2) * (g_ * g_)
        m_hat = m_ / (1.0 - ADAM_B1 ** ADAM_STEP)
        v_hat = v_ / (1.0 - ADAM_B2 ** ADAM_STEP)
        d_ref[...] = -ADAM_LR * (m_hat / (jnp.sqrt(v_hat) + ADAM_EPS) + ADAM_WD * w_ref[...])
        mo_ref[...] = m_
        vo_ref[...] = v_

    sp = pl.BlockSpec((tr, C), lambda i: (i, 0))
    sh = jax.ShapeDtypeStruct((R, C), F32)
    return pl.pallas_call(body, grid=(R // tr,), in_specs=[sp] * 4, out_specs=[sp] * 3, out_shape=[sh] * 3,
                          name=name, compiler_params=_cparams(("arbitrary",)))(w, g, m, v)


def _ffn_fwd(h, norm_w, w_gu, w_down, tag):
    n = _rms_fwd(h, norm_w, f"{tag}_norm")
    gu = _matmul(n, w_gu, mode="nn", out_dtype=BF16, name=f"{tag}_gu")
    a = _swiglu_fwd(gu, f"{tag}_act")
    out = _matmul(a, w_down, mode="nn", out_dtype=F32, alpha=0.5, res=h, name=f"{tag}_down")
    return out, (n, gu, a)


def _ffn_bwd(h, norm_w, w_gu, w_down, saved, dout, tag):
    n, gu, a = saved
    da = _matmul(dout, w_down, mode="nt", out_dtype=BF16, alpha=0.5, name=f"{tag}_d_act")
    dw_down = _matmul(a, dout, mode="tn", out_dtype=F32, alpha=0.5, name=f"{tag}_dw_down")
    dgu = _swiglu_bwd(gu, da, f"{tag}_d_gu")
    dn = _matmul(dgu, w_gu, mode="nt", out_dtype=F32, name=f"{tag}_d_norm")
    dw_gu = _matmul(n, dgu, mode="tn", out_dtype=F32, out_groups=N_CHIPS, name=f"{tag}_dw_gu")
    dh, dnw = _rms_bwd(h, norm_w, dn, dout, f"{tag}_d_in")
    return dh, dnw, dw_gu, dw_down


IN_NAMES = ("z", "xbc", "dt", "q", "f", "i", "g", "gates")


def _split_w_in(w_in_full):
    pts = [0]
    for s in IN_SIZES:
        pts.append(pts[-1] + s)
    sl = lambda i, j: w_in_full[:, pts[i]:pts[j]]
    qfig = sl(3, 7).reshape(D_MODEL, 4, HG_HEADS, 128).transpose(0, 2, 1, 3).reshape(D_MODEL, 4 * D_MODEL)
    return {"z": sl(0, 1), "xbc": sl(1, 2), "dt": jnp.pad(sl(2, 3), ((0, 0), (0, 128 - SSD_HEADS))),
            "qfig": qfig, "gates": sl(7, 9)}


def _local_step(x, target, W):
    B, S, _ = x.shape
    T = N_META + S
    pad = (-T) % CHUNK
    Tp = T + pad
    assert pad + N_META == CHUNK
    R = B * Tp
    meta = jnp.broadcast_to(W["meta_tokens"][None], (B, N_META, D_MODEL))
    h0 = jnp.concatenate([jnp.zeros((B, pad, D_MODEL), F32), meta, x], axis=1).reshape(R, D_MODEL)

    h1, sv1 = _ffn_fwd(h0, W["ffn1_norm"], W["ffn1_w_gu"], W["ffn1_w_down"], "ffn1")
    um = _rms_fwd(h1, W["mix_norm"], "mix_norm")
    wi = W["w_in"]
    z = _matmul(um, wi["z"], mode="nn", out_dtype=F32, name="in_z")
    xbc = _matmul(um, wi["xbc"], mode="nn", out_dtype=F32, name="in_xbc")
    dtr = _matmul(um, wi["dt"], mode="nn", out_dtype=F32, name="in_dt")
    qfig = _matmul(um, wi["qfig"], mode="nn", out_dtype=F32, name="in_qfig")
    gates = _matmul(um, wi["gates"], mode="nn", out_dtype=F32, name="in_gates")

    r3 = lambda t: t.reshape(B, Tp, t.shape[-1])
    lane_pad = lambda t: jnp.pad(t, ((0, 0), (0, 128 - t.shape[1])))
    dt_bias, a_log, dskip = lane_pad(W["ssd_dt_bias"]), lane_pad(W["ssd_a_log"]), lane_pad(W["ssd_d"])
    xact = _conv_fwd(r3(xbc), W["ssd_conv_w"], W["ssd_conv_b"], pad, "conv_fwd")
    ya, ssd_saved = _ssd_fwd(xact, r3(dtr), r3(z), dt_bias, a_log, dskip, W["ssd_norm"], pad, "ssd_fwd")
    lbh = W["hg_lower_bound"].reshape(2, HG_HEADS, 128).transpose(1, 0, 2)
    nwh = W["hg_norm"].reshape(HG_HEADS, 1, 128)
    yb, hg_saved = _hg_fwd(r3(qfig), lbh, nwh, pad, "hg_fwd")
    ya2, yb2 = ya.reshape(R, -1), yb.reshape(R, -1)
    pa = _matmul(ya2, W["w_branch_a"], mode="nn", out_dtype=F32, name="branch_a")
    pb = _matmul(yb2, W["w_branch_b"], mode="nn", out_dtype=F32, name="branch_b")
    mg = _merge_fwd(pa, pb, gates, "merge")
    h2 = _matmul(mg, W["w_out"], mode="nn", out_dtype=F32, res=h1, name="mix_out")
    h3, sv2 = _ffn_fwd(h2, W["ffn2_norm"], W["ffn2_w_gu"], W["ffn2_w_down"], "ffn2")

    loss, dh3, d_final = _loss_head(h3, W["final_norm"].reshape(1, D_MODEL), target, B, "loss_head")

    G = {"final_norm": d_final[0]}
    dh2, dnw, G["ffn2_w_gu"], G["ffn2_w_down"] = _ffn_bwd(h2, W["ffn2_norm"], W["ffn2_w_gu"], W["ffn2_w_down"], sv2, dh3, "ffn2")
    G["ffn2_norm"] = dnw[0:1]
    dmg = _matmul(dh2, W["w_out"], mode="nt", out_dtype=BF16, name="d_merge")
    G["w_out"] = _matmul(mg, dh2, mode="tn", out_dtype=F32, name="dw_out")
    dpa, dpb, dgates = _merge_bwd(pa, pb, gates, dmg, "merge_bwd")
    dya = _matmul(dpa, W["w_branch_a"], mode="nt", out_dtype=BF16, name="d_ya")
    dyb = _matmul(dpb, W["w_branch_b"], mode="nt", out_dtype=BF16, name="d_yb")
    G["w_branch_a"] = _matmul(ya2, dpa, mode="tn", out_dtype=F32, name="dw_branch_a")
    G["w_branch_b"] = _matmul(yb2, dpb, mode="tn", out_dtype=F32, name="dw_branch_b")

    dxact, ddtr, dz, dpar, dnw = _ssd_bwd(xact, r3(dtr), r3(z), dt_bias, a_log, dskip, W["ssd_norm"], ssd_saved,
                                          r3(dya), pad, "ssd_bwd")
    G["ssd_dt_bias"], G["ssd_a_log"], G["ssd_d"] = dpar[0:1, :SSD_HEADS], dpar[1:2, :SSD_HEADS], dpar[2:3, :SSD_HEADS]
    G["ssd_norm"] = dnw[0:1]
    dxbc, dcw, dcb = _conv_bwd(r3(xbc), W["ssd_conv_w"], W["ssd_conv_b"], dxact, pad, "conv_bwd")
    G["ssd_conv_w"], G["ssd_conv_b"] = dcw[0:SSD_CONV], dcb[0:1]
    dqfig, dlb, dhn = _hg_bwd(r3(qfig), lbh, nwh, hg_saved, r3(dyb), pad, "hg_bwd")
    G["hg_lower_bound"] = dlb[:, 0:2, :].transpose(1, 0, 2).reshape(2, D_MODEL)
    G["hg_norm"] = dhn[:, 0, :].reshape(1, D_MODEL)

    r2 = lambda t: t.reshape(R, t.shape[-1])
    pieces = [("z", r2(dz)), ("xbc", r2(dxbc)), ("dt", r2(ddtr)), ("qfig", r2(dqfig)), ("gates", dgates)]
    dum = None
    dwi = {}
    for nm, dpiece in pieces:
        dum = _matmul(dpiece, wi[nm], mode="nt", out_dtype=F32, res=dum, name=f"d_mix_{nm}")
        dwi[nm] = _matmul(um, dpiece, mode="tn", out_dtype=F32, name=f"dw_in_{nm}")
    dw_qfig = dwi["qfig"].reshape(D_MODEL, HG_HEADS, 4, 128).transpose(0, 2, 1, 3).reshape(D_MODEL, 4 * D_MODEL)
    G["w_in"] = jnp.concatenate([dwi["z"], dwi["xbc"], dwi["dt"][:, :SSD_HEADS], dw_qfig, dwi["gates"]], axis=1)
    dh1, dnw = _rms_bwd(h1, W["mix_norm"], dum, dh2, "mix_norm_bwd")
    G["mix_norm"] = dnw[0:1]
    dh0, dnw, G["ffn1_w_gu"], G["ffn1_w_down"] = _ffn_bwd(h0, W["ffn1_norm"], W["ffn1_w_gu"], W["ffn1_w_down"], sv1, dh1, "ffn1")
    G["ffn1_norm"] = dnw[0:1]
    dh0 = dh0.reshape(B, Tp, D_MODEL)
    G["meta_tokens"] = jnp.sum(dh0[:, pad:CHUNK], axis=0)
    return loss, dh0[:, CHUNK:], G


ANY = pl.BlockSpec(memory_space=pl.ANY)


def _place():
    return lax.axis_index("x"), lax.axis_index("y"), lax.axis_index("c")


def _other_chips(x, y):
    return [(1 - x, y), (x, 1 - y), (1 - x, 1 - y)]


def _remote(src, dst, ssem, rsem, dev):
    return pltpu.make_async_remote_copy(src_ref=src, dst_ref=dst, send_sem=ssem, recv_sem=rsem,
                                        device_id=dev, device_id_type=MESH)


def _exchange8(buf, reduce, name):
    n, w = buf.shape

    def body(x_ref, *rest):
        if reduce:
            red_ref, out_ref, ssem, rsem = rest
        else:
            out_ref, ssem, rsem = rest
        x, y, c = _place()
        me = 4 * x + 2 * y + c
        out_ref[me] = x_ref[...]
        copies = []
        for k in range(1, 8):
            px = 1 - x if (k >> 2) & 1 else x
            py = 1 - y if (k >> 1) & 1 else y
            pc = 1 - c if k & 1 else c
            cp = _remote(x_ref, out_ref.at[me], ssem.at[k - 1], rsem.at[k - 1], (px, py, pc))
            cp.start()
            copies.append((cp, 4 * px + 2 * py + pc))
        for k, (cp, peer) in enumerate(copies):
            _remote(x_ref, out_ref.at[peer], ssem.at[k], rsem.at[k], (x, y, c)).wait_recv()
        for cp, _ in copies:
            cp.wait_send()
        if reduce:
            acc = out_ref[0]
            for d in range(1, 8):
                acc = acc + out_ref[d]
            red_ref[...] = acc

    vm = pl.BlockSpec(memory_space=pltpu.VMEM)
    g_shape = jax.ShapeDtypeStruct((8, n, w), F32)
    if reduce:
        out_shape, out_specs, scratch = [jax.ShapeDtypeStruct((n, w), F32)], [vm], [pltpu.VMEM((8, n, w), F32)]
    else:
        out_shape, out_specs, scratch = [g_shape], [vm], []
    return pl.pallas_call(
        body, in_specs=[vm], out_specs=out_specs, out_shape=out_shape,
        scratch_shapes=scratch + [pltpu.SemaphoreType.DMA((7,)), pltpu.SemaphoreType.DMA((7,))], name=name,
    )(buf)[0]


def _gather_big(blocks, name):
    n = len(blocks)
    half = [s.shape[1] // 2 for s in blocks]

    def body(*refs):
        full = refs[n:2 * n]
        ssem, rsem, fssem, frsem = refs[2 * n:]
        x, y, c = _place()
        q = 2 * x + y
        chips = _other_chips(x, y)
        piece = lambda s, qq, cc: full[s].at[qq, pl.ds(cc * half[s], half[s])]
        sends = []
        for j, (px, py) in enumerate(chips):
            for s in range(n):
                cp = _remote(piece(s, q, c), piece(s, q, c), ssem.at[s, j], rsem.at[s, j], (px, py, c))
                cp.start()
                sends.append(cp)
        for j, (px, py) in enumerate(chips):
            for s in range(n):
                got = piece(s, 2 * px + py, c)
                _remote(got, got, ssem.at[s, j], rsem.at[s, j], (px, py, c)).wait_recv()
                cp = _remote(got, got, fssem.at[s, j], frsem.at[s, j], (x, y, 1 - c))
                cp.start()
                sends.append(cp)
        for j, (px, py) in enumerate(chips):
            for s in range(n):
                got = piece(s, 2 * px + py, 1 - c)
                _remote(got, got, fssem.at[s, j], frsem.at[s, j], (x, y, 1 - c)).wait_recv()
        for cp in sends:
            cp.wait_send()

    return pl.pallas_call(
        body, in_specs=[ANY] * n, out_specs=[ANY] * n,
        out_shape=[jax.ShapeDtypeStruct(s.shape, s.dtype) for s in blocks],
        input_output_aliases={s: s for s in range(n)},
        scratch_shapes=[pltpu.SemaphoreType.DMA((n, 3))] * 4, name=name,
    )(*blocks)


def _pair_swap(parts, name):
    n = len(parts)
    half = [p.shape[1] // 2 for p in parts]

    def body(*refs):
        src, got = refs[:n], refs[n:2 * n]
        ssem, rsem = refs[2 * n:]
        x, y, c = _place()
        copies = []
        for s in range(n):
            cp = _remote(src[s].at[pl.ds(0, N_CHIPS), pl.ds((1 - c) * half[s], half[s])], got[s], ssem.at[s], rsem.at[s], (x, y, 1 - c))
            cp.start()
            copies.append(cp)
        for cp in copies:
            cp.wait_recv()
        for cp in copies:
            cp.wait_send()

    return pl.pallas_call(
        body, in_specs=[ANY] * n, out_specs=[ANY] * n,
        out_shape=[jax.ShapeDtypeStruct((N_CHIPS, h, p.shape[2]), p.dtype) for p, h in zip(parts, half)],
        scratch_shapes=[pltpu.SemaphoreType.DMA((n,)), pltpu.SemaphoreType.DMA((n,))], name=name,
    )(*parts)


def _to_owners(sums, name):
    n = len(sums)

    def body(*refs):
        src, got = refs[:n], refs[n:2 * n]
        lsem, ssem, rsem = refs[2 * n:]
        x, y, c = _place()
        q = 2 * x + y
        chips = _other_chips(x, y)
        started, sends = [], []
        for s in range(n):
            cp = pltpu.make_async_copy(src[s].at[q], got[s].at[q], lsem.at[s])
            cp.start()
            started.append(cp)
        for j, (px, py) in enumerate(chips):
            for s in range(n):
                cp = _remote(src[s].at[2 * px + py], got[s].at[q], ssem.at[s, j], rsem.at[s, j], (px, py, c))
                cp.start()
                sends.append(cp)
        for j, (px, py) in enumerate(chips):
            for s in range(n):
                slot = got[s].at[2 * px + py]
                _remote(slot, slot, ssem.at[s, j], rsem.at[s, j], (px, py, c)).wait_recv()
        for cp in sends:
            cp.wait_send()
        for cp in started:
            cp.wait()

    return pl.pallas_call(
        body, in_specs=[ANY] * n, out_specs=[ANY] * n,
        out_shape=[jax.ShapeDtypeStruct(s.shape, s.dtype) for s in sums],
        scratch_shapes=[pltpu.SemaphoreType.DMA((n,)), pltpu.SemaphoreType.DMA((n, 3)), pltpu.SemaphoreType.DMA((n, 3))],
        name=name,
    )(*sums)


def _pair_join(blocks, name):
    n = len(blocks)

    def body(*refs):
        out = refs[n:2 * n]
        ssem, rsem = refs[2 * n:]
        x, y, c = _place()
        sends = []
        for s in range(n):
            h = blocks[s].shape[0] // 2
            mine = out[s].at[pl.ds(c * h, h)]
            cp = _remote(mine, mine, ssem.at[s], rsem.at[s], (x, y, 1 - c))
            cp.start()
            sends.append(cp)
        for s in range(n):
            h = blocks[s].shape[0] // 2
            theirs = out[s].at[pl.ds((1 - c) * h, h)]
            _remote(theirs, theirs, ssem.at[s], rsem.at[s], (x, y, 1 - c)).wait_recv()
        for cp in sends:
            cp.wait_send()

    return pl.pallas_call(
        body, in_specs=[ANY] * n, out_specs=[ANY] * n,
        out_shape=[jax.ShapeDtypeStruct(b.shape, b.dtype) for b in blocks],
        input_output_aliases={s: s for s in range(n)},
        scratch_shapes=[pltpu.SemaphoreType.DMA((n,))] * 2, name=name,
    )(*blocks)


WIRE = BF16


def _row_tile(h):
    return _pick(h, (256, 272, 128, 16))


def _add_pair(part, got, c, name):
    _, h, w = got.shape
    tr = _row_tile(h)
    nt = h // tr

    def body(c_ref, p_ref, g_ref, o_ref):
        o_ref[...] = (p_ref[...] + g_ref[...].astype(F32)).astype(o_ref.dtype)

    return pl.pallas_call(
        body,
        grid_spec=pltpu.PrefetchScalarGridSpec(
            num_scalar_prefetch=1, grid=(N_CHIPS, nt),
            in_specs=[pl.BlockSpec((None, tr, w), lambda q, i, c_ref: (q, c_ref[0] * nt + i, 0)),
                      pl.BlockSpec((None, tr, w), lambda q, i, c_ref: (q, i, 0))],
            out_specs=pl.BlockSpec((None, tr, w), lambda q, i, c_ref: (q, i, 0))),
        out_shape=jax.ShapeDtypeStruct(got.shape, WIRE), name=name,
        compiler_params=_cparams(("arbitrary", "arbitrary")),
    )(c.reshape(1).astype(jnp.int32), part, got)


def _sum_chips(slots, c, name):
    _, h, w = slots.shape
    tr = _row_tile(h)
    nt = h // tr

    def body(c_ref, s_ref, o_ref):
        o_ref[...] = ((s_ref[0].astype(F32) + s_ref[1].astype(F32)) + s_ref[2].astype(F32)) + s_ref[3].astype(F32)

    return pl.pallas_call(
        body,
        grid_spec=pltpu.PrefetchScalarGridSpec(
            num_scalar_prefetch=1, grid=(nt,),
            in_specs=[pl.BlockSpec((N_CHIPS, tr, w), lambda i, c_ref: (0, i, 0))],
            out_specs=pl.BlockSpec((tr, w), lambda i, c_ref: (c_ref[0] * nt + i, 0))),
        out_shape=jax.ShapeDtypeStruct((2 * h, w), F32), name=name,
        compiler_params=_cparams(("arbitrary",)),
    )(c.reshape(1).astype(jnp.int32), slots)


def _reduce_to_owners(parts, c):
    got = _pair_swap(parts, "grad_pair_swap")
    sums = [_add_pair(p, g, c, f"grad_pair_add{i}") for i, (p, g) in enumerate(zip(parts, got))]
    slots = _to_owners(sums, "grad_to_owners")
    blocks = [_sum_chips(s, c, f"grad_sum_chips{i}") for i, s in enumerate(slots)]
    return _pair_join(blocks, "grad_pair_join")


WEIGHTS = ("meta_tokens", "ffn1_norm", "ffn1_w_gu", "ffn1_w_down", "mix_norm", "w_in", "ssd_conv_w", "ssd_conv_b",
           "ssd_dt_bias", "ssd_a_log", "ssd_d", "ssd_norm", "hg_lower_bound", "hg_norm", "w_branch_a", "w_branch_b",
           "w_out", "ffn2_norm", "ffn2_w_gu", "ffn2_w_down", "final_norm")
BIG = ("ffn1_w_gu", "ffn1_w_down", "w_in", "w_branch_a", "w_branch_b", "w_out", "ffn2_w_gu", "ffn2_w_down")
ROW_SHARDED = ("ffn1_w_down", "ffn2_w_down", "w_branch_a", "w_branch_b", "w_out")
SMALL = tuple(n for n in WEIGHTS if n not in BIG)
SMALL_ROWS = 24


def _rows1024(a):
    flat = a.reshape(-1)
    n = -(-flat.shape[0] // 1024) * 1024
    return jnp.pad(flat, (0, n - flat.shape[0])).reshape(-1, 1024)


def _pack_small(d):
    rows = jnp.concatenate([_rows1024(d[n]) for n in SMALL], axis=0)
    return jnp.pad(rows, ((0, SMALL_ROWS - rows.shape[0]), (0, 0)))


def _unpack_small(packed, like):
    out, r = {}, 0
    for n in SMALL:
        size = like[n].size
        nr = -(-size // 1024)
        out[n] = packed[r:r + nr].reshape(-1)[:size].reshape(like[n].shape)
        r += nr
    return out


def kernel(x, meta_tokens, ffn1_norm, ffn1_w_gu, ffn1_w_down, mix_norm, w_in, ssd_conv_w, ssd_conv_b, ssd_dt_bias, ssd_a_log, ssd_d, ssd_norm, hg_lower_bound, hg_norm, w_branch_a, w_branch_b, w_out, ffn2_norm, ffn2_w_gu, ffn2_w_down, final_norm, loss_target, m_meta_tokens, m_ffn1_norm, m_ffn1_w_gu, m_ffn1_w_down, m_mix_norm, m_w_in, m_ssd_conv_w, m_ssd_conv_b, m_ssd_dt_bias, m_ssd_a_log, m_ssd_d, m_ssd_norm, m_hg_lower_bound, m_hg_norm, m_w_branch_a, m_w_branch_b, m_w_out, m_ffn2_norm, m_ffn2_w_gu, m_ffn2_w_down, m_final_norm, v_meta_tokens, v_ffn1_norm, v_ffn1_w_gu, v_ffn1_w_down, v_mix_norm, v_w_in, v_ssd_conv_w, v_ssd_conv_b, v_ssd_dt_bias, v_ssd_a_log, v_ssd_d, v_ssd_norm, v_hg_lower_bound, v_hg_norm, v_w_branch_a, v_w_branch_b, v_w_out, v_ffn2_norm, v_ffn2_w_gu, v_ffn2_w_down, v_final_norm):
    P = dict(zip(WEIGHTS, (meta_tokens, ffn1_norm, ffn1_w_gu, ffn1_w_down, mix_norm, w_in, ssd_conv_w, ssd_conv_b, ssd_dt_bias, ssd_a_log, ssd_d, ssd_norm, hg_lower_bound, hg_norm, w_branch_a, w_branch_b, w_out, ffn2_norm, ffn2_w_gu, ffn2_w_down, final_norm)))
    M = dict(zip(WEIGHTS, (m_meta_tokens, m_ffn1_norm, m_ffn1_w_gu, m_ffn1_w_down, m_mix_norm, m_w_in, m_ssd_conv_w, m_ssd_conv_b, m_ssd_dt_bias, m_ssd_a_log, m_ssd_d, m_ssd_norm, m_hg_lower_bound, m_hg_norm, m_w_branch_a, m_w_branch_b, m_w_out, m_ffn2_norm, m_ffn2_w_gu, m_ffn2_w_down, m_final_norm)))
    V = dict(zip(WEIGHTS, (v_meta_tokens, v_ffn1_norm, v_ffn1_w_gu, v_ffn1_w_down, v_mix_norm, v_w_in, v_ssd_conv_w, v_ssd_conv_b, v_ssd_dt_bias, v_ssd_a_log, v_ssd_d, v_ssd_norm, v_hg_lower_bound, v_hg_norm, v_w_branch_a, v_w_branch_b, v_w_out, v_ffn2_norm, v_ffn2_w_gu, v_ffn2_w_down, v_final_norm)))
    cx, cy, cc = _place()
    q = 2 * cx + cy

    mine = jnp.concatenate([meta_tokens.reshape(4, 1024), ssd_conv_w.reshape(2, 1024), jnp.zeros((2, 1024), F32)], axis=0)
    every = _exchange8(mine, False, "gather_small")
    meta_full = jnp.concatenate([every[2 * k, 0:4].reshape(N_META, 256) for k in range(N_CHIPS)], axis=1)
    conv_w_full = jnp.concatenate([every[2 * k, 4:6].reshape(SSD_CONV, 512) for k in range(N_CHIPS)], axis=1)

    rows = jnp.concatenate([P[n][0] for n in ROW_SHARDED], axis=0).astype(BF16)
    in_slot = lambda s: lax.dynamic_update_slice(lax.empty((N_CHIPS,) + s.shape, BF16), s.astype(BF16)[None], (q, 0, 0))
    gu1, gu2, w_in_all, rows_all = _gather_big(
        [in_slot(ffn1_w_gu[0]), in_slot(ffn2_w_gu[0]), in_slot(w_in[0]), in_slot(rows)], "gather_weights")
    W = {n: P[n] for n in SMALL}
    W["meta_tokens"], W["ssd_conv_w"] = meta_full, conv_w_full
    W["ffn1_w_gu"], W["ffn2_w_gu"] = gu1, gu2
    W["w_in"] = _split_w_in(w_in_all.transpose(1, 0, 2).reshape(D_MODEL, -1))
    r = 0
    for n in ROW_SHARDED:
        nr = P[n].shape[1]
        W[n] = rows_all[:, r:r + nr].reshape(N_CHIPS * nr, D_MODEL)
        r += nr

    loss8, grad_x, G = _local_step(x, loss_target, W)

    small = jnp.concatenate(
        [G["meta_tokens"]] + [_rows1024(G[n]) for n in SMALL if n != "meta_tokens"] + [_rows1024(loss8[0:1, 0:1])], axis=0)
    small = jnp.pad(small, ((0, 40 - small.shape[0]), (0, 0)))
    small = _exchange8(small, True, "reduce_small")
    Gs = {"meta_tokens": small[0:N_META]}
    r = N_META
    for n in SMALL:
        if n == "meta_tokens":
            continue
        nr = -(-G[n].size // 1024)
        Gs[n] = small[r:r + nr].reshape(-1)[:G[n].size].reshape(G[n].shape)
        r += nr
    loss = small[r, 0]
    Gs["meta_tokens"] = lax.dynamic_slice(Gs["meta_tokens"], (0, 256 * q), (N_META, 256))
    Gs["ssd_conv_w"] = lax.dynamic_slice(Gs["ssd_conv_w"], (0, 512 * q), (SSD_CONV, 512))[None]
    Gs = {n: Gs[n].reshape(P[n].shape) for n in SMALL}

    w_in_parts = G["w_in"].reshape(D_MODEL, N_CHIPS, -1).transpose(1, 0, 2)
    row_parts = jnp.concatenate([G[n].reshape(N_CHIPS, -1, D_MODEL) for n in ROW_SHARDED], axis=1)
    g_gu1, g_gu2, g_w_in, g_rows = _reduce_to_owners([G["ffn1_w_gu"], G["ffn2_w_gu"], w_in_parts, row_parts], cc)
    Gb = {"ffn1_w_gu": g_gu1, "ffn2_w_gu": g_gu2, "w_in": g_w_in}
    r = 0
    for n in ROW_SHARDED:
        nr = P[n].shape[1]
        Gb[n] = g_rows[r:r + nr]
        r += nr

    grads, delta, new_m, new_v = dict(Gs), {}, {}, {}
    d_s, m_s, v_s = _adamw(_pack_small(P), _pack_small(Gs), _pack_small(M), _pack_small(V), "adamw_small")
    delta.update(_unpack_small(d_s, P))
    new_m.update(_unpack_small(m_s, P))
    new_v.update(_unpack_small(v_s, P))
    for n in BIG:
        d_, m_, v_ = _adamw(P[n][0], Gb[n], M[n][0], V[n][0], f"adamw_{n}")
        grads[n], delta[n], new_m[n], new_v[n] = Gb[n][None], d_[None], m_[None], v_[None]
    return (loss, grad_x, *[grads[n] for n in WEIGHTS], *[delta[n] for n in WEIGHTS],
            *[new_m[n] for n in WEIGHTS], *[new_v[n] for n in WEIGHTS])
```

```python
import functools

import jax
import jax.numpy as jnp
from jax import lax
from jax.experimental import pallas as pl
from jax.experimental.pallas import tpu as pltpu
from jax.experimental.pallas import tpu_sc as plsc

F32 = jnp.float32
BF16 = jnp.bfloat16
HIGHEST = lax.Precision.HIGHEST
MESH = pl.DeviceIdType.MESH

D_MODEL = 1024
N_META = 16
EPS = 1e-6
SSD_HEADS = 16
SSD_HEAD_DIM = 64
SSD_INNER = 1024
SSD_GROUPS = 4
SSD_STATE = 128
SSD_CONV = 4
SSD_CONV_CH = 2048
HG_HEADS = 8
HG_SUB = 32
CHUNK = 128
D_FF = 2816
N_CHIPS = 4
IN_SIZES = (1024, 2048, 16, 1024, 1024, 1024, 1024, 1024, 1024)
ADAM_LR = 0.001
ADAM_B1 = 0.9
ADAM_B2 = 0.999
ADAM_EPS = 1e-08
ADAM_WD = 0.01
ADAM_STEP = 10
VMEM_LIMIT = 56 * 1024 * 1024
MATMUL_BLOCK_BYTES = 42 * 1024 * 1024


def _cparams(sem=None):
    return pltpu.CompilerParams(dimension_semantics=sem, vmem_limit_bytes=VMEM_LIMIT)


def _pick(n, cands):
    for c in cands:
        if n % c == 0:
            return c
    return n


def _dg(a, b, ca, cb):
    return lax.dot_general(a.astype(BF16), b.astype(BF16), (((ca,), (cb,)), ((), ())), preferred_element_type=F32)


@jax.custom_vjp
def _mm(a, b):
    return _dg(a, b, 1, 0)


def _mm_fwd(a, b):
    return _dg(a, b, 1, 0), (a, b)


def _mm_bwd(r, g):
    a, b = r
    return _dg(g, b, 1, 1), _dg(a, g, 0, 0)


_mm.defvjp(_mm_fwd, _mm_bwd)


@jax.custom_vjp
def _mm_nt(a, b):
    return _dg(a, b, 1, 1)


def _mm_nt_fwd(a, b):
    return _dg(a, b, 1, 1), (a, b)


def _mm_nt_bwd(r, g):
    a, b = r
    return _dg(g, b, 1, 0), _dg(g, a, 0, 0)


_mm_nt.defvjp(_mm_nt_fwd, _mm_nt_bwd)


@jax.custom_vjp
def _mm_tn(a, b):
    return _dg(a, b, 0, 0)


def _mm_tn_fwd(a, b):
    return _dg(a, b, 0, 0), (a, b)


def _mm_tn_bwd(r, g):
    a, b = r
    return _dg(b, g, 1, 1), _dg(a, g, 1, 0)


_mm_tn.defvjp(_mm_tn_fwd, _mm_tn_bwd)


def _silu(x):
    return x * jax.nn.sigmoid(x)


def _softplus(x):
    return jnp.maximum(x, 0.0) + jnp.log(1.0 + jnp.exp(-jnp.abs(x)))


def _tril(n):
    ri = lax.broadcasted_iota(jnp.int32, (n, n), 0)
    ci = lax.broadcasted_iota(jnp.int32, (n, n), 1)
    return ri >= ci


def _row_of(m, r):
    sub = lax.broadcasted_iota(jnp.int32, (m.shape[0], 1), 0)
    return jnp.sum(jnp.where(sub == r, m, 0.0), axis=0, keepdims=True)


def _col_of(m, c):
    lane = lax.broadcasted_iota(jnp.int32, (1, m.shape[1]), 1)
    return jnp.sum(jnp.where(lane == c, m, 0.0), axis=1, keepdims=True)


def _matmul(a, b, *, mode, out_dtype, name, alpha=1.0, res=None, tm=None, tn=None, tk=None, out_groups=None):
    b3 = b.ndim == 3
    if mode == "nn":
        M, K = a.shape
        G = b.shape[0] if b3 else 1
        Ng = b.shape[-1]
        N = G * Ng
    elif mode == "nt":
        M, K = a.shape
        G = b.shape[0] if b3 else 1
        N = b.shape[-2]
        Kg = b.shape[-1]
        assert G * Kg == K
    else:
        K, M = a.shape
        N = b.shape[1]
        G = out_groups or 1
        Ng = N // G
    if mode == "tn":
        tk = tk or K
        cands = (1408, 1024, 512, 256, 128)
        fits = [(m_ * n_, m_, n_) for m_ in cands if M % m_ == 0 for n_ in cands if Ng % n_ == 0
                if 2 * (tk * m_ * a.dtype.itemsize + tk * n_ * b.dtype.itemsize + m_ * n_ * 4) <= MATMUL_BLOCK_BYTES]
        _, tm_fit, tn_fit = max(fits)
        tm, tn = tm or tm_fit, tn or tn_fit
    else:
        tm = tm or _pick(M, (1088, 544, 256, 128))
        if mode == "nn":
            tn = tn or _pick(Ng, (1408, 512, 256, 128))
            tk = K
        else:
            tn = tn or _pick(N, (1408, 512, 256, 128))
            tk = tk or (Kg if b3 else K)
    nm, nn_, nk = M // tm, N // tn, K // tk
    assert nm * tm == M and nn_ * tn == N and nk * tk == K, (name, M, N, K, tm, tn, tk)

    if mode == "nn":
        a_spec = pl.BlockSpec((tm, tk), lambda i, j, k: (i, k))
        if b3:
            ns = Ng // tn
            b_spec = pl.BlockSpec((None, tk, tn), lambda i, j, k: (j // ns, k, j % ns))
        else:
            b_spec = pl.BlockSpec((tk, tn), lambda i, j, k: (k, j))
        ca, cb = 1, 0
    elif mode == "nt":
        a_spec = pl.BlockSpec((tm, tk), lambda i, j, k: (i, k))
        if b3:
            ks = Kg // tk
            b_spec = pl.BlockSpec((None, tn, tk), lambda i, j, k: (k // ks, j, k % ks))
        else:
            b_spec = pl.BlockSpec((tn, tk), lambda i, j, k: (j, k))
        ca, cb = 1, 1
    else:
        a_spec = pl.BlockSpec((tk, tm), lambda i, j, k: (k, i))
        b_spec = pl.BlockSpec((tk, tn), lambda i, j, k: (k, j))
        ca, cb = 0, 0
    if mode == "tn" and G > 1:
        ns = Ng // tn
        o_spec = pl.BlockSpec((None, tm, tn), lambda i, j, k: (j // ns, i, j % ns))
        out_shape = jax.ShapeDtypeStruct((G, M, Ng), out_dtype)
    else:
        o_spec = pl.BlockSpec((tm, tn), lambda i, j, k: (i, j))
        out_shape = jax.ShapeDtypeStruct((M, N), out_dtype)
    in_specs = [a_spec, b_spec]
    args = [a, b]
    if res is not None:
        in_specs.append(pl.BlockSpec((tm, tn), lambda i, j, k: (i, j)))
        args.append(res)
    has_res = res is not None

    def finish(refs, o):
        if alpha != 1.0:
            o = o * alpha
        if has_res:
            o = o + refs[2][...]
        return o

    def body_one(*refs):
        o_ref = refs[-1]
        o_ref[...] = finish(refs, _dg(refs[0][...], refs[1][...], ca, cb)).astype(o_ref.dtype)

    def body_acc(*refs):
        a_ref, b_ref = refs[0], refs[1]
        o_ref, acc_ref = refs[-2], refs[-1]
        k = pl.program_id(2)

        @pl.when(k == 0)
        def _():
            acc_ref[...] = jnp.zeros_like(acc_ref)

        acc_ref[...] += _dg(a_ref[...], b_ref[...], ca, cb)

        @pl.when(k == nk - 1)
        def _():
            o_ref[...] = finish(refs, acc_ref[...]).astype(o_ref.dtype)

    return pl.pallas_call(
        body_one if nk == 1 else body_acc, grid=(nm, nn_, nk), in_specs=in_specs, out_specs=o_spec, out_shape=out_shape,
        scratch_shapes=[] if nk == 1 else [pltpu.VMEM((tm, tn), F32)], name=name,
        compiler_params=_cparams(("parallel", "parallel", "arbitrary")),
    )(*args)


def _rms_fn(h, w):
    r = lax.rsqrt(jnp.mean(h * h, axis=-1, keepdims=True) + EPS)
    return h * r * w


def _swiglu_fn(gu):
    g = gu[:, :D_FF].astype(F32)
    u = gu[:, D_FF:].astype(F32)
    return _silu(g) * u


def _merge_fn(pa, pb, gates):
    return jax.nn.sigmoid(gates[:, :D_MODEL]) * pa + jax.nn.sigmoid(gates[:, D_MODEL:]) * pb


def _rows_call(body, *, rows, tr, ins, outs, accs=(), name):
    n = rows // tr
    assert n * tr == rows

    def spec(x):
        if isinstance(x, tuple):
            shp = x[1].shape
            return pl.BlockSpec(shp, lambda i: (0,) * len(shp))
        return pl.BlockSpec((tr, x.shape[1]), lambda i: (i, 0))

    in_specs = [spec(x) for x in ins]
    args = [x[1] if isinstance(x, tuple) else x for x in ins]
    out_specs = [spec(x) for x in outs] + [pl.BlockSpec(x.shape, lambda i: (0,) * len(x.shape)) for x in accs]
    out_shape = [x[1] if isinstance(x, tuple) else x for x in outs] + list(accs)
    return pl.pallas_call(
        body, grid=(n,), in_specs=in_specs, out_specs=out_specs, out_shape=out_shape, name=name,
        compiler_params=_cparams(("arbitrary",)),
    )(*args)


def _acc_rows(ref, val):
    @pl.when(pl.program_id(0) == 0)
    def _():
        ref[...] = jnp.zeros_like(ref)

    ref[0:1, :] += val


def _rms_fwd(h, w, name):
    def body(h_ref, w_ref, o_ref):
        o_ref[...] = _rms_fn(h_ref[...], w_ref[...]).astype(o_ref.dtype)

    R = h.shape[0]
    return _rows_call(body, rows=R, tr=_pick(R, (256, 128)), ins=[h, ("full", w)],
                      outs=[jax.ShapeDtypeStruct(h.shape, BF16)], name=name)[0]


def _rms_bwd(h, w, dn, dres, name):
    def body(h_ref, w_ref, dn_ref, dres_ref, dh_ref, dw_ref):
        _, vjp = jax.vjp(_rms_fn, h_ref[...], w_ref[...])
        dh, dw = vjp(dn_ref[...].astype(F32))
        dh_ref[...] = dh + dres_ref[...]
        _acc_rows(dw_ref, dw)

    R = h.shape[0]
    return _rows_call(body, rows=R, tr=_pick(R, (256, 128)), ins=[h, ("full", w), dn, dres],
                      outs=[jax.ShapeDtypeStruct(h.shape, F32)], accs=[jax.ShapeDtypeStruct((8, D_MODEL), F32)], name=name)


def _swiglu_fwd(gu, name):
    def body(gu_ref, o_ref):
        o_ref[...] = _swiglu_fn(gu_ref[...]).astype(o_ref.dtype)

    R = gu.shape[0]
    return _rows_call(body, rows=R, tr=_pick(R, (256, 128)), ins=[gu],
                      outs=[jax.ShapeDtypeStruct((R, D_FF), BF16)], name=name)[0]


def _swiglu_bwd(gu, da, name):
    def body(gu_ref, da_ref, o_ref):
        _, vjp = jax.vjp(_swiglu_fn, gu_ref[...].astype(F32))
        (dgu,) = vjp(da_ref[...].astype(F32))
        o_ref[...] = dgu.astype(o_ref.dtype)

    R = gu.shape[0]
    return _rows_call(body, rows=R, tr=_pick(R, (256, 128)), ins=[gu, da],
                      outs=[jax.ShapeDtypeStruct(gu.shape, BF16)], name=name)[0]


def _merge_fwd(pa, pb, gates, name):
    def body(pa_ref, pb_ref, g_ref, o_ref):
        o_ref[...] = _merge_fn(pa_ref[...], pb_ref[...], g_ref[...]).astype(o_ref.dtype)

    R = pa.shape[0]
    return _rows_call(body, rows=R, tr=_pick(R, (256, 128)), ins=[pa, pb, gates],
                      outs=[jax.ShapeDtypeStruct(pa.shape, BF16)], name=name)[0]


def _merge_bwd(pa, pb, gates, dm, name):
    def body(pa_ref, pb_ref, g_ref, dm_ref, dpa_ref, dpb_ref, dg_ref):
        _, vjp = jax.vjp(_merge_fn, pa_ref[...], pb_ref[...], g_ref[...])
        dpa, dpb, dg = vjp(dm_ref[...].astype(F32))
        dpa_ref[...] = dpa.astype(dpa_ref.dtype)
        dpb_ref[...] = dpb.astype(dpb_ref.dtype)
        dg_ref[...] = dg.astype(dg_ref.dtype)

    R = pa.shape[0]
    return _rows_call(body, rows=R, tr=_pick(R, (256, 128)), ins=[pa, pb, gates, dm],
                      outs=[jax.ShapeDtypeStruct(pa.shape, BF16), jax.ShapeDtypeStruct(pa.shape, BF16),
                            jax.ShapeDtypeStruct(gates.shape, BF16)], name=name)


def _loss_head(h3, w, target, nseq, name):
    Tp = h3.shape[0] // nseq
    nc = Tp // CHUNK

    def fn(h, w_, t, valid):
        y = _rms_fn(h, w_)
        e = (y - t) * valid
        return 0.5 * jnp.sum(jnp.mean(e * e, axis=-1, keepdims=True))

    def body(h_ref, w_ref, t_ref, loss_ref, dh_ref, dw_ref):
        b, c = pl.program_id(0), pl.program_id(1)
        valid = (c >= 1).astype(F32)
        t = t_ref[...]
        loss, vjp = jax.vjp(lambda h, w_: fn(h, w_, t, valid), h_ref[...], w_ref[...])
        dh, dw = vjp(jnp.ones((), F32))
        dh_ref[...] = dh

        @pl.when((b == 0) & (c == 0))
        def _():
            loss_ref[...] = jnp.zeros_like(loss_ref)
            dw_ref[...] = jnp.zeros_like(dw_ref)

        loss_ref[...] += jnp.full(loss_ref.shape, loss, F32)
        dw_ref[0:1, :] += dw

    return pl.pallas_call(
        body, grid=(nseq, nc),
        in_specs=[pl.BlockSpec((CHUNK, D_MODEL), lambda b, c: (b * nc + c, 0)),
                  pl.BlockSpec((1, D_MODEL), lambda b, c: (0, 0)),
                  pl.BlockSpec((None, CHUNK, D_MODEL), lambda b, c: (b, jnp.maximum(c - 1, 0), 0))],
        out_specs=[pl.BlockSpec((8, 128), lambda b, c: (0, 0)),
                   pl.BlockSpec((CHUNK, D_MODEL), lambda b, c: (b * nc + c, 0)),
                   pl.BlockSpec((8, D_MODEL), lambda b, c: (0, 0))],
        out_shape=[jax.ShapeDtypeStruct((8, 128), F32), jax.ShapeDtypeStruct(h3.shape, F32),
                   jax.ShapeDtypeStruct((8, D_MODEL), F32)],
        name=name, compiler_params=_cparams(("arbitrary", "arbitrary")),
    )(h3, w, target)


CONV_TILE = 512
CONV_HALO = 8


def _conv_fwd(xbc, w, b, pad, name):
    B, Tp, C = xbc.shape
    nch = Tp // CHUNK

    def body(x_ref, w_ref, b_ref, o_ref, xp):
        xp[0:CONV_HALO, :] = jnp.zeros((CONV_HALO, CONV_TILE), F32)
        xp[CONV_HALO:, :] = x_ref[...]
        for c in range(nch):
            acc = jnp.zeros((CHUNK, CONV_TILE), F32) + b_ref[...]
            for k in range(SSD_CONV):
                acc = acc + w_ref[k:k + 1, :] * xp[pl.ds(CONV_HALO + CHUNK * c - (SSD_CONV - 1) + k, CHUNK), :]
            row = CHUNK * c + lax.broadcasted_iota(jnp.int32, (CHUNK, 1), 0)
            o_ref[pl.ds(CHUNK * c, CHUNK), :] = jnp.where(row >= pad, _silu(acc), 0.0)

    return pl.pallas_call(
        body, grid=(B, C // CONV_TILE),
        in_specs=[pl.BlockSpec((None, Tp, CONV_TILE), lambda i, j: (i, 0, j)),
                  pl.BlockSpec((SSD_CONV, CONV_TILE), lambda i, j: (0, j)),
                  pl.BlockSpec((1, CONV_TILE), lambda i, j: (0, j))],
        out_specs=pl.BlockSpec((None, Tp, CONV_TILE), lambda i, j: (i, 0, j)),
        out_shape=jax.ShapeDtypeStruct(xbc.shape, F32),
        scratch_shapes=[pltpu.VMEM((Tp + CONV_HALO, CONV_TILE), F32)],
        name=name, compiler_params=_cparams(("arbitrary", "arbitrary")),
    )(xbc, w, b)


def _conv_bwd(xbc, w, b, dact, pad, name):
    B, Tp, C = xbc.shape
    nch = Tp // CHUNK

    def body(x_ref, w_ref, b_ref, da_ref, dx_ref, dw_ref, db_ref, xp, dp):
        bi = pl.program_id(1)
        xp[0:CONV_HALO, :] = jnp.zeros((CONV_HALO, CONV_TILE), F32)
        xp[CONV_HALO:, :] = x_ref[...]
        dp[pl.ds(Tp, CONV_HALO), :] = jnp.zeros((CONV_HALO, CONV_TILE), F32)
        dws = [jnp.zeros((1, CONV_TILE), F32) for _ in range(SSD_CONV)]
        dbs = jnp.zeros((1, CONV_TILE), F32)
        for c in range(nch):
            xs = [xp[pl.ds(CONV_HALO + CHUNK * c - (SSD_CONV - 1) + k, CHUNK), :] for k in range(SSD_CONV)]
            acc = jnp.zeros((CHUNK, CONV_TILE), F32) + b_ref[...]
            for k in range(SSD_CONV):
                acc = acc + w_ref[k:k + 1, :] * xs[k]
            row = CHUNK * c + lax.broadcasted_iota(jnp.int32, (CHUNK, 1), 0)
            sg = jax.nn.sigmoid(acc)
            dpre = jnp.where(row >= pad, da_ref[pl.ds(CHUNK * c, CHUNK), :] * (sg * (1.0 + acc * (1.0 - sg))), 0.0)
            dp[pl.ds(CHUNK * c, CHUNK), :] = dpre
            dbs = dbs + jnp.sum(dpre, axis=0, keepdims=True)
            for k in range(SSD_CONV):
                dws[k] = dws[k] + jnp.sum(dpre * xs[k], axis=0, keepdims=True)
        for c in range(nch):
            acc = jnp.zeros((CHUNK, CONV_TILE), F32)
            for k in range(SSD_CONV):
                acc = acc + w_ref[k:k + 1, :] * dp[pl.ds(CHUNK * c + (SSD_CONV - 1) - k, CHUNK), :]
            dx_ref[pl.ds(CHUNK * c, CHUNK), :] = acc

        @pl.when(bi == 0)
        def _():
            dw_ref[...] = jnp.zeros_like(dw_ref)
            db_ref[...] = jnp.zeros_like(db_ref)

        for k in range(SSD_CONV):
            dw_ref[k:k + 1, :] += dws[k]
        db_ref[0:1, :] += dbs

    return pl.pallas_call(
        body, grid=(C // CONV_TILE, B),
        in_specs=[pl.BlockSpec((None, Tp, CONV_TILE), lambda j, i: (i, 0, j)),
                  pl.BlockSpec((SSD_CONV, CONV_TILE), lambda j, i: (0, j)),
                  pl.BlockSpec((1, CONV_TILE), lambda j, i: (0, j)),
                  pl.BlockSpec((None, Tp, CONV_TILE), lambda j, i: (i, 0, j))],
        out_specs=[pl.BlockSpec((None, Tp, CONV_TILE), lambda j, i: (i, 0, j)),
                   pl.BlockSpec((8, CONV_TILE), lambda j, i: (0, j)),
                   pl.BlockSpec((8, CONV_TILE), lambda j, i: (0, j))],
        out_shape=[jax.ShapeDtypeStruct(xbc.shape, F32), jax.ShapeDtypeStruct((8, C), F32),
                   jax.ShapeDtypeStruct((8, C), F32)],
        scratch_shapes=[pltpu.VMEM((Tp + CONV_HALO, CONV_TILE), F32), pltpu.VMEM((Tp + CONV_HALO, CONV_TILE), F32)],
        name=name, compiler_params=_cparams(("arbitrary", "arbitrary")),
    )(xbc, w, b, dact)


def _ssd_chunk(xs, bm, cm, dtr, z, state, dt_bias, a_log, dskip, norm_w, valid):
    Q = xs.shape[0]
    lane = lax.broadcasted_iota(jnp.int32, (1, 128), 1)
    dt = jnp.where(lane < SSD_HEADS, _softplus(dtr + dt_bias), 0.0) * valid
    a = dt * (-jnp.exp(a_log))
    tril = _tril(Q)
    cs = jnp.dot(tril.astype(F32), a, precision=HIGHEST)
    cs_t = cs.T
    cs_end = _row_of(cs, Q - 1)
    low = lane < SSD_HEAD_DIM
    low_rows = lax.broadcasted_iota(jnp.int32, (128, 1), 0) < SSD_HEAD_DIM
    ys, new_state = [], []
    for g in range(SSD_GROUPS):
        bg = bm[:, 128 * g:128 * (g + 1)]
        cg = cm[:, 128 * g:128 * (g + 1)]
        cb = _mm_nt(cg, bg)
        for pr in range(2):
            p = 2 * g + pr
            h0, h1 = 2 * p, 2 * p + 1
            xp = xs[:, 128 * p:128 * (p + 1)]
            c0, c1 = _col_of(cs, h0), _col_of(cs, h1)
            e0, e1 = _col_of(cs_end, h0), _col_of(cs_end, h1)
            xd = xp * jnp.where(low, _col_of(dt, h0), _col_of(dt, h1))
            l0 = jnp.exp(jnp.where(tril, c0 - _row_of(cs_t, h0), -1e30))
            l1 = jnp.exp(jnp.where(tril, c1 - _row_of(cs_t, h1), -1e30))
            y_diag = jnp.where(low, _mm(cb * l0, xd), _mm(cb * l1, xd))
            to_end = jnp.where(low, jnp.exp(e0 - c0), jnp.exp(e1 - c1))
            sp = state[128 * p:128 * (p + 1), :]
            y_off = _mm_nt(cg, sp) * jnp.where(low, jnp.exp(c0), jnp.exp(c1))
            new_state.append(sp * jnp.where(low_rows, jnp.exp(e0), jnp.exp(e1)) + _mm_tn(xd * to_end, bg))
            ys.append(y_diag + y_off + xp * jnp.where(low, _col_of(dskip, h0), _col_of(dskip, h1)))
    y = jnp.concatenate(ys, axis=1) * _silu(z)
    gw = SSD_INNER // SSD_GROUPS
    outs = []
    for g in range(SSD_GROUPS):
        blk = y[:, gw * g:gw * (g + 1)]
        outs.append(blk * lax.rsqrt(jnp.mean(blk * blk, axis=-1, keepdims=True) + EPS))
    return jnp.concatenate(outs, axis=1) * norm_w, jnp.concatenate(new_state, axis=0)


def _valid_rows(c, pad):
    row = c * CHUNK + lax.broadcasted_iota(jnp.int32, (CHUNK, 1), 0)
    return (row >= pad).astype(F32)


def _ssd_fwd(xact, dtr, z, dt_bias, a_log, dskip, norm_w, pad, name):
    B, Tp, _ = xact.shape
    nc = Tp // CHUNK

    def body(xs_ref, bm_ref, cm_ref, dt_ref, z_ref, db_ref, al_ref, ds_ref, nw_ref, y_ref, save_ref, st):
        c = pl.program_id(1)

        @pl.when(c == 0)
        def _():
            st[...] = jnp.zeros_like(st)

        s0 = st[...]
        save_ref[...] = s0
        y, s1 = _ssd_chunk(xs_ref[...], bm_ref[...], cm_ref[...], dt_ref[...], z_ref[...], s0, db_ref[...],
                           al_ref[...], ds_ref[...], nw_ref[...], _valid_rows(c, pad))
        y_ref[...] = y.astype(y_ref.dtype)
        st[...] = s1

    row = lambda w, off=0: pl.BlockSpec((None, CHUNK, w), lambda b, c: (b, c, off))
    par = lambda w: pl.BlockSpec((1, w), lambda b, c: (0, 0))
    return pl.pallas_call(
        body, grid=(B, nc),
        in_specs=[row(1024, 0), row(512, 2), row(512, 3), row(128), row(1024), par(128), par(128), par(128), par(1024)],
        out_specs=[row(1024), pl.BlockSpec((None, None, 1024, 128), lambda b, c: (b, c, 0, 0))],
        out_shape=[jax.ShapeDtypeStruct((B, Tp, SSD_INNER), BF16), jax.ShapeDtypeStruct((B, nc, 1024, 128), F32)],
        scratch_shapes=[pltpu.VMEM((1024, 128), F32)],
        name=name, compiler_params=_cparams(("arbitrary", "arbitrary")),
    )(xact, xact, xact, dtr, z, dt_bias, a_log, dskip, norm_w)


def _ssd_bwd(xact, dtr, z, dt_bias, a_log, dskip, norm_w, saved, dy, pad, name):
    B, Tp, _ = xact.shape
    nc = Tp // CHUNK

    def body(xs_ref, bm_ref, cm_ref, dt_ref, z_ref, db_ref, al_ref, ds_ref, nw_ref, sv_ref, dy_ref,
             dx_ref, ddt_ref, dz_ref, dpar_ref, dnw_ref, dst):
        b, i = pl.program_id(0), pl.program_id(1)
        c = nc - 1 - i

        @pl.when(i == 0)
        def _():
            dst[...] = jnp.zeros_like(dst)

        valid = _valid_rows(c, pad)
        fn = lambda *a: _ssd_chunk(*a, valid)
        _, vjp = jax.vjp(fn, xs_ref[...], bm_ref[...], cm_ref[...], dt_ref[...], z_ref[...], sv_ref[...],
                         db_ref[...], al_ref[...], ds_ref[...], nw_ref[...])
        dxs, dbm, dcm, ddt, dz, dstate, ddb, dal, dds, dnw = vjp((dy_ref[...].astype(F32), dst[...]))
        dx_ref[:, 0:1024] = dxs
        dx_ref[:, 1024:1536] = dbm
        dx_ref[:, 1536:2048] = dcm
        ddt_ref[...] = ddt
        dz_ref[...] = dz
        dst[...] = dstate

        @pl.when((b == 0) & (i == 0))
        def _():
            dpar_ref[...] = jnp.zeros_like(dpar_ref)
            dnw_ref[...] = jnp.zeros_like(dnw_ref)

        dpar_ref[0:1, :] += ddb
        dpar_ref[1:2, :] += dal
        dpar_ref[2:3, :] += dds
        dnw_ref[0:1, :] += dnw

    row = lambda w, off=0: pl.BlockSpec((None, CHUNK, w), lambda b, i: (b, nc - 1 - i, off))
    par = lambda w: pl.BlockSpec((1, w), lambda b, i: (0, 0))
    acc = lambda w: pl.BlockSpec((8, w), lambda b, i: (0, 0))
    outs = pl.pallas_call(
        body, grid=(B, nc),
        in_specs=[row(1024, 0), row(512, 2), row(512, 3), row(128), row(1024), par(128), par(128), par(128), par(1024),
                  pl.BlockSpec((None, None, 1024, 128), lambda b, i: (b, nc - 1 - i, 0, 0)), row(1024)],
        out_specs=[row(2048), row(128), row(1024), acc(128), acc(1024)],
        out_shape=[jax.ShapeDtypeStruct((B, Tp, 2048), F32), jax.ShapeDtypeStruct((B, Tp, 128), F32),
                   jax.ShapeDtypeStruct((B, Tp, 1024), F32), jax.ShapeDtypeStruct((8, 128), F32),
                   jax.ShapeDtypeStruct((8, 1024), F32)],
        scratch_shapes=[pltpu.VMEM((1024, 128), F32)],
        name=name, compiler_params=_cparams(("arbitrary", "arbitrary")),
    )(xact, xact, xact, dtr, z, dt_bias, a_log, dskip, norm_w, saved, dy)
    return outs


def _hg_chunk(qr, fr, ir, gr, state_t, p0, p1, norm_w, valid):
    Q = qr.shape[0]
    lb = jax.nn.sigmoid(p0 - p1)
    f = lb + (1.0 - lb) * jax.nn.sigmoid(fr)
    k = 1.0 - f
    q = _silu(qr)
    v = ir * valid
    cum = jnp.dot(_tril(Q).astype(F32), jnp.log(f), precision=HIGHEST)
    cum_end = _row_of(cum, Q - 1)
    o_inter = _mm_nt(q * jnp.exp(cum), state_t)
    nblk = Q // HG_SUB
    row = lax.broadcasted_iota(jnp.int32, (Q, 1), 0)
    ri = lax.broadcasted_iota(jnp.int32, (Q, Q), 0)
    ci = lax.broadcasted_iota(jnp.int32, (Q, Q), 1)
    mids = jnp.concatenate([jnp.broadcast_to(_row_of(cum, HG_SUB * i + HG_SUB // 2 - 1), (HG_SUB, cum.shape[1]))
                            for i in range(nblk)], axis=0)
    sh = HG_SUB.bit_length() - 1
    same = (jnp.right_shift(ri, sh) == jnp.right_shift(ci, sh)) & (ri >= ci)
    att = jnp.where(same, _mm_nt(q * jnp.exp(cum - mids), k * jnp.exp(mids - cum)), 0.0)
    for i in range(1, nblk):
        lo = HG_SUB * i
        start = _row_of(cum, lo - 1)
        qa = q * jnp.exp(jnp.where((row >= lo) & (row < lo + HG_SUB), cum - start, -1e30))
        ka = k * jnp.exp(jnp.where(row < lo, start - cum, -1e30))
        att = att + _mm_nt(qa, ka)
    o = o_inter + _mm(att, v)
    new_state_t = state_t * jnp.exp(cum_end) + _mm_tn(v, k * jnp.exp(cum_end - cum))
    o = o * lax.rsqrt(jnp.mean(o * o, axis=-1, keepdims=True) + EPS) * norm_w
    return o * _silu(gr), new_state_t


HG_PER_STEP = 4
HG_COLS = 4 * 128


def _hg_fwd(qfig, lbh, nwh, pad, name):
    B, Tp, _ = qfig.shape
    nc = Tp // CHUNK
    hp = HG_PER_STEP

    def body(x_ref, lb_ref, nw_ref, y_ref, save_ref, st):
        c = pl.program_id(1)

        @pl.when(c == 0)
        def _():
            st[...] = jnp.zeros_like(st)

        valid = _valid_rows(c, pad)
        for j in range(hp):
            for b in range(B):
                s0 = st[j, b]
                save_ref[j, b] = s0
                col = lambda k: x_ref[b, :, HG_COLS * j + 128 * k:HG_COLS * j + 128 * (k + 1)]
                y, s1 = _hg_chunk(col(0), col(1), col(2), col(3), s0, lb_ref[j, 0:1, :], lb_ref[j, 1:2, :], nw_ref[j], valid)
                y_ref[b, :, 128 * j:128 * (j + 1)] = y.astype(y_ref.dtype)
                st[j, b] = s1

    return pl.pallas_call(
        body, grid=(HG_HEADS // hp, nc),
        in_specs=[pl.BlockSpec((B, CHUNK, HG_COLS * hp), lambda h, c: (0, c, h)),
                  pl.BlockSpec((hp, 2, 128), lambda h, c: (h, 0, 0)),
                  pl.BlockSpec((hp, 1, 128), lambda h, c: (h, 0, 0))],
        out_specs=[pl.BlockSpec((B, CHUNK, 128 * hp), lambda h, c: (0, c, h)),
                   pl.BlockSpec((hp, B, None, 128, 128), lambda h, c: (h, 0, c, 0, 0))],
        out_shape=[jax.ShapeDtypeStruct((B, Tp, 1024), BF16), jax.ShapeDtypeStruct((HG_HEADS, B, nc, 128, 128), F32)],
        scratch_shapes=[pltpu.VMEM((hp, B, 128, 128), F32)],
        name=name, compiler_params=_cparams(("arbitrary", "arbitrary")),
    )(qfig, lbh, nwh)


def _hg_bwd(qfig, lbh, nwh, saved, dy, pad, name):
    B, Tp, _ = qfig.shape
    nc = Tp // CHUNK
    hp = HG_PER_STEP

    def body(x_ref, lb_ref, nw_ref, sv_ref, dy_ref, dx_ref, dlb_ref, dnw_ref, dst):
        i = pl.program_id(1)
        c = nc - 1 - i

        @pl.when(i == 0)
        def _():
            dst[...] = jnp.zeros_like(dst)
            dlb_ref[...] = jnp.zeros_like(dlb_ref)
            dnw_ref[...] = jnp.zeros_like(dnw_ref)

        valid = _valid_rows(c, pad)
        fn = lambda *a: _hg_chunk(*a, valid)
        for j in range(hp):
            for b in range(B):
                col = lambda k: x_ref[b, :, HG_COLS * j + 128 * k:HG_COLS * j + 128 * (k + 1)]
                _, vjp = jax.vjp(fn, col(0), col(1), col(2), col(3), sv_ref[j, b], lb_ref[j, 0:1, :], lb_ref[j, 1:2, :], nw_ref[j])
                d4 = vjp((dy_ref[b, :, 128 * j:128 * (j + 1)].astype(F32), dst[j, b]))
                for k in range(4):
                    dx_ref[b, :, HG_COLS * j + 128 * k:HG_COLS * j + 128 * (k + 1)] = d4[k].astype(dx_ref.dtype)
                dst[j, b] = d4[4]
                dlb_ref[j, 0:1, :] += d4[5]
                dlb_ref[j, 1:2, :] += d4[6]
                dnw_ref[j, 0:1, :] += d4[7]

    acc = pl.BlockSpec((hp, 8, 128), lambda h, i: (h, 0, 0))
    return pl.pallas_call(
        body, grid=(HG_HEADS // hp, nc),
        in_specs=[pl.BlockSpec((B, CHUNK, HG_COLS * hp), lambda h, i: (0, nc - 1 - i, h)),
                  pl.BlockSpec((hp, 2, 128), lambda h, i: (h, 0, 0)),
                  pl.BlockSpec((hp, 1, 128), lambda h, i: (h, 0, 0)),
                  pl.BlockSpec((hp, B, None, 128, 128), lambda h, i: (h, 0, nc - 1 - i, 0, 0)),
                  pl.BlockSpec((B, CHUNK, 128 * hp), lambda h, i: (0, nc - 1 - i, h))],
        out_specs=[pl.BlockSpec((B, CHUNK, HG_COLS * hp), lambda h, i: (0, nc - 1 - i, h)), acc, acc],
        out_shape=[jax.ShapeDtypeStruct((B, Tp, 4096), BF16), jax.ShapeDtypeStruct((HG_HEADS, 8, 128), F32),
                   jax.ShapeDtypeStruct((HG_HEADS, 8, 128), F32)],
        scratch_shapes=[pltpu.VMEM((hp, B, 128, 128), F32)],
        name=name, compiler_params=_cparams(("arbitrary", "arbitrary")),
    )(qfig, lbh, nwh, saved, dy)


def _adamw(w, g, m, v, name):
    R, C = w.shape
    tr = _pick(R, (256, 176, 128, 64, 8)) if R > 256 else R

    def body(w_ref, g_ref, m_ref, v_ref, d_ref, mo_ref, vo_ref):
        g_ = g_ref[...]
        m_ = ADAM_B1 * m_ref[...] + (1.0 - ADAM_B1) * g_
        v_ = ADAM_B2 * v_ref[...] + (1.0 - ADAM_B2) * (g_ * g_)
        m_hat = m_ / (1.0 - ADAM_B1 ** ADAM_STEP)
        v_hat = v_ / (1.0 - ADAM_B2 ** ADAM_STEP)
        d_ref[...] = -ADAM_LR * (m_hat / (jnp.sqrt(v_hat) + ADAM_EPS) + ADAM_WD * w_ref[...])
        mo_ref[...] = m_
        vo_ref[...] = v_

    sp = pl.BlockSpec((tr, C), lambda i: (i, 0))
    sh = jax.ShapeDtypeStruct((R, C), F32)
    return pl.pallas_call(body, grid=(R // tr,), in_specs=[sp] * 4, out_specs=[sp] * 3, out_shape=[sh] * 3,
                          name=name, compiler_params=_cparams(("arbitrary",)))(w, g, m, v)


def _ffn_fwd(h, norm_w, w_gu, w_down, tag, after_norm=None):
    n = _rms_fwd(h, norm_w, f"{tag}_norm")
    if after_norm is not None:
        after_norm(n)
    gu = _matmul(n, w_gu, mode="nn", out_dtype=BF16, name=f"{tag}_gu")
    a = _swiglu_fwd(gu, f"{tag}_act")
    out = _matmul(a, w_down, mode="nn", out_dtype=F32, alpha=0.5, res=h, name=f"{tag}_down")
    return out, (n, gu, a)


def _ffn_bwd(h, norm_w, w_gu, w_down, saved, dout, tag):
    n, gu, a = saved
    da = _matmul(dout, w_down, mode="nt", out_dtype=BF16, alpha=0.5, name=f"{tag}_d_act")
    dw_down = _matmul(a, dout, mode="tn", out_dtype=F32, alpha=0.5, name=f"{tag}_dw_down")
    dgu = _swiglu_bwd(gu, da, f"{tag}_d_gu")
    dn = _matmul(dgu, w_gu, mode="nt", out_dtype=F32, name=f"{tag}_d_norm")
    dw_gu = _matmul(n, dgu, mode="tn", out_dtype=F32, out_groups=N_CHIPS, name=f"{tag}_dw_gu")
    dh, dnw = _rms_bwd(h, norm_w, dn, dout, f"{tag}_d_in")
    return dh, dnw, dw_gu, dw_down


IN_NAMES = ("z", "xbc", "dt", "q", "f", "i", "g", "gates")


def _split_w_in(w_in_full):
    pts = [0]
    for s in IN_SIZES:
        pts.append(pts[-1] + s)
    sl = lambda i, j: w_in_full[:, pts[i]:pts[j]]
    qfig = sl(3, 7).reshape(D_MODEL, 4, HG_HEADS, 128).transpose(0, 2, 1, 3).reshape(D_MODEL, 4 * D_MODEL)
    return {"z": sl(0, 1), "xbc": sl(1, 2), "dt": jnp.pad(sl(2, 3), ((0, 0), (0, 128 - SSD_HEADS))),
            "qfig": qfig, "gates": sl(7, 9)}


def _local_step(x, target, W):
    B, S, _ = x.shape
    T = N_META + S
    pad = (-T) % CHUNK
    Tp = T + pad
    assert pad + N_META == CHUNK
    R = B * Tp
    meta = jnp.broadcast_to(W["meta_tokens"][None], (B, N_META, D_MODEL))
    h0 = jnp.concatenate([jnp.zeros((B, pad, D_MODEL), F32), meta, x], axis=1).reshape(R, D_MODEL)

    stage = W.get("_stage", lambda name, x: {})
    W = dict(W)
    h1, sv1 = _ffn_fwd(h0, W["ffn1_norm"], W["ffn1_w_gu"], W["ffn1_w_down"], "ffn1", lambda n: W.update(stage("ffn1_norm", n)))
    W.update(stage("ffn1_out", h1))
    um = _rms_fwd(h1, W["mix_norm"], "mix_norm")
    wi = W["w_in"]
    z = _matmul(um, wi["z"], mode="nn", out_dtype=F32, name="in_z")
    xbc = _matmul(um, wi["xbc"], mode="nn", out_dtype=F32, name="in_xbc")
    dtr = _matmul(um, wi["dt"], mode="nn", out_dtype=F32, name="in_dt")
    qfig = _matmul(um, wi["qfig"], mode="nn", out_dtype=F32, name="in_qfig")
    gates = _matmul(um, wi["gates"], mode="nn", out_dtype=F32, name="in_gates")

    r3 = lambda t: t.reshape(B, Tp, t.shape[-1])
    lane_pad = lambda t: jnp.pad(t, ((0, 0), (0, 128 - t.shape[1])))
    dt_bias, a_log, dskip = lane_pad(W["ssd_dt_bias"]), lane_pad(W["ssd_a_log"]), lane_pad(W["ssd_d"])
    xact = _conv_fwd(r3(xbc), W["ssd_conv_w"], W["ssd_conv_b"], pad, "conv_fwd")
    ya, ssd_saved = _ssd_fwd(xact, r3(dtr), r3(z), dt_bias, a_log, dskip, W["ssd_norm"], pad, "ssd_fwd")
    lbh = W["hg_lower_bound"].reshape(2, HG_HEADS, 128).transpose(1, 0, 2)
    nwh = W["hg_norm"].reshape(HG_HEADS, 1, 128)
    yb, hg_saved = _hg_fwd(r3(qfig), lbh, nwh, pad, "hg_fwd")
    ya2, yb2 = ya.reshape(R, -1), yb.reshape(R, -1)
    W.update(stage("mixers_out", yb2))
    pa = _matmul(ya2, W["w_branch_a"], mode="nn", out_dtype=F32, name="branch_a")
    pb = _matmul(yb2, W["w_branch_b"], mode="nn", out_dtype=F32, name="branch_b")
    mg = _merge_fwd(pa, pb, gates, "merge")
    h2 = _matmul(mg, W["w_out"], mode="nn", out_dtype=F32, res=h1, name="mix_out")
    h3, sv2 = _ffn_fwd(h2, W["ffn2_norm"], W["ffn2_w_gu"], W["ffn2_w_down"], "ffn2")

    loss, dh3, d_final = _loss_head(h3, W["final_norm"].reshape(1, D_MODEL), target, B, "loss_head")

    G = {"final_norm": d_final[0]}
    dh2, dnw, G["ffn2_w_gu"], G["ffn2_w_down"] = _ffn_bwd(h2, W["ffn2_norm"], W["ffn2_w_gu"], W["ffn2_w_down"], sv2, dh3, "ffn2")
    G["ffn2_norm"] = dnw[0:1]
    dmg = _matmul(dh2, W["w_out"], mode="nt", out_dtype=BF16, name="d_merge")
    G["w_out"] = _matmul(mg, dh2, mode="tn", out_dtype=F32, name="dw_out")
    dpa, dpb, dgates = _merge_bwd(pa, pb, gates, dmg, "merge_bwd")
    dya = _matmul(dpa, W["w_branch_a"], mode="nt", out_dtype=BF16, name="d_ya")
    dyb = _matmul(dpb, W["w_branch_b"], mode="nt", out_dtype=BF16, name="d_yb")
    G["w_branch_a"] = _matmul(ya2, dpa, mode="tn", out_dtype=F32, name="dw_branch_a")
    G["w_branch_b"] = _matmul(yb2, dpb, mode="tn", out_dtype=F32, name="dw_branch_b")

    dxact, ddtr, dz, dpar, dnw = _ssd_bwd(xact, r3(dtr), r3(z), dt_bias, a_log, dskip, W["ssd_norm"], ssd_saved,
                                          r3(dya), pad, "ssd_bwd")
    G["ssd_dt_bias"], G["ssd_a_log"], G["ssd_d"] = dpar[0:1, :SSD_HEADS], dpar[1:2, :SSD_HEADS], dpar[2:3, :SSD_HEADS]
    G["ssd_norm"] = dnw[0:1]
    dxbc, dcw, dcb = _conv_bwd(r3(xbc), W["ssd_conv_w"], W["ssd_conv_b"], dxact, pad, "conv_bwd")
    G["ssd_conv_w"], G["ssd_conv_b"] = dcw[0:SSD_CONV], dcb[0:1]
    dqfig, dlb, dhn = _hg_bwd(r3(qfig), lbh, nwh, hg_saved, r3(dyb), pad, "hg_bwd")
    G["hg_lower_bound"] = dlb[:, 0:2, :].transpose(1, 0, 2).reshape(2, D_MODEL)
    G["hg_norm"] = dhn[:, 0, :].reshape(1, D_MODEL)

    r2 = lambda t: t.reshape(R, t.shape[-1])
    pieces = [("z", r2(dz)), ("xbc", r2(dxbc)), ("dt", r2(ddtr)), ("qfig", r2(dqfig)), ("gates", dgates)]
    dum = None
    dwi = {}
    for nm, dpiece in pieces:
        dum = _matmul(dpiece, wi[nm], mode="nt", out_dtype=F32, res=dum, name=f"d_mix_{nm}")
        dwi[nm] = _matmul(um, dpiece, mode="tn", out_dtype=F32, name=f"dw_in_{nm}")
    dw_qfig = dwi["qfig"].reshape(D_MODEL, HG_HEADS, 4, 128).transpose(0, 2, 1, 3).reshape(D_MODEL, 4 * D_MODEL)
    G["w_in"] = jnp.concatenate([dwi["z"], dwi["xbc"], dwi["dt"][:, :SSD_HEADS], dw_qfig, dwi["gates"]], axis=1)
    dh1, dnw = _rms_bwd(h1, W["mix_norm"], dum, dh2, "mix_norm_bwd")
    G["mix_norm"] = dnw[0:1]
    dh0, dnw, G["ffn1_w_gu"], G["ffn1_w_down"] = _ffn_bwd(h0, W["ffn1_norm"], W["ffn1_w_gu"], W["ffn1_w_down"], sv1, dh1, "ffn1")
    G["ffn1_norm"] = dnw[0:1]
    dh0 = dh0.reshape(B, Tp, D_MODEL)
    G["meta_tokens"] = jnp.sum(dh0[:, pad:CHUNK], axis=0)
    return loss, dh0[:, CHUNK:], G


ANY = pl.BlockSpec(memory_space=pl.ANY)


def _place():
    return lax.axis_index("x"), lax.axis_index("y"), lax.axis_index("c")


def _other_chips(x, y):
    return [(1 - x, y), (x, 1 - y), (1 - x, 1 - y)]


def _remote(src, dst, ssem, rsem, dev):
    return pltpu.make_async_remote_copy(src_ref=src, dst_ref=dst, send_sem=ssem, recv_sem=rsem,
                                        device_id=dev, device_id_type=MESH)


def _exchange8(buf, reduce, name):
    n, w = buf.shape

    def body(x_ref, *rest):
        if reduce:
            red_ref, out_ref, ssem, rsem = rest
        else:
            out_ref, ssem, rsem = rest
        x, y, c = _place()
        me = 4 * x + 2 * y + c
        out_ref[me] = x_ref[...]
        copies = []
        for k in range(1, 8):
            px = 1 - x if (k >> 2) & 1 else x
            py = 1 - y if (k >> 1) & 1 else y
            pc = 1 - c if k & 1 else c
            cp = _remote(x_ref, out_ref.at[me], ssem.at[k - 1], rsem.at[k - 1], (px, py, pc))
            cp.start()
            copies.append((cp, 4 * px + 2 * py + pc))
        for k, (cp, peer) in enumerate(copies):
            _remote(x_ref, out_ref.at[peer], ssem.at[k], rsem.at[k], (x, y, c)).wait_recv()
        for cp, _ in copies:
            cp.wait_send()
        if reduce:
            acc = out_ref[0]
            for d in range(1, 8):
                acc = acc + out_ref[d]
            red_ref[...] = acc

    vm = pl.BlockSpec(memory_space=pltpu.VMEM)
    g_shape = jax.ShapeDtypeStruct((8, n, w), F32)
    if reduce:
        out_shape, out_specs, scratch = [jax.ShapeDtypeStruct((n, w), F32)], [vm], [pltpu.VMEM((8, n, w), F32)]
    else:
        out_shape, out_specs, scratch = [g_shape], [vm], []
    return pl.pallas_call(
        body, in_specs=[vm], out_specs=out_specs, out_shape=out_shape,
        scratch_shapes=scratch + [pltpu.SemaphoreType.DMA((7,)), pltpu.SemaphoreType.DMA((7,))], name=name,
    )(buf)[0]


def _gather_big(blocks, name):
    n = len(blocks)
    half = [s.shape[1] // 2 for s in blocks]

    def body(*refs):
        full = refs[n:2 * n]
        ssem, rsem, fssem, frsem = refs[2 * n:]
        x, y, c = _place()
        q = 2 * x + y
        chips = _other_chips(x, y)
        piece = lambda s, qq, cc: full[s].at[qq, pl.ds(cc * half[s], half[s])]
        sends = []
        for j, (px, py) in enumerate(chips):
            for s in range(n):
                cp = _remote(piece(s, q, c), piece(s, q, c), ssem.at[s, j], rsem.at[s, j], (px, py, c))
                cp.start()
                sends.append(cp)
        for j, (px, py) in enumerate(chips):
            for s in range(n):
                got = piece(s, 2 * px + py, c)
                _remote(got, got, ssem.at[s, j], rsem.at[s, j], (px, py, c)).wait_recv()
                cp = _remote(got, got, fssem.at[s, j], frsem.at[s, j], (x, y, 1 - c))
                cp.start()
                sends.append(cp)
        for j, (px, py) in enumerate(chips):
            for s in range(n):
                got = piece(s, 2 * px + py, 1 - c)
                _remote(got, got, fssem.at[s, j], frsem.at[s, j], (x, y, 1 - c)).wait_recv()
        for cp in sends:
            cp.wait_send()

    return pl.pallas_call(
        body, in_specs=[ANY] * n, out_specs=[ANY] * n,
        out_shape=[jax.ShapeDtypeStruct(s.shape, s.dtype) for s in blocks],
        input_output_aliases={s: s for s in range(n)},
        scratch_shapes=[pltpu.SemaphoreType.DMA((n, 3))] * 4, name=name,
    )(*blocks)


def _gather_seq(blocks, name, collective_id):
    n = len(blocks)
    half = [s.shape[1] // 2 for s in blocks]
    full = [jax.new_ref(b, memory_space=pltpu.MemorySpace.HBM) for b in blocks]

    @pl.kernel(mesh=plsc.ScalarSubcoreMesh(axis_name="sequencer", num_cores=1), name=name,
               scratch_types=[pltpu.SemaphoreType.DMA((n, 3))] * 4,
               compiler_params=pltpu.CompilerParams(collective_id=collective_id))
    def launch(ssem, rsem, fssem, frsem):
        x, y, c = _place()
        q = 2 * x + y
        chips = _other_chips(x, y)
        barrier = pltpu.get_barrier_semaphore()
        for peer in [(px, py, c) for px, py in chips] + [(x, y, 1 - c)]:
            pl.semaphore_signal(barrier, inc=1, device_id=peer, device_id_type=MESH)
        pl.semaphore_wait(barrier, 4)
        piece = lambda s, qq, cc: full[s].at[qq, pl.ds(cc * half[s], half[s])]
        sends = []
        for j, (px, py) in enumerate(chips):
            for s in range(n):
                cp = _remote(piece(s, q, c), piece(s, q, c), ssem.at[s, j], rsem.at[s, j], (px, py, c))
                cp.start()
                sends.append(cp)
        for j, (px, py) in enumerate(chips):
            for s in range(n):
                got = piece(s, 2 * px + py, c)
                _remote(got, got, ssem.at[s, j], rsem.at[s, j], (px, py, c)).wait_recv()
                cp = _remote(got, got, fssem.at[s, j], frsem.at[s, j], (x, y, 1 - c))
                cp.start()
                sends.append(cp)
        for j, (px, py) in enumerate(chips):
            for s in range(n):
                got = piece(s, 2 * px + py, 1 - c)
                _remote(got, got, fssem.at[s, j], frsem.at[s, j], (x, y, 1 - c)).wait_recv()
        for cp in sends:
            cp.wait_send()

    launch()
    return [r[...] for r in full]


def _pair_swap(parts, name):
    n = len(parts)
    half = [p.shape[1] // 2 for p in parts]

    def body(*refs):
        src, got = refs[:n], refs[n:2 * n]
        ssem, rsem = refs[2 * n:]
        x, y, c = _place()
        copies = []
        for s in range(n):
            cp = _remote(src[s].at[pl.ds(0, N_CHIPS), pl.ds((1 - c) * half[s], half[s])], got[s], ssem.at[s], rsem.at[s], (x, y, 1 - c))
            cp.start()
            copies.append(cp)
        for cp in copies:
            cp.wait_recv()
        for cp in copies:
            cp.wait_send()

    return pl.pallas_call(
        body, in_specs=[ANY] * n, out_specs=[ANY] * n,
        out_shape=[jax.ShapeDtypeStruct((N_CHIPS, h, p.shape[2]), p.dtype) for p, h in zip(parts, half)],
        scratch_shapes=[pltpu.SemaphoreType.DMA((n,)), pltpu.SemaphoreType.DMA((n,))], name=name,
    )(*parts)


def _to_owners(sums, name):
    n = len(sums)

    def body(*refs):
        src, got = refs[:n], refs[n:2 * n]
        lsem, ssem, rsem = refs[2 * n:]
        x, y, c = _place()
        q = 2 * x + y
        chips = _other_chips(x, y)
        started, sends = [], []
        for s in range(n):
            cp = pltpu.make_async_copy(src[s].at[q], got[s].at[q], lsem.at[s])
            cp.start()
            started.append(cp)
        for j, (px, py) in enumerate(chips):
            for s in range(n):
                cp = _remote(src[s].at[2 * px + py], got[s].at[q], ssem.at[s, j], rsem.at[s, j], (px, py, c))
                cp.start()
                sends.append(cp)
        for j, (px, py) in enumerate(chips):
            for s in range(n):
                slot = got[s].at[2 * px + py]
                _remote(slot, slot, ssem.at[s, j], rsem.at[s, j], (px, py, c)).wait_recv()
        for cp in sends:
            cp.wait_send()
        for cp in started:
            cp.wait()

    return pl.pallas_call(
        body, in_specs=[ANY] * n, out_specs=[ANY] * n,
        out_shape=[jax.ShapeDtypeStruct(s.shape, s.dtype) for s in sums],
        scratch_shapes=[pltpu.SemaphoreType.DMA((n,)), pltpu.SemaphoreType.DMA((n, 3)), pltpu.SemaphoreType.DMA((n, 3))],
        name=name,
    )(*sums)


def _pair_join(blocks, name):
    n = len(blocks)

    def body(*refs):
        out = refs[n:2 * n]
        ssem, rsem = refs[2 * n:]
        x, y, c = _place()
        sends = []
        for s in range(n):
            h = blocks[s].shape[0] // 2
            mine = out[s].at[pl.ds(c * h, h)]
            cp = _remote(mine, mine, ssem.at[s], rsem.at[s], (x, y, 1 - c))
            cp.start()
            sends.append(cp)
        for s in range(n):
            h = blocks[s].shape[0] // 2
            theirs = out[s].at[pl.ds((1 - c) * h, h)]
            _remote(theirs, theirs, ssem.at[s], rsem.at[s], (x, y, 1 - c)).wait_recv()
        for cp in sends:
            cp.wait_send()

    return pl.pallas_call(
        body, in_specs=[ANY] * n, out_specs=[ANY] * n,
        out_shape=[jax.ShapeDtypeStruct(b.shape, b.dtype) for b in blocks],
        input_output_aliases={s: s for s in range(n)},
        scratch_shapes=[pltpu.SemaphoreType.DMA((n,))] * 2, name=name,
    )(*blocks)


WIRE = BF16


def _row_tile(h):
    return _pick(h, (256, 272, 128, 16))


def _add_pair(part, got, c, name):
    _, h, w = got.shape
    tr = _row_tile(h)
    nt = h // tr

    def body(c_ref, p_ref, g_ref, o_ref):
        o_ref[...] = (p_ref[...] + g_ref[...].astype(F32)).astype(o_ref.dtype)

    return pl.pallas_call(
        body,
        grid_spec=pltpu.PrefetchScalarGridSpec(
            num_scalar_prefetch=1, grid=(N_CHIPS, nt),
            in_specs=[pl.BlockSpec((None, tr, w), lambda q, i, c_ref: (q, c_ref[0] * nt + i, 0)),
                      pl.BlockSpec((None, tr, w), lambda q, i, c_ref: (q, i, 0))],
            out_specs=pl.BlockSpec((None, tr, w), lambda q, i, c_ref: (q, i, 0))),
        out_shape=jax.ShapeDtypeStruct(got.shape, WIRE), name=name,
        compiler_params=_cparams(("arbitrary", "arbitrary")),
    )(c.reshape(1).astype(jnp.int32), part, got)


def _sum_chips(slots, c, name):
    _, h, w = slots.shape
    tr = _row_tile(h)
    nt = h // tr

    def body(c_ref, s_ref, o_ref):
        o_ref[...] = ((s_ref[0].astype(F32) + s_ref[1].astype(F32)) + s_ref[2].astype(F32)) + s_ref[3].astype(F32)

    return pl.pallas_call(
        body,
        grid_spec=pltpu.PrefetchScalarGridSpec(
            num_scalar_prefetch=1, grid=(nt,),
            in_specs=[pl.BlockSpec((N_CHIPS, tr, w), lambda i, c_ref: (0, i, 0))],
            out_specs=pl.BlockSpec((tr, w), lambda i, c_ref: (c_ref[0] * nt + i, 0))),
        out_shape=jax.ShapeDtypeStruct((2 * h, w), F32), name=name,
        compiler_params=_cparams(("arbitrary",)),
    )(c.reshape(1).astype(jnp.int32), slots)


def _reduce_to_owners(parts, c):
    got = _pair_swap(parts, "grad_pair_swap")
    sums = [_add_pair(p, g, c, f"grad_pair_add{i}") for i, (p, g) in enumerate(zip(parts, got))]
    slots = _to_owners(sums, "grad_to_owners")
    blocks = [_sum_chips(s, c, f"grad_sum_chips{i}") for i, s in enumerate(slots)]
    return _pair_join(blocks, "grad_pair_join")


WEIGHTS = ("meta_tokens", "ffn1_norm", "ffn1_w_gu", "ffn1_w_down", "mix_norm", "w_in", "ssd_conv_w", "ssd_conv_b",
           "ssd_dt_bias", "ssd_a_log", "ssd_d", "ssd_norm", "hg_lower_bound", "hg_norm", "w_branch_a", "w_branch_b",
           "w_out", "ffn2_norm", "ffn2_w_gu", "ffn2_w_down", "final_norm")
BIG = ("ffn1_w_gu", "ffn1_w_down", "w_in", "w_branch_a", "w_branch_b", "w_out", "ffn2_w_gu", "ffn2_w_down")
ROW_SHARDED = ("ffn1_w_down", "ffn2_w_down", "w_branch_a", "w_branch_b", "w_out")
SMALL = tuple(n for n in WEIGHTS if n not in BIG)
SMALL_ROWS = 24


def _rows1024(a):
    flat = a.reshape(-1)
    n = -(-flat.shape[0] // 1024) * 1024
    return jnp.pad(flat, (0, n - flat.shape[0])).reshape(-1, 1024)


def _pack_small(d):
    rows = jnp.concatenate([_rows1024(d[n]) for n in SMALL], axis=0)
    return jnp.pad(rows, ((0, SMALL_ROWS - rows.shape[0]), (0, 0)))


def _unpack_small(packed, like):
    out, r = {}, 0
    for n in SMALL:
        size = like[n].size
        nr = -(-size // 1024)
        out[n] = packed[r:r + nr].reshape(-1)[:size].reshape(like[n].shape)
        r += nr
    return out


def kernel(x, meta_tokens, ffn1_norm, ffn1_w_gu, ffn1_w_down, mix_norm, w_in, ssd_conv_w, ssd_conv_b, ssd_dt_bias, ssd_a_log, ssd_d, ssd_norm, hg_lower_bound, hg_norm, w_branch_a, w_branch_b, w_out, ffn2_norm, ffn2_w_gu, ffn2_w_down, final_norm, loss_target, m_meta_tokens, m_ffn1_norm, m_ffn1_w_gu, m_ffn1_w_down, m_mix_norm, m_w_in, m_ssd_conv_w, m_ssd_conv_b, m_ssd_dt_bias, m_ssd_a_log, m_ssd_d, m_ssd_norm, m_hg_lower_bound, m_hg_norm, m_w_branch_a, m_w_branch_b, m_w_out, m_ffn2_norm, m_ffn2_w_gu, m_ffn2_w_down, m_final_norm, v_meta_tokens, v_ffn1_norm, v_ffn1_w_gu, v_ffn1_w_down, v_mix_norm, v_w_in, v_ssd_conv_w, v_ssd_conv_b, v_ssd_dt_bias, v_ssd_a_log, v_ssd_d, v_ssd_norm, v_hg_lower_bound, v_hg_norm, v_w_branch_a, v_w_branch_b, v_w_out, v_ffn2_norm, v_ffn2_w_gu, v_ffn2_w_down, v_final_norm):
    P = dict(zip(WEIGHTS, (meta_tokens, ffn1_norm, ffn1_w_gu, ffn1_w_down, mix_norm, w_in, ssd_conv_w, ssd_conv_b, ssd_dt_bias, ssd_a_log, ssd_d, ssd_norm, hg_lower_bound, hg_norm, w_branch_a, w_branch_b, w_out, ffn2_norm, ffn2_w_gu, ffn2_w_down, final_norm)))
    M = dict(zip(WEIGHTS, (m_meta_tokens, m_ffn1_norm, m_ffn1_w_gu, m_ffn1_w_down, m_mix_norm, m_w_in, m_ssd_conv_w, m_ssd_conv_b, m_ssd_dt_bias, m_ssd_a_log, m_ssd_d, m_ssd_norm, m_hg_lower_bound, m_hg_norm, m_w_branch_a, m_w_branch_b, m_w_out, m_ffn2_norm, m_ffn2_w_gu, m_ffn2_w_down, m_final_norm)))
    V = dict(zip(WEIGHTS, (v_meta_tokens, v_ffn1_norm, v_ffn1_w_gu, v_ffn1_w_down, v_mix_norm, v_w_in, v_ssd_conv_w, v_ssd_conv_b, v_ssd_dt_bias, v_ssd_a_log, v_ssd_d, v_ssd_norm, v_hg_lower_bound, v_hg_norm, v_w_branch_a, v_w_branch_b, v_w_out, v_ffn2_norm, v_ffn2_w_gu, v_ffn2_w_down, v_final_norm)))
    cx, cy, cc = _place()
    q = 2 * cx + cy

    mine = jnp.concatenate([meta_tokens.reshape(4, 1024), ssd_conv_w.reshape(2, 1024), jnp.zeros((2, 1024), F32)], axis=0)
    every = _exchange8(mine, False, "gather_small")
    meta_full = jnp.concatenate([every[2 * k, 0:4].reshape(N_META, 256) for k in range(N_CHIPS)], axis=1)
    conv_w_full = jnp.concatenate([every[2 * k, 4:6].reshape(SSD_CONV, 512) for k in range(N_CHIPS)], axis=1)

    late = ("ffn2_w_down", "w_branch_a", "w_branch_b", "w_out")
    rows = jnp.concatenate([P[n][0] for n in late], axis=0)
    zero = lambda t, dtype=F32: (t[0:1, 0:1] * 0).astype(dtype)

    def in_slot(s, after=None):
        s = s if after is None else s + zero(after)
        return lax.dynamic_update_slice(lax.empty((N_CHIPS,) + s.shape, BF16), s.astype(BF16)[None], (q, 0, 0))

    gu1, down1 = _gather_seq([in_slot(ffn1_w_gu[0]), in_slot(ffn1_w_down[0])], "gather_ffn1", 1)
    W = {n: P[n] for n in SMALL}
    W["meta_tokens"], W["ssd_conv_w"] = meta_full, conv_w_full
    W["ffn1_w_gu"], W["ffn1_w_down"] = gu1, down1.reshape(-1, D_MODEL)
    flying = {}

    def stage(name, t):
        if name == "ffn1_norm":
            flying["w_in"] = _gather_seq([in_slot(w_in[0], t)], "gather_w_in", 2)
            return {}
        if name == "ffn1_out":
            flying["late"] = _gather_seq([in_slot(ffn2_w_gu[0], t), in_slot(rows, t)], "gather_late", 3)
            (w_in_all,) = flying["w_in"]
            w_in_all = w_in_all + zero(t, BF16)
            return {"w_in": _split_w_in(w_in_all.transpose(1, 0, 2).reshape(D_MODEL, -1))}
        if name == "mixers_out":
            gu2, rows_all = flying["late"]
            out, r = {"ffn2_w_gu": gu2}, 0
            for n in late:
                nr = P[n].shape[1]
                out[n] = (rows_all[:, r:r + nr] + zero(t, BF16)).reshape(N_CHIPS * nr, D_MODEL)
                r += nr
            return out
        return {}

    W["_stage"] = stage

    loss8, grad_x, G = _local_step(x, loss_target, W)

    small = jnp.concatenate(
        [G["meta_tokens"]] + [_rows1024(G[n]) for n in SMALL if n != "meta_tokens"] + [_rows1024(loss8[0:1, 0:1])], axis=0)
    small = jnp.pad(small, ((0, 40 - small.shape[0]), (0, 0)))
    small = _exchange8(small, True, "reduce_small")
    Gs = {"meta_tokens": small[0:N_META]}
    r = N_META
    for n in SMALL:
        if n == "meta_tokens":
            continue
        nr = -(-G[n].size // 1024)
        Gs[n] = small[r:r + nr].reshape(-1)[:G[n].size].reshape(G[n].shape)
        r += nr
    loss = small[r, 0]
    Gs["meta_tokens"] = lax.dynamic_slice(Gs["meta_tokens"], (0, 256 * q), (N_META, 256))
    Gs["ssd_conv_w"] = lax.dynamic_slice(Gs["ssd_conv_w"], (0, 512 * q), (SSD_CONV, 512))[None]
    Gs = {n: Gs[n].reshape(P[n].shape) for n in SMALL}

    w_in_parts = G["w_in"].reshape(D_MODEL, N_CHIPS, -1).transpose(1, 0, 2)
    row_parts = jnp.concatenate([G[n].reshape(N_CHIPS, -1, D_MODEL) for n in ROW_SHARDED], axis=1)
    g_gu1, g_gu2, g_w_in, g_rows = _reduce_to_owners([G["ffn1_w_gu"], G["ffn2_w_gu"], w_in_parts, row_parts], cc)
    Gb = {"ffn1_w_gu": g_gu1, "ffn2_w_gu": g_gu2, "w_in": g_w_in}
    r = 0
    for n in ROW_SHARDED:
        nr = P[n].shape[1]
        Gb[n] = g_rows[r:r + nr]
        r += nr

    grads, delta, new_m, new_v = dict(Gs), {}, {}, {}
    d_s, m_s, v_s = _adamw(_pack_small(P), _pack_small(Gs), _pack_small(M), _pack_small(V), "adamw_small")
    delta.update(_unpack_small(d_s, P))
    new_m.update(_unpack_small(m_s, P))
    new_v.update(_unpack_small(v_s, P))
    for n in BIG:
        d_, m_, v_ = _adamw(P[n][0], Gb[n], M[n][0], V[n][0], f"adamw_{n}")
        grads[n], delta[n], new_m[n], new_v[n] = Gb[n][None], d_[None], m_[None], v_[None]
    return (loss, grad_x, *[grads[n] for n in WEIGHTS], *[delta[n] for n in WEIGHTS],
            *[new_m[n] for n in WEIGHTS], *[new_v[n] for n in WEIGHTS])
```

```python
import functools

import jax
import jax.numpy as jnp
from jax import lax
from jax.experimental import pallas as pl
from jax.experimental.pallas import tpu as pltpu
from jax.experimental.pallas import tpu_sc as plsc

F32 = jnp.float32
BF16 = jnp.bfloat16
HIGHEST = lax.Precision.HIGHEST
MESH = pl.DeviceIdType.MESH

D_MODEL = 1024
N_META = 16
EPS = 1e-6
SSD_HEADS = 16
SSD_HEAD_DIM = 64
SSD_INNER = 1024
SSD_GROUPS = 4
SSD_STATE = 128
SSD_CONV = 4
SSD_CONV_CH = 2048
HG_HEADS = 8
HG_SUB = 32
CHUNK = 128
D_FF = 2816
N_CHIPS = 4
IN_SIZES = (1024, 2048, 16, 1024, 1024, 1024, 1024, 1024, 1024)
ADAM_LR = 0.001
ADAM_B1 = 0.9
ADAM_B2 = 0.999
ADAM_EPS = 1e-08
ADAM_WD = 0.01
ADAM_STEP = 10
VMEM_LIMIT = 56 * 1024 * 1024
MATMUL_BLOCK_BYTES = 42 * 1024 * 1024


def _cparams(sem=None):
    return pltpu.CompilerParams(dimension_semantics=sem, vmem_limit_bytes=VMEM_LIMIT)


def _pick(n, cands):
    for c in cands:
        if n % c == 0:
            return c
    return n


def _deps(after):
    xs = after if isinstance(after, (list, tuple)) else [after]
    one = lambda x: lax.slice(x, (0,) * x.ndim, (1,) * x.ndim).reshape(1).astype(F32)
    return jnp.concatenate([one(x) for x in xs]).reshape(1, -1)


def _dep_spec(dep):
    return pl.BlockSpec(dep.shape, lambda *_: (0, 0))


def _skip_ref(body, pos):
    return lambda *refs: body(*refs[:pos], *refs[pos + 1:])


def _dg(a, b, ca, cb):
    return lax.dot_general(a.astype(BF16), b.astype(BF16), (((ca,), (cb,)), ((), ())), preferred_element_type=F32)


@jax.custom_vjp
def _mm(a, b):
    return _dg(a, b, 1, 0)


def _mm_fwd(a, b):
    return _dg(a, b, 1, 0), (a, b)


def _mm_bwd(r, g):
    a, b = r
    return _dg(g, b, 1, 1), _dg(a, g, 0, 0)


_mm.defvjp(_mm_fwd, _mm_bwd)


@jax.custom_vjp
def _mm_nt(a, b):
    return _dg(a, b, 1, 1)


def _mm_nt_fwd(a, b):
    return _dg(a, b, 1, 1), (a, b)


def _mm_nt_bwd(r, g):
    a, b = r
    return _dg(g, b, 1, 0), _dg(g, a, 0, 0)


_mm_nt.defvjp(_mm_nt_fwd, _mm_nt_bwd)


@jax.custom_vjp
def _mm_tn(a, b):
    return _dg(a, b, 0, 0)


def _mm_tn_fwd(a, b):
    return _dg(a, b, 0, 0), (a, b)


def _mm_tn_bwd(r, g):
    a, b = r
    return _dg(b, g, 1, 1), _dg(a, g, 1, 0)


_mm_tn.defvjp(_mm_tn_fwd, _mm_tn_bwd)


def _silu(x):
    return x * jax.nn.sigmoid(x)


def _softplus(x):
    return jnp.maximum(x, 0.0) + jnp.log(1.0 + jnp.exp(-jnp.abs(x)))


def _tril(n):
    ri = lax.broadcasted_iota(jnp.int32, (n, n), 0)
    ci = lax.broadcasted_iota(jnp.int32, (n, n), 1)
    return ri >= ci


def _row_of(m, r):
    sub = lax.broadcasted_iota(jnp.int32, (m.shape[0], 1), 0)
    return jnp.sum(jnp.where(sub == r, m, 0.0), axis=0, keepdims=True)


def _col_of(m, c):
    lane = lax.broadcasted_iota(jnp.int32, (1, m.shape[1]), 1)
    return jnp.sum(jnp.where(lane == c, m, 0.0), axis=1, keepdims=True)


def _matmul(a, b, *, mode, out_dtype, name, alpha=1.0, res=None, tm=None, tn=None, tk=None, out_groups=None):
    b3 = b.ndim == 3
    if mode == "nn":
        M, K = a.shape
        G = b.shape[0] if b3 else 1
        Ng = b.shape[-1]
        N = G * Ng
    elif mode == "nt":
        M, K = a.shape
        G = b.shape[0] if b3 else 1
        N = b.shape[-2]
        Kg = b.shape[-1]
        assert G * Kg == K
    else:
        K, M = a.shape
        N = b.shape[1]
        G = out_groups or 1
        Ng = N // G
    if mode == "tn":
        tk = tk or K
        cands = (1408, 1024, 512, 256, 128)
        fits = [(m_ * n_, m_, n_) for m_ in cands if M % m_ == 0 for n_ in cands if Ng % n_ == 0
                if 2 * (tk * m_ * a.dtype.itemsize + tk * n_ * b.dtype.itemsize + m_ * n_ * 4) <= MATMUL_BLOCK_BYTES]
        _, tm_fit, tn_fit = max(fits)
        tm, tn = tm or tm_fit, tn or tn_fit
    else:
        tm = tm or _pick(M, (1088, 544, 256, 128))
        if mode == "nn":
            tn = tn or _pick(Ng, (1408, 512, 256, 128))
            tk = K
        else:
            tn = tn or _pick(N, (1408, 512, 256, 128))
            tk = tk or (Kg if b3 else K)
    nm, nn_, nk = M // tm, N // tn, K // tk
    assert nm * tm == M and nn_ * tn == N and nk * tk == K, (name, M, N, K, tm, tn, tk)

    if mode == "nn":
        a_spec = pl.BlockSpec((tm, tk), lambda i, j, k: (i, k))
        if b3:
            ns = Ng // tn
            b_spec = pl.BlockSpec((None, tk, tn), lambda i, j, k: (j // ns, k, j % ns))
        else:
            b_spec = pl.BlockSpec((tk, tn), lambda i, j, k: (k, j))
        ca, cb = 1, 0
    elif mode == "nt":
        a_spec = pl.BlockSpec((tm, tk), lambda i, j, k: (i, k))
        if b3:
            ks = Kg // tk
            b_spec = pl.BlockSpec((None, tn, tk), lambda i, j, k: (k // ks, j, k % ks))
        else:
            b_spec = pl.BlockSpec((tn, tk), lambda i, j, k: (j, k))
        ca, cb = 1, 1
    else:
        a_spec = pl.BlockSpec((tk, tm), lambda i, j, k: (k, i))
        b_spec = pl.BlockSpec((tk, tn), lambda i, j, k: (k, j))
        ca, cb = 0, 0
    if mode == "tn" and G > 1:
        ns = Ng // tn
        o_spec = pl.BlockSpec((None, tm, tn), lambda i, j, k: (j // ns, i, j % ns))
        out_shape = jax.ShapeDtypeStruct((G, M, Ng), out_dtype)
    else:
        o_spec = pl.BlockSpec((tm, tn), lambda i, j, k: (i, j))
        out_shape = jax.ShapeDtypeStruct((M, N), out_dtype)
    in_specs = [a_spec, b_spec]
    args = [a, b]
    if res is not None:
        in_specs.append(pl.BlockSpec((tm, tn), lambda i, j, k: (i, j)))
        args.append(res)
    has_res = res is not None

    def finish(refs, o):
        if alpha != 1.0:
            o = o * alpha
        if has_res:
            o = o + refs[2][...]
        return o

    def body_one(*refs):
        o_ref = refs[-1]
        o_ref[...] = finish(refs, _dg(refs[0][...], refs[1][...], ca, cb)).astype(o_ref.dtype)

    def body_acc(*refs):
        a_ref, b_ref = refs[0], refs[1]
        o_ref, acc_ref = refs[-2], refs[-1]
        k = pl.program_id(2)

        @pl.when(k == 0)
        def _():
            acc_ref[...] = jnp.zeros_like(acc_ref)

        acc_ref[...] += _dg(a_ref[...], b_ref[...], ca, cb)

        @pl.when(k == nk - 1)
        def _():
            o_ref[...] = finish(refs, acc_ref[...]).astype(o_ref.dtype)

    return pl.pallas_call(
        body_one if nk == 1 else body_acc, grid=(nm, nn_, nk), in_specs=in_specs, out_specs=o_spec, out_shape=out_shape,
        scratch_shapes=[] if nk == 1 else [pltpu.VMEM((tm, tn), F32)], name=name,
        compiler_params=_cparams(("parallel", "parallel", "arbitrary")),
    )(*args)


def _rms_fn(h, w):
    r = lax.rsqrt(jnp.mean(h * h, axis=-1, keepdims=True) + EPS)
    return h * r * w


def _swiglu_fn(gu):
    g = gu[:, :D_FF].astype(F32)
    u = gu[:, D_FF:].astype(F32)
    return _silu(g) * u


def _merge_fn(pa, pb, gates):
    return jax.nn.sigmoid(gates[:, :D_MODEL]) * pa + jax.nn.sigmoid(gates[:, D_MODEL:]) * pb


def _rows_call(body, *, rows, tr, ins, outs, accs=(), name, after=None):
    n = rows // tr
    assert n * tr == rows
    if after is not None:
        body = _skip_ref(body, len(ins))
        ins = list(ins) + [("full", _deps(after))]

    def spec(x):
        if isinstance(x, tuple):
            shp = x[1].shape
            return pl.BlockSpec(shp, lambda i: (0,) * len(shp))
        return pl.BlockSpec((tr, x.shape[1]), lambda i: (i, 0))

    in_specs = [spec(x) for x in ins]
    args = [x[1] if isinstance(x, tuple) else x for x in ins]
    out_specs = [spec(x) for x in outs] + [pl.BlockSpec(x.shape, lambda i: (0,) * len(x.shape)) for x in accs]
    out_shape = [x[1] if isinstance(x, tuple) else x for x in outs] + list(accs)
    return pl.pallas_call(
        body, grid=(n,), in_specs=in_specs, out_specs=out_specs, out_shape=out_shape, name=name,
        compiler_params=_cparams(("arbitrary",)),
    )(*args)


def _acc_rows(ref, val):
    @pl.when(pl.program_id(0) == 0)
    def _():
        ref[...] = jnp.zeros_like(ref)

    ref[0:1, :] += val


def _rms_fwd(h, w, name):
    def body(h_ref, w_ref, o_ref):
        o_ref[...] = _rms_fn(h_ref[...], w_ref[...]).astype(o_ref.dtype)

    R = h.shape[0]
    return _rows_call(body, rows=R, tr=_pick(R, (256, 128)), ins=[h, ("full", w)],
                      outs=[jax.ShapeDtypeStruct(h.shape, BF16)], name=name)[0]


def _rms_bwd(h, w, dn, dres, name, after=None):
    def body(h_ref, w_ref, dn_ref, dres_ref, dh_ref, dw_ref):
        _, vjp = jax.vjp(_rms_fn, h_ref[...], w_ref[...])
        dh, dw = vjp(dn_ref[...].astype(F32))
        dh_ref[...] = dh + dres_ref[...]
        _acc_rows(dw_ref, dw)

    R = h.shape[0]
    return _rows_call(body, rows=R, tr=_pick(R, (256, 128)), ins=[h, ("full", w), dn, dres],
                      outs=[jax.ShapeDtypeStruct(h.shape, F32)], accs=[jax.ShapeDtypeStruct((8, D_MODEL), F32)], name=name,
                      after=after)


def _swiglu_fwd(gu, name):
    def body(gu_ref, o_ref):
        o_ref[...] = _swiglu_fn(gu_ref[...]).astype(o_ref.dtype)

    R = gu.shape[0]
    return _rows_call(body, rows=R, tr=_pick(R, (256, 128)), ins=[gu],
                      outs=[jax.ShapeDtypeStruct((R, D_FF), BF16)], name=name)[0]


def _swiglu_bwd(gu, da, name, after=None):
    def body(gu_ref, da_ref, o_ref):
        _, vjp = jax.vjp(_swiglu_fn, gu_ref[...].astype(F32))
        (dgu,) = vjp(da_ref[...].astype(F32))
        o_ref[...] = dgu.astype(o_ref.dtype)

    R = gu.shape[0]
    return _rows_call(body, rows=R, tr=_pick(R, (256, 128)), ins=[gu, da],
                      outs=[jax.ShapeDtypeStruct(gu.shape, BF16)], name=name, after=after)[0]


def _merge_fwd(pa, pb, gates, name):
    def body(pa_ref, pb_ref, g_ref, o_ref):
        o_ref[...] = _merge_fn(pa_ref[...], pb_ref[...], g_ref[...]).astype(o_ref.dtype)

    R = pa.shape[0]
    return _rows_call(body, rows=R, tr=_pick(R, (256, 128)), ins=[pa, pb, gates],
                      outs=[jax.ShapeDtypeStruct(pa.shape, BF16)], name=name)[0]


def _merge_bwd(pa, pb, gates, dm, name):
    def body(pa_ref, pb_ref, g_ref, dm_ref, dpa_ref, dpb_ref, dg_ref):
        _, vjp = jax.vjp(_merge_fn, pa_ref[...], pb_ref[...], g_ref[...])
        dpa, dpb, dg = vjp(dm_ref[...].astype(F32))
        dpa_ref[...] = dpa.astype(dpa_ref.dtype)
        dpb_ref[...] = dpb.astype(dpb_ref.dtype)
        dg_ref[...] = dg.astype(dg_ref.dtype)

    R = pa.shape[0]
    return _rows_call(body, rows=R, tr=_pick(R, (256, 128)), ins=[pa, pb, gates, dm],
                      outs=[jax.ShapeDtypeStruct(pa.shape, BF16), jax.ShapeDtypeStruct(pa.shape, BF16),
                            jax.ShapeDtypeStruct(gates.shape, BF16)], name=name)


def _loss_head(h3, w, target, nseq, name):
    Tp = h3.shape[0] // nseq
    nc = Tp // CHUNK

    def fn(h, w_, t, valid):
        y = _rms_fn(h, w_)
        e = (y - t) * valid
        return 0.5 * jnp.sum(jnp.mean(e * e, axis=-1, keepdims=True))

    def body(h_ref, w_ref, t_ref, loss_ref, dh_ref, dw_ref):
        b, c = pl.program_id(0), pl.program_id(1)
        valid = (c >= 1).astype(F32)
        t = t_ref[...]
        loss, vjp = jax.vjp(lambda h, w_: fn(h, w_, t, valid), h_ref[...], w_ref[...])
        dh, dw = vjp(jnp.ones((), F32))
        dh_ref[...] = dh

        @pl.when((b == 0) & (c == 0))
        def _():
            loss_ref[...] = jnp.zeros_like(loss_ref)
            dw_ref[...] = jnp.zeros_like(dw_ref)

        loss_ref[...] += jnp.full(loss_ref.shape, loss, F32)
        dw_ref[0:1, :] += dw

    return pl.pallas_call(
        body, grid=(nseq, nc),
        in_specs=[pl.BlockSpec((CHUNK, D_MODEL), lambda b, c: (b * nc + c, 0)),
                  pl.BlockSpec((1, D_MODEL), lambda b, c: (0, 0)),
                  pl.BlockSpec((None, CHUNK, D_MODEL), lambda b, c: (b, jnp.maximum(c - 1, 0), 0))],
        out_specs=[pl.BlockSpec((8, 128), lambda b, c: (0, 0)),
                   pl.BlockSpec((CHUNK, D_MODEL), lambda b, c: (b * nc + c, 0)),
                   pl.BlockSpec((8, D_MODEL), lambda b, c: (0, 0))],
        out_shape=[jax.ShapeDtypeStruct((8, 128), F32), jax.ShapeDtypeStruct(h3.shape, F32),
                   jax.ShapeDtypeStruct((8, D_MODEL), F32)],
        name=name, compiler_params=_cparams(("arbitrary", "arbitrary")),
    )(h3, w, target)


CONV_TILE = 512
CONV_HALO = 8


def _conv_fwd(xbc, w, b, pad, name):
    B, Tp, C = xbc.shape
    nch = Tp // CHUNK

    def body(x_ref, w_ref, b_ref, o_ref, xp):
        xp[0:CONV_HALO, :] = jnp.zeros((CONV_HALO, CONV_TILE), F32)
        xp[CONV_HALO:, :] = x_ref[...]
        for c in range(nch):
            acc = jnp.zeros((CHUNK, CONV_TILE), F32) + b_ref[...]
            for k in range(SSD_CONV):
                acc = acc + w_ref[k:k + 1, :] * xp[pl.ds(CONV_HALO + CHUNK * c - (SSD_CONV - 1) + k, CHUNK), :]
            row = CHUNK * c + lax.broadcasted_iota(jnp.int32, (CHUNK, 1), 0)
            o_ref[pl.ds(CHUNK * c, CHUNK), :] = jnp.where(row >= pad, _silu(acc), 0.0)

    return pl.pallas_call(
        body, grid=(B, C // CONV_TILE),
        in_specs=[pl.BlockSpec((None, Tp, CONV_TILE), lambda i, j: (i, 0, j)),
                  pl.BlockSpec((SSD_CONV, CONV_TILE), lambda i, j: (0, j)),
                  pl.BlockSpec((1, CONV_TILE), lambda i, j: (0, j))],
        out_specs=pl.BlockSpec((None, Tp, CONV_TILE), lambda i, j: (i, 0, j)),
        out_shape=jax.ShapeDtypeStruct(xbc.shape, F32),
        scratch_shapes=[pltpu.VMEM((Tp + CONV_HALO, CONV_TILE), F32)],
        name=name, compiler_params=_cparams(("arbitrary", "arbitrary")),
    )(xbc, w, b)


def _conv_bwd(xbc, w, b, dact, pad, name):
    B, Tp, C = xbc.shape
    nch = Tp // CHUNK

    def body(x_ref, w_ref, b_ref, da_ref, dx_ref, dw_ref, db_ref, xp, dp):
        bi = pl.program_id(1)
        xp[0:CONV_HALO, :] = jnp.zeros((CONV_HALO, CONV_TILE), F32)
        xp[CONV_HALO:, :] = x_ref[...]
        dp[pl.ds(Tp, CONV_HALO), :] = jnp.zeros((CONV_HALO, CONV_TILE), F32)
        dws = [jnp.zeros((1, CONV_TILE), F32) for _ in range(SSD_CONV)]
        dbs = jnp.zeros((1, CONV_TILE), F32)
        for c in range(nch):
            xs = [xp[pl.ds(CONV_HALO + CHUNK * c - (SSD_CONV - 1) + k, CHUNK), :] for k in range(SSD_CONV)]
            acc = jnp.zeros((CHUNK, CONV_TILE), F32) + b_ref[...]
            for k in range(SSD_CONV):
                acc = acc + w_ref[k:k + 1, :] * xs[k]
            row = CHUNK * c + lax.broadcasted_iota(jnp.int32, (CHUNK, 1), 0)
            sg = jax.nn.sigmoid(acc)
            dpre = jnp.where(row >= pad, da_ref[pl.ds(CHUNK * c, CHUNK), :] * (sg * (1.0 + acc * (1.0 - sg))), 0.0)
            dp[pl.ds(CHUNK * c, CHUNK), :] = dpre
            dbs = dbs + jnp.sum(dpre, axis=0, keepdims=True)
            for k in range(SSD_CONV):
                dws[k] = dws[k] + jnp.sum(dpre * xs[k], axis=0, keepdims=True)
        for c in range(nch):
            acc = jnp.zeros((CHUNK, CONV_TILE), F32)
            for k in range(SSD_CONV):
                acc = acc + w_ref[k:k + 1, :] * dp[pl.ds(CHUNK * c + (SSD_CONV - 1) - k, CHUNK), :]
            dx_ref[pl.ds(CHUNK * c, CHUNK), :] = acc

        @pl.when(bi == 0)
        def _():
            dw_ref[...] = jnp.zeros_like(dw_ref)
            db_ref[...] = jnp.zeros_like(db_ref)

        for k in range(SSD_CONV):
            dw_ref[k:k + 1, :] += dws[k]
        db_ref[0:1, :] += dbs

    return pl.pallas_call(
        body, grid=(C // CONV_TILE, B),
        in_specs=[pl.BlockSpec((None, Tp, CONV_TILE), lambda j, i: (i, 0, j)),
                  pl.BlockSpec((SSD_CONV, CONV_TILE), lambda j, i: (0, j)),
                  pl.BlockSpec((1, CONV_TILE), lambda j, i: (0, j)),
                  pl.BlockSpec((None, Tp, CONV_TILE), lambda j, i: (i, 0, j))],
        out_specs=[pl.BlockSpec((None, Tp, CONV_TILE), lambda j, i: (i, 0, j)),
                   pl.BlockSpec((8, CONV_TILE), lambda j, i: (0, j)),
                   pl.BlockSpec((8, CONV_TILE), lambda j, i: (0, j))],
        out_shape=[jax.ShapeDtypeStruct(xbc.shape, F32), jax.ShapeDtypeStruct((8, C), F32),
                   jax.ShapeDtypeStruct((8, C), F32)],
        scratch_shapes=[pltpu.VMEM((Tp + CONV_HALO, CONV_TILE), F32), pltpu.VMEM((Tp + CONV_HALO, CONV_TILE), F32)],
        name=name, compiler_params=_cparams(("arbitrary", "arbitrary")),
    )(xbc, w, b, dact)


def _ssd_chunk(xs, bm, cm, dtr, z, state, dt_bias, a_log, dskip, norm_w, valid):
    Q = xs.shape[0]
    lane = lax.broadcasted_iota(jnp.int32, (1, 128), 1)
    dt = jnp.where(lane < SSD_HEADS, _softplus(dtr + dt_bias), 0.0) * valid
    a = dt * (-jnp.exp(a_log))
    tril = _tril(Q)
    cs = jnp.dot(tril.astype(F32), a, precision=HIGHEST)
    cs_t = cs.T
    cs_end = _row_of(cs, Q - 1)
    low = lane < SSD_HEAD_DIM
    low_rows = lax.broadcasted_iota(jnp.int32, (128, 1), 0) < SSD_HEAD_DIM
    ys, new_state = [], []
    for g in range(SSD_GROUPS):
        bg = bm[:, 128 * g:128 * (g + 1)]
        cg = cm[:, 128 * g:128 * (g + 1)]
        cb = _mm_nt(cg, bg)
        for pr in range(2):
            p = 2 * g + pr
            h0, h1 = 2 * p, 2 * p + 1
            xp = xs[:, 128 * p:128 * (p + 1)]
            c0, c1 = _col_of(cs, h0), _col_of(cs, h1)
            e0, e1 = _col_of(cs_end, h0), _col_of(cs_end, h1)
            xd = xp * jnp.where(low, _col_of(dt, h0), _col_of(dt, h1))
            l0 = jnp.exp(jnp.where(tril, c0 - _row_of(cs_t, h0), -1e30))
            l1 = jnp.exp(jnp.where(tril, c1 - _row_of(cs_t, h1), -1e30))
            y_diag = jnp.where(low, _mm(cb * l0, xd), _mm(cb * l1, xd))
            to_end = jnp.where(low, jnp.exp(e0 - c0), jnp.exp(e1 - c1))
            sp = state[128 * p:128 * (p + 1), :]
            y_off = _mm_nt(cg, sp) * jnp.where(low, jnp.exp(c0), jnp.exp(c1))
            new_state.append(sp * jnp.where(low_rows, jnp.exp(e0), jnp.exp(e1)) + _mm_tn(xd * to_end, bg))
            ys.append(y_diag + y_off + xp * jnp.where(low, _col_of(dskip, h0), _col_of(dskip, h1)))
    y = jnp.concatenate(ys, axis=1) * _silu(z)
    gw = SSD_INNER // SSD_GROUPS
    outs = []
    for g in range(SSD_GROUPS):
        blk = y[:, gw * g:gw * (g + 1)]
        outs.append(blk * lax.rsqrt(jnp.mean(blk * blk, axis=-1, keepdims=True) + EPS))
    return jnp.concatenate(outs, axis=1) * norm_w, jnp.concatenate(new_state, axis=0)


def _valid_rows(c, pad):
    row = c * CHUNK + lax.broadcasted_iota(jnp.int32, (CHUNK, 1), 0)
    return (row >= pad).astype(F32)


def _ssd_fwd(xact, dtr, z, dt_bias, a_log, dskip, norm_w, pad, name):
    B, Tp, _ = xact.shape
    nc = Tp // CHUNK

    def body(xs_ref, bm_ref, cm_ref, dt_ref, z_ref, db_ref, al_ref, ds_ref, nw_ref, y_ref, save_ref, st):
        c = pl.program_id(1)

        @pl.when(c == 0)
        def _():
            st[...] = jnp.zeros_like(st)

        s0 = st[...]
        save_ref[...] = s0
        y, s1 = _ssd_chunk(xs_ref[...], bm_ref[...], cm_ref[...], dt_ref[...], z_ref[...], s0, db_ref[...],
                           al_ref[...], ds_ref[...], nw_ref[...], _valid_rows(c, pad))
        y_ref[...] = y.astype(y_ref.dtype)
        st[...] = s1

    row = lambda w, off=0: pl.BlockSpec((None, CHUNK, w), lambda b, c: (b, c, off))
    par = lambda w: pl.BlockSpec((1, w), lambda b, c: (0, 0))
    return pl.pallas_call(
        body, grid=(B, nc),
        in_specs=[row(1024, 0), row(512, 2), row(512, 3), row(128), row(1024), par(128), par(128), par(128), par(1024)],
        out_specs=[row(1024), pl.BlockSpec((None, None, 1024, 128), lambda b, c: (b, c, 0, 0))],
        out_shape=[jax.ShapeDtypeStruct((B, Tp, SSD_INNER), BF16), jax.ShapeDtypeStruct((B, nc, 1024, 128), F32)],
        scratch_shapes=[pltpu.VMEM((1024, 128), F32)],
        name=name, compiler_params=_cparams(("arbitrary", "arbitrary")),
    )(xact, xact, xact, dtr, z, dt_bias, a_log, dskip, norm_w)


def _ssd_bwd(xact, dtr, z, dt_bias, a_log, dskip, norm_w, saved, dy, pad, name, after=None):
    B, Tp, _ = xact.shape
    nc = Tp // CHUNK

    def body(xs_ref, bm_ref, cm_ref, dt_ref, z_ref, db_ref, al_ref, ds_ref, nw_ref, sv_ref, dy_ref,
             dx_ref, ddt_ref, dz_ref, dpar_ref, dnw_ref, dst):
        b, i = pl.program_id(0), pl.program_id(1)
        c = nc - 1 - i

        @pl.when(i == 0)
        def _():
            dst[...] = jnp.zeros_like(dst)

        valid = _valid_rows(c, pad)
        fn = lambda *a: _ssd_chunk(*a, valid)
        _, vjp = jax.vjp(fn, xs_ref[...], bm_ref[...], cm_ref[...], dt_ref[...], z_ref[...], sv_ref[...],
                         db_ref[...], al_ref[...], ds_ref[...], nw_ref[...])
        dxs, dbm, dcm, ddt, dz, dstate, ddb, dal, dds, dnw = vjp((dy_ref[...].astype(F32), dst[...]))
        dx_ref[:, 0:1024] = dxs
        dx_ref[:, 1024:1536] = dbm
        dx_ref[:, 1536:2048] = dcm
        ddt_ref[...] = ddt
        dz_ref[...] = dz
        dst[...] = dstate

        @pl.when((b == 0) & (i == 0))
        def _():
            dpar_ref[...] = jnp.zeros_like(dpar_ref)
            dnw_ref[...] = jnp.zeros_like(dnw_ref)

        dpar_ref[0:1, :] += ddb
        dpar_ref[1:2, :] += dal
        dpar_ref[2:3, :] += dds
        dnw_ref[0:1, :] += dnw

    row = lambda w, off=0: pl.BlockSpec((None, CHUNK, w), lambda b, i: (b, nc - 1 - i, off))
    par = lambda w: pl.BlockSpec((1, w), lambda b, i: (0, 0))
    acc = lambda w: pl.BlockSpec((8, w), lambda b, i: (0, 0))
    in_specs = [row(1024, 0), row(512, 2), row(512, 3), row(128), row(1024), par(128), par(128), par(128), par(1024),
                pl.BlockSpec((None, None, 1024, 128), lambda b, i: (b, nc - 1 - i, 0, 0)), row(1024)]
    args = [xact, xact, xact, dtr, z, dt_bias, a_log, dskip, norm_w, saved, dy]
    if after is not None:
        body = _skip_ref(body, len(args))
        args.append(_deps(after))
        in_specs.append(_dep_spec(args[-1]))
    outs = pl.pallas_call(
        body, grid=(B, nc), in_specs=in_specs,
        out_specs=[row(2048), row(128), row(1024), acc(128), acc(1024)],
        out_shape=[jax.ShapeDtypeStruct((B, Tp, 2048), F32), jax.ShapeDtypeStruct((B, Tp, 128), F32),
                   jax.ShapeDtypeStruct((B, Tp, 1024), F32), jax.ShapeDtypeStruct((8, 128), F32),
                   jax.ShapeDtypeStruct((8, 1024), F32)],
        scratch_shapes=[pltpu.VMEM((1024, 128), F32)],
        name=name, compiler_params=_cparams(("arbitrary", "arbitrary")),
    )(*args)
    return outs


def _hg_chunk(qr, fr, ir, gr, state_t, p0, p1, norm_w, valid):
    Q = qr.shape[0]
    lb = jax.nn.sigmoid(p0 - p1)
    f = lb + (1.0 - lb) * jax.nn.sigmoid(fr)
    k = 1.0 - f
    q = _silu(qr)
    v = ir * valid
    cum = jnp.dot(_tril(Q).astype(F32), jnp.log(f), precision=HIGHEST)
    cum_end = _row_of(cum, Q - 1)
    o_inter = _mm_nt(q * jnp.exp(cum), state_t)
    nblk = Q // HG_SUB
    row = lax.broadcasted_iota(jnp.int32, (Q, 1), 0)
    ri = lax.broadcasted_iota(jnp.int32, (Q, Q), 0)
    ci = lax.broadcasted_iota(jnp.int32, (Q, Q), 1)
    mids = jnp.concatenate([jnp.broadcast_to(_row_of(cum, HG_SUB * i + HG_SUB // 2 - 1), (HG_SUB, cum.shape[1]))
                            for i in range(nblk)], axis=0)
    sh = HG_SUB.bit_length() - 1
    same = (jnp.right_shift(ri, sh) == jnp.right_shift(ci, sh)) & (ri >= ci)
    att = jnp.where(same, _mm_nt(q * jnp.exp(cum - mids), k * jnp.exp(mids - cum)), 0.0)
    for i in range(1, nblk):
        lo = HG_SUB * i
        start = _row_of(cum, lo - 1)
        qa = q * jnp.exp(jnp.where((row >= lo) & (row < lo + HG_SUB), cum - start, -1e30))
        ka = k * jnp.exp(jnp.where(row < lo, start - cum, -1e30))
        att = att + _mm_nt(qa, ka)
    o = o_inter + _mm(att, v)
    new_state_t = state_t * jnp.exp(cum_end) + _mm_tn(v, k * jnp.exp(cum_end - cum))
    o = o * lax.rsqrt(jnp.mean(o * o, axis=-1, keepdims=True) + EPS) * norm_w
    return o * _silu(gr), new_state_t


HG_PER_STEP = 4
HG_COLS = 4 * 128


def _hg_fwd(qfig, lbh, nwh, pad, name):
    B, Tp, _ = qfig.shape
    nc = Tp // CHUNK
    hp = HG_PER_STEP

    def body(x_ref, lb_ref, nw_ref, y_ref, save_ref, st):
        c = pl.program_id(1)

        @pl.when(c == 0)
        def _():
            st[...] = jnp.zeros_like(st)

        valid = _valid_rows(c, pad)
        for j in range(hp):
            for b in range(B):
                s0 = st[j, b]
                save_ref[j, b] = s0
                col = lambda k: x_ref[b, :, HG_COLS * j + 128 * k:HG_COLS * j + 128 * (k + 1)]
                y, s1 = _hg_chunk(col(0), col(1), col(2), col(3), s0, lb_ref[j, 0:1, :], lb_ref[j, 1:2, :], nw_ref[j], valid)
                y_ref[b, :, 128 * j:128 * (j + 1)] = y.astype(y_ref.dtype)
                st[j, b] = s1

    return pl.pallas_call(
        body, grid=(HG_HEADS // hp, nc),
        in_specs=[pl.BlockSpec((B, CHUNK, HG_COLS * hp), lambda h, c: (0, c, h)),
                  pl.BlockSpec((hp, 2, 128), lambda h, c: (h, 0, 0)),
                  pl.BlockSpec((hp, 1, 128), lambda h, c: (h, 0, 0))],
        out_specs=[pl.BlockSpec((B, CHUNK, 128 * hp), lambda h, c: (0, c, h)),
                   pl.BlockSpec((hp, B, None, 128, 128), lambda h, c: (h, 0, c, 0, 0))],
        out_shape=[jax.ShapeDtypeStruct((B, Tp, 1024), BF16), jax.ShapeDtypeStruct((HG_HEADS, B, nc, 128, 128), F32)],
        scratch_shapes=[pltpu.VMEM((hp, B, 128, 128), F32)],
        name=name, compiler_params=_cparams(("arbitrary", "arbitrary")),
    )(qfig, lbh, nwh)


def _hg_bwd(qfig, lbh, nwh, saved, dy, pad, name, after=None):
    B, Tp, _ = qfig.shape
    nc = Tp // CHUNK
    hp = HG_PER_STEP

    def body(x_ref, lb_ref, nw_ref, sv_ref, dy_ref, dx_ref, dlb_ref, dnw_ref, dst):
        i = pl.program_id(1)
        c = nc - 1 - i

        @pl.when(i == 0)
        def _():
            dst[...] = jnp.zeros_like(dst)
            dlb_ref[...] = jnp.zeros_like(dlb_ref)
            dnw_ref[...] = jnp.zeros_like(dnw_ref)

        valid = _valid_rows(c, pad)
        fn = lambda *a: _hg_chunk(*a, valid)
        for j in range(hp):
            for b in range(B):
                col = lambda k: x_ref[b, :, HG_COLS * j + 128 * k:HG_COLS * j + 128 * (k + 1)]
                _, vjp = jax.vjp(fn, col(0), col(1), col(2), col(3), sv_ref[j, b], lb_ref[j, 0:1, :], lb_ref[j, 1:2, :], nw_ref[j])
                d4 = vjp((dy_ref[b, :, 128 * j:128 * (j + 1)].astype(F32), dst[j, b]))
                for k in range(4):
                    dx_ref[b, :, HG_COLS * j + 128 * k:HG_COLS * j + 128 * (k + 1)] = d4[k].astype(dx_ref.dtype)
                dst[j, b] = d4[4]
                dlb_ref[j, 0:1, :] += d4[5]
                dlb_ref[j, 1:2, :] += d4[6]
                dnw_ref[j, 0:1, :] += d4[7]

    acc = pl.BlockSpec((hp, 8, 128), lambda h, i: (h, 0, 0))
    in_specs = [pl.BlockSpec((B, CHUNK, HG_COLS * hp), lambda h, i: (0, nc - 1 - i, h)),
                pl.BlockSpec((hp, 2, 128), lambda h, i: (h, 0, 0)),
                pl.BlockSpec((hp, 1, 128), lambda h, i: (h, 0, 0)),
                pl.BlockSpec((hp, B, None, 128, 128), lambda h, i: (h, 0, nc - 1 - i, 0, 0)),
                pl.BlockSpec((B, CHUNK, 128 * hp), lambda h, i: (0, nc - 1 - i, h))]
    args = [qfig, lbh, nwh, saved, dy]
    if after is not None:
        body = _skip_ref(body, len(args))
        args.append(_deps(after))
        in_specs.append(_dep_spec(args[-1]))
    return pl.pallas_call(
        body, grid=(HG_HEADS // hp, nc), in_specs=in_specs,
        out_specs=[pl.BlockSpec((B, CHUNK, HG_COLS * hp), lambda h, i: (0, nc - 1 - i, h)), acc, acc],
        out_shape=[jax.ShapeDtypeStruct((B, Tp, 4096), BF16), jax.ShapeDtypeStruct((HG_HEADS, 8, 128), F32),
                   jax.ShapeDtypeStruct((HG_HEADS, 8, 128), F32)],
        scratch_shapes=[pltpu.VMEM((hp, B, 128, 128), F32)],
        name=name, compiler_params=_cparams(("arbitrary", "arbitrary")),
    )(*args)


def _adamw(w, g, m, v, name, after=None):
    R, C = w.shape
    tr = _pick(R, (256, 176, 128, 64, 8)) if R > 256 else R

    def body(w_ref, g_ref, m_ref, v_ref, d_ref, mo_ref, vo_ref):
        g_ = g_ref[...]
        m_ = ADAM_B1 * m_ref[...] + (1.0 - ADAM_B1) * g_
        v_ = ADAM_B2 * v_ref[...] + (1.0 - ADAM_B2) * (g_ * g_)
        m_hat = m_ / (1.0 - ADAM_B1 ** ADAM_STEP)
        v_hat = v_ / (1.0 - ADAM_B2 ** ADAM_STEP)
        d_ref[...] = -ADAM_LR * (m_hat / (jnp.sqrt(v_hat) + ADAM_EPS) + ADAM_WD * w_ref[...])
        mo_ref[...] = m_
        vo_ref[...] = v_

    sp = pl.BlockSpec((tr, C), lambda i: (i, 0))
    sh = jax.ShapeDtypeStruct((R, C), F32)
    in_specs, args = [sp] * 4, [w, g, m, v]
    if after is not None:
        body = _skip_ref(body, len(args))
        args.append(_deps(after))
        in_specs.append(_dep_spec(args[-1]))
    return pl.pallas_call(body, grid=(R // tr,), in_specs=in_specs, out_specs=[sp] * 3, out_shape=[sh] * 3,
                          name=name, compiler_params=_cparams(("arbitrary",)))(*args)


def _ffn_fwd(h, norm_w, w_gu, w_down, tag, after_norm=None):
    n = _rms_fwd(h, norm_w, f"{tag}_norm")
    if after_norm is not None:
        after_norm(n)
    gu = _matmul(n, w_gu, mode="nn", out_dtype=BF16, name=f"{tag}_gu")
    a = _swiglu_fwd(gu, f"{tag}_act")
    out = _matmul(a, w_down, mode="nn", out_dtype=F32, alpha=0.5, res=h, name=f"{tag}_down")
    return out, (n, gu, a)


def _ffn_bwd(h, norm_w, w_gu, w_down, saved, dout, tag, after_dw_down=None):
    n, gu, a = saved
    da = _matmul(dout, w_down, mode="nt", out_dtype=BF16, alpha=0.5, name=f"{tag}_d_act")
    dw_down = _matmul(a, dout, mode="tn", out_dtype=F32, alpha=0.5, name=f"{tag}_dw_down")
    dgu = _swiglu_bwd(gu, da, f"{tag}_d_gu", after=after_dw_down(dw_down) if after_dw_down else None)
    dw_gu = _matmul(n, dgu, mode="tn", out_dtype=F32, out_groups=N_CHIPS, name=f"{tag}_dw_gu")
    dn = _matmul(dgu, w_gu, mode="nt", out_dtype=F32, name=f"{tag}_d_norm")
    dh, dnw = _rms_bwd(h, norm_w, dn, dout, f"{tag}_d_in")
    return dh, dnw, dw_gu, dw_down


IN_NAMES = ("z", "xbc", "dt", "q", "f", "i", "g", "gates")


def _split_w_in(w_in_full):
    pts = [0]
    for s in IN_SIZES:
        pts.append(pts[-1] + s)
    sl = lambda i, j: w_in_full[:, pts[i]:pts[j]]
    qfig = sl(3, 7).reshape(D_MODEL, 4, HG_HEADS, 128).transpose(0, 2, 1, 3).reshape(D_MODEL, 4 * D_MODEL)
    return {"z": sl(0, 1), "xbc": sl(1, 2), "dt": jnp.pad(sl(2, 3), ((0, 0), (0, 128 - SSD_HEADS))),
            "qfig": qfig, "gates": sl(7, 9)}


def _local_step(x, target, W):
    B, S, _ = x.shape
    T = N_META + S
    pad = (-T) % CHUNK
    Tp = T + pad
    assert pad + N_META == CHUNK
    R = B * Tp
    meta = jnp.broadcast_to(W["meta_tokens"][None], (B, N_META, D_MODEL))
    h0 = jnp.concatenate([jnp.zeros((B, pad, D_MODEL), F32), meta, x], axis=1).reshape(R, D_MODEL)

    stage = W.get("_stage", lambda name, x: {})
    W = dict(W)
    h1, sv1 = _ffn_fwd(h0, W["ffn1_norm"], W["ffn1_w_gu"], W["ffn1_w_down"], "ffn1", lambda n: W.update(stage("ffn1_norm", n)))
    W.update(stage("ffn1_out", h1))
    um = _rms_fwd(h1, W["mix_norm"], "mix_norm")
    wi = W["w_in"]
    z = _matmul(um, wi["z"], mode="nn", out_dtype=F32, name="in_z")
    xbc = _matmul(um, wi["xbc"], mode="nn", out_dtype=F32, name="in_xbc")
    dtr = _matmul(um, wi["dt"], mode="nn", out_dtype=F32, name="in_dt")
    qfig = _matmul(um, wi["qfig"], mode="nn", out_dtype=F32, name="in_qfig")
    gates = _matmul(um, wi["gates"], mode="nn", out_dtype=F32, name="in_gates")

    r3 = lambda t: t.reshape(B, Tp, t.shape[-1])
    lane_pad = lambda t: jnp.pad(t, ((0, 0), (0, 128 - t.shape[1])))
    dt_bias, a_log, dskip = lane_pad(W["ssd_dt_bias"]), lane_pad(W["ssd_a_log"]), lane_pad(W["ssd_d"])
    xact = _conv_fwd(r3(xbc), W["ssd_conv_w"], W["ssd_conv_b"], pad, "conv_fwd")
    ya, ssd_saved = _ssd_fwd(xact, r3(dtr), r3(z), dt_bias, a_log, dskip, W["ssd_norm"], pad, "ssd_fwd")
    lbh = W["hg_lower_bound"].reshape(2, HG_HEADS, 128).transpose(1, 0, 2)
    nwh = W["hg_norm"].reshape(HG_HEADS, 1, 128)
    yb, hg_saved = _hg_fwd(r3(qfig), lbh, nwh, pad, "hg_fwd")
    ya2, yb2 = ya.reshape(R, -1), yb.reshape(R, -1)
    W.update(stage("mixers_out", yb2))
    pa = _matmul(ya2, W["w_branch_a"], mode="nn", out_dtype=F32, name="branch_a")
    pb = _matmul(yb2, W["w_branch_b"], mode="nn", out_dtype=F32, name="branch_b")
    mg = _merge_fwd(pa, pb, gates, "merge")
    h2 = _matmul(mg, W["w_out"], mode="nn", out_dtype=F32, res=h1, name="mix_out")
    h3, sv2 = _ffn_fwd(h2, W["ffn2_norm"], W["ffn2_w_gu"], W["ffn2_w_down"], "ffn2")

    loss, dh3, d_final = _loss_head(h3, W["final_norm"].reshape(1, D_MODEL), target, B, "loss_head")

    G = {"final_norm": d_final[0]}
    dh2, dnw, G["ffn2_w_gu"], G["ffn2_w_down"] = _ffn_bwd(h2, W["ffn2_norm"], W["ffn2_w_gu"], W["ffn2_w_down"], sv2, dh3, "ffn2")
    G["ffn2_norm"] = dnw[0:1]
    dmg = _matmul(dh2, W["w_out"], mode="nt", out_dtype=BF16, name="d_merge")
    G["w_out"] = _matmul(mg, dh2, mode="tn", out_dtype=F32, name="dw_out")
    dpa, dpb, dgates = _merge_bwd(pa, pb, gates, dmg, "merge_bwd")
    dya = _matmul(dpa, W["w_branch_a"], mode="nt", out_dtype=BF16, name="d_ya")
    dyb = _matmul(dpb, W["w_branch_b"], mode="nt", out_dtype=BF16, name="d_yb")
    G["w_branch_a"] = _matmul(ya2, dpa, mode="tn", out_dtype=F32, name="dw_branch_a")
    G["w_branch_b"] = _matmul(yb2, dpb, mode="tn", out_dtype=F32, name="dw_branch_b")

    dxact, ddtr, dz, dpar, dnw = _ssd_bwd(xact, r3(dtr), r3(z), dt_bias, a_log, dskip, W["ssd_norm"], ssd_saved,
                                          r3(dya), pad, "ssd_bwd", after=stage("late_grads", G).get("_after"))
    G["ssd_dt_bias"], G["ssd_a_log"], G["ssd_d"] = dpar[0:1, :SSD_HEADS], dpar[1:2, :SSD_HEADS], dpar[2:3, :SSD_HEADS]
    G["ssd_norm"] = dnw[0:1]
    dxbc, dcw, dcb = _conv_bwd(r3(xbc), W["ssd_conv_w"], W["ssd_conv_b"], dxact, pad, "conv_bwd")
    G["ssd_conv_w"], G["ssd_conv_b"] = dcw[0:SSD_CONV], dcb[0:1]
    dqfig, dlb, dhn = _hg_bwd(r3(qfig), lbh, nwh, hg_saved, r3(dyb), pad, "hg_bwd",
                              after=stage("after_conv_bwd", dcb).get("_after"))
    G["hg_lower_bound"] = dlb[:, 0:2, :].transpose(1, 0, 2).reshape(2, D_MODEL)
    G["hg_norm"] = dhn[:, 0, :].reshape(1, D_MODEL)

    r2 = lambda t: t.reshape(R, t.shape[-1])
    pieces = [("z", r2(dz)), ("xbc", r2(dxbc)), ("dt", r2(ddtr)), ("qfig", r2(dqfig)), ("gates", dgates)]
    dum = None
    dwi = {}
    for nm, dpiece in pieces:
        dum = _matmul(dpiece, wi[nm], mode="nt", out_dtype=F32, res=dum, name=f"d_mix_{nm}")
        dwi[nm] = _matmul(um, dpiece, mode="tn", out_dtype=F32, name=f"dw_in_{nm}")
    dw_qfig = dwi["qfig"].reshape(D_MODEL, HG_HEADS, 4, 128).transpose(0, 2, 1, 3).reshape(D_MODEL, 4 * D_MODEL)
    G["w_in"] = jnp.concatenate([dwi["z"], dwi["xbc"], dwi["dt"][:, :SSD_HEADS], dw_qfig, dwi["gates"]], axis=1)
    dh1, dnw = _rms_bwd(h1, W["mix_norm"], dum, dh2, "mix_norm_bwd", after=stage("w_in_grads", G).get("_after"))
    G["mix_norm"] = dnw[0:1]
    dh0, dnw, G["ffn1_w_gu"], G["ffn1_w_down"] = _ffn_bwd(h0, W["ffn1_norm"], W["ffn1_w_gu"], W["ffn1_w_down"], sv1, dh1, "ffn1",
                                                           lambda dw: stage("ffn1_dw_down", dw).get("_after"))
    G["ffn1_norm"] = dnw[0:1]
    dh0 = dh0.reshape(B, Tp, D_MODEL)
    G["meta_tokens"] = jnp.sum(dh0[:, pad:CHUNK], axis=0)
    return loss, dh0[:, CHUNK:], G


ANY = pl.BlockSpec(memory_space=pl.ANY)


def _place():
    return lax.axis_index("x"), lax.axis_index("y"), lax.axis_index("c")


def _other_chips(x, y):
    return [(1 - x, y), (x, 1 - y), (1 - x, 1 - y)]


def _remote(src, dst, ssem, rsem, dev):
    return pltpu.make_async_remote_copy(src_ref=src, dst_ref=dst, send_sem=ssem, recv_sem=rsem,
                                        device_id=dev, device_id_type=MESH)


def _exchange8(buf, reduce, name):
    n, w = buf.shape

    def body(x_ref, *rest):
        if reduce:
            red_ref, out_ref, ssem, rsem = rest
        else:
            out_ref, ssem, rsem = rest
        x, y, c = _place()
        me = 4 * x + 2 * y + c
        out_ref[me] = x_ref[...]
        copies = []
        for k in range(1, 8):
            px = 1 - x if (k >> 2) & 1 else x
            py = 1 - y if (k >> 1) & 1 else y
            pc = 1 - c if k & 1 else c
            cp = _remote(x_ref, out_ref.at[me], ssem.at[k - 1], rsem.at[k - 1], (px, py, pc))
            cp.start()
            copies.append((cp, 4 * px + 2 * py + pc))
        for k, (cp, peer) in enumerate(copies):
            _remote(x_ref, out_ref.at[peer], ssem.at[k], rsem.at[k], (x, y, c)).wait_recv()
        for cp, _ in copies:
            cp.wait_send()
        if reduce:
            acc = out_ref[0]
            for d in range(1, 8):
                acc = acc + out_ref[d]
            red_ref[...] = acc

    vm = pl.BlockSpec(memory_space=pltpu.VMEM)
    g_shape = jax.ShapeDtypeStruct((8, n, w), F32)
    if reduce:
        out_shape, out_specs, scratch = [jax.ShapeDtypeStruct((n, w), F32)], [vm], [pltpu.VMEM((8, n, w), F32)]
    else:
        out_shape, out_specs, scratch = [g_shape], [vm], []
    return pl.pallas_call(
        body, in_specs=[vm], out_specs=out_specs, out_shape=out_shape,
        scratch_shapes=scratch + [pltpu.SemaphoreType.DMA((7,)), pltpu.SemaphoreType.DMA((7,))], name=name,
    )(buf)[0]


HBM = pltpu.MemorySpace.HBM


def _sequencer(name, collective_id, sems, sent):
    return functools.partial(pl.kernel, mesh=plsc.ScalarSubcoreMesh(axis_name="sequencer", num_cores=1), name=name,
                             scratch_types=sems, compiler_params=pltpu.CompilerParams(collective_id=collective_id),
                             cost_estimate=pl.CostEstimate(flops=0, transcendentals=0, bytes_accessed=2 * sent,
                                                           remote_bytes_transferred=sent))


def _nbytes(arrays):
    return sum(a.size * a.dtype.itemsize for a in arrays)


def _handshake(peers):
    barrier = pltpu.get_barrier_semaphore()
    for peer in peers:
        pl.semaphore_signal(barrier, inc=1, device_id=peer, device_id_type=MESH)
    pl.semaphore_wait(barrier, len(peers))


def _gather_seq(blocks, name, collective_id):
    n = len(blocks)
    half = [s.shape[1] // 2 for s in blocks]
    full = [jax.new_ref(b, memory_space=HBM) for b in blocks]

    @_sequencer(name, collective_id, [pltpu.SemaphoreType.DMA((n, 3))] * 4, _nbytes(blocks) * 3 // 4)
    def launch(ssem, rsem, fssem, frsem):
        x, y, c = _place()
        q = 2 * x + y
        chips = _other_chips(x, y)
        _handshake([(px, py, c) for px, py in chips] + [(x, y, 1 - c)])
        piece = lambda s, qq, cc: full[s].at[qq, pl.ds(cc * half[s], half[s])]
        sends = []
        for j, (px, py) in enumerate(chips):
            for s in range(n):
                cp = _remote(piece(s, q, c), piece(s, q, c), ssem.at[s, j], rsem.at[s, j], (px, py, c))
                cp.start()
                sends.append(cp)
        for j, (px, py) in enumerate(chips):
            for s in range(n):
                got = piece(s, 2 * px + py, c)
                _remote(got, got, ssem.at[s, j], rsem.at[s, j], (px, py, c)).wait_recv()
                cp = _remote(got, got, fssem.at[s, j], frsem.at[s, j], (x, y, 1 - c))
                cp.start()
                sends.append(cp)
        for j, (px, py) in enumerate(chips):
            for s in range(n):
                got = piece(s, 2 * px + py, 1 - c)
                _remote(got, got, fssem.at[s, j], frsem.at[s, j], (x, y, 1 - c)).wait_recv()
        for cp in sends:
            cp.wait_send()

    launch()
    return [r[...] for r in full]


def _pair_swap(parts, name, collective_id):
    n = len(parts)
    half = [p.shape[1] // 2 for p in parts]
    src = [jax.new_ref(p, memory_space=HBM) for p in parts]
    got = [jax.empty_ref(jax.ShapeDtypeStruct((N_CHIPS, h, p.shape[2]), p.dtype), memory_space=HBM) for p, h in zip(parts, half)]

    @_sequencer(name, collective_id, [pltpu.SemaphoreType.DMA((n,))] * 2, _nbytes(parts) // 2)
    def launch(ssem, rsem):
        x, y, c = _place()
        _handshake([(x, y, 1 - c)])
        copies = []
        for s in range(n):
            cp = _remote(src[s].at[pl.ds(0, N_CHIPS), pl.ds((1 - c) * half[s], half[s])], got[s], ssem.at[s], rsem.at[s], (x, y, 1 - c))
            cp.start()
            copies.append(cp)
        for cp in copies:
            cp.wait_recv()
        for cp in copies:
            cp.wait_send()

    launch()
    return [g[...] for g in got]


def _to_owners(sums, name, collective_id):
    n = len(sums)
    src = [jax.new_ref(s, memory_space=HBM) for s in sums]
    got = [jax.empty_ref(jax.ShapeDtypeStruct(s.shape, s.dtype), memory_space=HBM) for s in sums]

    @_sequencer(name, collective_id, [pltpu.SemaphoreType.DMA((n, 3))] * 2, _nbytes(sums) * 3 // 4)
    def launch(ssem, rsem):
        x, y, c = _place()
        q = 2 * x + y
        chips = _other_chips(x, y)
        _handshake([(px, py, c) for px, py in chips])
        sends = []
        for j, (px, py) in enumerate(chips):
            for s in range(n):
                cp = _remote(src[s].at[2 * px + py], got[s].at[q], ssem.at[s, j], rsem.at[s, j], (px, py, c))
                cp.start()
                sends.append(cp)
        for j, (px, py) in enumerate(chips):
            for s in range(n):
                slot = got[s].at[2 * px + py]
                _remote(slot, slot, ssem.at[s, j], rsem.at[s, j], (px, py, c)).wait_recv()
        for cp in sends:
            cp.wait_send()

    launch()
    return [g[...] for g in got]


def _pair_join(blocks, name, collective_id):
    n = len(blocks)
    out = [jax.new_ref(b, memory_space=HBM) for b in blocks]

    @_sequencer(name, collective_id, [pltpu.SemaphoreType.DMA((n,))] * 2, _nbytes(blocks) // 2)
    def launch(ssem, rsem):
        x, y, c = _place()
        _handshake([(x, y, 1 - c)])
        sends = []
        for s in range(n):
            h = blocks[s].shape[0] // 2
            mine = out[s].at[pl.ds(c * h, h)]
            cp = _remote(mine, mine, ssem.at[s], rsem.at[s], (x, y, 1 - c))
            cp.start()
            sends.append(cp)
        for s in range(n):
            h = blocks[s].shape[0] // 2
            theirs = out[s].at[pl.ds((1 - c) * h, h)]
            _remote(theirs, theirs, ssem.at[s], rsem.at[s], (x, y, 1 - c)).wait_recv()
        for cp in sends:
            cp.wait_send()

    launch()
    return [o[...] for o in out]


WIRE = BF16


def _row_tile(h):
    return _pick(h, (256, 368, 352, 128, 16))


def _add_pair(part, got, c, name, after=None):
    _, h, w = got.shape
    tr = _row_tile(h)
    nt = h // tr

    def body(c_ref, p_ref, g_ref, o_ref):
        o_ref[...] = (p_ref[...] + g_ref[...].astype(F32)).astype(o_ref.dtype)

    in_specs = [pl.BlockSpec((None, tr, w), lambda q, i, c_ref: (q, c_ref[0] * nt + i, 0)),
                pl.BlockSpec((None, tr, w), lambda q, i, c_ref: (q, i, 0))]
    args = [c.reshape(1).astype(jnp.int32), part, got]
    if after is not None:
        body = _skip_ref(body, len(args))
        args.append(_deps(after))
        in_specs.append(_dep_spec(args[-1]))
    return pl.pallas_call(
        body,
        grid_spec=pltpu.PrefetchScalarGridSpec(
            num_scalar_prefetch=1, grid=(N_CHIPS, nt), in_specs=in_specs,
            out_specs=pl.BlockSpec((None, tr, w), lambda q, i, c_ref: (q, i, 0))),
        out_shape=jax.ShapeDtypeStruct(got.shape, WIRE), name=name,
        compiler_params=_cparams(("arbitrary", "arbitrary")),
    )(*args)


def _sum_chips(slots, sums, q, c, name, after=None):
    _, h, w = slots.shape
    tr = _row_tile(h)
    nt = h // tr

    def body(s_ref, mine_ref, a_ref, b_ref, d_ref, o_ref):
        o_ref[...] = ((mine_ref[...].astype(F32) + a_ref[...].astype(F32)) + b_ref[...].astype(F32)) + d_ref[...].astype(F32)

    slot = lambda k: pl.BlockSpec((None, tr, w), lambda i, s_ref: (s_ref[1 + k], i, 0))
    scalars = jnp.stack([c, q, (q + 1) % N_CHIPS, (q + 2) % N_CHIPS, (q + 3) % N_CHIPS]).astype(jnp.int32)
    in_specs, args = [slot(0), slot(1), slot(2), slot(3)], [scalars, sums, slots, slots, slots]
    if after is not None:
        body = _skip_ref(body, len(args))
        args.append(_deps(after))
        in_specs.append(_dep_spec(args[-1]))
    return pl.pallas_call(
        body,
        grid_spec=pltpu.PrefetchScalarGridSpec(
            num_scalar_prefetch=1, grid=(nt,), in_specs=in_specs,
            out_specs=pl.BlockSpec((tr, w), lambda i, s_ref: (s_ref[0] * nt + i, 0))),
        out_shape=jax.ShapeDtypeStruct((2 * h, w), F32), name=name,
        compiler_params=_cparams(("arbitrary",)),
    )(*args)


class _Reduce:
    def __init__(self, parts, q, c, tag, first_id):
        self.parts, self.q, self.c, self.tag, self.first_id = parts, q, c, tag, first_id
        self.got = _pair_swap(parts, f"{tag}_pair_swap", first_id)

    def to_owners(self, after=None):
        self.sums = [_add_pair(p, g, self.c, f"{self.tag}_pair_add{i}", after)
                     for i, (p, g) in enumerate(zip(self.parts, self.got))]
        self.slots = _to_owners(self.sums, f"{self.tag}_to_owners", self.first_id + 1)
        return self.sums

    def join(self, after=None):
        blocks = [_sum_chips(sl, sm, self.q, self.c, f"{self.tag}_sum_chips{i}", after)
                  for i, (sl, sm) in enumerate(zip(self.slots, self.sums))]
        self.out = _pair_join(blocks, f"{self.tag}_pair_join", self.first_id + 2)
        return blocks


WEIGHTS = ("meta_tokens", "ffn1_norm", "ffn1_w_gu", "ffn1_w_down", "mix_norm", "w_in", "ssd_conv_w", "ssd_conv_b",
           "ssd_dt_bias", "ssd_a_log", "ssd_d", "ssd_norm", "hg_lower_bound", "hg_norm", "w_branch_a", "w_branch_b",
           "w_out", "ffn2_norm", "ffn2_w_gu", "ffn2_w_down", "final_norm")
BIG = ("ffn1_w_gu", "ffn1_w_down", "w_in", "w_branch_a", "w_branch_b", "w_out", "ffn2_w_gu", "ffn2_w_down")
ROW_SHARDED = ("ffn1_w_down", "ffn2_w_down", "w_branch_a", "w_branch_b", "w_out")
SMALL = tuple(n for n in WEIGHTS if n not in BIG)
SMALL_ROWS = 24


def _rows1024(a):
    flat = a.reshape(-1)
    n = -(-flat.shape[0] // 1024) * 1024
    return jnp.pad(flat, (0, n - flat.shape[0])).reshape(-1, 1024)


def _pack_small(d):
    rows = jnp.concatenate([_rows1024(d[n]) for n in SMALL], axis=0)
    return jnp.pad(rows, ((0, SMALL_ROWS - rows.shape[0]), (0, 0)))


def _unpack_small(packed, like):
    out, r = {}, 0
    for n in SMALL:
        size = like[n].size
        nr = -(-size // 1024)
        out[n] = packed[r:r + nr].reshape(-1)[:size].reshape(like[n].shape)
        r += nr
    return out


def kernel(x, meta_tokens, ffn1_norm, ffn1_w_gu, ffn1_w_down, mix_norm, w_in, ssd_conv_w, ssd_conv_b, ssd_dt_bias, ssd_a_log, ssd_d, ssd_norm, hg_lower_bound, hg_norm, w_branch_a, w_branch_b, w_out, ffn2_norm, ffn2_w_gu, ffn2_w_down, final_norm, loss_target, m_meta_tokens, m_ffn1_norm, m_ffn1_w_gu, m_ffn1_w_down, m_mix_norm, m_w_in, m_ssd_conv_w, m_ssd_conv_b, m_ssd_dt_bias, m_ssd_a_log, m_ssd_d, m_ssd_norm, m_hg_lower_bound, m_hg_norm, m_w_branch_a, m_w_branch_b, m_w_out, m_ffn2_norm, m_ffn2_w_gu, m_ffn2_w_down, m_final_norm, v_meta_tokens, v_ffn1_norm, v_ffn1_w_gu, v_ffn1_w_down, v_mix_norm, v_w_in, v_ssd_conv_w, v_ssd_conv_b, v_ssd_dt_bias, v_ssd_a_log, v_ssd_d, v_ssd_norm, v_hg_lower_bound, v_hg_norm, v_w_branch_a, v_w_branch_b, v_w_out, v_ffn2_norm, v_ffn2_w_gu, v_ffn2_w_down, v_final_norm):
    P = dict(zip(WEIGHTS, (meta_tokens, ffn1_norm, ffn1_w_gu, ffn1_w_down, mix_norm, w_in, ssd_conv_w, ssd_conv_b, ssd_dt_bias, ssd_a_log, ssd_d, ssd_norm, hg_lower_bound, hg_norm, w_branch_a, w_branch_b, w_out, ffn2_norm, ffn2_w_gu, ffn2_w_down, final_norm)))
    M = dict(zip(WEIGHTS, (m_meta_tokens, m_ffn1_norm, m_ffn1_w_gu, m_ffn1_w_down, m_mix_norm, m_w_in, m_ssd_conv_w, m_ssd_conv_b, m_ssd_dt_bias, m_ssd_a_log, m_ssd_d, m_ssd_norm, m_hg_lower_bound, m_hg_norm, m_w_branch_a, m_w_branch_b, m_w_out, m_ffn2_norm, m_ffn2_w_gu, m_ffn2_w_down, m_final_norm)))
    V = dict(zip(WEIGHTS, (v_meta_tokens, v_ffn1_norm, v_ffn1_w_gu, v_ffn1_w_down, v_mix_norm, v_w_in, v_ssd_conv_w, v_ssd_conv_b, v_ssd_dt_bias, v_ssd_a_log, v_ssd_d, v_ssd_norm, v_hg_lower_bound, v_hg_norm, v_w_branch_a, v_w_branch_b, v_w_out, v_ffn2_norm, v_ffn2_w_gu, v_ffn2_w_down, v_final_norm)))
    cx, cy, cc = _place()
    q = 2 * cx + cy

    mine = jnp.concatenate([meta_tokens.reshape(4, 1024), ssd_conv_w.reshape(2, 1024), jnp.zeros((2, 1024), F32)], axis=0)
    every = _exchange8(mine, False, "gather_small")
    meta_full = jnp.concatenate([every[2 * k, 0:4].reshape(N_META, 256) for k in range(N_CHIPS)], axis=1)
    conv_w_full = jnp.concatenate([every[2 * k, 4:6].reshape(SSD_CONV, 512) for k in range(N_CHIPS)], axis=1)

    late = ("ffn2_w_down", "w_branch_a", "w_branch_b", "w_out")
    rows = jnp.concatenate([P[n][0] for n in late], axis=0)
    zero = lambda t, dtype=F32: (t[0:1, 0:1] * 0).astype(dtype)

    def in_slot(s, after=None):
        s = s if after is None else s + zero(after)
        return lax.dynamic_update_slice(lax.empty((N_CHIPS,) + s.shape, BF16), s.astype(BF16)[None], (q, 0, 0))

    gu1, down1 = _gather_seq([in_slot(ffn1_w_gu[0]), in_slot(ffn1_w_down[0])], "gather_ffn1", 1)
    W = {n: P[n] for n in SMALL}
    W["meta_tokens"], W["ssd_conv_w"] = meta_full, conv_w_full
    W["ffn1_w_gu"], W["ffn1_w_down"] = gu1, down1.reshape(-1, D_MODEL)
    flying = {}

    def stage(name, t):
        if name == "ffn1_norm":
            flying["w_in"] = _gather_seq([in_slot(w_in[0], t)], "gather_w_in", 2)
            return {}
        if name == "ffn1_out":
            flying["late"] = _gather_seq([in_slot(ffn2_w_gu[0], t), in_slot(rows, t)], "gather_late", 3)
            (w_in_all,) = flying["w_in"]
            w_in_all = w_in_all + zero(t, BF16)
            return {"w_in": _split_w_in(w_in_all.transpose(1, 0, 2).reshape(D_MODEL, -1))}
        if name == "mixers_out":
            gu2, rows_all = flying["late"]
            out, r = {"ffn2_w_gu": gu2}, 0
            for n in late:
                nr = P[n].shape[1]
                out[n] = (rows_all[:, r:r + nr] + zero(t, BF16)).reshape(N_CHIPS * nr, D_MODEL)
                r += nr
            return out
        if name == "late_grads":
            row_parts = jnp.concatenate([t[n].reshape(N_CHIPS, -1, D_MODEL) for n in late], axis=1)
            flying["grad_late"] = _Reduce([t["ffn2_w_gu"], row_parts], q, cc, "grad_late", 4)
            return {"_after": [t["ffn2_w_gu"]] + [t[n] for n in late]}
        if name == "after_conv_bwd":
            return {"_after": flying["grad_late"].to_owners(after=t)}
        if name == "w_in_grads":
            blocks = flying["grad_late"].join(after=t["w_in"])
            w_in_parts = t["w_in"].reshape(D_MODEL, N_CHIPS, -1).transpose(1, 0, 2)
            flying["grad_w_in"] = _Reduce([w_in_parts], q, cc, "grad_w_in", 7)
            return {"_after": blocks}
        if name == "ffn1_dw_down":
            return {"_after": flying["grad_w_in"].to_owners(after=t)}
        return {}

    W["_stage"] = stage

    loss8, grad_x, G = _local_step(x, loss_target, W)

    small = jnp.concatenate(
        [G["meta_tokens"]] + [_rows1024(G[n]) for n in SMALL if n != "meta_tokens"] + [_rows1024(loss8[0:1, 0:1])], axis=0)
    small = jnp.pad(small, ((0, 40 - small.shape[0]), (0, 0)))
    small = _exchange8(small, True, "reduce_small")
    Gs = {"meta_tokens": small[0:N_META]}
    r = N_META
    for n in SMALL:
        if n == "meta_tokens":
            continue
        nr = -(-G[n].size // 1024)
        Gs[n] = small[r:r + nr].reshape(-1)[:G[n].size].reshape(G[n].shape)
        r += nr
    loss = small[r, 0]
    Gs["meta_tokens"] = lax.dynamic_slice(Gs["meta_tokens"], (0, 256 * q), (N_META, 256))
    Gs["ssd_conv_w"] = lax.dynamic_slice(Gs["ssd_conv_w"], (0, 512 * q), (SSD_CONV, 512))[None]
    Gs = {n: Gs[n].reshape(P[n].shape) for n in SMALL}

    grad_ffn1 = _Reduce([G["ffn1_w_gu"], G["ffn1_w_down"].reshape(N_CHIPS, -1, D_MODEL)], q, cc, "grad_ffn1", 10)
    flying["grad_w_in"].join(after=grad_x)
    going = grad_ffn1.to_owners(after=grad_x)
    g_gu2, g_rows = flying["grad_late"].out
    (g_w_in,) = flying["grad_w_in"].out
    Gb = {"ffn2_w_gu": g_gu2, "w_in": g_w_in}
    r = 0
    for n in late:
        nr = P[n].shape[1]
        Gb[n] = g_rows[r:r + nr]
        r += nr

    grads, delta, new_m, new_v = dict(Gs), {}, {}, {}
    d_s, m_s, v_s = _adamw(_pack_small(P), _pack_small(Gs), _pack_small(M), _pack_small(V), "adamw_small", after=going)
    delta.update(_unpack_small(d_s, P))
    new_m.update(_unpack_small(m_s, P))
    new_v.update(_unpack_small(v_s, P))
    done = [d_s]
    for n in [n for n in BIG if n in Gb]:
        d_, m_, v_ = _adamw(P[n][0], Gb[n], M[n][0], V[n][0], f"adamw_{n}", after=going)
        grads[n], delta[n], new_m[n], new_v[n] = Gb[n][None], d_[None], m_[None], v_[None]
        done.append(d_)
    grad_ffn1.join(after=done)
    Gb["ffn1_w_gu"], Gb["ffn1_w_down"] = grad_ffn1.out
    for n in ("ffn1_w_gu", "ffn1_w_down"):
        d_, m_, v_ = _adamw(P[n][0], Gb[n], M[n][0], V[n][0], f"adamw_{n}")
        grads[n], delta[n], new_m[n], new_v[n] = Gb[n][None], d_[None], m_[None], v_[None]
    return (loss, grad_x, *[grads[n] for n in WEIGHTS], *[delta[n] for n in WEIGHTS],
            *[new_m[n] for n in WEIGHTS], *[new_v[n] for n in WEIGHTS])
```

```python
import functools

import jax
import jax.numpy as jnp
from jax import lax
from jax.experimental import pallas as pl
from jax.experimental.pallas import tpu as pltpu
from jax.experimental.pallas import tpu_sc as plsc

F32 = jnp.float32
BF16 = jnp.bfloat16
HIGHEST = lax.Precision.HIGHEST
MESH = pl.DeviceIdType.MESH

D_MODEL = 1024
N_META = 16
EPS = 1e-6
SSD_HEADS = 16
SSD_HEAD_DIM = 64
SSD_INNER = 1024
SSD_GROUPS = 4
SSD_STATE = 128
SSD_CONV = 4
SSD_CONV_CH = 2048
HG_HEADS = 8
HG_SUB = 32
CHUNK = 128
D_FF = 2816
N_CHIPS = 4
IN_SIZES = (1024, 2048, 16, 1024, 1024, 1024, 1024, 1024, 1024)
ADAM_LR = 0.001
ADAM_B1 = 0.9
ADAM_B2 = 0.999
ADAM_EPS = 1e-08
ADAM_WD = 0.01
ADAM_STEP = 10
VMEM_LIMIT = 56 * 1024 * 1024
MATMUL_BLOCK_BYTES = 42 * 1024 * 1024


def _cparams(sem=None):
    return pltpu.CompilerParams(dimension_semantics=sem, vmem_limit_bytes=VMEM_LIMIT)


def _pick(n, cands):
    for c in cands:
        if n % c == 0:
            return c
    return n


def _deps(after):
    xs = after if isinstance(after, (list, tuple)) else [after]
    one = lambda x: lax.slice(x, (0,) * x.ndim, (1,) * x.ndim).reshape(1).astype(F32)
    return jnp.concatenate([one(x) for x in xs]).reshape(1, -1)


def _dep_spec(dep):
    return pl.BlockSpec(dep.shape, lambda *_: (0, 0))


def _skip_ref(body, pos):
    return lambda *refs: body(*refs[:pos], *refs[pos + 1:])


def _dg(a, b, ca, cb):
    return lax.dot_general(a.astype(BF16), b.astype(BF16), (((ca,), (cb,)), ((), ())), preferred_element_type=F32)


@jax.custom_vjp
def _mm(a, b):
    return _dg(a, b, 1, 0)


def _mm_fwd(a, b):
    return _dg(a, b, 1, 0), (a, b)


def _mm_bwd(r, g):
    a, b = r
    return _dg(g, b, 1, 1), _dg(a, g, 0, 0)


_mm.defvjp(_mm_fwd, _mm_bwd)


@jax.custom_vjp
def _mm_nt(a, b):
    return _dg(a, b, 1, 1)


def _mm_nt_fwd(a, b):
    return _dg(a, b, 1, 1), (a, b)


def _mm_nt_bwd(r, g):
    a, b = r
    return _dg(g, b, 1, 0), _dg(g, a, 0, 0)


_mm_nt.defvjp(_mm_nt_fwd, _mm_nt_bwd)


@jax.custom_vjp
def _mm_tn(a, b):
    return _dg(a, b, 0, 0)


def _mm_tn_fwd(a, b):
    return _dg(a, b, 0, 0), (a, b)


def _mm_tn_bwd(r, g):
    a, b = r
    return _dg(b, g, 1, 1), _dg(a, g, 1, 0)


_mm_tn.defvjp(_mm_tn_fwd, _mm_tn_bwd)


def _silu(x):
    return x * jax.nn.sigmoid(x)


def _softplus(x):
    return jnp.maximum(x, 0.0) + jnp.log(1.0 + jnp.exp(-jnp.abs(x)))


def _tril(n):
    ri = lax.broadcasted_iota(jnp.int32, (n, n), 0)
    ci = lax.broadcasted_iota(jnp.int32, (n, n), 1)
    return ri >= ci


def _row_of(m, r):
    sub = lax.broadcasted_iota(jnp.int32, (m.shape[0], 1), 0)
    return jnp.sum(jnp.where(sub == r, m, 0.0), axis=0, keepdims=True)


def _col_of(m, c):
    lane = lax.broadcasted_iota(jnp.int32, (1, m.shape[1]), 1)
    return jnp.sum(jnp.where(lane == c, m, 0.0), axis=1, keepdims=True)


def _matmul(a, b, *, mode, out_dtype, name, alpha=1.0, res=None, tm=None, tn=None, tk=None, out_groups=None):
    b3 = b.ndim == 3
    if mode == "nn":
        M, K = a.shape
        G = b.shape[0] if b3 else 1
        Ng = b.shape[-1]
        N = G * Ng
    elif mode == "nt":
        M, K = a.shape
        G = b.shape[0] if b3 else 1
        N = b.shape[-2]
        Kg = b.shape[-1]
        assert G * Kg == K
    else:
        K, M = a.shape
        N = b.shape[1]
        G = out_groups or 1
        Ng = N // G
    if mode == "tn":
        tk = tk or K
        cands = (1408, 1024, 512, 256, 128)
        fits = [(m_ * n_, m_, n_) for m_ in cands if M % m_ == 0 for n_ in cands if Ng % n_ == 0
                if 2 * (tk * m_ * a.dtype.itemsize + tk * n_ * b.dtype.itemsize + m_ * n_ * 4) <= MATMUL_BLOCK_BYTES]
        _, tm_fit, tn_fit = max(fits)
        tm, tn = tm or tm_fit, tn or tn_fit
    else:
        tm = tm or _pick(M, (1088, 544, 256, 128))
        if mode == "nn":
            tn = tn or _pick(Ng, (1408, 512, 256, 128))
            tk = K
        else:
            tn = tn or _pick(N, (1408, 512, 256, 128))
            tk = tk or (Kg if b3 else K)
    nm, nn_, nk = M // tm, N // tn, K // tk
    assert nm * tm == M and nn_ * tn == N and nk * tk == K, (name, M, N, K, tm, tn, tk)

    if mode == "nn":
        a_spec = pl.BlockSpec((tm, tk), lambda i, j, k: (i, k))
        if b3:
            ns = Ng // tn
            b_spec = pl.BlockSpec((None, tk, tn), lambda i, j, k: (j // ns, k, j % ns))
        else:
            b_spec = pl.BlockSpec((tk, tn), lambda i, j, k: (k, j))
        ca, cb = 1, 0
    elif mode == "nt":
        a_spec = pl.BlockSpec((tm, tk), lambda i, j, k: (i, k))
        if b3:
            ks = Kg // tk
            b_spec = pl.BlockSpec((None, tn, tk), lambda i, j, k: (k // ks, j, k % ks))
        else:
            b_spec = pl.BlockSpec((tn, tk), lambda i, j, k: (j, k))
        ca, cb = 1, 1
    else:
        a_spec = pl.BlockSpec((tk, tm), lambda i, j, k: (k, i))
        b_spec = pl.BlockSpec((tk, tn), lambda i, j, k: (k, j))
        ca, cb = 0, 0
    if mode == "tn" and G > 1:
        ns = Ng // tn
        o_spec = pl.BlockSpec((None, tm, tn), lambda i, j, k: (j // ns, i, j % ns))
        out_shape = jax.ShapeDtypeStruct((G, M, Ng), out_dtype)
    else:
        o_spec = pl.BlockSpec((tm, tn), lambda i, j, k: (i, j))
        out_shape = jax.ShapeDtypeStruct((M, N), out_dtype)
    in_specs = [a_spec, b_spec]
    args = [a, b]
    if res is not None:
        in_specs.append(pl.BlockSpec((tm, tn), lambda i, j, k: (i, j)))
        args.append(res)
    has_res = res is not None

    def finish(refs, o):
        if alpha != 1.0:
            o = o * alpha
        if has_res:
            o = o + refs[2][...]
        return o

    def body_one(*refs):
        o_ref = refs[-1]
        o_ref[...] = finish(refs, _dg(refs[0][...], refs[1][...], ca, cb)).astype(o_ref.dtype)

    def body_acc(*refs):
        a_ref, b_ref = refs[0], refs[1]
        o_ref, acc_ref = refs[-2], refs[-1]
        k = pl.program_id(2)

        @pl.when(k == 0)
        def _():
            acc_ref[...] = jnp.zeros_like(acc_ref)

        acc_ref[...] += _dg(a_ref[...], b_ref[...], ca, cb)

        @pl.when(k == nk - 1)
        def _():
            o_ref[...] = finish(refs, acc_ref[...]).astype(o_ref.dtype)

    return pl.pallas_call(
        body_one if nk == 1 else body_acc, grid=(nm, nn_, nk), in_specs=in_specs, out_specs=o_spec, out_shape=out_shape,
        scratch_shapes=[] if nk == 1 else [pltpu.VMEM((tm, tn), F32)], name=name,
        compiler_params=_cparams(("parallel", "parallel", "arbitrary")),
    )(*args)


def _rms_fn(h, w):
    r = lax.rsqrt(jnp.mean(h * h, axis=-1, keepdims=True) + EPS)
    return h * r * w


def _swiglu_fn(gu):
    g = gu[:, :D_FF].astype(F32)
    u = gu[:, D_FF:].astype(F32)
    return _silu(g) * u


def _merge_fn(pa, pb, gates):
    return jax.nn.sigmoid(gates[:, :D_MODEL]) * pa + jax.nn.sigmoid(gates[:, D_MODEL:]) * pb


def _rows_call(body, *, rows, tr, ins, outs, accs=(), name, after=None):
    n = rows // tr
    assert n * tr == rows
    if after is not None:
        body = _skip_ref(body, len(ins))
        ins = list(ins) + [("full", _deps(after))]

    def spec(x):
        if isinstance(x, tuple):
            shp = x[1].shape
            return pl.BlockSpec(shp, lambda i: (0,) * len(shp))
        return pl.BlockSpec((tr, x.shape[1]), lambda i: (i, 0))

    in_specs = [spec(x) for x in ins]
    args = [x[1] if isinstance(x, tuple) else x for x in ins]
    out_specs = [spec(x) for x in outs] + [pl.BlockSpec(x.shape, lambda i: (0,) * len(x.shape)) for x in accs]
    out_shape = [x[1] if isinstance(x, tuple) else x for x in outs] + list(accs)
    return pl.pallas_call(
        body, grid=(n,), in_specs=in_specs, out_specs=out_specs, out_shape=out_shape, name=name,
        compiler_params=_cparams(("arbitrary",)),
    )(*args)


def _acc_rows(ref, val):
    @pl.when(pl.program_id(0) == 0)
    def _():
        ref[...] = jnp.zeros_like(ref)

    ref[0:1, :] += val


def _rms_fwd(h, w, name):
    def body(h_ref, w_ref, o_ref):
        o_ref[...] = _rms_fn(h_ref[...], w_ref[...]).astype(o_ref.dtype)

    R = h.shape[0]
    return _rows_call(body, rows=R, tr=_pick(R, (256, 128)), ins=[h, ("full", w)],
                      outs=[jax.ShapeDtypeStruct(h.shape, BF16)], name=name)[0]


def _rms_bwd(h, w, dn, dres, name, after=None):
    def body(h_ref, w_ref, dn_ref, dres_ref, dh_ref, dw_ref):
        _, vjp = jax.vjp(_rms_fn, h_ref[...], w_ref[...])
        dh, dw = vjp(dn_ref[...].astype(F32))
        dh_ref[...] = dh + dres_ref[...]
        _acc_rows(dw_ref, dw)

    R = h.shape[0]
    return _rows_call(body, rows=R, tr=_pick(R, (256, 128)), ins=[h, ("full", w), dn, dres],
                      outs=[jax.ShapeDtypeStruct(h.shape, F32)], accs=[jax.ShapeDtypeStruct((8, D_MODEL), F32)], name=name,
                      after=after)


def _swiglu_fwd(gu, name):
    def body(gu_ref, o_ref):
        o_ref[...] = _swiglu_fn(gu_ref[...]).astype(o_ref.dtype)

    R = gu.shape[0]
    return _rows_call(body, rows=R, tr=_pick(R, (256, 128)), ins=[gu],
                      outs=[jax.ShapeDtypeStruct((R, D_FF), BF16)], name=name)[0]


def _swiglu_bwd(gu, da, name, after=None):
    def body(gu_ref, da_ref, o_ref):
        _, vjp = jax.vjp(_swiglu_fn, gu_ref[...].astype(F32))
        (dgu,) = vjp(da_ref[...].astype(F32))
        o_ref[...] = dgu.astype(o_ref.dtype)

    R = gu.shape[0]
    return _rows_call(body, rows=R, tr=_pick(R, (256, 128)), ins=[gu, da],
                      outs=[jax.ShapeDtypeStruct(gu.shape, BF16)], name=name, after=after)[0]


def _merge_fwd(pa, pb, gates, name):
    def body(pa_ref, pb_ref, g_ref, o_ref):
        o_ref[...] = _merge_fn(pa_ref[...], pb_ref[...], g_ref[...]).astype(o_ref.dtype)

    R = pa.shape[0]
    return _rows_call(body, rows=R, tr=_pick(R, (256, 128)), ins=[pa, pb, gates],
                      outs=[jax.ShapeDtypeStruct(pa.shape, BF16)], name=name)[0]


def _merge_bwd(pa, pb, gates, dm, name):
    def body(pa_ref, pb_ref, g_ref, dm_ref, dpa_ref, dpb_ref, dg_ref):
        _, vjp = jax.vjp(_merge_fn, pa_ref[...], pb_ref[...], g_ref[...])
        dpa, dpb, dg = vjp(dm_ref[...].astype(F32))
        dpa_ref[...] = dpa.astype(dpa_ref.dtype)
        dpb_ref[...] = dpb.astype(dpb_ref.dtype)
        dg_ref[...] = dg.astype(dg_ref.dtype)

    R = pa.shape[0]
    return _rows_call(body, rows=R, tr=_pick(R, (256, 128)), ins=[pa, pb, gates, dm],
                      outs=[jax.ShapeDtypeStruct(pa.shape, BF16), jax.ShapeDtypeStruct(pa.shape, BF16),
                            jax.ShapeDtypeStruct(gates.shape, BF16)], name=name)


def _loss_head(h3, w, target, nseq, name):
    Tp = h3.shape[0] // nseq
    nc = Tp // CHUNK

    def fn(h, w_, t, valid):
        y = _rms_fn(h, w_)
        e = (y - t) * valid
        return 0.5 * jnp.sum(jnp.mean(e * e, axis=-1, keepdims=True))

    def body(h_ref, w_ref, t_ref, loss_ref, dh_ref, dw_ref):
        b, c = pl.program_id(0), pl.program_id(1)
        valid = (c >= 1).astype(F32)
        t = t_ref[...]
        loss, vjp = jax.vjp(lambda h, w_: fn(h, w_, t, valid), h_ref[...], w_ref[...])
        dh, dw = vjp(jnp.ones((), F32))
        dh_ref[...] = dh

        @pl.when((b == 0) & (c == 0))
        def _():
            loss_ref[...] = jnp.zeros_like(loss_ref)
            dw_ref[...] = jnp.zeros_like(dw_ref)

        loss_ref[...] += jnp.full(loss_ref.shape, loss, F32)
        dw_ref[0:1, :] += dw

    return pl.pallas_call(
        body, grid=(nseq, nc),
        in_specs=[pl.BlockSpec((CHUNK, D_MODEL), lambda b, c: (b * nc + c, 0)),
                  pl.BlockSpec((1, D_MODEL), lambda b, c: (0, 0)),
                  pl.BlockSpec((None, CHUNK, D_MODEL), lambda b, c: (b, jnp.maximum(c - 1, 0), 0))],
        out_specs=[pl.BlockSpec((8, 128), lambda b, c: (0, 0)),
                   pl.BlockSpec((CHUNK, D_MODEL), lambda b, c: (b * nc + c, 0)),
                   pl.BlockSpec((8, D_MODEL), lambda b, c: (0, 0))],
        out_shape=[jax.ShapeDtypeStruct((8, 128), F32), jax.ShapeDtypeStruct(h3.shape, F32),
                   jax.ShapeDtypeStruct((8, D_MODEL), F32)],
        name=name, compiler_params=_cparams(("arbitrary", "arbitrary")),
    )(h3, w, target)


CONV_TILE = 512
CONV_HALO = 8


def _conv_fwd(xbc, w, b, pad, name):
    B, Tp, C = xbc.shape
    nch = Tp // CHUNK

    def body(x_ref, w_ref, b_ref, o_ref, xp):
        xp[0:CONV_HALO, :] = jnp.zeros((CONV_HALO, CONV_TILE), F32)
        xp[CONV_HALO:, :] = x_ref[...]
        for c in range(nch):
            acc = jnp.zeros((CHUNK, CONV_TILE), F32) + b_ref[...]
            for k in range(SSD_CONV):
                acc = acc + w_ref[k:k + 1, :] * xp[pl.ds(CONV_HALO + CHUNK * c - (SSD_CONV - 1) + k, CHUNK), :]
            row = CHUNK * c + lax.broadcasted_iota(jnp.int32, (CHUNK, 1), 0)
            o_ref[pl.ds(CHUNK * c, CHUNK), :] = jnp.where(row >= pad, _silu(acc), 0.0)

    return pl.pallas_call(
        body, grid=(B, C // CONV_TILE),
        in_specs=[pl.BlockSpec((None, Tp, CONV_TILE), lambda i, j: (i, 0, j)),
                  pl.BlockSpec((SSD_CONV, CONV_TILE), lambda i, j: (0, j)),
                  pl.BlockSpec((1, CONV_TILE), lambda i, j: (0, j))],
        out_specs=pl.BlockSpec((None, Tp, CONV_TILE), lambda i, j: (i, 0, j)),
        out_shape=jax.ShapeDtypeStruct(xbc.shape, F32),
        scratch_shapes=[pltpu.VMEM((Tp + CONV_HALO, CONV_TILE), F32)],
        name=name, compiler_params=_cparams(("arbitrary", "arbitrary")),
    )(xbc, w, b)


def _conv_bwd(xbc, w, b, dact, pad, name):
    B, Tp, C = xbc.shape
    nch = Tp // CHUNK

    def body(x_ref, w_ref, b_ref, da_ref, dx_ref, dw_ref, db_ref, xp, dp):
        bi = pl.program_id(1)
        xp[0:CONV_HALO, :] = jnp.zeros((CONV_HALO, CONV_TILE), F32)
        xp[CONV_HALO:, :] = x_ref[...]
        dp[pl.ds(Tp, CONV_HALO), :] = jnp.zeros((CONV_HALO, CONV_TILE), F32)
        dws = [jnp.zeros((1, CONV_TILE), F32) for _ in range(SSD_CONV)]
        dbs = jnp.zeros((1, CONV_TILE), F32)
        for c in range(nch):
            xs = [xp[pl.ds(CONV_HALO + CHUNK * c - (SSD_CONV - 1) + k, CHUNK), :] for k in range(SSD_CONV)]
            acc = jnp.zeros((CHUNK, CONV_TILE), F32) + b_ref[...]
            for k in range(SSD_CONV):
                acc = acc + w_ref[k:k + 1, :] * xs[k]
            row = CHUNK * c + lax.broadcasted_iota(jnp.int32, (CHUNK, 1), 0)
            sg = jax.nn.sigmoid(acc)
            dpre = jnp.where(row >= pad, da_ref[pl.ds(CHUNK * c, CHUNK), :] * (sg * (1.0 + acc * (1.0 - sg))), 0.0)
            dp[pl.ds(CHUNK * c, CHUNK), :] = dpre
            dbs = dbs + jnp.sum(dpre, axis=0, keepdims=True)
            for k in range(SSD_CONV):
                dws[k] = dws[k] + jnp.sum(dpre * xs[k], axis=0, keepdims=True)
        for c in range(nch):
            acc = jnp.zeros((CHUNK, CONV_TILE), F32)
            for k in range(SSD_CONV):
                acc = acc + w_ref[k:k + 1, :] * dp[pl.ds(CHUNK * c + (SSD_CONV - 1) - k, CHUNK), :]
            dx_ref[pl.ds(CHUNK * c, CHUNK), :] = acc

        @pl.when(bi == 0)
        def _():
            dw_ref[...] = jnp.zeros_like(dw_ref)
            db_ref[...] = jnp.zeros_like(db_ref)

        for k in range(SSD_CONV):
            dw_ref[k:k + 1, :] += dws[k]
        db_ref[0:1, :] += dbs

    return pl.pallas_call(
        body, grid=(C // CONV_TILE, B),
        in_specs=[pl.BlockSpec((None, Tp, CONV_TILE), lambda j, i: (i, 0, j)),
                  pl.BlockSpec((SSD_CONV, CONV_TILE), lambda j, i: (0, j)),
                  pl.BlockSpec((1, CONV_TILE), lambda j, i: (0, j)),
                  pl.BlockSpec((None, Tp, CONV_TILE), lambda j, i: (i, 0, j))],
        out_specs=[pl.BlockSpec((None, Tp, CONV_TILE), lambda j, i: (i, 0, j)),
                   pl.BlockSpec((8, CONV_TILE), lambda j, i: (0, j)),
                   pl.BlockSpec((8, CONV_TILE), lambda j, i: (0, j))],
        out_shape=[jax.ShapeDtypeStruct(xbc.shape, F32), jax.ShapeDtypeStruct((8, C), F32),
                   jax.ShapeDtypeStruct((8, C), F32)],
        scratch_shapes=[pltpu.VMEM((Tp + CONV_HALO, CONV_TILE), F32), pltpu.VMEM((Tp + CONV_HALO, CONV_TILE), F32)],
        name=name, compiler_params=_cparams(("arbitrary", "arbitrary")),
    )(xbc, w, b, dact)


def _ssd_chunk(xs, bm, cm, dtr, z, state, dt_bias, a_log, dskip, norm_w, valid):
    Q = xs.shape[0]
    lane = lax.broadcasted_iota(jnp.int32, (1, 128), 1)
    dt = jnp.where(lane < SSD_HEADS, _softplus(dtr + dt_bias), 0.0) * valid
    a = dt * (-jnp.exp(a_log))
    tril = _tril(Q)
    cs = jnp.dot(tril.astype(F32), a, precision=HIGHEST)
    cs_t = cs.T
    cs_end = _row_of(cs, Q - 1)
    low = lane < SSD_HEAD_DIM
    low_rows = lax.broadcasted_iota(jnp.int32, (128, 1), 0) < SSD_HEAD_DIM
    ys, new_state = [], []
    for g in range(SSD_GROUPS):
        bg = bm[:, 128 * g:128 * (g + 1)]
        cg = cm[:, 128 * g:128 * (g + 1)]
        cb = _mm_nt(cg, bg)
        for pr in range(2):
            p = 2 * g + pr
            h0, h1 = 2 * p, 2 * p + 1
            xp = xs[:, 128 * p:128 * (p + 1)]
            c0, c1 = _col_of(cs, h0), _col_of(cs, h1)
            e0, e1 = _col_of(cs_end, h0), _col_of(cs_end, h1)
            xd = xp * jnp.where(low, _col_of(dt, h0), _col_of(dt, h1))
            l0 = jnp.exp(jnp.where(tril, c0 - _row_of(cs_t, h0), -1e30))
            l1 = jnp.exp(jnp.where(tril, c1 - _row_of(cs_t, h1), -1e30))
            y_diag = jnp.where(low, _mm(cb * l0, xd), _mm(cb * l1, xd))
            to_end = jnp.where(low, jnp.exp(e0 - c0), jnp.exp(e1 - c1))
            sp = state[128 * p:128 * (p + 1), :]
            y_off = _mm_nt(cg, sp) * jnp.where(low, jnp.exp(c0), jnp.exp(c1))
            new_state.append(sp * jnp.where(low_rows, jnp.exp(e0), jnp.exp(e1)) + _mm_tn(xd * to_end, bg))
            ys.append(y_diag + y_off + xp * jnp.where(low, _col_of(dskip, h0), _col_of(dskip, h1)))
    y = jnp.concatenate(ys, axis=1) * _silu(z)
    gw = SSD_INNER // SSD_GROUPS
    outs = []
    for g in range(SSD_GROUPS):
        blk = y[:, gw * g:gw * (g + 1)]
        outs.append(blk * lax.rsqrt(jnp.mean(blk * blk, axis=-1, keepdims=True) + EPS))
    return jnp.concatenate(outs, axis=1) * norm_w, jnp.concatenate(new_state, axis=0)


def _valid_rows(c, pad):
    row = c * CHUNK + lax.broadcasted_iota(jnp.int32, (CHUNK, 1), 0)
    return (row >= pad).astype(F32)


def _ssd_fwd(xact, dtr, z, dt_bias, a_log, dskip, norm_w, pad, name):
    B, Tp, _ = xact.shape
    nc = Tp // CHUNK

    def body(xs_ref, bm_ref, cm_ref, dt_ref, z_ref, db_ref, al_ref, ds_ref, nw_ref, y_ref, save_ref, st):
        c = pl.program_id(1)

        @pl.when(c == 0)
        def _():
            st[...] = jnp.zeros_like(st)

        s0 = st[...]
        save_ref[...] = s0
        y, s1 = _ssd_chunk(xs_ref[...], bm_ref[...], cm_ref[...], dt_ref[...], z_ref[...], s0, db_ref[...],
                           al_ref[...], ds_ref[...], nw_ref[...], _valid_rows(c, pad))
        y_ref[...] = y.astype(y_ref.dtype)
        st[...] = s1

    row = lambda w, off=0: pl.BlockSpec((None, CHUNK, w), lambda b, c: (b, c, off))
    par = lambda w: pl.BlockSpec((1, w), lambda b, c: (0, 0))
    return pl.pallas_call(
        body, grid=(B, nc),
        in_specs=[row(1024, 0), row(512, 2), row(512, 3), row(128), row(1024), par(128), par(128), par(128), par(1024)],
        out_specs=[row(1024), pl.BlockSpec((None, None, 1024, 128), lambda b, c: (b, c, 0, 0))],
        out_shape=[jax.ShapeDtypeStruct((B, Tp, SSD_INNER), BF16), jax.ShapeDtypeStruct((B, nc, 1024, 128), F32)],
        scratch_shapes=[pltpu.VMEM((1024, 128), F32)],
        name=name, compiler_params=_cparams(("arbitrary", "arbitrary")),
    )(xact, xact, xact, dtr, z, dt_bias, a_log, dskip, norm_w)


def _ssd_bwd(xact, dtr, z, dt_bias, a_log, dskip, norm_w, saved, dy, pad, name, after=None):
    B, Tp, _ = xact.shape
    nc = Tp // CHUNK

    def body(xs_ref, bm_ref, cm_ref, dt_ref, z_ref, db_ref, al_ref, ds_ref, nw_ref, sv_ref, dy_ref,
             dx_ref, ddt_ref, dz_ref, dpar_ref, dnw_ref, dst):
        b, i = pl.program_id(0), pl.program_id(1)
        c = nc - 1 - i

        @pl.when(i == 0)
        def _():
            dst[...] = jnp.zeros_like(dst)

        valid = _valid_rows(c, pad)
        fn = lambda *a: _ssd_chunk(*a, valid)
        _, vjp = jax.vjp(fn, xs_ref[...], bm_ref[...], cm_ref[...], dt_ref[...], z_ref[...], sv_ref[...],
                         db_ref[...], al_ref[...], ds_ref[...], nw_ref[...])
        dxs, dbm, dcm, ddt, dz, dstate, ddb, dal, dds, dnw = vjp((dy_ref[...].astype(F32), dst[...]))
        dx_ref[:, 0:1024] = dxs
        dx_ref[:, 1024:1536] = dbm
        dx_ref[:, 1536:2048] = dcm
        ddt_ref[...] = ddt
        dz_ref[...] = dz
        dst[...] = dstate

        @pl.when((b == 0) & (i == 0))
        def _():
            dpar_ref[...] = jnp.zeros_like(dpar_ref)
            dnw_ref[...] = jnp.zeros_like(dnw_ref)

        dpar_ref[0:1, :] += ddb
        dpar_ref[1:2, :] += dal
        dpar_ref[2:3, :] += dds
        dnw_ref[0:1, :] += dnw

    row = lambda w, off=0: pl.BlockSpec((None, CHUNK, w), lambda b, i: (b, nc - 1 - i, off))
    par = lambda w: pl.BlockSpec((1, w), lambda b, i: (0, 0))
    acc = lambda w: pl.BlockSpec((8, w), lambda b, i: (0, 0))
    in_specs = [row(1024, 0), row(512, 2), row(512, 3), row(128), row(1024), par(128), par(128), par(128), par(1024),
                pl.BlockSpec((None, None, 1024, 128), lambda b, i: (b, nc - 1 - i, 0, 0)), row(1024)]
    args = [xact, xact, xact, dtr, z, dt_bias, a_log, dskip, norm_w, saved, dy]
    if after is not None:
        body = _skip_ref(body, len(args))
        args.append(_deps(after))
        in_specs.append(_dep_spec(args[-1]))
    outs = pl.pallas_call(
        body, grid=(B, nc), in_specs=in_specs,
        out_specs=[row(2048), row(128), row(1024), acc(128), acc(1024)],
        out_shape=[jax.ShapeDtypeStruct((B, Tp, 2048), F32), jax.ShapeDtypeStruct((B, Tp, 128), F32),
                   jax.ShapeDtypeStruct((B, Tp, 1024), F32), jax.ShapeDtypeStruct((8, 128), F32),
                   jax.ShapeDtypeStruct((8, 1024), F32)],
        scratch_shapes=[pltpu.VMEM((1024, 128), F32)],
        name=name, compiler_params=_cparams(("arbitrary", "arbitrary")),
    )(*args)
    return outs


def _hg_chunk(qr, fr, ir, gr, state_t, p0, p1, norm_w, valid):
    Q = qr.shape[0]
    lb = jax.nn.sigmoid(p0 - p1)
    f = lb + (1.0 - lb) * jax.nn.sigmoid(fr)
    k = 1.0 - f
    q = _silu(qr)
    v = ir * valid
    cum = jnp.dot(_tril(Q).astype(F32), jnp.log(f), precision=HIGHEST)
    cum_end = _row_of(cum, Q - 1)
    o_inter = _mm_nt(q * jnp.exp(cum), state_t)
    nblk = Q // HG_SUB
    row = lax.broadcasted_iota(jnp.int32, (Q, 1), 0)
    ri = lax.broadcasted_iota(jnp.int32, (Q, Q), 0)
    ci = lax.broadcasted_iota(jnp.int32, (Q, Q), 1)
    mids = jnp.concatenate([jnp.broadcast_to(_row_of(cum, HG_SUB * i + HG_SUB // 2 - 1), (HG_SUB, cum.shape[1]))
                            for i in range(nblk)], axis=0)
    sh = HG_SUB.bit_length() - 1
    same = (jnp.right_shift(ri, sh) == jnp.right_shift(ci, sh)) & (ri >= ci)
    att = jnp.where(same, _mm_nt(q * jnp.exp(cum - mids), k * jnp.exp(mids - cum)), 0.0)
    for i in range(1, nblk):
        lo = HG_SUB * i
        start = _row_of(cum, lo - 1)
        qa = q * jnp.exp(jnp.where((row >= lo) & (row < lo + HG_SUB), cum - start, -1e30))
        ka = k * jnp.exp(jnp.where(row < lo, start - cum, -1e30))
        att = att + _mm_nt(qa, ka)
    o = o_inter + _mm(att, v)
    new_state_t = state_t * jnp.exp(cum_end) + _mm_tn(v, k * jnp.exp(cum_end - cum))
    o = o * lax.rsqrt(jnp.mean(o * o, axis=-1, keepdims=True) + EPS) * norm_w
    return o * _silu(gr), new_state_t


HG_PER_STEP = 4
HG_COLS = 4 * 128


def _hg_fwd(qfig, lbh, nwh, pad, name):
    B, Tp, _ = qfig.shape
    nc = Tp // CHUNK
    hp = HG_PER_STEP

    def body(x_ref, lb_ref, nw_ref, y_ref, save_ref, st):
        c = pl.program_id(1)

        @pl.when(c == 0)
        def _():
            st[...] = jnp.zeros_like(st)

        valid = _valid_rows(c, pad)
        for j in range(hp):
            for b in range(B):
                s0 = st[j, b]
                save_ref[j, b] = s0
                col = lambda k: x_ref[b, :, HG_COLS * j + 128 * k:HG_COLS * j + 128 * (k + 1)]
                y, s1 = _hg_chunk(col(0), col(1), col(2), col(3), s0, lb_ref[j, 0:1, :], lb_ref[j, 1:2, :], nw_ref[j], valid)
                y_ref[b, :, 128 * j:128 * (j + 1)] = y.astype(y_ref.dtype)
                st[j, b] = s1

    return pl.pallas_call(
        body, grid=(HG_HEADS // hp, nc),
        in_specs=[pl.BlockSpec((B, CHUNK, HG_COLS * hp), lambda h, c: (0, c, h)),
                  pl.BlockSpec((hp, 2, 128), lambda h, c: (h, 0, 0)),
                  pl.BlockSpec((hp, 1, 128), lambda h, c: (h, 0, 0))],
        out_specs=[pl.BlockSpec((B, CHUNK, 128 * hp), lambda h, c: (0, c, h)),
                   pl.BlockSpec((hp, B, None, 128, 128), lambda h, c: (h, 0, c, 0, 0))],
        out_shape=[jax.ShapeDtypeStruct((B, Tp, 1024), BF16), jax.ShapeDtypeStruct((HG_HEADS, B, nc, 128, 128), F32)],
        scratch_shapes=[pltpu.VMEM((hp, B, 128, 128), F32)],
        name=name, compiler_params=_cparams(("arbitrary", "arbitrary")),
    )(qfig, lbh, nwh)


def _hg_bwd(qfig, lbh, nwh, saved, dy, pad, name, after=None):
    B, Tp, _ = qfig.shape
    nc = Tp // CHUNK
    hp = HG_PER_STEP

    def body(x_ref, lb_ref, nw_ref, sv_ref, dy_ref, dx_ref, dlb_ref, dnw_ref, dst):
        i = pl.program_id(1)
        c = nc - 1 - i

        @pl.when(i == 0)
        def _():
            dst[...] = jnp.zeros_like(dst)
            dlb_ref[...] = jnp.zeros_like(dlb_ref)
            dnw_ref[...] = jnp.zeros_like(dnw_ref)

        valid = _valid_rows(c, pad)
        fn = lambda *a: _hg_chunk(*a, valid)
        for j in range(hp):
            for b in range(B):
                col = lambda k: x_ref[b, :, HG_COLS * j + 128 * k:HG_COLS * j + 128 * (k + 1)]
                _, vjp = jax.vjp(fn, col(0), col(1), col(2), col(3), sv_ref[j, b], lb_ref[j, 0:1, :], lb_ref[j, 1:2, :], nw_ref[j])
                d4 = vjp((dy_ref[b, :, 128 * j:128 * (j + 1)].astype(F32), dst[j, b]))
                for k in range(4):
                    dx_ref[b, :, HG_COLS * j + 128 * k:HG_COLS * j + 128 * (k + 1)] = d4[k].astype(dx_ref.dtype)
                dst[j, b] = d4[4]
                dlb_ref[j, 0:1, :] += d4[5]
                dlb_ref[j, 1:2, :] += d4[6]
                dnw_ref[j, 0:1, :] += d4[7]

    acc = pl.BlockSpec((hp, 8, 128), lambda h, i: (h, 0, 0))
    in_specs = [pl.BlockSpec((B, CHUNK, HG_COLS * hp), lambda h, i: (0, nc - 1 - i, h)),
                pl.BlockSpec((hp, 2, 128), lambda h, i: (h, 0, 0)),
                pl.BlockSpec((hp, 1, 128), lambda h, i: (h, 0, 0)),
                pl.BlockSpec((hp, B, None, 128, 128), lambda h, i: (h, 0, nc - 1 - i, 0, 0)),
                pl.BlockSpec((B, CHUNK, 128 * hp), lambda h, i: (0, nc - 1 - i, h))]
    args = [qfig, lbh, nwh, saved, dy]
    if after is not None:
        body = _skip_ref(body, len(args))
        args.append(_deps(after))
        in_specs.append(_dep_spec(args[-1]))
    return pl.pallas_call(
        body, grid=(HG_HEADS // hp, nc), in_specs=in_specs,
        out_specs=[pl.BlockSpec((B, CHUNK, HG_COLS * hp), lambda h, i: (0, nc - 1 - i, h)), acc, acc],
        out_shape=[jax.ShapeDtypeStruct((B, Tp, 4096), BF16), jax.ShapeDtypeStruct((HG_HEADS, 8, 128), F32),
                   jax.ShapeDtypeStruct((HG_HEADS, 8, 128), F32)],
        scratch_shapes=[pltpu.VMEM((hp, B, 128, 128), F32)],
        name=name, compiler_params=_cparams(("arbitrary", "arbitrary")),
    )(*args)


def _adamw(w, g, m, v, name, after=None):
    R, C = w.shape
    tr = _pick(R, (256, 176, 128, 64, 8)) if R > 256 else R

    def body(w_ref, g_ref, m_ref, v_ref, d_ref, mo_ref, vo_ref):
        g_ = g_ref[...]
        m_ = ADAM_B1 * m_ref[...] + (1.0 - ADAM_B1) * g_
        v_ = ADAM_B2 * v_ref[...] + (1.0 - ADAM_B2) * (g_ * g_)
        m_hat = m_ / (1.0 - ADAM_B1 ** ADAM_STEP)
        v_hat = v_ / (1.0 - ADAM_B2 ** ADAM_STEP)
        d_ref[...] = -ADAM_LR * (m_hat / (jnp.sqrt(v_hat) + ADAM_EPS) + ADAM_WD * w_ref[...])
        mo_ref[...] = m_
        vo_ref[...] = v_

    sp = pl.BlockSpec((tr, C), lambda i: (i, 0))
    sh = jax.ShapeDtypeStruct((R, C), F32)
    in_specs, args = [sp] * 4, [w, g, m, v]
    if after is not None:
        body = _skip_ref(body, len(args))
        args.append(_deps(after))
        in_specs.append(_dep_spec(args[-1]))
    return pl.pallas_call(body, grid=(R // tr,), in_specs=in_specs, out_specs=[sp] * 3, out_shape=[sh] * 3,
                          name=name, compiler_params=_cparams(("arbitrary",)))(*args)


def _ffn_fwd(h, norm_w, w_gu, w_down, tag, after_norm=None):
    n = _rms_fwd(h, norm_w, f"{tag}_norm")
    if after_norm is not None:
        after_norm(n)
    gu = _matmul(n, w_gu, mode="nn", out_dtype=BF16, name=f"{tag}_gu")
    a = _swiglu_fwd(gu, f"{tag}_act")
    out = _matmul(a, w_down, mode="nn", out_dtype=F32, alpha=0.5, res=h, name=f"{tag}_down")
    return out, (n, gu, a)


def _ffn_bwd(h, norm_w, w_gu, w_down, saved, dout, tag, after_dw_down=None):
    n, gu, a = saved
    da = _matmul(dout, w_down, mode="nt", out_dtype=BF16, alpha=0.5, name=f"{tag}_d_act")
    dw_down = _matmul(a, dout, mode="tn", out_dtype=F32, alpha=0.5, name=f"{tag}_dw_down")
    dgu = _swiglu_bwd(gu, da, f"{tag}_d_gu", after=after_dw_down(dw_down) if after_dw_down else None)
    dw_gu = _matmul(n, dgu, mode="tn", out_dtype=F32, out_groups=N_CHIPS, name=f"{tag}_dw_gu")
    dn = _matmul(dgu, w_gu, mode="nt", out_dtype=F32, name=f"{tag}_d_norm")
    dh, dnw = _rms_bwd(h, norm_w, dn, dout, f"{tag}_d_in")
    return dh, dnw, dw_gu, dw_down


IN_NAMES = ("z", "xbc", "dt", "q", "f", "i", "g", "gates")


def _split_w_in(w_in_full):
    pts = [0]
    for s in IN_SIZES:
        pts.append(pts[-1] + s)
    sl = lambda i, j: w_in_full[:, pts[i]:pts[j]]
    qfig = sl(3, 7).reshape(D_MODEL, 4, HG_HEADS, 128).transpose(0, 2, 1, 3).reshape(D_MODEL, 4 * D_MODEL)
    return {"z": sl(0, 1), "xbc": sl(1, 2), "dt": jnp.pad(sl(2, 3), ((0, 0), (0, 128 - SSD_HEADS))),
            "qfig": qfig, "gates": sl(7, 9)}


def _local_step(x, target, W):
    B, S, _ = x.shape
    T = N_META + S
    pad = (-T) % CHUNK
    Tp = T + pad
    assert pad + N_META == CHUNK
    R = B * Tp
    meta = jnp.broadcast_to(W["meta_tokens"][None], (B, N_META, D_MODEL))
    h0 = jnp.concatenate([jnp.zeros((B, pad, D_MODEL), F32), meta, x], axis=1).reshape(R, D_MODEL)

    stage = W.get("_stage", lambda name, x: {})
    W = dict(W)
    h1, sv1 = _ffn_fwd(h0, W["ffn1_norm"], W["ffn1_w_gu"], W["ffn1_w_down"], "ffn1", lambda n: W.update(stage("ffn1_norm", n)))
    W.update(stage("ffn1_out", h1))
    um = _rms_fwd(h1, W["mix_norm"], "mix_norm")
    wi = W["w_in"]
    z = _matmul(um, wi["z"], mode="nn", out_dtype=F32, name="in_z")
    xbc = _matmul(um, wi["xbc"], mode="nn", out_dtype=F32, name="in_xbc")
    dtr = _matmul(um, wi["dt"], mode="nn", out_dtype=F32, name="in_dt")
    qfig = _matmul(um, wi["qfig"], mode="nn", out_dtype=F32, name="in_qfig")
    gates = _matmul(um, wi["gates"], mode="nn", out_dtype=F32, name="in_gates")

    r3 = lambda t: t.reshape(B, Tp, t.shape[-1])
    lane_pad = lambda t: jnp.pad(t, ((0, 0), (0, 128 - t.shape[1])))
    dt_bias, a_log, dskip = lane_pad(W["ssd_dt_bias"]), lane_pad(W["ssd_a_log"]), lane_pad(W["ssd_d"])
    xact = _conv_fwd(r3(xbc), W["ssd_conv_w"], W["ssd_conv_b"], pad, "conv_fwd")
    ya, ssd_saved = _ssd_fwd(xact, r3(dtr), r3(z), dt_bias, a_log, dskip, W["ssd_norm"], pad, "ssd_fwd")
    lbh = W["hg_lower_bound"].reshape(2, HG_HEADS, 128).transpose(1, 0, 2)
    nwh = W["hg_norm"].reshape(HG_HEADS, 1, 128)
    yb, hg_saved = _hg_fwd(r3(qfig), lbh, nwh, pad, "hg_fwd")
    ya2, yb2 = ya.reshape(R, -1), yb.reshape(R, -1)
    W.update(stage("mixers_out", yb2))
    pa = _matmul(ya2, W["w_branch_a"], mode="nn", out_dtype=F32, name="branch_a")
    pb = _matmul(yb2, W["w_branch_b"], mode="nn", out_dtype=F32, name="branch_b")
    mg = _merge_fwd(pa, pb, gates, "merge")
    h2 = _matmul(mg, W["w_out"], mode="nn", out_dtype=F32, res=h1, name="mix_out")
    h3, sv2 = _ffn_fwd(h2, W["ffn2_norm"], W["ffn2_w_gu"], W["ffn2_w_down"], "ffn2")

    loss, dh3, d_final = _loss_head(h3, W["final_norm"].reshape(1, D_MODEL), target, B, "loss_head")

    G = {"final_norm": d_final[0]}
    dh2, dnw, G["ffn2_w_gu"], G["ffn2_w_down"] = _ffn_bwd(h2, W["ffn2_norm"], W["ffn2_w_gu"], W["ffn2_w_down"], sv2, dh3, "ffn2")
    G["ffn2_norm"] = dnw[0:1]
    dmg = _matmul(dh2, W["w_out"], mode="nt", out_dtype=BF16, name="d_merge")
    G["w_out"] = _matmul(mg, dh2, mode="tn", out_dtype=F32, name="dw_out")
    dpa, dpb, dgates = _merge_bwd(pa, pb, gates, dmg, "merge_bwd")
    dya = _matmul(dpa, W["w_branch_a"], mode="nt", out_dtype=BF16, name="d_ya")
    dyb = _matmul(dpb, W["w_branch_b"], mode="nt", out_dtype=BF16, name="d_yb")
    G["w_branch_a"] = _matmul(ya2, dpa, mode="tn", out_dtype=F32, name="dw_branch_a")
    G["w_branch_b"] = _matmul(yb2, dpb, mode="tn", out_dtype=F32, name="dw_branch_b")

    dxact, ddtr, dz, dpar, dnw = _ssd_bwd(xact, r3(dtr), r3(z), dt_bias, a_log, dskip, W["ssd_norm"], ssd_saved,
                                          r3(dya), pad, "ssd_bwd", after=stage("late_grads", G).get("_after"))
    G["ssd_dt_bias"], G["ssd_a_log"], G["ssd_d"] = dpar[0:1, :SSD_HEADS], dpar[1:2, :SSD_HEADS], dpar[2:3, :SSD_HEADS]
    G["ssd_norm"] = dnw[0:1]
    dxbc, dcw, dcb = _conv_bwd(r3(xbc), W["ssd_conv_w"], W["ssd_conv_b"], dxact, pad, "conv_bwd")
    G["ssd_conv_w"], G["ssd_conv_b"] = dcw[0:SSD_CONV], dcb[0:1]
    dqfig, dlb, dhn = _hg_bwd(r3(qfig), lbh, nwh, hg_saved, r3(dyb), pad, "hg_bwd",
                              after=stage("after_conv_bwd", dcb).get("_after"))
    G["hg_lower_bound"] = dlb[:, 0:2, :].transpose(1, 0, 2).reshape(2, D_MODEL)
    G["hg_norm"] = dhn[:, 0, :].reshape(1, D_MODEL)

    r2 = lambda t: t.reshape(R, t.shape[-1])
    pieces = [("z", r2(dz)), ("xbc", r2(dxbc)), ("dt", r2(ddtr)), ("qfig", r2(dqfig)), ("gates", dgates)]
    dum = None
    dwi = {}
    for nm, dpiece in pieces:
        dum = _matmul(dpiece, wi[nm], mode="nt", out_dtype=F32, res=dum, name=f"d_mix_{nm}")
        dwi[nm] = _matmul(um, dpiece, mode="tn", out_dtype=F32, name=f"dw_in_{nm}")
    dw_qfig = dwi["qfig"].reshape(D_MODEL, HG_HEADS, 4, 128).transpose(0, 2, 1, 3).reshape(D_MODEL, 4 * D_MODEL)
    G["w_in"] = jnp.concatenate([dwi["z"], dwi["xbc"], dwi["dt"][:, :SSD_HEADS], dw_qfig, dwi["gates"]], axis=1)
    dh1, dnw = _rms_bwd(h1, W["mix_norm"], dum, dh2, "mix_norm_bwd", after=stage("w_in_grads", dwi).get("_after"))
    G["mix_norm"] = dnw[0:1]
    dh0, dnw, G["ffn1_w_gu"], G["ffn1_w_down"] = _ffn_bwd(h0, W["ffn1_norm"], W["ffn1_w_gu"], W["ffn1_w_down"], sv1, dh1, "ffn1",
                                                           lambda dw: stage("ffn1_dw_down", dw).get("_after"))
    G["ffn1_norm"] = dnw[0:1]
    dh0 = dh0.reshape(B, Tp, D_MODEL)
    G["meta_tokens"] = jnp.sum(dh0[:, pad:CHUNK], axis=0)
    return loss, dh0[:, CHUNK:], G


ANY = pl.BlockSpec(memory_space=pl.ANY)


def _place():
    return lax.axis_index("x"), lax.axis_index("y"), lax.axis_index("c")


def _other_chips(x, y):
    return [(1 - x, y), (x, 1 - y), (1 - x, 1 - y)]


def _remote(src, dst, ssem, rsem, dev):
    return pltpu.make_async_remote_copy(src_ref=src, dst_ref=dst, send_sem=ssem, recv_sem=rsem,
                                        device_id=dev, device_id_type=MESH)


def _exchange8(buf, reduce, name):
    n, w = buf.shape

    def body(x_ref, *rest):
        if reduce:
            red_ref, out_ref, ssem, rsem = rest
        else:
            out_ref, ssem, rsem = rest
        x, y, c = _place()
        me = 4 * x + 2 * y + c
        out_ref[me] = x_ref[...]
        copies = []
        for k in range(1, 8):
            px = 1 - x if (k >> 2) & 1 else x
            py = 1 - y if (k >> 1) & 1 else y
            pc = 1 - c if k & 1 else c
            cp = _remote(x_ref, out_ref.at[me], ssem.at[k - 1], rsem.at[k - 1], (px, py, pc))
            cp.start()
            copies.append((cp, 4 * px + 2 * py + pc))
        for k, (cp, peer) in enumerate(copies):
            _remote(x_ref, out_ref.at[peer], ssem.at[k], rsem.at[k], (x, y, c)).wait_recv()
        for cp, _ in copies:
            cp.wait_send()
        if reduce:
            acc = out_ref[0]
            for d in range(1, 8):
                acc = acc + out_ref[d]
            red_ref[...] = acc

    vm = pl.BlockSpec(memory_space=pltpu.VMEM)
    g_shape = jax.ShapeDtypeStruct((8, n, w), F32)
    if reduce:
        out_shape, out_specs, scratch = [jax.ShapeDtypeStruct((n, w), F32)], [vm], [pltpu.VMEM((8, n, w), F32)]
    else:
        out_shape, out_specs, scratch = [g_shape], [vm], []
    return pl.pallas_call(
        body, in_specs=[vm], out_specs=out_specs, out_shape=out_shape,
        scratch_shapes=scratch + [pltpu.SemaphoreType.DMA((7,)), pltpu.SemaphoreType.DMA((7,))], name=name,
    )(buf)[0]


HBM = pltpu.MemorySpace.HBM


def _sequencer(name, collective_id, sems, sent):
    return functools.partial(pl.kernel, mesh=plsc.ScalarSubcoreMesh(axis_name="sequencer", num_cores=1), name=name,
                             scratch_types=sems, compiler_params=pltpu.CompilerParams(collective_id=collective_id),
                             cost_estimate=pl.CostEstimate(flops=0, transcendentals=0, bytes_accessed=2 * sent,
                                                           remote_bytes_transferred=sent))


def _nbytes(arrays):
    return sum(a.size * a.dtype.itemsize for a in arrays)


def _handshake(peers):
    barrier = pltpu.get_barrier_semaphore()
    for peer in peers:
        pl.semaphore_signal(barrier, inc=1, device_id=peer, device_id_type=MESH)
    pl.semaphore_wait(barrier, len(peers))


def _gather_seq(blocks, name, collective_id):
    n = len(blocks)
    half = [s.shape[1] // 2 for s in blocks]
    full = [jax.new_ref(b, memory_space=HBM) for b in blocks]

    @_sequencer(name, collective_id, [pltpu.SemaphoreType.DMA((n, 3))] * 4, _nbytes(blocks) * 3 // 4)
    def launch(ssem, rsem, fssem, frsem):
        x, y, c = _place()
        q = 2 * x + y
        chips = _other_chips(x, y)
        _handshake([(px, py, c) for px, py in chips] + [(x, y, 1 - c)])
        piece = lambda s, qq, cc: full[s].at[qq, pl.ds(cc * half[s], half[s])]
        sends = []
        for j, (px, py) in enumerate(chips):
            for s in range(n):
                cp = _remote(piece(s, q, c), piece(s, q, c), ssem.at[s, j], rsem.at[s, j], (px, py, c))
                cp.start()
                sends.append(cp)
        for j, (px, py) in enumerate(chips):
            for s in range(n):
                got = piece(s, 2 * px + py, c)
                _remote(got, got, ssem.at[s, j], rsem.at[s, j], (px, py, c)).wait_recv()
                cp = _remote(got, got, fssem.at[s, j], frsem.at[s, j], (x, y, 1 - c))
                cp.start()
                sends.append(cp)
        for j, (px, py) in enumerate(chips):
            for s in range(n):
                got = piece(s, 2 * px + py, 1 - c)
                _remote(got, got, fssem.at[s, j], frsem.at[s, j], (x, y, 1 - c)).wait_recv()
        for cp in sends:
            cp.wait_send()

    launch()
    return [r[...] for r in full]


def _pair_swap(parts, name, collective_id):
    n = len(parts)
    half = [p.shape[1] // 2 for p in parts]
    src = [jax.new_ref(p, memory_space=HBM) for p in parts]
    got = [jax.empty_ref(jax.ShapeDtypeStruct((p.shape[0], h, p.shape[2]), p.dtype), memory_space=HBM) for p, h in zip(parts, half)]

    @_sequencer(name, collective_id, [pltpu.SemaphoreType.DMA((n,))] * 2, _nbytes(parts) // 2)
    def launch(ssem, rsem):
        x, y, c = _place()
        _handshake([(x, y, 1 - c)])
        copies = []
        for s in range(n):
            cp = _remote(src[s].at[pl.ds(0, parts[s].shape[0]), pl.ds((1 - c) * half[s], half[s])], got[s], ssem.at[s], rsem.at[s], (x, y, 1 - c))
            cp.start()
            copies.append(cp)
        for cp in copies:
            cp.wait_recv()
        for cp in copies:
            cp.wait_send()

    launch()
    return [g[...] for g in got]


def _to_owners(sums, name, collective_id):
    n = len(sums)
    src = [jax.new_ref(s, memory_space=HBM) for s in sums]
    got = [jax.empty_ref(jax.ShapeDtypeStruct(s.shape, s.dtype), memory_space=HBM) for s in sums]

    @_sequencer(name, collective_id, [pltpu.SemaphoreType.DMA((n, 3))] * 2, _nbytes(sums) * 3 // 4)
    def launch(ssem, rsem):
        x, y, c = _place()
        q = 2 * x + y
        chips = _other_chips(x, y)
        _handshake([(px, py, c) for px, py in chips])
        sends = []
        for j, (px, py) in enumerate(chips):
            for s in range(n):
                cp = _remote(src[s].at[2 * px + py], got[s].at[q], ssem.at[s, j], rsem.at[s, j], (px, py, c))
                cp.start()
                sends.append(cp)
        for j, (px, py) in enumerate(chips):
            for s in range(n):
                slot = got[s].at[2 * px + py]
                _remote(slot, slot, ssem.at[s, j], rsem.at[s, j], (px, py, c)).wait_recv()
        for cp in sends:
            cp.wait_send()

    launch()
    return [g[...] for g in got]


def _pair_join(blocks, name, collective_id):
    n = len(blocks)
    out = [jax.new_ref(b, memory_space=HBM) for b in blocks]

    @_sequencer(name, collective_id, [pltpu.SemaphoreType.DMA((n,))] * 2, _nbytes(blocks) // 2)
    def launch(ssem, rsem):
        x, y, c = _place()
        _handshake([(x, y, 1 - c)])
        sends = []
        for s in range(n):
            h = blocks[s].shape[0] // 2
            mine = out[s].at[pl.ds(c * h, h)]
            cp = _remote(mine, mine, ssem.at[s], rsem.at[s], (x, y, 1 - c))
            cp.start()
            sends.append(cp)
        for s in range(n):
            h = blocks[s].shape[0] // 2
            theirs = out[s].at[pl.ds((1 - c) * h, h)]
            _remote(theirs, theirs, ssem.at[s], rsem.at[s], (x, y, 1 - c)).wait_recv()
        for cp in sends:
            cp.wait_send()

    launch()
    return [o[...] for o in out]


WIRE = BF16


def _row_tile(h):
    return _pick(h, (256, 368, 352, 128, 16))


def _add_pair(part, got, c, name, after=None):
    _, h, w = got.shape
    tr = _row_tile(h)
    nt = h // tr

    def body(c_ref, p_ref, g_ref, o_ref):
        o_ref[...] = (p_ref[...] + g_ref[...].astype(F32)).astype(o_ref.dtype)

    in_specs = [pl.BlockSpec((None, tr, w), lambda q, i, c_ref: (q, c_ref[0] * nt + i, 0)),
                pl.BlockSpec((None, tr, w), lambda q, i, c_ref: (q, i, 0))]
    args = [c.reshape(1).astype(jnp.int32), part, got]
    if after is not None:
        body = _skip_ref(body, len(args))
        args.append(_deps(after))
        in_specs.append(_dep_spec(args[-1]))
    return pl.pallas_call(
        body,
        grid_spec=pltpu.PrefetchScalarGridSpec(
            num_scalar_prefetch=1, grid=(got.shape[0], nt), in_specs=in_specs,
            out_specs=pl.BlockSpec((None, tr, w), lambda q, i, c_ref: (q, i, 0))),
        out_shape=jax.ShapeDtypeStruct(got.shape, WIRE), name=name,
        compiler_params=_cparams(("arbitrary", "arbitrary")),
    )(*args)


def _sum_chips(slots, sums, q, c, name, after=None):
    _, h, w = slots.shape
    tr = _row_tile(h)
    nt = h // tr

    def body(s_ref, mine_ref, a_ref, b_ref, d_ref, o_ref):
        o_ref[...] = ((mine_ref[...].astype(F32) + a_ref[...].astype(F32)) + b_ref[...].astype(F32)) + d_ref[...].astype(F32)

    slot = lambda k: pl.BlockSpec((None, tr, w), lambda i, s_ref: (s_ref[1 + k], i, 0))
    scalars = jnp.stack([c, q, (q + 1) % N_CHIPS, (q + 2) % N_CHIPS, (q + 3) % N_CHIPS]).astype(jnp.int32)
    in_specs, args = [slot(0), slot(1), slot(2), slot(3)], [scalars, sums, slots, slots, slots]
    if after is not None:
        body = _skip_ref(body, len(args))
        args.append(_deps(after))
        in_specs.append(_dep_spec(args[-1]))
    return pl.pallas_call(
        body,
        grid_spec=pltpu.PrefetchScalarGridSpec(
            num_scalar_prefetch=1, grid=(nt,), in_specs=in_specs,
            out_specs=pl.BlockSpec((tr, w), lambda i, s_ref: (s_ref[0] * nt + i, 0))),
        out_shape=jax.ShapeDtypeStruct((2 * h, w), F32), name=name,
        compiler_params=_cparams(("arbitrary",)),
    )(*args)


class _Reduce:
    def __init__(self, parts, q, c, tag, first_id, regions=None):
        self.parts, self.q, self.c, self.tag, self.first_id, self.regions = parts, q, c, tag, first_id, regions
        self.got = _pair_swap(parts, f"{tag}_pair_swap", first_id)

    def to_owners(self, after=None):
        self.sums = [_add_pair(p, g, self.c, f"{self.tag}_pair_add{i}", after)
                     for i, (p, g) in enumerate(zip(self.parts, self.got))]
        if self.regions is not None:
            self.sums = self.regions(self.sums)
        self.slots = _to_owners(self.sums, f"{self.tag}_to_owners", self.first_id + 1)
        return self.sums

    def join(self, after=None):
        blocks = [_sum_chips(sl, sm, self.q, self.c, f"{self.tag}_sum_chips{i}", after)
                  for i, (sl, sm) in enumerate(zip(self.slots, self.sums))]
        self.out = _pair_join(blocks, f"{self.tag}_pair_join", self.first_id + 2)
        return blocks


WEIGHTS = ("meta_tokens", "ffn1_norm", "ffn1_w_gu", "ffn1_w_down", "mix_norm", "w_in", "ssd_conv_w", "ssd_conv_b",
           "ssd_dt_bias", "ssd_a_log", "ssd_d", "ssd_norm", "hg_lower_bound", "hg_norm", "w_branch_a", "w_branch_b",
           "w_out", "ffn2_norm", "ffn2_w_gu", "ffn2_w_down", "final_norm")
BIG = ("ffn1_w_gu", "ffn1_w_down", "w_in", "w_branch_a", "w_branch_b", "w_out", "ffn2_w_gu", "ffn2_w_down")
ROW_SHARDED = ("ffn1_w_down", "ffn2_w_down", "w_branch_a", "w_branch_b", "w_out")
SMALL = tuple(n for n in WEIGHTS if n not in BIG)
SMALL_ROWS = 24


def _rows1024(a):
    flat = a.reshape(-1)
    n = -(-flat.shape[0] // 1024) * 1024
    return jnp.pad(flat, (0, n - flat.shape[0])).reshape(-1, 1024)


def _pack_small(d):
    rows = jnp.concatenate([_rows1024(d[n]) for n in SMALL], axis=0)
    return jnp.pad(rows, ((0, SMALL_ROWS - rows.shape[0]), (0, 0)))


def _unpack_small(packed, like):
    out, r = {}, 0
    for n in SMALL:
        size = like[n].size
        nr = -(-size // 1024)
        out[n] = packed[r:r + nr].reshape(-1)[:size].reshape(like[n].shape)
        r += nr
    return out


def kernel(x, meta_tokens, ffn1_norm, ffn1_w_gu, ffn1_w_down, mix_norm, w_in, ssd_conv_w, ssd_conv_b, ssd_dt_bias, ssd_a_log, ssd_d, ssd_norm, hg_lower_bound, hg_norm, w_branch_a, w_branch_b, w_out, ffn2_norm, ffn2_w_gu, ffn2_w_down, final_norm, loss_target, m_meta_tokens, m_ffn1_norm, m_ffn1_w_gu, m_ffn1_w_down, m_mix_norm, m_w_in, m_ssd_conv_w, m_ssd_conv_b, m_ssd_dt_bias, m_ssd_a_log, m_ssd_d, m_ssd_norm, m_hg_lower_bound, m_hg_norm, m_w_branch_a, m_w_branch_b, m_w_out, m_ffn2_norm, m_ffn2_w_gu, m_ffn2_w_down, m_final_norm, v_meta_tokens, v_ffn1_norm, v_ffn1_w_gu, v_ffn1_w_down, v_mix_norm, v_w_in, v_ssd_conv_w, v_ssd_conv_b, v_ssd_dt_bias, v_ssd_a_log, v_ssd_d, v_ssd_norm, v_hg_lower_bound, v_hg_norm, v_w_branch_a, v_w_branch_b, v_w_out, v_ffn2_norm, v_ffn2_w_gu, v_ffn2_w_down, v_final_norm):
    P = dict(zip(WEIGHTS, (meta_tokens, ffn1_norm, ffn1_w_gu, ffn1_w_down, mix_norm, w_in, ssd_conv_w, ssd_conv_b, ssd_dt_bias, ssd_a_log, ssd_d, ssd_norm, hg_lower_bound, hg_norm, w_branch_a, w_branch_b, w_out, ffn2_norm, ffn2_w_gu, ffn2_w_down, final_norm)))
    M = dict(zip(WEIGHTS, (m_meta_tokens, m_ffn1_norm, m_ffn1_w_gu, m_ffn1_w_down, m_mix_norm, m_w_in, m_ssd_conv_w, m_ssd_conv_b, m_ssd_dt_bias, m_ssd_a_log, m_ssd_d, m_ssd_norm, m_hg_lower_bound, m_hg_norm, m_w_branch_a, m_w_branch_b, m_w_out, m_ffn2_norm, m_ffn2_w_gu, m_ffn2_w_down, m_final_norm)))
    V = dict(zip(WEIGHTS, (v_meta_tokens, v_ffn1_norm, v_ffn1_w_gu, v_ffn1_w_down, v_mix_norm, v_w_in, v_ssd_conv_w, v_ssd_conv_b, v_ssd_dt_bias, v_ssd_a_log, v_ssd_d, v_ssd_norm, v_hg_lower_bound, v_hg_norm, v_w_branch_a, v_w_branch_b, v_w_out, v_ffn2_norm, v_ffn2_w_gu, v_ffn2_w_down, v_final_norm)))
    cx, cy, cc = _place()
    q = 2 * cx + cy

    mine = jnp.concatenate([meta_tokens.reshape(4, 1024), ssd_conv_w.reshape(2, 1024), jnp.zeros((2, 1024), F32)], axis=0)
    every = _exchange8(mine, False, "gather_small")
    meta_full = jnp.concatenate([every[2 * k, 0:4].reshape(N_META, 256) for k in range(N_CHIPS)], axis=1)
    conv_w_full = jnp.concatenate([every[2 * k, 4:6].reshape(SSD_CONV, 512) for k in range(N_CHIPS)], axis=1)

    late = ("ffn2_w_down", "w_branch_a", "w_branch_b", "w_out")
    rows = jnp.concatenate([P[n][0] for n in late], axis=0)
    zero = lambda t, dtype=F32: (t[0:1, 0:1] * 0).astype(dtype)

    def in_slot(s, after=None):
        s = s if after is None else s + zero(after)
        return lax.dynamic_update_slice(lax.empty((N_CHIPS,) + s.shape, BF16), s.astype(BF16)[None], (q, 0, 0))

    gu1, down1 = _gather_seq([in_slot(ffn1_w_gu[0]), in_slot(ffn1_w_down[0])], "gather_ffn1", 1)
    W = {n: P[n] for n in SMALL}
    W["meta_tokens"], W["ssd_conv_w"] = meta_full, conv_w_full
    W["ffn1_w_gu"], W["ffn1_w_down"] = gu1, down1.reshape(-1, D_MODEL)
    flying = {}

    def stage(name, t):
        if name == "ffn1_norm":
            flying["w_in"] = _gather_seq([in_slot(w_in[0], t)], "gather_w_in", 2)
            return {}
        if name == "ffn1_out":
            flying["late"] = _gather_seq([in_slot(ffn2_w_gu[0], t), in_slot(rows, t)], "gather_late", 3)
            (w_in_all,) = flying["w_in"]
            w_in_all = w_in_all + zero(t, BF16)
            return {"w_in": _split_w_in(w_in_all.transpose(1, 0, 2).reshape(D_MODEL, -1))}
        if name == "mixers_out":
            gu2, rows_all = flying["late"]
            out, r = {"ffn2_w_gu": gu2}, 0
            for n in late:
                nr = P[n].shape[1]
                out[n] = (rows_all[:, r:r + nr] + zero(t, BF16)).reshape(N_CHIPS * nr, D_MODEL)
                r += nr
            return out
        if name == "late_grads":
            row_parts = jnp.concatenate([t[n].reshape(N_CHIPS, -1, D_MODEL) for n in late], axis=1)
            flying["grad_late"] = _Reduce([t["ffn2_w_gu"], row_parts], q, cc, "grad_late", 4)
            return {"_after": [t["ffn2_w_gu"]] + [t[n] for n in late]}
        if name == "after_conv_bwd":
            return {"_after": flying["grad_late"].to_owners(after=t)}
        if name == "w_in_grads":
            order = ("z", "xbc", "dt", "qfig", "gates")
            blocks = flying["grad_late"].join(after=[t[k] for k in order])

            def regions(sums):
                z, xbc, dt, qfig, gates = [s[0] for s in sums]
                h = z.shape[0]
                qfig = qfig.reshape(h, HG_HEADS, 4, 128).transpose(0, 2, 1, 3).reshape(h, 4 * D_MODEL)
                cols = jnp.concatenate([z, xbc, dt[:, :SSD_HEADS], qfig, gates], axis=1)
                return [cols.reshape(h, N_CHIPS, -1).transpose(1, 0, 2)]

            flying["grad_w_in"] = _Reduce([t[k][None] for k in order], q, cc, "grad_w_in", 7, regions)
            return {"_after": blocks}
        if name == "ffn1_dw_down":
            return {"_after": flying["grad_w_in"].to_owners(after=t)}
        return {}

    W["_stage"] = stage

    loss8, grad_x, G = _local_step(x, loss_target, W)

    small = jnp.concatenate(
        [G["meta_tokens"]] + [_rows1024(G[n]) for n in SMALL if n != "meta_tokens"] + [_rows1024(loss8[0:1, 0:1])], axis=0)
    small = jnp.pad(small, ((0, 40 - small.shape[0]), (0, 0)))
    small = _exchange8(small, True, "reduce_small")
    Gs = {"meta_tokens": small[0:N_META]}
    r = N_META
    for n in SMALL:
        if n == "meta_tokens":
            continue
        nr = -(-G[n].size // 1024)
        Gs[n] = small[r:r + nr].reshape(-1)[:G[n].size].reshape(G[n].shape)
        r += nr
    loss = small[r, 0]
    Gs["meta_tokens"] = lax.dynamic_slice(Gs["meta_tokens"], (0, 256 * q), (N_META, 256))
    Gs["ssd_conv_w"] = lax.dynamic_slice(Gs["ssd_conv_w"], (0, 512 * q), (SSD_CONV, 512))[None]
    Gs = {n: Gs[n].reshape(P[n].shape) for n in SMALL}

    grad_ffn1 = _Reduce([G["ffn1_w_gu"], G["ffn1_w_down"].reshape(N_CHIPS, -1, D_MODEL)], q, cc, "grad_ffn1", 10)
    flying["grad_w_in"].join(after=grad_x)
    going = grad_ffn1.to_owners(after=grad_x)
    g_gu2, g_rows = flying["grad_late"].out
    (g_w_in,) = flying["grad_w_in"].out
    Gb = {"ffn2_w_gu": g_gu2, "w_in": g_w_in}
    r = 0
    for n in late:
        nr = P[n].shape[1]
        Gb[n] = g_rows[r:r + nr]
        r += nr

    grads, delta, new_m, new_v = dict(Gs), {}, {}, {}
    d_s, m_s, v_s = _adamw(_pack_small(P), _pack_small(Gs), _pack_small(M), _pack_small(V), "adamw_small", after=going)
    delta.update(_unpack_small(d_s, P))
    new_m.update(_unpack_small(m_s, P))
    new_v.update(_unpack_small(v_s, P))
    done = [d_s]
    for n in [n for n in BIG if n in Gb]:
        d_, m_, v_ = _adamw(P[n][0], Gb[n], M[n][0], V[n][0], f"adamw_{n}", after=going)
        grads[n], delta[n], new_m[n], new_v[n] = Gb[n][None], d_[None], m_[None], v_[None]
        done.append(d_)
    grad_ffn1.join(after=done)
    Gb["ffn1_w_gu"], Gb["ffn1_w_down"] = grad_ffn1.out
    for n in ("ffn1_w_gu", "ffn1_w_down"):
        d_, m_, v_ = _adamw(P[n][0], Gb[n], M[n][0], V[n][0], f"adamw_{n}")
        grads[n], delta[n], new_m[n], new_v[n] = Gb[n][None], d_[None], m_[None], v_[None]
    return (loss, grad_x, *[grads[n] for n in WEIGHTS], *[delta[n] for n in WEIGHTS],
            *[new_m[n] for n in WEIGHTS], *[new_v[n] for n in WEIGHTS])
```

```python
import functools

import jax
import jax.numpy as jnp
from jax import lax
from jax.experimental import pallas as pl
from jax.experimental.pallas import tpu as pltpu
from jax.experimental.pallas import tpu_sc as plsc

F32 = jnp.float32
BF16 = jnp.bfloat16
HIGHEST = lax.Precision.HIGHEST
MESH = pl.DeviceIdType.MESH

D_MODEL = 1024
N_META = 16
EPS = 1e-6
SSD_HEADS = 16
SSD_HEAD_DIM = 64
SSD_INNER = 1024
SSD_GROUPS = 4
SSD_STATE = 128
SSD_CONV = 4
SSD_CONV_CH = 2048
HG_HEADS = 8
HG_SUB = 32
CHUNK = 128
D_FF = 2816
N_CHIPS = 4
IN_SIZES = (1024, 2048, 16, 1024, 1024, 1024, 1024, 1024, 1024)
ADAM_LR = 0.001
ADAM_B1 = 0.9
ADAM_B2 = 0.999
ADAM_EPS = 1e-08
ADAM_WD = 0.01
ADAM_STEP = 10
VMEM_LIMIT = 56 * 1024 * 1024
MATMUL_BLOCK_BYTES = 42 * 1024 * 1024


def _cparams(sem=None):
    return pltpu.CompilerParams(dimension_semantics=sem, vmem_limit_bytes=VMEM_LIMIT)


def _pick(n, cands):
    for c in cands:
        if n % c == 0:
            return c
    return n


def _deps(after):
    xs = after if isinstance(after, (list, tuple)) else [after]
    one = lambda x: lax.slice(x, (0,) * x.ndim, (1,) * x.ndim).reshape(1).astype(F32)
    return jnp.concatenate([one(x) for x in xs]).reshape(1, -1)


def _dep_spec(dep):
    return pl.BlockSpec(dep.shape, lambda *_: (0, 0))


def _skip_ref(body, pos):
    return lambda *refs: body(*refs[:pos], *refs[pos + 1:])


def _dg(a, b, ca, cb):
    return lax.dot_general(a.astype(BF16), b.astype(BF16), (((ca,), (cb,)), ((), ())), preferred_element_type=F32)


@jax.custom_vjp
def _mm(a, b):
    return _dg(a, b, 1, 0)


def _mm_fwd(a, b):
    return _dg(a, b, 1, 0), (a, b)


def _mm_bwd(r, g):
    a, b = r
    return _dg(g, b, 1, 1), _dg(a, g, 0, 0)


_mm.defvjp(_mm_fwd, _mm_bwd)


@jax.custom_vjp
def _mm_nt(a, b):
    return _dg(a, b, 1, 1)


def _mm_nt_fwd(a, b):
    return _dg(a, b, 1, 1), (a, b)


def _mm_nt_bwd(r, g):
    a, b = r
    return _dg(g, b, 1, 0), _dg(g, a, 0, 0)


_mm_nt.defvjp(_mm_nt_fwd, _mm_nt_bwd)


@jax.custom_vjp
def _mm_tn(a, b):
    return _dg(a, b, 0, 0)


def _mm_tn_fwd(a, b):
    return _dg(a, b, 0, 0), (a, b)


def _mm_tn_bwd(r, g):
    a, b = r
    return _dg(b, g, 1, 1), _dg(a, g, 1, 0)


_mm_tn.defvjp(_mm_tn_fwd, _mm_tn_bwd)


def _silu(x):
    return x * jax.nn.sigmoid(x)


def _softplus(x):
    return jnp.maximum(x, 0.0) + jnp.log(1.0 + jnp.exp(-jnp.abs(x)))


def _tril(n):
    ri = lax.broadcasted_iota(jnp.int32, (n, n), 0)
    ci = lax.broadcasted_iota(jnp.int32, (n, n), 1)
    return ri >= ci


def _row_of(m, r):
    sub = lax.broadcasted_iota(jnp.int32, (m.shape[0], 1), 0)
    return jnp.sum(jnp.where(sub == r, m, 0.0), axis=0, keepdims=True)


def _col_of(m, c):
    lane = lax.broadcasted_iota(jnp.int32, (1, m.shape[1]), 1)
    return jnp.sum(jnp.where(lane == c, m, 0.0), axis=1, keepdims=True)


def _matmul(a, b, *, mode, out_dtype, name, alpha=1.0, res=None, tm=None, tn=None, tk=None, out_groups=None):
    b3 = b.ndim == 3
    if mode == "nn":
        M, K = a.shape
        G = b.shape[0] if b3 else 1
        Ng = b.shape[-1]
        N = G * Ng
    elif mode == "nt":
        M, K = a.shape
        G = b.shape[0] if b3 else 1
        N = b.shape[-2]
        Kg = b.shape[-1]
        assert G * Kg == K
    else:
        K, M = a.shape
        N = b.shape[1]
        G = out_groups or 1
        Ng = N // G
    has_res = res is not None
    split_n = (mode == "nn" and b3) or (mode == "tn" and G > 1)
    per_mn = jnp.dtype(out_dtype).itemsize + (res.dtype.itemsize if has_res else 0)
    fits = [(m_ * n_, m_, n_)
            for m_ in (4352, 2176, 1408, 1088, 1024, 544, 512, 256, 128) if M % m_ == 0
            for n_ in (2816, 2048, 1408, 1024, 512, 256, 128) if (Ng if split_n else N) % n_ == 0
            if 2 * (K * m_ * a.dtype.itemsize + K * n_ * b.dtype.itemsize + m_ * n_ * per_mn) + 4 * m_ * n_ <= MATMUL_BLOCK_BYTES]
    _, tm_fit, tn_fit = max(fits)
    tm, tn = tm or tm_fit, tn or tn_fit
    nm, nn_ = M // tm, N // tn
    assert nm * tm == M and nn_ * tn == N, (name, M, N, K, tm, tn)

    if mode == "nn":
        a_spec = pl.BlockSpec((tm, K), lambda i, j: (i, 0))
        if b3:
            ns = Ng // tn
            b_spec = pl.BlockSpec((None, K, tn), lambda i, j: (j // ns, 0, j % ns))
        else:
            b_spec = pl.BlockSpec((K, tn), lambda i, j: (0, j))
        ca, cb = 1, 0
    elif mode == "nt":
        a_spec = pl.BlockSpec((tm, K), lambda i, j: (i, 0))
        if b3:
            b_spec = pl.BlockSpec((G, tn, Kg), lambda i, j: (0, j, 0))
        else:
            b_spec = pl.BlockSpec((tn, K), lambda i, j: (j, 0))
        ca, cb = 1, 1
    else:
        a_spec = pl.BlockSpec((K, tm), lambda i, j: (0, i))
        b_spec = pl.BlockSpec((K, tn), lambda i, j: (0, j))
        ca, cb = 0, 0
    if mode == "tn" and G > 1:
        ns = Ng // tn
        o_spec = pl.BlockSpec((None, tm, tn), lambda i, j: (j // ns, i, j % ns))
        out_shape = jax.ShapeDtypeStruct((G, M, Ng), out_dtype)
    else:
        o_spec = pl.BlockSpec((tm, tn), lambda i, j: (i, j))
        out_shape = jax.ShapeDtypeStruct((M, N), out_dtype)
    in_specs = [a_spec, b_spec]
    args = [a, b]
    if has_res:
        in_specs.append(pl.BlockSpec((tm, tn), lambda i, j: (i, j)))
        args.append(res)

    def body(*refs):
        a_ref, b_ref, o_ref = refs[0], refs[1], refs[-1]
        if mode == "nt" and b3:
            o = _dg(a_ref[:, 0:Kg], b_ref[0], ca, cb)
            for g in range(1, G):
                o = o + _dg(a_ref[:, g * Kg:(g + 1) * Kg], b_ref[g], ca, cb)
        else:
            o = _dg(a_ref[...], b_ref[...], ca, cb)
        if alpha != 1.0:
            o = o * alpha
        if has_res:
            o = o + refs[2][...]
        o_ref[...] = o.astype(o_ref.dtype)

    return pl.pallas_call(
        body, grid=(nm, nn_), in_specs=in_specs, out_specs=o_spec, out_shape=out_shape, name=name,
        compiler_params=_cparams(("parallel", "parallel")),
    )(*args)


def _rms_fn(h, w):
    r = lax.rsqrt(jnp.mean(h * h, axis=-1, keepdims=True) + EPS)
    return h * r * w


def _swiglu_fn(gu):
    g = gu[:, :D_FF].astype(F32)
    u = gu[:, D_FF:].astype(F32)
    return _silu(g) * u


def _merge_fn(pa, pb, gates):
    return jax.nn.sigmoid(gates[:, :D_MODEL]) * pa + jax.nn.sigmoid(gates[:, D_MODEL:]) * pb


def _rows_call(body, *, rows, tr, ins, outs, accs=(), name, after=None):
    n = rows // tr
    assert n * tr == rows
    if after is not None:
        body = _skip_ref(body, len(ins))
        ins = list(ins) + [("full", _deps(after))]

    def spec(x):
        if isinstance(x, tuple):
            shp = x[1].shape
            return pl.BlockSpec(shp, lambda i: (0,) * len(shp))
        return pl.BlockSpec((tr, x.shape[1]), lambda i: (i, 0))

    in_specs = [spec(x) for x in ins]
    args = [x[1] if isinstance(x, tuple) else x for x in ins]
    out_specs = [spec(x) for x in outs] + [pl.BlockSpec(x.shape, lambda i: (0,) * len(x.shape)) for x in accs]
    out_shape = [x[1] if isinstance(x, tuple) else x for x in outs] + list(accs)
    return pl.pallas_call(
        body, grid=(n,), in_specs=in_specs, out_specs=out_specs, out_shape=out_shape, name=name,
        compiler_params=_cparams(("arbitrary",)),
    )(*args)


def _acc_rows(ref, val):
    @pl.when(pl.program_id(0) == 0)
    def _():
        ref[...] = jnp.zeros_like(ref)

    ref[0:1, :] += val


def _rms_fwd(h, w, name):
    def body(h_ref, w_ref, o_ref):
        o_ref[...] = _rms_fn(h_ref[...], w_ref[...]).astype(o_ref.dtype)

    R = h.shape[0]
    return _rows_call(body, rows=R, tr=_pick(R, (256, 128)), ins=[h, ("full", w)],
                      outs=[jax.ShapeDtypeStruct(h.shape, BF16)], name=name)[0]


def _rms_bwd(h, w, dn, dres, name, after=None):
    def body(h_ref, w_ref, dn_ref, dres_ref, dh_ref, dw_ref):
        _, vjp = jax.vjp(_rms_fn, h_ref[...], w_ref[...])
        dh, dw = vjp(dn_ref[...].astype(F32))
        dh_ref[...] = dh + dres_ref[...]
        _acc_rows(dw_ref, dw)

    R = h.shape[0]
    return _rows_call(body, rows=R, tr=_pick(R, (256, 128)), ins=[h, ("full", w), dn, dres],
                      outs=[jax.ShapeDtypeStruct(h.shape, F32)], accs=[jax.ShapeDtypeStruct((8, D_MODEL), F32)], name=name,
                      after=after)


def _swiglu_fwd(gu, name):
    def body(gu_ref, o_ref):
        o_ref[...] = _swiglu_fn(gu_ref[...]).astype(o_ref.dtype)

    R = gu.shape[0]
    return _rows_call(body, rows=R, tr=_pick(R, (256, 128)), ins=[gu],
                      outs=[jax.ShapeDtypeStruct((R, D_FF), BF16)], name=name)[0]


def _swiglu_bwd(gu, da, name, after=None):
    def body(gu_ref, da_ref, o_ref):
        _, vjp = jax.vjp(_swiglu_fn, gu_ref[...].astype(F32))
        (dgu,) = vjp(da_ref[...].astype(F32))
        o_ref[...] = dgu.astype(o_ref.dtype)

    R = gu.shape[0]
    return _rows_call(body, rows=R, tr=_pick(R, (256, 128)), ins=[gu, da],
                      outs=[jax.ShapeDtypeStruct(gu.shape, BF16)], name=name, after=after)[0]


def _merge_fwd(pa, pb, gates, name):
    def body(pa_ref, pb_ref, g_ref, o_ref):
        o_ref[...] = _merge_fn(pa_ref[...], pb_ref[...], g_ref[...]).astype(o_ref.dtype)

    R = pa.shape[0]
    return _rows_call(body, rows=R, tr=_pick(R, (256, 128)), ins=[pa, pb, gates],
                      outs=[jax.ShapeDtypeStruct(pa.shape, BF16)], name=name)[0]


def _merge_bwd(pa, pb, gates, dm, name):
    def body(pa_ref, pb_ref, g_ref, dm_ref, dpa_ref, dpb_ref, dg_ref):
        _, vjp = jax.vjp(_merge_fn, pa_ref[...], pb_ref[...], g_ref[...])
        dpa, dpb, dg = vjp(dm_ref[...].astype(F32))
        dpa_ref[...] = dpa.astype(dpa_ref.dtype)
        dpb_ref[...] = dpb.astype(dpb_ref.dtype)
        dg_ref[...] = dg.astype(dg_ref.dtype)

    R = pa.shape[0]
    return _rows_call(body, rows=R, tr=_pick(R, (256, 128)), ins=[pa, pb, gates, dm],
                      outs=[jax.ShapeDtypeStruct(pa.shape, BF16), jax.ShapeDtypeStruct(pa.shape, BF16),
                            jax.ShapeDtypeStruct(gates.shape, BF16)], name=name)


def _loss_head(h3, w, target, nseq, name):
    Tp = h3.shape[0] // nseq
    nc = Tp // CHUNK

    def fn(h, w_, t, valid):
        y = _rms_fn(h, w_)
        e = (y - t) * valid
        return 0.5 * jnp.sum(jnp.mean(e * e, axis=-1, keepdims=True))

    def body(h_ref, w_ref, t_ref, loss_ref, dh_ref, dw_ref):
        b, c = pl.program_id(0), pl.program_id(1)
        valid = (c >= 1).astype(F32)
        t = t_ref[...]
        loss, vjp = jax.vjp(lambda h, w_: fn(h, w_, t, valid), h_ref[...], w_ref[...])
        dh, dw = vjp(jnp.ones((), F32))
        dh_ref[...] = dh

        @pl.when((b == 0) & (c == 0))
        def _():
            loss_ref[...] = jnp.zeros_like(loss_ref)
            dw_ref[...] = jnp.zeros_like(dw_ref)

        loss_ref[...] += jnp.full(loss_ref.shape, loss, F32)
        dw_ref[0:1, :] += dw

    return pl.pallas_call(
        body, grid=(nseq, nc),
        in_specs=[pl.BlockSpec((CHUNK, D_MODEL), lambda b, c: (b * nc + c, 0)),
                  pl.BlockSpec((1, D_MODEL), lambda b, c: (0, 0)),
                  pl.BlockSpec((None, CHUNK, D_MODEL), lambda b, c: (b, jnp.maximum(c - 1, 0), 0))],
        out_specs=[pl.BlockSpec((8, 128), lambda b, c: (0, 0)),
                   pl.BlockSpec((CHUNK, D_MODEL), lambda b, c: (b * nc + c, 0)),
                   pl.BlockSpec((8, D_MODEL), lambda b, c: (0, 0))],
        out_shape=[jax.ShapeDtypeStruct((8, 128), F32), jax.ShapeDtypeStruct(h3.shape, F32),
                   jax.ShapeDtypeStruct((8, D_MODEL), F32)],
        name=name, compiler_params=_cparams(("arbitrary", "arbitrary")),
    )(h3, w, target)


CONV_TILE = 512
CONV_HALO = 8


def _conv_fwd(xbc, w, b, pad, name):
    B, Tp, C = xbc.shape
    nch = Tp // CHUNK

    def body(x_ref, w_ref, b_ref, o_ref, xp):
        xp[0:CONV_HALO, :] = jnp.zeros((CONV_HALO, CONV_TILE), F32)
        xp[CONV_HALO:, :] = x_ref[...]
        for c in range(nch):
            acc = jnp.zeros((CHUNK, CONV_TILE), F32) + b_ref[...]
            for k in range(SSD_CONV):
                acc = acc + w_ref[k:k + 1, :] * xp[pl.ds(CONV_HALO + CHUNK * c - (SSD_CONV - 1) + k, CHUNK), :]
            row = CHUNK * c + lax.broadcasted_iota(jnp.int32, (CHUNK, 1), 0)
            o_ref[pl.ds(CHUNK * c, CHUNK), :] = jnp.where(row >= pad, _silu(acc), 0.0)

    return pl.pallas_call(
        body, grid=(B, C // CONV_TILE),
        in_specs=[pl.BlockSpec((None, Tp, CONV_TILE), lambda i, j: (i, 0, j)),
                  pl.BlockSpec((SSD_CONV, CONV_TILE), lambda i, j: (0, j)),
                  pl.BlockSpec((1, CONV_TILE), lambda i, j: (0, j))],
        out_specs=pl.BlockSpec((None, Tp, CONV_TILE), lambda i, j: (i, 0, j)),
        out_shape=jax.ShapeDtypeStruct(xbc.shape, F32),
        scratch_shapes=[pltpu.VMEM((Tp + CONV_HALO, CONV_TILE), F32)],
        name=name, compiler_params=_cparams(("arbitrary", "arbitrary")),
    )(xbc, w, b)


def _conv_bwd(xbc, w, b, dact, pad, name):
    B, Tp, C = xbc.shape
    nch = Tp // CHUNK

    def body(x_ref, w_ref, b_ref, da_ref, dx_ref, dw_ref, db_ref, xp, dp):
        bi = pl.program_id(1)
        xp[0:CONV_HALO, :] = jnp.zeros((CONV_HALO, CONV_TILE), F32)
        xp[CONV_HALO:, :] = x_ref[...]
        dp[pl.ds(Tp, CONV_HALO), :] = jnp.zeros((CONV_HALO, CONV_TILE), F32)
        dws = [jnp.zeros((1, CONV_TILE), F32) for _ in range(SSD_CONV)]
        dbs = jnp.zeros((1, CONV_TILE), F32)
        for c in range(nch):
            xs = [xp[pl.ds(CONV_HALO + CHUNK * c - (SSD_CONV - 1) + k, CHUNK), :] for k in range(SSD_CONV)]
            acc = jnp.zeros((CHUNK, CONV_TILE), F32) + b_ref[...]
            for k in range(SSD_CONV):
                acc = acc + w_ref[k:k + 1, :] * xs[k]
            row = CHUNK * c + lax.broadcasted_iota(jnp.int32, (CHUNK, 1), 0)
            sg = jax.nn.sigmoid(acc)
            dpre = jnp.where(row >= pad, da_ref[pl.ds(CHUNK * c, CHUNK), :] * (sg * (1.0 + acc * (1.0 - sg))), 0.0)
            dp[pl.ds(CHUNK * c, CHUNK), :] = dpre
            dbs = dbs + jnp.sum(dpre, axis=0, keepdims=True)
            for k in range(SSD_CONV):
                dws[k] = dws[k] + jnp.sum(dpre * xs[k], axis=0, keepdims=True)
        for c in range(nch):
            acc = jnp.zeros((CHUNK, CONV_TILE), F32)
            for k in range(SSD_CONV):
                acc = acc + w_ref[k:k + 1, :] * dp[pl.ds(CHUNK * c + (SSD_CONV - 1) - k, CHUNK), :]
            dx_ref[pl.ds(CHUNK * c, CHUNK), :] = acc

        @pl.when(bi == 0)
        def _():
            dw_ref[...] = jnp.zeros_like(dw_ref)
            db_ref[...] = jnp.zeros_like(db_ref)

        for k in range(SSD_CONV):
            dw_ref[k:k + 1, :] += dws[k]
        db_ref[0:1, :] += dbs

    return pl.pallas_call(
        body, grid=(C // CONV_TILE, B),
        in_specs=[pl.BlockSpec((None, Tp, CONV_TILE), lambda j, i: (i, 0, j)),
                  pl.BlockSpec((SSD_CONV, CONV_TILE), lambda j, i: (0, j)),
                  pl.BlockSpec((1, CONV_TILE), lambda j, i: (0, j)),
                  pl.BlockSpec((None, Tp, CONV_TILE), lambda j, i: (i, 0, j))],
        out_specs=[pl.BlockSpec((None, Tp, CONV_TILE), lambda j, i: (i, 0, j)),
                   pl.BlockSpec((8, CONV_TILE), lambda j, i: (0, j)),
                   pl.BlockSpec((8, CONV_TILE), lambda j, i: (0, j))],
        out_shape=[jax.ShapeDtypeStruct(xbc.shape, F32), jax.ShapeDtypeStruct((8, C), F32),
                   jax.ShapeDtypeStruct((8, C), F32)],
        scratch_shapes=[pltpu.VMEM((Tp + CONV_HALO, CONV_TILE), F32), pltpu.VMEM((Tp + CONV_HALO, CONV_TILE), F32)],
        name=name, compiler_params=_cparams(("arbitrary", "arbitrary")),
    )(xbc, w, b, dact)


def _ssd_chunk(xs, bm, cm, dtr, z, state, dt_bias, a_log, dskip, norm_w, valid):
    Q = xs.shape[0]
    lane = lax.broadcasted_iota(jnp.int32, (1, 128), 1)
    dt = jnp.where(lane < SSD_HEADS, _softplus(dtr + dt_bias), 0.0) * valid
    a = dt * (-jnp.exp(a_log))
    tril = _tril(Q)
    cs = jnp.dot(tril.astype(F32), a, precision=HIGHEST)
    cs_t = cs.T
    cs_end = _row_of(cs, Q - 1)
    low = lane < SSD_HEAD_DIM
    low_rows = lax.broadcasted_iota(jnp.int32, (128, 1), 0) < SSD_HEAD_DIM
    ys, new_state = [], []
    for g in range(SSD_GROUPS):
        bg = bm[:, 128 * g:128 * (g + 1)]
        cg = cm[:, 128 * g:128 * (g + 1)]
        cb = _mm_nt(cg, bg)
        for pr in range(2):
            p = 2 * g + pr
            h0, h1 = 2 * p, 2 * p + 1
            xp = xs[:, 128 * p:128 * (p + 1)]
            c0, c1 = _col_of(cs, h0), _col_of(cs, h1)
            e0, e1 = _col_of(cs_end, h0), _col_of(cs_end, h1)
            xd = xp * jnp.where(low, _col_of(dt, h0), _col_of(dt, h1))
            l0 = jnp.exp(jnp.where(tril, c0 - _row_of(cs_t, h0), -1e30))
            l1 = jnp.exp(jnp.where(tril, c1 - _row_of(cs_t, h1), -1e30))
            y_diag = jnp.where(low, _mm(cb * l0, xd), _mm(cb * l1, xd))
            to_end = jnp.where(low, jnp.exp(e0 - c0), jnp.exp(e1 - c1))
            sp = state[128 * p:128 * (p + 1), :]
            y_off = _mm_nt(cg, sp) * jnp.where(low, jnp.exp(c0), jnp.exp(c1))
            new_state.append(sp * jnp.where(low_rows, jnp.exp(e0), jnp.exp(e1)) + _mm_tn(xd * to_end, bg))
            ys.append(y_diag + y_off + xp * jnp.where(low, _col_of(dskip, h0), _col_of(dskip, h1)))
    y = jnp.concatenate(ys, axis=1) * _silu(z)
    gw = SSD_INNER // SSD_GROUPS
    outs = []
    for g in range(SSD_GROUPS):
        blk = y[:, gw * g:gw * (g + 1)]
        outs.append(blk * lax.rsqrt(jnp.mean(blk * blk, axis=-1, keepdims=True) + EPS))
    return jnp.concatenate(outs, axis=1) * norm_w, jnp.concatenate(new_state, axis=0)


def _valid_rows(c, pad):
    row = c * CHUNK + lax.broadcasted_iota(jnp.int32, (CHUNK, 1), 0)
    return (row >= pad).astype(F32)


def _ssd_fwd(xact, dtr, z, dt_bias, a_log, dskip, norm_w, pad, name):
    B, Tp, _ = xact.shape
    nc = Tp // CHUNK

    def body(xs_ref, bm_ref, cm_ref, dt_ref, z_ref, db_ref, al_ref, ds_ref, nw_ref, y_ref, save_ref, st):
        c = pl.program_id(1)

        @pl.when(c == 0)
        def _():
            st[...] = jnp.zeros_like(st)

        s0 = st[...]
        save_ref[...] = s0
        y, s1 = _ssd_chunk(xs_ref[...], bm_ref[...], cm_ref[...], dt_ref[...], z_ref[...], s0, db_ref[...],
                           al_ref[...], ds_ref[...], nw_ref[...], _valid_rows(c, pad))
        y_ref[...] = y.astype(y_ref.dtype)
        st[...] = s1

    row = lambda w, off=0: pl.BlockSpec((None, CHUNK, w), lambda b, c: (b, c, off))
    par = lambda w: pl.BlockSpec((1, w), lambda b, c: (0, 0))
    return pl.pallas_call(
        body, grid=(B, nc),
        in_specs=[row(1024, 0), row(512, 2), row(512, 3), row(128), row(1024), par(128), par(128), par(128), par(1024)],
        out_specs=[row(1024), pl.BlockSpec((None, None, 1024, 128), lambda b, c: (b, c, 0, 0))],
        out_shape=[jax.ShapeDtypeStruct((B, Tp, SSD_INNER), BF16), jax.ShapeDtypeStruct((B, nc, 1024, 128), F32)],
        scratch_shapes=[pltpu.VMEM((1024, 128), F32)],
        name=name, compiler_params=_cparams(("arbitrary", "arbitrary")),
    )(xact, xact, xact, dtr, z, dt_bias, a_log, dskip, norm_w)


def _ssd_bwd(xact, dtr, z, dt_bias, a_log, dskip, norm_w, saved, dy, pad, name, after=None):
    B, Tp, _ = xact.shape
    nc = Tp // CHUNK

    def body(xs_ref, bm_ref, cm_ref, dt_ref, z_ref, db_ref, al_ref, ds_ref, nw_ref, sv_ref, dy_ref,
             dx_ref, ddt_ref, dz_ref, dpar_ref, dnw_ref, dst):
        b, i = pl.program_id(0), pl.program_id(1)
        c = nc - 1 - i

        @pl.when(i == 0)
        def _():
            dst[...] = jnp.zeros_like(dst)

        valid = _valid_rows(c, pad)
        fn = lambda *a: _ssd_chunk(*a, valid)
        _, vjp = jax.vjp(fn, xs_ref[...], bm_ref[...], cm_ref[...], dt_ref[...], z_ref[...], sv_ref[...],
                         db_ref[...], al_ref[...], ds_ref[...], nw_ref[...])
        dxs, dbm, dcm, ddt, dz, dstate, ddb, dal, dds, dnw = vjp((dy_ref[...].astype(F32), dst[...]))
        dx_ref[:, 0:1024] = dxs
        dx_ref[:, 1024:1536] = dbm
        dx_ref[:, 1536:2048] = dcm
        ddt_ref[...] = ddt
        dz_ref[...] = dz
        dst[...] = dstate

        @pl.when((b == 0) & (i == 0))
        def _():
            dpar_ref[...] = jnp.zeros_like(dpar_ref)
            dnw_ref[...] = jnp.zeros_like(dnw_ref)

        dpar_ref[0:1, :] += ddb
        dpar_ref[1:2, :] += dal
        dpar_ref[2:3, :] += dds
        dnw_ref[0:1, :] += dnw

    row = lambda w, off=0: pl.BlockSpec((None, CHUNK, w), lambda b, i: (b, nc - 1 - i, off))
    par = lambda w: pl.BlockSpec((1, w), lambda b, i: (0, 0))
    acc = lambda w: pl.BlockSpec((8, w), lambda b, i: (0, 0))
    in_specs = [row(1024, 0), row(512, 2), row(512, 3), row(128), row(1024), par(128), par(128), par(128), par(1024),
                pl.BlockSpec((None, None, 1024, 128), lambda b, i: (b, nc - 1 - i, 0, 0)), row(1024)]
    args = [xact, xact, xact, dtr, z, dt_bias, a_log, dskip, norm_w, saved, dy]
    if after is not None:
        body = _skip_ref(body, len(args))
        args.append(_deps(after))
        in_specs.append(_dep_spec(args[-1]))
    outs = pl.pallas_call(
        body, grid=(B, nc), in_specs=in_specs,
        out_specs=[row(2048), row(128), row(1024), acc(128), acc(1024)],
        out_shape=[jax.ShapeDtypeStruct((B, Tp, 2048), F32), jax.ShapeDtypeStruct((B, Tp, 128), F32),
                   jax.ShapeDtypeStruct((B, Tp, 1024), F32), jax.ShapeDtypeStruct((8, 128), F32),
                   jax.ShapeDtypeStruct((8, 1024), F32)],
        scratch_shapes=[pltpu.VMEM((1024, 128), F32)],
        name=name, compiler_params=_cparams(("arbitrary", "arbitrary")),
    )(*args)
    return outs


def _hg_chunk(qr, fr, ir, gr, state_t, p0, p1, norm_w, valid):
    Q = qr.shape[0]
    lb = jax.nn.sigmoid(p0 - p1)
    f = lb + (1.0 - lb) * jax.nn.sigmoid(fr)
    k = 1.0 - f
    q = _silu(qr)
    v = ir * valid
    cum = jnp.dot(_tril(Q).astype(F32), jnp.log(f), precision=HIGHEST)
    cum_end = _row_of(cum, Q - 1)
    o_inter = _mm_nt(q * jnp.exp(cum), state_t)
    nblk = Q // HG_SUB
    row = lax.broadcasted_iota(jnp.int32, (Q, 1), 0)
    ri = lax.broadcasted_iota(jnp.int32, (Q, Q), 0)
    ci = lax.broadcasted_iota(jnp.int32, (Q, Q), 1)
    mids = jnp.concatenate([jnp.broadcast_to(_row_of(cum, HG_SUB * i + HG_SUB // 2 - 1), (HG_SUB, cum.shape[1]))
                            for i in range(nblk)], axis=0)
    sh = HG_SUB.bit_length() - 1
    same = (jnp.right_shift(ri, sh) == jnp.right_shift(ci, sh)) & (ri >= ci)
    att = jnp.where(same, _mm_nt(q * jnp.exp(cum - mids), k * jnp.exp(mids - cum)), 0.0)
    for i in range(1, nblk):
        lo = HG_SUB * i
        start = _row_of(cum, lo - 1)
        qa = q * jnp.exp(jnp.where((row >= lo) & (row < lo + HG_SUB), cum - start, -1e30))
        ka = k * jnp.exp(jnp.where(row < lo, start - cum, -1e30))
        att = att + _mm_nt(qa, ka)
    o = o_inter + _mm(att, v)
    new_state_t = state_t * jnp.exp(cum_end) + _mm_tn(v, k * jnp.exp(cum_end - cum))
    o = o * lax.rsqrt(jnp.mean(o * o, axis=-1, keepdims=True) + EPS) * norm_w
    return o * _silu(gr), new_state_t


HG_PER_STEP = 4
HG_COLS = 4 * 128


def _hg_fwd(qfig, lbh, nwh, pad, name):
    B, Tp, _ = qfig.shape
    nc = Tp // CHUNK
    hp = HG_PER_STEP

    def body(x_ref, lb_ref, nw_ref, y_ref, save_ref, st):
        c = pl.program_id(1)

        @pl.when(c == 0)
        def _():
            st[...] = jnp.zeros_like(st)

        valid = _valid_rows(c, pad)
        for j in range(hp):
            for b in range(B):
                s0 = st[j, b]
                save_ref[j, b] = s0
                col = lambda k: x_ref[b, :, HG_COLS * j + 128 * k:HG_COLS * j + 128 * (k + 1)]
                y, s1 = _hg_chunk(col(0), col(1), col(2), col(3), s0, lb_ref[j, 0:1, :], lb_ref[j, 1:2, :], nw_ref[j], valid)
                y_ref[b, :, 128 * j:128 * (j + 1)] = y.astype(y_ref.dtype)
                st[j, b] = s1

    return pl.pallas_call(
        body, grid=(HG_HEADS // hp, nc),
        in_specs=[pl.BlockSpec((B, CHUNK, HG_COLS * hp), lambda h, c: (0, c, h)),
                  pl.BlockSpec((hp, 2, 128), lambda h, c: (h, 0, 0)),
                  pl.BlockSpec((hp, 1, 128), lambda h, c: (h, 0, 0))],
        out_specs=[pl.BlockSpec((B, CHUNK, 128 * hp), lambda h, c: (0, c, h)),
                   pl.BlockSpec((hp, B, None, 128, 128), lambda h, c: (h, 0, c, 0, 0))],
        out_shape=[jax.ShapeDtypeStruct((B, Tp, 1024), BF16), jax.ShapeDtypeStruct((HG_HEADS, B, nc, 128, 128), F32)],
        scratch_shapes=[pltpu.VMEM((hp, B, 128, 128), F32)],
        name=name, compiler_params=_cparams(("arbitrary", "arbitrary")),
    )(qfig, lbh, nwh)


def _hg_bwd(qfig, lbh, nwh, saved, dy, pad, name, after=None):
    B, Tp, _ = qfig.shape
    nc = Tp // CHUNK
    hp = HG_PER_STEP

    def body(x_ref, lb_ref, nw_ref, sv_ref, dy_ref, dx_ref, dlb_ref, dnw_ref, dst):
        i = pl.program_id(1)
        c = nc - 1 - i

        @pl.when(i == 0)
        def _():
            dst[...] = jnp.zeros_like(dst)
            dlb_ref[...] = jnp.zeros_like(dlb_ref)
            dnw_ref[...] = jnp.zeros_like(dnw_ref)

        valid = _valid_rows(c, pad)
        fn = lambda *a: _hg_chunk(*a, valid)
        for j in range(hp):
            for b in range(B):
                col = lambda k: x_ref[b, :, HG_COLS * j + 128 * k:HG_COLS * j + 128 * (k + 1)]
                _, vjp = jax.vjp(fn, col(0), col(1), col(2), col(3), sv_ref[j, b], lb_ref[j, 0:1, :], lb_ref[j, 1:2, :], nw_ref[j])
                d4 = vjp((dy_ref[b, :, 128 * j:128 * (j + 1)].astype(F32), dst[j, b]))
                for k in range(4):
                    dx_ref[b, :, HG_COLS * j + 128 * k:HG_COLS * j + 128 * (k + 1)] = d4[k].astype(dx_ref.dtype)
                dst[j, b] = d4[4]
                dlb_ref[j, 0:1, :] += d4[5]
                dlb_ref[j, 1:2, :] += d4[6]
                dnw_ref[j, 0:1, :] += d4[7]

    acc = pl.BlockSpec((hp, 8, 128), lambda h, i: (h, 0, 0))
    in_specs = [pl.BlockSpec((B, CHUNK, HG_COLS * hp), lambda h, i: (0, nc - 1 - i, h)),
                pl.BlockSpec((hp, 2, 128), lambda h, i: (h, 0, 0)),
                pl.BlockSpec((hp, 1, 128), lambda h, i: (h, 0, 0)),
                pl.BlockSpec((hp, B, None, 128, 128), lambda h, i: (h, 0, nc - 1 - i, 0, 0)),
                pl.BlockSpec((B, CHUNK, 128 * hp), lambda h, i: (0, nc - 1 - i, h))]
    args = [qfig, lbh, nwh, saved, dy]
    if after is not None:
        body = _skip_ref(body, len(args))
        args.append(_deps(after))
        in_specs.append(_dep_spec(args[-1]))
    return pl.pallas_call(
        body, grid=(HG_HEADS // hp, nc), in_specs=in_specs,
        out_specs=[pl.BlockSpec((B, CHUNK, HG_COLS * hp), lambda h, i: (0, nc - 1 - i, h)), acc, acc],
        out_shape=[jax.ShapeDtypeStruct((B, Tp, 4096), BF16), jax.ShapeDtypeStruct((HG_HEADS, 8, 128), F32),
                   jax.ShapeDtypeStruct((HG_HEADS, 8, 128), F32)],
        scratch_shapes=[pltpu.VMEM((hp, B, 128, 128), F32)],
        name=name, compiler_params=_cparams(("arbitrary", "arbitrary")),
    )(*args)


def _adamw(w, g, m, v, name, after=None):
    R, C = w.shape
    tr = _pick(R, (256, 176, 128, 64, 8)) if R > 256 else R

    def body(w_ref, g_ref, m_ref, v_ref, d_ref, mo_ref, vo_ref):
        g_ = g_ref[...]
        m_ = ADAM_B1 * m_ref[...] + (1.0 - ADAM_B1) * g_
        v_ = ADAM_B2 * v_ref[...] + (1.0 - ADAM_B2) * (g_ * g_)
        m_hat = m_ / (1.0 - ADAM_B1 ** ADAM_STEP)
        v_hat = v_ / (1.0 - ADAM_B2 ** ADAM_STEP)
        d_ref[...] = -ADAM_LR * (m_hat / (jnp.sqrt(v_hat) + ADAM_EPS) + ADAM_WD * w_ref[...])
        mo_ref[...] = m_
        vo_ref[...] = v_

    sp = pl.BlockSpec((tr, C), lambda i: (i, 0))
    sh = jax.ShapeDtypeStruct((R, C), F32)
    in_specs, args = [sp] * 4, [w, g, m, v]
    if after is not None:
        body = _skip_ref(body, len(args))
        args.append(_deps(after))
        in_specs.append(_dep_spec(args[-1]))
    return pl.pallas_call(body, grid=(R // tr,), in_specs=in_specs, out_specs=[sp] * 3, out_shape=[sh] * 3,
                          name=name, compiler_params=_cparams(("arbitrary",)))(*args)


def _ffn_fwd(h, norm_w, w_gu, w_down, tag, after_norm=None):
    n = _rms_fwd(h, norm_w, f"{tag}_norm")
    if after_norm is not None:
        after_norm(n)
    gu = _matmul(n, w_gu, mode="nn", out_dtype=BF16, name=f"{tag}_gu")
    a = _swiglu_fwd(gu, f"{tag}_act")
    out = _matmul(a, w_down, mode="nn", out_dtype=F32, alpha=0.5, res=h, name=f"{tag}_down")
    return out, (n, gu, a)


def _ffn_bwd(h, norm_w, w_gu, w_down, saved, dout, tag, after_dw_down=None):
    n, gu, a = saved
    da = _matmul(dout, w_down, mode="nt", out_dtype=BF16, alpha=0.5, name=f"{tag}_d_act")
    dw_down = _matmul(a, dout, mode="tn", out_dtype=F32, alpha=0.5, name=f"{tag}_dw_down")
    dgu = _swiglu_bwd(gu, da, f"{tag}_d_gu", after=after_dw_down(dw_down) if after_dw_down else None)
    dw_gu = _matmul(n, dgu, mode="tn", out_dtype=F32, out_groups=N_CHIPS, name=f"{tag}_dw_gu")
    dn = _matmul(dgu, w_gu, mode="nt", out_dtype=F32, name=f"{tag}_d_norm")
    dh, dnw = _rms_bwd(h, norm_w, dn, dout, f"{tag}_d_in")
    return dh, dnw, dw_gu, dw_down


IN_NAMES = ("z", "xbc", "dt", "q", "f", "i", "g", "gates")


def _split_w_in(w_in_full):
    pts = [0]
    for s in IN_SIZES:
        pts.append(pts[-1] + s)
    sl = lambda i, j: w_in_full[:, pts[i]:pts[j]]
    qfig = sl(3, 7).reshape(D_MODEL, 4, HG_HEADS, 128).transpose(0, 2, 1, 3).reshape(D_MODEL, 4 * D_MODEL)
    return {"z": sl(0, 1), "xbc": sl(1, 2), "dt": jnp.pad(sl(2, 3), ((0, 0), (0, 128 - SSD_HEADS))),
            "qfig": qfig, "gates": sl(7, 9)}


def _local_step(x, target, W):
    B, S, _ = x.shape
    T = N_META + S
    pad = (-T) % CHUNK
    Tp = T + pad
    assert pad + N_META == CHUNK
    R = B * Tp
    meta = jnp.broadcast_to(W["meta_tokens"][None], (B, N_META, D_MODEL))
    h0 = jnp.concatenate([jnp.zeros((B, pad, D_MODEL), F32), meta, x], axis=1).reshape(R, D_MODEL)

    stage = W.get("_stage", lambda name, x: {})
    W = dict(W)
    h1, sv1 = _ffn_fwd(h0, W["ffn1_norm"], W["ffn1_w_gu"], W["ffn1_w_down"], "ffn1", lambda n: W.update(stage("ffn1_norm", n)))
    W.update(stage("ffn1_out", h1))
    um = _rms_fwd(h1, W["mix_norm"], "mix_norm")
    wi = W["w_in"]
    z = _matmul(um, wi["z"], mode="nn", out_dtype=F32, name="in_z")
    xbc = _matmul(um, wi["xbc"], mode="nn", out_dtype=F32, name="in_xbc")
    dtr = _matmul(um, wi["dt"], mode="nn", out_dtype=F32, name="in_dt")
    qfig = _matmul(um, wi["qfig"], mode="nn", out_dtype=F32, name="in_qfig")
    gates = _matmul(um, wi["gates"], mode="nn", out_dtype=F32, name="in_gates")

    r3 = lambda t: t.reshape(B, Tp, t.shape[-1])
    lane_pad = lambda t: jnp.pad(t, ((0, 0), (0, 128 - t.shape[1])))
    dt_bias, a_log, dskip = lane_pad(W["ssd_dt_bias"]), lane_pad(W["ssd_a_log"]), lane_pad(W["ssd_d"])
    xact = _conv_fwd(r3(xbc), W["ssd_conv_w"], W["ssd_conv_b"], pad, "conv_fwd")
    ya, ssd_saved = _ssd_fwd(xact, r3(dtr), r3(z), dt_bias, a_log, dskip, W["ssd_norm"], pad, "ssd_fwd")
    lbh = W["hg_lower_bound"].reshape(2, HG_HEADS, 128).transpose(1, 0, 2)
    nwh = W["hg_norm"].reshape(HG_HEADS, 1, 128)
    yb, hg_saved = _hg_fwd(r3(qfig), lbh, nwh, pad, "hg_fwd")
    ya2, yb2 = ya.reshape(R, -1), yb.reshape(R, -1)
    W.update(stage("mixers_out", yb2))
    pa = _matmul(ya2, W["w_branch_a"], mode="nn", out_dtype=F32, name="branch_a")
    pb = _matmul(yb2, W["w_branch_b"], mode="nn", out_dtype=F32, name="branch_b")
    mg = _merge_fwd(pa, pb, gates, "merge")
    h2 = _matmul(mg, W["w_out"], mode="nn", out_dtype=F32, res=h1, name="mix_out")
    h3, sv2 = _ffn_fwd(h2, W["ffn2_norm"], W["ffn2_w_gu"], W["ffn2_w_down"], "ffn2")

    loss, dh3, d_final = _loss_head(h3, W["final_norm"].reshape(1, D_MODEL), target, B, "loss_head")

    G = {"final_norm": d_final[0]}
    dh2, dnw, G["ffn2_w_gu"], G["ffn2_w_down"] = _ffn_bwd(h2, W["ffn2_norm"], W["ffn2_w_gu"], W["ffn2_w_down"], sv2, dh3, "ffn2")
    G["ffn2_norm"] = dnw[0:1]
    dmg = _matmul(dh2, W["w_out"], mode="nt", out_dtype=BF16, name="d_merge")
    G["w_out"] = _matmul(mg, dh2, mode="tn", out_dtype=F32, name="dw_out")
    dpa, dpb, dgates = _merge_bwd(pa, pb, gates, dmg, "merge_bwd")
    dya = _matmul(dpa, W["w_branch_a"], mode="nt", out_dtype=BF16, name="d_ya")
    dyb = _matmul(dpb, W["w_branch_b"], mode="nt", out_dtype=BF16, name="d_yb")
    G["w_branch_a"] = _matmul(ya2, dpa, mode="tn", out_dtype=F32, name="dw_branch_a")
    G["w_branch_b"] = _matmul(yb2, dpb, mode="tn", out_dtype=F32, name="dw_branch_b")

    dxact, ddtr, dz, dpar, dnw = _ssd_bwd(xact, r3(dtr), r3(z), dt_bias, a_log, dskip, W["ssd_norm"], ssd_saved,
                                          r3(dya), pad, "ssd_bwd", after=stage("late_grads", G).get("_after"))
    G["ssd_dt_bias"], G["ssd_a_log"], G["ssd_d"] = dpar[0:1, :SSD_HEADS], dpar[1:2, :SSD_HEADS], dpar[2:3, :SSD_HEADS]
    G["ssd_norm"] = dnw[0:1]
    dxbc, dcw, dcb = _conv_bwd(r3(xbc), W["ssd_conv_w"], W["ssd_conv_b"], dxact, pad, "conv_bwd")
    G["ssd_conv_w"], G["ssd_conv_b"] = dcw[0:SSD_CONV], dcb[0:1]
    dqfig, dlb, dhn = _hg_bwd(r3(qfig), lbh, nwh, hg_saved, r3(dyb), pad, "hg_bwd",
                              after=stage("after_conv_bwd", dcb).get("_after"))
    G["hg_lower_bound"] = dlb[:, 0:2, :].transpose(1, 0, 2).reshape(2, D_MODEL)
    G["hg_norm"] = dhn[:, 0, :].reshape(1, D_MODEL)

    r2 = lambda t: t.reshape(R, t.shape[-1])
    pieces = [("z", r2(dz)), ("xbc", r2(dxbc)), ("dt", r2(ddtr)), ("qfig", r2(dqfig)), ("gates", dgates)]
    dum = None
    dwi = {}
    for nm, dpiece in pieces:
        dum = _matmul(dpiece, wi[nm], mode="nt", out_dtype=F32, res=dum, name=f"d_mix_{nm}")
        dwi[nm] = _matmul(um, dpiece, mode="tn", out_dtype=F32, name=f"dw_in_{nm}")
    dw_qfig = dwi["qfig"].reshape(D_MODEL, HG_HEADS, 4, 128).transpose(0, 2, 1, 3).reshape(D_MODEL, 4 * D_MODEL)
    G["w_in"] = jnp.concatenate([dwi["z"], dwi["xbc"], dwi["dt"][:, :SSD_HEADS], dw_qfig, dwi["gates"]], axis=1)
    dh1, dnw = _rms_bwd(h1, W["mix_norm"], dum, dh2, "mix_norm_bwd", after=stage("w_in_grads", dwi).get("_after"))
    G["mix_norm"] = dnw[0:1]
    dh0, dnw, G["ffn1_w_gu"], G["ffn1_w_down"] = _ffn_bwd(h0, W["ffn1_norm"], W["ffn1_w_gu"], W["ffn1_w_down"], sv1, dh1, "ffn1",
                                                           lambda dw: stage("ffn1_dw_down", dw).get("_after"))
    G["ffn1_norm"] = dnw[0:1]
    dh0 = dh0.reshape(B, Tp, D_MODEL)
    G["meta_tokens"] = jnp.sum(dh0[:, pad:CHUNK], axis=0)
    return loss, dh0[:, CHUNK:], G


ANY = pl.BlockSpec(memory_space=pl.ANY)


def _place():
    return lax.axis_index("x"), lax.axis_index("y"), lax.axis_index("c")


def _other_chips(x, y):
    return [(1 - x, y), (x, 1 - y), (1 - x, 1 - y)]


def _remote(src, dst, ssem, rsem, dev):
    return pltpu.make_async_remote_copy(src_ref=src, dst_ref=dst, send_sem=ssem, recv_sem=rsem,
                                        device_id=dev, device_id_type=MESH)


def _exchange8(buf, reduce, name):
    n, w = buf.shape

    def body(x_ref, *rest):
        if reduce:
            red_ref, out_ref, ssem, rsem = rest
        else:
            out_ref, ssem, rsem = rest
        x, y, c = _place()
        me = 4 * x + 2 * y + c
        out_ref[me] = x_ref[...]
        copies = []
        for k in range(1, 8):
            px = 1 - x if (k >> 2) & 1 else x
            py = 1 - y if (k >> 1) & 1 else y
            pc = 1 - c if k & 1 else c
            cp = _remote(x_ref, out_ref.at[me], ssem.at[k - 1], rsem.at[k - 1], (px, py, pc))
            cp.start()
            copies.append((cp, 4 * px + 2 * py + pc))
        for k, (cp, peer) in enumerate(copies):
            _remote(x_ref, out_ref.at[peer], ssem.at[k], rsem.at[k], (x, y, c)).wait_recv()
        for cp, _ in copies:
            cp.wait_send()
        if reduce:
            acc = out_ref[0]
            for d in range(1, 8):
                acc = acc + out_ref[d]
            red_ref[...] = acc

    vm = pl.BlockSpec(memory_space=pltpu.VMEM)
    g_shape = jax.ShapeDtypeStruct((8, n, w), F32)
    if reduce:
        out_shape, out_specs, scratch = [jax.ShapeDtypeStruct((n, w), F32)], [vm], [pltpu.VMEM((8, n, w), F32)]
    else:
        out_shape, out_specs, scratch = [g_shape], [vm], []
    return pl.pallas_call(
        body, in_specs=[vm], out_specs=out_specs, out_shape=out_shape,
        scratch_shapes=scratch + [pltpu.SemaphoreType.DMA((7,)), pltpu.SemaphoreType.DMA((7,))], name=name,
    )(buf)[0]


HBM = pltpu.MemorySpace.HBM


def _sequencer(name, collective_id, sems, sent):
    return functools.partial(pl.kernel, mesh=plsc.ScalarSubcoreMesh(axis_name="sequencer", num_cores=1), name=name,
                             scratch_types=sems, compiler_params=pltpu.CompilerParams(collective_id=collective_id),
                             cost_estimate=pl.CostEstimate(flops=0, transcendentals=0, bytes_accessed=2 * sent,
                                                           remote_bytes_transferred=sent))


def _nbytes(arrays):
    return sum(a.size * a.dtype.itemsize for a in arrays)


def _handshake(peers):
    barrier = pltpu.get_barrier_semaphore()
    for peer in peers:
        pl.semaphore_signal(barrier, inc=1, device_id=peer, device_id_type=MESH)
    pl.semaphore_wait(barrier, len(peers))


def _gather_seq(blocks, name, collective_id):
    n = len(blocks)
    half = [s.shape[1] // 2 for s in blocks]
    full = [jax.new_ref(b, memory_space=HBM) for b in blocks]

    @_sequencer(name, collective_id, [pltpu.SemaphoreType.DMA((n, 3))] * 4, _nbytes(blocks) * 3 // 4)
    def launch(ssem, rsem, fssem, frsem):
        x, y, c = _place()
        q = 2 * x + y
        chips = _other_chips(x, y)
        _handshake([(px, py, c) for px, py in chips] + [(x, y, 1 - c)])
        piece = lambda s, qq, cc: full[s].at[qq, pl.ds(cc * half[s], half[s])]
        sends = []
        for j, (px, py) in enumerate(chips):
            for s in range(n):
                cp = _remote(piece(s, q, c), piece(s, q, c), ssem.at[s, j], rsem.at[s, j], (px, py, c))
                cp.start()
                sends.append(cp)
        for j, (px, py) in enumerate(chips):
            for s in range(n):
                got = piece(s, 2 * px + py, c)
                _remote(got, got, ssem.at[s, j], rsem.at[s, j], (px, py, c)).wait_recv()
                cp = _remote(got, got, fssem.at[s, j], frsem.at[s, j], (x, y, 1 - c))
                cp.start()
                sends.append(cp)
        for j, (px, py) in enumerate(chips):
            for s in range(n):
                got = piece(s, 2 * px + py, 1 - c)
                _remote(got, got, fssem.at[s, j], frsem.at[s, j], (x, y, 1 - c)).wait_recv()
        for cp in sends:
            cp.wait_send()

    launch()
    return [r[...] for r in full]


def _pair_swap(parts, name, collective_id):
    n = len(parts)
    half = [p.shape[1] // 2 for p in parts]
    src = [jax.new_ref(p, memory_space=HBM) for p in parts]
    got = [jax.empty_ref(jax.ShapeDtypeStruct((p.shape[0], h, p.shape[2]), p.dtype), memory_space=HBM) for p, h in zip(parts, half)]

    @_sequencer(name, collective_id, [pltpu.SemaphoreType.DMA((n,))] * 2, _nbytes(parts) // 2)
    def launch(ssem, rsem):
        x, y, c = _place()
        _handshake([(x, y, 1 - c)])
        copies = []
        for s in range(n):
            cp = _remote(src[s].at[pl.ds(0, parts[s].shape[0]), pl.ds((1 - c) * half[s], half[s])], got[s], ssem.at[s], rsem.at[s], (x, y, 1 - c))
            cp.start()
            copies.append(cp)
        for cp in copies:
            cp.wait_recv()
        for cp in copies:
            cp.wait_send()

    launch()
    return [g[...] for g in got]


def _to_owners(sums, name, collective_id):
    n = len(sums)
    src = [jax.new_ref(s, memory_space=HBM) for s in sums]
    got = [jax.empty_ref(jax.ShapeDtypeStruct(s.shape, s.dtype), memory_space=HBM) for s in sums]

    @_sequencer(name, collective_id, [pltpu.SemaphoreType.DMA((n, 3))] * 2, _nbytes(sums) * 3 // 4)
    def launch(ssem, rsem):
        x, y, c = _place()
        q = 2 * x + y
        chips = _other_chips(x, y)
        _handshake([(px, py, c) for px, py in chips])
        sends = []
        for j, (px, py) in enumerate(chips):
            for s in range(n):
                cp = _remote(src[s].at[2 * px + py], got[s].at[q], ssem.at[s, j], rsem.at[s, j], (px, py, c))
                cp.start()
                sends.append(cp)
        for j, (px, py) in enumerate(chips):
            for s in range(n):
                slot = got[s].at[2 * px + py]
                _remote(slot, slot, ssem.at[s, j], rsem.at[s, j], (px, py, c)).wait_recv()
        for cp in sends:
            cp.wait_send()

    launch()
    return [g[...] for g in got]


def _pair_join(blocks, name, collective_id):
    n = len(blocks)
    out = [jax.new_ref(b, memory_space=HBM) for b in blocks]

    @_sequencer(name, collective_id, [pltpu.SemaphoreType.DMA((n,))] * 2, _nbytes(blocks) // 2)
    def launch(ssem, rsem):
        x, y, c = _place()
        _handshake([(x, y, 1 - c)])
        sends = []
        for s in range(n):
            h = blocks[s].shape[0] // 2
            mine = out[s].at[pl.ds(c * h, h)]
            cp = _remote(mine, mine, ssem.at[s], rsem.at[s], (x, y, 1 - c))
            cp.start()
            sends.append(cp)
        for s in range(n):
            h = blocks[s].shape[0] // 2
            theirs = out[s].at[pl.ds((1 - c) * h, h)]
            _remote(theirs, theirs, ssem.at[s], rsem.at[s], (x, y, 1 - c)).wait_recv()
        for cp in sends:
            cp.wait_send()

    launch()
    return [o[...] for o in out]


WIRE = BF16


def _row_tile(h):
    return _pick(h, (256, 368, 352, 128, 16))


def _add_pair(part, got, c, name, after=None):
    _, h, w = got.shape
    tr = _row_tile(h)
    nt = h // tr

    def body(c_ref, p_ref, g_ref, o_ref):
        o_ref[...] = (p_ref[...] + g_ref[...].astype(F32)).astype(o_ref.dtype)

    in_specs = [pl.BlockSpec((None, tr, w), lambda q, i, c_ref: (q, c_ref[0] * nt + i, 0)),
                pl.BlockSpec((None, tr, w), lambda q, i, c_ref: (q, i, 0))]
    args = [c.reshape(1).astype(jnp.int32), part, got]
    if after is not None:
        body = _skip_ref(body, len(args))
        args.append(_deps(after))
        in_specs.append(_dep_spec(args[-1]))
    return pl.pallas_call(
        body,
        grid_spec=pltpu.PrefetchScalarGridSpec(
            num_scalar_prefetch=1, grid=(got.shape[0], nt), in_specs=in_specs,
            out_specs=pl.BlockSpec((None, tr, w), lambda q, i, c_ref: (q, i, 0))),
        out_shape=jax.ShapeDtypeStruct(got.shape, WIRE), name=name,
        compiler_params=_cparams(("arbitrary", "arbitrary")),
    )(*args)


def _sum_chips(slots, sums, q, c, name, after=None):
    _, h, w = slots.shape
    tr = _row_tile(h)
    nt = h // tr

    def body(s_ref, mine_ref, a_ref, b_ref, d_ref, o_ref):
        o_ref[...] = ((mine_ref[...].astype(F32) + a_ref[...].astype(F32)) + b_ref[...].astype(F32)) + d_ref[...].astype(F32)

    slot = lambda k: pl.BlockSpec((None, tr, w), lambda i, s_ref: (s_ref[1 + k], i, 0))
    scalars = jnp.stack([c, q, (q + 1) % N_CHIPS, (q + 2) % N_CHIPS, (q + 3) % N_CHIPS]).astype(jnp.int32)
    in_specs, args = [slot(0), slot(1), slot(2), slot(3)], [scalars, sums, slots, slots, slots]
    if after is not None:
        body = _skip_ref(body, len(args))
        args.append(_deps(after))
        in_specs.append(_dep_spec(args[-1]))
    return pl.pallas_call(
        body,
        grid_spec=pltpu.PrefetchScalarGridSpec(
            num_scalar_prefetch=1, grid=(nt,), in_specs=in_specs,
            out_specs=pl.BlockSpec((tr, w), lambda i, s_ref: (s_ref[0] * nt + i, 0))),
        out_shape=jax.ShapeDtypeStruct((2 * h, w), F32), name=name,
        compiler_params=_cparams(("arbitrary",)),
    )(*args)


class _Reduce:
    def __init__(self, parts, q, c, tag, first_id, regions=None):
        self.parts, self.q, self.c, self.tag, self.first_id, self.regions = parts, q, c, tag, first_id, regions
        self.got = _pair_swap(parts, f"{tag}_pair_swap", first_id)

    def to_owners(self, after=None):
        self.sums = [_add_pair(p, g, self.c, f"{self.tag}_pair_add{i}", after)
                     for i, (p, g) in enumerate(zip(self.parts, self.got))]
        if self.regions is not None:
            self.sums = self.regions(self.sums)
        self.slots = _to_owners(self.sums, f"{self.tag}_to_owners", self.first_id + 1)
        return self.sums

    def join(self, after=None):
        blocks = [_sum_chips(sl, sm, self.q, self.c, f"{self.tag}_sum_chips{i}", after)
                  for i, (sl, sm) in enumerate(zip(self.slots, self.sums))]
        self.out = _pair_join(blocks, f"{self.tag}_pair_join", self.first_id + 2)
        return blocks


WEIGHTS = ("meta_tokens", "ffn1_norm", "ffn1_w_gu", "ffn1_w_down", "mix_norm", "w_in", "ssd_conv_w", "ssd_conv_b",
           "ssd_dt_bias", "ssd_a_log", "ssd_d", "ssd_norm", "hg_lower_bound", "hg_norm", "w_branch_a", "w_branch_b",
           "w_out", "ffn2_norm", "ffn2_w_gu", "ffn2_w_down", "final_norm")
BIG = ("ffn1_w_gu", "ffn1_w_down", "w_in", "w_branch_a", "w_branch_b", "w_out", "ffn2_w_gu", "ffn2_w_down")
ROW_SHARDED = ("ffn1_w_down", "ffn2_w_down", "w_branch_a", "w_branch_b", "w_out")
SMALL = tuple(n for n in WEIGHTS if n not in BIG)
SMALL_ROWS = 24


def _rows1024(a):
    flat = a.reshape(-1)
    n = -(-flat.shape[0] // 1024) * 1024
    return jnp.pad(flat, (0, n - flat.shape[0])).reshape(-1, 1024)


def _pack_small(d):
    rows = jnp.concatenate([_rows1024(d[n]) for n in SMALL], axis=0)
    return jnp.pad(rows, ((0, SMALL_ROWS - rows.shape[0]), (0, 0)))


def _unpack_small(packed, like):
    out, r = {}, 0
    for n in SMALL:
        size = like[n].size
        nr = -(-size // 1024)
        out[n] = packed[r:r + nr].reshape(-1)[:size].reshape(like[n].shape)
        r += nr
    return out


def kernel(x, meta_tokens, ffn1_norm, ffn1_w_gu, ffn1_w_down, mix_norm, w_in, ssd_conv_w, ssd_conv_b, ssd_dt_bias, ssd_a_log, ssd_d, ssd_norm, hg_lower_bound, hg_norm, w_branch_a, w_branch_b, w_out, ffn2_norm, ffn2_w_gu, ffn2_w_down, final_norm, loss_target, m_meta_tokens, m_ffn1_norm, m_ffn1_w_gu, m_ffn1_w_down, m_mix_norm, m_w_in, m_ssd_conv_w, m_ssd_conv_b, m_ssd_dt_bias, m_ssd_a_log, m_ssd_d, m_ssd_norm, m_hg_lower_bound, m_hg_norm, m_w_branch_a, m_w_branch_b, m_w_out, m_ffn2_norm, m_ffn2_w_gu, m_ffn2_w_down, m_final_norm, v_meta_tokens, v_ffn1_norm, v_ffn1_w_gu, v_ffn1_w_down, v_mix_norm, v_w_in, v_ssd_conv_w, v_ssd_conv_b, v_ssd_dt_bias, v_ssd_a_log, v_ssd_d, v_ssd_norm, v_hg_lower_bound, v_hg_norm, v_w_branch_a, v_w_branch_b, v_w_out, v_ffn2_norm, v_ffn2_w_gu, v_ffn2_w_down, v_final_norm):
    P = dict(zip(WEIGHTS, (meta_tokens, ffn1_norm, ffn1_w_gu, ffn1_w_down, mix_norm, w_in, ssd_conv_w, ssd_conv_b, ssd_dt_bias, ssd_a_log, ssd_d, ssd_norm, hg_lower_bound, hg_norm, w_branch_a, w_branch_b, w_out, ffn2_norm, ffn2_w_gu, ffn2_w_down, final_norm)))
    M = dict(zip(WEIGHTS, (m_meta_tokens, m_ffn1_norm, m_ffn1_w_gu, m_ffn1_w_down, m_mix_norm, m_w_in, m_ssd_conv_w, m_ssd_conv_b, m_ssd_dt_bias, m_ssd_a_log, m_ssd_d, m_ssd_norm, m_hg_lower_bound, m_hg_norm, m_w_branch_a, m_w_branch_b, m_w_out, m_ffn2_norm, m_ffn2_w_gu, m_ffn2_w_down, m_final_norm)))
    V = dict(zip(WEIGHTS, (v_meta_tokens, v_ffn1_norm, v_ffn1_w_gu, v_ffn1_w_down, v_mix_norm, v_w_in, v_ssd_conv_w, v_ssd_conv_b, v_ssd_dt_bias, v_ssd_a_log, v_ssd_d, v_ssd_norm, v_hg_lower_bound, v_hg_norm, v_w_branch_a, v_w_branch_b, v_w_out, v_ffn2_norm, v_ffn2_w_gu, v_ffn2_w_down, v_final_norm)))
    cx, cy, cc = _place()
    q = 2 * cx + cy

    mine = jnp.concatenate([meta_tokens.reshape(4, 1024), ssd_conv_w.reshape(2, 1024), jnp.zeros((2, 1024), F32)], axis=0)
    every = _exchange8(mine, False, "gather_small")
    meta_full = jnp.concatenate([every[2 * k, 0:4].reshape(N_META, 256) for k in range(N_CHIPS)], axis=1)
    conv_w_full = jnp.concatenate([every[2 * k, 4:6].reshape(SSD_CONV, 512) for k in range(N_CHIPS)], axis=1)

    late = ("ffn2_w_down", "w_branch_a", "w_branch_b", "w_out")
    rows = jnp.concatenate([P[n][0] for n in late], axis=0)
    zero = lambda t, dtype=F32: (t[0:1, 0:1] * 0).astype(dtype)

    def in_slot(s, after=None):
        s = s if after is None else s + zero(after)
        return lax.dynamic_update_slice(lax.empty((N_CHIPS,) + s.shape, BF16), s.astype(BF16)[None], (q, 0, 0))

    gu1, down1 = _gather_seq([in_slot(ffn1_w_gu[0]), in_slot(ffn1_w_down[0])], "gather_ffn1", 1)
    W = {n: P[n] for n in SMALL}
    W["meta_tokens"], W["ssd_conv_w"] = meta_full, conv_w_full
    W["ffn1_w_gu"], W["ffn1_w_down"] = gu1, down1.reshape(-1, D_MODEL)
    flying = {}

    def stage(name, t):
        if name == "ffn1_norm":
            flying["w_in"] = _gather_seq([in_slot(w_in[0], t)], "gather_w_in", 2)
            return {}
        if name == "ffn1_out":
            flying["late"] = _gather_seq([in_slot(ffn2_w_gu[0], t), in_slot(rows, t)], "gather_late", 3)
            (w_in_all,) = flying["w_in"]
            w_in_all = w_in_all + zero(t, BF16)
            return {"w_in": _split_w_in(w_in_all.transpose(1, 0, 2).reshape(D_MODEL, -1))}
        if name == "mixers_out":
            gu2, rows_all = flying["late"]
            out, r = {"ffn2_w_gu": gu2}, 0
            for n in late:
                nr = P[n].shape[1]
                out[n] = (rows_all[:, r:r + nr] + zero(t, BF16)).reshape(N_CHIPS * nr, D_MODEL)
                r += nr
            return out
        if name == "late_grads":
            row_parts = jnp.concatenate([t[n].reshape(N_CHIPS, -1, D_MODEL) for n in late], axis=1)
            flying["grad_late"] = _Reduce([t["ffn2_w_gu"], row_parts], q, cc, "grad_late", 4)
            return {"_after": [t["ffn2_w_gu"]] + [t[n] for n in late]}
        if name == "after_conv_bwd":
            return {"_after": flying["grad_late"].to_owners(after=t)}
        if name == "w_in_grads":
            order = ("z", "xbc", "dt", "qfig", "gates")
            blocks = flying["grad_late"].join(after=[t[k] for k in order])

            def regions(sums):
                z, xbc, dt, qfig, gates = [s[0] for s in sums]
                h = z.shape[0]
                qfig = qfig.reshape(h, HG_HEADS, 4, 128).transpose(0, 2, 1, 3).reshape(h, 4 * D_MODEL)
                cols = jnp.concatenate([z, xbc, dt[:, :SSD_HEADS], qfig, gates], axis=1)
                return [cols.reshape(h, N_CHIPS, -1).transpose(1, 0, 2)]

            flying["grad_w_in"] = _Reduce([t[k][None] for k in order], q, cc, "grad_w_in", 7, regions)
            return {"_after": blocks}
        if name == "ffn1_dw_down":
            return {"_after": flying["grad_w_in"].to_owners(after=t)}
        return {}

    W["_stage"] = stage

    loss8, grad_x, G = _local_step(x, loss_target, W)

    small = jnp.concatenate(
        [G["meta_tokens"]] + [_rows1024(G[n]) for n in SMALL if n != "meta_tokens"] + [_rows1024(loss8[0:1, 0:1])], axis=0)
    small = jnp.pad(small, ((0, 40 - small.shape[0]), (0, 0)))
    small = _exchange8(small, True, "reduce_small")
    Gs = {"meta_tokens": small[0:N_META]}
    r = N_META
    for n in SMALL:
        if n == "meta_tokens":
            continue
        nr = -(-G[n].size // 1024)
        Gs[n] = small[r:r + nr].reshape(-1)[:G[n].size].reshape(G[n].shape)
        r += nr
    loss = small[r, 0]
    Gs["meta_tokens"] = lax.dynamic_slice(Gs["meta_tokens"], (0, 256 * q), (N_META, 256))
    Gs["ssd_conv_w"] = lax.dynamic_slice(Gs["ssd_conv_w"], (0, 512 * q), (SSD_CONV, 512))[None]
    Gs = {n: Gs[n].reshape(P[n].shape) for n in SMALL}

    grad_ffn1 = _Reduce([G["ffn1_w_gu"], G["ffn1_w_down"].reshape(N_CHIPS, -1, D_MODEL)], q, cc, "grad_ffn1", 10)
    flying["grad_w_in"].join(after=grad_x)
    going = grad_ffn1.to_owners(after=grad_x)
    g_gu2, g_rows = flying["grad_late"].out
    (g_w_in,) = flying["grad_w_in"].out
    Gb = {"ffn2_w_gu": g_gu2, "w_in": g_w_in}
    r = 0
    for n in late:
        nr = P[n].shape[1]
        Gb[n] = g_rows[r:r + nr]
        r += nr

    grads, delta, new_m, new_v = dict(Gs), {}, {}, {}
    d_s, m_s, v_s = _adamw(_pack_small(P), _pack_small(Gs), _pack_small(M), _pack_small(V), "adamw_small", after=going)
    delta.update(_unpack_small(d_s, P))
    new_m.update(_unpack_small(m_s, P))
    new_v.update(_unpack_small(v_s, P))
    done = [d_s]
    for n in [n for n in BIG if n in Gb]:
        d_, m_, v_ = _adamw(P[n][0], Gb[n], M[n][0], V[n][0], f"adamw_{n}", after=going)
        grads[n], delta[n], new_m[n], new_v[n] = Gb[n][None], d_[None], m_[None], v_[None]
        done.append(d_)
    grad_ffn1.join(after=done)
    Gb["ffn1_w_gu"], Gb["ffn1_w_down"] = grad_ffn1.out
    for n in ("ffn1_w_gu", "ffn1_w_down"):
        d_, m_, v_ = _adamw(P[n][0], Gb[n], M[n][0], V[n][0], f"adamw_{n}")
        grads[n], delta[n], new_m[n], new_v[n] = Gb[n][None], d_[None], m_[None], v_[None]
    return (loss, grad_x, *[grads[n] for n in WEIGHTS], *[delta[n] for n in WEIGHTS],
            *[new_m[n] for n in WEIGHTS], *[new_v[n] for n in WEIGHTS])
```

```python
import functools

import jax
import jax.numpy as jnp
from jax import lax
from jax.experimental import pallas as pl
from jax.experimental.pallas import tpu as pltpu
from jax.experimental.pallas import tpu_sc as plsc

F32 = jnp.float32
BF16 = jnp.bfloat16
HIGHEST = lax.Precision.HIGHEST
MESH = pl.DeviceIdType.MESH

D_MODEL = 1024
N_META = 16
EPS = 1e-6
SSD_HEADS = 16
SSD_HEAD_DIM = 64
SSD_INNER = 1024
SSD_GROUPS = 4
SSD_STATE = 128
SSD_CONV = 4
SSD_CONV_CH = 2048
HG_HEADS = 8
HG_SUB = 32
CHUNK = 128
D_FF = 2816
N_CHIPS = 4
IN_SIZES = (1024, 2048, 16, 1024, 1024, 1024, 1024, 1024, 1024)
ADAM_LR = 0.001
ADAM_B1 = 0.9
ADAM_B2 = 0.999
ADAM_EPS = 1e-08
ADAM_WD = 0.01
ADAM_STEP = 10
VMEM_LIMIT = 56 * 1024 * 1024
MATMUL_BLOCK_BYTES = 42 * 1024 * 1024
ADAMW_BLOCK_BYTES = 5 * 512 * 1024


def _cparams(sem=None):
    return pltpu.CompilerParams(dimension_semantics=sem, vmem_limit_bytes=VMEM_LIMIT)


def _pick(n, cands):
    for c in cands:
        if n % c == 0:
            return c
    return n


def _deps(after):
    xs = after if isinstance(after, (list, tuple)) else [after]
    one = lambda x: lax.slice(x, (0,) * x.ndim, (1,) * x.ndim).reshape(1).astype(F32)
    return jnp.concatenate([one(x) for x in xs]).reshape(1, -1)


def _dep_spec(dep):
    return pl.BlockSpec(dep.shape, lambda *_: (0, 0))


def _skip_ref(body, pos):
    return lambda *refs: body(*refs[:pos], *refs[pos + 1:])


def _dg(a, b, ca, cb):
    return lax.dot_general(a.astype(BF16), b.astype(BF16), (((ca,), (cb,)), ((), ())), preferred_element_type=F32)


@jax.custom_vjp
def _mm(a, b):
    return _dg(a, b, 1, 0)


def _mm_fwd(a, b):
    return _dg(a, b, 1, 0), (a, b)


def _mm_bwd(r, g):
    a, b = r
    return _dg(g, b, 1, 1), _dg(a, g, 0, 0)


_mm.defvjp(_mm_fwd, _mm_bwd)


@jax.custom_vjp
def _mm_nt(a, b):
    return _dg(a, b, 1, 1)


def _mm_nt_fwd(a, b):
    return _dg(a, b, 1, 1), (a, b)


def _mm_nt_bwd(r, g):
    a, b = r
    return _dg(g, b, 1, 0), _dg(g, a, 0, 0)


_mm_nt.defvjp(_mm_nt_fwd, _mm_nt_bwd)


@jax.custom_vjp
def _mm_tn(a, b):
    return _dg(a, b, 0, 0)


def _mm_tn_fwd(a, b):
    return _dg(a, b, 0, 0), (a, b)


def _mm_tn_bwd(r, g):
    a, b = r
    return _dg(b, g, 1, 1), _dg(a, g, 1, 0)


_mm_tn.defvjp(_mm_tn_fwd, _mm_tn_bwd)


def _silu(x):
    return x * jax.nn.sigmoid(x)


def _softplus(x):
    return jnp.maximum(x, 0.0) + jnp.log(1.0 + jnp.exp(-jnp.abs(x)))


def _tril(n):
    ri = lax.broadcasted_iota(jnp.int32, (n, n), 0)
    ci = lax.broadcasted_iota(jnp.int32, (n, n), 1)
    return ri >= ci


def _row_of(m, r):
    sub = lax.broadcasted_iota(jnp.int32, (m.shape[0], 1), 0)
    return jnp.sum(jnp.where(sub == r, m, 0.0), axis=0, keepdims=True)


def _col_of(m, c):
    lane = lax.broadcasted_iota(jnp.int32, (1, m.shape[1]), 1)
    return jnp.sum(jnp.where(lane == c, m, 0.0), axis=1, keepdims=True)


def _matmul(a, b, *, mode, out_dtype, name, alpha=1.0, res=None, tm=None, tn=None, out_groups=None, after=None):
    b3 = b.ndim == 3
    if mode == "nn":
        M, K = a.shape
        G = b.shape[0] if b3 else 1
        Ng = b.shape[-1]
        N = G * Ng
    elif mode == "nt":
        M, K = a.shape
        G = b.shape[0] if b3 else 1
        N = b.shape[-2]
        Kg = b.shape[-1]
        assert G * Kg == K
    else:
        K, M = a.shape
        N = b.shape[1]
        G = out_groups or 1
        Ng = N // G
    has_res = res is not None
    split_n = (mode == "nn" and b3) or (mode == "tn" and G > 1)
    per_mn = jnp.dtype(out_dtype).itemsize + (res.dtype.itemsize if has_res else 0)
    fits = [(m_ * n_, m_, n_)
            for m_ in (4352, 2176, 1408, 1088, 1024, 544, 512, 256, 128) if M % m_ == 0
            for n_ in (2816, 2048, 1408, 1024, 512, 256, 128) if (Ng if split_n else N) % n_ == 0
            if 2 * (K * m_ * a.dtype.itemsize + K * n_ * b.dtype.itemsize + m_ * n_ * per_mn) + 4 * m_ * n_ <= MATMUL_BLOCK_BYTES]
    _, tm_fit, tn_fit = max(fits)
    tm, tn = tm or tm_fit, tn or tn_fit
    nm, nn_ = M // tm, N // tn
    assert nm * tm == M and nn_ * tn == N, (name, M, N, K, tm, tn)

    if mode == "nn":
        a_spec = pl.BlockSpec((tm, K), lambda i, j: (i, 0))
        if b3:
            ns = Ng // tn
            b_spec = pl.BlockSpec((None, K, tn), lambda i, j: (j // ns, 0, j % ns))
        else:
            b_spec = pl.BlockSpec((K, tn), lambda i, j: (0, j))
        ca, cb = 1, 0
    elif mode == "nt":
        a_spec = pl.BlockSpec((tm, K), lambda i, j: (i, 0))
        if b3:
            b_spec = pl.BlockSpec((G, tn, Kg), lambda i, j: (0, j, 0))
        else:
            b_spec = pl.BlockSpec((tn, K), lambda i, j: (j, 0))
        ca, cb = 1, 1
    else:
        a_spec = pl.BlockSpec((K, tm), lambda i, j: (0, i))
        b_spec = pl.BlockSpec((K, tn), lambda i, j: (0, j))
        ca, cb = 0, 0
    if mode == "tn" and G > 1:
        ns = Ng // tn
        o_spec = pl.BlockSpec((None, tm, tn), lambda i, j: (j // ns, i, j % ns))
        out_shape = jax.ShapeDtypeStruct((G, M, Ng), out_dtype)
    else:
        o_spec = pl.BlockSpec((tm, tn), lambda i, j: (i, j))
        out_shape = jax.ShapeDtypeStruct((M, N), out_dtype)
    in_specs = [a_spec, b_spec]
    args = [a, b]
    if has_res:
        in_specs.append(pl.BlockSpec((tm, tn), lambda i, j: (i, j)))
        args.append(res)
    if after is not None:
        args.append(_deps(after))
        in_specs.append(_dep_spec(args[-1]))

    def body(*refs):
        a_ref, b_ref, o_ref = refs[0], refs[1], refs[-1]
        if mode == "nt" and b3:
            o = _dg(a_ref[:, 0:Kg], b_ref[0], ca, cb)
            for g in range(1, G):
                o = o + _dg(a_ref[:, g * Kg:(g + 1) * Kg], b_ref[g], ca, cb)
        else:
            o = _dg(a_ref[...], b_ref[...], ca, cb)
        if alpha != 1.0:
            o = o * alpha
        if has_res:
            o = o + refs[2][...]
        o_ref[...] = o.astype(o_ref.dtype)

    return pl.pallas_call(
        body, grid=(nm, nn_), in_specs=in_specs, out_specs=o_spec, out_shape=out_shape, name=name,
        compiler_params=_cparams(("parallel", "parallel")),
    )(*args)


def _rms_fn(h, w):
    r = lax.rsqrt(jnp.mean(h * h, axis=-1, keepdims=True) + EPS)
    return h * r * w


def _swiglu_fn(gu):
    g = gu[:, :D_FF].astype(F32)
    u = gu[:, D_FF:].astype(F32)
    return _silu(g) * u


def _merge_fn(pa, pb, gates):
    return jax.nn.sigmoid(gates[:, :D_MODEL]) * pa + jax.nn.sigmoid(gates[:, D_MODEL:]) * pb


def _rows_call(body, *, rows, tr, ins, outs, accs=(), name, after=None):
    n = rows // tr
    assert n * tr == rows
    if after is not None:
        body = _skip_ref(body, len(ins))
        ins = list(ins) + [("full", _deps(after))]

    def spec(x):
        if isinstance(x, tuple):
            shp = x[1].shape
            return pl.BlockSpec(shp, lambda i: (0,) * len(shp))
        return pl.BlockSpec((tr, x.shape[1]), lambda i: (i, 0))

    in_specs = [spec(x) for x in ins]
    args = [x[1] if isinstance(x, tuple) else x for x in ins]
    out_specs = [spec(x) for x in outs] + [pl.BlockSpec(x.shape, lambda i: (0,) * len(x.shape)) for x in accs]
    out_shape = [x[1] if isinstance(x, tuple) else x for x in outs] + list(accs)
    return pl.pallas_call(
        body, grid=(n,), in_specs=in_specs, out_specs=out_specs, out_shape=out_shape, name=name,
        compiler_params=_cparams(("arbitrary",)),
    )(*args)


def _acc_rows(ref, val):
    @pl.when(pl.program_id(0) == 0)
    def _():
        ref[...] = jnp.zeros_like(ref)

    ref[0:1, :] += val


def _rms_fwd(h, w, name):
    def body(h_ref, w_ref, o_ref):
        o_ref[...] = _rms_fn(h_ref[...], w_ref[...]).astype(o_ref.dtype)

    R = h.shape[0]
    return _rows_call(body, rows=R, tr=_pick(R, (256, 128)), ins=[h, ("full", w)],
                      outs=[jax.ShapeDtypeStruct(h.shape, BF16)], name=name)[0]


def _rms_bwd(h, w, dn, dres, name, after=None):
    def body(h_ref, w_ref, dn_ref, dres_ref, dh_ref, dw_ref):
        _, vjp = jax.vjp(_rms_fn, h_ref[...], w_ref[...])
        dh, dw = vjp(dn_ref[...].astype(F32))
        dh_ref[...] = dh + dres_ref[...]
        _acc_rows(dw_ref, dw)

    R = h.shape[0]
    return _rows_call(body, rows=R, tr=_pick(R, (256, 128)), ins=[h, ("full", w), dn, dres],
                      outs=[jax.ShapeDtypeStruct(h.shape, F32)], accs=[jax.ShapeDtypeStruct((8, D_MODEL), F32)], name=name,
                      after=after)


def _gu_swiglu(n, w_gu, name):
    R = n.shape[0]
    G, _, ng = w_gu.shape

    def body(n_ref, w_ref, gu_ref, a_ref):
        x = n_ref[...]
        for r in range(G):
            gu_ref[:, ng * r:ng * (r + 1)] = _dg(x, w_ref[r], 1, 0).astype(gu_ref.dtype)
        a_ref[...] = _swiglu_fn(gu_ref[...]).astype(a_ref.dtype)

    return _rows_call(body, rows=R, tr=_pick(R, (256, 128)), ins=[n, ("full", w_gu)],
                      outs=[jax.ShapeDtypeStruct((R, 2 * D_FF), BF16), jax.ShapeDtypeStruct((R, D_FF), BF16)], name=name)


def _d_swiglu(dout, w_down, gu, alpha, name):
    R = gu.shape[0]

    def body(do_ref, w_ref, gu_ref, o_ref):
        da = _dg(do_ref[...], w_ref[...], 1, 1) * alpha
        _, vjp = jax.vjp(_swiglu_fn, gu_ref[...].astype(F32))
        (dgu,) = vjp(da)
        o_ref[...] = dgu.astype(o_ref.dtype)

    return _rows_call(body, rows=R, tr=_pick(R, (256, 128)), ins=[dout, ("full", w_down), gu],
                      outs=[jax.ShapeDtypeStruct(gu.shape, BF16)], name=name)[0]


def _merge_fwd(pa, pb, gates, name):
    def body(pa_ref, pb_ref, g_ref, o_ref):
        o_ref[...] = _merge_fn(pa_ref[...], pb_ref[...], g_ref[...]).astype(o_ref.dtype)

    R = pa.shape[0]
    return _rows_call(body, rows=R, tr=_pick(R, (256, 128)), ins=[pa, pb, gates],
                      outs=[jax.ShapeDtypeStruct(pa.shape, BF16)], name=name)[0]


def _merge_bwd(pa, pb, gates, dm, name):
    def body(pa_ref, pb_ref, g_ref, dm_ref, dpa_ref, dpb_ref, dg_ref):
        _, vjp = jax.vjp(_merge_fn, pa_ref[...], pb_ref[...], g_ref[...])
        dpa, dpb, dg = vjp(dm_ref[...].astype(F32))
        dpa_ref[...] = dpa.astype(dpa_ref.dtype)
        dpb_ref[...] = dpb.astype(dpb_ref.dtype)
        dg_ref[...] = dg.astype(dg_ref.dtype)

    R = pa.shape[0]
    return _rows_call(body, rows=R, tr=_pick(R, (256, 128)), ins=[pa, pb, gates, dm],
                      outs=[jax.ShapeDtypeStruct(pa.shape, BF16), jax.ShapeDtypeStruct(pa.shape, BF16),
                            jax.ShapeDtypeStruct(gates.shape, BF16)], name=name)


def _loss_head(h3, w, target, nseq, name):
    Tp = h3.shape[0] // nseq
    nc = Tp // CHUNK

    def fn(h, w_, t, valid):
        y = _rms_fn(h, w_)
        e = (y - t) * valid
        return 0.5 * jnp.sum(jnp.mean(e * e, axis=-1, keepdims=True))

    def body(h_ref, w_ref, t_ref, loss_ref, dh_ref, dw_ref):
        b, c = pl.program_id(0), pl.program_id(1)
        valid = (c >= 1).astype(F32)
        t = t_ref[...]
        loss, vjp = jax.vjp(lambda h, w_: fn(h, w_, t, valid), h_ref[...], w_ref[...])
        dh, dw = vjp(jnp.ones((), F32))
        dh_ref[...] = dh

        @pl.when((b == 0) & (c == 0))
        def _():
            loss_ref[...] = jnp.zeros_like(loss_ref)
            dw_ref[...] = jnp.zeros_like(dw_ref)

        loss_ref[...] += jnp.full(loss_ref.shape, loss, F32)
        dw_ref[0:1, :] += dw

    return pl.pallas_call(
        body, grid=(nseq, nc),
        in_specs=[pl.BlockSpec((CHUNK, D_MODEL), lambda b, c: (b * nc + c, 0)),
                  pl.BlockSpec((1, D_MODEL), lambda b, c: (0, 0)),
                  pl.BlockSpec((None, CHUNK, D_MODEL), lambda b, c: (b, jnp.maximum(c - 1, 0), 0))],
        out_specs=[pl.BlockSpec((8, 128), lambda b, c: (0, 0)),
                   pl.BlockSpec((CHUNK, D_MODEL), lambda b, c: (b * nc + c, 0)),
                   pl.BlockSpec((8, D_MODEL), lambda b, c: (0, 0))],
        out_shape=[jax.ShapeDtypeStruct((8, 128), F32), jax.ShapeDtypeStruct(h3.shape, F32),
                   jax.ShapeDtypeStruct((8, D_MODEL), F32)],
        name=name, compiler_params=_cparams(("arbitrary", "arbitrary")),
    )(h3, w, target)


CONV_TILE = 512
CONV_HALO = 8


def _conv_fwd(xbc, w, b, pad, name):
    B, Tp, C = xbc.shape
    nch = Tp // CHUNK

    def body(x_ref, w_ref, b_ref, o_ref, xp):
        xp[0:CONV_HALO, :] = jnp.zeros((CONV_HALO, CONV_TILE), F32)
        xp[CONV_HALO:, :] = x_ref[...]
        for c in range(nch):
            acc = jnp.zeros((CHUNK, CONV_TILE), F32) + b_ref[...]
            for k in range(SSD_CONV):
                acc = acc + w_ref[k:k + 1, :] * xp[pl.ds(CONV_HALO + CHUNK * c - (SSD_CONV - 1) + k, CHUNK), :]
            row = CHUNK * c + lax.broadcasted_iota(jnp.int32, (CHUNK, 1), 0)
            o_ref[pl.ds(CHUNK * c, CHUNK), :] = jnp.where(row >= pad, _silu(acc), 0.0)

    return pl.pallas_call(
        body, grid=(B, C // CONV_TILE),
        in_specs=[pl.BlockSpec((None, Tp, CONV_TILE), lambda i, j: (i, 0, j)),
                  pl.BlockSpec((SSD_CONV, CONV_TILE), lambda i, j: (0, j)),
                  pl.BlockSpec((1, CONV_TILE), lambda i, j: (0, j))],
        out_specs=pl.BlockSpec((None, Tp, CONV_TILE), lambda i, j: (i, 0, j)),
        out_shape=jax.ShapeDtypeStruct(xbc.shape, F32),
        scratch_shapes=[pltpu.VMEM((Tp + CONV_HALO, CONV_TILE), F32)],
        name=name, compiler_params=_cparams(("arbitrary", "arbitrary")),
    )(xbc, w, b)


def _conv_bwd(xbc, w, b, dact, pad, name):
    B, Tp, C = xbc.shape
    nch = Tp // CHUNK

    def body(x_ref, w_ref, b_ref, da_ref, dx_ref, dw_ref, db_ref, xp, dp):
        bi = pl.program_id(1)
        xp[0:CONV_HALO, :] = jnp.zeros((CONV_HALO, CONV_TILE), F32)
        xp[CONV_HALO:, :] = x_ref[...]
        dp[pl.ds(Tp, CONV_HALO), :] = jnp.zeros((CONV_HALO, CONV_TILE), F32)
        dws = [jnp.zeros((1, CONV_TILE), F32) for _ in range(SSD_CONV)]
        dbs = jnp.zeros((1, CONV_TILE), F32)
        for c in range(nch):
            xs = [xp[pl.ds(CONV_HALO + CHUNK * c - (SSD_CONV - 1) + k, CHUNK), :] for k in range(SSD_CONV)]
            acc = jnp.zeros((CHUNK, CONV_TILE), F32) + b_ref[...]
            for k in range(SSD_CONV):
                acc = acc + w_ref[k:k + 1, :] * xs[k]
            row = CHUNK * c + lax.broadcasted_iota(jnp.int32, (CHUNK, 1), 0)
            sg = jax.nn.sigmoid(acc)
            dpre = jnp.where(row >= pad, da_ref[pl.ds(CHUNK * c, CHUNK), :] * (sg * (1.0 + acc * (1.0 - sg))), 0.0)
            dp[pl.ds(CHUNK * c, CHUNK), :] = dpre
            dbs = dbs + jnp.sum(dpre, axis=0, keepdims=True)
            for k in range(SSD_CONV):
                dws[k] = dws[k] + jnp.sum(dpre * xs[k], axis=0, keepdims=True)
        for c in range(nch):
            acc = jnp.zeros((CHUNK, CONV_TILE), F32)
            for k in range(SSD_CONV):
                acc = acc + w_ref[k:k + 1, :] * dp[pl.ds(CHUNK * c + (SSD_CONV - 1) - k, CHUNK), :]
            dx_ref[pl.ds(CHUNK * c, CHUNK), :] = acc

        @pl.when(bi == 0)
        def _():
            dw_ref[...] = jnp.zeros_like(dw_ref)
            db_ref[...] = jnp.zeros_like(db_ref)

        for k in range(SSD_CONV):
            dw_ref[k:k + 1, :] += dws[k]
        db_ref[0:1, :] += dbs

    return pl.pallas_call(
        body, grid=(C // CONV_TILE, B),
        in_specs=[pl.BlockSpec((None, Tp, CONV_TILE), lambda j, i: (i, 0, j)),
                  pl.BlockSpec((SSD_CONV, CONV_TILE), lambda j, i: (0, j)),
                  pl.BlockSpec((1, CONV_TILE), lambda j, i: (0, j)),
                  pl.BlockSpec((None, Tp, CONV_TILE), lambda j, i: (i, 0, j))],
        out_specs=[pl.BlockSpec((None, Tp, CONV_TILE), lambda j, i: (i, 0, j)),
                   pl.BlockSpec((8, CONV_TILE), lambda j, i: (0, j)),
                   pl.BlockSpec((8, CONV_TILE), lambda j, i: (0, j))],
        out_shape=[jax.ShapeDtypeStruct(xbc.shape, F32), jax.ShapeDtypeStruct((8, C), F32),
                   jax.ShapeDtypeStruct((8, C), F32)],
        scratch_shapes=[pltpu.VMEM((Tp + CONV_HALO, CONV_TILE), F32), pltpu.VMEM((Tp + CONV_HALO, CONV_TILE), F32)],
        name=name, compiler_params=_cparams(("arbitrary", "arbitrary")),
    )(xbc, w, b, dact)


def _ssd_chunk(xs, bm, cm, dtr, z, state, dt_bias, a_log, dskip, norm_w, valid):
    Q = xs.shape[0]
    lane = lax.broadcasted_iota(jnp.int32, (1, 128), 1)
    dt = jnp.where(lane < SSD_HEADS, _softplus(dtr + dt_bias), 0.0) * valid
    a = dt * (-jnp.exp(a_log))
    tril = _tril(Q)
    cs = jnp.dot(tril.astype(F32), a, precision=HIGHEST)
    cs_t = cs.T
    cs_end = _row_of(cs, Q - 1)
    low = lane < SSD_HEAD_DIM
    low_rows = lax.broadcasted_iota(jnp.int32, (128, 1), 0) < SSD_HEAD_DIM
    ys, new_state = [], []
    for g in range(SSD_GROUPS):
        bg = bm[:, 128 * g:128 * (g + 1)]
        cg = cm[:, 128 * g:128 * (g + 1)]
        cb = _mm_nt(cg, bg)
        for pr in range(2):
            p = 2 * g + pr
            h0, h1 = 2 * p, 2 * p + 1
            xp = xs[:, 128 * p:128 * (p + 1)]
            c0, c1 = _col_of(cs, h0), _col_of(cs, h1)
            e0, e1 = _col_of(cs_end, h0), _col_of(cs_end, h1)
            xd = xp * jnp.where(low, _col_of(dt, h0), _col_of(dt, h1))
            l0 = jnp.exp(jnp.where(tril, c0 - _row_of(cs_t, h0), -1e30))
            l1 = jnp.exp(jnp.where(tril, c1 - _row_of(cs_t, h1), -1e30))
            y_diag = jnp.where(low, _mm(cb * l0, xd), _mm(cb * l1, xd))
            to_end = jnp.where(low, jnp.exp(e0 - c0), jnp.exp(e1 - c1))
            sp = state[128 * p:128 * (p + 1), :]
            y_off = _mm_nt(cg, sp) * jnp.where(low, jnp.exp(c0), jnp.exp(c1))
            new_state.append(sp * jnp.where(low_rows, jnp.exp(e0), jnp.exp(e1)) + _mm_tn(xd * to_end, bg))
            ys.append(y_diag + y_off + xp * jnp.where(low, _col_of(dskip, h0), _col_of(dskip, h1)))
    y = jnp.concatenate(ys, axis=1) * _silu(z)
    gw = SSD_INNER // SSD_GROUPS
    outs = []
    for g in range(SSD_GROUPS):
        blk = y[:, gw * g:gw * (g + 1)]
        outs.append(blk * lax.rsqrt(jnp.mean(blk * blk, axis=-1, keepdims=True) + EPS))
    return jnp.concatenate(outs, axis=1) * norm_w, jnp.concatenate(new_state, axis=0)


def _valid_rows(c, pad):
    row = c * CHUNK + lax.broadcasted_iota(jnp.int32, (CHUNK, 1), 0)
    return (row >= pad).astype(F32)


def _ssd_fwd(xact, dtr, z, dt_bias, a_log, dskip, norm_w, pad, name):
    B, Tp, _ = xact.shape
    nc = Tp // CHUNK

    def body(xs_ref, bm_ref, cm_ref, dt_ref, z_ref, db_ref, al_ref, ds_ref, nw_ref, y_ref, save_ref, st):
        c = pl.program_id(1)

        @pl.when(c == 0)
        def _():
            st[...] = jnp.zeros_like(st)

        s0 = st[...]
        save_ref[...] = s0
        y, s1 = _ssd_chunk(xs_ref[...], bm_ref[...], cm_ref[...], dt_ref[...], z_ref[...], s0, db_ref[...],
                           al_ref[...], ds_ref[...], nw_ref[...], _valid_rows(c, pad))
        y_ref[...] = y.astype(y_ref.dtype)
        st[...] = s1

    row = lambda w, off=0: pl.BlockSpec((None, CHUNK, w), lambda b, c: (b, c, off))
    par = lambda w: pl.BlockSpec((1, w), lambda b, c: (0, 0))
    return pl.pallas_call(
        body, grid=(B, nc),
        in_specs=[row(1024, 0), row(512, 2), row(512, 3), row(128), row(1024), par(128), par(128), par(128), par(1024)],
        out_specs=[row(1024), pl.BlockSpec((None, None, 1024, 128), lambda b, c: (b, c, 0, 0))],
        out_shape=[jax.ShapeDtypeStruct((B, Tp, SSD_INNER), BF16), jax.ShapeDtypeStruct((B, nc, 1024, 128), F32)],
        scratch_shapes=[pltpu.VMEM((1024, 128), F32)],
        name=name, compiler_params=_cparams(("arbitrary", "arbitrary")),
    )(xact, xact, xact, dtr, z, dt_bias, a_log, dskip, norm_w)


def _ssd_bwd(xact, dtr, z, dt_bias, a_log, dskip, norm_w, saved, dy, pad, name, after=None):
    B, Tp, _ = xact.shape
    nc = Tp // CHUNK

    def body(xs_ref, bm_ref, cm_ref, dt_ref, z_ref, db_ref, al_ref, ds_ref, nw_ref, sv_ref, dy_ref,
             dx_ref, ddt_ref, dz_ref, dpar_ref, dnw_ref, dst):
        b, i = pl.program_id(0), pl.program_id(1)
        c = nc - 1 - i

        @pl.when(i == 0)
        def _():
            dst[...] = jnp.zeros_like(dst)

        valid = _valid_rows(c, pad)
        fn = lambda *a: _ssd_chunk(*a, valid)
        _, vjp = jax.vjp(fn, xs_ref[...], bm_ref[...], cm_ref[...], dt_ref[...], z_ref[...], sv_ref[...],
                         db_ref[...], al_ref[...], ds_ref[...], nw_ref[...])
        dxs, dbm, dcm, ddt, dz, dstate, ddb, dal, dds, dnw = vjp((dy_ref[...].astype(F32), dst[...]))
        dx_ref[:, 0:1024] = dxs
        dx_ref[:, 1024:1536] = dbm
        dx_ref[:, 1536:2048] = dcm
        ddt_ref[...] = ddt
        dz_ref[...] = dz
        dst[...] = dstate

        @pl.when((b == 0) & (i == 0))
        def _():
            dpar_ref[...] = jnp.zeros_like(dpar_ref)
            dnw_ref[...] = jnp.zeros_like(dnw_ref)

        dpar_ref[0:1, :] += ddb
        dpar_ref[1:2, :] += dal
        dpar_ref[2:3, :] += dds
        dnw_ref[0:1, :] += dnw

    row = lambda w, off=0: pl.BlockSpec((None, CHUNK, w), lambda b, i: (b, nc - 1 - i, off))
    par = lambda w: pl.BlockSpec((1, w), lambda b, i: (0, 0))
    acc = lambda w: pl.BlockSpec((8, w), lambda b, i: (0, 0))
    in_specs = [row(1024, 0), row(512, 2), row(512, 3), row(128), row(1024), par(128), par(128), par(128), par(1024),
                pl.BlockSpec((None, None, 1024, 128), lambda b, i: (b, nc - 1 - i, 0, 0)), row(1024)]
    args = [xact, xact, xact, dtr, z, dt_bias, a_log, dskip, norm_w, saved, dy]
    if after is not None:
        body = _skip_ref(body, len(args))
        args.append(_deps(after))
        in_specs.append(_dep_spec(args[-1]))
    outs = pl.pallas_call(
        body, grid=(B, nc), in_specs=in_specs,
        out_specs=[row(2048), row(128), row(1024), acc(128), acc(1024)],
        out_shape=[jax.ShapeDtypeStruct((B, Tp, 2048), F32), jax.ShapeDtypeStruct((B, Tp, 128), F32),
                   jax.ShapeDtypeStruct((B, Tp, 1024), F32), jax.ShapeDtypeStruct((8, 128), F32),
                   jax.ShapeDtypeStruct((8, 1024), F32)],
        scratch_shapes=[pltpu.VMEM((1024, 128), F32)],
        name=name, compiler_params=_cparams(("arbitrary", "arbitrary")),
    )(*args)
    return outs


def _hg_chunk(qr, fr, ir, gr, state_t, p0, p1, norm_w, valid):
    Q = qr.shape[0]
    lb = jax.nn.sigmoid(p0 - p1)
    f = lb + (1.0 - lb) * jax.nn.sigmoid(fr)
    k = 1.0 - f
    q = _silu(qr)
    v = ir * valid
    cum = jnp.dot(_tril(Q).astype(F32), jnp.log(f), precision=HIGHEST)
    cum_end = _row_of(cum, Q - 1)
    o_inter = _mm_nt(q * jnp.exp(cum), state_t)
    nblk = Q // HG_SUB
    row = lax.broadcasted_iota(jnp.int32, (Q, 1), 0)
    ri = lax.broadcasted_iota(jnp.int32, (Q, Q), 0)
    ci = lax.broadcasted_iota(jnp.int32, (Q, Q), 1)
    mids = jnp.concatenate([jnp.broadcast_to(_row_of(cum, HG_SUB * i + HG_SUB // 2 - 1), (HG_SUB, cum.shape[1]))
                            for i in range(nblk)], axis=0)
    sh = HG_SUB.bit_length() - 1
    same = (jnp.right_shift(ri, sh) == jnp.right_shift(ci, sh)) & (ri >= ci)
    att = jnp.where(same, _mm_nt(q * jnp.exp(cum - mids), k * jnp.exp(mids - cum)), 0.0)
    for i in range(1, nblk):
        lo = HG_SUB * i
        start = _row_of(cum, lo - 1)
        qa = q * jnp.exp(jnp.where((row >= lo) & (row < lo + HG_SUB), cum - start, -1e30))
        ka = k * jnp.exp(jnp.where(row < lo, start - cum, -1e30))
        att = att + _mm_nt(qa, ka)
    o = o_inter + _mm(att, v)
    new_state_t = state_t * jnp.exp(cum_end) + _mm_tn(v, k * jnp.exp(cum_end - cum))
    o = o * lax.rsqrt(jnp.mean(o * o, axis=-1, keepdims=True) + EPS) * norm_w
    return o * _silu(gr), new_state_t


HG_PER_STEP = 4
HG_COLS = 4 * 128


def _hg_fwd(qfig, lbh, nwh, pad, name):
    B, Tp, _ = qfig.shape
    nc = Tp // CHUNK
    hp = HG_PER_STEP

    def body(x_ref, lb_ref, nw_ref, y_ref, save_ref, st):
        c = pl.program_id(1)

        @pl.when(c == 0)
        def _():
            st[...] = jnp.zeros_like(st)

        valid = _valid_rows(c, pad)
        for j in range(hp):
            for b in range(B):
                s0 = st[j, b]
                save_ref[j, b] = s0
                col = lambda k: x_ref[b, :, HG_COLS * j + 128 * k:HG_COLS * j + 128 * (k + 1)]
                y, s1 = _hg_chunk(col(0), col(1), col(2), col(3), s0, lb_ref[j, 0:1, :], lb_ref[j, 1:2, :], nw_ref[j], valid)
                y_ref[b, :, 128 * j:128 * (j + 1)] = y.astype(y_ref.dtype)
                st[j, b] = s1

    return pl.pallas_call(
        body, grid=(HG_HEADS // hp, nc),
        in_specs=[pl.BlockSpec((B, CHUNK, HG_COLS * hp), lambda h, c: (0, c, h)),
                  pl.BlockSpec((hp, 2, 128), lambda h, c: (h, 0, 0)),
                  pl.BlockSpec((hp, 1, 128), lambda h, c: (h, 0, 0))],
        out_specs=[pl.BlockSpec((B, CHUNK, 128 * hp), lambda h, c: (0, c, h)),
                   pl.BlockSpec((hp, B, None, 128, 128), lambda h, c: (h, 0, c, 0, 0))],
        out_shape=[jax.ShapeDtypeStruct((B, Tp, 1024), BF16), jax.ShapeDtypeStruct((HG_HEADS, B, nc, 128, 128), F32)],
        scratch_shapes=[pltpu.VMEM((hp, B, 128, 128), F32)],
        name=name, compiler_params=_cparams(("arbitrary", "arbitrary")),
    )(qfig, lbh, nwh)


def _hg_bwd(qfig, lbh, nwh, saved, dy, pad, name, after=None):
    B, Tp, _ = qfig.shape
    nc = Tp // CHUNK
    hp = HG_PER_STEP

    def body(x_ref, lb_ref, nw_ref, sv_ref, dy_ref, dx_ref, dlb_ref, dnw_ref, dst):
        i = pl.program_id(1)
        c = nc - 1 - i

        @pl.when(i == 0)
        def _():
            dst[...] = jnp.zeros_like(dst)
            dlb_ref[...] = jnp.zeros_like(dlb_ref)
            dnw_ref[...] = jnp.zeros_like(dnw_ref)

        valid = _valid_rows(c, pad)
        fn = lambda *a: _hg_chunk(*a, valid)
        for j in range(hp):
            for b in range(B):
                col = lambda k: x_ref[b, :, HG_COLS * j + 128 * k:HG_COLS * j + 128 * (k + 1)]
                _, vjp = jax.vjp(fn, col(0), col(1), col(2), col(3), sv_ref[j, b], lb_ref[j, 0:1, :], lb_ref[j, 1:2, :], nw_ref[j])
                d4 = vjp((dy_ref[b, :, 128 * j:128 * (j + 1)].astype(F32), dst[j, b]))
                for k in range(4):
                    dx_ref[b, :, HG_COLS * j + 128 * k:HG_COLS * j + 128 * (k + 1)] = d4[k].astype(dx_ref.dtype)
                dst[j, b] = d4[4]
                dlb_ref[j, 0:1, :] += d4[5]
                dlb_ref[j, 1:2, :] += d4[6]
                dnw_ref[j, 0:1, :] += d4[7]

    acc = pl.BlockSpec((hp, 8, 128), lambda h, i: (h, 0, 0))
    in_specs = [pl.BlockSpec((B, CHUNK, HG_COLS * hp), lambda h, i: (0, nc - 1 - i, h)),
                pl.BlockSpec((hp, 2, 128), lambda h, i: (h, 0, 0)),
                pl.BlockSpec((hp, 1, 128), lambda h, i: (h, 0, 0)),
                pl.BlockSpec((hp, B, None, 128, 128), lambda h, i: (h, 0, nc - 1 - i, 0, 0)),
                pl.BlockSpec((B, CHUNK, 128 * hp), lambda h, i: (0, nc - 1 - i, h))]
    args = [qfig, lbh, nwh, saved, dy]
    if after is not None:
        body = _skip_ref(body, len(args))
        args.append(_deps(after))
        in_specs.append(_dep_spec(args[-1]))
    return pl.pallas_call(
        body, grid=(HG_HEADS // hp, nc), in_specs=in_specs,
        out_specs=[pl.BlockSpec((B, CHUNK, HG_COLS * hp), lambda h, i: (0, nc - 1 - i, h)), acc, acc],
        out_shape=[jax.ShapeDtypeStruct((B, Tp, 4096), BF16), jax.ShapeDtypeStruct((HG_HEADS, 8, 128), F32),
                   jax.ShapeDtypeStruct((HG_HEADS, 8, 128), F32)],
        scratch_shapes=[pltpu.VMEM((hp, B, 128, 128), F32)],
        name=name, compiler_params=_cparams(("arbitrary", "arbitrary")),
    )(*args)


def _adamw(w, g, m, v, name, after=None):
    R, C = w.shape
    tr = max(t for t in range(8, R + 1, 8) if R % t == 0 and (t * C * 4 <= ADAMW_BLOCK_BYTES or t == 8))

    def body(w_ref, g_ref, m_ref, v_ref, d_ref, mo_ref, vo_ref):
        g_ = g_ref[...]
        m_ = ADAM_B1 * m_ref[...] + (1.0 - ADAM_B1) * g_
        v_ = ADAM_B2 * v_ref[...] + (1.0 - ADAM_B2) * (g_ * g_)
        m_hat = m_ / (1.0 - ADAM_B1 ** ADAM_STEP)
        v_hat = v_ / (1.0 - ADAM_B2 ** ADAM_STEP)
        d_ref[...] = -ADAM_LR * (m_hat / (jnp.sqrt(v_hat) + ADAM_EPS) + ADAM_WD * w_ref[...])
        mo_ref[...] = m_
        vo_ref[...] = v_

    sp = pl.BlockSpec((tr, C), lambda i: (i, 0))
    sh = jax.ShapeDtypeStruct((R, C), F32)
    in_specs, args = [sp] * 4, [w, g, m, v]
    if after is not None:
        body = _skip_ref(body, len(args))
        args.append(_deps(after))
        in_specs.append(_dep_spec(args[-1]))
    return pl.pallas_call(body, grid=(R // tr,), in_specs=in_specs, out_specs=[sp] * 3, out_shape=[sh] * 3,
                          name=name, compiler_params=_cparams(("arbitrary",)))(*args)


def _ffn_fwd(h, norm_w, w_gu, w_down, tag, after_norm=None):
    n = _rms_fwd(h, norm_w, f"{tag}_norm")
    if after_norm is not None:
        after_norm(n)
    gu, a = _gu_swiglu(n, w_gu, f"{tag}_gu")
    out = _matmul(a, w_down, mode="nn", out_dtype=F32, alpha=0.5, res=h, name=f"{tag}_down")
    return out, (n, gu, a)


def _ffn_bwd(h, norm_w, w_gu, w_down, saved, dout, tag, after_dw_down=None):
    n, gu, a = saved
    dgu = _d_swiglu(dout, w_down, gu, 0.5, f"{tag}_d_gu")
    dw_down = _matmul(a, dout, mode="tn", out_dtype=F32, alpha=0.5, name=f"{tag}_dw_down")
    dw_gu = _matmul(n, dgu, mode="tn", out_dtype=F32, out_groups=N_CHIPS, name=f"{tag}_dw_gu",
                    after=after_dw_down(dw_down) if after_dw_down else None)
    dn = _matmul(dgu, w_gu, mode="nt", out_dtype=F32, name=f"{tag}_d_norm")
    dh, dnw = _rms_bwd(h, norm_w, dn, dout, f"{tag}_d_in")
    return dh, dnw, dw_gu, dw_down


IN_NAMES = ("z", "xbc", "dt", "q", "f", "i", "g", "gates")


def _split_w_in(w_in_full):
    pts = [0]
    for s in IN_SIZES:
        pts.append(pts[-1] + s)
    sl = lambda i, j: w_in_full[:, pts[i]:pts[j]]
    qfig = sl(3, 7).reshape(D_MODEL, 4, HG_HEADS, 128).transpose(0, 2, 1, 3).reshape(D_MODEL, 4 * D_MODEL)
    return {"z": sl(0, 1), "xbc": sl(1, 2), "dt": jnp.pad(sl(2, 3), ((0, 0), (0, 128 - SSD_HEADS))),
            "qfig": qfig, "gates": sl(7, 9)}


def _local_step(x, target, W):
    B, S, _ = x.shape
    T = N_META + S
    pad = (-T) % CHUNK
    Tp = T + pad
    assert pad + N_META == CHUNK
    R = B * Tp
    meta = jnp.broadcast_to(W["meta_tokens"][None], (B, N_META, D_MODEL))
    h0 = jnp.concatenate([jnp.zeros((B, pad, D_MODEL), F32), meta, x], axis=1).reshape(R, D_MODEL)

    stage = W.get("_stage", lambda name, x: {})
    W = dict(W)
    h1, sv1 = _ffn_fwd(h0, W["ffn1_norm"], W["ffn1_w_gu"], W["ffn1_w_down"], "ffn1", lambda n: W.update(stage("ffn1_norm", n)))
    W.update(stage("ffn1_out", h1))
    um = _rms_fwd(h1, W["mix_norm"], "mix_norm")
    wi = W["w_in"]
    z = _matmul(um, wi["z"], mode="nn", out_dtype=F32, name="in_z")
    xbc = _matmul(um, wi["xbc"], mode="nn", out_dtype=F32, name="in_xbc")
    dtr = _matmul(um, wi["dt"], mode="nn", out_dtype=F32, name="in_dt")
    qfig = _matmul(um, wi["qfig"], mode="nn", out_dtype=F32, name="in_qfig")
    gates = _matmul(um, wi["gates"], mode="nn", out_dtype=F32, name="in_gates")

    r3 = lambda t: t.reshape(B, Tp, t.shape[-1])
    lane_pad = lambda t: jnp.pad(t, ((0, 0), (0, 128 - t.shape[1])))
    dt_bias, a_log, dskip = lane_pad(W["ssd_dt_bias"]), lane_pad(W["ssd_a_log"]), lane_pad(W["ssd_d"])
    xact = _conv_fwd(r3(xbc), W["ssd_conv_w"], W["ssd_conv_b"], pad, "conv_fwd")
    ya, ssd_saved = _ssd_fwd(xact, r3(dtr), r3(z), dt_bias, a_log, dskip, W["ssd_norm"], pad, "ssd_fwd")
    lbh = W["hg_lower_bound"].reshape(2, HG_HEADS, 128).transpose(1, 0, 2)
    nwh = W["hg_norm"].reshape(HG_HEADS, 1, 128)
    yb, hg_saved = _hg_fwd(r3(qfig), lbh, nwh, pad, "hg_fwd")
    ya2, yb2 = ya.reshape(R, -1), yb.reshape(R, -1)
    W.update(stage("mixers_out", yb2))
    pa = _matmul(ya2, W["w_branch_a"], mode="nn", out_dtype=F32, name="branch_a")
    pb = _matmul(yb2, W["w_branch_b"], mode="nn", out_dtype=F32, name="branch_b")
    mg = _merge_fwd(pa, pb, gates, "merge")
    h2 = _matmul(mg, W["w_out"], mode="nn", out_dtype=F32, res=h1, name="mix_out")
    h3, sv2 = _ffn_fwd(h2, W["ffn2_norm"], W["ffn2_w_gu"], W["ffn2_w_down"], "ffn2")

    loss, dh3, d_final = _loss_head(h3, W["final_norm"].reshape(1, D_MODEL), target, B, "loss_head")

    G = {"final_norm": d_final[0]}
    dh2, dnw, G["ffn2_w_gu"], G["ffn2_w_down"] = _ffn_bwd(h2, W["ffn2_norm"], W["ffn2_w_gu"], W["ffn2_w_down"], sv2, dh3, "ffn2")
    G["ffn2_norm"] = dnw[0:1]
    dmg = _matmul(dh2, W["w_out"], mode="nt", out_dtype=BF16, name="d_merge")
    G["w_out"] = _matmul(mg, dh2, mode="tn", out_dtype=F32, name="dw_out")
    dpa, dpb, dgates = _merge_bwd(pa, pb, gates, dmg, "merge_bwd")
    dya = _matmul(dpa, W["w_branch_a"], mode="nt", out_dtype=BF16, name="d_ya")
    dyb = _matmul(dpb, W["w_branch_b"], mode="nt", out_dtype=BF16, name="d_yb")
    G["w_branch_a"] = _matmul(ya2, dpa, mode="tn", out_dtype=F32, name="dw_branch_a")
    G["w_branch_b"] = _matmul(yb2, dpb, mode="tn", out_dtype=F32, name="dw_branch_b")

    dxact, ddtr, dz, dpar, dnw = _ssd_bwd(xact, r3(dtr), r3(z), dt_bias, a_log, dskip, W["ssd_norm"], ssd_saved,
                                          r3(dya), pad, "ssd_bwd", after=stage("late_grads", G).get("_after"))
    G["ssd_dt_bias"], G["ssd_a_log"], G["ssd_d"] = dpar[0:1, :SSD_HEADS], dpar[1:2, :SSD_HEADS], dpar[2:3, :SSD_HEADS]
    G["ssd_norm"] = dnw[0:1]
    dxbc, dcw, dcb = _conv_bwd(r3(xbc), W["ssd_conv_w"], W["ssd_conv_b"], dxact, pad, "conv_bwd")
    G["ssd_conv_w"], G["ssd_conv_b"] = dcw[0:SSD_CONV], dcb[0:1]
    dqfig, dlb, dhn = _hg_bwd(r3(qfig), lbh, nwh, hg_saved, r3(dyb), pad, "hg_bwd",
                              after=stage("after_conv_bwd", dcb).get("_after"))
    G["hg_lower_bound"] = dlb[:, 0:2, :].transpose(1, 0, 2).reshape(2, D_MODEL)
    G["hg_norm"] = dhn[:, 0, :].reshape(1, D_MODEL)

    r2 = lambda t: t.reshape(R, t.shape[-1])
    pieces = [("z", r2(dz)), ("xbc", r2(dxbc)), ("dt", r2(ddtr)), ("qfig", r2(dqfig)), ("gates", dgates)]
    dum = None
    dwi = {}
    for nm, dpiece in pieces:
        dum = _matmul(dpiece, wi[nm], mode="nt", out_dtype=F32, res=dum, name=f"d_mix_{nm}")
        dwi[nm] = _matmul(um, dpiece, mode="tn", out_dtype=F32, name=f"dw_in_{nm}")
    dw_qfig = dwi["qfig"].reshape(D_MODEL, HG_HEADS, 4, 128).transpose(0, 2, 1, 3).reshape(D_MODEL, 4 * D_MODEL)
    G["w_in"] = jnp.concatenate([dwi["z"], dwi["xbc"], dwi["dt"][:, :SSD_HEADS], dw_qfig, dwi["gates"]], axis=1)
    dh1, dnw = _rms_bwd(h1, W["mix_norm"], dum, dh2, "mix_norm_bwd", after=stage("w_in_grads", dwi).get("_after"))
    G["mix_norm"] = dnw[0:1]
    dh0, dnw, G["ffn1_w_gu"], G["ffn1_w_down"] = _ffn_bwd(h0, W["ffn1_norm"], W["ffn1_w_gu"], W["ffn1_w_down"], sv1, dh1, "ffn1",
                                                           lambda dw: stage("ffn1_dw_down", dw).get("_after"))
    G["ffn1_norm"] = dnw[0:1]
    dh0 = dh0.reshape(B, Tp, D_MODEL)
    G["meta_tokens"] = jnp.sum(dh0[:, pad:CHUNK], axis=0)
    return loss, dh0[:, CHUNK:], G


ANY = pl.BlockSpec(memory_space=pl.ANY)


def _place():
    return lax.axis_index("x"), lax.axis_index("y"), lax.axis_index("c")


def _other_chips(x, y):
    return [(1 - x, y), (x, 1 - y), (1 - x, 1 - y)]


def _remote(src, dst, ssem, rsem, dev):
    return pltpu.make_async_remote_copy(src_ref=src, dst_ref=dst, send_sem=ssem, recv_sem=rsem,
                                        device_id=dev, device_id_type=MESH)


def _exchange8(buf, reduce, name):
    n, w = buf.shape

    def body(x_ref, *rest):
        if reduce:
            red_ref, out_ref, ssem, rsem = rest
        else:
            out_ref, ssem, rsem = rest
        x, y, c = _place()
        me = 4 * x + 2 * y + c
        out_ref[me] = x_ref[...]
        copies = []
        for k in range(1, 8):
            px = 1 - x if (k >> 2) & 1 else x
            py = 1 - y if (k >> 1) & 1 else y
            pc = 1 - c if k & 1 else c
            cp = _remote(x_ref, out_ref.at[me], ssem.at[k - 1], rsem.at[k - 1], (px, py, pc))
            cp.start()
            copies.append((cp, 4 * px + 2 * py + pc))
        for k, (cp, peer) in enumerate(copies):
            _remote(x_ref, out_ref.at[peer], ssem.at[k], rsem.at[k], (x, y, c)).wait_recv()
        for cp, _ in copies:
            cp.wait_send()
        if reduce:
            acc = out_ref[0]
            for d in range(1, 8):
                acc = acc + out_ref[d]
            red_ref[...] = acc

    vm = pl.BlockSpec(memory_space=pltpu.VMEM)
    g_shape = jax.ShapeDtypeStruct((8, n, w), F32)
    if reduce:
        out_shape, out_specs, scratch = [jax.ShapeDtypeStruct((n, w), F32)], [vm], [pltpu.VMEM((8, n, w), F32)]
    else:
        out_shape, out_specs, scratch = [g_shape], [vm], []
    return pl.pallas_call(
        body, in_specs=[vm], out_specs=out_specs, out_shape=out_shape,
        scratch_shapes=scratch + [pltpu.SemaphoreType.DMA((7,)), pltpu.SemaphoreType.DMA((7,))], name=name,
    )(buf)[0]


HBM = pltpu.MemorySpace.HBM


def _sequencer(name, collective_id, sems, sent):
    return functools.partial(pl.kernel, mesh=plsc.ScalarSubcoreMesh(axis_name="sequencer", num_cores=1), name=name,
                             scratch_types=sems, compiler_params=pltpu.CompilerParams(collective_id=collective_id),
                             cost_estimate=pl.CostEstimate(flops=0, transcendentals=0, bytes_accessed=2 * sent,
                                                           remote_bytes_transferred=sent))


def _nbytes(arrays):
    return sum(a.size * a.dtype.itemsize for a in arrays)


def _handshake(peers):
    barrier = pltpu.get_barrier_semaphore()
    for peer in peers:
        pl.semaphore_signal(barrier, inc=1, device_id=peer, device_id_type=MESH)
    pl.semaphore_wait(barrier, len(peers))


def _gather_seq(blocks, name, collective_id):
    n = len(blocks)
    half = [s.shape[1] // 2 for s in blocks]
    full = [jax.new_ref(b, memory_space=HBM) for b in blocks]

    @_sequencer(name, collective_id, [pltpu.SemaphoreType.DMA((n, 3))] * 4, _nbytes(blocks) * 3 // 4)
    def launch(ssem, rsem, fssem, frsem):
        x, y, c = _place()
        q = 2 * x + y
        chips = _other_chips(x, y)
        _handshake([(px, py, c) for px, py in chips] + [(x, y, 1 - c)])
        piece = lambda s, qq, cc: full[s].at[qq, pl.ds(cc * half[s], half[s])]
        sends = []
        for j, (px, py) in enumerate(chips):
            for s in range(n):
                cp = _remote(piece(s, q, c), piece(s, q, c), ssem.at[s, j], rsem.at[s, j], (px, py, c))
                cp.start()
                sends.append(cp)
        for j, (px, py) in enumerate(chips):
            for s in range(n):
                got = piece(s, 2 * px + py, c)
                _remote(got, got, ssem.at[s, j], rsem.at[s, j], (px, py, c)).wait_recv()
                cp = _remote(got, got, fssem.at[s, j], frsem.at[s, j], (x, y, 1 - c))
                cp.start()
                sends.append(cp)
        for j, (px, py) in enumerate(chips):
            for s in range(n):
                got = piece(s, 2 * px + py, 1 - c)
                _remote(got, got, fssem.at[s, j], frsem.at[s, j], (x, y, 1 - c)).wait_recv()
        for cp in sends:
            cp.wait_send()

    launch()
    return [r[...] for r in full]


def _pair_swap(parts, name, collective_id):
    n = len(parts)
    half = [p.shape[1] // 2 for p in parts]
    src = [jax.new_ref(p, memory_space=HBM) for p in parts]
    got = [jax.empty_ref(jax.ShapeDtypeStruct((p.shape[0], h, p.shape[2]), p.dtype), memory_space=HBM) for p, h in zip(parts, half)]

    @_sequencer(name, collective_id, [pltpu.SemaphoreType.DMA((n,))] * 2, _nbytes(parts) // 2)
    def launch(ssem, rsem):
        x, y, c = _place()
        _handshake([(x, y, 1 - c)])
        copies = []
        for s in range(n):
            cp = _remote(src[s].at[pl.ds(0, parts[s].shape[0]), pl.ds((1 - c) * half[s], half[s])], got[s], ssem.at[s], rsem.at[s], (x, y, 1 - c))
            cp.start()
            copies.append(cp)
        for cp in copies:
            cp.wait_recv()
        for cp in copies:
            cp.wait_send()

    launch()
    return [g[...] for g in got]


def _to_owners(sums, name, collective_id):
    n = len(sums)
    src = [jax.new_ref(s, memory_space=HBM) for s in sums]
    got = [jax.empty_ref(jax.ShapeDtypeStruct(s.shape, s.dtype), memory_space=HBM) for s in sums]

    @_sequencer(name, collective_id, [pltpu.SemaphoreType.DMA((n, 3))] * 2, _nbytes(sums) * 3 // 4)
    def launch(ssem, rsem):
        x, y, c = _place()
        q = 2 * x + y
        chips = _other_chips(x, y)
        _handshake([(px, py, c) for px, py in chips])
        sends = []
        for j, (px, py) in enumerate(chips):
            for s in range(n):
                cp = _remote(src[s].at[2 * px + py], got[s].at[q], ssem.at[s, j], rsem.at[s, j], (px, py, c))
                cp.start()
                sends.append(cp)
        for j, (px, py) in enumerate(chips):
            for s in range(n):
                slot = got[s].at[2 * px + py]
                _remote(slot, slot, ssem.at[s, j], rsem.at[s, j], (px, py, c)).wait_recv()
        for cp in sends:
            cp.wait_send()

    launch()
    return [g[...] for g in got]


def _pair_join(blocks, name, collective_id):
    n = len(blocks)
    out = [jax.new_ref(b, memory_space=HBM) for b in blocks]

    @_sequencer(name, collective_id, [pltpu.SemaphoreType.DMA((n,))] * 2, _nbytes(blocks) // 2)
    def launch(ssem, rsem):
        x, y, c = _place()
        _handshake([(x, y, 1 - c)])
        sends = []
        for s in range(n):
            h = blocks[s].shape[0] // 2
            mine = out[s].at[pl.ds(c * h, h)]
            cp = _remote(mine, mine, ssem.at[s], rsem.at[s], (x, y, 1 - c))
            cp.start()
            sends.append(cp)
        for s in range(n):
            h = blocks[s].shape[0] // 2
            theirs = out[s].at[pl.ds((1 - c) * h, h)]
            _remote(theirs, theirs, ssem.at[s], rsem.at[s], (x, y, 1 - c)).wait_recv()
        for cp in sends:
            cp.wait_send()

    launch()
    return [o[...] for o in out]


WIRE = BF16


def _row_tile(h):
    return _pick(h, (256, 368, 352, 128, 16))


def _add_pair(part, got, c, name, after=None):
    _, h, w = got.shape
    tr = _row_tile(h)
    nt = h // tr

    def body(c_ref, p_ref, g_ref, o_ref):
        o_ref[...] = (p_ref[...] + g_ref[...].astype(F32)).astype(o_ref.dtype)

    in_specs = [pl.BlockSpec((None, tr, w), lambda q, i, c_ref: (q, c_ref[0] * nt + i, 0)),
                pl.BlockSpec((None, tr, w), lambda q, i, c_ref: (q, i, 0))]
    args = [c.reshape(1).astype(jnp.int32), part, got]
    if after is not None:
        body = _skip_ref(body, len(args))
        args.append(_deps(after))
        in_specs.append(_dep_spec(args[-1]))
    return pl.pallas_call(
        body,
        grid_spec=pltpu.PrefetchScalarGridSpec(
            num_scalar_prefetch=1, grid=(got.shape[0], nt), in_specs=in_specs,
            out_specs=pl.BlockSpec((None, tr, w), lambda q, i, c_ref: (q, i, 0))),
        out_shape=jax.ShapeDtypeStruct(got.shape, WIRE), name=name,
        compiler_params=_cparams(("arbitrary", "arbitrary")),
    )(*args)


def _sum_chips(slots, sums, q, c, name, after=None):
    _, h, w = slots.shape
    tr = _row_tile(h)
    nt = h // tr

    def body(s_ref, mine_ref, a_ref, b_ref, d_ref, o_ref):
        o_ref[...] = ((mine_ref[...].astype(F32) + a_ref[...].astype(F32)) + b_ref[...].astype(F32)) + d_ref[...].astype(F32)

    slot = lambda k: pl.BlockSpec((None, tr, w), lambda i, s_ref: (s_ref[1 + k], i, 0))
    scalars = jnp.stack([c, q, (q + 1) % N_CHIPS, (q + 2) % N_CHIPS, (q + 3) % N_CHIPS]).astype(jnp.int32)
    in_specs, args = [slot(0), slot(1), slot(2), slot(3)], [scalars, sums, slots, slots, slots]
    if after is not None:
        body = _skip_ref(body, len(args))
        args.append(_deps(after))
        in_specs.append(_dep_spec(args[-1]))
    return pl.pallas_call(
        body,
        grid_spec=pltpu.PrefetchScalarGridSpec(
            num_scalar_prefetch=1, grid=(nt,), in_specs=in_specs,
            out_specs=pl.BlockSpec((tr, w), lambda i, s_ref: (s_ref[0] * nt + i, 0))),
        out_shape=jax.ShapeDtypeStruct((2 * h, w), F32), name=name,
        compiler_params=_cparams(("arbitrary",)),
    )(*args)


class _Reduce:
    def __init__(self, parts, q, c, tag, first_id, regions=None):
        self.parts, self.q, self.c, self.tag, self.first_id, self.regions = parts, q, c, tag, first_id, regions
        self.got = _pair_swap(parts, f"{tag}_pair_swap", first_id)

    def to_owners(self, after=None):
        self.sums = [_add_pair(p, g, self.c, f"{self.tag}_pair_add{i}", after)
                     for i, (p, g) in enumerate(zip(self.parts, self.got))]
        if self.regions is not None:
            self.sums = self.regions(self.sums)
        self.slots = _to_owners(self.sums, f"{self.tag}_to_owners", self.first_id + 1)
        return self.sums

    def join(self, after=None):
        blocks = [_sum_chips(sl, sm, self.q, self.c, f"{self.tag}_sum_chips{i}", after)
                  for i, (sl, sm) in enumerate(zip(self.slots, self.sums))]
        self.out = _pair_join(blocks, f"{self.tag}_pair_join", self.first_id + 2)
        return blocks


WEIGHTS = ("meta_tokens", "ffn1_norm", "ffn1_w_gu", "ffn1_w_down", "mix_norm", "w_in", "ssd_conv_w", "ssd_conv_b",
           "ssd_dt_bias", "ssd_a_log", "ssd_d", "ssd_norm", "hg_lower_bound", "hg_norm", "w_branch_a", "w_branch_b",
           "w_out", "ffn2_norm", "ffn2_w_gu", "ffn2_w_down", "final_norm")
BIG = ("ffn1_w_gu", "ffn1_w_down", "w_in", "w_branch_a", "w_branch_b", "w_out", "ffn2_w_gu", "ffn2_w_down")
ROW_SHARDED = ("ffn1_w_down", "ffn2_w_down", "w_branch_a", "w_branch_b", "w_out")
SMALL = tuple(n for n in WEIGHTS if n not in BIG)
SMALL_ROWS = 24


def _rows1024(a):
    flat = a.reshape(-1)
    n = -(-flat.shape[0] // 1024) * 1024
    return jnp.pad(flat, (0, n - flat.shape[0])).reshape(-1, 1024)


def _pack_small(d):
    rows = jnp.concatenate([_rows1024(d[n]) for n in SMALL], axis=0)
    return jnp.pad(rows, ((0, SMALL_ROWS - rows.shape[0]), (0, 0)))


def _unpack_small(packed, like):
    out, r = {}, 0
    for n in SMALL:
        size = like[n].size
        nr = -(-size // 1024)
        out[n] = packed[r:r + nr].reshape(-1)[:size].reshape(like[n].shape)
        r += nr
    return out


def kernel(x, meta_tokens, ffn1_norm, ffn1_w_gu, ffn1_w_down, mix_norm, w_in, ssd_conv_w, ssd_conv_b, ssd_dt_bias, ssd_a_log, ssd_d, ssd_norm, hg_lower_bound, hg_norm, w_branch_a, w_branch_b, w_out, ffn2_norm, ffn2_w_gu, ffn2_w_down, final_norm, loss_target, m_meta_tokens, m_ffn1_norm, m_ffn1_w_gu, m_ffn1_w_down, m_mix_norm, m_w_in, m_ssd_conv_w, m_ssd_conv_b, m_ssd_dt_bias, m_ssd_a_log, m_ssd_d, m_ssd_norm, m_hg_lower_bound, m_hg_norm, m_w_branch_a, m_w_branch_b, m_w_out, m_ffn2_norm, m_ffn2_w_gu, m_ffn2_w_down, m_final_norm, v_meta_tokens, v_ffn1_norm, v_ffn1_w_gu, v_ffn1_w_down, v_mix_norm, v_w_in, v_ssd_conv_w, v_ssd_conv_b, v_ssd_dt_bias, v_ssd_a_log, v_ssd_d, v_ssd_norm, v_hg_lower_bound, v_hg_norm, v_w_branch_a, v_w_branch_b, v_w_out, v_ffn2_norm, v_ffn2_w_gu, v_ffn2_w_down, v_final_norm):
    P = dict(zip(WEIGHTS, (meta_tokens, ffn1_norm, ffn1_w_gu, ffn1_w_down, mix_norm, w_in, ssd_conv_w, ssd_conv_b, ssd_dt_bias, ssd_a_log, ssd_d, ssd_norm, hg_lower_bound, hg_norm, w_branch_a, w_branch_b, w_out, ffn2_norm, ffn2_w_gu, ffn2_w_down, final_norm)))
    M = dict(zip(WEIGHTS, (m_meta_tokens, m_ffn1_norm, m_ffn1_w_gu, m_ffn1_w_down, m_mix_norm, m_w_in, m_ssd_conv_w, m_ssd_conv_b, m_ssd_dt_bias, m_ssd_a_log, m_ssd_d, m_ssd_norm, m_hg_lower_bound, m_hg_norm, m_w_branch_a, m_w_branch_b, m_w_out, m_ffn2_norm, m_ffn2_w_gu, m_ffn2_w_down, m_final_norm)))
    V = dict(zip(WEIGHTS, (v_meta_tokens, v_ffn1_norm, v_ffn1_w_gu, v_ffn1_w_down, v_mix_norm, v_w_in, v_ssd_conv_w, v_ssd_conv_b, v_ssd_dt_bias, v_ssd_a_log, v_ssd_d, v_ssd_norm, v_hg_lower_bound, v_hg_norm, v_w_branch_a, v_w_branch_b, v_w_out, v_ffn2_norm, v_ffn2_w_gu, v_ffn2_w_down, v_final_norm)))
    cx, cy, cc = _place()
    q = 2 * cx + cy

    mine = jnp.concatenate([meta_tokens.reshape(4, 1024), ssd_conv_w.reshape(2, 1024), jnp.zeros((2, 1024), F32)], axis=0)
    every = _exchange8(mine, False, "gather_small")
    meta_full = jnp.concatenate([every[2 * k, 0:4].reshape(N_META, 256) for k in range(N_CHIPS)], axis=1)
    conv_w_full = jnp.concatenate([every[2 * k, 4:6].reshape(SSD_CONV, 512) for k in range(N_CHIPS)], axis=1)

    late = ("ffn2_w_down", "w_branch_a", "w_branch_b", "w_out")
    rows = jnp.concatenate([P[n][0] for n in late], axis=0)
    zero = lambda t, dtype=F32: (t[0:1, 0:1] * 0).astype(dtype)

    def in_slot(s, after=None):
        s = s if after is None else s + zero(after)
        return lax.dynamic_update_slice(lax.empty((N_CHIPS,) + s.shape, BF16), s.astype(BF16)[None], (q, 0, 0))

    gu1, down1 = _gather_seq([in_slot(ffn1_w_gu[0]), in_slot(ffn1_w_down[0])], "gather_ffn1", 1)
    W = {n: P[n] for n in SMALL}
    W["meta_tokens"], W["ssd_conv_w"] = meta_full, conv_w_full
    W["ffn1_w_gu"], W["ffn1_w_down"] = gu1, down1.reshape(-1, D_MODEL)
    flying = {}

    def stage(name, t):
        if name == "ffn1_norm":
            flying["w_in"] = _gather_seq([in_slot(w_in[0], t)], "gather_w_in", 2)
            return {}
        if name == "ffn1_out":
            flying["late"] = _gather_seq([in_slot(ffn2_w_gu[0], t), in_slot(rows, t)], "gather_late", 3)
            (w_in_all,) = flying["w_in"]
            w_in_all = w_in_all + zero(t, BF16)
            return {"w_in": _split_w_in(w_in_all.transpose(1, 0, 2).reshape(D_MODEL, -1))}
        if name == "mixers_out":
            gu2, rows_all = flying["late"]
            out, r = {"ffn2_w_gu": gu2}, 0
            for n in late:
                nr = P[n].shape[1]
                out[n] = (rows_all[:, r:r + nr] + zero(t, BF16)).reshape(N_CHIPS * nr, D_MODEL)
                r += nr
            return out
        if name == "late_grads":
            row_parts = jnp.concatenate([t[n].reshape(N_CHIPS, -1, D_MODEL) for n in late], axis=1)
            flying["grad_late"] = _Reduce([t["ffn2_w_gu"], row_parts], q, cc, "grad_late", 4)
            return {"_after": [t["ffn2_w_gu"]] + [t[n] for n in late]}
        if name == "after_conv_bwd":
            return {"_after": flying["grad_late"].to_owners(after=t)}
        if name == "w_in_grads":
            order = ("z", "xbc", "dt", "qfig", "gates")
            blocks = flying["grad_late"].join(after=[t[k] for k in order])

            def regions(sums):
                z, xbc, dt, qfig, gates = [s[0] for s in sums]
                h = z.shape[0]
                qfig = qfig.reshape(h, HG_HEADS, 4, 128).transpose(0, 2, 1, 3).reshape(h, 4 * D_MODEL)
                cols = jnp.concatenate([z, xbc, dt[:, :SSD_HEADS], qfig, gates], axis=1)
                return [cols.reshape(h, N_CHIPS, -1).transpose(1, 0, 2)]

            flying["grad_w_in"] = _Reduce([t[k][None] for k in order], q, cc, "grad_w_in", 7, regions)
            return {"_after": blocks}
        if name == "ffn1_dw_down":
            return {"_after": flying["grad_w_in"].to_owners(after=t)}
        return {}

    W["_stage"] = stage

    loss8, grad_x, G = _local_step(x, loss_target, W)

    small = jnp.concatenate(
        [G["meta_tokens"]] + [_rows1024(G[n]) for n in SMALL if n != "meta_tokens"] + [_rows1024(loss8[0:1, 0:1])], axis=0)
    small = jnp.pad(small, ((0, 40 - small.shape[0]), (0, 0)))
    small = _exchange8(small, True, "reduce_small")
    Gs = {"meta_tokens": small[0:N_META]}
    r = N_META
    for n in SMALL:
        if n == "meta_tokens":
            continue
        nr = -(-G[n].size // 1024)
        Gs[n] = small[r:r + nr].reshape(-1)[:G[n].size].reshape(G[n].shape)
        r += nr
    loss = small[r, 0]
    Gs["meta_tokens"] = lax.dynamic_slice(Gs["meta_tokens"], (0, 256 * q), (N_META, 256))
    Gs["ssd_conv_w"] = lax.dynamic_slice(Gs["ssd_conv_w"], (0, 512 * q), (SSD_CONV, 512))[None]
    Gs = {n: Gs[n].reshape(P[n].shape) for n in SMALL}

    grad_ffn1 = _Reduce([G["ffn1_w_gu"], G["ffn1_w_down"].reshape(N_CHIPS, -1, D_MODEL)], q, cc, "grad_ffn1", 10)
    flying["grad_w_in"].join(after=grad_x)
    going = grad_ffn1.to_owners(after=grad_x)
    g_gu2, g_rows = flying["grad_late"].out
    (g_w_in,) = flying["grad_w_in"].out
    Gb = {"ffn2_w_gu": g_gu2, "w_in": g_w_in}
    r = 0
    for n in late:
        nr = P[n].shape[1]
        Gb[n] = g_rows[r:r + nr]
        r += nr

    grads, delta, new_m, new_v = dict(Gs), {}, {}, {}
    d_s, m_s, v_s = _adamw(_pack_small(P), _pack_small(Gs), _pack_small(M), _pack_small(V), "adamw_small", after=going)
    delta.update(_unpack_small(d_s, P))
    new_m.update(_unpack_small(m_s, P))
    new_v.update(_unpack_small(v_s, P))
    done = [d_s]
    cols = w_in.shape[2]
    to_tiles = lambda a: a.transpose(2, 0, 1).reshape(cols, 8, 128).reshape(cols * 8, 128)
    from_tiles = lambda a: a.reshape(cols, 1, D_MODEL).transpose(1, 2, 0)
    for n in [n for n in BIG if n in Gb]:
        if n == "w_in":
            g_t = to_tiles(Gb[n][None])
            d_, m_, v_ = _adamw(to_tiles(P[n]), g_t, to_tiles(M[n]), to_tiles(V[n]), f"adamw_{n}", after=going)
            grads[n], delta[n], new_m[n], new_v[n] = from_tiles(g_t), from_tiles(d_), from_tiles(m_), from_tiles(v_)
        else:
            d_, m_, v_ = _adamw(P[n][0], Gb[n], M[n][0], V[n][0], f"adamw_{n}", after=going)
            grads[n], delta[n], new_m[n], new_v[n] = Gb[n][None], d_[None], m_[None], v_[None]
        done.append(d_)
    grad_ffn1.join(after=done)
    Gb["ffn1_w_gu"], Gb["ffn1_w_down"] = grad_ffn1.out
    for n in ("ffn1_w_gu", "ffn1_w_down"):
        d_, m_, v_ = _adamw(P[n][0], Gb[n], M[n][0], V[n][0], f"adamw_{n}")
        grads[n], delta[n], new_m[n], new_v[n] = Gb[n][None], d_[None], m_[None], v_[None]
    return (loss, grad_x, *[grads[n] for n in WEIGHTS], *[delta[n] for n in WEIGHTS],
            *[new_m[n] for n in WEIGHTS], *[new_v[n] for n in WEIGHTS])
```

```python
import functools

import jax
import jax.numpy as jnp
from jax import lax
from jax.experimental import pallas as pl
from jax.experimental.pallas import tpu as pltpu
from jax.experimental.pallas import tpu_sc as plsc

F32 = jnp.float32
BF16 = jnp.bfloat16
HIGHEST = lax.Precision.HIGHEST
MESH = pl.DeviceIdType.MESH

D_MODEL = 1024
N_META = 16
EPS = 1e-6
SSD_HEADS = 16
SSD_HEAD_DIM = 64
SSD_INNER = 1024
SSD_GROUPS = 4
SSD_STATE = 128
SSD_CONV = 4
SSD_CONV_CH = 2048
HG_HEADS = 8
HG_SUB = 32
CHUNK = 128
D_FF = 2816
N_CHIPS = 4
IN_SIZES = (1024, 2048, 16, 1024, 1024, 1024, 1024, 1024, 1024)
ADAM_LR = 0.001
ADAM_B1 = 0.9
ADAM_B2 = 0.999
ADAM_EPS = 1e-08
ADAM_WD = 0.01
ADAM_STEP = 10
VMEM_LIMIT = 56 * 1024 * 1024
MATMUL_BLOCK_BYTES = 42 * 1024 * 1024
ADAMW_BLOCK_BYTES = 5 * 512 * 1024


def _cparams(sem=None):
    return pltpu.CompilerParams(dimension_semantics=sem, vmem_limit_bytes=VMEM_LIMIT)


def _pick(n, cands):
    for c in cands:
        if n % c == 0:
            return c
    return n


def _deps(after):
    xs = after if isinstance(after, (list, tuple)) else [after]
    one = lambda x: lax.slice(x, (0,) * x.ndim, (1,) * x.ndim).reshape(1).astype(F32)
    return jnp.concatenate([one(x) for x in xs]).reshape(1, -1)


def _dep_spec(dep):
    return pl.BlockSpec(dep.shape, lambda *_: (0, 0))


def _skip_ref(body, pos):
    return lambda *refs: body(*refs[:pos], *refs[pos + 1:])


def _dg(a, b, ca, cb):
    return lax.dot_general(a.astype(BF16), b.astype(BF16), (((ca,), (cb,)), ((), ())), preferred_element_type=F32)


@jax.custom_vjp
def _mm(a, b):
    return _dg(a, b, 1, 0)


def _mm_fwd(a, b):
    return _dg(a, b, 1, 0), (a, b)


def _mm_bwd(r, g):
    a, b = r
    return _dg(g, b, 1, 1), _dg(a, g, 0, 0)


_mm.defvjp(_mm_fwd, _mm_bwd)


@jax.custom_vjp
def _mm_nt(a, b):
    return _dg(a, b, 1, 1)


def _mm_nt_fwd(a, b):
    return _dg(a, b, 1, 1), (a, b)


def _mm_nt_bwd(r, g):
    a, b = r
    return _dg(g, b, 1, 0), _dg(g, a, 0, 0)


_mm_nt.defvjp(_mm_nt_fwd, _mm_nt_bwd)


@jax.custom_vjp
def _mm_tn(a, b):
    return _dg(a, b, 0, 0)


def _mm_tn_fwd(a, b):
    return _dg(a, b, 0, 0), (a, b)


def _mm_tn_bwd(r, g):
    a, b = r
    return _dg(b, g, 1, 1), _dg(a, g, 1, 0)


_mm_tn.defvjp(_mm_tn_fwd, _mm_tn_bwd)


def _tri_sum(x, lower):
    n = x.shape[0]
    ri = lax.broadcasted_iota(jnp.int32, (n, n), 0)
    ci = lax.broadcasted_iota(jnp.int32, (n, n), 1)
    tri = ((ri >= ci) if lower else (ri <= ci)).astype(BF16)
    x1 = x.astype(BF16)
    r1 = x - x1.astype(F32)
    x2 = r1.astype(BF16)
    x3 = (r1 - x2.astype(F32)).astype(BF16)
    dot = lambda p: lax.dot_general(tri, p, (((1,), (0,)), ((), ())), preferred_element_type=F32)
    return (dot(x3) + dot(x2)) + dot(x1)


@jax.custom_vjp
def _cumsum_rows(x):
    return _tri_sum(x, True)


_cumsum_rows.defvjp(lambda x: (_tri_sum(x, True), None), lambda _, g: (_tri_sum(g, False),))


def _silu(x):
    return x * jax.nn.sigmoid(x)


def _softplus(x):
    return jnp.maximum(x, 0.0) + jnp.log(1.0 + jnp.exp(-jnp.abs(x)))


def _tril(n):
    ri = lax.broadcasted_iota(jnp.int32, (n, n), 0)
    ci = lax.broadcasted_iota(jnp.int32, (n, n), 1)
    return ri >= ci


def _row_of(m, r):
    sub = lax.broadcasted_iota(jnp.int32, (m.shape[0], 1), 0)
    return jnp.sum(jnp.where(sub == r, m, 0.0), axis=0, keepdims=True)


def _col_of(m, c):
    lane = lax.broadcasted_iota(jnp.int32, (1, m.shape[1]), 1)
    return jnp.sum(jnp.where(lane == c, m, 0.0), axis=1, keepdims=True)


def _matmul(a, b, *, mode, out_dtype, name, alpha=1.0, res=None, tm=None, tn=None, out_groups=None, after=None):
    b3 = b.ndim == 3
    if mode == "nn":
        M, K = a.shape
        G = b.shape[0] if b3 else 1
        Ng = b.shape[-1]
        N = G * Ng
    elif mode == "nt":
        M, K = a.shape
        G = b.shape[0] if b3 else 1
        N = b.shape[-2]
        Kg = b.shape[-1]
        assert G * Kg == K
    else:
        K, M = a.shape
        N = b.shape[1]
        G = out_groups or 1
        Ng = N // G
    has_res = res is not None
    split_n = (mode == "nn" and b3) or (mode == "tn" and G > 1)
    per_mn = jnp.dtype(out_dtype).itemsize + (res.dtype.itemsize if has_res else 0)
    fits = [(m_ * n_, m_, n_)
            for m_ in (4352, 2176, 1408, 1088, 1024, 544, 512, 256, 128) if M % m_ == 0
            for n_ in (2816, 2048, 1408, 1024, 512, 256, 128) if (Ng if split_n else N) % n_ == 0
            if 2 * (K * m_ * a.dtype.itemsize + K * n_ * b.dtype.itemsize + m_ * n_ * per_mn) + 4 * m_ * n_ <= MATMUL_BLOCK_BYTES]
    _, tm_fit, tn_fit = max(fits)
    tm, tn = tm or tm_fit, tn or tn_fit
    nm, nn_ = M // tm, N // tn
    assert nm * tm == M and nn_ * tn == N, (name, M, N, K, tm, tn)

    if mode == "nn":
        a_spec = pl.BlockSpec((tm, K), lambda i, j: (i, 0))
        if b3:
            ns = Ng // tn
            b_spec = pl.BlockSpec((None, K, tn), lambda i, j: (j // ns, 0, j % ns))
        else:
            b_spec = pl.BlockSpec((K, tn), lambda i, j: (0, j))
        ca, cb = 1, 0
    elif mode == "nt":
        a_spec = pl.BlockSpec((tm, K), lambda i, j: (i, 0))
        if b3:
            b_spec = pl.BlockSpec((G, tn, Kg), lambda i, j: (0, j, 0))
        else:
            b_spec = pl.BlockSpec((tn, K), lambda i, j: (j, 0))
        ca, cb = 1, 1
    else:
        a_spec = pl.BlockSpec((K, tm), lambda i, j: (0, i))
        b_spec = pl.BlockSpec((K, tn), lambda i, j: (0, j))
        ca, cb = 0, 0
    if mode == "tn" and G > 1:
        ns = Ng // tn
        o_spec = pl.BlockSpec((None, tm, tn), lambda i, j: (j // ns, i, j % ns))
        out_shape = jax.ShapeDtypeStruct((G, M, Ng), out_dtype)
    else:
        o_spec = pl.BlockSpec((tm, tn), lambda i, j: (i, j))
        out_shape = jax.ShapeDtypeStruct((M, N), out_dtype)
    in_specs = [a_spec, b_spec]
    args = [a, b]
    if has_res:
        in_specs.append(pl.BlockSpec((tm, tn), lambda i, j: (i, j)))
        args.append(res)
    if after is not None:
        args.append(_deps(after))
        in_specs.append(_dep_spec(args[-1]))

    def body(*refs):
        a_ref, b_ref, o_ref = refs[0], refs[1], refs[-1]
        if mode == "nt" and b3:
            o = _dg(a_ref[:, 0:Kg], b_ref[0], ca, cb)
            for g in range(1, G):
                o = o + _dg(a_ref[:, g * Kg:(g + 1) * Kg], b_ref[g], ca, cb)
        else:
            o = _dg(a_ref[...], b_ref[...], ca, cb)
        if alpha != 1.0:
            o = o * alpha
        if has_res:
            o = o + refs[2][...]
        o_ref[...] = o.astype(o_ref.dtype)

    return pl.pallas_call(
        body, grid=(nm, nn_), in_specs=in_specs, out_specs=o_spec, out_shape=out_shape, name=name,
        compiler_params=_cparams(("parallel", "parallel")),
    )(*args)


def _rms_fn(h, w):
    r = lax.rsqrt(jnp.mean(h * h, axis=-1, keepdims=True) + EPS)
    return h * r * w


def _swiglu_fn(gu):
    g = gu[:, :D_FF].astype(F32)
    u = gu[:, D_FF:].astype(F32)
    return _silu(g) * u


def _merge_fn(pa, pb, gates):
    return jax.nn.sigmoid(gates[:, :D_MODEL]) * pa + jax.nn.sigmoid(gates[:, D_MODEL:]) * pb


def _rows_call(body, *, rows, tr, ins, outs, accs=(), name, after=None):
    n = rows // tr
    assert n * tr == rows
    if after is not None:
        body = _skip_ref(body, len(ins))
        ins = list(ins) + [("full", _deps(after))]

    def spec(x):
        if isinstance(x, tuple):
            shp = x[1].shape
            return pl.BlockSpec(shp, lambda i: (0,) * len(shp))
        return pl.BlockSpec((tr, x.shape[1]), lambda i: (i, 0))

    in_specs = [spec(x) for x in ins]
    args = [x[1] if isinstance(x, tuple) else x for x in ins]
    out_specs = [spec(x) for x in outs] + [pl.BlockSpec(x.shape, lambda i: (0,) * len(x.shape)) for x in accs]
    out_shape = [x[1] if isinstance(x, tuple) else x for x in outs] + list(accs)
    return pl.pallas_call(
        body, grid=(n,), in_specs=in_specs, out_specs=out_specs, out_shape=out_shape, name=name,
        compiler_params=_cparams(("arbitrary",)),
    )(*args)


def _acc_rows(ref, val):
    @pl.when(pl.program_id(0) == 0)
    def _():
        ref[...] = jnp.zeros_like(ref)

    ref[0:1, :] += val


def _rms_fwd(h, w, name):
    def body(h_ref, w_ref, o_ref):
        o_ref[...] = _rms_fn(h_ref[...], w_ref[...]).astype(o_ref.dtype)

    R = h.shape[0]
    return _rows_call(body, rows=R, tr=_pick(R, (256, 128)), ins=[h, ("full", w)],
                      outs=[jax.ShapeDtypeStruct(h.shape, BF16)], name=name)[0]


def _rms_bwd(h, w, dn, dres, name, after=None):
    def body(h_ref, w_ref, dn_ref, dres_ref, dh_ref, dw_ref):
        _, vjp = jax.vjp(_rms_fn, h_ref[...], w_ref[...])
        dh, dw = vjp(dn_ref[...].astype(F32))
        dh_ref[...] = dh + dres_ref[...]
        _acc_rows(dw_ref, dw)

    R = h.shape[0]
    return _rows_call(body, rows=R, tr=_pick(R, (256, 128)), ins=[h, ("full", w), dn, dres],
                      outs=[jax.ShapeDtypeStruct(h.shape, F32)], accs=[jax.ShapeDtypeStruct((8, D_MODEL), F32)], name=name,
                      after=after)


def _gu_swiglu(n, w_gu, name):
    R = n.shape[0]
    G, _, ng = w_gu.shape

    def body(n_ref, w_ref, gu_ref, a_ref):
        x = n_ref[...]
        for r in range(G):
            gu_ref[:, ng * r:ng * (r + 1)] = _dg(x, w_ref[r], 1, 0).astype(gu_ref.dtype)
        a_ref[...] = _swiglu_fn(gu_ref[...]).astype(a_ref.dtype)

    return _rows_call(body, rows=R, tr=_pick(R, (256, 128)), ins=[n, ("full", w_gu)],
                      outs=[jax.ShapeDtypeStruct((R, 2 * D_FF), BF16), jax.ShapeDtypeStruct((R, D_FF), BF16)], name=name)


def _d_swiglu(dout, w_down, gu, alpha, name):
    R = gu.shape[0]

    def body(do_ref, w_ref, gu_ref, o_ref):
        da = _dg(do_ref[...], w_ref[...], 1, 1) * alpha
        _, vjp = jax.vjp(_swiglu_fn, gu_ref[...].astype(F32))
        (dgu,) = vjp(da)
        o_ref[...] = dgu.astype(o_ref.dtype)

    return _rows_call(body, rows=R, tr=_pick(R, (256, 128)), ins=[dout, ("full", w_down), gu],
                      outs=[jax.ShapeDtypeStruct(gu.shape, BF16)], name=name)[0]


def _merge_fwd(pa, pb, gates, name):
    def body(pa_ref, pb_ref, g_ref, o_ref):
        o_ref[...] = _merge_fn(pa_ref[...], pb_ref[...], g_ref[...].astype(F32)).astype(o_ref.dtype)

    R = pa.shape[0]
    return _rows_call(body, rows=R, tr=_pick(R, (256, 128)), ins=[pa, pb, gates],
                      outs=[jax.ShapeDtypeStruct(pa.shape, BF16)], name=name)[0]


def _merge_bwd(pa, pb, gates, dm, name):
    def body(pa_ref, pb_ref, g_ref, dm_ref, dpa_ref, dpb_ref, dg_ref):
        _, vjp = jax.vjp(_merge_fn, pa_ref[...], pb_ref[...], g_ref[...].astype(F32))
        dpa, dpb, dg = vjp(dm_ref[...].astype(F32))
        dpa_ref[...] = dpa.astype(dpa_ref.dtype)
        dpb_ref[...] = dpb.astype(dpb_ref.dtype)
        dg_ref[...] = dg.astype(dg_ref.dtype)

    R = pa.shape[0]
    return _rows_call(body, rows=R, tr=_pick(R, (256, 128)), ins=[pa, pb, gates, dm],
                      outs=[jax.ShapeDtypeStruct(pa.shape, BF16), jax.ShapeDtypeStruct(pa.shape, BF16),
                            jax.ShapeDtypeStruct(gates.shape, BF16)], name=name)


def _loss_head(h3, w, target, nseq, name):
    Tp = h3.shape[0] // nseq
    nc = Tp // CHUNK

    def fn(h, w_, t, valid):
        y = _rms_fn(h, w_)
        e = (y - t) * valid
        return 0.5 * jnp.sum(jnp.mean(e * e, axis=-1, keepdims=True))

    def body(h_ref, w_ref, t_ref, loss_ref, dh_ref, dw_ref):
        b, c = pl.program_id(0), pl.program_id(1)
        valid = (c >= 1).astype(F32)
        t = t_ref[...]
        loss, vjp = jax.vjp(lambda h, w_: fn(h, w_, t, valid), h_ref[...], w_ref[...])
        dh, dw = vjp(jnp.ones((), F32))
        dh_ref[...] = dh

        @pl.when((b == 0) & (c == 0))
        def _():
            loss_ref[...] = jnp.zeros_like(loss_ref)
            dw_ref[...] = jnp.zeros_like(dw_ref)

        loss_ref[...] += jnp.full(loss_ref.shape, loss, F32)
        dw_ref[0:1, :] += dw

    return pl.pallas_call(
        body, grid=(nseq, nc),
        in_specs=[pl.BlockSpec((CHUNK, D_MODEL), lambda b, c: (b * nc + c, 0)),
                  pl.BlockSpec((1, D_MODEL), lambda b, c: (0, 0)),
                  pl.BlockSpec((None, CHUNK, D_MODEL), lambda b, c: (b, jnp.maximum(c - 1, 0), 0))],
        out_specs=[pl.BlockSpec((8, 128), lambda b, c: (0, 0)),
                   pl.BlockSpec((CHUNK, D_MODEL), lambda b, c: (b * nc + c, 0)),
                   pl.BlockSpec((8, D_MODEL), lambda b, c: (0, 0))],
        out_shape=[jax.ShapeDtypeStruct((8, 128), F32), jax.ShapeDtypeStruct(h3.shape, F32),
                   jax.ShapeDtypeStruct((8, D_MODEL), F32)],
        name=name, compiler_params=_cparams(("arbitrary", "arbitrary")),
    )(h3, w, target)


CONV_TILE = 512
CONV_HALO = 8


def _conv_fwd(xbc, w, b, pad, name):
    B, Tp, C = xbc.shape
    nch = Tp // CHUNK

    def body(x_ref, w_ref, b_ref, o_ref, xp):
        xp[0:CONV_HALO, :] = jnp.zeros((CONV_HALO, CONV_TILE), F32)
        xp[CONV_HALO:, :] = x_ref[...]
        for c in range(nch):
            acc = jnp.zeros((CHUNK, CONV_TILE), F32) + b_ref[...]
            for k in range(SSD_CONV):
                acc = acc + w_ref[k:k + 1, :] * xp[pl.ds(CONV_HALO + CHUNK * c - (SSD_CONV - 1) + k, CHUNK), :]
            row = CHUNK * c + lax.broadcasted_iota(jnp.int32, (CHUNK, 1), 0)
            o_ref[pl.ds(CHUNK * c, CHUNK), :] = jnp.where(row >= pad, _silu(acc), 0.0)

    return pl.pallas_call(
        body, grid=(B, C // CONV_TILE),
        in_specs=[pl.BlockSpec((None, Tp, CONV_TILE), lambda i, j: (i, 0, j)),
                  pl.BlockSpec((SSD_CONV, CONV_TILE), lambda i, j: (0, j)),
                  pl.BlockSpec((1, CONV_TILE), lambda i, j: (0, j))],
        out_specs=pl.BlockSpec((None, Tp, CONV_TILE), lambda i, j: (i, 0, j)),
        out_shape=jax.ShapeDtypeStruct(xbc.shape, F32),
        scratch_shapes=[pltpu.VMEM((Tp + CONV_HALO, CONV_TILE), F32)],
        name=name, compiler_params=_cparams(("arbitrary", "arbitrary")),
    )(xbc, w, b)


def _conv_bwd(xbc, w, b, dact, pad, name):
    B, Tp, C = xbc.shape
    nch = Tp // CHUNK

    def body(x_ref, w_ref, b_ref, da_ref, dx_ref, dw_ref, db_ref, xp, dp):
        bi = pl.program_id(1)
        xp[0:CONV_HALO, :] = jnp.zeros((CONV_HALO, CONV_TILE), F32)
        xp[CONV_HALO:, :] = x_ref[...]
        dp[pl.ds(Tp, CONV_HALO), :] = jnp.zeros((CONV_HALO, CONV_TILE), F32)
        dws = [jnp.zeros((1, CONV_TILE), F32) for _ in range(SSD_CONV)]
        dbs = jnp.zeros((1, CONV_TILE), F32)
        for c in range(nch):
            xs = [xp[pl.ds(CONV_HALO + CHUNK * c - (SSD_CONV - 1) + k, CHUNK), :] for k in range(SSD_CONV)]
            acc = jnp.zeros((CHUNK, CONV_TILE), F32) + b_ref[...]
            for k in range(SSD_CONV):
                acc = acc + w_ref[k:k + 1, :] * xs[k]
            row = CHUNK * c + lax.broadcasted_iota(jnp.int32, (CHUNK, 1), 0)
            sg = jax.nn.sigmoid(acc)
            dpre = jnp.where(row >= pad, da_ref[pl.ds(CHUNK * c, CHUNK), :] * (sg * (1.0 + acc * (1.0 - sg))), 0.0)
            dp[pl.ds(CHUNK * c, CHUNK), :] = dpre
            dbs = dbs + jnp.sum(dpre, axis=0, keepdims=True)
            for k in range(SSD_CONV):
                dws[k] = dws[k] + jnp.sum(dpre * xs[k], axis=0, keepdims=True)
        for c in range(nch):
            acc = jnp.zeros((CHUNK, CONV_TILE), F32)
            for k in range(SSD_CONV):
                acc = acc + w_ref[k:k + 1, :] * dp[pl.ds(CHUNK * c + (SSD_CONV - 1) - k, CHUNK), :]
            dx_ref[pl.ds(CHUNK * c, CHUNK), :] = acc.astype(dx_ref.dtype)

        @pl.when(bi == 0)
        def _():
            dw_ref[...] = jnp.zeros_like(dw_ref)
            db_ref[...] = jnp.zeros_like(db_ref)

        for k in range(SSD_CONV):
            dw_ref[k:k + 1, :] += dws[k]
        db_ref[0:1, :] += dbs

    return pl.pallas_call(
        body, grid=(C // CONV_TILE, B),
        in_specs=[pl.BlockSpec((None, Tp, CONV_TILE), lambda j, i: (i, 0, j)),
                  pl.BlockSpec((SSD_CONV, CONV_TILE), lambda j, i: (0, j)),
                  pl.BlockSpec((1, CONV_TILE), lambda j, i: (0, j)),
                  pl.BlockSpec((None, Tp, CONV_TILE), lambda j, i: (i, 0, j))],
        out_specs=[pl.BlockSpec((None, Tp, CONV_TILE), lambda j, i: (i, 0, j)),
                   pl.BlockSpec((8, CONV_TILE), lambda j, i: (0, j)),
                   pl.BlockSpec((8, CONV_TILE), lambda j, i: (0, j))],
        out_shape=[jax.ShapeDtypeStruct(xbc.shape, BF16), jax.ShapeDtypeStruct((8, C), F32),
                   jax.ShapeDtypeStruct((8, C), F32)],
        scratch_shapes=[pltpu.VMEM((Tp + CONV_HALO, CONV_TILE), F32), pltpu.VMEM((Tp + CONV_HALO, CONV_TILE), F32)],
        name=name, compiler_params=_cparams(("arbitrary", "arbitrary")),
    )(xbc, w, b, dact)


def _ssd_chunk(xs, bm, cm, dtr, z, state, dt_bias, a_log, dskip, norm_w, valid):
    Q = xs.shape[0]
    lane = lax.broadcasted_iota(jnp.int32, (1, 128), 1)
    dt = jnp.where(lane < SSD_HEADS, _softplus(dtr + dt_bias), 0.0) * valid
    a = dt * (-jnp.exp(a_log))
    tril = _tril(Q)
    cs = _cumsum_rows(a)
    cs_t = cs.T
    cs_end = _row_of(cs, Q - 1)
    low = lane < SSD_HEAD_DIM
    low_rows = lax.broadcasted_iota(jnp.int32, (128, 1), 0) < SSD_HEAD_DIM
    ys, new_state = [], []
    for g in range(SSD_GROUPS):
        bg = bm[:, 128 * g:128 * (g + 1)]
        cg = cm[:, 128 * g:128 * (g + 1)]
        cb = _mm_nt(cg, bg)
        for pr in range(2):
            p = 2 * g + pr
            h0, h1 = 2 * p, 2 * p + 1
            xp = xs[:, 128 * p:128 * (p + 1)]
            c0, c1 = _col_of(cs, h0), _col_of(cs, h1)
            e0, e1 = _col_of(cs_end, h0), _col_of(cs_end, h1)
            xd = xp * jnp.where(low, _col_of(dt, h0), _col_of(dt, h1))
            l0 = jnp.exp(jnp.where(tril, c0 - _row_of(cs_t, h0), -1e30))
            l1 = jnp.exp(jnp.where(tril, c1 - _row_of(cs_t, h1), -1e30))
            y_diag = jnp.where(low, _mm(cb * l0, xd), _mm(cb * l1, xd))
            to_end = jnp.where(low, jnp.exp(e0 - c0), jnp.exp(e1 - c1))
            sp = state[128 * p:128 * (p + 1), :]
            y_off = _mm_nt(cg, sp) * jnp.where(low, jnp.exp(c0), jnp.exp(c1))
            new_state.append(sp * jnp.where(low_rows, jnp.exp(e0), jnp.exp(e1)) + _mm_tn(xd * to_end, bg))
            ys.append(y_diag + y_off + xp * jnp.where(low, _col_of(dskip, h0), _col_of(dskip, h1)))
    y = jnp.concatenate(ys, axis=1) * _silu(z)
    gw = SSD_INNER // SSD_GROUPS
    outs = []
    for g in range(SSD_GROUPS):
        blk = y[:, gw * g:gw * (g + 1)]
        outs.append(blk * lax.rsqrt(jnp.mean(blk * blk, axis=-1, keepdims=True) + EPS))
    return jnp.concatenate(outs, axis=1) * norm_w, jnp.concatenate(new_state, axis=0)


def _valid_rows(c, pad):
    row = c * CHUNK + lax.broadcasted_iota(jnp.int32, (CHUNK, 1), 0)
    return (row >= pad).astype(F32)


def _ssd_fwd(xact, dtr, z, dt_bias, a_log, dskip, norm_w, pad, name):
    B, Tp, _ = xact.shape
    nc = Tp // CHUNK

    def body(xs_ref, bm_ref, cm_ref, dt_ref, z_ref, db_ref, al_ref, ds_ref, nw_ref, y_ref, save_ref, st):
        c = pl.program_id(1)

        @pl.when(c == 0)
        def _():
            st[...] = jnp.zeros_like(st)

        s0 = st[...]
        save_ref[...] = s0
        y, s1 = _ssd_chunk(xs_ref[...], bm_ref[...], cm_ref[...], dt_ref[...], z_ref[...].astype(F32), s0, db_ref[...],
                           al_ref[...], ds_ref[...], nw_ref[...], _valid_rows(c, pad))
        y_ref[...] = y.astype(y_ref.dtype)
        st[...] = s1

    row = lambda w, off=0: pl.BlockSpec((None, CHUNK, w), lambda b, c: (b, c, off))
    par = lambda w: pl.BlockSpec((1, w), lambda b, c: (0, 0))
    return pl.pallas_call(
        body, grid=(B, nc),
        in_specs=[row(1024, 0), row(512, 2), row(512, 3), row(128), row(1024), par(128), par(128), par(128), par(1024)],
        out_specs=[row(1024), pl.BlockSpec((None, None, 1024, 128), lambda b, c: (b, c, 0, 0))],
        out_shape=[jax.ShapeDtypeStruct((B, Tp, SSD_INNER), BF16), jax.ShapeDtypeStruct((B, nc, 1024, 128), F32)],
        scratch_shapes=[pltpu.VMEM((1024, 128), F32)],
        name=name, compiler_params=_cparams(("arbitrary", "arbitrary")),
    )(xact, xact, xact, dtr, z, dt_bias, a_log, dskip, norm_w)


def _ssd_bwd(xact, dtr, z, dt_bias, a_log, dskip, norm_w, saved, dy, pad, name, after=None):
    B, Tp, _ = xact.shape
    nc = Tp // CHUNK

    def body(xs_ref, bm_ref, cm_ref, dt_ref, z_ref, db_ref, al_ref, ds_ref, nw_ref, sv_ref, dy_ref,
             dx_ref, ddt_ref, dz_ref, dpar_ref, dnw_ref, dst):
        b, i = pl.program_id(0), pl.program_id(1)
        c = nc - 1 - i

        @pl.when(i == 0)
        def _():
            dst[...] = jnp.zeros_like(dst)

        valid = _valid_rows(c, pad)
        fn = lambda *a: _ssd_chunk(*a, valid)
        _, vjp = jax.vjp(fn, xs_ref[...], bm_ref[...], cm_ref[...], dt_ref[...], z_ref[...].astype(F32), sv_ref[...],
                         db_ref[...], al_ref[...], ds_ref[...], nw_ref[...])
        dxs, dbm, dcm, ddt, dz, dstate, ddb, dal, dds, dnw = vjp((dy_ref[...].astype(F32), dst[...]))
        dx_ref[:, 0:1024] = dxs
        dx_ref[:, 1024:1536] = dbm
        dx_ref[:, 1536:2048] = dcm
        ddt_ref[...] = ddt
        dz_ref[...] = dz.astype(dz_ref.dtype)
        dst[...] = dstate

        @pl.when((b == 0) & (i == 0))
        def _():
            dpar_ref[...] = jnp.zeros_like(dpar_ref)
            dnw_ref[...] = jnp.zeros_like(dnw_ref)

        dpar_ref[0:1, :] += ddb
        dpar_ref[1:2, :] += dal
        dpar_ref[2:3, :] += dds
        dnw_ref[0:1, :] += dnw

    row = lambda w, off=0: pl.BlockSpec((None, CHUNK, w), lambda b, i: (b, nc - 1 - i, off))
    par = lambda w: pl.BlockSpec((1, w), lambda b, i: (0, 0))
    acc = lambda w: pl.BlockSpec((8, w), lambda b, i: (0, 0))
    in_specs = [row(1024, 0), row(512, 2), row(512, 3), row(128), row(1024), par(128), par(128), par(128), par(1024),
                pl.BlockSpec((None, None, 1024, 128), lambda b, i: (b, nc - 1 - i, 0, 0)), row(1024)]
    args = [xact, xact, xact, dtr, z, dt_bias, a_log, dskip, norm_w, saved, dy]
    if after is not None:
        body = _skip_ref(body, len(args))
        args.append(_deps(after))
        in_specs.append(_dep_spec(args[-1]))
    outs = pl.pallas_call(
        body, grid=(B, nc), in_specs=in_specs,
        out_specs=[row(2048), row(128), row(1024), acc(128), acc(1024)],
        out_shape=[jax.ShapeDtypeStruct((B, Tp, 2048), F32), jax.ShapeDtypeStruct((B, Tp, 128), F32),
                   jax.ShapeDtypeStruct((B, Tp, 1024), BF16), jax.ShapeDtypeStruct((8, 128), F32),
                   jax.ShapeDtypeStruct((8, 1024), F32)],
        scratch_shapes=[pltpu.VMEM((1024, 128), F32)],
        name=name, compiler_params=_cparams(("arbitrary", "arbitrary")),
    )(*args)
    return outs


def _hg_chunk(qr, fr, ir, gr, state_t, p0, p1, norm_w, valid):
    Q = qr.shape[0]
    lb = jax.nn.sigmoid(p0 - p1)
    f = lb + (1.0 - lb) * jax.nn.sigmoid(fr)
    k = 1.0 - f
    q = _silu(qr)
    v = ir * valid
    cum = _cumsum_rows(jnp.log(f))
    cum_end = _row_of(cum, Q - 1)
    o_inter = _mm_nt(q * jnp.exp(cum), state_t)
    nblk = Q // HG_SUB
    row = lax.broadcasted_iota(jnp.int32, (Q, 1), 0)
    ri = lax.broadcasted_iota(jnp.int32, (Q, Q), 0)
    ci = lax.broadcasted_iota(jnp.int32, (Q, Q), 1)
    mids = jnp.concatenate([jnp.broadcast_to(_row_of(cum, HG_SUB * i + HG_SUB // 2 - 1), (HG_SUB, cum.shape[1]))
                            for i in range(nblk)], axis=0)
    sh = HG_SUB.bit_length() - 1
    same = (jnp.right_shift(ri, sh) == jnp.right_shift(ci, sh)) & (ri >= ci)
    att = jnp.where(same, _mm_nt(q * jnp.exp(cum - mids), k * jnp.exp(mids - cum)), 0.0)
    for i in range(1, nblk):
        lo = HG_SUB * i
        start = _row_of(cum, lo - 1)
        qa = q * jnp.exp(jnp.where((row >= lo) & (row < lo + HG_SUB), cum - start, -1e30))
        ka = k * jnp.exp(jnp.where(row < lo, start - cum, -1e30))
        att = att + _mm_nt(qa, ka)
    o = o_inter + _mm(att, v)
    new_state_t = state_t * jnp.exp(cum_end) + _mm_tn(v, k * jnp.exp(cum_end - cum))
    o = o * lax.rsqrt(jnp.mean(o * o, axis=-1, keepdims=True) + EPS) * norm_w
    return o * _silu(gr), new_state_t


HG_PER_STEP = 4
HG_COLS = 4 * 128


def _hg_fwd(qfig, lbh, nwh, pad, name):
    B, Tp, _ = qfig.shape
    nc = Tp // CHUNK
    hp = HG_PER_STEP

    def body(x_ref, lb_ref, nw_ref, y_ref, save_ref, st):
        c = pl.program_id(1)

        @pl.when(c == 0)
        def _():
            st[...] = jnp.zeros_like(st)

        valid = _valid_rows(c, pad)
        for j in range(hp):
            for b in range(B):
                s0 = st[j, b]
                save_ref[j, b] = s0
                col = lambda k: x_ref[b, :, HG_COLS * j + 128 * k:HG_COLS * j + 128 * (k + 1)]
                y, s1 = _hg_chunk(col(0), col(1), col(2), col(3), s0, lb_ref[j, 0:1, :], lb_ref[j, 1:2, :], nw_ref[j], valid)
                y_ref[b, :, 128 * j:128 * (j + 1)] = y.astype(y_ref.dtype)
                st[j, b] = s1

    return pl.pallas_call(
        body, grid=(HG_HEADS // hp, nc),
        in_specs=[pl.BlockSpec((B, CHUNK, HG_COLS * hp), lambda h, c: (0, c, h)),
                  pl.BlockSpec((hp, 2, 128), lambda h, c: (h, 0, 0)),
                  pl.BlockSpec((hp, 1, 128), lambda h, c: (h, 0, 0))],
        out_specs=[pl.BlockSpec((B, CHUNK, 128 * hp), lambda h, c: (0, c, h)),
                   pl.BlockSpec((hp, B, None, 128, 128), lambda h, c: (h, 0, c, 0, 0))],
        out_shape=[jax.ShapeDtypeStruct((B, Tp, 1024), BF16), jax.ShapeDtypeStruct((HG_HEADS, B, nc, 128, 128), F32)],
        scratch_shapes=[pltpu.VMEM((hp, B, 128, 128), F32)],
        name=name, compiler_params=_cparams(("arbitrary", "arbitrary")),
    )(qfig, lbh, nwh)


def _hg_bwd(qfig, lbh, nwh, saved, dy, pad, name, after=None):
    B, Tp, _ = qfig.shape
    nc = Tp // CHUNK
    hp = HG_PER_STEP

    def body(x_ref, lb_ref, nw_ref, sv_ref, dy_ref, dx_ref, dlb_ref, dnw_ref, dst):
        i = pl.program_id(1)
        c = nc - 1 - i

        @pl.when(i == 0)
        def _():
            dst[...] = jnp.zeros_like(dst)
            dlb_ref[...] = jnp.zeros_like(dlb_ref)
            dnw_ref[...] = jnp.zeros_like(dnw_ref)

        valid = _valid_rows(c, pad)
        fn = lambda *a: _hg_chunk(*a, valid)
        for j in range(hp):
            for b in range(B):
                col = lambda k: x_ref[b, :, HG_COLS * j + 128 * k:HG_COLS * j + 128 * (k + 1)]
                _, vjp = jax.vjp(fn, col(0), col(1), col(2), col(3), sv_ref[j, b], lb_ref[j, 0:1, :], lb_ref[j, 1:2, :], nw_ref[j])
                d4 = vjp((dy_ref[b, :, 128 * j:128 * (j + 1)].astype(F32), dst[j, b]))
                for k in range(4):
                    dx_ref[b, :, HG_COLS * j + 128 * k:HG_COLS * j + 128 * (k + 1)] = d4[k].astype(dx_ref.dtype)
                dst[j, b] = d4[4]
                dlb_ref[j, 0:1, :] += d4[5]
                dlb_ref[j, 1:2, :] += d4[6]
                dnw_ref[j, 0:1, :] += d4[7]

    acc = pl.BlockSpec((hp, 8, 128), lambda h, i: (h, 0, 0))
    in_specs = [pl.BlockSpec((B, CHUNK, HG_COLS * hp), lambda h, i: (0, nc - 1 - i, h)),
                pl.BlockSpec((hp, 2, 128), lambda h, i: (h, 0, 0)),
                pl.BlockSpec((hp, 1, 128), lambda h, i: (h, 0, 0)),
                pl.BlockSpec((hp, B, None, 128, 128), lambda h, i: (h, 0, nc - 1 - i, 0, 0)),
                pl.BlockSpec((B, CHUNK, 128 * hp), lambda h, i: (0, nc - 1 - i, h))]
    args = [qfig, lbh, nwh, saved, dy]
    if after is not None:
        body = _skip_ref(body, len(args))
        args.append(_deps(after))
        in_specs.append(_dep_spec(args[-1]))
    return pl.pallas_call(
        body, grid=(HG_HEADS // hp, nc), in_specs=in_specs,
        out_specs=[pl.BlockSpec((B, CHUNK, HG_COLS * hp), lambda h, i: (0, nc - 1 - i, h)), acc, acc],
        out_shape=[jax.ShapeDtypeStruct((B, Tp, 4096), BF16), jax.ShapeDtypeStruct((HG_HEADS, 8, 128), F32),
                   jax.ShapeDtypeStruct((HG_HEADS, 8, 128), F32)],
        scratch_shapes=[pltpu.VMEM((hp, B, 128, 128), F32)],
        name=name, compiler_params=_cparams(("arbitrary", "arbitrary")),
    )(*args)


def _adamw(w, g, m, v, name, after=None):
    R, C = w.shape
    tr = max(t for t in range(8, R + 1, 8) if R % t == 0 and (t * C * 4 <= ADAMW_BLOCK_BYTES or t == 8))

    def body(w_ref, g_ref, m_ref, v_ref, d_ref, mo_ref, vo_ref):
        g_ = g_ref[...]
        m_ = ADAM_B1 * m_ref[...] + (1.0 - ADAM_B1) * g_
        v_ = ADAM_B2 * v_ref[...] + (1.0 - ADAM_B2) * (g_ * g_)
        m_hat = m_ / (1.0 - ADAM_B1 ** ADAM_STEP)
        v_hat = v_ / (1.0 - ADAM_B2 ** ADAM_STEP)
        d_ref[...] = -ADAM_LR * (m_hat / (jnp.sqrt(v_hat) + ADAM_EPS) + ADAM_WD * w_ref[...])
        mo_ref[...] = m_
        vo_ref[...] = v_

    sp = pl.BlockSpec((tr, C), lambda i: (i, 0))
    sh = jax.ShapeDtypeStruct((R, C), F32)
    in_specs, args = [sp] * 4, [w, g, m, v]
    if after is not None:
        body = _skip_ref(body, len(args))
        args.append(_deps(after))
        in_specs.append(_dep_spec(args[-1]))
    return pl.pallas_call(body, grid=(R // tr,), in_specs=in_specs, out_specs=[sp] * 3, out_shape=[sh] * 3,
                          name=name, compiler_params=_cparams(("arbitrary",)))(*args)


def _ffn_fwd(h, norm_w, w_gu, w_down, tag, after_norm=None):
    n = _rms_fwd(h, norm_w, f"{tag}_norm")
    if after_norm is not None:
        after_norm(n)
    gu, a = _gu_swiglu(n, w_gu, f"{tag}_gu")
    out = _matmul(a, w_down, mode="nn", out_dtype=F32, alpha=0.5, res=h, name=f"{tag}_down")
    return out, (n, gu, a)


def _ffn_bwd(h, norm_w, w_gu, w_down, saved, dout, tag, after_dw_down=None):
    n, gu, a = saved
    dgu = _d_swiglu(dout, w_down, gu, 0.5, f"{tag}_d_gu")
    dw_down = _matmul(a, dout, mode="tn", out_dtype=F32, alpha=0.5, name=f"{tag}_dw_down")
    dw_gu = _matmul(n, dgu, mode="tn", out_dtype=F32, out_groups=N_CHIPS, name=f"{tag}_dw_gu",
                    after=after_dw_down(dw_down) if after_dw_down else None)
    dn = _matmul(dgu, w_gu, mode="nt", out_dtype=F32, name=f"{tag}_d_norm")
    dh, dnw = _rms_bwd(h, norm_w, dn, dout, f"{tag}_d_in")
    return dh, dnw, dw_gu, dw_down


IN_NAMES = ("z", "xbc", "dt", "q", "f", "i", "g", "gates")


def _split_w_in(w_in_full):
    pts = [0]
    for s in IN_SIZES:
        pts.append(pts[-1] + s)
    sl = lambda i, j: w_in_full[:, pts[i]:pts[j]]
    qfig = sl(3, 7).reshape(D_MODEL, 4, HG_HEADS, 128).transpose(0, 2, 1, 3).reshape(D_MODEL, 4 * D_MODEL)
    return {"z": sl(0, 1), "xbc": sl(1, 2), "dt": jnp.pad(sl(2, 3), ((0, 0), (0, 128 - SSD_HEADS))),
            "qfig": qfig, "gates": sl(7, 9)}


def _local_step(x, target, W):
    B, S, _ = x.shape
    T = N_META + S
    pad = (-T) % CHUNK
    Tp = T + pad
    assert pad + N_META == CHUNK
    R = B * Tp
    meta = jnp.broadcast_to(W["meta_tokens"][None], (B, N_META, D_MODEL))
    h0 = jnp.concatenate([jnp.zeros((B, pad, D_MODEL), F32), meta, x], axis=1).reshape(R, D_MODEL)

    stage = W.get("_stage", lambda name, x: {})
    W = dict(W)
    h1, sv1 = _ffn_fwd(h0, W["ffn1_norm"], W["ffn1_w_gu"], W["ffn1_w_down"], "ffn1", lambda n: W.update(stage("ffn1_norm", n)))
    W.update(stage("ffn1_out", h1))
    um = _rms_fwd(h1, W["mix_norm"], "mix_norm")
    wi = W["w_in"]
    z = _matmul(um, wi["z"], mode="nn", out_dtype=BF16, name="in_z")
    xbc = _matmul(um, wi["xbc"], mode="nn", out_dtype=F32, name="in_xbc")
    dtr = _matmul(um, wi["dt"], mode="nn", out_dtype=F32, name="in_dt")
    qfig = _matmul(um, wi["qfig"], mode="nn", out_dtype=F32, name="in_qfig")
    gates = _matmul(um, wi["gates"], mode="nn", out_dtype=BF16, name="in_gates")

    r3 = lambda t: t.reshape(B, Tp, t.shape[-1])
    lane_pad = lambda t: jnp.pad(t, ((0, 0), (0, 128 - t.shape[1])))
    dt_bias, a_log, dskip = lane_pad(W["ssd_dt_bias"]), lane_pad(W["ssd_a_log"]), lane_pad(W["ssd_d"])
    xact = _conv_fwd(r3(xbc), W["ssd_conv_w"], W["ssd_conv_b"], pad, "conv_fwd")
    ya, ssd_saved = _ssd_fwd(xact, r3(dtr), r3(z), dt_bias, a_log, dskip, W["ssd_norm"], pad, "ssd_fwd")
    lbh = W["hg_lower_bound"].reshape(2, HG_HEADS, 128).transpose(1, 0, 2)
    nwh = W["hg_norm"].reshape(HG_HEADS, 1, 128)
    yb, hg_saved = _hg_fwd(r3(qfig), lbh, nwh, pad, "hg_fwd")
    ya2, yb2 = ya.reshape(R, -1), yb.reshape(R, -1)
    W.update(stage("mixers_out", yb2))
    pa = _matmul(ya2, W["w_branch_a"], mode="nn", out_dtype=F32, name="branch_a")
    pb = _matmul(yb2, W["w_branch_b"], mode="nn", out_dtype=F32, name="branch_b")
    mg = _merge_fwd(pa, pb, gates, "merge")
    h2 = _matmul(mg, W["w_out"], mode="nn", out_dtype=F32, res=h1, name="mix_out")
    h3, sv2 = _ffn_fwd(h2, W["ffn2_norm"], W["ffn2_w_gu"], W["ffn2_w_down"], "ffn2")

    loss, dh3, d_final = _loss_head(h3, W["final_norm"].reshape(1, D_MODEL), target, B, "loss_head")

    G = {"final_norm": d_final[0]}
    dh2, dnw, G["ffn2_w_gu"], G["ffn2_w_down"] = _ffn_bwd(h2, W["ffn2_norm"], W["ffn2_w_gu"], W["ffn2_w_down"], sv2, dh3, "ffn2")
    G["ffn2_norm"] = dnw[0:1]
    dmg = _matmul(dh2, W["w_out"], mode="nt", out_dtype=BF16, name="d_merge")
    G["w_out"] = _matmul(mg, dh2, mode="tn", out_dtype=F32, name="dw_out")
    dpa, dpb, dgates = _merge_bwd(pa, pb, gates, dmg, "merge_bwd")
    dya = _matmul(dpa, W["w_branch_a"], mode="nt", out_dtype=BF16, name="d_ya")
    dyb = _matmul(dpb, W["w_branch_b"], mode="nt", out_dtype=BF16, name="d_yb")
    G["w_branch_a"] = _matmul(ya2, dpa, mode="tn", out_dtype=F32, name="dw_branch_a")
    G["w_branch_b"] = _matmul(yb2, dpb, mode="tn", out_dtype=F32, name="dw_branch_b")

    dxact, ddtr, dz, dpar, dnw = _ssd_bwd(xact, r3(dtr), r3(z), dt_bias, a_log, dskip, W["ssd_norm"], ssd_saved,
                                          r3(dya), pad, "ssd_bwd", after=stage("late_grads", G).get("_after"))
    G["ssd_dt_bias"], G["ssd_a_log"], G["ssd_d"] = dpar[0:1, :SSD_HEADS], dpar[1:2, :SSD_HEADS], dpar[2:3, :SSD_HEADS]
    G["ssd_norm"] = dnw[0:1]
    dxbc, dcw, dcb = _conv_bwd(r3(xbc), W["ssd_conv_w"], W["ssd_conv_b"], dxact, pad, "conv_bwd")
    G["ssd_conv_w"], G["ssd_conv_b"] = dcw[0:SSD_CONV], dcb[0:1]
    dqfig, dlb, dhn = _hg_bwd(r3(qfig), lbh, nwh, hg_saved, r3(dyb), pad, "hg_bwd",
                              after=stage("after_conv_bwd", dcb).get("_after"))
    G["hg_lower_bound"] = dlb[:, 0:2, :].transpose(1, 0, 2).reshape(2, D_MODEL)
    G["hg_norm"] = dhn[:, 0, :].reshape(1, D_MODEL)

    r2 = lambda t: t.reshape(R, t.shape[-1])
    pieces = [("z", r2(dz)), ("xbc", r2(dxbc)), ("dt", r2(ddtr)), ("qfig", r2(dqfig)), ("gates", dgates)]
    dum = None
    dwi = {}
    for nm, dpiece in pieces:
        dum = _matmul(dpiece, wi[nm], mode="nt", out_dtype=F32, res=dum, name=f"d_mix_{nm}")
        dwi[nm] = _matmul(um, dpiece, mode="tn", out_dtype=F32, name=f"dw_in_{nm}")
    dw_qfig = dwi["qfig"].reshape(D_MODEL, HG_HEADS, 4, 128).transpose(0, 2, 1, 3).reshape(D_MODEL, 4 * D_MODEL)
    G["w_in"] = jnp.concatenate([dwi["z"], dwi["xbc"], dwi["dt"][:, :SSD_HEADS], dw_qfig, dwi["gates"]], axis=1)
    dh1, dnw = _rms_bwd(h1, W["mix_norm"], dum, dh2, "mix_norm_bwd", after=stage("w_in_grads", dwi).get("_after"))
    G["mix_norm"] = dnw[0:1]
    dh0, dnw, G["ffn1_w_gu"], G["ffn1_w_down"] = _ffn_bwd(h0, W["ffn1_norm"], W["ffn1_w_gu"], W["ffn1_w_down"], sv1, dh1, "ffn1",
                                                           lambda dw: stage("ffn1_dw_down", dw).get("_after"))
    G["ffn1_norm"] = dnw[0:1]
    dh0 = dh0.reshape(B, Tp, D_MODEL)
    G["meta_tokens"] = jnp.sum(dh0[:, pad:CHUNK], axis=0)
    return loss, dh0[:, CHUNK:], G


ANY = pl.BlockSpec(memory_space=pl.ANY)


def _place():
    return lax.axis_index("x"), lax.axis_index("y"), lax.axis_index("c")


def _other_chips(x, y):
    return [(1 - x, y), (x, 1 - y), (1 - x, 1 - y)]


def _remote(src, dst, ssem, rsem, dev):
    return pltpu.make_async_remote_copy(src_ref=src, dst_ref=dst, send_sem=ssem, recv_sem=rsem,
                                        device_id=dev, device_id_type=MESH)


def _exchange8(buf, reduce, name):
    n, w = buf.shape

    def body(x_ref, *rest):
        if reduce:
            red_ref, out_ref, ssem, rsem = rest
        else:
            out_ref, ssem, rsem = rest
        x, y, c = _place()
        me = 4 * x + 2 * y + c
        out_ref[me] = x_ref[...]
        copies = []
        for k in range(1, 8):
            px = 1 - x if (k >> 2) & 1 else x
            py = 1 - y if (k >> 1) & 1 else y
            pc = 1 - c if k & 1 else c
            cp = _remote(x_ref, out_ref.at[me], ssem.at[k - 1], rsem.at[k - 1], (px, py, pc))
            cp.start()
            copies.append((cp, 4 * px + 2 * py + pc))
        for k, (cp, peer) in enumerate(copies):
            _remote(x_ref, out_ref.at[peer], ssem.at[k], rsem.at[k], (x, y, c)).wait_recv()
        for cp, _ in copies:
            cp.wait_send()
        if reduce:
            acc = out_ref[0]
            for d in range(1, 8):
                acc = acc + out_ref[d]
            red_ref[...] = acc

    vm = pl.BlockSpec(memory_space=pltpu.VMEM)
    g_shape = jax.ShapeDtypeStruct((8, n, w), F32)
    if reduce:
        out_shape, out_specs, scratch = [jax.ShapeDtypeStruct((n, w), F32)], [vm], [pltpu.VMEM((8, n, w), F32)]
    else:
        out_shape, out_specs, scratch = [g_shape], [vm], []
    return pl.pallas_call(
        body, in_specs=[vm], out_specs=out_specs, out_shape=out_shape,
        scratch_shapes=scratch + [pltpu.SemaphoreType.DMA((7,)), pltpu.SemaphoreType.DMA((7,))], name=name,
    )(buf)[0]


HBM = pltpu.MemorySpace.HBM


def _sequencer(name, collective_id, sems, sent):
    return functools.partial(pl.kernel, mesh=plsc.ScalarSubcoreMesh(axis_name="sequencer", num_cores=1), name=name,
                             scratch_types=sems, compiler_params=pltpu.CompilerParams(collective_id=collective_id),
                             cost_estimate=pl.CostEstimate(flops=0, transcendentals=0, bytes_accessed=2 * sent,
                                                           remote_bytes_transferred=sent))


def _nbytes(arrays):
    return sum(a.size * a.dtype.itemsize for a in arrays)


def _handshake(peers):
    barrier = pltpu.get_barrier_semaphore()
    for peer in peers:
        pl.semaphore_signal(barrier, inc=1, device_id=peer, device_id_type=MESH)
    pl.semaphore_wait(barrier, len(peers))


def _gather_seq(blocks, name, collective_id):
    n = len(blocks)
    half = [s.shape[1] // 2 for s in blocks]
    full = [jax.new_ref(b, memory_space=HBM) for b in blocks]

    @_sequencer(name, collective_id, [pltpu.SemaphoreType.DMA((n, 3))] * 4, _nbytes(blocks) * 3 // 4)
    def launch(ssem, rsem, fssem, frsem):
        x, y, c = _place()
        q = 2 * x + y
        chips = _other_chips(x, y)
        _handshake([(px, py, c) for px, py in chips] + [(x, y, 1 - c)])
        piece = lambda s, qq, cc: full[s].at[qq, pl.ds(cc * half[s], half[s])]
        sends = []
        for j, (px, py) in enumerate(chips):
            for s in range(n):
                cp = _remote(piece(s, q, c), piece(s, q, c), ssem.at[s, j], rsem.at[s, j], (px, py, c))
                cp.start()
                sends.append(cp)
        for j, (px, py) in enumerate(chips):
            for s in range(n):
                got = piece(s, 2 * px + py, c)
                _remote(got, got, ssem.at[s, j], rsem.at[s, j], (px, py, c)).wait_recv()
                cp = _remote(got, got, fssem.at[s, j], frsem.at[s, j], (x, y, 1 - c))
                cp.start()
                sends.append(cp)
        for j, (px, py) in enumerate(chips):
            for s in range(n):
                got = piece(s, 2 * px + py, 1 - c)
                _remote(got, got, fssem.at[s, j], frsem.at[s, j], (x, y, 1 - c)).wait_recv()
        for cp in sends:
            cp.wait_send()

    launch()
    return [r[...] for r in full]


def _pair_swap(parts, name, collective_id):
    n = len(parts)
    half = [p.shape[1] // 2 for p in parts]
    src = [jax.new_ref(p, memory_space=HBM) for p in parts]
    got = [jax.empty_ref(jax.ShapeDtypeStruct((p.shape[0], h, p.shape[2]), p.dtype), memory_space=HBM) for p, h in zip(parts, half)]

    @_sequencer(name, collective_id, [pltpu.SemaphoreType.DMA((n,))] * 2, _nbytes(parts) // 2)
    def launch(ssem, rsem):
        x, y, c = _place()
        _handshake([(x, y, 1 - c)])
        copies = []
        for s in range(n):
            cp = _remote(src[s].at[pl.ds(0, parts[s].shape[0]), pl.ds((1 - c) * half[s], half[s])], got[s], ssem.at[s], rsem.at[s], (x, y, 1 - c))
            cp.start()
            copies.append(cp)
        for cp in copies:
            cp.wait_recv()
        for cp in copies:
            cp.wait_send()

    launch()
    return [g[...] for g in got]


def _to_owners(sums, name, collective_id):
    n = len(sums)
    src = [jax.new_ref(s, memory_space=HBM) for s in sums]
    got = [jax.empty_ref(jax.ShapeDtypeStruct(s.shape, s.dtype), memory_space=HBM) for s in sums]

    @_sequencer(name, collective_id, [pltpu.SemaphoreType.DMA((n, 3))] * 2, _nbytes(sums) * 3 // 4)
    def launch(ssem, rsem):
        x, y, c = _place()
        q = 2 * x + y
        chips = _other_chips(x, y)
        _handshake([(px, py, c) for px, py in chips])
        sends = []
        for j, (px, py) in enumerate(chips):
            for s in range(n):
                cp = _remote(src[s].at[2 * px + py], got[s].at[q], ssem.at[s, j], rsem.at[s, j], (px, py, c))
                cp.start()
                sends.append(cp)
        for j, (px, py) in enumerate(chips):
            for s in range(n):
                slot = got[s].at[2 * px + py]
                _remote(slot, slot, ssem.at[s, j], rsem.at[s, j], (px, py, c)).wait_recv()
        for cp in sends:
            cp.wait_send()

    launch()
    return [g[...] for g in got]


def _pair_join(blocks, name, collective_id):
    n = len(blocks)
    out = [jax.new_ref(b, memory_space=HBM) for b in blocks]

    @_sequencer(name, collective_id, [pltpu.SemaphoreType.DMA((n,))] * 2, _nbytes(blocks) // 2)
    def launch(ssem, rsem):
        x, y, c = _place()
        _handshake([(x, y, 1 - c)])
        sends = []
        for s in range(n):
            h = blocks[s].shape[0] // 2
            mine = out[s].at[pl.ds(c * h, h)]
            cp = _remote(mine, mine, ssem.at[s], rsem.at[s], (x, y, 1 - c))
            cp.start()
            sends.append(cp)
        for s in range(n):
            h = blocks[s].shape[0] // 2
            theirs = out[s].at[pl.ds((1 - c) * h, h)]
            _remote(theirs, theirs, ssem.at[s], rsem.at[s], (x, y, 1 - c)).wait_recv()
        for cp in sends:
            cp.wait_send()

    launch()
    return [o[...] for o in out]


WIRE = BF16


def _row_tile(h):
    return _pick(h, (256, 368, 352, 128, 16))


def _add_pair(part, got, c, name, after=None):
    _, h, w = got.shape
    tr = _row_tile(h)
    nt = h // tr

    def body(c_ref, p_ref, g_ref, o_ref):
        o_ref[...] = (p_ref[...] + g_ref[...].astype(F32)).astype(o_ref.dtype)

    in_specs = [pl.BlockSpec((None, tr, w), lambda q, i, c_ref: (q, c_ref[0] * nt + i, 0)),
                pl.BlockSpec((None, tr, w), lambda q, i, c_ref: (q, i, 0))]
    args = [c.reshape(1).astype(jnp.int32), part, got]
    if after is not None:
        body = _skip_ref(body, len(args))
        args.append(_deps(after))
        in_specs.append(_dep_spec(args[-1]))
    return pl.pallas_call(
        body,
        grid_spec=pltpu.PrefetchScalarGridSpec(
            num_scalar_prefetch=1, grid=(got.shape[0], nt), in_specs=in_specs,
            out_specs=pl.BlockSpec((None, tr, w), lambda q, i, c_ref: (q, i, 0))),
        out_shape=jax.ShapeDtypeStruct(got.shape, WIRE), name=name,
        compiler_params=_cparams(("arbitrary", "arbitrary")),
    )(*args)


def _sum_chips(slots, sums, q, c, name, after=None):
    _, h, w = slots.shape
    tr = _row_tile(h)
    nt = h // tr

    def body(s_ref, mine_ref, a_ref, b_ref, d_ref, o_ref):
        o_ref[...] = ((mine_ref[...].astype(F32) + a_ref[...].astype(F32)) + b_ref[...].astype(F32)) + d_ref[...].astype(F32)

    slot = lambda k: pl.BlockSpec((None, tr, w), lambda i, s_ref: (s_ref[1 + k], i, 0))
    scalars = jnp.stack([c, q, (q + 1) % N_CHIPS, (q + 2) % N_CHIPS, (q + 3) % N_CHIPS]).astype(jnp.int32)
    in_specs, args = [slot(0), slot(1), slot(2), slot(3)], [scalars, sums, slots, slots, slots]
    if after is not None:
        body = _skip_ref(body, len(args))
        args.append(_deps(after))
        in_specs.append(_dep_spec(args[-1]))
    return pl.pallas_call(
        body,
        grid_spec=pltpu.PrefetchScalarGridSpec(
            num_scalar_prefetch=1, grid=(nt,), in_specs=in_specs,
            out_specs=pl.BlockSpec((tr, w), lambda i, s_ref: (s_ref[0] * nt + i, 0))),
        out_shape=jax.ShapeDtypeStruct((2 * h, w), F32), name=name,
        compiler_params=_cparams(("arbitrary",)),
    )(*args)


class _Reduce:
    def __init__(self, parts, q, c, tag, first_id, regions=None):
        self.parts, self.q, self.c, self.tag, self.first_id, self.regions = parts, q, c, tag, first_id, regions
        self.got = _pair_swap(parts, f"{tag}_pair_swap", first_id)

    def to_owners(self, after=None):
        self.sums = [_add_pair(p, g, self.c, f"{self.tag}_pair_add{i}", after)
                     for i, (p, g) in enumerate(zip(self.parts, self.got))]
        if self.regions is not None:
            self.sums = self.regions(self.sums)
        self.slots = _to_owners(self.sums, f"{self.tag}_to_owners", self.first_id + 1)
        return self.sums

    def join(self, after=None):
        blocks = [_sum_chips(sl, sm, self.q, self.c, f"{self.tag}_sum_chips{i}", after)
                  for i, (sl, sm) in enumerate(zip(self.slots, self.sums))]
        self.out = _pair_join(blocks, f"{self.tag}_pair_join", self.first_id + 2)
        return blocks


WEIGHTS = ("meta_tokens", "ffn1_norm", "ffn1_w_gu", "ffn1_w_down", "mix_norm", "w_in", "ssd_conv_w", "ssd_conv_b",
           "ssd_dt_bias", "ssd_a_log", "ssd_d", "ssd_norm", "hg_lower_bound", "hg_norm", "w_branch_a", "w_branch_b",
           "w_out", "ffn2_norm", "ffn2_w_gu", "ffn2_w_down", "final_norm")
BIG = ("ffn1_w_gu", "ffn1_w_down", "w_in", "w_branch_a", "w_branch_b", "w_out", "ffn2_w_gu", "ffn2_w_down")
ROW_SHARDED = ("ffn1_w_down", "ffn2_w_down", "w_branch_a", "w_branch_b", "w_out")
SMALL = tuple(n for n in WEIGHTS if n not in BIG)
SMALL_ROWS = 24


def _rows1024(a):
    flat = a.reshape(-1)
    n = -(-flat.shape[0] // 1024) * 1024
    return jnp.pad(flat, (0, n - flat.shape[0])).reshape(-1, 1024)


def _pack_small(d):
    rows = jnp.concatenate([_rows1024(d[n]) for n in SMALL], axis=0)
    return jnp.pad(rows, ((0, SMALL_ROWS - rows.shape[0]), (0, 0)))


def _unpack_small(packed, like):
    out, r = {}, 0
    for n in SMALL:
        size = like[n].size
        nr = -(-size // 1024)
        out[n] = packed[r:r + nr].reshape(-1)[:size].reshape(like[n].shape)
        r += nr
    return out


def kernel(x, meta_tokens, ffn1_norm, ffn1_w_gu, ffn1_w_down, mix_norm, w_in, ssd_conv_w, ssd_conv_b, ssd_dt_bias, ssd_a_log, ssd_d, ssd_norm, hg_lower_bound, hg_norm, w_branch_a, w_branch_b, w_out, ffn2_norm, ffn2_w_gu, ffn2_w_down, final_norm, loss_target, m_meta_tokens, m_ffn1_norm, m_ffn1_w_gu, m_ffn1_w_down, m_mix_norm, m_w_in, m_ssd_conv_w, m_ssd_conv_b, m_ssd_dt_bias, m_ssd_a_log, m_ssd_d, m_ssd_norm, m_hg_lower_bound, m_hg_norm, m_w_branch_a, m_w_branch_b, m_w_out, m_ffn2_norm, m_ffn2_w_gu, m_ffn2_w_down, m_final_norm, v_meta_tokens, v_ffn1_norm, v_ffn1_w_gu, v_ffn1_w_down, v_mix_norm, v_w_in, v_ssd_conv_w, v_ssd_conv_b, v_ssd_dt_bias, v_ssd_a_log, v_ssd_d, v_ssd_norm, v_hg_lower_bound, v_hg_norm, v_w_branch_a, v_w_branch_b, v_w_out, v_ffn2_norm, v_ffn2_w_gu, v_ffn2_w_down, v_final_norm):
    P = dict(zip(WEIGHTS, (meta_tokens, ffn1_norm, ffn1_w_gu, ffn1_w_down, mix_norm, w_in, ssd_conv_w, ssd_conv_b, ssd_dt_bias, ssd_a_log, ssd_d, ssd_norm, hg_lower_bound, hg_norm, w_branch_a, w_branch_b, w_out, ffn2_norm, ffn2_w_gu, ffn2_w_down, final_norm)))
    M = dict(zip(WEIGHTS, (m_meta_tokens, m_ffn1_norm, m_ffn1_w_gu, m_ffn1_w_down, m_mix_norm, m_w_in, m_ssd_conv_w, m_ssd_conv_b, m_ssd_dt_bias, m_ssd_a_log, m_ssd_d, m_ssd_norm, m_hg_lower_bound, m_hg_norm, m_w_branch_a, m_w_branch_b, m_w_out, m_ffn2_norm, m_ffn2_w_gu, m_ffn2_w_down, m_final_norm)))
    V = dict(zip(WEIGHTS, (v_meta_tokens, v_ffn1_norm, v_ffn1_w_gu, v_ffn1_w_down, v_mix_norm, v_w_in, v_ssd_conv_w, v_ssd_conv_b, v_ssd_dt_bias, v_ssd_a_log, v_ssd_d, v_ssd_norm, v_hg_lower_bound, v_hg_norm, v_w_branch_a, v_w_branch_b, v_w_out, v_ffn2_norm, v_ffn2_w_gu, v_ffn2_w_down, v_final_norm)))
    cx, cy, cc = _place()
    q = 2 * cx + cy

    mine = jnp.concatenate([meta_tokens.reshape(4, 1024), ssd_conv_w.reshape(2, 1024), jnp.zeros((2, 1024), F32)], axis=0)
    every = _exchange8(mine, False, "gather_small")
    meta_full = jnp.concatenate([every[2 * k, 0:4].reshape(N_META, 256) for k in range(N_CHIPS)], axis=1)
    conv_w_full = jnp.concatenate([every[2 * k, 4:6].reshape(SSD_CONV, 512) for k in range(N_CHIPS)], axis=1)

    late = ("ffn2_w_down", "w_branch_a", "w_branch_b", "w_out")
    rows = jnp.concatenate([P[n][0] for n in late], axis=0)
    zero = lambda t, dtype=F32: (t[0:1, 0:1] * 0).astype(dtype)

    def in_slot(s, after=None):
        s = s if after is None else s + zero(after)
        return lax.dynamic_update_slice(lax.empty((N_CHIPS,) + s.shape, BF16), s.astype(BF16)[None], (q, 0, 0))

    gu1, down1 = _gather_seq([in_slot(ffn1_w_gu[0]), in_slot(ffn1_w_down[0])], "gather_ffn1", 1)
    W = {n: P[n] for n in SMALL}
    W["meta_tokens"], W["ssd_conv_w"] = meta_full, conv_w_full
    W["ffn1_w_gu"], W["ffn1_w_down"] = gu1, down1.reshape(-1, D_MODEL)
    flying = {}

    def stage(name, t):
        if name == "ffn1_norm":
            flying["w_in"] = _gather_seq([in_slot(w_in[0], t)], "gather_w_in", 2)
            return {}
        if name == "ffn1_out":
            flying["late"] = _gather_seq([in_slot(ffn2_w_gu[0], t), in_slot(rows, t)], "gather_late", 3)
            (w_in_all,) = flying["w_in"]
            w_in_all = w_in_all + zero(t, BF16)
            return {"w_in": _split_w_in(w_in_all.transpose(1, 0, 2).reshape(D_MODEL, -1))}
        if name == "mixers_out":
            gu2, rows_all = flying["late"]
            out, r = {"ffn2_w_gu": gu2}, 0
            for n in late:
                nr = P[n].shape[1]
                out[n] = (rows_all[:, r:r + nr] + zero(t, BF16)).reshape(N_CHIPS * nr, D_MODEL)
                r += nr
            return out
        if name == "late_grads":
            row_parts = jnp.concatenate([t[n].reshape(N_CHIPS, -1, D_MODEL) for n in late], axis=1)
            flying["grad_late"] = _Reduce([t["ffn2_w_gu"], row_parts], q, cc, "grad_late", 4)
            return {"_after": [t["ffn2_w_gu"]] + [t[n] for n in late]}
        if name == "after_conv_bwd":
            return {"_after": flying["grad_late"].to_owners(after=t)}
        if name == "w_in_grads":
            order = ("z", "xbc", "dt", "qfig", "gates")
            blocks = flying["grad_late"].join(after=[t[k] for k in order])

            def regions(sums):
                z, xbc, dt, qfig, gates = [s[0] for s in sums]
                h = z.shape[0]
                qfig = qfig.reshape(h, HG_HEADS, 4, 128).transpose(0, 2, 1, 3).reshape(h, 4 * D_MODEL)
                cols = jnp.concatenate([z, xbc, dt[:, :SSD_HEADS], qfig, gates], axis=1)
                return [cols.reshape(h, N_CHIPS, -1).transpose(1, 0, 2)]

            flying["grad_w_in"] = _Reduce([t[k][None] for k in order], q, cc, "grad_w_in", 7, regions)
            return {"_after": blocks}
        if name == "ffn1_dw_down":
            return {"_after": flying["grad_w_in"].to_owners(after=t)}
        return {}

    W["_stage"] = stage

    loss8, grad_x, G = _local_step(x, loss_target, W)

    small = jnp.concatenate(
        [G["meta_tokens"]] + [_rows1024(G[n]) for n in SMALL if n != "meta_tokens"] + [_rows1024(loss8[0:1, 0:1])], axis=0)
    small = jnp.pad(small, ((0, 40 - small.shape[0]), (0, 0)))
    small = _exchange8(small, True, "reduce_small")
    Gs = {"meta_tokens": small[0:N_META]}
    r = N_META
    for n in SMALL:
        if n == "meta_tokens":
            continue
        nr = -(-G[n].size // 1024)
        Gs[n] = small[r:r + nr].reshape(-1)[:G[n].size].reshape(G[n].shape)
        r += nr
    loss = small[r, 0]
    Gs["meta_tokens"] = lax.dynamic_slice(Gs["meta_tokens"], (0, 256 * q), (N_META, 256))
    Gs["ssd_conv_w"] = lax.dynamic_slice(Gs["ssd_conv_w"], (0, 512 * q), (SSD_CONV, 512))[None]
    Gs = {n: Gs[n].reshape(P[n].shape) for n in SMALL}

    grad_ffn1 = _Reduce([G["ffn1_w_gu"], G["ffn1_w_down"].reshape(N_CHIPS, -1, D_MODEL)], q, cc, "grad_ffn1", 10)
    flying["grad_w_in"].join(after=grad_x)
    going = grad_ffn1.to_owners(after=grad_x)
    g_gu2, g_rows = flying["grad_late"].out
    (g_w_in,) = flying["grad_w_in"].out
    Gb = {"ffn2_w_gu": g_gu2, "w_in": g_w_in}
    r = 0
    for n in late:
        nr = P[n].shape[1]
        Gb[n] = g_rows[r:r + nr]
        r += nr

    grads, delta, new_m, new_v = dict(Gs), {}, {}, {}
    d_s, m_s, v_s = _adamw(_pack_small(P), _pack_small(Gs), _pack_small(M), _pack_small(V), "adamw_small", after=going)
    delta.update(_unpack_small(d_s, P))
    new_m.update(_unpack_small(m_s, P))
    new_v.update(_unpack_small(v_s, P))
    done = [d_s]
    cols = w_in.shape[2]
    to_tiles = lambda a: a.transpose(2, 0, 1).reshape(cols, 8, 128).reshape(cols * 8, 128)
    from_tiles = lambda a: a.reshape(cols, 1, D_MODEL).transpose(1, 2, 0)
    for n in [n for n in BIG if n in Gb]:
        if n == "w_in":
            g_t = to_tiles(Gb[n][None])
            d_, m_, v_ = _adamw(to_tiles(P[n]), g_t, to_tiles(M[n]), to_tiles(V[n]), f"adamw_{n}", after=going)
            grads[n], delta[n], new_m[n], new_v[n] = from_tiles(g_t), from_tiles(d_), from_tiles(m_), from_tiles(v_)
        else:
            d_, m_, v_ = _adamw(P[n][0], Gb[n], M[n][0], V[n][0], f"adamw_{n}", after=going)
            grads[n], delta[n], new_m[n], new_v[n] = Gb[n][None], d_[None], m_[None], v_[None]
        done.append(d_)
    grad_ffn1.join(after=done)
    Gb["ffn1_w_gu"], Gb["ffn1_w_down"] = grad_ffn1.out
    for n in ("ffn1_w_gu", "ffn1_w_down"):
        d_, m_, v_ = _adamw(P[n][0], Gb[n], M[n][0], V[n][0], f"adamw_{n}")
        grads[n], delta[n], new_m[n], new_v[n] = Gb[n][None], d_[None], m_[None], v_[None]
    return (loss, grad_x, *[grads[n] for n in WEIGHTS], *[delta[n] for n in WEIGHTS],
            *[new_m[n] for n in WEIGHTS], *[new_v[n] for n in WEIGHTS])
```

```python
import functools

import jax
import jax.numpy as jnp
from jax import lax
from jax.experimental import pallas as pl
from jax.experimental.pallas import tpu as pltpu
from jax.experimental.pallas import tpu_sc as plsc

F32 = jnp.float32
BF16 = jnp.bfloat16
HIGHEST = lax.Precision.HIGHEST
MESH = pl.DeviceIdType.MESH

D_MODEL = 1024
N_META = 16
EPS = 1e-6
SSD_HEADS = 16
SSD_HEAD_DIM = 64
SSD_INNER = 1024
SSD_GROUPS = 4
SSD_STATE = 128
SSD_CONV = 4
SSD_CONV_CH = 2048
HG_HEADS = 8
HG_SUB = 32
CHUNK = 128
D_FF = 2816
N_CHIPS = 4
IN_SIZES = (1024, 2048, 16, 1024, 1024, 1024, 1024, 1024, 1024)
ADAM_LR = 0.001
ADAM_B1 = 0.9
ADAM_B2 = 0.999
ADAM_EPS = 1e-08
ADAM_WD = 0.01
ADAM_STEP = 10
VMEM_LIMIT = 56 * 1024 * 1024
MATMUL_BLOCK_BYTES = 42 * 1024 * 1024
ADAMW_BLOCK_BYTES = 5 * 512 * 1024


def _cparams(sem=None):
    return pltpu.CompilerParams(dimension_semantics=sem, vmem_limit_bytes=VMEM_LIMIT)


def _pick(n, cands):
    for c in cands:
        if n % c == 0:
            return c
    return n


def _deps(after):
    xs = after if isinstance(after, (list, tuple)) else [after]
    one = lambda x: lax.slice(x, (0,) * x.ndim, (1,) * x.ndim).reshape(1).astype(F32)
    return jnp.concatenate([one(x) for x in xs]).reshape(1, -1)


def _dep_spec(dep):
    return pl.BlockSpec(dep.shape, lambda *_: (0, 0))


def _skip_ref(body, pos):
    return lambda *refs: body(*refs[:pos], *refs[pos + 1:])


def _dg(a, b, ca, cb):
    return lax.dot_general(a.astype(BF16), b.astype(BF16), (((ca,), (cb,)), ((), ())), preferred_element_type=F32)


@jax.custom_vjp
def _mm(a, b):
    return _dg(a, b, 1, 0)


def _mm_fwd(a, b):
    return _dg(a, b, 1, 0), (a, b)


def _mm_bwd(r, g):
    a, b = r
    return _dg(g, b, 1, 1), _dg(a, g, 0, 0)


_mm.defvjp(_mm_fwd, _mm_bwd)


@jax.custom_vjp
def _mm_nt(a, b):
    return _dg(a, b, 1, 1)


def _mm_nt_fwd(a, b):
    return _dg(a, b, 1, 1), (a, b)


def _mm_nt_bwd(r, g):
    a, b = r
    return _dg(g, b, 1, 0), _dg(g, a, 0, 0)


_mm_nt.defvjp(_mm_nt_fwd, _mm_nt_bwd)


@jax.custom_vjp
def _mm_tn(a, b):
    return _dg(a, b, 0, 0)


def _mm_tn_fwd(a, b):
    return _dg(a, b, 0, 0), (a, b)


def _mm_tn_bwd(r, g):
    a, b = r
    return _dg(b, g, 1, 1), _dg(a, g, 1, 0)


_mm_tn.defvjp(_mm_tn_fwd, _mm_tn_bwd)


def _tri_sum(x, lower):
    n = x.shape[0]
    ri = lax.broadcasted_iota(jnp.int32, (n, n), 0)
    ci = lax.broadcasted_iota(jnp.int32, (n, n), 1)
    tri = ((ri >= ci) if lower else (ri <= ci)).astype(BF16)
    x1 = x.astype(BF16)
    r1 = x - x1.astype(F32)
    x2 = r1.astype(BF16)
    x3 = (r1 - x2.astype(F32)).astype(BF16)
    dot = lambda p: lax.dot_general(tri, p, (((1,), (0,)), ((), ())), preferred_element_type=F32)
    return (dot(x3) + dot(x2)) + dot(x1)


@jax.custom_vjp
def _cumsum_rows(x):
    return _tri_sum(x, True)


_cumsum_rows.defvjp(lambda x: (_tri_sum(x, True), None), lambda _, g: (_tri_sum(g, False),))


def _silu(x):
    return x * jax.nn.sigmoid(x)


def _softplus(x):
    return jnp.maximum(x, 0.0) + jnp.log(1.0 + jnp.exp(-jnp.abs(x)))


def _tril(n):
    ri = lax.broadcasted_iota(jnp.int32, (n, n), 0)
    ci = lax.broadcasted_iota(jnp.int32, (n, n), 1)
    return ri >= ci


def _row_of(m, r):
    sub = lax.broadcasted_iota(jnp.int32, (m.shape[0], 1), 0)
    return jnp.sum(jnp.where(sub == r, m, 0.0), axis=0, keepdims=True)


def _col_of(m, c):
    lane = lax.broadcasted_iota(jnp.int32, (1, m.shape[1]), 1)
    return jnp.sum(jnp.where(lane == c, m, 0.0), axis=1, keepdims=True)


def _matmul(a, b, *, mode, out_dtype, name, alpha=1.0, res=None, tm=None, tn=None, out_groups=None, after=None):
    b3 = b.ndim == 3
    if mode == "nn":
        M, K = a.shape
        G = b.shape[0] if b3 else 1
        Ng = b.shape[-1]
        N = G * Ng
    elif mode == "nt":
        M, K = a.shape
        G = b.shape[0] if b3 else 1
        N = b.shape[-2]
        Kg = b.shape[-1]
        assert G * Kg == K
    else:
        K, M = a.shape
        N = b.shape[1]
        G = out_groups or 1
        Ng = N // G
    has_res = res is not None
    split_n = (mode == "nn" and b3) or (mode == "tn" and G > 1)
    per_mn = jnp.dtype(out_dtype).itemsize + (res.dtype.itemsize if has_res else 0)
    fits = [(m_ * n_, m_, n_)
            for m_ in (4352, 2176, 1408, 1088, 1024, 544, 512, 256, 128) if M % m_ == 0
            for n_ in (2816, 2048, 1408, 1024, 512, 256, 128) if (Ng if split_n else N) % n_ == 0
            if 2 * (K * m_ * a.dtype.itemsize + K * n_ * b.dtype.itemsize + m_ * n_ * per_mn) + 4 * m_ * n_ <= MATMUL_BLOCK_BYTES]
    _, tm_fit, tn_fit = max(fits)
    tm, tn = tm or tm_fit, tn or tn_fit
    nm, nn_ = M // tm, N // tn
    assert nm * tm == M and nn_ * tn == N, (name, M, N, K, tm, tn)

    if mode == "nn":
        a_spec = pl.BlockSpec((tm, K), lambda i, j: (i, 0))
        if b3:
            ns = Ng // tn
            b_spec = pl.BlockSpec((None, K, tn), lambda i, j: (j // ns, 0, j % ns))
        else:
            b_spec = pl.BlockSpec((K, tn), lambda i, j: (0, j))
        ca, cb = 1, 0
    elif mode == "nt":
        a_spec = pl.BlockSpec((tm, K), lambda i, j: (i, 0))
        if b3:
            b_spec = pl.BlockSpec((G, tn, Kg), lambda i, j: (0, j, 0))
        else:
            b_spec = pl.BlockSpec((tn, K), lambda i, j: (j, 0))
        ca, cb = 1, 1
    else:
        a_spec = pl.BlockSpec((K, tm), lambda i, j: (0, i))
        b_spec = pl.BlockSpec((K, tn), lambda i, j: (0, j))
        ca, cb = 0, 0
    if mode == "tn" and G > 1:
        ns = Ng // tn
        o_spec = pl.BlockSpec((None, tm, tn), lambda i, j: (j // ns, i, j % ns))
        out_shape = jax.ShapeDtypeStruct((G, M, Ng), out_dtype)
    else:
        o_spec = pl.BlockSpec((tm, tn), lambda i, j: (i, j))
        out_shape = jax.ShapeDtypeStruct((M, N), out_dtype)
    in_specs = [a_spec, b_spec]
    args = [a, b]
    if has_res:
        in_specs.append(pl.BlockSpec((tm, tn), lambda i, j: (i, j)))
        args.append(res)
    if after is not None:
        args.append(_deps(after))
        in_specs.append(_dep_spec(args[-1]))

    def body(*refs):
        a_ref, b_ref, o_ref = refs[0], refs[1], refs[-1]
        if mode == "nt" and b3:
            o = _dg(a_ref[:, 0:Kg], b_ref[0], ca, cb)
            for g in range(1, G):
                o = o + _dg(a_ref[:, g * Kg:(g + 1) * Kg], b_ref[g], ca, cb)
        else:
            o = _dg(a_ref[...], b_ref[...], ca, cb)
        if alpha != 1.0:
            o = o * alpha
        if has_res:
            o = o + refs[2][...]
        o_ref[...] = o.astype(o_ref.dtype)

    return pl.pallas_call(
        body, grid=(nm, nn_), in_specs=in_specs, out_specs=o_spec, out_shape=out_shape, name=name,
        compiler_params=_cparams(("parallel", "parallel")),
    )(*args)


def _sum_nt(xs, ws, name):
    R, N = xs[0].shape[0], ws[0].shape[0]
    n = len(xs)
    per_m = sum(x.shape[1] * x.dtype.itemsize for x in xs)
    per_n = sum(w.shape[1] * w.dtype.itemsize for w in ws)
    fits = [(m_ * n_, m_, n_) for m_ in (1088, 544, 256, 128) if R % m_ == 0 for n_ in (1024, 512, 256, 128) if N % n_ == 0
            if 2 * (m_ * per_m + n_ * per_n + m_ * n_ * 4) + 4 * m_ * n_ <= MATMUL_BLOCK_BYTES]
    _, tm, tn = max(fits)

    def body(*refs):
        o = _dg(refs[0][...], refs[n][...], 1, 1)
        for p in range(1, n):
            o = o + _dg(refs[p][...], refs[n + p][...], 1, 1)
        refs[-1][...] = o

    return pl.pallas_call(
        body, grid=(R // tm, N // tn),
        in_specs=[pl.BlockSpec((tm, x.shape[1]), lambda i, j: (i, 0)) for x in xs]
        + [pl.BlockSpec((tn, w.shape[1]), lambda i, j: (j, 0)) for w in ws],
        out_specs=pl.BlockSpec((tm, tn), lambda i, j: (i, j)), out_shape=jax.ShapeDtypeStruct((R, N), F32), name=name,
        compiler_params=_cparams(("parallel", "parallel")),
    )(*xs, *ws)


def _rms_fn(h, w):
    r = lax.rsqrt(jnp.mean(h * h, axis=-1, keepdims=True) + EPS)
    return h * r * w


def _swiglu_fn(gu):
    g = gu[:, :D_FF].astype(F32)
    u = gu[:, D_FF:].astype(F32)
    return _silu(g) * u


def _merge_fn(pa, pb, gates):
    return jax.nn.sigmoid(gates[:, :D_MODEL]) * pa + jax.nn.sigmoid(gates[:, D_MODEL:]) * pb


def _rows_call(body, *, rows, tr, ins, outs, accs=(), name, after=None):
    n = rows // tr
    assert n * tr == rows
    if after is not None:
        body = _skip_ref(body, len(ins))
        ins = list(ins) + [("full", _deps(after))]

    def spec(x):
        if isinstance(x, tuple):
            shp = x[1].shape
            return pl.BlockSpec(shp, lambda i: (0,) * len(shp))
        return pl.BlockSpec((tr, x.shape[1]), lambda i: (i, 0))

    in_specs = [spec(x) for x in ins]
    args = [x[1] if isinstance(x, tuple) else x for x in ins]
    out_specs = [spec(x) for x in outs] + [pl.BlockSpec(x.shape, lambda i: (0,) * len(x.shape)) for x in accs]
    out_shape = [x[1] if isinstance(x, tuple) else x for x in outs] + list(accs)
    return pl.pallas_call(
        body, grid=(n,), in_specs=in_specs, out_specs=out_specs, out_shape=out_shape, name=name,
        compiler_params=_cparams(("arbitrary",)),
    )(*args)


def _acc_rows(ref, val):
    @pl.when(pl.program_id(0) == 0)
    def _():
        ref[...] = jnp.zeros_like(ref)

    ref[0:1, :] += val


def _rms_fwd(h, w, name):
    def body(h_ref, w_ref, o_ref):
        o_ref[...] = _rms_fn(h_ref[...], w_ref[...]).astype(o_ref.dtype)

    R = h.shape[0]
    return _rows_call(body, rows=R, tr=_pick(R, (256, 128)), ins=[h, ("full", w)],
                      outs=[jax.ShapeDtypeStruct(h.shape, BF16)], name=name)[0]


def _rms_bwd(h, w, dn, dres, name, after=None):
    def body(h_ref, w_ref, dn_ref, dres_ref, dh_ref, dw_ref):
        _, vjp = jax.vjp(_rms_fn, h_ref[...], w_ref[...])
        dh, dw = vjp(dn_ref[...].astype(F32))
        dh_ref[...] = dh + dres_ref[...]
        _acc_rows(dw_ref, dw)

    R = h.shape[0]
    return _rows_call(body, rows=R, tr=_pick(R, (256, 128)), ins=[h, ("full", w), dn, dres],
                      outs=[jax.ShapeDtypeStruct(h.shape, F32)], accs=[jax.ShapeDtypeStruct((8, D_MODEL), F32)], name=name,
                      after=after)


def _rms_bwd_tokens(h, w, dn, dres, nseq, name):
    Tp = h.shape[0] // nseq
    nc = Tp // CHUNK

    def body(h_ref, w_ref, dn_ref, dres_ref, dx_ref, dm_ref, dw_ref):
        b, c = pl.program_id(0), pl.program_id(1)
        _, vjp = jax.vjp(_rms_fn, h_ref[...], w_ref[...])
        dh, dw = vjp(dn_ref[...].astype(F32))
        dh = dh + dres_ref[...]

        @pl.when(c == 0)
        def _():
            dm_ref[...] = dh

        @pl.when(c > 0)
        def _():
            dx_ref[...] = dh

        @pl.when((b == 0) & (c == 0))
        def _():
            dw_ref[...] = jnp.zeros_like(dw_ref)

        dw_ref[0:1, :] += dw

    rows = pl.BlockSpec((CHUNK, D_MODEL), lambda b, c: (b * nc + c, 0))
    return pl.pallas_call(
        body, grid=(nseq, nc),
        in_specs=[rows, pl.BlockSpec((1, D_MODEL), lambda b, c: (0, 0)), rows, rows],
        out_specs=[pl.BlockSpec((None, CHUNK, D_MODEL), lambda b, c: (b, jnp.maximum(c - 1, 0), 0)),
                   pl.BlockSpec((None, CHUNK, D_MODEL), lambda b, c: (b, 0, 0)),
                   pl.BlockSpec((8, D_MODEL), lambda b, c: (0, 0))],
        out_shape=[jax.ShapeDtypeStruct((nseq, Tp - CHUNK, D_MODEL), F32), jax.ShapeDtypeStruct((nseq, CHUNK, D_MODEL), F32),
                   jax.ShapeDtypeStruct((8, D_MODEL), F32)],
        name=name, compiler_params=_cparams(("arbitrary", "arbitrary")),
    )(h, w, dn, dres)


def _gu_swiglu(n, w_gu, name):
    R = n.shape[0]
    G, _, ng = w_gu.shape

    def body(n_ref, w_ref, gu_ref, a_ref):
        x = n_ref[...]
        for r in range(G):
            gu_ref[:, ng * r:ng * (r + 1)] = _dg(x, w_ref[r], 1, 0).astype(gu_ref.dtype)
        a_ref[...] = _swiglu_fn(gu_ref[...]).astype(a_ref.dtype)

    return _rows_call(body, rows=R, tr=_pick(R, (256, 128)), ins=[n, ("full", w_gu)],
                      outs=[jax.ShapeDtypeStruct((R, 2 * D_FF), BF16), jax.ShapeDtypeStruct((R, D_FF), BF16)], name=name)


def _d_swiglu(dout, w_down, gu, alpha, name):
    R = gu.shape[0]

    def body(do_ref, w_ref, gu_ref, o_ref):
        da = _dg(do_ref[...], w_ref[...], 1, 1) * alpha
        _, vjp = jax.vjp(_swiglu_fn, gu_ref[...].astype(F32))
        (dgu,) = vjp(da)
        o_ref[...] = dgu.astype(o_ref.dtype)

    return _rows_call(body, rows=R, tr=_pick(R, (256, 128)), ins=[dout, ("full", w_down), gu],
                      outs=[jax.ShapeDtypeStruct(gu.shape, BF16)], name=name)[0]


def _merge_fwd(pa, pb, gates, name):
    def body(pa_ref, pb_ref, g_ref, o_ref):
        o_ref[...] = _merge_fn(pa_ref[...], pb_ref[...], g_ref[...].astype(F32)).astype(o_ref.dtype)

    R = pa.shape[0]
    return _rows_call(body, rows=R, tr=_pick(R, (256, 128)), ins=[pa, pb, gates],
                      outs=[jax.ShapeDtypeStruct(pa.shape, BF16)], name=name)[0]


def _merge_bwd(pa, pb, gates, dm, name):
    def body(pa_ref, pb_ref, g_ref, dm_ref, dpa_ref, dpb_ref, dg_ref):
        _, vjp = jax.vjp(_merge_fn, pa_ref[...], pb_ref[...], g_ref[...].astype(F32))
        dpa, dpb, dg = vjp(dm_ref[...].astype(F32))
        dpa_ref[...] = dpa.astype(dpa_ref.dtype)
        dpb_ref[...] = dpb.astype(dpb_ref.dtype)
        dg_ref[...] = dg.astype(dg_ref.dtype)

    R = pa.shape[0]
    return _rows_call(body, rows=R, tr=_pick(R, (256, 128)), ins=[pa, pb, gates, dm],
                      outs=[jax.ShapeDtypeStruct(pa.shape, BF16), jax.ShapeDtypeStruct(pa.shape, BF16),
                            jax.ShapeDtypeStruct(gates.shape, BF16)], name=name)


def _loss_head(h3, w, target, nseq, name):
    Tp = h3.shape[0] // nseq
    nc = Tp // CHUNK

    def fn(h, w_, t, valid):
        y = _rms_fn(h, w_)
        e = (y - t) * valid
        return 0.5 * jnp.sum(jnp.mean(e * e, axis=-1, keepdims=True))

    def body(h_ref, w_ref, t_ref, loss_ref, dh_ref, dw_ref):
        b, c = pl.program_id(0), pl.program_id(1)
        valid = (c >= 1).astype(F32)
        t = t_ref[...]
        loss, vjp = jax.vjp(lambda h, w_: fn(h, w_, t, valid), h_ref[...], w_ref[...])
        dh, dw = vjp(jnp.ones((), F32))
        dh_ref[...] = dh

        @pl.when((b == 0) & (c == 0))
        def _():
            loss_ref[...] = jnp.zeros_like(loss_ref)
            dw_ref[...] = jnp.zeros_like(dw_ref)

        loss_ref[...] += jnp.full(loss_ref.shape, loss, F32)
        dw_ref[0:1, :] += dw

    return pl.pallas_call(
        body, grid=(nseq, nc),
        in_specs=[pl.BlockSpec((CHUNK, D_MODEL), lambda b, c: (b * nc + c, 0)),
                  pl.BlockSpec((1, D_MODEL), lambda b, c: (0, 0)),
                  pl.BlockSpec((None, CHUNK, D_MODEL), lambda b, c: (b, jnp.maximum(c - 1, 0), 0))],
        out_specs=[pl.BlockSpec((8, 128), lambda b, c: (0, 0)),
                   pl.BlockSpec((CHUNK, D_MODEL), lambda b, c: (b * nc + c, 0)),
                   pl.BlockSpec((8, D_MODEL), lambda b, c: (0, 0))],
        out_shape=[jax.ShapeDtypeStruct((8, 128), F32), jax.ShapeDtypeStruct(h3.shape, F32),
                   jax.ShapeDtypeStruct((8, D_MODEL), F32)],
        name=name, compiler_params=_cparams(("arbitrary", "arbitrary")),
    )(h3, w, target)


CONV_TILE = 512
CONV_HALO = 8


def _conv_fwd(xbc, w, b, pad, name):
    B, Tp, C = xbc.shape
    nch = Tp // CHUNK

    def body(x_ref, w_ref, b_ref, o_ref, xp):
        xp[0:CONV_HALO, :] = jnp.zeros((CONV_HALO, CONV_TILE), F32)
        xp[CONV_HALO:, :] = x_ref[...]
        for c in range(nch):
            acc = jnp.zeros((CHUNK, CONV_TILE), F32) + b_ref[...]
            for k in range(SSD_CONV):
                acc = acc + w_ref[k:k + 1, :] * xp[pl.ds(CONV_HALO + CHUNK * c - (SSD_CONV - 1) + k, CHUNK), :]
            row = CHUNK * c + lax.broadcasted_iota(jnp.int32, (CHUNK, 1), 0)
            o_ref[pl.ds(CHUNK * c, CHUNK), :] = jnp.where(row >= pad, _silu(acc), 0.0)

    return pl.pallas_call(
        body, grid=(B, C // CONV_TILE),
        in_specs=[pl.BlockSpec((None, Tp, CONV_TILE), lambda i, j: (i, 0, j)),
                  pl.BlockSpec((SSD_CONV, CONV_TILE), lambda i, j: (0, j)),
                  pl.BlockSpec((1, CONV_TILE), lambda i, j: (0, j))],
        out_specs=pl.BlockSpec((None, Tp, CONV_TILE), lambda i, j: (i, 0, j)),
        out_shape=jax.ShapeDtypeStruct(xbc.shape, F32),
        scratch_shapes=[pltpu.VMEM((Tp + CONV_HALO, CONV_TILE), F32)],
        name=name, compiler_params=_cparams(("arbitrary", "arbitrary")),
    )(xbc, w, b)


def _conv_bwd(xbc, w, b, dact, pad, name):
    B, Tp, C = xbc.shape
    nch = Tp // CHUNK

    def body(x_ref, w_ref, b_ref, da_ref, dx_ref, dw_ref, db_ref, xp, dp):
        bi = pl.program_id(1)
        xp[0:CONV_HALO, :] = jnp.zeros((CONV_HALO, CONV_TILE), F32)
        xp[CONV_HALO:, :] = x_ref[...]
        dp[pl.ds(Tp, CONV_HALO), :] = jnp.zeros((CONV_HALO, CONV_TILE), F32)
        dws = [jnp.zeros((1, CONV_TILE), F32) for _ in range(SSD_CONV)]
        dbs = jnp.zeros((1, CONV_TILE), F32)
        for c in range(nch):
            xs = [xp[pl.ds(CONV_HALO + CHUNK * c - (SSD_CONV - 1) + k, CHUNK), :] for k in range(SSD_CONV)]
            acc = jnp.zeros((CHUNK, CONV_TILE), F32) + b_ref[...]
            for k in range(SSD_CONV):
                acc = acc + w_ref[k:k + 1, :] * xs[k]
            row = CHUNK * c + lax.broadcasted_iota(jnp.int32, (CHUNK, 1), 0)
            sg = jax.nn.sigmoid(acc)
            dpre = jnp.where(row >= pad, da_ref[pl.ds(CHUNK * c, CHUNK), :] * (sg * (1.0 + acc * (1.0 - sg))), 0.0)
            dp[pl.ds(CHUNK * c, CHUNK), :] = dpre
            dbs = dbs + jnp.sum(dpre, axis=0, keepdims=True)
            for k in range(SSD_CONV):
                dws[k] = dws[k] + jnp.sum(dpre * xs[k], axis=0, keepdims=True)
        for c in range(nch):
            acc = jnp.zeros((CHUNK, CONV_TILE), F32)
            for k in range(SSD_CONV):
                acc = acc + w_ref[k:k + 1, :] * dp[pl.ds(CHUNK * c + (SSD_CONV - 1) - k, CHUNK), :]
            dx_ref[pl.ds(CHUNK * c, CHUNK), :] = acc.astype(dx_ref.dtype)

        @pl.when(bi == 0)
        def _():
            dw_ref[...] = jnp.zeros_like(dw_ref)
            db_ref[...] = jnp.zeros_like(db_ref)

        for k in range(SSD_CONV):
            dw_ref[k:k + 1, :] += dws[k]
        db_ref[0:1, :] += dbs

    return pl.pallas_call(
        body, grid=(C // CONV_TILE, B),
        in_specs=[pl.BlockSpec((None, Tp, CONV_TILE), lambda j, i: (i, 0, j)),
                  pl.BlockSpec((SSD_CONV, CONV_TILE), lambda j, i: (0, j)),
                  pl.BlockSpec((1, CONV_TILE), lambda j, i: (0, j)),
                  pl.BlockSpec((None, Tp, CONV_TILE), lambda j, i: (i, 0, j))],
        out_specs=[pl.BlockSpec((None, Tp, CONV_TILE), lambda j, i: (i, 0, j)),
                   pl.BlockSpec((8, CONV_TILE), lambda j, i: (0, j)),
                   pl.BlockSpec((8, CONV_TILE), lambda j, i: (0, j))],
        out_shape=[jax.ShapeDtypeStruct(xbc.shape, BF16), jax.ShapeDtypeStruct((8, C), F32),
                   jax.ShapeDtypeStruct((8, C), F32)],
        scratch_shapes=[pltpu.VMEM((Tp + CONV_HALO, CONV_TILE), F32), pltpu.VMEM((Tp + CONV_HALO, CONV_TILE), F32)],
        name=name, compiler_params=_cparams(("arbitrary", "arbitrary")),
    )(xbc, w, b, dact)


def _ssd_chunk(xs, bm, cm, dtr, z, state, dt_bias, a_log, dskip, norm_w, valid):
    Q = xs.shape[0]
    lane = lax.broadcasted_iota(jnp.int32, (1, 128), 1)
    dt = jnp.where(lane < SSD_HEADS, _softplus(dtr + dt_bias), 0.0) * valid
    a = dt * (-jnp.exp(a_log))
    tril = _tril(Q)
    cs = _cumsum_rows(a)
    cs_t = cs.T
    cs_end = _row_of(cs, Q - 1)
    low = lane < SSD_HEAD_DIM
    low_rows = lax.broadcasted_iota(jnp.int32, (128, 1), 0) < SSD_HEAD_DIM
    ys, new_state = [], []
    for g in range(SSD_GROUPS):
        bg = bm[:, 128 * g:128 * (g + 1)]
        cg = cm[:, 128 * g:128 * (g + 1)]
        cb = _mm_nt(cg, bg)
        for pr in range(2):
            p = 2 * g + pr
            h0, h1 = 2 * p, 2 * p + 1
            xp = xs[:, 128 * p:128 * (p + 1)]
            c0, c1 = _col_of(cs, h0), _col_of(cs, h1)
            e0, e1 = _col_of(cs_end, h0), _col_of(cs_end, h1)
            xd = xp * jnp.where(low, _col_of(dt, h0), _col_of(dt, h1))
            l0 = jnp.exp(jnp.where(tril, c0 - _row_of(cs_t, h0), -1e30))
            l1 = jnp.exp(jnp.where(tril, c1 - _row_of(cs_t, h1), -1e30))
            y_diag = jnp.where(low, _mm(cb * l0, xd), _mm(cb * l1, xd))
            to_end = jnp.where(low, jnp.exp(e0 - c0), jnp.exp(e1 - c1))
            sp = state[128 * p:128 * (p + 1), :]
            y_off = _mm_nt(cg, sp) * jnp.where(low, jnp.exp(c0), jnp.exp(c1))
            new_state.append(sp * jnp.where(low_rows, jnp.exp(e0), jnp.exp(e1)) + _mm_tn(xd * to_end, bg))
            ys.append(y_diag + y_off + xp * jnp.where(low, _col_of(dskip, h0), _col_of(dskip, h1)))
    y = jnp.concatenate(ys, axis=1) * _silu(z)
    gw = SSD_INNER // SSD_GROUPS
    outs = []
    for g in range(SSD_GROUPS):
        blk = y[:, gw * g:gw * (g + 1)]
        outs.append(blk * lax.rsqrt(jnp.mean(blk * blk, axis=-1, keepdims=True) + EPS))
    return jnp.concatenate(outs, axis=1) * norm_w, jnp.concatenate(new_state, axis=0)


def _valid_rows(c, pad):
    row = c * CHUNK + lax.broadcasted_iota(jnp.int32, (CHUNK, 1), 0)
    return (row >= pad).astype(F32)


def _ssd_fwd(xact, dtr, z, dt_bias, a_log, dskip, norm_w, pad, name):
    B, Tp, _ = xact.shape
    nc = Tp // CHUNK

    def body(xs_ref, bm_ref, cm_ref, dt_ref, z_ref, db_ref, al_ref, ds_ref, nw_ref, y_ref, save_ref, st):
        c = pl.program_id(1)

        @pl.when(c == 0)
        def _():
            st[...] = jnp.zeros_like(st)

        s0 = st[...]
        save_ref[...] = s0
        y, s1 = _ssd_chunk(xs_ref[...], bm_ref[...], cm_ref[...], dt_ref[...], z_ref[...].astype(F32), s0, db_ref[...],
                           al_ref[...], ds_ref[...], nw_ref[...], _valid_rows(c, pad))
        y_ref[...] = y.astype(y_ref.dtype)
        st[...] = s1

    row = lambda w, off=0: pl.BlockSpec((None, CHUNK, w), lambda b, c: (b, c, off))
    par = lambda w: pl.BlockSpec((1, w), lambda b, c: (0, 0))
    return pl.pallas_call(
        body, grid=(B, nc),
        in_specs=[row(1024, 0), row(512, 2), row(512, 3), row(128), row(1024), par(128), par(128), par(128), par(1024)],
        out_specs=[row(1024), pl.BlockSpec((None, None, 1024, 128), lambda b, c: (b, c, 0, 0))],
        out_shape=[jax.ShapeDtypeStruct((B, Tp, SSD_INNER), BF16), jax.ShapeDtypeStruct((B, nc, 1024, 128), F32)],
        scratch_shapes=[pltpu.VMEM((1024, 128), F32)],
        name=name, compiler_params=_cparams(("arbitrary", "arbitrary")),
    )(xact, xact, xact, dtr, z, dt_bias, a_log, dskip, norm_w)


def _ssd_bwd(xact, dtr, z, dt_bias, a_log, dskip, norm_w, saved, dy, pad, name, after=None):
    B, Tp, _ = xact.shape
    nc = Tp // CHUNK

    def body(xs_ref, bm_ref, cm_ref, dt_ref, z_ref, db_ref, al_ref, ds_ref, nw_ref, sv_ref, dy_ref,
             dx_ref, ddt_ref, dz_ref, dpar_ref, dnw_ref, dst):
        b, i = pl.program_id(0), pl.program_id(1)
        c = nc - 1 - i

        @pl.when(i == 0)
        def _():
            dst[...] = jnp.zeros_like(dst)

        valid = _valid_rows(c, pad)
        fn = lambda *a: _ssd_chunk(*a, valid)
        _, vjp = jax.vjp(fn, xs_ref[...], bm_ref[...], cm_ref[...], dt_ref[...], z_ref[...].astype(F32), sv_ref[...],
                         db_ref[...], al_ref[...], ds_ref[...], nw_ref[...])
        dxs, dbm, dcm, ddt, dz, dstate, ddb, dal, dds, dnw = vjp((dy_ref[...].astype(F32), dst[...]))
        dx_ref[:, 0:1024] = dxs
        dx_ref[:, 1024:1536] = dbm
        dx_ref[:, 1536:2048] = dcm
        ddt_ref[...] = ddt
        dz_ref[...] = dz.astype(dz_ref.dtype)
        dst[...] = dstate

        @pl.when((b == 0) & (i == 0))
        def _():
            dpar_ref[...] = jnp.zeros_like(dpar_ref)
            dnw_ref[...] = jnp.zeros_like(dnw_ref)

        dpar_ref[0:1, :] += ddb
        dpar_ref[1:2, :] += dal
        dpar_ref[2:3, :] += dds
        dnw_ref[0:1, :] += dnw

    row = lambda w, off=0: pl.BlockSpec((None, CHUNK, w), lambda b, i: (b, nc - 1 - i, off))
    par = lambda w: pl.BlockSpec((1, w), lambda b, i: (0, 0))
    acc = lambda w: pl.BlockSpec((8, w), lambda b, i: (0, 0))
    in_specs = [row(1024, 0), row(512, 2), row(512, 3), row(128), row(1024), par(128), par(128), par(128), par(1024),
                pl.BlockSpec((None, None, 1024, 128), lambda b, i: (b, nc - 1 - i, 0, 0)), row(1024)]
    args = [xact, xact, xact, dtr, z, dt_bias, a_log, dskip, norm_w, saved, dy]
    if after is not None:
        body = _skip_ref(body, len(args))
        args.append(_deps(after))
        in_specs.append(_dep_spec(args[-1]))
    outs = pl.pallas_call(
        body, grid=(B, nc), in_specs=in_specs,
        out_specs=[row(2048), row(128), row(1024), acc(128), acc(1024)],
        out_shape=[jax.ShapeDtypeStruct((B, Tp, 2048), F32), jax.ShapeDtypeStruct((B, Tp, 128), F32),
                   jax.ShapeDtypeStruct((B, Tp, 1024), BF16), jax.ShapeDtypeStruct((8, 128), F32),
                   jax.ShapeDtypeStruct((8, 1024), F32)],
        scratch_shapes=[pltpu.VMEM((1024, 128), F32)],
        name=name, compiler_params=_cparams(("arbitrary", "arbitrary")),
    )(*args)
    return outs


def _hg_chunk(qr, fr, ir, gr, state_t, p0, p1, norm_w, valid):
    Q = qr.shape[0]
    lb = jax.nn.sigmoid(p0 - p1)
    f = lb + (1.0 - lb) * jax.nn.sigmoid(fr)
    k = 1.0 - f
    q = _silu(qr)
    v = ir * valid
    cum = _cumsum_rows(jnp.log(f))
    cum_end = _row_of(cum, Q - 1)
    o_inter = _mm_nt(q * jnp.exp(cum), state_t)
    nblk = Q // HG_SUB
    row = lax.broadcasted_iota(jnp.int32, (Q, 1), 0)
    ri = lax.broadcasted_iota(jnp.int32, (Q, Q), 0)
    ci = lax.broadcasted_iota(jnp.int32, (Q, Q), 1)
    mids = jnp.concatenate([jnp.broadcast_to(_row_of(cum, HG_SUB * i + HG_SUB // 2 - 1), (HG_SUB, cum.shape[1]))
                            for i in range(nblk)], axis=0)
    sh = HG_SUB.bit_length() - 1
    same = (jnp.right_shift(ri, sh) == jnp.right_shift(ci, sh)) & (ri >= ci)
    att = jnp.where(same, _mm_nt(q * jnp.exp(cum - mids), k * jnp.exp(mids - cum)), 0.0)
    for i in range(1, nblk):
        lo = HG_SUB * i
        start = _row_of(cum, lo - 1)
        qa = q * jnp.exp(jnp.where((row >= lo) & (row < lo + HG_SUB), cum - start, -1e30))
        ka = k * jnp.exp(jnp.where(row < lo, start - cum, -1e30))
        att = att + _mm_nt(qa, ka)
    o = o_inter + _mm(att, v)
    new_state_t = state_t * jnp.exp(cum_end) + _mm_tn(v, k * jnp.exp(cum_end - cum))
    o = o * lax.rsqrt(jnp.mean(o * o, axis=-1, keepdims=True) + EPS) * norm_w
    return o * _silu(gr), new_state_t


HG_PER_STEP = 4
HG_COLS = 4 * 128


def _hg_fwd(qfig, lbh, nwh, pad, name):
    B, Tp, _ = qfig.shape
    nc = Tp // CHUNK
    hp = HG_PER_STEP

    def body(x_ref, lb_ref, nw_ref, y_ref, save_ref, st):
        c = pl.program_id(1)

        @pl.when(c == 0)
        def _():
            st[...] = jnp.zeros_like(st)

        valid = _valid_rows(c, pad)
        for j in range(hp):
            for b in range(B):
                s0 = st[j, b]
                save_ref[j, b] = s0
                col = lambda k: x_ref[b, :, HG_COLS * j + 128 * k:HG_COLS * j + 128 * (k + 1)]
                y, s1 = _hg_chunk(col(0), col(1), col(2), col(3), s0, lb_ref[j, 0:1, :], lb_ref[j, 1:2, :], nw_ref[j], valid)
                y_ref[b, :, 128 * j:128 * (j + 1)] = y.astype(y_ref.dtype)
                st[j, b] = s1

    return pl.pallas_call(
        body, grid=(HG_HEADS // hp, nc),
        in_specs=[pl.BlockSpec((B, CHUNK, HG_COLS * hp), lambda h, c: (0, c, h)),
                  pl.BlockSpec((hp, 2, 128), lambda h, c: (h, 0, 0)),
                  pl.BlockSpec((hp, 1, 128), lambda h, c: (h, 0, 0))],
        out_specs=[pl.BlockSpec((B, CHUNK, 128 * hp), lambda h, c: (0, c, h)),
                   pl.BlockSpec((hp, B, None, 128, 128), lambda h, c: (h, 0, c, 0, 0))],
        out_shape=[jax.ShapeDtypeStruct((B, Tp, 1024), BF16), jax.ShapeDtypeStruct((HG_HEADS, B, nc, 128, 128), F32)],
        scratch_shapes=[pltpu.VMEM((hp, B, 128, 128), F32)],
        name=name, compiler_params=_cparams(("arbitrary", "arbitrary")),
    )(qfig, lbh, nwh)


def _hg_bwd(qfig, lbh, nwh, saved, dy, pad, name, after=None):
    B, Tp, _ = qfig.shape
    nc = Tp // CHUNK
    hp = HG_PER_STEP

    def body(x_ref, lb_ref, nw_ref, sv_ref, dy_ref, dx_ref, dlb_ref, dnw_ref, dst):
        i = pl.program_id(1)
        c = nc - 1 - i

        @pl.when(i == 0)
        def _():
            dst[...] = jnp.zeros_like(dst)
            dlb_ref[...] = jnp.zeros_like(dlb_ref)
            dnw_ref[...] = jnp.zeros_like(dnw_ref)

        valid = _valid_rows(c, pad)
        fn = lambda *a: _hg_chunk(*a, valid)
        for j in range(hp):
            for b in range(B):
                col = lambda k: x_ref[b, :, HG_COLS * j + 128 * k:HG_COLS * j + 128 * (k + 1)]
                _, vjp = jax.vjp(fn, col(0), col(1), col(2), col(3), sv_ref[j, b], lb_ref[j, 0:1, :], lb_ref[j, 1:2, :], nw_ref[j])
                d4 = vjp((dy_ref[b, :, 128 * j:128 * (j + 1)].astype(F32), dst[j, b]))
                for k in range(4):
                    dx_ref[b, :, HG_COLS * j + 128 * k:HG_COLS * j + 128 * (k + 1)] = d4[k].astype(dx_ref.dtype)
                dst[j, b] = d4[4]
                dlb_ref[j, 0:1, :] += d4[5]
                dlb_ref[j, 1:2, :] += d4[6]
                dnw_ref[j, 0:1, :] += d4[7]

    acc = pl.BlockSpec((hp, 8, 128), lambda h, i: (h, 0, 0))
    in_specs = [pl.BlockSpec((B, CHUNK, HG_COLS * hp), lambda h, i: (0, nc - 1 - i, h)),
                pl.BlockSpec((hp, 2, 128), lambda h, i: (h, 0, 0)),
                pl.BlockSpec((hp, 1, 128), lambda h, i: (h, 0, 0)),
                pl.BlockSpec((hp, B, None, 128, 128), lambda h, i: (h, 0, nc - 1 - i, 0, 0)),
                pl.BlockSpec((B, CHUNK, 128 * hp), lambda h, i: (0, nc - 1 - i, h))]
    args = [qfig, lbh, nwh, saved, dy]
    if after is not None:
        body = _skip_ref(body, len(args))
        args.append(_deps(after))
        in_specs.append(_dep_spec(args[-1]))
    return pl.pallas_call(
        body, grid=(HG_HEADS // hp, nc), in_specs=in_specs,
        out_specs=[pl.BlockSpec((B, CHUNK, HG_COLS * hp), lambda h, i: (0, nc - 1 - i, h)), acc, acc],
        out_shape=[jax.ShapeDtypeStruct((B, Tp, 4096), BF16), jax.ShapeDtypeStruct((HG_HEADS, 8, 128), F32),
                   jax.ShapeDtypeStruct((HG_HEADS, 8, 128), F32)],
        scratch_shapes=[pltpu.VMEM((hp, B, 128, 128), F32)],
        name=name, compiler_params=_cparams(("arbitrary", "arbitrary")),
    )(*args)


def _adamw(w, g, m, v, name, after=None):
    R, C = w.shape
    tr = max(t for t in range(8, R + 1, 8) if R % t == 0 and (t * C * 4 <= ADAMW_BLOCK_BYTES or t == 8))

    def body(w_ref, g_ref, m_ref, v_ref, d_ref, mo_ref, vo_ref):
        g_ = g_ref[...]
        m_ = ADAM_B1 * m_ref[...] + (1.0 - ADAM_B1) * g_
        v_ = ADAM_B2 * v_ref[...] + (1.0 - ADAM_B2) * (g_ * g_)
        m_hat = m_ / (1.0 - ADAM_B1 ** ADAM_STEP)
        v_hat = v_ / (1.0 - ADAM_B2 ** ADAM_STEP)
        d_ref[...] = -ADAM_LR * (m_hat / (jnp.sqrt(v_hat) + ADAM_EPS) + ADAM_WD * w_ref[...])
        mo_ref[...] = m_
        vo_ref[...] = v_

    sp = pl.BlockSpec((tr, C), lambda i: (i, 0))
    sh = jax.ShapeDtypeStruct((R, C), F32)
    in_specs, args = [sp] * 4, [w, g, m, v]
    if after is not None:
        body = _skip_ref(body, len(args))
        args.append(_deps(after))
        in_specs.append(_dep_spec(args[-1]))
    return pl.pallas_call(body, grid=(R // tr,), in_specs=in_specs, out_specs=[sp] * 3, out_shape=[sh] * 3,
                          name=name, compiler_params=_cparams(("arbitrary",)))(*args)


def _ffn_fwd(h, norm_w, w_gu, w_down, tag, after_norm=None):
    n = _rms_fwd(h, norm_w, f"{tag}_norm")
    if after_norm is not None:
        after_norm(n)
    gu, a = _gu_swiglu(n, w_gu, f"{tag}_gu")
    out = _matmul(a, w_down, mode="nn", out_dtype=F32, alpha=0.5, res=h, name=f"{tag}_down")
    return out, (n, gu, a)


def _ffn_bwd(h, norm_w, w_gu, w_down, saved, dout, tag, after_dw_down=None, token_seqs=None):
    n, gu, a = saved
    dgu = _d_swiglu(dout, w_down, gu, 0.5, f"{tag}_d_gu")
    dw_down = _matmul(a, dout, mode="tn", out_dtype=F32, alpha=0.5, name=f"{tag}_dw_down")
    dw_gu = _matmul(n, dgu, mode="tn", out_dtype=F32, out_groups=N_CHIPS, name=f"{tag}_dw_gu",
                    after=after_dw_down(dw_down) if after_dw_down else None)
    dn = _matmul(dgu, w_gu, mode="nt", out_dtype=F32, name=f"{tag}_d_norm", after=dw_gu)
    if token_seqs is None:
        dh, dnw = _rms_bwd(h, norm_w, dn, dout, f"{tag}_d_in")
    else:
        dx, dm, dnw = _rms_bwd_tokens(h, norm_w, dn, dout, token_seqs, f"{tag}_d_in")
        dh = (dx, dm)
    return dh, dnw, dw_gu, dw_down


def _split_w_in(w_in_full):
    pts = [0]
    for s in IN_SIZES:
        pts.append(pts[-1] + s)
    sl = lambda i, j: w_in_full[:, pts[i]:pts[j]]
    qfig = sl(3, 7).reshape(D_MODEL, 4, HG_HEADS, 128).transpose(0, 2, 1, 3).reshape(D_MODEL, 4 * D_MODEL)
    return {"z": sl(0, 1), "xbc": sl(1, 2), "dt": jnp.pad(sl(2, 3), ((0, 0), (0, 128 - SSD_HEADS))),
            "qfig": qfig, "gates": sl(7, 9)}


def _local_step(x, target, W):
    B, S, _ = x.shape
    T = N_META + S
    pad = (-T) % CHUNK
    Tp = T + pad
    assert pad + N_META == CHUNK
    R = B * Tp
    meta = jnp.broadcast_to(W["meta_tokens"][None], (B, N_META, D_MODEL))
    h0 = jnp.concatenate([jnp.zeros((B, pad, D_MODEL), F32), meta, x], axis=1).reshape(R, D_MODEL)

    stage = W.get("_stage", lambda name, x: {})
    W = dict(W)
    h1, sv1 = _ffn_fwd(h0, W["ffn1_norm"], W["ffn1_w_gu"], W["ffn1_w_down"], "ffn1", lambda n: W.update(stage("ffn1_norm", n)))
    W.update(stage("ffn1_out", h1))
    um = _rms_fwd(h1, W["mix_norm"], "mix_norm")
    wi = W["w_in"]
    z = _matmul(um, wi["z"], mode="nn", out_dtype=BF16, name="in_z")
    xbc = _matmul(um, wi["xbc"], mode="nn", out_dtype=F32, name="in_xbc")
    dtr = _matmul(um, wi["dt"], mode="nn", out_dtype=F32, name="in_dt")
    qfig = _matmul(um, wi["qfig"], mode="nn", out_dtype=F32, name="in_qfig")
    gates = _matmul(um, wi["gates"], mode="nn", out_dtype=BF16, name="in_gates")

    r3 = lambda t: t.reshape(B, Tp, t.shape[-1])
    lane_pad = lambda t: jnp.pad(t, ((0, 0), (0, 128 - t.shape[1])))
    dt_bias, a_log, dskip = lane_pad(W["ssd_dt_bias"]), lane_pad(W["ssd_a_log"]), lane_pad(W["ssd_d"])
    xact = _conv_fwd(r3(xbc), W["ssd_conv_w"], W["ssd_conv_b"], pad, "conv_fwd")
    ya, ssd_saved = _ssd_fwd(xact, r3(dtr), r3(z), dt_bias, a_log, dskip, W["ssd_norm"], pad, "ssd_fwd")
    lbh = W["hg_lower_bound"].reshape(2, HG_HEADS, 128).transpose(1, 0, 2)
    nwh = W["hg_norm"].reshape(HG_HEADS, 1, 128)
    yb, hg_saved = _hg_fwd(r3(qfig), lbh, nwh, pad, "hg_fwd")
    ya2, yb2 = ya.reshape(R, -1), yb.reshape(R, -1)
    W.update(stage("mixers_out", yb2))
    pa = _matmul(ya2, W["w_branch_a"], mode="nn", out_dtype=F32, name="branch_a")
    pb = _matmul(yb2, W["w_branch_b"], mode="nn", out_dtype=F32, name="branch_b")
    mg = _merge_fwd(pa, pb, gates, "merge")
    h2 = _matmul(mg, W["w_out"], mode="nn", out_dtype=F32, res=h1, name="mix_out")
    h3, sv2 = _ffn_fwd(h2, W["ffn2_norm"], W["ffn2_w_gu"], W["ffn2_w_down"], "ffn2")

    loss, dh3, d_final = _loss_head(h3, W["final_norm"].reshape(1, D_MODEL), target, B, "loss_head")

    G = {"final_norm": d_final[0]}
    dh2, dnw, G["ffn2_w_gu"], G["ffn2_w_down"] = _ffn_bwd(h2, W["ffn2_norm"], W["ffn2_w_gu"], W["ffn2_w_down"], sv2, dh3, "ffn2")
    G["ffn2_norm"] = dnw[0:1]
    dmg = _matmul(dh2, W["w_out"], mode="nt", out_dtype=BF16, name="d_merge")
    G["w_out"] = _matmul(mg, dh2, mode="tn", out_dtype=F32, name="dw_out")
    dpa, dpb, dgates = _merge_bwd(pa, pb, gates, dmg, "merge_bwd")
    dya = _matmul(dpa, W["w_branch_a"], mode="nt", out_dtype=BF16, name="d_ya")
    dyb = _matmul(dpb, W["w_branch_b"], mode="nt", out_dtype=BF16, name="d_yb")
    G["w_branch_a"] = _matmul(ya2, dpa, mode="tn", out_dtype=F32, name="dw_branch_a")
    G["w_branch_b"] = _matmul(yb2, dpb, mode="tn", out_dtype=F32, name="dw_branch_b")

    dxact, ddtr, dz, dpar, dnw = _ssd_bwd(xact, r3(dtr), r3(z), dt_bias, a_log, dskip, W["ssd_norm"], ssd_saved,
                                          r3(dya), pad, "ssd_bwd", after=stage("late_grads", G).get("_after"))
    G["ssd_dt_bias"], G["ssd_a_log"], G["ssd_d"] = dpar[0:1, :SSD_HEADS], dpar[1:2, :SSD_HEADS], dpar[2:3, :SSD_HEADS]
    G["ssd_norm"] = dnw[0:1]
    dxbc, dcw, dcb = _conv_bwd(r3(xbc), W["ssd_conv_w"], W["ssd_conv_b"], dxact, pad, "conv_bwd")
    G["ssd_conv_w"], G["ssd_conv_b"] = dcw[0:SSD_CONV], dcb[0:1]
    dqfig, dlb, dhn = _hg_bwd(r3(qfig), lbh, nwh, hg_saved, r3(dyb), pad, "hg_bwd",
                              after=stage("after_conv_bwd", dcb).get("_after"))
    G["hg_lower_bound"] = dlb[:, 0:2, :].transpose(1, 0, 2).reshape(2, D_MODEL)
    G["hg_norm"] = dhn[:, 0, :].reshape(1, D_MODEL)

    r2 = lambda t: t.reshape(R, t.shape[-1])
    pieces = [("z", r2(dz)), ("xbc", r2(dxbc)), ("dt", r2(ddtr)), ("qfig", r2(dqfig)), ("gates", dgates)]
    dum = _sum_nt([p for _, p in pieces], [wi[nm] for nm, _ in pieces], "d_mix")
    dwi = {nm: _matmul(um, dpiece, mode="tn", out_dtype=F32, name=f"dw_in_{nm}") for nm, dpiece in pieces}
    dw_qfig = dwi["qfig"].reshape(D_MODEL, HG_HEADS, 4, 128).transpose(0, 2, 1, 3).reshape(D_MODEL, 4 * D_MODEL)
    G["w_in"] = jnp.concatenate([dwi["z"], dwi["xbc"], dwi["dt"][:, :SSD_HEADS], dw_qfig, dwi["gates"]], axis=1)
    dh1, dnw = _rms_bwd(h1, W["mix_norm"], dum, dh2, "mix_norm_bwd", after=stage("w_in_grads", dwi).get("_after"))
    G["mix_norm"] = dnw[0:1]
    (dx, dfirst), dnw, G["ffn1_w_gu"], G["ffn1_w_down"] = _ffn_bwd(
        h0, W["ffn1_norm"], W["ffn1_w_gu"], W["ffn1_w_down"], sv1, dh1, "ffn1",
        lambda dw: stage("ffn1_dw_down", dw).get("_after"), token_seqs=B)
    G["ffn1_norm"] = dnw[0:1]
    G["meta_tokens"] = jnp.sum(dfirst[:, pad:CHUNK], axis=0)
    return loss, dx, G


ANY = pl.BlockSpec(memory_space=pl.ANY)


def _place():
    return lax.axis_index("x"), lax.axis_index("y"), lax.axis_index("c")


def _other_chips(x, y):
    return [(1 - x, y), (x, 1 - y), (1 - x, 1 - y)]


def _remote(src, dst, ssem, rsem, dev):
    return pltpu.make_async_remote_copy(src_ref=src, dst_ref=dst, send_sem=ssem, recv_sem=rsem,
                                        device_id=dev, device_id_type=MESH)


def _exchange8(buf, reduce, name):
    n, w = buf.shape

    def body(x_ref, *rest):
        if reduce:
            red_ref, out_ref, ssem, rsem = rest
        else:
            out_ref, ssem, rsem = rest
        x, y, c = _place()
        me = 4 * x + 2 * y + c
        out_ref[me] = x_ref[...]
        copies = []
        for k in range(1, 8):
            px = 1 - x if (k >> 2) & 1 else x
            py = 1 - y if (k >> 1) & 1 else y
            pc = 1 - c if k & 1 else c
            cp = _remote(x_ref, out_ref.at[me], ssem.at[k - 1], rsem.at[k - 1], (px, py, pc))
            cp.start()
            copies.append((cp, 4 * px + 2 * py + pc))
        for k, (cp, peer) in enumerate(copies):
            _remote(x_ref, out_ref.at[peer], ssem.at[k], rsem.at[k], (x, y, c)).wait_recv()
        for cp, _ in copies:
            cp.wait_send()
        if reduce:
            acc = out_ref[0]
            for d in range(1, 8):
                acc = acc + out_ref[d]
            red_ref[...] = acc

    vm = pl.BlockSpec(memory_space=pltpu.VMEM)
    g_shape = jax.ShapeDtypeStruct((8, n, w), F32)
    if reduce:
        out_shape, out_specs, scratch = [jax.ShapeDtypeStruct((n, w), F32)], [vm], [pltpu.VMEM((8, n, w), F32)]
    else:
        out_shape, out_specs, scratch = [g_shape], [vm], []
    return pl.pallas_call(
        body, in_specs=[vm], out_specs=out_specs, out_shape=out_shape,
        scratch_shapes=scratch + [pltpu.SemaphoreType.DMA((7,)), pltpu.SemaphoreType.DMA((7,))], name=name,
    )(buf)[0]


HBM = pltpu.MemorySpace.HBM


def _sequencer(name, collective_id, sems, sent):
    return functools.partial(pl.kernel, mesh=plsc.ScalarSubcoreMesh(axis_name="sequencer", num_cores=1), name=name,
                             scratch_types=sems, compiler_params=pltpu.CompilerParams(collective_id=collective_id),
                             cost_estimate=pl.CostEstimate(flops=0, transcendentals=0, bytes_accessed=2 * sent,
                                                           remote_bytes_transferred=sent))


def _nbytes(arrays):
    return sum(a.size * a.dtype.itemsize for a in arrays)


def _handshake(peers):
    barrier = pltpu.get_barrier_semaphore()
    for peer in peers:
        pl.semaphore_signal(barrier, inc=1, device_id=peer, device_id_type=MESH)
    pl.semaphore_wait(barrier, len(peers))


def _gather_seq(blocks, name, collective_id):
    n = len(blocks)
    half = [s.shape[1] // 2 for s in blocks]
    full = [jax.new_ref(b, memory_space=HBM) for b in blocks]

    @_sequencer(name, collective_id, [pltpu.SemaphoreType.DMA((n, 3))] * 4, _nbytes(blocks) * 3 // 4)
    def launch(ssem, rsem, fssem, frsem):
        x, y, c = _place()
        q = 2 * x + y
        chips = _other_chips(x, y)
        _handshake([(px, py, c) for px, py in chips] + [(x, y, 1 - c)])
        piece = lambda s, qq, cc: full[s].at[qq, pl.ds(cc * half[s], half[s])]
        sends = []
        for j, (px, py) in enumerate(chips):
            for s in range(n):
                cp = _remote(piece(s, q, c), piece(s, q, c), ssem.at[s, j], rsem.at[s, j], (px, py, c))
                cp.start()
                sends.append(cp)
        for j, (px, py) in enumerate(chips):
            for s in range(n):
                got = piece(s, 2 * px + py, c)
                _remote(got, got, ssem.at[s, j], rsem.at[s, j], (px, py, c)).wait_recv()
                cp = _remote(got, got, fssem.at[s, j], frsem.at[s, j], (x, y, 1 - c))
                cp.start()
                sends.append(cp)
        for j, (px, py) in enumerate(chips):
            for s in range(n):
                got = piece(s, 2 * px + py, 1 - c)
                _remote(got, got, fssem.at[s, j], frsem.at[s, j], (x, y, 1 - c)).wait_recv()
        for cp in sends:
            cp.wait_send()

    launch()
    return [r[...] for r in full]


def _pair_swap(parts, name, collective_id):
    n = len(parts)
    half = [p.shape[1] // 2 for p in parts]
    src = [jax.new_ref(p, memory_space=HBM) for p in parts]
    got = [jax.empty_ref(jax.ShapeDtypeStruct((p.shape[0], h, p.shape[2]), p.dtype), memory_space=HBM) for p, h in zip(parts, half)]

    @_sequencer(name, collective_id, [pltpu.SemaphoreType.DMA((n,))] * 2, _nbytes(parts) // 2)
    def launch(ssem, rsem):
        x, y, c = _place()
        _handshake([(x, y, 1 - c)])
        copies = []
        for s in range(n):
            cp = _remote(src[s].at[pl.ds(0, parts[s].shape[0]), pl.ds((1 - c) * half[s], half[s])], got[s], ssem.at[s], rsem.at[s], (x, y, 1 - c))
            cp.start()
            copies.append(cp)
        for cp in copies:
            cp.wait_recv()
        for cp in copies:
            cp.wait_send()

    launch()
    return [g[...] for g in got]


def _to_owners(sums, name, collective_id):
    n = len(sums)
    src = [jax.new_ref(s, memory_space=HBM) for s in sums]
    got = [jax.empty_ref(jax.ShapeDtypeStruct(s.shape, s.dtype), memory_space=HBM) for s in sums]

    @_sequencer(name, collective_id, [pltpu.SemaphoreType.DMA((n, 3))] * 2, _nbytes(sums) * 3 // 4)
    def launch(ssem, rsem):
        x, y, c = _place()
        q = 2 * x + y
        chips = _other_chips(x, y)
        _handshake([(px, py, c) for px, py in chips])
        sends = []
        for j, (px, py) in enumerate(chips):
            for s in range(n):
                cp = _remote(src[s].at[2 * px + py], got[s].at[q], ssem.at[s, j], rsem.at[s, j], (px, py, c))
                cp.start()
                sends.append(cp)
        for j, (px, py) in enumerate(chips):
            for s in range(n):
                slot = got[s].at[2 * px + py]
                _remote(slot, slot, ssem.at[s, j], rsem.at[s, j], (px, py, c)).wait_recv()
        for cp in sends:
            cp.wait_send()

    launch()
    return [g[...] for g in got]


def _pair_join(blocks, name, collective_id):
    n = len(blocks)
    out = [jax.new_ref(b, memory_space=HBM) for b in blocks]

    @_sequencer(name, collective_id, [pltpu.SemaphoreType.DMA((n,))] * 2, _nbytes(blocks) // 2)
    def launch(ssem, rsem):
        x, y, c = _place()
        _handshake([(x, y, 1 - c)])
        sends = []
        for s in range(n):
            h = blocks[s].shape[0] // 2
            mine = out[s].at[pl.ds(c * h, h)]
            cp = _remote(mine, mine, ssem.at[s], rsem.at[s], (x, y, 1 - c))
            cp.start()
            sends.append(cp)
        for s in range(n):
            h = blocks[s].shape[0] // 2
            theirs = out[s].at[pl.ds((1 - c) * h, h)]
            _remote(theirs, theirs, ssem.at[s], rsem.at[s], (x, y, 1 - c)).wait_recv()
        for cp in sends:
            cp.wait_send()

    launch()
    return [o[...] for o in out]


WIRE = BF16


def _row_tile(h):
    return _pick(h, (256, 368, 352, 128, 16))


def _add_pair(part, got, c, name, after=None):
    _, h, w = got.shape
    tr = _row_tile(h)
    nt = h // tr

    def body(c_ref, p_ref, g_ref, o_ref):
        o_ref[...] = (p_ref[...] + g_ref[...].astype(F32)).astype(o_ref.dtype)

    in_specs = [pl.BlockSpec((None, tr, w), lambda q, i, c_ref: (q, c_ref[0] * nt + i, 0)),
                pl.BlockSpec((None, tr, w), lambda q, i, c_ref: (q, i, 0))]
    args = [c.reshape(1).astype(jnp.int32), part, got]
    if after is not None:
        body = _skip_ref(body, len(args))
        args.append(_deps(after))
        in_specs.append(_dep_spec(args[-1]))
    return pl.pallas_call(
        body,
        grid_spec=pltpu.PrefetchScalarGridSpec(
            num_scalar_prefetch=1, grid=(got.shape[0], nt), in_specs=in_specs,
            out_specs=pl.BlockSpec((None, tr, w), lambda q, i, c_ref: (q, i, 0))),
        out_shape=jax.ShapeDtypeStruct(got.shape, WIRE), name=name,
        compiler_params=_cparams(("arbitrary", "arbitrary")),
    )(*args)


def _sum_chips(slots, sums, q, c, name, after=None):
    _, h, w = slots.shape
    tr = _row_tile(h)
    nt = h // tr

    def body(s_ref, mine_ref, a_ref, b_ref, d_ref, o_ref):
        o_ref[...] = ((mine_ref[...].astype(F32) + a_ref[...].astype(F32)) + b_ref[...].astype(F32)) + d_ref[...].astype(F32)

    slot = lambda k: pl.BlockSpec((None, tr, w), lambda i, s_ref: (s_ref[1 + k], i, 0))
    scalars = jnp.stack([c, q, (q + 1) % N_CHIPS, (q + 2) % N_CHIPS, (q + 3) % N_CHIPS]).astype(jnp.int32)
    in_specs, args = [slot(0), slot(1), slot(2), slot(3)], [scalars, sums, slots, slots, slots]
    if after is not None:
        body = _skip_ref(body, len(args))
        args.append(_deps(after))
        in_specs.append(_dep_spec(args[-1]))
    return pl.pallas_call(
        body,
        grid_spec=pltpu.PrefetchScalarGridSpec(
            num_scalar_prefetch=1, grid=(nt,), in_specs=in_specs,
            out_specs=pl.BlockSpec((tr, w), lambda i, s_ref: (s_ref[0] * nt + i, 0))),
        out_shape=jax.ShapeDtypeStruct((2 * h, w), F32), name=name,
        compiler_params=_cparams(("arbitrary",)),
    )(*args)


class _Reduce:
    def __init__(self, parts, q, c, tag, first_id, regions=None):
        self.parts, self.q, self.c, self.tag, self.first_id, self.regions = parts, q, c, tag, first_id, regions
        self.got = _pair_swap(parts, f"{tag}_pair_swap", first_id)

    def to_owners(self, after=None):
        self.sums = [_add_pair(p, g, self.c, f"{self.tag}_pair_add{i}", after)
                     for i, (p, g) in enumerate(zip(self.parts, self.got))]
        if self.regions is not None:
            self.sums = self.regions(self.sums)
        self.slots = _to_owners(self.sums, f"{self.tag}_to_owners", self.first_id + 1)
        return self.sums

    def join(self, after=None):
        blocks = [_sum_chips(sl, sm, self.q, self.c, f"{self.tag}_sum_chips{i}", after)
                  for i, (sl, sm) in enumerate(zip(self.slots, self.sums))]
        self.out = _pair_join(blocks, f"{self.tag}_pair_join", self.first_id + 2)
        return blocks


WEIGHTS = ("meta_tokens", "ffn1_norm", "ffn1_w_gu", "ffn1_w_down", "mix_norm", "w_in", "ssd_conv_w", "ssd_conv_b",
           "ssd_dt_bias", "ssd_a_log", "ssd_d", "ssd_norm", "hg_lower_bound", "hg_norm", "w_branch_a", "w_branch_b",
           "w_out", "ffn2_norm", "ffn2_w_gu", "ffn2_w_down", "final_norm")
BIG = ("ffn1_w_gu", "ffn1_w_down", "w_in", "w_branch_a", "w_branch_b", "w_out", "ffn2_w_gu", "ffn2_w_down")
ROW_SHARDED = ("ffn1_w_down", "ffn2_w_down", "w_branch_a", "w_branch_b", "w_out")
SMALL = tuple(n for n in WEIGHTS if n not in BIG)
SMALL_ROWS = 24


def _rows1024(a):
    flat = a.reshape(-1)
    n = -(-flat.shape[0] // 1024) * 1024
    return jnp.pad(flat, (0, n - flat.shape[0])).reshape(-1, 1024)


def _pack_small(d):
    rows = jnp.concatenate([_rows1024(d[n]) for n in SMALL], axis=0)
    return jnp.pad(rows, ((0, SMALL_ROWS - rows.shape[0]), (0, 0)))


def _unpack_small(packed, like):
    out, r = {}, 0
    for n in SMALL:
        size = like[n].size
        nr = -(-size // 1024)
        out[n] = packed[r:r + nr].reshape(-1)[:size].reshape(like[n].shape)
        r += nr
    return out


def kernel(x, meta_tokens, ffn1_norm, ffn1_w_gu, ffn1_w_down, mix_norm, w_in, ssd_conv_w, ssd_conv_b, ssd_dt_bias, ssd_a_log, ssd_d, ssd_norm, hg_lower_bound, hg_norm, w_branch_a, w_branch_b, w_out, ffn2_norm, ffn2_w_gu, ffn2_w_down, final_norm, loss_target, m_meta_tokens, m_ffn1_norm, m_ffn1_w_gu, m_ffn1_w_down, m_mix_norm, m_w_in, m_ssd_conv_w, m_ssd_conv_b, m_ssd_dt_bias, m_ssd_a_log, m_ssd_d, m_ssd_norm, m_hg_lower_bound, m_hg_norm, m_w_branch_a, m_w_branch_b, m_w_out, m_ffn2_norm, m_ffn2_w_gu, m_ffn2_w_down, m_final_norm, v_meta_tokens, v_ffn1_norm, v_ffn1_w_gu, v_ffn1_w_down, v_mix_norm, v_w_in, v_ssd_conv_w, v_ssd_conv_b, v_ssd_dt_bias, v_ssd_a_log, v_ssd_d, v_ssd_norm, v_hg_lower_bound, v_hg_norm, v_w_branch_a, v_w_branch_b, v_w_out, v_ffn2_norm, v_ffn2_w_gu, v_ffn2_w_down, v_final_norm):
    P = dict(zip(WEIGHTS, (meta_tokens, ffn1_norm, ffn1_w_gu, ffn1_w_down, mix_norm, w_in, ssd_conv_w, ssd_conv_b, ssd_dt_bias, ssd_a_log, ssd_d, ssd_norm, hg_lower_bound, hg_norm, w_branch_a, w_branch_b, w_out, ffn2_norm, ffn2_w_gu, ffn2_w_down, final_norm)))
    M = dict(zip(WEIGHTS, (m_meta_tokens, m_ffn1_norm, m_ffn1_w_gu, m_ffn1_w_down, m_mix_norm, m_w_in, m_ssd_conv_w, m_ssd_conv_b, m_ssd_dt_bias, m_ssd_a_log, m_ssd_d, m_ssd_norm, m_hg_lower_bound, m_hg_norm, m_w_branch_a, m_w_branch_b, m_w_out, m_ffn2_norm, m_ffn2_w_gu, m_ffn2_w_down, m_final_norm)))
    V = dict(zip(WEIGHTS, (v_meta_tokens, v_ffn1_norm, v_ffn1_w_gu, v_ffn1_w_down, v_mix_norm, v_w_in, v_ssd_conv_w, v_ssd_conv_b, v_ssd_dt_bias, v_ssd_a_log, v_ssd_d, v_ssd_norm, v_hg_lower_bound, v_hg_norm, v_w_branch_a, v_w_branch_b, v_w_out, v_ffn2_norm, v_ffn2_w_gu, v_ffn2_w_down, v_final_norm)))
    cx, cy, cc = _place()
    q = 2 * cx + cy

    mine = jnp.concatenate([meta_tokens.reshape(4, 1024), ssd_conv_w.reshape(2, 1024), jnp.zeros((2, 1024), F32)], axis=0)
    every = _exchange8(mine, False, "gather_small")
    meta_full = jnp.concatenate([every[2 * k, 0:4].reshape(N_META, 256) for k in range(N_CHIPS)], axis=1)
    conv_w_full = jnp.concatenate([every[2 * k, 4:6].reshape(SSD_CONV, 512) for k in range(N_CHIPS)], axis=1)

    late = ("ffn2_w_down", "w_branch_a", "w_branch_b", "w_out")
    rows = jnp.concatenate([P[n][0] for n in late], axis=0)
    zero = lambda t, dtype=F32: (t[0:1, 0:1] * 0).astype(dtype)

    def in_slot(s, after=None):
        s = s if after is None else s + zero(after)
        return lax.dynamic_update_slice(lax.empty((N_CHIPS,) + s.shape, BF16), s.astype(BF16)[None], (q, 0, 0))

    gu1, down1 = _gather_seq([in_slot(ffn1_w_gu[0]), in_slot(ffn1_w_down[0])], "gather_ffn1", 1)
    W = {n: P[n] for n in SMALL}
    W["meta_tokens"], W["ssd_conv_w"] = meta_full, conv_w_full
    W["ffn1_w_gu"], W["ffn1_w_down"] = gu1, down1.reshape(-1, D_MODEL)
    flying = {}

    def stage(name, t):
        if name == "ffn1_norm":
            flying["w_in"] = _gather_seq([in_slot(w_in[0], t)], "gather_w_in", 2)
            return {}
        if name == "ffn1_out":
            flying["late"] = _gather_seq([in_slot(ffn2_w_gu[0], t), in_slot(rows, t)], "gather_late", 3)
            (w_in_all,) = flying["w_in"]
            w_in_all = w_in_all + zero(t, BF16)
            return {"w_in": _split_w_in(w_in_all.transpose(1, 0, 2).reshape(D_MODEL, -1))}
        if name == "mixers_out":
            gu2, rows_all = flying["late"]
            out, r = {"ffn2_w_gu": gu2}, 0
            for n in late:
                nr = P[n].shape[1]
                out[n] = (rows_all[:, r:r + nr] + zero(t, BF16)).reshape(N_CHIPS * nr, D_MODEL)
                r += nr
            return out
        if name == "late_grads":
            row_parts = jnp.concatenate([t[n].reshape(N_CHIPS, -1, D_MODEL) for n in late], axis=1)
            flying["grad_late"] = _Reduce([t["ffn2_w_gu"], row_parts], q, cc, "grad_late", 4)
            return {"_after": [t["ffn2_w_gu"]] + [t[n] for n in late]}
        if name == "after_conv_bwd":
            return {"_after": flying["grad_late"].to_owners(after=t)}
        if name == "w_in_grads":
            order = ("z", "xbc", "dt", "qfig", "gates")
            blocks = flying["grad_late"].join(after=[t[k] for k in order])

            def regions(sums):
                z, xbc, dt, qfig, gates = [s[0] for s in sums]
                h = z.shape[0]
                qfig = qfig.reshape(h, HG_HEADS, 4, 128).transpose(0, 2, 1, 3).reshape(h, 4 * D_MODEL)
                cols = jnp.concatenate([z, xbc, dt[:, :SSD_HEADS], qfig, gates], axis=1)
                return [cols.reshape(h, N_CHIPS, -1).transpose(1, 0, 2)]

            flying["grad_w_in"] = _Reduce([t[k][None] for k in order], q, cc, "grad_w_in", 7, regions)
            return {"_after": blocks}
        if name == "ffn1_dw_down":
            return {"_after": flying["grad_w_in"].to_owners(after=t)}
        return {}

    W["_stage"] = stage

    loss8, grad_x, G = _local_step(x, loss_target, W)

    small = jnp.concatenate(
        [G["meta_tokens"]] + [_rows1024(G[n]) for n in SMALL if n != "meta_tokens"] + [_rows1024(loss8[0:1, 0:1])], axis=0)
    small = jnp.pad(small, ((0, 40 - small.shape[0]), (0, 0)))
    small = _exchange8(small, True, "reduce_small")
    Gs = {"meta_tokens": small[0:N_META]}
    r = N_META
    for n in SMALL:
        if n == "meta_tokens":
            continue
        nr = -(-G[n].size // 1024)
        Gs[n] = small[r:r + nr].reshape(-1)[:G[n].size].reshape(G[n].shape)
        r += nr
    loss = small[r, 0]
    Gs["meta_tokens"] = lax.dynamic_slice(Gs["meta_tokens"], (0, 256 * q), (N_META, 256))
    Gs["ssd_conv_w"] = lax.dynamic_slice(Gs["ssd_conv_w"], (0, 512 * q), (SSD_CONV, 512))[None]
    Gs = {n: Gs[n].reshape(P[n].shape) for n in SMALL}

    grad_ffn1 = _Reduce([G["ffn1_w_gu"], G["ffn1_w_down"].reshape(N_CHIPS, -1, D_MODEL)], q, cc, "grad_ffn1", 10)
    flying["grad_w_in"].join(after=grad_x)
    going = grad_ffn1.to_owners(after=grad_x)
    g_gu2, g_rows = flying["grad_late"].out
    (g_w_in,) = flying["grad_w_in"].out
    Gb = {"ffn2_w_gu": g_gu2, "w_in": g_w_in}
    r = 0
    for n in late:
        nr = P[n].shape[1]
        Gb[n] = g_rows[r:r + nr]
        r += nr

    grads, delta, new_m, new_v = dict(Gs), {}, {}, {}
    d_s, m_s, v_s = _adamw(_pack_small(P), _pack_small(Gs), _pack_small(M), _pack_small(V), "adamw_small", after=going)
    delta.update(_unpack_small(d_s, P))
    new_m.update(_unpack_small(m_s, P))
    new_v.update(_unpack_small(v_s, P))
    done = [d_s]
    cols = w_in.shape[2]
    to_tiles = lambda a: a.transpose(2, 0, 1).reshape(cols, 8, 128).reshape(cols * 8, 128)
    from_tiles = lambda a: a.reshape(cols, 1, D_MODEL).transpose(1, 2, 0)
    for n in [n for n in BIG if n in Gb]:
        if n == "w_in":
            g_t = to_tiles(Gb[n][None])
            d_, m_, v_ = _adamw(to_tiles(P[n]), g_t, to_tiles(M[n]), to_tiles(V[n]), f"adamw_{n}", after=going)
            grads[n], delta[n], new_m[n], new_v[n] = from_tiles(g_t), from_tiles(d_), from_tiles(m_), from_tiles(v_)
        else:
            d_, m_, v_ = _adamw(P[n][0], Gb[n], M[n][0], V[n][0], f"adamw_{n}", after=going)
            grads[n], delta[n], new_m[n], new_v[n] = Gb[n][None], d_[None], m_[None], v_[None]
        done.append(d_)
    grad_ffn1.join(after=done)
    Gb["ffn1_w_gu"], Gb["ffn1_w_down"] = grad_ffn1.out
    for n in ("ffn1_w_gu", "ffn1_w_down"):
        d_, m_, v_ = _adamw(P[n][0], Gb[n], M[n][0], V[n][0], f"adamw_{n}")
        grads[n], delta[n], new_m[n], new_v[n] = Gb[n][None], d_[None], m_[None], v_[None]
    return (loss, grad_x, *[grads[n] for n in WEIGHTS], *[delta[n] for n in WEIGHTS],
            *[new_m[n] for n in WEIGHTS], *[new_v[n] for n in WEIGHTS])
```

```python
import functools

import jax
import jax.numpy as jnp
from jax import lax
from jax.experimental import pallas as pl
from jax.experimental.pallas import tpu as pltpu
from jax.experimental.pallas import tpu_sc as plsc

F32 = jnp.float32
BF16 = jnp.bfloat16
HIGHEST = lax.Precision.HIGHEST
MESH = pl.DeviceIdType.MESH

D_MODEL = 1024
N_META = 16
EPS = 1e-6
SSD_HEADS = 16
SSD_HEAD_DIM = 64
SSD_INNER = 1024
SSD_GROUPS = 4
SSD_STATE = 128
SSD_CONV = 4
SSD_CONV_CH = 2048
HG_HEADS = 8
HG_SUB = 32
CHUNK = 128
D_FF = 2816
N_CHIPS = 4
IN_SIZES = (1024, 2048, 16, 1024, 1024, 1024, 1024, 1024, 1024)
ADAM_LR = 0.001
ADAM_B1 = 0.9
ADAM_B2 = 0.999
ADAM_EPS = 1e-08
ADAM_WD = 0.01
ADAM_STEP = 10
VMEM_LIMIT = 56 * 1024 * 1024
MATMUL_BLOCK_BYTES = 42 * 1024 * 1024
ADAMW_BLOCK_BYTES = 5 * 512 * 1024


def _cparams(sem=None):
    return pltpu.CompilerParams(dimension_semantics=sem, vmem_limit_bytes=VMEM_LIMIT)


def _pick(n, cands):
    for c in cands:
        if n % c == 0:
            return c
    return n


def _deps(after):
    xs = after if isinstance(after, (list, tuple)) else [after]
    one = lambda x: lax.slice(x, (0,) * x.ndim, (1,) * x.ndim).reshape(1).astype(F32)
    return jnp.concatenate([one(x) for x in xs]).reshape(1, -1)


def _dep_spec(dep):
    return pl.BlockSpec(dep.shape, lambda *_: (0, 0))


def _skip_ref(body, pos):
    return lambda *refs: body(*refs[:pos], *refs[pos + 1:])


def _dg(a, b, ca, cb):
    return lax.dot_general(a.astype(BF16), b.astype(BF16), (((ca,), (cb,)), ((), ())), preferred_element_type=F32)


@jax.custom_vjp
def _mm(a, b):
    return _dg(a, b, 1, 0)


def _mm_fwd(a, b):
    return _dg(a, b, 1, 0), (a, b)


def _mm_bwd(r, g):
    a, b = r
    return _dg(g, b, 1, 1), _dg(a, g, 0, 0)


_mm.defvjp(_mm_fwd, _mm_bwd)


@jax.custom_vjp
def _mm_nt(a, b):
    return _dg(a, b, 1, 1)


def _mm_nt_fwd(a, b):
    return _dg(a, b, 1, 1), (a, b)


def _mm_nt_bwd(r, g):
    a, b = r
    return _dg(g, b, 1, 0), _dg(g, a, 0, 0)


_mm_nt.defvjp(_mm_nt_fwd, _mm_nt_bwd)


@jax.custom_vjp
def _mm_tn(a, b):
    return _dg(a, b, 0, 0)


def _mm_tn_fwd(a, b):
    return _dg(a, b, 0, 0), (a, b)


def _mm_tn_bwd(r, g):
    a, b = r
    return _dg(b, g, 1, 1), _dg(a, g, 1, 0)


_mm_tn.defvjp(_mm_tn_fwd, _mm_tn_bwd)


def _tri_sum(x, lower):
    n = x.shape[0]
    ri = lax.broadcasted_iota(jnp.int32, (n, n), 0)
    ci = lax.broadcasted_iota(jnp.int32, (n, n), 1)
    tri = ((ri >= ci) if lower else (ri <= ci)).astype(BF16)
    x1 = x.astype(BF16)
    r1 = x - x1.astype(F32)
    x2 = r1.astype(BF16)
    x3 = (r1 - x2.astype(F32)).astype(BF16)
    dot = lambda p: lax.dot_general(tri, p, (((1,), (0,)), ((), ())), preferred_element_type=F32)
    return (dot(x3) + dot(x2)) + dot(x1)


@jax.custom_vjp
def _cumsum_rows(x):
    return _tri_sum(x, True)


_cumsum_rows.defvjp(lambda x: (_tri_sum(x, True), None), lambda _, g: (_tri_sum(g, False),))


def _silu(x):
    return x * jax.nn.sigmoid(x)


def _softplus(x):
    return jnp.maximum(x, 0.0) + jnp.log(1.0 + jnp.exp(-jnp.abs(x)))


def _tril(n):
    ri = lax.broadcasted_iota(jnp.int32, (n, n), 0)
    ci = lax.broadcasted_iota(jnp.int32, (n, n), 1)
    return ri >= ci


def _row_of(m, r):
    sub = lax.broadcasted_iota(jnp.int32, (m.shape[0], 1), 0)
    return jnp.sum(jnp.where(sub == r, m, 0.0), axis=0, keepdims=True)


def _col_of(m, c):
    lane = lax.broadcasted_iota(jnp.int32, (1, m.shape[1]), 1)
    return jnp.sum(jnp.where(lane == c, m, 0.0), axis=1, keepdims=True)


def _matmul(a, b, *, mode, out_dtype, name, alpha=1.0, res=None, tm=None, tn=None, out_groups=None, after=None):
    b3 = b.ndim == 3
    if mode == "nn":
        M, K = a.shape
        G = b.shape[0] if b3 else 1
        Ng = b.shape[-1]
        N = G * Ng
    elif mode == "nt":
        M, K = a.shape
        G = b.shape[0] if b3 else 1
        N = b.shape[-2]
        Kg = b.shape[-1]
        assert G * Kg == K
    else:
        K, M = a.shape
        N = b.shape[1]
        G = out_groups or 1
        Ng = N // G
    has_res = res is not None
    split_n = (mode == "nn" and b3) or (mode == "tn" and G > 1)
    per_mn = jnp.dtype(out_dtype).itemsize + (res.dtype.itemsize if has_res else 0)
    fits = [(m_ * n_, m_, n_)
            for m_ in (4352, 2176, 1408, 1088, 1024, 544, 512, 256, 128) if M % m_ == 0
            for n_ in (2816, 2048, 1408, 1024, 512, 256, 128) if (Ng if split_n else N) % n_ == 0
            if 2 * (K * m_ * a.dtype.itemsize + K * n_ * b.dtype.itemsize + m_ * n_ * per_mn) + 4 * m_ * n_ <= MATMUL_BLOCK_BYTES]
    _, tm_fit, tn_fit = max(fits)
    tm, tn = tm or tm_fit, tn or tn_fit
    nm, nn_ = M // tm, N // tn
    assert nm * tm == M and nn_ * tn == N, (name, M, N, K, tm, tn)

    if mode == "nn":
        a_spec = pl.BlockSpec((tm, K), lambda i, j: (i, 0))
        if b3:
            ns = Ng // tn
            b_spec = pl.BlockSpec((None, K, tn), lambda i, j: (j // ns, 0, j % ns))
        else:
            b_spec = pl.BlockSpec((K, tn), lambda i, j: (0, j))
        ca, cb = 1, 0
    elif mode == "nt":
        a_spec = pl.BlockSpec((tm, K), lambda i, j: (i, 0))
        if b3:
            b_spec = pl.BlockSpec((G, tn, Kg), lambda i, j: (0, j, 0))
        else:
            b_spec = pl.BlockSpec((tn, K), lambda i, j: (j, 0))
        ca, cb = 1, 1
    else:
        a_spec = pl.BlockSpec((K, tm), lambda i, j: (0, i))
        b_spec = pl.BlockSpec((K, tn), lambda i, j: (0, j))
        ca, cb = 0, 0
    if mode == "tn" and G > 1:
        ns = Ng // tn
        o_spec = pl.BlockSpec((None, tm, tn), lambda i, j: (j // ns, i, j % ns))
        out_shape = jax.ShapeDtypeStruct((G, M, Ng), out_dtype)
    else:
        o_spec = pl.BlockSpec((tm, tn), lambda i, j: (i, j))
        out_shape = jax.ShapeDtypeStruct((M, N), out_dtype)
    in_specs = [a_spec, b_spec]
    args = [a, b]
    if has_res:
        in_specs.append(pl.BlockSpec((tm, tn), lambda i, j: (i, j)))
        args.append(res)
    if after is not None:
        args.append(_deps(after))
        in_specs.append(_dep_spec(args[-1]))

    def body(*refs):
        a_ref, b_ref, o_ref = refs[0], refs[1], refs[-1]
        if mode == "nt" and b3:
            o = _dg(a_ref[:, 0:Kg], b_ref[0], ca, cb)
            for g in range(1, G):
                o = o + _dg(a_ref[:, g * Kg:(g + 1) * Kg], b_ref[g], ca, cb)
        else:
            o = _dg(a_ref[...], b_ref[...], ca, cb)
        if alpha != 1.0:
            o = o * alpha
        if has_res:
            o = o + refs[2][...]
        o_ref[...] = o.astype(o_ref.dtype)

    return pl.pallas_call(
        body, grid=(nm, nn_), in_specs=in_specs, out_specs=o_spec, out_shape=out_shape, name=name,
        compiler_params=_cparams(("parallel", "parallel")),
    )(*args)


def _sum_nt(xs, ws, name):
    R, N = xs[0].shape[0], ws[0].shape[0]
    n = len(xs)
    per_m = sum(x.shape[1] * x.dtype.itemsize for x in xs)
    per_n = sum(w.shape[1] * w.dtype.itemsize for w in ws)
    fits = [(m_ * n_, m_, n_) for m_ in (1088, 544, 256, 128) if R % m_ == 0 for n_ in (1024, 512, 256, 128) if N % n_ == 0
            if 2 * (m_ * per_m + n_ * per_n + m_ * n_ * 4) + 4 * m_ * n_ <= MATMUL_BLOCK_BYTES]
    _, tm, tn = max(fits)

    def body(*refs):
        o = _dg(refs[0][...], refs[n][...], 1, 1)
        for p in range(1, n):
            o = o + _dg(refs[p][...], refs[n + p][...], 1, 1)
        refs[-1][...] = o

    return pl.pallas_call(
        body, grid=(R // tm, N // tn),
        in_specs=[pl.BlockSpec((tm, x.shape[1]), lambda i, j: (i, 0)) for x in xs]
        + [pl.BlockSpec((tn, w.shape[1]), lambda i, j: (j, 0)) for w in ws],
        out_specs=pl.BlockSpec((tm, tn), lambda i, j: (i, j)), out_shape=jax.ShapeDtypeStruct((R, N), F32), name=name,
        compiler_params=_cparams(("parallel", "parallel")),
    )(*xs, *ws)


def _rms_fn(h, w):
    r = lax.rsqrt(jnp.mean(h * h, axis=-1, keepdims=True) + EPS)
    return h * r * w


def _swiglu_fn(gu):
    g = gu[:, :D_FF].astype(F32)
    u = gu[:, D_FF:].astype(F32)
    return _silu(g) * u


def _merge_fn(pa, pb, gates):
    return jax.nn.sigmoid(gates[:, :D_MODEL]) * pa + jax.nn.sigmoid(gates[:, D_MODEL:]) * pb


def _rows_call(body, *, rows, tr, ins, outs, accs=(), name, after=None):
    n = rows // tr
    assert n * tr == rows
    if after is not None:
        body = _skip_ref(body, len(ins))
        ins = list(ins) + [("full", _deps(after))]

    def spec(x):
        if isinstance(x, tuple):
            shp = x[1].shape
            return pl.BlockSpec(shp, lambda i: (0,) * len(shp))
        return pl.BlockSpec((tr, x.shape[1]), lambda i: (i, 0))

    in_specs = [spec(x) for x in ins]
    args = [x[1] if isinstance(x, tuple) else x for x in ins]
    out_specs = [spec(x) for x in outs] + [pl.BlockSpec(x.shape, lambda i: (0,) * len(x.shape)) for x in accs]
    out_shape = [x[1] if isinstance(x, tuple) else x for x in outs] + list(accs)
    return pl.pallas_call(
        body, grid=(n,), in_specs=in_specs, out_specs=out_specs, out_shape=out_shape, name=name,
        compiler_params=_cparams(("arbitrary",)),
    )(*args)


def _acc_rows(ref, val):
    @pl.when(pl.program_id(0) == 0)
    def _():
        ref[...] = jnp.zeros_like(ref)

    ref[0:1, :] += val


def _rms_fwd(h, w, name):
    def body(h_ref, w_ref, o_ref):
        o_ref[...] = _rms_fn(h_ref[...], w_ref[...]).astype(o_ref.dtype)

    R = h.shape[0]
    return _rows_call(body, rows=R, tr=_pick(R, (256, 128)), ins=[h, ("full", w)],
                      outs=[jax.ShapeDtypeStruct(h.shape, BF16)], name=name)[0]


def _rms_bwd(h, w, dn, dres, name, after=None):
    def body(h_ref, w_ref, dn_ref, dres_ref, dh_ref, dw_ref):
        _, vjp = jax.vjp(_rms_fn, h_ref[...], w_ref[...])
        dh, dw = vjp(dn_ref[...].astype(F32))
        dh_ref[...] = dh + dres_ref[...]
        _acc_rows(dw_ref, dw)

    R = h.shape[0]
    return _rows_call(body, rows=R, tr=_pick(R, (256, 128)), ins=[h, ("full", w), dn, dres],
                      outs=[jax.ShapeDtypeStruct(h.shape, F32)], accs=[jax.ShapeDtypeStruct((8, D_MODEL), F32)], name=name,
                      after=after)


def _rms_bwd_tokens(h, w, dn, dres, nseq, name):
    Tp = h.shape[0] // nseq
    nc = Tp // CHUNK

    def body(h_ref, w_ref, dn_ref, dres_ref, dx_ref, dm_ref, dw_ref):
        b, c = pl.program_id(0), pl.program_id(1)
        _, vjp = jax.vjp(_rms_fn, h_ref[...], w_ref[...])
        dh, dw = vjp(dn_ref[...].astype(F32))
        dh = dh + dres_ref[...]

        @pl.when(c == 0)
        def _():
            dm_ref[...] = dh

        @pl.when(c > 0)
        def _():
            dx_ref[...] = dh

        @pl.when((b == 0) & (c == 0))
        def _():
            dw_ref[...] = jnp.zeros_like(dw_ref)

        dw_ref[0:1, :] += dw

    rows = pl.BlockSpec((CHUNK, D_MODEL), lambda b, c: (b * nc + c, 0))
    return pl.pallas_call(
        body, grid=(nseq, nc),
        in_specs=[rows, pl.BlockSpec((1, D_MODEL), lambda b, c: (0, 0)), rows, rows],
        out_specs=[pl.BlockSpec((None, CHUNK, D_MODEL), lambda b, c: (b, jnp.maximum(c - 1, 0), 0)),
                   pl.BlockSpec((None, CHUNK, D_MODEL), lambda b, c: (b, 0, 0)),
                   pl.BlockSpec((8, D_MODEL), lambda b, c: (0, 0))],
        out_shape=[jax.ShapeDtypeStruct((nseq, Tp - CHUNK, D_MODEL), F32), jax.ShapeDtypeStruct((nseq, CHUNK, D_MODEL), F32),
                   jax.ShapeDtypeStruct((8, D_MODEL), F32)],
        name=name, compiler_params=_cparams(("arbitrary", "arbitrary")),
    )(h, w, dn, dres)


def _gu_swiglu(n, w_gu, name):
    R = n.shape[0]
    G, _, ng = w_gu.shape

    def body(n_ref, w_ref, gu_ref, a_ref):
        x = n_ref[...]
        for r in range(G):
            gu_ref[:, ng * r:ng * (r + 1)] = _dg(x, w_ref[r], 1, 0).astype(gu_ref.dtype)
        a_ref[...] = _swiglu_fn(gu_ref[...]).astype(a_ref.dtype)

    return _rows_call(body, rows=R, tr=_pick(R, (256, 128)), ins=[n, ("full", w_gu)],
                      outs=[jax.ShapeDtypeStruct((R, 2 * D_FF), BF16), jax.ShapeDtypeStruct((R, D_FF), BF16)], name=name)


def _d_swiglu(dout, w_down, gu, alpha, name):
    R = gu.shape[0]

    def body(do_ref, w_ref, gu_ref, o_ref):
        da = _dg(do_ref[...] * alpha, w_ref[...], 1, 1)
        g = gu_ref[:, :D_FF].astype(F32)
        u = gu_ref[:, D_FF:].astype(F32)
        s = jax.nn.sigmoid(g)
        t = g * s
        o_ref[:, :D_FF] = (da * u * (s + t - t * s)).astype(o_ref.dtype)
        o_ref[:, D_FF:] = (da * t).astype(o_ref.dtype)

    return _rows_call(body, rows=R, tr=_pick(R, (256, 128)), ins=[dout, ("full", w_down), gu],
                      outs=[jax.ShapeDtypeStruct(gu.shape, BF16)], name=name)[0]


def _merge_fwd(pa, pb, gates, name):
    def body(pa_ref, pb_ref, g_ref, o_ref):
        o_ref[...] = _merge_fn(pa_ref[...], pb_ref[...], g_ref[...].astype(F32)).astype(o_ref.dtype)

    R = pa.shape[0]
    return _rows_call(body, rows=R, tr=_pick(R, (256, 128)), ins=[pa, pb, gates],
                      outs=[jax.ShapeDtypeStruct(pa.shape, BF16)], name=name)[0]


def _merge_bwd(pa, pb, gates, dm, name):
    def body(pa_ref, pb_ref, g_ref, dm_ref, dpa_ref, dpb_ref, dg_ref):
        _, vjp = jax.vjp(_merge_fn, pa_ref[...], pb_ref[...], g_ref[...].astype(F32))
        dpa, dpb, dg = vjp(dm_ref[...].astype(F32))
        dpa_ref[...] = dpa.astype(dpa_ref.dtype)
        dpb_ref[...] = dpb.astype(dpb_ref.dtype)
        dg_ref[...] = dg.astype(dg_ref.dtype)

    R = pa.shape[0]
    return _rows_call(body, rows=R, tr=_pick(R, (256, 128)), ins=[pa, pb, gates, dm],
                      outs=[jax.ShapeDtypeStruct(pa.shape, BF16), jax.ShapeDtypeStruct(pa.shape, BF16),
                            jax.ShapeDtypeStruct(gates.shape, BF16)], name=name)


def _loss_head(h3, w, target, nseq, name):
    Tp = h3.shape[0] // nseq
    nc = Tp // CHUNK

    def fn(h, w_, t, valid):
        y = _rms_fn(h, w_)
        e = (y - t) * valid
        return 0.5 * jnp.sum(jnp.mean(e * e, axis=-1, keepdims=True))

    def body(h_ref, w_ref, t_ref, loss_ref, dh_ref, dw_ref):
        b, c = pl.program_id(0), pl.program_id(1)
        valid = (c >= 1).astype(F32)
        t = t_ref[...]
        loss, vjp = jax.vjp(lambda h, w_: fn(h, w_, t, valid), h_ref[...], w_ref[...])
        dh, dw = vjp(jnp.ones((), F32))
        dh_ref[...] = dh

        @pl.when((b == 0) & (c == 0))
        def _():
            loss_ref[...] = jnp.zeros_like(loss_ref)
            dw_ref[...] = jnp.zeros_like(dw_ref)

        loss_ref[...] += jnp.full(loss_ref.shape, loss, F32)
        dw_ref[0:1, :] += dw

    return pl.pallas_call(
        body, grid=(nseq, nc),
        in_specs=[pl.BlockSpec((CHUNK, D_MODEL), lambda b, c: (b * nc + c, 0)),
                  pl.BlockSpec((1, D_MODEL), lambda b, c: (0, 0)),
                  pl.BlockSpec((None, CHUNK, D_MODEL), lambda b, c: (b, jnp.maximum(c - 1, 0), 0))],
        out_specs=[pl.BlockSpec((8, 128), lambda b, c: (0, 0)),
                   pl.BlockSpec((CHUNK, D_MODEL), lambda b, c: (b * nc + c, 0)),
                   pl.BlockSpec((8, D_MODEL), lambda b, c: (0, 0))],
        out_shape=[jax.ShapeDtypeStruct((8, 128), F32), jax.ShapeDtypeStruct(h3.shape, F32),
                   jax.ShapeDtypeStruct((8, D_MODEL), F32)],
        name=name, compiler_params=_cparams(("arbitrary", "arbitrary")),
    )(h3, w, target)


CONV_TILE = 512
CONV_HALO = 8


def _conv_fwd(xbc, w, b, pad, name):
    B, Tp, C = xbc.shape
    nch = Tp // CHUNK

    def body(x_ref, w_ref, b_ref, o_ref, xp):
        xp[0:CONV_HALO, :] = jnp.zeros((CONV_HALO, CONV_TILE), F32)
        xp[CONV_HALO:, :] = x_ref[...]
        for c in range(nch):
            acc = jnp.zeros((CHUNK, CONV_TILE), F32) + b_ref[...]
            for k in range(SSD_CONV):
                acc = acc + w_ref[k:k + 1, :] * xp[pl.ds(CONV_HALO + CHUNK * c - (SSD_CONV - 1) + k, CHUNK), :]
            row = CHUNK * c + lax.broadcasted_iota(jnp.int32, (CHUNK, 1), 0)
            o_ref[pl.ds(CHUNK * c, CHUNK), :] = jnp.where(row >= pad, _silu(acc), 0.0)

    return pl.pallas_call(
        body, grid=(B, C // CONV_TILE),
        in_specs=[pl.BlockSpec((None, Tp, CONV_TILE), lambda i, j: (i, 0, j)),
                  pl.BlockSpec((SSD_CONV, CONV_TILE), lambda i, j: (0, j)),
                  pl.BlockSpec((1, CONV_TILE), lambda i, j: (0, j))],
        out_specs=pl.BlockSpec((None, Tp, CONV_TILE), lambda i, j: (i, 0, j)),
        out_shape=jax.ShapeDtypeStruct(xbc.shape, F32),
        scratch_shapes=[pltpu.VMEM((Tp + CONV_HALO, CONV_TILE), F32)],
        name=name, compiler_params=_cparams(("arbitrary", "arbitrary")),
    )(xbc, w, b)


def _conv_bwd(xbc, w, b, dact, pad, name):
    B, Tp, C = xbc.shape
    nch = Tp // CHUNK

    def body(x_ref, w_ref, b_ref, da_ref, dx_ref, dw_ref, db_ref, xp, dp):
        bi = pl.program_id(1)
        xp[0:CONV_HALO, :] = jnp.zeros((CONV_HALO, CONV_TILE), F32)
        xp[CONV_HALO:, :] = x_ref[...]
        dp[pl.ds(Tp, CONV_HALO), :] = jnp.zeros((CONV_HALO, CONV_TILE), F32)
        dws = [jnp.zeros((1, CONV_TILE), F32) for _ in range(SSD_CONV)]
        dbs = jnp.zeros((1, CONV_TILE), F32)
        for c in range(nch):
            xs = [xp[pl.ds(CONV_HALO + CHUNK * c - (SSD_CONV - 1) + k, CHUNK), :] for k in range(SSD_CONV)]
            acc = jnp.zeros((CHUNK, CONV_TILE), F32) + b_ref[...]
            for k in range(SSD_CONV):
                acc = acc + w_ref[k:k + 1, :] * xs[k]
            row = CHUNK * c + lax.broadcasted_iota(jnp.int32, (CHUNK, 1), 0)
            sg = jax.nn.sigmoid(acc)
            dpre = jnp.where(row >= pad, da_ref[pl.ds(CHUNK * c, CHUNK), :] * (sg * (1.0 + acc * (1.0 - sg))), 0.0)
            dp[pl.ds(CHUNK * c, CHUNK), :] = dpre
            dbs = dbs + jnp.sum(dpre, axis=0, keepdims=True)
            for k in range(SSD_CONV):
                dws[k] = dws[k] + jnp.sum(dpre * xs[k], axis=0, keepdims=True)
        for c in range(nch):
            acc = jnp.zeros((CHUNK, CONV_TILE), F32)
            for k in range(SSD_CONV):
                acc = acc + w_ref[k:k + 1, :] * dp[pl.ds(CHUNK * c + (SSD_CONV - 1) - k, CHUNK), :]
            dx_ref[pl.ds(CHUNK * c, CHUNK), :] = acc.astype(dx_ref.dtype)

        @pl.when(bi == 0)
        def _():
            dw_ref[...] = jnp.zeros_like(dw_ref)
            db_ref[...] = jnp.zeros_like(db_ref)

        for k in range(SSD_CONV):
            dw_ref[k:k + 1, :] += dws[k]
        db_ref[0:1, :] += dbs

    return pl.pallas_call(
        body, grid=(C // CONV_TILE, B),
        in_specs=[pl.BlockSpec((None, Tp, CONV_TILE), lambda j, i: (i, 0, j)),
                  pl.BlockSpec((SSD_CONV, CONV_TILE), lambda j, i: (0, j)),
                  pl.BlockSpec((1, CONV_TILE), lambda j, i: (0, j)),
                  pl.BlockSpec((None, Tp, CONV_TILE), lambda j, i: (i, 0, j))],
        out_specs=[pl.BlockSpec((None, Tp, CONV_TILE), lambda j, i: (i, 0, j)),
                   pl.BlockSpec((8, CONV_TILE), lambda j, i: (0, j)),
                   pl.BlockSpec((8, CONV_TILE), lambda j, i: (0, j))],
        out_shape=[jax.ShapeDtypeStruct(xbc.shape, BF16), jax.ShapeDtypeStruct((8, C), F32),
                   jax.ShapeDtypeStruct((8, C), F32)],
        scratch_shapes=[pltpu.VMEM((Tp + CONV_HALO, CONV_TILE), F32), pltpu.VMEM((Tp + CONV_HALO, CONV_TILE), F32)],
        name=name, compiler_params=_cparams(("arbitrary", "arbitrary")),
    )(xbc, w, b, dact)


def _ssd_chunk(xs, bm, cm, dtr, z, state, dt_bias, a_log, dskip, norm_w, valid):
    Q = xs.shape[0]
    lane = lax.broadcasted_iota(jnp.int32, (1, 128), 1)
    dt = jnp.where(lane < SSD_HEADS, _softplus(dtr + dt_bias), 0.0) * valid
    a = dt * (-jnp.exp(a_log))
    tril = _tril(Q)
    cs = _cumsum_rows(a)
    cs_t = cs.T
    cs_end = _row_of(cs, Q - 1)
    low = lane < SSD_HEAD_DIM
    low_rows = lax.broadcasted_iota(jnp.int32, (128, 1), 0) < SSD_HEAD_DIM
    ys, new_state = [], []
    for g in range(SSD_GROUPS):
        bg = bm[:, 128 * g:128 * (g + 1)]
        cg = cm[:, 128 * g:128 * (g + 1)]
        cb = _mm_nt(cg, bg)
        for pr in range(2):
            p = 2 * g + pr
            h0, h1 = 2 * p, 2 * p + 1
            xp = xs[:, 128 * p:128 * (p + 1)]
            c0, c1 = _col_of(cs, h0), _col_of(cs, h1)
            e0, e1 = _col_of(cs_end, h0), _col_of(cs_end, h1)
            xd = xp * jnp.where(low, _col_of(dt, h0), _col_of(dt, h1))
            l0 = jnp.exp(jnp.where(tril, c0 - _row_of(cs_t, h0), -1e30))
            l1 = jnp.exp(jnp.where(tril, c1 - _row_of(cs_t, h1), -1e30))
            y_diag = jnp.where(low, _mm(cb * l0, xd), _mm(cb * l1, xd))
            to_end = jnp.where(low, jnp.exp(e0 - c0), jnp.exp(e1 - c1))
            sp = state[128 * p:128 * (p + 1), :]
            y_off = _mm_nt(cg, sp) * jnp.where(low, jnp.exp(c0), jnp.exp(c1))
            new_state.append(sp * jnp.where(low_rows, jnp.exp(e0), jnp.exp(e1)) + _mm_tn(xd * to_end, bg))
            ys.append(y_diag + y_off + xp * jnp.where(low, _col_of(dskip, h0), _col_of(dskip, h1)))
    y = jnp.concatenate(ys, axis=1) * _silu(z)
    gw = SSD_INNER // SSD_GROUPS
    outs = []
    for g in range(SSD_GROUPS):
        blk = y[:, gw * g:gw * (g + 1)]
        outs.append(blk * lax.rsqrt(jnp.mean(blk * blk, axis=-1, keepdims=True) + EPS))
    return jnp.concatenate(outs, axis=1) * norm_w, jnp.concatenate(new_state, axis=0)


def _valid_rows(c, pad):
    row = c * CHUNK + lax.broadcasted_iota(jnp.int32, (CHUNK, 1), 0)
    return (row >= pad).astype(F32)


def _ssd_fwd(xact, dtr, z, dt_bias, a_log, dskip, norm_w, pad, name):
    B, Tp, _ = xact.shape
    nc = Tp // CHUNK

    def body(xs_ref, bm_ref, cm_ref, dt_ref, z_ref, db_ref, al_ref, ds_ref, nw_ref, y_ref, save_ref, st):
        c = pl.program_id(1)

        @pl.when(c == 0)
        def _():
            st[...] = jnp.zeros_like(st)

        s0 = st[...]
        save_ref[...] = s0
        y, s1 = _ssd_chunk(xs_ref[...], bm_ref[...], cm_ref[...], dt_ref[...], z_ref[...].astype(F32), s0, db_ref[...],
                           al_ref[...], ds_ref[...], nw_ref[...], _valid_rows(c, pad))
        y_ref[...] = y.astype(y_ref.dtype)
        st[...] = s1

    row = lambda w, off=0: pl.BlockSpec((None, CHUNK, w), lambda b, c: (b, c, off))
    par = lambda w: pl.BlockSpec((1, w), lambda b, c: (0, 0))
    return pl.pallas_call(
        body, grid=(B, nc),
        in_specs=[row(1024, 0), row(512, 2), row(512, 3), row(128), row(1024), par(128), par(128), par(128), par(1024)],
        out_specs=[row(1024), pl.BlockSpec((None, None, 1024, 128), lambda b, c: (b, c, 0, 0))],
        out_shape=[jax.ShapeDtypeStruct((B, Tp, SSD_INNER), BF16), jax.ShapeDtypeStruct((B, nc, 1024, 128), F32)],
        scratch_shapes=[pltpu.VMEM((1024, 128), F32)],
        name=name, compiler_params=_cparams(("arbitrary", "arbitrary")),
    )(xact, xact, xact, dtr, z, dt_bias, a_log, dskip, norm_w)


def _ssd_bwd(xact, dtr, z, dt_bias, a_log, dskip, norm_w, saved, dy, pad, name, after=None):
    B, Tp, _ = xact.shape
    nc = Tp // CHUNK

    def body(xs_ref, bm_ref, cm_ref, dt_ref, z_ref, db_ref, al_ref, ds_ref, nw_ref, sv_ref, dy_ref,
             dx_ref, ddt_ref, dz_ref, dpar_ref, dnw_ref, dst):
        b, i = pl.program_id(0), pl.program_id(1)
        c = nc - 1 - i

        @pl.when(i == 0)
        def _():
            dst[...] = jnp.zeros_like(dst)

        valid = _valid_rows(c, pad)
        fn = lambda *a: _ssd_chunk(*a, valid)
        _, vjp = jax.vjp(fn, xs_ref[...], bm_ref[...], cm_ref[...], dt_ref[...], z_ref[...].astype(F32), sv_ref[...],
                         db_ref[...], al_ref[...], ds_ref[...], nw_ref[...])
        dxs, dbm, dcm, ddt, dz, dstate, ddb, dal, dds, dnw = vjp((dy_ref[...].astype(F32), dst[...]))
        dx_ref[:, 0:1024] = dxs
        dx_ref[:, 1024:1536] = dbm
        dx_ref[:, 1536:2048] = dcm
        ddt_ref[...] = ddt
        dz_ref[...] = dz.astype(dz_ref.dtype)
        dst[...] = dstate

        @pl.when((b == 0) & (i == 0))
        def _():
            dpar_ref[...] = jnp.zeros_like(dpar_ref)
            dnw_ref[...] = jnp.zeros_like(dnw_ref)

        dpar_ref[0:1, :] += ddb
        dpar_ref[1:2, :] += dal
        dpar_ref[2:3, :] += dds
        dnw_ref[0:1, :] += dnw

    row = lambda w, off=0: pl.BlockSpec((None, CHUNK, w), lambda b, i: (b, nc - 1 - i, off))
    par = lambda w: pl.BlockSpec((1, w), lambda b, i: (0, 0))
    acc = lambda w: pl.BlockSpec((8, w), lambda b, i: (0, 0))
    in_specs = [row(1024, 0), row(512, 2), row(512, 3), row(128), row(1024), par(128), par(128), par(128), par(1024),
                pl.BlockSpec((None, None, 1024, 128), lambda b, i: (b, nc - 1 - i, 0, 0)), row(1024)]
    args = [xact, xact, xact, dtr, z, dt_bias, a_log, dskip, norm_w, saved, dy]
    if after is not None:
        body = _skip_ref(body, len(args))
        args.append(_deps(after))
        in_specs.append(_dep_spec(args[-1]))
    outs = pl.pallas_call(
        body, grid=(B, nc), in_specs=in_specs,
        out_specs=[row(2048), row(128), row(1024), acc(128), acc(1024)],
        out_shape=[jax.ShapeDtypeStruct((B, Tp, 2048), F32), jax.ShapeDtypeStruct((B, Tp, 128), F32),
                   jax.ShapeDtypeStruct((B, Tp, 1024), BF16), jax.ShapeDtypeStruct((8, 128), F32),
                   jax.ShapeDtypeStruct((8, 1024), F32)],
        scratch_shapes=[pltpu.VMEM((1024, 128), F32)],
        name=name, compiler_params=_cparams(("arbitrary", "arbitrary")),
    )(*args)
    return outs


def _hg_chunk(qr, fr, ir, gr, state_t, p0, p1, norm_w, valid):
    Q = qr.shape[0]
    lb = jax.nn.sigmoid(p0 - p1)
    f = lb + (1.0 - lb) * jax.nn.sigmoid(fr)
    k = 1.0 - f
    q = _silu(qr)
    v = ir * valid
    cum = _cumsum_rows(jnp.log(f))
    cum_end = _row_of(cum, Q - 1)
    o_inter = _mm_nt(q * jnp.exp(cum), state_t)
    nblk = Q // HG_SUB
    row = lax.broadcasted_iota(jnp.int32, (Q, 1), 0)
    ri = lax.broadcasted_iota(jnp.int32, (Q, Q), 0)
    ci = lax.broadcasted_iota(jnp.int32, (Q, Q), 1)
    mids = jnp.concatenate([jnp.broadcast_to(_row_of(cum, HG_SUB * i + HG_SUB // 2 - 1), (HG_SUB, cum.shape[1]))
                            for i in range(nblk)], axis=0)
    sh = HG_SUB.bit_length() - 1
    same = (jnp.right_shift(ri, sh) == jnp.right_shift(ci, sh)) & (ri >= ci)
    att = jnp.where(same, _mm_nt(q * jnp.exp(cum - mids), k * jnp.exp(mids - cum)), 0.0)
    for i in range(1, nblk):
        lo = HG_SUB * i
        start = _row_of(cum, lo - 1)
        qa = q * jnp.exp(jnp.where((row >= lo) & (row < lo + HG_SUB), cum - start, -1e30))
        ka = k * jnp.exp(jnp.where(row < lo, start - cum, -1e30))
        att = att + _mm_nt(qa, ka)
    o = o_inter + _mm(att, v)
    new_state_t = state_t * jnp.exp(cum_end) + _mm_tn(v, k * jnp.exp(cum_end - cum))
    o = o * lax.rsqrt(jnp.mean(o * o, axis=-1, keepdims=True) + EPS) * norm_w
    return o * _silu(gr), new_state_t


HG_PER_STEP = 4
HG_COLS = 4 * 128


def _hg_fwd(qfig, lbh, nwh, pad, name):
    B, Tp, _ = qfig.shape
    nc = Tp // CHUNK
    hp = HG_PER_STEP

    def body(x_ref, lb_ref, nw_ref, y_ref, save_ref, st):
        c = pl.program_id(1)

        @pl.when(c == 0)
        def _():
            st[...] = jnp.zeros_like(st)

        valid = _valid_rows(c, pad)
        for j in range(hp):
            for b in range(B):
                s0 = st[j, b]
                save_ref[j, b] = s0
                col = lambda k: x_ref[b, :, HG_COLS * j + 128 * k:HG_COLS * j + 128 * (k + 1)]
                y, s1 = _hg_chunk(col(0), col(1), col(2), col(3), s0, lb_ref[j, 0:1, :], lb_ref[j, 1:2, :], nw_ref[j], valid)
                y_ref[b, :, 128 * j:128 * (j + 1)] = y.astype(y_ref.dtype)
                st[j, b] = s1

    return pl.pallas_call(
        body, grid=(HG_HEADS // hp, nc),
        in_specs=[pl.BlockSpec((B, CHUNK, HG_COLS * hp), lambda h, c: (0, c, h)),
                  pl.BlockSpec((hp, 2, 128), lambda h, c: (h, 0, 0)),
                  pl.BlockSpec((hp, 1, 128), lambda h, c: (h, 0, 0))],
        out_specs=[pl.BlockSpec((B, CHUNK, 128 * hp), lambda h, c: (0, c, h)),
                   pl.BlockSpec((hp, B, None, 128, 128), lambda h, c: (h, 0, c, 0, 0))],
        out_shape=[jax.ShapeDtypeStruct((B, Tp, 1024), BF16), jax.ShapeDtypeStruct((HG_HEADS, B, nc, 128, 128), F32)],
        scratch_shapes=[pltpu.VMEM((hp, B, 128, 128), F32)],
        name=name, compiler_params=_cparams(("arbitrary", "arbitrary")),
    )(qfig, lbh, nwh)


def _hg_bwd(qfig, lbh, nwh, saved, dy, pad, name, after=None):
    B, Tp, _ = qfig.shape
    nc = Tp // CHUNK
    hp = HG_PER_STEP

    def body(x_ref, lb_ref, nw_ref, sv_ref, dy_ref, dx_ref, dlb_ref, dnw_ref, dst):
        i = pl.program_id(1)
        c = nc - 1 - i

        @pl.when(i == 0)
        def _():
            dst[...] = jnp.zeros_like(dst)
            dlb_ref[...] = jnp.zeros_like(dlb_ref)
            dnw_ref[...] = jnp.zeros_like(dnw_ref)

        valid = _valid_rows(c, pad)
        fn = lambda *a: _hg_chunk(*a, valid)
        for j in range(hp):
            for b in range(B):
                col = lambda k: x_ref[b, :, HG_COLS * j + 128 * k:HG_COLS * j + 128 * (k + 1)]
                _, vjp = jax.vjp(fn, col(0), col(1), col(2), col(3), sv_ref[j, b], lb_ref[j, 0:1, :], lb_ref[j, 1:2, :], nw_ref[j])
                d4 = vjp((dy_ref[b, :, 128 * j:128 * (j + 1)].astype(F32), dst[j, b]))
                for k in range(4):
                    dx_ref[b, :, HG_COLS * j + 128 * k:HG_COLS * j + 128 * (k + 1)] = d4[k].astype(dx_ref.dtype)
                dst[j, b] = d4[4]
                dlb_ref[j, 0:1, :] += d4[5]
                dlb_ref[j, 1:2, :] += d4[6]
                dnw_ref[j, 0:1, :] += d4[7]

    acc = pl.BlockSpec((hp, 8, 128), lambda h, i: (h, 0, 0))
    in_specs = [pl.BlockSpec((B, CHUNK, HG_COLS * hp), lambda h, i: (0, nc - 1 - i, h)),
                pl.BlockSpec((hp, 2, 128), lambda h, i: (h, 0, 0)),
                pl.BlockSpec((hp, 1, 128), lambda h, i: (h, 0, 0)),
                pl.BlockSpec((hp, B, None, 128, 128), lambda h, i: (h, 0, nc - 1 - i, 0, 0)),
                pl.BlockSpec((B, CHUNK, 128 * hp), lambda h, i: (0, nc - 1 - i, h))]
    args = [qfig, lbh, nwh, saved, dy]
    if after is not None:
        body = _skip_ref(body, len(args))
        args.append(_deps(after))
        in_specs.append(_dep_spec(args[-1]))
    return pl.pallas_call(
        body, grid=(HG_HEADS // hp, nc), in_specs=in_specs,
        out_specs=[pl.BlockSpec((B, CHUNK, HG_COLS * hp), lambda h, i: (0, nc - 1 - i, h)), acc, acc],
        out_shape=[jax.ShapeDtypeStruct((B, Tp, 4096), BF16), jax.ShapeDtypeStruct((HG_HEADS, 8, 128), F32),
                   jax.ShapeDtypeStruct((HG_HEADS, 8, 128), F32)],
        scratch_shapes=[pltpu.VMEM((hp, B, 128, 128), F32)],
        name=name, compiler_params=_cparams(("arbitrary", "arbitrary")),
    )(*args)


def _adamw(w, g, m, v, name, after=None):
    R, C = w.shape
    tr = max(t for t in range(8, R + 1, 8) if R % t == 0 and (t * C * 4 <= ADAMW_BLOCK_BYTES or t == 8))

    def body(w_ref, g_ref, m_ref, v_ref, d_ref, mo_ref, vo_ref):
        g_ = g_ref[...]
        m_ = ADAM_B1 * m_ref[...] + (1.0 - ADAM_B1) * g_
        v_ = ADAM_B2 * v_ref[...] + (1.0 - ADAM_B2) * (g_ * g_)
        m_hat = m_ / (1.0 - ADAM_B1 ** ADAM_STEP)
        v_hat = v_ / (1.0 - ADAM_B2 ** ADAM_STEP)
        d_ref[...] = -ADAM_LR * (m_hat / (jnp.sqrt(v_hat) + ADAM_EPS) + ADAM_WD * w_ref[...])
        mo_ref[...] = m_
        vo_ref[...] = v_

    sp = pl.BlockSpec((tr, C), lambda i: (i, 0))
    sh = jax.ShapeDtypeStruct((R, C), F32)
    in_specs, args = [sp] * 4, [w, g, m, v]
    if after is not None:
        body = _skip_ref(body, len(args))
        args.append(_deps(after))
        in_specs.append(_dep_spec(args[-1]))
    return pl.pallas_call(body, grid=(R // tr,), in_specs=in_specs, out_specs=[sp] * 3, out_shape=[sh] * 3,
                          name=name, compiler_params=_cparams(("arbitrary",)))(*args)


def _ffn_fwd(h, norm_w, w_gu, w_down, tag, after_norm=None):
    n = _rms_fwd(h, norm_w, f"{tag}_norm")
    if after_norm is not None:
        after_norm(n)
    gu, a = _gu_swiglu(n, w_gu, f"{tag}_gu")
    out = _matmul(a, w_down, mode="nn", out_dtype=F32, alpha=0.5, res=h, name=f"{tag}_down")
    return out, (n, gu, a)


def _ffn_bwd(h, norm_w, w_gu, w_down, saved, dout, tag, after_dw_down=None, token_seqs=None):
    n, gu, a = saved
    dgu = _d_swiglu(dout, w_down, gu, 0.5, f"{tag}_d_gu")
    dw_down = _matmul(a, dout, mode="tn", out_dtype=F32, alpha=0.5, name=f"{tag}_dw_down")
    dw_gu = _matmul(n, dgu, mode="tn", out_dtype=F32, out_groups=N_CHIPS, name=f"{tag}_dw_gu",
                    after=after_dw_down(dw_down) if after_dw_down else None)
    dn = _matmul(dgu, w_gu, mode="nt", out_dtype=F32, name=f"{tag}_d_norm", after=dw_gu)
    if token_seqs is None:
        dh, dnw = _rms_bwd(h, norm_w, dn, dout, f"{tag}_d_in")
    else:
        dx, dm, dnw = _rms_bwd_tokens(h, norm_w, dn, dout, token_seqs, f"{tag}_d_in")
        dh = (dx, dm)
    return dh, dnw, dw_gu, dw_down


def _split_w_in(w_in_full):
    pts = [0]
    for s in IN_SIZES:
        pts.append(pts[-1] + s)
    sl = lambda i, j: w_in_full[:, pts[i]:pts[j]]
    qfig = sl(3, 7).reshape(D_MODEL, 4, HG_HEADS, 128).transpose(0, 2, 1, 3).reshape(D_MODEL, 4 * D_MODEL)
    return {"z": sl(0, 1), "xbc": sl(1, 2), "dt": jnp.pad(sl(2, 3), ((0, 0), (0, 128 - SSD_HEADS))),
            "qfig": qfig, "gates": sl(7, 9)}


def _local_step(x, target, W):
    B, S, _ = x.shape
    T = N_META + S
    pad = (-T) % CHUNK
    Tp = T + pad
    assert pad + N_META == CHUNK
    R = B * Tp
    meta = jnp.broadcast_to(W["meta_tokens"][None], (B, N_META, D_MODEL))
    h0 = jnp.concatenate([jnp.zeros((B, pad, D_MODEL), F32), meta, x], axis=1).reshape(R, D_MODEL)

    stage = W.get("_stage", lambda name, x: {})
    W = dict(W)
    h1, sv1 = _ffn_fwd(h0, W["ffn1_norm"], W["ffn1_w_gu"], W["ffn1_w_down"], "ffn1", lambda n: W.update(stage("ffn1_norm", n)))
    W.update(stage("ffn1_out", h1))
    um = _rms_fwd(h1, W["mix_norm"], "mix_norm")
    wi = W["w_in"]
    z = _matmul(um, wi["z"], mode="nn", out_dtype=BF16, name="in_z")
    xbc = _matmul(um, wi["xbc"], mode="nn", out_dtype=F32, name="in_xbc")
    dtr = _matmul(um, wi["dt"], mode="nn", out_dtype=F32, name="in_dt")
    qfig = _matmul(um, wi["qfig"], mode="nn", out_dtype=F32, name="in_qfig")
    gates = _matmul(um, wi["gates"], mode="nn", out_dtype=BF16, name="in_gates")

    r3 = lambda t: t.reshape(B, Tp, t.shape[-1])
    lane_pad = lambda t: jnp.pad(t, ((0, 0), (0, 128 - t.shape[1])))
    dt_bias, a_log, dskip = lane_pad(W["ssd_dt_bias"]), lane_pad(W["ssd_a_log"]), lane_pad(W["ssd_d"])
    xact = _conv_fwd(r3(xbc), W["ssd_conv_w"], W["ssd_conv_b"], pad, "conv_fwd")
    ya, ssd_saved = _ssd_fwd(xact, r3(dtr), r3(z), dt_bias, a_log, dskip, W["ssd_norm"], pad, "ssd_fwd")
    lbh = W["hg_lower_bound"].reshape(2, HG_HEADS, 128).transpose(1, 0, 2)
    nwh = W["hg_norm"].reshape(HG_HEADS, 1, 128)
    yb, hg_saved = _hg_fwd(r3(qfig), lbh, nwh, pad, "hg_fwd")
    ya2, yb2 = ya.reshape(R, -1), yb.reshape(R, -1)
    W.update(stage("mixers_out", yb2))
    pa = _matmul(ya2, W["w_branch_a"], mode="nn", out_dtype=F32, name="branch_a")
    pb = _matmul(yb2, W["w_branch_b"], mode="nn", out_dtype=F32, name="branch_b")
    mg = _merge_fwd(pa, pb, gates, "merge")
    h2 = _matmul(mg, W["w_out"], mode="nn", out_dtype=F32, res=h1, name="mix_out")
    h3, sv2 = _ffn_fwd(h2, W["ffn2_norm"], W["ffn2_w_gu"], W["ffn2_w_down"], "ffn2")

    loss, dh3, d_final = _loss_head(h3, W["final_norm"].reshape(1, D_MODEL), target, B, "loss_head")

    G = {"final_norm": d_final[0]}
    dh2, dnw, G["ffn2_w_gu"], G["ffn2_w_down"] = _ffn_bwd(h2, W["ffn2_norm"], W["ffn2_w_gu"], W["ffn2_w_down"], sv2, dh3, "ffn2")
    G["ffn2_norm"] = dnw[0:1]
    dmg = _matmul(dh2, W["w_out"], mode="nt", out_dtype=BF16, name="d_merge")
    G["w_out"] = _matmul(mg, dh2, mode="tn", out_dtype=F32, name="dw_out")
    dpa, dpb, dgates = _merge_bwd(pa, pb, gates, dmg, "merge_bwd")
    dya = _matmul(dpa, W["w_branch_a"], mode="nt", out_dtype=BF16, name="d_ya")
    dyb = _matmul(dpb, W["w_branch_b"], mode="nt", out_dtype=BF16, name="d_yb")
    G["w_branch_a"] = _matmul(ya2, dpa, mode="tn", out_dtype=F32, name="dw_branch_a")
    G["w_branch_b"] = _matmul(yb2, dpb, mode="tn", out_dtype=F32, name="dw_branch_b")

    dxact, ddtr, dz, dpar, dnw = _ssd_bwd(xact, r3(dtr), r3(z), dt_bias, a_log, dskip, W["ssd_norm"], ssd_saved,
                                          r3(dya), pad, "ssd_bwd", after=stage("late_grads", G).get("_after"))
    G["ssd_dt_bias"], G["ssd_a_log"], G["ssd_d"] = dpar[0:1, :SSD_HEADS], dpar[1:2, :SSD_HEADS], dpar[2:3, :SSD_HEADS]
    G["ssd_norm"] = dnw[0:1]
    dxbc, dcw, dcb = _conv_bwd(r3(xbc), W["ssd_conv_w"], W["ssd_conv_b"], dxact, pad, "conv_bwd")
    G["ssd_conv_w"], G["ssd_conv_b"] = dcw[0:SSD_CONV], dcb[0:1]
    dqfig, dlb, dhn = _hg_bwd(r3(qfig), lbh, nwh, hg_saved, r3(dyb), pad, "hg_bwd",
                              after=stage("after_conv_bwd", dcb).get("_after"))
    G["hg_lower_bound"] = dlb[:, 0:2, :].transpose(1, 0, 2).reshape(2, D_MODEL)
    G["hg_norm"] = dhn[:, 0, :].reshape(1, D_MODEL)

    r2 = lambda t: t.reshape(R, t.shape[-1])
    pieces = [("z", r2(dz)), ("xbc", r2(dxbc)), ("dt", r2(ddtr)), ("qfig", r2(dqfig)), ("gates", dgates)]
    dum = _sum_nt([p for _, p in pieces], [wi[nm] for nm, _ in pieces], "d_mix")
    dwi = {nm: _matmul(um, dpiece, mode="tn", out_dtype=F32, name=f"dw_in_{nm}") for nm, dpiece in pieces}
    dw_qfig = dwi["qfig"].reshape(D_MODEL, HG_HEADS, 4, 128).transpose(0, 2, 1, 3).reshape(D_MODEL, 4 * D_MODEL)
    G["w_in"] = jnp.concatenate([dwi["z"], dwi["xbc"], dwi["dt"][:, :SSD_HEADS], dw_qfig, dwi["gates"]], axis=1)
    dh1, dnw = _rms_bwd(h1, W["mix_norm"], dum, dh2, "mix_norm_bwd", after=stage("w_in_grads", dwi).get("_after"))
    G["mix_norm"] = dnw[0:1]
    (dx, dfirst), dnw, G["ffn1_w_gu"], G["ffn1_w_down"] = _ffn_bwd(
        h0, W["ffn1_norm"], W["ffn1_w_gu"], W["ffn1_w_down"], sv1, dh1, "ffn1",
        lambda dw: stage("ffn1_dw_down", dw).get("_after"), token_seqs=B)
    G["ffn1_norm"] = dnw[0:1]
    G["meta_tokens"] = jnp.sum(dfirst[:, pad:CHUNK], axis=0)
    return loss, dx, G


ANY = pl.BlockSpec(memory_space=pl.ANY)


def _place():
    return lax.axis_index("x"), lax.axis_index("y"), lax.axis_index("c")


def _other_chips(x, y):
    return [(1 - x, y), (x, 1 - y), (1 - x, 1 - y)]


def _remote(src, dst, ssem, rsem, dev):
    return pltpu.make_async_remote_copy(src_ref=src, dst_ref=dst, send_sem=ssem, recv_sem=rsem,
                                        device_id=dev, device_id_type=MESH)


def _exchange8(buf, name):
    n, w = buf.shape

    def body(x_ref, out_ref, ssem, rsem):
        x, y, c = _place()
        me = 4 * x + 2 * y + c
        out_ref[me] = x_ref[...]
        copies = []
        for k in range(1, 8):
            px = 1 - x if (k >> 2) & 1 else x
            py = 1 - y if (k >> 1) & 1 else y
            pc = 1 - c if k & 1 else c
            cp = _remote(x_ref, out_ref.at[me], ssem.at[k - 1], rsem.at[k - 1], (px, py, pc))
            cp.start()
            copies.append((cp, 4 * px + 2 * py + pc))
        for k, (cp, peer) in enumerate(copies):
            _remote(x_ref, out_ref.at[peer], ssem.at[k], rsem.at[k], (x, y, c)).wait_recv()
        for cp, _ in copies:
            cp.wait_send()

    vm = pl.BlockSpec(memory_space=pltpu.VMEM)
    return pl.pallas_call(
        body, in_specs=[vm], out_specs=vm, out_shape=jax.ShapeDtypeStruct((8, n, w), F32),
        scratch_shapes=[pltpu.SemaphoreType.DMA((7,)), pltpu.SemaphoreType.DMA((7,))], name=name,
    )(buf)


HBM = pltpu.MemorySpace.HBM


def _sequencer(name, collective_id, sems, sent):
    return functools.partial(pl.kernel, mesh=plsc.ScalarSubcoreMesh(axis_name="sequencer", num_cores=1), name=name,
                             scratch_types=sems, compiler_params=pltpu.CompilerParams(collective_id=collective_id),
                             cost_estimate=pl.CostEstimate(flops=0, transcendentals=0, bytes_accessed=2 * sent,
                                                           remote_bytes_transferred=sent))


def _nbytes(arrays):
    return sum(a.size * a.dtype.itemsize for a in arrays)


def _handshake(peers):
    barrier = pltpu.get_barrier_semaphore()
    for peer in peers:
        pl.semaphore_signal(barrier, inc=1, device_id=peer, device_id_type=MESH)
    pl.semaphore_wait(barrier, len(peers))


def _gather_seq(blocks, name, collective_id):
    n = len(blocks)
    half = [s.shape[1] // 2 for s in blocks]
    full = [jax.new_ref(b, memory_space=HBM) for b in blocks]

    @_sequencer(name, collective_id, [pltpu.SemaphoreType.DMA((n, 3))] * 4, _nbytes(blocks) * 3 // 4)
    def launch(ssem, rsem, fssem, frsem):
        x, y, c = _place()
        q = 2 * x + y
        chips = _other_chips(x, y)
        _handshake([(px, py, c) for px, py in chips] + [(x, y, 1 - c)])
        piece = lambda s, qq, cc: full[s].at[qq, pl.ds(cc * half[s], half[s])]
        sends = []
        for j, (px, py) in enumerate(chips):
            for s in range(n):
                cp = _remote(piece(s, q, c), piece(s, q, c), ssem.at[s, j], rsem.at[s, j], (px, py, c))
                cp.start()
                sends.append(cp)
        for j, (px, py) in enumerate(chips):
            for s in range(n):
                got = piece(s, 2 * px + py, c)
                _remote(got, got, ssem.at[s, j], rsem.at[s, j], (px, py, c)).wait_recv()
                cp = _remote(got, got, fssem.at[s, j], frsem.at[s, j], (x, y, 1 - c))
                cp.start()
                sends.append(cp)
        for j, (px, py) in enumerate(chips):
            for s in range(n):
                got = piece(s, 2 * px + py, 1 - c)
                _remote(got, got, fssem.at[s, j], frsem.at[s, j], (x, y, 1 - c)).wait_recv()
        for cp in sends:
            cp.wait_send()

    launch()
    return [r[...] for r in full]


def _share8(buf, name, collective_id):
    n, w = buf.shape
    src = jax.new_ref(buf, memory_space=HBM)
    out = jax.empty_ref(jax.ShapeDtypeStruct((8, n, w), F32), memory_space=HBM)

    @_sequencer(name, collective_id, [pltpu.SemaphoreType.DMA((7,)), pltpu.SemaphoreType.DMA((7,)), pltpu.SemaphoreType.DMA((1,))],
                7 * buf.size * 4)
    def launch(ssem, rsem, lsem):
        x, y, c = _place()
        me = 4 * x + 2 * y + c
        peers = [(1 - x if (k >> 2) & 1 else x, 1 - y if (k >> 1) & 1 else y, 1 - c if k & 1 else c) for k in range(1, 8)]
        _handshake(peers)
        mine = pltpu.make_async_copy(src, out.at[me], lsem.at[0])
        mine.start()
        sends = []
        for k, peer in enumerate(peers):
            cp = _remote(src, out.at[me], ssem.at[k], rsem.at[k], peer)
            cp.start()
            sends.append(cp)
        for k, (px, py, pc) in enumerate(peers):
            slot = out.at[4 * px + 2 * py + pc]
            _remote(slot, slot, ssem.at[k], rsem.at[k], (px, py, pc)).wait_recv()
        for cp in sends:
            cp.wait_send()
        mine.wait()

    launch()
    return out[...]


def _sum_slots(slots, name, after=None):
    _, n, w = slots.shape

    def body(s_ref, o_ref):
        acc = s_ref[0]
        for d in range(1, 8):
            acc = acc + s_ref[d]
        o_ref[...] = acc

    vm = pl.BlockSpec(memory_space=pltpu.VMEM)
    in_specs, args = [vm], [slots]
    if after is not None:
        body = _skip_ref(body, 1)
        args.append(_deps(after))
        in_specs.append(vm)
    return pl.pallas_call(body, in_specs=in_specs, out_specs=vm, out_shape=jax.ShapeDtypeStruct((n, w), F32), name=name)(*args)


def _pair_swap(parts, name, collective_id):
    n = len(parts)
    half = [p.shape[1] // 2 for p in parts]
    src = [jax.new_ref(p, memory_space=HBM) for p in parts]
    got = [jax.empty_ref(jax.ShapeDtypeStruct((p.shape[0], h, p.shape[2]), p.dtype), memory_space=HBM) for p, h in zip(parts, half)]

    @_sequencer(name, collective_id, [pltpu.SemaphoreType.DMA((n,))] * 2, _nbytes(parts) // 2)
    def launch(ssem, rsem):
        x, y, c = _place()
        _handshake([(x, y, 1 - c)])
        copies = []
        for s in range(n):
            cp = _remote(src[s].at[pl.ds(0, parts[s].shape[0]), pl.ds((1 - c) * half[s], half[s])], got[s], ssem.at[s], rsem.at[s], (x, y, 1 - c))
            cp.start()
            copies.append(cp)
        for cp in copies:
            cp.wait_recv()
        for cp in copies:
            cp.wait_send()

    launch()
    return [g[...] for g in got]


def _to_owners(sums, name, collective_id):
    n = len(sums)
    src = [jax.new_ref(s, memory_space=HBM) for s in sums]
    got = [jax.empty_ref(jax.ShapeDtypeStruct(s.shape, s.dtype), memory_space=HBM) for s in sums]

    @_sequencer(name, collective_id, [pltpu.SemaphoreType.DMA((n, 3))] * 2, _nbytes(sums) * 3 // 4)
    def launch(ssem, rsem):
        x, y, c = _place()
        q = 2 * x + y
        chips = _other_chips(x, y)
        _handshake([(px, py, c) for px, py in chips])
        sends = []
        for j, (px, py) in enumerate(chips):
            for s in range(n):
                cp = _remote(src[s].at[2 * px + py], got[s].at[q], ssem.at[s, j], rsem.at[s, j], (px, py, c))
                cp.start()
                sends.append(cp)
        for j, (px, py) in enumerate(chips):
            for s in range(n):
                slot = got[s].at[2 * px + py]
                _remote(slot, slot, ssem.at[s, j], rsem.at[s, j], (px, py, c)).wait_recv()
        for cp in sends:
            cp.wait_send()

    launch()
    return [g[...] for g in got]


def _pair_join(blocks, name, collective_id):
    n = len(blocks)
    out = [jax.new_ref(b, memory_space=HBM) for b in blocks]

    @_sequencer(name, collective_id, [pltpu.SemaphoreType.DMA((n,))] * 2, _nbytes(blocks) // 2)
    def launch(ssem, rsem):
        x, y, c = _place()
        _handshake([(x, y, 1 - c)])
        sends = []
        for s in range(n):
            h = blocks[s].shape[0] // 2
            mine = out[s].at[pl.ds(c * h, h)]
            cp = _remote(mine, mine, ssem.at[s], rsem.at[s], (x, y, 1 - c))
            cp.start()
            sends.append(cp)
        for s in range(n):
            h = blocks[s].shape[0] // 2
            theirs = out[s].at[pl.ds((1 - c) * h, h)]
            _remote(theirs, theirs, ssem.at[s], rsem.at[s], (x, y, 1 - c)).wait_recv()
        for cp in sends:
            cp.wait_send()

    launch()
    return [o[...] for o in out]


WIRE = BF16


def _row_tile(h):
    return _pick(h, (256, 368, 352, 128, 16))


def _add_pair(part, got, c, name, after=None):
    _, h, w = got.shape
    tr = _row_tile(h)
    nt = h // tr

    def body(c_ref, p_ref, g_ref, o_ref):
        o_ref[...] = (p_ref[...] + g_ref[...].astype(F32)).astype(o_ref.dtype)

    in_specs = [pl.BlockSpec((None, tr, w), lambda q, i, c_ref: (q, c_ref[0] * nt + i, 0)),
                pl.BlockSpec((None, tr, w), lambda q, i, c_ref: (q, i, 0))]
    args = [c.reshape(1).astype(jnp.int32), part, got]
    if after is not None:
        body = _skip_ref(body, len(args))
        args.append(_deps(after))
        in_specs.append(_dep_spec(args[-1]))
    return pl.pallas_call(
        body,
        grid_spec=pltpu.PrefetchScalarGridSpec(
            num_scalar_prefetch=1, grid=(got.shape[0], nt), in_specs=in_specs,
            out_specs=pl.BlockSpec((None, tr, w), lambda q, i, c_ref: (q, i, 0))),
        out_shape=jax.ShapeDtypeStruct(got.shape, WIRE), name=name,
        compiler_params=_cparams(("arbitrary", "arbitrary")),
    )(*args)


def _sum_chips(slots, sums, q, c, name, after=None):
    _, h, w = slots.shape
    tr = _row_tile(h)
    nt = h // tr

    def body(s_ref, mine_ref, a_ref, b_ref, d_ref, o_ref):
        o_ref[...] = ((mine_ref[...].astype(F32) + a_ref[...].astype(F32)) + b_ref[...].astype(F32)) + d_ref[...].astype(F32)

    slot = lambda k: pl.BlockSpec((None, tr, w), lambda i, s_ref: (s_ref[1 + k], i, 0))
    scalars = jnp.stack([c, q, (q + 1) % N_CHIPS, (q + 2) % N_CHIPS, (q + 3) % N_CHIPS]).astype(jnp.int32)
    in_specs, args = [slot(0), slot(1), slot(2), slot(3)], [scalars, sums, slots, slots, slots]
    if after is not None:
        body = _skip_ref(body, len(args))
        args.append(_deps(after))
        in_specs.append(_dep_spec(args[-1]))
    return pl.pallas_call(
        body,
        grid_spec=pltpu.PrefetchScalarGridSpec(
            num_scalar_prefetch=1, grid=(nt,), in_specs=in_specs,
            out_specs=pl.BlockSpec((tr, w), lambda i, s_ref: (s_ref[0] * nt + i, 0))),
        out_shape=jax.ShapeDtypeStruct((2 * h, w), F32), name=name,
        compiler_params=_cparams(("arbitrary",)),
    )(*args)


class _Reduce:
    def __init__(self, parts, q, c, tag, first_id, regions=None):
        self.parts, self.q, self.c, self.tag, self.first_id, self.regions = parts, q, c, tag, first_id, regions
        self.got = _pair_swap(parts, f"{tag}_pair_swap", first_id)

    def to_owners(self, after=None):
        self.sums = [_add_pair(p, g, self.c, f"{self.tag}_pair_add{i}", after)
                     for i, (p, g) in enumerate(zip(self.parts, self.got))]
        if self.regions is not None:
            self.sums = self.regions(self.sums)
        self.slots = _to_owners(self.sums, f"{self.tag}_to_owners", self.first_id + 1)
        return self.sums

    def join(self, after=None):
        blocks = [_sum_chips(sl, sm, self.q, self.c, f"{self.tag}_sum_chips{i}", after)
                  for i, (sl, sm) in enumerate(zip(self.slots, self.sums))]
        self.out = _pair_join(blocks, f"{self.tag}_pair_join", self.first_id + 2)
        return blocks


WEIGHTS = ("meta_tokens", "ffn1_norm", "ffn1_w_gu", "ffn1_w_down", "mix_norm", "w_in", "ssd_conv_w", "ssd_conv_b",
           "ssd_dt_bias", "ssd_a_log", "ssd_d", "ssd_norm", "hg_lower_bound", "hg_norm", "w_branch_a", "w_branch_b",
           "w_out", "ffn2_norm", "ffn2_w_gu", "ffn2_w_down", "final_norm")
BIG = ("ffn1_w_gu", "ffn1_w_down", "w_in", "w_branch_a", "w_branch_b", "w_out", "ffn2_w_gu", "ffn2_w_down")
ROW_SHARDED = ("ffn1_w_down", "ffn2_w_down", "w_branch_a", "w_branch_b", "w_out")
SMALL = tuple(n for n in WEIGHTS if n not in BIG)
SMALL_ROWS = 24


def _rows1024(a):
    flat = a.reshape(-1)
    n = -(-flat.shape[0] // 1024) * 1024
    return jnp.pad(flat, (0, n - flat.shape[0])).reshape(-1, 1024)


def _pack_small(d):
    rows = jnp.concatenate([_rows1024(d[n]) for n in SMALL], axis=0)
    return jnp.pad(rows, ((0, SMALL_ROWS - rows.shape[0]), (0, 0)))


def _unpack_small(packed, like):
    out, r = {}, 0
    for n in SMALL:
        size = like[n].size
        nr = -(-size // 1024)
        out[n] = packed[r:r + nr].reshape(-1)[:size].reshape(like[n].shape)
        r += nr
    return out


def kernel(x, meta_tokens, ffn1_norm, ffn1_w_gu, ffn1_w_down, mix_norm, w_in, ssd_conv_w, ssd_conv_b, ssd_dt_bias, ssd_a_log, ssd_d, ssd_norm, hg_lower_bound, hg_norm, w_branch_a, w_branch_b, w_out, ffn2_norm, ffn2_w_gu, ffn2_w_down, final_norm, loss_target, m_meta_tokens, m_ffn1_norm, m_ffn1_w_gu, m_ffn1_w_down, m_mix_norm, m_w_in, m_ssd_conv_w, m_ssd_conv_b, m_ssd_dt_bias, m_ssd_a_log, m_ssd_d, m_ssd_norm, m_hg_lower_bound, m_hg_norm, m_w_branch_a, m_w_branch_b, m_w_out, m_ffn2_norm, m_ffn2_w_gu, m_ffn2_w_down, m_final_norm, v_meta_tokens, v_ffn1_norm, v_ffn1_w_gu, v_ffn1_w_down, v_mix_norm, v_w_in, v_ssd_conv_w, v_ssd_conv_b, v_ssd_dt_bias, v_ssd_a_log, v_ssd_d, v_ssd_norm, v_hg_lower_bound, v_hg_norm, v_w_branch_a, v_w_branch_b, v_w_out, v_ffn2_norm, v_ffn2_w_gu, v_ffn2_w_down, v_final_norm):
    P = dict(zip(WEIGHTS, (meta_tokens, ffn1_norm, ffn1_w_gu, ffn1_w_down, mix_norm, w_in, ssd_conv_w, ssd_conv_b, ssd_dt_bias, ssd_a_log, ssd_d, ssd_norm, hg_lower_bound, hg_norm, w_branch_a, w_branch_b, w_out, ffn2_norm, ffn2_w_gu, ffn2_w_down, final_norm)))
    M = dict(zip(WEIGHTS, (m_meta_tokens, m_ffn1_norm, m_ffn1_w_gu, m_ffn1_w_down, m_mix_norm, m_w_in, m_ssd_conv_w, m_ssd_conv_b, m_ssd_dt_bias, m_ssd_a_log, m_ssd_d, m_ssd_norm, m_hg_lower_bound, m_hg_norm, m_w_branch_a, m_w_branch_b, m_w_out, m_ffn2_norm, m_ffn2_w_gu, m_ffn2_w_down, m_final_norm)))
    V = dict(zip(WEIGHTS, (v_meta_tokens, v_ffn1_norm, v_ffn1_w_gu, v_ffn1_w_down, v_mix_norm, v_w_in, v_ssd_conv_w, v_ssd_conv_b, v_ssd_dt_bias, v_ssd_a_log, v_ssd_d, v_ssd_norm, v_hg_lower_bound, v_hg_norm, v_w_branch_a, v_w_branch_b, v_w_out, v_ffn2_norm, v_ffn2_w_gu, v_ffn2_w_down, v_final_norm)))
    cx, cy, cc = _place()
    q = 2 * cx + cy

    mine = jnp.concatenate([meta_tokens.reshape(4, 1024), ssd_conv_w.reshape(2, 1024), jnp.zeros((2, 1024), F32)], axis=0)
    every = _exchange8(mine, "gather_small")
    meta_full = jnp.concatenate([every[2 * k, 0:4].reshape(N_META, 256) for k in range(N_CHIPS)], axis=1)
    conv_w_full = jnp.concatenate([every[2 * k, 4:6].reshape(SSD_CONV, 512) for k in range(N_CHIPS)], axis=1)

    late = ("ffn2_w_down", "w_branch_a", "w_branch_b", "w_out")
    rows = jnp.concatenate([P[n][0] for n in late], axis=0)
    zero = lambda t, dtype=F32: (t[0:1, 0:1] * 0).astype(dtype)

    def in_slot(s, after=None):
        s = s if after is None else s + zero(after)
        return lax.dynamic_update_slice(lax.empty((N_CHIPS,) + s.shape, BF16), s.astype(BF16)[None], (q, 0, 0))

    gu1, down1 = _gather_seq([in_slot(ffn1_w_gu[0]), in_slot(ffn1_w_down[0])], "gather_ffn1", 1)
    W = {n: P[n] for n in SMALL}
    W["meta_tokens"], W["ssd_conv_w"] = meta_full, conv_w_full
    W["ffn1_w_gu"], W["ffn1_w_down"] = gu1, down1.reshape(-1, D_MODEL)
    flying = {}

    def stage(name, t):
        if name == "ffn1_norm":
            flying["w_in"] = _gather_seq([in_slot(w_in[0], t)], "gather_w_in", 2)
            return {}
        if name == "ffn1_out":
            flying["late"] = _gather_seq([in_slot(ffn2_w_gu[0], t), in_slot(rows, t)], "gather_late", 3)
            (w_in_all,) = flying["w_in"]
            w_in_all = w_in_all + zero(t, BF16)
            return {"w_in": _split_w_in(w_in_all.transpose(1, 0, 2).reshape(D_MODEL, -1))}
        if name == "mixers_out":
            gu2, rows_all = flying["late"]
            out, r = {"ffn2_w_gu": gu2}, 0
            for n in late:
                nr = P[n].shape[1]
                out[n] = (rows_all[:, r:r + nr] + zero(t, BF16)).reshape(N_CHIPS * nr, D_MODEL)
                r += nr
            return out
        if name == "late_grads":
            row_parts = jnp.concatenate([t[n].reshape(N_CHIPS, -1, D_MODEL) for n in late], axis=1)
            flying["grad_late"] = _Reduce([t["ffn2_w_gu"], row_parts], q, cc, "grad_late", 4)
            return {"_after": [t["ffn2_w_gu"]] + [t[n] for n in late]}
        if name == "after_conv_bwd":
            return {"_after": flying["grad_late"].to_owners(after=t)}
        if name == "w_in_grads":
            order = ("z", "xbc", "dt", "qfig", "gates")
            blocks = flying["grad_late"].join(after=[t[k] for k in order])

            def regions(sums):
                z, xbc, dt, qfig, gates = [s[0] for s in sums]
                h = z.shape[0]
                qfig = qfig.reshape(h, HG_HEADS, 4, 128).transpose(0, 2, 1, 3).reshape(h, 4 * D_MODEL)
                cols = jnp.concatenate([z, xbc, dt[:, :SSD_HEADS], qfig, gates], axis=1)
                return [cols.reshape(h, N_CHIPS, -1).transpose(1, 0, 2)]

            flying["grad_w_in"] = _Reduce([t[k][None] for k in order], q, cc, "grad_w_in", 7, regions)
            return {"_after": blocks}
        if name == "ffn1_dw_down":
            return {"_after": flying["grad_w_in"].to_owners(after=t)}
        return {}

    W["_stage"] = stage

    loss8, grad_x, G = _local_step(x, loss_target, W)

    small = jnp.concatenate(
        [G["meta_tokens"]] + [_rows1024(G[n]) for n in SMALL if n != "meta_tokens"] + [_rows1024(loss8[0:1, 0:1])], axis=0)
    small = jnp.pad(small, ((0, 40 - small.shape[0]), (0, 0)))
    small_slots = _share8(small, "share_small", 13)

    grad_ffn1 = _Reduce([G["ffn1_w_gu"], G["ffn1_w_down"].reshape(N_CHIPS, -1, D_MODEL)], q, cc, "grad_ffn1", 10)
    flying["grad_w_in"].join(after=grad_x)
    going = grad_ffn1.to_owners(after=grad_x)
    g_gu2, g_rows = flying["grad_late"].out
    (g_w_in,) = flying["grad_w_in"].out
    Gb = {"ffn2_w_gu": g_gu2, "w_in": g_w_in}
    r = 0
    for n in late:
        nr = P[n].shape[1]
        Gb[n] = g_rows[r:r + nr]
        r += nr

    grads, delta, new_m, new_v, done = {}, {}, {}, {}, []
    cols = w_in.shape[2]
    to_tiles = lambda a: a.transpose(2, 0, 1).reshape(cols, 8, 128).reshape(cols * 8, 128)
    from_tiles = lambda a: a.reshape(cols, 1, D_MODEL).transpose(1, 2, 0)
    for n in [n for n in BIG if n in Gb]:
        if n == "w_in":
            g_t = to_tiles(Gb[n][None])
            d_, m_, v_ = _adamw(to_tiles(P[n]), g_t, to_tiles(M[n]), to_tiles(V[n]), f"adamw_{n}", after=going)
            grads[n], delta[n], new_m[n], new_v[n] = from_tiles(g_t), from_tiles(d_), from_tiles(m_), from_tiles(v_)
        else:
            d_, m_, v_ = _adamw(P[n][0], Gb[n], M[n][0], V[n][0], f"adamw_{n}", after=going)
            grads[n], delta[n], new_m[n], new_v[n] = Gb[n][None], d_[None], m_[None], v_[None]
        done.append(d_)

    small = _sum_slots(small_slots, "sum_small", after=done)
    Gs = {"meta_tokens": small[0:N_META]}
    r = N_META
    for n in SMALL:
        if n == "meta_tokens":
            continue
        nr = -(-G[n].size // 1024)
        Gs[n] = small[r:r + nr].reshape(-1)[:G[n].size].reshape(G[n].shape)
        r += nr
    loss = small[r, 0]
    Gs["meta_tokens"] = lax.dynamic_slice(Gs["meta_tokens"], (0, 256 * q), (N_META, 256))
    Gs["ssd_conv_w"] = lax.dynamic_slice(Gs["ssd_conv_w"], (0, 512 * q), (SSD_CONV, 512))[None]
    Gs = {n: Gs[n].reshape(P[n].shape) for n in SMALL}
    grads.update(Gs)
    d_s, m_s, v_s = _adamw(_pack_small(P), _pack_small(Gs), _pack_small(M), _pack_small(V), "adamw_small")
    delta.update(_unpack_small(d_s, P))
    new_m.update(_unpack_small(m_s, P))
    new_v.update(_unpack_small(v_s, P))
    done.append(d_s)
    grad_ffn1.join(after=done)
    Gb["ffn1_w_gu"], Gb["ffn1_w_down"] = grad_ffn1.out
    for n in ("ffn1_w_gu", "ffn1_w_down"):
        d_, m_, v_ = _adamw(P[n][0], Gb[n], M[n][0], V[n][0], f"adamw_{n}")
        grads[n], delta[n], new_m[n], new_v[n] = Gb[n][None], d_[None], m_[None], v_[None]
    return (loss, grad_x, *[grads[n] for n in WEIGHTS], *[delta[n] for n in WEIGHTS],
            *[new_m[n] for n in WEIGHTS], *[new_v[n] for n in WEIGHTS])
```

```python
import functools

import jax
import jax.numpy as jnp
from jax import lax
from jax.experimental import pallas as pl
from jax.experimental.pallas import tpu as pltpu
from jax.experimental.pallas import tpu_sc as plsc

F32 = jnp.float32
BF16 = jnp.bfloat16
HIGHEST = lax.Precision.HIGHEST
MESH = pl.DeviceIdType.MESH

D_MODEL = 1024
N_META = 16
EPS = 1e-6
SSD_HEADS = 16
SSD_HEAD_DIM = 64
SSD_INNER = 1024
SSD_GROUPS = 4
SSD_STATE = 128
SSD_CONV = 4
SSD_CONV_CH = 2048
HG_HEADS = 8
HG_SUB = 32
CHUNK = 128
D_FF = 2816
N_CHIPS = 4
IN_SIZES = (1024, 2048, 16, 1024, 1024, 1024, 1024, 1024, 1024)
ADAM_LR = 0.001
ADAM_B1 = 0.9
ADAM_B2 = 0.999
ADAM_EPS = 1e-08
ADAM_WD = 0.01
ADAM_STEP = 10
VMEM_LIMIT = 56 * 1024 * 1024
MATMUL_BLOCK_BYTES = 42 * 1024 * 1024
ADAMW_BLOCK_BYTES = 5 * 512 * 1024


def _cparams(sem=None):
    return pltpu.CompilerParams(dimension_semantics=sem, vmem_limit_bytes=VMEM_LIMIT)


def _pick(n, cands):
    for c in cands:
        if n % c == 0:
            return c
    return n


def _deps(after):
    xs = after if isinstance(after, (list, tuple)) else [after]
    one = lambda x: lax.slice(x, (0,) * x.ndim, (1,) * x.ndim).reshape(1).astype(F32)
    return jnp.concatenate([one(x) for x in xs]).reshape(1, -1)


def _dep_spec(dep):
    return pl.BlockSpec(dep.shape, lambda *_: (0, 0))


def _skip_ref(body, pos):
    return lambda *refs: body(*refs[:pos], *refs[pos + 1:])


def _dg(a, b, ca, cb):
    return lax.dot_general(a.astype(BF16), b.astype(BF16), (((ca,), (cb,)), ((), ())), preferred_element_type=F32)


@jax.custom_vjp
def _mm(a, b):
    return _dg(a, b, 1, 0)


def _mm_fwd(a, b):
    return _dg(a, b, 1, 0), (a, b)


def _mm_bwd(r, g):
    a, b = r
    return _dg(g, b, 1, 1), _dg(a, g, 0, 0)


_mm.defvjp(_mm_fwd, _mm_bwd)


@jax.custom_vjp
def _mm_nt(a, b):
    return _dg(a, b, 1, 1)


def _mm_nt_fwd(a, b):
    return _dg(a, b, 1, 1), (a, b)


def _mm_nt_bwd(r, g):
    a, b = r
    return _dg(g, b, 1, 0), _dg(g, a, 0, 0)


_mm_nt.defvjp(_mm_nt_fwd, _mm_nt_bwd)


@jax.custom_vjp
def _mm_tn(a, b):
    return _dg(a, b, 0, 0)


def _mm_tn_fwd(a, b):
    return _dg(a, b, 0, 0), (a, b)


def _mm_tn_bwd(r, g):
    a, b = r
    return _dg(b, g, 1, 1), _dg(a, g, 1, 0)


_mm_tn.defvjp(_mm_tn_fwd, _mm_tn_bwd)


def _tri_sum(x, lower):
    n = x.shape[0]
    ri = lax.broadcasted_iota(jnp.int32, (n, n), 0)
    ci = lax.broadcasted_iota(jnp.int32, (n, n), 1)
    tri = ((ri >= ci) if lower else (ri <= ci)).astype(BF16)
    x1 = x.astype(BF16)
    r1 = x - x1.astype(F32)
    x2 = r1.astype(BF16)
    x3 = (r1 - x2.astype(F32)).astype(BF16)
    dot = lambda p: lax.dot_general(tri, p, (((1,), (0,)), ((), ())), preferred_element_type=F32)
    return (dot(x3) + dot(x2)) + dot(x1)


@jax.custom_vjp
def _cumsum_rows(x):
    return _tri_sum(x, True)


_cumsum_rows.defvjp(lambda x: (_tri_sum(x, True), None), lambda _, g: (_tri_sum(g, False),))


def _silu(x):
    return x * jax.nn.sigmoid(x)


def _softplus(x):
    return jnp.maximum(x, 0.0) + jnp.log(1.0 + jnp.exp(-jnp.abs(x)))


def _tril(n):
    ri = lax.broadcasted_iota(jnp.int32, (n, n), 0)
    ci = lax.broadcasted_iota(jnp.int32, (n, n), 1)
    return ri >= ci


def _row_of(m, r):
    sub = lax.broadcasted_iota(jnp.int32, (m.shape[0], 1), 0)
    return jnp.sum(jnp.where(sub == r, m, 0.0), axis=0, keepdims=True)


def _col_of(m, c):
    lane = lax.broadcasted_iota(jnp.int32, (1, m.shape[1]), 1)
    return jnp.sum(jnp.where(lane == c, m, 0.0), axis=1, keepdims=True)


def _matmul(a, b, *, mode, out_dtype, name, alpha=1.0, res=None, tm=None, tn=None, out_groups=None, after=None):
    b3 = b.ndim == 3
    if mode == "nn":
        M, K = a.shape
        G = b.shape[0] if b3 else 1
        Ng = b.shape[-1]
        N = G * Ng
    elif mode == "nt":
        M, K = a.shape
        G = b.shape[0] if b3 else 1
        N = b.shape[-2]
        Kg = b.shape[-1]
        assert G * Kg == K
    else:
        K, M = a.shape
        N = b.shape[1]
        G = out_groups or 1
        Ng = N // G
    has_res = res is not None
    split_n = (mode == "nn" and b3) or (mode == "tn" and G > 1)
    per_mn = jnp.dtype(out_dtype).itemsize + (res.dtype.itemsize if has_res else 0)
    fits = [(m_ * n_, m_, n_)
            for m_ in (4352, 2176, 1408, 1088, 1024, 544, 512, 256, 128) if M % m_ == 0
            for n_ in (2816, 2048, 1408, 1024, 512, 256, 128) if (Ng if split_n else N) % n_ == 0
            if 2 * (K * m_ * a.dtype.itemsize + K * n_ * b.dtype.itemsize + m_ * n_ * per_mn) + 4 * m_ * n_ <= MATMUL_BLOCK_BYTES]
    _, tm_fit, tn_fit = max(fits)
    tm, tn = tm or tm_fit, tn or tn_fit
    nm, nn_ = M // tm, N // tn
    assert nm * tm == M and nn_ * tn == N, (name, M, N, K, tm, tn)

    if mode == "nn":
        a_spec = pl.BlockSpec((tm, K), lambda i, j: (i, 0))
        if b3:
            ns = Ng // tn
            b_spec = pl.BlockSpec((None, K, tn), lambda i, j: (j // ns, 0, j % ns))
        else:
            b_spec = pl.BlockSpec((K, tn), lambda i, j: (0, j))
        ca, cb = 1, 0
    elif mode == "nt":
        a_spec = pl.BlockSpec((tm, K), lambda i, j: (i, 0))
        if b3:
            b_spec = pl.BlockSpec((G, tn, Kg), lambda i, j: (0, j, 0))
        else:
            b_spec = pl.BlockSpec((tn, K), lambda i, j: (j, 0))
        ca, cb = 1, 1
    else:
        a_spec = pl.BlockSpec((K, tm), lambda i, j: (0, i))
        b_spec = pl.BlockSpec((K, tn), lambda i, j: (0, j))
        ca, cb = 0, 0
    if mode == "tn" and G > 1:
        ns = Ng // tn
        o_spec = pl.BlockSpec((None, tm, tn), lambda i, j: (j // ns, i, j % ns))
        out_shape = jax.ShapeDtypeStruct((G, M, Ng), out_dtype)
    else:
        o_spec = pl.BlockSpec((tm, tn), lambda i, j: (i, j))
        out_shape = jax.ShapeDtypeStruct((M, N), out_dtype)
    in_specs = [a_spec, b_spec]
    args = [a, b]
    if has_res:
        in_specs.append(pl.BlockSpec((tm, tn), lambda i, j: (i, j)))
        args.append(res)
    if after is not None:
        args.append(_deps(after))
        in_specs.append(_dep_spec(args[-1]))

    def body(*refs):
        a_ref, b_ref, o_ref = refs[0], refs[1], refs[-1]
        if mode == "nt" and b3:
            o = _dg(a_ref[:, 0:Kg], b_ref[0], ca, cb)
            for g in range(1, G):
                o = o + _dg(a_ref[:, g * Kg:(g + 1) * Kg], b_ref[g], ca, cb)
        else:
            o = _dg(a_ref[...], b_ref[...], ca, cb)
        if alpha != 1.0:
            o = o * alpha
        if has_res:
            o = o + refs[2][...]
        o_ref[...] = o.astype(o_ref.dtype)

    return pl.pallas_call(
        body, grid=(nm, nn_), in_specs=in_specs, out_specs=o_spec, out_shape=out_shape, name=name,
        compiler_params=_cparams(("parallel", "parallel")),
    )(*args)


def _sum_nt(xs, ws, name):
    R, N = xs[0].shape[0], ws[0].shape[0]
    n = len(xs)
    per_m = sum(x.shape[1] * x.dtype.itemsize for x in xs)
    per_n = sum(w.shape[1] * w.dtype.itemsize for w in ws)
    fits = [(m_ * n_, m_, n_) for m_ in (1088, 544, 256, 128) if R % m_ == 0 for n_ in (1024, 512, 256, 128) if N % n_ == 0
            if 2 * (m_ * per_m + n_ * per_n + m_ * n_ * 4) + 4 * m_ * n_ <= MATMUL_BLOCK_BYTES]
    _, tm, tn = max(fits)

    def body(*refs):
        o = _dg(refs[0][...], refs[n][...], 1, 1)
        for p in range(1, n):
            o = o + _dg(refs[p][...], refs[n + p][...], 1, 1)
        refs[-1][...] = o

    return pl.pallas_call(
        body, grid=(R // tm, N // tn),
        in_specs=[pl.BlockSpec((tm, x.shape[1]), lambda i, j: (i, 0)) for x in xs]
        + [pl.BlockSpec((tn, w.shape[1]), lambda i, j: (j, 0)) for w in ws],
        out_specs=pl.BlockSpec((tm, tn), lambda i, j: (i, j)), out_shape=jax.ShapeDtypeStruct((R, N), F32), name=name,
        compiler_params=_cparams(("parallel", "parallel")),
    )(*xs, *ws)


def _rms_fn(h, w):
    r = lax.rsqrt(jnp.mean(h * h, axis=-1, keepdims=True) + EPS)
    return h * r * w


def _swiglu_fn(gu):
    g = gu[:, :D_FF].astype(F32)
    u = gu[:, D_FF:].astype(F32)
    return _silu(g) * u


def _merge_fn(pa, pb, gates):
    return jax.nn.sigmoid(gates[:, :D_MODEL]) * pa + jax.nn.sigmoid(gates[:, D_MODEL:]) * pb


def _rows_call(body, *, rows, tr, ins, outs, accs=(), name, after=None):
    n = rows // tr
    assert n * tr == rows
    if after is not None:
        body = _skip_ref(body, len(ins))
        ins = list(ins) + [("full", _deps(after))]

    def spec(x):
        if isinstance(x, tuple):
            shp = x[1].shape
            return pl.BlockSpec(shp, lambda i: (0,) * len(shp))
        return pl.BlockSpec((tr, x.shape[1]), lambda i: (i, 0))

    in_specs = [spec(x) for x in ins]
    args = [x[1] if isinstance(x, tuple) else x for x in ins]
    out_specs = [spec(x) for x in outs] + [pl.BlockSpec(x.shape, lambda i: (0,) * len(x.shape)) for x in accs]
    out_shape = [x[1] if isinstance(x, tuple) else x for x in outs] + list(accs)
    return pl.pallas_call(
        body, grid=(n,), in_specs=in_specs, out_specs=out_specs, out_shape=out_shape, name=name,
        compiler_params=_cparams(("arbitrary",)),
    )(*args)


def _acc_rows(ref, val):
    @pl.when(pl.program_id(0) == 0)
    def _():
        ref[...] = jnp.zeros_like(ref)

    ref[0:1, :] += val


def _rms_fwd(h, w, name):
    def body(h_ref, w_ref, o_ref):
        o_ref[...] = _rms_fn(h_ref[...], w_ref[...]).astype(o_ref.dtype)

    R = h.shape[0]
    return _rows_call(body, rows=R, tr=_pick(R, (256, 128)), ins=[h, ("full", w)],
                      outs=[jax.ShapeDtypeStruct(h.shape, BF16)], name=name)[0]


def _rms_bwd(h, w, dn, dres, name, after=None):
    def body(h_ref, w_ref, dn_ref, dres_ref, dh_ref, dw_ref):
        _, vjp = jax.vjp(_rms_fn, h_ref[...], w_ref[...])
        dh, dw = vjp(dn_ref[...].astype(F32))
        dh_ref[...] = dh + dres_ref[...]
        _acc_rows(dw_ref, dw)

    R = h.shape[0]
    return _rows_call(body, rows=R, tr=_pick(R, (256, 128)), ins=[h, ("full", w), dn, dres],
                      outs=[jax.ShapeDtypeStruct(h.shape, F32)], accs=[jax.ShapeDtypeStruct((8, D_MODEL), F32)], name=name,
                      after=after)


def _rms_bwd_tokens(h, w, dn, dres, nseq, name):
    Tp = h.shape[0] // nseq
    nc = Tp // CHUNK

    def body(h_ref, w_ref, dn_ref, dres_ref, dx_ref, dm_ref, dw_ref):
        b, c = pl.program_id(0), pl.program_id(1)
        _, vjp = jax.vjp(_rms_fn, h_ref[...], w_ref[...])
        dh, dw = vjp(dn_ref[...].astype(F32))
        dh = dh + dres_ref[...]

        @pl.when(c == 0)
        def _():
            dm_ref[...] = dh

        @pl.when(c > 0)
        def _():
            dx_ref[...] = dh

        @pl.when((b == 0) & (c == 0))
        def _():
            dw_ref[...] = jnp.zeros_like(dw_ref)

        dw_ref[0:1, :] += dw

    rows = pl.BlockSpec((CHUNK, D_MODEL), lambda b, c: (b * nc + c, 0))
    return pl.pallas_call(
        body, grid=(nseq, nc),
        in_specs=[rows, pl.BlockSpec((1, D_MODEL), lambda b, c: (0, 0)), rows, rows],
        out_specs=[pl.BlockSpec((None, CHUNK, D_MODEL), lambda b, c: (b, jnp.maximum(c - 1, 0), 0)),
                   pl.BlockSpec((None, CHUNK, D_MODEL), lambda b, c: (b, 0, 0)),
                   pl.BlockSpec((8, D_MODEL), lambda b, c: (0, 0))],
        out_shape=[jax.ShapeDtypeStruct((nseq, Tp - CHUNK, D_MODEL), F32), jax.ShapeDtypeStruct((nseq, CHUNK, D_MODEL), F32),
                   jax.ShapeDtypeStruct((8, D_MODEL), F32)],
        name=name, compiler_params=_cparams(("arbitrary", "arbitrary")),
    )(h, w, dn, dres)


def _gu_swiglu(n, w_gu, name):
    R = n.shape[0]
    G, _, ng = w_gu.shape

    def body(n_ref, w_ref, gu_ref, a_ref):
        x = n_ref[...]
        for r in range(G):
            gu_ref[:, ng * r:ng * (r + 1)] = _dg(x, w_ref[r], 1, 0).astype(gu_ref.dtype)
        a_ref[...] = _swiglu_fn(gu_ref[...]).astype(a_ref.dtype)

    return _rows_call(body, rows=R, tr=_pick(R, (256, 128)), ins=[n, ("full", w_gu)],
                      outs=[jax.ShapeDtypeStruct((R, 2 * D_FF), BF16), jax.ShapeDtypeStruct((R, D_FF), BF16)], name=name)


def _d_swiglu(dout, w_down, gu, alpha, name):
    R = gu.shape[0]

    def body(do_ref, w_ref, gu_ref, o_ref):
        da = _dg(do_ref[...] * alpha, w_ref[...], 1, 1)
        g = gu_ref[:, :D_FF].astype(F32)
        u = gu_ref[:, D_FF:].astype(F32)
        s = jax.nn.sigmoid(g)
        t = g * s
        o_ref[:, :D_FF] = (da * u * (s + t - t * s)).astype(o_ref.dtype)
        o_ref[:, D_FF:] = (da * t).astype(o_ref.dtype)

    return _rows_call(body, rows=R, tr=_pick(R, (256, 128)), ins=[dout, ("full", w_down), gu],
                      outs=[jax.ShapeDtypeStruct(gu.shape, BF16)], name=name)[0]


def _merge_fwd(pa, pb, gates, name):
    def body(pa_ref, pb_ref, g_ref, o_ref):
        o_ref[...] = _merge_fn(pa_ref[...], pb_ref[...], g_ref[...].astype(F32)).astype(o_ref.dtype)

    R = pa.shape[0]
    return _rows_call(body, rows=R, tr=_pick(R, (256, 128)), ins=[pa, pb, gates],
                      outs=[jax.ShapeDtypeStruct(pa.shape, BF16)], name=name)[0]


def _merge_bwd(pa, pb, gates, dm, name):
    def body(pa_ref, pb_ref, g_ref, dm_ref, dpa_ref, dpb_ref, dg_ref):
        _, vjp = jax.vjp(_merge_fn, pa_ref[...], pb_ref[...], g_ref[...].astype(F32))
        dpa, dpb, dg = vjp(dm_ref[...].astype(F32))
        dpa_ref[...] = dpa.astype(dpa_ref.dtype)
        dpb_ref[...] = dpb.astype(dpb_ref.dtype)
        dg_ref[...] = dg.astype(dg_ref.dtype)

    R = pa.shape[0]
    return _rows_call(body, rows=R, tr=_pick(R, (256, 128)), ins=[pa, pb, gates, dm],
                      outs=[jax.ShapeDtypeStruct(pa.shape, BF16), jax.ShapeDtypeStruct(pa.shape, BF16),
                            jax.ShapeDtypeStruct(gates.shape, BF16)], name=name)


def _loss_head(h3, w, target, nseq, name):
    Tp = h3.shape[0] // nseq
    nc = Tp // CHUNK

    def fn(h, w_, t, valid):
        y = _rms_fn(h, w_)
        e = (y - t) * valid
        return 0.5 * jnp.sum(jnp.mean(e * e, axis=-1, keepdims=True))

    def body(h_ref, w_ref, t_ref, loss_ref, dh_ref, dw_ref):
        b, c = pl.program_id(0), pl.program_id(1)
        valid = (c >= 1).astype(F32)
        t = t_ref[...]
        loss, vjp = jax.vjp(lambda h, w_: fn(h, w_, t, valid), h_ref[...], w_ref[...])
        dh, dw = vjp(jnp.ones((), F32))
        dh_ref[...] = dh

        @pl.when((b == 0) & (c == 0))
        def _():
            loss_ref[...] = jnp.zeros_like(loss_ref)
            dw_ref[...] = jnp.zeros_like(dw_ref)

        loss_ref[...] += jnp.full(loss_ref.shape, loss, F32)
        dw_ref[0:1, :] += dw

    return pl.pallas_call(
        body, grid=(nseq, nc),
        in_specs=[pl.BlockSpec((CHUNK, D_MODEL), lambda b, c: (b * nc + c, 0)),
                  pl.BlockSpec((1, D_MODEL), lambda b, c: (0, 0)),
                  pl.BlockSpec((None, CHUNK, D_MODEL), lambda b, c: (b, jnp.maximum(c - 1, 0), 0))],
        out_specs=[pl.BlockSpec((8, 128), lambda b, c: (0, 0)),
                   pl.BlockSpec((CHUNK, D_MODEL), lambda b, c: (b * nc + c, 0)),
                   pl.BlockSpec((8, D_MODEL), lambda b, c: (0, 0))],
        out_shape=[jax.ShapeDtypeStruct((8, 128), F32), jax.ShapeDtypeStruct(h3.shape, F32),
                   jax.ShapeDtypeStruct((8, D_MODEL), F32)],
        name=name, compiler_params=_cparams(("arbitrary", "arbitrary")),
    )(h3, w, target)


CONV_TILE = 512
CONV_HALO = 8


def _conv_fwd(xbc, w, b, pad, name):
    B, Tp, C = xbc.shape
    nch = Tp // CHUNK

    def body(x_ref, w_ref, b_ref, o_ref, xp):
        xp[0:CONV_HALO, :] = jnp.zeros((CONV_HALO, CONV_TILE), F32)
        xp[CONV_HALO:, :] = x_ref[...]
        for c in range(nch):
            acc = jnp.zeros((CHUNK, CONV_TILE), F32) + b_ref[...]
            for k in range(SSD_CONV):
                acc = acc + w_ref[k:k + 1, :] * xp[pl.ds(CONV_HALO + CHUNK * c - (SSD_CONV - 1) + k, CHUNK), :]
            row = CHUNK * c + lax.broadcasted_iota(jnp.int32, (CHUNK, 1), 0)
            o_ref[pl.ds(CHUNK * c, CHUNK), :] = jnp.where(row >= pad, _silu(acc), 0.0)

    return pl.pallas_call(
        body, grid=(B, C // CONV_TILE),
        in_specs=[pl.BlockSpec((None, Tp, CONV_TILE), lambda i, j: (i, 0, j)),
                  pl.BlockSpec((SSD_CONV, CONV_TILE), lambda i, j: (0, j)),
                  pl.BlockSpec((1, CONV_TILE), lambda i, j: (0, j))],
        out_specs=pl.BlockSpec((None, Tp, CONV_TILE), lambda i, j: (i, 0, j)),
        out_shape=jax.ShapeDtypeStruct(xbc.shape, F32),
        scratch_shapes=[pltpu.VMEM((Tp + CONV_HALO, CONV_TILE), F32)],
        name=name, compiler_params=_cparams(("arbitrary", "arbitrary")),
    )(xbc, w, b)


def _conv_bwd(xbc, w, b, dact, pad, name):
    B, Tp, C = xbc.shape
    nch = Tp // CHUNK

    def body(x_ref, w_ref, b_ref, da_ref, dx_ref, dw_ref, db_ref, xp, dp):
        bi = pl.program_id(1)
        xp[0:CONV_HALO, :] = jnp.zeros((CONV_HALO, CONV_TILE), F32)
        xp[CONV_HALO:, :] = x_ref[...]
        dp[pl.ds(Tp, CONV_HALO), :] = jnp.zeros((CONV_HALO, CONV_TILE), F32)
        dws = [jnp.zeros((1, CONV_TILE), F32) for _ in range(SSD_CONV)]
        dbs = jnp.zeros((1, CONV_TILE), F32)
        for c in range(nch):
            xs = [xp[pl.ds(CONV_HALO + CHUNK * c - (SSD_CONV - 1) + k, CHUNK), :] for k in range(SSD_CONV)]
            acc = jnp.zeros((CHUNK, CONV_TILE), F32) + b_ref[...]
            for k in range(SSD_CONV):
                acc = acc + w_ref[k:k + 1, :] * xs[k]
            row = CHUNK * c + lax.broadcasted_iota(jnp.int32, (CHUNK, 1), 0)
            sg = jax.nn.sigmoid(acc)
            dpre = jnp.where(row >= pad, da_ref[pl.ds(CHUNK * c, CHUNK), :] * (sg * (1.0 + acc * (1.0 - sg))), 0.0)
            dp[pl.ds(CHUNK * c, CHUNK), :] = dpre
            dbs = dbs + jnp.sum(dpre, axis=0, keepdims=True)
            for k in range(SSD_CONV):
                dws[k] = dws[k] + jnp.sum(dpre * xs[k], axis=0, keepdims=True)
        for c in range(nch):
            acc = jnp.zeros((CHUNK, CONV_TILE), F32)
            for k in range(SSD_CONV):
                acc = acc + w_ref[k:k + 1, :] * dp[pl.ds(CHUNK * c + (SSD_CONV - 1) - k, CHUNK), :]
            dx_ref[pl.ds(CHUNK * c, CHUNK), :] = acc.astype(dx_ref.dtype)

        @pl.when(bi == 0)
        def _():
            dw_ref[...] = jnp.zeros_like(dw_ref)
            db_ref[...] = jnp.zeros_like(db_ref)

        for k in range(SSD_CONV):
            dw_ref[k:k + 1, :] += dws[k]
        db_ref[0:1, :] += dbs

    return pl.pallas_call(
        body, grid=(C // CONV_TILE, B),
        in_specs=[pl.BlockSpec((None, Tp, CONV_TILE), lambda j, i: (i, 0, j)),
                  pl.BlockSpec((SSD_CONV, CONV_TILE), lambda j, i: (0, j)),
                  pl.BlockSpec((1, CONV_TILE), lambda j, i: (0, j)),
                  pl.BlockSpec((None, Tp, CONV_TILE), lambda j, i: (i, 0, j))],
        out_specs=[pl.BlockSpec((None, Tp, CONV_TILE), lambda j, i: (i, 0, j)),
                   pl.BlockSpec((8, CONV_TILE), lambda j, i: (0, j)),
                   pl.BlockSpec((8, CONV_TILE), lambda j, i: (0, j))],
        out_shape=[jax.ShapeDtypeStruct(xbc.shape, BF16), jax.ShapeDtypeStruct((8, C), F32),
                   jax.ShapeDtypeStruct((8, C), F32)],
        scratch_shapes=[pltpu.VMEM((Tp + CONV_HALO, CONV_TILE), F32), pltpu.VMEM((Tp + CONV_HALO, CONV_TILE), F32)],
        name=name, compiler_params=_cparams(("arbitrary", "arbitrary")),
    )(xbc, w, b, dact)


def _ssd_chunk(xs, bm, cm, dtr, z, state, dt_bias, a_log, dskip, norm_w, valid):
    Q = xs.shape[0]
    lane = lax.broadcasted_iota(jnp.int32, (1, 128), 1)
    dt = jnp.where(lane < SSD_HEADS, _softplus(dtr + dt_bias), 0.0) * valid
    a = dt * (-jnp.exp(a_log))
    tril = _tril(Q)
    cs = _cumsum_rows(a)
    cs_t = cs.T
    cs_end = _row_of(cs, Q - 1)
    low = lane < SSD_HEAD_DIM
    low_rows = lax.broadcasted_iota(jnp.int32, (128, 1), 0) < SSD_HEAD_DIM
    ys, new_state = [], []
    for g in range(SSD_GROUPS):
        bg = bm[:, 128 * g:128 * (g + 1)]
        cg = cm[:, 128 * g:128 * (g + 1)]
        cb = _mm_nt(cg, bg)
        for pr in range(2):
            p = 2 * g + pr
            h0, h1 = 2 * p, 2 * p + 1
            xp = xs[:, 128 * p:128 * (p + 1)]
            c0, c1 = _col_of(cs, h0), _col_of(cs, h1)
            e0, e1 = _col_of(cs_end, h0), _col_of(cs_end, h1)
            xd = xp * jnp.where(low, _col_of(dt, h0), _col_of(dt, h1))
            l0 = jnp.exp(jnp.where(tril, c0 - _row_of(cs_t, h0), -1e30))
            l1 = jnp.exp(jnp.where(tril, c1 - _row_of(cs_t, h1), -1e30))
            y_diag = jnp.where(low, _mm(cb * l0, xd), _mm(cb * l1, xd))
            to_end = jnp.where(low, jnp.exp(e0 - c0), jnp.exp(e1 - c1))
            sp = state[128 * p:128 * (p + 1), :]
            y_off = _mm_nt(cg, sp) * jnp.where(low, jnp.exp(c0), jnp.exp(c1))
            new_state.append(sp * jnp.where(low_rows, jnp.exp(e0), jnp.exp(e1)) + _mm_tn(xd * to_end, bg))
            ys.append(y_diag + y_off + xp * jnp.where(low, _col_of(dskip, h0), _col_of(dskip, h1)))
    y = jnp.concatenate(ys, axis=1) * _silu(z)
    gw = SSD_INNER // SSD_GROUPS
    outs = []
    for g in range(SSD_GROUPS):
        blk = y[:, gw * g:gw * (g + 1)]
        outs.append(blk * lax.rsqrt(jnp.mean(blk * blk, axis=-1, keepdims=True) + EPS))
    return jnp.concatenate(outs, axis=1) * norm_w, jnp.concatenate(new_state, axis=0)


def _valid_rows(c, pad):
    row = c * CHUNK + lax.broadcasted_iota(jnp.int32, (CHUNK, 1), 0)
    return (row >= pad).astype(F32)


def _ssd_fwd(xact, dtr, z, dt_bias, a_log, dskip, norm_w, pad, name):
    B, Tp, _ = xact.shape
    nc = Tp // CHUNK

    def body(xs_ref, bm_ref, cm_ref, dt_ref, z_ref, db_ref, al_ref, ds_ref, nw_ref, y_ref, save_ref, st):
        c = pl.program_id(1)

        @pl.when(c == 0)
        def _():
            st[...] = jnp.zeros_like(st)

        s0 = st[...]
        save_ref[...] = s0
        y, s1 = _ssd_chunk(xs_ref[...], bm_ref[...], cm_ref[...], dt_ref[...], z_ref[...].astype(F32), s0, db_ref[...],
                           al_ref[...], ds_ref[...], nw_ref[...], _valid_rows(c, pad))
        y_ref[...] = y.astype(y_ref.dtype)
        st[...] = s1

    row = lambda w, off=0: pl.BlockSpec((None, CHUNK, w), lambda b, c: (b, c, off))
    par = lambda w: pl.BlockSpec((1, w), lambda b, c: (0, 0))
    return pl.pallas_call(
        body, grid=(B, nc),
        in_specs=[row(1024, 0), row(512, 2), row(512, 3), row(128), row(1024), par(128), par(128), par(128), par(1024)],
        out_specs=[row(1024), pl.BlockSpec((None, None, 1024, 128), lambda b, c: (b, c, 0, 0))],
        out_shape=[jax.ShapeDtypeStruct((B, Tp, SSD_INNER), BF16), jax.ShapeDtypeStruct((B, nc, 1024, 128), F32)],
        scratch_shapes=[pltpu.VMEM((1024, 128), F32)],
        name=name, compiler_params=_cparams(("arbitrary", "arbitrary")),
    )(xact, xact, xact, dtr, z, dt_bias, a_log, dskip, norm_w)


def _ssd_bwd(xact, dtr, z, dt_bias, a_log, dskip, norm_w, saved, dy, pad, name, after=None):
    B, Tp, _ = xact.shape
    nc = Tp // CHUNK

    def body(xs_ref, bm_ref, cm_ref, dt_ref, z_ref, db_ref, al_ref, ds_ref, nw_ref, sv_ref, dy_ref,
             dx_ref, ddt_ref, dz_ref, dpar_ref, dnw_ref, dst):
        b, i = pl.program_id(0), pl.program_id(1)
        c = nc - 1 - i

        @pl.when(i == 0)
        def _():
            dst[...] = jnp.zeros_like(dst)

        valid = _valid_rows(c, pad)
        fn = lambda *a: _ssd_chunk(*a, valid)
        _, vjp = jax.vjp(fn, xs_ref[...], bm_ref[...], cm_ref[...], dt_ref[...], z_ref[...].astype(F32), sv_ref[...],
                         db_ref[...], al_ref[...], ds_ref[...], nw_ref[...])
        dxs, dbm, dcm, ddt, dz, dstate, ddb, dal, dds, dnw = vjp((dy_ref[...].astype(F32), dst[...]))
        dx_ref[:, 0:1024] = dxs
        dx_ref[:, 1024:1536] = dbm
        dx_ref[:, 1536:2048] = dcm
        ddt_ref[...] = ddt
        dz_ref[...] = dz.astype(dz_ref.dtype)
        dst[...] = dstate

        @pl.when((b == 0) & (i == 0))
        def _():
            dpar_ref[...] = jnp.zeros_like(dpar_ref)
            dnw_ref[...] = jnp.zeros_like(dnw_ref)

        dpar_ref[0:1, :] += ddb
        dpar_ref[1:2, :] += dal
        dpar_ref[2:3, :] += dds
        dnw_ref[0:1, :] += dnw

    row = lambda w, off=0: pl.BlockSpec((None, CHUNK, w), lambda b, i: (b, nc - 1 - i, off))
    par = lambda w: pl.BlockSpec((1, w), lambda b, i: (0, 0))
    acc = lambda w: pl.BlockSpec((8, w), lambda b, i: (0, 0))
    in_specs = [row(1024, 0), row(512, 2), row(512, 3), row(128), row(1024), par(128), par(128), par(128), par(1024),
                pl.BlockSpec((None, None, 1024, 128), lambda b, i: (b, nc - 1 - i, 0, 0)), row(1024)]
    args = [xact, xact, xact, dtr, z, dt_bias, a_log, dskip, norm_w, saved, dy]
    if after is not None:
        body = _skip_ref(body, len(args))
        args.append(_deps(after))
        in_specs.append(_dep_spec(args[-1]))
    outs = pl.pallas_call(
        body, grid=(B, nc), in_specs=in_specs,
        out_specs=[row(2048), row(128), row(1024), acc(128), acc(1024)],
        out_shape=[jax.ShapeDtypeStruct((B, Tp, 2048), F32), jax.ShapeDtypeStruct((B, Tp, 128), F32),
                   jax.ShapeDtypeStruct((B, Tp, 1024), BF16), jax.ShapeDtypeStruct((8, 128), F32),
                   jax.ShapeDtypeStruct((8, 1024), F32)],
        scratch_shapes=[pltpu.VMEM((1024, 128), F32)],
        name=name, compiler_params=_cparams(("arbitrary", "arbitrary")),
    )(*args)
    return outs


def _hg_chunk(qr, fr, ir, gr, state_t, p0, p1, norm_w, valid):
    Q = qr.shape[0]
    lb = jax.nn.sigmoid(p0 - p1)
    f = lb + (1.0 - lb) * jax.nn.sigmoid(fr)
    k = 1.0 - f
    q = _silu(qr)
    v = ir * valid
    cum = _cumsum_rows(jnp.log(f))
    cum_end = _row_of(cum, Q - 1)
    o_inter = _mm_nt(q * jnp.exp(cum), state_t)
    nblk = Q // HG_SUB
    row = lax.broadcasted_iota(jnp.int32, (Q, 1), 0)
    ri = lax.broadcasted_iota(jnp.int32, (Q, Q), 0)
    ci = lax.broadcasted_iota(jnp.int32, (Q, Q), 1)
    mids = jnp.concatenate([jnp.broadcast_to(_row_of(cum, HG_SUB * i + HG_SUB // 2 - 1), (HG_SUB, cum.shape[1]))
                            for i in range(nblk)], axis=0)
    sh = HG_SUB.bit_length() - 1
    same = (jnp.right_shift(ri, sh) == jnp.right_shift(ci, sh)) & (ri >= ci)
    att = jnp.where(same, _mm_nt(q * jnp.exp(cum - mids), k * jnp.exp(mids - cum)), 0.0)
    for i in range(1, nblk):
        lo = HG_SUB * i
        start = _row_of(cum, lo - 1)
        qa = q * jnp.exp(jnp.where((row >= lo) & (row < lo + HG_SUB), cum - start, -1e30))
        ka = k * jnp.exp(jnp.where(row < lo, start - cum, -1e30))
        att = att + _mm_nt(qa, ka)
    o = o_inter + _mm(att, v)
    new_state_t = state_t * jnp.exp(cum_end) + _mm_tn(v, k * jnp.exp(cum_end - cum))
    o = o * lax.rsqrt(jnp.mean(o * o, axis=-1, keepdims=True) + EPS) * norm_w
    return o * _silu(gr), new_state_t


HG_PER_STEP = 8
HG_COLS = 4 * 128


def _hg_fwd(qfig, lbh, nwh, pad, name):
    B, Tp, _ = qfig.shape
    nc = Tp // CHUNK
    hp = HG_PER_STEP

    def body(x_ref, lb_ref, nw_ref, y_ref, save_ref, st):
        c = pl.program_id(1)

        @pl.when(c == 0)
        def _():
            st[...] = jnp.zeros_like(st)

        valid = _valid_rows(c, pad)
        for j in range(hp):
            for b in range(B):
                s0 = st[j, b]
                save_ref[j, b] = s0
                col = lambda k: x_ref[b, :, HG_COLS * j + 128 * k:HG_COLS * j + 128 * (k + 1)]
                y, s1 = _hg_chunk(col(0), col(1), col(2), col(3), s0, lb_ref[j, 0:1, :], lb_ref[j, 1:2, :], nw_ref[j], valid)
                y_ref[b, :, 128 * j:128 * (j + 1)] = y.astype(y_ref.dtype)
                st[j, b] = s1

    return pl.pallas_call(
        body, grid=(HG_HEADS // hp, nc),
        in_specs=[pl.BlockSpec((B, CHUNK, HG_COLS * hp), lambda h, c: (0, c, h)),
                  pl.BlockSpec((hp, 2, 128), lambda h, c: (h, 0, 0)),
                  pl.BlockSpec((hp, 1, 128), lambda h, c: (h, 0, 0))],
        out_specs=[pl.BlockSpec((B, CHUNK, 128 * hp), lambda h, c: (0, c, h)),
                   pl.BlockSpec((hp, B, None, 128, 128), lambda h, c: (h, 0, c, 0, 0))],
        out_shape=[jax.ShapeDtypeStruct((B, Tp, 1024), BF16), jax.ShapeDtypeStruct((HG_HEADS, B, nc, 128, 128), F32)],
        scratch_shapes=[pltpu.VMEM((hp, B, 128, 128), F32)],
        name=name, compiler_params=_cparams(("arbitrary", "arbitrary")),
    )(qfig, lbh, nwh)


def _hg_bwd(qfig, lbh, nwh, saved, dy, pad, name, after=None):
    B, Tp, _ = qfig.shape
    nc = Tp // CHUNK
    hp = HG_PER_STEP

    def body(x_ref, lb_ref, nw_ref, sv_ref, dy_ref, dx_ref, dlb_ref, dnw_ref, dst):
        i = pl.program_id(1)
        c = nc - 1 - i

        @pl.when(i == 0)
        def _():
            dst[...] = jnp.zeros_like(dst)
            dlb_ref[...] = jnp.zeros_like(dlb_ref)
            dnw_ref[...] = jnp.zeros_like(dnw_ref)

        valid = _valid_rows(c, pad)
        fn = lambda *a: _hg_chunk(*a, valid)
        for j in range(hp):
            for b in range(B):
                col = lambda k: x_ref[b, :, HG_COLS * j + 128 * k:HG_COLS * j + 128 * (k + 1)]
                _, vjp = jax.vjp(fn, col(0), col(1), col(2), col(3), sv_ref[j, b], lb_ref[j, 0:1, :], lb_ref[j, 1:2, :], nw_ref[j])
                d4 = vjp((dy_ref[b, :, 128 * j:128 * (j + 1)].astype(F32), dst[j, b]))
                for k in range(4):
                    dx_ref[b, :, HG_COLS * j + 128 * k:HG_COLS * j + 128 * (k + 1)] = d4[k].astype(dx_ref.dtype)
                dst[j, b] = d4[4]
                dlb_ref[j, 0:1, :] += d4[5]
                dlb_ref[j, 1:2, :] += d4[6]
                dnw_ref[j, 0:1, :] += d4[7]

    acc = pl.BlockSpec((hp, 8, 128), lambda h, i: (h, 0, 0))
    in_specs = [pl.BlockSpec((B, CHUNK, HG_COLS * hp), lambda h, i: (0, nc - 1 - i, h)),
                pl.BlockSpec((hp, 2, 128), lambda h, i: (h, 0, 0)),
                pl.BlockSpec((hp, 1, 128), lambda h, i: (h, 0, 0)),
                pl.BlockSpec((hp, B, None, 128, 128), lambda h, i: (h, 0, nc - 1 - i, 0, 0)),
                pl.BlockSpec((B, CHUNK, 128 * hp), lambda h, i: (0, nc - 1 - i, h))]
    args = [qfig, lbh, nwh, saved, dy]
    if after is not None:
        body = _skip_ref(body, len(args))
        args.append(_deps(after))
        in_specs.append(_dep_spec(args[-1]))
    return pl.pallas_call(
        body, grid=(HG_HEADS // hp, nc), in_specs=in_specs,
        out_specs=[pl.BlockSpec((B, CHUNK, HG_COLS * hp), lambda h, i: (0, nc - 1 - i, h)), acc, acc],
        out_shape=[jax.ShapeDtypeStruct((B, Tp, 4096), BF16), jax.ShapeDtypeStruct((HG_HEADS, 8, 128), F32),
                   jax.ShapeDtypeStruct((HG_HEADS, 8, 128), F32)],
        scratch_shapes=[pltpu.VMEM((hp, B, 128, 128), F32)],
        name=name, compiler_params=_cparams(("arbitrary", "arbitrary")),
    )(*args)


def _adamw_math(w, g, m, v):
    m = ADAM_B1 * m + (1.0 - ADAM_B1) * g
    v = ADAM_B2 * v + (1.0 - ADAM_B2) * (g * g)
    m_hat = m / (1.0 - ADAM_B1 ** ADAM_STEP)
    v_hat = v / (1.0 - ADAM_B2 ** ADAM_STEP)
    return -ADAM_LR * (m_hat / (jnp.sqrt(v_hat) + ADAM_EPS) + ADAM_WD * w), m, v


def _adamw_many(ws, gs, ms, vs, name):
    n = len(ws)

    def body(*refs):
        for i in range(n):
            d, m, v = _adamw_math(refs[i][...], refs[n + i][...], refs[2 * n + i][...], refs[3 * n + i][...])
            refs[4 * n + i][...] = d
            refs[5 * n + i][...] = m
            refs[6 * n + i][...] = v

    vm = pl.BlockSpec(memory_space=pltpu.VMEM)
    outs = pl.pallas_call(body, in_specs=[vm] * (4 * n), out_specs=[vm] * (3 * n),
                          out_shape=[jax.ShapeDtypeStruct(w.shape, F32) for w in ws] * 3, name=name)(*ws, *gs, *ms, *vs)
    return outs[:n], outs[n:2 * n], outs[2 * n:]


def _adamw(w, g, m, v, name, after=None):
    R, C = w.shape
    tr = max(t for t in range(8, R + 1, 8) if R % t == 0 and (t * C * 4 <= ADAMW_BLOCK_BYTES or t == 8))

    def body(w_ref, g_ref, m_ref, v_ref, d_ref, mo_ref, vo_ref):
        d_ref[...], mo_ref[...], vo_ref[...] = _adamw_math(w_ref[...], g_ref[...], m_ref[...], v_ref[...])

    sp = pl.BlockSpec((tr, C), lambda i: (i, 0))
    sh = jax.ShapeDtypeStruct((R, C), F32)
    in_specs, args = [sp] * 4, [w, g, m, v]
    if after is not None:
        body = _skip_ref(body, len(args))
        args.append(_deps(after))
        in_specs.append(_dep_spec(args[-1]))
    return pl.pallas_call(body, grid=(R // tr,), in_specs=in_specs, out_specs=[sp] * 3, out_shape=[sh] * 3,
                          name=name, compiler_params=_cparams(("arbitrary",)))(*args)


def _ffn_fwd(h, norm_w, w_gu, w_down, tag, after_norm=None):
    n = _rms_fwd(h, norm_w, f"{tag}_norm")
    if after_norm is not None:
        after_norm(n)
    gu, a = _gu_swiglu(n, w_gu, f"{tag}_gu")
    out = _matmul(a, w_down, mode="nn", out_dtype=F32, alpha=0.5, res=h, name=f"{tag}_down")
    return out, (n, gu, a)


def _ffn_bwd(h, norm_w, w_gu, w_down, saved, dout, tag, after_dw_down=None, token_seqs=None):
    n, gu, a = saved
    dgu = _d_swiglu(dout, w_down, gu, 0.5, f"{tag}_d_gu")
    dw_down = _matmul(a, dout, mode="tn", out_dtype=F32, alpha=0.5, name=f"{tag}_dw_down")
    dw_gu = _matmul(n, dgu, mode="tn", out_dtype=F32, out_groups=N_CHIPS, name=f"{tag}_dw_gu",
                    after=after_dw_down(dw_down) if after_dw_down else None)
    dn = _matmul(dgu, w_gu, mode="nt", out_dtype=F32, name=f"{tag}_d_norm", after=dw_gu)
    if token_seqs is None:
        dh, dnw = _rms_bwd(h, norm_w, dn, dout, f"{tag}_d_in")
    else:
        dx, dm, dnw = _rms_bwd_tokens(h, norm_w, dn, dout, token_seqs, f"{tag}_d_in")
        dh = (dx, dm)
    return dh, dnw, dw_gu, dw_down


def _split_w_in(w_in_full):
    pts = [0]
    for s in IN_SIZES:
        pts.append(pts[-1] + s)
    sl = lambda i, j: w_in_full[:, pts[i]:pts[j]]
    qfig = sl(3, 7).reshape(D_MODEL, 4, HG_HEADS, 128).transpose(0, 2, 1, 3).reshape(D_MODEL, 4 * D_MODEL)
    return {"z": sl(0, 1), "xbc": sl(1, 2), "dt": jnp.pad(sl(2, 3), ((0, 0), (0, 128 - SSD_HEADS))),
            "qfig": qfig, "gates": sl(7, 9)}


def _local_step(x, target, W):
    B, S, _ = x.shape
    T = N_META + S
    pad = (-T) % CHUNK
    Tp = T + pad
    assert pad + N_META == CHUNK
    R = B * Tp
    meta = jnp.broadcast_to(W["meta_tokens"][None], (B, N_META, D_MODEL))
    h0 = jnp.concatenate([jnp.zeros((B, pad, D_MODEL), F32), meta, x], axis=1).reshape(R, D_MODEL)

    stage = W.get("_stage", lambda name, x: {})
    W = dict(W)
    h1, sv1 = _ffn_fwd(h0, W["ffn1_norm"], W["ffn1_w_gu"], W["ffn1_w_down"], "ffn1", lambda n: W.update(stage("ffn1_norm", n)))
    W.update(stage("ffn1_out", h1))
    um = _rms_fwd(h1, W["mix_norm"], "mix_norm")
    wi = W["w_in"]
    z = _matmul(um, wi["z"], mode="nn", out_dtype=BF16, name="in_z")
    xbc = _matmul(um, wi["xbc"], mode="nn", out_dtype=F32, name="in_xbc")
    dtr = _matmul(um, wi["dt"], mode="nn", out_dtype=F32, name="in_dt")
    qfig = _matmul(um, wi["qfig"], mode="nn", out_dtype=F32, name="in_qfig")
    gates = _matmul(um, wi["gates"], mode="nn", out_dtype=BF16, name="in_gates")

    r3 = lambda t: t.reshape(B, Tp, t.shape[-1])
    lane_pad = lambda t: jnp.pad(t, ((0, 0), (0, 128 - t.shape[1])))
    dt_bias, a_log, dskip = lane_pad(W["ssd_dt_bias"]), lane_pad(W["ssd_a_log"]), lane_pad(W["ssd_d"])
    xact = _conv_fwd(r3(xbc), W["ssd_conv_w"], W["ssd_conv_b"], pad, "conv_fwd")
    ya, ssd_saved = _ssd_fwd(xact, r3(dtr), r3(z), dt_bias, a_log, dskip, W["ssd_norm"], pad, "ssd_fwd")
    lbh = W["hg_lower_bound"].reshape(2, HG_HEADS, 128).transpose(1, 0, 2)
    nwh = W["hg_norm"].reshape(HG_HEADS, 1, 128)
    yb, hg_saved = _hg_fwd(r3(qfig), lbh, nwh, pad, "hg_fwd")
    ya2, yb2 = ya.reshape(R, -1), yb.reshape(R, -1)
    W.update(stage("mixers_out", yb2))
    pa = _matmul(ya2, W["w_branch_a"], mode="nn", out_dtype=F32, name="branch_a")
    pb = _matmul(yb2, W["w_branch_b"], mode="nn", out_dtype=F32, name="branch_b")
    mg = _merge_fwd(pa, pb, gates, "merge")
    h2 = _matmul(mg, W["w_out"], mode="nn", out_dtype=F32, res=h1, name="mix_out")
    h3, sv2 = _ffn_fwd(h2, W["ffn2_norm"], W["ffn2_w_gu"], W["ffn2_w_down"], "ffn2")

    loss, dh3, d_final = _loss_head(h3, W["final_norm"].reshape(1, D_MODEL), target, B, "loss_head")

    G = {"final_norm": d_final[0]}
    dh2, dnw, G["ffn2_w_gu"], G["ffn2_w_down"] = _ffn_bwd(h2, W["ffn2_norm"], W["ffn2_w_gu"], W["ffn2_w_down"], sv2, dh3, "ffn2")
    G["ffn2_norm"] = dnw[0:1]
    dmg = _matmul(dh2, W["w_out"], mode="nt", out_dtype=BF16, name="d_merge")
    G["w_out"] = _matmul(mg, dh2, mode="tn", out_dtype=F32, name="dw_out")
    dpa, dpb, dgates = _merge_bwd(pa, pb, gates, dmg, "merge_bwd")
    dya = _matmul(dpa, W["w_branch_a"], mode="nt", out_dtype=BF16, name="d_ya")
    dyb = _matmul(dpb, W["w_branch_b"], mode="nt", out_dtype=BF16, name="d_yb")
    G["w_branch_a"] = _matmul(ya2, dpa, mode="tn", out_dtype=F32, name="dw_branch_a")
    G["w_branch_b"] = _matmul(yb2, dpb, mode="tn", out_dtype=F32, name="dw_branch_b")

    dxact, ddtr, dz, dpar, dnw = _ssd_bwd(xact, r3(dtr), r3(z), dt_bias, a_log, dskip, W["ssd_norm"], ssd_saved,
                                          r3(dya), pad, "ssd_bwd", after=stage("late_grads", G).get("_after"))
    G["ssd_dt_bias"], G["ssd_a_log"], G["ssd_d"] = dpar[0:1, :SSD_HEADS], dpar[1:2, :SSD_HEADS], dpar[2:3, :SSD_HEADS]
    G["ssd_norm"] = dnw[0:1]
    dxbc, dcw, dcb = _conv_bwd(r3(xbc), W["ssd_conv_w"], W["ssd_conv_b"], dxact, pad, "conv_bwd")
    G["ssd_conv_w"], G["ssd_conv_b"] = dcw[0:SSD_CONV], dcb[0:1]
    dqfig, dlb, dhn = _hg_bwd(r3(qfig), lbh, nwh, hg_saved, r3(dyb), pad, "hg_bwd",
                              after=stage("after_conv_bwd", dcb).get("_after"))
    G["hg_lower_bound"] = dlb[:, 0:2, :].transpose(1, 0, 2).reshape(2, D_MODEL)
    G["hg_norm"] = dhn[:, 0, :].reshape(1, D_MODEL)

    r2 = lambda t: t.reshape(R, t.shape[-1])
    pieces = [("z", r2(dz)), ("xbc", r2(dxbc)), ("dt", r2(ddtr)), ("qfig", r2(dqfig)), ("gates", dgates)]
    dum = _sum_nt([p for _, p in pieces], [wi[nm] for nm, _ in pieces], "d_mix")
    dwi = {nm: _matmul(um, dpiece, mode="tn", out_dtype=F32, name=f"dw_in_{nm}") for nm, dpiece in pieces}
    dw_qfig = dwi["qfig"].reshape(D_MODEL, HG_HEADS, 4, 128).transpose(0, 2, 1, 3).reshape(D_MODEL, 4 * D_MODEL)
    G["w_in"] = jnp.concatenate([dwi["z"], dwi["xbc"], dwi["dt"][:, :SSD_HEADS], dw_qfig, dwi["gates"]], axis=1)
    dh1, dnw = _rms_bwd(h1, W["mix_norm"], dum, dh2, "mix_norm_bwd", after=stage("w_in_grads", dwi).get("_after"))
    G["mix_norm"] = dnw[0:1]
    (dx, dfirst), dnw, G["ffn1_w_gu"], G["ffn1_w_down"] = _ffn_bwd(
        h0, W["ffn1_norm"], W["ffn1_w_gu"], W["ffn1_w_down"], sv1, dh1, "ffn1",
        lambda dw: stage("ffn1_dw_down", dw).get("_after"), token_seqs=B)
    G["ffn1_norm"] = dnw[0:1]
    G["meta_tokens"] = jnp.sum(dfirst[:, pad:CHUNK], axis=0)
    return loss, dx, G


ANY = pl.BlockSpec(memory_space=pl.ANY)


def _place():
    return lax.axis_index("x"), lax.axis_index("y"), lax.axis_index("c")


def _other_chips(x, y):
    return [(1 - x, y), (x, 1 - y), (1 - x, 1 - y)]


def _remote(src, dst, ssem, rsem, dev):
    return pltpu.make_async_remote_copy(src_ref=src, dst_ref=dst, send_sem=ssem, recv_sem=rsem,
                                        device_id=dev, device_id_type=MESH)


def _exchange8(buf, name):
    n, w = buf.shape

    def body(x_ref, out_ref, ssem, rsem):
        x, y, c = _place()
        me = 4 * x + 2 * y + c
        out_ref[me] = x_ref[...]
        copies = []
        for k in range(1, 8):
            px = 1 - x if (k >> 2) & 1 else x
            py = 1 - y if (k >> 1) & 1 else y
            pc = 1 - c if k & 1 else c
            cp = _remote(x_ref, out_ref.at[me], ssem.at[k - 1], rsem.at[k - 1], (px, py, pc))
            cp.start()
            copies.append((cp, 4 * px + 2 * py + pc))
        for k, (cp, peer) in enumerate(copies):
            _remote(x_ref, out_ref.at[peer], ssem.at[k], rsem.at[k], (x, y, c)).wait_recv()
        for cp, _ in copies:
            cp.wait_send()

    vm = pl.BlockSpec(memory_space=pltpu.VMEM)
    return pl.pallas_call(
        body, in_specs=[vm], out_specs=vm, out_shape=jax.ShapeDtypeStruct((8, n, w), F32),
        scratch_shapes=[pltpu.SemaphoreType.DMA((7,)), pltpu.SemaphoreType.DMA((7,))], name=name,
    )(buf)


HBM = pltpu.MemorySpace.HBM


def _sequencer(name, collective_id, sems, sent):
    return functools.partial(pl.kernel, mesh=plsc.ScalarSubcoreMesh(axis_name="sequencer", num_cores=1), name=name,
                             scratch_types=sems, compiler_params=pltpu.CompilerParams(collective_id=collective_id),
                             cost_estimate=pl.CostEstimate(flops=0, transcendentals=0, bytes_accessed=2 * sent,
                                                           remote_bytes_transferred=sent))


def _nbytes(arrays):
    return sum(a.size * a.dtype.itemsize for a in arrays)


def _handshake(peers):
    barrier = pltpu.get_barrier_semaphore()
    for peer in peers:
        pl.semaphore_signal(barrier, inc=1, device_id=peer, device_id_type=MESH)
    pl.semaphore_wait(barrier, len(peers))


def _gather_seq(blocks, name, collective_id):
    n = len(blocks)
    half = [s.shape[1] // 2 for s in blocks]
    full = [jax.new_ref(b, memory_space=HBM) for b in blocks]

    @_sequencer(name, collective_id, [pltpu.SemaphoreType.DMA((n, 3))] * 4, _nbytes(blocks) * 3 // 4)
    def launch(ssem, rsem, fssem, frsem):
        x, y, c = _place()
        q = 2 * x + y
        chips = _other_chips(x, y)
        _handshake([(px, py, c) for px, py in chips] + [(x, y, 1 - c)])
        piece = lambda s, qq, cc: full[s].at[qq, pl.ds(cc * half[s], half[s])]
        sends = []
        for j, (px, py) in enumerate(chips):
            for s in range(n):
                cp = _remote(piece(s, q, c), piece(s, q, c), ssem.at[s, j], rsem.at[s, j], (px, py, c))
                cp.start()
                sends.append(cp)
        for j, (px, py) in enumerate(chips):
            for s in range(n):
                got = piece(s, 2 * px + py, c)
                _remote(got, got, ssem.at[s, j], rsem.at[s, j], (px, py, c)).wait_recv()
                cp = _remote(got, got, fssem.at[s, j], frsem.at[s, j], (x, y, 1 - c))
                cp.start()
                sends.append(cp)
        for j, (px, py) in enumerate(chips):
            for s in range(n):
                got = piece(s, 2 * px + py, 1 - c)
                _remote(got, got, fssem.at[s, j], frsem.at[s, j], (x, y, 1 - c)).wait_recv()
        for cp in sends:
            cp.wait_send()

    launch()
    return [r[...] for r in full]


def _share8(buf, name, collective_id):
    n, w = buf.shape
    src = jax.new_ref(buf, memory_space=HBM)
    out = jax.empty_ref(jax.ShapeDtypeStruct((8, n, w), F32), memory_space=HBM)

    @_sequencer(name, collective_id, [pltpu.SemaphoreType.DMA((7,)), pltpu.SemaphoreType.DMA((7,)), pltpu.SemaphoreType.DMA((1,))],
                7 * buf.size * 4)
    def launch(ssem, rsem, lsem):
        x, y, c = _place()
        me = 4 * x + 2 * y + c
        peers = [(1 - x if (k >> 2) & 1 else x, 1 - y if (k >> 1) & 1 else y, 1 - c if k & 1 else c) for k in range(1, 8)]
        _handshake(peers)
        mine = pltpu.make_async_copy(src, out.at[me], lsem.at[0])
        mine.start()
        sends = []
        for k, peer in enumerate(peers):
            cp = _remote(src, out.at[me], ssem.at[k], rsem.at[k], peer)
            cp.start()
            sends.append(cp)
        for k, (px, py, pc) in enumerate(peers):
            slot = out.at[4 * px + 2 * py + pc]
            _remote(slot, slot, ssem.at[k], rsem.at[k], (px, py, pc)).wait_recv()
        for cp in sends:
            cp.wait_send()
        mine.wait()

    launch()
    return out[...]


def _sum_slots(slots, name, after=None):
    _, n, w = slots.shape

    def body(s_ref, o_ref):
        acc = s_ref[0]
        for d in range(1, 8):
            acc = acc + s_ref[d]
        o_ref[...] = acc

    vm = pl.BlockSpec(memory_space=pltpu.VMEM)
    in_specs, args = [vm], [slots]
    if after is not None:
        body = _skip_ref(body, 1)
        args.append(_deps(after))
        in_specs.append(vm)
    return pl.pallas_call(body, in_specs=in_specs, out_specs=vm, out_shape=jax.ShapeDtypeStruct((n, w), F32), name=name)(*args)


def _pair_swap(parts, name, collective_id):
    n = len(parts)
    half = [p.shape[1] // 2 for p in parts]
    src = [jax.new_ref(p, memory_space=HBM) for p in parts]
    got = [jax.empty_ref(jax.ShapeDtypeStruct((p.shape[0], h, p.shape[2]), p.dtype), memory_space=HBM) for p, h in zip(parts, half)]

    @_sequencer(name, collective_id, [pltpu.SemaphoreType.DMA((n,))] * 2, _nbytes(parts) // 2)
    def launch(ssem, rsem):
        x, y, c = _place()
        _handshake([(x, y, 1 - c)])
        copies = []
        for s in range(n):
            cp = _remote(src[s].at[pl.ds(0, parts[s].shape[0]), pl.ds((1 - c) * half[s], half[s])], got[s], ssem.at[s], rsem.at[s], (x, y, 1 - c))
            cp.start()
            copies.append(cp)
        for cp in copies:
            cp.wait_recv()
        for cp in copies:
            cp.wait_send()

    launch()
    return [g[...] for g in got]


def _to_owners(sums, name, collective_id):
    n = len(sums)
    src = [jax.new_ref(s, memory_space=HBM) for s in sums]
    got = [jax.empty_ref(jax.ShapeDtypeStruct(s.shape, s.dtype), memory_space=HBM) for s in sums]

    @_sequencer(name, collective_id, [pltpu.SemaphoreType.DMA((n, 3))] * 2, _nbytes(sums) * 3 // 4)
    def launch(ssem, rsem):
        x, y, c = _place()
        q = 2 * x + y
        chips = _other_chips(x, y)
        _handshake([(px, py, c) for px, py in chips])
        sends = []
        for j, (px, py) in enumerate(chips):
            for s in range(n):
                cp = _remote(src[s].at[2 * px + py], got[s].at[q], ssem.at[s, j], rsem.at[s, j], (px, py, c))
                cp.start()
                sends.append(cp)
        for j, (px, py) in enumerate(chips):
            for s in range(n):
                slot = got[s].at[2 * px + py]
                _remote(slot, slot, ssem.at[s, j], rsem.at[s, j], (px, py, c)).wait_recv()
        for cp in sends:
            cp.wait_send()

    launch()
    return [g[...] for g in got]


def _pair_join(blocks, name, collective_id):
    n = len(blocks)
    out = [jax.new_ref(b, memory_space=HBM) for b in blocks]

    @_sequencer(name, collective_id, [pltpu.SemaphoreType.DMA((n,))] * 2, _nbytes(blocks) // 2)
    def launch(ssem, rsem):
        x, y, c = _place()
        _handshake([(x, y, 1 - c)])
        sends = []
        for s in range(n):
            h = blocks[s].shape[0] // 2
            mine = out[s].at[pl.ds(c * h, h)]
            cp = _remote(mine, mine, ssem.at[s], rsem.at[s], (x, y, 1 - c))
            cp.start()
            sends.append(cp)
        for s in range(n):
            h = blocks[s].shape[0] // 2
            theirs = out[s].at[pl.ds((1 - c) * h, h)]
            _remote(theirs, theirs, ssem.at[s], rsem.at[s], (x, y, 1 - c)).wait_recv()
        for cp in sends:
            cp.wait_send()

    launch()
    return [o[...] for o in out]


WIRE = BF16


def _row_tile(h):
    return _pick(h, (256, 368, 352, 128, 16))


def _add_pair(part, got, c, name, after=None):
    _, h, w = got.shape
    tr = _row_tile(h)
    nt = h // tr

    def body(c_ref, p_ref, g_ref, o_ref):
        o_ref[...] = (p_ref[...] + g_ref[...].astype(F32)).astype(o_ref.dtype)

    in_specs = [pl.BlockSpec((None, tr, w), lambda q, i, c_ref: (q, c_ref[0] * nt + i, 0)),
                pl.BlockSpec((None, tr, w), lambda q, i, c_ref: (q, i, 0))]
    args = [c.reshape(1).astype(jnp.int32), part, got]
    if after is not None:
        body = _skip_ref(body, len(args))
        args.append(_deps(after))
        in_specs.append(_dep_spec(args[-1]))
    return pl.pallas_call(
        body,
        grid_spec=pltpu.PrefetchScalarGridSpec(
            num_scalar_prefetch=1, grid=(got.shape[0], nt), in_specs=in_specs,
            out_specs=pl.BlockSpec((None, tr, w), lambda q, i, c_ref: (q, i, 0))),
        out_shape=jax.ShapeDtypeStruct(got.shape, WIRE), name=name,
        compiler_params=_cparams(("arbitrary", "arbitrary")),
    )(*args)


def _sum_chips(slots, sums, q, c, name, after=None):
    _, h, w = slots.shape
    tr = _row_tile(h)
    nt = h // tr

    def body(s_ref, mine_ref, a_ref, b_ref, d_ref, o_ref):
        o_ref[...] = ((mine_ref[...].astype(F32) + a_ref[...].astype(F32)) + b_ref[...].astype(F32)) + d_ref[...].astype(F32)

    slot = lambda k: pl.BlockSpec((None, tr, w), lambda i, s_ref: (s_ref[1 + k], i, 0))
    scalars = jnp.stack([c, q, (q + 1) % N_CHIPS, (q + 2) % N_CHIPS, (q + 3) % N_CHIPS]).astype(jnp.int32)
    in_specs, args = [slot(0), slot(1), slot(2), slot(3)], [scalars, sums, slots, slots, slots]
    if after is not None:
        body = _skip_ref(body, len(args))
        args.append(_deps(after))
        in_specs.append(_dep_spec(args[-1]))
    return pl.pallas_call(
        body,
        grid_spec=pltpu.PrefetchScalarGridSpec(
            num_scalar_prefetch=1, grid=(nt,), in_specs=in_specs,
            out_specs=pl.BlockSpec((tr, w), lambda i, s_ref: (s_ref[0] * nt + i, 0))),
        out_shape=jax.ShapeDtypeStruct((2 * h, w), F32), name=name,
        compiler_params=_cparams(("arbitrary",)),
    )(*args)


class _Reduce:
    def __init__(self, parts, q, c, tag, first_id, regions=None):
        self.parts, self.q, self.c, self.tag, self.first_id, self.regions = parts, q, c, tag, first_id, regions
        self.got = _pair_swap(parts, f"{tag}_pair_swap", first_id)

    def to_owners(self, after=None):
        self.sums = [_add_pair(p, g, self.c, f"{self.tag}_pair_add{i}", after)
                     for i, (p, g) in enumerate(zip(self.parts, self.got))]
        if self.regions is not None:
            self.sums = self.regions(self.sums)
        self.slots = _to_owners(self.sums, f"{self.tag}_to_owners", self.first_id + 1)
        return self.sums

    def join(self, after=None):
        blocks = [_sum_chips(sl, sm, self.q, self.c, f"{self.tag}_sum_chips{i}", after)
                  for i, (sl, sm) in enumerate(zip(self.slots, self.sums))]
        self.out = _pair_join(blocks, f"{self.tag}_pair_join", self.first_id + 2)
        return blocks


WEIGHTS = ("meta_tokens", "ffn1_norm", "ffn1_w_gu", "ffn1_w_down", "mix_norm", "w_in", "ssd_conv_w", "ssd_conv_b",
           "ssd_dt_bias", "ssd_a_log", "ssd_d", "ssd_norm", "hg_lower_bound", "hg_norm", "w_branch_a", "w_branch_b",
           "w_out", "ffn2_norm", "ffn2_w_gu", "ffn2_w_down", "final_norm")
BIG = ("ffn1_w_gu", "ffn1_w_down", "w_in", "w_branch_a", "w_branch_b", "w_out", "ffn2_w_gu", "ffn2_w_down")
SMALL = tuple(n for n in WEIGHTS if n not in BIG)


def _rows1024(a):
    flat = a.reshape(-1)
    n = -(-flat.shape[0] // 1024) * 1024
    return jnp.pad(flat, (0, n - flat.shape[0])).reshape(-1, 1024)


def kernel(x, meta_tokens, ffn1_norm, ffn1_w_gu, ffn1_w_down, mix_norm, w_in, ssd_conv_w, ssd_conv_b, ssd_dt_bias, ssd_a_log, ssd_d, ssd_norm, hg_lower_bound, hg_norm, w_branch_a, w_branch_b, w_out, ffn2_norm, ffn2_w_gu, ffn2_w_down, final_norm, loss_target, m_meta_tokens, m_ffn1_norm, m_ffn1_w_gu, m_ffn1_w_down, m_mix_norm, m_w_in, m_ssd_conv_w, m_ssd_conv_b, m_ssd_dt_bias, m_ssd_a_log, m_ssd_d, m_ssd_norm, m_hg_lower_bound, m_hg_norm, m_w_branch_a, m_w_branch_b, m_w_out, m_ffn2_norm, m_ffn2_w_gu, m_ffn2_w_down, m_final_norm, v_meta_tokens, v_ffn1_norm, v_ffn1_w_gu, v_ffn1_w_down, v_mix_norm, v_w_in, v_ssd_conv_w, v_ssd_conv_b, v_ssd_dt_bias, v_ssd_a_log, v_ssd_d, v_ssd_norm, v_hg_lower_bound, v_hg_norm, v_w_branch_a, v_w_branch_b, v_w_out, v_ffn2_norm, v_ffn2_w_gu, v_ffn2_w_down, v_final_norm):
    P = dict(zip(WEIGHTS, (meta_tokens, ffn1_norm, ffn1_w_gu, ffn1_w_down, mix_norm, w_in, ssd_conv_w, ssd_conv_b, ssd_dt_bias, ssd_a_log, ssd_d, ssd_norm, hg_lower_bound, hg_norm, w_branch_a, w_branch_b, w_out, ffn2_norm, ffn2_w_gu, ffn2_w_down, final_norm)))
    M = dict(zip(WEIGHTS, (m_meta_tokens, m_ffn1_norm, m_ffn1_w_gu, m_ffn1_w_down, m_mix_norm, m_w_in, m_ssd_conv_w, m_ssd_conv_b, m_ssd_dt_bias, m_ssd_a_log, m_ssd_d, m_ssd_norm, m_hg_lower_bound, m_hg_norm, m_w_branch_a, m_w_branch_b, m_w_out, m_ffn2_norm, m_ffn2_w_gu, m_ffn2_w_down, m_final_norm)))
    V = dict(zip(WEIGHTS, (v_meta_tokens, v_ffn1_norm, v_ffn1_w_gu, v_ffn1_w_down, v_mix_norm, v_w_in, v_ssd_conv_w, v_ssd_conv_b, v_ssd_dt_bias, v_ssd_a_log, v_ssd_d, v_ssd_norm, v_hg_lower_bound, v_hg_norm, v_w_branch_a, v_w_branch_b, v_w_out, v_ffn2_norm, v_ffn2_w_gu, v_ffn2_w_down, v_final_norm)))
    cx, cy, cc = _place()
    q = 2 * cx + cy

    mine = jnp.concatenate([meta_tokens.reshape(4, 1024), ssd_conv_w.reshape(2, 1024), jnp.zeros((2, 1024), F32)], axis=0)
    every = _exchange8(mine, "gather_small")
    meta_full = jnp.concatenate([every[2 * k, 0:4].reshape(N_META, 256) for k in range(N_CHIPS)], axis=1)
    conv_w_full = jnp.concatenate([every[2 * k, 4:6].reshape(SSD_CONV, 512) for k in range(N_CHIPS)], axis=1)

    late = ("ffn2_w_down", "w_branch_a", "w_branch_b", "w_out")
    rows = jnp.concatenate([P[n][0] for n in late], axis=0)
    zero = lambda t, dtype=F32: (t[0:1, 0:1] * 0).astype(dtype)

    def in_slot(s, after=None):
        s = s if after is None else s + zero(after)
        return lax.dynamic_update_slice(lax.empty((N_CHIPS,) + s.shape, BF16), s.astype(BF16)[None], (q, 0, 0))

    gu1, down1 = _gather_seq([in_slot(ffn1_w_gu[0]), in_slot(ffn1_w_down[0])], "gather_ffn1", 1)
    W = {n: P[n] for n in SMALL}
    W["meta_tokens"], W["ssd_conv_w"] = meta_full, conv_w_full
    W["ffn1_w_gu"], W["ffn1_w_down"] = gu1, down1.reshape(-1, D_MODEL)
    flying = {}

    def stage(name, t):
        if name == "ffn1_norm":
            flying["w_in"] = _gather_seq([in_slot(w_in[0], t)], "gather_w_in", 2)
            return {}
        if name == "ffn1_out":
            flying["late"] = _gather_seq([in_slot(ffn2_w_gu[0], t), in_slot(rows, t)], "gather_late", 3)
            (w_in_all,) = flying["w_in"]
            w_in_all = w_in_all + zero(t, BF16)
            return {"w_in": _split_w_in(w_in_all.transpose(1, 0, 2).reshape(D_MODEL, -1))}
        if name == "mixers_out":
            gu2, rows_all = flying["late"]
            out, r = {"ffn2_w_gu": gu2}, 0
            for n in late:
                nr = P[n].shape[1]
                out[n] = (rows_all[:, r:r + nr] + zero(t, BF16)).reshape(N_CHIPS * nr, D_MODEL)
                r += nr
            return out
        if name == "late_grads":
            row_parts = jnp.concatenate([t[n].reshape(N_CHIPS, -1, D_MODEL) for n in late], axis=1)
            flying["grad_late"] = _Reduce([t["ffn2_w_gu"], row_parts], q, cc, "grad_late", 4)
            return {"_after": [t["ffn2_w_gu"]] + [t[n] for n in late]}
        if name == "after_conv_bwd":
            return {"_after": flying["grad_late"].to_owners(after=t)}
        if name == "w_in_grads":
            order = ("z", "xbc", "dt", "qfig", "gates")
            blocks = flying["grad_late"].join(after=[t[k] for k in order])

            def regions(sums):
                z, xbc, dt, qfig, gates = [s[0] for s in sums]
                h = z.shape[0]
                qfig = qfig.reshape(h, HG_HEADS, 4, 128).transpose(0, 2, 1, 3).reshape(h, 4 * D_MODEL)
                cols = jnp.concatenate([z, xbc, dt[:, :SSD_HEADS], qfig, gates], axis=1)
                return [cols.reshape(h, N_CHIPS, -1).transpose(1, 0, 2)]

            flying["grad_w_in"] = _Reduce([t[k][None] for k in order], q, cc, "grad_w_in", 7, regions)
            return {"_after": blocks}
        if name == "ffn1_dw_down":
            return {"_after": flying["grad_w_in"].to_owners(after=t)}
        return {}

    W["_stage"] = stage

    loss8, grad_x, G = _local_step(x, loss_target, W)

    small = jnp.concatenate(
        [G["meta_tokens"]] + [_rows1024(G[n]) for n in SMALL if n != "meta_tokens"] + [_rows1024(loss8[0:1, 0:1])], axis=0)
    small = jnp.pad(small, ((0, 40 - small.shape[0]), (0, 0)))
    small_slots = _share8(small, "share_small", 13)

    grad_ffn1 = _Reduce([G["ffn1_w_gu"], G["ffn1_w_down"].reshape(N_CHIPS, -1, D_MODEL)], q, cc, "grad_ffn1", 10)
    flying["grad_w_in"].join(after=grad_x)
    going = grad_ffn1.to_owners(after=grad_x)
    g_gu2, g_rows = flying["grad_late"].out
    (g_w_in,) = flying["grad_w_in"].out
    Gb = {"ffn2_w_gu": g_gu2, "w_in": g_w_in}
    r = 0
    for n in late:
        nr = P[n].shape[1]
        Gb[n] = g_rows[r:r + nr]
        r += nr

    grads, delta, new_m, new_v, done = {}, {}, {}, {}, []
    cols = w_in.shape[2]
    to_tiles = lambda a: a.transpose(2, 0, 1).reshape(cols, 8, 128).reshape(cols * 8, 128)
    from_tiles = lambda a: a.reshape(cols, 1, D_MODEL).transpose(1, 2, 0)
    for n in [n for n in BIG if n in Gb]:
        if n == "w_in":
            g_t = to_tiles(Gb[n][None])
            d_, m_, v_ = _adamw(to_tiles(P[n]), g_t, to_tiles(M[n]), to_tiles(V[n]), f"adamw_{n}", after=going)
            grads[n], delta[n], new_m[n], new_v[n] = from_tiles(g_t), from_tiles(d_), from_tiles(m_), from_tiles(v_)
        else:
            d_, m_, v_ = _adamw(P[n][0], Gb[n], M[n][0], V[n][0], f"adamw_{n}", after=going)
            grads[n], delta[n], new_m[n], new_v[n] = Gb[n][None], d_[None], m_[None], v_[None]
        done.append(d_)

    small = _sum_slots(small_slots, "sum_small", after=done)
    Gs = {"meta_tokens": small[0:N_META]}
    r = N_META
    for n in SMALL:
        if n == "meta_tokens":
            continue
        nr = -(-G[n].size // 1024)
        Gs[n] = small[r:r + nr].reshape(-1)[:G[n].size].reshape(G[n].shape)
        r += nr
    loss = small[r, 0]
    Gs["meta_tokens"] = lax.dynamic_slice(Gs["meta_tokens"], (0, 256 * q), (N_META, 256))
    Gs["ssd_conv_w"] = lax.dynamic_slice(Gs["ssd_conv_w"], (0, 512 * q), (SSD_CONV, 512))[None]
    Gs = {n: Gs[n].reshape(P[n].shape) for n in SMALL}
    grads.update(Gs)
    flat = lambda a: a.reshape(-1, a.shape[-1])
    d_s, m_s, v_s = _adamw_many(*[[flat(D[n]) for n in SMALL] for D in (P, Gs, M, V)], "adamw_small")
    for i, n in enumerate(SMALL):
        delta[n], new_m[n], new_v[n] = d_s[i].reshape(P[n].shape), m_s[i].reshape(P[n].shape), v_s[i].reshape(P[n].shape)
    done.append(d_s[0])
    grad_ffn1.join(after=done)
    Gb["ffn1_w_gu"], Gb["ffn1_w_down"] = grad_ffn1.out
    for n in ("ffn1_w_gu", "ffn1_w_down"):
        d_, m_, v_ = _adamw(P[n][0], Gb[n], M[n][0], V[n][0], f"adamw_{n}")
        grads[n], delta[n], new_m[n], new_v[n] = Gb[n][None], d_[None], m_[None], v_[None]
    return (loss, grad_x, *[grads[n] for n in WEIGHTS], *[delta[n] for n in WEIGHTS],
            *[new_m[n] for n in WEIGHTS], *[new_v[n] for n in WEIGHTS])
```

```python
import functools

import jax
import jax.numpy as jnp
from jax import lax
from jax.experimental import pallas as pl
from jax.experimental.pallas import tpu as pltpu
from jax.experimental.pallas import tpu_sc as plsc

F32 = jnp.float32
BF16 = jnp.bfloat16
HIGHEST = lax.Precision.HIGHEST
MESH = pl.DeviceIdType.MESH

D_MODEL = 1024
N_META = 16
EPS = 1e-6
SSD_HEADS = 16
SSD_HEAD_DIM = 64
SSD_INNER = 1024
SSD_GROUPS = 4
SSD_STATE = 128
SSD_CONV = 4
SSD_CONV_CH = 2048
HG_HEADS = 8
HG_SUB = 32
CHUNK = 128
D_FF = 2816
N_CHIPS = 4
IN_SIZES = (1024, 2048, 16, 1024, 1024, 1024, 1024, 1024, 1024)
ADAM_LR = 0.001
ADAM_B1 = 0.9
ADAM_B2 = 0.999
ADAM_EPS = 1e-08
ADAM_WD = 0.01
ADAM_STEP = 10
VMEM_LIMIT = 56 * 1024 * 1024
MATMUL_BLOCK_BYTES = 42 * 1024 * 1024
ADAMW_BLOCK_BYTES = 5 * 512 * 1024


def _cparams(sem=None):
    return pltpu.CompilerParams(dimension_semantics=sem, vmem_limit_bytes=VMEM_LIMIT)


def _pick(n, cands):
    for c in cands:
        if n % c == 0:
            return c
    return n


def _deps(after):
    xs = after if isinstance(after, (list, tuple)) else [after]
    one = lambda x: lax.slice(x, (0,) * x.ndim, (1,) * x.ndim).reshape(1).astype(F32)
    return jnp.concatenate([one(x) for x in xs]).reshape(1, -1)


def _dep_spec(dep):
    return pl.BlockSpec(dep.shape, lambda *_: (0, 0))


def _skip_ref(body, pos):
    return lambda *refs: body(*refs[:pos], *refs[pos + 1:])


def _dg(a, b, ca, cb):
    return lax.dot_general(a.astype(BF16), b.astype(BF16), (((ca,), (cb,)), ((), ())), preferred_element_type=F32)


@jax.custom_vjp
def _mm(a, b):
    return _dg(a, b, 1, 0)


def _mm_fwd(a, b):
    return _dg(a, b, 1, 0), (a, b)


def _mm_bwd(r, g):
    a, b = r
    return _dg(g, b, 1, 1), _dg(a, g, 0, 0)


_mm.defvjp(_mm_fwd, _mm_bwd)


@jax.custom_vjp
def _mm_nt(a, b):
    return _dg(a, b, 1, 1)


def _mm_nt_fwd(a, b):
    return _dg(a, b, 1, 1), (a, b)


def _mm_nt_bwd(r, g):
    a, b = r
    return _dg(g, b, 1, 0), _dg(g, a, 0, 0)


_mm_nt.defvjp(_mm_nt_fwd, _mm_nt_bwd)


@jax.custom_vjp
def _mm_tn(a, b):
    return _dg(a, b, 0, 0)


def _mm_tn_fwd(a, b):
    return _dg(a, b, 0, 0), (a, b)


def _mm_tn_bwd(r, g):
    a, b = r
    return _dg(b, g, 1, 1), _dg(a, g, 1, 0)


_mm_tn.defvjp(_mm_tn_fwd, _mm_tn_bwd)


def _tri_sum(x, lower):
    n = x.shape[0]
    ri = lax.broadcasted_iota(jnp.int32, (n, n), 0)
    ci = lax.broadcasted_iota(jnp.int32, (n, n), 1)
    tri = ((ri >= ci) if lower else (ri <= ci)).astype(BF16)
    x1 = x.astype(BF16)
    r1 = x - x1.astype(F32)
    x2 = r1.astype(BF16)
    x3 = (r1 - x2.astype(F32)).astype(BF16)
    dot = lambda p: lax.dot_general(tri, p, (((1,), (0,)), ((), ())), preferred_element_type=F32)
    return (dot(x3) + dot(x2)) + dot(x1)


@jax.custom_vjp
def _cumsum_rows(x):
    return _tri_sum(x, True)


_cumsum_rows.defvjp(lambda x: (_tri_sum(x, True), None), lambda _, g: (_tri_sum(g, False),))


def _silu(x):
    return x * jax.nn.sigmoid(x)


def _softplus(x):
    return jnp.maximum(x, 0.0) + jnp.log(1.0 + jnp.exp(-jnp.abs(x)))


def _tril(n):
    ri = lax.broadcasted_iota(jnp.int32, (n, n), 0)
    ci = lax.broadcasted_iota(jnp.int32, (n, n), 1)
    return ri >= ci


def _row_of(m, r):
    sub = lax.broadcasted_iota(jnp.int32, (m.shape[0], 1), 0)
    return jnp.sum(jnp.where(sub == r, m, 0.0), axis=0, keepdims=True)


def _col_of(m, c):
    lane = lax.broadcasted_iota(jnp.int32, (1, m.shape[1]), 1)
    return jnp.sum(jnp.where(lane == c, m, 0.0), axis=1, keepdims=True)


def _matmul(a, b, *, mode, out_dtype, name, alpha=1.0, res=None, tm=None, tn=None, out_groups=None, after=None):
    b3 = b.ndim == 3
    if mode == "nn":
        M, K = a.shape
        G = b.shape[0] if b3 else 1
        Ng = b.shape[-1]
        N = G * Ng
    elif mode == "nt":
        M, K = a.shape
        G = b.shape[0] if b3 else 1
        N = b.shape[-2]
        Kg = b.shape[-1]
        assert G * Kg == K
    else:
        K, M = a.shape
        N = b.shape[1]
        G = out_groups or 1
        Ng = N // G
    has_res = res is not None
    split_n = (mode == "nn" and b3) or (mode == "tn" and G > 1)
    per_mn = jnp.dtype(out_dtype).itemsize + (res.dtype.itemsize if has_res else 0)
    fits = [(m_ * n_, m_, n_)
            for m_ in (4352, 2176, 1408, 1088, 1024, 544, 512, 256, 128) if M % m_ == 0
            for n_ in (2816, 2048, 1408, 1024, 512, 256, 128) if (Ng if split_n else N) % n_ == 0
            if 2 * (K * m_ * a.dtype.itemsize + K * n_ * b.dtype.itemsize + m_ * n_ * per_mn) + 4 * m_ * n_ <= MATMUL_BLOCK_BYTES]
    _, tm_fit, tn_fit = max(fits)
    tm, tn = tm or tm_fit, tn or tn_fit
    nm, nn_ = M // tm, N // tn
    assert nm * tm == M and nn_ * tn == N, (name, M, N, K, tm, tn)

    if mode == "nn":
        a_spec = pl.BlockSpec((tm, K), lambda i, j: (i, 0))
        if b3:
            ns = Ng // tn
            b_spec = pl.BlockSpec((None, K, tn), lambda i, j: (j // ns, 0, j % ns))
        else:
            b_spec = pl.BlockSpec((K, tn), lambda i, j: (0, j))
        ca, cb = 1, 0
    elif mode == "nt":
        a_spec = pl.BlockSpec((tm, K), lambda i, j: (i, 0))
        if b3:
            b_spec = pl.BlockSpec((G, tn, Kg), lambda i, j: (0, j, 0))
        else:
            b_spec = pl.BlockSpec((tn, K), lambda i, j: (j, 0))
        ca, cb = 1, 1
    else:
        a_spec = pl.BlockSpec((K, tm), lambda i, j: (0, i))
        b_spec = pl.BlockSpec((K, tn), lambda i, j: (0, j))
        ca, cb = 0, 0
    if mode == "tn" and G > 1:
        ns = Ng // tn
        o_spec = pl.BlockSpec((None, tm, tn), lambda i, j: (j // ns, i, j % ns))
        out_shape = jax.ShapeDtypeStruct((G, M, Ng), out_dtype)
    else:
        o_spec = pl.BlockSpec((tm, tn), lambda i, j: (i, j))
        out_shape = jax.ShapeDtypeStruct((M, N), out_dtype)
    in_specs = [a_spec, b_spec]
    args = [a, b]
    if has_res:
        in_specs.append(pl.BlockSpec((tm, tn), lambda i, j: (i, j)))
        args.append(res)
    if after is not None:
        args.append(_deps(after))
        in_specs.append(_dep_spec(args[-1]))

    def body(*refs):
        a_ref, b_ref, o_ref = refs[0], refs[1], refs[-1]
        if mode == "nt" and b3:
            o = _dg(a_ref[:, 0:Kg], b_ref[0], ca, cb)
            for g in range(1, G):
                o = o + _dg(a_ref[:, g * Kg:(g + 1) * Kg], b_ref[g], ca, cb)
        else:
            o = _dg(a_ref[...], b_ref[...], ca, cb)
        if alpha != 1.0:
            o = o * alpha
        if has_res:
            o = o + refs[2][...]
        o_ref[...] = o.astype(o_ref.dtype)

    return pl.pallas_call(
        body, grid=(nm, nn_), in_specs=in_specs, out_specs=o_spec, out_shape=out_shape, name=name,
        compiler_params=_cparams(("parallel", "parallel")),
    )(*args)


def _sum_nt(xs, ws, name):
    R, N = xs[0].shape[0], ws[0].shape[0]
    n = len(xs)
    per_m = sum(x.shape[1] * x.dtype.itemsize for x in xs)
    per_n = sum(w.shape[1] * w.dtype.itemsize for w in ws)
    fits = [(m_ * n_, m_, n_) for m_ in (1088, 544, 256, 128) if R % m_ == 0 for n_ in (1024, 512, 256, 128) if N % n_ == 0
            if 2 * (m_ * per_m + n_ * per_n + m_ * n_ * 4) + 4 * m_ * n_ <= MATMUL_BLOCK_BYTES]
    _, tm, tn = max(fits)

    def body(*refs):
        o = _dg(refs[0][...], refs[n][...], 1, 1)
        for p in range(1, n):
            o = o + _dg(refs[p][...], refs[n + p][...], 1, 1)
        refs[-1][...] = o

    return pl.pallas_call(
        body, grid=(R // tm, N // tn),
        in_specs=[pl.BlockSpec((tm, x.shape[1]), lambda i, j: (i, 0)) for x in xs]
        + [pl.BlockSpec((tn, w.shape[1]), lambda i, j: (j, 0)) for w in ws],
        out_specs=pl.BlockSpec((tm, tn), lambda i, j: (i, j)), out_shape=jax.ShapeDtypeStruct((R, N), F32), name=name,
        compiler_params=_cparams(("parallel", "parallel")),
    )(*xs, *ws)


def _rms_fn(h, w):
    r = lax.rsqrt(jnp.mean(h * h, axis=-1, keepdims=True) + EPS)
    return h * r * w


def _swiglu_fn(gu):
    g = gu[:, :D_FF].astype(F32)
    u = gu[:, D_FF:].astype(F32)
    return _silu(g) * u


def _merge_fn(pa, pb, gates):
    return jax.nn.sigmoid(gates[:, :D_MODEL]) * pa + jax.nn.sigmoid(gates[:, D_MODEL:]) * pb


def _rows_call(body, *, rows, tr, ins, outs, accs=(), name, after=None):
    n = rows // tr
    assert n * tr == rows
    if after is not None:
        body = _skip_ref(body, len(ins))
        ins = list(ins) + [("full", _deps(after))]

    def spec(x):
        if isinstance(x, tuple):
            shp = x[1].shape
            return pl.BlockSpec(shp, lambda i: (0,) * len(shp))
        return pl.BlockSpec((tr, x.shape[1]), lambda i: (i, 0))

    in_specs = [spec(x) for x in ins]
    args = [x[1] if isinstance(x, tuple) else x for x in ins]
    out_specs = [spec(x) for x in outs] + [pl.BlockSpec(x.shape, lambda i: (0,) * len(x.shape)) for x in accs]
    out_shape = [x[1] if isinstance(x, tuple) else x for x in outs] + list(accs)
    return pl.pallas_call(
        body, grid=(n,), in_specs=in_specs, out_specs=out_specs, out_shape=out_shape, name=name,
        compiler_params=_cparams(("arbitrary",)),
    )(*args)


def _acc_rows(ref, val):
    @pl.when(pl.program_id(0) == 0)
    def _():
        ref[...] = jnp.zeros_like(ref)

    ref[0:1, :] += val


def _rms_fwd(h, w, name):
    def body(h_ref, w_ref, o_ref):
        o_ref[...] = _rms_fn(h_ref[...], w_ref[...]).astype(o_ref.dtype)

    R = h.shape[0]
    return _rows_call(body, rows=R, tr=_pick(R, (256, 128)), ins=[h, ("full", w)],
                      outs=[jax.ShapeDtypeStruct(h.shape, BF16)], name=name)[0]


def _rms_bwd(h, w, dn, dres, name, after=None):
    def body(h_ref, w_ref, dn_ref, dres_ref, dh_ref, dw_ref):
        _, vjp = jax.vjp(_rms_fn, h_ref[...], w_ref[...])
        dh, dw = vjp(dn_ref[...].astype(F32))
        dh_ref[...] = dh + dres_ref[...]
        _acc_rows(dw_ref, dw)

    R = h.shape[0]
    return _rows_call(body, rows=R, tr=_pick(R, (256, 128)), ins=[h, ("full", w), dn, dres],
                      outs=[jax.ShapeDtypeStruct(h.shape, F32)], accs=[jax.ShapeDtypeStruct((8, D_MODEL), F32)], name=name,
                      after=after)


def _rms_bwd_tokens(h, w, dn, dres, nseq, name):
    Tp = h.shape[0] // nseq
    nc = Tp // CHUNK

    def body(h_ref, w_ref, dn_ref, dres_ref, dx_ref, dm_ref, dw_ref):
        b, c = pl.program_id(0), pl.program_id(1)
        _, vjp = jax.vjp(_rms_fn, h_ref[...], w_ref[...])
        dh, dw = vjp(dn_ref[...].astype(F32))
        dh = dh + dres_ref[...]

        @pl.when(c == 0)
        def _():
            dm_ref[...] = dh

        @pl.when(c > 0)
        def _():
            dx_ref[...] = dh

        @pl.when((b == 0) & (c == 0))
        def _():
            dw_ref[...] = jnp.zeros_like(dw_ref)

        dw_ref[0:1, :] += dw

    rows = pl.BlockSpec((CHUNK, D_MODEL), lambda b, c: (b * nc + c, 0))
    return pl.pallas_call(
        body, grid=(nseq, nc),
        in_specs=[rows, pl.BlockSpec((1, D_MODEL), lambda b, c: (0, 0)), rows, rows],
        out_specs=[pl.BlockSpec((None, CHUNK, D_MODEL), lambda b, c: (b, jnp.maximum(c - 1, 0), 0)),
                   pl.BlockSpec((None, CHUNK, D_MODEL), lambda b, c: (b, 0, 0)),
                   pl.BlockSpec((8, D_MODEL), lambda b, c: (0, 0))],
        out_shape=[jax.ShapeDtypeStruct((nseq, Tp - CHUNK, D_MODEL), F32), jax.ShapeDtypeStruct((nseq, CHUNK, D_MODEL), F32),
                   jax.ShapeDtypeStruct((8, D_MODEL), F32)],
        name=name, compiler_params=_cparams(("arbitrary", "arbitrary")),
    )(h, w, dn, dres)


def _gu_swiglu(n, w_gu, name):
    R = n.shape[0]
    G, _, ng = w_gu.shape

    def body(n_ref, w_ref, gu_ref, a_ref):
        x = n_ref[...]
        for r in range(G):
            gu_ref[:, ng * r:ng * (r + 1)] = _dg(x, w_ref[r], 1, 0).astype(gu_ref.dtype)
        a_ref[...] = _swiglu_fn(gu_ref[...]).astype(a_ref.dtype)

    return _rows_call(body, rows=R, tr=_pick(R, (256, 128)), ins=[n, ("full", w_gu)],
                      outs=[jax.ShapeDtypeStruct((R, 2 * D_FF), BF16), jax.ShapeDtypeStruct((R, D_FF), BF16)], name=name)


def _d_swiglu(dout, w_down, gu, alpha, name):
    R = gu.shape[0]

    def body(do_ref, w_ref, gu_ref, o_ref):
        da = _dg(do_ref[...] * alpha, w_ref[...], 1, 1)
        g = gu_ref[:, :D_FF].astype(F32)
        u = gu_ref[:, D_FF:].astype(F32)
        s = jax.nn.sigmoid(g)
        t = g * s
        o_ref[:, :D_FF] = (da * u * (s + t - t * s)).astype(o_ref.dtype)
        o_ref[:, D_FF:] = (da * t).astype(o_ref.dtype)

    return _rows_call(body, rows=R, tr=_pick(R, (256, 128)), ins=[dout, ("full", w_down), gu],
                      outs=[jax.ShapeDtypeStruct(gu.shape, BF16)], name=name)[0]


def _branch_merge(ya, yb, wa, wb, gates, name):
    def body(ya_ref, yb_ref, wa_ref, wb_ref, g_ref, pa_ref, pb_ref, o_ref):
        pa = _dg(ya_ref[...], wa_ref[...], 1, 0)
        pb = _dg(yb_ref[...], wb_ref[...], 1, 0)
        pa_ref[...] = pa
        pb_ref[...] = pb
        o_ref[...] = _merge_fn(pa, pb, g_ref[...].astype(F32)).astype(o_ref.dtype)

    R = ya.shape[0]
    f32 = jax.ShapeDtypeStruct((R, D_MODEL), F32)
    return _rows_call(body, rows=R, tr=_pick(R, (544, 256, 128)), ins=[ya, yb, ("full", wa), ("full", wb), gates],
                      outs=[f32, f32, jax.ShapeDtypeStruct((R, D_MODEL), BF16)], name=name)


def _branch_merge_bwd(pa, pb, gates, dm, wa, wb, name):
    def body(pa_ref, pb_ref, g_ref, dm_ref, wa_ref, wb_ref, dpa_ref, dpb_ref, dg_ref, dya_ref, dyb_ref):
        _, vjp = jax.vjp(_merge_fn, pa_ref[...], pb_ref[...], g_ref[...].astype(F32))
        dpa, dpb, dg = vjp(dm_ref[...].astype(F32))
        dpa_ref[...] = dpa.astype(dpa_ref.dtype)
        dpb_ref[...] = dpb.astype(dpb_ref.dtype)
        dg_ref[...] = dg.astype(dg_ref.dtype)
        dya_ref[...] = _dg(dpa, wa_ref[...], 1, 1).astype(dya_ref.dtype)
        dyb_ref[...] = _dg(dpb, wb_ref[...], 1, 1).astype(dyb_ref.dtype)

    R = pa.shape[0]
    b16 = jax.ShapeDtypeStruct(pa.shape, BF16)
    return _rows_call(body, rows=R, tr=_pick(R, (544, 256, 128)), ins=[pa, pb, gates, dm, ("full", wa), ("full", wb)],
                      outs=[b16, b16, jax.ShapeDtypeStruct(gates.shape, BF16), b16, b16], name=name)


def _loss_head(h3, w, target, nseq, name):
    Tp = h3.shape[0] // nseq
    nc = Tp // CHUNK

    def fn(h, w_, t, valid):
        y = _rms_fn(h, w_)
        e = (y - t) * valid
        return 0.5 * jnp.sum(jnp.mean(e * e, axis=-1, keepdims=True))

    def body(h_ref, w_ref, t_ref, loss_ref, dh_ref, dw_ref):
        b, c = pl.program_id(0), pl.program_id(1)
        valid = (c >= 1).astype(F32)
        t = t_ref[...]
        loss, vjp = jax.vjp(lambda h, w_: fn(h, w_, t, valid), h_ref[...], w_ref[...])
        dh, dw = vjp(jnp.ones((), F32))
        dh_ref[...] = dh

        @pl.when((b == 0) & (c == 0))
        def _():
            loss_ref[...] = jnp.zeros_like(loss_ref)
            dw_ref[...] = jnp.zeros_like(dw_ref)

        loss_ref[...] += jnp.full(loss_ref.shape, loss, F32)
        dw_ref[0:1, :] += dw

    return pl.pallas_call(
        body, grid=(nseq, nc),
        in_specs=[pl.BlockSpec((CHUNK, D_MODEL), lambda b, c: (b * nc + c, 0)),
                  pl.BlockSpec((1, D_MODEL), lambda b, c: (0, 0)),
                  pl.BlockSpec((None, CHUNK, D_MODEL), lambda b, c: (b, jnp.maximum(c - 1, 0), 0))],
        out_specs=[pl.BlockSpec((8, 128), lambda b, c: (0, 0)),
                   pl.BlockSpec((CHUNK, D_MODEL), lambda b, c: (b * nc + c, 0)),
                   pl.BlockSpec((8, D_MODEL), lambda b, c: (0, 0))],
        out_shape=[jax.ShapeDtypeStruct((8, 128), F32), jax.ShapeDtypeStruct(h3.shape, F32),
                   jax.ShapeDtypeStruct((8, D_MODEL), F32)],
        name=name, compiler_params=_cparams(("arbitrary", "arbitrary")),
    )(h3, w, target)


CONV_TILE = 512
CONV_HALO = 8


def _conv_fwd(xbc, w, b, pad, name):
    B, Tp, C = xbc.shape
    nch = Tp // CHUNK

    def body(x_ref, w_ref, b_ref, o_ref, xp):
        xp[0:CONV_HALO, :] = jnp.zeros((CONV_HALO, CONV_TILE), F32)
        xp[CONV_HALO:, :] = x_ref[...]
        for c in range(nch):
            acc = jnp.zeros((CHUNK, CONV_TILE), F32) + b_ref[...]
            for k in range(SSD_CONV):
                acc = acc + w_ref[k:k + 1, :] * xp[pl.ds(CONV_HALO + CHUNK * c - (SSD_CONV - 1) + k, CHUNK), :]
            out = _silu(acc)
            if CHUNK * c < pad:
                row = CHUNK * c + lax.broadcasted_iota(jnp.int32, (CHUNK, 1), 0)
                out = jnp.where(row >= pad, out, 0.0)
            o_ref[pl.ds(CHUNK * c, CHUNK), :] = out

    return pl.pallas_call(
        body, grid=(B, C // CONV_TILE),
        in_specs=[pl.BlockSpec((None, Tp, CONV_TILE), lambda i, j: (i, 0, j)),
                  pl.BlockSpec((SSD_CONV, CONV_TILE), lambda i, j: (0, j)),
                  pl.BlockSpec((1, CONV_TILE), lambda i, j: (0, j))],
        out_specs=pl.BlockSpec((None, Tp, CONV_TILE), lambda i, j: (i, 0, j)),
        out_shape=jax.ShapeDtypeStruct(xbc.shape, F32),
        scratch_shapes=[pltpu.VMEM((Tp + CONV_HALO, CONV_TILE), F32)],
        name=name, compiler_params=_cparams(("arbitrary", "arbitrary")),
    )(xbc, w, b)


def _conv_bwd(xbc, w, b, dact, pad, name):
    B, Tp, C = xbc.shape
    nch = Tp // CHUNK

    def body(x_ref, w_ref, b_ref, da_ref, dx_ref, dw_ref, db_ref, xp, dp):
        bi = pl.program_id(1)
        xp[0:CONV_HALO, :] = jnp.zeros((CONV_HALO, CONV_TILE), F32)
        xp[CONV_HALO:, :] = x_ref[...]
        dp[pl.ds(Tp, CONV_HALO), :] = jnp.zeros((CONV_HALO, CONV_TILE), F32)
        dws = [jnp.zeros((1, CONV_TILE), F32) for _ in range(SSD_CONV)]
        dbs = jnp.zeros((1, CONV_TILE), F32)
        for c in range(nch):
            xs = [xp[pl.ds(CONV_HALO + CHUNK * c - (SSD_CONV - 1) + k, CHUNK), :] for k in range(SSD_CONV)]
            acc = jnp.zeros((CHUNK, CONV_TILE), F32) + b_ref[...]
            for k in range(SSD_CONV):
                acc = acc + w_ref[k:k + 1, :] * xs[k]
            sg = jax.nn.sigmoid(acc)
            t = acc * sg
            dpre = da_ref[pl.ds(CHUNK * c, CHUNK), :] * (sg + t - t * sg)
            if CHUNK * c < pad:
                row = CHUNK * c + lax.broadcasted_iota(jnp.int32, (CHUNK, 1), 0)
                dpre = jnp.where(row >= pad, dpre, 0.0)
            dp[pl.ds(CHUNK * c, CHUNK), :] = dpre
            dbs = dbs + jnp.sum(dpre, axis=0, keepdims=True)
            for k in range(SSD_CONV):
                dws[k] = dws[k] + jnp.sum(dpre * xs[k], axis=0, keepdims=True)
        for c in range(nch):
            acc = jnp.zeros((CHUNK, CONV_TILE), F32)
            for k in range(SSD_CONV):
                acc = acc + w_ref[k:k + 1, :] * dp[pl.ds(CHUNK * c + (SSD_CONV - 1) - k, CHUNK), :]
            dx_ref[pl.ds(CHUNK * c, CHUNK), :] = acc.astype(dx_ref.dtype)

        @pl.when(bi == 0)
        def _():
            dw_ref[...] = jnp.zeros_like(dw_ref)
            db_ref[...] = jnp.zeros_like(db_ref)

        for k in range(SSD_CONV):
            dw_ref[k:k + 1, :] += dws[k]
        db_ref[0:1, :] += dbs

    return pl.pallas_call(
        body, grid=(C // CONV_TILE, B),
        in_specs=[pl.BlockSpec((None, Tp, CONV_TILE), lambda j, i: (i, 0, j)),
                  pl.BlockSpec((SSD_CONV, CONV_TILE), lambda j, i: (0, j)),
                  pl.BlockSpec((1, CONV_TILE), lambda j, i: (0, j)),
                  pl.BlockSpec((None, Tp, CONV_TILE), lambda j, i: (i, 0, j))],
        out_specs=[pl.BlockSpec((None, Tp, CONV_TILE), lambda j, i: (i, 0, j)),
                   pl.BlockSpec((8, CONV_TILE), lambda j, i: (0, j)),
                   pl.BlockSpec((8, CONV_TILE), lambda j, i: (0, j))],
        out_shape=[jax.ShapeDtypeStruct(xbc.shape, BF16), jax.ShapeDtypeStruct((8, C), F32),
                   jax.ShapeDtypeStruct((8, C), F32)],
        scratch_shapes=[pltpu.VMEM((Tp + CONV_HALO, CONV_TILE), F32), pltpu.VMEM((Tp + CONV_HALO, CONV_TILE), F32)],
        name=name, compiler_params=_cparams(("arbitrary", "arbitrary")),
    )(xbc, w, b, dact)


def _ssd_chunk(xs, bm, cm, dtr, z, state, dt_bias, a_log, dskip, norm_w, valid):
    Q = xs.shape[0]
    lane = lax.broadcasted_iota(jnp.int32, (1, 128), 1)
    dt = jnp.where(lane < SSD_HEADS, _softplus(dtr + dt_bias), 0.0) * valid
    a = dt * (-jnp.exp(a_log))
    tril = _tril(Q)
    cs = _cumsum_rows(a)
    cs_t = cs.T
    cs_end = _row_of(cs, Q - 1)
    low = lane < SSD_HEAD_DIM
    low_rows = lax.broadcasted_iota(jnp.int32, (128, 1), 0) < SSD_HEAD_DIM
    ys, new_state = [], []
    for g in range(SSD_GROUPS):
        bg = bm[:, 128 * g:128 * (g + 1)]
        cg = cm[:, 128 * g:128 * (g + 1)]
        cb = _mm_nt(cg, bg)
        for pr in range(2):
            p = 2 * g + pr
            h0, h1 = 2 * p, 2 * p + 1
            xp = xs[:, 128 * p:128 * (p + 1)]
            c0, c1 = _col_of(cs, h0), _col_of(cs, h1)
            e0, e1 = _col_of(cs_end, h0), _col_of(cs_end, h1)
            xd = xp * jnp.where(low, _col_of(dt, h0), _col_of(dt, h1))
            l0 = jnp.exp(jnp.where(tril, c0 - _row_of(cs_t, h0), -1e30))
            l1 = jnp.exp(jnp.where(tril, c1 - _row_of(cs_t, h1), -1e30))
            y_diag = jnp.where(low, _mm(cb * l0, xd), _mm(cb * l1, xd))
            to_end = jnp.where(low, jnp.exp(e0 - c0), jnp.exp(e1 - c1))
            sp = state[128 * p:128 * (p + 1), :]
            y_off = _mm_nt(cg, sp) * jnp.where(low, jnp.exp(c0), jnp.exp(c1))
            new_state.append(sp * jnp.where(low_rows, jnp.exp(e0), jnp.exp(e1)) + _mm_tn(xd * to_end, bg))
            ys.append(y_diag + y_off + xp * jnp.where(low, _col_of(dskip, h0), _col_of(dskip, h1)))
    y = jnp.concatenate(ys, axis=1) * _silu(z)
    gw = SSD_INNER // SSD_GROUPS
    outs = []
    for g in range(SSD_GROUPS):
        blk = y[:, gw * g:gw * (g + 1)]
        outs.append(blk * lax.rsqrt(jnp.mean(blk * blk, axis=-1, keepdims=True) + EPS))
    return jnp.concatenate(outs, axis=1) * norm_w, jnp.concatenate(new_state, axis=0)


def _valid_rows(c, pad):
    row = c * CHUNK + lax.broadcasted_iota(jnp.int32, (CHUNK, 1), 0)
    return (row >= pad).astype(F32)


def _ssd_fwd(xact, dtr, z, dt_bias, a_log, dskip, norm_w, pad, name):
    B, Tp, _ = xact.shape
    nc = Tp // CHUNK

    def body(xs_ref, bm_ref, cm_ref, dt_ref, z_ref, db_ref, al_ref, ds_ref, nw_ref, y_ref, save_ref, st):
        c = pl.program_id(1)

        @pl.when(c == 0)
        def _():
            st[...] = jnp.zeros_like(st)

        s0 = st[...]
        save_ref[...] = s0
        y, s1 = _ssd_chunk(xs_ref[...], bm_ref[...], cm_ref[...], dt_ref[...], z_ref[...].astype(F32), s0, db_ref[...],
                           al_ref[...], ds_ref[...], nw_ref[...], _valid_rows(c, pad))
        y_ref[...] = y.astype(y_ref.dtype)
        st[...] = s1

    row = lambda w, off=0: pl.BlockSpec((None, CHUNK, w), lambda b, c: (b, c, off))
    par = lambda w: pl.BlockSpec((1, w), lambda b, c: (0, 0))
    return pl.pallas_call(
        body, grid=(B, nc),
        in_specs=[row(1024, 0), row(512, 2), row(512, 3), row(128), row(1024), par(128), par(128), par(128), par(1024)],
        out_specs=[row(1024), pl.BlockSpec((None, None, 1024, 128), lambda b, c: (b, c, 0, 0))],
        out_shape=[jax.ShapeDtypeStruct((B, Tp, SSD_INNER), BF16), jax.ShapeDtypeStruct((B, nc, 1024, 128), F32)],
        scratch_shapes=[pltpu.VMEM((1024, 128), F32)],
        name=name, compiler_params=_cparams(("arbitrary", "arbitrary")),
    )(xact, xact, xact, dtr, z, dt_bias, a_log, dskip, norm_w)


def _ssd_bwd(xact, dtr, z, dt_bias, a_log, dskip, norm_w, saved, dy, pad, name, after=None):
    B, Tp, _ = xact.shape
    nc = Tp // CHUNK

    def body(xs_ref, bm_ref, cm_ref, dt_ref, z_ref, db_ref, al_ref, ds_ref, nw_ref, sv_ref, dy_ref,
             dx_ref, ddt_ref, dz_ref, dpar_ref, dnw_ref, dst):
        b, i = pl.program_id(0), pl.program_id(1)
        c = nc - 1 - i

        @pl.when(i == 0)
        def _():
            dst[...] = jnp.zeros_like(dst)

        valid = _valid_rows(c, pad)
        fn = lambda *a: _ssd_chunk(*a, valid)
        _, vjp = jax.vjp(fn, xs_ref[...], bm_ref[...], cm_ref[...], dt_ref[...], z_ref[...].astype(F32), sv_ref[...],
                         db_ref[...], al_ref[...], ds_ref[...], nw_ref[...])
        dxs, dbm, dcm, ddt, dz, dstate, ddb, dal, dds, dnw = vjp((dy_ref[...].astype(F32), dst[...]))
        dx_ref[:, 0:1024] = dxs
        dx_ref[:, 1024:1536] = dbm
        dx_ref[:, 1536:2048] = dcm
        ddt_ref[...] = ddt
        dz_ref[...] = dz.astype(dz_ref.dtype)
        dst[...] = dstate

        @pl.when((b == 0) & (i == 0))
        def _():
            dpar_ref[...] = jnp.zeros_like(dpar_ref)
            dnw_ref[...] = jnp.zeros_like(dnw_ref)

        dpar_ref[0:1, :] += ddb
        dpar_ref[1:2, :] += dal
        dpar_ref[2:3, :] += dds
        dnw_ref[0:1, :] += dnw

    row = lambda w, off=0: pl.BlockSpec((None, CHUNK, w), lambda b, i: (b, nc - 1 - i, off))
    par = lambda w: pl.BlockSpec((1, w), lambda b, i: (0, 0))
    acc = lambda w: pl.BlockSpec((8, w), lambda b, i: (0, 0))
    in_specs = [row(1024, 0), row(512, 2), row(512, 3), row(128), row(1024), par(128), par(128), par(128), par(1024),
                pl.BlockSpec((None, None, 1024, 128), lambda b, i: (b, nc - 1 - i, 0, 0)), row(1024)]
    args = [xact, xact, xact, dtr, z, dt_bias, a_log, dskip, norm_w, saved, dy]
    if after is not None:
        body = _skip_ref(body, len(args))
        args.append(_deps(after))
        in_specs.append(_dep_spec(args[-1]))
    outs = pl.pallas_call(
        body, grid=(B, nc), in_specs=in_specs,
        out_specs=[row(2048), row(128), row(1024), acc(128), acc(1024)],
        out_shape=[jax.ShapeDtypeStruct((B, Tp, 2048), F32), jax.ShapeDtypeStruct((B, Tp, 128), F32),
                   jax.ShapeDtypeStruct((B, Tp, 1024), BF16), jax.ShapeDtypeStruct((8, 128), F32),
                   jax.ShapeDtypeStruct((8, 1024), F32)],
        scratch_shapes=[pltpu.VMEM((1024, 128), F32)],
        name=name, compiler_params=_cparams(("arbitrary", "arbitrary")),
    )(*args)
    return outs


def _hg_chunk(qr, fr, ir, gr, state_t, p0, p1, norm_w, valid):
    Q = qr.shape[0]
    lb = jax.nn.sigmoid(p0 - p1)
    f = lb + (1.0 - lb) * jax.nn.sigmoid(fr)
    k = 1.0 - f
    q = _silu(qr)
    v = ir * valid
    cum = _cumsum_rows(jnp.log(f))
    cum_end = _row_of(cum, Q - 1)
    o_inter = _mm_nt(q * jnp.exp(cum), state_t)
    nblk = Q // HG_SUB
    row = lax.broadcasted_iota(jnp.int32, (Q, 1), 0)
    ri = lax.broadcasted_iota(jnp.int32, (Q, Q), 0)
    ci = lax.broadcasted_iota(jnp.int32, (Q, Q), 1)
    mids = jnp.concatenate([jnp.broadcast_to(_row_of(cum, HG_SUB * i + HG_SUB // 2 - 1), (HG_SUB, cum.shape[1]))
                            for i in range(nblk)], axis=0)
    sh = HG_SUB.bit_length() - 1
    same = (jnp.right_shift(ri, sh) == jnp.right_shift(ci, sh)) & (ri >= ci)
    att = jnp.where(same, _mm_nt(q * jnp.exp(cum - mids), k * jnp.exp(mids - cum)), 0.0)
    for i in range(1, nblk):
        lo = HG_SUB * i
        start = _row_of(cum, lo - 1)
        qa = q * jnp.exp(jnp.where((row >= lo) & (row < lo + HG_SUB), cum - start, -1e30))
        ka = k * jnp.exp(jnp.where(row < lo, start - cum, -1e30))
        att = att + _mm_nt(qa, ka)
    o = o_inter + _mm(att, v)
    new_state_t = state_t * jnp.exp(cum_end) + _mm_tn(v, k * jnp.exp(cum_end - cum))
    o = o * lax.rsqrt(jnp.mean(o * o, axis=-1, keepdims=True) + EPS) * norm_w
    return o * _silu(gr), new_state_t


HG_PER_STEP = 8
HG_COLS = 4 * 128


def _hg_fwd(qfig, lbh, nwh, pad, name):
    B, Tp, _ = qfig.shape
    nc = Tp // CHUNK
    hp = HG_PER_STEP

    def body(x_ref, lb_ref, nw_ref, y_ref, save_ref, st):
        c = pl.program_id(1)

        @pl.when(c == 0)
        def _():
            st[...] = jnp.zeros_like(st)

        valid = _valid_rows(c, pad)
        for j in range(hp):
            for b in range(B):
                s0 = st[j, b]
                save_ref[j, b] = s0
                col = lambda k: x_ref[b, :, HG_COLS * j + 128 * k:HG_COLS * j + 128 * (k + 1)]
                y, s1 = _hg_chunk(col(0), col(1), col(2), col(3), s0, lb_ref[j, 0:1, :], lb_ref[j, 1:2, :], nw_ref[j], valid)
                y_ref[b, :, 128 * j:128 * (j + 1)] = y.astype(y_ref.dtype)
                st[j, b] = s1

    return pl.pallas_call(
        body, grid=(HG_HEADS // hp, nc),
        in_specs=[pl.BlockSpec((B, CHUNK, HG_COLS * hp), lambda h, c: (0, c, h)),
                  pl.BlockSpec((hp, 2, 128), lambda h, c: (h, 0, 0)),
                  pl.BlockSpec((hp, 1, 128), lambda h, c: (h, 0, 0))],
        out_specs=[pl.BlockSpec((B, CHUNK, 128 * hp), lambda h, c: (0, c, h)),
                   pl.BlockSpec((hp, B, None, 128, 128), lambda h, c: (h, 0, c, 0, 0))],
        out_shape=[jax.ShapeDtypeStruct((B, Tp, 1024), BF16), jax.ShapeDtypeStruct((HG_HEADS, B, nc, 128, 128), F32)],
        scratch_shapes=[pltpu.VMEM((hp, B, 128, 128), F32)],
        name=name, compiler_params=_cparams(("arbitrary", "arbitrary")),
    )(qfig, lbh, nwh)


def _hg_bwd(qfig, lbh, nwh, saved, dy, pad, name, after=None):
    B, Tp, _ = qfig.shape
    nc = Tp // CHUNK
    hp = HG_PER_STEP

    def body(x_ref, lb_ref, nw_ref, sv_ref, dy_ref, dx_ref, dlb_ref, dnw_ref, dst):
        i = pl.program_id(1)
        c = nc - 1 - i

        @pl.when(i == 0)
        def _():
            dst[...] = jnp.zeros_like(dst)
            dlb_ref[...] = jnp.zeros_like(dlb_ref)
            dnw_ref[...] = jnp.zeros_like(dnw_ref)

        valid = _valid_rows(c, pad)
        fn = lambda *a: _hg_chunk(*a, valid)
        for j in range(hp):
            for b in range(B):
                col = lambda k: x_ref[b, :, HG_COLS * j + 128 * k:HG_COLS * j + 128 * (k + 1)]
                _, vjp = jax.vjp(fn, col(0), col(1), col(2), col(3), sv_ref[j, b], lb_ref[j, 0:1, :], lb_ref[j, 1:2, :], nw_ref[j])
                d4 = vjp((dy_ref[b, :, 128 * j:128 * (j + 1)].astype(F32), dst[j, b]))
                for k in range(4):
                    dx_ref[b, :, HG_COLS * j + 128 * k:HG_COLS * j + 128 * (k + 1)] = d4[k].astype(dx_ref.dtype)
                dst[j, b] = d4[4]
                dlb_ref[j, 0:1, :] += d4[5]
                dlb_ref[j, 1:2, :] += d4[6]
                dnw_ref[j, 0:1, :] += d4[7]

    acc = pl.BlockSpec((hp, 8, 128), lambda h, i: (h, 0, 0))
    in_specs = [pl.BlockSpec((B, CHUNK, HG_COLS * hp), lambda h, i: (0, nc - 1 - i, h)),
                pl.BlockSpec((hp, 2, 128), lambda h, i: (h, 0, 0)),
                pl.BlockSpec((hp, 1, 128), lambda h, i: (h, 0, 0)),
                pl.BlockSpec((hp, B, None, 128, 128), lambda h, i: (h, 0, nc - 1 - i, 0, 0)),
                pl.BlockSpec((B, CHUNK, 128 * hp), lambda h, i: (0, nc - 1 - i, h))]
    args = [qfig, lbh, nwh, saved, dy]
    if after is not None:
        body = _skip_ref(body, len(args))
        args.append(_deps(after))
        in_specs.append(_dep_spec(args[-1]))
    return pl.pallas_call(
        body, grid=(HG_HEADS // hp, nc), in_specs=in_specs,
        out_specs=[pl.BlockSpec((B, CHUNK, HG_COLS * hp), lambda h, i: (0, nc - 1 - i, h)), acc, acc],
        out_shape=[jax.ShapeDtypeStruct((B, Tp, 4096), BF16), jax.ShapeDtypeStruct((HG_HEADS, 8, 128), F32),
                   jax.ShapeDtypeStruct((HG_HEADS, 8, 128), F32)],
        scratch_shapes=[pltpu.VMEM((hp, B, 128, 128), F32)],
        name=name, compiler_params=_cparams(("arbitrary", "arbitrary")),
    )(*args)


def _adamw_math(w, g, m, v):
    m = ADAM_B1 * m + (1.0 - ADAM_B1) * g
    v = ADAM_B2 * v + (1.0 - ADAM_B2) * (g * g)
    m_hat = m / (1.0 - ADAM_B1 ** ADAM_STEP)
    v_hat = v / (1.0 - ADAM_B2 ** ADAM_STEP)
    return -ADAM_LR * (m_hat / (jnp.sqrt(v_hat) + ADAM_EPS) + ADAM_WD * w), m, v


def _adamw_many(ws, gs, ms, vs, name):
    n = len(ws)

    def body(*refs):
        for i in range(n):
            d, m, v = _adamw_math(refs[i][...], refs[n + i][...], refs[2 * n + i][...], refs[3 * n + i][...])
            refs[4 * n + i][...] = d
            refs[5 * n + i][...] = m
            refs[6 * n + i][...] = v

    vm = pl.BlockSpec(memory_space=pltpu.VMEM)
    outs = pl.pallas_call(body, in_specs=[vm] * (4 * n), out_specs=[vm] * (3 * n),
                          out_shape=[jax.ShapeDtypeStruct(w.shape, F32) for w in ws] * 3, name=name)(*ws, *gs, *ms, *vs)
    return outs[:n], outs[n:2 * n], outs[2 * n:]


def _adamw(w, g, m, v, name, after=None):
    R, C = w.shape
    tr = max(t for t in range(8, R + 1, 8) if R % t == 0 and (t * C * 4 <= ADAMW_BLOCK_BYTES or t == 8))

    def body(w_ref, g_ref, m_ref, v_ref, d_ref, mo_ref, vo_ref):
        d_ref[...], mo_ref[...], vo_ref[...] = _adamw_math(w_ref[...], g_ref[...], m_ref[...], v_ref[...])

    sp = pl.BlockSpec((tr, C), lambda i: (i, 0))
    sh = jax.ShapeDtypeStruct((R, C), F32)
    in_specs, args = [sp] * 4, [w, g, m, v]
    if after is not None:
        body = _skip_ref(body, len(args))
        args.append(_deps(after))
        in_specs.append(_dep_spec(args[-1]))
    return pl.pallas_call(body, grid=(R // tr,), in_specs=in_specs, out_specs=[sp] * 3, out_shape=[sh] * 3,
                          name=name, compiler_params=_cparams(("arbitrary",)))(*args)


def _ffn_fwd(h, norm_w, w_gu, w_down, tag, after_norm=None):
    n = _rms_fwd(h, norm_w, f"{tag}_norm")
    if after_norm is not None:
        after_norm(n)
    gu, a = _gu_swiglu(n, w_gu, f"{tag}_gu")
    out = _matmul(a, w_down, mode="nn", out_dtype=F32, alpha=0.5, res=h, name=f"{tag}_down")
    return out, (n, gu, a)


def _ffn_bwd(h, norm_w, w_gu, w_down, saved, dout, tag, after_dw_down=None, token_seqs=None):
    n, gu, a = saved
    dgu = _d_swiglu(dout, w_down, gu, 0.5, f"{tag}_d_gu")
    dw_down = _matmul(a, dout, mode="tn", out_dtype=F32, alpha=0.5, name=f"{tag}_dw_down")
    dw_gu = _matmul(n, dgu, mode="tn", out_dtype=F32, out_groups=N_CHIPS, name=f"{tag}_dw_gu",
                    after=after_dw_down(dw_down) if after_dw_down else None)
    dn = _matmul(dgu, w_gu, mode="nt", out_dtype=F32, name=f"{tag}_d_norm", after=dw_gu)
    if token_seqs is None:
        dh, dnw = _rms_bwd(h, norm_w, dn, dout, f"{tag}_d_in")
    else:
        dx, dm, dnw = _rms_bwd_tokens(h, norm_w, dn, dout, token_seqs, f"{tag}_d_in")
        dh = (dx, dm)
    return dh, dnw, dw_gu, dw_down


def _split_w_in(w_in_full):
    pts = [0]
    for s in IN_SIZES:
        pts.append(pts[-1] + s)
    sl = lambda i, j: w_in_full[:, pts[i]:pts[j]]
    qfig = sl(3, 7).reshape(D_MODEL, 4, HG_HEADS, 128).transpose(0, 2, 1, 3).reshape(D_MODEL, 4 * D_MODEL)
    return {"z": sl(0, 1), "xbc": sl(1, 2), "dt": jnp.pad(sl(2, 3), ((0, 0), (0, 128 - SSD_HEADS))),
            "qfig": qfig, "gates": sl(7, 9)}


def _local_step(x, target, W):
    B, S, _ = x.shape
    T = N_META + S
    pad = (-T) % CHUNK
    Tp = T + pad
    assert pad + N_META == CHUNK
    R = B * Tp
    meta = jnp.broadcast_to(W["meta_tokens"][None], (B, N_META, D_MODEL))
    h0 = jnp.concatenate([jnp.zeros((B, pad, D_MODEL), F32), meta, x], axis=1).reshape(R, D_MODEL)

    stage = W.get("_stage", lambda name, x: {})
    W = dict(W)
    h1, sv1 = _ffn_fwd(h0, W["ffn1_norm"], W["ffn1_w_gu"], W["ffn1_w_down"], "ffn1", lambda n: W.update(stage("ffn1_norm", n)))
    W.update(stage("ffn1_out", h1))
    um = _rms_fwd(h1, W["mix_norm"], "mix_norm")
    wi = W["w_in"]
    z = _matmul(um, wi["z"], mode="nn", out_dtype=BF16, name="in_z")
    xbc = _matmul(um, wi["xbc"], mode="nn", out_dtype=F32, name="in_xbc")
    dtr = _matmul(um, wi["dt"], mode="nn", out_dtype=F32, name="in_dt")
    qfig = _matmul(um, wi["qfig"], mode="nn", out_dtype=F32, name="in_qfig")
    gates = _matmul(um, wi["gates"], mode="nn", out_dtype=BF16, name="in_gates")

    r3 = lambda t: t.reshape(B, Tp, t.shape[-1])
    lane_pad = lambda t: jnp.pad(t, ((0, 0), (0, 128 - t.shape[1])))
    dt_bias, a_log, dskip = lane_pad(W["ssd_dt_bias"]), lane_pad(W["ssd_a_log"]), lane_pad(W["ssd_d"])
    xact = _conv_fwd(r3(xbc), W["ssd_conv_w"], W["ssd_conv_b"], pad, "conv_fwd")
    ya, ssd_saved = _ssd_fwd(xact, r3(dtr), r3(z), dt_bias, a_log, dskip, W["ssd_norm"], pad, "ssd_fwd")
    lbh = W["hg_lower_bound"].reshape(2, HG_HEADS, 128).transpose(1, 0, 2)
    nwh = W["hg_norm"].reshape(HG_HEADS, 1, 128)
    yb, hg_saved = _hg_fwd(r3(qfig), lbh, nwh, pad, "hg_fwd")
    ya2, yb2 = ya.reshape(R, -1), yb.reshape(R, -1)
    W.update(stage("mixers_out", yb2))
    pa, pb, mg = _branch_merge(ya2, yb2, W["w_branch_a"], W["w_branch_b"], gates, "branch_merge")
    h2 = _matmul(mg, W["w_out"], mode="nn", out_dtype=F32, res=h1, name="mix_out")
    h3, sv2 = _ffn_fwd(h2, W["ffn2_norm"], W["ffn2_w_gu"], W["ffn2_w_down"], "ffn2")

    loss, dh3, d_final = _loss_head(h3, W["final_norm"].reshape(1, D_MODEL), target, B, "loss_head")

    G = {"final_norm": d_final[0]}
    dh2, dnw, G["ffn2_w_gu"], G["ffn2_w_down"] = _ffn_bwd(h2, W["ffn2_norm"], W["ffn2_w_gu"], W["ffn2_w_down"], sv2, dh3, "ffn2")
    G["ffn2_norm"] = dnw[0:1]
    dmg = _matmul(dh2, W["w_out"], mode="nt", out_dtype=BF16, name="d_merge")
    G["w_out"] = _matmul(mg, dh2, mode="tn", out_dtype=F32, name="dw_out")
    dpa, dpb, dgates, dya, dyb = _branch_merge_bwd(pa, pb, gates, dmg, W["w_branch_a"], W["w_branch_b"], "branch_merge_bwd")
    G["w_branch_a"] = _matmul(ya2, dpa, mode="tn", out_dtype=F32, name="dw_branch_a")
    G["w_branch_b"] = _matmul(yb2, dpb, mode="tn", out_dtype=F32, name="dw_branch_b")

    dxact, ddtr, dz, dpar, dnw = _ssd_bwd(xact, r3(dtr), r3(z), dt_bias, a_log, dskip, W["ssd_norm"], ssd_saved,
                                          r3(dya), pad, "ssd_bwd", after=stage("late_grads", G).get("_after"))
    G["ssd_dt_bias"], G["ssd_a_log"], G["ssd_d"] = dpar[0:1, :SSD_HEADS], dpar[1:2, :SSD_HEADS], dpar[2:3, :SSD_HEADS]
    G["ssd_norm"] = dnw[0:1]
    dxbc, dcw, dcb = _conv_bwd(r3(xbc), W["ssd_conv_w"], W["ssd_conv_b"], dxact, pad, "conv_bwd")
    G["ssd_conv_w"], G["ssd_conv_b"] = dcw[0:SSD_CONV], dcb[0:1]
    dqfig, dlb, dhn = _hg_bwd(r3(qfig), lbh, nwh, hg_saved, r3(dyb), pad, "hg_bwd",
                              after=stage("after_conv_bwd", dcb).get("_after"))
    G["hg_lower_bound"] = dlb[:, 0:2, :].transpose(1, 0, 2).reshape(2, D_MODEL)
    G["hg_norm"] = dhn[:, 0, :].reshape(1, D_MODEL)

    r2 = lambda t: t.reshape(R, t.shape[-1])
    pieces = [("z", r2(dz)), ("xbc", r2(dxbc)), ("dt", r2(ddtr)), ("qfig", r2(dqfig)), ("gates", dgates)]
    dum = _sum_nt([p for _, p in pieces], [wi[nm] for nm, _ in pieces], "d_mix")
    dwi = {nm: _matmul(um, dpiece, mode="tn", out_dtype=F32, name=f"dw_in_{nm}") for nm, dpiece in pieces}
    dw_qfig = dwi["qfig"].reshape(D_MODEL, HG_HEADS, 4, 128).transpose(0, 2, 1, 3).reshape(D_MODEL, 4 * D_MODEL)
    G["w_in"] = jnp.concatenate([dwi["z"], dwi["xbc"], dwi["dt"][:, :SSD_HEADS], dw_qfig, dwi["gates"]], axis=1)
    dh1, dnw = _rms_bwd(h1, W["mix_norm"], dum, dh2, "mix_norm_bwd", after=stage("w_in_grads", dwi).get("_after"))
    G["mix_norm"] = dnw[0:1]
    (dx, dfirst), dnw, G["ffn1_w_gu"], G["ffn1_w_down"] = _ffn_bwd(
        h0, W["ffn1_norm"], W["ffn1_w_gu"], W["ffn1_w_down"], sv1, dh1, "ffn1",
        lambda dw: stage("ffn1_dw_down", dw).get("_after"), token_seqs=B)
    G["ffn1_norm"] = dnw[0:1]
    G["meta_tokens"] = jnp.sum(dfirst[:, pad:CHUNK], axis=0)
    return loss, dx, G


ANY = pl.BlockSpec(memory_space=pl.ANY)


def _place():
    return lax.axis_index("x"), lax.axis_index("y"), lax.axis_index("c")


def _other_chips(x, y):
    return [(1 - x, y), (x, 1 - y), (1 - x, 1 - y)]


def _remote(src, dst, ssem, rsem, dev):
    return pltpu.make_async_remote_copy(src_ref=src, dst_ref=dst, send_sem=ssem, recv_sem=rsem,
                                        device_id=dev, device_id_type=MESH)


def _exchange8(buf, name):
    n, w = buf.shape

    def body(x_ref, out_ref, ssem, rsem):
        x, y, c = _place()
        me = 4 * x + 2 * y + c
        out_ref[me] = x_ref[...]
        copies = []
        for k in range(1, 8):
            px = 1 - x if (k >> 2) & 1 else x
            py = 1 - y if (k >> 1) & 1 else y
            pc = 1 - c if k & 1 else c
            cp = _remote(x_ref, out_ref.at[me], ssem.at[k - 1], rsem.at[k - 1], (px, py, pc))
            cp.start()
            copies.append((cp, 4 * px + 2 * py + pc))
        for k, (cp, peer) in enumerate(copies):
            _remote(x_ref, out_ref.at[peer], ssem.at[k], rsem.at[k], (x, y, c)).wait_recv()
        for cp, _ in copies:
            cp.wait_send()

    vm = pl.BlockSpec(memory_space=pltpu.VMEM)
    return pl.pallas_call(
        body, in_specs=[vm], out_specs=vm, out_shape=jax.ShapeDtypeStruct((8, n, w), F32),
        scratch_shapes=[pltpu.SemaphoreType.DMA((7,)), pltpu.SemaphoreType.DMA((7,))], name=name,
    )(buf)


HBM = pltpu.MemorySpace.HBM


def _sequencer(name, collective_id, sems, sent):
    return functools.partial(pl.kernel, mesh=plsc.ScalarSubcoreMesh(axis_name="sequencer", num_cores=1), name=name,
                             scratch_types=sems, compiler_params=pltpu.CompilerParams(collective_id=collective_id),
                             cost_estimate=pl.CostEstimate(flops=0, transcendentals=0, bytes_accessed=2 * sent,
                                                           remote_bytes_transferred=sent))


def _nbytes(arrays):
    return sum(a.size * a.dtype.itemsize for a in arrays)


def _handshake(peers):
    barrier = pltpu.get_barrier_semaphore()
    for peer in peers:
        pl.semaphore_signal(barrier, inc=1, device_id=peer, device_id_type=MESH)
    pl.semaphore_wait(barrier, len(peers))


def _gather_seq(blocks, name, collective_id):
    n = len(blocks)
    half = [s.shape[1] // 2 for s in blocks]
    full = [jax.new_ref(b, memory_space=HBM) for b in blocks]

    @_sequencer(name, collective_id, [pltpu.SemaphoreType.DMA((n, 3))] * 4, _nbytes(blocks) * 3 // 4)
    def launch(ssem, rsem, fssem, frsem):
        x, y, c = _place()
        q = 2 * x + y
        chips = _other_chips(x, y)
        _handshake([(px, py, c) for px, py in chips] + [(x, y, 1 - c)])
        piece = lambda s, qq, cc: full[s].at[qq, pl.ds(cc * half[s], half[s])]
        sends = []
        for j, (px, py) in enumerate(chips):
            for s in range(n):
                cp = _remote(piece(s, q, c), piece(s, q, c), ssem.at[s, j], rsem.at[s, j], (px, py, c))
                cp.start()
                sends.append(cp)
        for j, (px, py) in enumerate(chips):
            for s in range(n):
                got = piece(s, 2 * px + py, c)
                _remote(got, got, ssem.at[s, j], rsem.at[s, j], (px, py, c)).wait_recv()
                cp = _remote(got, got, fssem.at[s, j], frsem.at[s, j], (x, y, 1 - c))
                cp.start()
                sends.append(cp)
        for j, (px, py) in enumerate(chips):
            for s in range(n):
                got = piece(s, 2 * px + py, 1 - c)
                _remote(got, got, fssem.at[s, j], frsem.at[s, j], (x, y, 1 - c)).wait_recv()
        for cp in sends:
            cp.wait_send()

    launch()
    return [r[...] for r in full]


def _share8(buf, name, collective_id):
    n, w = buf.shape
    src = jax.new_ref(buf, memory_space=HBM)
    out = jax.empty_ref(jax.ShapeDtypeStruct((8, n, w), F32), memory_space=HBM)

    @_sequencer(name, collective_id, [pltpu.SemaphoreType.DMA((7,)), pltpu.SemaphoreType.DMA((7,)), pltpu.SemaphoreType.DMA((1,))],
                7 * buf.size * 4)
    def launch(ssem, rsem, lsem):
        x, y, c = _place()
        me = 4 * x + 2 * y + c
        peers = [(1 - x if (k >> 2) & 1 else x, 1 - y if (k >> 1) & 1 else y, 1 - c if k & 1 else c) for k in range(1, 8)]
        _handshake(peers)
        mine = pltpu.make_async_copy(src, out.at[me], lsem.at[0])
        mine.start()
        sends = []
        for k, peer in enumerate(peers):
            cp = _remote(src, out.at[me], ssem.at[k], rsem.at[k], peer)
            cp.start()
            sends.append(cp)
        for k, (px, py, pc) in enumerate(peers):
            slot = out.at[4 * px + 2 * py + pc]
            _remote(slot, slot, ssem.at[k], rsem.at[k], (px, py, pc)).wait_recv()
        for cp in sends:
            cp.wait_send()
        mine.wait()

    launch()
    return out[...]


def _sum_slots(slots, name, after=None):
    _, n, w = slots.shape

    def body(s_ref, o_ref):
        acc = s_ref[0]
        for d in range(1, 8):
            acc = acc + s_ref[d]
        o_ref[...] = acc

    vm = pl.BlockSpec(memory_space=pltpu.VMEM)
    in_specs, args = [vm], [slots]
    if after is not None:
        body = _skip_ref(body, 1)
        args.append(_deps(after))
        in_specs.append(vm)
    return pl.pallas_call(body, in_specs=in_specs, out_specs=vm, out_shape=jax.ShapeDtypeStruct((n, w), F32), name=name)(*args)


def _pair_swap(parts, name, collective_id):
    n = len(parts)
    half = [p.shape[1] // 2 for p in parts]
    src = [jax.new_ref(p, memory_space=HBM) for p in parts]
    got = [jax.empty_ref(jax.ShapeDtypeStruct((p.shape[0], h, p.shape[2]), p.dtype), memory_space=HBM) for p, h in zip(parts, half)]

    @_sequencer(name, collective_id, [pltpu.SemaphoreType.DMA((n,))] * 2, _nbytes(parts) // 2)
    def launch(ssem, rsem):
        x, y, c = _place()
        _handshake([(x, y, 1 - c)])
        copies = []
        for s in range(n):
            cp = _remote(src[s].at[pl.ds(0, parts[s].shape[0]), pl.ds((1 - c) * half[s], half[s])], got[s], ssem.at[s], rsem.at[s], (x, y, 1 - c))
            cp.start()
            copies.append(cp)
        for cp in copies:
            cp.wait_recv()
        for cp in copies:
            cp.wait_send()

    launch()
    return [g[...] for g in got]


def _to_owners(sums, name, collective_id):
    n = len(sums)
    src = [jax.new_ref(s, memory_space=HBM) for s in sums]
    got = [jax.empty_ref(jax.ShapeDtypeStruct(s.shape, s.dtype), memory_space=HBM) for s in sums]

    @_sequencer(name, collective_id, [pltpu.SemaphoreType.DMA((n, 3))] * 2, _nbytes(sums) * 3 // 4)
    def launch(ssem, rsem):
        x, y, c = _place()
        q = 2 * x + y
        chips = _other_chips(x, y)
        _handshake([(px, py, c) for px, py in chips])
        sends = []
        for j, (px, py) in enumerate(chips):
            for s in range(n):
                cp = _remote(src[s].at[2 * px + py], got[s].at[q], ssem.at[s, j], rsem.at[s, j], (px, py, c))
                cp.start()
                sends.append(cp)
        for j, (px, py) in enumerate(chips):
            for s in range(n):
                slot = got[s].at[2 * px + py]
                _remote(slot, slot, ssem.at[s, j], rsem.at[s, j], (px, py, c)).wait_recv()
        for cp in sends:
            cp.wait_send()

    launch()
    return [g[...] for g in got]


def _pair_join(blocks, name, collective_id):
    n = len(blocks)
    out = [jax.new_ref(b, memory_space=HBM) for b in blocks]

    @_sequencer(name, collective_id, [pltpu.SemaphoreType.DMA((n,))] * 2, _nbytes(blocks) // 2)
    def launch(ssem, rsem):
        x, y, c = _place()
        _handshake([(x, y, 1 - c)])
        sends = []
        for s in range(n):
            h = blocks[s].shape[0] // 2
            mine = out[s].at[pl.ds(c * h, h)]
            cp = _remote(mine, mine, ssem.at[s], rsem.at[s], (x, y, 1 - c))
            cp.start()
            sends.append(cp)
        for s in range(n):
            h = blocks[s].shape[0] // 2
            theirs = out[s].at[pl.ds((1 - c) * h, h)]
            _remote(theirs, theirs, ssem.at[s], rsem.at[s], (x, y, 1 - c)).wait_recv()
        for cp in sends:
            cp.wait_send()

    launch()
    return [o[...] for o in out]


WIRE = BF16


def _row_tile(h):
    return _pick(h, (256, 368, 352, 128, 16))


def _add_pair(part, got, c, name, after=None):
    _, h, w = got.shape
    tr = _row_tile(h)
    nt = h // tr

    def body(c_ref, p_ref, g_ref, o_ref):
        o_ref[...] = (p_ref[...] + g_ref[...].astype(F32)).astype(o_ref.dtype)

    in_specs = [pl.BlockSpec((None, tr, w), lambda q, i, c_ref: (q, c_ref[0] * nt + i, 0)),
                pl.BlockSpec((None, tr, w), lambda q, i, c_ref: (q, i, 0))]
    args = [c.reshape(1).astype(jnp.int32), part, got]
    if after is not None:
        body = _skip_ref(body, len(args))
        args.append(_deps(after))
        in_specs.append(_dep_spec(args[-1]))
    return pl.pallas_call(
        body,
        grid_spec=pltpu.PrefetchScalarGridSpec(
            num_scalar_prefetch=1, grid=(got.shape[0], nt), in_specs=in_specs,
            out_specs=pl.BlockSpec((None, tr, w), lambda q, i, c_ref: (q, i, 0))),
        out_shape=jax.ShapeDtypeStruct(got.shape, WIRE), name=name,
        compiler_params=_cparams(("arbitrary", "arbitrary")),
    )(*args)


def _sum_chips(slots, sums, q, c, name, after=None):
    _, h, w = slots.shape
    tr = _row_tile(h)
    nt = h // tr

    def body(s_ref, mine_ref, a_ref, b_ref, d_ref, o_ref):
        o_ref[...] = ((mine_ref[...].astype(F32) + a_ref[...].astype(F32)) + b_ref[...].astype(F32)) + d_ref[...].astype(F32)

    slot = lambda k: pl.BlockSpec((None, tr, w), lambda i, s_ref: (s_ref[1 + k], i, 0))
    scalars = jnp.stack([c, q, (q + 1) % N_CHIPS, (q + 2) % N_CHIPS, (q + 3) % N_CHIPS]).astype(jnp.int32)
    in_specs, args = [slot(0), slot(1), slot(2), slot(3)], [scalars, sums, slots, slots, slots]
    if after is not None:
        body = _skip_ref(body, len(args))
        args.append(_deps(after))
        in_specs.append(_dep_spec(args[-1]))
    return pl.pallas_call(
        body,
        grid_spec=pltpu.PrefetchScalarGridSpec(
            num_scalar_prefetch=1, grid=(nt,), in_specs=in_specs,
            out_specs=pl.BlockSpec((tr, w), lambda i, s_ref: (s_ref[0] * nt + i, 0))),
        out_shape=jax.ShapeDtypeStruct((2 * h, w), F32), name=name,
        compiler_params=_cparams(("arbitrary",)),
    )(*args)


class _Reduce:
    def __init__(self, parts, q, c, tag, first_id, regions=None):
        self.parts, self.q, self.c, self.tag, self.first_id, self.regions = parts, q, c, tag, first_id, regions
        self.got = _pair_swap(parts, f"{tag}_pair_swap", first_id)

    def to_owners(self, after=None):
        self.sums = [_add_pair(p, g, self.c, f"{self.tag}_pair_add{i}", after)
                     for i, (p, g) in enumerate(zip(self.parts, self.got))]
        if self.regions is not None:
            self.sums = self.regions(self.sums)
        self.slots = _to_owners(self.sums, f"{self.tag}_to_owners", self.first_id + 1)
        return self.sums

    def join(self, after=None):
        blocks = [_sum_chips(sl, sm, self.q, self.c, f"{self.tag}_sum_chips{i}", after)
                  for i, (sl, sm) in enumerate(zip(self.slots, self.sums))]
        self.out = _pair_join(blocks, f"{self.tag}_pair_join", self.first_id + 2)
        return blocks


WEIGHTS = ("meta_tokens", "ffn1_norm", "ffn1_w_gu", "ffn1_w_down", "mix_norm", "w_in", "ssd_conv_w", "ssd_conv_b",
           "ssd_dt_bias", "ssd_a_log", "ssd_d", "ssd_norm", "hg_lower_bound", "hg_norm", "w_branch_a", "w_branch_b",
           "w_out", "ffn2_norm", "ffn2_w_gu", "ffn2_w_down", "final_norm")
BIG = ("ffn1_w_gu", "ffn1_w_down", "w_in", "w_branch_a", "w_branch_b", "w_out", "ffn2_w_gu", "ffn2_w_down")
SMALL = tuple(n for n in WEIGHTS if n not in BIG)


def _rows1024(a):
    flat = a.reshape(-1)
    n = -(-flat.shape[0] // 1024) * 1024
    return jnp.pad(flat, (0, n - flat.shape[0])).reshape(-1, 1024)


def kernel(x, meta_tokens, ffn1_norm, ffn1_w_gu, ffn1_w_down, mix_norm, w_in, ssd_conv_w, ssd_conv_b, ssd_dt_bias, ssd_a_log, ssd_d, ssd_norm, hg_lower_bound, hg_norm, w_branch_a, w_branch_b, w_out, ffn2_norm, ffn2_w_gu, ffn2_w_down, final_norm, loss_target, m_meta_tokens, m_ffn1_norm, m_ffn1_w_gu, m_ffn1_w_down, m_mix_norm, m_w_in, m_ssd_conv_w, m_ssd_conv_b, m_ssd_dt_bias, m_ssd_a_log, m_ssd_d, m_ssd_norm, m_hg_lower_bound, m_hg_norm, m_w_branch_a, m_w_branch_b, m_w_out, m_ffn2_norm, m_ffn2_w_gu, m_ffn2_w_down, m_final_norm, v_meta_tokens, v_ffn1_norm, v_ffn1_w_gu, v_ffn1_w_down, v_mix_norm, v_w_in, v_ssd_conv_w, v_ssd_conv_b, v_ssd_dt_bias, v_ssd_a_log, v_ssd_d, v_ssd_norm, v_hg_lower_bound, v_hg_norm, v_w_branch_a, v_w_branch_b, v_w_out, v_ffn2_norm, v_ffn2_w_gu, v_ffn2_w_down, v_final_norm):
    P = dict(zip(WEIGHTS, (meta_tokens, ffn1_norm, ffn1_w_gu, ffn1_w_down, mix_norm, w_in, ssd_conv_w, ssd_conv_b, ssd_dt_bias, ssd_a_log, ssd_d, ssd_norm, hg_lower_bound, hg_norm, w_branch_a, w_branch_b, w_out, ffn2_norm, ffn2_w_gu, ffn2_w_down, final_norm)))
    M = dict(zip(WEIGHTS, (m_meta_tokens, m_ffn1_norm, m_ffn1_w_gu, m_ffn1_w_down, m_mix_norm, m_w_in, m_ssd_conv_w, m_ssd_conv_b, m_ssd_dt_bias, m_ssd_a_log, m_ssd_d, m_ssd_norm, m_hg_lower_bound, m_hg_norm, m_w_branch_a, m_w_branch_b, m_w_out, m_ffn2_norm, m_ffn2_w_gu, m_ffn2_w_down, m_final_norm)))
    V = dict(zip(WEIGHTS, (v_meta_tokens, v_ffn1_norm, v_ffn1_w_gu, v_ffn1_w_down, v_mix_norm, v_w_in, v_ssd_conv_w, v_ssd_conv_b, v_ssd_dt_bias, v_ssd_a_log, v_ssd_d, v_ssd_norm, v_hg_lower_bound, v_hg_norm, v_w_branch_a, v_w_branch_b, v_w_out, v_ffn2_norm, v_ffn2_w_gu, v_ffn2_w_down, v_final_norm)))
    cx, cy, cc = _place()
    q = 2 * cx + cy

    mine = jnp.concatenate([meta_tokens.reshape(4, 1024), ssd_conv_w.reshape(2, 1024), jnp.zeros((2, 1024), F32)], axis=0)
    every = _exchange8(mine, "gather_small")
    meta_full = jnp.concatenate([every[2 * k, 0:4].reshape(N_META, 256) for k in range(N_CHIPS)], axis=1)
    conv_w_full = jnp.concatenate([every[2 * k, 4:6].reshape(SSD_CONV, 512) for k in range(N_CHIPS)], axis=1)

    late = ("ffn2_w_down", "w_branch_a", "w_branch_b", "w_out")
    rows = jnp.concatenate([P[n][0] for n in late], axis=0)
    zero = lambda t, dtype=F32: (t[0:1, 0:1] * 0).astype(dtype)

    def in_slot(s, after=None):
        s = s if after is None else s + zero(after)
        return lax.dynamic_update_slice(lax.empty((N_CHIPS,) + s.shape, BF16), s.astype(BF16)[None], (q, 0, 0))

    gu1, down1 = _gather_seq([in_slot(ffn1_w_gu[0]), in_slot(ffn1_w_down[0])], "gather_ffn1", 1)
    W = {n: P[n] for n in SMALL}
    W["meta_tokens"], W["ssd_conv_w"] = meta_full, conv_w_full
    W["ffn1_w_gu"], W["ffn1_w_down"] = gu1, down1.reshape(-1, D_MODEL)
    flying = {}

    def stage(name, t):
        if name == "ffn1_norm":
            flying["w_in"] = _gather_seq([in_slot(w_in[0], t)], "gather_w_in", 2)
            return {}
        if name == "ffn1_out":
            flying["late"] = _gather_seq([in_slot(ffn2_w_gu[0], t), in_slot(rows, t)], "gather_late", 3)
            (w_in_all,) = flying["w_in"]
            w_in_all = w_in_all + zero(t, BF16)
            return {"w_in": _split_w_in(w_in_all.transpose(1, 0, 2).reshape(D_MODEL, -1))}
        if name == "mixers_out":
            gu2, rows_all = flying["late"]
            out, r = {"ffn2_w_gu": gu2}, 0
            for n in late:
                nr = P[n].shape[1]
                out[n] = (rows_all[:, r:r + nr] + zero(t, BF16)).reshape(N_CHIPS * nr, D_MODEL)
                r += nr
            return out
        if name == "late_grads":
            row_parts = jnp.concatenate([t[n].reshape(N_CHIPS, -1, D_MODEL) for n in late], axis=1)
            flying["grad_late"] = _Reduce([t["ffn2_w_gu"], row_parts], q, cc, "grad_late", 4)
            return {"_after": [t["ffn2_w_gu"]] + [t[n] for n in late]}
        if name == "after_conv_bwd":
            return {"_after": flying["grad_late"].to_owners(after=t)}
        if name == "w_in_grads":
            order = ("z", "xbc", "dt", "qfig", "gates")
            blocks = flying["grad_late"].join(after=[t[k] for k in order])

            def regions(sums):
                z, xbc, dt, qfig, gates = [s[0] for s in sums]
                h = z.shape[0]
                qfig = qfig.reshape(h, HG_HEADS, 4, 128).transpose(0, 2, 1, 3).reshape(h, 4 * D_MODEL)
                cols = jnp.concatenate([z, xbc, dt[:, :SSD_HEADS], qfig, gates], axis=1)
                return [cols.reshape(h, N_CHIPS, -1).transpose(1, 0, 2)]

            flying["grad_w_in"] = _Reduce([t[k][None] for k in order], q, cc, "grad_w_in", 7, regions)
            return {"_after": blocks}
        if name == "ffn1_dw_down":
            return {"_after": flying["grad_w_in"].to_owners(after=t)}
        return {}

    W["_stage"] = stage

    loss8, grad_x, G = _local_step(x, loss_target, W)

    small = jnp.concatenate(
        [G["meta_tokens"]] + [_rows1024(G[n]) for n in SMALL if n != "meta_tokens"] + [_rows1024(loss8[0:1, 0:1])], axis=0)
    small = jnp.pad(small, ((0, 40 - small.shape[0]), (0, 0)))
    small_slots = _share8(small, "share_small", 13)

    grad_ffn1 = _Reduce([G["ffn1_w_gu"], G["ffn1_w_down"].reshape(N_CHIPS, -1, D_MODEL)], q, cc, "grad_ffn1", 10)
    flying["grad_w_in"].join(after=grad_x)
    going = grad_ffn1.to_owners(after=grad_x)
    g_gu2, g_rows = flying["grad_late"].out
    (g_w_in,) = flying["grad_w_in"].out
    Gb = {"ffn2_w_gu": g_gu2, "w_in": g_w_in}
    r = 0
    for n in late:
        nr = P[n].shape[1]
        Gb[n] = g_rows[r:r + nr]
        r += nr

    grads, delta, new_m, new_v, done = {}, {}, {}, {}, []
    cols = w_in.shape[2]
    to_tiles = lambda a: a.transpose(2, 0, 1).reshape(cols, 8, 128).reshape(cols * 8, 128)
    from_tiles = lambda a: a.reshape(cols, 1, D_MODEL).transpose(1, 2, 0)
    for n in [n for n in BIG if n in Gb]:
        if n == "w_in":
            g_t = to_tiles(Gb[n][None])
            d_, m_, v_ = _adamw(to_tiles(P[n]), g_t, to_tiles(M[n]), to_tiles(V[n]), f"adamw_{n}", after=going)
            grads[n], delta[n], new_m[n], new_v[n] = from_tiles(g_t), from_tiles(d_), from_tiles(m_), from_tiles(v_)
        else:
            d_, m_, v_ = _adamw(P[n][0], Gb[n], M[n][0], V[n][0], f"adamw_{n}", after=going)
            grads[n], delta[n], new_m[n], new_v[n] = Gb[n][None], d_[None], m_[None], v_[None]
        done.append(d_)

    small = _sum_slots(small_slots, "sum_small", after=done)
    Gs = {"meta_tokens": small[0:N_META]}
    r = N_META
    for n in SMALL:
        if n == "meta_tokens":
            continue
        nr = -(-G[n].size // 1024)
        Gs[n] = small[r:r + nr].reshape(-1)[:G[n].size].reshape(G[n].shape)
        r += nr
    loss = small[r, 0]
    Gs["meta_tokens"] = lax.dynamic_slice(Gs["meta_tokens"], (0, 256 * q), (N_META, 256))
    Gs["ssd_conv_w"] = lax.dynamic_slice(Gs["ssd_conv_w"], (0, 512 * q), (SSD_CONV, 512))[None]
    Gs = {n: Gs[n].reshape(P[n].shape) for n in SMALL}
    grads.update(Gs)
    flat = lambda a: a.reshape(-1, a.shape[-1])
    d_s, m_s, v_s = _adamw_many(*[[flat(D[n]) for n in SMALL] for D in (P, Gs, M, V)], "adamw_small")
    for i, n in enumerate(SMALL):
        delta[n], new_m[n], new_v[n] = d_s[i].reshape(P[n].shape), m_s[i].reshape(P[n].shape), v_s[i].reshape(P[n].shape)
    done.append(d_s[0])
    grad_ffn1.join(after=done)
    Gb["ffn1_w_gu"], Gb["ffn1_w_down"] = grad_ffn1.out
    for n in ("ffn1_w_gu", "ffn1_w_down"):
        d_, m_, v_ = _adamw(P[n][0], Gb[n], M[n][0], V[n][0], f"adamw_{n}")
        grads[n], delta[n], new_m[n], new_v[n] = Gb[n][None], d_[None], m_[None], v_[None]
    return (loss, grad_x, *[grads[n] for n in WEIGHTS], *[delta[n] for n in WEIGHTS],
            *[new_m[n] for n in WEIGHTS], *[new_v[n] for n in WEIGHTS])
```

```python
import functools

import jax
import jax.numpy as jnp
from jax import lax
from jax.experimental import pallas as pl
from jax.experimental.pallas import tpu as pltpu
from jax.experimental.pallas import tpu_sc as plsc

F32 = jnp.float32
BF16 = jnp.bfloat16
HIGHEST = lax.Precision.HIGHEST
MESH = pl.DeviceIdType.MESH

D_MODEL = 1024
N_META = 16
EPS = 1e-6
SSD_HEADS = 16
SSD_HEAD_DIM = 64
SSD_INNER = 1024
SSD_GROUPS = 4
SSD_STATE = 128
SSD_CONV = 4
SSD_CONV_CH = 2048
HG_HEADS = 8
HG_SUB = 32
CHUNK = 128
D_FF = 2816
N_CHIPS = 4
IN_SIZES = (1024, 2048, 16, 1024, 1024, 1024, 1024, 1024, 1024)
ADAM_LR = 0.001
ADAM_B1 = 0.9
ADAM_B2 = 0.999
ADAM_EPS = 1e-08
ADAM_WD = 0.01
ADAM_STEP = 10
VMEM_LIMIT = 56 * 1024 * 1024
MATMUL_BLOCK_BYTES = 42 * 1024 * 1024
ADAMW_BLOCK_BYTES = 5 * 512 * 1024


def _cparams(sem=None):
    return pltpu.CompilerParams(dimension_semantics=sem, vmem_limit_bytes=VMEM_LIMIT)


def _pick(n, cands):
    for c in cands:
        if n % c == 0:
            return c
    return n


def _deps(after):
    xs = after if isinstance(after, (list, tuple)) else [after]
    one = lambda x: lax.slice(x, (0,) * x.ndim, (1,) * x.ndim).reshape(1).astype(F32)
    return jnp.concatenate([one(x) for x in xs]).reshape(1, -1)


def _dep_spec(dep):
    return pl.BlockSpec(dep.shape, lambda *_: (0, 0))


def _skip_ref(body, pos):
    return lambda *refs: body(*refs[:pos], *refs[pos + 1:])


def _dg(a, b, ca, cb):
    return lax.dot_general(a.astype(BF16), b.astype(BF16), (((ca,), (cb,)), ((), ())), preferred_element_type=F32)


@jax.custom_vjp
def _mm(a, b):
    return _dg(a, b, 1, 0)


def _mm_fwd(a, b):
    return _dg(a, b, 1, 0), (a, b)


def _mm_bwd(r, g):
    a, b = r
    return _dg(g, b, 1, 1), _dg(a, g, 0, 0)


_mm.defvjp(_mm_fwd, _mm_bwd)


@jax.custom_vjp
def _mm_nt(a, b):
    return _dg(a, b, 1, 1)


def _mm_nt_fwd(a, b):
    return _dg(a, b, 1, 1), (a, b)


def _mm_nt_bwd(r, g):
    a, b = r
    return _dg(g, b, 1, 0), _dg(g, a, 0, 0)


_mm_nt.defvjp(_mm_nt_fwd, _mm_nt_bwd)


@jax.custom_vjp
def _mm_tn(a, b):
    return _dg(a, b, 0, 0)


def _mm_tn_fwd(a, b):
    return _dg(a, b, 0, 0), (a, b)


def _mm_tn_bwd(r, g):
    a, b = r
    return _dg(b, g, 1, 1), _dg(a, g, 1, 0)


_mm_tn.defvjp(_mm_tn_fwd, _mm_tn_bwd)


def _tri_sum(x, lower):
    n = x.shape[0]
    ri = lax.broadcasted_iota(jnp.int32, (n, n), 0)
    ci = lax.broadcasted_iota(jnp.int32, (n, n), 1)
    tri = ((ri >= ci) if lower else (ri <= ci)).astype(BF16)
    x1 = x.astype(BF16)
    r1 = x - x1.astype(F32)
    x2 = r1.astype(BF16)
    x3 = (r1 - x2.astype(F32)).astype(BF16)
    dot = lambda p: lax.dot_general(tri, p, (((1,), (0,)), ((), ())), preferred_element_type=F32)
    return (dot(x3) + dot(x2)) + dot(x1)


@jax.custom_vjp
def _cumsum_rows(x):
    return _tri_sum(x, True)


_cumsum_rows.defvjp(lambda x: (_tri_sum(x, True), None), lambda _, g: (_tri_sum(g, False),))


def _silu(x):
    return x * jax.nn.sigmoid(x)


def _softplus(x):
    return jnp.maximum(x, 0.0) + jnp.log(1.0 + jnp.exp(-jnp.abs(x)))


def _tril(n):
    ri = lax.broadcasted_iota(jnp.int32, (n, n), 0)
    ci = lax.broadcasted_iota(jnp.int32, (n, n), 1)
    return ri >= ci


def _row_of(m, r):
    sub = lax.broadcasted_iota(jnp.int32, (m.shape[0], 1), 0)
    return jnp.sum(jnp.where(sub == r, m, 0.0), axis=0, keepdims=True)


def _col_of(m, c):
    lane = lax.broadcasted_iota(jnp.int32, (1, m.shape[1]), 1)
    return jnp.sum(jnp.where(lane == c, m, 0.0), axis=1, keepdims=True)


def _matmul(a, b, *, mode, out_dtype, name, alpha=1.0, res=None, tm=None, tn=None, out_groups=None, after=None):
    b3 = b.ndim == 3
    if mode == "nn":
        M, K = a.shape
        G = b.shape[0] if b3 else 1
        Ng = b.shape[-1]
        N = G * Ng
    elif mode == "nt":
        M, K = a.shape
        G = b.shape[0] if b3 else 1
        N = b.shape[-2]
        Kg = b.shape[-1]
        assert G * Kg == K
    else:
        K, M = a.shape
        N = b.shape[1]
        G = out_groups or 1
        Ng = N // G
    has_res = res is not None
    split_n = (mode == "nn" and b3) or (mode == "tn" and G > 1)
    per_mn = jnp.dtype(out_dtype).itemsize + (res.dtype.itemsize if has_res else 0)
    fits = [(m_ * n_, m_, n_)
            for m_ in (4352, 2176, 1408, 1088, 1024, 544, 512, 256, 128) if M % m_ == 0
            for n_ in (2816, 2048, 1408, 1024, 512, 256, 128) if (Ng if split_n else N) % n_ == 0
            if 2 * (K * m_ * a.dtype.itemsize + K * n_ * b.dtype.itemsize + m_ * n_ * per_mn) + 4 * m_ * n_ <= MATMUL_BLOCK_BYTES]
    _, tm_fit, tn_fit = max(fits)
    tm, tn = tm or tm_fit, tn or tn_fit
    nm, nn_ = M // tm, N // tn
    assert nm * tm == M and nn_ * tn == N, (name, M, N, K, tm, tn)

    if mode == "nn":
        a_spec = pl.BlockSpec((tm, K), lambda i, j: (i, 0))
        if b3:
            ns = Ng // tn
            b_spec = pl.BlockSpec((None, K, tn), lambda i, j: (j // ns, 0, j % ns))
        else:
            b_spec = pl.BlockSpec((K, tn), lambda i, j: (0, j))
        ca, cb = 1, 0
    elif mode == "nt":
        a_spec = pl.BlockSpec((tm, K), lambda i, j: (i, 0))
        if b3:
            b_spec = pl.BlockSpec((G, tn, Kg), lambda i, j: (0, j, 0))
        else:
            b_spec = pl.BlockSpec((tn, K), lambda i, j: (j, 0))
        ca, cb = 1, 1
    else:
        a_spec = pl.BlockSpec((K, tm), lambda i, j: (0, i))
        b_spec = pl.BlockSpec((K, tn), lambda i, j: (0, j))
        ca, cb = 0, 0
    if mode == "tn" and G > 1:
        ns = Ng // tn
        o_spec = pl.BlockSpec((None, tm, tn), lambda i, j: (j // ns, i, j % ns))
        out_shape = jax.ShapeDtypeStruct((G, M, Ng), out_dtype)
    else:
        o_spec = pl.BlockSpec((tm, tn), lambda i, j: (i, j))
        out_shape = jax.ShapeDtypeStruct((M, N), out_dtype)
    in_specs = [a_spec, b_spec]
    args = [a, b]
    if has_res:
        in_specs.append(pl.BlockSpec((tm, tn), lambda i, j: (i, j)))
        args.append(res)
    if after is not None:
        args.append(_deps(after))
        in_specs.append(_dep_spec(args[-1]))

    def body(*refs):
        a_ref, b_ref, o_ref = refs[0], refs[1], refs[-1]
        if mode == "nt" and b3:
            o = _dg(a_ref[:, 0:Kg], b_ref[0], ca, cb)
            for g in range(1, G):
                o = o + _dg(a_ref[:, g * Kg:(g + 1) * Kg], b_ref[g], ca, cb)
        else:
            o = _dg(a_ref[...], b_ref[...], ca, cb)
        if alpha != 1.0:
            o = o * alpha
        if has_res:
            o = o + refs[2][...]
        o_ref[...] = o.astype(o_ref.dtype)

    return pl.pallas_call(
        body, grid=(nm, nn_), in_specs=in_specs, out_specs=o_spec, out_shape=out_shape, name=name,
        compiler_params=_cparams(("parallel", "parallel")),
    )(*args)


def _sum_nt(xs, ws, name):
    R, N = xs[0].shape[0], ws[0].shape[0]
    n = len(xs)
    per_m = sum(x.shape[1] * x.dtype.itemsize for x in xs)
    per_n = sum(w.shape[1] * w.dtype.itemsize for w in ws)
    fits = [(m_ * n_, m_, n_) for m_ in (1088, 544, 256, 128) if R % m_ == 0 for n_ in (1024, 512, 256, 128) if N % n_ == 0
            if 2 * (m_ * per_m + n_ * per_n + m_ * n_ * 4) + 4 * m_ * n_ <= MATMUL_BLOCK_BYTES]
    _, tm, tn = max(fits)

    def body(*refs):
        o = _dg(refs[0][...], refs[n][...], 1, 1)
        for p in range(1, n):
            o = o + _dg(refs[p][...], refs[n + p][...], 1, 1)
        refs[-1][...] = o

    return pl.pallas_call(
        body, grid=(R // tm, N // tn),
        in_specs=[pl.BlockSpec((tm, x.shape[1]), lambda i, j: (i, 0)) for x in xs]
        + [pl.BlockSpec((tn, w.shape[1]), lambda i, j: (j, 0)) for w in ws],
        out_specs=pl.BlockSpec((tm, tn), lambda i, j: (i, j)), out_shape=jax.ShapeDtypeStruct((R, N), F32), name=name,
        compiler_params=_cparams(("parallel", "parallel")),
    )(*xs, *ws)


def _rms_fn(h, w):
    r = lax.rsqrt(jnp.mean(h * h, axis=-1, keepdims=True) + EPS)
    return h * r * w


def _swiglu_fn(gu):
    g = gu[:, :D_FF].astype(F32)
    u = gu[:, D_FF:].astype(F32)
    return _silu(g) * u


def _merge_fn(pa, pb, gates):
    return jax.nn.sigmoid(gates[:, :D_MODEL]) * pa + jax.nn.sigmoid(gates[:, D_MODEL:]) * pb


def _rows_call(body, *, rows, tr, ins, outs, accs=(), name, after=None):
    n = rows // tr
    assert n * tr == rows
    if after is not None:
        body = _skip_ref(body, len(ins))
        ins = list(ins) + [("full", _deps(after))]

    def spec(x):
        if isinstance(x, tuple):
            shp = x[1].shape
            return pl.BlockSpec(shp, lambda i: (0,) * len(shp))
        return pl.BlockSpec((tr, x.shape[1]), lambda i: (i, 0))

    in_specs = [spec(x) for x in ins]
    args = [x[1] if isinstance(x, tuple) else x for x in ins]
    out_specs = [spec(x) for x in outs] + [pl.BlockSpec(x.shape, lambda i: (0,) * len(x.shape)) for x in accs]
    out_shape = [x[1] if isinstance(x, tuple) else x for x in outs] + list(accs)
    return pl.pallas_call(
        body, grid=(n,), in_specs=in_specs, out_specs=out_specs, out_shape=out_shape, name=name,
        compiler_params=_cparams(("arbitrary",)),
    )(*args)


def _acc_rows(ref, val):
    @pl.when(pl.program_id(0) == 0)
    def _():
        ref[...] = jnp.zeros_like(ref)

    ref[0:1, :] += val


def _rms_fwd(h, w, name):
    def body(h_ref, w_ref, o_ref):
        o_ref[...] = _rms_fn(h_ref[...], w_ref[...]).astype(o_ref.dtype)

    R = h.shape[0]
    return _rows_call(body, rows=R, tr=_pick(R, (256, 128)), ins=[h, ("full", w)],
                      outs=[jax.ShapeDtypeStruct(h.shape, BF16)], name=name)[0]


def _rms_bwd(h, w, dn, dres, name, after=None):
    def body(h_ref, w_ref, dn_ref, dres_ref, dh_ref, dw_ref):
        _, vjp = jax.vjp(_rms_fn, h_ref[...], w_ref[...])
        dh, dw = vjp(dn_ref[...].astype(F32))
        dh_ref[...] = dh + dres_ref[...]
        _acc_rows(dw_ref, dw)

    R = h.shape[0]
    return _rows_call(body, rows=R, tr=_pick(R, (256, 128)), ins=[h, ("full", w), dn, dres],
                      outs=[jax.ShapeDtypeStruct(h.shape, F32)], accs=[jax.ShapeDtypeStruct((8, D_MODEL), F32)], name=name,
                      after=after)


def _rms_bwd_tokens(h, w, dn, dres, nseq, name):
    Tp = h.shape[0] // nseq
    nc = Tp // CHUNK

    def body(h_ref, w_ref, dn_ref, dres_ref, dx_ref, dm_ref, dw_ref):
        b, c = pl.program_id(0), pl.program_id(1)
        _, vjp = jax.vjp(_rms_fn, h_ref[...], w_ref[...])
        dh, dw = vjp(dn_ref[...].astype(F32))
        dh = dh + dres_ref[...]

        @pl.when(c == 0)
        def _():
            dm_ref[...] = dh

        @pl.when(c > 0)
        def _():
            dx_ref[...] = dh

        @pl.when((b == 0) & (c == 0))
        def _():
            dw_ref[...] = jnp.zeros_like(dw_ref)

        dw_ref[0:1, :] += dw

    rows = pl.BlockSpec((CHUNK, D_MODEL), lambda b, c: (b * nc + c, 0))
    return pl.pallas_call(
        body, grid=(nseq, nc),
        in_specs=[rows, pl.BlockSpec((1, D_MODEL), lambda b, c: (0, 0)), rows, rows],
        out_specs=[pl.BlockSpec((None, CHUNK, D_MODEL), lambda b, c: (b, jnp.maximum(c - 1, 0), 0)),
                   pl.BlockSpec((None, CHUNK, D_MODEL), lambda b, c: (b, 0, 0)),
                   pl.BlockSpec((8, D_MODEL), lambda b, c: (0, 0))],
        out_shape=[jax.ShapeDtypeStruct((nseq, Tp - CHUNK, D_MODEL), F32), jax.ShapeDtypeStruct((nseq, CHUNK, D_MODEL), F32),
                   jax.ShapeDtypeStruct((8, D_MODEL), F32)],
        name=name, compiler_params=_cparams(("arbitrary", "arbitrary")),
    )(h, w, dn, dres)


def _gu_swiglu(n, w_gu, name):
    R = n.shape[0]
    G, _, ng = w_gu.shape

    def body(n_ref, w_ref, gu_ref, a_ref):
        x = n_ref[...]
        for r in range(G):
            gu_ref[:, ng * r:ng * (r + 1)] = _dg(x, w_ref[r], 1, 0).astype(gu_ref.dtype)
        a_ref[...] = _swiglu_fn(gu_ref[...]).astype(a_ref.dtype)

    return _rows_call(body, rows=R, tr=_pick(R, (256, 128)), ins=[n, ("full", w_gu)],
                      outs=[jax.ShapeDtypeStruct((R, 2 * D_FF), BF16), jax.ShapeDtypeStruct((R, D_FF), BF16)], name=name)


def _d_swiglu(dout, w_down, gu, alpha, name):
    R = gu.shape[0]

    def body(do_ref, w_ref, gu_ref, o_ref):
        da = _dg(do_ref[...] * alpha, w_ref[...], 1, 1)
        g = gu_ref[:, :D_FF].astype(F32)
        u = gu_ref[:, D_FF:].astype(F32)
        s = jax.nn.sigmoid(g)
        t = g * s
        o_ref[:, :D_FF] = (da * u * (s + t - t * s)).astype(o_ref.dtype)
        o_ref[:, D_FF:] = (da * t).astype(o_ref.dtype)

    return _rows_call(body, rows=R, tr=_pick(R, (256, 128)), ins=[dout, ("full", w_down), gu],
                      outs=[jax.ShapeDtypeStruct(gu.shape, BF16)], name=name)[0]


def _residual_matmul(a, w, res, alpha, name, norm_w=None):
    R, K = a.shape

    def body(a_ref, w_ref, r_ref, *rest):
        out = r_ref[...] + alpha * _dg(a_ref[...], w_ref[...], 1, 0)
        if norm_w is None:
            rest[0][...] = out
        else:
            rest[1][...] = out
            rest[2][...] = _rms_fn(out, rest[0][...]).astype(rest[2].dtype)

    f32 = jax.ShapeDtypeStruct((R, D_MODEL), F32)
    ins = [a, ("full", w), res] + ([] if norm_w is None else [("full", norm_w)])
    outs = [f32] + ([] if norm_w is None else [jax.ShapeDtypeStruct((R, D_MODEL), BF16)])
    got = _rows_call(body, rows=R, tr=_pick(R, (544, 256, 128)), ins=ins, outs=outs, name=name)
    return got[0] if norm_w is None else (got[0], got[1])


def _branch_merge(ya, yb, wa, wb, gates, name):
    def body(ya_ref, yb_ref, wa_ref, wb_ref, g_ref, pa_ref, pb_ref, o_ref):
        pa = _dg(ya_ref[...], wa_ref[...], 1, 0)
        pb = _dg(yb_ref[...], wb_ref[...], 1, 0)
        pa_ref[...] = pa
        pb_ref[...] = pb
        o_ref[...] = _merge_fn(pa, pb, g_ref[...].astype(F32)).astype(o_ref.dtype)

    R = ya.shape[0]
    f32 = jax.ShapeDtypeStruct((R, D_MODEL), F32)
    return _rows_call(body, rows=R, tr=_pick(R, (544, 256, 128)), ins=[ya, yb, ("full", wa), ("full", wb), gates],
                      outs=[f32, f32, jax.ShapeDtypeStruct((R, D_MODEL), BF16)], name=name)


def _branch_merge_bwd(pa, pb, gates, dm, wa, wb, name):
    def body(pa_ref, pb_ref, g_ref, dm_ref, wa_ref, wb_ref, dpa_ref, dpb_ref, dg_ref, dya_ref, dyb_ref):
        _, vjp = jax.vjp(_merge_fn, pa_ref[...], pb_ref[...], g_ref[...].astype(F32))
        dpa, dpb, dg = vjp(dm_ref[...].astype(F32))
        dpa_ref[...] = dpa.astype(dpa_ref.dtype)
        dpb_ref[...] = dpb.astype(dpb_ref.dtype)
        dg_ref[...] = dg.astype(dg_ref.dtype)
        dya_ref[...] = _dg(dpa, wa_ref[...], 1, 1).astype(dya_ref.dtype)
        dyb_ref[...] = _dg(dpb, wb_ref[...], 1, 1).astype(dyb_ref.dtype)

    R = pa.shape[0]
    b16 = jax.ShapeDtypeStruct(pa.shape, BF16)
    return _rows_call(body, rows=R, tr=_pick(R, (544, 256, 128)), ins=[pa, pb, gates, dm, ("full", wa), ("full", wb)],
                      outs=[b16, b16, jax.ShapeDtypeStruct(gates.shape, BF16), b16, b16], name=name)


def _loss_head(h3, w, target, nseq, name):
    Tp = h3.shape[0] // nseq
    nc = Tp // CHUNK

    def fn(h, w_, t, valid):
        y = _rms_fn(h, w_)
        e = (y - t) * valid
        return 0.5 * jnp.sum(jnp.mean(e * e, axis=-1, keepdims=True))

    def body(h_ref, w_ref, t_ref, loss_ref, dh_ref, dw_ref):
        b, c = pl.program_id(0), pl.program_id(1)
        valid = (c >= 1).astype(F32)
        t = t_ref[...]
        loss, vjp = jax.vjp(lambda h, w_: fn(h, w_, t, valid), h_ref[...], w_ref[...])
        dh, dw = vjp(jnp.ones((), F32))
        dh_ref[...] = dh

        @pl.when((b == 0) & (c == 0))
        def _():
            loss_ref[...] = jnp.zeros_like(loss_ref)
            dw_ref[...] = jnp.zeros_like(dw_ref)

        loss_ref[...] += jnp.full(loss_ref.shape, loss, F32)
        dw_ref[0:1, :] += dw

    return pl.pallas_call(
        body, grid=(nseq, nc),
        in_specs=[pl.BlockSpec((CHUNK, D_MODEL), lambda b, c: (b * nc + c, 0)),
                  pl.BlockSpec((1, D_MODEL), lambda b, c: (0, 0)),
                  pl.BlockSpec((None, CHUNK, D_MODEL), lambda b, c: (b, jnp.maximum(c - 1, 0), 0))],
        out_specs=[pl.BlockSpec((8, 128), lambda b, c: (0, 0)),
                   pl.BlockSpec((CHUNK, D_MODEL), lambda b, c: (b * nc + c, 0)),
                   pl.BlockSpec((8, D_MODEL), lambda b, c: (0, 0))],
        out_shape=[jax.ShapeDtypeStruct((8, 128), F32), jax.ShapeDtypeStruct(h3.shape, F32),
                   jax.ShapeDtypeStruct((8, D_MODEL), F32)],
        name=name, compiler_params=_cparams(("arbitrary", "arbitrary")),
    )(h3, w, target)


CONV_TILE = 512
CONV_HALO = 8


def _conv_fwd(xbc, w, b, pad, name):
    B, Tp, C = xbc.shape
    nch = Tp // CHUNK

    def body(x_ref, w_ref, b_ref, o_ref, xp):
        xp[0:CONV_HALO, :] = jnp.zeros((CONV_HALO, CONV_TILE), F32)
        xp[CONV_HALO:, :] = x_ref[...]
        for c in range(nch):
            acc = jnp.zeros((CHUNK, CONV_TILE), F32) + b_ref[...]
            for k in range(SSD_CONV):
                acc = acc + w_ref[k:k + 1, :] * xp[pl.ds(CONV_HALO + CHUNK * c - (SSD_CONV - 1) + k, CHUNK), :]
            out = _silu(acc)
            if CHUNK * c < pad:
                row = CHUNK * c + lax.broadcasted_iota(jnp.int32, (CHUNK, 1), 0)
                out = jnp.where(row >= pad, out, 0.0)
            o_ref[pl.ds(CHUNK * c, CHUNK), :] = out

    return pl.pallas_call(
        body, grid=(B, C // CONV_TILE),
        in_specs=[pl.BlockSpec((None, Tp, CONV_TILE), lambda i, j: (i, 0, j)),
                  pl.BlockSpec((SSD_CONV, CONV_TILE), lambda i, j: (0, j)),
                  pl.BlockSpec((1, CONV_TILE), lambda i, j: (0, j))],
        out_specs=pl.BlockSpec((None, Tp, CONV_TILE), lambda i, j: (i, 0, j)),
        out_shape=jax.ShapeDtypeStruct(xbc.shape, F32),
        scratch_shapes=[pltpu.VMEM((Tp + CONV_HALO, CONV_TILE), F32)],
        name=name, compiler_params=_cparams(("arbitrary", "arbitrary")),
    )(xbc, w, b)


def _conv_bwd(xbc, w, b, dact, pad, name):
    B, Tp, C = xbc.shape
    nch = Tp // CHUNK

    def body(x_ref, w_ref, b_ref, da_ref, dx_ref, dw_ref, db_ref, xp, dp):
        bi = pl.program_id(1)
        xp[0:CONV_HALO, :] = jnp.zeros((CONV_HALO, CONV_TILE), F32)
        xp[CONV_HALO:, :] = x_ref[...]
        dp[pl.ds(Tp, CONV_HALO), :] = jnp.zeros((CONV_HALO, CONV_TILE), F32)
        dws = [jnp.zeros((1, CONV_TILE), F32) for _ in range(SSD_CONV)]
        dbs = jnp.zeros((1, CONV_TILE), F32)
        for c in range(nch):
            xs = [xp[pl.ds(CONV_HALO + CHUNK * c - (SSD_CONV - 1) + k, CHUNK), :] for k in range(SSD_CONV)]
            acc = jnp.zeros((CHUNK, CONV_TILE), F32) + b_ref[...]
            for k in range(SSD_CONV):
                acc = acc + w_ref[k:k + 1, :] * xs[k]
            sg = jax.nn.sigmoid(acc)
            t = acc * sg
            dpre = da_ref[pl.ds(CHUNK * c, CHUNK), :] * (sg + t - t * sg)
            if CHUNK * c < pad:
                row = CHUNK * c + lax.broadcasted_iota(jnp.int32, (CHUNK, 1), 0)
                dpre = jnp.where(row >= pad, dpre, 0.0)
            dp[pl.ds(CHUNK * c, CHUNK), :] = dpre
            dbs = dbs + jnp.sum(dpre, axis=0, keepdims=True)
            for k in range(SSD_CONV):
                dws[k] = dws[k] + jnp.sum(dpre * xs[k], axis=0, keepdims=True)
        for c in range(nch):
            acc = jnp.zeros((CHUNK, CONV_TILE), F32)
            for k in range(SSD_CONV):
                acc = acc + w_ref[k:k + 1, :] * dp[pl.ds(CHUNK * c + (SSD_CONV - 1) - k, CHUNK), :]
            dx_ref[pl.ds(CHUNK * c, CHUNK), :] = acc.astype(dx_ref.dtype)

        @pl.when(bi == 0)
        def _():
            dw_ref[...] = jnp.zeros_like(dw_ref)
            db_ref[...] = jnp.zeros_like(db_ref)

        for k in range(SSD_CONV):
            dw_ref[k:k + 1, :] += dws[k]
        db_ref[0:1, :] += dbs

    return pl.pallas_call(
        body, grid=(C // CONV_TILE, B),
        in_specs=[pl.BlockSpec((None, Tp, CONV_TILE), lambda j, i: (i, 0, j)),
                  pl.BlockSpec((SSD_CONV, CONV_TILE), lambda j, i: (0, j)),
                  pl.BlockSpec((1, CONV_TILE), lambda j, i: (0, j)),
                  pl.BlockSpec((None, Tp, CONV_TILE), lambda j, i: (i, 0, j))],
        out_specs=[pl.BlockSpec((None, Tp, CONV_TILE), lambda j, i: (i, 0, j)),
                   pl.BlockSpec((8, CONV_TILE), lambda j, i: (0, j)),
                   pl.BlockSpec((8, CONV_TILE), lambda j, i: (0, j))],
        out_shape=[jax.ShapeDtypeStruct(xbc.shape, BF16), jax.ShapeDtypeStruct((8, C), F32),
                   jax.ShapeDtypeStruct((8, C), F32)],
        scratch_shapes=[pltpu.VMEM((Tp + CONV_HALO, CONV_TILE), F32), pltpu.VMEM((Tp + CONV_HALO, CONV_TILE), F32)],
        name=name, compiler_params=_cparams(("arbitrary", "arbitrary")),
    )(xbc, w, b, dact)


def _ssd_chunk(xs, bm, cm, dtr, z, state, dt_bias, a_log, dskip, norm_w, valid):
    Q = xs.shape[0]
    lane = lax.broadcasted_iota(jnp.int32, (1, 128), 1)
    dt = jnp.where(lane < SSD_HEADS, _softplus(dtr + dt_bias), 0.0) * valid
    a = dt * (-jnp.exp(a_log))
    tril = _tril(Q)
    cs = _cumsum_rows(a)
    cs_t = cs.T
    cs_end = _row_of(cs, Q - 1)
    low = lane < SSD_HEAD_DIM
    low_rows = lax.broadcasted_iota(jnp.int32, (128, 1), 0) < SSD_HEAD_DIM
    ys, new_state = [], []
    for g in range(SSD_GROUPS):
        bg = bm[:, 128 * g:128 * (g + 1)]
        cg = cm[:, 128 * g:128 * (g + 1)]
        cb = _mm_nt(cg, bg)
        for pr in range(2):
            p = 2 * g + pr
            h0, h1 = 2 * p, 2 * p + 1
            xp = xs[:, 128 * p:128 * (p + 1)]
            c0, c1 = _col_of(cs, h0), _col_of(cs, h1)
            e0, e1 = _col_of(cs_end, h0), _col_of(cs_end, h1)
            xd = xp * jnp.where(low, _col_of(dt, h0), _col_of(dt, h1))
            l0 = jnp.exp(jnp.where(tril, c0 - _row_of(cs_t, h0), -1e30))
            l1 = jnp.exp(jnp.where(tril, c1 - _row_of(cs_t, h1), -1e30))
            y_diag = jnp.where(low, _mm(cb * l0, xd), _mm(cb * l1, xd))
            to_end = jnp.where(low, jnp.exp(e0 - c0), jnp.exp(e1 - c1))
            sp = state[128 * p:128 * (p + 1), :]
            y_off = _mm_nt(cg, sp) * jnp.where(low, jnp.exp(c0), jnp.exp(c1))
            new_state.append(sp * jnp.where(low_rows, jnp.exp(e0), jnp.exp(e1)) + _mm_tn(xd * to_end, bg))
            ys.append(y_diag + y_off + xp * jnp.where(low, _col_of(dskip, h0), _col_of(dskip, h1)))
    y = jnp.concatenate(ys, axis=1) * _silu(z)
    gw = SSD_INNER // SSD_GROUPS
    outs = []
    for g in range(SSD_GROUPS):
        blk = y[:, gw * g:gw * (g + 1)]
        outs.append(blk * lax.rsqrt(jnp.mean(blk * blk, axis=-1, keepdims=True) + EPS))
    return jnp.concatenate(outs, axis=1) * norm_w, jnp.concatenate(new_state, axis=0)


def _valid_rows(c, pad):
    row = c * CHUNK + lax.broadcasted_iota(jnp.int32, (CHUNK, 1), 0)
    return (row >= pad).astype(F32)


def _ssd_fwd(xact, dtr, z, dt_bias, a_log, dskip, norm_w, pad, name):
    B, Tp, _ = xact.shape
    nc = Tp // CHUNK

    def body(xs_ref, bm_ref, cm_ref, dt_ref, z_ref, db_ref, al_ref, ds_ref, nw_ref, y_ref, save_ref, st):
        c = pl.program_id(1)

        @pl.when(c == 0)
        def _():
            st[...] = jnp.zeros_like(st)

        s0 = st[...]
        save_ref[...] = s0
        y, s1 = _ssd_chunk(xs_ref[...], bm_ref[...], cm_ref[...], dt_ref[...], z_ref[...].astype(F32), s0, db_ref[...],
                           al_ref[...], ds_ref[...], nw_ref[...], _valid_rows(c, pad))
        y_ref[...] = y.astype(y_ref.dtype)
        st[...] = s1

    row = lambda w, off=0: pl.BlockSpec((None, CHUNK, w), lambda b, c: (b, c, off))
    par = lambda w: pl.BlockSpec((1, w), lambda b, c: (0, 0))
    return pl.pallas_call(
        body, grid=(B, nc),
        in_specs=[row(1024, 0), row(512, 2), row(512, 3), row(128), row(1024), par(128), par(128), par(128), par(1024)],
        out_specs=[row(1024), pl.BlockSpec((None, None, 1024, 128), lambda b, c: (b, c, 0, 0))],
        out_shape=[jax.ShapeDtypeStruct((B, Tp, SSD_INNER), BF16), jax.ShapeDtypeStruct((B, nc, 1024, 128), F32)],
        scratch_shapes=[pltpu.VMEM((1024, 128), F32)],
        name=name, compiler_params=_cparams(("arbitrary", "arbitrary")),
    )(xact, xact, xact, dtr, z, dt_bias, a_log, dskip, norm_w)


def _ssd_bwd(xact, dtr, z, dt_bias, a_log, dskip, norm_w, saved, dy, pad, name, after=None):
    B, Tp, _ = xact.shape
    nc = Tp // CHUNK

    def body(xs_ref, bm_ref, cm_ref, dt_ref, z_ref, db_ref, al_ref, ds_ref, nw_ref, sv_ref, dy_ref,
             dx_ref, ddt_ref, dz_ref, dpar_ref, dnw_ref, dst):
        b, i = pl.program_id(0), pl.program_id(1)
        c = nc - 1 - i

        @pl.when(i == 0)
        def _():
            dst[...] = jnp.zeros_like(dst)

        valid = _valid_rows(c, pad)
        fn = lambda *a: _ssd_chunk(*a, valid)
        _, vjp = jax.vjp(fn, xs_ref[...], bm_ref[...], cm_ref[...], dt_ref[...], z_ref[...].astype(F32), sv_ref[...],
                         db_ref[...], al_ref[...], ds_ref[...], nw_ref[...])
        dxs, dbm, dcm, ddt, dz, dstate, ddb, dal, dds, dnw = vjp((dy_ref[...].astype(F32), dst[...]))
        dx_ref[:, 0:1024] = dxs
        dx_ref[:, 1024:1536] = dbm
        dx_ref[:, 1536:2048] = dcm
        ddt_ref[...] = ddt
        dz_ref[...] = dz.astype(dz_ref.dtype)
        dst[...] = dstate

        @pl.when((b == 0) & (i == 0))
        def _():
            dpar_ref[...] = jnp.zeros_like(dpar_ref)
            dnw_ref[...] = jnp.zeros_like(dnw_ref)

        dpar_ref[0:1, :] += ddb
        dpar_ref[1:2, :] += dal
        dpar_ref[2:3, :] += dds
        dnw_ref[0:1, :] += dnw

    row = lambda w, off=0: pl.BlockSpec((None, CHUNK, w), lambda b, i: (b, nc - 1 - i, off))
    par = lambda w: pl.BlockSpec((1, w), lambda b, i: (0, 0))
    acc = lambda w: pl.BlockSpec((8, w), lambda b, i: (0, 0))
    in_specs = [row(1024, 0), row(512, 2), row(512, 3), row(128), row(1024), par(128), par(128), par(128), par(1024),
                pl.BlockSpec((None, None, 1024, 128), lambda b, i: (b, nc - 1 - i, 0, 0)), row(1024)]
    args = [xact, xact, xact, dtr, z, dt_bias, a_log, dskip, norm_w, saved, dy]
    if after is not None:
        body = _skip_ref(body, len(args))
        args.append(_deps(after))
        in_specs.append(_dep_spec(args[-1]))
    outs = pl.pallas_call(
        body, grid=(B, nc), in_specs=in_specs,
        out_specs=[row(2048), row(128), row(1024), acc(128), acc(1024)],
        out_shape=[jax.ShapeDtypeStruct((B, Tp, 2048), F32), jax.ShapeDtypeStruct((B, Tp, 128), F32),
                   jax.ShapeDtypeStruct((B, Tp, 1024), BF16), jax.ShapeDtypeStruct((8, 128), F32),
                   jax.ShapeDtypeStruct((8, 1024), F32)],
        scratch_shapes=[pltpu.VMEM((1024, 128), F32)],
        name=name, compiler_params=_cparams(("arbitrary", "arbitrary")),
    )(*args)
    return outs


def _hg_chunk(qr, fr, ir, gr, state_t, p0, p1, norm_w, valid):
    Q = qr.shape[0]
    lb = jax.nn.sigmoid(p0 - p1)
    f = lb + (1.0 - lb) * jax.nn.sigmoid(fr)
    k = 1.0 - f
    q = _silu(qr)
    v = ir * valid
    cum = _cumsum_rows(jnp.log(f))
    cum_end = _row_of(cum, Q - 1)
    o_inter = _mm_nt(q * jnp.exp(cum), state_t)
    nblk = Q // HG_SUB
    row = lax.broadcasted_iota(jnp.int32, (Q, 1), 0)
    ri = lax.broadcasted_iota(jnp.int32, (Q, Q), 0)
    ci = lax.broadcasted_iota(jnp.int32, (Q, Q), 1)
    mids = jnp.concatenate([jnp.broadcast_to(_row_of(cum, HG_SUB * i + HG_SUB // 2 - 1), (HG_SUB, cum.shape[1]))
                            for i in range(nblk)], axis=0)
    sh = HG_SUB.bit_length() - 1
    same = (jnp.right_shift(ri, sh) == jnp.right_shift(ci, sh)) & (ri >= ci)
    att = jnp.where(same, _mm_nt(q * jnp.exp(cum - mids), k * jnp.exp(mids - cum)), 0.0)
    for i in range(1, nblk):
        lo = HG_SUB * i
        start = _row_of(cum, lo - 1)
        qa = q * jnp.exp(jnp.where((row >= lo) & (row < lo + HG_SUB), cum - start, -1e30))
        ka = k * jnp.exp(jnp.where(row < lo, start - cum, -1e30))
        att = att + _mm_nt(qa, ka)
    o = o_inter + _mm(att, v)
    new_state_t = state_t * jnp.exp(cum_end) + _mm_tn(v, k * jnp.exp(cum_end - cum))
    o = o * lax.rsqrt(jnp.mean(o * o, axis=-1, keepdims=True) + EPS) * norm_w
    return o * _silu(gr), new_state_t


HG_PER_STEP = 8
HG_COLS = 4 * 128


def _hg_fwd(qfig, lbh, nwh, pad, name):
    B, Tp, _ = qfig.shape
    nc = Tp // CHUNK
    hp = HG_PER_STEP

    def body(x_ref, lb_ref, nw_ref, y_ref, save_ref, st):
        c = pl.program_id(1)

        @pl.when(c == 0)
        def _():
            st[...] = jnp.zeros_like(st)

        valid = _valid_rows(c, pad)
        for j in range(hp):
            for b in range(B):
                s0 = st[j, b]
                save_ref[j, b] = s0
                col = lambda k: x_ref[b, :, HG_COLS * j + 128 * k:HG_COLS * j + 128 * (k + 1)]
                y, s1 = _hg_chunk(col(0), col(1), col(2), col(3), s0, lb_ref[j, 0:1, :], lb_ref[j, 1:2, :], nw_ref[j], valid)
                y_ref[b, :, 128 * j:128 * (j + 1)] = y.astype(y_ref.dtype)
                st[j, b] = s1

    return pl.pallas_call(
        body, grid=(HG_HEADS // hp, nc),
        in_specs=[pl.BlockSpec((B, CHUNK, HG_COLS * hp), lambda h, c: (0, c, h)),
                  pl.BlockSpec((hp, 2, 128), lambda h, c: (h, 0, 0)),
                  pl.BlockSpec((hp, 1, 128), lambda h, c: (h, 0, 0))],
        out_specs=[pl.BlockSpec((B, CHUNK, 128 * hp), lambda h, c: (0, c, h)),
                   pl.BlockSpec((hp, B, None, 128, 128), lambda h, c: (h, 0, c, 0, 0))],
        out_shape=[jax.ShapeDtypeStruct((B, Tp, 1024), BF16), jax.ShapeDtypeStruct((HG_HEADS, B, nc, 128, 128), F32)],
        scratch_shapes=[pltpu.VMEM((hp, B, 128, 128), F32)],
        name=name, compiler_params=_cparams(("arbitrary", "arbitrary")),
    )(qfig, lbh, nwh)


def _hg_bwd(qfig, lbh, nwh, saved, dy, pad, name, after=None):
    B, Tp, _ = qfig.shape
    nc = Tp // CHUNK
    hp = HG_PER_STEP

    def body(x_ref, lb_ref, nw_ref, sv_ref, dy_ref, dx_ref, dlb_ref, dnw_ref, dst):
        i = pl.program_id(1)
        c = nc - 1 - i

        @pl.when(i == 0)
        def _():
            dst[...] = jnp.zeros_like(dst)
            dlb_ref[...] = jnp.zeros_like(dlb_ref)
            dnw_ref[...] = jnp.zeros_like(dnw_ref)

        valid = _valid_rows(c, pad)
        fn = lambda *a: _hg_chunk(*a, valid)
        for j in range(hp):
            for b in range(B):
                col = lambda k: x_ref[b, :, HG_COLS * j + 128 * k:HG_COLS * j + 128 * (k + 1)]
                _, vjp = jax.vjp(fn, col(0), col(1), col(2), col(3), sv_ref[j, b], lb_ref[j, 0:1, :], lb_ref[j, 1:2, :], nw_ref[j])
                d4 = vjp((dy_ref[b, :, 128 * j:128 * (j + 1)].astype(F32), dst[j, b]))
                for k in range(4):
                    dx_ref[b, :, HG_COLS * j + 128 * k:HG_COLS * j + 128 * (k + 1)] = d4[k].astype(dx_ref.dtype)
                dst[j, b] = d4[4]
                dlb_ref[j, 0:1, :] += d4[5]
                dlb_ref[j, 1:2, :] += d4[6]
                dnw_ref[j, 0:1, :] += d4[7]

    acc = pl.BlockSpec((hp, 8, 128), lambda h, i: (h, 0, 0))
    in_specs = [pl.BlockSpec((B, CHUNK, HG_COLS * hp), lambda h, i: (0, nc - 1 - i, h)),
                pl.BlockSpec((hp, 2, 128), lambda h, i: (h, 0, 0)),
                pl.BlockSpec((hp, 1, 128), lambda h, i: (h, 0, 0)),
                pl.BlockSpec((hp, B, None, 128, 128), lambda h, i: (h, 0, nc - 1 - i, 0, 0)),
                pl.BlockSpec((B, CHUNK, 128 * hp), lambda h, i: (0, nc - 1 - i, h))]
    args = [qfig, lbh, nwh, saved, dy]
    if after is not None:
        body = _skip_ref(body, len(args))
        args.append(_deps(after))
        in_specs.append(_dep_spec(args[-1]))
    return pl.pallas_call(
        body, grid=(HG_HEADS // hp, nc), in_specs=in_specs,
        out_specs=[pl.BlockSpec((B, CHUNK, HG_COLS * hp), lambda h, i: (0, nc - 1 - i, h)), acc, acc],
        out_shape=[jax.ShapeDtypeStruct((B, Tp, 4096), BF16), jax.ShapeDtypeStruct((HG_HEADS, 8, 128), F32),
                   jax.ShapeDtypeStruct((HG_HEADS, 8, 128), F32)],
        scratch_shapes=[pltpu.VMEM((hp, B, 128, 128), F32)],
        name=name, compiler_params=_cparams(("arbitrary", "arbitrary")),
    )(*args)


def _adamw_math(w, g, m, v):
    m = ADAM_B1 * m + (1.0 - ADAM_B1) * g
    v = ADAM_B2 * v + (1.0 - ADAM_B2) * (g * g)
    m_hat = m / (1.0 - ADAM_B1 ** ADAM_STEP)
    v_hat = v / (1.0 - ADAM_B2 ** ADAM_STEP)
    return -ADAM_LR * (m_hat / (jnp.sqrt(v_hat) + ADAM_EPS) + ADAM_WD * w), m, v


def _adamw_many(ws, gs, ms, vs, name):
    n = len(ws)

    def body(*refs):
        for i in range(n):
            d, m, v = _adamw_math(refs[i][...], refs[n + i][...], refs[2 * n + i][...], refs[3 * n + i][...])
            refs[4 * n + i][...] = d
            refs[5 * n + i][...] = m
            refs[6 * n + i][...] = v

    vm = pl.BlockSpec(memory_space=pltpu.VMEM)
    outs = pl.pallas_call(body, in_specs=[vm] * (4 * n), out_specs=[vm] * (3 * n),
                          out_shape=[jax.ShapeDtypeStruct(w.shape, F32) for w in ws] * 3, name=name)(*ws, *gs, *ms, *vs)
    return outs[:n], outs[n:2 * n], outs[2 * n:]


def _adamw(w, g, m, v, name, after=None):
    R, C = w.shape
    tr = max(t for t in range(8, R + 1, 8) if R % t == 0 and (t * C * 4 <= ADAMW_BLOCK_BYTES or t == 8))

    def body(w_ref, g_ref, m_ref, v_ref, d_ref, mo_ref, vo_ref):
        d_ref[...], mo_ref[...], vo_ref[...] = _adamw_math(w_ref[...], g_ref[...], m_ref[...], v_ref[...])

    sp = pl.BlockSpec((tr, C), lambda i: (i, 0))
    sh = jax.ShapeDtypeStruct((R, C), F32)
    in_specs, args = [sp] * 4, [w, g, m, v]
    if after is not None:
        body = _skip_ref(body, len(args))
        args.append(_deps(after))
        in_specs.append(_dep_spec(args[-1]))
    return pl.pallas_call(body, grid=(R // tr,), in_specs=in_specs, out_specs=[sp] * 3, out_shape=[sh] * 3,
                          name=name, compiler_params=_cparams(("arbitrary",)))(*args)


def _ffn_fwd(h, norm_w, w_gu, w_down, tag, after_norm=None, n=None, next_norm_w=None):
    if n is None:
        n = _rms_fwd(h, norm_w, f"{tag}_norm")
    if after_norm is not None:
        after_norm(n)
    gu, a = _gu_swiglu(n, w_gu, f"{tag}_gu")
    out = _residual_matmul(a, w_down, h, 0.5, f"{tag}_down", next_norm_w)
    return out, (n, gu, a)


def _ffn_bwd(h, norm_w, w_gu, w_down, saved, dout, tag, after_dw_down=None, token_seqs=None):
    n, gu, a = saved
    dgu = _d_swiglu(dout, w_down, gu, 0.5, f"{tag}_d_gu")
    dw_down = _matmul(a, dout, mode="tn", out_dtype=F32, alpha=0.5, name=f"{tag}_dw_down")
    dw_gu = _matmul(n, dgu, mode="tn", out_dtype=F32, out_groups=N_CHIPS, name=f"{tag}_dw_gu",
                    after=after_dw_down(dw_down) if after_dw_down else None)
    dn = _matmul(dgu, w_gu, mode="nt", out_dtype=F32, name=f"{tag}_d_norm", after=dw_gu)
    if token_seqs is None:
        dh, dnw = _rms_bwd(h, norm_w, dn, dout, f"{tag}_d_in")
    else:
        dx, dm, dnw = _rms_bwd_tokens(h, norm_w, dn, dout, token_seqs, f"{tag}_d_in")
        dh = (dx, dm)
    return dh, dnw, dw_gu, dw_down


def _split_w_in(w_in_full):
    pts = [0]
    for s in IN_SIZES:
        pts.append(pts[-1] + s)
    sl = lambda i, j: w_in_full[:, pts[i]:pts[j]]
    qfig = sl(3, 7).reshape(D_MODEL, 4, HG_HEADS, 128).transpose(0, 2, 1, 3).reshape(D_MODEL, 4 * D_MODEL)
    return {"z": sl(0, 1), "xbc": sl(1, 2), "dt": jnp.pad(sl(2, 3), ((0, 0), (0, 128 - SSD_HEADS))),
            "qfig": qfig, "gates": sl(7, 9)}


def _local_step(x, target, W):
    B, S, _ = x.shape
    T = N_META + S
    pad = (-T) % CHUNK
    Tp = T + pad
    assert pad + N_META == CHUNK
    R = B * Tp
    meta = jnp.broadcast_to(W["meta_tokens"][None], (B, N_META, D_MODEL))
    h0 = jnp.concatenate([jnp.zeros((B, pad, D_MODEL), F32), meta, x], axis=1).reshape(R, D_MODEL)

    stage = W.get("_stage", lambda name, x: {})
    W = dict(W)
    (h1, um), sv1 = _ffn_fwd(h0, W["ffn1_norm"], W["ffn1_w_gu"], W["ffn1_w_down"], "ffn1",
                             lambda n: W.update(stage("ffn1_norm", n)), next_norm_w=W["mix_norm"])
    W.update(stage("ffn1_out", h1))
    wi = W["w_in"]
    z = _matmul(um, wi["z"], mode="nn", out_dtype=BF16, name="in_z")
    xbc = _matmul(um, wi["xbc"], mode="nn", out_dtype=F32, name="in_xbc")
    dtr = _matmul(um, wi["dt"], mode="nn", out_dtype=F32, name="in_dt")
    qfig = _matmul(um, wi["qfig"], mode="nn", out_dtype=F32, name="in_qfig")
    gates = _matmul(um, wi["gates"], mode="nn", out_dtype=BF16, name="in_gates")

    r3 = lambda t: t.reshape(B, Tp, t.shape[-1])
    lane_pad = lambda t: jnp.pad(t, ((0, 0), (0, 128 - t.shape[1])))
    dt_bias, a_log, dskip = lane_pad(W["ssd_dt_bias"]), lane_pad(W["ssd_a_log"]), lane_pad(W["ssd_d"])
    xact = _conv_fwd(r3(xbc), W["ssd_conv_w"], W["ssd_conv_b"], pad, "conv_fwd")
    ya, ssd_saved = _ssd_fwd(xact, r3(dtr), r3(z), dt_bias, a_log, dskip, W["ssd_norm"], pad, "ssd_fwd")
    lbh = W["hg_lower_bound"].reshape(2, HG_HEADS, 128).transpose(1, 0, 2)
    nwh = W["hg_norm"].reshape(HG_HEADS, 1, 128)
    yb, hg_saved = _hg_fwd(r3(qfig), lbh, nwh, pad, "hg_fwd")
    ya2, yb2 = ya.reshape(R, -1), yb.reshape(R, -1)
    W.update(stage("mixers_out", yb2))
    pa, pb, mg = _branch_merge(ya2, yb2, W["w_branch_a"], W["w_branch_b"], gates, "branch_merge")
    h2, n2 = _residual_matmul(mg, W["w_out"], h1, 1.0, "mix_out", W["ffn2_norm"])
    h3, sv2 = _ffn_fwd(h2, W["ffn2_norm"], W["ffn2_w_gu"], W["ffn2_w_down"], "ffn2", n=n2)

    loss, dh3, d_final = _loss_head(h3, W["final_norm"].reshape(1, D_MODEL), target, B, "loss_head")

    G = {"final_norm": d_final[0]}
    dh2, dnw, G["ffn2_w_gu"], G["ffn2_w_down"] = _ffn_bwd(h2, W["ffn2_norm"], W["ffn2_w_gu"], W["ffn2_w_down"], sv2, dh3, "ffn2")
    G["ffn2_norm"] = dnw[0:1]
    dmg = _matmul(dh2, W["w_out"], mode="nt", out_dtype=BF16, name="d_merge")
    G["w_out"] = _matmul(mg, dh2, mode="tn", out_dtype=F32, name="dw_out")
    dpa, dpb, dgates, dya, dyb = _branch_merge_bwd(pa, pb, gates, dmg, W["w_branch_a"], W["w_branch_b"], "branch_merge_bwd")
    G["w_branch_a"] = _matmul(ya2, dpa, mode="tn", out_dtype=F32, name="dw_branch_a")
    G["w_branch_b"] = _matmul(yb2, dpb, mode="tn", out_dtype=F32, name="dw_branch_b")

    dxact, ddtr, dz, dpar, dnw = _ssd_bwd(xact, r3(dtr), r3(z), dt_bias, a_log, dskip, W["ssd_norm"], ssd_saved,
                                          r3(dya), pad, "ssd_bwd", after=stage("late_grads", G).get("_after"))
    G["ssd_dt_bias"], G["ssd_a_log"], G["ssd_d"] = dpar[0:1, :SSD_HEADS], dpar[1:2, :SSD_HEADS], dpar[2:3, :SSD_HEADS]
    G["ssd_norm"] = dnw[0:1]
    dxbc, dcw, dcb = _conv_bwd(r3(xbc), W["ssd_conv_w"], W["ssd_conv_b"], dxact, pad, "conv_bwd")
    G["ssd_conv_w"], G["ssd_conv_b"] = dcw[0:SSD_CONV], dcb[0:1]
    dqfig, dlb, dhn = _hg_bwd(r3(qfig), lbh, nwh, hg_saved, r3(dyb), pad, "hg_bwd",
                              after=stage("after_conv_bwd", dcb).get("_after"))
    G["hg_lower_bound"] = dlb[:, 0:2, :].transpose(1, 0, 2).reshape(2, D_MODEL)
    G["hg_norm"] = dhn[:, 0, :].reshape(1, D_MODEL)

    r2 = lambda t: t.reshape(R, t.shape[-1])
    pieces = [("z", r2(dz)), ("xbc", r2(dxbc)), ("dt", r2(ddtr)), ("qfig", r2(dqfig)), ("gates", dgates)]
    dum = _sum_nt([p for _, p in pieces], [wi[nm] for nm, _ in pieces], "d_mix")
    dwi = {nm: _matmul(um, dpiece, mode="tn", out_dtype=F32, name=f"dw_in_{nm}") for nm, dpiece in pieces}
    dw_qfig = dwi["qfig"].reshape(D_MODEL, HG_HEADS, 4, 128).transpose(0, 2, 1, 3).reshape(D_MODEL, 4 * D_MODEL)
    G["w_in"] = jnp.concatenate([dwi["z"], dwi["xbc"], dwi["dt"][:, :SSD_HEADS], dw_qfig, dwi["gates"]], axis=1)
    dh1, dnw = _rms_bwd(h1, W["mix_norm"], dum, dh2, "mix_norm_bwd", after=stage("w_in_grads", dwi).get("_after"))
    G["mix_norm"] = dnw[0:1]
    (dx, dfirst), dnw, G["ffn1_w_gu"], G["ffn1_w_down"] = _ffn_bwd(
        h0, W["ffn1_norm"], W["ffn1_w_gu"], W["ffn1_w_down"], sv1, dh1, "ffn1",
        lambda dw: stage("ffn1_dw_down", dw).get("_after"), token_seqs=B)
    G["ffn1_norm"] = dnw[0:1]
    G["meta_tokens"] = jnp.sum(dfirst[:, pad:CHUNK], axis=0)
    return loss, dx, G


ANY = pl.BlockSpec(memory_space=pl.ANY)


def _place():
    return lax.axis_index("x"), lax.axis_index("y"), lax.axis_index("c")


def _other_chips(x, y):
    return [(1 - x, y), (x, 1 - y), (1 - x, 1 - y)]


def _remote(src, dst, ssem, rsem, dev):
    return pltpu.make_async_remote_copy(src_ref=src, dst_ref=dst, send_sem=ssem, recv_sem=rsem,
                                        device_id=dev, device_id_type=MESH)


def _exchange8(buf, name):
    n, w = buf.shape

    def body(x_ref, out_ref, ssem, rsem):
        x, y, c = _place()
        me = 4 * x + 2 * y + c
        out_ref[me] = x_ref[...]
        copies = []
        for k in range(1, 8):
            px = 1 - x if (k >> 2) & 1 else x
            py = 1 - y if (k >> 1) & 1 else y
            pc = 1 - c if k & 1 else c
            cp = _remote(x_ref, out_ref.at[me], ssem.at[k - 1], rsem.at[k - 1], (px, py, pc))
            cp.start()
            copies.append((cp, 4 * px + 2 * py + pc))
        for k, (cp, peer) in enumerate(copies):
            _remote(x_ref, out_ref.at[peer], ssem.at[k], rsem.at[k], (x, y, c)).wait_recv()
        for cp, _ in copies:
            cp.wait_send()

    vm = pl.BlockSpec(memory_space=pltpu.VMEM)
    return pl.pallas_call(
        body, in_specs=[vm], out_specs=vm, out_shape=jax.ShapeDtypeStruct((8, n, w), F32),
        scratch_shapes=[pltpu.SemaphoreType.DMA((7,)), pltpu.SemaphoreType.DMA((7,))], name=name,
    )(buf)


HBM = pltpu.MemorySpace.HBM


def _sequencer(name, collective_id, sems, sent):
    return functools.partial(pl.kernel, mesh=plsc.ScalarSubcoreMesh(axis_name="sequencer", num_cores=1), name=name,
                             scratch_types=sems, compiler_params=pltpu.CompilerParams(collective_id=collective_id),
                             cost_estimate=pl.CostEstimate(flops=0, transcendentals=0, bytes_accessed=2 * sent,
                                                           remote_bytes_transferred=sent))


def _nbytes(arrays):
    return sum(a.size * a.dtype.itemsize for a in arrays)


def _handshake(peers):
    barrier = pltpu.get_barrier_semaphore()
    for peer in peers:
        pl.semaphore_signal(barrier, inc=1, device_id=peer, device_id_type=MESH)
    pl.semaphore_wait(barrier, len(peers))


def _gather_seq(blocks, name, collective_id):
    n = len(blocks)
    half = [s.shape[1] // 2 for s in blocks]
    full = [jax.new_ref(b, memory_space=HBM) for b in blocks]

    @_sequencer(name, collective_id, [pltpu.SemaphoreType.DMA((n, 3))] * 4, _nbytes(blocks) * 3 // 4)
    def launch(ssem, rsem, fssem, frsem):
        x, y, c = _place()
        q = 2 * x + y
        chips = _other_chips(x, y)
        _handshake([(px, py, c) for px, py in chips] + [(x, y, 1 - c)])
        piece = lambda s, qq, cc: full[s].at[qq, pl.ds(cc * half[s], half[s])]
        sends = []
        for j, (px, py) in enumerate(chips):
            for s in range(n):
                cp = _remote(piece(s, q, c), piece(s, q, c), ssem.at[s, j], rsem.at[s, j], (px, py, c))
                cp.start()
                sends.append(cp)
        for j, (px, py) in enumerate(chips):
            for s in range(n):
                got = piece(s, 2 * px + py, c)
                _remote(got, got, ssem.at[s, j], rsem.at[s, j], (px, py, c)).wait_recv()
                cp = _remote(got, got, fssem.at[s, j], frsem.at[s, j], (x, y, 1 - c))
                cp.start()
                sends.append(cp)
        for j, (px, py) in enumerate(chips):
            for s in range(n):
                got = piece(s, 2 * px + py, 1 - c)
                _remote(got, got, fssem.at[s, j], frsem.at[s, j], (x, y, 1 - c)).wait_recv()
        for cp in sends:
            cp.wait_send()

    launch()
    return [r[...] for r in full]


def _share8(buf, name, collective_id):
    n, w = buf.shape
    src = jax.new_ref(buf, memory_space=HBM)
    out = jax.empty_ref(jax.ShapeDtypeStruct((8, n, w), F32), memory_space=HBM)

    @_sequencer(name, collective_id, [pltpu.SemaphoreType.DMA((7,)), pltpu.SemaphoreType.DMA((7,)), pltpu.SemaphoreType.DMA((1,))],
                7 * buf.size * 4)
    def launch(ssem, rsem, lsem):
        x, y, c = _place()
        me = 4 * x + 2 * y + c
        peers = [(1 - x if (k >> 2) & 1 else x, 1 - y if (k >> 1) & 1 else y, 1 - c if k & 1 else c) for k in range(1, 8)]
        _handshake(peers)
        mine = pltpu.make_async_copy(src, out.at[me], lsem.at[0])
        mine.start()
        sends = []
        for k, peer in enumerate(peers):
            cp = _remote(src, out.at[me], ssem.at[k], rsem.at[k], peer)
            cp.start()
            sends.append(cp)
        for k, (px, py, pc) in enumerate(peers):
            slot = out.at[4 * px + 2 * py + pc]
            _remote(slot, slot, ssem.at[k], rsem.at[k], (px, py, pc)).wait_recv()
        for cp in sends:
            cp.wait_send()
        mine.wait()

    launch()
    return out[...]


def _sum_slots(slots, name, after=None):
    _, n, w = slots.shape

    def body(s_ref, o_ref):
        acc = s_ref[0]
        for d in range(1, 8):
            acc = acc + s_ref[d]
        o_ref[...] = acc

    vm = pl.BlockSpec(memory_space=pltpu.VMEM)
    in_specs, args = [vm], [slots]
    if after is not None:
        body = _skip_ref(body, 1)
        args.append(_deps(after))
        in_specs.append(vm)
    return pl.pallas_call(body, in_specs=in_specs, out_specs=vm, out_shape=jax.ShapeDtypeStruct((n, w), F32), name=name)(*args)


def _pair_swap(parts, name, collective_id):
    n = len(parts)
    half = [p.shape[1] // 2 for p in parts]
    src = [jax.new_ref(p, memory_space=HBM) for p in parts]
    got = [jax.empty_ref(jax.ShapeDtypeStruct((p.shape[0], h, p.shape[2]), p.dtype), memory_space=HBM) for p, h in zip(parts, half)]

    @_sequencer(name, collective_id, [pltpu.SemaphoreType.DMA((n,))] * 2, _nbytes(parts) // 2)
    def launch(ssem, rsem):
        x, y, c = _place()
        _handshake([(x, y, 1 - c)])
        copies = []
        for s in range(n):
            cp = _remote(src[s].at[pl.ds(0, parts[s].shape[0]), pl.ds((1 - c) * half[s], half[s])], got[s], ssem.at[s], rsem.at[s], (x, y, 1 - c))
            cp.start()
            copies.append(cp)
        for cp in copies:
            cp.wait_recv()
        for cp in copies:
            cp.wait_send()

    launch()
    return [g[...] for g in got]


def _to_owners(sums, name, collective_id):
    n = len(sums)
    src = [jax.new_ref(s, memory_space=HBM) for s in sums]
    got = [jax.empty_ref(jax.ShapeDtypeStruct(s.shape, s.dtype), memory_space=HBM) for s in sums]

    @_sequencer(name, collective_id, [pltpu.SemaphoreType.DMA((n, 3))] * 2, _nbytes(sums) * 3 // 4)
    def launch(ssem, rsem):
        x, y, c = _place()
        q = 2 * x + y
        chips = _other_chips(x, y)
        _handshake([(px, py, c) for px, py in chips])
        sends = []
        for j, (px, py) in enumerate(chips):
            for s in range(n):
                cp = _remote(src[s].at[2 * px + py], got[s].at[q], ssem.at[s, j], rsem.at[s, j], (px, py, c))
                cp.start()
                sends.append(cp)
        for j, (px, py) in enumerate(chips):
            for s in range(n):
                slot = got[s].at[2 * px + py]
                _remote(slot, slot, ssem.at[s, j], rsem.at[s, j], (px, py, c)).wait_recv()
        for cp in sends:
            cp.wait_send()

    launch()
    return [g[...] for g in got]


def _pair_join(blocks, name, collective_id):
    n = len(blocks)
    out = [jax.new_ref(b, memory_space=HBM) for b in blocks]

    @_sequencer(name, collective_id, [pltpu.SemaphoreType.DMA((n,))] * 2, _nbytes(blocks) // 2)
    def launch(ssem, rsem):
        x, y, c = _place()
        _handshake([(x, y, 1 - c)])
        sends = []
        for s in range(n):
            h = blocks[s].shape[0] // 2
            mine = out[s].at[pl.ds(c * h, h)]
            cp = _remote(mine, mine, ssem.at[s], rsem.at[s], (x, y, 1 - c))
            cp.start()
            sends.append(cp)
        for s in range(n):
            h = blocks[s].shape[0] // 2
            theirs = out[s].at[pl.ds((1 - c) * h, h)]
            _remote(theirs, theirs, ssem.at[s], rsem.at[s], (x, y, 1 - c)).wait_recv()
        for cp in sends:
            cp.wait_send()

    launch()
    return [o[...] for o in out]


WIRE = BF16


def _row_tile(h):
    return _pick(h, (256, 368, 352, 128, 16))


def _add_pair(part, got, c, name, after=None):
    _, h, w = got.shape
    tr = _row_tile(h)
    nt = h // tr

    def body(c_ref, p_ref, g_ref, o_ref):
        o_ref[...] = (p_ref[...] + g_ref[...].astype(F32)).astype(o_ref.dtype)

    in_specs = [pl.BlockSpec((None, tr, w), lambda q, i, c_ref: (q, c_ref[0] * nt + i, 0)),
                pl.BlockSpec((None, tr, w), lambda q, i, c_ref: (q, i, 0))]
    args = [c.reshape(1).astype(jnp.int32), part, got]
    if after is not None:
        body = _skip_ref(body, len(args))
        args.append(_deps(after))
        in_specs.append(_dep_spec(args[-1]))
    return pl.pallas_call(
        body,
        grid_spec=pltpu.PrefetchScalarGridSpec(
            num_scalar_prefetch=1, grid=(got.shape[0], nt), in_specs=in_specs,
            out_specs=pl.BlockSpec((None, tr, w), lambda q, i, c_ref: (q, i, 0))),
        out_shape=jax.ShapeDtypeStruct(got.shape, WIRE), name=name,
        compiler_params=_cparams(("arbitrary", "arbitrary")),
    )(*args)


def _sum_chips(slots, sums, q, c, name, after=None):
    _, h, w = slots.shape
    tr = _row_tile(h)
    nt = h // tr

    def body(s_ref, mine_ref, a_ref, b_ref, d_ref, o_ref):
        o_ref[...] = ((mine_ref[...].astype(F32) + a_ref[...].astype(F32)) + b_ref[...].astype(F32)) + d_ref[...].astype(F32)

    slot = lambda k: pl.BlockSpec((None, tr, w), lambda i, s_ref: (s_ref[1 + k], i, 0))
    scalars = jnp.stack([c, q, (q + 1) % N_CHIPS, (q + 2) % N_CHIPS, (q + 3) % N_CHIPS]).astype(jnp.int32)
    in_specs, args = [slot(0), slot(1), slot(2), slot(3)], [scalars, sums, slots, slots, slots]
    if after is not None:
        body = _skip_ref(body, len(args))
        args.append(_deps(after))
        in_specs.append(_dep_spec(args[-1]))
    return pl.pallas_call(
        body,
        grid_spec=pltpu.PrefetchScalarGridSpec(
            num_scalar_prefetch=1, grid=(nt,), in_specs=in_specs,
            out_specs=pl.BlockSpec((tr, w), lambda i, s_ref: (s_ref[0] * nt + i, 0))),
        out_shape=jax.ShapeDtypeStruct((2 * h, w), F32), name=name,
        compiler_params=_cparams(("arbitrary",)),
    )(*args)


class _Reduce:
    def __init__(self, parts, q, c, tag, first_id, regions=None):
        self.parts, self.q, self.c, self.tag, self.first_id, self.regions = parts, q, c, tag, first_id, regions
        self.got = _pair_swap(parts, f"{tag}_pair_swap", first_id)

    def to_owners(self, after=None):
        self.sums = [_add_pair(p, g, self.c, f"{self.tag}_pair_add{i}", after)
                     for i, (p, g) in enumerate(zip(self.parts, self.got))]
        if self.regions is not None:
            self.sums = self.regions(self.sums)
        self.slots = _to_owners(self.sums, f"{self.tag}_to_owners", self.first_id + 1)
        return self.sums

    def join(self, after=None):
        blocks = [_sum_chips(sl, sm, self.q, self.c, f"{self.tag}_sum_chips{i}", after)
                  for i, (sl, sm) in enumerate(zip(self.slots, self.sums))]
        self.out = _pair_join(blocks, f"{self.tag}_pair_join", self.first_id + 2)
        return blocks


WEIGHTS = ("meta_tokens", "ffn1_norm", "ffn1_w_gu", "ffn1_w_down", "mix_norm", "w_in", "ssd_conv_w", "ssd_conv_b",
           "ssd_dt_bias", "ssd_a_log", "ssd_d", "ssd_norm", "hg_lower_bound", "hg_norm", "w_branch_a", "w_branch_b",
           "w_out", "ffn2_norm", "ffn2_w_gu", "ffn2_w_down", "final_norm")
BIG = ("ffn1_w_gu", "ffn1_w_down", "w_in", "w_branch_a", "w_branch_b", "w_out", "ffn2_w_gu", "ffn2_w_down")
SMALL = tuple(n for n in WEIGHTS if n not in BIG)


def _rows1024(a):
    flat = a.reshape(-1)
    n = -(-flat.shape[0] // 1024) * 1024
    return jnp.pad(flat, (0, n - flat.shape[0])).reshape(-1, 1024)


def kernel(x, meta_tokens, ffn1_norm, ffn1_w_gu, ffn1_w_down, mix_norm, w_in, ssd_conv_w, ssd_conv_b, ssd_dt_bias, ssd_a_log, ssd_d, ssd_norm, hg_lower_bound, hg_norm, w_branch_a, w_branch_b, w_out, ffn2_norm, ffn2_w_gu, ffn2_w_down, final_norm, loss_target, m_meta_tokens, m_ffn1_norm, m_ffn1_w_gu, m_ffn1_w_down, m_mix_norm, m_w_in, m_ssd_conv_w, m_ssd_conv_b, m_ssd_dt_bias, m_ssd_a_log, m_ssd_d, m_ssd_norm, m_hg_lower_bound, m_hg_norm, m_w_branch_a, m_w_branch_b, m_w_out, m_ffn2_norm, m_ffn2_w_gu, m_ffn2_w_down, m_final_norm, v_meta_tokens, v_ffn1_norm, v_ffn1_w_gu, v_ffn1_w_down, v_mix_norm, v_w_in, v_ssd_conv_w, v_ssd_conv_b, v_ssd_dt_bias, v_ssd_a_log, v_ssd_d, v_ssd_norm, v_hg_lower_bound, v_hg_norm, v_w_branch_a, v_w_branch_b, v_w_out, v_ffn2_norm, v_ffn2_w_gu, v_ffn2_w_down, v_final_norm):
    P = dict(zip(WEIGHTS, (meta_tokens, ffn1_norm, ffn1_w_gu, ffn1_w_down, mix_norm, w_in, ssd_conv_w, ssd_conv_b, ssd_dt_bias, ssd_a_log, ssd_d, ssd_norm, hg_lower_bound, hg_norm, w_branch_a, w_branch_b, w_out, ffn2_norm, ffn2_w_gu, ffn2_w_down, final_norm)))
    M = dict(zip(WEIGHTS, (m_meta_tokens, m_ffn1_norm, m_ffn1_w_gu, m_ffn1_w_down, m_mix_norm, m_w_in, m_ssd_conv_w, m_ssd_conv_b, m_ssd_dt_bias, m_ssd_a_log, m_ssd_d, m_ssd_norm, m_hg_lower_bound, m_hg_norm, m_w_branch_a, m_w_branch_b, m_w_out, m_ffn2_norm, m_ffn2_w_gu, m_ffn2_w_down, m_final_norm)))
    V = dict(zip(WEIGHTS, (v_meta_tokens, v_ffn1_norm, v_ffn1_w_gu, v_ffn1_w_down, v_mix_norm, v_w_in, v_ssd_conv_w, v_ssd_conv_b, v_ssd_dt_bias, v_ssd_a_log, v_ssd_d, v_ssd_norm, v_hg_lower_bound, v_hg_norm, v_w_branch_a, v_w_branch_b, v_w_out, v_ffn2_norm, v_ffn2_w_gu, v_ffn2_w_down, v_final_norm)))
    cx, cy, cc = _place()
    q = 2 * cx + cy

    mine = jnp.concatenate([meta_tokens.reshape(4, 1024), ssd_conv_w.reshape(2, 1024), jnp.zeros((2, 1024), F32)], axis=0)
    every = _exchange8(mine, "gather_small")
    meta_full = jnp.concatenate([every[2 * k, 0:4].reshape(N_META, 256) for k in range(N_CHIPS)], axis=1)
    conv_w_full = jnp.concatenate([every[2 * k, 4:6].reshape(SSD_CONV, 512) for k in range(N_CHIPS)], axis=1)

    late = ("ffn2_w_down", "w_branch_a", "w_branch_b", "w_out")
    rows = jnp.concatenate([P[n][0] for n in late], axis=0)
    zero = lambda t, dtype=F32: (t[0:1, 0:1] * 0).astype(dtype)

    def in_slot(s, after=None):
        s = s if after is None else s + zero(after)
        return lax.dynamic_update_slice(lax.empty((N_CHIPS,) + s.shape, BF16), s.astype(BF16)[None], (q, 0, 0))

    gu1, down1 = _gather_seq([in_slot(ffn1_w_gu[0]), in_slot(ffn1_w_down[0])], "gather_ffn1", 1)
    W = {n: P[n] for n in SMALL}
    W["meta_tokens"], W["ssd_conv_w"] = meta_full, conv_w_full
    W["ffn1_w_gu"], W["ffn1_w_down"] = gu1, down1.reshape(-1, D_MODEL)
    flying = {}

    def stage(name, t):
        if name == "ffn1_norm":
            flying["w_in"] = _gather_seq([in_slot(w_in[0], t)], "gather_w_in", 2)
            return {}
        if name == "ffn1_out":
            flying["late"] = _gather_seq([in_slot(ffn2_w_gu[0], t), in_slot(rows, t)], "gather_late", 3)
            (w_in_all,) = flying["w_in"]
            w_in_all = w_in_all + zero(t, BF16)
            return {"w_in": _split_w_in(w_in_all.transpose(1, 0, 2).reshape(D_MODEL, -1))}
        if name == "mixers_out":
            gu2, rows_all = flying["late"]
            out, r = {"ffn2_w_gu": gu2}, 0
            for n in late:
                nr = P[n].shape[1]
                out[n] = (rows_all[:, r:r + nr] + zero(t, BF16)).reshape(N_CHIPS * nr, D_MODEL)
                r += nr
            return out
        if name == "late_grads":
            row_parts = jnp.concatenate([t[n].reshape(N_CHIPS, -1, D_MODEL) for n in late], axis=1)
            flying["grad_late"] = _Reduce([t["ffn2_w_gu"], row_parts], q, cc, "grad_late", 4)
            return {"_after": [t["ffn2_w_gu"]] + [t[n] for n in late]}
        if name == "after_conv_bwd":
            return {"_after": flying["grad_late"].to_owners(after=t)}
        if name == "w_in_grads":
            order = ("z", "xbc", "dt", "qfig", "gates")
            blocks = flying["grad_late"].join(after=[t[k] for k in order])

            def regions(sums):
                z, xbc, dt, qfig, gates = [s[0] for s in sums]
                h = z.shape[0]
                qfig = qfig.reshape(h, HG_HEADS, 4, 128).transpose(0, 2, 1, 3).reshape(h, 4 * D_MODEL)
                cols = jnp.concatenate([z, xbc, dt[:, :SSD_HEADS], qfig, gates], axis=1)
                return [cols.reshape(h, N_CHIPS, -1).transpose(1, 0, 2)]

            flying["grad_w_in"] = _Reduce([t[k][None] for k in order], q, cc, "grad_w_in", 7, regions)
            return {"_after": blocks}
        if name == "ffn1_dw_down":
            return {"_after": flying["grad_w_in"].to_owners(after=t)}
        return {}

    W["_stage"] = stage

    loss8, grad_x, G = _local_step(x, loss_target, W)

    small = jnp.concatenate(
        [G["meta_tokens"]] + [_rows1024(G[n]) for n in SMALL if n != "meta_tokens"] + [_rows1024(loss8[0:1, 0:1])], axis=0)
    small = jnp.pad(small, ((0, 40 - small.shape[0]), (0, 0)))
    small_slots = _share8(small, "share_small", 13)

    grad_ffn1 = _Reduce([G["ffn1_w_gu"], G["ffn1_w_down"].reshape(N_CHIPS, -1, D_MODEL)], q, cc, "grad_ffn1", 10)
    flying["grad_w_in"].join(after=grad_x)
    going = grad_ffn1.to_owners(after=grad_x)
    g_gu2, g_rows = flying["grad_late"].out
    (g_w_in,) = flying["grad_w_in"].out
    Gb = {"ffn2_w_gu": g_gu2, "w_in": g_w_in}
    r = 0
    for n in late:
        nr = P[n].shape[1]
        Gb[n] = g_rows[r:r + nr]
        r += nr

    grads, delta, new_m, new_v, done = {}, {}, {}, {}, []
    cols = w_in.shape[2]
    to_tiles = lambda a: a.transpose(2, 0, 1).reshape(cols, 8, 128).reshape(cols * 8, 128)
    from_tiles = lambda a: a.reshape(cols, 1, D_MODEL).transpose(1, 2, 0)
    for n in [n for n in BIG if n in Gb]:
        if n == "w_in":
            g_t = to_tiles(Gb[n][None])
            d_, m_, v_ = _adamw(to_tiles(P[n]), g_t, to_tiles(M[n]), to_tiles(V[n]), f"adamw_{n}", after=going)
            grads[n], delta[n], new_m[n], new_v[n] = from_tiles(g_t), from_tiles(d_), from_tiles(m_), from_tiles(v_)
        else:
            d_, m_, v_ = _adamw(P[n][0], Gb[n], M[n][0], V[n][0], f"adamw_{n}", after=going)
            grads[n], delta[n], new_m[n], new_v[n] = Gb[n][None], d_[None], m_[None], v_[None]
        done.append(d_)

    small = _sum_slots(small_slots, "sum_small", after=done)
    Gs = {"meta_tokens": small[0:N_META]}
    r = N_META
    for n in SMALL:
        if n == "meta_tokens":
            continue
        nr = -(-G[n].size // 1024)
        Gs[n] = small[r:r + nr].reshape(-1)[:G[n].size].reshape(G[n].shape)
        r += nr
    loss = small[r, 0]
    Gs["meta_tokens"] = lax.dynamic_slice(Gs["meta_tokens"], (0, 256 * q), (N_META, 256))
    Gs["ssd_conv_w"] = lax.dynamic_slice(Gs["ssd_conv_w"], (0, 512 * q), (SSD_CONV, 512))[None]
    Gs = {n: Gs[n].reshape(P[n].shape) for n in SMALL}
    grads.update(Gs)
    flat = lambda a: a.reshape(-1, a.shape[-1])
    d_s, m_s, v_s = _adamw_many(*[[flat(D[n]) for n in SMALL] for D in (P, Gs, M, V)], "adamw_small")
    for i, n in enumerate(SMALL):
        delta[n], new_m[n], new_v[n] = d_s[i].reshape(P[n].shape), m_s[i].reshape(P[n].shape), v_s[i].reshape(P[n].shape)
    done.append(d_s[0])
    grad_ffn1.join(after=done)
    Gb["ffn1_w_gu"], Gb["ffn1_w_down"] = grad_ffn1.out
    for n in ("ffn1_w_gu", "ffn1_w_down"):
        d_, m_, v_ = _adamw(P[n][0], Gb[n], M[n][0], V[n][0], f"adamw_{n}")
        grads[n], delta[n], new_m[n], new_v[n] = Gb[n][None], d_[None], m_[None], v_[None]
    return (loss, grad_x, *[grads[n] for n in WEIGHTS], *[delta[n] for n in WEIGHTS],
            *[new_m[n] for n in WEIGHTS], *[new_v[n] for n in WEIGHTS])
```

```python
import functools

import jax
import jax.numpy as jnp
from jax import lax
from jax.experimental import pallas as pl
from jax.experimental.pallas import tpu as pltpu
from jax.experimental.pallas import tpu_sc as plsc

F32 = jnp.float32
BF16 = jnp.bfloat16
HIGHEST = lax.Precision.HIGHEST
MESH = pl.DeviceIdType.MESH

D_MODEL = 1024
N_META = 16
EPS = 1e-6
SSD_HEADS = 16
SSD_HEAD_DIM = 64
SSD_INNER = 1024
SSD_GROUPS = 4
SSD_STATE = 128
SSD_CONV = 4
SSD_CONV_CH = 2048
HG_HEADS = 8
HG_SUB = 32
CHUNK = 128
D_FF = 2816
N_CHIPS = 4
IN_SIZES = (1024, 2048, 16, 1024, 1024, 1024, 1024, 1024, 1024)
ADAM_LR = 0.001
ADAM_B1 = 0.9
ADAM_B2 = 0.999
ADAM_EPS = 1e-08
ADAM_WD = 0.01
ADAM_STEP = 10
VMEM_LIMIT = 56 * 1024 * 1024
MATMUL_BLOCK_BYTES = 42 * 1024 * 1024
ADAMW_BLOCK_BYTES = 5 * 512 * 1024


def _cparams(sem=None):
    return pltpu.CompilerParams(dimension_semantics=sem, vmem_limit_bytes=VMEM_LIMIT)


def _pick(n, cands):
    for c in cands:
        if n % c == 0:
            return c
    return n


def _deps(after):
    xs = after if isinstance(after, (list, tuple)) else [after]
    one = lambda x: lax.slice(x, (0,) * x.ndim, (1,) * x.ndim).reshape(1).astype(F32)
    return jnp.concatenate([one(x) for x in xs]).reshape(1, -1)


def _dep_spec(dep):
    return pl.BlockSpec(dep.shape, lambda *_: (0, 0))


def _skip_ref(body, pos):
    return lambda *refs: body(*refs[:pos], *refs[pos + 1:])


def _dg(a, b, ca, cb):
    return lax.dot_general(a.astype(BF16), b.astype(BF16), (((ca,), (cb,)), ((), ())), preferred_element_type=F32)


@jax.custom_vjp
def _mm(a, b):
    return _dg(a, b, 1, 0)


def _mm_fwd(a, b):
    return _dg(a, b, 1, 0), (a, b)


def _mm_bwd(r, g):
    a, b = r
    return _dg(g, b, 1, 1), _dg(a, g, 0, 0)


_mm.defvjp(_mm_fwd, _mm_bwd)


@jax.custom_vjp
def _mm_nt(a, b):
    return _dg(a, b, 1, 1)


def _mm_nt_fwd(a, b):
    return _dg(a, b, 1, 1), (a, b)


def _mm_nt_bwd(r, g):
    a, b = r
    return _dg(g, b, 1, 0), _dg(g, a, 0, 0)


_mm_nt.defvjp(_mm_nt_fwd, _mm_nt_bwd)


@jax.custom_vjp
def _mm_tn(a, b):
    return _dg(a, b, 0, 0)


def _mm_tn_fwd(a, b):
    return _dg(a, b, 0, 0), (a, b)


def _mm_tn_bwd(r, g):
    a, b = r
    return _dg(b, g, 1, 1), _dg(a, g, 1, 0)


_mm_tn.defvjp(_mm_tn_fwd, _mm_tn_bwd)


def _tri_sum(x, lower):
    n = x.shape[0]
    ri = lax.broadcasted_iota(jnp.int32, (n, n), 0)
    ci = lax.broadcasted_iota(jnp.int32, (n, n), 1)
    tri = ((ri >= ci) if lower else (ri <= ci)).astype(BF16)
    x1 = x.astype(BF16)
    r1 = x - x1.astype(F32)
    x2 = r1.astype(BF16)
    x3 = (r1 - x2.astype(F32)).astype(BF16)
    dot = lambda p: lax.dot_general(tri, p, (((1,), (0,)), ((), ())), preferred_element_type=F32)
    return (dot(x3) + dot(x2)) + dot(x1)


@jax.custom_vjp
def _cumsum_rows(x):
    return _tri_sum(x, True)


_cumsum_rows.defvjp(lambda x: (_tri_sum(x, True), None), lambda _, g: (_tri_sum(g, False),))


def _silu(x):
    return x * jax.nn.sigmoid(x)


def _softplus(x):
    return jnp.maximum(x, 0.0) + jnp.log(1.0 + jnp.exp(-jnp.abs(x)))


def _tril(n):
    ri = lax.broadcasted_iota(jnp.int32, (n, n), 0)
    ci = lax.broadcasted_iota(jnp.int32, (n, n), 1)
    return ri >= ci


def _row_of(m, r):
    sub = lax.broadcasted_iota(jnp.int32, (m.shape[0], 1), 0)
    return jnp.sum(jnp.where(sub == r, m, 0.0), axis=0, keepdims=True)


def _col_of(m, c):
    lane = lax.broadcasted_iota(jnp.int32, (1, m.shape[1]), 1)
    return jnp.sum(jnp.where(lane == c, m, 0.0), axis=1, keepdims=True)


def _matmul(a, b, *, mode, out_dtype, name, alpha=1.0, res=None, tm=None, tn=None, out_groups=None, after=None):
    b3 = b.ndim == 3
    if mode == "nn":
        M, K = a.shape
        G = b.shape[0] if b3 else 1
        Ng = b.shape[-1]
        N = G * Ng
    elif mode == "nt":
        M, K = a.shape
        G = b.shape[0] if b3 else 1
        N = b.shape[-2]
        Kg = b.shape[-1]
        assert G * Kg == K
    else:
        K, M = a.shape
        N = b.shape[1]
        G = out_groups or 1
        Ng = N // G
    has_res = res is not None
    split_n = (mode == "nn" and b3) or (mode == "tn" and G > 1)
    per_mn = jnp.dtype(out_dtype).itemsize + (res.dtype.itemsize if has_res else 0)
    fits = [(m_ * n_, m_, n_)
            for m_ in (4352, 2176, 1408, 1088, 1024, 544, 512, 256, 128) if M % m_ == 0
            for n_ in (2816, 2048, 1408, 1024, 512, 256, 128) if (Ng if split_n else N) % n_ == 0
            if 2 * (K * m_ * a.dtype.itemsize + K * n_ * b.dtype.itemsize + m_ * n_ * per_mn) + 4 * m_ * n_ <= MATMUL_BLOCK_BYTES]
    _, tm_fit, tn_fit = max(fits)
    tm, tn = tm or tm_fit, tn or tn_fit
    nm, nn_ = M // tm, N // tn
    assert nm * tm == M and nn_ * tn == N, (name, M, N, K, tm, tn)

    if mode == "nn":
        a_spec = pl.BlockSpec((tm, K), lambda i, j: (i, 0))
        if b3:
            ns = Ng // tn
            b_spec = pl.BlockSpec((None, K, tn), lambda i, j: (j // ns, 0, j % ns))
        else:
            b_spec = pl.BlockSpec((K, tn), lambda i, j: (0, j))
        ca, cb = 1, 0
    elif mode == "nt":
        a_spec = pl.BlockSpec((tm, K), lambda i, j: (i, 0))
        if b3:
            b_spec = pl.BlockSpec((G, tn, Kg), lambda i, j: (0, j, 0))
        else:
            b_spec = pl.BlockSpec((tn, K), lambda i, j: (j, 0))
        ca, cb = 1, 1
    else:
        a_spec = pl.BlockSpec((K, tm), lambda i, j: (0, i))
        b_spec = pl.BlockSpec((K, tn), lambda i, j: (0, j))
        ca, cb = 0, 0
    if mode == "tn" and G > 1:
        ns = Ng // tn
        o_spec = pl.BlockSpec((None, tm, tn), lambda i, j: (j // ns, i, j % ns))
        out_shape = jax.ShapeDtypeStruct((G, M, Ng), out_dtype)
    else:
        o_spec = pl.BlockSpec((tm, tn), lambda i, j: (i, j))
        out_shape = jax.ShapeDtypeStruct((M, N), out_dtype)
    in_specs = [a_spec, b_spec]
    args = [a, b]
    if has_res:
        in_specs.append(pl.BlockSpec((tm, tn), lambda i, j: (i, j)))
        args.append(res)
    if after is not None:
        args.append(_deps(after))
        in_specs.append(_dep_spec(args[-1]))

    def body(*refs):
        a_ref, b_ref, o_ref = refs[0], refs[1], refs[-1]
        if mode == "nt" and b3:
            o = _dg(a_ref[:, 0:Kg], b_ref[0], ca, cb)
            for g in range(1, G):
                o = o + _dg(a_ref[:, g * Kg:(g + 1) * Kg], b_ref[g], ca, cb)
        else:
            o = _dg(a_ref[...], b_ref[...], ca, cb)
        if alpha != 1.0:
            o = o * alpha
        if has_res:
            o = o + refs[2][...]
        o_ref[...] = o.astype(o_ref.dtype)

    return pl.pallas_call(
        body, grid=(nm, nn_), in_specs=in_specs, out_specs=o_spec, out_shape=out_shape, name=name,
        compiler_params=_cparams(("parallel", "parallel")),
    )(*args)


def _sum_nt(xs, ws, name):
    R, N = xs[0].shape[0], ws[0].shape[0]
    n = len(xs)
    per_m = sum(x.shape[1] * x.dtype.itemsize for x in xs)
    per_n = sum(w.shape[1] * w.dtype.itemsize for w in ws)
    fits = [(m_ * n_, m_, n_) for m_ in (1088, 544, 256, 128) if R % m_ == 0 for n_ in (1024, 512, 256, 128) if N % n_ == 0
            if 2 * (m_ * per_m + n_ * per_n + m_ * n_ * 4) + 4 * m_ * n_ <= MATMUL_BLOCK_BYTES]
    _, tm, tn = max(fits)

    def body(*refs):
        o = _dg(refs[0][...], refs[n][...], 1, 1)
        for p in range(1, n):
            o = o + _dg(refs[p][...], refs[n + p][...], 1, 1)
        refs[-1][...] = o

    return pl.pallas_call(
        body, grid=(R // tm, N // tn),
        in_specs=[pl.BlockSpec((tm, x.shape[1]), lambda i, j: (i, 0)) for x in xs]
        + [pl.BlockSpec((tn, w.shape[1]), lambda i, j: (j, 0)) for w in ws],
        out_specs=pl.BlockSpec((tm, tn), lambda i, j: (i, j)), out_shape=jax.ShapeDtypeStruct((R, N), F32), name=name,
        compiler_params=_cparams(("parallel", "parallel")),
    )(*xs, *ws)


def _rms_fn(h, w):
    r = lax.rsqrt(jnp.mean(h * h, axis=-1, keepdims=True) + EPS)
    return h * r * w


def _swiglu_fn(gu):
    g = gu[:, :D_FF].astype(F32)
    u = gu[:, D_FF:].astype(F32)
    return _silu(g) * u


def _merge_fn(pa, pb, gates):
    return jax.nn.sigmoid(gates[:, :D_MODEL]) * pa + jax.nn.sigmoid(gates[:, D_MODEL:]) * pb


def _rows_call(body, *, rows, tr, ins, outs, accs=(), name, after=None):
    n = rows // tr
    assert n * tr == rows
    if after is not None:
        body = _skip_ref(body, len(ins))
        ins = list(ins) + [("full", _deps(after))]

    def spec(x):
        if isinstance(x, tuple):
            shp = x[1].shape
            return pl.BlockSpec(shp, lambda i: (0,) * len(shp))
        return pl.BlockSpec((tr, x.shape[1]), lambda i: (i, 0))

    in_specs = [spec(x) for x in ins]
    args = [x[1] if isinstance(x, tuple) else x for x in ins]
    out_specs = [spec(x) for x in outs] + [pl.BlockSpec(x.shape, lambda i: (0,) * len(x.shape)) for x in accs]
    out_shape = [x[1] if isinstance(x, tuple) else x for x in outs] + list(accs)
    return pl.pallas_call(
        body, grid=(n,), in_specs=in_specs, out_specs=out_specs, out_shape=out_shape, name=name,
        compiler_params=_cparams(("arbitrary",)),
    )(*args)


def _acc_rows(ref, val):
    @pl.when(pl.program_id(0) == 0)
    def _():
        ref[...] = jnp.zeros_like(ref)

    ref[0:1, :] += val


def _rms_fwd(h, w, name):
    def body(h_ref, w_ref, o_ref):
        o_ref[...] = _rms_fn(h_ref[...], w_ref[...]).astype(o_ref.dtype)

    R = h.shape[0]
    return _rows_call(body, rows=R, tr=_pick(R, (256, 128)), ins=[h, ("full", w)],
                      outs=[jax.ShapeDtypeStruct(h.shape, BF16)], name=name)[0]


def _rms_bwd(h, w, dn, dres, name, after=None):
    def body(h_ref, w_ref, dn_ref, dres_ref, dh_ref, dw_ref):
        _, vjp = jax.vjp(_rms_fn, h_ref[...], w_ref[...])
        dh, dw = vjp(dn_ref[...].astype(F32))
        dh_ref[...] = dh + dres_ref[...]
        _acc_rows(dw_ref, dw)

    R = h.shape[0]
    return _rows_call(body, rows=R, tr=_pick(R, (256, 128)), ins=[h, ("full", w), dn, dres],
                      outs=[jax.ShapeDtypeStruct(h.shape, F32)], accs=[jax.ShapeDtypeStruct((8, D_MODEL), F32)], name=name,
                      after=after)


def _d_norm_in(dgu, w_gu, h, norm_w, dres, name, after=None):
    R = h.shape[0]
    G, _, kg = w_gu.shape

    def body(dgu_ref, w_ref, h_ref, nw_ref, dres_ref, dh_ref, dw_ref):
        dn = _dg(dgu_ref[:, 0:kg], w_ref[0], 1, 1)
        for g in range(1, G):
            dn = dn + _dg(dgu_ref[:, kg * g:kg * (g + 1)], w_ref[g], 1, 1)
        _, vjp = jax.vjp(_rms_fn, h_ref[...], nw_ref[...])
        dh, dw = vjp(dn)
        dh_ref[...] = dh + dres_ref[...]
        _acc_rows(dw_ref, dw)

    return _rows_call(body, rows=R, tr=_pick(R, (256, 128)), ins=[dgu, ("full", w_gu), h, ("full", norm_w), dres],
                      outs=[jax.ShapeDtypeStruct(h.shape, F32)], accs=[jax.ShapeDtypeStruct((8, D_MODEL), F32)], name=name,
                      after=after)


def _rms_bwd_tokens(h, w, dn, dres, nseq, name):
    Tp = h.shape[0] // nseq
    nc = Tp // CHUNK

    def body(h_ref, w_ref, dn_ref, dres_ref, dx_ref, dm_ref, dw_ref):
        b, c = pl.program_id(0), pl.program_id(1)
        _, vjp = jax.vjp(_rms_fn, h_ref[...], w_ref[...])
        dh, dw = vjp(dn_ref[...].astype(F32))
        dh = dh + dres_ref[...]

        @pl.when(c == 0)
        def _():
            dm_ref[...] = dh

        @pl.when(c > 0)
        def _():
            dx_ref[...] = dh

        @pl.when((b == 0) & (c == 0))
        def _():
            dw_ref[...] = jnp.zeros_like(dw_ref)

        dw_ref[0:1, :] += dw

    rows = pl.BlockSpec((CHUNK, D_MODEL), lambda b, c: (b * nc + c, 0))
    return pl.pallas_call(
        body, grid=(nseq, nc),
        in_specs=[rows, pl.BlockSpec((1, D_MODEL), lambda b, c: (0, 0)), rows, rows],
        out_specs=[pl.BlockSpec((None, CHUNK, D_MODEL), lambda b, c: (b, jnp.maximum(c - 1, 0), 0)),
                   pl.BlockSpec((None, CHUNK, D_MODEL), lambda b, c: (b, 0, 0)),
                   pl.BlockSpec((8, D_MODEL), lambda b, c: (0, 0))],
        out_shape=[jax.ShapeDtypeStruct((nseq, Tp - CHUNK, D_MODEL), F32), jax.ShapeDtypeStruct((nseq, CHUNK, D_MODEL), F32),
                   jax.ShapeDtypeStruct((8, D_MODEL), F32)],
        name=name, compiler_params=_cparams(("arbitrary", "arbitrary")),
    )(h, w, dn, dres)


def _gu_swiglu(n, w_gu, name):
    R = n.shape[0]
    G, _, ng = w_gu.shape

    def body(n_ref, w_ref, gu_ref, a_ref):
        x = n_ref[...]
        for r in range(G):
            gu_ref[:, ng * r:ng * (r + 1)] = _dg(x, w_ref[r], 1, 0).astype(gu_ref.dtype)
        a_ref[...] = _swiglu_fn(gu_ref[...]).astype(a_ref.dtype)

    return _rows_call(body, rows=R, tr=_pick(R, (256, 128)), ins=[n, ("full", w_gu)],
                      outs=[jax.ShapeDtypeStruct((R, 2 * D_FF), BF16), jax.ShapeDtypeStruct((R, D_FF), BF16)], name=name)


def _d_swiglu(dout, w_down, gu, alpha, name):
    R = gu.shape[0]

    def body(do_ref, w_ref, gu_ref, o_ref):
        da = _dg(do_ref[...] * alpha, w_ref[...], 1, 1)
        g = gu_ref[:, :D_FF].astype(F32)
        u = gu_ref[:, D_FF:].astype(F32)
        s = jax.nn.sigmoid(g)
        t = g * s
        o_ref[:, :D_FF] = (da * u * (s + t - t * s)).astype(o_ref.dtype)
        o_ref[:, D_FF:] = (da * t).astype(o_ref.dtype)

    return _rows_call(body, rows=R, tr=_pick(R, (256, 128)), ins=[dout, ("full", w_down), gu],
                      outs=[jax.ShapeDtypeStruct(gu.shape, BF16)], name=name)[0]


def _residual_matmul(a, w, res, alpha, name, norm_w=None):
    R, K = a.shape

    def body(a_ref, w_ref, r_ref, *rest):
        out = r_ref[...] + alpha * _dg(a_ref[...], w_ref[...], 1, 0)
        if norm_w is None:
            rest[0][...] = out
        else:
            rest[1][...] = out
            rest[2][...] = _rms_fn(out, rest[0][...]).astype(rest[2].dtype)

    f32 = jax.ShapeDtypeStruct((R, D_MODEL), F32)
    ins = [a, ("full", w), res] + ([] if norm_w is None else [("full", norm_w)])
    outs = [f32] + ([] if norm_w is None else [jax.ShapeDtypeStruct((R, D_MODEL), BF16)])
    got = _rows_call(body, rows=R, tr=_pick(R, (544, 256, 128)), ins=ins, outs=outs, name=name)
    return got[0] if norm_w is None else (got[0], got[1])


def _branch_merge(ya, yb, wa, wb, gates, name):
    def body(ya_ref, yb_ref, wa_ref, wb_ref, g_ref, pa_ref, pb_ref, o_ref):
        pa = _dg(ya_ref[...], wa_ref[...], 1, 0)
        pb = _dg(yb_ref[...], wb_ref[...], 1, 0)
        pa_ref[...] = pa
        pb_ref[...] = pb
        o_ref[...] = _merge_fn(pa, pb, g_ref[...].astype(F32)).astype(o_ref.dtype)

    R = ya.shape[0]
    f32 = jax.ShapeDtypeStruct((R, D_MODEL), F32)
    return _rows_call(body, rows=R, tr=_pick(R, (544, 256, 128)), ins=[ya, yb, ("full", wa), ("full", wb), gates],
                      outs=[f32, f32, jax.ShapeDtypeStruct((R, D_MODEL), BF16)], name=name)


def _branch_merge_bwd(pa, pb, gates, dm, wa, wb, name):
    def body(pa_ref, pb_ref, g_ref, dm_ref, wa_ref, wb_ref, dpa_ref, dpb_ref, dg_ref, dya_ref, dyb_ref):
        _, vjp = jax.vjp(_merge_fn, pa_ref[...], pb_ref[...], g_ref[...].astype(F32))
        dpa, dpb, dg = vjp(dm_ref[...].astype(F32))
        dpa_ref[...] = dpa.astype(dpa_ref.dtype)
        dpb_ref[...] = dpb.astype(dpb_ref.dtype)
        dg_ref[...] = dg.astype(dg_ref.dtype)
        dya_ref[...] = _dg(dpa, wa_ref[...], 1, 1).astype(dya_ref.dtype)
        dyb_ref[...] = _dg(dpb, wb_ref[...], 1, 1).astype(dyb_ref.dtype)

    R = pa.shape[0]
    b16 = jax.ShapeDtypeStruct(pa.shape, BF16)
    return _rows_call(body, rows=R, tr=_pick(R, (544, 256, 128)), ins=[pa, pb, gates, dm, ("full", wa), ("full", wb)],
                      outs=[b16, b16, jax.ShapeDtypeStruct(gates.shape, BF16), b16, b16], name=name)


def _loss_head(h3, w, target, nseq, name):
    Tp = h3.shape[0] // nseq
    nc = Tp // CHUNK

    def fn(h, w_, t, valid):
        y = _rms_fn(h, w_)
        e = (y - t) * valid
        return 0.5 * jnp.sum(jnp.mean(e * e, axis=-1, keepdims=True))

    def body(h_ref, w_ref, t_ref, loss_ref, dh_ref, dw_ref):
        b, c = pl.program_id(0), pl.program_id(1)
        valid = (c >= 1).astype(F32)
        t = t_ref[...]
        loss, vjp = jax.vjp(lambda h, w_: fn(h, w_, t, valid), h_ref[...], w_ref[...])
        dh, dw = vjp(jnp.ones((), F32))
        dh_ref[...] = dh

        @pl.when((b == 0) & (c == 0))
        def _():
            loss_ref[...] = jnp.zeros_like(loss_ref)
            dw_ref[...] = jnp.zeros_like(dw_ref)

        loss_ref[...] += jnp.full(loss_ref.shape, loss, F32)
        dw_ref[0:1, :] += dw

    return pl.pallas_call(
        body, grid=(nseq, nc),
        in_specs=[pl.BlockSpec((CHUNK, D_MODEL), lambda b, c: (b * nc + c, 0)),
                  pl.BlockSpec((1, D_MODEL), lambda b, c: (0, 0)),
                  pl.BlockSpec((None, CHUNK, D_MODEL), lambda b, c: (b, jnp.maximum(c - 1, 0), 0))],
        out_specs=[pl.BlockSpec((8, 128), lambda b, c: (0, 0)),
                   pl.BlockSpec((CHUNK, D_MODEL), lambda b, c: (b * nc + c, 0)),
                   pl.BlockSpec((8, D_MODEL), lambda b, c: (0, 0))],
        out_shape=[jax.ShapeDtypeStruct((8, 128), F32), jax.ShapeDtypeStruct(h3.shape, F32),
                   jax.ShapeDtypeStruct((8, D_MODEL), F32)],
        name=name, compiler_params=_cparams(("arbitrary", "arbitrary")),
    )(h3, w, target)


CONV_TILE = 512
CONV_HALO = 8


def _conv_fwd(xbc, w, b, pad, name):
    B, Tp, C = xbc.shape
    nch = Tp // CHUNK

    def body(x_ref, w_ref, b_ref, o_ref, xp):
        xp[0:CONV_HALO, :] = jnp.zeros((CONV_HALO, CONV_TILE), F32)
        xp[CONV_HALO:, :] = x_ref[...]
        for c in range(nch):
            acc = jnp.zeros((CHUNK, CONV_TILE), F32) + b_ref[...]
            for k in range(SSD_CONV):
                acc = acc + w_ref[k:k + 1, :] * xp[pl.ds(CONV_HALO + CHUNK * c - (SSD_CONV - 1) + k, CHUNK), :]
            out = _silu(acc)
            if CHUNK * c < pad:
                row = CHUNK * c + lax.broadcasted_iota(jnp.int32, (CHUNK, 1), 0)
                out = jnp.where(row >= pad, out, 0.0)
            o_ref[pl.ds(CHUNK * c, CHUNK), :] = out

    return pl.pallas_call(
        body, grid=(B, C // CONV_TILE),
        in_specs=[pl.BlockSpec((None, Tp, CONV_TILE), lambda i, j: (i, 0, j)),
                  pl.BlockSpec((SSD_CONV, CONV_TILE), lambda i, j: (0, j)),
                  pl.BlockSpec((1, CONV_TILE), lambda i, j: (0, j))],
        out_specs=pl.BlockSpec((None, Tp, CONV_TILE), lambda i, j: (i, 0, j)),
        out_shape=jax.ShapeDtypeStruct(xbc.shape, F32),
        scratch_shapes=[pltpu.VMEM((Tp + CONV_HALO, CONV_TILE), F32)],
        name=name, compiler_params=_cparams(("arbitrary", "arbitrary")),
    )(xbc, w, b)


def _conv_bwd(xbc, w, b, dact, pad, name):
    B, Tp, C = xbc.shape
    nch = Tp // CHUNK

    def body(x_ref, w_ref, b_ref, da_ref, dx_ref, dw_ref, db_ref, xp, dp):
        bi = pl.program_id(1)
        xp[0:CONV_HALO, :] = jnp.zeros((CONV_HALO, CONV_TILE), F32)
        xp[CONV_HALO:, :] = x_ref[...]
        dp[pl.ds(Tp, CONV_HALO), :] = jnp.zeros((CONV_HALO, CONV_TILE), F32)
        dws = [jnp.zeros((1, CONV_TILE), F32) for _ in range(SSD_CONV)]
        dbs = jnp.zeros((1, CONV_TILE), F32)
        for c in range(nch):
            xs = [xp[pl.ds(CONV_HALO + CHUNK * c - (SSD_CONV - 1) + k, CHUNK), :] for k in range(SSD_CONV)]
            acc = jnp.zeros((CHUNK, CONV_TILE), F32) + b_ref[...]
            for k in range(SSD_CONV):
                acc = acc + w_ref[k:k + 1, :] * xs[k]
            sg = jax.nn.sigmoid(acc)
            t = acc * sg
            dpre = da_ref[pl.ds(CHUNK * c, CHUNK), :] * (sg + t - t * sg)
            if CHUNK * c < pad:
                row = CHUNK * c + lax.broadcasted_iota(jnp.int32, (CHUNK, 1), 0)
                dpre = jnp.where(row >= pad, dpre, 0.0)
            dp[pl.ds(CHUNK * c, CHUNK), :] = dpre
            dbs = dbs + jnp.sum(dpre, axis=0, keepdims=True)
            for k in range(SSD_CONV):
                dws[k] = dws[k] + jnp.sum(dpre * xs[k], axis=0, keepdims=True)
        for c in range(nch):
            acc = jnp.zeros((CHUNK, CONV_TILE), F32)
            for k in range(SSD_CONV):
                acc = acc + w_ref[k:k + 1, :] * dp[pl.ds(CHUNK * c + (SSD_CONV - 1) - k, CHUNK), :]
            dx_ref[pl.ds(CHUNK * c, CHUNK), :] = acc.astype(dx_ref.dtype)

        @pl.when(bi == 0)
        def _():
            dw_ref[...] = jnp.zeros_like(dw_ref)
            db_ref[...] = jnp.zeros_like(db_ref)

        for k in range(SSD_CONV):
            dw_ref[k:k + 1, :] += dws[k]
        db_ref[0:1, :] += dbs

    return pl.pallas_call(
        body, grid=(C // CONV_TILE, B),
        in_specs=[pl.BlockSpec((None, Tp, CONV_TILE), lambda j, i: (i, 0, j)),
                  pl.BlockSpec((SSD_CONV, CONV_TILE), lambda j, i: (0, j)),
                  pl.BlockSpec((1, CONV_TILE), lambda j, i: (0, j)),
                  pl.BlockSpec((None, Tp, CONV_TILE), lambda j, i: (i, 0, j))],
        out_specs=[pl.BlockSpec((None, Tp, CONV_TILE), lambda j, i: (i, 0, j)),
                   pl.BlockSpec((8, CONV_TILE), lambda j, i: (0, j)),
                   pl.BlockSpec((8, CONV_TILE), lambda j, i: (0, j))],
        out_shape=[jax.ShapeDtypeStruct(xbc.shape, BF16), jax.ShapeDtypeStruct((8, C), F32),
                   jax.ShapeDtypeStruct((8, C), F32)],
        scratch_shapes=[pltpu.VMEM((Tp + CONV_HALO, CONV_TILE), F32), pltpu.VMEM((Tp + CONV_HALO, CONV_TILE), F32)],
        name=name, compiler_params=_cparams(("arbitrary", "arbitrary")),
    )(xbc, w, b, dact)


def _ssd_chunk(xs, bm, cm, dtr, z, state, dt_bias, a_log, dskip, norm_w, valid):
    Q = xs.shape[0]
    lane = lax.broadcasted_iota(jnp.int32, (1, 128), 1)
    dt = jnp.where(lane < SSD_HEADS, _softplus(dtr + dt_bias), 0.0) * valid
    a = dt * (-jnp.exp(a_log))
    tril = _tril(Q)
    cs = _cumsum_rows(a)
    cs_t = cs.T
    cs_end = _row_of(cs, Q - 1)
    low = lane < SSD_HEAD_DIM
    low_rows = lax.broadcasted_iota(jnp.int32, (128, 1), 0) < SSD_HEAD_DIM
    ys, new_state = [], []
    for g in range(SSD_GROUPS):
        bg = bm[:, 128 * g:128 * (g + 1)]
        cg = cm[:, 128 * g:128 * (g + 1)]
        cb = _mm_nt(cg, bg)
        for pr in range(2):
            p = 2 * g + pr
            h0, h1 = 2 * p, 2 * p + 1
            xp = xs[:, 128 * p:128 * (p + 1)]
            c0, c1 = _col_of(cs, h0), _col_of(cs, h1)
            e0, e1 = _col_of(cs_end, h0), _col_of(cs_end, h1)
            xd = xp * jnp.where(low, _col_of(dt, h0), _col_of(dt, h1))
            l0 = jnp.exp(jnp.where(tril, c0 - _row_of(cs_t, h0), -1e30))
            l1 = jnp.exp(jnp.where(tril, c1 - _row_of(cs_t, h1), -1e30))
            y_diag = jnp.where(low, _mm(cb * l0, xd), _mm(cb * l1, xd))
            to_end = jnp.where(low, jnp.exp(e0 - c0), jnp.exp(e1 - c1))
            sp = state[128 * p:128 * (p + 1), :]
            y_off = _mm_nt(cg, sp) * jnp.where(low, jnp.exp(c0), jnp.exp(c1))
            new_state.append(sp * jnp.where(low_rows, jnp.exp(e0), jnp.exp(e1)) + _mm_tn(xd * to_end, bg))
            ys.append(y_diag + y_off + xp * jnp.where(low, _col_of(dskip, h0), _col_of(dskip, h1)))
    y = jnp.concatenate(ys, axis=1) * _silu(z)
    gw = SSD_INNER // SSD_GROUPS
    outs = []
    for g in range(SSD_GROUPS):
        blk = y[:, gw * g:gw * (g + 1)]
        outs.append(blk * lax.rsqrt(jnp.mean(blk * blk, axis=-1, keepdims=True) + EPS))
    return jnp.concatenate(outs, axis=1) * norm_w, jnp.concatenate(new_state, axis=0)


def _valid_rows(c, pad):
    row = c * CHUNK + lax.broadcasted_iota(jnp.int32, (CHUNK, 1), 0)
    return (row >= pad).astype(F32)


def _ssd_fwd(xact, dtr, z, dt_bias, a_log, dskip, norm_w, pad, name):
    B, Tp, _ = xact.shape
    nc = Tp // CHUNK

    def body(xs_ref, bm_ref, cm_ref, dt_ref, z_ref, db_ref, al_ref, ds_ref, nw_ref, y_ref, save_ref, st):
        c = pl.program_id(1)

        @pl.when(c == 0)
        def _():
            st[...] = jnp.zeros_like(st)

        s0 = st[...]
        save_ref[...] = s0
        y, s1 = _ssd_chunk(xs_ref[...], bm_ref[...], cm_ref[...], dt_ref[...], z_ref[...].astype(F32), s0, db_ref[...],
                           al_ref[...], ds_ref[...], nw_ref[...], _valid_rows(c, pad))
        y_ref[...] = y.astype(y_ref.dtype)
        st[...] = s1

    row = lambda w, off=0: pl.BlockSpec((None, CHUNK, w), lambda b, c: (b, c, off))
    par = lambda w: pl.BlockSpec((1, w), lambda b, c: (0, 0))
    return pl.pallas_call(
        body, grid=(B, nc),
        in_specs=[row(1024, 0), row(512, 2), row(512, 3), row(128), row(1024), par(128), par(128), par(128), par(1024)],
        out_specs=[row(1024), pl.BlockSpec((None, None, 1024, 128), lambda b, c: (b, c, 0, 0))],
        out_shape=[jax.ShapeDtypeStruct((B, Tp, SSD_INNER), BF16), jax.ShapeDtypeStruct((B, nc, 1024, 128), F32)],
        scratch_shapes=[pltpu.VMEM((1024, 128), F32)],
        name=name, compiler_params=_cparams(("arbitrary", "arbitrary")),
    )(xact, xact, xact, dtr, z, dt_bias, a_log, dskip, norm_w)


def _ssd_bwd(xact, dtr, z, dt_bias, a_log, dskip, norm_w, saved, dy, pad, name, after=None):
    B, Tp, _ = xact.shape
    nc = Tp // CHUNK

    def body(xs_ref, bm_ref, cm_ref, dt_ref, z_ref, db_ref, al_ref, ds_ref, nw_ref, sv_ref, dy_ref,
             dx_ref, ddt_ref, dz_ref, dpar_ref, dnw_ref, dst):
        b, i = pl.program_id(0), pl.program_id(1)
        c = nc - 1 - i

        @pl.when(i == 0)
        def _():
            dst[...] = jnp.zeros_like(dst)

        valid = _valid_rows(c, pad)
        fn = lambda *a: _ssd_chunk(*a, valid)
        _, vjp = jax.vjp(fn, xs_ref[...], bm_ref[...], cm_ref[...], dt_ref[...], z_ref[...].astype(F32), sv_ref[...],
                         db_ref[...], al_ref[...], ds_ref[...], nw_ref[...])
        dxs, dbm, dcm, ddt, dz, dstate, ddb, dal, dds, dnw = vjp((dy_ref[...].astype(F32), dst[...]))
        dx_ref[:, 0:1024] = dxs
        dx_ref[:, 1024:1536] = dbm
        dx_ref[:, 1536:2048] = dcm
        ddt_ref[...] = ddt
        dz_ref[...] = dz.astype(dz_ref.dtype)
        dst[...] = dstate

        @pl.when((b == 0) & (i == 0))
        def _():
            dpar_ref[...] = jnp.zeros_like(dpar_ref)
            dnw_ref[...] = jnp.zeros_like(dnw_ref)

        dpar_ref[0:1, :] += ddb
        dpar_ref[1:2, :] += dal
        dpar_ref[2:3, :] += dds
        dnw_ref[0:1, :] += dnw

    row = lambda w, off=0: pl.BlockSpec((None, CHUNK, w), lambda b, i: (b, nc - 1 - i, off))
    par = lambda w: pl.BlockSpec((1, w), lambda b, i: (0, 0))
    acc = lambda w: pl.BlockSpec((8, w), lambda b, i: (0, 0))
    in_specs = [row(1024, 0), row(512, 2), row(512, 3), row(128), row(1024), par(128), par(128), par(128), par(1024),
                pl.BlockSpec((None, None, 1024, 128), lambda b, i: (b, nc - 1 - i, 0, 0)), row(1024)]
    args = [xact, xact, xact, dtr, z, dt_bias, a_log, dskip, norm_w, saved, dy]
    if after is not None:
        body = _skip_ref(body, len(args))
        args.append(_deps(after))
        in_specs.append(_dep_spec(args[-1]))
    outs = pl.pallas_call(
        body, grid=(B, nc), in_specs=in_specs,
        out_specs=[row(2048), row(128), row(1024), acc(128), acc(1024)],
        out_shape=[jax.ShapeDtypeStruct((B, Tp, 2048), F32), jax.ShapeDtypeStruct((B, Tp, 128), F32),
                   jax.ShapeDtypeStruct((B, Tp, 1024), BF16), jax.ShapeDtypeStruct((8, 128), F32),
                   jax.ShapeDtypeStruct((8, 1024), F32)],
        scratch_shapes=[pltpu.VMEM((1024, 128), F32)],
        name=name, compiler_params=_cparams(("arbitrary", "arbitrary")),
    )(*args)
    return outs


def _hg_chunk(qr, fr, ir, gr, state_t, p0, p1, norm_w, valid):
    Q = qr.shape[0]
    lb = jax.nn.sigmoid(p0 - p1)
    f = lb + (1.0 - lb) * jax.nn.sigmoid(fr)
    k = 1.0 - f
    q = _silu(qr)
    v = ir * valid
    cum = _cumsum_rows(jnp.log(f))
    cum_end = _row_of(cum, Q - 1)
    o_inter = _mm_nt(q * jnp.exp(cum), state_t)
    nblk = Q // HG_SUB
    row = lax.broadcasted_iota(jnp.int32, (Q, 1), 0)
    ri = lax.broadcasted_iota(jnp.int32, (Q, Q), 0)
    ci = lax.broadcasted_iota(jnp.int32, (Q, Q), 1)
    mids = jnp.concatenate([jnp.broadcast_to(_row_of(cum, HG_SUB * i + HG_SUB // 2 - 1), (HG_SUB, cum.shape[1]))
                            for i in range(nblk)], axis=0)
    sh = HG_SUB.bit_length() - 1
    same = (jnp.right_shift(ri, sh) == jnp.right_shift(ci, sh)) & (ri >= ci)
    att = jnp.where(same, _mm_nt(q * jnp.exp(cum - mids), k * jnp.exp(mids - cum)), 0.0)
    for i in range(1, nblk):
        lo = HG_SUB * i
        start = _row_of(cum, lo - 1)
        qa = q * jnp.exp(jnp.where((row >= lo) & (row < lo + HG_SUB), cum - start, -1e30))
        ka = k * jnp.exp(jnp.where(row < lo, start - cum, -1e30))
        att = att + _mm_nt(qa, ka)
    o = o_inter + _mm(att, v)
    new_state_t = state_t * jnp.exp(cum_end) + _mm_tn(v, k * jnp.exp(cum_end - cum))
    o = o * lax.rsqrt(jnp.mean(o * o, axis=-1, keepdims=True) + EPS) * norm_w
    return o * _silu(gr), new_state_t


HG_PER_STEP = 8
HG_COLS = 4 * 128


def _hg_fwd(qfig, lbh, nwh, pad, name):
    B, Tp, _ = qfig.shape
    nc = Tp // CHUNK
    hp = HG_PER_STEP

    def body(x_ref, lb_ref, nw_ref, y_ref, save_ref, st):
        c = pl.program_id(1)

        @pl.when(c == 0)
        def _():
            st[...] = jnp.zeros_like(st)

        valid = _valid_rows(c, pad)
        for j in range(hp):
            for b in range(B):
                s0 = st[j, b]
                save_ref[j, b] = s0
                col = lambda k: x_ref[b, :, HG_COLS * j + 128 * k:HG_COLS * j + 128 * (k + 1)]
                y, s1 = _hg_chunk(col(0), col(1), col(2), col(3), s0, lb_ref[j, 0:1, :], lb_ref[j, 1:2, :], nw_ref[j], valid)
                y_ref[b, :, 128 * j:128 * (j + 1)] = y.astype(y_ref.dtype)
                st[j, b] = s1

    return pl.pallas_call(
        body, grid=(HG_HEADS // hp, nc),
        in_specs=[pl.BlockSpec((B, CHUNK, HG_COLS * hp), lambda h, c: (0, c, h)),
                  pl.BlockSpec((hp, 2, 128), lambda h, c: (h, 0, 0)),
                  pl.BlockSpec((hp, 1, 128), lambda h, c: (h, 0, 0))],
        out_specs=[pl.BlockSpec((B, CHUNK, 128 * hp), lambda h, c: (0, c, h)),
                   pl.BlockSpec((hp, B, None, 128, 128), lambda h, c: (h, 0, c, 0, 0))],
        out_shape=[jax.ShapeDtypeStruct((B, Tp, 1024), BF16), jax.ShapeDtypeStruct((HG_HEADS, B, nc, 128, 128), F32)],
        scratch_shapes=[pltpu.VMEM((hp, B, 128, 128), F32)],
        name=name, compiler_params=_cparams(("arbitrary", "arbitrary")),
    )(qfig, lbh, nwh)


def _hg_bwd(qfig, lbh, nwh, saved, dy, pad, name, after=None):
    B, Tp, _ = qfig.shape
    nc = Tp // CHUNK
    hp = HG_PER_STEP

    def body(x_ref, lb_ref, nw_ref, sv_ref, dy_ref, dx_ref, dlb_ref, dnw_ref, dst):
        i = pl.program_id(1)
        c = nc - 1 - i

        @pl.when(i == 0)
        def _():
            dst[...] = jnp.zeros_like(dst)
            dlb_ref[...] = jnp.zeros_like(dlb_ref)
            dnw_ref[...] = jnp.zeros_like(dnw_ref)

        valid = _valid_rows(c, pad)
        fn = lambda *a: _hg_chunk(*a, valid)
        for j in range(hp):
            for b in range(B):
                col = lambda k: x_ref[b, :, HG_COLS * j + 128 * k:HG_COLS * j + 128 * (k + 1)]
                _, vjp = jax.vjp(fn, col(0), col(1), col(2), col(3), sv_ref[j, b], lb_ref[j, 0:1, :], lb_ref[j, 1:2, :], nw_ref[j])
                d4 = vjp((dy_ref[b, :, 128 * j:128 * (j + 1)].astype(F32), dst[j, b]))
                for k in range(4):
                    dx_ref[b, :, HG_COLS * j + 128 * k:HG_COLS * j + 128 * (k + 1)] = d4[k].astype(dx_ref.dtype)
                dst[j, b] = d4[4]
                dlb_ref[j, 0:1, :] += d4[5]
                dlb_ref[j, 1:2, :] += d4[6]
                dnw_ref[j, 0:1, :] += d4[7]

    acc = pl.BlockSpec((hp, 8, 128), lambda h, i: (h, 0, 0))
    in_specs = [pl.BlockSpec((B, CHUNK, HG_COLS * hp), lambda h, i: (0, nc - 1 - i, h)),
                pl.BlockSpec((hp, 2, 128), lambda h, i: (h, 0, 0)),
                pl.BlockSpec((hp, 1, 128), lambda h, i: (h, 0, 0)),
                pl.BlockSpec((hp, B, None, 128, 128), lambda h, i: (h, 0, nc - 1 - i, 0, 0)),
                pl.BlockSpec((B, CHUNK, 128 * hp), lambda h, i: (0, nc - 1 - i, h))]
    args = [qfig, lbh, nwh, saved, dy]
    if after is not None:
        body = _skip_ref(body, len(args))
        args.append(_deps(after))
        in_specs.append(_dep_spec(args[-1]))
    return pl.pallas_call(
        body, grid=(HG_HEADS // hp, nc), in_specs=in_specs,
        out_specs=[pl.BlockSpec((B, CHUNK, HG_COLS * hp), lambda h, i: (0, nc - 1 - i, h)), acc, acc],
        out_shape=[jax.ShapeDtypeStruct((B, Tp, 4096), BF16), jax.ShapeDtypeStruct((HG_HEADS, 8, 128), F32),
                   jax.ShapeDtypeStruct((HG_HEADS, 8, 128), F32)],
        scratch_shapes=[pltpu.VMEM((hp, B, 128, 128), F32)],
        name=name, compiler_params=_cparams(("arbitrary", "arbitrary")),
    )(*args)


def _adamw_math(w, g, m, v):
    m = ADAM_B1 * m + (1.0 - ADAM_B1) * g
    v = ADAM_B2 * v + (1.0 - ADAM_B2) * (g * g)
    m_hat = m / (1.0 - ADAM_B1 ** ADAM_STEP)
    v_hat = v / (1.0 - ADAM_B2 ** ADAM_STEP)
    return -ADAM_LR * (m_hat / (jnp.sqrt(v_hat) + ADAM_EPS) + ADAM_WD * w), m, v


def _adamw_many(ws, gs, ms, vs, name):
    n = len(ws)

    def body(*refs):
        for i in range(n):
            d, m, v = _adamw_math(refs[i][...], refs[n + i][...], refs[2 * n + i][...], refs[3 * n + i][...])
            refs[4 * n + i][...] = d
            refs[5 * n + i][...] = m
            refs[6 * n + i][...] = v

    vm = pl.BlockSpec(memory_space=pltpu.VMEM)
    outs = pl.pallas_call(body, in_specs=[vm] * (4 * n), out_specs=[vm] * (3 * n),
                          out_shape=[jax.ShapeDtypeStruct(w.shape, F32) for w in ws] * 3, name=name)(*ws, *gs, *ms, *vs)
    return outs[:n], outs[n:2 * n], outs[2 * n:]


def _adamw(w, g, m, v, name, after=None):
    R, C = w.shape
    tr = max(t for t in range(8, R + 1, 8) if R % t == 0 and (t * C * 4 <= ADAMW_BLOCK_BYTES or t == 8))

    def body(w_ref, g_ref, m_ref, v_ref, d_ref, mo_ref, vo_ref):
        d_ref[...], mo_ref[...], vo_ref[...] = _adamw_math(w_ref[...], g_ref[...], m_ref[...], v_ref[...])

    sp = pl.BlockSpec((tr, C), lambda i: (i, 0))
    sh = jax.ShapeDtypeStruct((R, C), F32)
    in_specs, args = [sp] * 4, [w, g, m, v]
    if after is not None:
        body = _skip_ref(body, len(args))
        args.append(_deps(after))
        in_specs.append(_dep_spec(args[-1]))
    return pl.pallas_call(body, grid=(R // tr,), in_specs=in_specs, out_specs=[sp] * 3, out_shape=[sh] * 3,
                          name=name, compiler_params=_cparams(("arbitrary",)))(*args)


def _ffn_fwd(h, norm_w, w_gu, w_down, tag, after_norm=None, n=None, next_norm_w=None):
    if n is None:
        n = _rms_fwd(h, norm_w, f"{tag}_norm")
    if after_norm is not None:
        after_norm(n)
    gu, a = _gu_swiglu(n, w_gu, f"{tag}_gu")
    out = _residual_matmul(a, w_down, h, 0.5, f"{tag}_down", next_norm_w)
    return out, (n, gu, a)


def _ffn_bwd(h, norm_w, w_gu, w_down, saved, dout, tag, after_dw_down=None, token_seqs=None):
    n, gu, a = saved
    dgu = _d_swiglu(dout, w_down, gu, 0.5, f"{tag}_d_gu")
    dw_down = _matmul(a, dout, mode="tn", out_dtype=F32, alpha=0.5, name=f"{tag}_dw_down")
    dw_gu = _matmul(n, dgu, mode="tn", out_dtype=F32, out_groups=N_CHIPS, name=f"{tag}_dw_gu",
                    after=after_dw_down(dw_down) if after_dw_down else None)
    if token_seqs is None:
        dh, dnw = _d_norm_in(dgu, w_gu, h, norm_w, dout, f"{tag}_d_in", after=dw_gu)
    else:
        dn = _matmul(dgu, w_gu, mode="nt", out_dtype=F32, name=f"{tag}_d_norm", after=dw_gu)
        dx, dm, dnw = _rms_bwd_tokens(h, norm_w, dn, dout, token_seqs, f"{tag}_d_in")
        dh = (dx, dm)
    return dh, dnw, dw_gu, dw_down


def _split_w_in(w_in_full):
    pts = [0]
    for s in IN_SIZES:
        pts.append(pts[-1] + s)
    sl = lambda i, j: w_in_full[:, pts[i]:pts[j]]
    qfig = sl(3, 7).reshape(D_MODEL, 4, HG_HEADS, 128).transpose(0, 2, 1, 3).reshape(D_MODEL, 4 * D_MODEL)
    return {"z": sl(0, 1), "xbc": sl(1, 2), "dt": jnp.pad(sl(2, 3), ((0, 0), (0, 128 - SSD_HEADS))),
            "qfig": qfig, "gates": sl(7, 9)}


def _local_step(x, target, W):
    B, S, _ = x.shape
    T = N_META + S
    pad = (-T) % CHUNK
    Tp = T + pad
    assert pad + N_META == CHUNK
    R = B * Tp
    meta = jnp.broadcast_to(W["meta_tokens"][None], (B, N_META, D_MODEL))
    h0 = jnp.concatenate([jnp.zeros((B, pad, D_MODEL), F32), meta, x], axis=1).reshape(R, D_MODEL)

    stage = W.get("_stage", lambda name, x: {})
    W = dict(W)
    (h1, um), sv1 = _ffn_fwd(h0, W["ffn1_norm"], W["ffn1_w_gu"], W["ffn1_w_down"], "ffn1",
                             lambda n: W.update(stage("ffn1_norm", n)), next_norm_w=W["mix_norm"])
    W.update(stage("ffn1_out", h1))
    wi = W["w_in"]
    z = _matmul(um, wi["z"], mode="nn", out_dtype=BF16, name="in_z")
    xbc = _matmul(um, wi["xbc"], mode="nn", out_dtype=F32, name="in_xbc")
    dtr = _matmul(um, wi["dt"], mode="nn", out_dtype=F32, name="in_dt")
    qfig = _matmul(um, wi["qfig"], mode="nn", out_dtype=F32, name="in_qfig")
    gates = _matmul(um, wi["gates"], mode="nn", out_dtype=BF16, name="in_gates")

    r3 = lambda t: t.reshape(B, Tp, t.shape[-1])
    lane_pad = lambda t: jnp.pad(t, ((0, 0), (0, 128 - t.shape[1])))
    dt_bias, a_log, dskip = lane_pad(W["ssd_dt_bias"]), lane_pad(W["ssd_a_log"]), lane_pad(W["ssd_d"])
    xact = _conv_fwd(r3(xbc), W["ssd_conv_w"], W["ssd_conv_b"], pad, "conv_fwd")
    ya, ssd_saved = _ssd_fwd(xact, r3(dtr), r3(z), dt_bias, a_log, dskip, W["ssd_norm"], pad, "ssd_fwd")
    lbh = W["hg_lower_bound"].reshape(2, HG_HEADS, 128).transpose(1, 0, 2)
    nwh = W["hg_norm"].reshape(HG_HEADS, 1, 128)
    yb, hg_saved = _hg_fwd(r3(qfig), lbh, nwh, pad, "hg_fwd")
    ya2, yb2 = ya.reshape(R, -1), yb.reshape(R, -1)
    W.update(stage("mixers_out", yb2))
    pa, pb, mg = _branch_merge(ya2, yb2, W["w_branch_a"], W["w_branch_b"], gates, "branch_merge")
    h2, n2 = _residual_matmul(mg, W["w_out"], h1, 1.0, "mix_out", W["ffn2_norm"])
    h3, sv2 = _ffn_fwd(h2, W["ffn2_norm"], W["ffn2_w_gu"], W["ffn2_w_down"], "ffn2", n=n2)

    loss, dh3, d_final = _loss_head(h3, W["final_norm"].reshape(1, D_MODEL), target, B, "loss_head")

    G = {"final_norm": d_final[0]}
    dh2, dnw, G["ffn2_w_gu"], G["ffn2_w_down"] = _ffn_bwd(h2, W["ffn2_norm"], W["ffn2_w_gu"], W["ffn2_w_down"], sv2, dh3, "ffn2")
    G["ffn2_norm"] = dnw[0:1]
    dmg = _matmul(dh2, W["w_out"], mode="nt", out_dtype=BF16, name="d_merge")
    G["w_out"] = _matmul(mg, dh2, mode="tn", out_dtype=F32, name="dw_out")
    dpa, dpb, dgates, dya, dyb = _branch_merge_bwd(pa, pb, gates, dmg, W["w_branch_a"], W["w_branch_b"], "branch_merge_bwd")
    G["w_branch_a"] = _matmul(ya2, dpa, mode="tn", out_dtype=F32, name="dw_branch_a")
    G["w_branch_b"] = _matmul(yb2, dpb, mode="tn", out_dtype=F32, name="dw_branch_b")

    dxact, ddtr, dz, dpar, dnw = _ssd_bwd(xact, r3(dtr), r3(z), dt_bias, a_log, dskip, W["ssd_norm"], ssd_saved,
                                          r3(dya), pad, "ssd_bwd", after=stage("late_grads", G).get("_after"))
    G["ssd_dt_bias"], G["ssd_a_log"], G["ssd_d"] = dpar[0:1, :SSD_HEADS], dpar[1:2, :SSD_HEADS], dpar[2:3, :SSD_HEADS]
    G["ssd_norm"] = dnw[0:1]
    dxbc, dcw, dcb = _conv_bwd(r3(xbc), W["ssd_conv_w"], W["ssd_conv_b"], dxact, pad, "conv_bwd")
    G["ssd_conv_w"], G["ssd_conv_b"] = dcw[0:SSD_CONV], dcb[0:1]
    dqfig, dlb, dhn = _hg_bwd(r3(qfig), lbh, nwh, hg_saved, r3(dyb), pad, "hg_bwd",
                              after=stage("after_conv_bwd", dcb).get("_after"))
    G["hg_lower_bound"] = dlb[:, 0:2, :].transpose(1, 0, 2).reshape(2, D_MODEL)
    G["hg_norm"] = dhn[:, 0, :].reshape(1, D_MODEL)

    r2 = lambda t: t.reshape(R, t.shape[-1])
    pieces = [("z", r2(dz)), ("xbc", r2(dxbc)), ("dt", r2(ddtr)), ("qfig", r2(dqfig)), ("gates", dgates)]
    dum = _sum_nt([p for _, p in pieces], [wi[nm] for nm, _ in pieces], "d_mix")
    dwi = {nm: _matmul(um, dpiece, mode="tn", out_dtype=F32, name=f"dw_in_{nm}") for nm, dpiece in pieces}
    dw_qfig = dwi["qfig"].reshape(D_MODEL, HG_HEADS, 4, 128).transpose(0, 2, 1, 3).reshape(D_MODEL, 4 * D_MODEL)
    G["w_in"] = jnp.concatenate([dwi["z"], dwi["xbc"], dwi["dt"][:, :SSD_HEADS], dw_qfig, dwi["gates"]], axis=1)
    dh1, dnw = _rms_bwd(h1, W["mix_norm"], dum, dh2, "mix_norm_bwd", after=stage("w_in_grads", dwi).get("_after"))
    G["mix_norm"] = dnw[0:1]
    (dx, dfirst), dnw, G["ffn1_w_gu"], G["ffn1_w_down"] = _ffn_bwd(
        h0, W["ffn1_norm"], W["ffn1_w_gu"], W["ffn1_w_down"], sv1, dh1, "ffn1",
        lambda dw: stage("ffn1_dw_down", dw).get("_after"), token_seqs=B)
    G["ffn1_norm"] = dnw[0:1]
    G["meta_tokens"] = jnp.sum(dfirst[:, pad:CHUNK], axis=0)
    return loss, dx, G


ANY = pl.BlockSpec(memory_space=pl.ANY)


def _place():
    return lax.axis_index("x"), lax.axis_index("y"), lax.axis_index("c")


def _other_chips(x, y):
    return [(1 - x, y), (x, 1 - y), (1 - x, 1 - y)]


def _remote(src, dst, ssem, rsem, dev):
    return pltpu.make_async_remote_copy(src_ref=src, dst_ref=dst, send_sem=ssem, recv_sem=rsem,
                                        device_id=dev, device_id_type=MESH)


def _exchange8(buf, name):
    n, w = buf.shape

    def body(x_ref, out_ref, ssem, rsem):
        x, y, c = _place()
        me = 4 * x + 2 * y + c
        out_ref[me] = x_ref[...]
        copies = []
        for k in range(1, 8):
            px = 1 - x if (k >> 2) & 1 else x
            py = 1 - y if (k >> 1) & 1 else y
            pc = 1 - c if k & 1 else c
            cp = _remote(x_ref, out_ref.at[me], ssem.at[k - 1], rsem.at[k - 1], (px, py, pc))
            cp.start()
            copies.append((cp, 4 * px + 2 * py + pc))
        for k, (cp, peer) in enumerate(copies):
            _remote(x_ref, out_ref.at[peer], ssem.at[k], rsem.at[k], (x, y, c)).wait_recv()
        for cp, _ in copies:
            cp.wait_send()

    vm = pl.BlockSpec(memory_space=pltpu.VMEM)
    return pl.pallas_call(
        body, in_specs=[vm], out_specs=vm, out_shape=jax.ShapeDtypeStruct((8, n, w), F32),
        scratch_shapes=[pltpu.SemaphoreType.DMA((7,)), pltpu.SemaphoreType.DMA((7,))], name=name,
    )(buf)


HBM = pltpu.MemorySpace.HBM


def _sequencer(name, collective_id, sems, sent):
    return functools.partial(pl.kernel, mesh=plsc.ScalarSubcoreMesh(axis_name="sequencer", num_cores=1), name=name,
                             scratch_types=sems, compiler_params=pltpu.CompilerParams(collective_id=collective_id),
                             cost_estimate=pl.CostEstimate(flops=0, transcendentals=0, bytes_accessed=2 * sent,
                                                           remote_bytes_transferred=sent))


def _nbytes(arrays):
    return sum(a.size * a.dtype.itemsize for a in arrays)


def _handshake(peers):
    barrier = pltpu.get_barrier_semaphore()
    for peer in peers:
        pl.semaphore_signal(barrier, inc=1, device_id=peer, device_id_type=MESH)
    pl.semaphore_wait(barrier, len(peers))


def _gather_seq(blocks, name, collective_id):
    n = len(blocks)
    half = [s.shape[1] // 2 for s in blocks]
    full = [jax.new_ref(b, memory_space=HBM) for b in blocks]

    @_sequencer(name, collective_id, [pltpu.SemaphoreType.DMA((n, 3))] * 4, _nbytes(blocks) * 3 // 4)
    def launch(ssem, rsem, fssem, frsem):
        x, y, c = _place()
        q = 2 * x + y
        chips = _other_chips(x, y)
        _handshake([(px, py, c) for px, py in chips] + [(x, y, 1 - c)])
        piece = lambda s, qq, cc: full[s].at[qq, pl.ds(cc * half[s], half[s])]
        sends = []
        for j, (px, py) in enumerate(chips):
            for s in range(n):
                cp = _remote(piece(s, q, c), piece(s, q, c), ssem.at[s, j], rsem.at[s, j], (px, py, c))
                cp.start()
                sends.append(cp)
        for j, (px, py) in enumerate(chips):
            for s in range(n):
                got = piece(s, 2 * px + py, c)
                _remote(got, got, ssem.at[s, j], rsem.at[s, j], (px, py, c)).wait_recv()
                cp = _remote(got, got, fssem.at[s, j], frsem.at[s, j], (x, y, 1 - c))
                cp.start()
                sends.append(cp)
        for j, (px, py) in enumerate(chips):
            for s in range(n):
                got = piece(s, 2 * px + py, 1 - c)
                _remote(got, got, fssem.at[s, j], frsem.at[s, j], (x, y, 1 - c)).wait_recv()
        for cp in sends:
            cp.wait_send()

    launch()
    return [r[...] for r in full]


def _share8(buf, name, collective_id):
    n, w = buf.shape
    src = jax.new_ref(buf, memory_space=HBM)
    out = jax.empty_ref(jax.ShapeDtypeStruct((8, n, w), F32), memory_space=HBM)

    @_sequencer(name, collective_id, [pltpu.SemaphoreType.DMA((7,)), pltpu.SemaphoreType.DMA((7,)), pltpu.SemaphoreType.DMA((1,))],
                7 * buf.size * 4)
    def launch(ssem, rsem, lsem):
        x, y, c = _place()
        me = 4 * x + 2 * y + c
        peers = [(1 - x if (k >> 2) & 1 else x, 1 - y if (k >> 1) & 1 else y, 1 - c if k & 1 else c) for k in range(1, 8)]
        _handshake(peers)
        mine = pltpu.make_async_copy(src, out.at[me], lsem.at[0])
        mine.start()
        sends = []
        for k, peer in enumerate(peers):
            cp = _remote(src, out.at[me], ssem.at[k], rsem.at[k], peer)
            cp.start()
            sends.append(cp)
        for k, (px, py, pc) in enumerate(peers):
            slot = out.at[4 * px + 2 * py + pc]
            _remote(slot, slot, ssem.at[k], rsem.at[k], (px, py, pc)).wait_recv()
        for cp in sends:
            cp.wait_send()
        mine.wait()

    launch()
    return out[...]


def _sum_slots(slots, name, after=None):
    _, n, w = slots.shape

    def body(s_ref, o_ref):
        acc = s_ref[0]
        for d in range(1, 8):
            acc = acc + s_ref[d]
        o_ref[...] = acc

    vm = pl.BlockSpec(memory_space=pltpu.VMEM)
    in_specs, args = [vm], [slots]
    if after is not None:
        body = _skip_ref(body, 1)
        args.append(_deps(after))
        in_specs.append(vm)
    return pl.pallas_call(body, in_specs=in_specs, out_specs=vm, out_shape=jax.ShapeDtypeStruct((n, w), F32), name=name)(*args)


def _pair_swap(parts, name, collective_id):
    n = len(parts)
    half = [p.shape[1] // 2 for p in parts]
    src = [jax.new_ref(p, memory_space=HBM) for p in parts]
    got = [jax.empty_ref(jax.ShapeDtypeStruct((p.shape[0], h, p.shape[2]), p.dtype), memory_space=HBM) for p, h in zip(parts, half)]

    @_sequencer(name, collective_id, [pltpu.SemaphoreType.DMA((n,))] * 2, _nbytes(parts) // 2)
    def launch(ssem, rsem):
        x, y, c = _place()
        _handshake([(x, y, 1 - c)])
        copies = []
        for s in range(n):
            cp = _remote(src[s].at[pl.ds(0, parts[s].shape[0]), pl.ds((1 - c) * half[s], half[s])], got[s], ssem.at[s], rsem.at[s], (x, y, 1 - c))
            cp.start()
            copies.append(cp)
        for cp in copies:
            cp.wait_recv()
        for cp in copies:
            cp.wait_send()

    launch()
    return [g[...] for g in got]


def _to_owners(sums, name, collective_id):
    n = len(sums)
    src = [jax.new_ref(s, memory_space=HBM) for s in sums]
    got = [jax.empty_ref(jax.ShapeDtypeStruct(s.shape, s.dtype), memory_space=HBM) for s in sums]

    @_sequencer(name, collective_id, [pltpu.SemaphoreType.DMA((n, 3))] * 2, _nbytes(sums) * 3 // 4)
    def launch(ssem, rsem):
        x, y, c = _place()
        q = 2 * x + y
        chips = _other_chips(x, y)
        _handshake([(px, py, c) for px, py in chips])
        sends = []
        for j, (px, py) in enumerate(chips):
            for s in range(n):
                cp = _remote(src[s].at[2 * px + py], got[s].at[q], ssem.at[s, j], rsem.at[s, j], (px, py, c))
                cp.start()
                sends.append(cp)
        for j, (px, py) in enumerate(chips):
            for s in range(n):
                slot = got[s].at[2 * px + py]
                _remote(slot, slot, ssem.at[s, j], rsem.at[s, j], (px, py, c)).wait_recv()
        for cp in sends:
            cp.wait_send()

    launch()
    return [g[...] for g in got]


def _pair_join(blocks, name, collective_id):
    n = len(blocks)
    out = [jax.new_ref(b, memory_space=HBM) for b in blocks]

    @_sequencer(name, collective_id, [pltpu.SemaphoreType.DMA((n,))] * 2, _nbytes(blocks) // 2)
    def launch(ssem, rsem):
        x, y, c = _place()
        _handshake([(x, y, 1 - c)])
        sends = []
        for s in range(n):
            h = blocks[s].shape[0] // 2
            mine = out[s].at[pl.ds(c * h, h)]
            cp = _remote(mine, mine, ssem.at[s], rsem.at[s], (x, y, 1 - c))
            cp.start()
            sends.append(cp)
        for s in range(n):
            h = blocks[s].shape[0] // 2
            theirs = out[s].at[pl.ds((1 - c) * h, h)]
            _remote(theirs, theirs, ssem.at[s], rsem.at[s], (x, y, 1 - c)).wait_recv()
        for cp in sends:
            cp.wait_send()

    launch()
    return [o[...] for o in out]


WIRE = BF16


def _row_tile(h):
    return _pick(h, (256, 368, 352, 128, 16))


def _add_pair(part, got, c, name, after=None):
    _, h, w = got.shape
    tr = _row_tile(h)
    nt = h // tr

    def body(c_ref, p_ref, g_ref, o_ref):
        o_ref[...] = (p_ref[...] + g_ref[...].astype(F32)).astype(o_ref.dtype)

    in_specs = [pl.BlockSpec((None, tr, w), lambda q, i, c_ref: (q, c_ref[0] * nt + i, 0)),
                pl.BlockSpec((None, tr, w), lambda q, i, c_ref: (q, i, 0))]
    args = [c.reshape(1).astype(jnp.int32), part, got]
    if after is not None:
        body = _skip_ref(body, len(args))
        args.append(_deps(after))
        in_specs.append(_dep_spec(args[-1]))
    return pl.pallas_call(
        body,
        grid_spec=pltpu.PrefetchScalarGridSpec(
            num_scalar_prefetch=1, grid=(got.shape[0], nt), in_specs=in_specs,
            out_specs=pl.BlockSpec((None, tr, w), lambda q, i, c_ref: (q, i, 0))),
        out_shape=jax.ShapeDtypeStruct(got.shape, WIRE), name=name,
        compiler_params=_cparams(("arbitrary", "arbitrary")),
    )(*args)


def _sum_chips(slots, sums, q, c, name, after=None):
    _, h, w = slots.shape
    tr = _row_tile(h)
    nt = h // tr

    def body(s_ref, mine_ref, a_ref, b_ref, d_ref, o_ref):
        o_ref[...] = ((mine_ref[...].astype(F32) + a_ref[...].astype(F32)) + b_ref[...].astype(F32)) + d_ref[...].astype(F32)

    slot = lambda k: pl.BlockSpec((None, tr, w), lambda i, s_ref: (s_ref[1 + k], i, 0))
    scalars = jnp.stack([c, q, (q + 1) % N_CHIPS, (q + 2) % N_CHIPS, (q + 3) % N_CHIPS]).astype(jnp.int32)
    in_specs, args = [slot(0), slot(1), slot(2), slot(3)], [scalars, sums, slots, slots, slots]
    if after is not None:
        body = _skip_ref(body, len(args))
        args.append(_deps(after))
        in_specs.append(_dep_spec(args[-1]))
    return pl.pallas_call(
        body,
        grid_spec=pltpu.PrefetchScalarGridSpec(
            num_scalar_prefetch=1, grid=(nt,), in_specs=in_specs,
            out_specs=pl.BlockSpec((tr, w), lambda i, s_ref: (s_ref[0] * nt + i, 0))),
        out_shape=jax.ShapeDtypeStruct((2 * h, w), F32), name=name,
        compiler_params=_cparams(("arbitrary",)),
    )(*args)


class _Reduce:
    def __init__(self, parts, q, c, tag, first_id, regions=None):
        self.parts, self.q, self.c, self.tag, self.first_id, self.regions = parts, q, c, tag, first_id, regions
        self.got = _pair_swap(parts, f"{tag}_pair_swap", first_id)

    def to_owners(self, after=None):
        self.sums = [_add_pair(p, g, self.c, f"{self.tag}_pair_add{i}", after)
                     for i, (p, g) in enumerate(zip(self.parts, self.got))]
        if self.regions is not None:
            self.sums = self.regions(self.sums)
        self.slots = _to_owners(self.sums, f"{self.tag}_to_owners", self.first_id + 1)
        return self.sums

    def join(self, after=None):
        blocks = [_sum_chips(sl, sm, self.q, self.c, f"{self.tag}_sum_chips{i}", after)
                  for i, (sl, sm) in enumerate(zip(self.slots, self.sums))]
        self.out = _pair_join(blocks, f"{self.tag}_pair_join", self.first_id + 2)
        return blocks


WEIGHTS = ("meta_tokens", "ffn1_norm", "ffn1_w_gu", "ffn1_w_down", "mix_norm", "w_in", "ssd_conv_w", "ssd_conv_b",
           "ssd_dt_bias", "ssd_a_log", "ssd_d", "ssd_norm", "hg_lower_bound", "hg_norm", "w_branch_a", "w_branch_b",
           "w_out", "ffn2_norm", "ffn2_w_gu", "ffn2_w_down", "final_norm")
BIG = ("ffn1_w_gu", "ffn1_w_down", "w_in", "w_branch_a", "w_branch_b", "w_out", "ffn2_w_gu", "ffn2_w_down")
SMALL = tuple(n for n in WEIGHTS if n not in BIG)


def _rows1024(a):
    flat = a.reshape(-1)
    n = -(-flat.shape[0] // 1024) * 1024
    return jnp.pad(flat, (0, n - flat.shape[0])).reshape(-1, 1024)


def kernel(x, meta_tokens, ffn1_norm, ffn1_w_gu, ffn1_w_down, mix_norm, w_in, ssd_conv_w, ssd_conv_b, ssd_dt_bias, ssd_a_log, ssd_d, ssd_norm, hg_lower_bound, hg_norm, w_branch_a, w_branch_b, w_out, ffn2_norm, ffn2_w_gu, ffn2_w_down, final_norm, loss_target, m_meta_tokens, m_ffn1_norm, m_ffn1_w_gu, m_ffn1_w_down, m_mix_norm, m_w_in, m_ssd_conv_w, m_ssd_conv_b, m_ssd_dt_bias, m_ssd_a_log, m_ssd_d, m_ssd_norm, m_hg_lower_bound, m_hg_norm, m_w_branch_a, m_w_branch_b, m_w_out, m_ffn2_norm, m_ffn2_w_gu, m_ffn2_w_down, m_final_norm, v_meta_tokens, v_ffn1_norm, v_ffn1_w_gu, v_ffn1_w_down, v_mix_norm, v_w_in, v_ssd_conv_w, v_ssd_conv_b, v_ssd_dt_bias, v_ssd_a_log, v_ssd_d, v_ssd_norm, v_hg_lower_bound, v_hg_norm, v_w_branch_a, v_w_branch_b, v_w_out, v_ffn2_norm, v_ffn2_w_gu, v_ffn2_w_down, v_final_norm):
    P = dict(zip(WEIGHTS, (meta_tokens, ffn1_norm, ffn1_w_gu, ffn1_w_down, mix_norm, w_in, ssd_conv_w, ssd_conv_b, ssd_dt_bias, ssd_a_log, ssd_d, ssd_norm, hg_lower_bound, hg_norm, w_branch_a, w_branch_b, w_out, ffn2_norm, ffn2_w_gu, ffn2_w_down, final_norm)))
    M = dict(zip(WEIGHTS, (m_meta_tokens, m_ffn1_norm, m_ffn1_w_gu, m_ffn1_w_down, m_mix_norm, m_w_in, m_ssd_conv_w, m_ssd_conv_b, m_ssd_dt_bias, m_ssd_a_log, m_ssd_d, m_ssd_norm, m_hg_lower_bound, m_hg_norm, m_w_branch_a, m_w_branch_b, m_w_out, m_ffn2_norm, m_ffn2_w_gu, m_ffn2_w_down, m_final_norm)))
    V = dict(zip(WEIGHTS, (v_meta_tokens, v_ffn1_norm, v_ffn1_w_gu, v_ffn1_w_down, v_mix_norm, v_w_in, v_ssd_conv_w, v_ssd_conv_b, v_ssd_dt_bias, v_ssd_a_log, v_ssd_d, v_ssd_norm, v_hg_lower_bound, v_hg_norm, v_w_branch_a, v_w_branch_b, v_w_out, v_ffn2_norm, v_ffn2_w_gu, v_ffn2_w_down, v_final_norm)))
    cx, cy, cc = _place()
    q = 2 * cx + cy

    mine = jnp.concatenate([meta_tokens.reshape(4, 1024), ssd_conv_w.reshape(2, 1024), jnp.zeros((2, 1024), F32)], axis=0)
    every = _exchange8(mine, "gather_small")
    meta_full = jnp.concatenate([every[2 * k, 0:4].reshape(N_META, 256) for k in range(N_CHIPS)], axis=1)
    conv_w_full = jnp.concatenate([every[2 * k, 4:6].reshape(SSD_CONV, 512) for k in range(N_CHIPS)], axis=1)

    late = ("ffn2_w_down", "w_branch_a", "w_branch_b", "w_out")
    rows = jnp.concatenate([P[n][0] for n in late], axis=0)
    zero = lambda t, dtype=F32: (t[0:1, 0:1] * 0).astype(dtype)

    def in_slot(s, after=None):
        s = s if after is None else s + zero(after)
        return lax.dynamic_update_slice(lax.empty((N_CHIPS,) + s.shape, BF16), s.astype(BF16)[None], (q, 0, 0))

    gu1, down1 = _gather_seq([in_slot(ffn1_w_gu[0]), in_slot(ffn1_w_down[0])], "gather_ffn1", 1)
    W = {n: P[n] for n in SMALL}
    W["meta_tokens"], W["ssd_conv_w"] = meta_full, conv_w_full
    W["ffn1_w_gu"], W["ffn1_w_down"] = gu1, down1.reshape(-1, D_MODEL)
    flying = {}

    def stage(name, t):
        if name == "ffn1_norm":
            flying["w_in"] = _gather_seq([in_slot(w_in[0], t)], "gather_w_in", 2)
            return {}
        if name == "ffn1_out":
            flying["late"] = _gather_seq([in_slot(ffn2_w_gu[0], t), in_slot(rows, t)], "gather_late", 3)
            (w_in_all,) = flying["w_in"]
            w_in_all = w_in_all + zero(t, BF16)
            return {"w_in": _split_w_in(w_in_all.transpose(1, 0, 2).reshape(D_MODEL, -1))}
        if name == "mixers_out":
            gu2, rows_all = flying["late"]
            out, r = {"ffn2_w_gu": gu2}, 0
            for n in late:
                nr = P[n].shape[1]
                out[n] = (rows_all[:, r:r + nr] + zero(t, BF16)).reshape(N_CHIPS * nr, D_MODEL)
                r += nr
            return out
        if name == "late_grads":
            parts = [t["ffn2_w_gu"]] + [t[n].reshape(N_CHIPS, -1, D_MODEL) for n in late]
            flying["grad_late"] = _Reduce(parts, q, cc, "grad_late", 4)
            return {"_after": [t["ffn2_w_gu"]] + [t[n] for n in late]}
        if name == "after_conv_bwd":
            return {"_after": flying["grad_late"].to_owners(after=t)}
        if name == "w_in_grads":
            order = ("z", "xbc", "dt", "qfig", "gates")
            blocks = flying["grad_late"].join(after=[t[k] for k in order])

            def regions(sums):
                z, xbc, dt, qfig, gates = [s[0] for s in sums]
                h = z.shape[0]
                qfig = qfig.reshape(h, HG_HEADS, 4, 128).transpose(0, 2, 1, 3).reshape(h, 4 * D_MODEL)
                cols = jnp.concatenate([z, xbc, dt[:, :SSD_HEADS], qfig, gates], axis=1)
                return [cols.reshape(h, N_CHIPS, -1).transpose(1, 0, 2)]

            flying["grad_w_in"] = _Reduce([t[k][None] for k in order], q, cc, "grad_w_in", 7, regions)
            return {"_after": blocks}
        if name == "ffn1_dw_down":
            return {"_after": flying["grad_w_in"].to_owners(after=t)}
        return {}

    W["_stage"] = stage

    loss8, grad_x, G = _local_step(x, loss_target, W)

    small = jnp.concatenate(
        [G["meta_tokens"]] + [_rows1024(G[n]) for n in SMALL if n != "meta_tokens"] + [_rows1024(loss8[0:1, 0:1])], axis=0)
    small = jnp.pad(small, ((0, 40 - small.shape[0]), (0, 0)))
    small_slots = _share8(small, "share_small", 13)

    grad_ffn1 = _Reduce([G["ffn1_w_gu"], G["ffn1_w_down"].reshape(N_CHIPS, -1, D_MODEL)], q, cc, "grad_ffn1", 10)
    flying["grad_w_in"].join(after=grad_x)
    going = grad_ffn1.to_owners(after=grad_x)
    (g_w_in,) = flying["grad_w_in"].out
    Gb = dict(zip(("ffn2_w_gu",) + late, flying["grad_late"].out))
    Gb["w_in"] = g_w_in

    grads, delta, new_m, new_v, done = {}, {}, {}, {}, []
    cols = w_in.shape[2]
    to_tiles = lambda a: a.transpose(2, 0, 1).reshape(cols, 8, 128).reshape(cols * 8, 128)
    from_tiles = lambda a: a.reshape(cols, 1, D_MODEL).transpose(1, 2, 0)
    for n in [n for n in BIG if n in Gb]:
        if n == "w_in":
            g_t = to_tiles(Gb[n][None])
            d_, m_, v_ = _adamw(to_tiles(P[n]), g_t, to_tiles(M[n]), to_tiles(V[n]), f"adamw_{n}", after=going)
            grads[n], delta[n], new_m[n], new_v[n] = from_tiles(g_t), from_tiles(d_), from_tiles(m_), from_tiles(v_)
        else:
            d_, m_, v_ = _adamw(P[n][0], Gb[n], M[n][0], V[n][0], f"adamw_{n}", after=going)
            grads[n], delta[n], new_m[n], new_v[n] = Gb[n][None], d_[None], m_[None], v_[None]
        done.append(d_)

    small = _sum_slots(small_slots, "sum_small", after=done)
    Gs = {"meta_tokens": small[0:N_META]}
    r = N_META
    for n in SMALL:
        if n == "meta_tokens":
            continue
        nr = -(-G[n].size // 1024)
        Gs[n] = small[r:r + nr].reshape(-1)[:G[n].size].reshape(G[n].shape)
        r += nr
    loss = small[r, 0]
    Gs["meta_tokens"] = lax.dynamic_slice(Gs["meta_tokens"], (0, 256 * q), (N_META, 256))
    Gs["ssd_conv_w"] = lax.dynamic_slice(Gs["ssd_conv_w"], (0, 512 * q), (SSD_CONV, 512))[None]
    Gs = {n: Gs[n].reshape(P[n].shape) for n in SMALL}
    grads.update(Gs)
    flat = lambda a: a.reshape(-1, a.shape[-1])
    d_s, m_s, v_s = _adamw_many(*[[flat(D[n]) for n in SMALL] for D in (P, Gs, M, V)], "adamw_small")
    for i, n in enumerate(SMALL):
        delta[n], new_m[n], new_v[n] = d_s[i].reshape(P[n].shape), m_s[i].reshape(P[n].shape), v_s[i].reshape(P[n].shape)
    done.append(d_s[0])
    grad_ffn1.join(after=done)
    Gb["ffn1_w_gu"], Gb["ffn1_w_down"] = grad_ffn1.out
    for n in ("ffn1_w_gu", "ffn1_w_down"):
        d_, m_, v_ = _adamw(P[n][0], Gb[n], M[n][0], V[n][0], f"adamw_{n}")
        grads[n], delta[n], new_m[n], new_v[n] = Gb[n][None], d_[None], m_[None], v_[None]
    return (loss, grad_x, *[grads[n] for n in WEIGHTS], *[delta[n] for n in WEIGHTS],
            *[new_m[n] for n in WEIGHTS], *[new_v[n] for n in WEIGHTS])
```

```python
import functools

import jax
import jax.numpy as jnp
from jax import lax
from jax.experimental import pallas as pl
from jax.experimental.pallas import tpu as pltpu
from jax.experimental.pallas import tpu_sc as plsc

F32 = jnp.float32
BF16 = jnp.bfloat16
HIGHEST = lax.Precision.HIGHEST
MESH = pl.DeviceIdType.MESH

D_MODEL = 1024
N_META = 16
EPS = 1e-6
SSD_HEADS = 16
SSD_HEAD_DIM = 64
SSD_INNER = 1024
SSD_GROUPS = 4
SSD_STATE = 128
SSD_CONV = 4
SSD_CONV_CH = 2048
HG_HEADS = 8
HG_SUB = 32
CHUNK = 128
D_FF = 2816
N_CHIPS = 4
IN_SIZES = (1024, 2048, 16, 1024, 1024, 1024, 1024, 1024, 1024)
ADAM_LR = 0.001
ADAM_B1 = 0.9
ADAM_B2 = 0.999
ADAM_EPS = 1e-08
ADAM_WD = 0.01
ADAM_STEP = 10
VMEM_LIMIT = 56 * 1024 * 1024
MATMUL_BLOCK_BYTES = 42 * 1024 * 1024
ADAMW_BLOCK_BYTES = 5 * 512 * 1024


def _cparams(sem=None):
    return pltpu.CompilerParams(dimension_semantics=sem, vmem_limit_bytes=VMEM_LIMIT)


def _pick(n, cands):
    for c in cands:
        if n % c == 0:
            return c
    return n


def _deps(after):
    xs = after if isinstance(after, (list, tuple)) else [after]
    one = lambda x: lax.slice(x, (0,) * x.ndim, (1,) * x.ndim).reshape(1).astype(F32)
    return jnp.concatenate([one(x) for x in xs]).reshape(1, -1)


def _dep_spec(dep):
    return pl.BlockSpec(dep.shape, lambda *_: (0, 0))


def _skip_ref(body, pos):
    return lambda *refs: body(*refs[:pos], *refs[pos + 1:])


def _dg(a, b, ca, cb):
    return lax.dot_general(a.astype(BF16), b.astype(BF16), (((ca,), (cb,)), ((), ())), preferred_element_type=F32)


@jax.custom_vjp
def _mm(a, b):
    return _dg(a, b, 1, 0)


def _mm_fwd(a, b):
    return _dg(a, b, 1, 0), (a, b)


def _mm_bwd(r, g):
    a, b = r
    return _dg(g, b, 1, 1), _dg(a, g, 0, 0)


_mm.defvjp(_mm_fwd, _mm_bwd)


@jax.custom_vjp
def _mm_nt(a, b):
    return _dg(a, b, 1, 1)


def _mm_nt_fwd(a, b):
    return _dg(a, b, 1, 1), (a, b)


def _mm_nt_bwd(r, g):
    a, b = r
    return _dg(g, b, 1, 0), _dg(g, a, 0, 0)


_mm_nt.defvjp(_mm_nt_fwd, _mm_nt_bwd)


@jax.custom_vjp
def _mm_tn(a, b):
    return _dg(a, b, 0, 0)


def _mm_tn_fwd(a, b):
    return _dg(a, b, 0, 0), (a, b)


def _mm_tn_bwd(r, g):
    a, b = r
    return _dg(b, g, 1, 1), _dg(a, g, 1, 0)


_mm_tn.defvjp(_mm_tn_fwd, _mm_tn_bwd)


def _tri_sum(x, lower):
    n = x.shape[0]
    ri = lax.broadcasted_iota(jnp.int32, (n, n), 0)
    ci = lax.broadcasted_iota(jnp.int32, (n, n), 1)
    tri = ((ri >= ci) if lower else (ri <= ci)).astype(BF16)
    x1 = x.astype(BF16)
    r1 = x - x1.astype(F32)
    x2 = r1.astype(BF16)
    x3 = (r1 - x2.astype(F32)).astype(BF16)
    dot = lambda p: lax.dot_general(tri, p, (((1,), (0,)), ((), ())), preferred_element_type=F32)
    return (dot(x3) + dot(x2)) + dot(x1)


@jax.custom_vjp
def _cumsum_rows(x):
    return _tri_sum(x, True)


_cumsum_rows.defvjp(lambda x: (_tri_sum(x, True), None), lambda _, g: (_tri_sum(g, False),))


def _silu(x):
    return x * jax.nn.sigmoid(x)


def _softplus(x):
    return jnp.maximum(x, 0.0) + jnp.log(1.0 + jnp.exp(-jnp.abs(x)))


def _tril(n):
    ri = lax.broadcasted_iota(jnp.int32, (n, n), 0)
    ci = lax.broadcasted_iota(jnp.int32, (n, n), 1)
    return ri >= ci


def _row_of(m, r):
    sub = lax.broadcasted_iota(jnp.int32, (m.shape[0], 1), 0)
    return jnp.sum(jnp.where(sub == r, m, 0.0), axis=0, keepdims=True)


def _col_of(m, c):
    lane = lax.broadcasted_iota(jnp.int32, (1, m.shape[1]), 1)
    return jnp.sum(jnp.where(lane == c, m, 0.0), axis=1, keepdims=True)


def _matmul(a, b, *, mode, out_dtype, name, alpha=1.0, res=None, tm=None, tn=None, out_groups=None, after=None):
    b3 = b.ndim == 3
    if mode == "nn":
        M, K = a.shape
        G = b.shape[0] if b3 else 1
        Ng = b.shape[-1]
        N = G * Ng
    elif mode == "nt":
        M, K = a.shape
        G = b.shape[0] if b3 else 1
        N = b.shape[-2]
        Kg = b.shape[-1]
        assert G * Kg == K
    else:
        K, M = a.shape
        N = b.shape[1]
        G = out_groups or 1
        Ng = N // G
    has_res = res is not None
    split_n = (mode == "nn" and b3) or (mode == "tn" and G > 1)
    per_mn = jnp.dtype(out_dtype).itemsize + (res.dtype.itemsize if has_res else 0)
    fits = [(m_ * n_, m_, n_)
            for m_ in (4352, 2176, 1408, 1088, 1024, 544, 512, 256, 128) if M % m_ == 0
            for n_ in (2816, 2048, 1408, 1024, 512, 256, 128) if (Ng if split_n else N) % n_ == 0
            if 2 * (K * m_ * a.dtype.itemsize + K * n_ * b.dtype.itemsize + m_ * n_ * per_mn) + 4 * m_ * n_ <= MATMUL_BLOCK_BYTES]
    _, tm_fit, tn_fit = max(fits)
    tm, tn = tm or tm_fit, tn or tn_fit
    nm, nn_ = M // tm, N // tn
    assert nm * tm == M and nn_ * tn == N, (name, M, N, K, tm, tn)

    if mode == "nn":
        a_spec = pl.BlockSpec((tm, K), lambda i, j: (i, 0))
        if b3:
            ns = Ng // tn
            b_spec = pl.BlockSpec((None, K, tn), lambda i, j: (j // ns, 0, j % ns))
        else:
            b_spec = pl.BlockSpec((K, tn), lambda i, j: (0, j))
        ca, cb = 1, 0
    elif mode == "nt":
        a_spec = pl.BlockSpec((tm, K), lambda i, j: (i, 0))
        if b3:
            b_spec = pl.BlockSpec((G, tn, Kg), lambda i, j: (0, j, 0))
        else:
            b_spec = pl.BlockSpec((tn, K), lambda i, j: (j, 0))
        ca, cb = 1, 1
    else:
        a_spec = pl.BlockSpec((K, tm), lambda i, j: (0, i))
        b_spec = pl.BlockSpec((K, tn), lambda i, j: (0, j))
        ca, cb = 0, 0
    if mode == "tn" and G > 1:
        ns = Ng // tn
        o_spec = pl.BlockSpec((None, tm, tn), lambda i, j: (j // ns, i, j % ns))
        out_shape = jax.ShapeDtypeStruct((G, M, Ng), out_dtype)
    else:
        o_spec = pl.BlockSpec((tm, tn), lambda i, j: (i, j))
        out_shape = jax.ShapeDtypeStruct((M, N), out_dtype)
    in_specs = [a_spec, b_spec]
    args = [a, b]
    if has_res:
        in_specs.append(pl.BlockSpec((tm, tn), lambda i, j: (i, j)))
        args.append(res)
    if after is not None:
        args.append(_deps(after))
        in_specs.append(_dep_spec(args[-1]))

    def body(*refs):
        a_ref, b_ref, o_ref = refs[0], refs[1], refs[-1]
        if mode == "nt" and b3:
            o = _dg(a_ref[:, 0:Kg], b_ref[0], ca, cb)
            for g in range(1, G):
                o = o + _dg(a_ref[:, g * Kg:(g + 1) * Kg], b_ref[g], ca, cb)
        else:
            o = _dg(a_ref[...], b_ref[...], ca, cb)
        if alpha != 1.0:
            o = o * alpha
        if has_res:
            o = o + refs[2][...]
        o_ref[...] = o.astype(o_ref.dtype)

    return pl.pallas_call(
        body, grid=(nm, nn_), in_specs=in_specs, out_specs=o_spec, out_shape=out_shape, name=name,
        compiler_params=_cparams(("parallel", "parallel")),
    )(*args)


def _sum_nt(xs, ws, name):
    R, N = xs[0].shape[0], ws[0].shape[0]
    n = len(xs)
    per_m = sum(x.shape[1] * x.dtype.itemsize for x in xs)
    per_n = sum(w.shape[1] * w.dtype.itemsize for w in ws)
    fits = [(m_ * n_, m_, n_) for m_ in (1088, 544, 256, 128) if R % m_ == 0 for n_ in (1024, 512, 256, 128) if N % n_ == 0
            if 2 * (m_ * per_m + n_ * per_n + m_ * n_ * 4) + 4 * m_ * n_ <= MATMUL_BLOCK_BYTES]
    _, tm, tn = max(fits)

    def body(*refs):
        o = _dg(refs[0][...], refs[n][...], 1, 1)
        for p in range(1, n):
            o = o + _dg(refs[p][...], refs[n + p][...], 1, 1)
        refs[-1][...] = o

    return pl.pallas_call(
        body, grid=(R // tm, N // tn),
        in_specs=[pl.BlockSpec((tm, x.shape[1]), lambda i, j: (i, 0)) for x in xs]
        + [pl.BlockSpec((tn, w.shape[1]), lambda i, j: (j, 0)) for w in ws],
        out_specs=pl.BlockSpec((tm, tn), lambda i, j: (i, j)), out_shape=jax.ShapeDtypeStruct((R, N), F32), name=name,
        compiler_params=_cparams(("parallel", "parallel")),
    )(*xs, *ws)


def _rms_fn(h, w):
    r = lax.rsqrt(jnp.mean(h * h, axis=-1, keepdims=True) + EPS)
    return h * r * w


def _swiglu_fn(gu):
    g = gu[:, :D_FF].astype(F32)
    u = gu[:, D_FF:].astype(F32)
    return _silu(g) * u


def _merge_fn(pa, pb, gates):
    return jax.nn.sigmoid(gates[:, :D_MODEL]) * pa + jax.nn.sigmoid(gates[:, D_MODEL:]) * pb


def _rows_call(body, *, rows, tr, ins, outs, accs=(), name, after=None):
    n = rows // tr
    assert n * tr == rows
    if after is not None:
        body = _skip_ref(body, len(ins))
        ins = list(ins) + [("full", _deps(after))]

    def spec(x):
        if isinstance(x, tuple):
            shp = x[1].shape
            return pl.BlockSpec(shp, lambda i: (0,) * len(shp))
        return pl.BlockSpec((tr, x.shape[1]), lambda i: (i, 0))

    in_specs = [spec(x) for x in ins]
    args = [x[1] if isinstance(x, tuple) else x for x in ins]
    out_specs = [spec(x) for x in outs] + [pl.BlockSpec(x.shape, lambda i: (0,) * len(x.shape)) for x in accs]
    out_shape = [x[1] if isinstance(x, tuple) else x for x in outs] + list(accs)
    return pl.pallas_call(
        body, grid=(n,), in_specs=in_specs, out_specs=out_specs, out_shape=out_shape, name=name,
        compiler_params=_cparams(("arbitrary",)),
    )(*args)


def _acc_rows(ref, val):
    @pl.when(pl.program_id(0) == 0)
    def _():
        ref[...] = jnp.zeros_like(ref)

    ref[0:1, :] += val


def _rms_fwd(h, w, name):
    def body(h_ref, w_ref, o_ref):
        o_ref[...] = _rms_fn(h_ref[...], w_ref[...]).astype(o_ref.dtype)

    R = h.shape[0]
    return _rows_call(body, rows=R, tr=_pick(R, (256, 128)), ins=[h, ("full", w)],
                      outs=[jax.ShapeDtypeStruct(h.shape, BF16)], name=name)[0]


def _rms_bwd(h, w, dn, dres, name, after=None):
    def body(h_ref, w_ref, dn_ref, dres_ref, dh_ref, dw_ref):
        _, vjp = jax.vjp(_rms_fn, h_ref[...], w_ref[...])
        dh, dw = vjp(dn_ref[...].astype(F32))
        dh_ref[...] = dh + dres_ref[...]
        _acc_rows(dw_ref, dw)

    R = h.shape[0]
    return _rows_call(body, rows=R, tr=_pick(R, (256, 128)), ins=[h, ("full", w), dn, dres],
                      outs=[jax.ShapeDtypeStruct(h.shape, F32)], accs=[jax.ShapeDtypeStruct((8, D_MODEL), F32)], name=name,
                      after=after)


def _d_norm_in(dgu, w_gu, h, norm_w, dres, name, after=None):
    R = h.shape[0]
    G, _, kg = w_gu.shape

    def body(dgu_ref, w_ref, h_ref, nw_ref, dres_ref, dh_ref, dw_ref):
        dn = _dg(dgu_ref[:, 0:kg], w_ref[0], 1, 1)
        for g in range(1, G):
            dn = dn + _dg(dgu_ref[:, kg * g:kg * (g + 1)], w_ref[g], 1, 1)
        _, vjp = jax.vjp(_rms_fn, h_ref[...], nw_ref[...])
        dh, dw = vjp(dn)
        dh_ref[...] = dh + dres_ref[...]
        _acc_rows(dw_ref, dw)

    return _rows_call(body, rows=R, tr=_pick(R, (256, 128)), ins=[dgu, ("full", w_gu), h, ("full", norm_w), dres],
                      outs=[jax.ShapeDtypeStruct(h.shape, F32)], accs=[jax.ShapeDtypeStruct((8, D_MODEL), F32)], name=name,
                      after=after)


def _rms_bwd_tokens(h, w, dn, dres, nseq, name, after=None):
    Tp = h.shape[0] // nseq
    nc = Tp // CHUNK

    def body(h_ref, w_ref, dn_ref, dres_ref, dx_ref, dm_ref, dw_ref):
        b, c = pl.program_id(0), pl.program_id(1)
        _, vjp = jax.vjp(_rms_fn, h_ref[...], w_ref[...])
        dh, dw = vjp(dn_ref[...].astype(F32))
        dh = dh + dres_ref[...]

        @pl.when(c == 0)
        def _():
            dm_ref[...] = dh

        @pl.when(c > 0)
        def _():
            dx_ref[...] = dh

        @pl.when((b == 0) & (c == 0))
        def _():
            dw_ref[...] = jnp.zeros_like(dw_ref)

        dw_ref[0:1, :] += dw

    rows = pl.BlockSpec((CHUNK, D_MODEL), lambda b, c: (b * nc + c, 0))
    in_specs, args = [rows, pl.BlockSpec((1, D_MODEL), lambda b, c: (0, 0)), rows, rows], [h, w, dn, dres]
    if after is not None:
        body = _skip_ref(body, len(args))
        args.append(_deps(after))
        in_specs.append(_dep_spec(args[-1]))
    return pl.pallas_call(
        body, grid=(nseq, nc), in_specs=in_specs,
        out_specs=[pl.BlockSpec((None, CHUNK, D_MODEL), lambda b, c: (b, jnp.maximum(c - 1, 0), 0)),
                   pl.BlockSpec((None, CHUNK, D_MODEL), lambda b, c: (b, 0, 0)),
                   pl.BlockSpec((8, D_MODEL), lambda b, c: (0, 0))],
        out_shape=[jax.ShapeDtypeStruct((nseq, Tp - CHUNK, D_MODEL), F32), jax.ShapeDtypeStruct((nseq, CHUNK, D_MODEL), F32),
                   jax.ShapeDtypeStruct((8, D_MODEL), F32)],
        name=name, compiler_params=_cparams(("arbitrary", "arbitrary")),
    )(*args)


def _gu_swiglu(n, w_gu, name):
    R = n.shape[0]
    G, _, ng = w_gu.shape

    def body(n_ref, w_ref, gu_ref, a_ref):
        x = n_ref[...]
        for r in range(G):
            gu_ref[:, ng * r:ng * (r + 1)] = _dg(x, w_ref[r], 1, 0).astype(gu_ref.dtype)
        a_ref[...] = _swiglu_fn(gu_ref[...]).astype(a_ref.dtype)

    return _rows_call(body, rows=R, tr=_pick(R, (256, 128)), ins=[n, ("full", w_gu)],
                      outs=[jax.ShapeDtypeStruct((R, 2 * D_FF), BF16), jax.ShapeDtypeStruct((R, D_FF), BF16)], name=name)


def _d_swiglu(dout, w_down, gu, alpha, name):
    R = gu.shape[0]

    def body(do_ref, w_ref, gu_ref, o_ref):
        da = _dg(do_ref[...] * alpha, w_ref[...], 1, 1)
        g = gu_ref[:, :D_FF].astype(F32)
        u = gu_ref[:, D_FF:].astype(F32)
        s = jax.nn.sigmoid(g)
        t = g * s
        o_ref[:, :D_FF] = (da * u * (s + t - t * s)).astype(o_ref.dtype)
        o_ref[:, D_FF:] = (da * t).astype(o_ref.dtype)

    return _rows_call(body, rows=R, tr=_pick(R, (256, 128)), ins=[dout, ("full", w_down), gu],
                      outs=[jax.ShapeDtypeStruct(gu.shape, BF16)], name=name)[0]


def _residual_matmul(a, w, res, alpha, name, norm_w=None):
    R, K = a.shape

    def body(a_ref, w_ref, r_ref, *rest):
        out = r_ref[...] + alpha * _dg(a_ref[...], w_ref[...], 1, 0)
        if norm_w is None:
            rest[0][...] = out
        else:
            rest[1][...] = out
            rest[2][...] = _rms_fn(out, rest[0][...]).astype(rest[2].dtype)

    f32 = jax.ShapeDtypeStruct((R, D_MODEL), F32)
    ins = [a, ("full", w), res] + ([] if norm_w is None else [("full", norm_w)])
    outs = [f32] + ([] if norm_w is None else [jax.ShapeDtypeStruct((R, D_MODEL), BF16)])
    got = _rows_call(body, rows=R, tr=_pick(R, (544, 256, 128)), ins=ins, outs=outs, name=name)
    return got[0] if norm_w is None else (got[0], got[1])


def _branch_merge(ya, yb, wa, wb, gates, name):
    def body(ya_ref, yb_ref, wa_ref, wb_ref, g_ref, pa_ref, pb_ref, o_ref):
        pa = _dg(ya_ref[...], wa_ref[...], 1, 0)
        pb = _dg(yb_ref[...], wb_ref[...], 1, 0)
        pa_ref[...] = pa
        pb_ref[...] = pb
        o_ref[...] = _merge_fn(pa, pb, g_ref[...].astype(F32)).astype(o_ref.dtype)

    R = ya.shape[0]
    f32 = jax.ShapeDtypeStruct((R, D_MODEL), F32)
    return _rows_call(body, rows=R, tr=_pick(R, (544, 256, 128)), ins=[ya, yb, ("full", wa), ("full", wb), gates],
                      outs=[f32, f32, jax.ShapeDtypeStruct((R, D_MODEL), BF16)], name=name)


def _branch_merge_bwd(pa, pb, gates, dm, wa, wb, name):
    def body(pa_ref, pb_ref, g_ref, dm_ref, wa_ref, wb_ref, dpa_ref, dpb_ref, dg_ref, dya_ref, dyb_ref):
        _, vjp = jax.vjp(_merge_fn, pa_ref[...], pb_ref[...], g_ref[...].astype(F32))
        dpa, dpb, dg = vjp(dm_ref[...].astype(F32))
        dpa_ref[...] = dpa.astype(dpa_ref.dtype)
        dpb_ref[...] = dpb.astype(dpb_ref.dtype)
        dg_ref[...] = dg.astype(dg_ref.dtype)
        dya_ref[...] = _dg(dpa, wa_ref[...], 1, 1).astype(dya_ref.dtype)
        dyb_ref[...] = _dg(dpb, wb_ref[...], 1, 1).astype(dyb_ref.dtype)

    R = pa.shape[0]
    b16 = jax.ShapeDtypeStruct(pa.shape, BF16)
    return _rows_call(body, rows=R, tr=_pick(R, (544, 256, 128)), ins=[pa, pb, gates, dm, ("full", wa), ("full", wb)],
                      outs=[b16, b16, jax.ShapeDtypeStruct(gates.shape, BF16), b16, b16], name=name)


def _loss_head(h3, w, target, nseq, name):
    Tp = h3.shape[0] // nseq
    nc = Tp // CHUNK

    def fn(h, w_, t, valid):
        y = _rms_fn(h, w_)
        e = (y - t) * valid
        return 0.5 * jnp.sum(jnp.mean(e * e, axis=-1, keepdims=True))

    def body(h_ref, w_ref, t_ref, loss_ref, dh_ref, dw_ref):
        b, c = pl.program_id(0), pl.program_id(1)
        valid = (c >= 1).astype(F32)
        t = t_ref[...]
        loss, vjp = jax.vjp(lambda h, w_: fn(h, w_, t, valid), h_ref[...], w_ref[...])
        dh, dw = vjp(jnp.ones((), F32))
        dh_ref[...] = dh

        @pl.when((b == 0) & (c == 0))
        def _():
            loss_ref[...] = jnp.zeros_like(loss_ref)
            dw_ref[...] = jnp.zeros_like(dw_ref)

        loss_ref[...] += jnp.full(loss_ref.shape, loss, F32)
        dw_ref[0:1, :] += dw

    return pl.pallas_call(
        body, grid=(nseq, nc),
        in_specs=[pl.BlockSpec((CHUNK, D_MODEL), lambda b, c: (b * nc + c, 0)),
                  pl.BlockSpec((1, D_MODEL), lambda b, c: (0, 0)),
                  pl.BlockSpec((None, CHUNK, D_MODEL), lambda b, c: (b, jnp.maximum(c - 1, 0), 0))],
        out_specs=[pl.BlockSpec((8, 128), lambda b, c: (0, 0)),
                   pl.BlockSpec((CHUNK, D_MODEL), lambda b, c: (b * nc + c, 0)),
                   pl.BlockSpec((8, D_MODEL), lambda b, c: (0, 0))],
        out_shape=[jax.ShapeDtypeStruct((8, 128), F32), jax.ShapeDtypeStruct(h3.shape, F32),
                   jax.ShapeDtypeStruct((8, D_MODEL), F32)],
        name=name, compiler_params=_cparams(("arbitrary", "arbitrary")),
    )(h3, w, target)


CONV_TILE = 512
CONV_HALO = 8


def _conv_fwd(xbc, w, b, pad, name):
    B, Tp, C = xbc.shape
    nch = Tp // CHUNK

    def body(x_ref, w_ref, b_ref, o_ref, xp):
        xp[0:CONV_HALO, :] = jnp.zeros((CONV_HALO, CONV_TILE), F32)
        xp[CONV_HALO:, :] = x_ref[...]
        for c in range(nch):
            acc = jnp.zeros((CHUNK, CONV_TILE), F32) + b_ref[...]
            for k in range(SSD_CONV):
                acc = acc + w_ref[k:k + 1, :] * xp[pl.ds(CONV_HALO + CHUNK * c - (SSD_CONV - 1) + k, CHUNK), :]
            out = _silu(acc)
            if CHUNK * c < pad:
                row = CHUNK * c + lax.broadcasted_iota(jnp.int32, (CHUNK, 1), 0)
                out = jnp.where(row >= pad, out, 0.0)
            o_ref[pl.ds(CHUNK * c, CHUNK), :] = out

    return pl.pallas_call(
        body, grid=(B, C // CONV_TILE),
        in_specs=[pl.BlockSpec((None, Tp, CONV_TILE), lambda i, j: (i, 0, j)),
                  pl.BlockSpec((SSD_CONV, CONV_TILE), lambda i, j: (0, j)),
                  pl.BlockSpec((1, CONV_TILE), lambda i, j: (0, j))],
        out_specs=pl.BlockSpec((None, Tp, CONV_TILE), lambda i, j: (i, 0, j)),
        out_shape=jax.ShapeDtypeStruct(xbc.shape, F32),
        scratch_shapes=[pltpu.VMEM((Tp + CONV_HALO, CONV_TILE), F32)],
        name=name, compiler_params=_cparams(("arbitrary", "arbitrary")),
    )(xbc, w, b)


def _conv_bwd(xbc, w, b, dact, pad, name):
    B, Tp, C = xbc.shape
    nch = Tp // CHUNK

    def body(x_ref, w_ref, b_ref, da_ref, dx_ref, dw_ref, db_ref, xp, dp):
        bi = pl.program_id(1)
        xp[0:CONV_HALO, :] = jnp.zeros((CONV_HALO, CONV_TILE), F32)
        xp[CONV_HALO:, :] = x_ref[...]
        dp[pl.ds(Tp, CONV_HALO), :] = jnp.zeros((CONV_HALO, CONV_TILE), F32)
        dws = [jnp.zeros((1, CONV_TILE), F32) for _ in range(SSD_CONV)]
        dbs = jnp.zeros((1, CONV_TILE), F32)
        for c in range(nch):
            xs = [xp[pl.ds(CONV_HALO + CHUNK * c - (SSD_CONV - 1) + k, CHUNK), :] for k in range(SSD_CONV)]
            acc = jnp.zeros((CHUNK, CONV_TILE), F32) + b_ref[...]
            for k in range(SSD_CONV):
                acc = acc + w_ref[k:k + 1, :] * xs[k]
            sg = jax.nn.sigmoid(acc)
            t = acc * sg
            dpre = da_ref[pl.ds(CHUNK * c, CHUNK), :] * (sg + t - t * sg)
            if CHUNK * c < pad:
                row = CHUNK * c + lax.broadcasted_iota(jnp.int32, (CHUNK, 1), 0)
                dpre = jnp.where(row >= pad, dpre, 0.0)
            dp[pl.ds(CHUNK * c, CHUNK), :] = dpre
            dbs = dbs + jnp.sum(dpre, axis=0, keepdims=True)
            for k in range(SSD_CONV):
                dws[k] = dws[k] + jnp.sum(dpre * xs[k], axis=0, keepdims=True)
        for c in range(nch):
            acc = jnp.zeros((CHUNK, CONV_TILE), F32)
            for k in range(SSD_CONV):
                acc = acc + w_ref[k:k + 1, :] * dp[pl.ds(CHUNK * c + (SSD_CONV - 1) - k, CHUNK), :]
            dx_ref[pl.ds(CHUNK * c, CHUNK), :] = acc.astype(dx_ref.dtype)

        @pl.when(bi == 0)
        def _():
            dw_ref[...] = jnp.zeros_like(dw_ref)
            db_ref[...] = jnp.zeros_like(db_ref)

        for k in range(SSD_CONV):
            dw_ref[k:k + 1, :] += dws[k]
        db_ref[0:1, :] += dbs

    return pl.pallas_call(
        body, grid=(C // CONV_TILE, B),
        in_specs=[pl.BlockSpec((None, Tp, CONV_TILE), lambda j, i: (i, 0, j)),
                  pl.BlockSpec((SSD_CONV, CONV_TILE), lambda j, i: (0, j)),
                  pl.BlockSpec((1, CONV_TILE), lambda j, i: (0, j)),
                  pl.BlockSpec((None, Tp, CONV_TILE), lambda j, i: (i, 0, j))],
        out_specs=[pl.BlockSpec((None, Tp, CONV_TILE), lambda j, i: (i, 0, j)),
                   pl.BlockSpec((8, CONV_TILE), lambda j, i: (0, j)),
                   pl.BlockSpec((8, CONV_TILE), lambda j, i: (0, j))],
        out_shape=[jax.ShapeDtypeStruct(xbc.shape, BF16), jax.ShapeDtypeStruct((8, C), F32),
                   jax.ShapeDtypeStruct((8, C), F32)],
        scratch_shapes=[pltpu.VMEM((Tp + CONV_HALO, CONV_TILE), F32), pltpu.VMEM((Tp + CONV_HALO, CONV_TILE), F32)],
        name=name, compiler_params=_cparams(("arbitrary", "arbitrary")),
    )(xbc, w, b, dact)


def _ssd_chunk(xs, bm, cm, dtr, z, state, dt_bias, a_log, dskip, norm_w, valid):
    Q = xs.shape[0]
    lane = lax.broadcasted_iota(jnp.int32, (1, 128), 1)
    dt = jnp.where(lane < SSD_HEADS, _softplus(dtr + dt_bias), 0.0) * valid
    a = dt * (-jnp.exp(a_log))
    tril = _tril(Q)
    cs = _cumsum_rows(a)
    cs_t = cs.T
    cs_end = _row_of(cs, Q - 1)
    low = lane < SSD_HEAD_DIM
    low_rows = lax.broadcasted_iota(jnp.int32, (128, 1), 0) < SSD_HEAD_DIM
    ys, new_state = [], []
    for g in range(SSD_GROUPS):
        bg = bm[:, 128 * g:128 * (g + 1)]
        cg = cm[:, 128 * g:128 * (g + 1)]
        cb = _mm_nt(cg, bg)
        for pr in range(2):
            p = 2 * g + pr
            h0, h1 = 2 * p, 2 * p + 1
            xp = xs[:, 128 * p:128 * (p + 1)]
            c0, c1 = _col_of(cs, h0), _col_of(cs, h1)
            e0, e1 = _col_of(cs_end, h0), _col_of(cs_end, h1)
            xd = xp * jnp.where(low, _col_of(dt, h0), _col_of(dt, h1))
            l0 = jnp.exp(jnp.where(tril, c0 - _row_of(cs_t, h0), -1e30))
            l1 = jnp.exp(jnp.where(tril, c1 - _row_of(cs_t, h1), -1e30))
            y_diag = jnp.where(low, _mm(cb * l0, xd), _mm(cb * l1, xd))
            to_end = jnp.where(low, jnp.exp(e0 - c0), jnp.exp(e1 - c1))
            sp = state[128 * p:128 * (p + 1), :]
            y_off = _mm_nt(cg, sp) * jnp.where(low, jnp.exp(c0), jnp.exp(c1))
            new_state.append(sp * jnp.where(low_rows, jnp.exp(e0), jnp.exp(e1)) + _mm_tn(xd * to_end, bg))
            ys.append(y_diag + y_off + xp * jnp.where(low, _col_of(dskip, h0), _col_of(dskip, h1)))
    y = jnp.concatenate(ys, axis=1) * _silu(z)
    gw = SSD_INNER // SSD_GROUPS
    outs = []
    for g in range(SSD_GROUPS):
        blk = y[:, gw * g:gw * (g + 1)]
        outs.append(blk * lax.rsqrt(jnp.mean(blk * blk, axis=-1, keepdims=True) + EPS))
    return jnp.concatenate(outs, axis=1) * norm_w, jnp.concatenate(new_state, axis=0)


def _valid_rows(c, pad):
    row = c * CHUNK + lax.broadcasted_iota(jnp.int32, (CHUNK, 1), 0)
    return (row >= pad).astype(F32)


def _ssd_fwd(xact, dtr, z, dt_bias, a_log, dskip, norm_w, pad, name):
    B, Tp, _ = xact.shape
    nc = Tp // CHUNK

    def body(xs_ref, bm_ref, cm_ref, dt_ref, z_ref, db_ref, al_ref, ds_ref, nw_ref, y_ref, save_ref, st):
        c = pl.program_id(1)

        @pl.when(c == 0)
        def _():
            st[...] = jnp.zeros_like(st)

        s0 = st[...]
        save_ref[...] = s0
        y, s1 = _ssd_chunk(xs_ref[...], bm_ref[...], cm_ref[...], dt_ref[...], z_ref[...].astype(F32), s0, db_ref[...],
                           al_ref[...], ds_ref[...], nw_ref[...], _valid_rows(c, pad))
        y_ref[...] = y.astype(y_ref.dtype)
        st[...] = s1

    row = lambda w, off=0: pl.BlockSpec((None, CHUNK, w), lambda b, c: (b, c, off))
    par = lambda w: pl.BlockSpec((1, w), lambda b, c: (0, 0))
    return pl.pallas_call(
        body, grid=(B, nc),
        in_specs=[row(1024, 0), row(512, 2), row(512, 3), row(128), row(1024), par(128), par(128), par(128), par(1024)],
        out_specs=[row(1024), pl.BlockSpec((None, None, 1024, 128), lambda b, c: (b, c, 0, 0))],
        out_shape=[jax.ShapeDtypeStruct((B, Tp, SSD_INNER), BF16), jax.ShapeDtypeStruct((B, nc, 1024, 128), F32)],
        scratch_shapes=[pltpu.VMEM((1024, 128), F32)],
        name=name, compiler_params=_cparams(("arbitrary", "arbitrary")),
    )(xact, xact, xact, dtr, z, dt_bias, a_log, dskip, norm_w)


def _ssd_bwd(xact, dtr, z, dt_bias, a_log, dskip, norm_w, saved, dy, pad, name, after=None):
    B, Tp, _ = xact.shape
    nc = Tp // CHUNK

    def body(xs_ref, bm_ref, cm_ref, dt_ref, z_ref, db_ref, al_ref, ds_ref, nw_ref, sv_ref, dy_ref,
             dx_ref, ddt_ref, dz_ref, dpar_ref, dnw_ref, dst):
        b, i = pl.program_id(0), pl.program_id(1)
        c = nc - 1 - i

        @pl.when(i == 0)
        def _():
            dst[...] = jnp.zeros_like(dst)

        valid = _valid_rows(c, pad)
        fn = lambda *a: _ssd_chunk(*a, valid)
        _, vjp = jax.vjp(fn, xs_ref[...], bm_ref[...], cm_ref[...], dt_ref[...], z_ref[...].astype(F32), sv_ref[...],
                         db_ref[...], al_ref[...], ds_ref[...], nw_ref[...])
        dxs, dbm, dcm, ddt, dz, dstate, ddb, dal, dds, dnw = vjp((dy_ref[...].astype(F32), dst[...]))
        dx_ref[:, 0:1024] = dxs
        dx_ref[:, 1024:1536] = dbm
        dx_ref[:, 1536:2048] = dcm
        ddt_ref[...] = ddt
        dz_ref[...] = dz.astype(dz_ref.dtype)
        dst[...] = dstate

        @pl.when((b == 0) & (i == 0))
        def _():
            dpar_ref[...] = jnp.zeros_like(dpar_ref)
            dnw_ref[...] = jnp.zeros_like(dnw_ref)

        dpar_ref[0:1, :] += ddb
        dpar_ref[1:2, :] += dal
        dpar_ref[2:3, :] += dds
        dnw_ref[0:1, :] += dnw

    row = lambda w, off=0: pl.BlockSpec((None, CHUNK, w), lambda b, i: (b, nc - 1 - i, off))
    par = lambda w: pl.BlockSpec((1, w), lambda b, i: (0, 0))
    acc = lambda w: pl.BlockSpec((8, w), lambda b, i: (0, 0))
    in_specs = [row(1024, 0), row(512, 2), row(512, 3), row(128), row(1024), par(128), par(128), par(128), par(1024),
                pl.BlockSpec((None, None, 1024, 128), lambda b, i: (b, nc - 1 - i, 0, 0)), row(1024)]
    args = [xact, xact, xact, dtr, z, dt_bias, a_log, dskip, norm_w, saved, dy]
    if after is not None:
        body = _skip_ref(body, len(args))
        args.append(_deps(after))
        in_specs.append(_dep_spec(args[-1]))
    outs = pl.pallas_call(
        body, grid=(B, nc), in_specs=in_specs,
        out_specs=[row(2048), row(128), row(1024), acc(128), acc(1024)],
        out_shape=[jax.ShapeDtypeStruct((B, Tp, 2048), F32), jax.ShapeDtypeStruct((B, Tp, 128), F32),
                   jax.ShapeDtypeStruct((B, Tp, 1024), BF16), jax.ShapeDtypeStruct((8, 128), F32),
                   jax.ShapeDtypeStruct((8, 1024), F32)],
        scratch_shapes=[pltpu.VMEM((1024, 128), F32)],
        name=name, compiler_params=_cparams(("arbitrary", "arbitrary")),
    )(*args)
    return outs


def _hg_chunk(qr, fr, ir, gr, state_t, p0, p1, norm_w, valid):
    Q = qr.shape[0]
    lb = jax.nn.sigmoid(p0 - p1)
    f = lb + (1.0 - lb) * jax.nn.sigmoid(fr)
    k = 1.0 - f
    q = _silu(qr)
    v = ir * valid
    cum = _cumsum_rows(jnp.log(f))
    cum_end = _row_of(cum, Q - 1)
    o_inter = _mm_nt(q * jnp.exp(cum), state_t)
    nblk = Q // HG_SUB
    row = lax.broadcasted_iota(jnp.int32, (Q, 1), 0)
    ri = lax.broadcasted_iota(jnp.int32, (Q, Q), 0)
    ci = lax.broadcasted_iota(jnp.int32, (Q, Q), 1)
    mids = jnp.concatenate([jnp.broadcast_to(_row_of(cum, HG_SUB * i + HG_SUB // 2 - 1), (HG_SUB, cum.shape[1]))
                            for i in range(nblk)], axis=0)
    sh = HG_SUB.bit_length() - 1
    same = (jnp.right_shift(ri, sh) == jnp.right_shift(ci, sh)) & (ri >= ci)
    att = jnp.where(same, _mm_nt(q * jnp.exp(cum - mids), k * jnp.exp(mids - cum)), 0.0)
    for i in range(1, nblk):
        lo = HG_SUB * i
        start = _row_of(cum, lo - 1)
        qa = q * jnp.exp(jnp.where((row >= lo) & (row < lo + HG_SUB), cum - start, -1e30))
        ka = k * jnp.exp(jnp.where(row < lo, start - cum, -1e30))
        att = att + _mm_nt(qa, ka)
    o = o_inter + _mm(att, v)
    new_state_t = state_t * jnp.exp(cum_end) + _mm_tn(v, k * jnp.exp(cum_end - cum))
    o = o * lax.rsqrt(jnp.mean(o * o, axis=-1, keepdims=True) + EPS) * norm_w
    return o * _silu(gr), new_state_t


HG_PER_STEP = 8
HG_COLS = 4 * 128


def _hg_fwd(qfig, lbh, nwh, pad, name):
    B, Tp, _ = qfig.shape
    nc = Tp // CHUNK
    hp = HG_PER_STEP

    def body(x_ref, lb_ref, nw_ref, y_ref, save_ref, st):
        c = pl.program_id(1)

        @pl.when(c == 0)
        def _():
            st[...] = jnp.zeros_like(st)

        valid = _valid_rows(c, pad)
        for j in range(hp):
            for b in range(B):
                s0 = st[j, b]
                save_ref[j, b] = s0
                col = lambda k: x_ref[b, :, HG_COLS * j + 128 * k:HG_COLS * j + 128 * (k + 1)]
                y, s1 = _hg_chunk(col(0), col(1), col(2), col(3), s0, lb_ref[j, 0:1, :], lb_ref[j, 1:2, :], nw_ref[j], valid)
                y_ref[b, :, 128 * j:128 * (j + 1)] = y.astype(y_ref.dtype)
                st[j, b] = s1

    return pl.pallas_call(
        body, grid=(HG_HEADS // hp, nc),
        in_specs=[pl.BlockSpec((B, CHUNK, HG_COLS * hp), lambda h, c: (0, c, h)),
                  pl.BlockSpec((hp, 2, 128), lambda h, c: (h, 0, 0)),
                  pl.BlockSpec((hp, 1, 128), lambda h, c: (h, 0, 0))],
        out_specs=[pl.BlockSpec((B, CHUNK, 128 * hp), lambda h, c: (0, c, h)),
                   pl.BlockSpec((hp, B, None, 128, 128), lambda h, c: (h, 0, c, 0, 0))],
        out_shape=[jax.ShapeDtypeStruct((B, Tp, 1024), BF16), jax.ShapeDtypeStruct((HG_HEADS, B, nc, 128, 128), F32)],
        scratch_shapes=[pltpu.VMEM((hp, B, 128, 128), F32)],
        name=name, compiler_params=_cparams(("arbitrary", "arbitrary")),
    )(qfig, lbh, nwh)


def _hg_bwd(qfig, lbh, nwh, saved, dy, pad, name, after=None):
    B, Tp, _ = qfig.shape
    nc = Tp // CHUNK
    hp = HG_PER_STEP

    def body(x_ref, lb_ref, nw_ref, sv_ref, dy_ref, dx_ref, dlb_ref, dnw_ref, dst):
        i = pl.program_id(1)
        c = nc - 1 - i

        @pl.when(i == 0)
        def _():
            dst[...] = jnp.zeros_like(dst)
            dlb_ref[...] = jnp.zeros_like(dlb_ref)
            dnw_ref[...] = jnp.zeros_like(dnw_ref)

        valid = _valid_rows(c, pad)
        fn = lambda *a: _hg_chunk(*a, valid)
        for j in range(hp):
            for b in range(B):
                col = lambda k: x_ref[b, :, HG_COLS * j + 128 * k:HG_COLS * j + 128 * (k + 1)]
                _, vjp = jax.vjp(fn, col(0), col(1), col(2), col(3), sv_ref[j, b], lb_ref[j, 0:1, :], lb_ref[j, 1:2, :], nw_ref[j])
                d4 = vjp((dy_ref[b, :, 128 * j:128 * (j + 1)].astype(F32), dst[j, b]))
                for k in range(4):
                    dx_ref[b, :, HG_COLS * j + 128 * k:HG_COLS * j + 128 * (k + 1)] = d4[k].astype(dx_ref.dtype)
                dst[j, b] = d4[4]
                dlb_ref[j, 0:1, :] += d4[5]
                dlb_ref[j, 1:2, :] += d4[6]
                dnw_ref[j, 0:1, :] += d4[7]

    acc = pl.BlockSpec((hp, 8, 128), lambda h, i: (h, 0, 0))
    in_specs = [pl.BlockSpec((B, CHUNK, HG_COLS * hp), lambda h, i: (0, nc - 1 - i, h)),
                pl.BlockSpec((hp, 2, 128), lambda h, i: (h, 0, 0)),
                pl.BlockSpec((hp, 1, 128), lambda h, i: (h, 0, 0)),
                pl.BlockSpec((hp, B, None, 128, 128), lambda h, i: (h, 0, nc - 1 - i, 0, 0)),
                pl.BlockSpec((B, CHUNK, 128 * hp), lambda h, i: (0, nc - 1 - i, h))]
    args = [qfig, lbh, nwh, saved, dy]
    if after is not None:
        body = _skip_ref(body, len(args))
        args.append(_deps(after))
        in_specs.append(_dep_spec(args[-1]))
    return pl.pallas_call(
        body, grid=(HG_HEADS // hp, nc), in_specs=in_specs,
        out_specs=[pl.BlockSpec((B, CHUNK, HG_COLS * hp), lambda h, i: (0, nc - 1 - i, h)), acc, acc],
        out_shape=[jax.ShapeDtypeStruct((B, Tp, 4096), BF16), jax.ShapeDtypeStruct((HG_HEADS, 8, 128), F32),
                   jax.ShapeDtypeStruct((HG_HEADS, 8, 128), F32)],
        scratch_shapes=[pltpu.VMEM((hp, B, 128, 128), F32)],
        name=name, compiler_params=_cparams(("arbitrary", "arbitrary")),
    )(*args)


def _adamw_math(w, g, m, v):
    m = ADAM_B1 * m + (1.0 - ADAM_B1) * g
    v = ADAM_B2 * v + (1.0 - ADAM_B2) * (g * g)
    m_hat = m / (1.0 - ADAM_B1 ** ADAM_STEP)
    v_hat = v / (1.0 - ADAM_B2 ** ADAM_STEP)
    return -ADAM_LR * (m_hat / (jnp.sqrt(v_hat) + ADAM_EPS) + ADAM_WD * w), m, v


def _adamw_many(ws, gs, ms, vs, name):
    n = len(ws)

    def body(*refs):
        for i in range(n):
            d, m, v = _adamw_math(refs[i][...], refs[n + i][...], refs[2 * n + i][...], refs[3 * n + i][...])
            refs[4 * n + i][...] = d
            refs[5 * n + i][...] = m
            refs[6 * n + i][...] = v

    vm = pl.BlockSpec(memory_space=pltpu.VMEM)
    outs = pl.pallas_call(body, in_specs=[vm] * (4 * n), out_specs=[vm] * (3 * n),
                          out_shape=[jax.ShapeDtypeStruct(w.shape, F32) for w in ws] * 3, name=name)(*ws, *gs, *ms, *vs)
    return outs[:n], outs[n:2 * n], outs[2 * n:]


def _adamw(w, g, m, v, name, after=None):
    R, C = w.shape
    tr = max(t for t in range(8, R + 1, 8) if R % t == 0 and (t * C * 4 <= ADAMW_BLOCK_BYTES or t == 8))

    def body(w_ref, g_ref, m_ref, v_ref, d_ref, mo_ref, vo_ref):
        d_ref[...], mo_ref[...], vo_ref[...] = _adamw_math(w_ref[...], g_ref[...], m_ref[...], v_ref[...])

    sp = pl.BlockSpec((tr, C), lambda i: (i, 0))
    sh = jax.ShapeDtypeStruct((R, C), F32)
    in_specs, args = [sp] * 4, [w, g, m, v]
    if after is not None:
        body = _skip_ref(body, len(args))
        args.append(_deps(after))
        in_specs.append(_dep_spec(args[-1]))
    return pl.pallas_call(body, grid=(R // tr,), in_specs=in_specs, out_specs=[sp] * 3, out_shape=[sh] * 3,
                          name=name, compiler_params=_cparams(("arbitrary",)))(*args)


def _ffn_fwd(h, norm_w, w_gu, w_down, tag, after_norm=None, n=None, next_norm_w=None):
    if n is None:
        n = _rms_fwd(h, norm_w, f"{tag}_norm")
    if after_norm is not None:
        after_norm(n)
    gu, a = _gu_swiglu(n, w_gu, f"{tag}_gu")
    out = _residual_matmul(a, w_down, h, 0.5, f"{tag}_down", next_norm_w)
    return out, (n, gu, a)


def _ffn_bwd(h, norm_w, w_gu, w_down, saved, dout, tag, after_dw_down=None, token_seqs=None, told=None):
    n, gu, a = saved
    dgu = _d_swiglu(dout, w_down, gu, 0.5, f"{tag}_d_gu")
    dw_down = _matmul(a, dout, mode="tn", out_dtype=F32, alpha=0.5, name=f"{tag}_dw_down")
    dw_gu = _matmul(n, dgu, mode="tn", out_dtype=F32, out_groups=N_CHIPS, name=f"{tag}_dw_gu",
                    after=after_dw_down(dw_down) if after_dw_down else None)
    if token_seqs is None:
        dh, dnw = _d_norm_in(dgu, w_gu, h, norm_w, dout, f"{tag}_d_in", after=dw_gu)
    else:
        if told is not None:
            told("dw", (dw_gu, dw_down))
        dn = _matmul(dgu, w_gu, mode="nt", out_dtype=F32, name=f"{tag}_d_norm", after=dw_gu)
        dx, dm, dnw = _rms_bwd_tokens(h, norm_w, dn, dout, token_seqs, f"{tag}_d_in",
                                      after=told("d_norm", dn) if told is not None else None)
        dh = (dx, dm)
    return dh, dnw, dw_gu, dw_down


def _split_w_in(w_in_full):
    pts = [0]
    for s in IN_SIZES:
        pts.append(pts[-1] + s)
    sl = lambda i, j: w_in_full[:, pts[i]:pts[j]]
    qfig = sl(3, 7).reshape(D_MODEL, 4, HG_HEADS, 128).transpose(0, 2, 1, 3).reshape(D_MODEL, 4 * D_MODEL)
    return {"z": sl(0, 1), "xbc": sl(1, 2), "dt": jnp.pad(sl(2, 3), ((0, 0), (0, 128 - SSD_HEADS))),
            "qfig": qfig, "gates": sl(7, 9)}


def _local_step(x, target, W):
    B, S, _ = x.shape
    T = N_META + S
    pad = (-T) % CHUNK
    Tp = T + pad
    assert pad + N_META == CHUNK
    R = B * Tp
    meta = jnp.broadcast_to(W["meta_tokens"][None], (B, N_META, D_MODEL))
    h0 = jnp.concatenate([jnp.zeros((B, pad, D_MODEL), F32), meta, x], axis=1).reshape(R, D_MODEL)

    stage = W.get("_stage", lambda name, x: {})
    W = dict(W)
    (h1, um), sv1 = _ffn_fwd(h0, W["ffn1_norm"], W["ffn1_w_gu"], W["ffn1_w_down"], "ffn1",
                             lambda n: W.update(stage("ffn1_norm", n)), next_norm_w=W["mix_norm"])
    W.update(stage("ffn1_out", h1))
    wi = W["w_in"]
    z = _matmul(um, wi["z"], mode="nn", out_dtype=BF16, name="in_z")
    xbc = _matmul(um, wi["xbc"], mode="nn", out_dtype=F32, name="in_xbc")
    dtr = _matmul(um, wi["dt"], mode="nn", out_dtype=F32, name="in_dt")
    qfig = _matmul(um, wi["qfig"], mode="nn", out_dtype=F32, name="in_qfig")
    gates = _matmul(um, wi["gates"], mode="nn", out_dtype=BF16, name="in_gates")

    r3 = lambda t: t.reshape(B, Tp, t.shape[-1])
    lane_pad = lambda t: jnp.pad(t, ((0, 0), (0, 128 - t.shape[1])))
    dt_bias, a_log, dskip = lane_pad(W["ssd_dt_bias"]), lane_pad(W["ssd_a_log"]), lane_pad(W["ssd_d"])
    xact = _conv_fwd(r3(xbc), W["ssd_conv_w"], W["ssd_conv_b"], pad, "conv_fwd")
    ya, ssd_saved = _ssd_fwd(xact, r3(dtr), r3(z), dt_bias, a_log, dskip, W["ssd_norm"], pad, "ssd_fwd")
    lbh = W["hg_lower_bound"].reshape(2, HG_HEADS, 128).transpose(1, 0, 2)
    nwh = W["hg_norm"].reshape(HG_HEADS, 1, 128)
    yb, hg_saved = _hg_fwd(r3(qfig), lbh, nwh, pad, "hg_fwd")
    ya2, yb2 = ya.reshape(R, -1), yb.reshape(R, -1)
    W.update(stage("mixers_out", yb2))
    pa, pb, mg = _branch_merge(ya2, yb2, W["w_branch_a"], W["w_branch_b"], gates, "branch_merge")
    h2, n2 = _residual_matmul(mg, W["w_out"], h1, 1.0, "mix_out", W["ffn2_norm"])
    h3, sv2 = _ffn_fwd(h2, W["ffn2_norm"], W["ffn2_w_gu"], W["ffn2_w_down"], "ffn2", n=n2)

    loss, dh3, d_final = _loss_head(h3, W["final_norm"].reshape(1, D_MODEL), target, B, "loss_head")

    G = {"final_norm": d_final[0]}
    dh2, dnw, G["ffn2_w_gu"], G["ffn2_w_down"] = _ffn_bwd(h2, W["ffn2_norm"], W["ffn2_w_gu"], W["ffn2_w_down"], sv2, dh3, "ffn2")
    G["ffn2_norm"] = dnw[0:1]
    dmg = _matmul(dh2, W["w_out"], mode="nt", out_dtype=BF16, name="d_merge")
    G["w_out"] = _matmul(mg, dh2, mode="tn", out_dtype=F32, name="dw_out")
    dpa, dpb, dgates, dya, dyb = _branch_merge_bwd(pa, pb, gates, dmg, W["w_branch_a"], W["w_branch_b"], "branch_merge_bwd")
    G["w_branch_a"] = _matmul(ya2, dpa, mode="tn", out_dtype=F32, name="dw_branch_a")
    G["w_branch_b"] = _matmul(yb2, dpb, mode="tn", out_dtype=F32, name="dw_branch_b")

    dxact, ddtr, dz, dpar, dnw = _ssd_bwd(xact, r3(dtr), r3(z), dt_bias, a_log, dskip, W["ssd_norm"], ssd_saved,
                                          r3(dya), pad, "ssd_bwd", after=stage("late_grads", G).get("_after"))
    G["ssd_dt_bias"], G["ssd_a_log"], G["ssd_d"] = dpar[0:1, :SSD_HEADS], dpar[1:2, :SSD_HEADS], dpar[2:3, :SSD_HEADS]
    G["ssd_norm"] = dnw[0:1]
    dxbc, dcw, dcb = _conv_bwd(r3(xbc), W["ssd_conv_w"], W["ssd_conv_b"], dxact, pad, "conv_bwd")
    G["ssd_conv_w"], G["ssd_conv_b"] = dcw[0:SSD_CONV], dcb[0:1]
    dqfig, dlb, dhn = _hg_bwd(r3(qfig), lbh, nwh, hg_saved, r3(dyb), pad, "hg_bwd",
                              after=stage("after_conv_bwd", dcb).get("_after"))
    G["hg_lower_bound"] = dlb[:, 0:2, :].transpose(1, 0, 2).reshape(2, D_MODEL)
    G["hg_norm"] = dhn[:, 0, :].reshape(1, D_MODEL)

    r2 = lambda t: t.reshape(R, t.shape[-1])
    pieces = [("z", r2(dz)), ("xbc", r2(dxbc)), ("dt", r2(ddtr)), ("qfig", r2(dqfig)), ("gates", dgates)]
    dum = _sum_nt([p for _, p in pieces], [wi[nm] for nm, _ in pieces], "d_mix")
    dwi = {nm: _matmul(um, dpiece, mode="tn", out_dtype=F32, name=f"dw_in_{nm}") for nm, dpiece in pieces}
    dw_qfig = dwi["qfig"].reshape(D_MODEL, HG_HEADS, 4, 128).transpose(0, 2, 1, 3).reshape(D_MODEL, 4 * D_MODEL)
    G["w_in"] = jnp.concatenate([dwi["z"], dwi["xbc"], dwi["dt"][:, :SSD_HEADS], dw_qfig, dwi["gates"]], axis=1)
    dh1, dnw = _rms_bwd(h1, W["mix_norm"], dum, dh2, "mix_norm_bwd", after=stage("w_in_grads", dwi).get("_after"))
    G["mix_norm"] = dnw[0:1]
    (dx, dfirst), dnw, G["ffn1_w_gu"], G["ffn1_w_down"] = _ffn_bwd(
        h0, W["ffn1_norm"], W["ffn1_w_gu"], W["ffn1_w_down"], sv1, dh1, "ffn1",
        lambda dw: stage("ffn1_dw_down", dw).get("_after"), token_seqs=B,
        told=lambda name, t: stage("ffn1_" + name, t).get("_after"))
    G["ffn1_norm"] = dnw[0:1]
    G["meta_tokens"] = jnp.sum(dfirst[:, pad:CHUNK], axis=0)
    return loss, dx, G


ANY = pl.BlockSpec(memory_space=pl.ANY)


def _place():
    return lax.axis_index("x"), lax.axis_index("y"), lax.axis_index("c")


def _other_chips(x, y):
    return [(1 - x, y), (x, 1 - y), (1 - x, 1 - y)]


def _remote(src, dst, ssem, rsem, dev):
    return pltpu.make_async_remote_copy(src_ref=src, dst_ref=dst, send_sem=ssem, recv_sem=rsem,
                                        device_id=dev, device_id_type=MESH)


def _exchange8(buf, name):
    n, w = buf.shape

    def body(x_ref, out_ref, ssem, rsem):
        x, y, c = _place()
        me = 4 * x + 2 * y + c
        out_ref[me] = x_ref[...]
        copies = []
        for k in range(1, 8):
            px = 1 - x if (k >> 2) & 1 else x
            py = 1 - y if (k >> 1) & 1 else y
            pc = 1 - c if k & 1 else c
            cp = _remote(x_ref, out_ref.at[me], ssem.at[k - 1], rsem.at[k - 1], (px, py, pc))
            cp.start()
            copies.append((cp, 4 * px + 2 * py + pc))
        for k, (cp, peer) in enumerate(copies):
            _remote(x_ref, out_ref.at[peer], ssem.at[k], rsem.at[k], (x, y, c)).wait_recv()
        for cp, _ in copies:
            cp.wait_send()

    vm = pl.BlockSpec(memory_space=pltpu.VMEM)
    return pl.pallas_call(
        body, in_specs=[vm], out_specs=vm, out_shape=jax.ShapeDtypeStruct((8, n, w), F32),
        scratch_shapes=[pltpu.SemaphoreType.DMA((7,)), pltpu.SemaphoreType.DMA((7,))], name=name,
    )(buf)


HBM = pltpu.MemorySpace.HBM


def _sequencer(name, collective_id, sems, sent):
    return functools.partial(pl.kernel, mesh=plsc.ScalarSubcoreMesh(axis_name="sequencer", num_cores=1), name=name,
                             scratch_types=sems, compiler_params=pltpu.CompilerParams(collective_id=collective_id),
                             cost_estimate=pl.CostEstimate(flops=0, transcendentals=0, bytes_accessed=2 * sent,
                                                           remote_bytes_transferred=sent))


def _nbytes(arrays):
    return sum(a.size * a.dtype.itemsize for a in arrays)


def _handshake(peers):
    barrier = pltpu.get_barrier_semaphore()
    for peer in peers:
        pl.semaphore_signal(barrier, inc=1, device_id=peer, device_id_type=MESH)
    pl.semaphore_wait(barrier, len(peers))


def _gather_seq(blocks, name, collective_id):
    n = len(blocks)
    half = [s.shape[1] // 2 for s in blocks]
    full = [jax.new_ref(b, memory_space=HBM) for b in blocks]

    @_sequencer(name, collective_id, [pltpu.SemaphoreType.DMA((n, 3))] * 4, _nbytes(blocks) * 3 // 4)
    def launch(ssem, rsem, fssem, frsem):
        x, y, c = _place()
        q = 2 * x + y
        chips = _other_chips(x, y)
        _handshake([(px, py, c) for px, py in chips] + [(x, y, 1 - c)])
        piece = lambda s, qq, cc: full[s].at[qq, pl.ds(cc * half[s], half[s])]
        sends = []
        for j, (px, py) in enumerate(chips):
            for s in range(n):
                cp = _remote(piece(s, q, c), piece(s, q, c), ssem.at[s, j], rsem.at[s, j], (px, py, c))
                cp.start()
                sends.append(cp)
        for j, (px, py) in enumerate(chips):
            for s in range(n):
                got = piece(s, 2 * px + py, c)
                _remote(got, got, ssem.at[s, j], rsem.at[s, j], (px, py, c)).wait_recv()
                cp = _remote(got, got, fssem.at[s, j], frsem.at[s, j], (x, y, 1 - c))
                cp.start()
                sends.append(cp)
        for j, (px, py) in enumerate(chips):
            for s in range(n):
                got = piece(s, 2 * px + py, 1 - c)
                _remote(got, got, fssem.at[s, j], frsem.at[s, j], (x, y, 1 - c)).wait_recv()
        for cp in sends:
            cp.wait_send()

    launch()
    return [r[...] for r in full]


def _share8(buf, name, collective_id):
    n, w = buf.shape
    src = jax.new_ref(buf, memory_space=HBM)
    out = jax.empty_ref(jax.ShapeDtypeStruct((8, n, w), F32), memory_space=HBM)

    @_sequencer(name, collective_id, [pltpu.SemaphoreType.DMA((7,)), pltpu.SemaphoreType.DMA((7,)), pltpu.SemaphoreType.DMA((1,))],
                7 * buf.size * 4)
    def launch(ssem, rsem, lsem):
        x, y, c = _place()
        me = 4 * x + 2 * y + c
        peers = [(1 - x if (k >> 2) & 1 else x, 1 - y if (k >> 1) & 1 else y, 1 - c if k & 1 else c) for k in range(1, 8)]
        _handshake(peers)
        mine = pltpu.make_async_copy(src, out.at[me], lsem.at[0])
        mine.start()
        sends = []
        for k, peer in enumerate(peers):
            cp = _remote(src, out.at[me], ssem.at[k], rsem.at[k], peer)
            cp.start()
            sends.append(cp)
        for k, (px, py, pc) in enumerate(peers):
            slot = out.at[4 * px + 2 * py + pc]
            _remote(slot, slot, ssem.at[k], rsem.at[k], (px, py, pc)).wait_recv()
        for cp in sends:
            cp.wait_send()
        mine.wait()

    launch()
    return out[...]


def _sum_slots(slots, name, after=None):
    _, n, w = slots.shape

    def body(s_ref, o_ref):
        acc = s_ref[0]
        for d in range(1, 8):
            acc = acc + s_ref[d]
        o_ref[...] = acc

    vm = pl.BlockSpec(memory_space=pltpu.VMEM)
    in_specs, args = [vm], [slots]
    if after is not None:
        body = _skip_ref(body, 1)
        args.append(_deps(after))
        in_specs.append(vm)
    return pl.pallas_call(body, in_specs=in_specs, out_specs=vm, out_shape=jax.ShapeDtypeStruct((n, w), F32), name=name)(*args)


def _pair_swap(parts, name, collective_id):
    n = len(parts)
    half = [p.shape[1] // 2 for p in parts]
    src = [jax.new_ref(p, memory_space=HBM) for p in parts]
    got = [jax.empty_ref(jax.ShapeDtypeStruct((p.shape[0], h, p.shape[2]), p.dtype), memory_space=HBM) for p, h in zip(parts, half)]

    @_sequencer(name, collective_id, [pltpu.SemaphoreType.DMA((n,))] * 2, _nbytes(parts) // 2)
    def launch(ssem, rsem):
        x, y, c = _place()
        _handshake([(x, y, 1 - c)])
        copies = []
        for s in range(n):
            cp = _remote(src[s].at[pl.ds(0, parts[s].shape[0]), pl.ds((1 - c) * half[s], half[s])], got[s], ssem.at[s], rsem.at[s], (x, y, 1 - c))
            cp.start()
            copies.append(cp)
        for cp in copies:
            cp.wait_recv()
        for cp in copies:
            cp.wait_send()

    launch()
    return [g[...] for g in got]


def _to_owners(sums, name, collective_id):
    n = len(sums)
    src = [jax.new_ref(s, memory_space=HBM) for s in sums]
    got = [jax.empty_ref(jax.ShapeDtypeStruct(s.shape, s.dtype), memory_space=HBM) for s in sums]

    @_sequencer(name, collective_id, [pltpu.SemaphoreType.DMA((n, 3))] * 2, _nbytes(sums) * 3 // 4)
    def launch(ssem, rsem):
        x, y, c = _place()
        q = 2 * x + y
        chips = _other_chips(x, y)
        _handshake([(px, py, c) for px, py in chips])
        sends = []
        for j, (px, py) in enumerate(chips):
            for s in range(n):
                cp = _remote(src[s].at[2 * px + py], got[s].at[q], ssem.at[s, j], rsem.at[s, j], (px, py, c))
                cp.start()
                sends.append(cp)
        for j, (px, py) in enumerate(chips):
            for s in range(n):
                slot = got[s].at[2 * px + py]
                _remote(slot, slot, ssem.at[s, j], rsem.at[s, j], (px, py, c)).wait_recv()
        for cp in sends:
            cp.wait_send()

    launch()
    return [g[...] for g in got]


def _pair_join(blocks, name, collective_id):
    n = len(blocks)
    out = [jax.new_ref(b, memory_space=HBM) for b in blocks]

    @_sequencer(name, collective_id, [pltpu.SemaphoreType.DMA((n,))] * 2, _nbytes(blocks) // 2)
    def launch(ssem, rsem):
        x, y, c = _place()
        _handshake([(x, y, 1 - c)])
        sends = []
        for s in range(n):
            h = blocks[s].shape[0] // 2
            mine = out[s].at[pl.ds(c * h, h)]
            cp = _remote(mine, mine, ssem.at[s], rsem.at[s], (x, y, 1 - c))
            cp.start()
            sends.append(cp)
        for s in range(n):
            h = blocks[s].shape[0] // 2
            theirs = out[s].at[pl.ds((1 - c) * h, h)]
            _remote(theirs, theirs, ssem.at[s], rsem.at[s], (x, y, 1 - c)).wait_recv()
        for cp in sends:
            cp.wait_send()

    launch()
    return [o[...] for o in out]


WIRE = BF16


def _row_tile(h):
    return _pick(h, (256, 368, 352, 128, 16))


def _add_pair(part, got, c, name, after=None):
    _, h, w = got.shape
    tr = _row_tile(h)
    nt = h // tr

    def body(c_ref, p_ref, g_ref, o_ref):
        o_ref[...] = (p_ref[...] + g_ref[...].astype(F32)).astype(o_ref.dtype)

    in_specs = [pl.BlockSpec((None, tr, w), lambda q, i, c_ref: (q, c_ref[0] * nt + i, 0)),
                pl.BlockSpec((None, tr, w), lambda q, i, c_ref: (q, i, 0))]
    args = [c.reshape(1).astype(jnp.int32), part, got]
    if after is not None:
        body = _skip_ref(body, len(args))
        args.append(_deps(after))
        in_specs.append(_dep_spec(args[-1]))
    return pl.pallas_call(
        body,
        grid_spec=pltpu.PrefetchScalarGridSpec(
            num_scalar_prefetch=1, grid=(got.shape[0], nt), in_specs=in_specs,
            out_specs=pl.BlockSpec((None, tr, w), lambda q, i, c_ref: (q, i, 0))),
        out_shape=jax.ShapeDtypeStruct(got.shape, WIRE), name=name,
        compiler_params=_cparams(("arbitrary", "arbitrary")),
    )(*args)


def _sum_chips(slots, sums, q, c, name, after=None):
    _, h, w = slots.shape
    tr = _row_tile(h)
    nt = h // tr

    def body(s_ref, mine_ref, a_ref, b_ref, d_ref, o_ref):
        o_ref[...] = ((mine_ref[...].astype(F32) + a_ref[...].astype(F32)) + b_ref[...].astype(F32)) + d_ref[...].astype(F32)

    slot = lambda k: pl.BlockSpec((None, tr, w), lambda i, s_ref: (s_ref[1 + k], i, 0))
    scalars = jnp.stack([c, q, (q + 1) % N_CHIPS, (q + 2) % N_CHIPS, (q + 3) % N_CHIPS]).astype(jnp.int32)
    in_specs, args = [slot(0), slot(1), slot(2), slot(3)], [scalars, sums, slots, slots, slots]
    if after is not None:
        body = _skip_ref(body, len(args))
        args.append(_deps(after))
        in_specs.append(_dep_spec(args[-1]))
    return pl.pallas_call(
        body,
        grid_spec=pltpu.PrefetchScalarGridSpec(
            num_scalar_prefetch=1, grid=(nt,), in_specs=in_specs,
            out_specs=pl.BlockSpec((tr, w), lambda i, s_ref: (s_ref[0] * nt + i, 0))),
        out_shape=jax.ShapeDtypeStruct((2 * h, w), F32), name=name,
        compiler_params=_cparams(("arbitrary",)),
    )(*args)


class _Reduce:
    def __init__(self, parts, q, c, tag, first_id, regions=None):
        self.parts, self.q, self.c, self.tag, self.first_id, self.regions = parts, q, c, tag, first_id, regions
        self.got = _pair_swap(parts, f"{tag}_pair_swap", first_id)

    def to_owners(self, after=None):
        self.sums = [_add_pair(p, g, self.c, f"{self.tag}_pair_add{i}", after)
                     for i, (p, g) in enumerate(zip(self.parts, self.got))]
        if self.regions is not None:
            self.sums = self.regions(self.sums)
        self.slots = _to_owners(self.sums, f"{self.tag}_to_owners", self.first_id + 1)
        return self.sums

    def join(self, after=None):
        blocks = [_sum_chips(sl, sm, self.q, self.c, f"{self.tag}_sum_chips{i}", after)
                  for i, (sl, sm) in enumerate(zip(self.slots, self.sums))]
        self.out = _pair_join(blocks, f"{self.tag}_pair_join", self.first_id + 2)
        return blocks


WEIGHTS = ("meta_tokens", "ffn1_norm", "ffn1_w_gu", "ffn1_w_down", "mix_norm", "w_in", "ssd_conv_w", "ssd_conv_b",
           "ssd_dt_bias", "ssd_a_log", "ssd_d", "ssd_norm", "hg_lower_bound", "hg_norm", "w_branch_a", "w_branch_b",
           "w_out", "ffn2_norm", "ffn2_w_gu", "ffn2_w_down", "final_norm")
BIG = ("ffn1_w_gu", "ffn1_w_down", "w_in", "w_branch_a", "w_branch_b", "w_out", "ffn2_w_gu", "ffn2_w_down")
SMALL = tuple(n for n in WEIGHTS if n not in BIG)


def _rows1024(a):
    flat = a.reshape(-1)
    n = -(-flat.shape[0] // 1024) * 1024
    return jnp.pad(flat, (0, n - flat.shape[0])).reshape(-1, 1024)


def kernel(x, meta_tokens, ffn1_norm, ffn1_w_gu, ffn1_w_down, mix_norm, w_in, ssd_conv_w, ssd_conv_b, ssd_dt_bias, ssd_a_log, ssd_d, ssd_norm, hg_lower_bound, hg_norm, w_branch_a, w_branch_b, w_out, ffn2_norm, ffn2_w_gu, ffn2_w_down, final_norm, loss_target, m_meta_tokens, m_ffn1_norm, m_ffn1_w_gu, m_ffn1_w_down, m_mix_norm, m_w_in, m_ssd_conv_w, m_ssd_conv_b, m_ssd_dt_bias, m_ssd_a_log, m_ssd_d, m_ssd_norm, m_hg_lower_bound, m_hg_norm, m_w_branch_a, m_w_branch_b, m_w_out, m_ffn2_norm, m_ffn2_w_gu, m_ffn2_w_down, m_final_norm, v_meta_tokens, v_ffn1_norm, v_ffn1_w_gu, v_ffn1_w_down, v_mix_norm, v_w_in, v_ssd_conv_w, v_ssd_conv_b, v_ssd_dt_bias, v_ssd_a_log, v_ssd_d, v_ssd_norm, v_hg_lower_bound, v_hg_norm, v_w_branch_a, v_w_branch_b, v_w_out, v_ffn2_norm, v_ffn2_w_gu, v_ffn2_w_down, v_final_norm):
    P = dict(zip(WEIGHTS, (meta_tokens, ffn1_norm, ffn1_w_gu, ffn1_w_down, mix_norm, w_in, ssd_conv_w, ssd_conv_b, ssd_dt_bias, ssd_a_log, ssd_d, ssd_norm, hg_lower_bound, hg_norm, w_branch_a, w_branch_b, w_out, ffn2_norm, ffn2_w_gu, ffn2_w_down, final_norm)))
    M = dict(zip(WEIGHTS, (m_meta_tokens, m_ffn1_norm, m_ffn1_w_gu, m_ffn1_w_down, m_mix_norm, m_w_in, m_ssd_conv_w, m_ssd_conv_b, m_ssd_dt_bias, m_ssd_a_log, m_ssd_d, m_ssd_norm, m_hg_lower_bound, m_hg_norm, m_w_branch_a, m_w_branch_b, m_w_out, m_ffn2_norm, m_ffn2_w_gu, m_ffn2_w_down, m_final_norm)))
    V = dict(zip(WEIGHTS, (v_meta_tokens, v_ffn1_norm, v_ffn1_w_gu, v_ffn1_w_down, v_mix_norm, v_w_in, v_ssd_conv_w, v_ssd_conv_b, v_ssd_dt_bias, v_ssd_a_log, v_ssd_d, v_ssd_norm, v_hg_lower_bound, v_hg_norm, v_w_branch_a, v_w_branch_b, v_w_out, v_ffn2_norm, v_ffn2_w_gu, v_ffn2_w_down, v_final_norm)))
    cx, cy, cc = _place()
    q = 2 * cx + cy

    mine = jnp.concatenate([meta_tokens.reshape(4, 1024), ssd_conv_w.reshape(2, 1024), jnp.zeros((2, 1024), F32)], axis=0)
    every = _exchange8(mine, "gather_small")
    meta_full = jnp.concatenate([every[2 * k, 0:4].reshape(N_META, 256) for k in range(N_CHIPS)], axis=1)
    conv_w_full = jnp.concatenate([every[2 * k, 4:6].reshape(SSD_CONV, 512) for k in range(N_CHIPS)], axis=1)

    late = ("ffn2_w_down", "w_branch_a", "w_branch_b", "w_out")
    rows = jnp.concatenate([P[n][0] for n in late], axis=0)
    zero = lambda t, dtype=F32: (t[0:1, 0:1] * 0).astype(dtype)

    def in_slot(s, after=None):
        s = s if after is None else s + zero(after)
        return lax.dynamic_update_slice(lax.empty((N_CHIPS,) + s.shape, BF16), s.astype(BF16)[None], (q, 0, 0))

    gu1, down1 = _gather_seq([in_slot(ffn1_w_gu[0]), in_slot(ffn1_w_down[0])], "gather_ffn1", 1)
    W = {n: P[n] for n in SMALL}
    W["meta_tokens"], W["ssd_conv_w"] = meta_full, conv_w_full
    W["ffn1_w_gu"], W["ffn1_w_down"] = gu1, down1.reshape(-1, D_MODEL)
    flying = {}

    def stage(name, t):
        if name == "ffn1_norm":
            flying["w_in"] = _gather_seq([in_slot(w_in[0], t)], "gather_w_in", 2)
            return {}
        if name == "ffn1_out":
            flying["late"] = _gather_seq([in_slot(ffn2_w_gu[0], t), in_slot(rows, t)], "gather_late", 3)
            (w_in_all,) = flying["w_in"]
            w_in_all = w_in_all + zero(t, BF16)
            return {"w_in": _split_w_in(w_in_all.transpose(1, 0, 2).reshape(D_MODEL, -1))}
        if name == "mixers_out":
            gu2, rows_all = flying["late"]
            out, r = {"ffn2_w_gu": gu2}, 0
            for n in late:
                nr = P[n].shape[1]
                out[n] = (rows_all[:, r:r + nr] + zero(t, BF16)).reshape(N_CHIPS * nr, D_MODEL)
                r += nr
            return out
        if name == "late_grads":
            parts = [t["ffn2_w_gu"]] + [t[n].reshape(N_CHIPS, -1, D_MODEL) for n in late]
            flying["grad_late"] = _Reduce(parts, q, cc, "grad_late", 4)
            return {"_after": [t["ffn2_w_gu"]] + [t[n] for n in late]}
        if name == "after_conv_bwd":
            return {"_after": flying["grad_late"].to_owners(after=t)}
        if name == "w_in_grads":
            order = ("z", "xbc", "dt", "qfig", "gates")
            blocks = flying["grad_late"].join(after=[t[k] for k in order])

            def regions(sums):
                z, xbc, dt, qfig, gates = [s[0] for s in sums]
                h = z.shape[0]
                qfig = qfig.reshape(h, HG_HEADS, 4, 128).transpose(0, 2, 1, 3).reshape(h, 4 * D_MODEL)
                cols = jnp.concatenate([z, xbc, dt[:, :SSD_HEADS], qfig, gates], axis=1)
                return [cols.reshape(h, N_CHIPS, -1).transpose(1, 0, 2)]

            flying["grad_w_in"] = _Reduce([t[k][None] for k in order], q, cc, "grad_w_in", 7, regions)
            return {"_after": blocks}
        if name == "ffn1_dw_down":
            return {"_after": flying["grad_w_in"].to_owners(after=t)}
        if name == "ffn1_dw":
            dw_gu, dw_down = t
            flying["grad_ffn1"] = _Reduce([dw_gu, dw_down.reshape(N_CHIPS, -1, D_MODEL)], q, cc, "grad_ffn1", 10)
            return {}
        if name == "ffn1_d_norm":
            blocks = flying["grad_w_in"].join(after=t)
            return {"_after": flying["grad_ffn1"].to_owners(after=blocks)}
        return {}

    W["_stage"] = stage

    loss8, grad_x, G = _local_step(x, loss_target, W)

    small = jnp.concatenate(
        [G["meta_tokens"]] + [_rows1024(G[n]) for n in SMALL if n != "meta_tokens"] + [_rows1024(loss8[0:1, 0:1])], axis=0)
    small = jnp.pad(small, ((0, 40 - small.shape[0]), (0, 0)))
    small_slots = _share8(small, "share_small", 13)

    grad_ffn1 = flying["grad_ffn1"]
    going = grad_ffn1.sums
    (g_w_in,) = flying["grad_w_in"].out
    Gb = dict(zip(("ffn2_w_gu",) + late, flying["grad_late"].out))
    Gb["w_in"] = g_w_in

    grads, delta, new_m, new_v, done = {}, {}, {}, {}, []
    cols = w_in.shape[2]
    to_tiles = lambda a: a.transpose(2, 0, 1).reshape(cols, 8, 128).reshape(cols * 8, 128)
    from_tiles = lambda a: a.reshape(cols, 1, D_MODEL).transpose(1, 2, 0)
    for n in [n for n in BIG if n in Gb]:
        if n == "w_in":
            g_t = to_tiles(Gb[n][None])
            d_, m_, v_ = _adamw(to_tiles(P[n]), g_t, to_tiles(M[n]), to_tiles(V[n]), f"adamw_{n}", after=going)
            grads[n], delta[n], new_m[n], new_v[n] = from_tiles(g_t), from_tiles(d_), from_tiles(m_), from_tiles(v_)
        else:
            d_, m_, v_ = _adamw(P[n][0], Gb[n], M[n][0], V[n][0], f"adamw_{n}", after=going)
            grads[n], delta[n], new_m[n], new_v[n] = Gb[n][None], d_[None], m_[None], v_[None]
        done.append(d_)

    small = _sum_slots(small_slots, "sum_small", after=done)
    Gs = {"meta_tokens": small[0:N_META]}
    r = N_META
    for n in SMALL:
        if n == "meta_tokens":
            continue
        nr = -(-G[n].size // 1024)
        Gs[n] = small[r:r + nr].reshape(-1)[:G[n].size].reshape(G[n].shape)
        r += nr
    loss = small[r, 0]
    Gs["meta_tokens"] = lax.dynamic_slice(Gs["meta_tokens"], (0, 256 * q), (N_META, 256))
    Gs["ssd_conv_w"] = lax.dynamic_slice(Gs["ssd_conv_w"], (0, 512 * q), (SSD_CONV, 512))[None]
    Gs = {n: Gs[n].reshape(P[n].shape) for n in SMALL}
    grads.update(Gs)
    flat = lambda a: a.reshape(-1, a.shape[-1])
    d_s, m_s, v_s = _adamw_many(*[[flat(D[n]) for n in SMALL] for D in (P, Gs, M, V)], "adamw_small")
    for i, n in enumerate(SMALL):
        delta[n], new_m[n], new_v[n] = d_s[i].reshape(P[n].shape), m_s[i].reshape(P[n].shape), v_s[i].reshape(P[n].shape)
    done.append(d_s[0])
    grad_ffn1.join(after=done)
    Gb["ffn1_w_gu"], Gb["ffn1_w_down"] = grad_ffn1.out
    for n in ("ffn1_w_gu", "ffn1_w_down"):
        d_, m_, v_ = _adamw(P[n][0], Gb[n], M[n][0], V[n][0], f"adamw_{n}")
        grads[n], delta[n], new_m[n], new_v[n] = Gb[n][None], d_[None], m_[None], v_[None]
    return (loss, grad_x, *[grads[n] for n in WEIGHTS], *[delta[n] for n in WEIGHTS],
            *[new_m[n] for n in WEIGHTS], *[new_v[n] for n in WEIGHTS])
```

```python
import functools

import jax
import jax.numpy as jnp
from jax import lax
from jax.experimental import pallas as pl
from jax.experimental.pallas import tpu as pltpu
from jax.experimental.pallas import tpu_sc as plsc

F32 = jnp.float32
BF16 = jnp.bfloat16
HIGHEST = lax.Precision.HIGHEST
MESH = pl.DeviceIdType.MESH

D_MODEL = 1024
N_META = 16
EPS = 1e-6
SSD_HEADS = 16
SSD_HEAD_DIM = 64
SSD_INNER = 1024
SSD_GROUPS = 4
SSD_STATE = 128
SSD_CONV = 4
SSD_CONV_CH = 2048
HG_HEADS = 8
HG_SUB = 32
CHUNK = 128
D_FF = 2816
N_CHIPS = 4
IN_SIZES = (1024, 2048, 16, 1024, 1024, 1024, 1024, 1024, 1024)
ADAM_LR = 0.001
ADAM_B1 = 0.9
ADAM_B2 = 0.999
ADAM_EPS = 1e-08
ADAM_WD = 0.01
ADAM_STEP = 10
VMEM_LIMIT = 56 * 1024 * 1024
MATMUL_BLOCK_BYTES = 42 * 1024 * 1024
ADAMW_BLOCK_BYTES = 5 * 512 * 1024


def _cparams(sem=None):
    return pltpu.CompilerParams(dimension_semantics=sem, vmem_limit_bytes=VMEM_LIMIT)


def _pick(n, cands):
    for c in cands:
        if n % c == 0:
            return c
    return n


def _deps(after):
    xs = after if isinstance(after, (list, tuple)) else [after]
    one = lambda x: lax.slice(x, (0,) * x.ndim, (1,) * x.ndim).reshape(1).astype(F32)
    return jnp.concatenate([one(x) for x in xs]).reshape(1, -1)


def _dep_spec(dep):
    return pl.BlockSpec(dep.shape, lambda *_: (0, 0))


def _skip_ref(body, pos):
    return lambda *refs: body(*refs[:pos], *refs[pos + 1:])


def _dg(a, b, ca, cb):
    return lax.dot_general(a.astype(BF16), b.astype(BF16), (((ca,), (cb,)), ((), ())), preferred_element_type=F32)


@jax.custom_vjp
def _mm(a, b):
    return _dg(a, b, 1, 0)


def _mm_fwd(a, b):
    return _dg(a, b, 1, 0), (a, b)


def _mm_bwd(r, g):
    a, b = r
    return _dg(g, b, 1, 1), _dg(a, g, 0, 0)


_mm.defvjp(_mm_fwd, _mm_bwd)


@jax.custom_vjp
def _mm_nt(a, b):
    return _dg(a, b, 1, 1)


def _mm_nt_fwd(a, b):
    return _dg(a, b, 1, 1), (a, b)


def _mm_nt_bwd(r, g):
    a, b = r
    return _dg(g, b, 1, 0), _dg(g, a, 0, 0)


_mm_nt.defvjp(_mm_nt_fwd, _mm_nt_bwd)


@jax.custom_vjp
def _mm_tn(a, b):
    return _dg(a, b, 0, 0)


def _mm_tn_fwd(a, b):
    return _dg(a, b, 0, 0), (a, b)


def _mm_tn_bwd(r, g):
    a, b = r
    return _dg(b, g, 1, 1), _dg(a, g, 1, 0)


_mm_tn.defvjp(_mm_tn_fwd, _mm_tn_bwd)


def _tri_sum(x, lower):
    n = x.shape[0]
    ri = lax.broadcasted_iota(jnp.int32, (n, n), 0)
    ci = lax.broadcasted_iota(jnp.int32, (n, n), 1)
    tri = ((ri >= ci) if lower else (ri <= ci)).astype(BF16)
    x1 = x.astype(BF16)
    r1 = x - x1.astype(F32)
    x2 = r1.astype(BF16)
    x3 = (r1 - x2.astype(F32)).astype(BF16)
    dot = lambda p: lax.dot_general(tri, p, (((1,), (0,)), ((), ())), preferred_element_type=F32)
    return (dot(x3) + dot(x2)) + dot(x1)


@jax.custom_vjp
def _cumsum_rows(x):
    return _tri_sum(x, True)


_cumsum_rows.defvjp(lambda x: (_tri_sum(x, True), None), lambda _, g: (_tri_sum(g, False),))


def _silu(x):
    return x * jax.nn.sigmoid(x)


def _softplus(x):
    return jnp.maximum(x, 0.0) + jnp.log(1.0 + jnp.exp(-jnp.abs(x)))


def _tril(n):
    ri = lax.broadcasted_iota(jnp.int32, (n, n), 0)
    ci = lax.broadcasted_iota(jnp.int32, (n, n), 1)
    return ri >= ci


def _row_of(m, r):
    sub = lax.broadcasted_iota(jnp.int32, (m.shape[0], 1), 0)
    return jnp.sum(jnp.where(sub == r, m, 0.0), axis=0, keepdims=True)


def _col_of(m, c):
    lane = lax.broadcasted_iota(jnp.int32, (1, m.shape[1]), 1)
    return jnp.sum(jnp.where(lane == c, m, 0.0), axis=1, keepdims=True)


def _matmul(a, b, *, mode, out_dtype, name, alpha=1.0, res=None, tm=None, tn=None, out_groups=None, after=None):
    b3 = b.ndim == 3
    if mode == "nn":
        M, K = a.shape
        G = b.shape[0] if b3 else 1
        Ng = b.shape[-1]
        N = G * Ng
    elif mode == "nt":
        M, K = a.shape
        G = b.shape[0] if b3 else 1
        N = b.shape[-2]
        Kg = b.shape[-1]
        assert G * Kg == K
    else:
        K, M = a.shape
        N = b.shape[1]
        G = out_groups or 1
        Ng = N // G
    has_res = res is not None
    split_n = (mode == "nn" and b3) or (mode == "tn" and G > 1)
    per_mn = jnp.dtype(out_dtype).itemsize + (res.dtype.itemsize if has_res else 0)
    fits = [(m_ * n_, m_, n_)
            for m_ in (4352, 2176, 1408, 1088, 1024, 544, 512, 256, 128) if M % m_ == 0
            for n_ in (2816, 2048, 1408, 1024, 512, 256, 128) if (Ng if split_n else N) % n_ == 0
            if 2 * (K * m_ * a.dtype.itemsize + K * n_ * b.dtype.itemsize + m_ * n_ * per_mn) + 4 * m_ * n_ <= MATMUL_BLOCK_BYTES]
    _, tm_fit, tn_fit = max(fits)
    tm, tn = tm or tm_fit, tn or tn_fit
    nm, nn_ = M // tm, N // tn
    assert nm * tm == M and nn_ * tn == N, (name, M, N, K, tm, tn)

    if mode == "nn":
        a_spec = pl.BlockSpec((tm, K), lambda i, j: (i, 0))
        if b3:
            ns = Ng // tn
            b_spec = pl.BlockSpec((None, K, tn), lambda i, j: (j // ns, 0, j % ns))
        else:
            b_spec = pl.BlockSpec((K, tn), lambda i, j: (0, j))
        ca, cb = 1, 0
    elif mode == "nt":
        a_spec = pl.BlockSpec((tm, K), lambda i, j: (i, 0))
        if b3:
            b_spec = pl.BlockSpec((G, tn, Kg), lambda i, j: (0, j, 0))
        else:
            b_spec = pl.BlockSpec((tn, K), lambda i, j: (j, 0))
        ca, cb = 1, 1
    else:
        a_spec = pl.BlockSpec((K, tm), lambda i, j: (0, i))
        b_spec = pl.BlockSpec((K, tn), lambda i, j: (0, j))
        ca, cb = 0, 0
    if mode == "tn" and G > 1:
        ns = Ng // tn
        o_spec = pl.BlockSpec((None, tm, tn), lambda i, j: (j // ns, i, j % ns))
        out_shape = jax.ShapeDtypeStruct((G, M, Ng), out_dtype)
    else:
        o_spec = pl.BlockSpec((tm, tn), lambda i, j: (i, j))
        out_shape = jax.ShapeDtypeStruct((M, N), out_dtype)
    in_specs = [a_spec, b_spec]
    args = [a, b]
    if has_res:
        in_specs.append(pl.BlockSpec((tm, tn), lambda i, j: (i, j)))
        args.append(res)
    if after is not None:
        args.append(_deps(after))
        in_specs.append(_dep_spec(args[-1]))

    def body(*refs):
        a_ref, b_ref, o_ref = refs[0], refs[1], refs[-1]
        if mode == "nt" and b3:
            o = _dg(a_ref[:, 0:Kg], b_ref[0], ca, cb)
            for g in range(1, G):
                o = o + _dg(a_ref[:, g * Kg:(g + 1) * Kg], b_ref[g], ca, cb)
        else:
            o = _dg(a_ref[...], b_ref[...], ca, cb)
        if alpha != 1.0:
            o = o * alpha
        if has_res:
            o = o + refs[2][...]
        o_ref[...] = o.astype(o_ref.dtype)

    return pl.pallas_call(
        body, grid=(nm, nn_), in_specs=in_specs, out_specs=o_spec, out_shape=out_shape, name=name,
        compiler_params=_cparams(("parallel", "parallel")),
    )(*args)


def _sum_nt(xs, ws, name):
    R, N = xs[0].shape[0], ws[0].shape[0]
    n = len(xs)
    per_m = sum(x.shape[1] * x.dtype.itemsize for x in xs)
    per_n = sum(w.shape[1] * w.dtype.itemsize for w in ws)
    fits = [(m_ * n_, m_, n_) for m_ in (1088, 544, 256, 128) if R % m_ == 0 for n_ in (1024, 512, 256, 128) if N % n_ == 0
            if 2 * (m_ * per_m + n_ * per_n + m_ * n_ * 4) + 4 * m_ * n_ <= MATMUL_BLOCK_BYTES]
    _, tm, tn = max(fits)

    def body(*refs):
        o = _dg(refs[0][...], refs[n][...], 1, 1)
        for p in range(1, n):
            o = o + _dg(refs[p][...], refs[n + p][...], 1, 1)
        refs[-1][...] = o

    return pl.pallas_call(
        body, grid=(R // tm, N // tn),
        in_specs=[pl.BlockSpec((tm, x.shape[1]), lambda i, j: (i, 0)) for x in xs]
        + [pl.BlockSpec((tn, w.shape[1]), lambda i, j: (j, 0)) for w in ws],
        out_specs=pl.BlockSpec((tm, tn), lambda i, j: (i, j)), out_shape=jax.ShapeDtypeStruct((R, N), F32), name=name,
        compiler_params=_cparams(("parallel", "parallel")),
    )(*xs, *ws)


def _rms_fn(h, w):
    r = lax.rsqrt(jnp.mean(h * h, axis=-1, keepdims=True) + EPS)
    return h * r * w


def _swiglu_fn(gu):
    g = gu[:, :D_FF].astype(F32)
    u = gu[:, D_FF:].astype(F32)
    return _silu(g) * u


def _merge_fn(pa, pb, gates):
    return jax.nn.sigmoid(gates[:, :D_MODEL]) * pa + jax.nn.sigmoid(gates[:, D_MODEL:]) * pb


def _rows_call(body, *, rows, tr, ins, outs, accs=(), name, after=None):
    n = rows // tr
    assert n * tr == rows
    if after is not None:
        body = _skip_ref(body, len(ins))
        ins = list(ins) + [("full", _deps(after))]

    def spec(x):
        if isinstance(x, tuple):
            shp = x[1].shape
            return pl.BlockSpec(shp, lambda i: (0,) * len(shp))
        return pl.BlockSpec((tr, x.shape[1]), lambda i: (i, 0))

    in_specs = [spec(x) for x in ins]
    args = [x[1] if isinstance(x, tuple) else x for x in ins]
    out_specs = [spec(x) for x in outs] + [pl.BlockSpec(x.shape, lambda i: (0,) * len(x.shape)) for x in accs]
    out_shape = [x[1] if isinstance(x, tuple) else x for x in outs] + list(accs)
    return pl.pallas_call(
        body, grid=(n,), in_specs=in_specs, out_specs=out_specs, out_shape=out_shape, name=name,
        compiler_params=_cparams(("arbitrary",)),
    )(*args)


def _acc_rows(ref, val):
    @pl.when(pl.program_id(0) == 0)
    def _():
        ref[...] = jnp.zeros_like(ref)

    ref[0:1, :] += val


def _rms_fwd(h, w, name):
    def body(h_ref, w_ref, o_ref):
        o_ref[...] = _rms_fn(h_ref[...], w_ref[...]).astype(o_ref.dtype)

    R = h.shape[0]
    return _rows_call(body, rows=R, tr=_pick(R, (256, 128)), ins=[h, ("full", w)],
                      outs=[jax.ShapeDtypeStruct(h.shape, BF16)], name=name)[0]


def _rms_bwd(h, w, dn, dres, name, after=None):
    def body(h_ref, w_ref, dn_ref, dres_ref, dh_ref, dw_ref):
        _, vjp = jax.vjp(_rms_fn, h_ref[...], w_ref[...])
        dh, dw = vjp(dn_ref[...].astype(F32))
        dh_ref[...] = dh + dres_ref[...]
        _acc_rows(dw_ref, dw)

    R = h.shape[0]
    return _rows_call(body, rows=R, tr=_pick(R, (256, 128)), ins=[h, ("full", w), dn, dres],
                      outs=[jax.ShapeDtypeStruct(h.shape, F32)], accs=[jax.ShapeDtypeStruct((8, D_MODEL), F32)], name=name,
                      after=after)


def _d_norm_in(dgu, w_gu, h, norm_w, dres, name, after=None):
    R = h.shape[0]
    G, _, kg = w_gu.shape

    def body(dgu_ref, w_ref, h_ref, nw_ref, dres_ref, dh_ref, dw_ref):
        dn = _dg(dgu_ref[:, 0:kg], w_ref[0], 1, 1)
        for g in range(1, G):
            dn = dn + _dg(dgu_ref[:, kg * g:kg * (g + 1)], w_ref[g], 1, 1)
        _, vjp = jax.vjp(_rms_fn, h_ref[...], nw_ref[...])
        dh, dw = vjp(dn)
        dh_ref[...] = dh + dres_ref[...]
        _acc_rows(dw_ref, dw)

    return _rows_call(body, rows=R, tr=_pick(R, (256, 128)), ins=[dgu, ("full", w_gu), h, ("full", norm_w), dres],
                      outs=[jax.ShapeDtypeStruct(h.shape, F32)], accs=[jax.ShapeDtypeStruct((8, D_MODEL), F32)], name=name,
                      after=after)


def _rms_bwd_tokens(h, w, dn, dres, nseq, name, after=None):
    Tp = h.shape[0] // nseq
    nc = Tp // CHUNK

    def body(h_ref, w_ref, dn_ref, dres_ref, dx_ref, dm_ref, dw_ref):
        b, c = pl.program_id(0), pl.program_id(1)
        _, vjp = jax.vjp(_rms_fn, h_ref[...], w_ref[...])
        dh, dw = vjp(dn_ref[...].astype(F32))
        dh = dh + dres_ref[...]

        @pl.when(c == 0)
        def _():
            dm_ref[...] = dh

        @pl.when(c > 0)
        def _():
            dx_ref[...] = dh

        @pl.when((b == 0) & (c == 0))
        def _():
            dw_ref[...] = jnp.zeros_like(dw_ref)

        dw_ref[0:1, :] += dw

    rows = pl.BlockSpec((CHUNK, D_MODEL), lambda b, c: (b * nc + c, 0))
    in_specs, args = [rows, pl.BlockSpec((1, D_MODEL), lambda b, c: (0, 0)), rows, rows], [h, w, dn, dres]
    if after is not None:
        body = _skip_ref(body, len(args))
        args.append(_deps(after))
        in_specs.append(_dep_spec(args[-1]))
    return pl.pallas_call(
        body, grid=(nseq, nc), in_specs=in_specs,
        out_specs=[pl.BlockSpec((None, CHUNK, D_MODEL), lambda b, c: (b, jnp.maximum(c - 1, 0), 0)),
                   pl.BlockSpec((None, CHUNK, D_MODEL), lambda b, c: (b, 0, 0)),
                   pl.BlockSpec((8, D_MODEL), lambda b, c: (0, 0))],
        out_shape=[jax.ShapeDtypeStruct((nseq, Tp - CHUNK, D_MODEL), F32), jax.ShapeDtypeStruct((nseq, CHUNK, D_MODEL), F32),
                   jax.ShapeDtypeStruct((8, D_MODEL), F32)],
        name=name, compiler_params=_cparams(("arbitrary", "arbitrary")),
    )(*args)


def _gu_swiglu(n, w_gu, name):
    R = n.shape[0]
    G, _, ng = w_gu.shape

    def body(n_ref, w_ref, gu_ref, a_ref):
        x = n_ref[...]
        for r in range(G):
            gu_ref[:, ng * r:ng * (r + 1)] = _dg(x, w_ref[r], 1, 0).astype(gu_ref.dtype)
        a_ref[...] = _swiglu_fn(gu_ref[...]).astype(a_ref.dtype)

    return _rows_call(body, rows=R, tr=_pick(R, (256, 128)), ins=[n, ("full", w_gu)],
                      outs=[jax.ShapeDtypeStruct((R, 2 * D_FF), BF16), jax.ShapeDtypeStruct((R, D_FF), BF16)], name=name)


def _d_swiglu(dout, w_down, gu, alpha, name):
    R = gu.shape[0]

    def body(do_ref, w_ref, gu_ref, o_ref):
        da = _dg(do_ref[...] * alpha, w_ref[...], 1, 1)
        g = gu_ref[:, :D_FF].astype(F32)
        u = gu_ref[:, D_FF:].astype(F32)
        s = jax.nn.sigmoid(g)
        t = g * s
        o_ref[:, :D_FF] = (da * u * (s + t - t * s)).astype(o_ref.dtype)
        o_ref[:, D_FF:] = (da * t).astype(o_ref.dtype)

    return _rows_call(body, rows=R, tr=_pick(R, (256, 128)), ins=[dout, ("full", w_down), gu],
                      outs=[jax.ShapeDtypeStruct(gu.shape, BF16)], name=name)[0]


def _residual_matmul(a, w, res, alpha, name, norm_w=None):
    R, K = a.shape

    def body(a_ref, w_ref, r_ref, *rest):
        out = r_ref[...] + alpha * _dg(a_ref[...], w_ref[...], 1, 0)
        if norm_w is None:
            rest[0][...] = out
        else:
            rest[1][...] = out
            rest[2][...] = _rms_fn(out, rest[0][...]).astype(rest[2].dtype)

    f32 = jax.ShapeDtypeStruct((R, D_MODEL), F32)
    ins = [a, ("full", w), res] + ([] if norm_w is None else [("full", norm_w)])
    outs = [f32] + ([] if norm_w is None else [jax.ShapeDtypeStruct((R, D_MODEL), BF16)])
    got = _rows_call(body, rows=R, tr=_pick(R, (544, 256, 128)), ins=ins, outs=outs, name=name)
    return got[0] if norm_w is None else (got[0], got[1])


def _branch_merge(ya, yb, wa, wb, gates, name):
    def body(ya_ref, yb_ref, wa_ref, wb_ref, g_ref, pa_ref, pb_ref, o_ref):
        pa = _dg(ya_ref[...], wa_ref[...], 1, 0)
        pb = _dg(yb_ref[...], wb_ref[...], 1, 0)
        pa_ref[...] = pa
        pb_ref[...] = pb
        o_ref[...] = _merge_fn(pa, pb, g_ref[...].astype(F32)).astype(o_ref.dtype)

    R = ya.shape[0]
    f32 = jax.ShapeDtypeStruct((R, D_MODEL), F32)
    return _rows_call(body, rows=R, tr=_pick(R, (544, 256, 128)), ins=[ya, yb, ("full", wa), ("full", wb), gates],
                      outs=[f32, f32, jax.ShapeDtypeStruct((R, D_MODEL), BF16)], name=name)


def _branch_merge_bwd(pa, pb, gates, dm, wa, wb, name):
    def body(pa_ref, pb_ref, g_ref, dm_ref, wa_ref, wb_ref, dpa_ref, dpb_ref, dg_ref, dya_ref, dyb_ref):
        _, vjp = jax.vjp(_merge_fn, pa_ref[...], pb_ref[...], g_ref[...].astype(F32))
        dpa, dpb, dg = vjp(dm_ref[...].astype(F32))
        dpa_ref[...] = dpa.astype(dpa_ref.dtype)
        dpb_ref[...] = dpb.astype(dpb_ref.dtype)
        dg_ref[...] = dg.astype(dg_ref.dtype)
        dya_ref[...] = _dg(dpa, wa_ref[...], 1, 1).astype(dya_ref.dtype)
        dyb_ref[...] = _dg(dpb, wb_ref[...], 1, 1).astype(dyb_ref.dtype)

    R = pa.shape[0]
    b16 = jax.ShapeDtypeStruct(pa.shape, BF16)
    return _rows_call(body, rows=R, tr=_pick(R, (544, 256, 128)), ins=[pa, pb, gates, dm, ("full", wa), ("full", wb)],
                      outs=[b16, b16, jax.ShapeDtypeStruct(gates.shape, BF16), b16, b16], name=name)


def _loss_head(h3, w, target, nseq, name):
    Tp = h3.shape[0] // nseq
    nc = Tp // CHUNK

    def fn(h, w_, t, valid):
        y = _rms_fn(h, w_)
        e = (y - t) * valid
        return 0.5 * jnp.sum(jnp.mean(e * e, axis=-1, keepdims=True))

    def body(h_ref, w_ref, t_ref, loss_ref, dh_ref, dw_ref):
        b, c = pl.program_id(0), pl.program_id(1)
        valid = (c >= 1).astype(F32)
        t = t_ref[...]
        loss, vjp = jax.vjp(lambda h, w_: fn(h, w_, t, valid), h_ref[...], w_ref[...])
        dh, dw = vjp(jnp.ones((), F32))
        dh_ref[...] = dh

        @pl.when((b == 0) & (c == 0))
        def _():
            loss_ref[...] = jnp.zeros_like(loss_ref)
            dw_ref[...] = jnp.zeros_like(dw_ref)

        loss_ref[...] += jnp.full(loss_ref.shape, loss, F32)
        dw_ref[0:1, :] += dw

    return pl.pallas_call(
        body, grid=(nseq, nc),
        in_specs=[pl.BlockSpec((CHUNK, D_MODEL), lambda b, c: (b * nc + c, 0)),
                  pl.BlockSpec((1, D_MODEL), lambda b, c: (0, 0)),
                  pl.BlockSpec((None, CHUNK, D_MODEL), lambda b, c: (b, jnp.maximum(c - 1, 0), 0))],
        out_specs=[pl.BlockSpec((8, 128), lambda b, c: (0, 0)),
                   pl.BlockSpec((CHUNK, D_MODEL), lambda b, c: (b * nc + c, 0)),
                   pl.BlockSpec((8, D_MODEL), lambda b, c: (0, 0))],
        out_shape=[jax.ShapeDtypeStruct((8, 128), F32), jax.ShapeDtypeStruct(h3.shape, F32),
                   jax.ShapeDtypeStruct((8, D_MODEL), F32)],
        name=name, compiler_params=_cparams(("arbitrary", "arbitrary")),
    )(h3, w, target)


CONV_TILE = 512
CONV_HALO = 8


def _conv_fwd(xbc, w, b, pad, name):
    B, Tp, C = xbc.shape
    nch = Tp // CHUNK

    def body(x_ref, w_ref, b_ref, o_ref, xp):
        xp[0:CONV_HALO, :] = jnp.zeros((CONV_HALO, CONV_TILE), F32)
        xp[CONV_HALO:, :] = x_ref[...]
        for c in range(nch):
            acc = jnp.zeros((CHUNK, CONV_TILE), F32) + b_ref[...]
            for k in range(SSD_CONV):
                acc = acc + w_ref[k:k + 1, :] * xp[pl.ds(CONV_HALO + CHUNK * c - (SSD_CONV - 1) + k, CHUNK), :]
            out = _silu(acc)
            if CHUNK * c < pad:
                row = CHUNK * c + lax.broadcasted_iota(jnp.int32, (CHUNK, 1), 0)
                out = jnp.where(row >= pad, out, 0.0)
            o_ref[pl.ds(CHUNK * c, CHUNK), :] = out

    return pl.pallas_call(
        body, grid=(B, C // CONV_TILE),
        in_specs=[pl.BlockSpec((None, Tp, CONV_TILE), lambda i, j: (i, 0, j)),
                  pl.BlockSpec((SSD_CONV, CONV_TILE), lambda i, j: (0, j)),
                  pl.BlockSpec((1, CONV_TILE), lambda i, j: (0, j))],
        out_specs=pl.BlockSpec((None, Tp, CONV_TILE), lambda i, j: (i, 0, j)),
        out_shape=jax.ShapeDtypeStruct(xbc.shape, F32),
        scratch_shapes=[pltpu.VMEM((Tp + CONV_HALO, CONV_TILE), F32)],
        name=name, compiler_params=_cparams(("arbitrary", "arbitrary")),
    )(xbc, w, b)


def _conv_bwd(xbc, w, b, dact, pad, name):
    B, Tp, C = xbc.shape
    nch = Tp // CHUNK

    def body(x_ref, w_ref, b_ref, da_ref, dx_ref, dw_ref, db_ref, xp, dp):
        bi = pl.program_id(1)
        xp[0:CONV_HALO, :] = jnp.zeros((CONV_HALO, CONV_TILE), F32)
        xp[CONV_HALO:, :] = x_ref[...]
        dp[pl.ds(Tp, CONV_HALO), :] = jnp.zeros((CONV_HALO, CONV_TILE), F32)
        dws = [jnp.zeros((1, CONV_TILE), F32) for _ in range(SSD_CONV)]
        dbs = jnp.zeros((1, CONV_TILE), F32)
        for c in range(nch):
            xs = [xp[pl.ds(CONV_HALO + CHUNK * c - (SSD_CONV - 1) + k, CHUNK), :] for k in range(SSD_CONV)]
            acc = jnp.zeros((CHUNK, CONV_TILE), F32) + b_ref[...]
            for k in range(SSD_CONV):
                acc = acc + w_ref[k:k + 1, :] * xs[k]
            sg = jax.nn.sigmoid(acc)
            t = acc * sg
            dpre = da_ref[pl.ds(CHUNK * c, CHUNK), :] * (sg + t - t * sg)
            if CHUNK * c < pad:
                row = CHUNK * c + lax.broadcasted_iota(jnp.int32, (CHUNK, 1), 0)
                dpre = jnp.where(row >= pad, dpre, 0.0)
            dp[pl.ds(CHUNK * c, CHUNK), :] = dpre
            dbs = dbs + jnp.sum(dpre, axis=0, keepdims=True)
            for k in range(SSD_CONV):
                dws[k] = dws[k] + jnp.sum(dpre * xs[k], axis=0, keepdims=True)
        for c in range(nch):
            acc = jnp.zeros((CHUNK, CONV_TILE), F32)
            for k in range(SSD_CONV):
                acc = acc + w_ref[k:k + 1, :] * dp[pl.ds(CHUNK * c + (SSD_CONV - 1) - k, CHUNK), :]
            dx_ref[pl.ds(CHUNK * c, CHUNK), :] = acc.astype(dx_ref.dtype)

        @pl.when(bi == 0)
        def _():
            dw_ref[...] = jnp.zeros_like(dw_ref)
            db_ref[...] = jnp.zeros_like(db_ref)

        for k in range(SSD_CONV):
            dw_ref[k:k + 1, :] += dws[k]
        db_ref[0:1, :] += dbs

    return pl.pallas_call(
        body, grid=(C // CONV_TILE, B),
        in_specs=[pl.BlockSpec((None, Tp, CONV_TILE), lambda j, i: (i, 0, j)),
                  pl.BlockSpec((SSD_CONV, CONV_TILE), lambda j, i: (0, j)),
                  pl.BlockSpec((1, CONV_TILE), lambda j, i: (0, j)),
                  pl.BlockSpec((None, Tp, CONV_TILE), lambda j, i: (i, 0, j))],
        out_specs=[pl.BlockSpec((None, Tp, CONV_TILE), lambda j, i: (i, 0, j)),
                   pl.BlockSpec((8, CONV_TILE), lambda j, i: (0, j)),
                   pl.BlockSpec((8, CONV_TILE), lambda j, i: (0, j))],
        out_shape=[jax.ShapeDtypeStruct(xbc.shape, BF16), jax.ShapeDtypeStruct((8, C), F32),
                   jax.ShapeDtypeStruct((8, C), F32)],
        scratch_shapes=[pltpu.VMEM((Tp + CONV_HALO, CONV_TILE), F32), pltpu.VMEM((Tp + CONV_HALO, CONV_TILE), F32)],
        name=name, compiler_params=_cparams(("arbitrary", "arbitrary")),
    )(xbc, w, b, dact)


def _ssd_chunk(xs, bm, cm, dtr, z, state, dt_bias, a_log, dskip, norm_w, valid):
    Q = xs.shape[0]
    lane = lax.broadcasted_iota(jnp.int32, (1, 128), 1)
    dt = jnp.where(lane < SSD_HEADS, _softplus(dtr + dt_bias), 0.0) * valid
    a = dt * (-jnp.exp(a_log))
    tril = _tril(Q)
    cs = _cumsum_rows(a)
    cs_t = cs.T
    cs_end = _row_of(cs, Q - 1)
    low = lane < SSD_HEAD_DIM
    low_rows = lax.broadcasted_iota(jnp.int32, (128, 1), 0) < SSD_HEAD_DIM
    ys, new_state = [], []
    for g in range(SSD_GROUPS):
        bg = bm[:, 128 * g:128 * (g + 1)]
        cg = cm[:, 128 * g:128 * (g + 1)]
        cb = _mm_nt(cg, bg)
        for pr in range(2):
            p = 2 * g + pr
            h0, h1 = 2 * p, 2 * p + 1
            xp = xs[:, 128 * p:128 * (p + 1)]
            c0, c1 = _col_of(cs, h0), _col_of(cs, h1)
            e0, e1 = _col_of(cs_end, h0), _col_of(cs_end, h1)
            xd = xp * jnp.where(low, _col_of(dt, h0), _col_of(dt, h1))
            l0 = jnp.exp(jnp.where(tril, c0 - _row_of(cs_t, h0), -1e30))
            l1 = jnp.exp(jnp.where(tril, c1 - _row_of(cs_t, h1), -1e30))
            y_diag = jnp.where(low, _mm(cb * l0, xd), _mm(cb * l1, xd))
            to_end = jnp.where(low, jnp.exp(e0 - c0), jnp.exp(e1 - c1))
            sp = state[128 * p:128 * (p + 1), :]
            y_off = _mm_nt(cg, sp) * jnp.where(low, jnp.exp(c0), jnp.exp(c1))
            new_state.append(sp * jnp.where(low_rows, jnp.exp(e0), jnp.exp(e1)) + _mm_tn(xd * to_end, bg))
            ys.append(y_diag + y_off + xp * jnp.where(low, _col_of(dskip, h0), _col_of(dskip, h1)))
    y = jnp.concatenate(ys, axis=1) * _silu(z)
    gw = SSD_INNER // SSD_GROUPS
    outs = []
    for g in range(SSD_GROUPS):
        blk = y[:, gw * g:gw * (g + 1)]
        outs.append(blk * lax.rsqrt(jnp.mean(blk * blk, axis=-1, keepdims=True) + EPS))
    return jnp.concatenate(outs, axis=1) * norm_w, jnp.concatenate(new_state, axis=0)


def _valid_rows(c, pad):
    row = c * CHUNK + lax.broadcasted_iota(jnp.int32, (CHUNK, 1), 0)
    return (row >= pad).astype(F32)


def _ssd_fwd(xact, dtr, z, dt_bias, a_log, dskip, norm_w, pad, name):
    B, Tp, _ = xact.shape
    nc = Tp // CHUNK

    def body(xs_ref, bm_ref, cm_ref, dt_ref, z_ref, db_ref, al_ref, ds_ref, nw_ref, y_ref, save_ref, st):
        c = pl.program_id(1)

        @pl.when(c == 0)
        def _():
            st[...] = jnp.zeros_like(st)

        s0 = st[...]
        save_ref[...] = s0
        y, s1 = _ssd_chunk(xs_ref[...], bm_ref[...], cm_ref[...], dt_ref[...], z_ref[...].astype(F32), s0, db_ref[...],
                           al_ref[...], ds_ref[...], nw_ref[...], _valid_rows(c, pad))
        y_ref[...] = y.astype(y_ref.dtype)
        st[...] = s1

    row = lambda w, off=0: pl.BlockSpec((None, CHUNK, w), lambda b, c: (b, c, off))
    par = lambda w: pl.BlockSpec((1, w), lambda b, c: (0, 0))
    return pl.pallas_call(
        body, grid=(B, nc),
        in_specs=[row(1024, 0), row(512, 2), row(512, 3), row(128), row(1024), par(128), par(128), par(128), par(1024)],
        out_specs=[row(1024), pl.BlockSpec((None, None, 1024, 128), lambda b, c: (b, c, 0, 0))],
        out_shape=[jax.ShapeDtypeStruct((B, Tp, SSD_INNER), BF16), jax.ShapeDtypeStruct((B, nc, 1024, 128), F32)],
        scratch_shapes=[pltpu.VMEM((1024, 128), F32)],
        name=name, compiler_params=_cparams(("arbitrary", "arbitrary")),
    )(xact, xact, xact, dtr, z, dt_bias, a_log, dskip, norm_w)


def _ssd_bwd(xact, dtr, z, dt_bias, a_log, dskip, norm_w, saved, dy, pad, name, after=None):
    B, Tp, _ = xact.shape
    nc = Tp // CHUNK

    def body(xs_ref, bm_ref, cm_ref, dt_ref, z_ref, db_ref, al_ref, ds_ref, nw_ref, sv_ref, dy_ref,
             dx_ref, ddt_ref, dz_ref, dpar_ref, dnw_ref, dst):
        b, i = pl.program_id(0), pl.program_id(1)
        c = nc - 1 - i

        @pl.when(i == 0)
        def _():
            dst[...] = jnp.zeros_like(dst)

        valid = _valid_rows(c, pad)
        fn = lambda *a: _ssd_chunk(*a, valid)
        _, vjp = jax.vjp(fn, xs_ref[...], bm_ref[...], cm_ref[...], dt_ref[...], z_ref[...].astype(F32), sv_ref[...],
                         db_ref[...], al_ref[...], ds_ref[...], nw_ref[...])
        dxs, dbm, dcm, ddt, dz, dstate, ddb, dal, dds, dnw = vjp((dy_ref[...].astype(F32), dst[...]))
        dx_ref[:, 0:1024] = dxs
        dx_ref[:, 1024:1536] = dbm
        dx_ref[:, 1536:2048] = dcm
        ddt_ref[...] = ddt
        dz_ref[...] = dz.astype(dz_ref.dtype)
        dst[...] = dstate

        @pl.when((b == 0) & (i == 0))
        def _():
            dpar_ref[...] = jnp.zeros_like(dpar_ref)
            dnw_ref[...] = jnp.zeros_like(dnw_ref)

        dpar_ref[0:1, :] += ddb
        dpar_ref[1:2, :] += dal
        dpar_ref[2:3, :] += dds
        dnw_ref[0:1, :] += dnw

    row = lambda w, off=0: pl.BlockSpec((None, CHUNK, w), lambda b, i: (b, nc - 1 - i, off))
    par = lambda w: pl.BlockSpec((1, w), lambda b, i: (0, 0))
    acc = lambda w: pl.BlockSpec((8, w), lambda b, i: (0, 0))
    in_specs = [row(1024, 0), row(512, 2), row(512, 3), row(128), row(1024), par(128), par(128), par(128), par(1024),
                pl.BlockSpec((None, None, 1024, 128), lambda b, i: (b, nc - 1 - i, 0, 0)), row(1024)]
    args = [xact, xact, xact, dtr, z, dt_bias, a_log, dskip, norm_w, saved, dy]
    if after is not None:
        body = _skip_ref(body, len(args))
        args.append(_deps(after))
        in_specs.append(_dep_spec(args[-1]))
    outs = pl.pallas_call(
        body, grid=(B, nc), in_specs=in_specs,
        out_specs=[row(2048), row(128), row(1024), acc(128), acc(1024)],
        out_shape=[jax.ShapeDtypeStruct((B, Tp, 2048), F32), jax.ShapeDtypeStruct((B, Tp, 128), F32),
                   jax.ShapeDtypeStruct((B, Tp, 1024), BF16), jax.ShapeDtypeStruct((8, 128), F32),
                   jax.ShapeDtypeStruct((8, 1024), F32)],
        scratch_shapes=[pltpu.VMEM((1024, 128), F32)],
        name=name, compiler_params=_cparams(("arbitrary", "arbitrary")),
    )(*args)
    return outs


@jax.custom_vjp
def _known(x, value):
    return value


_known.defvjp(lambda x, value: (value, None), lambda _, g: (g, jnp.zeros_like(g)))


def _hg_chunk(qr, fr, ir, gr, state_t, p0, p1, norm_w, valid, kept=None, keep=False):
    Q = qr.shape[0]
    known = (lambda x, i: x) if kept is None else (lambda x, i: _known(x, kept[i].astype(x.dtype)))
    lb = jax.nn.sigmoid(p0 - p1)
    f = lb + (1.0 - lb) * jax.nn.sigmoid(fr)
    k = 1.0 - f
    q = _silu(qr)
    v = ir * valid
    cum = known(_cumsum_rows(jnp.log(f)), 0)
    cum_end = _row_of(cum, Q - 1)
    o_inter = _mm_nt(q * jnp.exp(cum), state_t)
    nblk = Q // HG_SUB
    row = lax.broadcasted_iota(jnp.int32, (Q, 1), 0)
    ri = lax.broadcasted_iota(jnp.int32, (Q, Q), 0)
    ci = lax.broadcasted_iota(jnp.int32, (Q, Q), 1)
    mids = jnp.concatenate([jnp.broadcast_to(_row_of(cum, HG_SUB * i + HG_SUB // 2 - 1), (HG_SUB, cum.shape[1]))
                            for i in range(nblk)], axis=0)
    sh = HG_SUB.bit_length() - 1
    same = (jnp.right_shift(ri, sh) == jnp.right_shift(ci, sh)) & (ri >= ci)
    att = jnp.where(same, _mm_nt(q * jnp.exp(cum - mids), k * jnp.exp(mids - cum)), 0.0)
    for i in range(1, nblk):
        lo = HG_SUB * i
        start = _row_of(cum, lo - 1)
        qa = q * jnp.exp(jnp.where((row >= lo) & (row < lo + HG_SUB), cum - start, -1e30))
        ka = k * jnp.exp(jnp.where(row < lo, start - cum, -1e30))
        att = att + _mm_nt(qa, ka)
    att = known(att, 1)
    o = known(o_inter + _mm(att, v), 2)
    new_state_t = state_t * jnp.exp(cum_end) + _mm_tn(v, k * jnp.exp(cum_end - cum))
    if kept is not None:
        new_state_t = _known(new_state_t, state_t)
    y = o * lax.rsqrt(jnp.mean(o * o, axis=-1, keepdims=True) + EPS) * norm_w * _silu(gr)
    return (y, new_state_t, (cum, att, o)) if keep else (y, new_state_t)


HG_PER_STEP = 8
HG_COLS = 4 * 128


def _hg_fwd(qfig, lbh, nwh, pad, name):
    B, Tp, _ = qfig.shape
    nc = Tp // CHUNK
    hp = HG_PER_STEP

    def body(x_ref, lb_ref, nw_ref, y_ref, save_ref, cum_ref, att_ref, o_ref, st):
        c = pl.program_id(1)

        @pl.when(c == 0)
        def _():
            st[...] = jnp.zeros_like(st)

        valid = _valid_rows(c, pad)
        for j in range(hp):
            for b in range(B):
                s0 = st[j, b]
                save_ref[j, b] = s0
                col = lambda k: x_ref[b, :, HG_COLS * j + 128 * k:HG_COLS * j + 128 * (k + 1)]
                y, s1, (cum, att, o) = _hg_chunk(col(0), col(1), col(2), col(3), s0, lb_ref[j, 0:1, :], lb_ref[j, 1:2, :],
                                                 nw_ref[j], valid, keep=True)
                y_ref[b, :, 128 * j:128 * (j + 1)] = y.astype(y_ref.dtype)
                cum_ref[b, :, 128 * j:128 * (j + 1)] = cum
                att_ref[j, b] = att.astype(att_ref.dtype)
                o_ref[b, :, 128 * j:128 * (j + 1)] = o
                st[j, b] = s1

    rows = pl.BlockSpec((B, CHUNK, 128 * hp), lambda h, c: (0, c, h))
    per_chunk = pl.BlockSpec((hp, B, None, 128, 128), lambda h, c: (h, 0, c, 0, 0))
    return pl.pallas_call(
        body, grid=(HG_HEADS // hp, nc),
        in_specs=[pl.BlockSpec((B, CHUNK, HG_COLS * hp), lambda h, c: (0, c, h)),
                  pl.BlockSpec((hp, 2, 128), lambda h, c: (h, 0, 0)),
                  pl.BlockSpec((hp, 1, 128), lambda h, c: (h, 0, 0))],
        out_specs=[rows, per_chunk, rows, per_chunk, rows],
        out_shape=[jax.ShapeDtypeStruct((B, Tp, 1024), BF16), jax.ShapeDtypeStruct((HG_HEADS, B, nc, 128, 128), F32),
                   jax.ShapeDtypeStruct((B, Tp, 1024), F32), jax.ShapeDtypeStruct((HG_HEADS, B, nc, 128, 128), BF16),
                   jax.ShapeDtypeStruct((B, Tp, 1024), F32)],
        scratch_shapes=[pltpu.VMEM((hp, B, 128, 128), F32)],
        name=name, compiler_params=_cparams(("arbitrary", "arbitrary")),
    )(qfig, lbh, nwh)


def _hg_bwd(qfig, lbh, nwh, saved, kept, dy, pad, name, after=None):
    B, Tp, _ = qfig.shape
    nc = Tp // CHUNK
    hp = HG_PER_STEP

    def body(x_ref, lb_ref, nw_ref, sv_ref, cum_ref, att_ref, o_ref, dy_ref, dx_ref, dlb_ref, dnw_ref, dst):
        i = pl.program_id(1)
        c = nc - 1 - i

        @pl.when(i == 0)
        def _():
            dst[...] = jnp.zeros_like(dst)
            dlb_ref[...] = jnp.zeros_like(dlb_ref)
            dnw_ref[...] = jnp.zeros_like(dnw_ref)

        valid = _valid_rows(c, pad)
        for j in range(hp):
            for b in range(B):
                col = lambda k: x_ref[b, :, HG_COLS * j + 128 * k:HG_COLS * j + 128 * (k + 1)]
                head = slice(128 * j, 128 * (j + 1))
                kept_jb = (cum_ref[b, :, head], att_ref[j, b], o_ref[b, :, head])
                fn = lambda *a: _hg_chunk(*a, valid, kept=kept_jb)
                _, vjp = jax.vjp(fn, col(0), col(1), col(2), col(3), sv_ref[j, b], lb_ref[j, 0:1, :], lb_ref[j, 1:2, :], nw_ref[j])
                d4 = vjp((dy_ref[b, :, 128 * j:128 * (j + 1)].astype(F32), dst[j, b]))
                for k in range(4):
                    dx_ref[b, :, HG_COLS * j + 128 * k:HG_COLS * j + 128 * (k + 1)] = d4[k].astype(dx_ref.dtype)
                dst[j, b] = d4[4]
                dlb_ref[j, 0:1, :] += d4[5]
                dlb_ref[j, 1:2, :] += d4[6]
                dnw_ref[j, 0:1, :] += d4[7]

    acc = pl.BlockSpec((hp, 8, 128), lambda h, i: (h, 0, 0))
    rows = pl.BlockSpec((B, CHUNK, 128 * hp), lambda h, i: (0, nc - 1 - i, h))
    per_chunk = pl.BlockSpec((hp, B, None, 128, 128), lambda h, i: (h, 0, nc - 1 - i, 0, 0))
    in_specs = [pl.BlockSpec((B, CHUNK, HG_COLS * hp), lambda h, i: (0, nc - 1 - i, h)),
                pl.BlockSpec((hp, 2, 128), lambda h, i: (h, 0, 0)),
                pl.BlockSpec((hp, 1, 128), lambda h, i: (h, 0, 0)),
                per_chunk, rows, per_chunk, rows, rows]
    args = [qfig, lbh, nwh, saved, kept[0], kept[1], kept[2], dy]
    if after is not None:
        body = _skip_ref(body, len(args))
        args.append(_deps(after))
        in_specs.append(_dep_spec(args[-1]))
    return pl.pallas_call(
        body, grid=(HG_HEADS // hp, nc), in_specs=in_specs,
        out_specs=[pl.BlockSpec((B, CHUNK, HG_COLS * hp), lambda h, i: (0, nc - 1 - i, h)), acc, acc],
        out_shape=[jax.ShapeDtypeStruct((B, Tp, 4096), BF16), jax.ShapeDtypeStruct((HG_HEADS, 8, 128), F32),
                   jax.ShapeDtypeStruct((HG_HEADS, 8, 128), F32)],
        scratch_shapes=[pltpu.VMEM((hp, B, 128, 128), F32)],
        name=name, compiler_params=_cparams(("arbitrary", "arbitrary")),
    )(*args)


def _adamw_math(w, g, m, v):
    m = ADAM_B1 * m + (1.0 - ADAM_B1) * g
    v = ADAM_B2 * v + (1.0 - ADAM_B2) * (g * g)
    m_hat = m / (1.0 - ADAM_B1 ** ADAM_STEP)
    v_hat = v / (1.0 - ADAM_B2 ** ADAM_STEP)
    return -ADAM_LR * (m_hat / (jnp.sqrt(v_hat) + ADAM_EPS) + ADAM_WD * w), m, v


def _adamw_many(ws, gs, ms, vs, name):
    n = len(ws)

    def body(*refs):
        for i in range(n):
            d, m, v = _adamw_math(refs[i][...], refs[n + i][...], refs[2 * n + i][...], refs[3 * n + i][...])
            refs[4 * n + i][...] = d
            refs[5 * n + i][...] = m
            refs[6 * n + i][...] = v

    vm = pl.BlockSpec(memory_space=pltpu.VMEM)
    outs = pl.pallas_call(body, in_specs=[vm] * (4 * n), out_specs=[vm] * (3 * n),
                          out_shape=[jax.ShapeDtypeStruct(w.shape, F32) for w in ws] * 3, name=name)(*ws, *gs, *ms, *vs)
    return outs[:n], outs[n:2 * n], outs[2 * n:]


def _adamw(w, g, m, v, name, after=None):
    R, C = w.shape
    tr = max(t for t in range(8, R + 1, 8) if R % t == 0 and (t * C * 4 <= ADAMW_BLOCK_BYTES or t == 8))

    def body(w_ref, g_ref, m_ref, v_ref, d_ref, mo_ref, vo_ref):
        d_ref[...], mo_ref[...], vo_ref[...] = _adamw_math(w_ref[...], g_ref[...], m_ref[...], v_ref[...])

    sp = pl.BlockSpec((tr, C), lambda i: (i, 0))
    sh = jax.ShapeDtypeStruct((R, C), F32)
    in_specs, args = [sp] * 4, [w, g, m, v]
    if after is not None:
        body = _skip_ref(body, len(args))
        args.append(_deps(after))
        in_specs.append(_dep_spec(args[-1]))
    return pl.pallas_call(body, grid=(R // tr,), in_specs=in_specs, out_specs=[sp] * 3, out_shape=[sh] * 3,
                          name=name, compiler_params=_cparams(("arbitrary",)))(*args)


def _ffn_fwd(h, norm_w, w_gu, w_down, tag, after_norm=None, n=None, next_norm_w=None):
    if n is None:
        n = _rms_fwd(h, norm_w, f"{tag}_norm")
    if after_norm is not None:
        after_norm(n)
    gu, a = _gu_swiglu(n, w_gu, f"{tag}_gu")
    out = _residual_matmul(a, w_down, h, 0.5, f"{tag}_down", next_norm_w)
    return out, (n, gu, a)


def _ffn_bwd(h, norm_w, w_gu, w_down, saved, dout, tag, after_dw_down=None, token_seqs=None, told=None):
    n, gu, a = saved
    dgu = _d_swiglu(dout, w_down, gu, 0.5, f"{tag}_d_gu")
    dw_down = _matmul(a, dout, mode="tn", out_dtype=F32, alpha=0.5, name=f"{tag}_dw_down")
    dw_gu = _matmul(n, dgu, mode="tn", out_dtype=F32, out_groups=N_CHIPS, name=f"{tag}_dw_gu",
                    after=after_dw_down(dw_down) if after_dw_down else None)
    if token_seqs is None:
        dh, dnw = _d_norm_in(dgu, w_gu, h, norm_w, dout, f"{tag}_d_in", after=dw_gu)
    else:
        if told is not None:
            told("dw", (dw_gu, dw_down))
        dn = _matmul(dgu, w_gu, mode="nt", out_dtype=F32, name=f"{tag}_d_norm", after=dw_gu)
        dx, dm, dnw = _rms_bwd_tokens(h, norm_w, dn, dout, token_seqs, f"{tag}_d_in",
                                      after=told("d_norm", dn) if told is not None else None)
        dh = (dx, dm)
    return dh, dnw, dw_gu, dw_down


def _split_w_in(w_in_full):
    pts = [0]
    for s in IN_SIZES:
        pts.append(pts[-1] + s)
    sl = lambda i, j: w_in_full[:, pts[i]:pts[j]]
    qfig = sl(3, 7).reshape(D_MODEL, 4, HG_HEADS, 128).transpose(0, 2, 1, 3).reshape(D_MODEL, 4 * D_MODEL)
    return {"z": sl(0, 1), "xbc": sl(1, 2), "dt": jnp.pad(sl(2, 3), ((0, 0), (0, 128 - SSD_HEADS))),
            "qfig": qfig, "gates": sl(7, 9)}


def _local_step(x, target, W):
    B, S, _ = x.shape
    T = N_META + S
    pad = (-T) % CHUNK
    Tp = T + pad
    assert pad + N_META == CHUNK
    R = B * Tp
    meta = jnp.broadcast_to(W["meta_tokens"][None], (B, N_META, D_MODEL))
    h0 = jnp.concatenate([jnp.zeros((B, pad, D_MODEL), F32), meta, x], axis=1).reshape(R, D_MODEL)

    stage = W.get("_stage", lambda name, x: {})
    W = dict(W)
    (h1, um), sv1 = _ffn_fwd(h0, W["ffn1_norm"], W["ffn1_w_gu"], W["ffn1_w_down"], "ffn1",
                             lambda n: W.update(stage("ffn1_norm", n)), next_norm_w=W["mix_norm"])
    W.update(stage("ffn1_out", h1))
    wi = W["w_in"]
    z = _matmul(um, wi["z"], mode="nn", out_dtype=BF16, name="in_z")
    xbc = _matmul(um, wi["xbc"], mode="nn", out_dtype=F32, name="in_xbc")
    dtr = _matmul(um, wi["dt"], mode="nn", out_dtype=F32, name="in_dt")
    qfig = _matmul(um, wi["qfig"], mode="nn", out_dtype=F32, name="in_qfig")
    gates = _matmul(um, wi["gates"], mode="nn", out_dtype=BF16, name="in_gates")

    r3 = lambda t: t.reshape(B, Tp, t.shape[-1])
    lane_pad = lambda t: jnp.pad(t, ((0, 0), (0, 128 - t.shape[1])))
    dt_bias, a_log, dskip = lane_pad(W["ssd_dt_bias"]), lane_pad(W["ssd_a_log"]), lane_pad(W["ssd_d"])
    xact = _conv_fwd(r3(xbc), W["ssd_conv_w"], W["ssd_conv_b"], pad, "conv_fwd")
    ya, ssd_saved = _ssd_fwd(xact, r3(dtr), r3(z), dt_bias, a_log, dskip, W["ssd_norm"], pad, "ssd_fwd")
    lbh = W["hg_lower_bound"].reshape(2, HG_HEADS, 128).transpose(1, 0, 2)
    nwh = W["hg_norm"].reshape(HG_HEADS, 1, 128)
    yb, hg_saved, *hg_kept = _hg_fwd(r3(qfig), lbh, nwh, pad, "hg_fwd")
    ya2, yb2 = ya.reshape(R, -1), yb.reshape(R, -1)
    W.update(stage("mixers_out", yb2))
    pa, pb, mg = _branch_merge(ya2, yb2, W["w_branch_a"], W["w_branch_b"], gates, "branch_merge")
    h2, n2 = _residual_matmul(mg, W["w_out"], h1, 1.0, "mix_out", W["ffn2_norm"])
    h3, sv2 = _ffn_fwd(h2, W["ffn2_norm"], W["ffn2_w_gu"], W["ffn2_w_down"], "ffn2", n=n2)

    loss, dh3, d_final = _loss_head(h3, W["final_norm"].reshape(1, D_MODEL), target, B, "loss_head")

    G = {"final_norm": d_final[0]}
    dh2, dnw, G["ffn2_w_gu"], G["ffn2_w_down"] = _ffn_bwd(h2, W["ffn2_norm"], W["ffn2_w_gu"], W["ffn2_w_down"], sv2, dh3, "ffn2")
    G["ffn2_norm"] = dnw[0:1]
    dmg = _matmul(dh2, W["w_out"], mode="nt", out_dtype=BF16, name="d_merge")
    G["w_out"] = _matmul(mg, dh2, mode="tn", out_dtype=F32, name="dw_out")
    dpa, dpb, dgates, dya, dyb = _branch_merge_bwd(pa, pb, gates, dmg, W["w_branch_a"], W["w_branch_b"], "branch_merge_bwd")
    G["w_branch_a"] = _matmul(ya2, dpa, mode="tn", out_dtype=F32, name="dw_branch_a")
    G["w_branch_b"] = _matmul(yb2, dpb, mode="tn", out_dtype=F32, name="dw_branch_b")

    dxact, ddtr, dz, dpar, dnw = _ssd_bwd(xact, r3(dtr), r3(z), dt_bias, a_log, dskip, W["ssd_norm"], ssd_saved,
                                          r3(dya), pad, "ssd_bwd", after=stage("late_grads", G).get("_after"))
    G["ssd_dt_bias"], G["ssd_a_log"], G["ssd_d"] = dpar[0:1, :SSD_HEADS], dpar[1:2, :SSD_HEADS], dpar[2:3, :SSD_HEADS]
    G["ssd_norm"] = dnw[0:1]
    dxbc, dcw, dcb = _conv_bwd(r3(xbc), W["ssd_conv_w"], W["ssd_conv_b"], dxact, pad, "conv_bwd")
    G["ssd_conv_w"], G["ssd_conv_b"] = dcw[0:SSD_CONV], dcb[0:1]
    dqfig, dlb, dhn = _hg_bwd(r3(qfig), lbh, nwh, hg_saved, hg_kept, r3(dyb), pad, "hg_bwd",
                              after=stage("after_conv_bwd", dcb).get("_after"))
    G["hg_lower_bound"] = dlb[:, 0:2, :].transpose(1, 0, 2).reshape(2, D_MODEL)
    G["hg_norm"] = dhn[:, 0, :].reshape(1, D_MODEL)

    r2 = lambda t: t.reshape(R, t.shape[-1])
    pieces = [("z", r2(dz)), ("xbc", r2(dxbc)), ("dt", r2(ddtr)), ("qfig", r2(dqfig)), ("gates", dgates)]
    dum = _sum_nt([p for _, p in pieces], [wi[nm] for nm, _ in pieces], "d_mix")
    dwi = {nm: _matmul(um, dpiece, mode="tn", out_dtype=F32, name=f"dw_in_{nm}") for nm, dpiece in pieces}
    dw_qfig = dwi["qfig"].reshape(D_MODEL, HG_HEADS, 4, 128).transpose(0, 2, 1, 3).reshape(D_MODEL, 4 * D_MODEL)
    G["w_in"] = jnp.concatenate([dwi["z"], dwi["xbc"], dwi["dt"][:, :SSD_HEADS], dw_qfig, dwi["gates"]], axis=1)
    dh1, dnw = _rms_bwd(h1, W["mix_norm"], dum, dh2, "mix_norm_bwd", after=stage("w_in_grads", dwi).get("_after"))
    G["mix_norm"] = dnw[0:1]
    (dx, dfirst), dnw, G["ffn1_w_gu"], G["ffn1_w_down"] = _ffn_bwd(
        h0, W["ffn1_norm"], W["ffn1_w_gu"], W["ffn1_w_down"], sv1, dh1, "ffn1",
        lambda dw: stage("ffn1_dw_down", dw).get("_after"), token_seqs=B,
        told=lambda name, t: stage("ffn1_" + name, t).get("_after"))
    G["ffn1_norm"] = dnw[0:1]
    G["meta_tokens"] = jnp.sum(dfirst[:, pad:CHUNK], axis=0)
    return loss, dx, G


ANY = pl.BlockSpec(memory_space=pl.ANY)


def _place():
    return lax.axis_index("x"), lax.axis_index("y"), lax.axis_index("c")


def _other_chips(x, y):
    return [(1 - x, y), (x, 1 - y), (1 - x, 1 - y)]


def _remote(src, dst, ssem, rsem, dev):
    return pltpu.make_async_remote_copy(src_ref=src, dst_ref=dst, send_sem=ssem, recv_sem=rsem,
                                        device_id=dev, device_id_type=MESH)


def _exchange8(buf, name):
    n, w = buf.shape

    def body(x_ref, out_ref, ssem, rsem):
        x, y, c = _place()
        me = 4 * x + 2 * y + c
        out_ref[me] = x_ref[...]
        copies = []
        for k in range(1, 8):
            px = 1 - x if (k >> 2) & 1 else x
            py = 1 - y if (k >> 1) & 1 else y
            pc = 1 - c if k & 1 else c
            cp = _remote(x_ref, out_ref.at[me], ssem.at[k - 1], rsem.at[k - 1], (px, py, pc))
            cp.start()
            copies.append((cp, 4 * px + 2 * py + pc))
        for k, (cp, peer) in enumerate(copies):
            _remote(x_ref, out_ref.at[peer], ssem.at[k], rsem.at[k], (x, y, c)).wait_recv()
        for cp, _ in copies:
            cp.wait_send()

    vm = pl.BlockSpec(memory_space=pltpu.VMEM)
    return pl.pallas_call(
        body, in_specs=[vm], out_specs=vm, out_shape=jax.ShapeDtypeStruct((8, n, w), F32),
        scratch_shapes=[pltpu.SemaphoreType.DMA((7,)), pltpu.SemaphoreType.DMA((7,))], name=name,
    )(buf)


HBM = pltpu.MemorySpace.HBM


def _sequencer(name, collective_id, sems, sent):
    return functools.partial(pl.kernel, mesh=plsc.ScalarSubcoreMesh(axis_name="sequencer", num_cores=1), name=name,
                             scratch_types=sems, compiler_params=pltpu.CompilerParams(collective_id=collective_id),
                             cost_estimate=pl.CostEstimate(flops=0, transcendentals=0, bytes_accessed=2 * sent,
                                                           remote_bytes_transferred=sent))


def _nbytes(arrays):
    return sum(a.size * a.dtype.itemsize for a in arrays)


def _handshake(peers):
    barrier = pltpu.get_barrier_semaphore()
    for peer in peers:
        pl.semaphore_signal(barrier, inc=1, device_id=peer, device_id_type=MESH)
    pl.semaphore_wait(barrier, len(peers))


def _gather_seq(blocks, name, collective_id):
    n = len(blocks)
    half = [s.shape[1] // 2 for s in blocks]
    full = [jax.new_ref(b, memory_space=HBM) for b in blocks]

    @_sequencer(name, collective_id, [pltpu.SemaphoreType.DMA((n, 3))] * 4, _nbytes(blocks) * 3 // 4)
    def launch(ssem, rsem, fssem, frsem):
        x, y, c = _place()
        q = 2 * x + y
        chips = _other_chips(x, y)
        _handshake([(px, py, c) for px, py in chips] + [(x, y, 1 - c)])
        piece = lambda s, qq, cc: full[s].at[qq, pl.ds(cc * half[s], half[s])]
        sends = []
        for j, (px, py) in enumerate(chips):
            for s in range(n):
                cp = _remote(piece(s, q, c), piece(s, q, c), ssem.at[s, j], rsem.at[s, j], (px, py, c))
                cp.start()
                sends.append(cp)
        for j, (px, py) in enumerate(chips):
            for s in range(n):
                got = piece(s, 2 * px + py, c)
                _remote(got, got, ssem.at[s, j], rsem.at[s, j], (px, py, c)).wait_recv()
                cp = _remote(got, got, fssem.at[s, j], frsem.at[s, j], (x, y, 1 - c))
                cp.start()
                sends.append(cp)
        for j, (px, py) in enumerate(chips):
            for s in range(n):
                got = piece(s, 2 * px + py, 1 - c)
                _remote(got, got, fssem.at[s, j], frsem.at[s, j], (x, y, 1 - c)).wait_recv()
        for cp in sends:
            cp.wait_send()

    launch()
    return [r[...] for r in full]


def _share8(buf, name, collective_id):
    n, w = buf.shape
    src = jax.new_ref(buf, memory_space=HBM)
    out = jax.empty_ref(jax.ShapeDtypeStruct((8, n, w), F32), memory_space=HBM)

    @_sequencer(name, collective_id, [pltpu.SemaphoreType.DMA((7,)), pltpu.SemaphoreType.DMA((7,)), pltpu.SemaphoreType.DMA((1,))],
                7 * buf.size * 4)
    def launch(ssem, rsem, lsem):
        x, y, c = _place()
        me = 4 * x + 2 * y + c
        peers = [(1 - x if (k >> 2) & 1 else x, 1 - y if (k >> 1) & 1 else y, 1 - c if k & 1 else c) for k in range(1, 8)]
        _handshake(peers)
        mine = pltpu.make_async_copy(src, out.at[me], lsem.at[0])
        mine.start()
        sends = []
        for k, peer in enumerate(peers):
            cp = _remote(src, out.at[me], ssem.at[k], rsem.at[k], peer)
            cp.start()
            sends.append(cp)
        for k, (px, py, pc) in enumerate(peers):
            slot = out.at[4 * px + 2 * py + pc]
            _remote(slot, slot, ssem.at[k], rsem.at[k], (px, py, pc)).wait_recv()
        for cp in sends:
            cp.wait_send()
        mine.wait()

    launch()
    return out[...]


def _sum_slots(slots, name, after=None):
    _, n, w = slots.shape

    def body(s_ref, o_ref):
        acc = s_ref[0]
        for d in range(1, 8):
            acc = acc + s_ref[d]
        o_ref[...] = acc

    vm = pl.BlockSpec(memory_space=pltpu.VMEM)
    in_specs, args = [vm], [slots]
    if after is not None:
        body = _skip_ref(body, 1)
        args.append(_deps(after))
        in_specs.append(vm)
    return pl.pallas_call(body, in_specs=in_specs, out_specs=vm, out_shape=jax.ShapeDtypeStruct((n, w), F32), name=name)(*args)


def _pair_swap(parts, name, collective_id):
    n = len(parts)
    half = [p.shape[1] // 2 for p in parts]
    src = [jax.new_ref(p, memory_space=HBM) for p in parts]
    got = [jax.empty_ref(jax.ShapeDtypeStruct((p.shape[0], h, p.shape[2]), p.dtype), memory_space=HBM) for p, h in zip(parts, half)]

    @_sequencer(name, collective_id, [pltpu.SemaphoreType.DMA((n,))] * 2, _nbytes(parts) // 2)
    def launch(ssem, rsem):
        x, y, c = _place()
        _handshake([(x, y, 1 - c)])
        copies = []
        for s in range(n):
            cp = _remote(src[s].at[pl.ds(0, parts[s].shape[0]), pl.ds((1 - c) * half[s], half[s])], got[s], ssem.at[s], rsem.at[s], (x, y, 1 - c))
            cp.start()
            copies.append(cp)
        for cp in copies:
            cp.wait_recv()
        for cp in copies:
            cp.wait_send()

    launch()
    return [g[...] for g in got]


def _to_owners(sums, name, collective_id):
    n = len(sums)
    src = [jax.new_ref(s, memory_space=HBM) for s in sums]
    got = [jax.empty_ref(jax.ShapeDtypeStruct(s.shape, s.dtype), memory_space=HBM) for s in sums]

    @_sequencer(name, collective_id, [pltpu.SemaphoreType.DMA((n, 3))] * 2, _nbytes(sums) * 3 // 4)
    def launch(ssem, rsem):
        x, y, c = _place()
        q = 2 * x + y
        chips = _other_chips(x, y)
        _handshake([(px, py, c) for px, py in chips])
        sends = []
        for j, (px, py) in enumerate(chips):
            for s in range(n):
                cp = _remote(src[s].at[2 * px + py], got[s].at[q], ssem.at[s, j], rsem.at[s, j], (px, py, c))
                cp.start()
                sends.append(cp)
        for j, (px, py) in enumerate(chips):
            for s in range(n):
                slot = got[s].at[2 * px + py]
                _remote(slot, slot, ssem.at[s, j], rsem.at[s, j], (px, py, c)).wait_recv()
        for cp in sends:
            cp.wait_send()

    launch()
    return [g[...] for g in got]


def _pair_join(blocks, name, collective_id):
    n = len(blocks)
    out = [jax.new_ref(b, memory_space=HBM) for b in blocks]

    @_sequencer(name, collective_id, [pltpu.SemaphoreType.DMA((n,))] * 2, _nbytes(blocks) // 2)
    def launch(ssem, rsem):
        x, y, c = _place()
        _handshake([(x, y, 1 - c)])
        sends = []
        for s in range(n):
            h = blocks[s].shape[0] // 2
            mine = out[s].at[pl.ds(c * h, h)]
            cp = _remote(mine, mine, ssem.at[s], rsem.at[s], (x, y, 1 - c))
            cp.start()
            sends.append(cp)
        for s in range(n):
            h = blocks[s].shape[0] // 2
            theirs = out[s].at[pl.ds((1 - c) * h, h)]
            _remote(theirs, theirs, ssem.at[s], rsem.at[s], (x, y, 1 - c)).wait_recv()
        for cp in sends:
            cp.wait_send()

    launch()
    return [o[...] for o in out]


WIRE = BF16


def _row_tile(h):
    return _pick(h, (256, 368, 352, 128, 16))


def _add_pair(part, got, c, name, after=None):
    _, h, w = got.shape
    tr = _row_tile(h)
    nt = h // tr

    def body(c_ref, p_ref, g_ref, o_ref):
        o_ref[...] = (p_ref[...] + g_ref[...].astype(F32)).astype(o_ref.dtype)

    in_specs = [pl.BlockSpec((None, tr, w), lambda q, i, c_ref: (q, c_ref[0] * nt + i, 0)),
                pl.BlockSpec((None, tr, w), lambda q, i, c_ref: (q, i, 0))]
    args = [c.reshape(1).astype(jnp.int32), part, got]
    if after is not None:
        body = _skip_ref(body, len(args))
        args.append(_deps(after))
        in_specs.append(_dep_spec(args[-1]))
    return pl.pallas_call(
        body,
        grid_spec=pltpu.PrefetchScalarGridSpec(
            num_scalar_prefetch=1, grid=(got.shape[0], nt), in_specs=in_specs,
            out_specs=pl.BlockSpec((None, tr, w), lambda q, i, c_ref: (q, i, 0))),
        out_shape=jax.ShapeDtypeStruct(got.shape, WIRE), name=name,
        compiler_params=_cparams(("arbitrary", "arbitrary")),
    )(*args)


def _sum_chips(slots, sums, q, c, name, after=None):
    _, h, w = slots.shape
    tr = _row_tile(h)
    nt = h // tr

    def body(s_ref, mine_ref, a_ref, b_ref, d_ref, o_ref):
        o_ref[...] = ((mine_ref[...].astype(F32) + a_ref[...].astype(F32)) + b_ref[...].astype(F32)) + d_ref[...].astype(F32)

    slot = lambda k: pl.BlockSpec((None, tr, w), lambda i, s_ref: (s_ref[1 + k], i, 0))
    scalars = jnp.stack([c, q, (q + 1) % N_CHIPS, (q + 2) % N_CHIPS, (q + 3) % N_CHIPS]).astype(jnp.int32)
    in_specs, args = [slot(0), slot(1), slot(2), slot(3)], [scalars, sums, slots, slots, slots]
    if after is not None:
        body = _skip_ref(body, len(args))
        args.append(_deps(after))
        in_specs.append(_dep_spec(args[-1]))
    return pl.pallas_call(
        body,
        grid_spec=pltpu.PrefetchScalarGridSpec(
            num_scalar_prefetch=1, grid=(nt,), in_specs=in_specs,
            out_specs=pl.BlockSpec((tr, w), lambda i, s_ref: (s_ref[0] * nt + i, 0))),
        out_shape=jax.ShapeDtypeStruct((2 * h, w), F32), name=name,
        compiler_params=_cparams(("arbitrary",)),
    )(*args)


class _Reduce:
    def __init__(self, parts, q, c, tag, first_id, regions=None):
        self.parts, self.q, self.c, self.tag, self.first_id, self.regions = parts, q, c, tag, first_id, regions
        self.got = _pair_swap(parts, f"{tag}_pair_swap", first_id)

    def to_owners(self, after=None):
        self.sums = [_add_pair(p, g, self.c, f"{self.tag}_pair_add{i}", after)
                     for i, (p, g) in enumerate(zip(self.parts, self.got))]
        if self.regions is not None:
            self.sums = self.regions(self.sums)
        self.slots = _to_owners(self.sums, f"{self.tag}_to_owners", self.first_id + 1)
        return self.sums

    def join(self, after=None):
        blocks = [_sum_chips(sl, sm, self.q, self.c, f"{self.tag}_sum_chips{i}", after)
                  for i, (sl, sm) in enumerate(zip(self.slots, self.sums))]
        self.out = _pair_join(blocks, f"{self.tag}_pair_join", self.first_id + 2)
        return blocks


WEIGHTS = ("meta_tokens", "ffn1_norm", "ffn1_w_gu", "ffn1_w_down", "mix_norm", "w_in", "ssd_conv_w", "ssd_conv_b",
           "ssd_dt_bias", "ssd_a_log", "ssd_d", "ssd_norm", "hg_lower_bound", "hg_norm", "w_branch_a", "w_branch_b",
           "w_out", "ffn2_norm", "ffn2_w_gu", "ffn2_w_down", "final_norm")
BIG = ("ffn1_w_gu", "ffn1_w_down", "w_in", "w_branch_a", "w_branch_b", "w_out", "ffn2_w_gu", "ffn2_w_down")
SMALL = tuple(n for n in WEIGHTS if n not in BIG)


def _rows1024(a):
    flat = a.reshape(-1)
    n = -(-flat.shape[0] // 1024) * 1024
    return jnp.pad(flat, (0, n - flat.shape[0])).reshape(-1, 1024)


def kernel(x, meta_tokens, ffn1_norm, ffn1_w_gu, ffn1_w_down, mix_norm, w_in, ssd_conv_w, ssd_conv_b, ssd_dt_bias, ssd_a_log, ssd_d, ssd_norm, hg_lower_bound, hg_norm, w_branch_a, w_branch_b, w_out, ffn2_norm, ffn2_w_gu, ffn2_w_down, final_norm, loss_target, m_meta_tokens, m_ffn1_norm, m_ffn1_w_gu, m_ffn1_w_down, m_mix_norm, m_w_in, m_ssd_conv_w, m_ssd_conv_b, m_ssd_dt_bias, m_ssd_a_log, m_ssd_d, m_ssd_norm, m_hg_lower_bound, m_hg_norm, m_w_branch_a, m_w_branch_b, m_w_out, m_ffn2_norm, m_ffn2_w_gu, m_ffn2_w_down, m_final_norm, v_meta_tokens, v_ffn1_norm, v_ffn1_w_gu, v_ffn1_w_down, v_mix_norm, v_w_in, v_ssd_conv_w, v_ssd_conv_b, v_ssd_dt_bias, v_ssd_a_log, v_ssd_d, v_ssd_norm, v_hg_lower_bound, v_hg_norm, v_w_branch_a, v_w_branch_b, v_w_out, v_ffn2_norm, v_ffn2_w_gu, v_ffn2_w_down, v_final_norm):
    P = dict(zip(WEIGHTS, (meta_tokens, ffn1_norm, ffn1_w_gu, ffn1_w_down, mix_norm, w_in, ssd_conv_w, ssd_conv_b, ssd_dt_bias, ssd_a_log, ssd_d, ssd_norm, hg_lower_bound, hg_norm, w_branch_a, w_branch_b, w_out, ffn2_norm, ffn2_w_gu, ffn2_w_down, final_norm)))
    M = dict(zip(WEIGHTS, (m_meta_tokens, m_ffn1_norm, m_ffn1_w_gu, m_ffn1_w_down, m_mix_norm, m_w_in, m_ssd_conv_w, m_ssd_conv_b, m_ssd_dt_bias, m_ssd_a_log, m_ssd_d, m_ssd_norm, m_hg_lower_bound, m_hg_norm, m_w_branch_a, m_w_branch_b, m_w_out, m_ffn2_norm, m_ffn2_w_gu, m_ffn2_w_down, m_final_norm)))
    V = dict(zip(WEIGHTS, (v_meta_tokens, v_ffn1_norm, v_ffn1_w_gu, v_ffn1_w_down, v_mix_norm, v_w_in, v_ssd_conv_w, v_ssd_conv_b, v_ssd_dt_bias, v_ssd_a_log, v_ssd_d, v_ssd_norm, v_hg_lower_bound, v_hg_norm, v_w_branch_a, v_w_branch_b, v_w_out, v_ffn2_norm, v_ffn2_w_gu, v_ffn2_w_down, v_final_norm)))
    cx, cy, cc = _place()
    q = 2 * cx + cy

    mine = jnp.concatenate([meta_tokens.reshape(4, 1024), ssd_conv_w.reshape(2, 1024), jnp.zeros((2, 1024), F32)], axis=0)
    every = _exchange8(mine, "gather_small")
    meta_full = jnp.concatenate([every[2 * k, 0:4].reshape(N_META, 256) for k in range(N_CHIPS)], axis=1)
    conv_w_full = jnp.concatenate([every[2 * k, 4:6].reshape(SSD_CONV, 512) for k in range(N_CHIPS)], axis=1)

    late = ("ffn2_w_down", "w_branch_a", "w_branch_b", "w_out")
    rows = jnp.concatenate([P[n][0] for n in late], axis=0)
    zero = lambda t, dtype=F32: (t[0:1, 0:1] * 0).astype(dtype)

    def in_slot(s, after=None):
        s = s if after is None else s + zero(after)
        return lax.dynamic_update_slice(lax.empty((N_CHIPS,) + s.shape, BF16), s.astype(BF16)[None], (q, 0, 0))

    gu1, down1 = _gather_seq([in_slot(ffn1_w_gu[0]), in_slot(ffn1_w_down[0])], "gather_ffn1", 1)
    W = {n: P[n] for n in SMALL}
    W["meta_tokens"], W["ssd_conv_w"] = meta_full, conv_w_full
    W["ffn1_w_gu"], W["ffn1_w_down"] = gu1, down1.reshape(-1, D_MODEL)
    flying = {}

    def stage(name, t):
        if name == "ffn1_norm":
            flying["w_in"] = _gather_seq([in_slot(w_in[0], t)], "gather_w_in", 2)
            return {}
        if name == "ffn1_out":
            flying["late"] = _gather_seq([in_slot(ffn2_w_gu[0], t), in_slot(rows, t)], "gather_late", 3)
            (w_in_all,) = flying["w_in"]
            w_in_all = w_in_all + zero(t, BF16)
            return {"w_in": _split_w_in(w_in_all.transpose(1, 0, 2).reshape(D_MODEL, -1))}
        if name == "mixers_out":
            gu2, rows_all = flying["late"]
            out, r = {"ffn2_w_gu": gu2}, 0
            for n in late:
                nr = P[n].shape[1]
                out[n] = (rows_all[:, r:r + nr] + zero(t, BF16)).reshape(N_CHIPS * nr, D_MODEL)
                r += nr
            return out
        if name == "late_grads":
            parts = [t["ffn2_w_gu"]] + [t[n].reshape(N_CHIPS, -1, D_MODEL) for n in late]
            flying["grad_late"] = _Reduce(parts, q, cc, "grad_late", 4)
            return {"_after": [t["ffn2_w_gu"]] + [t[n] for n in late]}
        if name == "after_conv_bwd":
            return {"_after": flying["grad_late"].to_owners(after=t)}
        if name == "w_in_grads":
            order = ("z", "xbc", "dt", "qfig", "gates")
            blocks = flying["grad_late"].join(after=[t[k] for k in order])

            def regions(sums):
                z, xbc, dt, qfig, gates = [s[0] for s in sums]
                h = z.shape[0]
                qfig = qfig.reshape(h, HG_HEADS, 4, 128).transpose(0, 2, 1, 3).reshape(h, 4 * D_MODEL)
                cols = jnp.concatenate([z, xbc, dt[:, :SSD_HEADS], qfig, gates], axis=1)
                return [cols.reshape(h, N_CHIPS, -1).transpose(1, 0, 2)]

            flying["grad_w_in"] = _Reduce([t[k][None] for k in order], q, cc, "grad_w_in", 7, regions)
            return {"_after": blocks}
        if name == "ffn1_dw_down":
            return {"_after": flying["grad_w_in"].to_owners(after=t)}
        if name == "ffn1_dw":
            dw_gu, dw_down = t
            flying["grad_ffn1"] = _Reduce([dw_gu, dw_down.reshape(N_CHIPS, -1, D_MODEL)], q, cc, "grad_ffn1", 10)
            return {}
        if name == "ffn1_d_norm":
            blocks = flying["grad_w_in"].join(after=t)
            return {"_after": flying["grad_ffn1"].to_owners(after=blocks)}
        return {}

    W["_stage"] = stage

    loss8, grad_x, G = _local_step(x, loss_target, W)

    small = jnp.concatenate(
        [G["meta_tokens"]] + [_rows1024(G[n]) for n in SMALL if n != "meta_tokens"] + [_rows1024(loss8[0:1, 0:1])], axis=0)
    small = jnp.pad(small, ((0, 40 - small.shape[0]), (0, 0)))
    small_slots = _share8(small, "share_small", 13)

    grad_ffn1 = flying["grad_ffn1"]
    going = grad_ffn1.sums
    (g_w_in,) = flying["grad_w_in"].out
    Gb = dict(zip(("ffn2_w_gu",) + late, flying["grad_late"].out))
    Gb["w_in"] = g_w_in

    grads, delta, new_m, new_v, done = {}, {}, {}, {}, []
    cols = w_in.shape[2]
    to_tiles = lambda a: a.transpose(2, 0, 1).reshape(cols, 8, 128).reshape(cols * 8, 128)
    from_tiles = lambda a: a.reshape(cols, 1, D_MODEL).transpose(1, 2, 0)
    for n in [n for n in BIG if n in Gb]:
        if n == "w_in":
            g_t = to_tiles(Gb[n][None])
            d_, m_, v_ = _adamw(to_tiles(P[n]), g_t, to_tiles(M[n]), to_tiles(V[n]), f"adamw_{n}", after=going)
            grads[n], delta[n], new_m[n], new_v[n] = from_tiles(g_t), from_tiles(d_), from_tiles(m_), from_tiles(v_)
        else:
            d_, m_, v_ = _adamw(P[n][0], Gb[n], M[n][0], V[n][0], f"adamw_{n}", after=going)
            grads[n], delta[n], new_m[n], new_v[n] = Gb[n][None], d_[None], m_[None], v_[None]
        done.append(d_)

    small = _sum_slots(small_slots, "sum_small", after=done)
    Gs = {"meta_tokens": small[0:N_META]}
    r = N_META
    for n in SMALL:
        if n == "meta_tokens":
            continue
        nr = -(-G[n].size // 1024)
        Gs[n] = small[r:r + nr].reshape(-1)[:G[n].size].reshape(G[n].shape)
        r += nr
    loss = small[r, 0]
    Gs["meta_tokens"] = lax.dynamic_slice(Gs["meta_tokens"], (0, 256 * q), (N_META, 256))
    Gs["ssd_conv_w"] = lax.dynamic_slice(Gs["ssd_conv_w"], (0, 512 * q), (SSD_CONV, 512))[None]
    Gs = {n: Gs[n].reshape(P[n].shape) for n in SMALL}
    grads.update(Gs)
    flat = lambda a: a.reshape(-1, a.shape[-1])
    d_s, m_s, v_s = _adamw_many(*[[flat(D[n]) for n in SMALL] for D in (P, Gs, M, V)], "adamw_small")
    for i, n in enumerate(SMALL):
        delta[n], new_m[n], new_v[n] = d_s[i].reshape(P[n].shape), m_s[i].reshape(P[n].shape), v_s[i].reshape(P[n].shape)
    done.append(d_s[0])
    grad_ffn1.join(after=done)
    Gb["ffn1_w_gu"], Gb["ffn1_w_down"] = grad_ffn1.out
    for n in ("ffn1_w_gu", "ffn1_w_down"):
        d_, m_, v_ = _adamw(P[n][0], Gb[n], M[n][0], V[n][0], f"adamw_{n}")
        grads[n], delta[n], new_m[n], new_v[n] = Gb[n][None], d_[None], m_[None], v_[None]
    return (loss, grad_x, *[grads[n] for n in WEIGHTS], *[delta[n] for n in WEIGHTS],
            *[new_m[n] for n in WEIGHTS], *[new_v[n] for n in WEIGHTS])
```

```python
import functools

import jax
import jax.numpy as jnp
from jax import lax
from jax.experimental import pallas as pl
from jax.experimental.pallas import tpu as pltpu
from jax.experimental.pallas import tpu_sc as plsc

F32 = jnp.float32
BF16 = jnp.bfloat16
HIGHEST = lax.Precision.HIGHEST
MESH = pl.DeviceIdType.MESH

D_MODEL = 1024
N_META = 16
EPS = 1e-6
SSD_HEADS = 16
SSD_HEAD_DIM = 64
SSD_INNER = 1024
SSD_GROUPS = 4
SSD_STATE = 128
SSD_CONV = 4
SSD_CONV_CH = 2048
HG_HEADS = 8
HG_SUB = 32
CHUNK = 128
D_FF = 2816
N_CHIPS = 4
IN_SIZES = (1024, 2048, 16, 1024, 1024, 1024, 1024, 1024, 1024)
ADAM_LR = 0.001
ADAM_B1 = 0.9
ADAM_B2 = 0.999
ADAM_EPS = 1e-08
ADAM_WD = 0.01
ADAM_STEP = 10
VMEM_LIMIT = 56 * 1024 * 1024
MATMUL_BLOCK_BYTES = 42 * 1024 * 1024
ADAMW_BLOCK_BYTES = 5 * 512 * 1024


def _cparams(sem=None):
    return pltpu.CompilerParams(dimension_semantics=sem, vmem_limit_bytes=VMEM_LIMIT)


def _pick(n, cands):
    for c in cands:
        if n % c == 0:
            return c
    return n


def _deps(after):
    xs = after if isinstance(after, (list, tuple)) else [after]
    one = lambda x: lax.slice(x, (0,) * x.ndim, (1,) * x.ndim).reshape(1).astype(F32)
    return jnp.concatenate([one(x) for x in xs]).reshape(1, -1)


def _dep_spec(dep):
    return pl.BlockSpec(dep.shape, lambda *_: (0, 0))


def _skip_ref(body, pos):
    return lambda *refs: body(*refs[:pos], *refs[pos + 1:])


def _dg(a, b, ca, cb):
    return lax.dot_general(a.astype(BF16), b.astype(BF16), (((ca,), (cb,)), ((), ())), preferred_element_type=F32)


@jax.custom_vjp
def _mm(a, b):
    return _dg(a, b, 1, 0)


def _mm_fwd(a, b):
    return _dg(a, b, 1, 0), (a, b)


def _mm_bwd(r, g):
    a, b = r
    return _dg(g, b, 1, 1), _dg(a, g, 0, 0)


_mm.defvjp(_mm_fwd, _mm_bwd)


@jax.custom_vjp
def _mm_nt(a, b):
    return _dg(a, b, 1, 1)


def _mm_nt_fwd(a, b):
    return _dg(a, b, 1, 1), (a, b)


def _mm_nt_bwd(r, g):
    a, b = r
    return _dg(g, b, 1, 0), _dg(g, a, 0, 0)


_mm_nt.defvjp(_mm_nt_fwd, _mm_nt_bwd)


@jax.custom_vjp
def _mm_tn(a, b):
    return _dg(a, b, 0, 0)


def _mm_tn_fwd(a, b):
    return _dg(a, b, 0, 0), (a, b)


def _mm_tn_bwd(r, g):
    a, b = r
    return _dg(b, g, 1, 1), _dg(a, g, 1, 0)


_mm_tn.defvjp(_mm_tn_fwd, _mm_tn_bwd)


def _tri_sum(x, lower):
    n = x.shape[0]
    ri = lax.broadcasted_iota(jnp.int32, (n, n), 0)
    ci = lax.broadcasted_iota(jnp.int32, (n, n), 1)
    tri = ((ri >= ci) if lower else (ri <= ci)).astype(BF16)
    x1 = x.astype(BF16)
    r1 = x - x1.astype(F32)
    x2 = r1.astype(BF16)
    x3 = (r1 - x2.astype(F32)).astype(BF16)
    dot = lambda p: lax.dot_general(tri, p, (((1,), (0,)), ((), ())), preferred_element_type=F32)
    return (dot(x3) + dot(x2)) + dot(x1)


@jax.custom_vjp
def _cumsum_rows(x):
    return _tri_sum(x, True)


_cumsum_rows.defvjp(lambda x: (_tri_sum(x, True), None), lambda _, g: (_tri_sum(g, False),))


def _silu(x):
    return x * jax.nn.sigmoid(x)


def _softplus(x):
    return jnp.maximum(x, 0.0) + jnp.log(1.0 + jnp.exp(-jnp.abs(x)))


def _tril(n):
    ri = lax.broadcasted_iota(jnp.int32, (n, n), 0)
    ci = lax.broadcasted_iota(jnp.int32, (n, n), 1)
    return ri >= ci


def _row_of(m, r):
    sub = lax.broadcasted_iota(jnp.int32, (m.shape[0], 1), 0)
    return jnp.sum(jnp.where(sub == r, m, 0.0), axis=0, keepdims=True)


def _col_of(m, c):
    lane = lax.broadcasted_iota(jnp.int32, (1, m.shape[1]), 1)
    return jnp.sum(jnp.where(lane == c, m, 0.0), axis=1, keepdims=True)


def _matmul(a, b, *, mode, out_dtype, name, alpha=1.0, res=None, tm=None, tn=None, out_groups=None, after=None):
    b3 = b.ndim == 3
    if mode == "nn":
        M, K = a.shape
        G = b.shape[0] if b3 else 1
        Ng = b.shape[-1]
        N = G * Ng
    elif mode == "nt":
        M, K = a.shape
        G = b.shape[0] if b3 else 1
        N = b.shape[-2]
        Kg = b.shape[-1]
        assert G * Kg == K
    else:
        K, M = a.shape
        N = b.shape[1]
        G = out_groups or 1
        Ng = N // G
    has_res = res is not None
    split_n = (mode == "nn" and b3) or (mode == "tn" and G > 1)
    per_mn = jnp.dtype(out_dtype).itemsize + (res.dtype.itemsize if has_res else 0)
    fits = [(m_ * n_, m_, n_)
            for m_ in (4352, 2176, 1408, 1088, 1024, 544, 512, 256, 128) if M % m_ == 0
            for n_ in (2816, 2048, 1408, 1024, 512, 256, 128) if (Ng if split_n else N) % n_ == 0
            if 2 * (K * m_ * a.dtype.itemsize + K * n_ * b.dtype.itemsize + m_ * n_ * per_mn) + 4 * m_ * n_ <= MATMUL_BLOCK_BYTES]
    _, tm_fit, tn_fit = max(fits)
    tm, tn = tm or tm_fit, tn or tn_fit
    nm, nn_ = M // tm, N // tn
    assert nm * tm == M and nn_ * tn == N, (name, M, N, K, tm, tn)

    if mode == "nn":
        a_spec = pl.BlockSpec((tm, K), lambda i, j: (i, 0))
        if b3:
            ns = Ng // tn
            b_spec = pl.BlockSpec((None, K, tn), lambda i, j: (j // ns, 0, j % ns))
        else:
            b_spec = pl.BlockSpec((K, tn), lambda i, j: (0, j))
        ca, cb = 1, 0
    elif mode == "nt":
        a_spec = pl.BlockSpec((tm, K), lambda i, j: (i, 0))
        if b3:
            b_spec = pl.BlockSpec((G, tn, Kg), lambda i, j: (0, j, 0))
        else:
            b_spec = pl.BlockSpec((tn, K), lambda i, j: (j, 0))
        ca, cb = 1, 1
    else:
        a_spec = pl.BlockSpec((K, tm), lambda i, j: (0, i))
        b_spec = pl.BlockSpec((K, tn), lambda i, j: (0, j))
        ca, cb = 0, 0
    if mode == "tn" and G > 1:
        ns = Ng // tn
        o_spec = pl.BlockSpec((None, tm, tn), lambda i, j: (j // ns, i, j % ns))
        out_shape = jax.ShapeDtypeStruct((G, M, Ng), out_dtype)
    else:
        o_spec = pl.BlockSpec((tm, tn), lambda i, j: (i, j))
        out_shape = jax.ShapeDtypeStruct((M, N), out_dtype)
    in_specs = [a_spec, b_spec]
    args = [a, b]
    if has_res:
        in_specs.append(pl.BlockSpec((tm, tn), lambda i, j: (i, j)))
        args.append(res)
    if after is not None:
        args.append(_deps(after))
        in_specs.append(_dep_spec(args[-1]))

    def body(*refs):
        a_ref, b_ref, o_ref = refs[0], refs[1], refs[-1]
        if mode == "nt" and b3:
            o = _dg(a_ref[:, 0:Kg], b_ref[0], ca, cb)
            for g in range(1, G):
                o = o + _dg(a_ref[:, g * Kg:(g + 1) * Kg], b_ref[g], ca, cb)
        else:
            o = _dg(a_ref[...], b_ref[...], ca, cb)
        if alpha != 1.0:
            o = o * alpha
        if has_res:
            o = o + refs[2][...]
        o_ref[...] = o.astype(o_ref.dtype)

    return pl.pallas_call(
        body, grid=(nm, nn_), in_specs=in_specs, out_specs=o_spec, out_shape=out_shape, name=name,
        compiler_params=_cparams(("parallel", "parallel")),
    )(*args)


def _sum_nt(xs, ws, name):
    R, N = xs[0].shape[0], ws[0].shape[0]
    n = len(xs)
    per_m = sum(x.shape[1] * x.dtype.itemsize for x in xs)
    per_n = sum(w.shape[1] * w.dtype.itemsize for w in ws)
    fits = [(m_ * n_, m_, n_) for m_ in (1088, 544, 256, 128) if R % m_ == 0 for n_ in (1024, 512, 256, 128) if N % n_ == 0
            if 2 * (m_ * per_m + n_ * per_n + m_ * n_ * 4) + 4 * m_ * n_ <= MATMUL_BLOCK_BYTES]
    _, tm, tn = max(fits)

    def body(*refs):
        o = _dg(refs[0][...], refs[n][...], 1, 1)
        for p in range(1, n):
            o = o + _dg(refs[p][...], refs[n + p][...], 1, 1)
        refs[-1][...] = o

    return pl.pallas_call(
        body, grid=(R // tm, N // tn),
        in_specs=[pl.BlockSpec((tm, x.shape[1]), lambda i, j: (i, 0)) for x in xs]
        + [pl.BlockSpec((tn, w.shape[1]), lambda i, j: (j, 0)) for w in ws],
        out_specs=pl.BlockSpec((tm, tn), lambda i, j: (i, j)), out_shape=jax.ShapeDtypeStruct((R, N), F32), name=name,
        compiler_params=_cparams(("parallel", "parallel")),
    )(*xs, *ws)


def _rms_fn(h, w):
    r = lax.rsqrt(jnp.mean(h * h, axis=-1, keepdims=True) + EPS)
    return h * r * w


def _swiglu_fn(gu):
    g = gu[:, :D_FF].astype(F32)
    u = gu[:, D_FF:].astype(F32)
    return _silu(g) * u


def _merge_fn(pa, pb, gates):
    return jax.nn.sigmoid(gates[:, :D_MODEL]) * pa + jax.nn.sigmoid(gates[:, D_MODEL:]) * pb


def _rows_call(body, *, rows, tr, ins, outs, accs=(), name, after=None):
    n = rows // tr
    assert n * tr == rows
    if after is not None:
        body = _skip_ref(body, len(ins))
        ins = list(ins) + [("full", _deps(after))]

    def spec(x):
        if isinstance(x, tuple):
            shp = x[1].shape
            return pl.BlockSpec(shp, lambda i: (0,) * len(shp))
        return pl.BlockSpec((tr, x.shape[1]), lambda i: (i, 0))

    in_specs = [spec(x) for x in ins]
    args = [x[1] if isinstance(x, tuple) else x for x in ins]
    out_specs = [spec(x) for x in outs] + [pl.BlockSpec(x.shape, lambda i: (0,) * len(x.shape)) for x in accs]
    out_shape = [x[1] if isinstance(x, tuple) else x for x in outs] + list(accs)
    return pl.pallas_call(
        body, grid=(n,), in_specs=in_specs, out_specs=out_specs, out_shape=out_shape, name=name,
        compiler_params=_cparams(("arbitrary",)),
    )(*args)


def _acc_rows(ref, val):
    @pl.when(pl.program_id(0) == 0)
    def _():
        ref[...] = jnp.zeros_like(ref)

    ref[0:1, :] += val


def _rms_fwd(h, w, name):
    def body(h_ref, w_ref, o_ref):
        o_ref[...] = _rms_fn(h_ref[...], w_ref[...]).astype(o_ref.dtype)

    R = h.shape[0]
    return _rows_call(body, rows=R, tr=_pick(R, (256, 128)), ins=[h, ("full", w)],
                      outs=[jax.ShapeDtypeStruct(h.shape, BF16)], name=name)[0]


def _rms_bwd(h, w, dn, dres, name, after=None):
    def body(h_ref, w_ref, dn_ref, dres_ref, dh_ref, dw_ref):
        _, vjp = jax.vjp(_rms_fn, h_ref[...], w_ref[...])
        dh, dw = vjp(dn_ref[...].astype(F32))
        dh_ref[...] = dh + dres_ref[...]
        _acc_rows(dw_ref, dw)

    R = h.shape[0]
    return _rows_call(body, rows=R, tr=_pick(R, (256, 128)), ins=[h, ("full", w), dn, dres],
                      outs=[jax.ShapeDtypeStruct(h.shape, F32)], accs=[jax.ShapeDtypeStruct((8, D_MODEL), F32)], name=name,
                      after=after)


def _d_norm_in(dgu, w_gu, h, norm_w, dres, name, after=None):
    R = h.shape[0]
    G, _, kg = w_gu.shape

    def body(dgu_ref, w_ref, h_ref, nw_ref, dres_ref, dh_ref, dw_ref):
        dn = _dg(dgu_ref[:, 0:kg], w_ref[0], 1, 1)
        for g in range(1, G):
            dn = dn + _dg(dgu_ref[:, kg * g:kg * (g + 1)], w_ref[g], 1, 1)
        _, vjp = jax.vjp(_rms_fn, h_ref[...], nw_ref[...])
        dh, dw = vjp(dn)
        dh_ref[...] = dh + dres_ref[...]
        _acc_rows(dw_ref, dw)

    return _rows_call(body, rows=R, tr=_pick(R, (256, 128)), ins=[dgu, ("full", w_gu), h, ("full", norm_w), dres],
                      outs=[jax.ShapeDtypeStruct(h.shape, F32)], accs=[jax.ShapeDtypeStruct((8, D_MODEL), F32)], name=name,
                      after=after)


def _rms_bwd_tokens(h, w, dn, dres, nseq, name, after=None, w_gu=None):
    Tp = h.shape[0] // nseq
    nc = Tp // CHUNK

    def body(h_ref, w_ref, dn_ref, dres_ref, *rest):
        dx_ref, dm_ref, dw_ref = rest[-3:]
        b, c = pl.program_id(0), pl.program_id(1)
        if w_gu is None:
            dn_ = dn_ref[...].astype(F32)
        else:
            kg = w_gu.shape[2]
            dn_ = _dg(dn_ref[:, 0:kg], rest[0][0], 1, 1)
            for g in range(1, w_gu.shape[0]):
                dn_ = dn_ + _dg(dn_ref[:, kg * g:kg * (g + 1)], rest[0][g], 1, 1)
        _, vjp = jax.vjp(_rms_fn, h_ref[...], w_ref[...])
        dh, dw = vjp(dn_)
        dh = dh + dres_ref[...]

        @pl.when(c == 0)
        def _():
            dm_ref[...] = dh

        @pl.when(c > 0)
        def _():
            dx_ref[...] = dh

        @pl.when((b == 0) & (c == 0))
        def _():
            dw_ref[...] = jnp.zeros_like(dw_ref)

        dw_ref[0:1, :] += dw

    rows = pl.BlockSpec((CHUNK, D_MODEL), lambda b, c: (b * nc + c, 0))
    dn_rows = pl.BlockSpec((CHUNK, dn.shape[1]), lambda b, c: (b * nc + c, 0))
    in_specs, args = [rows, pl.BlockSpec((1, D_MODEL), lambda b, c: (0, 0)), dn_rows, rows], [h, w, dn, dres]
    if w_gu is not None:
        args.append(w_gu)
        in_specs.append(pl.BlockSpec(w_gu.shape, lambda b, c: (0, 0, 0)))
    if after is not None:
        body = _skip_ref(body, len(args))
        args.append(_deps(after))
        in_specs.append(_dep_spec(args[-1]))
    return pl.pallas_call(
        body, grid=(nseq, nc), in_specs=in_specs,
        out_specs=[pl.BlockSpec((None, CHUNK, D_MODEL), lambda b, c: (b, jnp.maximum(c - 1, 0), 0)),
                   pl.BlockSpec((None, CHUNK, D_MODEL), lambda b, c: (b, 0, 0)),
                   pl.BlockSpec((8, D_MODEL), lambda b, c: (0, 0))],
        out_shape=[jax.ShapeDtypeStruct((nseq, Tp - CHUNK, D_MODEL), F32), jax.ShapeDtypeStruct((nseq, CHUNK, D_MODEL), F32),
                   jax.ShapeDtypeStruct((8, D_MODEL), F32)],
        name=name, compiler_params=_cparams(("arbitrary", "arbitrary")),
    )(*args)


def _gu_swiglu(n, w_gu, name):
    R = n.shape[0]
    G, _, ng = w_gu.shape

    def body(n_ref, w_ref, gu_ref, a_ref):
        x = n_ref[...]
        for r in range(G):
            gu_ref[:, ng * r:ng * (r + 1)] = _dg(x, w_ref[r], 1, 0).astype(gu_ref.dtype)
        a_ref[...] = _swiglu_fn(gu_ref[...]).astype(a_ref.dtype)

    return _rows_call(body, rows=R, tr=_pick(R, (256, 128)), ins=[n, ("full", w_gu)],
                      outs=[jax.ShapeDtypeStruct((R, 2 * D_FF), BF16), jax.ShapeDtypeStruct((R, D_FF), BF16)], name=name)


def _d_swiglu(dout, w_down, gu, alpha, name):
    R = gu.shape[0]

    def body(do_ref, w_ref, gu_ref, o_ref):
        da = _dg(do_ref[...] * alpha, w_ref[...], 1, 1)
        g = gu_ref[:, :D_FF].astype(F32)
        u = gu_ref[:, D_FF:].astype(F32)
        s = jax.nn.sigmoid(g)
        t = g * s
        o_ref[:, :D_FF] = (da * u * (s + t - t * s)).astype(o_ref.dtype)
        o_ref[:, D_FF:] = (da * t).astype(o_ref.dtype)

    return _rows_call(body, rows=R, tr=_pick(R, (256, 128)), ins=[dout, ("full", w_down), gu],
                      outs=[jax.ShapeDtypeStruct(gu.shape, BF16)], name=name)[0]


def _residual_matmul(a, w, res, alpha, name, norm_w=None):
    R, K = a.shape

    def body(a_ref, w_ref, r_ref, *rest):
        out = r_ref[...] + alpha * _dg(a_ref[...], w_ref[...], 1, 0)
        if norm_w is None:
            rest[0][...] = out
        else:
            rest[1][...] = out
            rest[2][...] = _rms_fn(out, rest[0][...]).astype(rest[2].dtype)

    f32 = jax.ShapeDtypeStruct((R, D_MODEL), F32)
    ins = [a, ("full", w), res] + ([] if norm_w is None else [("full", norm_w)])
    outs = [f32] + ([] if norm_w is None else [jax.ShapeDtypeStruct((R, D_MODEL), BF16)])
    got = _rows_call(body, rows=R, tr=_pick(R, (544, 256, 128)), ins=ins, outs=outs, name=name)
    return got[0] if norm_w is None else (got[0], got[1])


def _branch_merge(ya, yb, wa, wb, gates, name):
    def body(ya_ref, yb_ref, wa_ref, wb_ref, g_ref, pa_ref, pb_ref, o_ref):
        pa = _dg(ya_ref[...], wa_ref[...], 1, 0)
        pb = _dg(yb_ref[...], wb_ref[...], 1, 0)
        pa_ref[...] = pa
        pb_ref[...] = pb
        o_ref[...] = _merge_fn(pa, pb, g_ref[...].astype(F32)).astype(o_ref.dtype)

    R = ya.shape[0]
    f32 = jax.ShapeDtypeStruct((R, D_MODEL), F32)
    return _rows_call(body, rows=R, tr=_pick(R, (544, 256, 128)), ins=[ya, yb, ("full", wa), ("full", wb), gates],
                      outs=[f32, f32, jax.ShapeDtypeStruct((R, D_MODEL), BF16)], name=name)


def _branch_merge_bwd(pa, pb, gates, dm, wa, wb, name):
    def body(pa_ref, pb_ref, g_ref, dm_ref, wa_ref, wb_ref, dpa_ref, dpb_ref, dg_ref, dya_ref, dyb_ref):
        _, vjp = jax.vjp(_merge_fn, pa_ref[...], pb_ref[...], g_ref[...].astype(F32))
        dpa, dpb, dg = vjp(dm_ref[...].astype(F32))
        dpa_ref[...] = dpa.astype(dpa_ref.dtype)
        dpb_ref[...] = dpb.astype(dpb_ref.dtype)
        dg_ref[...] = dg.astype(dg_ref.dtype)
        dya_ref[...] = _dg(dpa, wa_ref[...], 1, 1).astype(dya_ref.dtype)
        dyb_ref[...] = _dg(dpb, wb_ref[...], 1, 1).astype(dyb_ref.dtype)

    R = pa.shape[0]
    b16 = jax.ShapeDtypeStruct(pa.shape, BF16)
    return _rows_call(body, rows=R, tr=_pick(R, (544, 256, 128)), ins=[pa, pb, gates, dm, ("full", wa), ("full", wb)],
                      outs=[b16, b16, jax.ShapeDtypeStruct(gates.shape, BF16), b16, b16], name=name)


def _loss_head(h3, w, target, nseq, name):
    Tp = h3.shape[0] // nseq
    nc = Tp // CHUNK

    def fn(h, w_, t, valid):
        y = _rms_fn(h, w_)
        e = (y - t) * valid
        return 0.5 * jnp.sum(jnp.mean(e * e, axis=-1, keepdims=True))

    def body(h_ref, w_ref, t_ref, loss_ref, dh_ref, dw_ref):
        b, c = pl.program_id(0), pl.program_id(1)
        valid = (c >= 1).astype(F32)
        t = t_ref[...]
        loss, vjp = jax.vjp(lambda h, w_: fn(h, w_, t, valid), h_ref[...], w_ref[...])
        dh, dw = vjp(jnp.ones((), F32))
        dh_ref[...] = dh

        @pl.when((b == 0) & (c == 0))
        def _():
            loss_ref[...] = jnp.zeros_like(loss_ref)
            dw_ref[...] = jnp.zeros_like(dw_ref)

        loss_ref[...] += jnp.full(loss_ref.shape, loss, F32)
        dw_ref[0:1, :] += dw

    return pl.pallas_call(
        body, grid=(nseq, nc),
        in_specs=[pl.BlockSpec((CHUNK, D_MODEL), lambda b, c: (b * nc + c, 0)),
                  pl.BlockSpec((1, D_MODEL), lambda b, c: (0, 0)),
                  pl.BlockSpec((None, CHUNK, D_MODEL), lambda b, c: (b, jnp.maximum(c - 1, 0), 0))],
        out_specs=[pl.BlockSpec((8, 128), lambda b, c: (0, 0)),
                   pl.BlockSpec((CHUNK, D_MODEL), lambda b, c: (b * nc + c, 0)),
                   pl.BlockSpec((8, D_MODEL), lambda b, c: (0, 0))],
        out_shape=[jax.ShapeDtypeStruct((8, 128), F32), jax.ShapeDtypeStruct(h3.shape, F32),
                   jax.ShapeDtypeStruct((8, D_MODEL), F32)],
        name=name, compiler_params=_cparams(("arbitrary", "arbitrary")),
    )(h3, w, target)


CONV_TILE = 512
CONV_HALO = 8


def _conv_fwd(xbc, w, b, pad, name):
    B, Tp, C = xbc.shape
    nch = Tp // CHUNK

    def body(x_ref, w_ref, b_ref, o_ref, xp):
        xp[0:CONV_HALO, :] = jnp.zeros((CONV_HALO, CONV_TILE), F32)
        xp[CONV_HALO:, :] = x_ref[...]
        for c in range(nch):
            acc = jnp.zeros((CHUNK, CONV_TILE), F32) + b_ref[...]
            for k in range(SSD_CONV):
                acc = acc + w_ref[k:k + 1, :] * xp[pl.ds(CONV_HALO + CHUNK * c - (SSD_CONV - 1) + k, CHUNK), :]
            out = _silu(acc)
            if CHUNK * c < pad:
                row = CHUNK * c + lax.broadcasted_iota(jnp.int32, (CHUNK, 1), 0)
                out = jnp.where(row >= pad, out, 0.0)
            o_ref[pl.ds(CHUNK * c, CHUNK), :] = out

    return pl.pallas_call(
        body, grid=(B, C // CONV_TILE),
        in_specs=[pl.BlockSpec((None, Tp, CONV_TILE), lambda i, j: (i, 0, j)),
                  pl.BlockSpec((SSD_CONV, CONV_TILE), lambda i, j: (0, j)),
                  pl.BlockSpec((1, CONV_TILE), lambda i, j: (0, j))],
        out_specs=pl.BlockSpec((None, Tp, CONV_TILE), lambda i, j: (i, 0, j)),
        out_shape=jax.ShapeDtypeStruct(xbc.shape, F32),
        scratch_shapes=[pltpu.VMEM((Tp + CONV_HALO, CONV_TILE), F32)],
        name=name, compiler_params=_cparams(("arbitrary", "arbitrary")),
    )(xbc, w, b)


def _conv_bwd(xbc, w, b, dact, pad, name):
    B, Tp, C = xbc.shape
    nch = Tp // CHUNK

    def body(x_ref, w_ref, b_ref, da_ref, dx_ref, dw_ref, db_ref, xp, dp):
        bi = pl.program_id(1)
        xp[0:CONV_HALO, :] = jnp.zeros((CONV_HALO, CONV_TILE), F32)
        xp[CONV_HALO:, :] = x_ref[...]
        dp[pl.ds(Tp, CONV_HALO), :] = jnp.zeros((CONV_HALO, CONV_TILE), F32)
        dws = [jnp.zeros((1, CONV_TILE), F32) for _ in range(SSD_CONV)]
        dbs = jnp.zeros((1, CONV_TILE), F32)
        for c in range(nch):
            xs = [xp[pl.ds(CONV_HALO + CHUNK * c - (SSD_CONV - 1) + k, CHUNK), :] for k in range(SSD_CONV)]
            acc = jnp.zeros((CHUNK, CONV_TILE), F32) + b_ref[...]
            for k in range(SSD_CONV):
                acc = acc + w_ref[k:k + 1, :] * xs[k]
            sg = jax.nn.sigmoid(acc)
            t = acc * sg
            dpre = da_ref[pl.ds(CHUNK * c, CHUNK), :] * (sg + t - t * sg)
            if CHUNK * c < pad:
                row = CHUNK * c + lax.broadcasted_iota(jnp.int32, (CHUNK, 1), 0)
                dpre = jnp.where(row >= pad, dpre, 0.0)
            dp[pl.ds(CHUNK * c, CHUNK), :] = dpre
            dbs = dbs + jnp.sum(dpre, axis=0, keepdims=True)
            for k in range(SSD_CONV):
                dws[k] = dws[k] + jnp.sum(dpre * xs[k], axis=0, keepdims=True)
        for c in range(nch):
            acc = jnp.zeros((CHUNK, CONV_TILE), F32)
            for k in range(SSD_CONV):
                acc = acc + w_ref[k:k + 1, :] * dp[pl.ds(CHUNK * c + (SSD_CONV - 1) - k, CHUNK), :]
            dx_ref[pl.ds(CHUNK * c, CHUNK), :] = acc.astype(dx_ref.dtype)

        @pl.when(bi == 0)
        def _():
            dw_ref[...] = jnp.zeros_like(dw_ref)
            db_ref[...] = jnp.zeros_like(db_ref)

        for k in range(SSD_CONV):
            dw_ref[k:k + 1, :] += dws[k]
        db_ref[0:1, :] += dbs

    return pl.pallas_call(
        body, grid=(C // CONV_TILE, B),
        in_specs=[pl.BlockSpec((None, Tp, CONV_TILE), lambda j, i: (i, 0, j)),
                  pl.BlockSpec((SSD_CONV, CONV_TILE), lambda j, i: (0, j)),
                  pl.BlockSpec((1, CONV_TILE), lambda j, i: (0, j)),
                  pl.BlockSpec((None, Tp, CONV_TILE), lambda j, i: (i, 0, j))],
        out_specs=[pl.BlockSpec((None, Tp, CONV_TILE), lambda j, i: (i, 0, j)),
                   pl.BlockSpec((8, CONV_TILE), lambda j, i: (0, j)),
                   pl.BlockSpec((8, CONV_TILE), lambda j, i: (0, j))],
        out_shape=[jax.ShapeDtypeStruct(xbc.shape, BF16), jax.ShapeDtypeStruct((8, C), F32),
                   jax.ShapeDtypeStruct((8, C), F32)],
        scratch_shapes=[pltpu.VMEM((Tp + CONV_HALO, CONV_TILE), F32), pltpu.VMEM((Tp + CONV_HALO, CONV_TILE), F32)],
        name=name, compiler_params=_cparams(("arbitrary", "arbitrary")),
    )(xbc, w, b, dact)


def _ssd_chunk(xs, bm, cm, dtr, z, state, dt_bias, a_log, dskip, norm_w, valid, kept=None, keep=False):
    Q = xs.shape[0]
    known = (lambda x, v: x) if kept is None else _known
    lane = lax.broadcasted_iota(jnp.int32, (1, 128), 1)
    dt = jnp.where(lane < SSD_HEADS, _softplus(dtr + dt_bias), 0.0) * valid
    a = dt * (-jnp.exp(a_log))
    tril = _tril(Q)
    cs = known(_cumsum_rows(a), None if kept is None else kept[0])
    cs_t = cs.T
    cs_end = _row_of(cs, Q - 1)
    low = lane < SSD_HEAD_DIM
    low_rows = lax.broadcasted_iota(jnp.int32, (128, 1), 0) < SSD_HEAD_DIM
    ys, new_state, cbs = [], [], []
    for g in range(SSD_GROUPS):
        bg = bm[:, 128 * g:128 * (g + 1)]
        cg = cm[:, 128 * g:128 * (g + 1)]
        cb = known(_mm_nt(cg, bg), None if kept is None else kept[1][Q * g:Q * (g + 1)])
        cbs.append(cb)
        for pr in range(2):
            p = 2 * g + pr
            h0, h1 = 2 * p, 2 * p + 1
            xp = xs[:, 128 * p:128 * (p + 1)]
            c0, c1 = _col_of(cs, h0), _col_of(cs, h1)
            e0, e1 = _col_of(cs_end, h0), _col_of(cs_end, h1)
            xd = xp * jnp.where(low, _col_of(dt, h0), _col_of(dt, h1))
            l0 = jnp.exp(jnp.where(tril, c0 - _row_of(cs_t, h0), -1e30))
            l1 = jnp.exp(jnp.where(tril, c1 - _row_of(cs_t, h1), -1e30))
            y_diag = jnp.where(low, _mm(cb * l0, xd), _mm(cb * l1, xd))
            to_end = jnp.where(low, jnp.exp(e0 - c0), jnp.exp(e1 - c1))
            sp = state[128 * p:128 * (p + 1), :]
            y_off = _mm_nt(cg, sp) * jnp.where(low, jnp.exp(c0), jnp.exp(c1))
            new_state.append(sp * jnp.where(low_rows, jnp.exp(e0), jnp.exp(e1)) + _mm_tn(xd * to_end, bg))
            ys.append(y_diag + y_off + xp * jnp.where(low, _col_of(dskip, h0), _col_of(dskip, h1)))
    y_raw = known(jnp.concatenate(ys, axis=1), None if kept is None else kept[2])
    y = y_raw * _silu(z)
    gw = SSD_INNER // SSD_GROUPS
    outs = []
    for g in range(SSD_GROUPS):
        blk = y[:, gw * g:gw * (g + 1)]
        outs.append(blk * lax.rsqrt(jnp.mean(blk * blk, axis=-1, keepdims=True) + EPS))
    out, state_out = jnp.concatenate(outs, axis=1) * norm_w, jnp.concatenate(new_state, axis=0)
    if kept is not None:
        state_out = _known(state_out, state)
    return (out, state_out, (cs, jnp.concatenate(cbs, axis=0), y_raw)) if keep else (out, state_out)


def _valid_rows(c, pad):
    row = c * CHUNK + lax.broadcasted_iota(jnp.int32, (CHUNK, 1), 0)
    return (row >= pad).astype(F32)


def _ssd_fwd(xact, dtr, z, dt_bias, a_log, dskip, norm_w, pad, name):
    B, Tp, _ = xact.shape
    nc = Tp // CHUNK

    def body(xs_ref, bm_ref, cm_ref, dt_ref, z_ref, db_ref, al_ref, ds_ref, nw_ref, y_ref, save_ref, cs_ref, cb_ref, yr_ref, st):
        c = pl.program_id(1)

        @pl.when(c == 0)
        def _():
            st[...] = jnp.zeros_like(st)

        s0 = st[...]
        save_ref[...] = s0
        y, s1, (cs, cb, y_raw) = _ssd_chunk(xs_ref[...], bm_ref[...], cm_ref[...], dt_ref[...], z_ref[...].astype(F32), s0,
                                            db_ref[...], al_ref[...], ds_ref[...], nw_ref[...], _valid_rows(c, pad), keep=True)
        y_ref[...] = y.astype(y_ref.dtype)
        cs_ref[...] = cs
        cb_ref[...] = cb
        yr_ref[...] = y_raw
        st[...] = s1

    row = lambda w, off=0: pl.BlockSpec((None, CHUNK, w), lambda b, c: (b, c, off))
    par = lambda w: pl.BlockSpec((1, w), lambda b, c: (0, 0))
    per_chunk = lambda r: pl.BlockSpec((None, None, r, 128), lambda b, c: (b, c, 0, 0))
    return pl.pallas_call(
        body, grid=(B, nc),
        in_specs=[row(1024, 0), row(512, 2), row(512, 3), row(128), row(1024), par(128), par(128), par(128), par(1024)],
        out_specs=[row(1024), per_chunk(1024), row(128), per_chunk(SSD_GROUPS * CHUNK), row(1024)],
        out_shape=[jax.ShapeDtypeStruct((B, Tp, SSD_INNER), BF16), jax.ShapeDtypeStruct((B, nc, 1024, 128), F32),
                   jax.ShapeDtypeStruct((B, Tp, 128), F32), jax.ShapeDtypeStruct((B, nc, SSD_GROUPS * CHUNK, 128), F32),
                   jax.ShapeDtypeStruct((B, Tp, SSD_INNER), F32)],
        scratch_shapes=[pltpu.VMEM((1024, 128), F32)],
        name=name, compiler_params=_cparams(("arbitrary", "arbitrary")),
    )(xact, xact, xact, dtr, z, dt_bias, a_log, dskip, norm_w)


def _ssd_bwd(xact, dtr, z, dt_bias, a_log, dskip, norm_w, saved, kept, dy, pad, name, after=None):
    B, Tp, _ = xact.shape
    nc = Tp // CHUNK

    def body(xs_ref, bm_ref, cm_ref, dt_ref, z_ref, db_ref, al_ref, ds_ref, nw_ref, sv_ref, cs_ref, cb_ref, yr_ref, dy_ref,
             dx_ref, ddt_ref, dz_ref, dpar_ref, dnw_ref, dst):
        b, i = pl.program_id(0), pl.program_id(1)
        c = nc - 1 - i

        @pl.when(i == 0)
        def _():
            dst[...] = jnp.zeros_like(dst)

        valid = _valid_rows(c, pad)
        kept_c = (cs_ref[...], cb_ref[...], yr_ref[...])
        fn = lambda *a: _ssd_chunk(*a, valid, kept=kept_c)
        _, vjp = jax.vjp(fn, xs_ref[...], bm_ref[...], cm_ref[...], dt_ref[...], z_ref[...].astype(F32), sv_ref[...],
                         db_ref[...], al_ref[...], ds_ref[...], nw_ref[...])
        dxs, dbm, dcm, ddt, dz, dstate, ddb, dal, dds, dnw = vjp((dy_ref[...].astype(F32), dst[...]))
        dx_ref[:, 0:1024] = dxs
        dx_ref[:, 1024:1536] = dbm
        dx_ref[:, 1536:2048] = dcm
        ddt_ref[...] = ddt
        dz_ref[...] = dz.astype(dz_ref.dtype)
        dst[...] = dstate

        @pl.when((b == 0) & (i == 0))
        def _():
            dpar_ref[...] = jnp.zeros_like(dpar_ref)
            dnw_ref[...] = jnp.zeros_like(dnw_ref)

        dpar_ref[0:1, :] += ddb
        dpar_ref[1:2, :] += dal
        dpar_ref[2:3, :] += dds
        dnw_ref[0:1, :] += dnw

    row = lambda w, off=0: pl.BlockSpec((None, CHUNK, w), lambda b, i: (b, nc - 1 - i, off))
    par = lambda w: pl.BlockSpec((1, w), lambda b, i: (0, 0))
    acc = lambda w: pl.BlockSpec((8, w), lambda b, i: (0, 0))
    per_chunk = lambda r: pl.BlockSpec((None, None, r, 128), lambda b, i: (b, nc - 1 - i, 0, 0))
    in_specs = [row(1024, 0), row(512, 2), row(512, 3), row(128), row(1024), par(128), par(128), par(128), par(1024),
                per_chunk(1024), row(128), per_chunk(SSD_GROUPS * CHUNK), row(1024), row(1024)]
    args = [xact, xact, xact, dtr, z, dt_bias, a_log, dskip, norm_w, saved, kept[0], kept[1], kept[2], dy]
    if after is not None:
        body = _skip_ref(body, len(args))
        args.append(_deps(after))
        in_specs.append(_dep_spec(args[-1]))
    outs = pl.pallas_call(
        body, grid=(B, nc), in_specs=in_specs,
        out_specs=[row(2048), row(128), row(1024), acc(128), acc(1024)],
        out_shape=[jax.ShapeDtypeStruct((B, Tp, 2048), F32), jax.ShapeDtypeStruct((B, Tp, 128), F32),
                   jax.ShapeDtypeStruct((B, Tp, 1024), BF16), jax.ShapeDtypeStruct((8, 128), F32),
                   jax.ShapeDtypeStruct((8, 1024), F32)],
        scratch_shapes=[pltpu.VMEM((1024, 128), F32)],
        name=name, compiler_params=_cparams(("arbitrary", "arbitrary")),
    )(*args)
    return outs


@jax.custom_vjp
def _known(x, value):
    return value


_known.defvjp(lambda x, value: (value, None), lambda _, g: (g, jnp.zeros_like(g)))


def _hg_chunk(qr, fr, ir, gr, state_t, p0, p1, norm_w, valid, kept=None, keep=False):
    Q = qr.shape[0]
    known = (lambda x, i: x) if kept is None else (lambda x, i: _known(x, kept[i].astype(x.dtype)))
    lb = jax.nn.sigmoid(p0 - p1)
    f = lb + (1.0 - lb) * jax.nn.sigmoid(fr)
    k = 1.0 - f
    q = _silu(qr)
    v = ir * valid
    cum = known(_cumsum_rows(jnp.log(f)), 0)
    cum_end = _row_of(cum, Q - 1)
    o_inter = _mm_nt(q * jnp.exp(cum), state_t)
    nblk = Q // HG_SUB
    row = lax.broadcasted_iota(jnp.int32, (Q, 1), 0)
    ri = lax.broadcasted_iota(jnp.int32, (Q, Q), 0)
    ci = lax.broadcasted_iota(jnp.int32, (Q, Q), 1)
    mids = jnp.concatenate([jnp.broadcast_to(_row_of(cum, HG_SUB * i + HG_SUB // 2 - 1), (HG_SUB, cum.shape[1]))
                            for i in range(nblk)], axis=0)
    sh = HG_SUB.bit_length() - 1
    same = (jnp.right_shift(ri, sh) == jnp.right_shift(ci, sh)) & (ri >= ci)
    att = jnp.where(same, _mm_nt(q * jnp.exp(cum - mids), k * jnp.exp(mids - cum)), 0.0)
    for i in range(1, nblk):
        lo = HG_SUB * i
        start = _row_of(cum, lo - 1)
        qa = q * jnp.exp(jnp.where((row >= lo) & (row < lo + HG_SUB), cum - start, -1e30))
        ka = k * jnp.exp(jnp.where(row < lo, start - cum, -1e30))
        att = att + _mm_nt(qa, ka)
    att = known(att, 1)
    o = known(o_inter + _mm(att, v), 2)
    new_state_t = state_t * jnp.exp(cum_end) + _mm_tn(v, k * jnp.exp(cum_end - cum))
    if kept is not None:
        new_state_t = _known(new_state_t, state_t)
    y = o * lax.rsqrt(jnp.mean(o * o, axis=-1, keepdims=True) + EPS) * norm_w * _silu(gr)
    return (y, new_state_t, (cum, att, o)) if keep else (y, new_state_t)


HG_PER_STEP = 8
HG_COLS = 4 * 128


def _hg_fwd(qfig, lbh, nwh, pad, name):
    B, Tp, _ = qfig.shape
    nc = Tp // CHUNK
    hp = HG_PER_STEP

    def body(x_ref, lb_ref, nw_ref, y_ref, save_ref, cum_ref, att_ref, o_ref, st):
        c = pl.program_id(1)

        @pl.when(c == 0)
        def _():
            st[...] = jnp.zeros_like(st)

        valid = _valid_rows(c, pad)
        for j in range(hp):
            for b in range(B):
                s0 = st[j, b]
                save_ref[j, b] = s0
                col = lambda k: x_ref[b, :, HG_COLS * j + 128 * k:HG_COLS * j + 128 * (k + 1)]
                y, s1, (cum, att, o) = _hg_chunk(col(0), col(1), col(2), col(3), s0, lb_ref[j, 0:1, :], lb_ref[j, 1:2, :],
                                                 nw_ref[j], valid, keep=True)
                y_ref[b, :, 128 * j:128 * (j + 1)] = y.astype(y_ref.dtype)
                cum_ref[b, :, 128 * j:128 * (j + 1)] = cum
                att_ref[j, b] = att.astype(att_ref.dtype)
                o_ref[b, :, 128 * j:128 * (j + 1)] = o
                st[j, b] = s1

    rows = pl.BlockSpec((B, CHUNK, 128 * hp), lambda h, c: (0, c, h))
    per_chunk = pl.BlockSpec((hp, B, None, 128, 128), lambda h, c: (h, 0, c, 0, 0))
    return pl.pallas_call(
        body, grid=(HG_HEADS // hp, nc),
        in_specs=[pl.BlockSpec((B, CHUNK, HG_COLS * hp), lambda h, c: (0, c, h)),
                  pl.BlockSpec((hp, 2, 128), lambda h, c: (h, 0, 0)),
                  pl.BlockSpec((hp, 1, 128), lambda h, c: (h, 0, 0))],
        out_specs=[rows, per_chunk, rows, per_chunk, rows],
        out_shape=[jax.ShapeDtypeStruct((B, Tp, 1024), BF16), jax.ShapeDtypeStruct((HG_HEADS, B, nc, 128, 128), F32),
                   jax.ShapeDtypeStruct((B, Tp, 1024), F32), jax.ShapeDtypeStruct((HG_HEADS, B, nc, 128, 128), BF16),
                   jax.ShapeDtypeStruct((B, Tp, 1024), F32)],
        scratch_shapes=[pltpu.VMEM((hp, B, 128, 128), F32)],
        name=name, compiler_params=_cparams(("arbitrary", "arbitrary")),
    )(qfig, lbh, nwh)


def _hg_bwd(qfig, lbh, nwh, saved, kept, dy, pad, name, after=None):
    B, Tp, _ = qfig.shape
    nc = Tp // CHUNK
    hp = HG_PER_STEP

    def body(x_ref, lb_ref, nw_ref, sv_ref, cum_ref, att_ref, o_ref, dy_ref, dx_ref, dlb_ref, dnw_ref, dst):
        i = pl.program_id(1)
        c = nc - 1 - i

        @pl.when(i == 0)
        def _():
            dst[...] = jnp.zeros_like(dst)
            dlb_ref[...] = jnp.zeros_like(dlb_ref)
            dnw_ref[...] = jnp.zeros_like(dnw_ref)

        valid = _valid_rows(c, pad)
        for j in range(hp):
            for b in range(B):
                col = lambda k: x_ref[b, :, HG_COLS * j + 128 * k:HG_COLS * j + 128 * (k + 1)]
                head = slice(128 * j, 128 * (j + 1))
                kept_jb = (cum_ref[b, :, head], att_ref[j, b], o_ref[b, :, head])
                fn = lambda *a: _hg_chunk(*a, valid, kept=kept_jb)
                _, vjp = jax.vjp(fn, col(0), col(1), col(2), col(3), sv_ref[j, b], lb_ref[j, 0:1, :], lb_ref[j, 1:2, :], nw_ref[j])
                d4 = vjp((dy_ref[b, :, 128 * j:128 * (j + 1)].astype(F32), dst[j, b]))
                for k in range(4):
                    dx_ref[b, :, HG_COLS * j + 128 * k:HG_COLS * j + 128 * (k + 1)] = d4[k].astype(dx_ref.dtype)
                dst[j, b] = d4[4]
                dlb_ref[j, 0:1, :] += d4[5]
                dlb_ref[j, 1:2, :] += d4[6]
                dnw_ref[j, 0:1, :] += d4[7]

    acc = pl.BlockSpec((hp, 8, 128), lambda h, i: (h, 0, 0))
    rows = pl.BlockSpec((B, CHUNK, 128 * hp), lambda h, i: (0, nc - 1 - i, h))
    per_chunk = pl.BlockSpec((hp, B, None, 128, 128), lambda h, i: (h, 0, nc - 1 - i, 0, 0))
    in_specs = [pl.BlockSpec((B, CHUNK, HG_COLS * hp), lambda h, i: (0, nc - 1 - i, h)),
                pl.BlockSpec((hp, 2, 128), lambda h, i: (h, 0, 0)),
                pl.BlockSpec((hp, 1, 128), lambda h, i: (h, 0, 0)),
                per_chunk, rows, per_chunk, rows, rows]
    args = [qfig, lbh, nwh, saved, kept[0], kept[1], kept[2], dy]
    if after is not None:
        body = _skip_ref(body, len(args))
        args.append(_deps(after))
        in_specs.append(_dep_spec(args[-1]))
    return pl.pallas_call(
        body, grid=(HG_HEADS // hp, nc), in_specs=in_specs,
        out_specs=[pl.BlockSpec((B, CHUNK, HG_COLS * hp), lambda h, i: (0, nc - 1 - i, h)), acc, acc],
        out_shape=[jax.ShapeDtypeStruct((B, Tp, 4096), BF16), jax.ShapeDtypeStruct((HG_HEADS, 8, 128), F32),
                   jax.ShapeDtypeStruct((HG_HEADS, 8, 128), F32)],
        scratch_shapes=[pltpu.VMEM((hp, B, 128, 128), F32)],
        name=name, compiler_params=_cparams(("arbitrary", "arbitrary")),
    )(*args)


def _adamw_math(w, g, m, v):
    m = ADAM_B1 * m + (1.0 - ADAM_B1) * g
    v = ADAM_B2 * v + (1.0 - ADAM_B2) * (g * g)
    m_hat = m / (1.0 - ADAM_B1 ** ADAM_STEP)
    v_hat = v / (1.0 - ADAM_B2 ** ADAM_STEP)
    return -ADAM_LR * (m_hat / (jnp.sqrt(v_hat) + ADAM_EPS) + ADAM_WD * w), m, v


def _adamw_many(ws, gs, ms, vs, name):
    n = len(ws)

    def body(*refs):
        for i in range(n):
            d, m, v = _adamw_math(refs[i][...], refs[n + i][...], refs[2 * n + i][...], refs[3 * n + i][...])
            refs[4 * n + i][...] = d
            refs[5 * n + i][...] = m
            refs[6 * n + i][...] = v

    vm = pl.BlockSpec(memory_space=pltpu.VMEM)
    outs = pl.pallas_call(body, in_specs=[vm] * (4 * n), out_specs=[vm] * (3 * n),
                          out_shape=[jax.ShapeDtypeStruct(w.shape, F32) for w in ws] * 3, name=name)(*ws, *gs, *ms, *vs)
    return outs[:n], outs[n:2 * n], outs[2 * n:]


def _adamw(w, g, m, v, name, after=None):
    R, C = w.shape
    tr = max(t for t in range(8, R + 1, 8) if R % t == 0 and (t * C * 4 <= ADAMW_BLOCK_BYTES or t == 8))

    def body(w_ref, g_ref, m_ref, v_ref, d_ref, mo_ref, vo_ref):
        d_ref[...], mo_ref[...], vo_ref[...] = _adamw_math(w_ref[...], g_ref[...], m_ref[...], v_ref[...])

    sp = pl.BlockSpec((tr, C), lambda i: (i, 0))
    sh = jax.ShapeDtypeStruct((R, C), F32)
    in_specs, args = [sp] * 4, [w, g, m, v]
    if after is not None:
        body = _skip_ref(body, len(args))
        args.append(_deps(after))
        in_specs.append(_dep_spec(args[-1]))
    return pl.pallas_call(body, grid=(R // tr,), in_specs=in_specs, out_specs=[sp] * 3, out_shape=[sh] * 3,
                          name=name, compiler_params=_cparams(("arbitrary",)))(*args)


def _ffn_fwd(h, norm_w, w_gu, w_down, tag, after_norm=None, n=None, next_norm_w=None):
    if n is None:
        n = _rms_fwd(h, norm_w, f"{tag}_norm")
    if after_norm is not None:
        after_norm(n)
    gu, a = _gu_swiglu(n, w_gu, f"{tag}_gu")
    out = _residual_matmul(a, w_down, h, 0.5, f"{tag}_down", next_norm_w)
    return out, (n, gu, a)


def _ffn_bwd(h, norm_w, w_gu, w_down, saved, dout, tag, after_dw_down=None, token_seqs=None, told=None):
    n, gu, a = saved
    dgu = _d_swiglu(dout, w_down, gu, 0.5, f"{tag}_d_gu")
    dw_down = _matmul(a, dout, mode="tn", out_dtype=F32, alpha=0.5, name=f"{tag}_dw_down")
    dw_gu = _matmul(n, dgu, mode="tn", out_dtype=F32, out_groups=N_CHIPS, name=f"{tag}_dw_gu",
                    after=after_dw_down(dw_down) if after_dw_down else None)
    if token_seqs is None:
        dh, dnw = _d_norm_in(dgu, w_gu, h, norm_w, dout, f"{tag}_d_in", after=dw_gu)
    else:
        if told is not None:
            told("dw", (dw_gu, dw_down))
        dx, dm, dnw = _rms_bwd_tokens(h, norm_w, dgu, dout, token_seqs, f"{tag}_d_in", after=dw_gu, w_gu=w_gu)
        if told is not None:
            told("d_in", dx)
        dh = (dx, dm)
    return dh, dnw, dw_gu, dw_down


def _split_w_in(w_in_full):
    pts = [0]
    for s in IN_SIZES:
        pts.append(pts[-1] + s)
    sl = lambda i, j: w_in_full[:, pts[i]:pts[j]]
    qfig = sl(3, 7).reshape(D_MODEL, 4, HG_HEADS, 128).transpose(0, 2, 1, 3).reshape(D_MODEL, 4 * D_MODEL)
    return {"z": sl(0, 1), "xbc": sl(1, 2), "dt": jnp.pad(sl(2, 3), ((0, 0), (0, 128 - SSD_HEADS))),
            "qfig": qfig, "gates": sl(7, 9)}


def _local_step(x, target, W):
    B, S, _ = x.shape
    T = N_META + S
    pad = (-T) % CHUNK
    Tp = T + pad
    assert pad + N_META == CHUNK
    R = B * Tp
    meta = jnp.broadcast_to(W["meta_tokens"][None], (B, N_META, D_MODEL))
    h0 = jnp.concatenate([jnp.zeros((B, pad, D_MODEL), F32), meta, x], axis=1).reshape(R, D_MODEL)

    stage = W.get("_stage", lambda name, x: {})
    W = dict(W)
    (h1, um), sv1 = _ffn_fwd(h0, W["ffn1_norm"], W["ffn1_w_gu"], W["ffn1_w_down"], "ffn1",
                             lambda n: W.update(stage("ffn1_norm", n)), next_norm_w=W["mix_norm"])
    W.update(stage("ffn1_out", h1))
    wi = W["w_in"]
    z = _matmul(um, wi["z"], mode="nn", out_dtype=BF16, name="in_z")
    xbc = _matmul(um, wi["xbc"], mode="nn", out_dtype=F32, name="in_xbc")
    dtr = _matmul(um, wi["dt"], mode="nn", out_dtype=F32, name="in_dt")
    qfig = _matmul(um, wi["qfig"], mode="nn", out_dtype=F32, name="in_qfig")
    gates = _matmul(um, wi["gates"], mode="nn", out_dtype=BF16, name="in_gates")

    r3 = lambda t: t.reshape(B, Tp, t.shape[-1])
    lane_pad = lambda t: jnp.pad(t, ((0, 0), (0, 128 - t.shape[1])))
    dt_bias, a_log, dskip = lane_pad(W["ssd_dt_bias"]), lane_pad(W["ssd_a_log"]), lane_pad(W["ssd_d"])
    xact = _conv_fwd(r3(xbc), W["ssd_conv_w"], W["ssd_conv_b"], pad, "conv_fwd")
    ya, ssd_saved, *ssd_kept = _ssd_fwd(xact, r3(dtr), r3(z), dt_bias, a_log, dskip, W["ssd_norm"], pad, "ssd_fwd")
    lbh = W["hg_lower_bound"].reshape(2, HG_HEADS, 128).transpose(1, 0, 2)
    nwh = W["hg_norm"].reshape(HG_HEADS, 1, 128)
    yb, hg_saved, *hg_kept = _hg_fwd(r3(qfig), lbh, nwh, pad, "hg_fwd")
    ya2, yb2 = ya.reshape(R, -1), yb.reshape(R, -1)
    W.update(stage("mixers_out", yb2))
    pa, pb, mg = _branch_merge(ya2, yb2, W["w_branch_a"], W["w_branch_b"], gates, "branch_merge")
    h2, n2 = _residual_matmul(mg, W["w_out"], h1, 1.0, "mix_out", W["ffn2_norm"])
    h3, sv2 = _ffn_fwd(h2, W["ffn2_norm"], W["ffn2_w_gu"], W["ffn2_w_down"], "ffn2", n=n2)

    loss, dh3, d_final = _loss_head(h3, W["final_norm"].reshape(1, D_MODEL), target, B, "loss_head")

    G = {"final_norm": d_final[0]}
    dh2, dnw, G["ffn2_w_gu"], G["ffn2_w_down"] = _ffn_bwd(h2, W["ffn2_norm"], W["ffn2_w_gu"], W["ffn2_w_down"], sv2, dh3, "ffn2")
    G["ffn2_norm"] = dnw[0:1]
    dmg = _matmul(dh2, W["w_out"], mode="nt", out_dtype=BF16, name="d_merge")
    G["w_out"] = _matmul(mg, dh2, mode="tn", out_dtype=F32, name="dw_out")
    dpa, dpb, dgates, dya, dyb = _branch_merge_bwd(pa, pb, gates, dmg, W["w_branch_a"], W["w_branch_b"], "branch_merge_bwd")
    G["w_branch_a"] = _matmul(ya2, dpa, mode="tn", out_dtype=F32, name="dw_branch_a")
    G["w_branch_b"] = _matmul(yb2, dpb, mode="tn", out_dtype=F32, name="dw_branch_b")

    dxact, ddtr, dz, dpar, dnw = _ssd_bwd(xact, r3(dtr), r3(z), dt_bias, a_log, dskip, W["ssd_norm"], ssd_saved, ssd_kept,
                                          r3(dya), pad, "ssd_bwd", after=stage("late_grads", G).get("_after"))
    G["ssd_dt_bias"], G["ssd_a_log"], G["ssd_d"] = dpar[0:1, :SSD_HEADS], dpar[1:2, :SSD_HEADS], dpar[2:3, :SSD_HEADS]
    G["ssd_norm"] = dnw[0:1]
    dxbc, dcw, dcb = _conv_bwd(r3(xbc), W["ssd_conv_w"], W["ssd_conv_b"], dxact, pad, "conv_bwd")
    G["ssd_conv_w"], G["ssd_conv_b"] = dcw[0:SSD_CONV], dcb[0:1]
    dqfig, dlb, dhn = _hg_bwd(r3(qfig), lbh, nwh, hg_saved, hg_kept, r3(dyb), pad, "hg_bwd",
                              after=stage("after_conv_bwd", dcb).get("_after"))
    G["hg_lower_bound"] = dlb[:, 0:2, :].transpose(1, 0, 2).reshape(2, D_MODEL)
    G["hg_norm"] = dhn[:, 0, :].reshape(1, D_MODEL)

    r2 = lambda t: t.reshape(R, t.shape[-1])
    pieces = [("z", r2(dz)), ("xbc", r2(dxbc)), ("dt", r2(ddtr)), ("qfig", r2(dqfig)), ("gates", dgates)]
    dum = _sum_nt([p for _, p in pieces], [wi[nm] for nm, _ in pieces], "d_mix")
    dwi = {nm: _matmul(um, dpiece, mode="tn", out_dtype=F32, name=f"dw_in_{nm}") for nm, dpiece in pieces}
    dw_qfig = dwi["qfig"].reshape(D_MODEL, HG_HEADS, 4, 128).transpose(0, 2, 1, 3).reshape(D_MODEL, 4 * D_MODEL)
    G["w_in"] = jnp.concatenate([dwi["z"], dwi["xbc"], dwi["dt"][:, :SSD_HEADS], dw_qfig, dwi["gates"]], axis=1)
    dh1, dnw = _rms_bwd(h1, W["mix_norm"], dum, dh2, "mix_norm_bwd", after=stage("w_in_grads", dwi).get("_after"))
    G["mix_norm"] = dnw[0:1]
    (dx, dfirst), dnw, G["ffn1_w_gu"], G["ffn1_w_down"] = _ffn_bwd(
        h0, W["ffn1_norm"], W["ffn1_w_gu"], W["ffn1_w_down"], sv1, dh1, "ffn1",
        lambda dw: stage("ffn1_dw_down", dw).get("_after"), token_seqs=B,
        told=lambda name, t: stage("ffn1_" + name, t).get("_after"))
    G["ffn1_norm"] = dnw[0:1]
    G["meta_tokens"] = jnp.sum(dfirst[:, pad:CHUNK], axis=0)
    return loss, dx, G


ANY = pl.BlockSpec(memory_space=pl.ANY)


def _place():
    return lax.axis_index("x"), lax.axis_index("y"), lax.axis_index("c")


def _other_chips(x, y):
    return [(1 - x, y), (x, 1 - y), (1 - x, 1 - y)]


def _remote(src, dst, ssem, rsem, dev):
    return pltpu.make_async_remote_copy(src_ref=src, dst_ref=dst, send_sem=ssem, recv_sem=rsem,
                                        device_id=dev, device_id_type=MESH)


def _exchange8(buf, name):
    n, w = buf.shape

    def body(x_ref, out_ref, ssem, rsem):
        x, y, c = _place()
        me = 4 * x + 2 * y + c
        out_ref[me] = x_ref[...]
        copies = []
        for k in range(1, 8):
            px = 1 - x if (k >> 2) & 1 else x
            py = 1 - y if (k >> 1) & 1 else y
            pc = 1 - c if k & 1 else c
            cp = _remote(x_ref, out_ref.at[me], ssem.at[k - 1], rsem.at[k - 1], (px, py, pc))
            cp.start()
            copies.append((cp, 4 * px + 2 * py + pc))
        for k, (cp, peer) in enumerate(copies):
            _remote(x_ref, out_ref.at[peer], ssem.at[k], rsem.at[k], (x, y, c)).wait_recv()
        for cp, _ in copies:
            cp.wait_send()

    vm = pl.BlockSpec(memory_space=pltpu.VMEM)
    return pl.pallas_call(
        body, in_specs=[vm], out_specs=vm, out_shape=jax.ShapeDtypeStruct((8, n, w), F32),
        scratch_shapes=[pltpu.SemaphoreType.DMA((7,)), pltpu.SemaphoreType.DMA((7,))], name=name,
    )(buf)


HBM = pltpu.MemorySpace.HBM


def _sequencer(name, collective_id, sems, sent):
    return functools.partial(pl.kernel, mesh=plsc.ScalarSubcoreMesh(axis_name="sequencer", num_cores=1), name=name,
                             scratch_types=sems, compiler_params=pltpu.CompilerParams(collective_id=collective_id),
                             cost_estimate=pl.CostEstimate(flops=0, transcendentals=0, bytes_accessed=2 * sent,
                                                           remote_bytes_transferred=sent))


def _nbytes(arrays):
    return sum(a.size * a.dtype.itemsize for a in arrays)


def _handshake(peers):
    barrier = pltpu.get_barrier_semaphore()
    for peer in peers:
        pl.semaphore_signal(barrier, inc=1, device_id=peer, device_id_type=MESH)
    pl.semaphore_wait(barrier, len(peers))


def _gather_seq(blocks, name, collective_id):
    n = len(blocks)
    half = [s.shape[1] // 2 for s in blocks]
    full = [jax.new_ref(b, memory_space=HBM) for b in blocks]

    @_sequencer(name, collective_id, [pltpu.SemaphoreType.DMA((n, 3))] * 4, _nbytes(blocks) * 3 // 4)
    def launch(ssem, rsem, fssem, frsem):
        x, y, c = _place()
        q = 2 * x + y
        chips = _other_chips(x, y)
        _handshake([(px, py, c) for px, py in chips] + [(x, y, 1 - c)])
        piece = lambda s, qq, cc: full[s].at[qq, pl.ds(cc * half[s], half[s])]
        sends = []
        for j, (px, py) in enumerate(chips):
            for s in range(n):
                cp = _remote(piece(s, q, c), piece(s, q, c), ssem.at[s, j], rsem.at[s, j], (px, py, c))
                cp.start()
                sends.append(cp)
        for j, (px, py) in enumerate(chips):
            for s in range(n):
                got = piece(s, 2 * px + py, c)
                _remote(got, got, ssem.at[s, j], rsem.at[s, j], (px, py, c)).wait_recv()
                cp = _remote(got, got, fssem.at[s, j], frsem.at[s, j], (x, y, 1 - c))
                cp.start()
                sends.append(cp)
        for j, (px, py) in enumerate(chips):
            for s in range(n):
                got = piece(s, 2 * px + py, 1 - c)
                _remote(got, got, fssem.at[s, j], frsem.at[s, j], (x, y, 1 - c)).wait_recv()
        for cp in sends:
            cp.wait_send()

    launch()
    return [r[...] for r in full]


def _share8(buf, name, collective_id):
    n, w = buf.shape
    src = jax.new_ref(buf, memory_space=HBM)
    out = jax.empty_ref(jax.ShapeDtypeStruct((8, n, w), F32), memory_space=HBM)

    @_sequencer(name, collective_id, [pltpu.SemaphoreType.DMA((7,)), pltpu.SemaphoreType.DMA((7,)), pltpu.SemaphoreType.DMA((1,))],
                7 * buf.size * 4)
    def launch(ssem, rsem, lsem):
        x, y, c = _place()
        me = 4 * x + 2 * y + c
        peers = [(1 - x if (k >> 2) & 1 else x, 1 - y if (k >> 1) & 1 else y, 1 - c if k & 1 else c) for k in range(1, 8)]
        _handshake(peers)
        mine = pltpu.make_async_copy(src, out.at[me], lsem.at[0])
        mine.start()
        sends = []
        for k, peer in enumerate(peers):
            cp = _remote(src, out.at[me], ssem.at[k], rsem.at[k], peer)
            cp.start()
            sends.append(cp)
        for k, (px, py, pc) in enumerate(peers):
            slot = out.at[4 * px + 2 * py + pc]
            _remote(slot, slot, ssem.at[k], rsem.at[k], (px, py, pc)).wait_recv()
        for cp in sends:
            cp.wait_send()
        mine.wait()

    launch()
    return out[...]


def _sum_slots(slots, name, after=None):
    _, n, w = slots.shape

    def body(s_ref, o_ref):
        acc = s_ref[0]
        for d in range(1, 8):
            acc = acc + s_ref[d]
        o_ref[...] = acc

    vm = pl.BlockSpec(memory_space=pltpu.VMEM)
    in_specs, args = [vm], [slots]
    if after is not None:
        body = _skip_ref(body, 1)
        args.append(_deps(after))
        in_specs.append(vm)
    return pl.pallas_call(body, in_specs=in_specs, out_specs=vm, out_shape=jax.ShapeDtypeStruct((n, w), F32), name=name)(*args)


def _pair_swap(parts, name, collective_id):
    n = len(parts)
    half = [p.shape[1] // 2 for p in parts]
    src = [jax.new_ref(p, memory_space=HBM) for p in parts]
    got = [jax.empty_ref(jax.ShapeDtypeStruct((p.shape[0], h, p.shape[2]), p.dtype), memory_space=HBM) for p, h in zip(parts, half)]

    @_sequencer(name, collective_id, [pltpu.SemaphoreType.DMA((n,))] * 2, _nbytes(parts) // 2)
    def launch(ssem, rsem):
        x, y, c = _place()
        _handshake([(x, y, 1 - c)])
        copies = []
        for s in range(n):
            cp = _remote(src[s].at[pl.ds(0, parts[s].shape[0]), pl.ds((1 - c) * half[s], half[s])], got[s], ssem.at[s], rsem.at[s], (x, y, 1 - c))
            cp.start()
            copies.append(cp)
        for cp in copies:
            cp.wait_recv()
        for cp in copies:
            cp.wait_send()

    launch()
    return [g[...] for g in got]


def _to_owners(sums, name, collective_id):
    n = len(sums)
    src = [jax.new_ref(s, memory_space=HBM) for s in sums]
    got = [jax.empty_ref(jax.ShapeDtypeStruct(s.shape, s.dtype), memory_space=HBM) for s in sums]

    @_sequencer(name, collective_id, [pltpu.SemaphoreType.DMA((n, 3))] * 2, _nbytes(sums) * 3 // 4)
    def launch(ssem, rsem):
        x, y, c = _place()
        q = 2 * x + y
        chips = _other_chips(x, y)
        _handshake([(px, py, c) for px, py in chips])
        sends = []
        for j, (px, py) in enumerate(chips):
            for s in range(n):
                cp = _remote(src[s].at[2 * px + py], got[s].at[q], ssem.at[s, j], rsem.at[s, j], (px, py, c))
                cp.start()
                sends.append(cp)
        for j, (px, py) in enumerate(chips):
            for s in range(n):
                slot = got[s].at[2 * px + py]
                _remote(slot, slot, ssem.at[s, j], rsem.at[s, j], (px, py, c)).wait_recv()
        for cp in sends:
            cp.wait_send()

    launch()
    return [g[...] for g in got]


def _pair_join(blocks, name, collective_id):
    n = len(blocks)
    out = [jax.new_ref(b, memory_space=HBM) for b in blocks]

    @_sequencer(name, collective_id, [pltpu.SemaphoreType.DMA((n,))] * 2, _nbytes(blocks) // 2)
    def launch(ssem, rsem):
        x, y, c = _place()
        _handshake([(x, y, 1 - c)])
        sends = []
        for s in range(n):
            h = blocks[s].shape[0] // 2
            mine = out[s].at[pl.ds(c * h, h)]
            cp = _remote(mine, mine, ssem.at[s], rsem.at[s], (x, y, 1 - c))
            cp.start()
            sends.append(cp)
        for s in range(n):
            h = blocks[s].shape[0] // 2
            theirs = out[s].at[pl.ds((1 - c) * h, h)]
            _remote(theirs, theirs, ssem.at[s], rsem.at[s], (x, y, 1 - c)).wait_recv()
        for cp in sends:
            cp.wait_send()

    launch()
    return [o[...] for o in out]


WIRE = BF16


def _row_tile(h):
    return _pick(h, (256, 368, 352, 128, 16))


def _add_pair(part, got, c, name, after=None):
    _, h, w = got.shape
    tr = _row_tile(h)
    nt = h // tr

    def body(c_ref, p_ref, g_ref, o_ref):
        o_ref[...] = (p_ref[...] + g_ref[...].astype(F32)).astype(o_ref.dtype)

    in_specs = [pl.BlockSpec((None, tr, w), lambda q, i, c_ref: (q, c_ref[0] * nt + i, 0)),
                pl.BlockSpec((None, tr, w), lambda q, i, c_ref: (q, i, 0))]
    args = [c.reshape(1).astype(jnp.int32), part, got]
    if after is not None:
        body = _skip_ref(body, len(args))
        args.append(_deps(after))
        in_specs.append(_dep_spec(args[-1]))
    return pl.pallas_call(
        body,
        grid_spec=pltpu.PrefetchScalarGridSpec(
            num_scalar_prefetch=1, grid=(got.shape[0], nt), in_specs=in_specs,
            out_specs=pl.BlockSpec((None, tr, w), lambda q, i, c_ref: (q, i, 0))),
        out_shape=jax.ShapeDtypeStruct(got.shape, WIRE), name=name,
        compiler_params=_cparams(("arbitrary", "arbitrary")),
    )(*args)


def _sum_chips(slots, sums, q, c, name, after=None):
    _, h, w = slots.shape
    tr = _row_tile(h)
    nt = h // tr

    def body(s_ref, mine_ref, a_ref, b_ref, d_ref, o_ref):
        o_ref[...] = ((mine_ref[...].astype(F32) + a_ref[...].astype(F32)) + b_ref[...].astype(F32)) + d_ref[...].astype(F32)

    slot = lambda k: pl.BlockSpec((None, tr, w), lambda i, s_ref: (s_ref[1 + k], i, 0))
    scalars = jnp.stack([c, q, (q + 1) % N_CHIPS, (q + 2) % N_CHIPS, (q + 3) % N_CHIPS]).astype(jnp.int32)
    in_specs, args = [slot(0), slot(1), slot(2), slot(3)], [scalars, sums, slots, slots, slots]
    if after is not None:
        body = _skip_ref(body, len(args))
        args.append(_deps(after))
        in_specs.append(_dep_spec(args[-1]))
    return pl.pallas_call(
        body,
        grid_spec=pltpu.PrefetchScalarGridSpec(
            num_scalar_prefetch=1, grid=(nt,), in_specs=in_specs,
            out_specs=pl.BlockSpec((tr, w), lambda i, s_ref: (s_ref[0] * nt + i, 0))),
        out_shape=jax.ShapeDtypeStruct((2 * h, w), F32), name=name,
        compiler_params=_cparams(("arbitrary",)),
    )(*args)


class _Reduce:
    def __init__(self, parts, q, c, tag, first_id, regions=None):
        self.parts, self.q, self.c, self.tag, self.first_id, self.regions = parts, q, c, tag, first_id, regions
        self.got = _pair_swap(parts, f"{tag}_pair_swap", first_id)

    def to_owners(self, after=None):
        self.sums = [_add_pair(p, g, self.c, f"{self.tag}_pair_add{i}", after)
                     for i, (p, g) in enumerate(zip(self.parts, self.got))]
        if self.regions is not None:
            self.sums = self.regions(self.sums)
        self.slots = _to_owners(self.sums, f"{self.tag}_to_owners", self.first_id + 1)
        return self.sums

    def join(self, after=None):
        blocks = [_sum_chips(sl, sm, self.q, self.c, f"{self.tag}_sum_chips{i}", after)
                  for i, (sl, sm) in enumerate(zip(self.slots, self.sums))]
        self.out = _pair_join(blocks, f"{self.tag}_pair_join", self.first_id + 2)
        return blocks


WEIGHTS = ("meta_tokens", "ffn1_norm", "ffn1_w_gu", "ffn1_w_down", "mix_norm", "w_in", "ssd_conv_w", "ssd_conv_b",
           "ssd_dt_bias", "ssd_a_log", "ssd_d", "ssd_norm", "hg_lower_bound", "hg_norm", "w_branch_a", "w_branch_b",
           "w_out", "ffn2_norm", "ffn2_w_gu", "ffn2_w_down", "final_norm")
BIG = ("ffn1_w_gu", "ffn1_w_down", "w_in", "w_branch_a", "w_branch_b", "w_out", "ffn2_w_gu", "ffn2_w_down")
SMALL = tuple(n for n in WEIGHTS if n not in BIG)


def _rows1024(a):
    flat = a.reshape(-1)
    n = -(-flat.shape[0] // 1024) * 1024
    return jnp.pad(flat, (0, n - flat.shape[0])).reshape(-1, 1024)


def kernel(x, meta_tokens, ffn1_norm, ffn1_w_gu, ffn1_w_down, mix_norm, w_in, ssd_conv_w, ssd_conv_b, ssd_dt_bias, ssd_a_log, ssd_d, ssd_norm, hg_lower_bound, hg_norm, w_branch_a, w_branch_b, w_out, ffn2_norm, ffn2_w_gu, ffn2_w_down, final_norm, loss_target, m_meta_tokens, m_ffn1_norm, m_ffn1_w_gu, m_ffn1_w_down, m_mix_norm, m_w_in, m_ssd_conv_w, m_ssd_conv_b, m_ssd_dt_bias, m_ssd_a_log, m_ssd_d, m_ssd_norm, m_hg_lower_bound, m_hg_norm, m_w_branch_a, m_w_branch_b, m_w_out, m_ffn2_norm, m_ffn2_w_gu, m_ffn2_w_down, m_final_norm, v_meta_tokens, v_ffn1_norm, v_ffn1_w_gu, v_ffn1_w_down, v_mix_norm, v_w_in, v_ssd_conv_w, v_ssd_conv_b, v_ssd_dt_bias, v_ssd_a_log, v_ssd_d, v_ssd_norm, v_hg_lower_bound, v_hg_norm, v_w_branch_a, v_w_branch_b, v_w_out, v_ffn2_norm, v_ffn2_w_gu, v_ffn2_w_down, v_final_norm):
    P = dict(zip(WEIGHTS, (meta_tokens, ffn1_norm, ffn1_w_gu, ffn1_w_down, mix_norm, w_in, ssd_conv_w, ssd_conv_b, ssd_dt_bias, ssd_a_log, ssd_d, ssd_norm, hg_lower_bound, hg_norm, w_branch_a, w_branch_b, w_out, ffn2_norm, ffn2_w_gu, ffn2_w_down, final_norm)))
    M = dict(zip(WEIGHTS, (m_meta_tokens, m_ffn1_norm, m_ffn1_w_gu, m_ffn1_w_down, m_mix_norm, m_w_in, m_ssd_conv_w, m_ssd_conv_b, m_ssd_dt_bias, m_ssd_a_log, m_ssd_d, m_ssd_norm, m_hg_lower_bound, m_hg_norm, m_w_branch_a, m_w_branch_b, m_w_out, m_ffn2_norm, m_ffn2_w_gu, m_ffn2_w_down, m_final_norm)))
    V = dict(zip(WEIGHTS, (v_meta_tokens, v_ffn1_norm, v_ffn1_w_gu, v_ffn1_w_down, v_mix_norm, v_w_in, v_ssd_conv_w, v_ssd_conv_b, v_ssd_dt_bias, v_ssd_a_log, v_ssd_d, v_ssd_norm, v_hg_lower_bound, v_hg_norm, v_w_branch_a, v_w_branch_b, v_w_out, v_ffn2_norm, v_ffn2_w_gu, v_ffn2_w_down, v_final_norm)))
    cx, cy, cc = _place()
    q = 2 * cx + cy

    mine = jnp.concatenate([meta_tokens.reshape(4, 1024), ssd_conv_w.reshape(2, 1024), jnp.zeros((2, 1024), F32)], axis=0)
    every = _exchange8(mine, "gather_small")
    meta_full = jnp.concatenate([every[2 * k, 0:4].reshape(N_META, 256) for k in range(N_CHIPS)], axis=1)
    conv_w_full = jnp.concatenate([every[2 * k, 4:6].reshape(SSD_CONV, 512) for k in range(N_CHIPS)], axis=1)

    late = ("ffn2_w_down", "w_branch_a", "w_branch_b", "w_out")
    rows = jnp.concatenate([P[n][0] for n in late], axis=0)
    zero = lambda t, dtype=F32: (t[0:1, 0:1] * 0).astype(dtype)

    def in_slot(s, after=None):
        s = s if after is None else s + zero(after)
        return lax.dynamic_update_slice(lax.empty((N_CHIPS,) + s.shape, BF16), s.astype(BF16)[None], (q, 0, 0))

    gu1, down1 = _gather_seq([in_slot(ffn1_w_gu[0]), in_slot(ffn1_w_down[0])], "gather_ffn1", 1)
    W = {n: P[n] for n in SMALL}
    W["meta_tokens"], W["ssd_conv_w"] = meta_full, conv_w_full
    W["ffn1_w_gu"], W["ffn1_w_down"] = gu1, down1.reshape(-1, D_MODEL)
    flying = {}

    def stage(name, t):
        if name == "ffn1_norm":
            flying["w_in"] = _gather_seq([in_slot(w_in[0], t)], "gather_w_in", 2)
            return {}
        if name == "ffn1_out":
            flying["late"] = _gather_seq([in_slot(ffn2_w_gu[0], t), in_slot(rows, t)], "gather_late", 3)
            (w_in_all,) = flying["w_in"]
            w_in_all = w_in_all + zero(t, BF16)
            return {"w_in": _split_w_in(w_in_all.transpose(1, 0, 2).reshape(D_MODEL, -1))}
        if name == "mixers_out":
            gu2, rows_all = flying["late"]
            out, r = {"ffn2_w_gu": gu2}, 0
            for n in late:
                nr = P[n].shape[1]
                out[n] = (rows_all[:, r:r + nr] + zero(t, BF16)).reshape(N_CHIPS * nr, D_MODEL)
                r += nr
            return out
        if name == "late_grads":
            parts = [t["ffn2_w_gu"]] + [t[n].reshape(N_CHIPS, -1, D_MODEL) for n in late]
            flying["grad_late"] = _Reduce(parts, q, cc, "grad_late", 4)
            return {"_after": [t["ffn2_w_gu"]] + [t[n] for n in late]}
        if name == "after_conv_bwd":
            return {"_after": flying["grad_late"].to_owners(after=t)}
        if name == "w_in_grads":
            order = ("z", "xbc", "dt", "qfig", "gates")
            blocks = flying["grad_late"].join(after=[t[k] for k in order])

            def regions(sums):
                z, xbc, dt, qfig, gates = [s[0] for s in sums]
                h = z.shape[0]
                qfig = qfig.reshape(h, HG_HEADS, 4, 128).transpose(0, 2, 1, 3).reshape(h, 4 * D_MODEL)
                cols = jnp.concatenate([z, xbc, dt[:, :SSD_HEADS], qfig, gates], axis=1)
                return [cols.reshape(h, N_CHIPS, -1).transpose(1, 0, 2)]

            flying["grad_w_in"] = _Reduce([t[k][None] for k in order], q, cc, "grad_w_in", 7, regions)
            return {"_after": blocks}
        if name == "ffn1_dw_down":
            return {"_after": flying["grad_w_in"].to_owners(after=t)}
        if name == "ffn1_dw":
            dw_gu, dw_down = t
            flying["grad_ffn1"] = _Reduce([dw_gu, dw_down.reshape(N_CHIPS, -1, D_MODEL)], q, cc, "grad_ffn1", 10)
            return {}
        if name == "ffn1_d_in":
            blocks = flying["grad_w_in"].join(after=t)
            flying["grad_ffn1"].to_owners(after=blocks)
            return {}
        return {}

    W["_stage"] = stage

    loss8, grad_x, G = _local_step(x, loss_target, W)

    small = jnp.concatenate(
        [G["meta_tokens"]] + [_rows1024(G[n]) for n in SMALL if n != "meta_tokens"] + [_rows1024(loss8[0:1, 0:1])], axis=0)
    small = jnp.pad(small, ((0, 40 - small.shape[0]), (0, 0)))
    small_slots = _share8(small, "share_small", 13)

    grad_ffn1 = flying["grad_ffn1"]
    going = grad_ffn1.sums
    (g_w_in,) = flying["grad_w_in"].out
    Gb = dict(zip(("ffn2_w_gu",) + late, flying["grad_late"].out))
    Gb["w_in"] = g_w_in

    grads, delta, new_m, new_v, done = {}, {}, {}, {}, []
    cols = w_in.shape[2]
    to_tiles = lambda a: a.transpose(2, 0, 1).reshape(cols, 8, 128).reshape(cols * 8, 128)
    from_tiles = lambda a: a.reshape(cols, 1, D_MODEL).transpose(1, 2, 0)
    for n in [n for n in BIG if n in Gb]:
        if n == "w_in":
            g_t = to_tiles(Gb[n][None])
            d_, m_, v_ = _adamw(to_tiles(P[n]), g_t, to_tiles(M[n]), to_tiles(V[n]), f"adamw_{n}", after=going)
            grads[n], delta[n], new_m[n], new_v[n] = from_tiles(g_t), from_tiles(d_), from_tiles(m_), from_tiles(v_)
        else:
            d_, m_, v_ = _adamw(P[n][0], Gb[n], M[n][0], V[n][0], f"adamw_{n}", after=going)
            grads[n], delta[n], new_m[n], new_v[n] = Gb[n][None], d_[None], m_[None], v_[None]
        done.append(d_)

    small = _sum_slots(small_slots, "sum_small", after=done)
    Gs = {"meta_tokens": small[0:N_META]}
    r = N_META
    for n in SMALL:
        if n == "meta_tokens":
            continue
        nr = -(-G[n].size // 1024)
        Gs[n] = small[r:r + nr].reshape(-1)[:G[n].size].reshape(G[n].shape)
        r += nr
    loss = small[r, 0]
    Gs["meta_tokens"] = lax.dynamic_slice(Gs["meta_tokens"], (0, 256 * q), (N_META, 256))
    Gs["ssd_conv_w"] = lax.dynamic_slice(Gs["ssd_conv_w"], (0, 512 * q), (SSD_CONV, 512))[None]
    Gs = {n: Gs[n].reshape(P[n].shape) for n in SMALL}
    grads.update(Gs)
    flat = lambda a: a.reshape(-1, a.shape[-1])
    d_s, m_s, v_s = _adamw_many(*[[flat(D[n]) for n in SMALL] for D in (P, Gs, M, V)], "adamw_small")
    for i, n in enumerate(SMALL):
        delta[n], new_m[n], new_v[n] = d_s[i].reshape(P[n].shape), m_s[i].reshape(P[n].shape), v_s[i].reshape(P[n].shape)
    done.append(d_s[0])
    grad_ffn1.join(after=done)
    Gb["ffn1_w_gu"], Gb["ffn1_w_down"] = grad_ffn1.out
    for n in ("ffn1_w_gu", "ffn1_w_down"):
        d_, m_, v_ = _adamw(P[n][0], Gb[n], M[n][0], V[n][0], f"adamw_{n}")
        grads[n], delta[n], new_m[n], new_v[n] = Gb[n][None], d_[None], m_[None], v_[None]
    return (loss, grad_x, *[grads[n] for n in WEIGHTS], *[delta[n] for n in WEIGHTS],
            *[new_m[n] for n in WEIGHTS], *[new_v[n] for n in WEIGHTS])
```

```python
import functools

import jax
import jax.numpy as jnp
from jax import lax
from jax.experimental import pallas as pl
from jax.experimental.pallas import tpu as pltpu
from jax.experimental.pallas import tpu_sc as plsc

F32 = jnp.float32
BF16 = jnp.bfloat16
HIGHEST = lax.Precision.HIGHEST
MESH = pl.DeviceIdType.MESH

D_MODEL = 1024
N_META = 16
EPS = 1e-6
SSD_HEADS = 16
SSD_HEAD_DIM = 64
SSD_INNER = 1024
SSD_GROUPS = 4
SSD_STATE = 128
SSD_CONV = 4
SSD_CONV_CH = 2048
HG_HEADS = 8
HG_SUB = 32
CHUNK = 128
D_FF = 2816
N_CHIPS = 4
IN_SIZES = (1024, 2048, 16, 1024, 1024, 1024, 1024, 1024, 1024)
ADAM_LR = 0.001
ADAM_B1 = 0.9
ADAM_B2 = 0.999
ADAM_EPS = 1e-08
ADAM_WD = 0.01
ADAM_STEP = 10
VMEM_LIMIT = 56 * 1024 * 1024
MATMUL_BLOCK_BYTES = 42 * 1024 * 1024
ADAMW_BLOCK_BYTES = 5 * 512 * 1024


def _cparams(sem=None):
    return pltpu.CompilerParams(dimension_semantics=sem, vmem_limit_bytes=VMEM_LIMIT)


def _pick(n, cands):
    for c in cands:
        if n % c == 0:
            return c
    return n


def _deps(after):
    xs = after if isinstance(after, (list, tuple)) else [after]
    one = lambda x: lax.slice(x, (0,) * x.ndim, (1,) * x.ndim).reshape(1).astype(F32)
    return jnp.concatenate([one(x) for x in xs]).reshape(1, -1)


def _dep_spec(dep):
    return pl.BlockSpec(dep.shape, lambda *_: (0, 0))


def _skip_ref(body, pos):
    return lambda *refs: body(*refs[:pos], *refs[pos + 1:])


def _dg(a, b, ca, cb):
    return lax.dot_general(a.astype(BF16), b.astype(BF16), (((ca,), (cb,)), ((), ())), preferred_element_type=F32)


@jax.custom_vjp
def _mm(a, b):
    return _dg(a, b, 1, 0)


def _mm_fwd(a, b):
    return _dg(a, b, 1, 0), (a, b)


def _mm_bwd(r, g):
    a, b = r
    return _dg(g, b, 1, 1), _dg(a, g, 0, 0)


_mm.defvjp(_mm_fwd, _mm_bwd)


@jax.custom_vjp
def _mm_nt(a, b):
    return _dg(a, b, 1, 1)


def _mm_nt_fwd(a, b):
    return _dg(a, b, 1, 1), (a, b)


def _mm_nt_bwd(r, g):
    a, b = r
    return _dg(g, b, 1, 0), _dg(g, a, 0, 0)


_mm_nt.defvjp(_mm_nt_fwd, _mm_nt_bwd)


@jax.custom_vjp
def _mm_tn(a, b):
    return _dg(a, b, 0, 0)


def _mm_tn_fwd(a, b):
    return _dg(a, b, 0, 0), (a, b)


def _mm_tn_bwd(r, g):
    a, b = r
    return _dg(b, g, 1, 1), _dg(a, g, 1, 0)


_mm_tn.defvjp(_mm_tn_fwd, _mm_tn_bwd)


def _tri_sum(x, lower):
    n = x.shape[0]
    ri = lax.broadcasted_iota(jnp.int32, (n, n), 0)
    ci = lax.broadcasted_iota(jnp.int32, (n, n), 1)
    tri = ((ri >= ci) if lower else (ri <= ci)).astype(BF16)
    x1 = x.astype(BF16)
    r1 = x - x1.astype(F32)
    x2 = r1.astype(BF16)
    x3 = (r1 - x2.astype(F32)).astype(BF16)
    dot = lambda p: lax.dot_general(tri, p, (((1,), (0,)), ((), ())), preferred_element_type=F32)
    return (dot(x3) + dot(x2)) + dot(x1)


@jax.custom_vjp
def _cumsum_rows(x):
    return _tri_sum(x, True)


_cumsum_rows.defvjp(lambda x: (_tri_sum(x, True), None), lambda _, g: (_tri_sum(g, False),))


def _silu(x):
    return x * jax.nn.sigmoid(x)


def _softplus(x):
    return jnp.maximum(x, 0.0) + jnp.log(1.0 + jnp.exp(-jnp.abs(x)))


def _tril(n):
    ri = lax.broadcasted_iota(jnp.int32, (n, n), 0)
    ci = lax.broadcasted_iota(jnp.int32, (n, n), 1)
    return ri >= ci


def _row_of(m, r):
    sub = lax.broadcasted_iota(jnp.int32, (m.shape[0], 1), 0)
    return jnp.sum(jnp.where(sub == r, m, 0.0), axis=0, keepdims=True)


def _col_of(m, c):
    lane = lax.broadcasted_iota(jnp.int32, (1, m.shape[1]), 1)
    return jnp.sum(jnp.where(lane == c, m, 0.0), axis=1, keepdims=True)


def _matmul(a, b, *, mode, out_dtype, name, alpha=1.0, res=None, tm=None, tn=None, out_groups=None, after=None):
    b3 = b.ndim == 3
    if mode == "nn":
        M, K = a.shape
        G = b.shape[0] if b3 else 1
        Ng = b.shape[-1]
        N = G * Ng
    elif mode == "nt":
        M, K = a.shape
        G = b.shape[0] if b3 else 1
        N = b.shape[-2]
        Kg = b.shape[-1]
        assert G * Kg == K
    else:
        K, M = a.shape
        N = b.shape[1]
        G = out_groups or 1
        Ng = N // G
    has_res = res is not None
    split_n = (mode == "nn" and b3) or (mode == "tn" and G > 1)
    per_mn = jnp.dtype(out_dtype).itemsize + (res.dtype.itemsize if has_res else 0)
    fits = [(m_ * n_, m_, n_)
            for m_ in (4352, 2176, 1408, 1088, 1024, 544, 512, 256, 128) if M % m_ == 0
            for n_ in (2816, 2048, 1408, 1024, 512, 256, 128) if (Ng if split_n else N) % n_ == 0
            if 2 * (K * m_ * a.dtype.itemsize + K * n_ * b.dtype.itemsize + m_ * n_ * per_mn) + 4 * m_ * n_ <= MATMUL_BLOCK_BYTES]
    _, tm_fit, tn_fit = max(fits)
    tm, tn = tm or tm_fit, tn or tn_fit
    nm, nn_ = M // tm, N // tn
    assert nm * tm == M and nn_ * tn == N, (name, M, N, K, tm, tn)

    if mode == "nn":
        a_spec = pl.BlockSpec((tm, K), lambda i, j: (i, 0))
        if b3:
            ns = Ng // tn
            b_spec = pl.BlockSpec((None, K, tn), lambda i, j: (j // ns, 0, j % ns))
        else:
            b_spec = pl.BlockSpec((K, tn), lambda i, j: (0, j))
        ca, cb = 1, 0
    elif mode == "nt":
        a_spec = pl.BlockSpec((tm, K), lambda i, j: (i, 0))
        if b3:
            b_spec = pl.BlockSpec((G, tn, Kg), lambda i, j: (0, j, 0))
        else:
            b_spec = pl.BlockSpec((tn, K), lambda i, j: (j, 0))
        ca, cb = 1, 1
    else:
        a_spec = pl.BlockSpec((K, tm), lambda i, j: (0, i))
        b_spec = pl.BlockSpec((K, tn), lambda i, j: (0, j))
        ca, cb = 0, 0
    if mode == "tn" and G > 1:
        ns = Ng // tn
        o_spec = pl.BlockSpec((None, tm, tn), lambda i, j: (j // ns, i, j % ns))
        out_shape = jax.ShapeDtypeStruct((G, M, Ng), out_dtype)
    else:
        o_spec = pl.BlockSpec((tm, tn), lambda i, j: (i, j))
        out_shape = jax.ShapeDtypeStruct((M, N), out_dtype)
    in_specs = [a_spec, b_spec]
    args = [a, b]
    if has_res:
        in_specs.append(pl.BlockSpec((tm, tn), lambda i, j: (i, j)))
        args.append(res)
    if after is not None:
        args.append(_deps(after))
        in_specs.append(_dep_spec(args[-1]))

    def body(*refs):
        a_ref, b_ref, o_ref = refs[0], refs[1], refs[-1]
        if mode == "nt" and b3:
            o = _dg(a_ref[:, 0:Kg], b_ref[0], ca, cb)
            for g in range(1, G):
                o = o + _dg(a_ref[:, g * Kg:(g + 1) * Kg], b_ref[g], ca, cb)
        else:
            o = _dg(a_ref[...], b_ref[...], ca, cb)
        if alpha != 1.0:
            o = o * alpha
        if has_res:
            o = o + refs[2][...]
        o_ref[...] = o.astype(o_ref.dtype)

    return pl.pallas_call(
        body, grid=(nm, nn_), in_specs=in_specs, out_specs=o_spec, out_shape=out_shape, name=name,
        compiler_params=_cparams(("parallel", "parallel")),
    )(*args)


def _sum_nt(xs, ws, name):
    R, N = xs[0].shape[0], ws[0].shape[0]
    n = len(xs)
    per_m = sum(x.shape[1] * x.dtype.itemsize for x in xs)
    per_n = sum(w.shape[1] * w.dtype.itemsize for w in ws)
    fits = [(m_ * n_, m_, n_) for m_ in (1088, 544, 256, 128) if R % m_ == 0 for n_ in (1024, 512, 256, 128) if N % n_ == 0
            if 2 * (m_ * per_m + n_ * per_n + m_ * n_ * 4) + 4 * m_ * n_ <= MATMUL_BLOCK_BYTES]
    _, tm, tn = max(fits)

    def body(*refs):
        o = _dg(refs[0][...], refs[n][...], 1, 1)
        for p in range(1, n):
            o = o + _dg(refs[p][...], refs[n + p][...], 1, 1)
        refs[-1][...] = o

    return pl.pallas_call(
        body, grid=(R // tm, N // tn),
        in_specs=[pl.BlockSpec((tm, x.shape[1]), lambda i, j: (i, 0)) for x in xs]
        + [pl.BlockSpec((tn, w.shape[1]), lambda i, j: (j, 0)) for w in ws],
        out_specs=pl.BlockSpec((tm, tn), lambda i, j: (i, j)), out_shape=jax.ShapeDtypeStruct((R, N), F32), name=name,
        compiler_params=_cparams(("parallel", "parallel")),
    )(*xs, *ws)


def _rms_fn(h, w):
    r = lax.rsqrt(jnp.mean(h * h, axis=-1, keepdims=True) + EPS)
    return h * r * w


def _swiglu_fn(gu):
    g = gu[:, :D_FF].astype(F32)
    u = gu[:, D_FF:].astype(F32)
    return _silu(g) * u


def _merge_fn(pa, pb, gates):
    return jax.nn.sigmoid(gates[:, :D_MODEL]) * pa + jax.nn.sigmoid(gates[:, D_MODEL:]) * pb


def _rows_call(body, *, rows, tr, ins, outs, accs=(), name, after=None):
    n = rows // tr
    assert n * tr == rows
    if after is not None:
        body = _skip_ref(body, len(ins))
        ins = list(ins) + [("full", _deps(after))]

    def spec(x):
        if isinstance(x, tuple):
            shp = x[1].shape
            return pl.BlockSpec(shp, lambda i: (0,) * len(shp))
        return pl.BlockSpec((tr, x.shape[1]), lambda i: (i, 0))

    in_specs = [spec(x) for x in ins]
    args = [x[1] if isinstance(x, tuple) else x for x in ins]
    out_specs = [spec(x) for x in outs] + [pl.BlockSpec(x.shape, lambda i: (0,) * len(x.shape)) for x in accs]
    out_shape = [x[1] if isinstance(x, tuple) else x for x in outs] + list(accs)
    return pl.pallas_call(
        body, grid=(n,), in_specs=in_specs, out_specs=out_specs, out_shape=out_shape, name=name,
        compiler_params=_cparams(("arbitrary",)),
    )(*args)


def _acc_rows(ref, val):
    @pl.when(pl.program_id(0) == 0)
    def _():
        ref[...] = jnp.zeros_like(ref)

    ref[0:1, :] += val


def _rms_fwd(h, w, name):
    def body(h_ref, w_ref, o_ref):
        o_ref[...] = _rms_fn(h_ref[...], w_ref[...]).astype(o_ref.dtype)

    R = h.shape[0]
    return _rows_call(body, rows=R, tr=_pick(R, (256, 128)), ins=[h, ("full", w)],
                      outs=[jax.ShapeDtypeStruct(h.shape, BF16)], name=name)[0]


def _rms_bwd(h, w, dn, dres, name, after=None):
    def body(h_ref, w_ref, dn_ref, dres_ref, dh_ref, dw_ref):
        _, vjp = jax.vjp(_rms_fn, h_ref[...], w_ref[...])
        dh, dw = vjp(dn_ref[...].astype(F32))
        dh_ref[...] = dh + dres_ref[...]
        _acc_rows(dw_ref, dw)

    R = h.shape[0]
    return _rows_call(body, rows=R, tr=_pick(R, (256, 128)), ins=[h, ("full", w), dn, dres],
                      outs=[jax.ShapeDtypeStruct(h.shape, F32)], accs=[jax.ShapeDtypeStruct((8, D_MODEL), F32)], name=name,
                      after=after)


def _d_norm_in(dgu, w_gu, h, norm_w, dres, name, after=None):
    R = h.shape[0]
    G, _, kg = w_gu.shape

    def body(dgu_ref, w_ref, h_ref, nw_ref, dres_ref, dh_ref, dw_ref):
        dn = _dg(dgu_ref[:, 0:kg], w_ref[0], 1, 1)
        for g in range(1, G):
            dn = dn + _dg(dgu_ref[:, kg * g:kg * (g + 1)], w_ref[g], 1, 1)
        _, vjp = jax.vjp(_rms_fn, h_ref[...], nw_ref[...])
        dh, dw = vjp(dn)
        dh_ref[...] = dh + dres_ref[...]
        _acc_rows(dw_ref, dw)

    return _rows_call(body, rows=R, tr=_pick(R, (256, 128)), ins=[dgu, ("full", w_gu), h, ("full", norm_w), dres],
                      outs=[jax.ShapeDtypeStruct(h.shape, F32)], accs=[jax.ShapeDtypeStruct((8, D_MODEL), F32)], name=name,
                      after=after)


def _rms_bwd_tokens(h, w, dn, dres, nseq, name, after=None):
    Tp = h.shape[0] // nseq
    nc = Tp // CHUNK

    def body(h_ref, w_ref, dn_ref, dres_ref, dx_ref, dm_ref, dw_ref):
        b, c = pl.program_id(0), pl.program_id(1)
        _, vjp = jax.vjp(_rms_fn, h_ref[...], w_ref[...])
        dh, dw = vjp(dn_ref[...].astype(F32))
        dh = dh + dres_ref[...]

        @pl.when(c == 0)
        def _():
            dm_ref[...] = dh

        @pl.when(c > 0)
        def _():
            dx_ref[...] = dh

        @pl.when((b == 0) & (c == 0))
        def _():
            dw_ref[...] = jnp.zeros_like(dw_ref)

        dw_ref[0:1, :] += dw

    rows = pl.BlockSpec((CHUNK, D_MODEL), lambda b, c: (b * nc + c, 0))
    in_specs, args = [rows, pl.BlockSpec((1, D_MODEL), lambda b, c: (0, 0)), rows, rows], [h, w, dn, dres]
    if after is not None:
        body = _skip_ref(body, len(args))
        args.append(_deps(after))
        in_specs.append(_dep_spec(args[-1]))
    return pl.pallas_call(
        body, grid=(nseq, nc), in_specs=in_specs,
        out_specs=[pl.BlockSpec((None, CHUNK, D_MODEL), lambda b, c: (b, jnp.maximum(c - 1, 0), 0)),
                   pl.BlockSpec((None, CHUNK, D_MODEL), lambda b, c: (b, 0, 0)),
                   pl.BlockSpec((8, D_MODEL), lambda b, c: (0, 0))],
        out_shape=[jax.ShapeDtypeStruct((nseq, Tp - CHUNK, D_MODEL), F32), jax.ShapeDtypeStruct((nseq, CHUNK, D_MODEL), F32),
                   jax.ShapeDtypeStruct((8, D_MODEL), F32)],
        name=name, compiler_params=_cparams(("arbitrary", "arbitrary")),
    )(*args)


def _gu_swiglu(n, w_gu, name):
    R = n.shape[0]
    G, _, ng = w_gu.shape

    def body(n_ref, w_ref, gu_ref, a_ref):
        x = n_ref[...]
        for r in range(G):
            gu_ref[:, ng * r:ng * (r + 1)] = _dg(x, w_ref[r], 1, 0).astype(gu_ref.dtype)
        a_ref[...] = _swiglu_fn(gu_ref[...]).astype(a_ref.dtype)

    return _rows_call(body, rows=R, tr=_pick(R, (256, 128)), ins=[n, ("full", w_gu)],
                      outs=[jax.ShapeDtypeStruct((R, 2 * D_FF), BF16), jax.ShapeDtypeStruct((R, D_FF), BF16)], name=name)


def _d_swiglu(dout, w_down, gu, alpha, name):
    R = gu.shape[0]

    def body(do_ref, w_ref, gu_ref, o_ref):
        da = _dg(do_ref[...] * alpha, w_ref[...], 1, 1)
        g = gu_ref[:, :D_FF].astype(F32)
        u = gu_ref[:, D_FF:].astype(F32)
        s = jax.nn.sigmoid(g)
        t = g * s
        o_ref[:, :D_FF] = (da * u * (s + t - t * s)).astype(o_ref.dtype)
        o_ref[:, D_FF:] = (da * t).astype(o_ref.dtype)

    return _rows_call(body, rows=R, tr=_pick(R, (256, 128)), ins=[dout, ("full", w_down), gu],
                      outs=[jax.ShapeDtypeStruct(gu.shape, BF16)], name=name)[0]


def _residual_matmul(a, w, res, alpha, name, norm_w=None):
    R, K = a.shape

    def body(a_ref, w_ref, r_ref, *rest):
        out = r_ref[...] + alpha * _dg(a_ref[...], w_ref[...], 1, 0)
        if norm_w is None:
            rest[0][...] = out
        else:
            rest[1][...] = out
            rest[2][...] = _rms_fn(out, rest[0][...]).astype(rest[2].dtype)

    f32 = jax.ShapeDtypeStruct((R, D_MODEL), F32)
    ins = [a, ("full", w), res] + ([] if norm_w is None else [("full", norm_w)])
    outs = [f32] + ([] if norm_w is None else [jax.ShapeDtypeStruct((R, D_MODEL), BF16)])
    got = _rows_call(body, rows=R, tr=_pick(R, (544, 256, 128)), ins=ins, outs=outs, name=name)
    return got[0] if norm_w is None else (got[0], got[1])


def _branch_merge(ya, yb, wa, wb, gates, name):
    def body(ya_ref, yb_ref, wa_ref, wb_ref, g_ref, pa_ref, pb_ref, o_ref):
        pa = _dg(ya_ref[...], wa_ref[...], 1, 0)
        pb = _dg(yb_ref[...], wb_ref[...], 1, 0)
        pa_ref[...] = pa
        pb_ref[...] = pb
        o_ref[...] = _merge_fn(pa, pb, g_ref[...].astype(F32)).astype(o_ref.dtype)

    R = ya.shape[0]
    f32 = jax.ShapeDtypeStruct((R, D_MODEL), F32)
    return _rows_call(body, rows=R, tr=_pick(R, (544, 256, 128)), ins=[ya, yb, ("full", wa), ("full", wb), gates],
                      outs=[f32, f32, jax.ShapeDtypeStruct((R, D_MODEL), BF16)], name=name)


def _branch_merge_bwd(pa, pb, gates, dm, wa, wb, name):
    def body(pa_ref, pb_ref, g_ref, dm_ref, wa_ref, wb_ref, dpa_ref, dpb_ref, dg_ref, dya_ref, dyb_ref):
        _, vjp = jax.vjp(_merge_fn, pa_ref[...], pb_ref[...], g_ref[...].astype(F32))
        dpa, dpb, dg = vjp(dm_ref[...].astype(F32))
        dpa_ref[...] = dpa.astype(dpa_ref.dtype)
        dpb_ref[...] = dpb.astype(dpb_ref.dtype)
        dg_ref[...] = dg.astype(dg_ref.dtype)
        dya_ref[...] = _dg(dpa, wa_ref[...], 1, 1).astype(dya_ref.dtype)
        dyb_ref[...] = _dg(dpb, wb_ref[...], 1, 1).astype(dyb_ref.dtype)

    R = pa.shape[0]
    b16 = jax.ShapeDtypeStruct(pa.shape, BF16)
    return _rows_call(body, rows=R, tr=_pick(R, (544, 256, 128)), ins=[pa, pb, gates, dm, ("full", wa), ("full", wb)],
                      outs=[b16, b16, jax.ShapeDtypeStruct(gates.shape, BF16), b16, b16], name=name)


def _loss_head(h3, w, target, nseq, name):
    Tp = h3.shape[0] // nseq
    nc = Tp // CHUNK

    def fn(h, w_, t, valid):
        y = _rms_fn(h, w_)
        e = (y - t) * valid
        return 0.5 * jnp.sum(jnp.mean(e * e, axis=-1, keepdims=True))

    def body(h_ref, w_ref, t_ref, loss_ref, dh_ref, dw_ref):
        b, c = pl.program_id(0), pl.program_id(1)
        valid = (c >= 1).astype(F32)
        t = t_ref[...]
        loss, vjp = jax.vjp(lambda h, w_: fn(h, w_, t, valid), h_ref[...], w_ref[...])
        dh, dw = vjp(jnp.ones((), F32))
        dh_ref[...] = dh

        @pl.when((b == 0) & (c == 0))
        def _():
            loss_ref[...] = jnp.zeros_like(loss_ref)
            dw_ref[...] = jnp.zeros_like(dw_ref)

        loss_ref[...] += jnp.full(loss_ref.shape, loss, F32)
        dw_ref[0:1, :] += dw

    return pl.pallas_call(
        body, grid=(nseq, nc),
        in_specs=[pl.BlockSpec((CHUNK, D_MODEL), lambda b, c: (b * nc + c, 0)),
                  pl.BlockSpec((1, D_MODEL), lambda b, c: (0, 0)),
                  pl.BlockSpec((None, CHUNK, D_MODEL), lambda b, c: (b, jnp.maximum(c - 1, 0), 0))],
        out_specs=[pl.BlockSpec((8, 128), lambda b, c: (0, 0)),
                   pl.BlockSpec((CHUNK, D_MODEL), lambda b, c: (b * nc + c, 0)),
                   pl.BlockSpec((8, D_MODEL), lambda b, c: (0, 0))],
        out_shape=[jax.ShapeDtypeStruct((8, 128), F32), jax.ShapeDtypeStruct(h3.shape, F32),
                   jax.ShapeDtypeStruct((8, D_MODEL), F32)],
        name=name, compiler_params=_cparams(("arbitrary", "arbitrary")),
    )(h3, w, target)


CONV_TILE = 512
CONV_HALO = 8


def _conv_fwd(xbc, w, b, pad, name):
    B, Tp, C = xbc.shape
    nch = Tp // CHUNK

    def body(x_ref, w_ref, b_ref, o_ref, xp):
        xp[0:CONV_HALO, :] = jnp.zeros((CONV_HALO, CONV_TILE), F32)
        xp[CONV_HALO:, :] = x_ref[...]
        for c in range(nch):
            acc = jnp.zeros((CHUNK, CONV_TILE), F32) + b_ref[...]
            for k in range(SSD_CONV):
                acc = acc + w_ref[k:k + 1, :] * xp[pl.ds(CONV_HALO + CHUNK * c - (SSD_CONV - 1) + k, CHUNK), :]
            out = _silu(acc)
            if CHUNK * c < pad:
                row = CHUNK * c + lax.broadcasted_iota(jnp.int32, (CHUNK, 1), 0)
                out = jnp.where(row >= pad, out, 0.0)
            o_ref[pl.ds(CHUNK * c, CHUNK), :] = out

    return pl.pallas_call(
        body, grid=(B, C // CONV_TILE),
        in_specs=[pl.BlockSpec((None, Tp, CONV_TILE), lambda i, j: (i, 0, j)),
                  pl.BlockSpec((SSD_CONV, CONV_TILE), lambda i, j: (0, j)),
                  pl.BlockSpec((1, CONV_TILE), lambda i, j: (0, j))],
        out_specs=pl.BlockSpec((None, Tp, CONV_TILE), lambda i, j: (i, 0, j)),
        out_shape=jax.ShapeDtypeStruct(xbc.shape, F32),
        scratch_shapes=[pltpu.VMEM((Tp + CONV_HALO, CONV_TILE), F32)],
        name=name, compiler_params=_cparams(("arbitrary", "arbitrary")),
    )(xbc, w, b)


def _conv_bwd(xbc, w, b, dact, pad, name):
    B, Tp, C = xbc.shape
    nch = Tp // CHUNK

    def body(x_ref, w_ref, b_ref, da_ref, dx_ref, dw_ref, db_ref, xp, dp):
        bi = pl.program_id(1)
        xp[0:CONV_HALO, :] = jnp.zeros((CONV_HALO, CONV_TILE), F32)
        xp[CONV_HALO:, :] = x_ref[...]
        dp[pl.ds(Tp, CONV_HALO), :] = jnp.zeros((CONV_HALO, CONV_TILE), F32)
        dws = [jnp.zeros((1, CONV_TILE), F32) for _ in range(SSD_CONV)]
        dbs = jnp.zeros((1, CONV_TILE), F32)
        for c in range(nch):
            xs = [xp[pl.ds(CONV_HALO + CHUNK * c - (SSD_CONV - 1) + k, CHUNK), :] for k in range(SSD_CONV)]
            acc = jnp.zeros((CHUNK, CONV_TILE), F32) + b_ref[...]
            for k in range(SSD_CONV):
                acc = acc + w_ref[k:k + 1, :] * xs[k]
            sg = jax.nn.sigmoid(acc)
            t = acc * sg
            dpre = da_ref[pl.ds(CHUNK * c, CHUNK), :] * (sg + t - t * sg)
            if CHUNK * c < pad:
                row = CHUNK * c + lax.broadcasted_iota(jnp.int32, (CHUNK, 1), 0)
                dpre = jnp.where(row >= pad, dpre, 0.0)
            dp[pl.ds(CHUNK * c, CHUNK), :] = dpre
            dbs = dbs + jnp.sum(dpre, axis=0, keepdims=True)
            for k in range(SSD_CONV):
                dws[k] = dws[k] + jnp.sum(dpre * xs[k], axis=0, keepdims=True)
        for c in range(nch):
            acc = jnp.zeros((CHUNK, CONV_TILE), F32)
            for k in range(SSD_CONV):
                acc = acc + w_ref[k:k + 1, :] * dp[pl.ds(CHUNK * c + (SSD_CONV - 1) - k, CHUNK), :]
            dx_ref[pl.ds(CHUNK * c, CHUNK), :] = acc.astype(dx_ref.dtype)

        @pl.when(bi == 0)
        def _():
            dw_ref[...] = jnp.zeros_like(dw_ref)
            db_ref[...] = jnp.zeros_like(db_ref)

        for k in range(SSD_CONV):
            dw_ref[k:k + 1, :] += dws[k]
        db_ref[0:1, :] += dbs

    return pl.pallas_call(
        body, grid=(C // CONV_TILE, B),
        in_specs=[pl.BlockSpec((None, Tp, CONV_TILE), lambda j, i: (i, 0, j)),
                  pl.BlockSpec((SSD_CONV, CONV_TILE), lambda j, i: (0, j)),
                  pl.BlockSpec((1, CONV_TILE), lambda j, i: (0, j)),
                  pl.BlockSpec((None, Tp, CONV_TILE), lambda j, i: (i, 0, j))],
        out_specs=[pl.BlockSpec((None, Tp, CONV_TILE), lambda j, i: (i, 0, j)),
                   pl.BlockSpec((8, CONV_TILE), lambda j, i: (0, j)),
                   pl.BlockSpec((8, CONV_TILE), lambda j, i: (0, j))],
        out_shape=[jax.ShapeDtypeStruct(xbc.shape, BF16), jax.ShapeDtypeStruct((8, C), F32),
                   jax.ShapeDtypeStruct((8, C), F32)],
        scratch_shapes=[pltpu.VMEM((Tp + CONV_HALO, CONV_TILE), F32), pltpu.VMEM((Tp + CONV_HALO, CONV_TILE), F32)],
        name=name, compiler_params=_cparams(("arbitrary", "arbitrary")),
    )(xbc, w, b, dact)


def _ssd_chunk(xs, bm, cm, dtr, z, state, dt_bias, a_log, dskip, norm_w, valid, kept=None, keep=False):
    Q = xs.shape[0]
    known = (lambda x, v: x) if kept is None else _known
    lane = lax.broadcasted_iota(jnp.int32, (1, 128), 1)
    dt = jnp.where(lane < SSD_HEADS, _softplus(dtr + dt_bias), 0.0) * valid
    a = dt * (-jnp.exp(a_log))
    tril = _tril(Q)
    cs = known(_cumsum_rows(a), None if kept is None else kept[0])
    cs_t = cs.T
    cs_end = _row_of(cs, Q - 1)
    low = lane < SSD_HEAD_DIM
    low_rows = lax.broadcasted_iota(jnp.int32, (128, 1), 0) < SSD_HEAD_DIM
    ys, new_state, cbs = [], [], []
    for g in range(SSD_GROUPS):
        bg = bm[:, 128 * g:128 * (g + 1)]
        cg = cm[:, 128 * g:128 * (g + 1)]
        cb = known(_mm_nt(cg, bg), None if kept is None else kept[1][Q * g:Q * (g + 1)])
        cbs.append(cb)
        for pr in range(2):
            p = 2 * g + pr
            h0, h1 = 2 * p, 2 * p + 1
            xp = xs[:, 128 * p:128 * (p + 1)]
            c0, c1 = _col_of(cs, h0), _col_of(cs, h1)
            e0, e1 = _col_of(cs_end, h0), _col_of(cs_end, h1)
            xd = xp * jnp.where(low, _col_of(dt, h0), _col_of(dt, h1))
            l0 = jnp.exp(jnp.where(tril, c0 - _row_of(cs_t, h0), -1e30))
            l1 = jnp.exp(jnp.where(tril, c1 - _row_of(cs_t, h1), -1e30))
            y_diag = jnp.where(low, _mm(cb * l0, xd), _mm(cb * l1, xd))
            to_end = jnp.where(low, jnp.exp(e0 - c0), jnp.exp(e1 - c1))
            sp = state[128 * p:128 * (p + 1), :]
            y_off = _mm_nt(cg, sp) * jnp.where(low, jnp.exp(c0), jnp.exp(c1))
            new_state.append(sp * jnp.where(low_rows, jnp.exp(e0), jnp.exp(e1)) + _mm_tn(xd * to_end, bg))
            ys.append(y_diag + y_off + xp * jnp.where(low, _col_of(dskip, h0), _col_of(dskip, h1)))
    y_raw = known(jnp.concatenate(ys, axis=1), None if kept is None else kept[2])
    y = y_raw * _silu(z)
    gw = SSD_INNER // SSD_GROUPS
    outs = []
    for g in range(SSD_GROUPS):
        blk = y[:, gw * g:gw * (g + 1)]
        outs.append(blk * lax.rsqrt(jnp.mean(blk * blk, axis=-1, keepdims=True) + EPS))
    out, state_out = jnp.concatenate(outs, axis=1) * norm_w, jnp.concatenate(new_state, axis=0)
    if kept is not None:
        state_out = _known(state_out, state)
    return (out, state_out, (cs, jnp.concatenate(cbs, axis=0), y_raw)) if keep else (out, state_out)


def _valid_rows(c, pad):
    row = c * CHUNK + lax.broadcasted_iota(jnp.int32, (CHUNK, 1), 0)
    return (row >= pad).astype(F32)


def _ssd_fwd(xact, dtr, z, dt_bias, a_log, dskip, norm_w, pad, name):
    B, Tp, _ = xact.shape
    nc = Tp // CHUNK

    def body(xs_ref, bm_ref, cm_ref, dt_ref, z_ref, db_ref, al_ref, ds_ref, nw_ref, y_ref, save_ref, cs_ref, cb_ref, yr_ref, st):
        c = pl.program_id(1)

        @pl.when(c == 0)
        def _():
            st[...] = jnp.zeros_like(st)

        s0 = st[...]
        save_ref[...] = s0
        y, s1, (cs, cb, y_raw) = _ssd_chunk(xs_ref[...], bm_ref[...], cm_ref[...], dt_ref[...], z_ref[...].astype(F32), s0,
                                            db_ref[...], al_ref[...], ds_ref[...], nw_ref[...], _valid_rows(c, pad), keep=True)
        y_ref[...] = y.astype(y_ref.dtype)
        cs_ref[...] = cs
        cb_ref[...] = cb
        yr_ref[...] = y_raw
        st[...] = s1

    row = lambda w, off=0: pl.BlockSpec((None, CHUNK, w), lambda b, c: (b, c, off))
    par = lambda w: pl.BlockSpec((1, w), lambda b, c: (0, 0))
    per_chunk = lambda r: pl.BlockSpec((None, None, r, 128), lambda b, c: (b, c, 0, 0))
    return pl.pallas_call(
        body, grid=(B, nc),
        in_specs=[row(1024, 0), row(512, 2), row(512, 3), row(128), row(1024), par(128), par(128), par(128), par(1024)],
        out_specs=[row(1024), per_chunk(1024), row(128), per_chunk(SSD_GROUPS * CHUNK), row(1024)],
        out_shape=[jax.ShapeDtypeStruct((B, Tp, SSD_INNER), BF16), jax.ShapeDtypeStruct((B, nc, 1024, 128), F32),
                   jax.ShapeDtypeStruct((B, Tp, 128), F32), jax.ShapeDtypeStruct((B, nc, SSD_GROUPS * CHUNK, 128), F32),
                   jax.ShapeDtypeStruct((B, Tp, SSD_INNER), F32)],
        scratch_shapes=[pltpu.VMEM((1024, 128), F32)],
        name=name, compiler_params=_cparams(("arbitrary", "arbitrary")),
    )(xact, xact, xact, dtr, z, dt_bias, a_log, dskip, norm_w)


def _ssd_bwd(xact, dtr, z, dt_bias, a_log, dskip, norm_w, saved, kept, dy, pad, name, after=None):
    B, Tp, _ = xact.shape
    nc = Tp // CHUNK

    def body(xs_ref, bm_ref, cm_ref, dt_ref, z_ref, db_ref, al_ref, ds_ref, nw_ref, sv_ref, cs_ref, cb_ref, yr_ref, dy_ref,
             dx_ref, ddt_ref, dz_ref, dpar_ref, dnw_ref, dst):
        b, i = pl.program_id(0), pl.program_id(1)
        c = nc - 1 - i

        @pl.when(i == 0)
        def _():
            dst[...] = jnp.zeros_like(dst)

        valid = _valid_rows(c, pad)
        kept_c = (cs_ref[...], cb_ref[...], yr_ref[...])
        fn = lambda *a: _ssd_chunk(*a, valid, kept=kept_c)
        _, vjp = jax.vjp(fn, xs_ref[...], bm_ref[...], cm_ref[...], dt_ref[...], z_ref[...].astype(F32), sv_ref[...],
                         db_ref[...], al_ref[...], ds_ref[...], nw_ref[...])
        dxs, dbm, dcm, ddt, dz, dstate, ddb, dal, dds, dnw = vjp((dy_ref[...].astype(F32), dst[...]))
        dx_ref[:, 0:1024] = dxs
        dx_ref[:, 1024:1536] = dbm
        dx_ref[:, 1536:2048] = dcm
        ddt_ref[...] = ddt
        dz_ref[...] = dz.astype(dz_ref.dtype)
        dst[...] = dstate

        @pl.when((b == 0) & (i == 0))
        def _():
            dpar_ref[...] = jnp.zeros_like(dpar_ref)
            dnw_ref[...] = jnp.zeros_like(dnw_ref)

        dpar_ref[0:1, :] += ddb
        dpar_ref[1:2, :] += dal
        dpar_ref[2:3, :] += dds
        dnw_ref[0:1, :] += dnw

    row = lambda w, off=0: pl.BlockSpec((None, CHUNK, w), lambda b, i: (b, nc - 1 - i, off))
    par = lambda w: pl.BlockSpec((1, w), lambda b, i: (0, 0))
    acc = lambda w: pl.BlockSpec((8, w), lambda b, i: (0, 0))
    per_chunk = lambda r: pl.BlockSpec((None, None, r, 128), lambda b, i: (b, nc - 1 - i, 0, 0))
    in_specs = [row(1024, 0), row(512, 2), row(512, 3), row(128), row(1024), par(128), par(128), par(128), par(1024),
                per_chunk(1024), row(128), per_chunk(SSD_GROUPS * CHUNK), row(1024), row(1024)]
    args = [xact, xact, xact, dtr, z, dt_bias, a_log, dskip, norm_w, saved, kept[0], kept[1], kept[2], dy]
    if after is not None:
        body = _skip_ref(body, len(args))
        args.append(_deps(after))
        in_specs.append(_dep_spec(args[-1]))
    outs = pl.pallas_call(
        body, grid=(B, nc), in_specs=in_specs,
        out_specs=[row(2048), row(128), row(1024), acc(128), acc(1024)],
        out_shape=[jax.ShapeDtypeStruct((B, Tp, 2048), F32), jax.ShapeDtypeStruct((B, Tp, 128), F32),
                   jax.ShapeDtypeStruct((B, Tp, 1024), BF16), jax.ShapeDtypeStruct((8, 128), F32),
                   jax.ShapeDtypeStruct((8, 1024), F32)],
        scratch_shapes=[pltpu.VMEM((1024, 128), F32)],
        name=name, compiler_params=_cparams(("arbitrary", "arbitrary")),
    )(*args)
    return outs


@jax.custom_vjp
def _known(x, value):
    return value


_known.defvjp(lambda x, value: (value, None), lambda _, g: (g, jnp.zeros_like(g)))


def _hg_chunk(qr, fr, ir, gr, state_t, p0, p1, norm_w, valid, kept=None, keep=False):
    Q = qr.shape[0]
    known = (lambda x, i: x) if kept is None else (lambda x, i: _known(x, kept[i].astype(x.dtype)))
    lb = jax.nn.sigmoid(p0 - p1)
    f = lb + (1.0 - lb) * jax.nn.sigmoid(fr)
    k = 1.0 - f
    q = _silu(qr)
    v = ir * valid
    cum = known(_cumsum_rows(jnp.log(f)), 0)
    cum_end = _row_of(cum, Q - 1)
    o_inter = _mm_nt(q * jnp.exp(cum), state_t)
    nblk = Q // HG_SUB
    row = lax.broadcasted_iota(jnp.int32, (Q, 1), 0)
    ri = lax.broadcasted_iota(jnp.int32, (Q, Q), 0)
    ci = lax.broadcasted_iota(jnp.int32, (Q, Q), 1)
    mids = jnp.concatenate([jnp.broadcast_to(_row_of(cum, HG_SUB * i + HG_SUB // 2 - 1), (HG_SUB, cum.shape[1]))
                            for i in range(nblk)], axis=0)
    sh = HG_SUB.bit_length() - 1
    same = (jnp.right_shift(ri, sh) == jnp.right_shift(ci, sh)) & (ri >= ci)
    att = jnp.where(same, _mm_nt(q * jnp.exp(cum - mids), k * jnp.exp(mids - cum)), 0.0)
    for i in range(1, nblk):
        lo = HG_SUB * i
        start = _row_of(cum, lo - 1)
        qa = q * jnp.exp(jnp.where((row >= lo) & (row < lo + HG_SUB), cum - start, -1e30))
        ka = k * jnp.exp(jnp.where(row < lo, start - cum, -1e30))
        att = att + _mm_nt(qa, ka)
    att = known(att, 1)
    o = known(o_inter + _mm(att, v), 2)
    new_state_t = state_t * jnp.exp(cum_end) + _mm_tn(v, k * jnp.exp(cum_end - cum))
    if kept is not None:
        new_state_t = _known(new_state_t, state_t)
    y = o * lax.rsqrt(jnp.mean(o * o, axis=-1, keepdims=True) + EPS) * norm_w * _silu(gr)
    return (y, new_state_t, (cum, att, o)) if keep else (y, new_state_t)


HG_PER_STEP = 8
HG_COLS = 4 * 128


def _hg_fwd(qfig, lbh, nwh, pad, name):
    B, Tp, _ = qfig.shape
    nc = Tp // CHUNK
    hp = HG_PER_STEP

    def body(x_ref, lb_ref, nw_ref, y_ref, save_ref, cum_ref, att_ref, o_ref, st):
        c = pl.program_id(1)

        @pl.when(c == 0)
        def _():
            st[...] = jnp.zeros_like(st)

        valid = _valid_rows(c, pad)
        for j in range(hp):
            for b in range(B):
                s0 = st[j, b]
                save_ref[j, b] = s0
                col = lambda k: x_ref[b, :, HG_COLS * j + 128 * k:HG_COLS * j + 128 * (k + 1)]
                y, s1, (cum, att, o) = _hg_chunk(col(0), col(1), col(2), col(3), s0, lb_ref[j, 0:1, :], lb_ref[j, 1:2, :],
                                                 nw_ref[j], valid, keep=True)
                y_ref[b, :, 128 * j:128 * (j + 1)] = y.astype(y_ref.dtype)
                cum_ref[b, :, 128 * j:128 * (j + 1)] = cum
                att_ref[j, b] = att.astype(att_ref.dtype)
                o_ref[b, :, 128 * j:128 * (j + 1)] = o
                st[j, b] = s1

    rows = pl.BlockSpec((B, CHUNK, 128 * hp), lambda h, c: (0, c, h))
    per_chunk = pl.BlockSpec((hp, B, None, 128, 128), lambda h, c: (h, 0, c, 0, 0))
    return pl.pallas_call(
        body, grid=(HG_HEADS // hp, nc),
        in_specs=[pl.BlockSpec((B, CHUNK, HG_COLS * hp), lambda h, c: (0, c, h)),
                  pl.BlockSpec((hp, 2, 128), lambda h, c: (h, 0, 0)),
                  pl.BlockSpec((hp, 1, 128), lambda h, c: (h, 0, 0))],
        out_specs=[rows, per_chunk, rows, per_chunk, rows],
        out_shape=[jax.ShapeDtypeStruct((B, Tp, 1024), BF16), jax.ShapeDtypeStruct((HG_HEADS, B, nc, 128, 128), F32),
                   jax.ShapeDtypeStruct((B, Tp, 1024), F32), jax.ShapeDtypeStruct((HG_HEADS, B, nc, 128, 128), BF16),
                   jax.ShapeDtypeStruct((B, Tp, 1024), F32)],
        scratch_shapes=[pltpu.VMEM((hp, B, 128, 128), F32)],
        name=name, compiler_params=_cparams(("arbitrary", "arbitrary")),
    )(qfig, lbh, nwh)


def _hg_bwd(qfig, lbh, nwh, saved, kept, dy, pad, name, after=None):
    B, Tp, _ = qfig.shape
    nc = Tp // CHUNK
    hp = HG_PER_STEP

    def body(x_ref, lb_ref, nw_ref, sv_ref, cum_ref, att_ref, o_ref, dy_ref, dx_ref, dlb_ref, dnw_ref, dst):
        i = pl.program_id(1)
        c = nc - 1 - i

        @pl.when(i == 0)
        def _():
            dst[...] = jnp.zeros_like(dst)
            dlb_ref[...] = jnp.zeros_like(dlb_ref)
            dnw_ref[...] = jnp.zeros_like(dnw_ref)

        valid = _valid_rows(c, pad)
        for j in range(hp):
            for b in range(B):
                col = lambda k: x_ref[b, :, HG_COLS * j + 128 * k:HG_COLS * j + 128 * (k + 1)]
                head = slice(128 * j, 128 * (j + 1))
                kept_jb = (cum_ref[b, :, head], att_ref[j, b], o_ref[b, :, head])
                fn = lambda *a: _hg_chunk(*a, valid, kept=kept_jb)
                _, vjp = jax.vjp(fn, col(0), col(1), col(2), col(3), sv_ref[j, b], lb_ref[j, 0:1, :], lb_ref[j, 1:2, :], nw_ref[j])
                d4 = vjp((dy_ref[b, :, 128 * j:128 * (j + 1)].astype(F32), dst[j, b]))
                for k in range(4):
                    dx_ref[b, :, HG_COLS * j + 128 * k:HG_COLS * j + 128 * (k + 1)] = d4[k].astype(dx_ref.dtype)
                dst[j, b] = d4[4]
                dlb_ref[j, 0:1, :] += d4[5]
                dlb_ref[j, 1:2, :] += d4[6]
                dnw_ref[j, 0:1, :] += d4[7]

    acc = pl.BlockSpec((hp, 8, 128), lambda h, i: (h, 0, 0))
    rows = pl.BlockSpec((B, CHUNK, 128 * hp), lambda h, i: (0, nc - 1 - i, h))
    per_chunk = pl.BlockSpec((hp, B, None, 128, 128), lambda h, i: (h, 0, nc - 1 - i, 0, 0))
    in_specs = [pl.BlockSpec((B, CHUNK, HG_COLS * hp), lambda h, i: (0, nc - 1 - i, h)),
                pl.BlockSpec((hp, 2, 128), lambda h, i: (h, 0, 0)),
                pl.BlockSpec((hp, 1, 128), lambda h, i: (h, 0, 0)),
                per_chunk, rows, per_chunk, rows, rows]
    args = [qfig, lbh, nwh, saved, kept[0], kept[1], kept[2], dy]
    if after is not None:
        body = _skip_ref(body, len(args))
        args.append(_deps(after))
        in_specs.append(_dep_spec(args[-1]))
    return pl.pallas_call(
        body, grid=(HG_HEADS // hp, nc), in_specs=in_specs,
        out_specs=[pl.BlockSpec((B, CHUNK, HG_COLS * hp), lambda h, i: (0, nc - 1 - i, h)), acc, acc],
        out_shape=[jax.ShapeDtypeStruct((B, Tp, 4096), BF16), jax.ShapeDtypeStruct((HG_HEADS, 8, 128), F32),
                   jax.ShapeDtypeStruct((HG_HEADS, 8, 128), F32)],
        scratch_shapes=[pltpu.VMEM((hp, B, 128, 128), F32)],
        name=name, compiler_params=_cparams(("arbitrary", "arbitrary")),
    )(*args)


def _adamw_math(w, g, m, v):
    m = ADAM_B1 * m + (1.0 - ADAM_B1) * g
    v = ADAM_B2 * v + (1.0 - ADAM_B2) * (g * g)
    m_hat = m / (1.0 - ADAM_B1 ** ADAM_STEP)
    v_hat = v / (1.0 - ADAM_B2 ** ADAM_STEP)
    return -ADAM_LR * (m_hat / (jnp.sqrt(v_hat) + ADAM_EPS) + ADAM_WD * w), m, v


def _adamw_many(ws, gs, ms, vs, name):
    n = len(ws)

    def body(*refs):
        for i in range(n):
            d, m, v = _adamw_math(refs[i][...], refs[n + i][...], refs[2 * n + i][...], refs[3 * n + i][...])
            refs[4 * n + i][...] = d
            refs[5 * n + i][...] = m
            refs[6 * n + i][...] = v

    vm = pl.BlockSpec(memory_space=pltpu.VMEM)
    outs = pl.pallas_call(body, in_specs=[vm] * (4 * n), out_specs=[vm] * (3 * n),
                          out_shape=[jax.ShapeDtypeStruct(w.shape, F32) for w in ws] * 3, name=name)(*ws, *gs, *ms, *vs)
    return outs[:n], outs[n:2 * n], outs[2 * n:]


def _adamw(w, g, m, v, name, after=None):
    R, C = w.shape
    tr = max(t for t in range(8, R + 1, 8) if R % t == 0 and (t * C * 4 <= ADAMW_BLOCK_BYTES or t == 8))

    def body(w_ref, g_ref, m_ref, v_ref, d_ref, mo_ref, vo_ref):
        d_ref[...], mo_ref[...], vo_ref[...] = _adamw_math(w_ref[...], g_ref[...], m_ref[...], v_ref[...])

    sp = pl.BlockSpec((tr, C), lambda i: (i, 0))
    sh = jax.ShapeDtypeStruct((R, C), F32)
    in_specs, args = [sp] * 4, [w, g, m, v]
    if after is not None:
        body = _skip_ref(body, len(args))
        args.append(_deps(after))
        in_specs.append(_dep_spec(args[-1]))
    return pl.pallas_call(body, grid=(R // tr,), in_specs=in_specs, out_specs=[sp] * 3, out_shape=[sh] * 3,
                          name=name, compiler_params=_cparams(("arbitrary",)))(*args)


def _ffn_fwd(h, norm_w, w_gu, w_down, tag, after_norm=None, n=None, next_norm_w=None):
    if n is None:
        n = _rms_fwd(h, norm_w, f"{tag}_norm")
    if after_norm is not None:
        after_norm(n)
    gu, a = _gu_swiglu(n, w_gu, f"{tag}_gu")
    out = _residual_matmul(a, w_down, h, 0.5, f"{tag}_down", next_norm_w)
    return out, (n, gu, a)


def _ffn_bwd(h, norm_w, w_gu, w_down, saved, dout, tag, after_dw_down=None, token_seqs=None, told=None):
    n, gu, a = saved
    dgu = _d_swiglu(dout, w_down, gu, 0.5, f"{tag}_d_gu")
    dw_down = _matmul(a, dout, mode="tn", out_dtype=F32, alpha=0.5, name=f"{tag}_dw_down")
    dw_gu = _matmul(n, dgu, mode="tn", out_dtype=F32, out_groups=N_CHIPS, name=f"{tag}_dw_gu",
                    after=after_dw_down(dw_down) if after_dw_down else None)
    if token_seqs is None:
        dh, dnw = _d_norm_in(dgu, w_gu, h, norm_w, dout, f"{tag}_d_in", after=dw_gu)
    else:
        if told is not None:
            told("dw", (dw_gu, dw_down))
        dn = _matmul(dgu, w_gu, mode="nt", out_dtype=F32, name=f"{tag}_d_norm", after=dw_gu)
        dx, dm, dnw = _rms_bwd_tokens(h, norm_w, dn, dout, token_seqs, f"{tag}_d_in",
                                      after=told("d_norm", dn) if told is not None else None)
        dh = (dx, dm)
    return dh, dnw, dw_gu, dw_down


def _split_w_in(w_in_full):
    pts = [0]
    for s in IN_SIZES:
        pts.append(pts[-1] + s)
    sl = lambda i, j: w_in_full[:, pts[i]:pts[j]]
    qfig = sl(3, 7).reshape(D_MODEL, 4, HG_HEADS, 128).transpose(0, 2, 1, 3).reshape(D_MODEL, 4 * D_MODEL)
    return {"z": sl(0, 1), "xbc": sl(1, 2), "dt": jnp.pad(sl(2, 3), ((0, 0), (0, 128 - SSD_HEADS))),
            "qfig": qfig, "gates": sl(7, 9)}


def _local_step(x, target, W):
    B, S, _ = x.shape
    T = N_META + S
    pad = (-T) % CHUNK
    Tp = T + pad
    assert pad + N_META == CHUNK
    R = B * Tp
    meta = jnp.broadcast_to(W["meta_tokens"][None], (B, N_META, D_MODEL))
    h0 = jnp.concatenate([jnp.zeros((B, pad, D_MODEL), F32), meta, x], axis=1).reshape(R, D_MODEL)

    stage = W.get("_stage", lambda name, x: {})
    W = dict(W)
    (h1, um), sv1 = _ffn_fwd(h0, W["ffn1_norm"], W["ffn1_w_gu"], W["ffn1_w_down"], "ffn1",
                             lambda n: W.update(stage("ffn1_norm", n)), next_norm_w=W["mix_norm"])
    W.update(stage("ffn1_out", h1))
    wi = W["w_in"]
    z = _matmul(um, wi["z"], mode="nn", out_dtype=BF16, name="in_z")
    xbc = _matmul(um, wi["xbc"], mode="nn", out_dtype=F32, name="in_xbc")
    dtr = _matmul(um, wi["dt"], mode="nn", out_dtype=F32, name="in_dt")
    qfig = _matmul(um, wi["qfig"], mode="nn", out_dtype=F32, name="in_qfig")
    gates = _matmul(um, wi["gates"], mode="nn", out_dtype=BF16, name="in_gates")

    r3 = lambda t: t.reshape(B, Tp, t.shape[-1])
    lane_pad = lambda t: jnp.pad(t, ((0, 0), (0, 128 - t.shape[1])))
    dt_bias, a_log, dskip = lane_pad(W["ssd_dt_bias"]), lane_pad(W["ssd_a_log"]), lane_pad(W["ssd_d"])
    xact = _conv_fwd(r3(xbc), W["ssd_conv_w"], W["ssd_conv_b"], pad, "conv_fwd")
    ya, ssd_saved, *ssd_kept = _ssd_fwd(xact, r3(dtr), r3(z), dt_bias, a_log, dskip, W["ssd_norm"], pad, "ssd_fwd")
    lbh = W["hg_lower_bound"].reshape(2, HG_HEADS, 128).transpose(1, 0, 2)
    nwh = W["hg_norm"].reshape(HG_HEADS, 1, 128)
    yb, hg_saved, *hg_kept = _hg_fwd(r3(qfig), lbh, nwh, pad, "hg_fwd")
    ya2, yb2 = ya.reshape(R, -1), yb.reshape(R, -1)
    W.update(stage("mixers_out", yb2))
    pa, pb, mg = _branch_merge(ya2, yb2, W["w_branch_a"], W["w_branch_b"], gates, "branch_merge")
    h2, n2 = _residual_matmul(mg, W["w_out"], h1, 1.0, "mix_out", W["ffn2_norm"])
    h3, sv2 = _ffn_fwd(h2, W["ffn2_norm"], W["ffn2_w_gu"], W["ffn2_w_down"], "ffn2", n=n2)

    loss, dh3, d_final = _loss_head(h3, W["final_norm"].reshape(1, D_MODEL), target, B, "loss_head")

    G = {"final_norm": d_final[0]}
    dh2, dnw, G["ffn2_w_gu"], G["ffn2_w_down"] = _ffn_bwd(h2, W["ffn2_norm"], W["ffn2_w_gu"], W["ffn2_w_down"], sv2, dh3, "ffn2")
    G["ffn2_norm"] = dnw[0:1]
    dmg = _matmul(dh2, W["w_out"], mode="nt", out_dtype=BF16, name="d_merge")
    G["w_out"] = _matmul(mg, dh2, mode="tn", out_dtype=F32, name="dw_out")
    dpa, dpb, dgates, dya, dyb = _branch_merge_bwd(pa, pb, gates, dmg, W["w_branch_a"], W["w_branch_b"], "branch_merge_bwd")
    G["w_branch_a"] = _matmul(ya2, dpa, mode="tn", out_dtype=F32, name="dw_branch_a")
    G["w_branch_b"] = _matmul(yb2, dpb, mode="tn", out_dtype=F32, name="dw_branch_b")

    dxact, ddtr, dz, dpar, dnw = _ssd_bwd(xact, r3(dtr), r3(z), dt_bias, a_log, dskip, W["ssd_norm"], ssd_saved, ssd_kept,
                                          r3(dya), pad, "ssd_bwd", after=stage("late_grads", G).get("_after"))
    G["ssd_dt_bias"], G["ssd_a_log"], G["ssd_d"] = dpar[0:1, :SSD_HEADS], dpar[1:2, :SSD_HEADS], dpar[2:3, :SSD_HEADS]
    G["ssd_norm"] = dnw[0:1]
    dxbc, dcw, dcb = _conv_bwd(r3(xbc), W["ssd_conv_w"], W["ssd_conv_b"], dxact, pad, "conv_bwd")
    G["ssd_conv_w"], G["ssd_conv_b"] = dcw[0:SSD_CONV], dcb[0:1]
    dqfig, dlb, dhn = _hg_bwd(r3(qfig), lbh, nwh, hg_saved, hg_kept, r3(dyb), pad, "hg_bwd",
                              after=stage("after_conv_bwd", dcb).get("_after"))
    G["hg_lower_bound"] = dlb[:, 0:2, :].transpose(1, 0, 2).reshape(2, D_MODEL)
    G["hg_norm"] = dhn[:, 0, :].reshape(1, D_MODEL)

    r2 = lambda t: t.reshape(R, t.shape[-1])
    pieces = [("z", r2(dz)), ("xbc", r2(dxbc)), ("dt", r2(ddtr)), ("qfig", r2(dqfig)), ("gates", dgates)]
    dum = _sum_nt([p for _, p in pieces], [wi[nm] for nm, _ in pieces], "d_mix")
    dwi = {nm: _matmul(um, dpiece, mode="tn", out_dtype=F32, name=f"dw_in_{nm}") for nm, dpiece in pieces}
    dw_qfig = dwi["qfig"].reshape(D_MODEL, HG_HEADS, 4, 128).transpose(0, 2, 1, 3).reshape(D_MODEL, 4 * D_MODEL)
    G["w_in"] = jnp.concatenate([dwi["z"], dwi["xbc"], dwi["dt"][:, :SSD_HEADS], dw_qfig, dwi["gates"]], axis=1)
    dh1, dnw = _rms_bwd(h1, W["mix_norm"], dum, dh2, "mix_norm_bwd", after=stage("w_in_grads", dwi).get("_after"))
    G["mix_norm"] = dnw[0:1]
    (dx, dfirst), dnw, G["ffn1_w_gu"], G["ffn1_w_down"] = _ffn_bwd(
        h0, W["ffn1_norm"], W["ffn1_w_gu"], W["ffn1_w_down"], sv1, dh1, "ffn1",
        lambda dw: stage("ffn1_dw_down", dw).get("_after"), token_seqs=B,
        told=lambda name, t: stage("ffn1_" + name, t).get("_after"))
    G["ffn1_norm"] = dnw[0:1]
    G["meta_tokens"] = jnp.sum(dfirst[:, pad:CHUNK], axis=0)
    return loss, dx, G


ANY = pl.BlockSpec(memory_space=pl.ANY)


def _place():
    return lax.axis_index("x"), lax.axis_index("y"), lax.axis_index("c")


def _other_chips(x, y):
    return [(1 - x, y), (x, 1 - y), (1 - x, 1 - y)]


def _remote(src, dst, ssem, rsem, dev):
    return pltpu.make_async_remote_copy(src_ref=src, dst_ref=dst, send_sem=ssem, recv_sem=rsem,
                                        device_id=dev, device_id_type=MESH)


def _exchange8(buf, name):
    n, w = buf.shape

    def body(x_ref, out_ref, ssem, rsem):
        x, y, c = _place()
        me = 4 * x + 2 * y + c
        out_ref[me] = x_ref[...]
        copies = []
        for k in range(1, 8):
            px = 1 - x if (k >> 2) & 1 else x
            py = 1 - y if (k >> 1) & 1 else y
            pc = 1 - c if k & 1 else c
            cp = _remote(x_ref, out_ref.at[me], ssem.at[k - 1], rsem.at[k - 1], (px, py, pc))
            cp.start()
            copies.append((cp, 4 * px + 2 * py + pc))
        for k, (cp, peer) in enumerate(copies):
            _remote(x_ref, out_ref.at[peer], ssem.at[k], rsem.at[k], (x, y, c)).wait_recv()
        for cp, _ in copies:
            cp.wait_send()

    vm = pl.BlockSpec(memory_space=pltpu.VMEM)
    return pl.pallas_call(
        body, in_specs=[vm], out_specs=vm, out_shape=jax.ShapeDtypeStruct((8, n, w), F32),
        scratch_shapes=[pltpu.SemaphoreType.DMA((7,)), pltpu.SemaphoreType.DMA((7,))], name=name,
    )(buf)


HBM = pltpu.MemorySpace.HBM


def _sequencer(name, collective_id, sems, sent):
    return functools.partial(pl.kernel, mesh=plsc.ScalarSubcoreMesh(axis_name="sequencer", num_cores=1), name=name,
                             scratch_types=sems, compiler_params=pltpu.CompilerParams(collective_id=collective_id),
                             cost_estimate=pl.CostEstimate(flops=0, transcendentals=0, bytes_accessed=2 * sent,
                                                           remote_bytes_transferred=sent))


def _nbytes(arrays):
    return sum(a.size * a.dtype.itemsize for a in arrays)


def _handshake(peers):
    barrier = pltpu.get_barrier_semaphore()
    for peer in peers:
        pl.semaphore_signal(barrier, inc=1, device_id=peer, device_id_type=MESH)
    pl.semaphore_wait(barrier, len(peers))


def _gather_seq(blocks, name, collective_id):
    n = len(blocks)
    half = [s.shape[1] // 2 for s in blocks]
    full = [jax.new_ref(b, memory_space=HBM) for b in blocks]

    @_sequencer(name, collective_id, [pltpu.SemaphoreType.DMA((n, 3))] * 4, _nbytes(blocks) * 3 // 4)
    def launch(ssem, rsem, fssem, frsem):
        x, y, c = _place()
        q = 2 * x + y
        chips = _other_chips(x, y)
        _handshake([(px, py, c) for px, py in chips] + [(x, y, 1 - c)])
        piece = lambda s, qq, cc: full[s].at[qq, pl.ds(cc * half[s], half[s])]
        sends = []
        for j, (px, py) in enumerate(chips):
            for s in range(n):
                cp = _remote(piece(s, q, c), piece(s, q, c), ssem.at[s, j], rsem.at[s, j], (px, py, c))
                cp.start()
                sends.append(cp)
        for j, (px, py) in enumerate(chips):
            for s in range(n):
                got = piece(s, 2 * px + py, c)
                _remote(got, got, ssem.at[s, j], rsem.at[s, j], (px, py, c)).wait_recv()
                cp = _remote(got, got, fssem.at[s, j], frsem.at[s, j], (x, y, 1 - c))
                cp.start()
                sends.append(cp)
        for j, (px, py) in enumerate(chips):
            for s in range(n):
                got = piece(s, 2 * px + py, 1 - c)
                _remote(got, got, fssem.at[s, j], frsem.at[s, j], (x, y, 1 - c)).wait_recv()
        for cp in sends:
            cp.wait_send()

    launch()
    return [r[...] for r in full]


def _share8(buf, name, collective_id):
    n, w = buf.shape
    src = jax.new_ref(buf, memory_space=HBM)
    out = jax.empty_ref(jax.ShapeDtypeStruct((8, n, w), F32), memory_space=HBM)

    @_sequencer(name, collective_id, [pltpu.SemaphoreType.DMA((7,)), pltpu.SemaphoreType.DMA((7,)), pltpu.SemaphoreType.DMA((1,))],
                7 * buf.size * 4)
    def launch(ssem, rsem, lsem):
        x, y, c = _place()
        me = 4 * x + 2 * y + c
        peers = [(1 - x if (k >> 2) & 1 else x, 1 - y if (k >> 1) & 1 else y, 1 - c if k & 1 else c) for k in range(1, 8)]
        _handshake(peers)
        mine = pltpu.make_async_copy(src, out.at[me], lsem.at[0])
        mine.start()
        sends = []
        for k, peer in enumerate(peers):
            cp = _remote(src, out.at[me], ssem.at[k], rsem.at[k], peer)
            cp.start()
            sends.append(cp)
        for k, (px, py, pc) in enumerate(peers):
            slot = out.at[4 * px + 2 * py + pc]
            _remote(slot, slot, ssem.at[k], rsem.at[k], (px, py, pc)).wait_recv()
        for cp in sends:
            cp.wait_send()
        mine.wait()

    launch()
    return out[...]


def _sum_slots(slots, name, after=None):
    _, n, w = slots.shape

    def body(s_ref, o_ref):
        acc = s_ref[0]
        for d in range(1, 8):
            acc = acc + s_ref[d]
        o_ref[...] = acc

    vm = pl.BlockSpec(memory_space=pltpu.VMEM)
    in_specs, args = [vm], [slots]
    if after is not None:
        body = _skip_ref(body, 1)
        args.append(_deps(after))
        in_specs.append(vm)
    return pl.pallas_call(body, in_specs=in_specs, out_specs=vm, out_shape=jax.ShapeDtypeStruct((n, w), F32), name=name)(*args)


def _pair_swap(parts, name, collective_id):
    n = len(parts)
    half = [p.shape[1] // 2 for p in parts]
    src = [jax.new_ref(p, memory_space=HBM) for p in parts]
    got = [jax.empty_ref(jax.ShapeDtypeStruct((p.shape[0], h, p.shape[2]), p.dtype), memory_space=HBM) for p, h in zip(parts, half)]

    @_sequencer(name, collective_id, [pltpu.SemaphoreType.DMA((n,))] * 2, _nbytes(parts) // 2)
    def launch(ssem, rsem):
        x, y, c = _place()
        _handshake([(x, y, 1 - c)])
        copies = []
        for s in range(n):
            cp = _remote(src[s].at[pl.ds(0, parts[s].shape[0]), pl.ds((1 - c) * half[s], half[s])], got[s], ssem.at[s], rsem.at[s], (x, y, 1 - c))
            cp.start()
            copies.append(cp)
        for cp in copies:
            cp.wait_recv()
        for cp in copies:
            cp.wait_send()

    launch()
    return [g[...] for g in got]


def _to_owners(sums, name, collective_id):
    n = len(sums)
    src = [jax.new_ref(s, memory_space=HBM) for s in sums]
    got = [jax.empty_ref(jax.ShapeDtypeStruct(s.shape, s.dtype), memory_space=HBM) for s in sums]

    @_sequencer(name, collective_id, [pltpu.SemaphoreType.DMA((n, 3))] * 2, _nbytes(sums) * 3 // 4)
    def launch(ssem, rsem):
        x, y, c = _place()
        q = 2 * x + y
        chips = _other_chips(x, y)
        _handshake([(px, py, c) for px, py in chips])
        sends = []
        for j, (px, py) in enumerate(chips):
            for s in range(n):
                cp = _remote(src[s].at[2 * px + py], got[s].at[q], ssem.at[s, j], rsem.at[s, j], (px, py, c))
                cp.start()
                sends.append(cp)
        for j, (px, py) in enumerate(chips):
            for s in range(n):
                slot = got[s].at[2 * px + py]
                _remote(slot, slot, ssem.at[s, j], rsem.at[s, j], (px, py, c)).wait_recv()
        for cp in sends:
            cp.wait_send()

    launch()
    return [g[...] for g in got]


def _pair_join(blocks, name, collective_id):
    n = len(blocks)
    out = [jax.new_ref(b, memory_space=HBM) for b in blocks]

    @_sequencer(name, collective_id, [pltpu.SemaphoreType.DMA((n,))] * 2, _nbytes(blocks) // 2)
    def launch(ssem, rsem):
        x, y, c = _place()
        _handshake([(x, y, 1 - c)])
        sends = []
        for s in range(n):
            h = blocks[s].shape[0] // 2
            mine = out[s].at[pl.ds(c * h, h)]
            cp = _remote(mine, mine, ssem.at[s], rsem.at[s], (x, y, 1 - c))
            cp.start()
            sends.append(cp)
        for s in range(n):
            h = blocks[s].shape[0] // 2
            theirs = out[s].at[pl.ds((1 - c) * h, h)]
            _remote(theirs, theirs, ssem.at[s], rsem.at[s], (x, y, 1 - c)).wait_recv()
        for cp in sends:
            cp.wait_send()

    launch()
    return [o[...] for o in out]


WIRE = BF16


def _row_tile(h):
    return _pick(h, (256, 368, 352, 128, 16))


def _add_pair(part, got, c, name, after=None):
    _, h, w = got.shape
    tr = _row_tile(h)
    nt = h // tr

    def body(c_ref, p_ref, g_ref, o_ref):
        o_ref[...] = (p_ref[...] + g_ref[...].astype(F32)).astype(o_ref.dtype)

    in_specs = [pl.BlockSpec((None, tr, w), lambda q, i, c_ref: (q, c_ref[0] * nt + i, 0)),
                pl.BlockSpec((None, tr, w), lambda q, i, c_ref: (q, i, 0))]
    args = [c.reshape(1).astype(jnp.int32), part, got]
    if after is not None:
        body = _skip_ref(body, len(args))
        args.append(_deps(after))
        in_specs.append(_dep_spec(args[-1]))
    return pl.pallas_call(
        body,
        grid_spec=pltpu.PrefetchScalarGridSpec(
            num_scalar_prefetch=1, grid=(got.shape[0], nt), in_specs=in_specs,
            out_specs=pl.BlockSpec((None, tr, w), lambda q, i, c_ref: (q, i, 0))),
        out_shape=jax.ShapeDtypeStruct(got.shape, WIRE), name=name,
        compiler_params=_cparams(("arbitrary", "arbitrary")),
    )(*args)


def _sum_chips(slots, sums, q, c, name, after=None):
    _, h, w = slots.shape
    tr = _row_tile(h)
    nt = h // tr

    def body(s_ref, mine_ref, a_ref, b_ref, d_ref, o_ref):
        o_ref[...] = ((mine_ref[...].astype(F32) + a_ref[...].astype(F32)) + b_ref[...].astype(F32)) + d_ref[...].astype(F32)

    slot = lambda k: pl.BlockSpec((None, tr, w), lambda i, s_ref: (s_ref[1 + k], i, 0))
    scalars = jnp.stack([c, q, (q + 1) % N_CHIPS, (q + 2) % N_CHIPS, (q + 3) % N_CHIPS]).astype(jnp.int32)
    in_specs, args = [slot(0), slot(1), slot(2), slot(3)], [scalars, sums, slots, slots, slots]
    if after is not None:
        body = _skip_ref(body, len(args))
        args.append(_deps(after))
        in_specs.append(_dep_spec(args[-1]))
    return pl.pallas_call(
        body,
        grid_spec=pltpu.PrefetchScalarGridSpec(
            num_scalar_prefetch=1, grid=(nt,), in_specs=in_specs,
            out_specs=pl.BlockSpec((tr, w), lambda i, s_ref: (s_ref[0] * nt + i, 0))),
        out_shape=jax.ShapeDtypeStruct((2 * h, w), F32), name=name,
        compiler_params=_cparams(("arbitrary",)),
    )(*args)


class _Reduce:
    def __init__(self, parts, q, c, tag, first_id, regions=None):
        self.parts, self.q, self.c, self.tag, self.first_id, self.regions = parts, q, c, tag, first_id, regions
        self.got = _pair_swap(parts, f"{tag}_pair_swap", first_id)

    def to_owners(self, after=None):
        self.sums = [_add_pair(p, g, self.c, f"{self.tag}_pair_add{i}", after)
                     for i, (p, g) in enumerate(zip(self.parts, self.got))]
        if self.regions is not None:
            self.sums = self.regions(self.sums)
        self.slots = _to_owners(self.sums, f"{self.tag}_to_owners", self.first_id + 1)
        return self.sums

    def join(self, after=None):
        blocks = [_sum_chips(sl, sm, self.q, self.c, f"{self.tag}_sum_chips{i}", after)
                  for i, (sl, sm) in enumerate(zip(self.slots, self.sums))]
        self.out = _pair_join(blocks, f"{self.tag}_pair_join", self.first_id + 2)
        return blocks


WEIGHTS = ("meta_tokens", "ffn1_norm", "ffn1_w_gu", "ffn1_w_down", "mix_norm", "w_in", "ssd_conv_w", "ssd_conv_b",
           "ssd_dt_bias", "ssd_a_log", "ssd_d", "ssd_norm", "hg_lower_bound", "hg_norm", "w_branch_a", "w_branch_b",
           "w_out", "ffn2_norm", "ffn2_w_gu", "ffn2_w_down", "final_norm")
BIG = ("ffn1_w_gu", "ffn1_w_down", "w_in", "w_branch_a", "w_branch_b", "w_out", "ffn2_w_gu", "ffn2_w_down")
SMALL = tuple(n for n in WEIGHTS if n not in BIG)


def _rows1024(a):
    flat = a.reshape(-1)
    n = -(-flat.shape[0] // 1024) * 1024
    return jnp.pad(flat, (0, n - flat.shape[0])).reshape(-1, 1024)


def kernel(x, meta_tokens, ffn1_norm, ffn1_w_gu, ffn1_w_down, mix_norm, w_in, ssd_conv_w, ssd_conv_b, ssd_dt_bias, ssd_a_log, ssd_d, ssd_norm, hg_lower_bound, hg_norm, w_branch_a, w_branch_b, w_out, ffn2_norm, ffn2_w_gu, ffn2_w_down, final_norm, loss_target, m_meta_tokens, m_ffn1_norm, m_ffn1_w_gu, m_ffn1_w_down, m_mix_norm, m_w_in, m_ssd_conv_w, m_ssd_conv_b, m_ssd_dt_bias, m_ssd_a_log, m_ssd_d, m_ssd_norm, m_hg_lower_bound, m_hg_norm, m_w_branch_a, m_w_branch_b, m_w_out, m_ffn2_norm, m_ffn2_w_gu, m_ffn2_w_down, m_final_norm, v_meta_tokens, v_ffn1_norm, v_ffn1_w_gu, v_ffn1_w_down, v_mix_norm, v_w_in, v_ssd_conv_w, v_ssd_conv_b, v_ssd_dt_bias, v_ssd_a_log, v_ssd_d, v_ssd_norm, v_hg_lower_bound, v_hg_norm, v_w_branch_a, v_w_branch_b, v_w_out, v_ffn2_norm, v_ffn2_w_gu, v_ffn2_w_down, v_final_norm):
    P = dict(zip(WEIGHTS, (meta_tokens, ffn1_norm, ffn1_w_gu, ffn1_w_down, mix_norm, w_in, ssd_conv_w, ssd_conv_b, ssd_dt_bias, ssd_a_log, ssd_d, ssd_norm, hg_lower_bound, hg_norm, w_branch_a, w_branch_b, w_out, ffn2_norm, ffn2_w_gu, ffn2_w_down, final_norm)))
    M = dict(zip(WEIGHTS, (m_meta_tokens, m_ffn1_norm, m_ffn1_w_gu, m_ffn1_w_down, m_mix_norm, m_w_in, m_ssd_conv_w, m_ssd_conv_b, m_ssd_dt_bias, m_ssd_a_log, m_ssd_d, m_ssd_norm, m_hg_lower_bound, m_hg_norm, m_w_branch_a, m_w_branch_b, m_w_out, m_ffn2_norm, m_ffn2_w_gu, m_ffn2_w_down, m_final_norm)))
    V = dict(zip(WEIGHTS, (v_meta_tokens, v_ffn1_norm, v_ffn1_w_gu, v_ffn1_w_down, v_mix_norm, v_w_in, v_ssd_conv_w, v_ssd_conv_b, v_ssd_dt_bias, v_ssd_a_log, v_ssd_d, v_ssd_norm, v_hg_lower_bound, v_hg_norm, v_w_branch_a, v_w_branch_b, v_w_out, v_ffn2_norm, v_ffn2_w_gu, v_ffn2_w_down, v_final_norm)))
    cx, cy, cc = _place()
    q = 2 * cx + cy

    mine = jnp.concatenate([meta_tokens.reshape(4, 1024), ssd_conv_w.reshape(2, 1024), jnp.zeros((2, 1024), F32)], axis=0)
    every = _exchange8(mine, "gather_small")
    meta_full = jnp.concatenate([every[2 * k, 0:4].reshape(N_META, 256) for k in range(N_CHIPS)], axis=1)
    conv_w_full = jnp.concatenate([every[2 * k, 4:6].reshape(SSD_CONV, 512) for k in range(N_CHIPS)], axis=1)

    late = ("ffn2_w_down", "w_branch_a", "w_branch_b", "w_out")
    rows = jnp.concatenate([P[n][0] for n in late], axis=0)
    zero = lambda t, dtype=F32: (t[0:1, 0:1] * 0).astype(dtype)

    def in_slot(s, after=None):
        s = s if after is None else s + zero(after)
        return lax.dynamic_update_slice(lax.empty((N_CHIPS,) + s.shape, BF16), s.astype(BF16)[None], (q, 0, 0))

    gu1, down1 = _gather_seq([in_slot(ffn1_w_gu[0]), in_slot(ffn1_w_down[0])], "gather_ffn1", 1)
    W = {n: P[n] for n in SMALL}
    W["meta_tokens"], W["ssd_conv_w"] = meta_full, conv_w_full
    W["ffn1_w_gu"], W["ffn1_w_down"] = gu1, down1.reshape(-1, D_MODEL)
    flying = {}

    def stage(name, t):
        if name == "ffn1_norm":
            flying["w_in"] = _gather_seq([in_slot(w_in[0], t)], "gather_w_in", 2)
            return {}
        if name == "ffn1_out":
            flying["late"] = _gather_seq([in_slot(ffn2_w_gu[0], t), in_slot(rows, t)], "gather_late", 3)
            (w_in_all,) = flying["w_in"]
            w_in_all = w_in_all + zero(t, BF16)
            return {"w_in": _split_w_in(w_in_all.transpose(1, 0, 2).reshape(D_MODEL, -1))}
        if name == "mixers_out":
            gu2, rows_all = flying["late"]
            out, r = {"ffn2_w_gu": gu2}, 0
            for n in late:
                nr = P[n].shape[1]
                out[n] = (rows_all[:, r:r + nr] + zero(t, BF16)).reshape(N_CHIPS * nr, D_MODEL)
                r += nr
            return out
        if name == "late_grads":
            parts = [t["ffn2_w_gu"]] + [t[n].reshape(N_CHIPS, -1, D_MODEL) for n in late]
            flying["grad_late"] = _Reduce(parts, q, cc, "grad_late", 4)
            return {"_after": [t["ffn2_w_gu"]] + [t[n] for n in late]}
        if name == "after_conv_bwd":
            return {"_after": flying["grad_late"].to_owners(after=t)}
        if name == "w_in_grads":
            order = ("z", "xbc", "dt", "qfig", "gates")
            blocks = flying["grad_late"].join(after=[t[k] for k in order])

            def regions(sums):
                z, xbc, dt, qfig, gates = [s[0] for s in sums]
                h = z.shape[0]
                qfig = qfig.reshape(h, HG_HEADS, 4, 128).transpose(0, 2, 1, 3).reshape(h, 4 * D_MODEL)
                cols = jnp.concatenate([z, xbc, dt[:, :SSD_HEADS], qfig, gates], axis=1)
                return [cols.reshape(h, N_CHIPS, -1).transpose(1, 0, 2)]

            flying["grad_w_in"] = _Reduce([t[k][None] for k in order], q, cc, "grad_w_in", 7, regions)
            return {"_after": blocks}
        if name == "ffn1_dw_down":
            return {"_after": flying["grad_w_in"].to_owners(after=t)}
        if name == "ffn1_dw":
            dw_gu, dw_down = t
            flying["grad_ffn1"] = _Reduce([dw_gu, dw_down.reshape(N_CHIPS, -1, D_MODEL)], q, cc, "grad_ffn1", 10)
            return {}
        if name == "ffn1_d_norm":
            blocks = flying["grad_w_in"].join(after=t)
            return {"_after": flying["grad_ffn1"].to_owners(after=blocks)}
        return {}

    W["_stage"] = stage

    loss8, grad_x, G = _local_step(x, loss_target, W)

    small = jnp.concatenate(
        [G["meta_tokens"]] + [_rows1024(G[n]) for n in SMALL if n != "meta_tokens"] + [_rows1024(loss8[0:1, 0:1])], axis=0)
    small = jnp.pad(small, ((0, 40 - small.shape[0]), (0, 0)))
    small_slots = _share8(small, "share_small", 13)

    grad_ffn1 = flying["grad_ffn1"]
    going = grad_ffn1.sums
    (g_w_in,) = flying["grad_w_in"].out
    Gb = dict(zip(("ffn2_w_gu",) + late, flying["grad_late"].out))
    Gb["w_in"] = g_w_in

    grads, delta, new_m, new_v, done = {}, {}, {}, {}, []
    cols = w_in.shape[2]
    to_tiles = lambda a: a.transpose(2, 0, 1).reshape(cols, 8, 128).reshape(cols * 8, 128)
    from_tiles = lambda a: a.reshape(cols, 1, D_MODEL).transpose(1, 2, 0)
    for n in [n for n in BIG if n in Gb]:
        if n == "w_in":
            g_t = to_tiles(Gb[n][None])
            d_, m_, v_ = _adamw(to_tiles(P[n]), g_t, to_tiles(M[n]), to_tiles(V[n]), f"adamw_{n}", after=going)
            grads[n], delta[n], new_m[n], new_v[n] = from_tiles(g_t), from_tiles(d_), from_tiles(m_), from_tiles(v_)
        else:
            d_, m_, v_ = _adamw(P[n][0], Gb[n], M[n][0], V[n][0], f"adamw_{n}", after=going)
            grads[n], delta[n], new_m[n], new_v[n] = Gb[n][None], d_[None], m_[None], v_[None]
        done.append(d_)

    small = _sum_slots(small_slots, "sum_small", after=done)
    Gs = {"meta_tokens": small[0:N_META]}
    r = N_META
    for n in SMALL:
        if n == "meta_tokens":
            continue
        nr = -(-G[n].size // 1024)
        Gs[n] = small[r:r + nr].reshape(-1)[:G[n].size].reshape(G[n].shape)
        r += nr
    loss = small[r, 0]
    Gs["meta_tokens"] = lax.dynamic_slice(Gs["meta_tokens"], (0, 256 * q), (N_META, 256))
    Gs["ssd_conv_w"] = lax.dynamic_slice(Gs["ssd_conv_w"], (0, 512 * q), (SSD_CONV, 512))[None]
    Gs = {n: Gs[n].reshape(P[n].shape) for n in SMALL}
    grads.update(Gs)
    flat = lambda a: a.reshape(-1, a.shape[-1])
    d_s, m_s, v_s = _adamw_many(*[[flat(D[n]) for n in SMALL] for D in (P, Gs, M, V)], "adamw_small")
    for i, n in enumerate(SMALL):
        delta[n], new_m[n], new_v[n] = d_s[i].reshape(P[n].shape), m_s[i].reshape(P[n].shape), v_s[i].reshape(P[n].shape)
    done.append(d_s[0])
    grad_ffn1.join(after=done)
    Gb["ffn1_w_gu"], Gb["ffn1_w_down"] = grad_ffn1.out
    for n in ("ffn1_w_gu", "ffn1_w_down"):
        d_, m_, v_ = _adamw(P[n][0], Gb[n], M[n][0], V[n][0], f"adamw_{n}")
        grads[n], delta[n], new_m[n], new_v[n] = Gb[n][None], d_[None], m_[None], v_[None]
    return (loss, grad_x, *[grads[n] for n in WEIGHTS], *[delta[n] for n in WEIGHTS],
            *[new_m[n] for n in WEIGHTS], *[new_v[n] for n in WEIGHTS])
```

```python
import functools

import jax
import jax.numpy as jnp
from jax import lax
from jax.experimental import pallas as pl
from jax.experimental.pallas import tpu as pltpu
from jax.experimental.pallas import tpu_sc as plsc

F32 = jnp.float32
BF16 = jnp.bfloat16
HIGHEST = lax.Precision.HIGHEST
MESH = pl.DeviceIdType.MESH

D_MODEL = 1024
N_META = 16
EPS = 1e-6
SSD_HEADS = 16
SSD_HEAD_DIM = 64
SSD_INNER = 1024
SSD_GROUPS = 4
SSD_STATE = 128
SSD_CONV = 4
SSD_CONV_CH = 2048
HG_HEADS = 8
HG_SUB = 32
CHUNK = 128
D_FF = 2816
N_CHIPS = 4
IN_SIZES = (1024, 2048, 16, 1024, 1024, 1024, 1024, 1024, 1024)
ADAM_LR = 0.001
ADAM_B1 = 0.9
ADAM_B2 = 0.999
ADAM_EPS = 1e-08
ADAM_WD = 0.01
ADAM_STEP = 10
VMEM_LIMIT = 56 * 1024 * 1024
MATMUL_BLOCK_BYTES = 44 * 1024 * 1024
ADAMW_BLOCK_BYTES = 5 * 512 * 1024


def _cparams(sem=None):
    return pltpu.CompilerParams(dimension_semantics=sem, vmem_limit_bytes=VMEM_LIMIT)


def _pick(n, cands):
    for c in cands:
        if n % c == 0:
            return c
    return n


def _deps(after):
    xs = after if isinstance(after, (list, tuple)) else [after]
    one = lambda x: lax.slice(x, (0,) * x.ndim, (1,) * x.ndim).reshape(1).astype(F32)
    return jnp.concatenate([one(x) for x in xs]).reshape(1, -1)


def _dep_spec(dep):
    return pl.BlockSpec(dep.shape, lambda *_: (0, 0))


def _skip_ref(body, pos):
    return lambda *refs: body(*refs[:pos], *refs[pos + 1:])


def _dg(a, b, ca, cb):
    return lax.dot_general(a.astype(BF16), b.astype(BF16), (((ca,), (cb,)), ((), ())), preferred_element_type=F32)


@jax.custom_vjp
def _mm(a, b):
    return _dg(a, b, 1, 0)


def _mm_fwd(a, b):
    return _dg(a, b, 1, 0), (a, b)


def _mm_bwd(r, g):
    a, b = r
    return _dg(g, b, 1, 1), _dg(a, g, 0, 0)


_mm.defvjp(_mm_fwd, _mm_bwd)


@jax.custom_vjp
def _mm_nt(a, b):
    return _dg(a, b, 1, 1)


def _mm_nt_fwd(a, b):
    return _dg(a, b, 1, 1), (a, b)


def _mm_nt_bwd(r, g):
    a, b = r
    return _dg(g, b, 1, 0), _dg(g, a, 0, 0)


_mm_nt.defvjp(_mm_nt_fwd, _mm_nt_bwd)


@jax.custom_vjp
def _mm_tn(a, b):
    return _dg(a, b, 0, 0)


def _mm_tn_fwd(a, b):
    return _dg(a, b, 0, 0), (a, b)


def _mm_tn_bwd(r, g):
    a, b = r
    return _dg(b, g, 1, 1), _dg(a, g, 1, 0)


_mm_tn.defvjp(_mm_tn_fwd, _mm_tn_bwd)


def _tri_sum(x, lower):
    n = x.shape[0]
    ri = lax.broadcasted_iota(jnp.int32, (n, n), 0)
    ci = lax.broadcasted_iota(jnp.int32, (n, n), 1)
    tri = ((ri >= ci) if lower else (ri <= ci)).astype(BF16)
    x1 = x.astype(BF16)
    r1 = x - x1.astype(F32)
    x2 = r1.astype(BF16)
    x3 = (r1 - x2.astype(F32)).astype(BF16)
    dot = lambda p: lax.dot_general(tri, p, (((1,), (0,)), ((), ())), preferred_element_type=F32)
    return (dot(x3) + dot(x2)) + dot(x1)


@jax.custom_vjp
def _cumsum_rows(x):
    return _tri_sum(x, True)


_cumsum_rows.defvjp(lambda x: (_tri_sum(x, True), None), lambda _, g: (_tri_sum(g, False),))


def _silu(x):
    return x * jax.nn.sigmoid(x)


def _softplus(x):
    return jnp.maximum(x, 0.0) + jnp.log(1.0 + jnp.exp(-jnp.abs(x)))


def _tril(n):
    ri = lax.broadcasted_iota(jnp.int32, (n, n), 0)
    ci = lax.broadcasted_iota(jnp.int32, (n, n), 1)
    return ri >= ci


def _row_of(m, r):
    sub = lax.broadcasted_iota(jnp.int32, (m.shape[0], 1), 0)
    return jnp.sum(jnp.where(sub == r, m, 0.0), axis=0, keepdims=True)


def _col_of(m, c):
    lane = lax.broadcasted_iota(jnp.int32, (1, m.shape[1]), 1)
    return jnp.sum(jnp.where(lane == c, m, 0.0), axis=1, keepdims=True)


def _matmul(a, b, *, mode, out_dtype, name, alpha=1.0, res=None, tm=None, tn=None, out_groups=None, after=None):
    b3 = b.ndim == 3
    if mode == "nn":
        M, K = a.shape
        G = b.shape[0] if b3 else 1
        Ng = b.shape[-1]
        N = G * Ng
    elif mode == "nt":
        M, K = a.shape
        G = b.shape[0] if b3 else 1
        N = b.shape[-2]
        Kg = b.shape[-1]
        assert G * Kg == K
    else:
        K, M = a.shape
        N = b.shape[1]
        G = out_groups or 1
        Ng = N // G
    has_res = res is not None
    split_n = (mode == "nn" and b3) or (mode == "tn" and G > 1)
    per_mn = jnp.dtype(out_dtype).itemsize + (res.dtype.itemsize if has_res else 0)
    fits = [(m_ * n_, m_, n_)
            for m_ in (4352, 2176, 1408, 1088, 1024, 544, 512, 256, 128) if M % m_ == 0
            for n_ in (2816, 2048, 1408, 1024, 512, 256, 128) if (Ng if split_n else N) % n_ == 0
            if 2 * (K * m_ * a.dtype.itemsize + K * n_ * b.dtype.itemsize + m_ * n_ * per_mn) + 4 * m_ * n_ <= MATMUL_BLOCK_BYTES]
    _, tm_fit, tn_fit = max(fits)
    tm, tn = tm or tm_fit, tn or tn_fit
    nm, nn_ = M // tm, N // tn
    assert nm * tm == M and nn_ * tn == N, (name, M, N, K, tm, tn)

    if mode == "nn":
        a_spec = pl.BlockSpec((tm, K), lambda i, j: (i, 0))
        if b3:
            ns = Ng // tn
            b_spec = pl.BlockSpec((None, K, tn), lambda i, j: (j // ns, 0, j % ns))
        else:
            b_spec = pl.BlockSpec((K, tn), lambda i, j: (0, j))
        ca, cb = 1, 0
    elif mode == "nt":
        a_spec = pl.BlockSpec((tm, K), lambda i, j: (i, 0))
        if b3:
            b_spec = pl.BlockSpec((G, tn, Kg), lambda i, j: (0, j, 0))
        else:
            b_spec = pl.BlockSpec((tn, K), lambda i, j: (j, 0))
        ca, cb = 1, 1
    else:
        a_spec = pl.BlockSpec((K, tm), lambda i, j: (0, i))
        b_spec = pl.BlockSpec((K, tn), lambda i, j: (0, j))
        ca, cb = 0, 0
    if mode == "tn" and G > 1:
        ns = Ng // tn
        o_spec = pl.BlockSpec((None, tm, tn), lambda i, j: (j // ns, i, j % ns))
        out_shape = jax.ShapeDtypeStruct((G, M, Ng), out_dtype)
    else:
        o_spec = pl.BlockSpec((tm, tn), lambda i, j: (i, j))
        out_shape = jax.ShapeDtypeStruct((M, N), out_dtype)
    in_specs = [a_spec, b_spec]
    args = [a, b]
    if has_res:
        in_specs.append(pl.BlockSpec((tm, tn), lambda i, j: (i, j)))
        args.append(res)
    if after is not None:
        args.append(_deps(after))
        in_specs.append(_dep_spec(args[-1]))

    def body(*refs):
        a_ref, b_ref, o_ref = refs[0], refs[1], refs[-1]
        if mode == "nt" and b3:
            o = _dg(a_ref[:, 0:Kg], b_ref[0], ca, cb)
            for g in range(1, G):
                o = o + _dg(a_ref[:, g * Kg:(g + 1) * Kg], b_ref[g], ca, cb)
        else:
            o = _dg(a_ref[...], b_ref[...], ca, cb)
        if alpha != 1.0:
            o = o * alpha
        if has_res:
            o = o + refs[2][...]
        o_ref[...] = o.astype(o_ref.dtype)

    return pl.pallas_call(
        body, grid=(nm, nn_), in_specs=in_specs, out_specs=o_spec, out_shape=out_shape, name=name,
        compiler_params=_cparams(("parallel", "parallel")),
    )(*args)


def _sum_nt(xs, ws, name):
    R, N = xs[0].shape[0], ws[0].shape[0]
    n = len(xs)
    per_m = sum(x.shape[1] * x.dtype.itemsize for x in xs)
    per_n = sum(w.shape[1] * w.dtype.itemsize for w in ws)
    fits = [(m_ * n_, m_, n_) for m_ in (1088, 544, 256, 128) if R % m_ == 0 for n_ in (1024, 512, 256, 128) if N % n_ == 0
            if 2 * (m_ * per_m + n_ * per_n + m_ * n_ * 4) + 4 * m_ * n_ <= MATMUL_BLOCK_BYTES]
    _, tm, tn = max(fits)

    def body(*refs):
        o = _dg(refs[0][...], refs[n][...], 1, 1)
        for p in range(1, n):
            o = o + _dg(refs[p][...], refs[n + p][...], 1, 1)
        refs[-1][...] = o

    return pl.pallas_call(
        body, grid=(R // tm, N // tn),
        in_specs=[pl.BlockSpec((tm, x.shape[1]), lambda i, j: (i, 0)) for x in xs]
        + [pl.BlockSpec((tn, w.shape[1]), lambda i, j: (j, 0)) for w in ws],
        out_specs=pl.BlockSpec((tm, tn), lambda i, j: (i, j)), out_shape=jax.ShapeDtypeStruct((R, N), F32), name=name,
        compiler_params=_cparams(("parallel", "parallel")),
    )(*xs, *ws)


def _rms_fn(h, w):
    r = lax.rsqrt(jnp.mean(h * h, axis=-1, keepdims=True) + EPS)
    return h * r * w


def _swiglu_fn(gu):
    g = gu[:, :D_FF].astype(F32)
    u = gu[:, D_FF:].astype(F32)
    return _silu(g) * u


def _merge_fn(pa, pb, gates):
    return jax.nn.sigmoid(gates[:, :D_MODEL]) * pa + jax.nn.sigmoid(gates[:, D_MODEL:]) * pb


def _rows_call(body, *, rows, tr, ins, outs, accs=(), name, after=None):
    n = rows // tr
    assert n * tr == rows
    if after is not None:
        body = _skip_ref(body, len(ins))
        ins = list(ins) + [("full", _deps(after))]

    def spec(x):
        if isinstance(x, tuple):
            shp = x[1].shape
            return pl.BlockSpec(shp, lambda i: (0,) * len(shp))
        return pl.BlockSpec((tr, x.shape[1]), lambda i: (i, 0))

    in_specs = [spec(x) for x in ins]
    args = [x[1] if isinstance(x, tuple) else x for x in ins]
    out_specs = [spec(x) for x in outs] + [pl.BlockSpec(x.shape, lambda i: (0,) * len(x.shape)) for x in accs]
    out_shape = [x[1] if isinstance(x, tuple) else x for x in outs] + list(accs)
    return pl.pallas_call(
        body, grid=(n,), in_specs=in_specs, out_specs=out_specs, out_shape=out_shape, name=name,
        compiler_params=_cparams(("arbitrary",)),
    )(*args)


def _acc_rows(ref, val):
    @pl.when(pl.program_id(0) == 0)
    def _():
        ref[...] = jnp.zeros_like(ref)

    ref[0:1, :] += val


def _rms_fwd(h, w, name):
    def body(h_ref, w_ref, o_ref):
        o_ref[...] = _rms_fn(h_ref[...], w_ref[...]).astype(o_ref.dtype)

    R = h.shape[0]
    return _rows_call(body, rows=R, tr=_pick(R, (256, 128)), ins=[h, ("full", w)],
                      outs=[jax.ShapeDtypeStruct(h.shape, BF16)], name=name)[0]


def _rms_bwd(h, w, dn, dres, name, after=None):
    def body(h_ref, w_ref, dn_ref, dres_ref, dh_ref, dw_ref):
        _, vjp = jax.vjp(_rms_fn, h_ref[...], w_ref[...])
        dh, dw = vjp(dn_ref[...].astype(F32))
        dh_ref[...] = dh + dres_ref[...]
        _acc_rows(dw_ref, dw)

    R = h.shape[0]
    return _rows_call(body, rows=R, tr=_pick(R, (256, 128)), ins=[h, ("full", w), dn, dres],
                      outs=[jax.ShapeDtypeStruct(h.shape, F32)], accs=[jax.ShapeDtypeStruct((8, D_MODEL), F32)], name=name,
                      after=after)


def _d_norm_in(dgu, w_gu, h, norm_w, dres, name, after=None):
    R = h.shape[0]
    G, _, kg = w_gu.shape

    def body(dgu_ref, w_ref, h_ref, nw_ref, dres_ref, dh_ref, dw_ref):
        dn = _dg(dgu_ref[:, 0:kg], w_ref[0], 1, 1)
        for g in range(1, G):
            dn = dn + _dg(dgu_ref[:, kg * g:kg * (g + 1)], w_ref[g], 1, 1)
        _, vjp = jax.vjp(_rms_fn, h_ref[...], nw_ref[...])
        dh, dw = vjp(dn)
        dh_ref[...] = dh + dres_ref[...]
        _acc_rows(dw_ref, dw)

    return _rows_call(body, rows=R, tr=_pick(R, (256, 128)), ins=[dgu, ("full", w_gu), h, ("full", norm_w), dres],
                      outs=[jax.ShapeDtypeStruct(h.shape, F32)], accs=[jax.ShapeDtypeStruct((8, D_MODEL), F32)], name=name,
                      after=after)


def _rms_bwd_tokens(h, w, dn, dres, nseq, name, after=None):
    Tp = h.shape[0] // nseq
    nc = Tp // CHUNK

    def body(h_ref, w_ref, dn_ref, dres_ref, dx_ref, dm_ref, dw_ref):
        b, c = pl.program_id(0), pl.program_id(1)
        _, vjp = jax.vjp(_rms_fn, h_ref[...], w_ref[...])
        dh, dw = vjp(dn_ref[...].astype(F32))
        dh = dh + dres_ref[...]

        @pl.when(c == 0)
        def _():
            dm_ref[...] = dh

        @pl.when(c > 0)
        def _():
            dx_ref[...] = dh

        @pl.when((b == 0) & (c == 0))
        def _():
            dw_ref[...] = jnp.zeros_like(dw_ref)

        dw_ref[0:1, :] += dw

    rows = pl.BlockSpec((CHUNK, D_MODEL), lambda b, c: (b * nc + c, 0))
    in_specs, args = [rows, pl.BlockSpec((1, D_MODEL), lambda b, c: (0, 0)), rows, rows], [h, w, dn, dres]
    if after is not None:
        body = _skip_ref(body, len(args))
        args.append(_deps(after))
        in_specs.append(_dep_spec(args[-1]))
    return pl.pallas_call(
        body, grid=(nseq, nc), in_specs=in_specs,
        out_specs=[pl.BlockSpec((None, CHUNK, D_MODEL), lambda b, c: (b, jnp.maximum(c - 1, 0), 0)),
                   pl.BlockSpec((None, CHUNK, D_MODEL), lambda b, c: (b, 0, 0)),
                   pl.BlockSpec((8, D_MODEL), lambda b, c: (0, 0))],
        out_shape=[jax.ShapeDtypeStruct((nseq, Tp - CHUNK, D_MODEL), F32), jax.ShapeDtypeStruct((nseq, CHUNK, D_MODEL), F32),
                   jax.ShapeDtypeStruct((8, D_MODEL), F32)],
        name=name, compiler_params=_cparams(("arbitrary", "arbitrary")),
    )(*args)


def _gu_swiglu(n, w_gu, name):
    R = n.shape[0]
    G, _, ng = w_gu.shape

    def body(n_ref, w_ref, gu_ref, a_ref):
        x = n_ref[...]
        for r in range(G):
            gu_ref[:, ng * r:ng * (r + 1)] = _dg(x, w_ref[r], 1, 0).astype(gu_ref.dtype)
        a_ref[...] = _swiglu_fn(gu_ref[...]).astype(a_ref.dtype)

    return _rows_call(body, rows=R, tr=_pick(R, (256, 128)), ins=[n, ("full", w_gu)],
                      outs=[jax.ShapeDtypeStruct((R, 2 * D_FF), BF16), jax.ShapeDtypeStruct((R, D_FF), BF16)], name=name)


def _d_swiglu(dout, w_down, gu, alpha, name):
    R = gu.shape[0]

    def body(do_ref, w_ref, gu_ref, o_ref):
        da = _dg(do_ref[...] * alpha, w_ref[...], 1, 1)
        g = gu_ref[:, :D_FF].astype(F32)
        u = gu_ref[:, D_FF:].astype(F32)
        s = jax.nn.sigmoid(g)
        t = g * s
        o_ref[:, :D_FF] = (da * u * (s + t - t * s)).astype(o_ref.dtype)
        o_ref[:, D_FF:] = (da * t).astype(o_ref.dtype)

    return _rows_call(body, rows=R, tr=_pick(R, (256, 128)), ins=[dout, ("full", w_down), gu],
                      outs=[jax.ShapeDtypeStruct(gu.shape, BF16)], name=name)[0]


def _residual_matmul(a, w, res, alpha, name, norm_w=None):
    R, K = a.shape

    def body(a_ref, w_ref, r_ref, *rest):
        out = r_ref[...] + alpha * _dg(a_ref[...], w_ref[...], 1, 0)
        if norm_w is None:
            rest[0][...] = out
        else:
            rest[1][...] = out
            rest[2][...] = _rms_fn(out, rest[0][...]).astype(rest[2].dtype)

    f32 = jax.ShapeDtypeStruct((R, D_MODEL), F32)
    ins = [a, ("full", w), res] + ([] if norm_w is None else [("full", norm_w)])
    outs = [f32] + ([] if norm_w is None else [jax.ShapeDtypeStruct((R, D_MODEL), BF16)])
    got = _rows_call(body, rows=R, tr=_pick(R, (544, 256, 128)), ins=ins, outs=outs, name=name)
    return got[0] if norm_w is None else (got[0], got[1])


def _branch_merge(ya, yb, wa, wb, gates, name):
    def body(ya_ref, yb_ref, wa_ref, wb_ref, g_ref, pa_ref, pb_ref, o_ref):
        pa = _dg(ya_ref[...], wa_ref[...], 1, 0)
        pb = _dg(yb_ref[...], wb_ref[...], 1, 0)
        pa_ref[...] = pa
        pb_ref[...] = pb
        o_ref[...] = _merge_fn(pa, pb, g_ref[...].astype(F32)).astype(o_ref.dtype)

    R = ya.shape[0]
    f32 = jax.ShapeDtypeStruct((R, D_MODEL), F32)
    return _rows_call(body, rows=R, tr=_pick(R, (544, 256, 128)), ins=[ya, yb, ("full", wa), ("full", wb), gates],
                      outs=[f32, f32, jax.ShapeDtypeStruct((R, D_MODEL), BF16)], name=name)


def _branch_merge_bwd(pa, pb, gates, dm, wa, wb, name):
    def body(pa_ref, pb_ref, g_ref, dm_ref, wa_ref, wb_ref, dpa_ref, dpb_ref, dg_ref, dya_ref, dyb_ref):
        _, vjp = jax.vjp(_merge_fn, pa_ref[...], pb_ref[...], g_ref[...].astype(F32))
        dpa, dpb, dg = vjp(dm_ref[...].astype(F32))
        dpa_ref[...] = dpa.astype(dpa_ref.dtype)
        dpb_ref[...] = dpb.astype(dpb_ref.dtype)
        dg_ref[...] = dg.astype(dg_ref.dtype)
        dya_ref[...] = _dg(dpa, wa_ref[...], 1, 1).astype(dya_ref.dtype)
        dyb_ref[...] = _dg(dpb, wb_ref[...], 1, 1).astype(dyb_ref.dtype)

    R = pa.shape[0]
    b16 = jax.ShapeDtypeStruct(pa.shape, BF16)
    return _rows_call(body, rows=R, tr=_pick(R, (544, 256, 128)), ins=[pa, pb, gates, dm, ("full", wa), ("full", wb)],
                      outs=[b16, b16, jax.ShapeDtypeStruct(gates.shape, BF16), b16, b16], name=name)


def _loss_head(h3, w, target, nseq, name):
    Tp = h3.shape[0] // nseq
    nc = Tp // CHUNK

    def fn(h, w_, t, valid):
        y = _rms_fn(h, w_)
        e = (y - t) * valid
        return 0.5 * jnp.sum(jnp.mean(e * e, axis=-1, keepdims=True))

    def body(h_ref, w_ref, t_ref, loss_ref, dh_ref, dw_ref):
        b, c = pl.program_id(0), pl.program_id(1)
        valid = (c >= 1).astype(F32)
        t = t_ref[...]
        loss, vjp = jax.vjp(lambda h, w_: fn(h, w_, t, valid), h_ref[...], w_ref[...])
        dh, dw = vjp(jnp.ones((), F32))
        dh_ref[...] = dh

        @pl.when((b == 0) & (c == 0))
        def _():
            loss_ref[...] = jnp.zeros_like(loss_ref)
            dw_ref[...] = jnp.zeros_like(dw_ref)

        loss_ref[...] += jnp.full(loss_ref.shape, loss, F32)
        dw_ref[0:1, :] += dw

    return pl.pallas_call(
        body, grid=(nseq, nc),
        in_specs=[pl.BlockSpec((CHUNK, D_MODEL), lambda b, c: (b * nc + c, 0)),
                  pl.BlockSpec((1, D_MODEL), lambda b, c: (0, 0)),
                  pl.BlockSpec((None, CHUNK, D_MODEL), lambda b, c: (b, jnp.maximum(c - 1, 0), 0))],
        out_specs=[pl.BlockSpec((8, 128), lambda b, c: (0, 0)),
                   pl.BlockSpec((CHUNK, D_MODEL), lambda b, c: (b * nc + c, 0)),
                   pl.BlockSpec((8, D_MODEL), lambda b, c: (0, 0))],
        out_shape=[jax.ShapeDtypeStruct((8, 128), F32), jax.ShapeDtypeStruct(h3.shape, F32),
                   jax.ShapeDtypeStruct((8, D_MODEL), F32)],
        name=name, compiler_params=_cparams(("arbitrary", "arbitrary")),
    )(h3, w, target)


CONV_TILE = 512
CONV_HALO = 8


def _conv_fwd(xbc, w, b, pad, name):
    B, Tp, C = xbc.shape
    nch = Tp // CHUNK

    def body(x_ref, w_ref, b_ref, o_ref, xp):
        xp[0:CONV_HALO, :] = jnp.zeros((CONV_HALO, CONV_TILE), F32)
        xp[CONV_HALO:, :] = x_ref[...]
        for c in range(nch):
            acc = jnp.zeros((CHUNK, CONV_TILE), F32) + b_ref[...]
            for k in range(SSD_CONV):
                acc = acc + w_ref[k:k + 1, :] * xp[pl.ds(CONV_HALO + CHUNK * c - (SSD_CONV - 1) + k, CHUNK), :]
            out = _silu(acc)
            if CHUNK * c < pad:
                row = CHUNK * c + lax.broadcasted_iota(jnp.int32, (CHUNK, 1), 0)
                out = jnp.where(row >= pad, out, 0.0)
            o_ref[pl.ds(CHUNK * c, CHUNK), :] = out

    return pl.pallas_call(
        body, grid=(B, C // CONV_TILE),
        in_specs=[pl.BlockSpec((None, Tp, CONV_TILE), lambda i, j: (i, 0, j)),
                  pl.BlockSpec((SSD_CONV, CONV_TILE), lambda i, j: (0, j)),
                  pl.BlockSpec((1, CONV_TILE), lambda i, j: (0, j))],
        out_specs=pl.BlockSpec((None, Tp, CONV_TILE), lambda i, j: (i, 0, j)),
        out_shape=jax.ShapeDtypeStruct(xbc.shape, F32),
        scratch_shapes=[pltpu.VMEM((Tp + CONV_HALO, CONV_TILE), F32)],
        name=name, compiler_params=_cparams(("arbitrary", "arbitrary")),
    )(xbc, w, b)


def _conv_bwd(xbc, w, b, dact, pad, name):
    B, Tp, C = xbc.shape
    nch = Tp // CHUNK

    def body(x_ref, w_ref, b_ref, da_ref, dx_ref, dw_ref, db_ref, xp, dp):
        bi = pl.program_id(1)
        xp[0:CONV_HALO, :] = jnp.zeros((CONV_HALO, CONV_TILE), F32)
        xp[CONV_HALO:, :] = x_ref[...]
        dp[pl.ds(Tp, CONV_HALO), :] = jnp.zeros((CONV_HALO, CONV_TILE), F32)
        dws = [jnp.zeros((1, CONV_TILE), F32) for _ in range(SSD_CONV)]
        dbs = jnp.zeros((1, CONV_TILE), F32)
        for c in range(nch):
            xs = [xp[pl.ds(CONV_HALO + CHUNK * c - (SSD_CONV - 1) + k, CHUNK), :] for k in range(SSD_CONV)]
            acc = jnp.zeros((CHUNK, CONV_TILE), F32) + b_ref[...]
            for k in range(SSD_CONV):
                acc = acc + w_ref[k:k + 1, :] * xs[k]
            sg = jax.nn.sigmoid(acc)
            t = acc * sg
            dpre = da_ref[pl.ds(CHUNK * c, CHUNK), :] * (sg + t - t * sg)
            if CHUNK * c < pad:
                row = CHUNK * c + lax.broadcasted_iota(jnp.int32, (CHUNK, 1), 0)
                dpre = jnp.where(row >= pad, dpre, 0.0)
            dp[pl.ds(CHUNK * c, CHUNK), :] = dpre
            dbs = dbs + jnp.sum(dpre, axis=0, keepdims=True)
            for k in range(SSD_CONV):
                dws[k] = dws[k] + jnp.sum(dpre * xs[k], axis=0, keepdims=True)
        for c in range(nch):
            acc = jnp.zeros((CHUNK, CONV_TILE), F32)
            for k in range(SSD_CONV):
                acc = acc + w_ref[k:k + 1, :] * dp[pl.ds(CHUNK * c + (SSD_CONV - 1) - k, CHUNK), :]
            dx_ref[pl.ds(CHUNK * c, CHUNK), :] = acc.astype(dx_ref.dtype)

        @pl.when(bi == 0)
        def _():
            dw_ref[...] = jnp.zeros_like(dw_ref)
            db_ref[...] = jnp.zeros_like(db_ref)

        for k in range(SSD_CONV):
            dw_ref[k:k + 1, :] += dws[k]
        db_ref[0:1, :] += dbs

    return pl.pallas_call(
        body, grid=(C // CONV_TILE, B),
        in_specs=[pl.BlockSpec((None, Tp, CONV_TILE), lambda j, i: (i, 0, j)),
                  pl.BlockSpec((SSD_CONV, CONV_TILE), lambda j, i: (0, j)),
                  pl.BlockSpec((1, CONV_TILE), lambda j, i: (0, j)),
                  pl.BlockSpec((None, Tp, CONV_TILE), lambda j, i: (i, 0, j))],
        out_specs=[pl.BlockSpec((None, Tp, CONV_TILE), lambda j, i: (i, 0, j)),
                   pl.BlockSpec((8, CONV_TILE), lambda j, i: (0, j)),
                   pl.BlockSpec((8, CONV_TILE), lambda j, i: (0, j))],
        out_shape=[jax.ShapeDtypeStruct(xbc.shape, BF16), jax.ShapeDtypeStruct((8, C), F32),
                   jax.ShapeDtypeStruct((8, C), F32)],
        scratch_shapes=[pltpu.VMEM((Tp + CONV_HALO, CONV_TILE), F32), pltpu.VMEM((Tp + CONV_HALO, CONV_TILE), F32)],
        name=name, compiler_params=_cparams(("arbitrary", "arbitrary")),
    )(xbc, w, b, dact)


def _ssd_chunk(xs, bm, cm, dtr, z, state, dt_bias, a_log, dskip, norm_w, valid, kept=None, keep=False):
    Q = xs.shape[0]
    known = (lambda x, v: x) if kept is None else _known
    lane = lax.broadcasted_iota(jnp.int32, (1, 128), 1)
    dt = jnp.where(lane < SSD_HEADS, _softplus(dtr + dt_bias), 0.0) * valid
    a = dt * (-jnp.exp(a_log))
    tril = _tril(Q)
    cs = known(_cumsum_rows(a), None if kept is None else kept[0])
    cs_t = cs.T
    cs_end = _row_of(cs, Q - 1)
    low = lane < SSD_HEAD_DIM
    low_rows = lax.broadcasted_iota(jnp.int32, (128, 1), 0) < SSD_HEAD_DIM
    ys, new_state, cbs = [], [], []
    for g in range(SSD_GROUPS):
        bg = bm[:, 128 * g:128 * (g + 1)]
        cg = cm[:, 128 * g:128 * (g + 1)]
        cb = known(_mm_nt(cg, bg), None if kept is None else kept[1][Q * g:Q * (g + 1)])
        cbs.append(cb)
        for pr in range(2):
            p = 2 * g + pr
            h0, h1 = 2 * p, 2 * p + 1
            xp = xs[:, 128 * p:128 * (p + 1)]
            c0, c1 = _col_of(cs, h0), _col_of(cs, h1)
            e0, e1 = _col_of(cs_end, h0), _col_of(cs_end, h1)
            xd = xp * jnp.where(low, _col_of(dt, h0), _col_of(dt, h1))
            l0 = jnp.exp(jnp.where(tril, c0 - _row_of(cs_t, h0), -1e30))
            l1 = jnp.exp(jnp.where(tril, c1 - _row_of(cs_t, h1), -1e30))
            y_diag = jnp.where(low, _mm(cb * l0, xd), _mm(cb * l1, xd))
            to_end = jnp.where(low, jnp.exp(e0 - c0), jnp.exp(e1 - c1))
            sp = state[128 * p:128 * (p + 1), :]
            y_off = _mm_nt(cg, sp) * jnp.where(low, jnp.exp(c0), jnp.exp(c1))
            new_state.append(sp * jnp.where(low_rows, jnp.exp(e0), jnp.exp(e1)) + _mm_tn(xd * to_end, bg))
            ys.append(y_diag + y_off + xp * jnp.where(low, _col_of(dskip, h0), _col_of(dskip, h1)))
    y_raw = known(jnp.concatenate(ys, axis=1), None if kept is None else kept[2])
    y = y_raw * _silu(z)
    gw = SSD_INNER // SSD_GROUPS
    outs = []
    for g in range(SSD_GROUPS):
        blk = y[:, gw * g:gw * (g + 1)]
        outs.append(blk * lax.rsqrt(jnp.mean(blk * blk, axis=-1, keepdims=True) + EPS))
    out, state_out = jnp.concatenate(outs, axis=1) * norm_w, jnp.concatenate(new_state, axis=0)
    if kept is not None:
        state_out = _known(state_out, state)
    return (out, state_out, (cs, jnp.concatenate(cbs, axis=0), y_raw)) if keep else (out, state_out)


def _valid_rows(c, pad):
    row = c * CHUNK + lax.broadcasted_iota(jnp.int32, (CHUNK, 1), 0)
    return (row >= pad).astype(F32)


def _ssd_fwd(xact, dtr, z, dt_bias, a_log, dskip, norm_w, pad, name):
    B, Tp, _ = xact.shape
    nc = Tp // CHUNK

    def body(xs_ref, bm_ref, cm_ref, dt_ref, z_ref, db_ref, al_ref, ds_ref, nw_ref, y_ref, save_ref, cs_ref, cb_ref, yr_ref, st):
        c = pl.program_id(1)

        @pl.when(c == 0)
        def _():
            st[...] = jnp.zeros_like(st)

        s0 = st[...]
        save_ref[...] = s0
        y, s1, (cs, cb, y_raw) = _ssd_chunk(xs_ref[...], bm_ref[...], cm_ref[...], dt_ref[...], z_ref[...].astype(F32), s0,
                                            db_ref[...], al_ref[...], ds_ref[...], nw_ref[...], _valid_rows(c, pad), keep=True)
        y_ref[...] = y.astype(y_ref.dtype)
        cs_ref[...] = cs
        cb_ref[...] = cb
        yr_ref[...] = y_raw
        st[...] = s1

    row = lambda w, off=0: pl.BlockSpec((None, CHUNK, w), lambda b, c: (b, c, off))
    par = lambda w: pl.BlockSpec((1, w), lambda b, c: (0, 0))
    per_chunk = lambda r: pl.BlockSpec((None, None, r, 128), lambda b, c: (b, c, 0, 0))
    return pl.pallas_call(
        body, grid=(B, nc),
        in_specs=[row(1024, 0), row(512, 2), row(512, 3), row(128), row(1024), par(128), par(128), par(128), par(1024)],
        out_specs=[row(1024), per_chunk(1024), row(128), per_chunk(SSD_GROUPS * CHUNK), row(1024)],
        out_shape=[jax.ShapeDtypeStruct((B, Tp, SSD_INNER), BF16), jax.ShapeDtypeStruct((B, nc, 1024, 128), F32),
                   jax.ShapeDtypeStruct((B, Tp, 128), F32), jax.ShapeDtypeStruct((B, nc, SSD_GROUPS * CHUNK, 128), F32),
                   jax.ShapeDtypeStruct((B, Tp, SSD_INNER), F32)],
        scratch_shapes=[pltpu.VMEM((1024, 128), F32)],
        name=name, compiler_params=_cparams(("arbitrary", "arbitrary")),
    )(xact, xact, xact, dtr, z, dt_bias, a_log, dskip, norm_w)


def _ssd_bwd(xact, dtr, z, dt_bias, a_log, dskip, norm_w, saved, kept, dy, pad, name, after=None):
    B, Tp, _ = xact.shape
    nc = Tp // CHUNK

    def body(xs_ref, bm_ref, cm_ref, dt_ref, z_ref, db_ref, al_ref, ds_ref, nw_ref, sv_ref, cs_ref, cb_ref, yr_ref, dy_ref,
             dx_ref, ddt_ref, dz_ref, dpar_ref, dnw_ref, dst):
        b, i = pl.program_id(0), pl.program_id(1)
        c = nc - 1 - i

        @pl.when(i == 0)
        def _():
            dst[...] = jnp.zeros_like(dst)

        valid = _valid_rows(c, pad)
        kept_c = (cs_ref[...], cb_ref[...], yr_ref[...])
        fn = lambda *a: _ssd_chunk(*a, valid, kept=kept_c)
        _, vjp = jax.vjp(fn, xs_ref[...], bm_ref[...], cm_ref[...], dt_ref[...], z_ref[...].astype(F32), sv_ref[...],
                         db_ref[...], al_ref[...], ds_ref[...], nw_ref[...])
        dxs, dbm, dcm, ddt, dz, dstate, ddb, dal, dds, dnw = vjp((dy_ref[...].astype(F32), dst[...]))
        dx_ref[:, 0:1024] = dxs
        dx_ref[:, 1024:1536] = dbm
        dx_ref[:, 1536:2048] = dcm
        ddt_ref[...] = ddt
        dz_ref[...] = dz.astype(dz_ref.dtype)
        dst[...] = dstate

        @pl.when((b == 0) & (i == 0))
        def _():
            dpar_ref[...] = jnp.zeros_like(dpar_ref)
            dnw_ref[...] = jnp.zeros_like(dnw_ref)

        dpar_ref[0:1, :] += ddb
        dpar_ref[1:2, :] += dal
        dpar_ref[2:3, :] += dds
        dnw_ref[0:1, :] += dnw

    row = lambda w, off=0: pl.BlockSpec((None, CHUNK, w), lambda b, i: (b, nc - 1 - i, off))
    par = lambda w: pl.BlockSpec((1, w), lambda b, i: (0, 0))
    acc = lambda w: pl.BlockSpec((8, w), lambda b, i: (0, 0))
    per_chunk = lambda r: pl.BlockSpec((None, None, r, 128), lambda b, i: (b, nc - 1 - i, 0, 0))
    in_specs = [row(1024, 0), row(512, 2), row(512, 3), row(128), row(1024), par(128), par(128), par(128), par(1024),
                per_chunk(1024), row(128), per_chunk(SSD_GROUPS * CHUNK), row(1024), row(1024)]
    args = [xact, xact, xact, dtr, z, dt_bias, a_log, dskip, norm_w, saved, kept[0], kept[1], kept[2], dy]
    if after is not None:
        body = _skip_ref(body, len(args))
        args.append(_deps(after))
        in_specs.append(_dep_spec(args[-1]))
    outs = pl.pallas_call(
        body, grid=(B, nc), in_specs=in_specs,
        out_specs=[row(2048), row(128), row(1024), acc(128), acc(1024)],
        out_shape=[jax.ShapeDtypeStruct((B, Tp, 2048), F32), jax.ShapeDtypeStruct((B, Tp, 128), F32),
                   jax.ShapeDtypeStruct((B, Tp, 1024), BF16), jax.ShapeDtypeStruct((8, 128), F32),
                   jax.ShapeDtypeStruct((8, 1024), F32)],
        scratch_shapes=[pltpu.VMEM((1024, 128), F32)],
        name=name, compiler_params=_cparams(("arbitrary", "arbitrary")),
    )(*args)
    return outs


@jax.custom_vjp
def _known(x, value):
    return value


_known.defvjp(lambda x, value: (value, None), lambda _, g: (g, jnp.zeros_like(g)))


def _hg_chunk(qr, fr, ir, gr, state_t, p0, p1, norm_w, valid, kept=None, keep=False):
    Q = qr.shape[0]
    known = (lambda x, i: x) if kept is None else (lambda x, i: _known(x, kept[i].astype(x.dtype)))
    lb = jax.nn.sigmoid(p0 - p1)
    f = lb + (1.0 - lb) * jax.nn.sigmoid(fr)
    k = 1.0 - f
    q = _silu(qr)
    v = ir * valid
    cum = known(_cumsum_rows(jnp.log(f)), 0)
    cum_end = _row_of(cum, Q - 1)
    o_inter = _mm_nt(q * jnp.exp(cum), state_t)
    nblk = Q // HG_SUB
    row = lax.broadcasted_iota(jnp.int32, (Q, 1), 0)
    ri = lax.broadcasted_iota(jnp.int32, (Q, Q), 0)
    ci = lax.broadcasted_iota(jnp.int32, (Q, Q), 1)
    mids = jnp.concatenate([jnp.broadcast_to(_row_of(cum, HG_SUB * i + HG_SUB // 2 - 1), (HG_SUB, cum.shape[1]))
                            for i in range(nblk)], axis=0)
    sh = HG_SUB.bit_length() - 1
    same = (jnp.right_shift(ri, sh) == jnp.right_shift(ci, sh)) & (ri >= ci)
    att = jnp.where(same, _mm_nt(q * jnp.exp(cum - mids), k * jnp.exp(mids - cum)), 0.0)
    qas, kas = [], []
    for i in range(1, nblk):
        lo = HG_SUB * i
        start = _row_of(cum, lo - 1)
        qas.append(q * jnp.exp(jnp.where((row >= lo) & (row < lo + HG_SUB), cum - start, -1e30)))
        kas.append(k * jnp.exp(jnp.where(row < lo, start - cum, -1e30)))
    att = att + _mm_nt(jnp.concatenate(qas, axis=1), jnp.concatenate(kas, axis=1))
    att = known(att, 1)
    o = known(o_inter + _mm(att, v), 2)
    new_state_t = state_t * jnp.exp(cum_end) + _mm_tn(v, k * jnp.exp(cum_end - cum))
    if kept is not None:
        new_state_t = _known(new_state_t, state_t)
    y = o * lax.rsqrt(jnp.mean(o * o, axis=-1, keepdims=True) + EPS) * norm_w * _silu(gr)
    return (y, new_state_t, (cum, att, o)) if keep else (y, new_state_t)


HG_PER_STEP = 8
HG_COLS = 4 * 128


def _hg_fwd(qfig, lbh, nwh, pad, name):
    B, Tp, _ = qfig.shape
    nc = Tp // CHUNK
    hp = HG_PER_STEP

    def body(x_ref, lb_ref, nw_ref, y_ref, save_ref, cum_ref, att_ref, o_ref, st):
        c = pl.program_id(1)

        @pl.when(c == 0)
        def _():
            st[...] = jnp.zeros_like(st)

        valid = _valid_rows(c, pad)
        for j in range(hp):
            for b in range(B):
                s0 = st[j, b]
                save_ref[j, b] = s0
                col = lambda k: x_ref[b, :, HG_COLS * j + 128 * k:HG_COLS * j + 128 * (k + 1)]
                y, s1, (cum, att, o) = _hg_chunk(col(0), col(1), col(2), col(3), s0, lb_ref[j, 0:1, :], lb_ref[j, 1:2, :],
                                                 nw_ref[j], valid, keep=True)
                y_ref[b, :, 128 * j:128 * (j + 1)] = y.astype(y_ref.dtype)
                cum_ref[b, :, 128 * j:128 * (j + 1)] = cum
                att_ref[j, b] = att.astype(att_ref.dtype)
                o_ref[b, :, 128 * j:128 * (j + 1)] = o
                st[j, b] = s1

    rows = pl.BlockSpec((B, CHUNK, 128 * hp), lambda h, c: (0, c, h))
    per_chunk = pl.BlockSpec((hp, B, None, 128, 128), lambda h, c: (h, 0, c, 0, 0))
    return pl.pallas_call(
        body, grid=(HG_HEADS // hp, nc),
        in_specs=[pl.BlockSpec((B, CHUNK, HG_COLS * hp), lambda h, c: (0, c, h)),
                  pl.BlockSpec((hp, 2, 128), lambda h, c: (h, 0, 0)),
                  pl.BlockSpec((hp, 1, 128), lambda h, c: (h, 0, 0))],
        out_specs=[rows, per_chunk, rows, per_chunk, rows],
        out_shape=[jax.ShapeDtypeStruct((B, Tp, 1024), BF16), jax.ShapeDtypeStruct((HG_HEADS, B, nc, 128, 128), F32),
                   jax.ShapeDtypeStruct((B, Tp, 1024), F32), jax.ShapeDtypeStruct((HG_HEADS, B, nc, 128, 128), BF16),
                   jax.ShapeDtypeStruct((B, Tp, 1024), F32)],
        scratch_shapes=[pltpu.VMEM((hp, B, 128, 128), F32)],
        name=name, compiler_params=_cparams(("arbitrary", "arbitrary")),
    )(qfig, lbh, nwh)


def _hg_bwd(qfig, lbh, nwh, saved, kept, dy, pad, name, after=None):
    B, Tp, _ = qfig.shape
    nc = Tp // CHUNK
    hp = HG_PER_STEP

    def body(x_ref, lb_ref, nw_ref, sv_ref, cum_ref, att_ref, o_ref, dy_ref, dx_ref, dlb_ref, dnw_ref, dst):
        i = pl.program_id(1)
        c = nc - 1 - i

        @pl.when(i == 0)
        def _():
            dst[...] = jnp.zeros_like(dst)
            dlb_ref[...] = jnp.zeros_like(dlb_ref)
            dnw_ref[...] = jnp.zeros_like(dnw_ref)

        valid = _valid_rows(c, pad)
        for j in range(hp):
            for b in range(B):
                col = lambda k: x_ref[b, :, HG_COLS * j + 128 * k:HG_COLS * j + 128 * (k + 1)]
                head = slice(128 * j, 128 * (j + 1))
                kept_jb = (cum_ref[b, :, head], att_ref[j, b], o_ref[b, :, head])
                fn = lambda *a: _hg_chunk(*a, valid, kept=kept_jb)
                _, vjp = jax.vjp(fn, col(0), col(1), col(2), col(3), sv_ref[j, b], lb_ref[j, 0:1, :], lb_ref[j, 1:2, :], nw_ref[j])
                d4 = vjp((dy_ref[b, :, 128 * j:128 * (j + 1)].astype(F32), dst[j, b]))
                for k in range(4):
                    dx_ref[b, :, HG_COLS * j + 128 * k:HG_COLS * j + 128 * (k + 1)] = d4[k].astype(dx_ref.dtype)
                dst[j, b] = d4[4]
                dlb_ref[j, 0:1, :] += d4[5]
                dlb_ref[j, 1:2, :] += d4[6]
                dnw_ref[j, 0:1, :] += d4[7]

    acc = pl.BlockSpec((hp, 8, 128), lambda h, i: (h, 0, 0))
    rows = pl.BlockSpec((B, CHUNK, 128 * hp), lambda h, i: (0, nc - 1 - i, h))
    per_chunk = pl.BlockSpec((hp, B, None, 128, 128), lambda h, i: (h, 0, nc - 1 - i, 0, 0))
    in_specs = [pl.BlockSpec((B, CHUNK, HG_COLS * hp), lambda h, i: (0, nc - 1 - i, h)),
                pl.BlockSpec((hp, 2, 128), lambda h, i: (h, 0, 0)),
                pl.BlockSpec((hp, 1, 128), lambda h, i: (h, 0, 0)),
                per_chunk, rows, per_chunk, rows, rows]
    args = [qfig, lbh, nwh, saved, kept[0], kept[1], kept[2], dy]
    if after is not None:
        body = _skip_ref(body, len(args))
        args.append(_deps(after))
        in_specs.append(_dep_spec(args[-1]))
    return pl.pallas_call(
        body, grid=(HG_HEADS // hp, nc), in_specs=in_specs,
        out_specs=[pl.BlockSpec((B, CHUNK, HG_COLS * hp), lambda h, i: (0, nc - 1 - i, h)), acc, acc],
        out_shape=[jax.ShapeDtypeStruct((B, Tp, 4096), BF16), jax.ShapeDtypeStruct((HG_HEADS, 8, 128), F32),
                   jax.ShapeDtypeStruct((HG_HEADS, 8, 128), F32)],
        scratch_shapes=[pltpu.VMEM((hp, B, 128, 128), F32)],
        name=name, compiler_params=_cparams(("arbitrary", "arbitrary")),
    )(*args)


def _adamw_math(w, g, m, v):
    m = ADAM_B1 * m + (1.0 - ADAM_B1) * g
    v = ADAM_B2 * v + (1.0 - ADAM_B2) * (g * g)
    m_hat = m / (1.0 - ADAM_B1 ** ADAM_STEP)
    v_hat = v / (1.0 - ADAM_B2 ** ADAM_STEP)
    return -ADAM_LR * (m_hat / (jnp.sqrt(v_hat) + ADAM_EPS) + ADAM_WD * w), m, v


def _adamw_many(ws, gs, ms, vs, name):
    n = len(ws)

    def body(*refs):
        for i in range(n):
            d, m, v = _adamw_math(refs[i][...], refs[n + i][...], refs[2 * n + i][...], refs[3 * n + i][...])
            refs[4 * n + i][...] = d
            refs[5 * n + i][...] = m
            refs[6 * n + i][...] = v

    vm = pl.BlockSpec(memory_space=pltpu.VMEM)
    outs = pl.pallas_call(body, in_specs=[vm] * (4 * n), out_specs=[vm] * (3 * n),
                          out_shape=[jax.ShapeDtypeStruct(w.shape, F32) for w in ws] * 3, name=name)(*ws, *gs, *ms, *vs)
    return outs[:n], outs[n:2 * n], outs[2 * n:]


def _adamw(w, g, m, v, name, after=None):
    R, C = w.shape
    tr = max(t for t in range(8, R + 1, 8) if R % t == 0 and (t * C * 4 <= ADAMW_BLOCK_BYTES or t == 8))

    def body(w_ref, g_ref, m_ref, v_ref, d_ref, mo_ref, vo_ref):
        d_ref[...], mo_ref[...], vo_ref[...] = _adamw_math(w_ref[...], g_ref[...], m_ref[...], v_ref[...])

    sp = pl.BlockSpec((tr, C), lambda i: (i, 0))
    sh = jax.ShapeDtypeStruct((R, C), F32)
    in_specs, args = [sp] * 4, [w, g, m, v]
    if after is not None:
        body = _skip_ref(body, len(args))
        args.append(_deps(after))
        in_specs.append(_dep_spec(args[-1]))
    return pl.pallas_call(body, grid=(R // tr,), in_specs=in_specs, out_specs=[sp] * 3, out_shape=[sh] * 3,
                          name=name, compiler_params=_cparams(("arbitrary",)))(*args)


def _ffn_fwd(h, norm_w, w_gu, w_down, tag, after_norm=None, n=None, next_norm_w=None):
    if n is None:
        n = _rms_fwd(h, norm_w, f"{tag}_norm")
    if after_norm is not None:
        after_norm(n)
    gu, a = _gu_swiglu(n, w_gu, f"{tag}_gu")
    out = _residual_matmul(a, w_down, h, 0.5, f"{tag}_down", next_norm_w)
    return out, (n, gu, a)


def _ffn_bwd(h, norm_w, w_gu, w_down, saved, dout, tag, after_dw_down=None, token_seqs=None, told=None):
    n, gu, a = saved
    dgu = _d_swiglu(dout, w_down, gu, 0.5, f"{tag}_d_gu")
    dw_down = _matmul(a, dout, mode="tn", out_dtype=F32, alpha=0.5, name=f"{tag}_dw_down")
    dw_gu = _matmul(n, dgu, mode="tn", out_dtype=F32, out_groups=N_CHIPS, name=f"{tag}_dw_gu",
                    after=after_dw_down(dw_down) if after_dw_down else None)
    if token_seqs is None:
        dh, dnw = _d_norm_in(dgu, w_gu, h, norm_w, dout, f"{tag}_d_in", after=dw_gu)
    else:
        if told is not None:
            told("dw", (dw_gu, dw_down))
        dn = _matmul(dgu, w_gu, mode="nt", out_dtype=F32, name=f"{tag}_d_norm", after=dw_gu)
        dx, dm, dnw = _rms_bwd_tokens(h, norm_w, dn, dout, token_seqs, f"{tag}_d_in",
                                      after=told("d_norm", dn) if told is not None else None)
        dh = (dx, dm)
    return dh, dnw, dw_gu, dw_down


def _split_w_in(w_in_full):
    pts = [0]
    for s in IN_SIZES:
        pts.append(pts[-1] + s)
    sl = lambda i, j: w_in_full[:, pts[i]:pts[j]]
    qfig = sl(3, 7).reshape(D_MODEL, 4, HG_HEADS, 128).transpose(0, 2, 1, 3).reshape(D_MODEL, 4 * D_MODEL)
    return {"z": sl(0, 1), "xbc": sl(1, 2), "dt": jnp.pad(sl(2, 3), ((0, 0), (0, 128 - SSD_HEADS))),
            "qfig": qfig, "gates": sl(7, 9)}


def _local_step(x, target, W):
    B, S, _ = x.shape
    T = N_META + S
    pad = (-T) % CHUNK
    Tp = T + pad
    assert pad + N_META == CHUNK
    R = B * Tp
    meta = jnp.broadcast_to(W["meta_tokens"][None], (B, N_META, D_MODEL))
    h0 = jnp.concatenate([jnp.zeros((B, pad, D_MODEL), F32), meta, x], axis=1).reshape(R, D_MODEL)

    stage = W.get("_stage", lambda name, x: {})
    W = dict(W)
    (h1, um), sv1 = _ffn_fwd(h0, W["ffn1_norm"], W["ffn1_w_gu"], W["ffn1_w_down"], "ffn1",
                             lambda n: W.update(stage("ffn1_norm", n)), next_norm_w=W["mix_norm"])
    W.update(stage("ffn1_out", h1))
    wi = W["w_in"]
    z = _matmul(um, wi["z"], mode="nn", out_dtype=BF16, name="in_z")
    xbc = _matmul(um, wi["xbc"], mode="nn", out_dtype=F32, name="in_xbc")
    dtr = _matmul(um, wi["dt"], mode="nn", out_dtype=F32, name="in_dt")
    qfig = _matmul(um, wi["qfig"], mode="nn", out_dtype=F32, name="in_qfig")
    gates = _matmul(um, wi["gates"], mode="nn", out_dtype=BF16, name="in_gates")

    r3 = lambda t: t.reshape(B, Tp, t.shape[-1])
    lane_pad = lambda t: jnp.pad(t, ((0, 0), (0, 128 - t.shape[1])))
    dt_bias, a_log, dskip = lane_pad(W["ssd_dt_bias"]), lane_pad(W["ssd_a_log"]), lane_pad(W["ssd_d"])
    xact = _conv_fwd(r3(xbc), W["ssd_conv_w"], W["ssd_conv_b"], pad, "conv_fwd")
    ya, ssd_saved, *ssd_kept = _ssd_fwd(xact, r3(dtr), r3(z), dt_bias, a_log, dskip, W["ssd_norm"], pad, "ssd_fwd")
    lbh = W["hg_lower_bound"].reshape(2, HG_HEADS, 128).transpose(1, 0, 2)
    nwh = W["hg_norm"].reshape(HG_HEADS, 1, 128)
    yb, hg_saved, *hg_kept = _hg_fwd(r3(qfig), lbh, nwh, pad, "hg_fwd")
    ya2, yb2 = ya.reshape(R, -1), yb.reshape(R, -1)
    W.update(stage("mixers_out", yb2))
    pa, pb, mg = _branch_merge(ya2, yb2, W["w_branch_a"], W["w_branch_b"], gates, "branch_merge")
    h2, n2 = _residual_matmul(mg, W["w_out"], h1, 1.0, "mix_out", W["ffn2_norm"])
    h3, sv2 = _ffn_fwd(h2, W["ffn2_norm"], W["ffn2_w_gu"], W["ffn2_w_down"], "ffn2", n=n2)

    loss, dh3, d_final = _loss_head(h3, W["final_norm"].reshape(1, D_MODEL), target, B, "loss_head")

    G = {"final_norm": d_final[0]}
    dh2, dnw, G["ffn2_w_gu"], G["ffn2_w_down"] = _ffn_bwd(h2, W["ffn2_norm"], W["ffn2_w_gu"], W["ffn2_w_down"], sv2, dh3, "ffn2")
    G["ffn2_norm"] = dnw[0:1]
    dmg = _matmul(dh2, W["w_out"], mode="nt", out_dtype=BF16, name="d_merge")
    G["w_out"] = _matmul(mg, dh2, mode="tn", out_dtype=F32, name="dw_out")
    dpa, dpb, dgates, dya, dyb = _branch_merge_bwd(pa, pb, gates, dmg, W["w_branch_a"], W["w_branch_b"], "branch_merge_bwd")
    G["w_branch_a"] = _matmul(ya2, dpa, mode="tn", out_dtype=F32, name="dw_branch_a")
    G["w_branch_b"] = _matmul(yb2, dpb, mode="tn", out_dtype=F32, name="dw_branch_b")

    dxact, ddtr, dz, dpar, dnw = _ssd_bwd(xact, r3(dtr), r3(z), dt_bias, a_log, dskip, W["ssd_norm"], ssd_saved, ssd_kept,
                                          r3(dya), pad, "ssd_bwd", after=stage("late_grads", G).get("_after"))
    G["ssd_dt_bias"], G["ssd_a_log"], G["ssd_d"] = dpar[0:1, :SSD_HEADS], dpar[1:2, :SSD_HEADS], dpar[2:3, :SSD_HEADS]
    G["ssd_norm"] = dnw[0:1]
    dxbc, dcw, dcb = _conv_bwd(r3(xbc), W["ssd_conv_w"], W["ssd_conv_b"], dxact, pad, "conv_bwd")
    G["ssd_conv_w"], G["ssd_conv_b"] = dcw[0:SSD_CONV], dcb[0:1]
    dqfig, dlb, dhn = _hg_bwd(r3(qfig), lbh, nwh, hg_saved, hg_kept, r3(dyb), pad, "hg_bwd",
                              after=stage("after_conv_bwd", dcb).get("_after"))
    G["hg_lower_bound"] = dlb[:, 0:2, :].transpose(1, 0, 2).reshape(2, D_MODEL)
    G["hg_norm"] = dhn[:, 0, :].reshape(1, D_MODEL)

    r2 = lambda t: t.reshape(R, t.shape[-1])
    pieces = [("z", r2(dz)), ("xbc", r2(dxbc)), ("dt", r2(ddtr)), ("qfig", r2(dqfig)), ("gates", dgates)]
    dum = _sum_nt([p for _, p in pieces], [wi[nm] for nm, _ in pieces], "d_mix")
    dwi = {nm: _matmul(um, dpiece, mode="tn", out_dtype=F32, name=f"dw_in_{nm}") for nm, dpiece in pieces}
    dw_qfig = dwi["qfig"].reshape(D_MODEL, HG_HEADS, 4, 128).transpose(0, 2, 1, 3).reshape(D_MODEL, 4 * D_MODEL)
    G["w_in"] = jnp.concatenate([dwi["z"], dwi["xbc"], dwi["dt"][:, :SSD_HEADS], dw_qfig, dwi["gates"]], axis=1)
    dh1, dnw = _rms_bwd(h1, W["mix_norm"], dum, dh2, "mix_norm_bwd", after=stage("w_in_grads", dwi).get("_after"))
    G["mix_norm"] = dnw[0:1]
    (dx, dfirst), dnw, G["ffn1_w_gu"], G["ffn1_w_down"] = _ffn_bwd(
        h0, W["ffn1_norm"], W["ffn1_w_gu"], W["ffn1_w_down"], sv1, dh1, "ffn1",
        lambda dw: stage("ffn1_dw_down", dw).get("_after"), token_seqs=B,
        told=lambda name, t: stage("ffn1_" + name, t).get("_after"))
    G["ffn1_norm"] = dnw[0:1]
    G["meta_tokens"] = jnp.sum(dfirst[:, pad:CHUNK], axis=0)
    return loss, dx, G


ANY = pl.BlockSpec(memory_space=pl.ANY)


def _place():
    return lax.axis_index("x"), lax.axis_index("y"), lax.axis_index("c")


def _other_chips(x, y):
    return [(1 - x, y), (x, 1 - y), (1 - x, 1 - y)]


def _remote(src, dst, ssem, rsem, dev):
    return pltpu.make_async_remote_copy(src_ref=src, dst_ref=dst, send_sem=ssem, recv_sem=rsem,
                                        device_id=dev, device_id_type=MESH)


def _exchange8(buf, name):
    n, w = buf.shape

    def body(x_ref, out_ref, ssem, rsem):
        x, y, c = _place()
        me = 4 * x + 2 * y + c
        out_ref[me] = x_ref[...]
        copies = []
        for k in range(1, 8):
            px = 1 - x if (k >> 2) & 1 else x
            py = 1 - y if (k >> 1) & 1 else y
            pc = 1 - c if k & 1 else c
            cp = _remote(x_ref, out_ref.at[me], ssem.at[k - 1], rsem.at[k - 1], (px, py, pc))
            cp.start()
            copies.append((cp, 4 * px + 2 * py + pc))
        for k, (cp, peer) in enumerate(copies):
            _remote(x_ref, out_ref.at[peer], ssem.at[k], rsem.at[k], (x, y, c)).wait_recv()
        for cp, _ in copies:
            cp.wait_send()

    vm = pl.BlockSpec(memory_space=pltpu.VMEM)
    return pl.pallas_call(
        body, in_specs=[vm], out_specs=vm, out_shape=jax.ShapeDtypeStruct((8, n, w), F32),
        scratch_shapes=[pltpu.SemaphoreType.DMA((7,)), pltpu.SemaphoreType.DMA((7,))], name=name,
    )(buf)


HBM = pltpu.MemorySpace.HBM


def _sequencer(name, collective_id, sems, sent):
    return functools.partial(pl.kernel, mesh=plsc.ScalarSubcoreMesh(axis_name="sequencer", num_cores=1), name=name,
                             scratch_types=sems, compiler_params=pltpu.CompilerParams(collective_id=collective_id),
                             cost_estimate=pl.CostEstimate(flops=0, transcendentals=0, bytes_accessed=2 * sent,
                                                           remote_bytes_transferred=sent))


def _nbytes(arrays):
    return sum(a.size * a.dtype.itemsize for a in arrays)


def _handshake(peers):
    barrier = pltpu.get_barrier_semaphore()
    for peer in peers:
        pl.semaphore_signal(barrier, inc=1, device_id=peer, device_id_type=MESH)
    pl.semaphore_wait(barrier, len(peers))


def _gather_seq(blocks, name, collective_id):
    n = len(blocks)
    half = [s.shape[1] // 2 for s in blocks]
    full = [jax.new_ref(b, memory_space=HBM) for b in blocks]

    @_sequencer(name, collective_id, [pltpu.SemaphoreType.DMA((n, 3))] * 4, _nbytes(blocks) * 3 // 4)
    def launch(ssem, rsem, fssem, frsem):
        x, y, c = _place()
        q = 2 * x + y
        chips = _other_chips(x, y)
        _handshake([(px, py, c) for px, py in chips] + [(x, y, 1 - c)])
        piece = lambda s, qq, cc: full[s].at[qq, pl.ds(cc * half[s], half[s])]
        sends = []
        for j, (px, py) in enumerate(chips):
            for s in range(n):
                cp = _remote(piece(s, q, c), piece(s, q, c), ssem.at[s, j], rsem.at[s, j], (px, py, c))
                cp.start()
                sends.append(cp)
        for j, (px, py) in enumerate(chips):
            for s in range(n):
                got = piece(s, 2 * px + py, c)
                _remote(got, got, ssem.at[s, j], rsem.at[s, j], (px, py, c)).wait_recv()
                cp = _remote(got, got, fssem.at[s, j], frsem.at[s, j], (x, y, 1 - c))
                cp.start()
                sends.append(cp)
        for j, (px, py) in enumerate(chips):
            for s in range(n):
                got = piece(s, 2 * px + py, 1 - c)
                _remote(got, got, fssem.at[s, j], frsem.at[s, j], (x, y, 1 - c)).wait_recv()
        for cp in sends:
            cp.wait_send()

    launch()
    return [r[...] for r in full]


def _share8(buf, name, collective_id):
    n, w = buf.shape
    src = jax.new_ref(buf, memory_space=HBM)
    out = jax.empty_ref(jax.ShapeDtypeStruct((8, n, w), F32), memory_space=HBM)

    @_sequencer(name, collective_id, [pltpu.SemaphoreType.DMA((7,)), pltpu.SemaphoreType.DMA((7,)), pltpu.SemaphoreType.DMA((1,))],
                7 * buf.size * 4)
    def launch(ssem, rsem, lsem):
        x, y, c = _place()
        me = 4 * x + 2 * y + c
        peers = [(1 - x if (k >> 2) & 1 else x, 1 - y if (k >> 1) & 1 else y, 1 - c if k & 1 else c) for k in range(1, 8)]
        _handshake(peers)
        mine = pltpu.make_async_copy(src, out.at[me], lsem.at[0])
        mine.start()
        sends = []
        for k, peer in enumerate(peers):
            cp = _remote(src, out.at[me], ssem.at[k], rsem.at[k], peer)
            cp.start()
            sends.append(cp)
        for k, (px, py, pc) in enumerate(peers):
            slot = out.at[4 * px + 2 * py + pc]
            _remote(slot, slot, ssem.at[k], rsem.at[k], (px, py, pc)).wait_recv()
        for cp in sends:
            cp.wait_send()
        mine.wait()

    launch()
    return out[...]


def _sum_slots(slots, name, after=None):
    _, n, w = slots.shape

    def body(s_ref, o_ref):
        acc = s_ref[0]
        for d in range(1, 8):
            acc = acc + s_ref[d]
        o_ref[...] = acc

    vm = pl.BlockSpec(memory_space=pltpu.VMEM)
    in_specs, args = [vm], [slots]
    if after is not None:
        body = _skip_ref(body, 1)
        args.append(_deps(after))
        in_specs.append(vm)
    return pl.pallas_call(body, in_specs=in_specs, out_specs=vm, out_shape=jax.ShapeDtypeStruct((n, w), F32), name=name)(*args)


def _pair_swap(parts, name, collective_id):
    n = len(parts)
    half = [p.shape[1] // 2 for p in parts]
    src = [jax.new_ref(p, memory_space=HBM) for p in parts]
    got = [jax.empty_ref(jax.ShapeDtypeStruct((p.shape[0], h, p.shape[2]), p.dtype), memory_space=HBM) for p, h in zip(parts, half)]

    @_sequencer(name, collective_id, [pltpu.SemaphoreType.DMA((n,))] * 2, _nbytes(parts) // 2)
    def launch(ssem, rsem):
        x, y, c = _place()
        _handshake([(x, y, 1 - c)])
        copies = []
        for s in range(n):
            cp = _remote(src[s].at[pl.ds(0, parts[s].shape[0]), pl.ds((1 - c) * half[s], half[s])], got[s], ssem.at[s], rsem.at[s], (x, y, 1 - c))
            cp.start()
            copies.append(cp)
        for cp in copies:
            cp.wait_recv()
        for cp in copies:
            cp.wait_send()

    launch()
    return [g[...] for g in got]


def _to_owners(sums, name, collective_id):
    n = len(sums)
    src = [jax.new_ref(s, memory_space=HBM) for s in sums]
    got = [jax.empty_ref(jax.ShapeDtypeStruct(s.shape, s.dtype), memory_space=HBM) for s in sums]

    @_sequencer(name, collective_id, [pltpu.SemaphoreType.DMA((n, 3))] * 2, _nbytes(sums) * 3 // 4)
    def launch(ssem, rsem):
        x, y, c = _place()
        q = 2 * x + y
        chips = _other_chips(x, y)
        _handshake([(px, py, c) for px, py in chips])
        sends = []
        for j, (px, py) in enumerate(chips):
            for s in range(n):
                cp = _remote(src[s].at[2 * px + py], got[s].at[q], ssem.at[s, j], rsem.at[s, j], (px, py, c))
                cp.start()
                sends.append(cp)
        for j, (px, py) in enumerate(chips):
            for s in range(n):
                slot = got[s].at[2 * px + py]
                _remote(slot, slot, ssem.at[s, j], rsem.at[s, j], (px, py, c)).wait_recv()
        for cp in sends:
            cp.wait_send()

    launch()
    return [g[...] for g in got]


def _pair_join(blocks, name, collective_id):
    n = len(blocks)
    out = [jax.new_ref(b, memory_space=HBM) for b in blocks]

    @_sequencer(name, collective_id, [pltpu.SemaphoreType.DMA((n,))] * 2, _nbytes(blocks) // 2)
    def launch(ssem, rsem):
        x, y, c = _place()
        _handshake([(x, y, 1 - c)])
        sends = []
        for s in range(n):
            h = blocks[s].shape[0] // 2
            mine = out[s].at[pl.ds(c * h, h)]
            cp = _remote(mine, mine, ssem.at[s], rsem.at[s], (x, y, 1 - c))
            cp.start()
            sends.append(cp)
        for s in range(n):
            h = blocks[s].shape[0] // 2
            theirs = out[s].at[pl.ds((1 - c) * h, h)]
            _remote(theirs, theirs, ssem.at[s], rsem.at[s], (x, y, 1 - c)).wait_recv()
        for cp in sends:
            cp.wait_send()

    launch()
    return [o[...] for o in out]


WIRE = BF16


def _row_tile(h):
    return _pick(h, (256, 368, 352, 128, 16))


def _add_pair(part, got, c, name, after=None):
    _, h, w = got.shape
    tr = _row_tile(h)
    nt = h // tr

    def body(c_ref, p_ref, g_ref, o_ref):
        o_ref[...] = (p_ref[...] + g_ref[...].astype(F32)).astype(o_ref.dtype)

    in_specs = [pl.BlockSpec((None, tr, w), lambda q, i, c_ref: (q, c_ref[0] * nt + i, 0)),
                pl.BlockSpec((None, tr, w), lambda q, i, c_ref: (q, i, 0))]
    args = [c.reshape(1).astype(jnp.int32), part, got]
    if after is not None:
        body = _skip_ref(body, len(args))
        args.append(_deps(after))
        in_specs.append(_dep_spec(args[-1]))
    return pl.pallas_call(
        body,
        grid_spec=pltpu.PrefetchScalarGridSpec(
            num_scalar_prefetch=1, grid=(got.shape[0], nt), in_specs=in_specs,
            out_specs=pl.BlockSpec((None, tr, w), lambda q, i, c_ref: (q, i, 0))),
        out_shape=jax.ShapeDtypeStruct(got.shape, WIRE), name=name,
        compiler_params=_cparams(("arbitrary", "arbitrary")),
    )(*args)


def _sum_chips(slots, sums, q, c, name, after=None):
    _, h, w = slots.shape
    tr = _row_tile(h)
    nt = h // tr

    def body(s_ref, mine_ref, a_ref, b_ref, d_ref, o_ref):
        o_ref[...] = ((mine_ref[...].astype(F32) + a_ref[...].astype(F32)) + b_ref[...].astype(F32)) + d_ref[...].astype(F32)

    slot = lambda k: pl.BlockSpec((None, tr, w), lambda i, s_ref: (s_ref[1 + k], i, 0))
    scalars = jnp.stack([c, q, (q + 1) % N_CHIPS, (q + 2) % N_CHIPS, (q + 3) % N_CHIPS]).astype(jnp.int32)
    in_specs, args = [slot(0), slot(1), slot(2), slot(3)], [scalars, sums, slots, slots, slots]
    if after is not None:
        body = _skip_ref(body, len(args))
        args.append(_deps(after))
        in_specs.append(_dep_spec(args[-1]))
    return pl.pallas_call(
        body,
        grid_spec=pltpu.PrefetchScalarGridSpec(
            num_scalar_prefetch=1, grid=(nt,), in_specs=in_specs,
            out_specs=pl.BlockSpec((tr, w), lambda i, s_ref: (s_ref[0] * nt + i, 0))),
        out_shape=jax.ShapeDtypeStruct((2 * h, w), F32), name=name,
        compiler_params=_cparams(("arbitrary",)),
    )(*args)


class _Reduce:
    def __init__(self, parts, q, c, tag, first_id, regions=None):
        self.parts, self.q, self.c, self.tag, self.first_id, self.regions = parts, q, c, tag, first_id, regions
        self.got = _pair_swap(parts, f"{tag}_pair_swap", first_id)

    def to_owners(self, after=None):
        self.sums = [_add_pair(p, g, self.c, f"{self.tag}_pair_add{i}", after)
                     for i, (p, g) in enumerate(zip(self.parts, self.got))]
        if self.regions is not None:
            self.sums = self.regions(self.sums)
        self.slots = _to_owners(self.sums, f"{self.tag}_to_owners", self.first_id + 1)
        return self.sums

    def join(self, after=None):
        blocks = [_sum_chips(sl, sm, self.q, self.c, f"{self.tag}_sum_chips{i}", after)
                  for i, (sl, sm) in enumerate(zip(self.slots, self.sums))]
        self.out = _pair_join(blocks, f"{self.tag}_pair_join", self.first_id + 2)
        return blocks


WEIGHTS = ("meta_tokens", "ffn1_norm", "ffn1_w_gu", "ffn1_w_down", "mix_norm", "w_in", "ssd_conv_w", "ssd_conv_b",
           "ssd_dt_bias", "ssd_a_log", "ssd_d", "ssd_norm", "hg_lower_bound", "hg_norm", "w_branch_a", "w_branch_b",
           "w_out", "ffn2_norm", "ffn2_w_gu", "ffn2_w_down", "final_norm")
BIG = ("ffn1_w_gu", "ffn1_w_down", "w_in", "w_branch_a", "w_branch_b", "w_out", "ffn2_w_gu", "ffn2_w_down")
SMALL = tuple(n for n in WEIGHTS if n not in BIG)


def _rows1024(a):
    flat = a.reshape(-1)
    n = -(-flat.shape[0] // 1024) * 1024
    return jnp.pad(flat, (0, n - flat.shape[0])).reshape(-1, 1024)


def kernel(x, meta_tokens, ffn1_norm, ffn1_w_gu, ffn1_w_down, mix_norm, w_in, ssd_conv_w, ssd_conv_b, ssd_dt_bias, ssd_a_log, ssd_d, ssd_norm, hg_lower_bound, hg_norm, w_branch_a, w_branch_b, w_out, ffn2_norm, ffn2_w_gu, ffn2_w_down, final_norm, loss_target, m_meta_tokens, m_ffn1_norm, m_ffn1_w_gu, m_ffn1_w_down, m_mix_norm, m_w_in, m_ssd_conv_w, m_ssd_conv_b, m_ssd_dt_bias, m_ssd_a_log, m_ssd_d, m_ssd_norm, m_hg_lower_bound, m_hg_norm, m_w_branch_a, m_w_branch_b, m_w_out, m_ffn2_norm, m_ffn2_w_gu, m_ffn2_w_down, m_final_norm, v_meta_tokens, v_ffn1_norm, v_ffn1_w_gu, v_ffn1_w_down, v_mix_norm, v_w_in, v_ssd_conv_w, v_ssd_conv_b, v_ssd_dt_bias, v_ssd_a_log, v_ssd_d, v_ssd_norm, v_hg_lower_bound, v_hg_norm, v_w_branch_a, v_w_branch_b, v_w_out, v_ffn2_norm, v_ffn2_w_gu, v_ffn2_w_down, v_final_norm):
    P = dict(zip(WEIGHTS, (meta_tokens, ffn1_norm, ffn1_w_gu, ffn1_w_down, mix_norm, w_in, ssd_conv_w, ssd_conv_b, ssd_dt_bias, ssd_a_log, ssd_d, ssd_norm, hg_lower_bound, hg_norm, w_branch_a, w_branch_b, w_out, ffn2_norm, ffn2_w_gu, ffn2_w_down, final_norm)))
    M = dict(zip(WEIGHTS, (m_meta_tokens, m_ffn1_norm, m_ffn1_w_gu, m_ffn1_w_down, m_mix_norm, m_w_in, m_ssd_conv_w, m_ssd_conv_b, m_ssd_dt_bias, m_ssd_a_log, m_ssd_d, m_ssd_norm, m_hg_lower_bound, m_hg_norm, m_w_branch_a, m_w_branch_b, m_w_out, m_ffn2_norm, m_ffn2_w_gu, m_ffn2_w_down, m_final_norm)))
    V = dict(zip(WEIGHTS, (v_meta_tokens, v_ffn1_norm, v_ffn1_w_gu, v_ffn1_w_down, v_mix_norm, v_w_in, v_ssd_conv_w, v_ssd_conv_b, v_ssd_dt_bias, v_ssd_a_log, v_ssd_d, v_ssd_norm, v_hg_lower_bound, v_hg_norm, v_w_branch_a, v_w_branch_b, v_w_out, v_ffn2_norm, v_ffn2_w_gu, v_ffn2_w_down, v_final_norm)))
    cx, cy, cc = _place()
    q = 2 * cx + cy

    mine = jnp.concatenate([meta_tokens.reshape(4, 1024), ssd_conv_w.reshape(2, 1024), jnp.zeros((2, 1024), F32)], axis=0)
    every = _exchange8(mine, "gather_small")
    meta_full = jnp.concatenate([every[2 * k, 0:4].reshape(N_META, 256) for k in range(N_CHIPS)], axis=1)
    conv_w_full = jnp.concatenate([every[2 * k, 4:6].reshape(SSD_CONV, 512) for k in range(N_CHIPS)], axis=1)

    late = ("ffn2_w_down", "w_branch_a", "w_branch_b", "w_out")
    rows = jnp.concatenate([P[n][0] for n in late], axis=0)
    zero = lambda t, dtype=F32: (t[0:1, 0:1] * 0).astype(dtype)

    def in_slot(s, after=None):
        s = s if after is None else s + zero(after)
        return lax.dynamic_update_slice(lax.empty((N_CHIPS,) + s.shape, BF16), s.astype(BF16)[None], (q, 0, 0))

    gu1, down1 = _gather_seq([in_slot(ffn1_w_gu[0]), in_slot(ffn1_w_down[0])], "gather_ffn1", 1)
    W = {n: P[n] for n in SMALL}
    W["meta_tokens"], W["ssd_conv_w"] = meta_full, conv_w_full
    W["ffn1_w_gu"], W["ffn1_w_down"] = gu1, down1.reshape(-1, D_MODEL)
    flying = {}

    def stage(name, t):
        if name == "ffn1_norm":
            flying["w_in"] = _gather_seq([in_slot(w_in[0], t)], "gather_w_in", 2)
            return {}
        if name == "ffn1_out":
            flying["late"] = _gather_seq([in_slot(ffn2_w_gu[0], t), in_slot(rows, t)], "gather_late", 3)
            (w_in_all,) = flying["w_in"]
            w_in_all = w_in_all + zero(t, BF16)
            return {"w_in": _split_w_in(w_in_all.transpose(1, 0, 2).reshape(D_MODEL, -1))}
        if name == "mixers_out":
            gu2, rows_all = flying["late"]
            out, r = {"ffn2_w_gu": gu2}, 0
            for n in late:
                nr = P[n].shape[1]
                out[n] = (rows_all[:, r:r + nr] + zero(t, BF16)).reshape(N_CHIPS * nr, D_MODEL)
                r += nr
            return out
        if name == "late_grads":
            parts = [t["ffn2_w_gu"]] + [t[n].reshape(N_CHIPS, -1, D_MODEL) for n in late]
            flying["grad_late"] = _Reduce(parts, q, cc, "grad_late", 4)
            return {"_after": [t["ffn2_w_gu"]] + [t[n] for n in late]}
        if name == "after_conv_bwd":
            return {"_after": flying["grad_late"].to_owners(after=t)}
        if name == "w_in_grads":
            order = ("z", "xbc", "dt", "qfig", "gates")
            blocks = flying["grad_late"].join(after=[t[k] for k in order])

            def regions(sums):
                z, xbc, dt, qfig, gates = [s[0] for s in sums]
                h = z.shape[0]
                qfig = qfig.reshape(h, HG_HEADS, 4, 128).transpose(0, 2, 1, 3).reshape(h, 4 * D_MODEL)
                cols = jnp.concatenate([z, xbc, dt[:, :SSD_HEADS], qfig, gates], axis=1)
                return [cols.reshape(h, N_CHIPS, -1).transpose(1, 0, 2)]

            flying["grad_w_in"] = _Reduce([t[k][None] for k in order], q, cc, "grad_w_in", 7, regions)
            return {"_after": blocks}
        if name == "ffn1_dw_down":
            return {"_after": flying["grad_w_in"].to_owners(after=t)}
        if name == "ffn1_dw":
            dw_gu, dw_down = t
            flying["grad_ffn1"] = _Reduce([dw_gu, dw_down.reshape(N_CHIPS, -1, D_MODEL)], q, cc, "grad_ffn1", 10)
            return {}
        if name == "ffn1_d_norm":
            blocks = flying["grad_w_in"].join(after=t)
            return {"_after": flying["grad_ffn1"].to_owners(after=blocks)}
        return {}

    W["_stage"] = stage

    loss8, grad_x, G = _local_step(x, loss_target, W)

    small = jnp.concatenate(
        [G["meta_tokens"]] + [_rows1024(G[n]) for n in SMALL if n != "meta_tokens"] + [_rows1024(loss8[0:1, 0:1])], axis=0)
    small = jnp.pad(small, ((0, 40 - small.shape[0]), (0, 0)))
    small_slots = _share8(small, "share_small", 13)

    grad_ffn1 = flying["grad_ffn1"]
    going = grad_ffn1.sums
    (g_w_in,) = flying["grad_w_in"].out
    Gb = dict(zip(("ffn2_w_gu",) + late, flying["grad_late"].out))
    Gb["w_in"] = g_w_in

    grads, delta, new_m, new_v, done = {}, {}, {}, {}, []
    cols = w_in.shape[2]
    to_tiles = lambda a: a.transpose(2, 0, 1).reshape(cols, 8, 128).reshape(cols * 8, 128)
    from_tiles = lambda a: a.reshape(cols, 1, D_MODEL).transpose(1, 2, 0)
    for n in [n for n in BIG if n in Gb]:
        if n == "w_in":
            g_t = to_tiles(Gb[n][None])
            d_, m_, v_ = _adamw(to_tiles(P[n]), g_t, to_tiles(M[n]), to_tiles(V[n]), f"adamw_{n}", after=going)
            grads[n], delta[n], new_m[n], new_v[n] = from_tiles(g_t), from_tiles(d_), from_tiles(m_), from_tiles(v_)
        else:
            d_, m_, v_ = _adamw(P[n][0], Gb[n], M[n][0], V[n][0], f"adamw_{n}", after=going)
            grads[n], delta[n], new_m[n], new_v[n] = Gb[n][None], d_[None], m_[None], v_[None]
        done.append(d_)

    small = _sum_slots(small_slots, "sum_small", after=done)
    Gs = {"meta_tokens": small[0:N_META]}
    r = N_META
    for n in SMALL:
        if n == "meta_tokens":
            continue
        nr = -(-G[n].size // 1024)
        Gs[n] = small[r:r + nr].reshape(-1)[:G[n].size].reshape(G[n].shape)
        r += nr
    loss = small[r, 0]
    Gs["meta_tokens"] = lax.dynamic_slice(Gs["meta_tokens"], (0, 256 * q), (N_META, 256))
    Gs["ssd_conv_w"] = lax.dynamic_slice(Gs["ssd_conv_w"], (0, 512 * q), (SSD_CONV, 512))[None]
    Gs = {n: Gs[n].reshape(P[n].shape) for n in SMALL}
    grads.update(Gs)
    flat = lambda a: a.reshape(-1, a.shape[-1])
    d_s, m_s, v_s = _adamw_many(*[[flat(D[n]) for n in SMALL] for D in (P, Gs, M, V)], "adamw_small")
    for i, n in enumerate(SMALL):
        delta[n], new_m[n], new_v[n] = d_s[i].reshape(P[n].shape), m_s[i].reshape(P[n].shape), v_s[i].reshape(P[n].shape)
    done.append(d_s[0])
    grad_ffn1.join(after=done)
    Gb["ffn1_w_gu"], Gb["ffn1_w_down"] = grad_ffn1.out
    for n in ("ffn1_w_gu", "ffn1_w_down"):
        d_, m_, v_ = _adamw(P[n][0], Gb[n], M[n][0], V[n][0], f"adamw_{n}")
        grads[n], delta[n], new_m[n], new_v[n] = Gb[n][None], d_[None], m_[None], v_[None]
    return (loss, grad_x, *[grads[n] for n in WEIGHTS], *[delta[n] for n in WEIGHTS],
            *[new_m[n] for n in WEIGHTS], *[new_v[n] for n in WEIGHTS])
```

```python
import functools

import jax
import jax.numpy as jnp
from jax import lax
from jax.experimental import pallas as pl
from jax.experimental.pallas import tpu as pltpu
from jax.experimental.pallas import tpu_sc as plsc

F32 = jnp.float32
BF16 = jnp.bfloat16
MESH = pl.DeviceIdType.MESH

D_MODEL = 1024
N_META = 16
EPS = 1e-6
SSD_HEADS = 16
SSD_HEAD_DIM = 64
SSD_INNER = 1024
SSD_GROUPS = 4
SSD_CONV = 4
HG_HEADS = 8
HG_SUB = 32
CHUNK = 128
D_FF = 2816
N_CHIPS = 4
IN_SIZES = (1024, 2048, 16, 1024, 1024, 1024, 1024, 1024, 1024)
ADAM_LR = 0.001
ADAM_B1 = 0.9
ADAM_B2 = 0.999
ADAM_EPS = 1e-08
ADAM_WD = 0.01
ADAM_STEP = 10
VMEM_LIMIT = 56 * 1024 * 1024
MATMUL_BLOCK_BYTES = 44 * 1024 * 1024
ADAMW_BLOCK_BYTES = 5 * 512 * 1024


def _cparams(sem=None):
    return pltpu.CompilerParams(dimension_semantics=sem, vmem_limit_bytes=VMEM_LIMIT)


def _pick(n, cands):
    for c in cands:
        if n % c == 0:
            return c
    return n


def _deps(after):
    xs = after if isinstance(after, (list, tuple)) else [after]
    one = lambda x: lax.slice(x, (0,) * x.ndim, (1,) * x.ndim).reshape(1).astype(F32)
    return jnp.concatenate([one(x) for x in xs]).reshape(1, -1)


def _dep_spec(dep):
    return pl.BlockSpec(dep.shape, lambda *_: (0, 0))


def _skip_ref(body, pos):
    return lambda *refs: body(*refs[:pos], *refs[pos + 1:])


def _dg(a, b, ca, cb):
    return lax.dot_general(a.astype(BF16), b.astype(BF16), (((ca,), (cb,)), ((), ())), preferred_element_type=F32)


@jax.custom_vjp
def _mm(a, b):
    return _dg(a, b, 1, 0)


def _mm_fwd(a, b):
    return _dg(a, b, 1, 0), (a, b)


def _mm_bwd(r, g):
    a, b = r
    return _dg(g, b, 1, 1), _dg(a, g, 0, 0)


_mm.defvjp(_mm_fwd, _mm_bwd)


@jax.custom_vjp
def _mm_nt(a, b):
    return _dg(a, b, 1, 1)


def _mm_nt_fwd(a, b):
    return _dg(a, b, 1, 1), (a, b)


def _mm_nt_bwd(r, g):
    a, b = r
    return _dg(g, b, 1, 0), _dg(g, a, 0, 0)


_mm_nt.defvjp(_mm_nt_fwd, _mm_nt_bwd)


@jax.custom_vjp
def _mm_tn(a, b):
    return _dg(a, b, 0, 0)


def _mm_tn_fwd(a, b):
    return _dg(a, b, 0, 0), (a, b)


def _mm_tn_bwd(r, g):
    a, b = r
    return _dg(b, g, 1, 1), _dg(a, g, 1, 0)


_mm_tn.defvjp(_mm_tn_fwd, _mm_tn_bwd)


def _tri_sum(x, lower):
    n = x.shape[0]
    ri = lax.broadcasted_iota(jnp.int32, (n, n), 0)
    ci = lax.broadcasted_iota(jnp.int32, (n, n), 1)
    tri = ((ri >= ci) if lower else (ri <= ci)).astype(BF16)
    x1 = x.astype(BF16)
    r1 = x - x1.astype(F32)
    x2 = r1.astype(BF16)
    x3 = (r1 - x2.astype(F32)).astype(BF16)
    dot = lambda p: lax.dot_general(tri, p, (((1,), (0,)), ((), ())), preferred_element_type=F32)
    return (dot(x3) + dot(x2)) + dot(x1)


@jax.custom_vjp
def _cumsum_rows(x):
    return _tri_sum(x, True)


_cumsum_rows.defvjp(lambda x: (_tri_sum(x, True), None), lambda _, g: (_tri_sum(g, False),))


def _silu(x):
    return x * jax.nn.sigmoid(x)


def _softplus(x):
    return jnp.maximum(x, 0.0) + jnp.log(1.0 + jnp.exp(-jnp.abs(x)))


def _tril(n):
    ri = lax.broadcasted_iota(jnp.int32, (n, n), 0)
    ci = lax.broadcasted_iota(jnp.int32, (n, n), 1)
    return ri >= ci


def _row_of(m, r):
    sub = lax.broadcasted_iota(jnp.int32, (m.shape[0], 1), 0)
    return jnp.sum(jnp.where(sub == r, m, 0.0), axis=0, keepdims=True)


def _col_of(m, c):
    lane = lax.broadcasted_iota(jnp.int32, (1, m.shape[1]), 1)
    return jnp.sum(jnp.where(lane == c, m, 0.0), axis=1, keepdims=True)


def _matmul(a, b, *, mode, out_dtype, name, alpha=1.0, res=None, tm=None, tn=None, out_groups=None, after=None):
    b3 = b.ndim == 3
    if mode == "nn":
        M, K = a.shape
        G = b.shape[0] if b3 else 1
        Ng = b.shape[-1]
        N = G * Ng
    elif mode == "nt":
        M, K = a.shape
        G = b.shape[0] if b3 else 1
        N = b.shape[-2]
        Kg = b.shape[-1]
        assert G * Kg == K
    else:
        K, M = a.shape
        N = b.shape[1]
        G = out_groups or 1
        Ng = N // G
    has_res = res is not None
    split_n = (mode == "nn" and b3) or (mode == "tn" and G > 1)
    per_mn = jnp.dtype(out_dtype).itemsize + (res.dtype.itemsize if has_res else 0)
    fits = [(m_ * n_, m_, n_)
            for m_ in (4352, 2176, 1408, 1088, 1024, 544, 512, 256, 128) if M % m_ == 0
            for n_ in (2816, 2048, 1408, 1024, 512, 256, 128) if (Ng if split_n else N) % n_ == 0
            if 2 * (K * m_ * a.dtype.itemsize + K * n_ * b.dtype.itemsize + m_ * n_ * per_mn) + 4 * m_ * n_ <= MATMUL_BLOCK_BYTES]
    _, tm_fit, tn_fit = max(fits)
    tm, tn = tm or tm_fit, tn or tn_fit
    nm, nn_ = M // tm, N // tn
    assert nm * tm == M and nn_ * tn == N, (name, M, N, K, tm, tn)

    if mode == "nn":
        a_spec = pl.BlockSpec((tm, K), lambda i, j: (i, 0))
        if b3:
            ns = Ng // tn
            b_spec = pl.BlockSpec((None, K, tn), lambda i, j: (j // ns, 0, j % ns))
        else:
            b_spec = pl.BlockSpec((K, tn), lambda i, j: (0, j))
        ca, cb = 1, 0
    elif mode == "nt":
        a_spec = pl.BlockSpec((tm, K), lambda i, j: (i, 0))
        if b3:
            b_spec = pl.BlockSpec((G, tn, Kg), lambda i, j: (0, j, 0))
        else:
            b_spec = pl.BlockSpec((tn, K), lambda i, j: (j, 0))
        ca, cb = 1, 1
    else:
        a_spec = pl.BlockSpec((K, tm), lambda i, j: (0, i))
        b_spec = pl.BlockSpec((K, tn), lambda i, j: (0, j))
        ca, cb = 0, 0
    if mode == "tn" and G > 1:
        ns = Ng // tn
        o_spec = pl.BlockSpec((None, tm, tn), lambda i, j: (j // ns, i, j % ns))
        out_shape = jax.ShapeDtypeStruct((G, M, Ng), out_dtype)
    else:
        o_spec = pl.BlockSpec((tm, tn), lambda i, j: (i, j))
        out_shape = jax.ShapeDtypeStruct((M, N), out_dtype)
    in_specs = [a_spec, b_spec]
    args = [a, b]
    if has_res:
        in_specs.append(pl.BlockSpec((tm, tn), lambda i, j: (i, j)))
        args.append(res)
    if after is not None:
        args.append(_deps(after))
        in_specs.append(_dep_spec(args[-1]))

    def body(*refs):
        a_ref, b_ref, o_ref = refs[0], refs[1], refs[-1]
        if mode == "nt" and b3:
            o = _dg(a_ref[:, 0:Kg], b_ref[0], ca, cb)
            for g in range(1, G):
                o = o + _dg(a_ref[:, g * Kg:(g + 1) * Kg], b_ref[g], ca, cb)
        else:
            o = _dg(a_ref[...], b_ref[...], ca, cb)
        if alpha != 1.0:
            o = o * alpha
        if has_res:
            o = o + refs[2][...]
        o_ref[...] = o.astype(o_ref.dtype)

    return pl.pallas_call(
        body, grid=(nm, nn_), in_specs=in_specs, out_specs=o_spec, out_shape=out_shape, name=name,
        compiler_params=_cparams(("parallel", "parallel")),
    )(*args)


def _sum_nt(xs, ws, name):
    R, N = xs[0].shape[0], ws[0].shape[0]
    n = len(xs)
    per_m = sum(x.shape[1] * x.dtype.itemsize for x in xs)
    per_n = sum(w.shape[1] * w.dtype.itemsize for w in ws)
    fits = [(m_ * n_, m_, n_) for m_ in (1088, 544, 256, 128) if R % m_ == 0 for n_ in (1024, 512, 256, 128) if N % n_ == 0
            if 2 * (m_ * per_m + n_ * per_n + m_ * n_ * 4) + 4 * m_ * n_ <= MATMUL_BLOCK_BYTES]
    _, tm, tn = max(fits)

    def body(*refs):
        o = _dg(refs[0][...], refs[n][...], 1, 1)
        for p in range(1, n):
            o = o + _dg(refs[p][...], refs[n + p][...], 1, 1)
        refs[-1][...] = o

    return pl.pallas_call(
        body, grid=(R // tm, N // tn),
        in_specs=[pl.BlockSpec((tm, x.shape[1]), lambda i, j: (i, 0)) for x in xs]
        + [pl.BlockSpec((tn, w.shape[1]), lambda i, j: (j, 0)) for w in ws],
        out_specs=pl.BlockSpec((tm, tn), lambda i, j: (i, j)), out_shape=jax.ShapeDtypeStruct((R, N), F32), name=name,
        compiler_params=_cparams(("parallel", "parallel")),
    )(*xs, *ws)


def _rms_fn(h, w):
    r = lax.rsqrt(jnp.mean(h * h, axis=-1, keepdims=True) + EPS)
    return h * r * w


def _swiglu_fn(gu):
    g = gu[:, :D_FF].astype(F32)
    u = gu[:, D_FF:].astype(F32)
    return _silu(g) * u


def _merge_fn(pa, pb, gates):
    return jax.nn.sigmoid(gates[:, :D_MODEL]) * pa + jax.nn.sigmoid(gates[:, D_MODEL:]) * pb


def _rows_call(body, *, rows, tr, ins, outs, accs=(), name, after=None):
    n = rows // tr
    assert n * tr == rows
    if after is not None:
        body = _skip_ref(body, len(ins))
        ins = list(ins) + [("full", _deps(after))]

    def spec(x):
        if isinstance(x, tuple):
            shp = x[1].shape
            return pl.BlockSpec(shp, lambda i: (0,) * len(shp))
        return pl.BlockSpec((tr, x.shape[1]), lambda i: (i, 0))

    in_specs = [spec(x) for x in ins]
    args = [x[1] if isinstance(x, tuple) else x for x in ins]
    out_specs = [spec(x) for x in outs] + [pl.BlockSpec(x.shape, lambda i: (0,) * len(x.shape)) for x in accs]
    out_shape = [x[1] if isinstance(x, tuple) else x for x in outs] + list(accs)
    return pl.pallas_call(
        body, grid=(n,), in_specs=in_specs, out_specs=out_specs, out_shape=out_shape, name=name,
        compiler_params=_cparams(("arbitrary",)),
    )(*args)


def _acc_rows(ref, val):
    @pl.when(pl.program_id(0) == 0)
    def _():
        ref[...] = jnp.zeros_like(ref)

    ref[0:1, :] += val


def _rms_fwd(h, w, name):
    def body(h_ref, w_ref, o_ref):
        o_ref[...] = _rms_fn(h_ref[...], w_ref[...]).astype(o_ref.dtype)

    R = h.shape[0]
    return _rows_call(body, rows=R, tr=_pick(R, (256, 128)), ins=[h, ("full", w)],
                      outs=[jax.ShapeDtypeStruct(h.shape, BF16)], name=name)[0]


def _rms_bwd(h, w, dn, dres, name, after=None):
    def body(h_ref, w_ref, dn_ref, dres_ref, dh_ref, dw_ref):
        _, vjp = jax.vjp(_rms_fn, h_ref[...], w_ref[...])
        dh, dw = vjp(dn_ref[...].astype(F32))
        dh_ref[...] = dh + dres_ref[...]
        _acc_rows(dw_ref, dw)

    R = h.shape[0]
    return _rows_call(body, rows=R, tr=_pick(R, (256, 128)), ins=[h, ("full", w), dn, dres],
                      outs=[jax.ShapeDtypeStruct(h.shape, F32)], accs=[jax.ShapeDtypeStruct((8, D_MODEL), F32)], name=name,
                      after=after)


def _d_norm_in(dgu, w_gu, h, norm_w, dres, name, after=None):
    R = h.shape[0]
    G, _, kg = w_gu.shape

    def body(dgu_ref, w_ref, h_ref, nw_ref, dres_ref, dh_ref, dw_ref):
        dn = _dg(dgu_ref[:, 0:kg], w_ref[0], 1, 1)
        for g in range(1, G):
            dn = dn + _dg(dgu_ref[:, kg * g:kg * (g + 1)], w_ref[g], 1, 1)
        _, vjp = jax.vjp(_rms_fn, h_ref[...], nw_ref[...])
        dh, dw = vjp(dn)
        dh_ref[...] = dh + dres_ref[...]
        _acc_rows(dw_ref, dw)

    return _rows_call(body, rows=R, tr=_pick(R, (256, 128)), ins=[dgu, ("full", w_gu), h, ("full", norm_w), dres],
                      outs=[jax.ShapeDtypeStruct(h.shape, F32)], accs=[jax.ShapeDtypeStruct((8, D_MODEL), F32)], name=name,
                      after=after)


def _rms_bwd_tokens(h, w, dn, dres, nseq, name, after=None):
    Tp = h.shape[0] // nseq
    nc = Tp // CHUNK

    def body(h_ref, w_ref, dn_ref, dres_ref, dx_ref, dm_ref, dw_ref):
        b, c = pl.program_id(0), pl.program_id(1)
        _, vjp = jax.vjp(_rms_fn, h_ref[...], w_ref[...])
        dh, dw = vjp(dn_ref[...].astype(F32))
        dh = dh + dres_ref[...]

        @pl.when(c == 0)
        def _():
            dm_ref[...] = dh

        @pl.when(c > 0)
        def _():
            dx_ref[...] = dh

        @pl.when((b == 0) & (c == 0))
        def _():
            dw_ref[...] = jnp.zeros_like(dw_ref)

        dw_ref[0:1, :] += dw

    rows = pl.BlockSpec((CHUNK, D_MODEL), lambda b, c: (b * nc + c, 0))
    in_specs, args = [rows, pl.BlockSpec((1, D_MODEL), lambda b, c: (0, 0)), rows, rows], [h, w, dn, dres]
    if after is not None:
        body = _skip_ref(body, len(args))
        args.append(_deps(after))
        in_specs.append(_dep_spec(args[-1]))
    return pl.pallas_call(
        body, grid=(nseq, nc), in_specs=in_specs,
        out_specs=[pl.BlockSpec((None, CHUNK, D_MODEL), lambda b, c: (b, jnp.maximum(c - 1, 0), 0)),
                   pl.BlockSpec((None, CHUNK, D_MODEL), lambda b, c: (b, 0, 0)),
                   pl.BlockSpec((8, D_MODEL), lambda b, c: (0, 0))],
        out_shape=[jax.ShapeDtypeStruct((nseq, Tp - CHUNK, D_MODEL), F32), jax.ShapeDtypeStruct((nseq, CHUNK, D_MODEL), F32),
                   jax.ShapeDtypeStruct((8, D_MODEL), F32)],
        name=name, compiler_params=_cparams(("arbitrary", "arbitrary")),
    )(*args)


def _gu_swiglu(n, w_gu, name):
    R = n.shape[0]
    G, _, ng = w_gu.shape

    def body(n_ref, w_ref, gu_ref, a_ref):
        x = n_ref[...]
        for r in range(G):
            gu_ref[:, ng * r:ng * (r + 1)] = _dg(x, w_ref[r], 1, 0).astype(gu_ref.dtype)
        a_ref[...] = _swiglu_fn(gu_ref[...]).astype(a_ref.dtype)

    return _rows_call(body, rows=R, tr=_pick(R, (256, 128)), ins=[n, ("full", w_gu)],
                      outs=[jax.ShapeDtypeStruct((R, 2 * D_FF), BF16), jax.ShapeDtypeStruct((R, D_FF), BF16)], name=name)


def _d_swiglu(dout, w_down, gu, alpha, name):
    R = gu.shape[0]

    def body(do_ref, w_ref, gu_ref, o_ref):
        da = _dg(do_ref[...] * alpha, w_ref[...], 1, 1)
        g = gu_ref[:, :D_FF].astype(F32)
        u = gu_ref[:, D_FF:].astype(F32)
        s = jax.nn.sigmoid(g)
        t = g * s
        o_ref[:, :D_FF] = (da * u * (s + t - t * s)).astype(o_ref.dtype)
        o_ref[:, D_FF:] = (da * t).astype(o_ref.dtype)

    return _rows_call(body, rows=R, tr=_pick(R, (256, 128)), ins=[dout, ("full", w_down), gu],
                      outs=[jax.ShapeDtypeStruct(gu.shape, BF16)], name=name)[0]


def _residual_matmul(a, w, res, alpha, name, norm_w=None):
    R, K = a.shape

    def body(a_ref, w_ref, r_ref, *rest):
        out = r_ref[...] + alpha * _dg(a_ref[...], w_ref[...], 1, 0)
        if norm_w is None:
            rest[0][...] = out
        else:
            rest[1][...] = out
            rest[2][...] = _rms_fn(out, rest[0][...]).astype(rest[2].dtype)

    f32 = jax.ShapeDtypeStruct((R, D_MODEL), F32)
    ins = [a, ("full", w), res] + ([] if norm_w is None else [("full", norm_w)])
    outs = [f32] + ([] if norm_w is None else [jax.ShapeDtypeStruct((R, D_MODEL), BF16)])
    got = _rows_call(body, rows=R, tr=_pick(R, (544, 256, 128)), ins=ins, outs=outs, name=name)
    return got[0] if norm_w is None else (got[0], got[1])


def _branch_merge(ya, yb, wa, wb, gates, name):
    def body(ya_ref, yb_ref, wa_ref, wb_ref, g_ref, pa_ref, pb_ref, o_ref):
        pa = _dg(ya_ref[...], wa_ref[...], 1, 0)
        pb = _dg(yb_ref[...], wb_ref[...], 1, 0)
        pa_ref[...] = pa
        pb_ref[...] = pb
        o_ref[...] = _merge_fn(pa, pb, g_ref[...].astype(F32)).astype(o_ref.dtype)

    R = ya.shape[0]
    f32 = jax.ShapeDtypeStruct((R, D_MODEL), F32)
    return _rows_call(body, rows=R, tr=_pick(R, (544, 256, 128)), ins=[ya, yb, ("full", wa), ("full", wb), gates],
                      outs=[f32, f32, jax.ShapeDtypeStruct((R, D_MODEL), BF16)], name=name)


def _branch_merge_bwd(pa, pb, gates, dm, wa, wb, name):
    def body(pa_ref, pb_ref, g_ref, dm_ref, wa_ref, wb_ref, dpa_ref, dpb_ref, dg_ref, dya_ref, dyb_ref):
        _, vjp = jax.vjp(_merge_fn, pa_ref[...], pb_ref[...], g_ref[...].astype(F32))
        dpa, dpb, dg = vjp(dm_ref[...].astype(F32))
        dpa_ref[...] = dpa.astype(dpa_ref.dtype)
        dpb_ref[...] = dpb.astype(dpb_ref.dtype)
        dg_ref[...] = dg.astype(dg_ref.dtype)
        dya_ref[...] = _dg(dpa, wa_ref[...], 1, 1).astype(dya_ref.dtype)
        dyb_ref[...] = _dg(dpb, wb_ref[...], 1, 1).astype(dyb_ref.dtype)

    R = pa.shape[0]
    b16 = jax.ShapeDtypeStruct(pa.shape, BF16)
    return _rows_call(body, rows=R, tr=_pick(R, (544, 256, 128)), ins=[pa, pb, gates, dm, ("full", wa), ("full", wb)],
                      outs=[b16, b16, jax.ShapeDtypeStruct(gates.shape, BF16), b16, b16], name=name)


def _loss_head(h3, w, target, nseq, name):
    Tp = h3.shape[0] // nseq
    nc = Tp // CHUNK

    def fn(h, w_, t, valid):
        y = _rms_fn(h, w_)
        e = (y - t) * valid
        return 0.5 * jnp.sum(jnp.mean(e * e, axis=-1, keepdims=True))

    def body(h_ref, w_ref, t_ref, loss_ref, dh_ref, dw_ref):
        b, c = pl.program_id(0), pl.program_id(1)
        valid = (c >= 1).astype(F32)
        t = t_ref[...]
        loss, vjp = jax.vjp(lambda h, w_: fn(h, w_, t, valid), h_ref[...], w_ref[...])
        dh, dw = vjp(jnp.ones((), F32))
        dh_ref[...] = dh

        @pl.when((b == 0) & (c == 0))
        def _():
            loss_ref[...] = jnp.zeros_like(loss_ref)
            dw_ref[...] = jnp.zeros_like(dw_ref)

        loss_ref[...] += jnp.full(loss_ref.shape, loss, F32)
        dw_ref[0:1, :] += dw

    return pl.pallas_call(
        body, grid=(nseq, nc),
        in_specs=[pl.BlockSpec((CHUNK, D_MODEL), lambda b, c: (b * nc + c, 0)),
                  pl.BlockSpec((1, D_MODEL), lambda b, c: (0, 0)),
                  pl.BlockSpec((None, CHUNK, D_MODEL), lambda b, c: (b, jnp.maximum(c - 1, 0), 0))],
        out_specs=[pl.BlockSpec((8, 128), lambda b, c: (0, 0)),
                   pl.BlockSpec((CHUNK, D_MODEL), lambda b, c: (b * nc + c, 0)),
                   pl.BlockSpec((8, D_MODEL), lambda b, c: (0, 0))],
        out_shape=[jax.ShapeDtypeStruct((8, 128), F32), jax.ShapeDtypeStruct(h3.shape, F32),
                   jax.ShapeDtypeStruct((8, D_MODEL), F32)],
        name=name, compiler_params=_cparams(("arbitrary", "arbitrary")),
    )(h3, w, target)


CONV_TILE = 512
CONV_HALO = 8


def _conv_fwd(xbc, w, b, pad, name):
    B, Tp, C = xbc.shape
    nch = Tp // CHUNK

    def body(x_ref, w_ref, b_ref, o_ref, xp):
        xp[0:CONV_HALO, :] = jnp.zeros((CONV_HALO, CONV_TILE), F32)
        xp[CONV_HALO:, :] = x_ref[...]
        for c in range(nch):
            acc = jnp.zeros((CHUNK, CONV_TILE), F32) + b_ref[...]
            for k in range(SSD_CONV):
                acc = acc + w_ref[k:k + 1, :] * xp[pl.ds(CONV_HALO + CHUNK * c - (SSD_CONV - 1) + k, CHUNK), :]
            out = _silu(acc)
            if CHUNK * c < pad:
                row = CHUNK * c + lax.broadcasted_iota(jnp.int32, (CHUNK, 1), 0)
                out = jnp.where(row >= pad, out, 0.0)
            o_ref[pl.ds(CHUNK * c, CHUNK), :] = out

    return pl.pallas_call(
        body, grid=(B, C // CONV_TILE),
        in_specs=[pl.BlockSpec((None, Tp, CONV_TILE), lambda i, j: (i, 0, j)),
                  pl.BlockSpec((SSD_CONV, CONV_TILE), lambda i, j: (0, j)),
                  pl.BlockSpec((1, CONV_TILE), lambda i, j: (0, j))],
        out_specs=pl.BlockSpec((None, Tp, CONV_TILE), lambda i, j: (i, 0, j)),
        out_shape=jax.ShapeDtypeStruct(xbc.shape, F32),
        scratch_shapes=[pltpu.VMEM((Tp + CONV_HALO, CONV_TILE), F32)],
        name=name, compiler_params=_cparams(("arbitrary", "arbitrary")),
    )(xbc, w, b)


def _conv_bwd(xbc, w, b, dact, pad, name):
    B, Tp, C = xbc.shape
    nch = Tp // CHUNK

    def body(x_ref, w_ref, b_ref, da_ref, dx_ref, dw_ref, db_ref, xp, dp):
        bi = pl.program_id(1)
        xp[0:CONV_HALO, :] = jnp.zeros((CONV_HALO, CONV_TILE), F32)
        xp[CONV_HALO:, :] = x_ref[...]
        dp[pl.ds(Tp, CONV_HALO), :] = jnp.zeros((CONV_HALO, CONV_TILE), F32)
        dws = [jnp.zeros((1, CONV_TILE), F32) for _ in range(SSD_CONV)]
        dbs = jnp.zeros((1, CONV_TILE), F32)
        for c in range(nch):
            xs = [xp[pl.ds(CONV_HALO + CHUNK * c - (SSD_CONV - 1) + k, CHUNK), :] for k in range(SSD_CONV)]
            acc = jnp.zeros((CHUNK, CONV_TILE), F32) + b_ref[...]
            for k in range(SSD_CONV):
                acc = acc + w_ref[k:k + 1, :] * xs[k]
            sg = jax.nn.sigmoid(acc)
            t = acc * sg
            dpre = da_ref[pl.ds(CHUNK * c, CHUNK), :] * (sg + t - t * sg)
            if CHUNK * c < pad:
                row = CHUNK * c + lax.broadcasted_iota(jnp.int32, (CHUNK, 1), 0)
                dpre = jnp.where(row >= pad, dpre, 0.0)
            dp[pl.ds(CHUNK * c, CHUNK), :] = dpre
            dbs = dbs + jnp.sum(dpre, axis=0, keepdims=True)
            for k in range(SSD_CONV):
                dws[k] = dws[k] + jnp.sum(dpre * xs[k], axis=0, keepdims=True)
        for c in range(nch):
            acc = jnp.zeros((CHUNK, CONV_TILE), F32)
            for k in range(SSD_CONV):
                acc = acc + w_ref[k:k + 1, :] * dp[pl.ds(CHUNK * c + (SSD_CONV - 1) - k, CHUNK), :]
            dx_ref[pl.ds(CHUNK * c, CHUNK), :] = acc.astype(dx_ref.dtype)

        @pl.when(bi == 0)
        def _():
            dw_ref[...] = jnp.zeros_like(dw_ref)
            db_ref[...] = jnp.zeros_like(db_ref)

        for k in range(SSD_CONV):
            dw_ref[k:k + 1, :] += dws[k]
        db_ref[0:1, :] += dbs

    return pl.pallas_call(
        body, grid=(C // CONV_TILE, B),
        in_specs=[pl.BlockSpec((None, Tp, CONV_TILE), lambda j, i: (i, 0, j)),
                  pl.BlockSpec((SSD_CONV, CONV_TILE), lambda j, i: (0, j)),
                  pl.BlockSpec((1, CONV_TILE), lambda j, i: (0, j)),
                  pl.BlockSpec((None, Tp, CONV_TILE), lambda j, i: (i, 0, j))],
        out_specs=[pl.BlockSpec((None, Tp, CONV_TILE), lambda j, i: (i, 0, j)),
                   pl.BlockSpec((8, CONV_TILE), lambda j, i: (0, j)),
                   pl.BlockSpec((8, CONV_TILE), lambda j, i: (0, j))],
        out_shape=[jax.ShapeDtypeStruct(xbc.shape, BF16), jax.ShapeDtypeStruct((8, C), F32),
                   jax.ShapeDtypeStruct((8, C), F32)],
        scratch_shapes=[pltpu.VMEM((Tp + CONV_HALO, CONV_TILE), F32), pltpu.VMEM((Tp + CONV_HALO, CONV_TILE), F32)],
        name=name, compiler_params=_cparams(("arbitrary", "arbitrary")),
    )(xbc, w, b, dact)


def _ssd_chunk(xs, bm, cm, dtr, z, state, dt_bias, a_log, dskip, norm_w, valid, kept=None, keep=False):
    Q = xs.shape[0]
    known = (lambda x, v: x) if kept is None else _known
    lane = lax.broadcasted_iota(jnp.int32, (1, 128), 1)
    dt = jnp.where(lane < SSD_HEADS, _softplus(dtr + dt_bias), 0.0) * valid
    a = dt * (-jnp.exp(a_log))
    tril = _tril(Q)
    cs = known(_cumsum_rows(a), None if kept is None else kept[0])
    cs_t = cs.T
    cs_end = _row_of(cs, Q - 1)
    low = lane < SSD_HEAD_DIM
    low_rows = lax.broadcasted_iota(jnp.int32, (128, 1), 0) < SSD_HEAD_DIM
    ys, new_state, cbs = [], [], []
    for g in range(SSD_GROUPS):
        bg = bm[:, 128 * g:128 * (g + 1)]
        cg = cm[:, 128 * g:128 * (g + 1)]
        cb = known(_mm_nt(cg, bg), None if kept is None else kept[1][Q * g:Q * (g + 1)])
        cbs.append(cb)
        for pr in range(2):
            p = 2 * g + pr
            h0, h1 = 2 * p, 2 * p + 1
            xp = xs[:, 128 * p:128 * (p + 1)]
            c0, c1 = _col_of(cs, h0), _col_of(cs, h1)
            e0, e1 = _col_of(cs_end, h0), _col_of(cs_end, h1)
            xd = xp * jnp.where(low, _col_of(dt, h0), _col_of(dt, h1))
            l0 = jnp.exp(jnp.where(tril, c0 - _row_of(cs_t, h0), -1e30))
            l1 = jnp.exp(jnp.where(tril, c1 - _row_of(cs_t, h1), -1e30))
            y_diag = jnp.where(low, _mm(cb * l0, xd), _mm(cb * l1, xd))
            to_end = jnp.where(low, jnp.exp(e0 - c0), jnp.exp(e1 - c1))
            sp = state[128 * p:128 * (p + 1), :]
            y_off = _mm_nt(cg, sp) * jnp.where(low, jnp.exp(c0), jnp.exp(c1))
            new_state.append(sp * jnp.where(low_rows, jnp.exp(e0), jnp.exp(e1)) + _mm_tn(xd * to_end, bg))
            ys.append(y_diag + y_off + xp * jnp.where(low, _col_of(dskip, h0), _col_of(dskip, h1)))
    y_raw = known(jnp.concatenate(ys, axis=1), None if kept is None else kept[2])
    y = y_raw * _silu(z)
    gw = SSD_INNER // SSD_GROUPS
    outs = []
    for g in range(SSD_GROUPS):
        blk = y[:, gw * g:gw * (g + 1)]
        outs.append(blk * lax.rsqrt(jnp.mean(blk * blk, axis=-1, keepdims=True) + EPS))
    out, state_out = jnp.concatenate(outs, axis=1) * norm_w, jnp.concatenate(new_state, axis=0)
    if kept is not None:
        state_out = _known(state_out, state)
    return (out, state_out, (cs, jnp.concatenate(cbs, axis=0), y_raw)) if keep else (out, state_out)


def _valid_rows(c, pad):
    row = c * CHUNK + lax.broadcasted_iota(jnp.int32, (CHUNK, 1), 0)
    return (row >= pad).astype(F32)


def _ssd_fwd(xact, dtr, z, dt_bias, a_log, dskip, norm_w, pad, name):
    B, Tp, _ = xact.shape
    nc = Tp // CHUNK

    def body(xs_ref, bm_ref, cm_ref, dt_ref, z_ref, db_ref, al_ref, ds_ref, nw_ref, y_ref, save_ref, cs_ref, cb_ref, yr_ref, st):
        c = pl.program_id(1)

        @pl.when(c == 0)
        def _():
            st[...] = jnp.zeros_like(st)

        s0 = st[...]
        save_ref[...] = s0
        y, s1, (cs, cb, y_raw) = _ssd_chunk(xs_ref[...], bm_ref[...], cm_ref[...], dt_ref[...], z_ref[...].astype(F32), s0,
                                            db_ref[...], al_ref[...], ds_ref[...], nw_ref[...], _valid_rows(c, pad), keep=True)
        y_ref[...] = y.astype(y_ref.dtype)
        cs_ref[...] = cs
        cb_ref[...] = cb
        yr_ref[...] = y_raw
        st[...] = s1

    row = lambda w, off=0: pl.BlockSpec((None, CHUNK, w), lambda b, c: (b, c, off))
    par = lambda w: pl.BlockSpec((1, w), lambda b, c: (0, 0))
    per_chunk = lambda r: pl.BlockSpec((None, None, r, 128), lambda b, c: (b, c, 0, 0))
    return pl.pallas_call(
        body, grid=(B, nc),
        in_specs=[row(1024, 0), row(512, 2), row(512, 3), row(128), row(1024), par(128), par(128), par(128), par(1024)],
        out_specs=[row(1024), per_chunk(1024), row(128), per_chunk(SSD_GROUPS * CHUNK), row(1024)],
        out_shape=[jax.ShapeDtypeStruct((B, Tp, SSD_INNER), BF16), jax.ShapeDtypeStruct((B, nc, 1024, 128), F32),
                   jax.ShapeDtypeStruct((B, Tp, 128), F32), jax.ShapeDtypeStruct((B, nc, SSD_GROUPS * CHUNK, 128), F32),
                   jax.ShapeDtypeStruct((B, Tp, SSD_INNER), F32)],
        scratch_shapes=[pltpu.VMEM((1024, 128), F32)],
        name=name, compiler_params=_cparams(("arbitrary", "arbitrary")),
    )(xact, xact, xact, dtr, z, dt_bias, a_log, dskip, norm_w)


def _ssd_bwd(xact, dtr, z, dt_bias, a_log, dskip, norm_w, saved, kept, dy, pad, name, after=None):
    B, Tp, _ = xact.shape
    nc = Tp // CHUNK

    def body(xs_ref, bm_ref, cm_ref, dt_ref, z_ref, db_ref, al_ref, ds_ref, nw_ref, sv_ref, cs_ref, cb_ref, yr_ref, dy_ref,
             dx_ref, ddt_ref, dz_ref, dpar_ref, dnw_ref, dst):
        b, i = pl.program_id(0), pl.program_id(1)
        c = nc - 1 - i

        @pl.when(i == 0)
        def _():
            dst[...] = jnp.zeros_like(dst)

        valid = _valid_rows(c, pad)
        kept_c = (cs_ref[...], cb_ref[...], yr_ref[...])
        fn = lambda *a: _ssd_chunk(*a, valid, kept=kept_c)
        _, vjp = jax.vjp(fn, xs_ref[...], bm_ref[...], cm_ref[...], dt_ref[...], z_ref[...].astype(F32), sv_ref[...],
                         db_ref[...], al_ref[...], ds_ref[...], nw_ref[...])
        dxs, dbm, dcm, ddt, dz, dstate, ddb, dal, dds, dnw = vjp((dy_ref[...].astype(F32), dst[...]))
        dx_ref[:, 0:1024] = dxs
        dx_ref[:, 1024:1536] = dbm
        dx_ref[:, 1536:2048] = dcm
        ddt_ref[...] = ddt
        dz_ref[...] = dz.astype(dz_ref.dtype)
        dst[...] = dstate

        @pl.when((b == 0) & (i == 0))
        def _():
            dpar_ref[...] = jnp.zeros_like(dpar_ref)
            dnw_ref[...] = jnp.zeros_like(dnw_ref)

        dpar_ref[0:1, :] += ddb
        dpar_ref[1:2, :] += dal
        dpar_ref[2:3, :] += dds
        dnw_ref[0:1, :] += dnw

    row = lambda w, off=0: pl.BlockSpec((None, CHUNK, w), lambda b, i: (b, nc - 1 - i, off))
    par = lambda w: pl.BlockSpec((1, w), lambda b, i: (0, 0))
    acc = lambda w: pl.BlockSpec((8, w), lambda b, i: (0, 0))
    per_chunk = lambda r: pl.BlockSpec((None, None, r, 128), lambda b, i: (b, nc - 1 - i, 0, 0))
    in_specs = [row(1024, 0), row(512, 2), row(512, 3), row(128), row(1024), par(128), par(128), par(128), par(1024),
                per_chunk(1024), row(128), per_chunk(SSD_GROUPS * CHUNK), row(1024), row(1024)]
    args = [xact, xact, xact, dtr, z, dt_bias, a_log, dskip, norm_w, saved, kept[0], kept[1], kept[2], dy]
    if after is not None:
        body = _skip_ref(body, len(args))
        args.append(_deps(after))
        in_specs.append(_dep_spec(args[-1]))
    outs = pl.pallas_call(
        body, grid=(B, nc), in_specs=in_specs,
        out_specs=[row(2048), row(128), row(1024), acc(128), acc(1024)],
        out_shape=[jax.ShapeDtypeStruct((B, Tp, 2048), F32), jax.ShapeDtypeStruct((B, Tp, 128), F32),
                   jax.ShapeDtypeStruct((B, Tp, 1024), BF16), jax.ShapeDtypeStruct((8, 128), F32),
                   jax.ShapeDtypeStruct((8, 1024), F32)],
        scratch_shapes=[pltpu.VMEM((1024, 128), F32)],
        name=name, compiler_params=_cparams(("arbitrary", "arbitrary")),
    )(*args)
    return outs


@jax.custom_vjp
def _known(x, value):
    return value


_known.defvjp(lambda x, value: (value, None), lambda _, g: (g, jnp.zeros_like(g)))


def _hg_chunk(qr, fr, ir, gr, state_t, p0, p1, norm_w, valid, kept=None, keep=False):
    Q = qr.shape[0]
    known = (lambda x, i: x) if kept is None else (lambda x, i: _known(x, kept[i].astype(x.dtype)))
    lb = jax.nn.sigmoid(p0 - p1)
    f = lb + (1.0 - lb) * jax.nn.sigmoid(fr)
    k = 1.0 - f
    q = _silu(qr)
    v = ir * valid
    cum = known(_cumsum_rows(jnp.log(f)), 0)
    cum_end = _row_of(cum, Q - 1)
    o_inter = _mm_nt(q * jnp.exp(cum), state_t)
    nblk = Q // HG_SUB
    row = lax.broadcasted_iota(jnp.int32, (Q, 1), 0)
    ri = lax.broadcasted_iota(jnp.int32, (Q, Q), 0)
    ci = lax.broadcasted_iota(jnp.int32, (Q, Q), 1)
    mids = jnp.concatenate([jnp.broadcast_to(_row_of(cum, HG_SUB * i + HG_SUB // 2 - 1), (HG_SUB, cum.shape[1]))
                            for i in range(nblk)], axis=0)
    sh = HG_SUB.bit_length() - 1
    same = (jnp.right_shift(ri, sh) == jnp.right_shift(ci, sh)) & (ri >= ci)
    att = jnp.where(same, _mm_nt(q * jnp.exp(cum - mids), k * jnp.exp(mids - cum)), 0.0)
    qas, kas = [], []
    for i in range(1, nblk):
        lo = HG_SUB * i
        start = _row_of(cum, lo - 1)
        qas.append(q * jnp.exp(jnp.where((row >= lo) & (row < lo + HG_SUB), cum - start, -1e30)))
        kas.append(k * jnp.exp(jnp.where(row < lo, start - cum, -1e30)))
    att = att + _mm_nt(jnp.concatenate(qas, axis=1), jnp.concatenate(kas, axis=1))
    att = known(att, 1)
    o = known(o_inter + _mm(att, v), 2)
    new_state_t = state_t * jnp.exp(cum_end) + _mm_tn(v, k * jnp.exp(cum_end - cum))
    if kept is not None:
        new_state_t = _known(new_state_t, state_t)
    y = o * lax.rsqrt(jnp.mean(o * o, axis=-1, keepdims=True) + EPS) * norm_w * _silu(gr)
    return (y, new_state_t, (cum, att, o)) if keep else (y, new_state_t)


HG_PER_STEP = 8
HG_COLS = 4 * 128


def _hg_fwd(qfig, lbh, nwh, pad, name):
    B, Tp, _ = qfig.shape
    nc = Tp // CHUNK
    hp = HG_PER_STEP

    def body(x_ref, lb_ref, nw_ref, y_ref, save_ref, cum_ref, att_ref, o_ref, st):
        c = pl.program_id(1)

        @pl.when(c == 0)
        def _():
            st[...] = jnp.zeros_like(st)

        valid = _valid_rows(c, pad)
        for j in range(hp):
            for b in range(B):
                s0 = st[j, b]
                save_ref[j, b] = s0
                col = lambda k: x_ref[b, :, HG_COLS * j + 128 * k:HG_COLS * j + 128 * (k + 1)]
                y, s1, (cum, att, o) = _hg_chunk(col(0), col(1), col(2), col(3), s0, lb_ref[j, 0:1, :], lb_ref[j, 1:2, :],
                                                 nw_ref[j], valid, keep=True)
                y_ref[b, :, 128 * j:128 * (j + 1)] = y.astype(y_ref.dtype)
                cum_ref[b, :, 128 * j:128 * (j + 1)] = cum
                att_ref[j, b] = att.astype(att_ref.dtype)
                o_ref[b, :, 128 * j:128 * (j + 1)] = o
                st[j, b] = s1

    rows = pl.BlockSpec((B, CHUNK, 128 * hp), lambda h, c: (0, c, h))
    per_chunk = pl.BlockSpec((hp, B, None, 128, 128), lambda h, c: (h, 0, c, 0, 0))
    return pl.pallas_call(
        body, grid=(HG_HEADS // hp, nc),
        in_specs=[pl.BlockSpec((B, CHUNK, HG_COLS * hp), lambda h, c: (0, c, h)),
                  pl.BlockSpec((hp, 2, 128), lambda h, c: (h, 0, 0)),
                  pl.BlockSpec((hp, 1, 128), lambda h, c: (h, 0, 0))],
        out_specs=[rows, per_chunk, rows, per_chunk, rows],
        out_shape=[jax.ShapeDtypeStruct((B, Tp, 1024), BF16), jax.ShapeDtypeStruct((HG_HEADS, B, nc, 128, 128), F32),
                   jax.ShapeDtypeStruct((B, Tp, 1024), F32), jax.ShapeDtypeStruct((HG_HEADS, B, nc, 128, 128), BF16),
                   jax.ShapeDtypeStruct((B, Tp, 1024), F32)],
        scratch_shapes=[pltpu.VMEM((hp, B, 128, 128), F32)],
        name=name, compiler_params=_cparams(("arbitrary", "arbitrary")),
    )(qfig, lbh, nwh)


def _hg_bwd(qfig, lbh, nwh, saved, kept, dy, pad, name, after=None):
    B, Tp, _ = qfig.shape
    nc = Tp // CHUNK
    hp = HG_PER_STEP

    def body(x_ref, lb_ref, nw_ref, sv_ref, cum_ref, att_ref, o_ref, dy_ref, dx_ref, dlb_ref, dnw_ref, dst):
        i = pl.program_id(1)
        c = nc - 1 - i

        @pl.when(i == 0)
        def _():
            dst[...] = jnp.zeros_like(dst)
            dlb_ref[...] = jnp.zeros_like(dlb_ref)
            dnw_ref[...] = jnp.zeros_like(dnw_ref)

        valid = _valid_rows(c, pad)
        for j in range(hp):
            for b in range(B):
                col = lambda k: x_ref[b, :, HG_COLS * j + 128 * k:HG_COLS * j + 128 * (k + 1)]
                head = slice(128 * j, 128 * (j + 1))
                kept_jb = (cum_ref[b, :, head], att_ref[j, b], o_ref[b, :, head])
                fn = lambda *a: _hg_chunk(*a, valid, kept=kept_jb)
                _, vjp = jax.vjp(fn, col(0), col(1), col(2), col(3), sv_ref[j, b], lb_ref[j, 0:1, :], lb_ref[j, 1:2, :], nw_ref[j])
                d4 = vjp((dy_ref[b, :, 128 * j:128 * (j + 1)].astype(F32), dst[j, b]))
                for k in range(4):
                    dx_ref[b, :, HG_COLS * j + 128 * k:HG_COLS * j + 128 * (k + 1)] = d4[k].astype(dx_ref.dtype)
                dst[j, b] = d4[4]
                dlb_ref[j, 0:1, :] += d4[5]
                dlb_ref[j, 1:2, :] += d4[6]
                dnw_ref[j, 0:1, :] += d4[7]

    acc = pl.BlockSpec((hp, 8, 128), lambda h, i: (h, 0, 0))
    rows = pl.BlockSpec((B, CHUNK, 128 * hp), lambda h, i: (0, nc - 1 - i, h))
    per_chunk = pl.BlockSpec((hp, B, None, 128, 128), lambda h, i: (h, 0, nc - 1 - i, 0, 0))
    in_specs = [pl.BlockSpec((B, CHUNK, HG_COLS * hp), lambda h, i: (0, nc - 1 - i, h)),
                pl.BlockSpec((hp, 2, 128), lambda h, i: (h, 0, 0)),
                pl.BlockSpec((hp, 1, 128), lambda h, i: (h, 0, 0)),
                per_chunk, rows, per_chunk, rows, rows]
    args = [qfig, lbh, nwh, saved, kept[0], kept[1], kept[2], dy]
    if after is not None:
        body = _skip_ref(body, len(args))
        args.append(_deps(after))
        in_specs.append(_dep_spec(args[-1]))
    return pl.pallas_call(
        body, grid=(HG_HEADS // hp, nc), in_specs=in_specs,
        out_specs=[pl.BlockSpec((B, CHUNK, HG_COLS * hp), lambda h, i: (0, nc - 1 - i, h)), acc, acc],
        out_shape=[jax.ShapeDtypeStruct((B, Tp, 4096), BF16), jax.ShapeDtypeStruct((HG_HEADS, 8, 128), F32),
                   jax.ShapeDtypeStruct((HG_HEADS, 8, 128), F32)],
        scratch_shapes=[pltpu.VMEM((hp, B, 128, 128), F32)],
        name=name, compiler_params=_cparams(("arbitrary", "arbitrary")),
    )(*args)


def _adamw_math(w, g, m, v):
    m = ADAM_B1 * m + (1.0 - ADAM_B1) * g
    v = ADAM_B2 * v + (1.0 - ADAM_B2) * (g * g)
    m_hat = m / (1.0 - ADAM_B1 ** ADAM_STEP)
    v_hat = v / (1.0 - ADAM_B2 ** ADAM_STEP)
    return -ADAM_LR * (m_hat / (jnp.sqrt(v_hat) + ADAM_EPS) + ADAM_WD * w), m, v


def _adamw_many(ws, gs, ms, vs, name):
    n = len(ws)

    def body(*refs):
        for i in range(n):
            d, m, v = _adamw_math(refs[i][...], refs[n + i][...], refs[2 * n + i][...], refs[3 * n + i][...])
            refs[4 * n + i][...] = d
            refs[5 * n + i][...] = m
            refs[6 * n + i][...] = v

    vm = pl.BlockSpec(memory_space=pltpu.VMEM)
    outs = pl.pallas_call(body, in_specs=[vm] * (4 * n), out_specs=[vm] * (3 * n),
                          out_shape=[jax.ShapeDtypeStruct(w.shape, F32) for w in ws] * 3, name=name)(*ws, *gs, *ms, *vs)
    return outs[:n], outs[n:2 * n], outs[2 * n:]


def _adamw(w, g, m, v, name, after=None):
    R, C = w.shape
    tr = max(t for t in range(8, R + 1, 8) if R % t == 0 and (t * C * 4 <= ADAMW_BLOCK_BYTES or t == 8))

    def body(w_ref, g_ref, m_ref, v_ref, d_ref, mo_ref, vo_ref):
        d_ref[...], mo_ref[...], vo_ref[...] = _adamw_math(w_ref[...], g_ref[...], m_ref[...], v_ref[...])

    sp = pl.BlockSpec((tr, C), lambda i: (i, 0))
    sh = jax.ShapeDtypeStruct((R, C), F32)
    in_specs, args = [sp] * 4, [w, g, m, v]
    if after is not None:
        body = _skip_ref(body, len(args))
        args.append(_deps(after))
        in_specs.append(_dep_spec(args[-1]))
    return pl.pallas_call(body, grid=(R // tr,), in_specs=in_specs, out_specs=[sp] * 3, out_shape=[sh] * 3,
                          name=name, compiler_params=_cparams(("arbitrary",)))(*args)


def _ffn_fwd(h, norm_w, w_gu, w_down, tag, after_norm=None, n=None, next_norm_w=None):
    if n is None:
        n = _rms_fwd(h, norm_w, f"{tag}_norm")
    if after_norm is not None:
        after_norm(n)
    gu, a = _gu_swiglu(n, w_gu, f"{tag}_gu")
    out = _residual_matmul(a, w_down, h, 0.5, f"{tag}_down", next_norm_w)
    return out, (n, gu, a)


def _ffn_bwd(h, norm_w, w_gu, w_down, saved, dout, tag, after_dw_down=None, token_seqs=None, told=None):
    n, gu, a = saved
    dgu = _d_swiglu(dout, w_down, gu, 0.5, f"{tag}_d_gu")
    dw_down = _matmul(a, dout, mode="tn", out_dtype=F32, alpha=0.5, name=f"{tag}_dw_down")
    dw_gu = _matmul(n, dgu, mode="tn", out_dtype=F32, out_groups=N_CHIPS, name=f"{tag}_dw_gu",
                    after=after_dw_down(dw_down) if after_dw_down else None)
    if token_seqs is None:
        dh, dnw = _d_norm_in(dgu, w_gu, h, norm_w, dout, f"{tag}_d_in", after=dw_gu)
    else:
        if told is not None:
            told("dw", (dw_gu, dw_down))
        dn = _matmul(dgu, w_gu, mode="nt", out_dtype=F32, name=f"{tag}_d_norm", after=dw_gu)
        dx, dm, dnw = _rms_bwd_tokens(h, norm_w, dn, dout, token_seqs, f"{tag}_d_in",
                                      after=told("d_norm", dn) if told is not None else None)
        dh = (dx, dm)
    return dh, dnw, dw_gu, dw_down


def _split_w_in(w_in_full):
    pts = [0]
    for s in IN_SIZES:
        pts.append(pts[-1] + s)
    sl = lambda i, j: w_in_full[:, pts[i]:pts[j]]
    qfig = sl(3, 7).reshape(D_MODEL, 4, HG_HEADS, 128).transpose(0, 2, 1, 3).reshape(D_MODEL, 4 * D_MODEL)
    return {"z": sl(0, 1), "xbc": sl(1, 2), "dt": jnp.pad(sl(2, 3), ((0, 0), (0, 128 - SSD_HEADS))),
            "qfig": qfig, "gates": sl(7, 9)}


def _local_step(x, target, W):
    B, S, _ = x.shape
    T = N_META + S
    pad = (-T) % CHUNK
    Tp = T + pad
    assert pad + N_META == CHUNK
    R = B * Tp
    meta = jnp.broadcast_to(W["meta_tokens"][None], (B, N_META, D_MODEL))
    h0 = jnp.concatenate([jnp.zeros((B, pad, D_MODEL), F32), meta, x], axis=1).reshape(R, D_MODEL)

    stage = W.get("_stage", lambda name, x: {})
    W = dict(W)
    (h1, um), sv1 = _ffn_fwd(h0, W["ffn1_norm"], W["ffn1_w_gu"], W["ffn1_w_down"], "ffn1",
                             lambda n: W.update(stage("ffn1_norm", n)), next_norm_w=W["mix_norm"])
    W.update(stage("ffn1_out", h1))
    wi = W["w_in"]
    z = _matmul(um, wi["z"], mode="nn", out_dtype=BF16, name="in_z")
    xbc = _matmul(um, wi["xbc"], mode="nn", out_dtype=F32, name="in_xbc")
    dtr = _matmul(um, wi["dt"], mode="nn", out_dtype=F32, name="in_dt")
    qfig = _matmul(um, wi["qfig"], mode="nn", out_dtype=F32, name="in_qfig")
    gates = _matmul(um, wi["gates"], mode="nn", out_dtype=BF16, name="in_gates")

    r3 = lambda t: t.reshape(B, Tp, t.shape[-1])
    lane_pad = lambda t: jnp.pad(t, ((0, 0), (0, 128 - t.shape[1])))
    dt_bias, a_log, dskip = lane_pad(W["ssd_dt_bias"]), lane_pad(W["ssd_a_log"]), lane_pad(W["ssd_d"])
    xact = _conv_fwd(r3(xbc), W["ssd_conv_w"], W["ssd_conv_b"], pad, "conv_fwd")
    ya, ssd_saved, *ssd_kept = _ssd_fwd(xact, r3(dtr), r3(z), dt_bias, a_log, dskip, W["ssd_norm"], pad, "ssd_fwd")
    lbh = W["hg_lower_bound"].reshape(2, HG_HEADS, 128).transpose(1, 0, 2)
    nwh = W["hg_norm"].reshape(HG_HEADS, 1, 128)
    yb, hg_saved, *hg_kept = _hg_fwd(r3(qfig), lbh, nwh, pad, "hg_fwd")
    ya2, yb2 = ya.reshape(R, -1), yb.reshape(R, -1)
    W.update(stage("mixers_out", yb2))
    pa, pb, mg = _branch_merge(ya2, yb2, W["w_branch_a"], W["w_branch_b"], gates, "branch_merge")
    h2, n2 = _residual_matmul(mg, W["w_out"], h1, 1.0, "mix_out", W["ffn2_norm"])
    h3, sv2 = _ffn_fwd(h2, W["ffn2_norm"], W["ffn2_w_gu"], W["ffn2_w_down"], "ffn2", n=n2)

    loss, dh3, d_final = _loss_head(h3, W["final_norm"].reshape(1, D_MODEL), target, B, "loss_head")

    G = {"final_norm": d_final[0]}
    dh2, dnw, G["ffn2_w_gu"], G["ffn2_w_down"] = _ffn_bwd(h2, W["ffn2_norm"], W["ffn2_w_gu"], W["ffn2_w_down"], sv2, dh3, "ffn2")
    G["ffn2_norm"] = dnw[0:1]
    dmg = _matmul(dh2, W["w_out"], mode="nt", out_dtype=BF16, name="d_merge")
    G["w_out"] = _matmul(mg, dh2, mode="tn", out_dtype=F32, name="dw_out")
    dpa, dpb, dgates, dya, dyb = _branch_merge_bwd(pa, pb, gates, dmg, W["w_branch_a"], W["w_branch_b"], "branch_merge_bwd")
    G["w_branch_a"] = _matmul(ya2, dpa, mode="tn", out_dtype=F32, name="dw_branch_a")
    G["w_branch_b"] = _matmul(yb2, dpb, mode="tn", out_dtype=F32, name="dw_branch_b")

    dxact, ddtr, dz, dpar, dnw = _ssd_bwd(xact, r3(dtr), r3(z), dt_bias, a_log, dskip, W["ssd_norm"], ssd_saved, ssd_kept,
                                          r3(dya), pad, "ssd_bwd", after=stage("late_grads", G).get("_after"))
    G["ssd_dt_bias"], G["ssd_a_log"], G["ssd_d"] = dpar[0:1, :SSD_HEADS], dpar[1:2, :SSD_HEADS], dpar[2:3, :SSD_HEADS]
    G["ssd_norm"] = dnw[0:1]
    dxbc, dcw, dcb = _conv_bwd(r3(xbc), W["ssd_conv_w"], W["ssd_conv_b"], dxact, pad, "conv_bwd")
    G["ssd_conv_w"], G["ssd_conv_b"] = dcw[0:SSD_CONV], dcb[0:1]
    dqfig, dlb, dhn = _hg_bwd(r3(qfig), lbh, nwh, hg_saved, hg_kept, r3(dyb), pad, "hg_bwd",
                              after=stage("after_conv_bwd", dcb).get("_after"))
    G["hg_lower_bound"] = dlb[:, 0:2, :].transpose(1, 0, 2).reshape(2, D_MODEL)
    G["hg_norm"] = dhn[:, 0, :].reshape(1, D_MODEL)

    r2 = lambda t: t.reshape(R, t.shape[-1])
    pieces = [("z", r2(dz)), ("xbc", r2(dxbc)), ("dt", r2(ddtr)), ("qfig", r2(dqfig)), ("gates", dgates)]
    dum = _sum_nt([p for _, p in pieces], [wi[nm] for nm, _ in pieces], "d_mix")
    dwi = {nm: _matmul(um, dpiece, mode="tn", out_dtype=F32, name=f"dw_in_{nm}") for nm, dpiece in pieces}
    dw_qfig = dwi["qfig"].reshape(D_MODEL, HG_HEADS, 4, 128).transpose(0, 2, 1, 3).reshape(D_MODEL, 4 * D_MODEL)
    G["w_in"] = jnp.concatenate([dwi["z"], dwi["xbc"], dwi["dt"][:, :SSD_HEADS], dw_qfig, dwi["gates"]], axis=1)
    dh1, dnw = _rms_bwd(h1, W["mix_norm"], dum, dh2, "mix_norm_bwd", after=stage("w_in_grads", dwi).get("_after"))
    G["mix_norm"] = dnw[0:1]
    (dx, dfirst), dnw, G["ffn1_w_gu"], G["ffn1_w_down"] = _ffn_bwd(
        h0, W["ffn1_norm"], W["ffn1_w_gu"], W["ffn1_w_down"], sv1, dh1, "ffn1",
        lambda dw: stage("ffn1_dw_down", dw).get("_after"), token_seqs=B,
        told=lambda name, t: stage("ffn1_" + name, t).get("_after"))
    G["ffn1_norm"] = dnw[0:1]
    G["meta_tokens"] = jnp.sum(dfirst[:, pad:CHUNK], axis=0)
    return loss, dx, G


def _place():
    return lax.axis_index("x"), lax.axis_index("y"), lax.axis_index("c")


def _other_chips(x, y):
    return [(1 - x, y), (x, 1 - y), (1 - x, 1 - y)]


def _remote(src, dst, ssem, rsem, dev):
    return pltpu.make_async_remote_copy(src_ref=src, dst_ref=dst, send_sem=ssem, recv_sem=rsem,
                                        device_id=dev, device_id_type=MESH)


def _exchange8(buf, name):
    n, w = buf.shape

    def body(x_ref, out_ref, ssem, rsem):
        x, y, c = _place()
        me = 4 * x + 2 * y + c
        out_ref[me] = x_ref[...]
        copies = []
        for k in range(1, 8):
            px = 1 - x if (k >> 2) & 1 else x
            py = 1 - y if (k >> 1) & 1 else y
            pc = 1 - c if k & 1 else c
            cp = _remote(x_ref, out_ref.at[me], ssem.at[k - 1], rsem.at[k - 1], (px, py, pc))
            cp.start()
            copies.append((cp, 4 * px + 2 * py + pc))
        for k, (cp, peer) in enumerate(copies):
            _remote(x_ref, out_ref.at[peer], ssem.at[k], rsem.at[k], (x, y, c)).wait_recv()
        for cp, _ in copies:
            cp.wait_send()

    vm = pl.BlockSpec(memory_space=pltpu.VMEM)
    return pl.pallas_call(
        body, in_specs=[vm], out_specs=vm, out_shape=jax.ShapeDtypeStruct((8, n, w), F32),
        scratch_shapes=[pltpu.SemaphoreType.DMA((7,)), pltpu.SemaphoreType.DMA((7,))], name=name,
    )(buf)


HBM = pltpu.MemorySpace.HBM


def _sequencer(name, collective_id, sems, sent):
    return functools.partial(pl.kernel, mesh=plsc.ScalarSubcoreMesh(axis_name="sequencer", num_cores=1), name=name,
                             scratch_types=sems, compiler_params=pltpu.CompilerParams(collective_id=collective_id),
                             cost_estimate=pl.CostEstimate(flops=0, transcendentals=0, bytes_accessed=2 * sent,
                                                           remote_bytes_transferred=sent))


def _nbytes(arrays):
    return sum(a.size * a.dtype.itemsize for a in arrays)


def _handshake(peers):
    barrier = pltpu.get_barrier_semaphore()
    for peer in peers:
        pl.semaphore_signal(barrier, inc=1, device_id=peer, device_id_type=MESH)
    pl.semaphore_wait(barrier, len(peers))


def _gather_seq(blocks, name, collective_id):
    n = len(blocks)
    half = [s.shape[1] // 2 for s in blocks]
    full = [jax.new_ref(b, memory_space=HBM) for b in blocks]

    @_sequencer(name, collective_id, [pltpu.SemaphoreType.DMA((n, 3))] * 4, _nbytes(blocks) * 3 // 4)
    def launch(ssem, rsem, fssem, frsem):
        x, y, c = _place()
        q = 2 * x + y
        chips = _other_chips(x, y)
        _handshake([(px, py, c) for px, py in chips] + [(x, y, 1 - c)])
        piece = lambda s, qq, cc: full[s].at[qq, pl.ds(cc * half[s], half[s])]
        sends = []
        for j, (px, py) in enumerate(chips):
            for s in range(n):
                cp = _remote(piece(s, q, c), piece(s, q, c), ssem.at[s, j], rsem.at[s, j], (px, py, c))
                cp.start()
                sends.append(cp)
        for j, (px, py) in enumerate(chips):
            for s in range(n):
                got = piece(s, 2 * px + py, c)
                _remote(got, got, ssem.at[s, j], rsem.at[s, j], (px, py, c)).wait_recv()
                cp = _remote(got, got, fssem.at[s, j], frsem.at[s, j], (x, y, 1 - c))
                cp.start()
                sends.append(cp)
        for j, (px, py) in enumerate(chips):
            for s in range(n):
                got = piece(s, 2 * px + py, 1 - c)
                _remote(got, got, fssem.at[s, j], frsem.at[s, j], (x, y, 1 - c)).wait_recv()
        for cp in sends:
            cp.wait_send()

    launch()
    return [r[...] for r in full]


def _share8(buf, name, collective_id):
    n, w = buf.shape
    src = jax.new_ref(buf, memory_space=HBM)
    out = jax.empty_ref(jax.ShapeDtypeStruct((8, n, w), F32), memory_space=HBM)

    @_sequencer(name, collective_id, [pltpu.SemaphoreType.DMA((7,)), pltpu.SemaphoreType.DMA((7,)), pltpu.SemaphoreType.DMA((1,))],
                7 * buf.size * 4)
    def launch(ssem, rsem, lsem):
        x, y, c = _place()
        me = 4 * x + 2 * y + c
        peers = [(1 - x if (k >> 2) & 1 else x, 1 - y if (k >> 1) & 1 else y, 1 - c if k & 1 else c) for k in range(1, 8)]
        _handshake(peers)
        mine = pltpu.make_async_copy(src, out.at[me], lsem.at[0])
        mine.start()
        sends = []
        for k, peer in enumerate(peers):
            cp = _remote(src, out.at[me], ssem.at[k], rsem.at[k], peer)
            cp.start()
            sends.append(cp)
        for k, (px, py, pc) in enumerate(peers):
            slot = out.at[4 * px + 2 * py + pc]
            _remote(slot, slot, ssem.at[k], rsem.at[k], (px, py, pc)).wait_recv()
        for cp in sends:
            cp.wait_send()
        mine.wait()

    launch()
    return out[...]


def _sum_slots(slots, name, after=None):
    _, n, w = slots.shape

    def body(s_ref, o_ref):
        acc = s_ref[0]
        for d in range(1, 8):
            acc = acc + s_ref[d]
        o_ref[...] = acc

    vm = pl.BlockSpec(memory_space=pltpu.VMEM)
    in_specs, args = [vm], [slots]
    if after is not None:
        body = _skip_ref(body, 1)
        args.append(_deps(after))
        in_specs.append(vm)
    return pl.pallas_call(body, in_specs=in_specs, out_specs=vm, out_shape=jax.ShapeDtypeStruct((n, w), F32), name=name)(*args)


def _pair_swap(parts, name, collective_id):
    n = len(parts)
    half = [p.shape[1] // 2 for p in parts]
    src = [jax.new_ref(p, memory_space=HBM) for p in parts]
    got = [jax.empty_ref(jax.ShapeDtypeStruct((p.shape[0], h, p.shape[2]), p.dtype), memory_space=HBM) for p, h in zip(parts, half)]

    @_sequencer(name, collective_id, [pltpu.SemaphoreType.DMA((n,))] * 2, _nbytes(parts) // 2)
    def launch(ssem, rsem):
        x, y, c = _place()
        _handshake([(x, y, 1 - c)])
        copies = []
        for s in range(n):
            cp = _remote(src[s].at[pl.ds(0, parts[s].shape[0]), pl.ds((1 - c) * half[s], half[s])], got[s], ssem.at[s], rsem.at[s], (x, y, 1 - c))
            cp.start()
            copies.append(cp)
        for cp in copies:
            cp.wait_recv()
        for cp in copies:
            cp.wait_send()

    launch()
    return [g[...] for g in got]


def _to_owners(sums, name, collective_id):
    n = len(sums)
    src = [jax.new_ref(s, memory_space=HBM) for s in sums]
    got = [jax.empty_ref(jax.ShapeDtypeStruct(s.shape, s.dtype), memory_space=HBM) for s in sums]

    @_sequencer(name, collective_id, [pltpu.SemaphoreType.DMA((n, 3))] * 2, _nbytes(sums) * 3 // 4)
    def launch(ssem, rsem):
        x, y, c = _place()
        q = 2 * x + y
        chips = _other_chips(x, y)
        _handshake([(px, py, c) for px, py in chips])
        sends = []
        for j, (px, py) in enumerate(chips):
            for s in range(n):
                cp = _remote(src[s].at[2 * px + py], got[s].at[q], ssem.at[s, j], rsem.at[s, j], (px, py, c))
                cp.start()
                sends.append(cp)
        for j, (px, py) in enumerate(chips):
            for s in range(n):
                slot = got[s].at[2 * px + py]
                _remote(slot, slot, ssem.at[s, j], rsem.at[s, j], (px, py, c)).wait_recv()
        for cp in sends:
            cp.wait_send()

    launch()
    return [g[...] for g in got]


def _pair_join(blocks, name, collective_id):
    n = len(blocks)
    out = [jax.new_ref(b, memory_space=HBM) for b in blocks]

    @_sequencer(name, collective_id, [pltpu.SemaphoreType.DMA((n,))] * 2, _nbytes(blocks) // 2)
    def launch(ssem, rsem):
        x, y, c = _place()
        _handshake([(x, y, 1 - c)])
        sends = []
        for s in range(n):
            h = blocks[s].shape[0] // 2
            mine = out[s].at[pl.ds(c * h, h)]
            cp = _remote(mine, mine, ssem.at[s], rsem.at[s], (x, y, 1 - c))
            cp.start()
            sends.append(cp)
        for s in range(n):
            h = blocks[s].shape[0] // 2
            theirs = out[s].at[pl.ds((1 - c) * h, h)]
            _remote(theirs, theirs, ssem.at[s], rsem.at[s], (x, y, 1 - c)).wait_recv()
        for cp in sends:
            cp.wait_send()

    launch()
    return [o[...] for o in out]


WIRE = BF16


def _row_tile(h):
    return _pick(h, (256, 368, 352, 128, 16))


def _add_pair(part, got, c, name, after=None):
    _, h, w = got.shape
    tr = _row_tile(h)
    nt = h // tr

    def body(c_ref, p_ref, g_ref, o_ref):
        o_ref[...] = (p_ref[...] + g_ref[...].astype(F32)).astype(o_ref.dtype)

    in_specs = [pl.BlockSpec((None, tr, w), lambda q, i, c_ref: (q, c_ref[0] * nt + i, 0)),
                pl.BlockSpec((None, tr, w), lambda q, i, c_ref: (q, i, 0))]
    args = [c.reshape(1).astype(jnp.int32), part, got]
    if after is not None:
        body = _skip_ref(body, len(args))
        args.append(_deps(after))
        in_specs.append(_dep_spec(args[-1]))
    return pl.pallas_call(
        body,
        grid_spec=pltpu.PrefetchScalarGridSpec(
            num_scalar_prefetch=1, grid=(got.shape[0], nt), in_specs=in_specs,
            out_specs=pl.BlockSpec((None, tr, w), lambda q, i, c_ref: (q, i, 0))),
        out_shape=jax.ShapeDtypeStruct(got.shape, WIRE), name=name,
        compiler_params=_cparams(("arbitrary", "arbitrary")),
    )(*args)


def _sum_chips(slots, sums, q, c, name, after=None):
    _, h, w = slots.shape
    tr = _row_tile(h)
    nt = h // tr

    def body(s_ref, mine_ref, a_ref, b_ref, d_ref, o_ref):
        o_ref[...] = ((mine_ref[...].astype(F32) + a_ref[...].astype(F32)) + b_ref[...].astype(F32)) + d_ref[...].astype(F32)

    slot = lambda k: pl.BlockSpec((None, tr, w), lambda i, s_ref: (s_ref[1 + k], i, 0))
    scalars = jnp.stack([c, q, (q + 1) % N_CHIPS, (q + 2) % N_CHIPS, (q + 3) % N_CHIPS]).astype(jnp.int32)
    in_specs, args = [slot(0), slot(1), slot(2), slot(3)], [scalars, sums, slots, slots, slots]
    if after is not None:
        body = _skip_ref(body, len(args))
        args.append(_deps(after))
        in_specs.append(_dep_spec(args[-1]))
    return pl.pallas_call(
        body,
        grid_spec=pltpu.PrefetchScalarGridSpec(
            num_scalar_prefetch=1, grid=(nt,), in_specs=in_specs,
            out_specs=pl.BlockSpec((tr, w), lambda i, s_ref: (s_ref[0] * nt + i, 0))),
        out_shape=jax.ShapeDtypeStruct((2 * h, w), F32), name=name,
        compiler_params=_cparams(("arbitrary",)),
    )(*args)


class _Reduce:
    def __init__(self, parts, q, c, tag, first_id, regions=None):
        self.parts, self.q, self.c, self.tag, self.first_id, self.regions = parts, q, c, tag, first_id, regions
        self.got = _pair_swap(parts, f"{tag}_pair_swap", first_id)

    def to_owners(self, after=None):
        self.sums = [_add_pair(p, g, self.c, f"{self.tag}_pair_add{i}", after)
                     for i, (p, g) in enumerate(zip(self.parts, self.got))]
        if self.regions is not None:
            self.sums = self.regions(self.sums)
        self.slots = _to_owners(self.sums, f"{self.tag}_to_owners", self.first_id + 1)
        return self.sums

    def join(self, after=None):
        blocks = [_sum_chips(sl, sm, self.q, self.c, f"{self.tag}_sum_chips{i}", after)
                  for i, (sl, sm) in enumerate(zip(self.slots, self.sums))]
        self.out = _pair_join(blocks, f"{self.tag}_pair_join", self.first_id + 2)
        return blocks


WEIGHTS = ("meta_tokens", "ffn1_norm", "ffn1_w_gu", "ffn1_w_down", "mix_norm", "w_in", "ssd_conv_w", "ssd_conv_b",
           "ssd_dt_bias", "ssd_a_log", "ssd_d", "ssd_norm", "hg_lower_bound", "hg_norm", "w_branch_a", "w_branch_b",
           "w_out", "ffn2_norm", "ffn2_w_gu", "ffn2_w_down", "final_norm")
BIG = ("ffn1_w_gu", "ffn1_w_down", "w_in", "w_branch_a", "w_branch_b", "w_out", "ffn2_w_gu", "ffn2_w_down")
SMALL = tuple(n for n in WEIGHTS if n not in BIG)


def _rows1024(a):
    flat = a.reshape(-1)
    n = -(-flat.shape[0] // 1024) * 1024
    return jnp.pad(flat, (0, n - flat.shape[0])).reshape(-1, 1024)


def kernel(x, meta_tokens, ffn1_norm, ffn1_w_gu, ffn1_w_down, mix_norm, w_in, ssd_conv_w, ssd_conv_b, ssd_dt_bias, ssd_a_log, ssd_d, ssd_norm, hg_lower_bound, hg_norm, w_branch_a, w_branch_b, w_out, ffn2_norm, ffn2_w_gu, ffn2_w_down, final_norm, loss_target, m_meta_tokens, m_ffn1_norm, m_ffn1_w_gu, m_ffn1_w_down, m_mix_norm, m_w_in, m_ssd_conv_w, m_ssd_conv_b, m_ssd_dt_bias, m_ssd_a_log, m_ssd_d, m_ssd_norm, m_hg_lower_bound, m_hg_norm, m_w_branch_a, m_w_branch_b, m_w_out, m_ffn2_norm, m_ffn2_w_gu, m_ffn2_w_down, m_final_norm, v_meta_tokens, v_ffn1_norm, v_ffn1_w_gu, v_ffn1_w_down, v_mix_norm, v_w_in, v_ssd_conv_w, v_ssd_conv_b, v_ssd_dt_bias, v_ssd_a_log, v_ssd_d, v_ssd_norm, v_hg_lower_bound, v_hg_norm, v_w_branch_a, v_w_branch_b, v_w_out, v_ffn2_norm, v_ffn2_w_gu, v_ffn2_w_down, v_final_norm):
    P = dict(zip(WEIGHTS, (meta_tokens, ffn1_norm, ffn1_w_gu, ffn1_w_down, mix_norm, w_in, ssd_conv_w, ssd_conv_b, ssd_dt_bias, ssd_a_log, ssd_d, ssd_norm, hg_lower_bound, hg_norm, w_branch_a, w_branch_b, w_out, ffn2_norm, ffn2_w_gu, ffn2_w_down, final_norm)))
    M = dict(zip(WEIGHTS, (m_meta_tokens, m_ffn1_norm, m_ffn1_w_gu, m_ffn1_w_down, m_mix_norm, m_w_in, m_ssd_conv_w, m_ssd_conv_b, m_ssd_dt_bias, m_ssd_a_log, m_ssd_d, m_ssd_norm, m_hg_lower_bound, m_hg_norm, m_w_branch_a, m_w_branch_b, m_w_out, m_ffn2_norm, m_ffn2_w_gu, m_ffn2_w_down, m_final_norm)))
    V = dict(zip(WEIGHTS, (v_meta_tokens, v_ffn1_norm, v_ffn1_w_gu, v_ffn1_w_down, v_mix_norm, v_w_in, v_ssd_conv_w, v_ssd_conv_b, v_ssd_dt_bias, v_ssd_a_log, v_ssd_d, v_ssd_norm, v_hg_lower_bound, v_hg_norm, v_w_branch_a, v_w_branch_b, v_w_out, v_ffn2_norm, v_ffn2_w_gu, v_ffn2_w_down, v_final_norm)))
    cx, cy, cc = _place()
    q = 2 * cx + cy

    mine = jnp.concatenate([meta_tokens.reshape(4, 1024), ssd_conv_w.reshape(2, 1024), jnp.zeros((2, 1024), F32)], axis=0)
    every = _exchange8(mine, "gather_small")
    meta_full = jnp.concatenate([every[2 * k, 0:4].reshape(N_META, 256) for k in range(N_CHIPS)], axis=1)
    conv_w_full = jnp.concatenate([every[2 * k, 4:6].reshape(SSD_CONV, 512) for k in range(N_CHIPS)], axis=1)

    late = ("ffn2_w_down", "w_branch_a", "w_branch_b", "w_out")
    rows = jnp.concatenate([P[n][0] for n in late], axis=0)
    zero = lambda t, dtype=F32: (t[0:1, 0:1] * 0).astype(dtype)

    def in_slot(s, after=None):
        s = s if after is None else s + zero(after)
        return lax.dynamic_update_slice(lax.empty((N_CHIPS,) + s.shape, BF16), s.astype(BF16)[None], (q, 0, 0))

    (gu1,) = _gather_seq([in_slot(ffn1_w_gu[0])], "gather_ffn1_gu", 1)
    (down1,) = _gather_seq([in_slot(ffn1_w_down[0])], "gather_ffn1_down", 14)
    W = {n: P[n] for n in SMALL}
    W["meta_tokens"], W["ssd_conv_w"] = meta_full, conv_w_full
    W["ffn1_w_gu"], W["ffn1_w_down"] = gu1, down1.reshape(-1, D_MODEL)
    flying = {}

    def stage(name, t):
        if name == "ffn1_norm":
            flying["w_in"] = _gather_seq([in_slot(w_in[0], t)], "gather_w_in", 2)
            return {}
        if name == "ffn1_out":
            flying["late"] = _gather_seq([in_slot(ffn2_w_gu[0], t), in_slot(rows, t)], "gather_late", 3)
            (w_in_all,) = flying["w_in"]
            w_in_all = w_in_all + zero(t, BF16)
            return {"w_in": _split_w_in(w_in_all.transpose(1, 0, 2).reshape(D_MODEL, -1))}
        if name == "mixers_out":
            gu2, rows_all = flying["late"]
            out, r = {"ffn2_w_gu": gu2}, 0
            for n in late:
                nr = P[n].shape[1]
                out[n] = (rows_all[:, r:r + nr] + zero(t, BF16)).reshape(N_CHIPS * nr, D_MODEL)
                r += nr
            return out
        if name == "late_grads":
            parts = [t["ffn2_w_gu"]] + [t[n].reshape(N_CHIPS, -1, D_MODEL) for n in late]
            flying["grad_late"] = _Reduce(parts, q, cc, "grad_late", 4)
            return {"_after": [t["ffn2_w_gu"]] + [t[n] for n in late]}
        if name == "after_conv_bwd":
            return {"_after": flying["grad_late"].to_owners(after=t)}
        if name == "w_in_grads":
            order = ("z", "xbc", "dt", "qfig", "gates")
            blocks = flying["grad_late"].join(after=[t[k] for k in order])

            def regions(sums):
                z, xbc, dt, qfig, gates = [s[0] for s in sums]
                h = z.shape[0]
                qfig = qfig.reshape(h, HG_HEADS, 4, 128).transpose(0, 2, 1, 3).reshape(h, 4 * D_MODEL)
                cols = jnp.concatenate([z, xbc, dt[:, :SSD_HEADS], qfig, gates], axis=1)
                return [cols.reshape(h, N_CHIPS, -1).transpose(1, 0, 2)]

            flying["grad_w_in"] = _Reduce([t[k][None] for k in order], q, cc, "grad_w_in", 7, regions)
            return {"_after": blocks}
        if name == "ffn1_dw_down":
            return {"_after": flying["grad_w_in"].to_owners(after=t)}
        if name == "ffn1_dw":
            dw_gu, dw_down = t
            flying["grad_ffn1"] = _Reduce([dw_gu, dw_down.reshape(N_CHIPS, -1, D_MODEL)], q, cc, "grad_ffn1", 10)
            return {}
        if name == "ffn1_d_norm":
            blocks = flying["grad_w_in"].join(after=t)
            return {"_after": flying["grad_ffn1"].to_owners(after=blocks)}
        return {}

    W["_stage"] = stage

    loss8, grad_x, G = _local_step(x, loss_target, W)

    small = jnp.concatenate(
        [G["meta_tokens"]] + [_rows1024(G[n]) for n in SMALL if n != "meta_tokens"] + [_rows1024(loss8[0:1, 0:1])], axis=0)
    small = jnp.pad(small, ((0, 40 - small.shape[0]), (0, 0)))
    small_slots = _share8(small, "share_small", 13)

    grad_ffn1 = flying["grad_ffn1"]
    going = grad_ffn1.sums
    (g_w_in,) = flying["grad_w_in"].out
    Gb = dict(zip(("ffn2_w_gu",) + late, flying["grad_late"].out))
    Gb["w_in"] = g_w_in

    grads, delta, new_m, new_v, done = {}, {}, {}, {}, []
    cols = w_in.shape[2]
    to_tiles = lambda a: a.transpose(2, 0, 1).reshape(cols, 8, 128).reshape(cols * 8, 128)
    from_tiles = lambda a: a.reshape(cols, 1, D_MODEL).transpose(1, 2, 0)
    for n in [n for n in BIG if n in Gb]:
        if n == "w_in":
            g_t = to_tiles(Gb[n][None])
            d_, m_, v_ = _adamw(to_tiles(P[n]), g_t, to_tiles(M[n]), to_tiles(V[n]), f"adamw_{n}", after=going)
            grads[n], delta[n], new_m[n], new_v[n] = from_tiles(g_t), from_tiles(d_), from_tiles(m_), from_tiles(v_)
        else:
            d_, m_, v_ = _adamw(P[n][0], Gb[n], M[n][0], V[n][0], f"adamw_{n}", after=going)
            grads[n], delta[n], new_m[n], new_v[n] = Gb[n][None], d_[None], m_[None], v_[None]
        done.append(d_)

    small = _sum_slots(small_slots, "sum_small", after=done)
    Gs = {"meta_tokens": small[0:N_META]}
    r = N_META
    for n in SMALL:
        if n == "meta_tokens":
            continue
        nr = -(-G[n].size // 1024)
        Gs[n] = small[r:r + nr].reshape(-1)[:G[n].size].reshape(G[n].shape)
        r += nr
    loss = small[r, 0]
    Gs["meta_tokens"] = lax.dynamic_slice(Gs["meta_tokens"], (0, 256 * q), (N_META, 256))
    Gs["ssd_conv_w"] = lax.dynamic_slice(Gs["ssd_conv_w"], (0, 512 * q), (SSD_CONV, 512))[None]
    Gs = {n: Gs[n].reshape(P[n].shape) for n in SMALL}
    grads.update(Gs)
    flat = lambda a: a.reshape(-1, a.shape[-1])
    d_s, m_s, v_s = _adamw_many(*[[flat(D[n]) for n in SMALL] for D in (P, Gs, M, V)], "adamw_small")
    for i, n in enumerate(SMALL):
        delta[n], new_m[n], new_v[n] = d_s[i].reshape(P[n].shape), m_s[i].reshape(P[n].shape), v_s[i].reshape(P[n].shape)
    done.append(d_s[0])
    grad_ffn1.join(after=done)
    Gb["ffn1_w_gu"], Gb["ffn1_w_down"] = grad_ffn1.out
    for n in ("ffn1_w_gu", "ffn1_w_down"):
        d_, m_, v_ = _adamw(P[n][0], Gb[n], M[n][0], V[n][0], f"adamw_{n}")
        grads[n], delta[n], new_m[n], new_v[n] = Gb[n][None], d_[None], m_[None], v_[None]
    return (loss, grad_x, *[grads[n] for n in WEIGHTS], *[delta[n] for n in WEIGHTS],
            *[new_m[n] for n in WEIGHTS], *[new_v[n] for n in WEIGHTS])
```

```python
import functools

import jax
import jax.numpy as jnp
from jax import lax
from jax.experimental import pallas as pl
from jax.experimental.pallas import tpu as pltpu
from jax.experimental.pallas import tpu_sc as plsc

F32 = jnp.float32
BF16 = jnp.bfloat16
MESH = pl.DeviceIdType.MESH

D_MODEL = 1024
N_META = 16
EPS = 1e-6
SSD_HEADS = 16
SSD_HEAD_DIM = 64
SSD_INNER = 1024
SSD_GROUPS = 4
SSD_CONV = 4
HG_HEADS = 8
HG_SUB = 32
CHUNK = 128
D_FF = 2816
N_CHIPS = 4
IN_SIZES = (1024, 2048, 16, 1024, 1024, 1024, 1024, 1024, 1024)
ADAM_LR = 0.001
ADAM_B1 = 0.9
ADAM_B2 = 0.999
ADAM_EPS = 1e-08
ADAM_WD = 0.01
ADAM_STEP = 10
VMEM_LIMIT = 56 * 1024 * 1024
MATMUL_BLOCK_BYTES = 44 * 1024 * 1024
ADAMW_BLOCK_BYTES = 5 * 512 * 1024
MIN_MATMUL_STEPS = 8


def _cparams(sem=None):
    return pltpu.CompilerParams(dimension_semantics=sem, vmem_limit_bytes=VMEM_LIMIT)


def _pick(n, cands):
    for c in cands:
        if n % c == 0:
            return c
    return n


def _deps(after):
    xs = after if isinstance(after, (list, tuple)) else [after]
    one = lambda x: lax.slice(x, (0,) * x.ndim, (1,) * x.ndim).reshape(1).astype(F32)
    return jnp.concatenate([one(x) for x in xs]).reshape(1, -1)


def _dep_spec(dep):
    return pl.BlockSpec(dep.shape, lambda *_: (0, 0))


def _skip_ref(body, pos):
    return lambda *refs: body(*refs[:pos], *refs[pos + 1:])


def _dg(a, b, ca, cb):
    return lax.dot_general(a.astype(BF16), b.astype(BF16), (((ca,), (cb,)), ((), ())), preferred_element_type=F32)


@jax.custom_vjp
def _mm(a, b):
    return _dg(a, b, 1, 0)


def _mm_fwd(a, b):
    return _dg(a, b, 1, 0), (a, b)


def _mm_bwd(r, g):
    a, b = r
    return _dg(g, b, 1, 1), _dg(a, g, 0, 0)


_mm.defvjp(_mm_fwd, _mm_bwd)


@jax.custom_vjp
def _mm_nt(a, b):
    return _dg(a, b, 1, 1)


def _mm_nt_fwd(a, b):
    return _dg(a, b, 1, 1), (a, b)


def _mm_nt_bwd(r, g):
    a, b = r
    return _dg(g, b, 1, 0), _dg(g, a, 0, 0)


_mm_nt.defvjp(_mm_nt_fwd, _mm_nt_bwd)


@jax.custom_vjp
def _mm_tn(a, b):
    return _dg(a, b, 0, 0)


def _mm_tn_fwd(a, b):
    return _dg(a, b, 0, 0), (a, b)


def _mm_tn_bwd(r, g):
    a, b = r
    return _dg(b, g, 1, 1), _dg(a, g, 1, 0)


_mm_tn.defvjp(_mm_tn_fwd, _mm_tn_bwd)


def _tri_sum(x, lower):
    n = x.shape[0]
    ri = lax.broadcasted_iota(jnp.int32, (n, n), 0)
    ci = lax.broadcasted_iota(jnp.int32, (n, n), 1)
    tri = ((ri >= ci) if lower else (ri <= ci)).astype(BF16)
    x1 = x.astype(BF16)
    r1 = x - x1.astype(F32)
    x2 = r1.astype(BF16)
    x3 = (r1 - x2.astype(F32)).astype(BF16)
    dot = lambda p: lax.dot_general(tri, p, (((1,), (0,)), ((), ())), preferred_element_type=F32)
    return (dot(x3) + dot(x2)) + dot(x1)


@jax.custom_vjp
def _cumsum_rows(x):
    return _tri_sum(x, True)


_cumsum_rows.defvjp(lambda x: (_tri_sum(x, True), None), lambda _, g: (_tri_sum(g, False),))


def _silu(x):
    return x * jax.nn.sigmoid(x)


def _softplus(x):
    return jnp.maximum(x, 0.0) + jnp.log(1.0 + jnp.exp(-jnp.abs(x)))


def _tril(n):
    ri = lax.broadcasted_iota(jnp.int32, (n, n), 0)
    ci = lax.broadcasted_iota(jnp.int32, (n, n), 1)
    return ri >= ci


def _row_of(m, r):
    sub = lax.broadcasted_iota(jnp.int32, (m.shape[0], 1), 0)
    return jnp.sum(jnp.where(sub == r, m, 0.0), axis=0, keepdims=True)


def _col_of(m, c):
    lane = lax.broadcasted_iota(jnp.int32, (1, m.shape[1]), 1)
    return jnp.sum(jnp.where(lane == c, m, 0.0), axis=1, keepdims=True)


def _matmul(a, b, *, mode, out_dtype, name, alpha=1.0, res=None, tm=None, tn=None, out_groups=None, after=None):
    b3 = b.ndim == 3
    if mode == "nn":
        M, K = a.shape
        G = b.shape[0] if b3 else 1
        Ng = b.shape[-1]
        N = G * Ng
    elif mode == "nt":
        M, K = a.shape
        G = b.shape[0] if b3 else 1
        N = b.shape[-2]
        Kg = b.shape[-1]
        assert G * Kg == K
    else:
        K, M = a.shape
        N = b.shape[1]
        G = out_groups or 1
        Ng = N // G
    has_res = res is not None
    split_n = (mode == "nn" and b3) or (mode == "tn" and G > 1)
    per_mn = jnp.dtype(out_dtype).itemsize + (res.dtype.itemsize if has_res else 0)
    fits = [(m_ * n_, m_, n_)
            for m_ in (4352, 2176, 1408, 1088, 1024, 544, 512, 256, 128) if M % m_ == 0
            for n_ in (2816, 2048, 1408, 1024, 512, 256, 128) if (Ng if split_n else N) % n_ == 0
            if 2 * (K * m_ * a.dtype.itemsize + K * n_ * b.dtype.itemsize + m_ * n_ * per_mn) + 4 * m_ * n_ <= MATMUL_BLOCK_BYTES]
    if mode != "tn":
        fits = [f for f in fits if (M // f[1]) * (N // f[2]) >= MIN_MATMUL_STEPS] or fits
    _, tm_fit, tn_fit = max(fits)
    tm, tn = tm or tm_fit, tn or tn_fit
    nm, nn_ = M // tm, N // tn
    assert nm * tm == M and nn_ * tn == N, (name, M, N, K, tm, tn)

    if mode == "nn":
        a_spec = pl.BlockSpec((tm, K), lambda i, j: (i, 0))
        if b3:
            ns = Ng // tn
            b_spec = pl.BlockSpec((None, K, tn), lambda i, j: (j // ns, 0, j % ns))
        else:
            b_spec = pl.BlockSpec((K, tn), lambda i, j: (0, j))
        ca, cb = 1, 0
    elif mode == "nt":
        a_spec = pl.BlockSpec((tm, K), lambda i, j: (i, 0))
        if b3:
            b_spec = pl.BlockSpec((G, tn, Kg), lambda i, j: (0, j, 0))
        else:
            b_spec = pl.BlockSpec((tn, K), lambda i, j: (j, 0))
        ca, cb = 1, 1
    else:
        a_spec = pl.BlockSpec((K, tm), lambda i, j: (0, i))
        b_spec = pl.BlockSpec((K, tn), lambda i, j: (0, j))
        ca, cb = 0, 0
    if mode == "tn" and G > 1:
        ns = Ng // tn
        o_spec = pl.BlockSpec((None, tm, tn), lambda i, j: (j // ns, i, j % ns))
        out_shape = jax.ShapeDtypeStruct((G, M, Ng), out_dtype)
    else:
        o_spec = pl.BlockSpec((tm, tn), lambda i, j: (i, j))
        out_shape = jax.ShapeDtypeStruct((M, N), out_dtype)
    in_specs = [a_spec, b_spec]
    args = [a, b]
    if has_res:
        in_specs.append(pl.BlockSpec((tm, tn), lambda i, j: (i, j)))
        args.append(res)
    if after is not None:
        args.append(_deps(after))
        in_specs.append(_dep_spec(args[-1]))

    def body(*refs):
        a_ref, b_ref, o_ref = refs[0], refs[1], refs[-1]
        if mode == "nt" and b3:
            o = _dg(a_ref[:, 0:Kg], b_ref[0], ca, cb)
            for g in range(1, G):
                o = o + _dg(a_ref[:, g * Kg:(g + 1) * Kg], b_ref[g], ca, cb)
        else:
            o = _dg(a_ref[...], b_ref[...], ca, cb)
        if alpha != 1.0:
            o = o * alpha
        if has_res:
            o = o + refs[2][...]
        o_ref[...] = o.astype(o_ref.dtype)

    return pl.pallas_call(
        body, grid=(nm, nn_), in_specs=in_specs, out_specs=o_spec, out_shape=out_shape, name=name,
        compiler_params=_cparams(("parallel", "parallel")),
    )(*args)


def _sum_nt(xs, ws, name):
    R, N = xs[0].shape[0], ws[0].shape[0]
    n = len(xs)
    per_m = sum(x.shape[1] * x.dtype.itemsize for x in xs)
    per_n = sum(w.shape[1] * w.dtype.itemsize for w in ws)
    fits = [(m_ * n_, m_, n_) for m_ in (1088, 544, 256, 128) if R % m_ == 0 for n_ in (1024, 512, 256, 128) if N % n_ == 0
            if 2 * (m_ * per_m + n_ * per_n + m_ * n_ * 4) + 4 * m_ * n_ <= MATMUL_BLOCK_BYTES]
    _, tm, tn = max(fits)

    def body(*refs):
        o = _dg(refs[0][...], refs[n][...], 1, 1)
        for p in range(1, n):
            o = o + _dg(refs[p][...], refs[n + p][...], 1, 1)
        refs[-1][...] = o

    return pl.pallas_call(
        body, grid=(R // tm, N // tn),
        in_specs=[pl.BlockSpec((tm, x.shape[1]), lambda i, j: (i, 0)) for x in xs]
        + [pl.BlockSpec((tn, w.shape[1]), lambda i, j: (j, 0)) for w in ws],
        out_specs=pl.BlockSpec((tm, tn), lambda i, j: (i, j)), out_shape=jax.ShapeDtypeStruct((R, N), F32), name=name,
        compiler_params=_cparams(("parallel", "parallel")),
    )(*xs, *ws)


def _rms_fn(h, w):
    r = lax.rsqrt(jnp.mean(h * h, axis=-1, keepdims=True) + EPS)
    return h * r * w


def _swiglu_fn(gu):
    g = gu[:, :D_FF].astype(F32)
    u = gu[:, D_FF:].astype(F32)
    return _silu(g) * u


def _merge_fn(pa, pb, gates):
    return jax.nn.sigmoid(gates[:, :D_MODEL]) * pa + jax.nn.sigmoid(gates[:, D_MODEL:]) * pb


def _rows_call(body, *, rows, tr, ins, outs, accs=(), name, after=None):
    n = rows // tr
    assert n * tr == rows
    if after is not None:
        body = _skip_ref(body, len(ins))
        ins = list(ins) + [("full", _deps(after))]

    def spec(x):
        if isinstance(x, tuple):
            shp = x[1].shape
            return pl.BlockSpec(shp, lambda i: (0,) * len(shp))
        return pl.BlockSpec((tr, x.shape[1]), lambda i: (i, 0))

    in_specs = [spec(x) for x in ins]
    args = [x[1] if isinstance(x, tuple) else x for x in ins]
    out_specs = [spec(x) for x in outs] + [pl.BlockSpec(x.shape, lambda i: (0,) * len(x.shape)) for x in accs]
    out_shape = [x[1] if isinstance(x, tuple) else x for x in outs] + list(accs)
    return pl.pallas_call(
        body, grid=(n,), in_specs=in_specs, out_specs=out_specs, out_shape=out_shape, name=name,
        compiler_params=_cparams(("arbitrary",)),
    )(*args)


def _acc_rows(ref, val):
    @pl.when(pl.program_id(0) == 0)
    def _():
        ref[...] = jnp.zeros_like(ref)

    ref[0:1, :] += val


def _rms_fwd(h, w, name):
    def body(h_ref, w_ref, o_ref):
        o_ref[...] = _rms_fn(h_ref[...], w_ref[...]).astype(o_ref.dtype)

    R = h.shape[0]
    return _rows_call(body, rows=R, tr=_pick(R, (256, 128)), ins=[h, ("full", w)],
                      outs=[jax.ShapeDtypeStruct(h.shape, BF16)], name=name)[0]


def _rms_bwd(h, w, dn, dres, name, after=None):
    def body(h_ref, w_ref, dn_ref, dres_ref, dh_ref, dw_ref):
        _, vjp = jax.vjp(_rms_fn, h_ref[...], w_ref[...])
        dh, dw = vjp(dn_ref[...].astype(F32))
        dh_ref[...] = dh + dres_ref[...]
        _acc_rows(dw_ref, dw)

    R = h.shape[0]
    return _rows_call(body, rows=R, tr=_pick(R, (256, 128)), ins=[h, ("full", w), dn, dres],
                      outs=[jax.ShapeDtypeStruct(h.shape, F32)], accs=[jax.ShapeDtypeStruct((8, D_MODEL), F32)], name=name,
                      after=after)


def _d_norm_in(dgu, w_gu, h, norm_w, dres, name, after=None):
    R = h.shape[0]
    G, _, kg = w_gu.shape

    def body(dgu_ref, w_ref, h_ref, nw_ref, dres_ref, dh_ref, dw_ref):
        dn = _dg(dgu_ref[:, 0:kg], w_ref[0], 1, 1)
        for g in range(1, G):
            dn = dn + _dg(dgu_ref[:, kg * g:kg * (g + 1)], w_ref[g], 1, 1)
        _, vjp = jax.vjp(_rms_fn, h_ref[...], nw_ref[...])
        dh, dw = vjp(dn)
        dh_ref[...] = dh + dres_ref[...]
        _acc_rows(dw_ref, dw)

    return _rows_call(body, rows=R, tr=_pick(R, (256, 128)), ins=[dgu, ("full", w_gu), h, ("full", norm_w), dres],
                      outs=[jax.ShapeDtypeStruct(h.shape, F32)], accs=[jax.ShapeDtypeStruct((8, D_MODEL), F32)], name=name,
                      after=after)


def _rms_bwd_tokens(h, w, dn, dres, nseq, name, after=None):
    Tp = h.shape[0] // nseq
    nc = Tp // CHUNK

    def body(h_ref, w_ref, dn_ref, dres_ref, dx_ref, dm_ref, dw_ref):
        b, c = pl.program_id(0), pl.program_id(1)
        _, vjp = jax.vjp(_rms_fn, h_ref[...], w_ref[...])
        dh, dw = vjp(dn_ref[...].astype(F32))
        dh = dh + dres_ref[...]

        @pl.when(c == 0)
        def _():
            dm_ref[...] = dh

        @pl.when(c > 0)
        def _():
            dx_ref[...] = dh

        @pl.when((b == 0) & (c == 0))
        def _():
            dw_ref[...] = jnp.zeros_like(dw_ref)

        dw_ref[0:1, :] += dw

    rows = pl.BlockSpec((CHUNK, D_MODEL), lambda b, c: (b * nc + c, 0))
    in_specs, args = [rows, pl.BlockSpec((1, D_MODEL), lambda b, c: (0, 0)), rows, rows], [h, w, dn, dres]
    if after is not None:
        body = _skip_ref(body, len(args))
        args.append(_deps(after))
        in_specs.append(_dep_spec(args[-1]))
    return pl.pallas_call(
        body, grid=(nseq, nc), in_specs=in_specs,
        out_specs=[pl.BlockSpec((None, CHUNK, D_MODEL), lambda b, c: (b, jnp.maximum(c - 1, 0), 0)),
                   pl.BlockSpec((None, CHUNK, D_MODEL), lambda b, c: (b, 0, 0)),
                   pl.BlockSpec((8, D_MODEL), lambda b, c: (0, 0))],
        out_shape=[jax.ShapeDtypeStruct((nseq, Tp - CHUNK, D_MODEL), F32), jax.ShapeDtypeStruct((nseq, CHUNK, D_MODEL), F32),
                   jax.ShapeDtypeStruct((8, D_MODEL), F32)],
        name=name, compiler_params=_cparams(("arbitrary", "arbitrary")),
    )(*args)


def _gu_swiglu(n, w_gu, name):
    R = n.shape[0]
    G, _, ng = w_gu.shape

    def body(n_ref, w_ref, gu_ref, a_ref):
        x = n_ref[...]
        for r in range(G):
            gu_ref[:, ng * r:ng * (r + 1)] = _dg(x, w_ref[r], 1, 0).astype(gu_ref.dtype)
        a_ref[...] = _swiglu_fn(gu_ref[...]).astype(a_ref.dtype)

    return _rows_call(body, rows=R, tr=_pick(R, (256, 128)), ins=[n, ("full", w_gu)],
                      outs=[jax.ShapeDtypeStruct((R, 2 * D_FF), BF16), jax.ShapeDtypeStruct((R, D_FF), BF16)], name=name)


def _d_swiglu(dout, w_down, gu, alpha, name):
    R = gu.shape[0]

    def body(do_ref, w_ref, gu_ref, o_ref):
        da = _dg(do_ref[...] * alpha, w_ref[...], 1, 1)
        g = gu_ref[:, :D_FF].astype(F32)
        u = gu_ref[:, D_FF:].astype(F32)
        s = jax.nn.sigmoid(g)
        t = g * s
        o_ref[:, :D_FF] = (da * u * (s + t - t * s)).astype(o_ref.dtype)
        o_ref[:, D_FF:] = (da * t).astype(o_ref.dtype)

    return _rows_call(body, rows=R, tr=_pick(R, (256, 128)), ins=[dout, ("full", w_down), gu],
                      outs=[jax.ShapeDtypeStruct(gu.shape, BF16)], name=name)[0]


def _residual_matmul(a, w, res, alpha, name, norm_w=None):
    R, K = a.shape

    def body(a_ref, w_ref, r_ref, *rest):
        out = r_ref[...] + alpha * _dg(a_ref[...], w_ref[...], 1, 0)
        if norm_w is None:
            rest[0][...] = out
        else:
            rest[1][...] = out
            rest[2][...] = _rms_fn(out, rest[0][...]).astype(rest[2].dtype)

    f32 = jax.ShapeDtypeStruct((R, D_MODEL), F32)
    ins = [a, ("full", w), res] + ([] if norm_w is None else [("full", norm_w)])
    outs = [f32] + ([] if norm_w is None else [jax.ShapeDtypeStruct((R, D_MODEL), BF16)])
    got = _rows_call(body, rows=R, tr=_pick(R, (544, 256, 128)), ins=ins, outs=outs, name=name)
    return got[0] if norm_w is None else (got[0], got[1])


def _branch_merge(ya, yb, wa, wb, gates, name):
    def body(ya_ref, yb_ref, wa_ref, wb_ref, g_ref, pa_ref, pb_ref, o_ref):
        pa = _dg(ya_ref[...], wa_ref[...], 1, 0)
        pb = _dg(yb_ref[...], wb_ref[...], 1, 0)
        pa_ref[...] = pa
        pb_ref[...] = pb
        o_ref[...] = _merge_fn(pa, pb, g_ref[...].astype(F32)).astype(o_ref.dtype)

    R = ya.shape[0]
    f32 = jax.ShapeDtypeStruct((R, D_MODEL), F32)
    return _rows_call(body, rows=R, tr=_pick(R, (544, 256, 128)), ins=[ya, yb, ("full", wa), ("full", wb), gates],
                      outs=[f32, f32, jax.ShapeDtypeStruct((R, D_MODEL), BF16)], name=name)


def _branch_merge_bwd(pa, pb, gates, dm, wa, wb, name):
    def body(pa_ref, pb_ref, g_ref, dm_ref, wa_ref, wb_ref, dpa_ref, dpb_ref, dg_ref, dya_ref, dyb_ref):
        _, vjp = jax.vjp(_merge_fn, pa_ref[...], pb_ref[...], g_ref[...].astype(F32))
        dpa, dpb, dg = vjp(dm_ref[...].astype(F32))
        dpa_ref[...] = dpa.astype(dpa_ref.dtype)
        dpb_ref[...] = dpb.astype(dpb_ref.dtype)
        dg_ref[...] = dg.astype(dg_ref.dtype)
        dya_ref[...] = _dg(dpa, wa_ref[...], 1, 1).astype(dya_ref.dtype)
        dyb_ref[...] = _dg(dpb, wb_ref[...], 1, 1).astype(dyb_ref.dtype)

    R = pa.shape[0]
    b16 = jax.ShapeDtypeStruct(pa.shape, BF16)
    return _rows_call(body, rows=R, tr=_pick(R, (544, 256, 128)), ins=[pa, pb, gates, dm, ("full", wa), ("full", wb)],
                      outs=[b16, b16, jax.ShapeDtypeStruct(gates.shape, BF16), b16, b16], name=name)


def _loss_head(h3, w, target, nseq, name):
    Tp = h3.shape[0] // nseq
    nc = Tp // CHUNK

    def fn(h, w_, t, valid):
        y = _rms_fn(h, w_)
        e = (y - t) * valid
        return 0.5 * jnp.sum(jnp.mean(e * e, axis=-1, keepdims=True))

    def body(h_ref, w_ref, t_ref, loss_ref, dh_ref, dw_ref):
        b, c = pl.program_id(0), pl.program_id(1)
        valid = (c >= 1).astype(F32)
        t = t_ref[...]
        loss, vjp = jax.vjp(lambda h, w_: fn(h, w_, t, valid), h_ref[...], w_ref[...])
        dh, dw = vjp(jnp.ones((), F32))
        dh_ref[...] = dh

        @pl.when((b == 0) & (c == 0))
        def _():
            loss_ref[...] = jnp.zeros_like(loss_ref)
            dw_ref[...] = jnp.zeros_like(dw_ref)

        loss_ref[...] += jnp.full(loss_ref.shape, loss, F32)
        dw_ref[0:1, :] += dw

    return pl.pallas_call(
        body, grid=(nseq, nc),
        in_specs=[pl.BlockSpec((CHUNK, D_MODEL), lambda b, c: (b * nc + c, 0)),
                  pl.BlockSpec((1, D_MODEL), lambda b, c: (0, 0)),
                  pl.BlockSpec((None, CHUNK, D_MODEL), lambda b, c: (b, jnp.maximum(c - 1, 0), 0))],
        out_specs=[pl.BlockSpec((8, 128), lambda b, c: (0, 0)),
                   pl.BlockSpec((CHUNK, D_MODEL), lambda b, c: (b * nc + c, 0)),
                   pl.BlockSpec((8, D_MODEL), lambda b, c: (0, 0))],
        out_shape=[jax.ShapeDtypeStruct((8, 128), F32), jax.ShapeDtypeStruct(h3.shape, F32),
                   jax.ShapeDtypeStruct((8, D_MODEL), F32)],
        name=name, compiler_params=_cparams(("arbitrary", "arbitrary")),
    )(h3, w, target)


CONV_TILE = 512
CONV_HALO = 8


def _conv_fwd(xbc, w, b, pad, name):
    B, Tp, C = xbc.shape
    nch = Tp // CHUNK

    def body(x_ref, w_ref, b_ref, o_ref, xp):
        xp[0:CONV_HALO, :] = jnp.zeros((CONV_HALO, CONV_TILE), F32)
        xp[CONV_HALO:, :] = x_ref[...]
        for c in range(nch):
            acc = jnp.zeros((CHUNK, CONV_TILE), F32) + b_ref[...]
            for k in range(SSD_CONV):
                acc = acc + w_ref[k:k + 1, :] * xp[pl.ds(CONV_HALO + CHUNK * c - (SSD_CONV - 1) + k, CHUNK), :]
            out = _silu(acc)
            if CHUNK * c < pad:
                row = CHUNK * c + lax.broadcasted_iota(jnp.int32, (CHUNK, 1), 0)
                out = jnp.where(row >= pad, out, 0.0)
            o_ref[pl.ds(CHUNK * c, CHUNK), :] = out

    return pl.pallas_call(
        body, grid=(B, C // CONV_TILE),
        in_specs=[pl.BlockSpec((None, Tp, CONV_TILE), lambda i, j: (i, 0, j)),
                  pl.BlockSpec((SSD_CONV, CONV_TILE), lambda i, j: (0, j)),
                  pl.BlockSpec((1, CONV_TILE), lambda i, j: (0, j))],
        out_specs=pl.BlockSpec((None, Tp, CONV_TILE), lambda i, j: (i, 0, j)),
        out_shape=jax.ShapeDtypeStruct(xbc.shape, F32),
        scratch_shapes=[pltpu.VMEM((Tp + CONV_HALO, CONV_TILE), F32)],
        name=name, compiler_params=_cparams(("arbitrary", "arbitrary")),
    )(xbc, w, b)


def _conv_bwd(xbc, w, b, dact, pad, name):
    B, Tp, C = xbc.shape
    nch = Tp // CHUNK

    def body(x_ref, w_ref, b_ref, da_ref, dx_ref, dw_ref, db_ref, xp, dp):
        bi = pl.program_id(1)
        xp[0:CONV_HALO, :] = jnp.zeros((CONV_HALO, CONV_TILE), F32)
        xp[CONV_HALO:, :] = x_ref[...]
        dp[pl.ds(Tp, CONV_HALO), :] = jnp.zeros((CONV_HALO, CONV_TILE), F32)
        dws = [jnp.zeros((1, CONV_TILE), F32) for _ in range(SSD_CONV)]
        dbs = jnp.zeros((1, CONV_TILE), F32)
        for c in range(nch):
            xs = [xp[pl.ds(CONV_HALO + CHUNK * c - (SSD_CONV - 1) + k, CHUNK), :] for k in range(SSD_CONV)]
            acc = jnp.zeros((CHUNK, CONV_TILE), F32) + b_ref[...]
            for k in range(SSD_CONV):
                acc = acc + w_ref[k:k + 1, :] * xs[k]
            sg = jax.nn.sigmoid(acc)
            t = acc * sg
            dpre = da_ref[pl.ds(CHUNK * c, CHUNK), :] * (sg + t - t * sg)
            if CHUNK * c < pad:
                row = CHUNK * c + lax.broadcasted_iota(jnp.int32, (CHUNK, 1), 0)
                dpre = jnp.where(row >= pad, dpre, 0.0)
            dp[pl.ds(CHUNK * c, CHUNK), :] = dpre
            dbs = dbs + jnp.sum(dpre, axis=0, keepdims=True)
            for k in range(SSD_CONV):
                dws[k] = dws[k] + jnp.sum(dpre * xs[k], axis=0, keepdims=True)
        for c in range(nch):
            acc = jnp.zeros((CHUNK, CONV_TILE), F32)
            for k in range(SSD_CONV):
                acc = acc + w_ref[k:k + 1, :] * dp[pl.ds(CHUNK * c + (SSD_CONV - 1) - k, CHUNK), :]
            dx_ref[pl.ds(CHUNK * c, CHUNK), :] = acc.astype(dx_ref.dtype)

        @pl.when(bi == 0)
        def _():
            dw_ref[...] = jnp.zeros_like(dw_ref)
            db_ref[...] = jnp.zeros_like(db_ref)

        for k in range(SSD_CONV):
            dw_ref[k:k + 1, :] += dws[k]
        db_ref[0:1, :] += dbs

    return pl.pallas_call(
        body, grid=(C // CONV_TILE, B),
        in_specs=[pl.BlockSpec((None, Tp, CONV_TILE), lambda j, i: (i, 0, j)),
                  pl.BlockSpec((SSD_CONV, CONV_TILE), lambda j, i: (0, j)),
                  pl.BlockSpec((1, CONV_TILE), lambda j, i: (0, j)),
                  pl.BlockSpec((None, Tp, CONV_TILE), lambda j, i: (i, 0, j))],
        out_specs=[pl.BlockSpec((None, Tp, CONV_TILE), lambda j, i: (i, 0, j)),
                   pl.BlockSpec((8, CONV_TILE), lambda j, i: (0, j)),
                   pl.BlockSpec((8, CONV_TILE), lambda j, i: (0, j))],
        out_shape=[jax.ShapeDtypeStruct(xbc.shape, BF16), jax.ShapeDtypeStruct((8, C), F32),
                   jax.ShapeDtypeStruct((8, C), F32)],
        scratch_shapes=[pltpu.VMEM((Tp + CONV_HALO, CONV_TILE), F32), pltpu.VMEM((Tp + CONV_HALO, CONV_TILE), F32)],
        name=name, compiler_params=_cparams(("arbitrary", "arbitrary")),
    )(xbc, w, b, dact)


def _ssd_chunk(xs, bm, cm, dtr, z, state, dt_bias, a_log, dskip, norm_w, valid, kept=None, keep=False):
    Q = xs.shape[0]
    known = (lambda x, v: x) if kept is None else _known
    lane = lax.broadcasted_iota(jnp.int32, (1, 128), 1)
    dt = jnp.where(lane < SSD_HEADS, _softplus(dtr + dt_bias), 0.0) * valid
    a = dt * (-jnp.exp(a_log))
    tril = _tril(Q)
    cs = known(_cumsum_rows(a), None if kept is None else kept[0])
    cs_t = cs.T
    cs_end = _row_of(cs, Q - 1)
    low = lane < SSD_HEAD_DIM
    low_rows = lax.broadcasted_iota(jnp.int32, (128, 1), 0) < SSD_HEAD_DIM
    ys, new_state, cbs = [], [], []
    for g in range(SSD_GROUPS):
        bg = bm[:, 128 * g:128 * (g + 1)]
        cg = cm[:, 128 * g:128 * (g + 1)]
        cb = known(_mm_nt(cg, bg), None if kept is None else kept[1][Q * g:Q * (g + 1)])
        cbs.append(cb)
        for pr in range(2):
            p = 2 * g + pr
            h0, h1 = 2 * p, 2 * p + 1
            xp = xs[:, 128 * p:128 * (p + 1)]
            c0, c1 = _col_of(cs, h0), _col_of(cs, h1)
            e0, e1 = _col_of(cs_end, h0), _col_of(cs_end, h1)
            xd = xp * jnp.where(low, _col_of(dt, h0), _col_of(dt, h1))
            l0 = jnp.exp(jnp.where(tril, c0 - _row_of(cs_t, h0), -1e30))
            l1 = jnp.exp(jnp.where(tril, c1 - _row_of(cs_t, h1), -1e30))
            y_diag = jnp.where(low, _mm(cb * l0, xd), _mm(cb * l1, xd))
            to_end = jnp.where(low, jnp.exp(e0 - c0), jnp.exp(e1 - c1))
            sp = state[128 * p:128 * (p + 1), :]
            y_off = _mm_nt(cg, sp) * jnp.where(low, jnp.exp(c0), jnp.exp(c1))
            new_state.append(sp * jnp.where(low_rows, jnp.exp(e0), jnp.exp(e1)) + _mm_tn(xd * to_end, bg))
            ys.append(y_diag + y_off + xp * jnp.where(low, _col_of(dskip, h0), _col_of(dskip, h1)))
    y_raw = known(jnp.concatenate(ys, axis=1), None if kept is None else kept[2])
    y = y_raw * _silu(z)
    gw = SSD_INNER // SSD_GROUPS
    outs = []
    for g in range(SSD_GROUPS):
        blk = y[:, gw * g:gw * (g + 1)]
        outs.append(blk * lax.rsqrt(jnp.mean(blk * blk, axis=-1, keepdims=True) + EPS))
    out, state_out = jnp.concatenate(outs, axis=1) * norm_w, jnp.concatenate(new_state, axis=0)
    if kept is not None:
        state_out = _known(state_out, state)
    return (out, state_out, (cs, jnp.concatenate(cbs, axis=0), y_raw)) if keep else (out, state_out)


def _valid_rows(c, pad):
    row = c * CHUNK + lax.broadcasted_iota(jnp.int32, (CHUNK, 1), 0)
    return (row >= pad).astype(F32)


def _ssd_fwd(xact, dtr, z, dt_bias, a_log, dskip, norm_w, pad, name):
    B, Tp, _ = xact.shape
    nc = Tp // CHUNK

    def body(xs_ref, bm_ref, cm_ref, dt_ref, z_ref, db_ref, al_ref, ds_ref, nw_ref, y_ref, save_ref, cs_ref, cb_ref, yr_ref, st):
        c = pl.program_id(1)

        @pl.when(c == 0)
        def _():
            st[...] = jnp.zeros_like(st)

        s0 = st[...]
        save_ref[...] = s0
        y, s1, (cs, cb, y_raw) = _ssd_chunk(xs_ref[...], bm_ref[...], cm_ref[...], dt_ref[...], z_ref[...].astype(F32), s0,
                                            db_ref[...], al_ref[...], ds_ref[...], nw_ref[...], _valid_rows(c, pad), keep=True)
        y_ref[...] = y.astype(y_ref.dtype)
        cs_ref[...] = cs
        cb_ref[...] = cb
        yr_ref[...] = y_raw
        st[...] = s1

    row = lambda w, off=0: pl.BlockSpec((None, CHUNK, w), lambda b, c: (b, c, off))
    par = lambda w: pl.BlockSpec((1, w), lambda b, c: (0, 0))
    per_chunk = lambda r: pl.BlockSpec((None, None, r, 128), lambda b, c: (b, c, 0, 0))
    return pl.pallas_call(
        body, grid=(B, nc),
        in_specs=[row(1024, 0), row(512, 2), row(512, 3), row(128), row(1024), par(128), par(128), par(128), par(1024)],
        out_specs=[row(1024), per_chunk(1024), row(128), per_chunk(SSD_GROUPS * CHUNK), row(1024)],
        out_shape=[jax.ShapeDtypeStruct((B, Tp, SSD_INNER), BF16), jax.ShapeDtypeStruct((B, nc, 1024, 128), F32),
                   jax.ShapeDtypeStruct((B, Tp, 128), F32), jax.ShapeDtypeStruct((B, nc, SSD_GROUPS * CHUNK, 128), F32),
                   jax.ShapeDtypeStruct((B, Tp, SSD_INNER), F32)],
        scratch_shapes=[pltpu.VMEM((1024, 128), F32)],
        name=name, compiler_params=_cparams(("arbitrary", "arbitrary")),
    )(xact, xact, xact, dtr, z, dt_bias, a_log, dskip, norm_w)


def _ssd_bwd(xact, dtr, z, dt_bias, a_log, dskip, norm_w, saved, kept, dy, pad, name, after=None):
    B, Tp, _ = xact.shape
    nc = Tp // CHUNK

    def body(xs_ref, bm_ref, cm_ref, dt_ref, z_ref, db_ref, al_ref, ds_ref, nw_ref, sv_ref, cs_ref, cb_ref, yr_ref, dy_ref,
             dx_ref, ddt_ref, dz_ref, dpar_ref, dnw_ref, dst):
        b, i = pl.program_id(0), pl.program_id(1)
        c = nc - 1 - i

        @pl.when(i == 0)
        def _():
            dst[...] = jnp.zeros_like(dst)

        valid = _valid_rows(c, pad)
        kept_c = (cs_ref[...], cb_ref[...], yr_ref[...])
        fn = lambda *a: _ssd_chunk(*a, valid, kept=kept_c)
        _, vjp = jax.vjp(fn, xs_ref[...], bm_ref[...], cm_ref[...], dt_ref[...], z_ref[...].astype(F32), sv_ref[...],
                         db_ref[...], al_ref[...], ds_ref[...], nw_ref[...])
        dxs, dbm, dcm, ddt, dz, dstate, ddb, dal, dds, dnw = vjp((dy_ref[...].astype(F32), dst[...]))
        dx_ref[:, 0:1024] = dxs
        dx_ref[:, 1024:1536] = dbm
        dx_ref[:, 1536:2048] = dcm
        ddt_ref[...] = ddt
        dz_ref[...] = dz.astype(dz_ref.dtype)
        dst[...] = dstate

        @pl.when((b == 0) & (i == 0))
        def _():
            dpar_ref[...] = jnp.zeros_like(dpar_ref)
            dnw_ref[...] = jnp.zeros_like(dnw_ref)

        dpar_ref[0:1, :] += ddb
        dpar_ref[1:2, :] += dal
        dpar_ref[2:3, :] += dds
        dnw_ref[0:1, :] += dnw

    row = lambda w, off=0: pl.BlockSpec((None, CHUNK, w), lambda b, i: (b, nc - 1 - i, off))
    par = lambda w: pl.BlockSpec((1, w), lambda b, i: (0, 0))
    acc = lambda w: pl.BlockSpec((8, w), lambda b, i: (0, 0))
    per_chunk = lambda r: pl.BlockSpec((None, None, r, 128), lambda b, i: (b, nc - 1 - i, 0, 0))
    in_specs = [row(1024, 0), row(512, 2), row(512, 3), row(128), row(1024), par(128), par(128), par(128), par(1024),
                per_chunk(1024), row(128), per_chunk(SSD_GROUPS * CHUNK), row(1024), row(1024)]
    args = [xact, xact, xact, dtr, z, dt_bias, a_log, dskip, norm_w, saved, kept[0], kept[1], kept[2], dy]
    if after is not None:
        body = _skip_ref(body, len(args))
        args.append(_deps(after))
        in_specs.append(_dep_spec(args[-1]))
    outs = pl.pallas_call(
        body, grid=(B, nc), in_specs=in_specs,
        out_specs=[row(2048), row(128), row(1024), acc(128), acc(1024)],
        out_shape=[jax.ShapeDtypeStruct((B, Tp, 2048), F32), jax.ShapeDtypeStruct((B, Tp, 128), F32),
                   jax.ShapeDtypeStruct((B, Tp, 1024), BF16), jax.ShapeDtypeStruct((8, 128), F32),
                   jax.ShapeDtypeStruct((8, 1024), F32)],
        scratch_shapes=[pltpu.VMEM((1024, 128), F32)],
        name=name, compiler_params=_cparams(("arbitrary", "arbitrary")),
    )(*args)
    return outs


@jax.custom_vjp
def _known(x, value):
    return value


_known.defvjp(lambda x, value: (value, None), lambda _, g: (g, jnp.zeros_like(g)))


def _hg_chunk(qr, fr, ir, gr, state_t, p0, p1, norm_w, valid, kept=None, keep=False):
    Q = qr.shape[0]
    known = (lambda x, i: x) if kept is None else (lambda x, i: _known(x, kept[i].astype(x.dtype)))
    lb = jax.nn.sigmoid(p0 - p1)
    f = lb + (1.0 - lb) * jax.nn.sigmoid(fr)
    k = 1.0 - f
    q = _silu(qr)
    v = ir * valid
    cum = known(_cumsum_rows(jnp.log(f)), 0)
    cum_end = _row_of(cum, Q - 1)
    o_inter = _mm_nt(q * jnp.exp(cum), state_t)
    nblk = Q // HG_SUB
    row = lax.broadcasted_iota(jnp.int32, (Q, 1), 0)
    ri = lax.broadcasted_iota(jnp.int32, (Q, Q), 0)
    ci = lax.broadcasted_iota(jnp.int32, (Q, Q), 1)
    mids = jnp.concatenate([jnp.broadcast_to(_row_of(cum, HG_SUB * i + HG_SUB // 2 - 1), (HG_SUB, cum.shape[1]))
                            for i in range(nblk)], axis=0)
    sh = HG_SUB.bit_length() - 1
    same = (jnp.right_shift(ri, sh) == jnp.right_shift(ci, sh)) & (ri >= ci)
    att = jnp.where(same, _mm_nt(q * jnp.exp(cum - mids), k * jnp.exp(mids - cum)), 0.0)
    qas, kas = [], []
    for i in range(1, nblk):
        lo = HG_SUB * i
        start = _row_of(cum, lo - 1)
        qas.append(q * jnp.exp(jnp.where((row >= lo) & (row < lo + HG_SUB), cum - start, -1e30)))
        kas.append(k * jnp.exp(jnp.where(row < lo, start - cum, -1e30)))
    att = att + _mm_nt(jnp.concatenate(qas, axis=1), jnp.concatenate(kas, axis=1))
    att = known(att, 1)
    o = known(o_inter + _mm(att, v), 2)
    new_state_t = state_t * jnp.exp(cum_end) + _mm_tn(v, k * jnp.exp(cum_end - cum))
    if kept is not None:
        new_state_t = _known(new_state_t, state_t)
    y = o * lax.rsqrt(jnp.mean(o * o, axis=-1, keepdims=True) + EPS) * norm_w * _silu(gr)
    return (y, new_state_t, (cum, att, o)) if keep else (y, new_state_t)


HG_PER_STEP = 8
HG_COLS = 4 * 128


def _hg_fwd(qfig, lbh, nwh, pad, name):
    B, Tp, _ = qfig.shape
    nc = Tp // CHUNK
    hp = HG_PER_STEP

    def body(x_ref, lb_ref, nw_ref, y_ref, save_ref, cum_ref, att_ref, o_ref, st):
        c = pl.program_id(1)

        @pl.when(c == 0)
        def _():
            st[...] = jnp.zeros_like(st)

        valid = _valid_rows(c, pad)
        for j in range(hp):
            for b in range(B):
                s0 = st[j, b]
                save_ref[j, b] = s0
                col = lambda k: x_ref[b, :, HG_COLS * j + 128 * k:HG_COLS * j + 128 * (k + 1)]
                y, s1, (cum, att, o) = _hg_chunk(col(0), col(1), col(2), col(3), s0, lb_ref[j, 0:1, :], lb_ref[j, 1:2, :],
                                                 nw_ref[j], valid, keep=True)
                y_ref[b, :, 128 * j:128 * (j + 1)] = y.astype(y_ref.dtype)
                cum_ref[b, :, 128 * j:128 * (j + 1)] = cum
                att_ref[j, b] = att.astype(att_ref.dtype)
                o_ref[b, :, 128 * j:128 * (j + 1)] = o
                st[j, b] = s1

    rows = pl.BlockSpec((B, CHUNK, 128 * hp), lambda h, c: (0, c, h))
    per_chunk = pl.BlockSpec((hp, B, None, 128, 128), lambda h, c: (h, 0, c, 0, 0))
    return pl.pallas_call(
        body, grid=(HG_HEADS // hp, nc),
        in_specs=[pl.BlockSpec((B, CHUNK, HG_COLS * hp), lambda h, c: (0, c, h)),
                  pl.BlockSpec((hp, 2, 128), lambda h, c: (h, 0, 0)),
                  pl.BlockSpec((hp, 1, 128), lambda h, c: (h, 0, 0))],
        out_specs=[rows, per_chunk, rows, per_chunk, rows],
        out_shape=[jax.ShapeDtypeStruct((B, Tp, 1024), BF16), jax.ShapeDtypeStruct((HG_HEADS, B, nc, 128, 128), F32),
                   jax.ShapeDtypeStruct((B, Tp, 1024), F32), jax.ShapeDtypeStruct((HG_HEADS, B, nc, 128, 128), BF16),
                   jax.ShapeDtypeStruct((B, Tp, 1024), F32)],
        scratch_shapes=[pltpu.VMEM((hp, B, 128, 128), F32)],
        name=name, compiler_params=_cparams(("arbitrary", "arbitrary")),
    )(qfig, lbh, nwh)


def _hg_bwd(qfig, lbh, nwh, saved, kept, dy, pad, name, after=None):
    B, Tp, _ = qfig.shape
    nc = Tp // CHUNK
    hp = HG_PER_STEP

    def body(x_ref, lb_ref, nw_ref, sv_ref, cum_ref, att_ref, o_ref, dy_ref, dx_ref, dlb_ref, dnw_ref, dst):
        i = pl.program_id(1)
        c = nc - 1 - i

        @pl.when(i == 0)
        def _():
            dst[...] = jnp.zeros_like(dst)
            dlb_ref[...] = jnp.zeros_like(dlb_ref)
            dnw_ref[...] = jnp.zeros_like(dnw_ref)

        valid = _valid_rows(c, pad)
        for j in range(hp):
            for b in range(B):
                col = lambda k: x_ref[b, :, HG_COLS * j + 128 * k:HG_COLS * j + 128 * (k + 1)]
                head = slice(128 * j, 128 * (j + 1))
                kept_jb = (cum_ref[b, :, head], att_ref[j, b], o_ref[b, :, head])
                fn = lambda *a: _hg_chunk(*a, valid, kept=kept_jb)
                _, vjp = jax.vjp(fn, col(0), col(1), col(2), col(3), sv_ref[j, b], lb_ref[j, 0:1, :], lb_ref[j, 1:2, :], nw_ref[j])
                d4 = vjp((dy_ref[b, :, 128 * j:128 * (j + 1)].astype(F32), dst[j, b]))
                for k in range(4):
                    dx_ref[b, :, HG_COLS * j + 128 * k:HG_COLS * j + 128 * (k + 1)] = d4[k].astype(dx_ref.dtype)
                dst[j, b] = d4[4]
                dlb_ref[j, 0:1, :] += d4[5]
                dlb_ref[j, 1:2, :] += d4[6]
                dnw_ref[j, 0:1, :] += d4[7]

    acc = pl.BlockSpec((hp, 8, 128), lambda h, i: (h, 0, 0))
    rows = pl.BlockSpec((B, CHUNK, 128 * hp), lambda h, i: (0, nc - 1 - i, h))
    per_chunk = pl.BlockSpec((hp, B, None, 128, 128), lambda h, i: (h, 0, nc - 1 - i, 0, 0))
    in_specs = [pl.BlockSpec((B, CHUNK, HG_COLS * hp), lambda h, i: (0, nc - 1 - i, h)),
                pl.BlockSpec((hp, 2, 128), lambda h, i: (h, 0, 0)),
                pl.BlockSpec((hp, 1, 128), lambda h, i: (h, 0, 0)),
                per_chunk, rows, per_chunk, rows, rows]
    args = [qfig, lbh, nwh, saved, kept[0], kept[1], kept[2], dy]
    if after is not None:
        body = _skip_ref(body, len(args))
        args.append(_deps(after))
        in_specs.append(_dep_spec(args[-1]))
    return pl.pallas_call(
        body, grid=(HG_HEADS // hp, nc), in_specs=in_specs,
        out_specs=[pl.BlockSpec((B, CHUNK, HG_COLS * hp), lambda h, i: (0, nc - 1 - i, h)), acc, acc],
        out_shape=[jax.ShapeDtypeStruct((B, Tp, 4096), BF16), jax.ShapeDtypeStruct((HG_HEADS, 8, 128), F32),
                   jax.ShapeDtypeStruct((HG_HEADS, 8, 128), F32)],
        scratch_shapes=[pltpu.VMEM((hp, B, 128, 128), F32)],
        name=name, compiler_params=_cparams(("arbitrary", "arbitrary")),
    )(*args)


def _adamw_math(w, g, m, v):
    m = ADAM_B1 * m + (1.0 - ADAM_B1) * g
    v = ADAM_B2 * v + (1.0 - ADAM_B2) * (g * g)
    m_hat = m / (1.0 - ADAM_B1 ** ADAM_STEP)
    v_hat = v / (1.0 - ADAM_B2 ** ADAM_STEP)
    return -ADAM_LR * (m_hat / (jnp.sqrt(v_hat) + ADAM_EPS) + ADAM_WD * w), m, v


def _adamw_many(ws, gs, ms, vs, name):
    n = len(ws)

    def body(*refs):
        for i in range(n):
            d, m, v = _adamw_math(refs[i][...], refs[n + i][...], refs[2 * n + i][...], refs[3 * n + i][...])
            refs[4 * n + i][...] = d
            refs[5 * n + i][...] = m
            refs[6 * n + i][...] = v

    vm = pl.BlockSpec(memory_space=pltpu.VMEM)
    outs = pl.pallas_call(body, in_specs=[vm] * (4 * n), out_specs=[vm] * (3 * n),
                          out_shape=[jax.ShapeDtypeStruct(w.shape, F32) for w in ws] * 3, name=name)(*ws, *gs, *ms, *vs)
    return outs[:n], outs[n:2 * n], outs[2 * n:]


def _adamw(w, g, m, v, name, after=None):
    R, C = w.shape
    tr = max(t for t in range(8, R + 1, 8) if R % t == 0 and (t * C * 4 <= ADAMW_BLOCK_BYTES or t == 8))

    def body(w_ref, g_ref, m_ref, v_ref, d_ref, mo_ref, vo_ref):
        d_ref[...], mo_ref[...], vo_ref[...] = _adamw_math(w_ref[...], g_ref[...], m_ref[...], v_ref[...])

    sp = pl.BlockSpec((tr, C), lambda i: (i, 0))
    sh = jax.ShapeDtypeStruct((R, C), F32)
    in_specs, args = [sp] * 4, [w, g, m, v]
    if after is not None:
        body = _skip_ref(body, len(args))
        args.append(_deps(after))
        in_specs.append(_dep_spec(args[-1]))
    return pl.pallas_call(body, grid=(R // tr,), in_specs=in_specs, out_specs=[sp] * 3, out_shape=[sh] * 3,
                          name=name, compiler_params=_cparams(("arbitrary",)))(*args)


def _ffn_fwd(h, norm_w, w_gu, w_down, tag, after_norm=None, n=None, next_norm_w=None):
    if n is None:
        n = _rms_fwd(h, norm_w, f"{tag}_norm")
    if after_norm is not None:
        after_norm(n)
    gu, a = _gu_swiglu(n, w_gu, f"{tag}_gu")
    out = _residual_matmul(a, w_down, h, 0.5, f"{tag}_down", next_norm_w)
    return out, (n, gu, a)


def _ffn_bwd(h, norm_w, w_gu, w_down, saved, dout, tag, after_dw_down=None, token_seqs=None, told=None):
    n, gu, a = saved
    dgu = _d_swiglu(dout, w_down, gu, 0.5, f"{tag}_d_gu")
    dw_down = _matmul(a, dout, mode="tn", out_dtype=F32, alpha=0.5, name=f"{tag}_dw_down")
    dw_gu = _matmul(n, dgu, mode="tn", out_dtype=F32, out_groups=N_CHIPS, name=f"{tag}_dw_gu",
                    after=after_dw_down(dw_down) if after_dw_down else None)
    if token_seqs is None:
        dh, dnw = _d_norm_in(dgu, w_gu, h, norm_w, dout, f"{tag}_d_in", after=dw_gu)
    else:
        if told is not None:
            told("dw", (dw_gu, dw_down))
        dn = _matmul(dgu, w_gu, mode="nt", out_dtype=F32, name=f"{tag}_d_norm", after=dw_gu)
        dx, dm, dnw = _rms_bwd_tokens(h, norm_w, dn, dout, token_seqs, f"{tag}_d_in",
                                      after=told("d_norm", dn) if told is not None else None)
        dh = (dx, dm)
    return dh, dnw, dw_gu, dw_down


def _split_w_in(w_in_full):
    pts = [0]
    for s in IN_SIZES:
        pts.append(pts[-1] + s)
    sl = lambda i, j: w_in_full[:, pts[i]:pts[j]]
    qfig = sl(3, 7).reshape(D_MODEL, 4, HG_HEADS, 128).transpose(0, 2, 1, 3).reshape(D_MODEL, 4 * D_MODEL)
    return {"z": sl(0, 1), "xbc": sl(1, 2), "dt": jnp.pad(sl(2, 3), ((0, 0), (0, 128 - SSD_HEADS))),
            "qfig": qfig, "gates": sl(7, 9)}


def _local_step(x, target, W):
    B, S, _ = x.shape
    T = N_META + S
    pad = (-T) % CHUNK
    Tp = T + pad
    assert pad + N_META == CHUNK
    R = B * Tp
    meta = jnp.broadcast_to(W["meta_tokens"][None], (B, N_META, D_MODEL))
    h0 = jnp.concatenate([jnp.zeros((B, pad, D_MODEL), F32), meta, x], axis=1).reshape(R, D_MODEL)

    stage = W.get("_stage", lambda name, x: {})
    W = dict(W)
    (h1, um), sv1 = _ffn_fwd(h0, W["ffn1_norm"], W["ffn1_w_gu"], W["ffn1_w_down"], "ffn1",
                             lambda n: W.update(stage("ffn1_norm", n)), next_norm_w=W["mix_norm"])
    W.update(stage("ffn1_out", h1))
    wi = W["w_in"]
    z = _matmul(um, wi["z"], mode="nn", out_dtype=BF16, name="in_z")
    xbc = _matmul(um, wi["xbc"], mode="nn", out_dtype=F32, name="in_xbc")
    dtr = _matmul(um, wi["dt"], mode="nn", out_dtype=F32, name="in_dt")
    qfig = _matmul(um, wi["qfig"], mode="nn", out_dtype=F32, name="in_qfig")
    gates = _matmul(um, wi["gates"], mode="nn", out_dtype=BF16, name="in_gates")

    r3 = lambda t: t.reshape(B, Tp, t.shape[-1])
    lane_pad = lambda t: jnp.pad(t, ((0, 0), (0, 128 - t.shape[1])))
    dt_bias, a_log, dskip = lane_pad(W["ssd_dt_bias"]), lane_pad(W["ssd_a_log"]), lane_pad(W["ssd_d"])
    xact = _conv_fwd(r3(xbc), W["ssd_conv_w"], W["ssd_conv_b"], pad, "conv_fwd")
    ya, ssd_saved, *ssd_kept = _ssd_fwd(xact, r3(dtr), r3(z), dt_bias, a_log, dskip, W["ssd_norm"], pad, "ssd_fwd")
    lbh = W["hg_lower_bound"].reshape(2, HG_HEADS, 128).transpose(1, 0, 2)
    nwh = W["hg_norm"].reshape(HG_HEADS, 1, 128)
    yb, hg_saved, *hg_kept = _hg_fwd(r3(qfig), lbh, nwh, pad, "hg_fwd")
    ya2, yb2 = ya.reshape(R, -1), yb.reshape(R, -1)
    W.update(stage("mixers_out", yb2))
    pa, pb, mg = _branch_merge(ya2, yb2, W["w_branch_a"], W["w_branch_b"], gates, "branch_merge")
    h2, n2 = _residual_matmul(mg, W["w_out"], h1, 1.0, "mix_out", W["ffn2_norm"])
    h3, sv2 = _ffn_fwd(h2, W["ffn2_norm"], W["ffn2_w_gu"], W["ffn2_w_down"], "ffn2", n=n2)

    loss, dh3, d_final = _loss_head(h3, W["final_norm"].reshape(1, D_MODEL), target, B, "loss_head")

    G = {"final_norm": d_final[0]}
    dh2, dnw, G["ffn2_w_gu"], G["ffn2_w_down"] = _ffn_bwd(h2, W["ffn2_norm"], W["ffn2_w_gu"], W["ffn2_w_down"], sv2, dh3, "ffn2")
    G["ffn2_norm"] = dnw[0:1]
    dmg = _matmul(dh2, W["w_out"], mode="nt", out_dtype=BF16, name="d_merge")
    G["w_out"] = _matmul(mg, dh2, mode="tn", out_dtype=F32, name="dw_out")
    dpa, dpb, dgates, dya, dyb = _branch_merge_bwd(pa, pb, gates, dmg, W["w_branch_a"], W["w_branch_b"], "branch_merge_bwd")
    G["w_branch_a"] = _matmul(ya2, dpa, mode="tn", out_dtype=F32, name="dw_branch_a")
    G["w_branch_b"] = _matmul(yb2, dpb, mode="tn", out_dtype=F32, name="dw_branch_b")

    dxact, ddtr, dz, dpar, dnw = _ssd_bwd(xact, r3(dtr), r3(z), dt_bias, a_log, dskip, W["ssd_norm"], ssd_saved, ssd_kept,
                                          r3(dya), pad, "ssd_bwd", after=stage("late_grads", G).get("_after"))
    G["ssd_dt_bias"], G["ssd_a_log"], G["ssd_d"] = dpar[0:1, :SSD_HEADS], dpar[1:2, :SSD_HEADS], dpar[2:3, :SSD_HEADS]
    G["ssd_norm"] = dnw[0:1]
    dxbc, dcw, dcb = _conv_bwd(r3(xbc), W["ssd_conv_w"], W["ssd_conv_b"], dxact, pad, "conv_bwd")
    G["ssd_conv_w"], G["ssd_conv_b"] = dcw[0:SSD_CONV], dcb[0:1]
    dqfig, dlb, dhn = _hg_bwd(r3(qfig), lbh, nwh, hg_saved, hg_kept, r3(dyb), pad, "hg_bwd",
                              after=stage("after_conv_bwd", dcb).get("_after"))
    G["hg_lower_bound"] = dlb[:, 0:2, :].transpose(1, 0, 2).reshape(2, D_MODEL)
    G["hg_norm"] = dhn[:, 0, :].reshape(1, D_MODEL)

    r2 = lambda t: t.reshape(R, t.shape[-1])
    pieces = [("z", r2(dz)), ("xbc", r2(dxbc)), ("dt", r2(ddtr)), ("qfig", r2(dqfig)), ("gates", dgates)]
    dum = _sum_nt([p for _, p in pieces], [wi[nm] for nm, _ in pieces], "d_mix")
    dwi = {nm: _matmul(um, dpiece, mode="tn", out_dtype=F32, name=f"dw_in_{nm}") for nm, dpiece in pieces}
    dw_qfig = dwi["qfig"].reshape(D_MODEL, HG_HEADS, 4, 128).transpose(0, 2, 1, 3).reshape(D_MODEL, 4 * D_MODEL)
    G["w_in"] = jnp.concatenate([dwi["z"], dwi["xbc"], dwi["dt"][:, :SSD_HEADS], dw_qfig, dwi["gates"]], axis=1)
    dh1, dnw = _rms_bwd(h1, W["mix_norm"], dum, dh2, "mix_norm_bwd", after=stage("w_in_grads", dwi).get("_after"))
    G["mix_norm"] = dnw[0:1]
    (dx, dfirst), dnw, G["ffn1_w_gu"], G["ffn1_w_down"] = _ffn_bwd(
        h0, W["ffn1_norm"], W["ffn1_w_gu"], W["ffn1_w_down"], sv1, dh1, "ffn1",
        lambda dw: stage("ffn1_dw_down", dw).get("_after"), token_seqs=B,
        told=lambda name, t: stage("ffn1_" + name, t).get("_after"))
    G["ffn1_norm"] = dnw[0:1]
    G["meta_tokens"] = jnp.sum(dfirst[:, pad:CHUNK], axis=0)
    return loss, dx, G


def _place():
    return lax.axis_index("x"), lax.axis_index("y"), lax.axis_index("c")


def _other_chips(x, y):
    return [(1 - x, y), (x, 1 - y), (1 - x, 1 - y)]


def _remote(src, dst, ssem, rsem, dev):
    return pltpu.make_async_remote_copy(src_ref=src, dst_ref=dst, send_sem=ssem, recv_sem=rsem,
                                        device_id=dev, device_id_type=MESH)


def _exchange8(buf, name):
    n, w = buf.shape

    def body(x_ref, out_ref, ssem, rsem):
        x, y, c = _place()
        me = 4 * x + 2 * y + c
        out_ref[me] = x_ref[...]
        copies = []
        for k in range(1, 8):
            px = 1 - x if (k >> 2) & 1 else x
            py = 1 - y if (k >> 1) & 1 else y
            pc = 1 - c if k & 1 else c
            cp = _remote(x_ref, out_ref.at[me], ssem.at[k - 1], rsem.at[k - 1], (px, py, pc))
            cp.start()
            copies.append((cp, 4 * px + 2 * py + pc))
        for k, (cp, peer) in enumerate(copies):
            _remote(x_ref, out_ref.at[peer], ssem.at[k], rsem.at[k], (x, y, c)).wait_recv()
        for cp, _ in copies:
            cp.wait_send()

    vm = pl.BlockSpec(memory_space=pltpu.VMEM)
    return pl.pallas_call(
        body, in_specs=[vm], out_specs=vm, out_shape=jax.ShapeDtypeStruct((8, n, w), F32),
        scratch_shapes=[pltpu.SemaphoreType.DMA((7,)), pltpu.SemaphoreType.DMA((7,))], name=name,
    )(buf)


HBM = pltpu.MemorySpace.HBM


def _sequencer(name, collective_id, sems, sent):
    return functools.partial(pl.kernel, mesh=plsc.ScalarSubcoreMesh(axis_name="sequencer", num_cores=1), name=name,
                             scratch_types=sems, compiler_params=pltpu.CompilerParams(collective_id=collective_id),
                             cost_estimate=pl.CostEstimate(flops=0, transcendentals=0, bytes_accessed=2 * sent,
                                                           remote_bytes_transferred=sent))


def _nbytes(arrays):
    return sum(a.size * a.dtype.itemsize for a in arrays)


def _handshake(peers):
    barrier = pltpu.get_barrier_semaphore()
    for peer in peers:
        pl.semaphore_signal(barrier, inc=1, device_id=peer, device_id_type=MESH)
    pl.semaphore_wait(barrier, len(peers))


def _gather_seq(blocks, name, collective_id):
    n = len(blocks)
    half = [s.shape[1] // 2 for s in blocks]
    full = [jax.new_ref(b, memory_space=HBM) for b in blocks]

    @_sequencer(name, collective_id, [pltpu.SemaphoreType.DMA((n, 3))] * 4, _nbytes(blocks) * 3 // 4)
    def launch(ssem, rsem, fssem, frsem):
        x, y, c = _place()
        q = 2 * x + y
        chips = _other_chips(x, y)
        _handshake([(px, py, c) for px, py in chips] + [(x, y, 1 - c)])
        piece = lambda s, qq, cc: full[s].at[qq, pl.ds(cc * half[s], half[s])]
        sends = []
        for j, (px, py) in enumerate(chips):
            for s in range(n):
                cp = _remote(piece(s, q, c), piece(s, q, c), ssem.at[s, j], rsem.at[s, j], (px, py, c))
                cp.start()
                sends.append(cp)
        for j, (px, py) in enumerate(chips):
            for s in range(n):
                got = piece(s, 2 * px + py, c)
                _remote(got, got, ssem.at[s, j], rsem.at[s, j], (px, py, c)).wait_recv()
                cp = _remote(got, got, fssem.at[s, j], frsem.at[s, j], (x, y, 1 - c))
                cp.start()
                sends.append(cp)
        for j, (px, py) in enumerate(chips):
            for s in range(n):
                got = piece(s, 2 * px + py, 1 - c)
                _remote(got, got, fssem.at[s, j], frsem.at[s, j], (x, y, 1 - c)).wait_recv()
        for cp in sends:
            cp.wait_send()

    launch()
    return [r[...] for r in full]


def _share8(buf, name, collective_id):
    n, w = buf.shape
    src = jax.new_ref(buf, memory_space=HBM)
    out = jax.empty_ref(jax.ShapeDtypeStruct((8, n, w), F32), memory_space=HBM)

    @_sequencer(name, collective_id, [pltpu.SemaphoreType.DMA((7,)), pltpu.SemaphoreType.DMA((7,)), pltpu.SemaphoreType.DMA((1,))],
                7 * buf.size * 4)
    def launch(ssem, rsem, lsem):
        x, y, c = _place()
        me = 4 * x + 2 * y + c
        peers = [(1 - x if (k >> 2) & 1 else x, 1 - y if (k >> 1) & 1 else y, 1 - c if k & 1 else c) for k in range(1, 8)]
        _handshake(peers)
        mine = pltpu.make_async_copy(src, out.at[me], lsem.at[0])
        mine.start()
        sends = []
        for k, peer in enumerate(peers):
            cp = _remote(src, out.at[me], ssem.at[k], rsem.at[k], peer)
            cp.start()
            sends.append(cp)
        for k, (px, py, pc) in enumerate(peers):
            slot = out.at[4 * px + 2 * py + pc]
            _remote(slot, slot, ssem.at[k], rsem.at[k], (px, py, pc)).wait_recv()
        for cp in sends:
            cp.wait_send()
        mine.wait()

    launch()
    return out[...]


def _sum_slots(slots, name, after=None):
    _, n, w = slots.shape

    def body(s_ref, o_ref):
        acc = s_ref[0]
        for d in range(1, 8):
            acc = acc + s_ref[d]
        o_ref[...] = acc

    vm = pl.BlockSpec(memory_space=pltpu.VMEM)
    in_specs, args = [vm], [slots]
    if after is not None:
        body = _skip_ref(body, 1)
        args.append(_deps(after))
        in_specs.append(vm)
    return pl.pallas_call(body, in_specs=in_specs, out_specs=vm, out_shape=jax.ShapeDtypeStruct((n, w), F32), name=name)(*args)


def _pair_swap(parts, name, collective_id):
    n = len(parts)
    half = [p.shape[1] // 2 for p in parts]
    src = [jax.new_ref(p, memory_space=HBM) for p in parts]
    got = [jax.empty_ref(jax.ShapeDtypeStruct((p.shape[0], h, p.shape[2]), p.dtype), memory_space=HBM) for p, h in zip(parts, half)]

    @_sequencer(name, collective_id, [pltpu.SemaphoreType.DMA((n,))] * 2, _nbytes(parts) // 2)
    def launch(ssem, rsem):
        x, y, c = _place()
        _handshake([(x, y, 1 - c)])
        copies = []
        for s in range(n):
            cp = _remote(src[s].at[pl.ds(0, parts[s].shape[0]), pl.ds((1 - c) * half[s], half[s])], got[s], ssem.at[s], rsem.at[s], (x, y, 1 - c))
            cp.start()
            copies.append(cp)
        for cp in copies:
            cp.wait_recv()
        for cp in copies:
            cp.wait_send()

    launch()
    return [g[...] for g in got]


def _to_owners(sums, name, collective_id):
    n = len(sums)
    src = [jax.new_ref(s, memory_space=HBM) for s in sums]
    got = [jax.empty_ref(jax.ShapeDtypeStruct(s.shape, s.dtype), memory_space=HBM) for s in sums]

    @_sequencer(name, collective_id, [pltpu.SemaphoreType.DMA((n, 3))] * 2, _nbytes(sums) * 3 // 4)
    def launch(ssem, rsem):
        x, y, c = _place()
        q = 2 * x + y
        chips = _other_chips(x, y)
        _handshake([(px, py, c) for px, py in chips])
        sends = []
        for j, (px, py) in enumerate(chips):
            for s in range(n):
                cp = _remote(src[s].at[2 * px + py], got[s].at[q], ssem.at[s, j], rsem.at[s, j], (px, py, c))
                cp.start()
                sends.append(cp)
        for j, (px, py) in enumerate(chips):
            for s in range(n):
                slot = got[s].at[2 * px + py]
                _remote(slot, slot, ssem.at[s, j], rsem.at[s, j], (px, py, c)).wait_recv()
        for cp in sends:
            cp.wait_send()

    launch()
    return [g[...] for g in got]


def _pair_join(blocks, name, collective_id):
    n = len(blocks)
    out = [jax.new_ref(b, memory_space=HBM) for b in blocks]

    @_sequencer(name, collective_id, [pltpu.SemaphoreType.DMA((n,))] * 2, _nbytes(blocks) // 2)
    def launch(ssem, rsem):
        x, y, c = _place()
        _handshake([(x, y, 1 - c)])
        sends = []
        for s in range(n):
            h = blocks[s].shape[0] // 2
            mine = out[s].at[pl.ds(c * h, h)]
            cp = _remote(mine, mine, ssem.at[s], rsem.at[s], (x, y, 1 - c))
            cp.start()
            sends.append(cp)
        for s in range(n):
            h = blocks[s].shape[0] // 2
            theirs = out[s].at[pl.ds((1 - c) * h, h)]
            _remote(theirs, theirs, ssem.at[s], rsem.at[s], (x, y, 1 - c)).wait_recv()
        for cp in sends:
            cp.wait_send()

    launch()
    return [o[...] for o in out]


WIRE = BF16


def _row_tile(h):
    return _pick(h, (256, 368, 352, 128, 16))


def _add_pair(part, got, c, name, after=None):
    _, h, w = got.shape
    tr = _row_tile(h)
    nt = h // tr

    def body(c_ref, p_ref, g_ref, o_ref):
        o_ref[...] = (p_ref[...] + g_ref[...].astype(F32)).astype(o_ref.dtype)

    in_specs = [pl.BlockSpec((None, tr, w), lambda q, i, c_ref: (q, c_ref[0] * nt + i, 0)),
                pl.BlockSpec((None, tr, w), lambda q, i, c_ref: (q, i, 0))]
    args = [c.reshape(1).astype(jnp.int32), part, got]
    if after is not None:
        body = _skip_ref(body, len(args))
        args.append(_deps(after))
        in_specs.append(_dep_spec(args[-1]))
    return pl.pallas_call(
        body,
        grid_spec=pltpu.PrefetchScalarGridSpec(
            num_scalar_prefetch=1, grid=(got.shape[0], nt), in_specs=in_specs,
            out_specs=pl.BlockSpec((None, tr, w), lambda q, i, c_ref: (q, i, 0))),
        out_shape=jax.ShapeDtypeStruct(got.shape, WIRE), name=name,
        compiler_params=_cparams(("arbitrary", "arbitrary")),
    )(*args)


def _sum_chips(slots, sums, q, c, name, after=None):
    _, h, w = slots.shape
    tr = _row_tile(h)
    nt = h // tr

    def body(s_ref, mine_ref, a_ref, b_ref, d_ref, o_ref):
        o_ref[...] = ((mine_ref[...].astype(F32) + a_ref[...].astype(F32)) + b_ref[...].astype(F32)) + d_ref[...].astype(F32)

    slot = lambda k: pl.BlockSpec((None, tr, w), lambda i, s_ref: (s_ref[1 + k], i, 0))
    scalars = jnp.stack([c, q, (q + 1) % N_CHIPS, (q + 2) % N_CHIPS, (q + 3) % N_CHIPS]).astype(jnp.int32)
    in_specs, args = [slot(0), slot(1), slot(2), slot(3)], [scalars, sums, slots, slots, slots]
    if after is not None:
        body = _skip_ref(body, len(args))
        args.append(_deps(after))
        in_specs.append(_dep_spec(args[-1]))
    return pl.pallas_call(
        body,
        grid_spec=pltpu.PrefetchScalarGridSpec(
            num_scalar_prefetch=1, grid=(nt,), in_specs=in_specs,
            out_specs=pl.BlockSpec((tr, w), lambda i, s_ref: (s_ref[0] * nt + i, 0))),
        out_shape=jax.ShapeDtypeStruct((2 * h, w), F32), name=name,
        compiler_params=_cparams(("arbitrary",)),
    )(*args)


class _Reduce:
    def __init__(self, parts, q, c, tag, first_id, regions=None):
        self.parts, self.q, self.c, self.tag, self.first_id, self.regions = parts, q, c, tag, first_id, regions
        self.got = _pair_swap(parts, f"{tag}_pair_swap", first_id)

    def to_owners(self, after=None):
        self.sums = [_add_pair(p, g, self.c, f"{self.tag}_pair_add{i}", after)
                     for i, (p, g) in enumerate(zip(self.parts, self.got))]
        if self.regions is not None:
            self.sums = self.regions(self.sums)
        self.slots = _to_owners(self.sums, f"{self.tag}_to_owners", self.first_id + 1)
        return self.sums

    def join(self, after=None):
        blocks = [_sum_chips(sl, sm, self.q, self.c, f"{self.tag}_sum_chips{i}", after)
                  for i, (sl, sm) in enumerate(zip(self.slots, self.sums))]
        self.out = _pair_join(blocks, f"{self.tag}_pair_join", self.first_id + 2)
        return blocks


WEIGHTS = ("meta_tokens", "ffn1_norm", "ffn1_w_gu", "ffn1_w_down", "mix_norm", "w_in", "ssd_conv_w", "ssd_conv_b",
           "ssd_dt_bias", "ssd_a_log", "ssd_d", "ssd_norm", "hg_lower_bound", "hg_norm", "w_branch_a", "w_branch_b",
           "w_out", "ffn2_norm", "ffn2_w_gu", "ffn2_w_down", "final_norm")
BIG = ("ffn1_w_gu", "ffn1_w_down", "w_in", "w_branch_a", "w_branch_b", "w_out", "ffn2_w_gu", "ffn2_w_down")
SMALL = tuple(n for n in WEIGHTS if n not in BIG)


def _rows1024(a):
    flat = a.reshape(-1)
    n = -(-flat.shape[0] // 1024) * 1024
    return jnp.pad(flat, (0, n - flat.shape[0])).reshape(-1, 1024)


def kernel(x, meta_tokens, ffn1_norm, ffn1_w_gu, ffn1_w_down, mix_norm, w_in, ssd_conv_w, ssd_conv_b, ssd_dt_bias, ssd_a_log, ssd_d, ssd_norm, hg_lower_bound, hg_norm, w_branch_a, w_branch_b, w_out, ffn2_norm, ffn2_w_gu, ffn2_w_down, final_norm, loss_target, m_meta_tokens, m_ffn1_norm, m_ffn1_w_gu, m_ffn1_w_down, m_mix_norm, m_w_in, m_ssd_conv_w, m_ssd_conv_b, m_ssd_dt_bias, m_ssd_a_log, m_ssd_d, m_ssd_norm, m_hg_lower_bound, m_hg_norm, m_w_branch_a, m_w_branch_b, m_w_out, m_ffn2_norm, m_ffn2_w_gu, m_ffn2_w_down, m_final_norm, v_meta_tokens, v_ffn1_norm, v_ffn1_w_gu, v_ffn1_w_down, v_mix_norm, v_w_in, v_ssd_conv_w, v_ssd_conv_b, v_ssd_dt_bias, v_ssd_a_log, v_ssd_d, v_ssd_norm, v_hg_lower_bound, v_hg_norm, v_w_branch_a, v_w_branch_b, v_w_out, v_ffn2_norm, v_ffn2_w_gu, v_ffn2_w_down, v_final_norm):
    P = dict(zip(WEIGHTS, (meta_tokens, ffn1_norm, ffn1_w_gu, ffn1_w_down, mix_norm, w_in, ssd_conv_w, ssd_conv_b, ssd_dt_bias, ssd_a_log, ssd_d, ssd_norm, hg_lower_bound, hg_norm, w_branch_a, w_branch_b, w_out, ffn2_norm, ffn2_w_gu, ffn2_w_down, final_norm)))
    M = dict(zip(WEIGHTS, (m_meta_tokens, m_ffn1_norm, m_ffn1_w_gu, m_ffn1_w_down, m_mix_norm, m_w_in, m_ssd_conv_w, m_ssd_conv_b, m_ssd_dt_bias, m_ssd_a_log, m_ssd_d, m_ssd_norm, m_hg_lower_bound, m_hg_norm, m_w_branch_a, m_w_branch_b, m_w_out, m_ffn2_norm, m_ffn2_w_gu, m_ffn2_w_down, m_final_norm)))
    V = dict(zip(WEIGHTS, (v_meta_tokens, v_ffn1_norm, v_ffn1_w_gu, v_ffn1_w_down, v_mix_norm, v_w_in, v_ssd_conv_w, v_ssd_conv_b, v_ssd_dt_bias, v_ssd_a_log, v_ssd_d, v_ssd_norm, v_hg_lower_bound, v_hg_norm, v_w_branch_a, v_w_branch_b, v_w_out, v_ffn2_norm, v_ffn2_w_gu, v_ffn2_w_down, v_final_norm)))
    cx, cy, cc = _place()
    q = 2 * cx + cy

    mine = jnp.concatenate([meta_tokens.reshape(4, 1024), ssd_conv_w.reshape(2, 1024), jnp.zeros((2, 1024), F32)], axis=0)
    every = _exchange8(mine, "gather_small")
    meta_full = jnp.concatenate([every[2 * k, 0:4].reshape(N_META, 256) for k in range(N_CHIPS)], axis=1)
    conv_w_full = jnp.concatenate([every[2 * k, 4:6].reshape(SSD_CONV, 512) for k in range(N_CHIPS)], axis=1)

    late = ("ffn2_w_down", "w_branch_a", "w_branch_b", "w_out")
    rows = jnp.concatenate([P[n][0] for n in late], axis=0)
    zero = lambda t, dtype=F32: (t[0:1, 0:1] * 0).astype(dtype)

    def in_slot(s, after=None):
        s = s if after is None else s + zero(after)
        return lax.dynamic_update_slice(lax.empty((N_CHIPS,) + s.shape, BF16), s.astype(BF16)[None], (q, 0, 0))

    gu1, down1 = _gather_seq([in_slot(ffn1_w_gu[0]), in_slot(ffn1_w_down[0])], "gather_ffn1", 1)
    W = {n: P[n] for n in SMALL}
    W["meta_tokens"], W["ssd_conv_w"] = meta_full, conv_w_full
    W["ffn1_w_gu"], W["ffn1_w_down"] = gu1, down1.reshape(-1, D_MODEL)
    flying = {}

    def stage(name, t):
        if name == "ffn1_norm":
            flying["w_in"] = _gather_seq([in_slot(w_in[0], t)], "gather_w_in", 2)
            return {}
        if name == "ffn1_out":
            flying["late"] = _gather_seq([in_slot(ffn2_w_gu[0], t), in_slot(rows, t)], "gather_late", 3)
            (w_in_all,) = flying["w_in"]
            w_in_all = w_in_all + zero(t, BF16)
            return {"w_in": _split_w_in(w_in_all.transpose(1, 0, 2).reshape(D_MODEL, -1))}
        if name == "mixers_out":
            gu2, rows_all = flying["late"]
            out, r = {"ffn2_w_gu": gu2}, 0
            for n in late:
                nr = P[n].shape[1]
                out[n] = (rows_all[:, r:r + nr] + zero(t, BF16)).reshape(N_CHIPS * nr, D_MODEL)
                r += nr
            return out
        if name == "late_grads":
            parts = [t["ffn2_w_gu"]] + [t[n].reshape(N_CHIPS, -1, D_MODEL) for n in late]
            flying["grad_late"] = _Reduce(parts, q, cc, "grad_late", 4)
            return {"_after": [t["ffn2_w_gu"]] + [t[n] for n in late]}
        if name == "after_conv_bwd":
            return {"_after": flying["grad_late"].to_owners(after=t)}
        if name == "w_in_grads":
            order = ("z", "xbc", "dt", "qfig", "gates")
            blocks = flying["grad_late"].join(after=[t[k] for k in order])

            def regions(sums):
                z, xbc, dt, qfig, gates = [s[0] for s in sums]
                h = z.shape[0]
                qfig = qfig.reshape(h, HG_HEADS, 4, 128).transpose(0, 2, 1, 3).reshape(h, 4 * D_MODEL)
                cols = jnp.concatenate([z, xbc, dt[:, :SSD_HEADS], qfig, gates], axis=1)
                return [cols.reshape(h, N_CHIPS, -1).transpose(1, 0, 2)]

            flying["grad_w_in"] = _Reduce([t[k][None] for k in order], q, cc, "grad_w_in", 7, regions)
            return {"_after": blocks}
        if name == "ffn1_dw_down":
            return {"_after": flying["grad_w_in"].to_owners(after=t)}
        if name == "ffn1_dw":
            dw_gu, dw_down = t
            flying["grad_ffn1"] = _Reduce([dw_gu, dw_down.reshape(N_CHIPS, -1, D_MODEL)], q, cc, "grad_ffn1", 10)
            return {}
        if name == "ffn1_d_norm":
            blocks = flying["grad_w_in"].join(after=t)
            return {"_after": flying["grad_ffn1"].to_owners(after=blocks)}
        return {}

    W["_stage"] = stage

    loss8, grad_x, G = _local_step(x, loss_target, W)

    small = jnp.concatenate(
        [G["meta_tokens"]] + [_rows1024(G[n]) for n in SMALL if n != "meta_tokens"] + [_rows1024(loss8[0:1, 0:1])], axis=0)
    small = jnp.pad(small, ((0, 40 - small.shape[0]), (0, 0)))
    small_slots = _share8(small, "share_small", 13)

    grad_ffn1 = flying["grad_ffn1"]
    going = grad_ffn1.sums
    (g_w_in,) = flying["grad_w_in"].out
    Gb = dict(zip(("ffn2_w_gu",) + late, flying["grad_late"].out))
    Gb["w_in"] = g_w_in

    grads, delta, new_m, new_v, done = {}, {}, {}, {}, []
    cols = w_in.shape[2]
    to_tiles = lambda a: a.transpose(2, 0, 1).reshape(cols, 8, 128).reshape(cols * 8, 128)
    from_tiles = lambda a: a.reshape(cols, 1, D_MODEL).transpose(1, 2, 0)
    for n in [n for n in BIG if n in Gb]:
        if n == "w_in":
            g_t = to_tiles(Gb[n][None])
            d_, m_, v_ = _adamw(to_tiles(P[n]), g_t, to_tiles(M[n]), to_tiles(V[n]), f"adamw_{n}", after=going)
            grads[n], delta[n], new_m[n], new_v[n] = from_tiles(g_t), from_tiles(d_), from_tiles(m_), from_tiles(v_)
        else:
            d_, m_, v_ = _adamw(P[n][0], Gb[n], M[n][0], V[n][0], f"adamw_{n}", after=going)
            grads[n], delta[n], new_m[n], new_v[n] = Gb[n][None], d_[None], m_[None], v_[None]
        done.append(d_)

    small = _sum_slots(small_slots, "sum_small", after=done)
    Gs = {"meta_tokens": small[0:N_META]}
    r = N_META
    for n in SMALL:
        if n == "meta_tokens":
            continue
        nr = -(-G[n].size // 1024)
        Gs[n] = small[r:r + nr].reshape(-1)[:G[n].size].reshape(G[n].shape)
        r += nr
    loss = small[r, 0]
    Gs["meta_tokens"] = lax.dynamic_slice(Gs["meta_tokens"], (0, 256 * q), (N_META, 256))
    Gs["ssd_conv_w"] = lax.dynamic_slice(Gs["ssd_conv_w"], (0, 512 * q), (SSD_CONV, 512))[None]
    Gs = {n: Gs[n].reshape(P[n].shape) for n in SMALL}
    grads.update(Gs)
    flat = lambda a: a.reshape(-1, a.shape[-1])
    d_s, m_s, v_s = _adamw_many(*[[flat(D[n]) for n in SMALL] for D in (P, Gs, M, V)], "adamw_small")
    for i, n in enumerate(SMALL):
        delta[n], new_m[n], new_v[n] = d_s[i].reshape(P[n].shape), m_s[i].reshape(P[n].shape), v_s[i].reshape(P[n].shape)
    done.append(d_s[0])
    grad_ffn1.join(after=done)
    Gb["ffn1_w_gu"], Gb["ffn1_w_down"] = grad_ffn1.out
    for n in ("ffn1_w_gu", "ffn1_w_down"):
        d_, m_, v_ = _adamw(P[n][0], Gb[n], M[n][0], V[n][0], f"adamw_{n}")
        grads[n], delta[n], new_m[n], new_v[n] = Gb[n][None], d_[None], m_[None], v_[None]
    return (loss, grad_x, *[grads[n] for n in WEIGHTS], *[delta[n] for n in WEIGHTS],
            *[new_m[n] for n in WEIGHTS], *[new_v[n] for n in WEIGHTS])
```

```python
import functools

import jax
import jax.numpy as jnp
from jax import lax
from jax.experimental import pallas as pl
from jax.experimental.pallas import tpu as pltpu
from jax.experimental.pallas import tpu_sc as plsc

F32 = jnp.float32
BF16 = jnp.bfloat16
MESH = pl.DeviceIdType.MESH

D_MODEL = 1024
N_META = 16
EPS = 1e-6
SSD_HEADS = 16
SSD_HEAD_DIM = 64
SSD_INNER = 1024
SSD_GROUPS = 4
SSD_CONV = 4
HG_HEADS = 8
HG_SUB = 32
CHUNK = 128
D_FF = 2816
N_CHIPS = 4
IN_SIZES = (1024, 2048, 16, 1024, 1024, 1024, 1024, 1024, 1024)
ADAM_LR = 0.001
ADAM_B1 = 0.9
ADAM_B2 = 0.999
ADAM_EPS = 1e-08
ADAM_WD = 0.01
ADAM_STEP = 10
VMEM_LIMIT = 56 * 1024 * 1024
MATMUL_BLOCK_BYTES = 48 * 1024 * 1024
ADAMW_BLOCK_BYTES = 5 * 512 * 1024


def _cparams(sem=None):
    return pltpu.CompilerParams(dimension_semantics=sem, vmem_limit_bytes=VMEM_LIMIT)


def _pick(n, cands):
    for c in cands:
        if n % c == 0:
            return c
    return n


def _deps(after):
    xs = after if isinstance(after, (list, tuple)) else [after]
    one = lambda x: lax.slice(x, (0,) * x.ndim, (1,) * x.ndim).reshape(1).astype(F32)
    return jnp.concatenate([one(x) for x in xs]).reshape(1, -1)


def _dep_spec(dep):
    return pl.BlockSpec(dep.shape, lambda *_: (0, 0))


def _skip_ref(body, pos):
    return lambda *refs: body(*refs[:pos], *refs[pos + 1:])


def _dg(a, b, ca, cb):
    return lax.dot_general(a.astype(BF16), b.astype(BF16), (((ca,), (cb,)), ((), ())), preferred_element_type=F32)


@jax.custom_vjp
def _mm(a, b):
    return _dg(a, b, 1, 0)


def _mm_fwd(a, b):
    return _dg(a, b, 1, 0), (a, b)


def _mm_bwd(r, g):
    a, b = r
    return _dg(g, b, 1, 1), _dg(a, g, 0, 0)


_mm.defvjp(_mm_fwd, _mm_bwd)


@jax.custom_vjp
def _mm_nt(a, b):
    return _dg(a, b, 1, 1)


def _mm_nt_fwd(a, b):
    return _dg(a, b, 1, 1), (a, b)


def _mm_nt_bwd(r, g):
    a, b = r
    return _dg(g, b, 1, 0), _dg(g, a, 0, 0)


_mm_nt.defvjp(_mm_nt_fwd, _mm_nt_bwd)


@jax.custom_vjp
def _mm_tn(a, b):
    return _dg(a, b, 0, 0)


def _mm_tn_fwd(a, b):
    return _dg(a, b, 0, 0), (a, b)


def _mm_tn_bwd(r, g):
    a, b = r
    return _dg(b, g, 1, 1), _dg(a, g, 1, 0)


_mm_tn.defvjp(_mm_tn_fwd, _mm_tn_bwd)


def _tri_sum(x, lower):
    n = x.shape[0]
    ri = lax.broadcasted_iota(jnp.int32, (n, n), 0)
    ci = lax.broadcasted_iota(jnp.int32, (n, n), 1)
    tri = ((ri >= ci) if lower else (ri <= ci)).astype(BF16)
    x1 = x.astype(BF16)
    r1 = x - x1.astype(F32)
    x2 = r1.astype(BF16)
    x3 = (r1 - x2.astype(F32)).astype(BF16)
    dot = lambda p: lax.dot_general(tri, p, (((1,), (0,)), ((), ())), preferred_element_type=F32)
    return (dot(x3) + dot(x2)) + dot(x1)


@jax.custom_vjp
def _cumsum_rows(x):
    return _tri_sum(x, True)


_cumsum_rows.defvjp(lambda x: (_tri_sum(x, True), None), lambda _, g: (_tri_sum(g, False),))


def _silu(x):
    return x * jax.nn.sigmoid(x)


def _softplus(x):
    return jnp.maximum(x, 0.0) + jnp.log(1.0 + jnp.exp(-jnp.abs(x)))


def _tril(n):
    ri = lax.broadcasted_iota(jnp.int32, (n, n), 0)
    ci = lax.broadcasted_iota(jnp.int32, (n, n), 1)
    return ri >= ci


def _row_of(m, r):
    sub = lax.broadcasted_iota(jnp.int32, (m.shape[0], 1), 0)
    return jnp.sum(jnp.where(sub == r, m, 0.0), axis=0, keepdims=True)


def _col_of(m, c):
    lane = lax.broadcasted_iota(jnp.int32, (1, m.shape[1]), 1)
    return jnp.sum(jnp.where(lane == c, m, 0.0), axis=1, keepdims=True)


def _matmul(a, b, *, mode, out_dtype, name, alpha=1.0, res=None, tm=None, tn=None, out_groups=None, after=None):
    b3 = b.ndim == 3
    if mode == "nn":
        M, K = a.shape
        G = b.shape[0] if b3 else 1
        Ng = b.shape[-1]
        N = G * Ng
    elif mode == "nt":
        M, K = a.shape
        G = b.shape[0] if b3 else 1
        N = b.shape[-2]
        Kg = b.shape[-1]
        assert G * Kg == K
    else:
        K, M = a.shape
        N = b.shape[1]
        G = out_groups or 1
        Ng = N // G
    has_res = res is not None
    split_n = (mode == "nn" and b3) or (mode == "tn" and G > 1)
    per_mn = jnp.dtype(out_dtype).itemsize + (res.dtype.itemsize if has_res else 0)
    fits = [(m_ * n_, m_, n_)
            for m_ in (4352, 2176, 1408, 1088, 1024, 544, 512, 256, 128) if M % m_ == 0
            for n_ in (2816, 2048, 1408, 1024, 512, 256, 128) if (Ng if split_n else N) % n_ == 0
            if 2 * (K * m_ * a.dtype.itemsize + K * n_ * b.dtype.itemsize + m_ * n_ * per_mn) + 4 * m_ * n_ <= MATMUL_BLOCK_BYTES]
    _, tm_fit, tn_fit = max(fits)
    tm, tn = tm or tm_fit, tn or tn_fit
    nm, nn_ = M // tm, N // tn
    assert nm * tm == M and nn_ * tn == N, (name, M, N, K, tm, tn)

    if mode == "nn":
        a_spec = pl.BlockSpec((tm, K), lambda i, j: (i, 0))
        if b3:
            ns = Ng // tn
            b_spec = pl.BlockSpec((None, K, tn), lambda i, j: (j // ns, 0, j % ns))
        else:
            b_spec = pl.BlockSpec((K, tn), lambda i, j: (0, j))
        ca, cb = 1, 0
    elif mode == "nt":
        a_spec = pl.BlockSpec((tm, K), lambda i, j: (i, 0))
        if b3:
            b_spec = pl.BlockSpec((G, tn, Kg), lambda i, j: (0, j, 0))
        else:
            b_spec = pl.BlockSpec((tn, K), lambda i, j: (j, 0))
        ca, cb = 1, 1
    else:
        a_spec = pl.BlockSpec((K, tm), lambda i, j: (0, i))
        b_spec = pl.BlockSpec((K, tn), lambda i, j: (0, j))
        ca, cb = 0, 0
    if mode == "tn" and G > 1:
        ns = Ng // tn
        o_spec = pl.BlockSpec((None, tm, tn), lambda i, j: (j // ns, i, j % ns))
        out_shape = jax.ShapeDtypeStruct((G, M, Ng), out_dtype)
    else:
        o_spec = pl.BlockSpec((tm, tn), lambda i, j: (i, j))
        out_shape = jax.ShapeDtypeStruct((M, N), out_dtype)
    in_specs = [a_spec, b_spec]
    args = [a, b]
    if has_res:
        in_specs.append(pl.BlockSpec((tm, tn), lambda i, j: (i, j)))
        args.append(res)
    if after is not None:
        args.append(_deps(after))
        in_specs.append(_dep_spec(args[-1]))

    def body(*refs):
        a_ref, b_ref, o_ref = refs[0], refs[1], refs[-1]
        if mode == "nt" and b3:
            o = _dg(a_ref[:, 0:Kg], b_ref[0], ca, cb)
            for g in range(1, G):
                o = o + _dg(a_ref[:, g * Kg:(g + 1) * Kg], b_ref[g], ca, cb)
        else:
            o = _dg(a_ref[...], b_ref[...], ca, cb)
        if alpha != 1.0:
            o = o * alpha
        if has_res:
            o = o + refs[2][...]
        o_ref[...] = o.astype(o_ref.dtype)

    return pl.pallas_call(
        body, grid=(nm, nn_), in_specs=in_specs, out_specs=o_spec, out_shape=out_shape, name=name,
        compiler_params=_cparams(("parallel", "parallel")),
    )(*args)


def _sum_nt(xs, ws, name):
    R, N = xs[0].shape[0], ws[0].shape[0]
    n = len(xs)
    per_m = sum(x.shape[1] * x.dtype.itemsize for x in xs)
    per_n = sum(w.shape[1] * w.dtype.itemsize for w in ws)
    fits = [(m_ * n_, m_, n_) for m_ in (1088, 544, 256, 128) if R % m_ == 0 for n_ in (1024, 512, 256, 128) if N % n_ == 0
            if 2 * (m_ * per_m + n_ * per_n + m_ * n_ * 4) + 4 * m_ * n_ <= MATMUL_BLOCK_BYTES]
    _, tm, tn = max(fits)

    def body(*refs):
        o = _dg(refs[0][...], refs[n][...], 1, 1)
        for p in range(1, n):
            o = o + _dg(refs[p][...], refs[n + p][...], 1, 1)
        refs[-1][...] = o

    return pl.pallas_call(
        body, grid=(R // tm, N // tn),
        in_specs=[pl.BlockSpec((tm, x.shape[1]), lambda i, j: (i, 0)) for x in xs]
        + [pl.BlockSpec((tn, w.shape[1]), lambda i, j: (j, 0)) for w in ws],
        out_specs=pl.BlockSpec((tm, tn), lambda i, j: (i, j)), out_shape=jax.ShapeDtypeStruct((R, N), F32), name=name,
        compiler_params=_cparams(("parallel", "parallel")),
    )(*xs, *ws)


def _rms_fn(h, w):
    r = lax.rsqrt(jnp.mean(h * h, axis=-1, keepdims=True) + EPS)
    return h * r * w


def _swiglu_fn(gu):
    g = gu[:, :D_FF].astype(F32)
    u = gu[:, D_FF:].astype(F32)
    return _silu(g) * u


def _merge_fn(pa, pb, gates):
    return jax.nn.sigmoid(gates[:, :D_MODEL]) * pa + jax.nn.sigmoid(gates[:, D_MODEL:]) * pb


def _rows_call(body, *, rows, tr, ins, outs, accs=(), name, after=None):
    n = rows // tr
    assert n * tr == rows
    if after is not None:
        body = _skip_ref(body, len(ins))
        ins = list(ins) + [("full", _deps(after))]

    def spec(x):
        if isinstance(x, tuple):
            shp = x[1].shape
            return pl.BlockSpec(shp, lambda i: (0,) * len(shp))
        return pl.BlockSpec((tr, x.shape[1]), lambda i: (i, 0))

    in_specs = [spec(x) for x in ins]
    args = [x[1] if isinstance(x, tuple) else x for x in ins]
    out_specs = [spec(x) for x in outs] + [pl.BlockSpec(x.shape, lambda i: (0,) * len(x.shape)) for x in accs]
    out_shape = [x[1] if isinstance(x, tuple) else x for x in outs] + list(accs)
    return pl.pallas_call(
        body, grid=(n,), in_specs=in_specs, out_specs=out_specs, out_shape=out_shape, name=name,
        compiler_params=_cparams(("arbitrary",)),
    )(*args)


def _acc_rows(ref, val):
    @pl.when(pl.program_id(0) == 0)
    def _():
        ref[...] = jnp.zeros_like(ref)

    ref[0:1, :] += val


def _rms_fwd(h, w, name):
    def body(h_ref, w_ref, o_ref):
        o_ref[...] = _rms_fn(h_ref[...], w_ref[...]).astype(o_ref.dtype)

    R = h.shape[0]
    return _rows_call(body, rows=R, tr=_pick(R, (256, 128)), ins=[h, ("full", w)],
                      outs=[jax.ShapeDtypeStruct(h.shape, BF16)], name=name)[0]


def _rms_bwd(h, w, dn, dres, name, after=None):
    def body(h_ref, w_ref, dn_ref, dres_ref, dh_ref, dw_ref):
        _, vjp = jax.vjp(_rms_fn, h_ref[...], w_ref[...])
        dh, dw = vjp(dn_ref[...].astype(F32))
        dh_ref[...] = dh + dres_ref[...]
        _acc_rows(dw_ref, dw)

    R = h.shape[0]
    return _rows_call(body, rows=R, tr=_pick(R, (256, 128)), ins=[h, ("full", w), dn, dres],
                      outs=[jax.ShapeDtypeStruct(h.shape, F32)], accs=[jax.ShapeDtypeStruct((8, D_MODEL), F32)], name=name,
                      after=after)


def _d_norm_in(dgu, w_gu, h, norm_w, dres, name, after=None):
    R = h.shape[0]
    G, _, kg = w_gu.shape

    def body(dgu_ref, w_ref, h_ref, nw_ref, dres_ref, dh_ref, dw_ref):
        dn = _dg(dgu_ref[:, 0:kg], w_ref[0], 1, 1)
        for g in range(1, G):
            dn = dn + _dg(dgu_ref[:, kg * g:kg * (g + 1)], w_ref[g], 1, 1)
        _, vjp = jax.vjp(_rms_fn, h_ref[...], nw_ref[...])
        dh, dw = vjp(dn)
        dh_ref[...] = dh + dres_ref[...]
        _acc_rows(dw_ref, dw)

    return _rows_call(body, rows=R, tr=_pick(R, (256, 128)), ins=[dgu, ("full", w_gu), h, ("full", norm_w), dres],
                      outs=[jax.ShapeDtypeStruct(h.shape, F32)], accs=[jax.ShapeDtypeStruct((8, D_MODEL), F32)], name=name,
                      after=after)


def _rms_bwd_tokens(h, w, dn, dres, nseq, name, after=None):
    Tp = h.shape[0] // nseq
    nc = Tp // CHUNK

    def body(h_ref, w_ref, dn_ref, dres_ref, dx_ref, dm_ref, dw_ref):
        b, c = pl.program_id(0), pl.program_id(1)
        _, vjp = jax.vjp(_rms_fn, h_ref[...], w_ref[...])
        dh, dw = vjp(dn_ref[...].astype(F32))
        dh = dh + dres_ref[...]

        @pl.when(c == 0)
        def _():
            dm_ref[...] = dh

        @pl.when(c > 0)
        def _():
            dx_ref[...] = dh

        @pl.when((b == 0) & (c == 0))
        def _():
            dw_ref[...] = jnp.zeros_like(dw_ref)

        dw_ref[0:1, :] += dw

    rows = pl.BlockSpec((CHUNK, D_MODEL), lambda b, c: (b * nc + c, 0))
    in_specs, args = [rows, pl.BlockSpec((1, D_MODEL), lambda b, c: (0, 0)), rows, rows], [h, w, dn, dres]
    if after is not None:
        body = _skip_ref(body, len(args))
        args.append(_deps(after))
        in_specs.append(_dep_spec(args[-1]))
    return pl.pallas_call(
        body, grid=(nseq, nc), in_specs=in_specs,
        out_specs=[pl.BlockSpec((None, CHUNK, D_MODEL), lambda b, c: (b, jnp.maximum(c - 1, 0), 0)),
                   pl.BlockSpec((None, CHUNK, D_MODEL), lambda b, c: (b, 0, 0)),
                   pl.BlockSpec((8, D_MODEL), lambda b, c: (0, 0))],
        out_shape=[jax.ShapeDtypeStruct((nseq, Tp - CHUNK, D_MODEL), F32), jax.ShapeDtypeStruct((nseq, CHUNK, D_MODEL), F32),
                   jax.ShapeDtypeStruct((8, D_MODEL), F32)],
        name=name, compiler_params=_cparams(("arbitrary", "arbitrary")),
    )(*args)


def _gu_swiglu(n, w_gu, name):
    R = n.shape[0]
    G, _, ng = w_gu.shape

    def body(n_ref, w_ref, gu_ref, a_ref):
        x = n_ref[...]
        for r in range(G):
            gu_ref[:, ng * r:ng * (r + 1)] = _dg(x, w_ref[r], 1, 0).astype(gu_ref.dtype)
        a_ref[...] = _swiglu_fn(gu_ref[...]).astype(a_ref.dtype)

    return _rows_call(body, rows=R, tr=_pick(R, (256, 128)), ins=[n, ("full", w_gu)],
                      outs=[jax.ShapeDtypeStruct((R, 2 * D_FF), BF16), jax.ShapeDtypeStruct((R, D_FF), BF16)], name=name)


def _d_swiglu(dout, w_down, gu, alpha, name):
    R = gu.shape[0]

    def body(do_ref, w_ref, gu_ref, o_ref):
        da = _dg(do_ref[...] * alpha, w_ref[...], 1, 1)
        g = gu_ref[:, :D_FF].astype(F32)
        u = gu_ref[:, D_FF:].astype(F32)
        s = jax.nn.sigmoid(g)
        t = g * s
        o_ref[:, :D_FF] = (da * u * (s + t - t * s)).astype(o_ref.dtype)
        o_ref[:, D_FF:] = (da * t).astype(o_ref.dtype)

    return _rows_call(body, rows=R, tr=_pick(R, (256, 128)), ins=[dout, ("full", w_down), gu],
                      outs=[jax.ShapeDtypeStruct(gu.shape, BF16)], name=name)[0]


def _residual_matmul(a, w, res, alpha, name, norm_w=None):
    R, K = a.shape

    def body(a_ref, w_ref, r_ref, *rest):
        out = r_ref[...] + alpha * _dg(a_ref[...], w_ref[...], 1, 0)
        if norm_w is None:
            rest[0][...] = out
        else:
            rest[1][...] = out
            rest[2][...] = _rms_fn(out, rest[0][...]).astype(rest[2].dtype)

    f32 = jax.ShapeDtypeStruct((R, D_MODEL), F32)
    ins = [a, ("full", w), res] + ([] if norm_w is None else [("full", norm_w)])
    outs = [f32] + ([] if norm_w is None else [jax.ShapeDtypeStruct((R, D_MODEL), BF16)])
    got = _rows_call(body, rows=R, tr=_pick(R, (544, 256, 128)), ins=ins, outs=outs, name=name)
    return got[0] if norm_w is None else (got[0], got[1])


def _branch_merge(ya, yb, wa, wb, gates, name):
    def body(ya_ref, yb_ref, wa_ref, wb_ref, g_ref, pa_ref, pb_ref, o_ref):
        pa = _dg(ya_ref[...], wa_ref[...], 1, 0)
        pb = _dg(yb_ref[...], wb_ref[...], 1, 0)
        pa_ref[...] = pa
        pb_ref[...] = pb
        o_ref[...] = _merge_fn(pa, pb, g_ref[...].astype(F32)).astype(o_ref.dtype)

    R = ya.shape[0]
    f32 = jax.ShapeDtypeStruct((R, D_MODEL), F32)
    return _rows_call(body, rows=R, tr=_pick(R, (544, 256, 128)), ins=[ya, yb, ("full", wa), ("full", wb), gates],
                      outs=[f32, f32, jax.ShapeDtypeStruct((R, D_MODEL), BF16)], name=name)


def _branch_merge_bwd(pa, pb, gates, dm, wa, wb, name):
    def body(pa_ref, pb_ref, g_ref, dm_ref, wa_ref, wb_ref, dpa_ref, dpb_ref, dg_ref, dya_ref, dyb_ref):
        _, vjp = jax.vjp(_merge_fn, pa_ref[...], pb_ref[...], g_ref[...].astype(F32))
        dpa, dpb, dg = vjp(dm_ref[...].astype(F32))
        dpa_ref[...] = dpa.astype(dpa_ref.dtype)
        dpb_ref[...] = dpb.astype(dpb_ref.dtype)
        dg_ref[...] = dg.astype(dg_ref.dtype)
        dya_ref[...] = _dg(dpa, wa_ref[...], 1, 1).astype(dya_ref.dtype)
        dyb_ref[...] = _dg(dpb, wb_ref[...], 1, 1).astype(dyb_ref.dtype)

    R = pa.shape[0]
    b16 = jax.ShapeDtypeStruct(pa.shape, BF16)
    return _rows_call(body, rows=R, tr=_pick(R, (544, 256, 128)), ins=[pa, pb, gates, dm, ("full", wa), ("full", wb)],
                      outs=[b16, b16, jax.ShapeDtypeStruct(gates.shape, BF16), b16, b16], name=name)


def _loss_head(h3, w, target, nseq, name):
    Tp = h3.shape[0] // nseq
    nc = Tp // CHUNK

    def fn(h, w_, t, valid):
        y = _rms_fn(h, w_)
        e = (y - t) * valid
        return 0.5 * jnp.sum(jnp.mean(e * e, axis=-1, keepdims=True))

    def body(h_ref, w_ref, t_ref, loss_ref, dh_ref, dw_ref):
        b, c = pl.program_id(0), pl.program_id(1)
        valid = (c >= 1).astype(F32)
        t = t_ref[...]
        loss, vjp = jax.vjp(lambda h, w_: fn(h, w_, t, valid), h_ref[...], w_ref[...])
        dh, dw = vjp(jnp.ones((), F32))
        dh_ref[...] = dh

        @pl.when((b == 0) & (c == 0))
        def _():
            loss_ref[...] = jnp.zeros_like(loss_ref)
            dw_ref[...] = jnp.zeros_like(dw_ref)

        loss_ref[...] += jnp.full(loss_ref.shape, loss, F32)
        dw_ref[0:1, :] += dw

    return pl.pallas_call(
        body, grid=(nseq, nc),
        in_specs=[pl.BlockSpec((CHUNK, D_MODEL), lambda b, c: (b * nc + c, 0)),
                  pl.BlockSpec((1, D_MODEL), lambda b, c: (0, 0)),
                  pl.BlockSpec((None, CHUNK, D_MODEL), lambda b, c: (b, jnp.maximum(c - 1, 0), 0))],
        out_specs=[pl.BlockSpec((8, 128), lambda b, c: (0, 0)),
                   pl.BlockSpec((CHUNK, D_MODEL), lambda b, c: (b * nc + c, 0)),
                   pl.BlockSpec((8, D_MODEL), lambda b, c: (0, 0))],
        out_shape=[jax.ShapeDtypeStruct((8, 128), F32), jax.ShapeDtypeStruct(h3.shape, F32),
                   jax.ShapeDtypeStruct((8, D_MODEL), F32)],
        name=name, compiler_params=_cparams(("arbitrary", "arbitrary")),
    )(h3, w, target)


CONV_TILE = 512
CONV_HALO = 8


def _conv_fwd(xbc, w, b, pad, name):
    B, Tp, C = xbc.shape
    nch = Tp // CHUNK

    def body(x_ref, w_ref, b_ref, o_ref, xp):
        xp[0:CONV_HALO, :] = jnp.zeros((CONV_HALO, CONV_TILE), F32)
        xp[CONV_HALO:, :] = x_ref[...]
        for c in range(nch):
            acc = jnp.zeros((CHUNK, CONV_TILE), F32) + b_ref[...]
            for k in range(SSD_CONV):
                acc = acc + w_ref[k:k + 1, :] * xp[pl.ds(CONV_HALO + CHUNK * c - (SSD_CONV - 1) + k, CHUNK), :]
            out = _silu(acc)
            if CHUNK * c < pad:
                row = CHUNK * c + lax.broadcasted_iota(jnp.int32, (CHUNK, 1), 0)
                out = jnp.where(row >= pad, out, 0.0)
            o_ref[pl.ds(CHUNK * c, CHUNK), :] = out

    return pl.pallas_call(
        body, grid=(B, C // CONV_TILE),
        in_specs=[pl.BlockSpec((None, Tp, CONV_TILE), lambda i, j: (i, 0, j)),
                  pl.BlockSpec((SSD_CONV, CONV_TILE), lambda i, j: (0, j)),
                  pl.BlockSpec((1, CONV_TILE), lambda i, j: (0, j))],
        out_specs=pl.BlockSpec((None, Tp, CONV_TILE), lambda i, j: (i, 0, j)),
        out_shape=jax.ShapeDtypeStruct(xbc.shape, F32),
        scratch_shapes=[pltpu.VMEM((Tp + CONV_HALO, CONV_TILE), F32)],
        name=name, compiler_params=_cparams(("arbitrary", "arbitrary")),
    )(xbc, w, b)


def _conv_bwd(xbc, w, b, dact, pad, name):
    B, Tp, C = xbc.shape
    nch = Tp // CHUNK

    def body(x_ref, w_ref, b_ref, da_ref, dx_ref, dw_ref, db_ref, xp, dp):
        bi = pl.program_id(1)
        xp[0:CONV_HALO, :] = jnp.zeros((CONV_HALO, CONV_TILE), F32)
        xp[CONV_HALO:, :] = x_ref[...]
        dp[pl.ds(Tp, CONV_HALO), :] = jnp.zeros((CONV_HALO, CONV_TILE), F32)
        dws = [jnp.zeros((1, CONV_TILE), F32) for _ in range(SSD_CONV)]
        dbs = jnp.zeros((1, CONV_TILE), F32)
        for c in range(nch):
            xs = [xp[pl.ds(CONV_HALO + CHUNK * c - (SSD_CONV - 1) + k, CHUNK), :] for k in range(SSD_CONV)]
            acc = jnp.zeros((CHUNK, CONV_TILE), F32) + b_ref[...]
            for k in range(SSD_CONV):
                acc = acc + w_ref[k:k + 1, :] * xs[k]
            sg = jax.nn.sigmoid(acc)
            t = acc * sg
            dpre = da_ref[pl.ds(CHUNK * c, CHUNK), :] * (sg + t - t * sg)
            if CHUNK * c < pad:
                row = CHUNK * c + lax.broadcasted_iota(jnp.int32, (CHUNK, 1), 0)
                dpre = jnp.where(row >= pad, dpre, 0.0)
            dp[pl.ds(CHUNK * c, CHUNK), :] = dpre
            dbs = dbs + jnp.sum(dpre, axis=0, keepdims=True)
            for k in range(SSD_CONV):
                dws[k] = dws[k] + jnp.sum(dpre * xs[k], axis=0, keepdims=True)
        for c in range(nch):
            acc = jnp.zeros((CHUNK, CONV_TILE), F32)
            for k in range(SSD_CONV):
                acc = acc + w_ref[k:k + 1, :] * dp[pl.ds(CHUNK * c + (SSD_CONV - 1) - k, CHUNK), :]
            dx_ref[pl.ds(CHUNK * c, CHUNK), :] = acc.astype(dx_ref.dtype)

        @pl.when(bi == 0)
        def _():
            dw_ref[...] = jnp.zeros_like(dw_ref)
            db_ref[...] = jnp.zeros_like(db_ref)

        for k in range(SSD_CONV):
            dw_ref[k:k + 1, :] += dws[k]
        db_ref[0:1, :] += dbs

    return pl.pallas_call(
        body, grid=(C // CONV_TILE, B),
        in_specs=[pl.BlockSpec((None, Tp, CONV_TILE), lambda j, i: (i, 0, j)),
                  pl.BlockSpec((SSD_CONV, CONV_TILE), lambda j, i: (0, j)),
                  pl.BlockSpec((1, CONV_TILE), lambda j, i: (0, j)),
                  pl.BlockSpec((None, Tp, CONV_TILE), lambda j, i: (i, 0, j))],
        out_specs=[pl.BlockSpec((None, Tp, CONV_TILE), lambda j, i: (i, 0, j)),
                   pl.BlockSpec((8, CONV_TILE), lambda j, i: (0, j)),
                   pl.BlockSpec((8, CONV_TILE), lambda j, i: (0, j))],
        out_shape=[jax.ShapeDtypeStruct(xbc.shape, BF16), jax.ShapeDtypeStruct((8, C), F32),
                   jax.ShapeDtypeStruct((8, C), F32)],
        scratch_shapes=[pltpu.VMEM((Tp + CONV_HALO, CONV_TILE), F32), pltpu.VMEM((Tp + CONV_HALO, CONV_TILE), F32)],
        name=name, compiler_params=_cparams(("arbitrary", "arbitrary")),
    )(xbc, w, b, dact)


def _ssd_chunk(xs, bm, cm, dtr, z, state, dt_bias, a_log, dskip, norm_w, valid, kept=None, keep=False):
    Q = xs.shape[0]
    known = (lambda x, v: x) if kept is None else _known
    lane = lax.broadcasted_iota(jnp.int32, (1, 128), 1)
    dt = jnp.where(lane < SSD_HEADS, _softplus(dtr + dt_bias), 0.0) * valid
    a = dt * (-jnp.exp(a_log))
    tril = _tril(Q)
    cs = known(_cumsum_rows(a), None if kept is None else kept[0])
    cs_t = cs.T
    cs_end = _row_of(cs, Q - 1)
    low = lane < SSD_HEAD_DIM
    low_rows = lax.broadcasted_iota(jnp.int32, (128, 1), 0) < SSD_HEAD_DIM
    ys, new_state, cbs = [], [], []
    for g in range(SSD_GROUPS):
        bg = bm[:, 128 * g:128 * (g + 1)]
        cg = cm[:, 128 * g:128 * (g + 1)]
        cb = known(_mm_nt(cg, bg), None if kept is None else kept[1][Q * g:Q * (g + 1)])
        cbs.append(cb)
        for pr in range(2):
            p = 2 * g + pr
            h0, h1 = 2 * p, 2 * p + 1
            xp = xs[:, 128 * p:128 * (p + 1)]
            c0, c1 = _col_of(cs, h0), _col_of(cs, h1)
            e0, e1 = _col_of(cs_end, h0), _col_of(cs_end, h1)
            xd = xp * jnp.where(low, _col_of(dt, h0), _col_of(dt, h1))
            l0 = jnp.exp(jnp.where(tril, c0 - _row_of(cs_t, h0), -1e30))
            l1 = jnp.exp(jnp.where(tril, c1 - _row_of(cs_t, h1), -1e30))
            y_diag = jnp.where(low, _mm(cb * l0, xd), _mm(cb * l1, xd))
            to_end = jnp.where(low, jnp.exp(e0 - c0), jnp.exp(e1 - c1))
            sp = state[128 * p:128 * (p + 1), :]
            y_off = _mm_nt(cg, sp) * jnp.where(low, jnp.exp(c0), jnp.exp(c1))
            new_state.append(sp * jnp.where(low_rows, jnp.exp(e0), jnp.exp(e1)) + _mm_tn(xd * to_end, bg))
            ys.append(y_diag + y_off + xp * jnp.where(low, _col_of(dskip, h0), _col_of(dskip, h1)))
    y_raw = known(jnp.concatenate(ys, axis=1), None if kept is None else kept[2])
    y = y_raw * _silu(z)
    gw = SSD_INNER // SSD_GROUPS
    outs = []
    for g in range(SSD_GROUPS):
        blk = y[:, gw * g:gw * (g + 1)]
        outs.append(blk * lax.rsqrt(jnp.mean(blk * blk, axis=-1, keepdims=True) + EPS))
    out, state_out = jnp.concatenate(outs, axis=1) * norm_w, jnp.concatenate(new_state, axis=0)
    if kept is not None:
        state_out = _known(state_out, state)
    return (out, state_out, (cs, jnp.concatenate(cbs, axis=0), y_raw)) if keep else (out, state_out)


def _valid_rows(c, pad):
    row = c * CHUNK + lax.broadcasted_iota(jnp.int32, (CHUNK, 1), 0)
    return (row >= pad).astype(F32)


def _ssd_fwd(xact, dtr, z, dt_bias, a_log, dskip, norm_w, pad, name):
    B, Tp, _ = xact.shape
    nc = Tp // CHUNK

    def body(xs_ref, bm_ref, cm_ref, dt_ref, z_ref, db_ref, al_ref, ds_ref, nw_ref, y_ref, save_ref, cs_ref, cb_ref, yr_ref, st):
        c = pl.program_id(1)

        @pl.when(c == 0)
        def _():
            st[...] = jnp.zeros_like(st)

        s0 = st[...]
        save_ref[...] = s0
        y, s1, (cs, cb, y_raw) = _ssd_chunk(xs_ref[...], bm_ref[...], cm_ref[...], dt_ref[...], z_ref[...].astype(F32), s0,
                                            db_ref[...], al_ref[...], ds_ref[...], nw_ref[...], _valid_rows(c, pad), keep=True)
        y_ref[...] = y.astype(y_ref.dtype)
        cs_ref[...] = cs
        cb_ref[...] = cb
        yr_ref[...] = y_raw
        st[...] = s1

    row = lambda w, off=0: pl.BlockSpec((None, CHUNK, w), lambda b, c: (b, c, off))
    par = lambda w: pl.BlockSpec((1, w), lambda b, c: (0, 0))
    per_chunk = lambda r: pl.BlockSpec((None, None, r, 128), lambda b, c: (b, c, 0, 0))
    return pl.pallas_call(
        body, grid=(B, nc),
        in_specs=[row(1024, 0), row(512, 2), row(512, 3), row(128), row(1024), par(128), par(128), par(128), par(1024)],
        out_specs=[row(1024), per_chunk(1024), row(128), per_chunk(SSD_GROUPS * CHUNK), row(1024)],
        out_shape=[jax.ShapeDtypeStruct((B, Tp, SSD_INNER), BF16), jax.ShapeDtypeStruct((B, nc, 1024, 128), F32),
                   jax.ShapeDtypeStruct((B, Tp, 128), F32), jax.ShapeDtypeStruct((B, nc, SSD_GROUPS * CHUNK, 128), F32),
                   jax.ShapeDtypeStruct((B, Tp, SSD_INNER), F32)],
        scratch_shapes=[pltpu.VMEM((1024, 128), F32)],
        name=name, compiler_params=_cparams(("arbitrary", "arbitrary")),
    )(xact, xact, xact, dtr, z, dt_bias, a_log, dskip, norm_w)


def _ssd_bwd(xact, dtr, z, dt_bias, a_log, dskip, norm_w, saved, kept, dy, pad, name, after=None):
    B, Tp, _ = xact.shape
    nc = Tp // CHUNK

    def body(xs_ref, bm_ref, cm_ref, dt_ref, z_ref, db_ref, al_ref, ds_ref, nw_ref, sv_ref, cs_ref, cb_ref, yr_ref, dy_ref,
             dx_ref, ddt_ref, dz_ref, dpar_ref, dnw_ref, dst):
        b, i = pl.program_id(0), pl.program_id(1)
        c = nc - 1 - i

        @pl.when(i == 0)
        def _():
            dst[...] = jnp.zeros_like(dst)

        valid = _valid_rows(c, pad)
        kept_c = (cs_ref[...], cb_ref[...], yr_ref[...])
        fn = lambda *a: _ssd_chunk(*a, valid, kept=kept_c)
        _, vjp = jax.vjp(fn, xs_ref[...], bm_ref[...], cm_ref[...], dt_ref[...], z_ref[...].astype(F32), sv_ref[...],
                         db_ref[...], al_ref[...], ds_ref[...], nw_ref[...])
        dxs, dbm, dcm, ddt, dz, dstate, ddb, dal, dds, dnw = vjp((dy_ref[...].astype(F32), dst[...]))
        dx_ref[:, 0:1024] = dxs
        dx_ref[:, 1024:1536] = dbm
        dx_ref[:, 1536:2048] = dcm
        ddt_ref[...] = ddt
        dz_ref[...] = dz.astype(dz_ref.dtype)
        dst[...] = dstate

        @pl.when((b == 0) & (i == 0))
        def _():
            dpar_ref[...] = jnp.zeros_like(dpar_ref)
            dnw_ref[...] = jnp.zeros_like(dnw_ref)

        dpar_ref[0:1, :] += ddb
        dpar_ref[1:2, :] += dal
        dpar_ref[2:3, :] += dds
        dnw_ref[0:1, :] += dnw

    row = lambda w, off=0: pl.BlockSpec((None, CHUNK, w), lambda b, i: (b, nc - 1 - i, off))
    par = lambda w: pl.BlockSpec((1, w), lambda b, i: (0, 0))
    acc = lambda w: pl.BlockSpec((8, w), lambda b, i: (0, 0))
    per_chunk = lambda r: pl.BlockSpec((None, None, r, 128), lambda b, i: (b, nc - 1 - i, 0, 0))
    in_specs = [row(1024, 0), row(512, 2), row(512, 3), row(128), row(1024), par(128), par(128), par(128), par(1024),
                per_chunk(1024), row(128), per_chunk(SSD_GROUPS * CHUNK), row(1024), row(1024)]
    args = [xact, xact, xact, dtr, z, dt_bias, a_log, dskip, norm_w, saved, kept[0], kept[1], kept[2], dy]
    if after is not None:
        body = _skip_ref(body, len(args))
        args.append(_deps(after))
        in_specs.append(_dep_spec(args[-1]))
    outs = pl.pallas_call(
        body, grid=(B, nc), in_specs=in_specs,
        out_specs=[row(2048), row(128), row(1024), acc(128), acc(1024)],
        out_shape=[jax.ShapeDtypeStruct((B, Tp, 2048), F32), jax.ShapeDtypeStruct((B, Tp, 128), F32),
                   jax.ShapeDtypeStruct((B, Tp, 1024), BF16), jax.ShapeDtypeStruct((8, 128), F32),
                   jax.ShapeDtypeStruct((8, 1024), F32)],
        scratch_shapes=[pltpu.VMEM((1024, 128), F32)],
        name=name, compiler_params=_cparams(("arbitrary", "arbitrary")),
    )(*args)
    return outs


@jax.custom_vjp
def _known(x, value):
    return value


_known.defvjp(lambda x, value: (value, None), lambda _, g: (g, jnp.zeros_like(g)))


def _hg_chunk(qr, fr, ir, gr, state_t, p0, p1, norm_w, valid, kept=None, keep=False):
    Q = qr.shape[0]
    known = (lambda x, i: x) if kept is None else (lambda x, i: _known(x, kept[i].astype(x.dtype)))
    lb = jax.nn.sigmoid(p0 - p1)
    f = lb + (1.0 - lb) * jax.nn.sigmoid(fr)
    k = 1.0 - f
    q = _silu(qr)
    v = ir * valid
    cum = known(_cumsum_rows(jnp.log(f)), 0)
    cum_end = _row_of(cum, Q - 1)
    o_inter = _mm_nt(q * jnp.exp(cum), state_t)
    nblk = Q // HG_SUB
    row = lax.broadcasted_iota(jnp.int32, (Q, 1), 0)
    ri = lax.broadcasted_iota(jnp.int32, (Q, Q), 0)
    ci = lax.broadcasted_iota(jnp.int32, (Q, Q), 1)
    mids = jnp.concatenate([jnp.broadcast_to(_row_of(cum, HG_SUB * i + HG_SUB // 2 - 1), (HG_SUB, cum.shape[1]))
                            for i in range(nblk)], axis=0)
    sh = HG_SUB.bit_length() - 1
    same = (jnp.right_shift(ri, sh) == jnp.right_shift(ci, sh)) & (ri >= ci)
    att = jnp.where(same, _mm_nt(q * jnp.exp(cum - mids), k * jnp.exp(mids - cum)), 0.0)
    qas, kas = [], []
    for i in range(1, nblk):
        lo = HG_SUB * i
        start = _row_of(cum, lo - 1)
        qas.append(q * jnp.exp(jnp.where((row >= lo) & (row < lo + HG_SUB), cum - start, -1e30)))
        kas.append(k * jnp.exp(jnp.where(row < lo, start - cum, -1e30)))
    att = att + _mm_nt(jnp.concatenate(qas, axis=1), jnp.concatenate(kas, axis=1))
    att = known(att, 1)
    o = known(o_inter + _mm(att, v), 2)
    new_state_t = state_t * jnp.exp(cum_end) + _mm_tn(v, k * jnp.exp(cum_end - cum))
    if kept is not None:
        new_state_t = _known(new_state_t, state_t)
    y = o * lax.rsqrt(jnp.mean(o * o, axis=-1, keepdims=True) + EPS) * norm_w * _silu(gr)
    return (y, new_state_t, (cum, att, o)) if keep else (y, new_state_t)


HG_PER_STEP = 8
HG_COLS = 4 * 128


def _hg_fwd(qfig, lbh, nwh, pad, name):
    B, Tp, _ = qfig.shape
    nc = Tp // CHUNK
    hp = HG_PER_STEP

    def body(x_ref, lb_ref, nw_ref, y_ref, save_ref, cum_ref, att_ref, o_ref, st):
        c = pl.program_id(1)

        @pl.when(c == 0)
        def _():
            st[...] = jnp.zeros_like(st)

        valid = _valid_rows(c, pad)
        for j in range(hp):
            for b in range(B):
                s0 = st[j, b]
                save_ref[j, b] = s0
                col = lambda k: x_ref[b, :, HG_COLS * j + 128 * k:HG_COLS * j + 128 * (k + 1)]
                y, s1, (cum, att, o) = _hg_chunk(col(0), col(1), col(2), col(3), s0, lb_ref[j, 0:1, :], lb_ref[j, 1:2, :],
                                                 nw_ref[j], valid, keep=True)
                y_ref[b, :, 128 * j:128 * (j + 1)] = y.astype(y_ref.dtype)
                cum_ref[b, :, 128 * j:128 * (j + 1)] = cum
                att_ref[j, b] = att.astype(att_ref.dtype)
                o_ref[b, :, 128 * j:128 * (j + 1)] = o
                st[j, b] = s1

    rows = pl.BlockSpec((B, CHUNK, 128 * hp), lambda h, c: (0, c, h))
    per_chunk = pl.BlockSpec((hp, B, None, 128, 128), lambda h, c: (h, 0, c, 0, 0))
    return pl.pallas_call(
        body, grid=(HG_HEADS // hp, nc),
        in_specs=[pl.BlockSpec((B, CHUNK, HG_COLS * hp), lambda h, c: (0, c, h)),
                  pl.BlockSpec((hp, 2, 128), lambda h, c: (h, 0, 0)),
                  pl.BlockSpec((hp, 1, 128), lambda h, c: (h, 0, 0))],
        out_specs=[rows, per_chunk, rows, per_chunk, rows],
        out_shape=[jax.ShapeDtypeStruct((B, Tp, 1024), BF16), jax.ShapeDtypeStruct((HG_HEADS, B, nc, 128, 128), F32),
                   jax.ShapeDtypeStruct((B, Tp, 1024), F32), jax.ShapeDtypeStruct((HG_HEADS, B, nc, 128, 128), BF16),
                   jax.ShapeDtypeStruct((B, Tp, 1024), F32)],
        scratch_shapes=[pltpu.VMEM((hp, B, 128, 128), F32)],
        name=name, compiler_params=_cparams(("arbitrary", "arbitrary")),
    )(qfig, lbh, nwh)


def _hg_bwd(qfig, lbh, nwh, saved, kept, dy, pad, name, after=None):
    B, Tp, _ = qfig.shape
    nc = Tp // CHUNK
    hp = HG_PER_STEP

    def body(x_ref, lb_ref, nw_ref, sv_ref, cum_ref, att_ref, o_ref, dy_ref, dx_ref, dlb_ref, dnw_ref, dst):
        i = pl.program_id(1)
        c = nc - 1 - i

        @pl.when(i == 0)
        def _():
            dst[...] = jnp.zeros_like(dst)
            dlb_ref[...] = jnp.zeros_like(dlb_ref)
            dnw_ref[...] = jnp.zeros_like(dnw_ref)

        valid = _valid_rows(c, pad)
        for j in range(hp):
            for b in range(B):
                col = lambda k: x_ref[b, :, HG_COLS * j + 128 * k:HG_COLS * j + 128 * (k + 1)]
                head = slice(128 * j, 128 * (j + 1))
                kept_jb = (cum_ref[b, :, head], att_ref[j, b], o_ref[b, :, head])
                fn = lambda *a: _hg_chunk(*a, valid, kept=kept_jb)
                _, vjp = jax.vjp(fn, col(0), col(1), col(2), col(3), sv_ref[j, b], lb_ref[j, 0:1, :], lb_ref[j, 1:2, :], nw_ref[j])
                d4 = vjp((dy_ref[b, :, 128 * j:128 * (j + 1)].astype(F32), dst[j, b]))
                for k in range(4):
                    dx_ref[b, :, HG_COLS * j + 128 * k:HG_COLS * j + 128 * (k + 1)] = d4[k].astype(dx_ref.dtype)
                dst[j, b] = d4[4]
                dlb_ref[j, 0:1, :] += d4[5]
                dlb_ref[j, 1:2, :] += d4[6]
                dnw_ref[j, 0:1, :] += d4[7]

    acc = pl.BlockSpec((hp, 8, 128), lambda h, i: (h, 0, 0))
    rows = pl.BlockSpec((B, CHUNK, 128 * hp), lambda h, i: (0, nc - 1 - i, h))
    per_chunk = pl.BlockSpec((hp, B, None, 128, 128), lambda h, i: (h, 0, nc - 1 - i, 0, 0))
    in_specs = [pl.BlockSpec((B, CHUNK, HG_COLS * hp), lambda h, i: (0, nc - 1 - i, h)),
                pl.BlockSpec((hp, 2, 128), lambda h, i: (h, 0, 0)),
                pl.BlockSpec((hp, 1, 128), lambda h, i: (h, 0, 0)),
                per_chunk, rows, per_chunk, rows, rows]
    args = [qfig, lbh, nwh, saved, kept[0], kept[1], kept[2], dy]
    if after is not None:
        body = _skip_ref(body, len(args))
        args.append(_deps(after))
        in_specs.append(_dep_spec(args[-1]))
    return pl.pallas_call(
        body, grid=(HG_HEADS // hp, nc), in_specs=in_specs,
        out_specs=[pl.BlockSpec((B, CHUNK, HG_COLS * hp), lambda h, i: (0, nc - 1 - i, h)), acc, acc],
        out_shape=[jax.ShapeDtypeStruct((B, Tp, 4096), BF16), jax.ShapeDtypeStruct((HG_HEADS, 8, 128), F32),
                   jax.ShapeDtypeStruct((HG_HEADS, 8, 128), F32)],
        scratch_shapes=[pltpu.VMEM((hp, B, 128, 128), F32)],
        name=name, compiler_params=_cparams(("arbitrary", "arbitrary")),
    )(*args)


def _adamw_math(w, g, m, v):
    m = ADAM_B1 * m + (1.0 - ADAM_B1) * g
    v = ADAM_B2 * v + (1.0 - ADAM_B2) * (g * g)
    m_hat = m / (1.0 - ADAM_B1 ** ADAM_STEP)
    v_hat = v / (1.0 - ADAM_B2 ** ADAM_STEP)
    return -ADAM_LR * (m_hat / (jnp.sqrt(v_hat) + ADAM_EPS) + ADAM_WD * w), m, v


def _adamw_many(ws, gs, ms, vs, name):
    n = len(ws)

    def body(*refs):
        for i in range(n):
            d, m, v = _adamw_math(refs[i][...], refs[n + i][...], refs[2 * n + i][...], refs[3 * n + i][...])
            refs[4 * n + i][...] = d
            refs[5 * n + i][...] = m
            refs[6 * n + i][...] = v

    vm = pl.BlockSpec(memory_space=pltpu.VMEM)
    outs = pl.pallas_call(body, in_specs=[vm] * (4 * n), out_specs=[vm] * (3 * n),
                          out_shape=[jax.ShapeDtypeStruct(w.shape, F32) for w in ws] * 3, name=name)(*ws, *gs, *ms, *vs)
    return outs[:n], outs[n:2 * n], outs[2 * n:]


def _adamw(w, g, m, v, name, after=None):
    R, C = w.shape
    tr = max(t for t in range(8, R + 1, 8) if R % t == 0 and (t * C * 4 <= ADAMW_BLOCK_BYTES or t == 8))

    def body(w_ref, g_ref, m_ref, v_ref, d_ref, mo_ref, vo_ref):
        d_ref[...], mo_ref[...], vo_ref[...] = _adamw_math(w_ref[...], g_ref[...], m_ref[...], v_ref[...])

    sp = pl.BlockSpec((tr, C), lambda i: (i, 0))
    sh = jax.ShapeDtypeStruct((R, C), F32)
    in_specs, args = [sp] * 4, [w, g, m, v]
    if after is not None:
        body = _skip_ref(body, len(args))
        args.append(_deps(after))
        in_specs.append(_dep_spec(args[-1]))
    return pl.pallas_call(body, grid=(R // tr,), in_specs=in_specs, out_specs=[sp] * 3, out_shape=[sh] * 3,
                          name=name, compiler_params=_cparams(("arbitrary",)))(*args)


def _ffn_fwd(h, norm_w, w_gu, w_down, tag, after_norm=None, n=None, next_norm_w=None):
    if n is None:
        n = _rms_fwd(h, norm_w, f"{tag}_norm")
    if after_norm is not None:
        after_norm(n)
    gu, a = _gu_swiglu(n, w_gu, f"{tag}_gu")
    out = _residual_matmul(a, w_down, h, 0.5, f"{tag}_down", next_norm_w)
    return out, (n, gu, a)


def _ffn_bwd(h, norm_w, w_gu, w_down, saved, dout, tag, after_dw_down=None, token_seqs=None, told=None):
    n, gu, a = saved
    dgu = _d_swiglu(dout, w_down, gu, 0.5, f"{tag}_d_gu")
    dw_down = _matmul(a, dout, mode="tn", out_dtype=F32, alpha=0.5, name=f"{tag}_dw_down")
    dw_gu = _matmul(n, dgu, mode="tn", out_dtype=F32, out_groups=N_CHIPS, name=f"{tag}_dw_gu",
                    after=after_dw_down(dw_down) if after_dw_down else None)
    if token_seqs is None:
        dh, dnw = _d_norm_in(dgu, w_gu, h, norm_w, dout, f"{tag}_d_in", after=dw_gu)
    else:
        if told is not None:
            told("dw", (dw_gu, dw_down))
        dn = _matmul(dgu, w_gu, mode="nt", out_dtype=F32, name=f"{tag}_d_norm", after=dw_gu)
        dx, dm, dnw = _rms_bwd_tokens(h, norm_w, dn, dout, token_seqs, f"{tag}_d_in",
                                      after=told("d_norm", dn) if told is not None else None)
        dh = (dx, dm)
    return dh, dnw, dw_gu, dw_down


def _split_w_in(w_in_full):
    pts = [0]
    for s in IN_SIZES:
        pts.append(pts[-1] + s)
    sl = lambda i, j: w_in_full[:, pts[i]:pts[j]]
    qfig = sl(3, 7).reshape(D_MODEL, 4, HG_HEADS, 128).transpose(0, 2, 1, 3).reshape(D_MODEL, 4 * D_MODEL)
    return {"z": sl(0, 1), "xbc": sl(1, 2), "dt": jnp.pad(sl(2, 3), ((0, 0), (0, 128 - SSD_HEADS))),
            "qfig": qfig, "gates": sl(7, 9)}


def _local_step(x, target, W):
    B, S, _ = x.shape
    T = N_META + S
    pad = (-T) % CHUNK
    Tp = T + pad
    assert pad + N_META == CHUNK
    R = B * Tp
    meta = jnp.broadcast_to(W["meta_tokens"][None], (B, N_META, D_MODEL))
    h0 = jnp.concatenate([jnp.zeros((B, pad, D_MODEL), F32), meta, x], axis=1).reshape(R, D_MODEL)

    stage = W.get("_stage", lambda name, x: {})
    W = dict(W)
    (h1, um), sv1 = _ffn_fwd(h0, W["ffn1_norm"], W["ffn1_w_gu"], W["ffn1_w_down"], "ffn1",
                             lambda n: W.update(stage("ffn1_norm", n)), next_norm_w=W["mix_norm"])
    W.update(stage("ffn1_out", h1))
    wi = W["w_in"]
    z = _matmul(um, wi["z"], mode="nn", out_dtype=BF16, name="in_z")
    xbc = _matmul(um, wi["xbc"], mode="nn", out_dtype=F32, name="in_xbc")
    dtr = _matmul(um, wi["dt"], mode="nn", out_dtype=F32, name="in_dt")
    qfig = _matmul(um, wi["qfig"], mode="nn", out_dtype=F32, name="in_qfig")
    gates = _matmul(um, wi["gates"], mode="nn", out_dtype=BF16, name="in_gates")

    r3 = lambda t: t.reshape(B, Tp, t.shape[-1])
    lane_pad = lambda t: jnp.pad(t, ((0, 0), (0, 128 - t.shape[1])))
    dt_bias, a_log, dskip = lane_pad(W["ssd_dt_bias"]), lane_pad(W["ssd_a_log"]), lane_pad(W["ssd_d"])
    xact = _conv_fwd(r3(xbc), W["ssd_conv_w"], W["ssd_conv_b"], pad, "conv_fwd")
    ya, ssd_saved, *ssd_kept = _ssd_fwd(xact, r3(dtr), r3(z), dt_bias, a_log, dskip, W["ssd_norm"], pad, "ssd_fwd")
    lbh = W["hg_lower_bound"].reshape(2, HG_HEADS, 128).transpose(1, 0, 2)
    nwh = W["hg_norm"].reshape(HG_HEADS, 1, 128)
    yb, hg_saved, *hg_kept = _hg_fwd(r3(qfig), lbh, nwh, pad, "hg_fwd")
    ya2, yb2 = ya.reshape(R, -1), yb.reshape(R, -1)
    W.update(stage("mixers_out", yb2))
    pa, pb, mg = _branch_merge(ya2, yb2, W["w_branch_a"], W["w_branch_b"], gates, "branch_merge")
    h2, n2 = _residual_matmul(mg, W["w_out"], h1, 1.0, "mix_out", W["ffn2_norm"])
    h3, sv2 = _ffn_fwd(h2, W["ffn2_norm"], W["ffn2_w_gu"], W["ffn2_w_down"], "ffn2", n=n2)

    loss, dh3, d_final = _loss_head(h3, W["final_norm"].reshape(1, D_MODEL), target, B, "loss_head")

    G = {"final_norm": d_final[0]}
    dh2, dnw, G["ffn2_w_gu"], G["ffn2_w_down"] = _ffn_bwd(h2, W["ffn2_norm"], W["ffn2_w_gu"], W["ffn2_w_down"], sv2, dh3, "ffn2")
    G["ffn2_norm"] = dnw[0:1]
    dmg = _matmul(dh2, W["w_out"], mode="nt", out_dtype=BF16, name="d_merge")
    G["w_out"] = _matmul(mg, dh2, mode="tn", out_dtype=F32, name="dw_out")
    dpa, dpb, dgates, dya, dyb = _branch_merge_bwd(pa, pb, gates, dmg, W["w_branch_a"], W["w_branch_b"], "branch_merge_bwd")
    G["w_branch_a"] = _matmul(ya2, dpa, mode="tn", out_dtype=F32, name="dw_branch_a")
    G["w_branch_b"] = _matmul(yb2, dpb, mode="tn", out_dtype=F32, name="dw_branch_b")

    dxact, ddtr, dz, dpar, dnw = _ssd_bwd(xact, r3(dtr), r3(z), dt_bias, a_log, dskip, W["ssd_norm"], ssd_saved, ssd_kept,
                                          r3(dya), pad, "ssd_bwd", after=stage("late_grads", G).get("_after"))
    G["ssd_dt_bias"], G["ssd_a_log"], G["ssd_d"] = dpar[0:1, :SSD_HEADS], dpar[1:2, :SSD_HEADS], dpar[2:3, :SSD_HEADS]
    G["ssd_norm"] = dnw[0:1]
    dxbc, dcw, dcb = _conv_bwd(r3(xbc), W["ssd_conv_w"], W["ssd_conv_b"], dxact, pad, "conv_bwd")
    G["ssd_conv_w"], G["ssd_conv_b"] = dcw[0:SSD_CONV], dcb[0:1]
    dqfig, dlb, dhn = _hg_bwd(r3(qfig), lbh, nwh, hg_saved, hg_kept, r3(dyb), pad, "hg_bwd",
                              after=stage("after_conv_bwd", dcb).get("_after"))
    G["hg_lower_bound"] = dlb[:, 0:2, :].transpose(1, 0, 2).reshape(2, D_MODEL)
    G["hg_norm"] = dhn[:, 0, :].reshape(1, D_MODEL)

    r2 = lambda t: t.reshape(R, t.shape[-1])
    pieces = [("z", r2(dz)), ("xbc", r2(dxbc)), ("dt", r2(ddtr)), ("qfig", r2(dqfig)), ("gates", dgates)]
    dum = _sum_nt([p for _, p in pieces], [wi[nm] for nm, _ in pieces], "d_mix")
    dwi = {nm: _matmul(um, dpiece, mode="tn", out_dtype=F32, name=f"dw_in_{nm}") for nm, dpiece in pieces}
    dw_qfig = dwi["qfig"].reshape(D_MODEL, HG_HEADS, 4, 128).transpose(0, 2, 1, 3).reshape(D_MODEL, 4 * D_MODEL)
    G["w_in"] = jnp.concatenate([dwi["z"], dwi["xbc"], dwi["dt"][:, :SSD_HEADS], dw_qfig, dwi["gates"]], axis=1)
    dh1, dnw = _rms_bwd(h1, W["mix_norm"], dum, dh2, "mix_norm_bwd", after=stage("w_in_grads", dwi).get("_after"))
    G["mix_norm"] = dnw[0:1]
    (dx, dfirst), dnw, G["ffn1_w_gu"], G["ffn1_w_down"] = _ffn_bwd(
        h0, W["ffn1_norm"], W["ffn1_w_gu"], W["ffn1_w_down"], sv1, dh1, "ffn1",
        lambda dw: stage("ffn1_dw_down", dw).get("_after"), token_seqs=B,
        told=lambda name, t: stage("ffn1_" + name, t).get("_after"))
    G["ffn1_norm"] = dnw[0:1]
    G["meta_tokens"] = jnp.sum(dfirst[:, pad:CHUNK], axis=0)
    return loss, dx, G


def _place():
    return lax.axis_index("x"), lax.axis_index("y"), lax.axis_index("c")


def _other_chips(x, y):
    return [(1 - x, y), (x, 1 - y), (1 - x, 1 - y)]


def _remote(src, dst, ssem, rsem, dev):
    return pltpu.make_async_remote_copy(src_ref=src, dst_ref=dst, send_sem=ssem, recv_sem=rsem,
                                        device_id=dev, device_id_type=MESH)


def _exchange8(buf, name):
    n, w = buf.shape

    def body(x_ref, out_ref, ssem, rsem):
        x, y, c = _place()
        me = 4 * x + 2 * y + c
        out_ref[me] = x_ref[...]
        copies = []
        for k in range(1, 8):
            px = 1 - x if (k >> 2) & 1 else x
            py = 1 - y if (k >> 1) & 1 else y
            pc = 1 - c if k & 1 else c
            cp = _remote(x_ref, out_ref.at[me], ssem.at[k - 1], rsem.at[k - 1], (px, py, pc))
            cp.start()
            copies.append((cp, 4 * px + 2 * py + pc))
        for k, (cp, peer) in enumerate(copies):
            _remote(x_ref, out_ref.at[peer], ssem.at[k], rsem.at[k], (x, y, c)).wait_recv()
        for cp, _ in copies:
            cp.wait_send()

    vm = pl.BlockSpec(memory_space=pltpu.VMEM)
    return pl.pallas_call(
        body, in_specs=[vm], out_specs=vm, out_shape=jax.ShapeDtypeStruct((8, n, w), F32),
        scratch_shapes=[pltpu.SemaphoreType.DMA((7,)), pltpu.SemaphoreType.DMA((7,))], name=name,
    )(buf)


HBM = pltpu.MemorySpace.HBM


def _sequencer(name, collective_id, sems, sent):
    return functools.partial(pl.kernel, mesh=plsc.ScalarSubcoreMesh(axis_name="sequencer", num_cores=1), name=name,
                             scratch_types=sems, compiler_params=pltpu.CompilerParams(collective_id=collective_id),
                             cost_estimate=pl.CostEstimate(flops=0, transcendentals=0, bytes_accessed=2 * sent,
                                                           remote_bytes_transferred=sent))


def _nbytes(arrays):
    return sum(a.size * a.dtype.itemsize for a in arrays)


def _handshake(peers):
    barrier = pltpu.get_barrier_semaphore()
    for peer in peers:
        pl.semaphore_signal(barrier, inc=1, device_id=peer, device_id_type=MESH)
    pl.semaphore_wait(barrier, len(peers))


def _gather_seq(blocks, name, collective_id):
    n = len(blocks)
    half = [s.shape[1] // 2 for s in blocks]
    full = [jax.new_ref(b, memory_space=HBM) for b in blocks]

    @_sequencer(name, collective_id, [pltpu.SemaphoreType.DMA((n, 3))] * 4, _nbytes(blocks) * 3 // 4)
    def launch(ssem, rsem, fssem, frsem):
        x, y, c = _place()
        q = 2 * x + y
        chips = _other_chips(x, y)
        _handshake([(px, py, c) for px, py in chips] + [(x, y, 1 - c)])
        piece = lambda s, qq, cc: full[s].at[qq, pl.ds(cc * half[s], half[s])]
        sends = []
        for j, (px, py) in enumerate(chips):
            for s in range(n):
                cp = _remote(piece(s, q, c), piece(s, q, c), ssem.at[s, j], rsem.at[s, j], (px, py, c))
                cp.start()
                sends.append(cp)
        for j, (px, py) in enumerate(chips):
            for s in range(n):
                got = piece(s, 2 * px + py, c)
                _remote(got, got, ssem.at[s, j], rsem.at[s, j], (px, py, c)).wait_recv()
                cp = _remote(got, got, fssem.at[s, j], frsem.at[s, j], (x, y, 1 - c))
                cp.start()
                sends.append(cp)
        for j, (px, py) in enumerate(chips):
            for s in range(n):
                got = piece(s, 2 * px + py, 1 - c)
                _remote(got, got, fssem.at[s, j], frsem.at[s, j], (x, y, 1 - c)).wait_recv()
        for cp in sends:
            cp.wait_send()

    launch()
    return [r[...] for r in full]


def _share8(buf, name, collective_id):
    n, w = buf.shape
    src = jax.new_ref(buf, memory_space=HBM)
    out = jax.empty_ref(jax.ShapeDtypeStruct((8, n, w), F32), memory_space=HBM)

    @_sequencer(name, collective_id, [pltpu.SemaphoreType.DMA((7,)), pltpu.SemaphoreType.DMA((7,)), pltpu.SemaphoreType.DMA((1,))],
                7 * buf.size * 4)
    def launch(ssem, rsem, lsem):
        x, y, c = _place()
        me = 4 * x + 2 * y + c
        peers = [(1 - x if (k >> 2) & 1 else x, 1 - y if (k >> 1) & 1 else y, 1 - c if k & 1 else c) for k in range(1, 8)]
        _handshake(peers)
        mine = pltpu.make_async_copy(src, out.at[me], lsem.at[0])
        mine.start()
        sends = []
        for k, peer in enumerate(peers):
            cp = _remote(src, out.at[me], ssem.at[k], rsem.at[k], peer)
            cp.start()
            sends.append(cp)
        for k, (px, py, pc) in enumerate(peers):
            slot = out.at[4 * px + 2 * py + pc]
            _remote(slot, slot, ssem.at[k], rsem.at[k], (px, py, pc)).wait_recv()
        for cp in sends:
            cp.wait_send()
        mine.wait()

    launch()
    return out[...]


def _sum_slots(slots, name, after=None):
    _, n, w = slots.shape

    def body(s_ref, o_ref):
        acc = s_ref[0]
        for d in range(1, 8):
            acc = acc + s_ref[d]
        o_ref[...] = acc

    vm = pl.BlockSpec(memory_space=pltpu.VMEM)
    in_specs, args = [vm], [slots]
    if after is not None:
        body = _skip_ref(body, 1)
        args.append(_deps(after))
        in_specs.append(vm)
    return pl.pallas_call(body, in_specs=in_specs, out_specs=vm, out_shape=jax.ShapeDtypeStruct((n, w), F32), name=name)(*args)


def _pair_swap(parts, name, collective_id):
    n = len(parts)
    half = [p.shape[1] // 2 for p in parts]
    src = [jax.new_ref(p, memory_space=HBM) for p in parts]
    got = [jax.empty_ref(jax.ShapeDtypeStruct((p.shape[0], h, p.shape[2]), p.dtype), memory_space=HBM) for p, h in zip(parts, half)]

    @_sequencer(name, collective_id, [pltpu.SemaphoreType.DMA((n,))] * 2, _nbytes(parts) // 2)
    def launch(ssem, rsem):
        x, y, c = _place()
        _handshake([(x, y, 1 - c)])
        copies = []
        for s in range(n):
            cp = _remote(src[s].at[pl.ds(0, parts[s].shape[0]), pl.ds((1 - c) * half[s], half[s])], got[s], ssem.at[s], rsem.at[s], (x, y, 1 - c))
            cp.start()
            copies.append(cp)
        for cp in copies:
            cp.wait_recv()
        for cp in copies:
            cp.wait_send()

    launch()
    return [g[...] for g in got]


def _to_owners(sums, name, collective_id):
    n = len(sums)
    src = [jax.new_ref(s, memory_space=HBM) for s in sums]
    got = [jax.empty_ref(jax.ShapeDtypeStruct(s.shape, s.dtype), memory_space=HBM) for s in sums]

    @_sequencer(name, collective_id, [pltpu.SemaphoreType.DMA((n, 3))] * 2, _nbytes(sums) * 3 // 4)
    def launch(ssem, rsem):
        x, y, c = _place()
        q = 2 * x + y
        chips = _other_chips(x, y)
        _handshake([(px, py, c) for px, py in chips])
        sends = []
        for j, (px, py) in enumerate(chips):
            for s in range(n):
                cp = _remote(src[s].at[2 * px + py], got[s].at[q], ssem.at[s, j], rsem.at[s, j], (px, py, c))
                cp.start()
                sends.append(cp)
        for j, (px, py) in enumerate(chips):
            for s in range(n):
                slot = got[s].at[2 * px + py]
                _remote(slot, slot, ssem.at[s, j], rsem.at[s, j], (px, py, c)).wait_recv()
        for cp in sends:
            cp.wait_send()

    launch()
    return [g[...] for g in got]


def _pair_join(blocks, name, collective_id):
    n = len(blocks)
    out = [jax.new_ref(b, memory_space=HBM) for b in blocks]

    @_sequencer(name, collective_id, [pltpu.SemaphoreType.DMA((n,))] * 2, _nbytes(blocks) // 2)
    def launch(ssem, rsem):
        x, y, c = _place()
        _handshake([(x, y, 1 - c)])
        sends = []
        for s in range(n):
            h = blocks[s].shape[0] // 2
            mine = out[s].at[pl.ds(c * h, h)]
            cp = _remote(mine, mine, ssem.at[s], rsem.at[s], (x, y, 1 - c))
            cp.start()
            sends.append(cp)
        for s in range(n):
            h = blocks[s].shape[0] // 2
            theirs = out[s].at[pl.ds((1 - c) * h, h)]
            _remote(theirs, theirs, ssem.at[s], rsem.at[s], (x, y, 1 - c)).wait_recv()
        for cp in sends:
            cp.wait_send()

    launch()
    return [o[...] for o in out]


WIRE = BF16


def _row_tile(h):
    return _pick(h, (256, 368, 352, 128, 16))


def _add_pair(part, got, c, name, after=None):
    _, h, w = got.shape
    tr = _row_tile(h)
    nt = h // tr

    def body(c_ref, p_ref, g_ref, o_ref):
        o_ref[...] = (p_ref[...] + g_ref[...].astype(F32)).astype(o_ref.dtype)

    in_specs = [pl.BlockSpec((None, tr, w), lambda q, i, c_ref: (q, c_ref[0] * nt + i, 0)),
                pl.BlockSpec((None, tr, w), lambda q, i, c_ref: (q, i, 0))]
    args = [c.reshape(1).astype(jnp.int32), part, got]
    if after is not None:
        body = _skip_ref(body, len(args))
        args.append(_deps(after))
        in_specs.append(_dep_spec(args[-1]))
    return pl.pallas_call(
        body,
        grid_spec=pltpu.PrefetchScalarGridSpec(
            num_scalar_prefetch=1, grid=(got.shape[0], nt), in_specs=in_specs,
            out_specs=pl.BlockSpec((None, tr, w), lambda q, i, c_ref: (q, i, 0))),
        out_shape=jax.ShapeDtypeStruct(got.shape, WIRE), name=name,
        compiler_params=_cparams(("arbitrary", "arbitrary")),
    )(*args)


def _sum_chips(slots, sums, q, c, name, after=None):
    _, h, w = slots.shape
    tr = _row_tile(h)
    nt = h // tr

    def body(s_ref, mine_ref, a_ref, b_ref, d_ref, o_ref):
        o_ref[...] = ((mine_ref[...].astype(F32) + a_ref[...].astype(F32)) + b_ref[...].astype(F32)) + d_ref[...].astype(F32)

    slot = lambda k: pl.BlockSpec((None, tr, w), lambda i, s_ref: (s_ref[1 + k], i, 0))
    scalars = jnp.stack([c, q, (q + 1) % N_CHIPS, (q + 2) % N_CHIPS, (q + 3) % N_CHIPS]).astype(jnp.int32)
    in_specs, args = [slot(0), slot(1), slot(2), slot(3)], [scalars, sums, slots, slots, slots]
    if after is not None:
        body = _skip_ref(body, len(args))
        args.append(_deps(after))
        in_specs.append(_dep_spec(args[-1]))
    return pl.pallas_call(
        body,
        grid_spec=pltpu.PrefetchScalarGridSpec(
            num_scalar_prefetch=1, grid=(nt,), in_specs=in_specs,
            out_specs=pl.BlockSpec((tr, w), lambda i, s_ref: (s_ref[0] * nt + i, 0))),
        out_shape=jax.ShapeDtypeStruct((2 * h, w), F32), name=name,
        compiler_params=_cparams(("arbitrary",)),
    )(*args)


class _Reduce:
    def __init__(self, parts, q, c, tag, first_id, regions=None):
        self.parts, self.q, self.c, self.tag, self.first_id, self.regions = parts, q, c, tag, first_id, regions
        self.got = _pair_swap(parts, f"{tag}_pair_swap", first_id)

    def to_owners(self, after=None):
        self.sums = [_add_pair(p, g, self.c, f"{self.tag}_pair_add{i}", after)
                     for i, (p, g) in enumerate(zip(self.parts, self.got))]
        if self.regions is not None:
            self.sums = self.regions(self.sums)
        self.slots = _to_owners(self.sums, f"{self.tag}_to_owners", self.first_id + 1)
        return self.sums

    def join(self, after=None):
        blocks = [_sum_chips(sl, sm, self.q, self.c, f"{self.tag}_sum_chips{i}", after)
                  for i, (sl, sm) in enumerate(zip(self.slots, self.sums))]
        self.out = _pair_join(blocks, f"{self.tag}_pair_join", self.first_id + 2)
        return blocks


WEIGHTS = ("meta_tokens", "ffn1_norm", "ffn1_w_gu", "ffn1_w_down", "mix_norm", "w_in", "ssd_conv_w", "ssd_conv_b",
           "ssd_dt_bias", "ssd_a_log", "ssd_d", "ssd_norm", "hg_lower_bound", "hg_norm", "w_branch_a", "w_branch_b",
           "w_out", "ffn2_norm", "ffn2_w_gu", "ffn2_w_down", "final_norm")
BIG = ("ffn1_w_gu", "ffn1_w_down", "w_in", "w_branch_a", "w_branch_b", "w_out", "ffn2_w_gu", "ffn2_w_down")
SMALL = tuple(n for n in WEIGHTS if n not in BIG)


def _rows1024(a):
    flat = a.reshape(-1)
    n = -(-flat.shape[0] // 1024) * 1024
    return jnp.pad(flat, (0, n - flat.shape[0])).reshape(-1, 1024)


def kernel(x, meta_tokens, ffn1_norm, ffn1_w_gu, ffn1_w_down, mix_norm, w_in, ssd_conv_w, ssd_conv_b, ssd_dt_bias, ssd_a_log, ssd_d, ssd_norm, hg_lower_bound, hg_norm, w_branch_a, w_branch_b, w_out, ffn2_norm, ffn2_w_gu, ffn2_w_down, final_norm, loss_target, m_meta_tokens, m_ffn1_norm, m_ffn1_w_gu, m_ffn1_w_down, m_mix_norm, m_w_in, m_ssd_conv_w, m_ssd_conv_b, m_ssd_dt_bias, m_ssd_a_log, m_ssd_d, m_ssd_norm, m_hg_lower_bound, m_hg_norm, m_w_branch_a, m_w_branch_b, m_w_out, m_ffn2_norm, m_ffn2_w_gu, m_ffn2_w_down, m_final_norm, v_meta_tokens, v_ffn1_norm, v_ffn1_w_gu, v_ffn1_w_down, v_mix_norm, v_w_in, v_ssd_conv_w, v_ssd_conv_b, v_ssd_dt_bias, v_ssd_a_log, v_ssd_d, v_ssd_norm, v_hg_lower_bound, v_hg_norm, v_w_branch_a, v_w_branch_b, v_w_out, v_ffn2_norm, v_ffn2_w_gu, v_ffn2_w_down, v_final_norm):
    P = dict(zip(WEIGHTS, (meta_tokens, ffn1_norm, ffn1_w_gu, ffn1_w_down, mix_norm, w_in, ssd_conv_w, ssd_conv_b, ssd_dt_bias, ssd_a_log, ssd_d, ssd_norm, hg_lower_bound, hg_norm, w_branch_a, w_branch_b, w_out, ffn2_norm, ffn2_w_gu, ffn2_w_down, final_norm)))
    M = dict(zip(WEIGHTS, (m_meta_tokens, m_ffn1_norm, m_ffn1_w_gu, m_ffn1_w_down, m_mix_norm, m_w_in, m_ssd_conv_w, m_ssd_conv_b, m_ssd_dt_bias, m_ssd_a_log, m_ssd_d, m_ssd_norm, m_hg_lower_bound, m_hg_norm, m_w_branch_a, m_w_branch_b, m_w_out, m_ffn2_norm, m_ffn2_w_gu, m_ffn2_w_down, m_final_norm)))
    V = dict(zip(WEIGHTS, (v_meta_tokens, v_ffn1_norm, v_ffn1_w_gu, v_ffn1_w_down, v_mix_norm, v_w_in, v_ssd_conv_w, v_ssd_conv_b, v_ssd_dt_bias, v_ssd_a_log, v_ssd_d, v_ssd_norm, v_hg_lower_bound, v_hg_norm, v_w_branch_a, v_w_branch_b, v_w_out, v_ffn2_norm, v_ffn2_w_gu, v_ffn2_w_down, v_final_norm)))
    cx, cy, cc = _place()
    q = 2 * cx + cy

    mine = jnp.concatenate([meta_tokens.reshape(4, 1024), ssd_conv_w.reshape(2, 1024), jnp.zeros((2, 1024), F32)], axis=0)
    every = _exchange8(mine, "gather_small")
    meta_full = jnp.concatenate([every[2 * k, 0:4].reshape(N_META, 256) for k in range(N_CHIPS)], axis=1)
    conv_w_full = jnp.concatenate([every[2 * k, 4:6].reshape(SSD_CONV, 512) for k in range(N_CHIPS)], axis=1)

    late = ("ffn2_w_down", "w_branch_a", "w_branch_b", "w_out")
    rows = jnp.concatenate([P[n][0] for n in late], axis=0)
    zero = lambda t, dtype=F32: (t[0:1, 0:1] * 0).astype(dtype)

    def in_slot(s, after=None):
        s = s if after is None else s + zero(after)
        return lax.dynamic_update_slice(lax.empty((N_CHIPS,) + s.shape, BF16), s.astype(BF16)[None], (q, 0, 0))

    gu1, down1 = _gather_seq([in_slot(ffn1_w_gu[0]), in_slot(ffn1_w_down[0])], "gather_ffn1", 1)
    W = {n: P[n] for n in SMALL}
    W["meta_tokens"], W["ssd_conv_w"] = meta_full, conv_w_full
    W["ffn1_w_gu"], W["ffn1_w_down"] = gu1, down1.reshape(-1, D_MODEL)
    flying = {}

    def stage(name, t):
        if name == "ffn1_norm":
            flying["w_in"] = _gather_seq([in_slot(w_in[0], t)], "gather_w_in", 2)
            return {}
        if name == "ffn1_out":
            flying["late"] = _gather_seq([in_slot(ffn2_w_gu[0], t), in_slot(rows, t)], "gather_late", 3)
            (w_in_all,) = flying["w_in"]
            w_in_all = w_in_all + zero(t, BF16)
            return {"w_in": _split_w_in(w_in_all.transpose(1, 0, 2).reshape(D_MODEL, -1))}
        if name == "mixers_out":
            gu2, rows_all = flying["late"]
            out, r = {"ffn2_w_gu": gu2}, 0
            for n in late:
                nr = P[n].shape[1]
                out[n] = (rows_all[:, r:r + nr] + zero(t, BF16)).reshape(N_CHIPS * nr, D_MODEL)
                r += nr
            return out
        if name == "late_grads":
            parts = [t["ffn2_w_gu"]] + [t[n].reshape(N_CHIPS, -1, D_MODEL) for n in late]
            flying["grad_late"] = _Reduce(parts, q, cc, "grad_late", 4)
            return {"_after": [t["ffn2_w_gu"]] + [t[n] for n in late]}
        if name == "after_conv_bwd":
            return {"_after": flying["grad_late"].to_owners(after=t)}
        if name == "w_in_grads":
            order = ("z", "xbc", "dt", "qfig", "gates")
            blocks = flying["grad_late"].join(after=[t[k] for k in order])

            def regions(sums):
                z, xbc, dt, qfig, gates = [s[0] for s in sums]
                h = z.shape[0]
                qfig = qfig.reshape(h, HG_HEADS, 4, 128).transpose(0, 2, 1, 3).reshape(h, 4 * D_MODEL)
                cols = jnp.concatenate([z, xbc, dt[:, :SSD_HEADS], qfig, gates], axis=1)
                return [cols.reshape(h, N_CHIPS, -1).transpose(1, 0, 2)]

            flying["grad_w_in"] = _Reduce([t[k][None] for k in order], q, cc, "grad_w_in", 7, regions)
            return {"_after": blocks}
        if name == "ffn1_dw_down":
            return {"_after": flying["grad_w_in"].to_owners(after=t)}
        if name == "ffn1_dw":
            dw_gu, dw_down = t
            flying["grad_ffn1"] = _Reduce([dw_gu, dw_down.reshape(N_CHIPS, -1, D_MODEL)], q, cc, "grad_ffn1", 10)
            return {}
        if name == "ffn1_d_norm":
            blocks = flying["grad_w_in"].join(after=t)
            return {"_after": flying["grad_ffn1"].to_owners(after=blocks)}
        return {}

    W["_stage"] = stage

    loss8, grad_x, G = _local_step(x, loss_target, W)

    small = jnp.concatenate(
        [G["meta_tokens"]] + [_rows1024(G[n]) for n in SMALL if n != "meta_tokens"] + [_rows1024(loss8[0:1, 0:1])], axis=0)
    small = jnp.pad(small, ((0, 40 - small.shape[0]), (0, 0)))
    small_slots = _share8(small, "share_small", 13)

    grad_ffn1 = flying["grad_ffn1"]
    going = grad_ffn1.sums
    (g_w_in,) = flying["grad_w_in"].out
    Gb = dict(zip(("ffn2_w_gu",) + late, flying["grad_late"].out))
    Gb["w_in"] = g_w_in

    grads, delta, new_m, new_v, done = {}, {}, {}, {}, []
    cols = w_in.shape[2]
    to_tiles = lambda a: a.transpose(2, 0, 1).reshape(cols, 8, 128).reshape(cols * 8, 128)
    from_tiles = lambda a: a.reshape(cols, 1, D_MODEL).transpose(1, 2, 0)
    for n in [n for n in BIG if n in Gb]:
        if n == "w_in":
            g_t = to_tiles(Gb[n][None])
            d_, m_, v_ = _adamw(to_tiles(P[n]), g_t, to_tiles(M[n]), to_tiles(V[n]), f"adamw_{n}", after=going)
            grads[n], delta[n], new_m[n], new_v[n] = from_tiles(g_t), from_tiles(d_), from_tiles(m_), from_tiles(v_)
        else:
            d_, m_, v_ = _adamw(P[n][0], Gb[n], M[n][0], V[n][0], f"adamw_{n}", after=going)
            grads[n], delta[n], new_m[n], new_v[n] = Gb[n][None], d_[None], m_[None], v_[None]
        done.append(d_)

    small = _sum_slots(small_slots, "sum_small", after=done)
    Gs = {"meta_tokens": small[0:N_META]}
    r = N_META
    for n in SMALL:
        if n == "meta_tokens":
            continue
        nr = -(-G[n].size // 1024)
        Gs[n] = small[r:r + nr].reshape(-1)[:G[n].size].reshape(G[n].shape)
        r += nr
    loss = small[r, 0]
    Gs["meta_tokens"] = lax.dynamic_slice(Gs["meta_tokens"], (0, 256 * q), (N_META, 256))
    Gs["ssd_conv_w"] = lax.dynamic_slice(Gs["ssd_conv_w"], (0, 512 * q), (SSD_CONV, 512))[None]
    Gs = {n: Gs[n].reshape(P[n].shape) for n in SMALL}
    grads.update(Gs)
    flat = lambda a: a.reshape(-1, a.shape[-1])
    d_s, m_s, v_s = _adamw_many(*[[flat(D[n]) for n in SMALL] for D in (P, Gs, M, V)], "adamw_small")
    for i, n in enumerate(SMALL):
        delta[n], new_m[n], new_v[n] = d_s[i].reshape(P[n].shape), m_s[i].reshape(P[n].shape), v_s[i].reshape(P[n].shape)
    done.append(d_s[0])
    grad_ffn1.join(after=done)
    Gb["ffn1_w_gu"], Gb["ffn1_w_down"] = grad_ffn1.out
    for n in ("ffn1_w_gu", "ffn1_w_down"):
        d_, m_, v_ = _adamw(P[n][0], Gb[n], M[n][0], V[n][0], f"adamw_{n}")
        grads[n], delta[n], new_m[n], new_v[n] = Gb[n][None], d_[None], m_[None], v_[None]
    return (loss, grad_x, *[grads[n] for n in WEIGHTS], *[delta[n] for n in WEIGHTS],
            *[new_m[n] for n in WEIGHTS], *[new_v[n] for n in WEIGHTS])
```

```python
import functools

import jax
import jax.numpy as jnp
from jax import lax
from jax.experimental import pallas as pl
from jax.experimental.pallas import tpu as pltpu
from jax.experimental.pallas import tpu_sc as plsc

F32 = jnp.float32
BF16 = jnp.bfloat16
MESH = pl.DeviceIdType.MESH

D_MODEL = 1024
N_META = 16
EPS = 1e-6
SSD_HEADS = 16
SSD_HEAD_DIM = 64
SSD_INNER = 1024
SSD_GROUPS = 4
SSD_CONV = 4
HG_HEADS = 8
HG_SUB = 32
CHUNK = 128
D_FF = 2816
N_CHIPS = 4
IN_SIZES = (1024, 2048, 16, 1024, 1024, 1024, 1024, 1024, 1024)
ADAM_LR = 0.001
ADAM_B1 = 0.9
ADAM_B2 = 0.999
ADAM_EPS = 1e-08
ADAM_WD = 0.01
ADAM_STEP = 10
VMEM_LIMIT = 56 * 1024 * 1024
MATMUL_BLOCK_BYTES = 44 * 1024 * 1024
ADAMW_BLOCK_BYTES = 5 * 512 * 1024


def _cparams(sem=None):
    return pltpu.CompilerParams(dimension_semantics=sem, vmem_limit_bytes=VMEM_LIMIT)


def _pick(n, cands):
    for c in cands:
        if n % c == 0:
            return c
    return n


def _deps(after):
    xs = after if isinstance(after, (list, tuple)) else [after]
    one = lambda x: lax.slice(x, (0,) * x.ndim, (1,) * x.ndim).reshape(1).astype(F32)
    return jnp.concatenate([one(x) for x in xs]).reshape(1, -1)


def _dep_spec(dep):
    return pl.BlockSpec(dep.shape, lambda *_: (0, 0))


def _skip_ref(body, pos):
    return lambda *refs: body(*refs[:pos], *refs[pos + 1:])


def _dg(a, b, ca, cb):
    return lax.dot_general(a.astype(BF16), b.astype(BF16), (((ca,), (cb,)), ((), ())), preferred_element_type=F32)


@jax.custom_vjp
def _mm(a, b):
    return _dg(a, b, 1, 0)


def _mm_fwd(a, b):
    return _dg(a, b, 1, 0), (a, b)


def _mm_bwd(r, g):
    a, b = r
    return _dg(g, b, 1, 1), _dg(a, g, 0, 0)


_mm.defvjp(_mm_fwd, _mm_bwd)


@jax.custom_vjp
def _mm_nt(a, b):
    return _dg(a, b, 1, 1)


def _mm_nt_fwd(a, b):
    return _dg(a, b, 1, 1), (a, b)


def _mm_nt_bwd(r, g):
    a, b = r
    return _dg(g, b, 1, 0), _dg(g, a, 0, 0)


_mm_nt.defvjp(_mm_nt_fwd, _mm_nt_bwd)


@jax.custom_vjp
def _mm_tn(a, b):
    return _dg(a, b, 0, 0)


def _mm_tn_fwd(a, b):
    return _dg(a, b, 0, 0), (a, b)


def _mm_tn_bwd(r, g):
    a, b = r
    return _dg(b, g, 1, 1), _dg(a, g, 1, 0)


_mm_tn.defvjp(_mm_tn_fwd, _mm_tn_bwd)


def _tri_sum(x, lower):
    n = x.shape[0]
    ri = lax.broadcasted_iota(jnp.int32, (n, n), 0)
    ci = lax.broadcasted_iota(jnp.int32, (n, n), 1)
    tri = ((ri >= ci) if lower else (ri <= ci)).astype(BF16)
    x1 = x.astype(BF16)
    r1 = x - x1.astype(F32)
    x2 = r1.astype(BF16)
    x3 = (r1 - x2.astype(F32)).astype(BF16)
    dot = lambda p: lax.dot_general(tri, p, (((1,), (0,)), ((), ())), preferred_element_type=F32)
    return (dot(x3) + dot(x2)) + dot(x1)


@jax.custom_vjp
def _cumsum_rows(x):
    return _tri_sum(x, True)


_cumsum_rows.defvjp(lambda x: (_tri_sum(x, True), None), lambda _, g: (_tri_sum(g, False),))


def _silu(x):
    return x * jax.nn.sigmoid(x)


def _softplus(x):
    return jnp.maximum(x, 0.0) + jnp.log(1.0 + jnp.exp(-jnp.abs(x)))


def _tril(n):
    ri = lax.broadcasted_iota(jnp.int32, (n, n), 0)
    ci = lax.broadcasted_iota(jnp.int32, (n, n), 1)
    return ri >= ci


def _row_of(m, r):
    sub = lax.broadcasted_iota(jnp.int32, (m.shape[0], 1), 0)
    return jnp.sum(jnp.where(sub == r, m, 0.0), axis=0, keepdims=True)


def _col_of(m, c):
    lane = lax.broadcasted_iota(jnp.int32, (1, m.shape[1]), 1)
    return jnp.sum(jnp.where(lane == c, m, 0.0), axis=1, keepdims=True)


def _matmul(a, b, *, mode, out_dtype, name, alpha=1.0, res=None, tm=None, tn=None, out_groups=None, after=None):
    b3 = b.ndim == 3
    if mode == "nn":
        M, K = a.shape
        G = b.shape[0] if b3 else 1
        Ng = b.shape[-1]
        N = G * Ng
    elif mode == "nt":
        M, K = a.shape
        G = b.shape[0] if b3 else 1
        N = b.shape[-2]
        Kg = b.shape[-1]
        assert G * Kg == K
    else:
        K, M = a.shape
        N = b.shape[1]
        G = out_groups or 1
        Ng = N // G
    has_res = res is not None
    split_n = (mode == "nn" and b3) or (mode == "tn" and G > 1)
    per_mn = jnp.dtype(out_dtype).itemsize + (res.dtype.itemsize if has_res else 0)
    fits = [(m_ * n_, m_, n_)
            for m_ in (4352, 2176, 1408, 1088, 1024, 544, 512, 256, 128) if M % m_ == 0
            for n_ in (2816, 2048, 1408, 1024, 512, 256, 128) if (Ng if split_n else N) % n_ == 0
            if 2 * (K * m_ * a.dtype.itemsize + K * n_ * b.dtype.itemsize + m_ * n_ * per_mn) + 4 * m_ * n_ <= MATMUL_BLOCK_BYTES]
    _, tm_fit, tn_fit = max(fits)
    tm, tn = tm or tm_fit, tn or tn_fit
    nm, nn_ = M // tm, N // tn
    assert nm * tm == M and nn_ * tn == N, (name, M, N, K, tm, tn)

    if mode == "nn":
        a_spec = pl.BlockSpec((tm, K), lambda i, j: (i, 0))
        if b3:
            ns = Ng // tn
            b_spec = pl.BlockSpec((None, K, tn), lambda i, j: (j // ns, 0, j % ns))
        else:
            b_spec = pl.BlockSpec((K, tn), lambda i, j: (0, j))
        ca, cb = 1, 0
    elif mode == "nt":
        a_spec = pl.BlockSpec((tm, K), lambda i, j: (i, 0))
        if b3:
            b_spec = pl.BlockSpec((G, tn, Kg), lambda i, j: (0, j, 0))
        else:
            b_spec = pl.BlockSpec((tn, K), lambda i, j: (j, 0))
        ca, cb = 1, 1
    else:
        a_spec = pl.BlockSpec((K, tm), lambda i, j: (0, i))
        b_spec = pl.BlockSpec((K, tn), lambda i, j: (0, j))
        ca, cb = 0, 0
    if mode == "tn" and G > 1:
        ns = Ng // tn
        o_spec = pl.BlockSpec((None, tm, tn), lambda i, j: (j // ns, i, j % ns))
        out_shape = jax.ShapeDtypeStruct((G, M, Ng), out_dtype)
    else:
        o_spec = pl.BlockSpec((tm, tn), lambda i, j: (i, j))
        out_shape = jax.ShapeDtypeStruct((M, N), out_dtype)
    in_specs = [a_spec, b_spec]
    args = [a, b]
    if has_res:
        in_specs.append(pl.BlockSpec((tm, tn), lambda i, j: (i, j)))
        args.append(res)
    if after is not None:
        args.append(_deps(after))
        in_specs.append(_dep_spec(args[-1]))

    def body(*refs):
        a_ref, b_ref, o_ref = refs[0], refs[1], refs[-1]
        if mode == "nt" and b3:
            o = _dg(a_ref[:, 0:Kg], b_ref[0], ca, cb)
            for g in range(1, G):
                o = o + _dg(a_ref[:, g * Kg:(g + 1) * Kg], b_ref[g], ca, cb)
        else:
            o = _dg(a_ref[...], b_ref[...], ca, cb)
        if alpha != 1.0:
            o = o * alpha
        if has_res:
            o = o + refs[2][...]
        o_ref[...] = o.astype(o_ref.dtype)

    return pl.pallas_call(
        body, grid=(nm, nn_), in_specs=in_specs, out_specs=o_spec, out_shape=out_shape, name=name,
        compiler_params=_cparams(("parallel", "parallel")),
    )(*args)


def _sum_nt(xs, ws, name):
    R, N = xs[0].shape[0], ws[0].shape[0]
    n = len(xs)
    per_m = sum(x.shape[1] * x.dtype.itemsize for x in xs)
    per_n = sum(w.shape[1] * w.dtype.itemsize for w in ws)
    fits = [(m_ * n_, m_, n_) for m_ in (1088, 544, 256, 128) if R % m_ == 0 for n_ in (1024, 512, 256, 128) if N % n_ == 0
            if 2 * (m_ * per_m + n_ * per_n + m_ * n_ * 4) + 4 * m_ * n_ <= MATMUL_BLOCK_BYTES]
    _, tm, tn = max(fits)

    def body(*refs):
        o = _dg(refs[0][...], refs[n][...], 1, 1)
        for p in range(1, n):
            o = o + _dg(refs[p][...], refs[n + p][...], 1, 1)
        refs[-1][...] = o

    return pl.pallas_call(
        body, grid=(R // tm, N // tn),
        in_specs=[pl.BlockSpec((tm, x.shape[1]), lambda i, j: (i, 0)) for x in xs]
        + [pl.BlockSpec((tn, w.shape[1]), lambda i, j: (j, 0)) for w in ws],
        out_specs=pl.BlockSpec((tm, tn), lambda i, j: (i, j)), out_shape=jax.ShapeDtypeStruct((R, N), F32), name=name,
        compiler_params=_cparams(("parallel", "parallel")),
    )(*xs, *ws)


def _rms_fn(h, w):
    r = lax.rsqrt(jnp.mean(h * h, axis=-1, keepdims=True) + EPS)
    return h * r * w


def _swiglu_fn(gu):
    g = gu[:, :D_FF].astype(F32)
    u = gu[:, D_FF:].astype(F32)
    return _silu(g) * u


def _merge_fn(pa, pb, gates):
    return jax.nn.sigmoid(gates[:, :D_MODEL]) * pa + jax.nn.sigmoid(gates[:, D_MODEL:]) * pb


def _rows_call(body, *, rows, tr, ins, outs, accs=(), name, after=None):
    n = rows // tr
    assert n * tr == rows
    if after is not None:
        body = _skip_ref(body, len(ins))
        ins = list(ins) + [("full", _deps(after))]

    def spec(x):
        if isinstance(x, tuple):
            shp = x[1].shape
            return pl.BlockSpec(shp, lambda i: (0,) * len(shp), pipeline_mode=pl.Buffered(1))
        return pl.BlockSpec((tr, x.shape[1]), lambda i: (i, 0))

    in_specs = [spec(x) for x in ins]
    args = [x[1] if isinstance(x, tuple) else x for x in ins]
    out_specs = [spec(x) for x in outs] + [pl.BlockSpec(x.shape, lambda i: (0,) * len(x.shape)) for x in accs]
    out_shape = [x[1] if isinstance(x, tuple) else x for x in outs] + list(accs)
    return pl.pallas_call(
        body, grid=(n,), in_specs=in_specs, out_specs=out_specs, out_shape=out_shape, name=name,
        compiler_params=_cparams(("arbitrary",)),
    )(*args)


def _acc_rows(ref, val):
    @pl.when(pl.program_id(0) == 0)
    def _():
        ref[...] = jnp.zeros_like(ref)

    ref[0:1, :] += val


def _rms_fwd(h, w, name):
    def body(h_ref, w_ref, o_ref):
        o_ref[...] = _rms_fn(h_ref[...], w_ref[...]).astype(o_ref.dtype)

    R = h.shape[0]
    return _rows_call(body, rows=R, tr=_pick(R, (256, 128)), ins=[h, ("full", w)],
                      outs=[jax.ShapeDtypeStruct(h.shape, BF16)], name=name)[0]


def _rms_bwd(h, w, dn, dres, name, after=None):
    def body(h_ref, w_ref, dn_ref, dres_ref, dh_ref, dw_ref):
        _, vjp = jax.vjp(_rms_fn, h_ref[...], w_ref[...])
        dh, dw = vjp(dn_ref[...].astype(F32))
        dh_ref[...] = dh + dres_ref[...]
        _acc_rows(dw_ref, dw)

    R = h.shape[0]
    return _rows_call(body, rows=R, tr=_pick(R, (256, 128)), ins=[h, ("full", w), dn, dres],
                      outs=[jax.ShapeDtypeStruct(h.shape, F32)], accs=[jax.ShapeDtypeStruct((8, D_MODEL), F32)], name=name,
                      after=after)


def _d_norm_in(dgu, w_gu, h, norm_w, dres, name, after=None):
    R = h.shape[0]
    G, _, kg = w_gu.shape

    def body(dgu_ref, w_ref, h_ref, nw_ref, dres_ref, dh_ref, dw_ref):
        dn = _dg(dgu_ref[:, 0:kg], w_ref[0], 1, 1)
        for g in range(1, G):
            dn = dn + _dg(dgu_ref[:, kg * g:kg * (g + 1)], w_ref[g], 1, 1)
        _, vjp = jax.vjp(_rms_fn, h_ref[...], nw_ref[...])
        dh, dw = vjp(dn)
        dh_ref[...] = dh + dres_ref[...]
        _acc_rows(dw_ref, dw)

    return _rows_call(body, rows=R, tr=_pick(R, (544, 256, 128)), ins=[dgu, ("full", w_gu), h, ("full", norm_w), dres],
                      outs=[jax.ShapeDtypeStruct(h.shape, F32)], accs=[jax.ShapeDtypeStruct((8, D_MODEL), F32)], name=name,
                      after=after)


def _rms_bwd_tokens(h, w, dn, dres, nseq, name, after=None):
    Tp = h.shape[0] // nseq
    nc = Tp // CHUNK

    def body(h_ref, w_ref, dn_ref, dres_ref, dx_ref, dm_ref, dw_ref):
        b, c = pl.program_id(0), pl.program_id(1)
        _, vjp = jax.vjp(_rms_fn, h_ref[...], w_ref[...])
        dh, dw = vjp(dn_ref[...].astype(F32))
        dh = dh + dres_ref[...]

        @pl.when(c == 0)
        def _():
            dm_ref[...] = dh

        @pl.when(c > 0)
        def _():
            dx_ref[...] = dh

        @pl.when((b == 0) & (c == 0))
        def _():
            dw_ref[...] = jnp.zeros_like(dw_ref)

        dw_ref[0:1, :] += dw

    rows = pl.BlockSpec((CHUNK, D_MODEL), lambda b, c: (b * nc + c, 0))
    in_specs, args = [rows, pl.BlockSpec((1, D_MODEL), lambda b, c: (0, 0)), rows, rows], [h, w, dn, dres]
    if after is not None:
        body = _skip_ref(body, len(args))
        args.append(_deps(after))
        in_specs.append(_dep_spec(args[-1]))
    return pl.pallas_call(
        body, grid=(nseq, nc), in_specs=in_specs,
        out_specs=[pl.BlockSpec((None, CHUNK, D_MODEL), lambda b, c: (b, jnp.maximum(c - 1, 0), 0)),
                   pl.BlockSpec((None, CHUNK, D_MODEL), lambda b, c: (b, 0, 0)),
                   pl.BlockSpec((8, D_MODEL), lambda b, c: (0, 0))],
        out_shape=[jax.ShapeDtypeStruct((nseq, Tp - CHUNK, D_MODEL), F32), jax.ShapeDtypeStruct((nseq, CHUNK, D_MODEL), F32),
                   jax.ShapeDtypeStruct((8, D_MODEL), F32)],
        name=name, compiler_params=_cparams(("arbitrary", "arbitrary")),
    )(*args)


def _gu_swiglu(n, w_gu, name):
    R = n.shape[0]
    G, _, ng = w_gu.shape

    def body(n_ref, w_ref, gu_ref, a_ref):
        x = n_ref[...]
        for r in range(G):
            gu_ref[:, ng * r:ng * (r + 1)] = _dg(x, w_ref[r], 1, 0).astype(gu_ref.dtype)
        a_ref[...] = _swiglu_fn(gu_ref[...]).astype(a_ref.dtype)

    return _rows_call(body, rows=R, tr=_pick(R, (256, 128)), ins=[n, ("full", w_gu)],
                      outs=[jax.ShapeDtypeStruct((R, 2 * D_FF), BF16), jax.ShapeDtypeStruct((R, D_FF), BF16)], name=name)


def _d_swiglu(dout, w_down, gu, alpha, name):
    R = gu.shape[0]

    def body(do_ref, w_ref, gu_ref, o_ref):
        da = _dg(do_ref[...] * alpha, w_ref[...], 1, 1)
        g = gu_ref[:, :D_FF].astype(F32)
        u = gu_ref[:, D_FF:].astype(F32)
        s = jax.nn.sigmoid(g)
        t = g * s
        o_ref[:, :D_FF] = (da * u * (s + t - t * s)).astype(o_ref.dtype)
        o_ref[:, D_FF:] = (da * t).astype(o_ref.dtype)

    return _rows_call(body, rows=R, tr=_pick(R, (256, 128)), ins=[dout, ("full", w_down), gu],
                      outs=[jax.ShapeDtypeStruct(gu.shape, BF16)], name=name)[0]


def _residual_matmul(a, w, res, alpha, name, norm_w=None):
    R, K = a.shape

    def body(a_ref, w_ref, r_ref, *rest):
        out = r_ref[...] + alpha * _dg(a_ref[...], w_ref[...], 1, 0)
        if norm_w is None:
            rest[0][...] = out
        else:
            rest[1][...] = out
            rest[2][...] = _rms_fn(out, rest[0][...]).astype(rest[2].dtype)

    f32 = jax.ShapeDtypeStruct((R, D_MODEL), F32)
    ins = [a, ("full", w), res] + ([] if norm_w is None else [("full", norm_w)])
    outs = [f32] + ([] if norm_w is None else [jax.ShapeDtypeStruct((R, D_MODEL), BF16)])
    got = _rows_call(body, rows=R, tr=_pick(R, (544, 256, 128)), ins=ins, outs=outs, name=name)
    return got[0] if norm_w is None else (got[0], got[1])


def _branch_merge(ya, yb, wa, wb, gates, name):
    def body(ya_ref, yb_ref, wa_ref, wb_ref, g_ref, pa_ref, pb_ref, o_ref):
        pa = _dg(ya_ref[...], wa_ref[...], 1, 0)
        pb = _dg(yb_ref[...], wb_ref[...], 1, 0)
        pa_ref[...] = pa
        pb_ref[...] = pb
        o_ref[...] = _merge_fn(pa, pb, g_ref[...].astype(F32)).astype(o_ref.dtype)

    R = ya.shape[0]
    f32 = jax.ShapeDtypeStruct((R, D_MODEL), F32)
    return _rows_call(body, rows=R, tr=_pick(R, (544, 256, 128)), ins=[ya, yb, ("full", wa), ("full", wb), gates],
                      outs=[f32, f32, jax.ShapeDtypeStruct((R, D_MODEL), BF16)], name=name)


def _branch_merge_bwd(pa, pb, gates, dm, wa, wb, name):
    def body(pa_ref, pb_ref, g_ref, dm_ref, wa_ref, wb_ref, dpa_ref, dpb_ref, dg_ref, dya_ref, dyb_ref):
        _, vjp = jax.vjp(_merge_fn, pa_ref[...], pb_ref[...], g_ref[...].astype(F32))
        dpa, dpb, dg = vjp(dm_ref[...].astype(F32))
        dpa_ref[...] = dpa.astype(dpa_ref.dtype)
        dpb_ref[...] = dpb.astype(dpb_ref.dtype)
        dg_ref[...] = dg.astype(dg_ref.dtype)
        dya_ref[...] = _dg(dpa, wa_ref[...], 1, 1).astype(dya_ref.dtype)
        dyb_ref[...] = _dg(dpb, wb_ref[...], 1, 1).astype(dyb_ref.dtype)

    R = pa.shape[0]
    b16 = jax.ShapeDtypeStruct(pa.shape, BF16)
    return _rows_call(body, rows=R, tr=_pick(R, (544, 256, 128)), ins=[pa, pb, gates, dm, ("full", wa), ("full", wb)],
                      outs=[b16, b16, jax.ShapeDtypeStruct(gates.shape, BF16), b16, b16], name=name)


def _loss_head(h3, w, target, nseq, name):
    Tp = h3.shape[0] // nseq
    nc = Tp // CHUNK

    def fn(h, w_, t, valid):
        y = _rms_fn(h, w_)
        e = (y - t) * valid
        return 0.5 * jnp.sum(jnp.mean(e * e, axis=-1, keepdims=True))

    def body(h_ref, w_ref, t_ref, loss_ref, dh_ref, dw_ref):
        b, c = pl.program_id(0), pl.program_id(1)
        valid = (c >= 1).astype(F32)
        t = t_ref[...]
        loss, vjp = jax.vjp(lambda h, w_: fn(h, w_, t, valid), h_ref[...], w_ref[...])
        dh, dw = vjp(jnp.ones((), F32))
        dh_ref[...] = dh

        @pl.when((b == 0) & (c == 0))
        def _():
            loss_ref[...] = jnp.zeros_like(loss_ref)
            dw_ref[...] = jnp.zeros_like(dw_ref)

        loss_ref[...] += jnp.full(loss_ref.shape, loss, F32)
        dw_ref[0:1, :] += dw

    return pl.pallas_call(
        body, grid=(nseq, nc),
        in_specs=[pl.BlockSpec((CHUNK, D_MODEL), lambda b, c: (b * nc + c, 0)),
                  pl.BlockSpec((1, D_MODEL), lambda b, c: (0, 0)),
                  pl.BlockSpec((None, CHUNK, D_MODEL), lambda b, c: (b, jnp.maximum(c - 1, 0), 0))],
        out_specs=[pl.BlockSpec((8, 128), lambda b, c: (0, 0)),
                   pl.BlockSpec((CHUNK, D_MODEL), lambda b, c: (b * nc + c, 0)),
                   pl.BlockSpec((8, D_MODEL), lambda b, c: (0, 0))],
        out_shape=[jax.ShapeDtypeStruct((8, 128), F32), jax.ShapeDtypeStruct(h3.shape, F32),
                   jax.ShapeDtypeStruct((8, D_MODEL), F32)],
        name=name, compiler_params=_cparams(("arbitrary", "arbitrary")),
    )(h3, w, target)


CONV_TILE = 512
CONV_HALO = 8


def _conv_fwd(xbc, w, b, pad, name):
    B, Tp, C = xbc.shape
    nch = Tp // CHUNK

    def body(x_ref, w_ref, b_ref, o_ref, xp):
        xp[0:CONV_HALO, :] = jnp.zeros((CONV_HALO, CONV_TILE), F32)
        xp[CONV_HALO:, :] = x_ref[...]
        for c in range(nch):
            acc = jnp.zeros((CHUNK, CONV_TILE), F32) + b_ref[...]
            for k in range(SSD_CONV):
                acc = acc + w_ref[k:k + 1, :] * xp[pl.ds(CONV_HALO + CHUNK * c - (SSD_CONV - 1) + k, CHUNK), :]
            out = _silu(acc)
            if CHUNK * c < pad:
                row = CHUNK * c + lax.broadcasted_iota(jnp.int32, (CHUNK, 1), 0)
                out = jnp.where(row >= pad, out, 0.0)
            o_ref[pl.ds(CHUNK * c, CHUNK), :] = out

    return pl.pallas_call(
        body, grid=(B, C // CONV_TILE),
        in_specs=[pl.BlockSpec((None, Tp, CONV_TILE), lambda i, j: (i, 0, j)),
                  pl.BlockSpec((SSD_CONV, CONV_TILE), lambda i, j: (0, j)),
                  pl.BlockSpec((1, CONV_TILE), lambda i, j: (0, j))],
        out_specs=pl.BlockSpec((None, Tp, CONV_TILE), lambda i, j: (i, 0, j)),
        out_shape=jax.ShapeDtypeStruct(xbc.shape, F32),
        scratch_shapes=[pltpu.VMEM((Tp + CONV_HALO, CONV_TILE), F32)],
        name=name, compiler_params=_cparams(("arbitrary", "arbitrary")),
    )(xbc, w, b)


def _conv_bwd(xbc, w, b, dact, pad, name):
    B, Tp, C = xbc.shape
    nch = Tp // CHUNK

    def body(x_ref, w_ref, b_ref, da_ref, dx_ref, dw_ref, db_ref, xp, dp):
        bi = pl.program_id(1)
        xp[0:CONV_HALO, :] = jnp.zeros((CONV_HALO, CONV_TILE), F32)
        xp[CONV_HALO:, :] = x_ref[...]
        dp[pl.ds(Tp, CONV_HALO), :] = jnp.zeros((CONV_HALO, CONV_TILE), F32)
        dws = [jnp.zeros((1, CONV_TILE), F32) for _ in range(SSD_CONV)]
        dbs = jnp.zeros((1, CONV_TILE), F32)
        for c in range(nch):
            xs = [xp[pl.ds(CONV_HALO + CHUNK * c - (SSD_CONV - 1) + k, CHUNK), :] for k in range(SSD_CONV)]
            acc = jnp.zeros((CHUNK, CONV_TILE), F32) + b_ref[...]
            for k in range(SSD_CONV):
                acc = acc + w_ref[k:k + 1, :] * xs[k]
            sg = jax.nn.sigmoid(acc)
            t = acc * sg
            dpre = da_ref[pl.ds(CHUNK * c, CHUNK), :] * (sg + t - t * sg)
            if CHUNK * c < pad:
                row = CHUNK * c + lax.broadcasted_iota(jnp.int32, (CHUNK, 1), 0)
                dpre = jnp.where(row >= pad, dpre, 0.0)
            dp[pl.ds(CHUNK * c, CHUNK), :] = dpre
            dbs = dbs + jnp.sum(dpre, axis=0, keepdims=True)
            for k in range(SSD_CONV):
                dws[k] = dws[k] + jnp.sum(dpre * xs[k], axis=0, keepdims=True)
        for c in range(nch):
            acc = jnp.zeros((CHUNK, CONV_TILE), F32)
            for k in range(SSD_CONV):
                acc = acc + w_ref[k:k + 1, :] * dp[pl.ds(CHUNK * c + (SSD_CONV - 1) - k, CHUNK), :]
            dx_ref[pl.ds(CHUNK * c, CHUNK), :] = acc.astype(dx_ref.dtype)

        @pl.when(bi == 0)
        def _():
            dw_ref[...] = jnp.zeros_like(dw_ref)
            db_ref[...] = jnp.zeros_like(db_ref)

        for k in range(SSD_CONV):
            dw_ref[k:k + 1, :] += dws[k]
        db_ref[0:1, :] += dbs

    return pl.pallas_call(
        body, grid=(C // CONV_TILE, B),
        in_specs=[pl.BlockSpec((None, Tp, CONV_TILE), lambda j, i: (i, 0, j)),
                  pl.BlockSpec((SSD_CONV, CONV_TILE), lambda j, i: (0, j)),
                  pl.BlockSpec((1, CONV_TILE), lambda j, i: (0, j)),
                  pl.BlockSpec((None, Tp, CONV_TILE), lambda j, i: (i, 0, j))],
        out_specs=[pl.BlockSpec((None, Tp, CONV_TILE), lambda j, i: (i, 0, j)),
                   pl.BlockSpec((8, CONV_TILE), lambda j, i: (0, j)),
                   pl.BlockSpec((8, CONV_TILE), lambda j, i: (0, j))],
        out_shape=[jax.ShapeDtypeStruct(xbc.shape, BF16), jax.ShapeDtypeStruct((8, C), F32),
                   jax.ShapeDtypeStruct((8, C), F32)],
        scratch_shapes=[pltpu.VMEM((Tp + CONV_HALO, CONV_TILE), F32), pltpu.VMEM((Tp + CONV_HALO, CONV_TILE), F32)],
        name=name, compiler_params=_cparams(("arbitrary", "arbitrary")),
    )(xbc, w, b, dact)


def _ssd_chunk(xs, bm, cm, dtr, z, state, dt_bias, a_log, dskip, norm_w, valid, kept=None, keep=False):
    Q = xs.shape[0]
    known = (lambda x, v: x) if kept is None else _known
    lane = lax.broadcasted_iota(jnp.int32, (1, 128), 1)
    dt = jnp.where(lane < SSD_HEADS, _softplus(dtr + dt_bias), 0.0) * valid
    a = dt * (-jnp.exp(a_log))
    tril = _tril(Q)
    cs = known(_cumsum_rows(a), None if kept is None else kept[0])
    cs_t = cs.T
    cs_end = _row_of(cs, Q - 1)
    low = lane < SSD_HEAD_DIM
    low_rows = lax.broadcasted_iota(jnp.int32, (128, 1), 0) < SSD_HEAD_DIM
    ys, new_state, cbs = [], [], []
    for g in range(SSD_GROUPS):
        bg = bm[:, 128 * g:128 * (g + 1)]
        cg = cm[:, 128 * g:128 * (g + 1)]
        cb = known(_mm_nt(cg, bg), None if kept is None else kept[1][Q * g:Q * (g + 1)])
        cbs.append(cb)
        for pr in range(2):
            p = 2 * g + pr
            h0, h1 = 2 * p, 2 * p + 1
            xp = xs[:, 128 * p:128 * (p + 1)]
            c0, c1 = _col_of(cs, h0), _col_of(cs, h1)
            e0, e1 = _col_of(cs_end, h0), _col_of(cs_end, h1)
            xd = xp * jnp.where(low, _col_of(dt, h0), _col_of(dt, h1))
            l0 = jnp.exp(jnp.where(tril, c0 - _row_of(cs_t, h0), -1e30))
            l1 = jnp.exp(jnp.where(tril, c1 - _row_of(cs_t, h1), -1e30))
            y_diag = jnp.where(low, _mm(cb * l0, xd), _mm(cb * l1, xd))
            to_end = jnp.where(low, jnp.exp(e0 - c0), jnp.exp(e1 - c1))
            sp = state[128 * p:128 * (p + 1), :]
            y_off = _mm_nt(cg, sp) * jnp.where(low, jnp.exp(c0), jnp.exp(c1))
            new_state.append(sp * jnp.where(low_rows, jnp.exp(e0), jnp.exp(e1)) + _mm_tn(xd * to_end, bg))
            ys.append(y_diag + y_off + xp * jnp.where(low, _col_of(dskip, h0), _col_of(dskip, h1)))
    y_raw = known(jnp.concatenate(ys, axis=1), None if kept is None else kept[2])
    y = y_raw * _silu(z)
    gw = SSD_INNER // SSD_GROUPS
    outs = []
    for g in range(SSD_GROUPS):
        blk = y[:, gw * g:gw * (g + 1)]
        outs.append(blk * lax.rsqrt(jnp.mean(blk * blk, axis=-1, keepdims=True) + EPS))
    out, state_out = jnp.concatenate(outs, axis=1) * norm_w, jnp.concatenate(new_state, axis=0)
    if kept is not None:
        state_out = _known(state_out, state)
    return (out, state_out, (cs, jnp.concatenate(cbs, axis=0), y_raw)) if keep else (out, state_out)


def _valid_rows(c, pad):
    row = c * CHUNK + lax.broadcasted_iota(jnp.int32, (CHUNK, 1), 0)
    return (row >= pad).astype(F32)


def _ssd_fwd(xact, dtr, z, dt_bias, a_log, dskip, norm_w, pad, name):
    B, Tp, _ = xact.shape
    nc = Tp // CHUNK

    def body(xs_ref, bm_ref, cm_ref, dt_ref, z_ref, db_ref, al_ref, ds_ref, nw_ref, y_ref, save_ref, cs_ref, cb_ref, yr_ref, st):
        c = pl.program_id(1)

        @pl.when(c == 0)
        def _():
            st[...] = jnp.zeros_like(st)

        s0 = st[...]
        save_ref[...] = s0
        y, s1, (cs, cb, y_raw) = _ssd_chunk(xs_ref[...], bm_ref[...], cm_ref[...], dt_ref[...], z_ref[...].astype(F32), s0,
                                            db_ref[...], al_ref[...], ds_ref[...], nw_ref[...], _valid_rows(c, pad), keep=True)
        y_ref[...] = y.astype(y_ref.dtype)
        cs_ref[...] = cs
        cb_ref[...] = cb
        yr_ref[...] = y_raw
        st[...] = s1

    row = lambda w, off=0: pl.BlockSpec((None, CHUNK, w), lambda b, c: (b, c, off))
    par = lambda w: pl.BlockSpec((1, w), lambda b, c: (0, 0))
    per_chunk = lambda r: pl.BlockSpec((None, None, r, 128), lambda b, c: (b, c, 0, 0))
    return pl.pallas_call(
        body, grid=(B, nc),
        in_specs=[row(1024, 0), row(512, 2), row(512, 3), row(128), row(1024), par(128), par(128), par(128), par(1024)],
        out_specs=[row(1024), per_chunk(1024), row(128), per_chunk(SSD_GROUPS * CHUNK), row(1024)],
        out_shape=[jax.ShapeDtypeStruct((B, Tp, SSD_INNER), BF16), jax.ShapeDtypeStruct((B, nc, 1024, 128), F32),
                   jax.ShapeDtypeStruct((B, Tp, 128), F32), jax.ShapeDtypeStruct((B, nc, SSD_GROUPS * CHUNK, 128), F32),
                   jax.ShapeDtypeStruct((B, Tp, SSD_INNER), F32)],
        scratch_shapes=[pltpu.VMEM((1024, 128), F32)],
        name=name, compiler_params=_cparams(("arbitrary", "arbitrary")),
    )(xact, xact, xact, dtr, z, dt_bias, a_log, dskip, norm_w)


def _ssd_bwd(xact, dtr, z, dt_bias, a_log, dskip, norm_w, saved, kept, dy, pad, name, after=None):
    B, Tp, _ = xact.shape
    nc = Tp // CHUNK

    def body(xs_ref, bm_ref, cm_ref, dt_ref, z_ref, db_ref, al_ref, ds_ref, nw_ref, sv_ref, cs_ref, cb_ref, yr_ref, dy_ref,
             dx_ref, ddt_ref, dz_ref, dpar_ref, dnw_ref, dst):
        b, i = pl.program_id(0), pl.program_id(1)
        c = nc - 1 - i

        @pl.when(i == 0)
        def _():
            dst[...] = jnp.zeros_like(dst)

        valid = _valid_rows(c, pad)
        kept_c = (cs_ref[...], cb_ref[...], yr_ref[...])
        fn = lambda *a: _ssd_chunk(*a, valid, kept=kept_c)
        _, vjp = jax.vjp(fn, xs_ref[...], bm_ref[...], cm_ref[...], dt_ref[...], z_ref[...].astype(F32), sv_ref[...],
                         db_ref[...], al_ref[...], ds_ref[...], nw_ref[...])
        dxs, dbm, dcm, ddt, dz, dstate, ddb, dal, dds, dnw = vjp((dy_ref[...].astype(F32), dst[...]))
        dx_ref[:, 0:1024] = dxs
        dx_ref[:, 1024:1536] = dbm
        dx_ref[:, 1536:2048] = dcm
        ddt_ref[...] = ddt
        dz_ref[...] = dz.astype(dz_ref.dtype)
        dst[...] = dstate

        @pl.when((b == 0) & (i == 0))
        def _():
            dpar_ref[...] = jnp.zeros_like(dpar_ref)
            dnw_ref[...] = jnp.zeros_like(dnw_ref)

        dpar_ref[0:1, :] += ddb
        dpar_ref[1:2, :] += dal
        dpar_ref[2:3, :] += dds
        dnw_ref[0:1, :] += dnw

    row = lambda w, off=0: pl.BlockSpec((None, CHUNK, w), lambda b, i: (b, nc - 1 - i, off))
    par = lambda w: pl.BlockSpec((1, w), lambda b, i: (0, 0))
    acc = lambda w: pl.BlockSpec((8, w), lambda b, i: (0, 0))
    per_chunk = lambda r: pl.BlockSpec((None, None, r, 128), lambda b, i: (b, nc - 1 - i, 0, 0))
    in_specs = [row(1024, 0), row(512, 2), row(512, 3), row(128), row(1024), par(128), par(128), par(128), par(1024),
                per_chunk(1024), row(128), per_chunk(SSD_GROUPS * CHUNK), row(1024), row(1024)]
    args = [xact, xact, xact, dtr, z, dt_bias, a_log, dskip, norm_w, saved, kept[0], kept[1], kept[2], dy]
    if after is not None:
        body = _skip_ref(body, len(args))
        args.append(_deps(after))
        in_specs.append(_dep_spec(args[-1]))
    outs = pl.pallas_call(
        body, grid=(B, nc), in_specs=in_specs,
        out_specs=[row(2048), row(128), row(1024), acc(128), acc(1024)],
        out_shape=[jax.ShapeDtypeStruct((B, Tp, 2048), F32), jax.ShapeDtypeStruct((B, Tp, 128), F32),
                   jax.ShapeDtypeStruct((B, Tp, 1024), BF16), jax.ShapeDtypeStruct((8, 128), F32),
                   jax.ShapeDtypeStruct((8, 1024), F32)],
        scratch_shapes=[pltpu.VMEM((1024, 128), F32)],
        name=name, compiler_params=_cparams(("arbitrary", "arbitrary")),
    )(*args)
    return outs


@jax.custom_vjp
def _known(x, value):
    return value


_known.defvjp(lambda x, value: (value, None), lambda _, g: (g, jnp.zeros_like(g)))


def _hg_chunk(qr, fr, ir, gr, state_t, p0, p1, norm_w, valid, kept=None, keep=False):
    Q = qr.shape[0]
    known = (lambda x, i: x) if kept is None else (lambda x, i: _known(x, kept[i].astype(x.dtype)))
    lb = jax.nn.sigmoid(p0 - p1)
    f = lb + (1.0 - lb) * jax.nn.sigmoid(fr)
    k = 1.0 - f
    q = _silu(qr)
    v = ir * valid
    cum = known(_cumsum_rows(jnp.log(f)), 0)
    cum_end = _row_of(cum, Q - 1)
    o_inter = _mm_nt(q * jnp.exp(cum), state_t)
    nblk = Q // HG_SUB
    row = lax.broadcasted_iota(jnp.int32, (Q, 1), 0)
    ri = lax.broadcasted_iota(jnp.int32, (Q, Q), 0)
    ci = lax.broadcasted_iota(jnp.int32, (Q, Q), 1)
    mids = jnp.concatenate([jnp.broadcast_to(_row_of(cum, HG_SUB * i + HG_SUB // 2 - 1), (HG_SUB, cum.shape[1]))
                            for i in range(nblk)], axis=0)
    sh = HG_SUB.bit_length() - 1
    same = (jnp.right_shift(ri, sh) == jnp.right_shift(ci, sh)) & (ri >= ci)
    att = jnp.where(same, _mm_nt(q * jnp.exp(cum - mids), k * jnp.exp(mids - cum)), 0.0)
    qas, kas = [], []
    for i in range(1, nblk):
        lo = HG_SUB * i
        start = _row_of(cum, lo - 1)
        qas.append(q * jnp.exp(jnp.where((row >= lo) & (row < lo + HG_SUB), cum - start, -1e30)))
        kas.append(k * jnp.exp(jnp.where(row < lo, start - cum, -1e30)))
    att = att + _mm_nt(jnp.concatenate(qas, axis=1), jnp.concatenate(kas, axis=1))
    att = known(att, 1)
    o = known(o_inter + _mm(att, v), 2)
    new_state_t = state_t * jnp.exp(cum_end) + _mm_tn(v, k * jnp.exp(cum_end - cum))
    if kept is not None:
        new_state_t = _known(new_state_t, state_t)
    y = o * lax.rsqrt(jnp.mean(o * o, axis=-1, keepdims=True) + EPS) * norm_w * _silu(gr)
    return (y, new_state_t, (cum, att, o)) if keep else (y, new_state_t)


HG_PER_STEP = 8
HG_COLS = 4 * 128


def _hg_fwd(qfig, lbh, nwh, pad, name):
    B, Tp, _ = qfig.shape
    nc = Tp // CHUNK
    hp = HG_PER_STEP

    def body(x_ref, lb_ref, nw_ref, y_ref, save_ref, cum_ref, att_ref, o_ref, st):
        c = pl.program_id(1)

        @pl.when(c == 0)
        def _():
            st[...] = jnp.zeros_like(st)

        valid = _valid_rows(c, pad)
        for j in range(hp):
            for b in range(B):
                s0 = st[j, b]
                save_ref[j, b] = s0
                col = lambda k: x_ref[b, :, HG_COLS * j + 128 * k:HG_COLS * j + 128 * (k + 1)]
                y, s1, (cum, att, o) = _hg_chunk(col(0), col(1), col(2), col(3), s0, lb_ref[j, 0:1, :], lb_ref[j, 1:2, :],
                                                 nw_ref[j], valid, keep=True)
                y_ref[b, :, 128 * j:128 * (j + 1)] = y.astype(y_ref.dtype)
                cum_ref[b, :, 128 * j:128 * (j + 1)] = cum
                att_ref[j, b] = att.astype(att_ref.dtype)
                o_ref[b, :, 128 * j:128 * (j + 1)] = o
                st[j, b] = s1

    rows = pl.BlockSpec((B, CHUNK, 128 * hp), lambda h, c: (0, c, h))
    per_chunk = pl.BlockSpec((hp, B, None, 128, 128), lambda h, c: (h, 0, c, 0, 0))
    return pl.pallas_call(
        body, grid=(HG_HEADS // hp, nc),
        in_specs=[pl.BlockSpec((B, CHUNK, HG_COLS * hp), lambda h, c: (0, c, h)),
                  pl.BlockSpec((hp, 2, 128), lambda h, c: (h, 0, 0)),
                  pl.BlockSpec((hp, 1, 128), lambda h, c: (h, 0, 0))],
        out_specs=[rows, per_chunk, rows, per_chunk, rows],
        out_shape=[jax.ShapeDtypeStruct((B, Tp, 1024), BF16), jax.ShapeDtypeStruct((HG_HEADS, B, nc, 128, 128), F32),
                   jax.ShapeDtypeStruct((B, Tp, 1024), F32), jax.ShapeDtypeStruct((HG_HEADS, B, nc, 128, 128), BF16),
                   jax.ShapeDtypeStruct((B, Tp, 1024), F32)],
        scratch_shapes=[pltpu.VMEM((hp, B, 128, 128), F32)],
        name=name, compiler_params=_cparams(("arbitrary", "arbitrary")),
    )(qfig, lbh, nwh)


def _hg_bwd(qfig, lbh, nwh, saved, kept, dy, pad, name, after=None):
    B, Tp, _ = qfig.shape
    nc = Tp // CHUNK
    hp = HG_PER_STEP

    def body(x_ref, lb_ref, nw_ref, sv_ref, cum_ref, att_ref, o_ref, dy_ref, dx_ref, dlb_ref, dnw_ref, dst):
        i = pl.program_id(1)
        c = nc - 1 - i

        @pl.when(i == 0)
        def _():
            dst[...] = jnp.zeros_like(dst)
            dlb_ref[...] = jnp.zeros_like(dlb_ref)
            dnw_ref[...] = jnp.zeros_like(dnw_ref)

        valid = _valid_rows(c, pad)
        for j in range(hp):
            for b in range(B):
                col = lambda k: x_ref[b, :, HG_COLS * j + 128 * k:HG_COLS * j + 128 * (k + 1)]
                head = slice(128 * j, 128 * (j + 1))
                kept_jb = (cum_ref[b, :, head], att_ref[j, b], o_ref[b, :, head])
                fn = lambda *a: _hg_chunk(*a, valid, kept=kept_jb)
                _, vjp = jax.vjp(fn, col(0), col(1), col(2), col(3), sv_ref[j, b], lb_ref[j, 0:1, :], lb_ref[j, 1:2, :], nw_ref[j])
                d4 = vjp((dy_ref[b, :, 128 * j:128 * (j + 1)].astype(F32), dst[j, b]))
                for k in range(4):
                    dx_ref[b, :, HG_COLS * j + 128 * k:HG_COLS * j + 128 * (k + 1)] = d4[k].astype(dx_ref.dtype)
                dst[j, b] = d4[4]
                dlb_ref[j, 0:1, :] += d4[5]
                dlb_ref[j, 1:2, :] += d4[6]
                dnw_ref[j, 0:1, :] += d4[7]

    acc = pl.BlockSpec((hp, 8, 128), lambda h, i: (h, 0, 0))
    rows = pl.BlockSpec((B, CHUNK, 128 * hp), lambda h, i: (0, nc - 1 - i, h))
    per_chunk = pl.BlockSpec((hp, B, None, 128, 128), lambda h, i: (h, 0, nc - 1 - i, 0, 0))
    in_specs = [pl.BlockSpec((B, CHUNK, HG_COLS * hp), lambda h, i: (0, nc - 1 - i, h)),
                pl.BlockSpec((hp, 2, 128), lambda h, i: (h, 0, 0)),
                pl.BlockSpec((hp, 1, 128), lambda h, i: (h, 0, 0)),
                per_chunk, rows, per_chunk, rows, rows]
    args = [qfig, lbh, nwh, saved, kept[0], kept[1], kept[2], dy]
    if after is not None:
        body = _skip_ref(body, len(args))
        args.append(_deps(after))
        in_specs.append(_dep_spec(args[-1]))
    return pl.pallas_call(
        body, grid=(HG_HEADS // hp, nc), in_specs=in_specs,
        out_specs=[pl.BlockSpec((B, CHUNK, HG_COLS * hp), lambda h, i: (0, nc - 1 - i, h)), acc, acc],
        out_shape=[jax.ShapeDtypeStruct((B, Tp, 4096), BF16), jax.ShapeDtypeStruct((HG_HEADS, 8, 128), F32),
                   jax.ShapeDtypeStruct((HG_HEADS, 8, 128), F32)],
        scratch_shapes=[pltpu.VMEM((hp, B, 128, 128), F32)],
        name=name, compiler_params=_cparams(("arbitrary", "arbitrary")),
    )(*args)


def _adamw_math(w, g, m, v):
    m = ADAM_B1 * m + (1.0 - ADAM_B1) * g
    v = ADAM_B2 * v + (1.0 - ADAM_B2) * (g * g)
    m_hat = m / (1.0 - ADAM_B1 ** ADAM_STEP)
    v_hat = v / (1.0 - ADAM_B2 ** ADAM_STEP)
    return -ADAM_LR * (m_hat / (jnp.sqrt(v_hat) + ADAM_EPS) + ADAM_WD * w), m, v


def _adamw_many(ws, gs, ms, vs, name):
    n = len(ws)

    def body(*refs):
        for i in range(n):
            d, m, v = _adamw_math(refs[i][...], refs[n + i][...], refs[2 * n + i][...], refs[3 * n + i][...])
            refs[4 * n + i][...] = d
            refs[5 * n + i][...] = m
            refs[6 * n + i][...] = v

    vm = pl.BlockSpec(memory_space=pltpu.VMEM)
    outs = pl.pallas_call(body, in_specs=[vm] * (4 * n), out_specs=[vm] * (3 * n),
                          out_shape=[jax.ShapeDtypeStruct(w.shape, F32) for w in ws] * 3, name=name)(*ws, *gs, *ms, *vs)
    return outs[:n], outs[n:2 * n], outs[2 * n:]


def _adamw(w, g, m, v, name, after=None):
    R, C = w.shape
    tr = max(t for t in range(8, R + 1, 8) if R % t == 0 and (t * C * 4 <= ADAMW_BLOCK_BYTES or t == 8))

    def body(w_ref, g_ref, m_ref, v_ref, d_ref, mo_ref, vo_ref):
        d_ref[...], mo_ref[...], vo_ref[...] = _adamw_math(w_ref[...], g_ref[...], m_ref[...], v_ref[...])

    sp = pl.BlockSpec((tr, C), lambda i: (i, 0))
    sh = jax.ShapeDtypeStruct((R, C), F32)
    in_specs, args = [sp] * 4, [w, g, m, v]
    if after is not None:
        body = _skip_ref(body, len(args))
        args.append(_deps(after))
        in_specs.append(_dep_spec(args[-1]))
    return pl.pallas_call(body, grid=(R // tr,), in_specs=in_specs, out_specs=[sp] * 3, out_shape=[sh] * 3,
                          name=name, compiler_params=_cparams(("arbitrary",)))(*args)


def _ffn_fwd(h, norm_w, w_gu, w_down, tag, after_norm=None, n=None, next_norm_w=None):
    if n is None:
        n = _rms_fwd(h, norm_w, f"{tag}_norm")
    if after_norm is not None:
        after_norm(n)
    gu, a = _gu_swiglu(n, w_gu, f"{tag}_gu")
    out = _residual_matmul(a, w_down, h, 0.5, f"{tag}_down", next_norm_w)
    return out, (n, gu, a)


def _ffn_bwd(h, norm_w, w_gu, w_down, saved, dout, tag, after_dw_down=None, token_seqs=None, told=None):
    n, gu, a = saved
    dgu = _d_swiglu(dout, w_down, gu, 0.5, f"{tag}_d_gu")
    dw_down = _matmul(a, dout, mode="tn", out_dtype=F32, alpha=0.5, name=f"{tag}_dw_down")
    dw_gu = _matmul(n, dgu, mode="tn", out_dtype=F32, out_groups=N_CHIPS, name=f"{tag}_dw_gu",
                    after=after_dw_down(dw_down) if after_dw_down else None)
    if token_seqs is None:
        dh, dnw = _d_norm_in(dgu, w_gu, h, norm_w, dout, f"{tag}_d_in", after=dw_gu)
    else:
        if told is not None:
            told("dw", (dw_gu, dw_down))
        dn = _matmul(dgu, w_gu, mode="nt", out_dtype=F32, name=f"{tag}_d_norm", after=dw_gu)
        dx, dm, dnw = _rms_bwd_tokens(h, norm_w, dn, dout, token_seqs, f"{tag}_d_in",
                                      after=told("d_norm", dn) if told is not None else None)
        dh = (dx, dm)
    return dh, dnw, dw_gu, dw_down


def _split_w_in(w_in_full):
    pts = [0]
    for s in IN_SIZES:
        pts.append(pts[-1] + s)
    sl = lambda i, j: w_in_full[:, pts[i]:pts[j]]
    qfig = sl(3, 7).reshape(D_MODEL, 4, HG_HEADS, 128).transpose(0, 2, 1, 3).reshape(D_MODEL, 4 * D_MODEL)
    return {"z": sl(0, 1), "xbc": sl(1, 2), "dt": jnp.pad(sl(2, 3), ((0, 0), (0, 128 - SSD_HEADS))),
            "qfig": qfig, "gates": sl(7, 9)}


def _local_step(x, target, W):
    B, S, _ = x.shape
    T = N_META + S
    pad = (-T) % CHUNK
    Tp = T + pad
    assert pad + N_META == CHUNK
    R = B * Tp
    meta = jnp.broadcast_to(W["meta_tokens"][None], (B, N_META, D_MODEL))
    h0 = jnp.concatenate([jnp.zeros((B, pad, D_MODEL), F32), meta, x], axis=1).reshape(R, D_MODEL)

    stage = W.get("_stage", lambda name, x: {})
    W = dict(W)
    (h1, um), sv1 = _ffn_fwd(h0, W["ffn1_norm"], W["ffn1_w_gu"], W["ffn1_w_down"], "ffn1",
                             lambda n: W.update(stage("ffn1_norm", n)), next_norm_w=W["mix_norm"])
    W.update(stage("ffn1_out", h1))
    wi = W["w_in"]
    z = _matmul(um, wi["z"], mode="nn", out_dtype=BF16, name="in_z")
    xbc = _matmul(um, wi["xbc"], mode="nn", out_dtype=F32, name="in_xbc")
    dtr = _matmul(um, wi["dt"], mode="nn", out_dtype=F32, name="in_dt")
    qfig = _matmul(um, wi["qfig"], mode="nn", out_dtype=F32, name="in_qfig")
    gates = _matmul(um, wi["gates"], mode="nn", out_dtype=BF16, name="in_gates")

    r3 = lambda t: t.reshape(B, Tp, t.shape[-1])
    lane_pad = lambda t: jnp.pad(t, ((0, 0), (0, 128 - t.shape[1])))
    dt_bias, a_log, dskip = lane_pad(W["ssd_dt_bias"]), lane_pad(W["ssd_a_log"]), lane_pad(W["ssd_d"])
    xact = _conv_fwd(r3(xbc), W["ssd_conv_w"], W["ssd_conv_b"], pad, "conv_fwd")
    ya, ssd_saved, *ssd_kept = _ssd_fwd(xact, r3(dtr), r3(z), dt_bias, a_log, dskip, W["ssd_norm"], pad, "ssd_fwd")
    lbh = W["hg_lower_bound"].reshape(2, HG_HEADS, 128).transpose(1, 0, 2)
    nwh = W["hg_norm"].reshape(HG_HEADS, 1, 128)
    yb, hg_saved, *hg_kept = _hg_fwd(r3(qfig), lbh, nwh, pad, "hg_fwd")
    ya2, yb2 = ya.reshape(R, -1), yb.reshape(R, -1)
    W.update(stage("mixers_out", yb2))
    pa, pb, mg = _branch_merge(ya2, yb2, W["w_branch_a"], W["w_branch_b"], gates, "branch_merge")
    h2, n2 = _residual_matmul(mg, W["w_out"], h1, 1.0, "mix_out", W["ffn2_norm"])
    h3, sv2 = _ffn_fwd(h2, W["ffn2_norm"], W["ffn2_w_gu"], W["ffn2_w_down"], "ffn2", n=n2)

    loss, dh3, d_final = _loss_head(h3, W["final_norm"].reshape(1, D_MODEL), target, B, "loss_head")

    G = {"final_norm": d_final[0]}
    dh2, dnw, G["ffn2_w_gu"], G["ffn2_w_down"] = _ffn_bwd(h2, W["ffn2_norm"], W["ffn2_w_gu"], W["ffn2_w_down"], sv2, dh3, "ffn2")
    G["ffn2_norm"] = dnw[0:1]
    dmg = _matmul(dh2, W["w_out"], mode="nt", out_dtype=BF16, name="d_merge")
    G["w_out"] = _matmul(mg, dh2, mode="tn", out_dtype=F32, name="dw_out")
    dpa, dpb, dgates, dya, dyb = _branch_merge_bwd(pa, pb, gates, dmg, W["w_branch_a"], W["w_branch_b"], "branch_merge_bwd")
    G["w_branch_a"] = _matmul(ya2, dpa, mode="tn", out_dtype=F32, name="dw_branch_a")
    G["w_branch_b"] = _matmul(yb2, dpb, mode="tn", out_dtype=F32, name="dw_branch_b")

    dxact, ddtr, dz, dpar, dnw = _ssd_bwd(xact, r3(dtr), r3(z), dt_bias, a_log, dskip, W["ssd_norm"], ssd_saved, ssd_kept,
                                          r3(dya), pad, "ssd_bwd", after=stage("late_grads", G).get("_after"))
    G["ssd_dt_bias"], G["ssd_a_log"], G["ssd_d"] = dpar[0:1, :SSD_HEADS], dpar[1:2, :SSD_HEADS], dpar[2:3, :SSD_HEADS]
    G["ssd_norm"] = dnw[0:1]
    dxbc, dcw, dcb = _conv_bwd(r3(xbc), W["ssd_conv_w"], W["ssd_conv_b"], dxact, pad, "conv_bwd")
    G["ssd_conv_w"], G["ssd_conv_b"] = dcw[0:SSD_CONV], dcb[0:1]
    dqfig, dlb, dhn = _hg_bwd(r3(qfig), lbh, nwh, hg_saved, hg_kept, r3(dyb), pad, "hg_bwd",
                              after=stage("after_conv_bwd", dcb).get("_after"))
    G["hg_lower_bound"] = dlb[:, 0:2, :].transpose(1, 0, 2).reshape(2, D_MODEL)
    G["hg_norm"] = dhn[:, 0, :].reshape(1, D_MODEL)

    r2 = lambda t: t.reshape(R, t.shape[-1])
    pieces = [("z", r2(dz)), ("xbc", r2(dxbc)), ("dt", r2(ddtr)), ("qfig", r2(dqfig)), ("gates", dgates)]
    dum = _sum_nt([p for _, p in pieces], [wi[nm] for nm, _ in pieces], "d_mix")
    dwi = {nm: _matmul(um, dpiece, mode="tn", out_dtype=F32, name=f"dw_in_{nm}") for nm, dpiece in pieces}
    dw_qfig = dwi["qfig"].reshape(D_MODEL, HG_HEADS, 4, 128).transpose(0, 2, 1, 3).reshape(D_MODEL, 4 * D_MODEL)
    G["w_in"] = jnp.concatenate([dwi["z"], dwi["xbc"], dwi["dt"][:, :SSD_HEADS], dw_qfig, dwi["gates"]], axis=1)
    dh1, dnw = _rms_bwd(h1, W["mix_norm"], dum, dh2, "mix_norm_bwd", after=stage("w_in_grads", dwi).get("_after"))
    G["mix_norm"] = dnw[0:1]
    (dx, dfirst), dnw, G["ffn1_w_gu"], G["ffn1_w_down"] = _ffn_bwd(
        h0, W["ffn1_norm"], W["ffn1_w_gu"], W["ffn1_w_down"], sv1, dh1, "ffn1",
        lambda dw: stage("ffn1_dw_down", dw).get("_after"), token_seqs=B,
        told=lambda name, t: stage("ffn1_" + name, t).get("_after"))
    G["ffn1_norm"] = dnw[0:1]
    G["meta_tokens"] = jnp.sum(dfirst[:, pad:CHUNK], axis=0)
    return loss, dx, G


def _place():
    return lax.axis_index("x"), lax.axis_index("y"), lax.axis_index("c")


def _other_chips(x, y):
    return [(1 - x, y), (x, 1 - y), (1 - x, 1 - y)]


def _remote(src, dst, ssem, rsem, dev):
    return pltpu.make_async_remote_copy(src_ref=src, dst_ref=dst, send_sem=ssem, recv_sem=rsem,
                                        device_id=dev, device_id_type=MESH)


def _exchange8(buf, name):
    n, w = buf.shape

    def body(x_ref, out_ref, ssem, rsem):
        x, y, c = _place()
        me = 4 * x + 2 * y + c
        out_ref[me] = x_ref[...]
        copies = []
        for k in range(1, 8):
            px = 1 - x if (k >> 2) & 1 else x
            py = 1 - y if (k >> 1) & 1 else y
            pc = 1 - c if k & 1 else c
            cp = _remote(x_ref, out_ref.at[me], ssem.at[k - 1], rsem.at[k - 1], (px, py, pc))
            cp.start()
            copies.append((cp, 4 * px + 2 * py + pc))
        for k, (cp, peer) in enumerate(copies):
            _remote(x_ref, out_ref.at[peer], ssem.at[k], rsem.at[k], (x, y, c)).wait_recv()
        for cp, _ in copies:
            cp.wait_send()

    vm = pl.BlockSpec(memory_space=pltpu.VMEM)
    return pl.pallas_call(
        body, in_specs=[vm], out_specs=vm, out_shape=jax.ShapeDtypeStruct((8, n, w), F32),
        scratch_shapes=[pltpu.SemaphoreType.DMA((7,)), pltpu.SemaphoreType.DMA((7,))], name=name,
    )(buf)


HBM = pltpu.MemorySpace.HBM


def _sequencer(name, collective_id, sems, sent):
    return functools.partial(pl.kernel, mesh=plsc.ScalarSubcoreMesh(axis_name="sequencer", num_cores=1), name=name,
                             scratch_types=sems, compiler_params=pltpu.CompilerParams(collective_id=collective_id),
                             cost_estimate=pl.CostEstimate(flops=0, transcendentals=0, bytes_accessed=2 * sent,
                                                           remote_bytes_transferred=sent))


def _nbytes(arrays):
    return sum(a.size * a.dtype.itemsize for a in arrays)


def _handshake(peers):
    barrier = pltpu.get_barrier_semaphore()
    for peer in peers:
        pl.semaphore_signal(barrier, inc=1, device_id=peer, device_id_type=MESH)
    pl.semaphore_wait(barrier, len(peers))


def _gather_seq(blocks, name, collective_id):
    n = len(blocks)
    half = [s.shape[1] // 2 for s in blocks]
    full = [jax.new_ref(b, memory_space=HBM) for b in blocks]

    @_sequencer(name, collective_id, [pltpu.SemaphoreType.DMA((n, 3))] * 4, _nbytes(blocks) * 3 // 4)
    def launch(ssem, rsem, fssem, frsem):
        x, y, c = _place()
        q = 2 * x + y
        chips = _other_chips(x, y)
        _handshake([(px, py, c) for px, py in chips] + [(x, y, 1 - c)])
        piece = lambda s, qq, cc: full[s].at[qq, pl.ds(cc * half[s], half[s])]
        sends = []
        for j, (px, py) in enumerate(chips):
            for s in range(n):
                cp = _remote(piece(s, q, c), piece(s, q, c), ssem.at[s, j], rsem.at[s, j], (px, py, c))
                cp.start()
                sends.append(cp)
        for j, (px, py) in enumerate(chips):
            for s in range(n):
                got = piece(s, 2 * px + py, c)
                _remote(got, got, ssem.at[s, j], rsem.at[s, j], (px, py, c)).wait_recv()
                cp = _remote(got, got, fssem.at[s, j], frsem.at[s, j], (x, y, 1 - c))
                cp.start()
                sends.append(cp)
        for j, (px, py) in enumerate(chips):
            for s in range(n):
                got = piece(s, 2 * px + py, 1 - c)
                _remote(got, got, fssem.at[s, j], frsem.at[s, j], (x, y, 1 - c)).wait_recv()
        for cp in sends:
            cp.wait_send()

    launch()
    return [r[...] for r in full]


def _share8(buf, name, collective_id):
    n, w = buf.shape
    src = jax.new_ref(buf, memory_space=HBM)
    out = jax.empty_ref(jax.ShapeDtypeStruct((8, n, w), F32), memory_space=HBM)

    @_sequencer(name, collective_id, [pltpu.SemaphoreType.DMA((7,)), pltpu.SemaphoreType.DMA((7,)), pltpu.SemaphoreType.DMA((1,))],
                7 * buf.size * 4)
    def launch(ssem, rsem, lsem):
        x, y, c = _place()
        me = 4 * x + 2 * y + c
        peers = [(1 - x if (k >> 2) & 1 else x, 1 - y if (k >> 1) & 1 else y, 1 - c if k & 1 else c) for k in range(1, 8)]
        _handshake(peers)
        mine = pltpu.make_async_copy(src, out.at[me], lsem.at[0])
        mine.start()
        sends = []
        for k, peer in enumerate(peers):
            cp = _remote(src, out.at[me], ssem.at[k], rsem.at[k], peer)
            cp.start()
            sends.append(cp)
        for k, (px, py, pc) in enumerate(peers):
            slot = out.at[4 * px + 2 * py + pc]
            _remote(slot, slot, ssem.at[k], rsem.at[k], (px, py, pc)).wait_recv()
        for cp in sends:
            cp.wait_send()
        mine.wait()

    launch()
    return out[...]


def _sum_slots(slots, name, after=None):
    _, n, w = slots.shape

    def body(s_ref, o_ref):
        acc = s_ref[0]
        for d in range(1, 8):
            acc = acc + s_ref[d]
        o_ref[...] = acc

    vm = pl.BlockSpec(memory_space=pltpu.VMEM)
    in_specs, args = [vm], [slots]
    if after is not None:
        body = _skip_ref(body, 1)
        args.append(_deps(after))
        in_specs.append(vm)
    return pl.pallas_call(body, in_specs=in_specs, out_specs=vm, out_shape=jax.ShapeDtypeStruct((n, w), F32), name=name)(*args)


def _pair_swap(parts, name, collective_id):
    n = len(parts)
    half = [p.shape[1] // 2 for p in parts]
    src = [jax.new_ref(p, memory_space=HBM) for p in parts]
    got = [jax.empty_ref(jax.ShapeDtypeStruct((p.shape[0], h, p.shape[2]), p.dtype), memory_space=HBM) for p, h in zip(parts, half)]

    @_sequencer(name, collective_id, [pltpu.SemaphoreType.DMA((n,))] * 2, _nbytes(parts) // 2)
    def launch(ssem, rsem):
        x, y, c = _place()
        _handshake([(x, y, 1 - c)])
        copies = []
        for s in range(n):
            cp = _remote(src[s].at[pl.ds(0, parts[s].shape[0]), pl.ds((1 - c) * half[s], half[s])], got[s], ssem.at[s], rsem.at[s], (x, y, 1 - c))
            cp.start()
            copies.append(cp)
        for cp in copies:
            cp.wait_recv()
        for cp in copies:
            cp.wait_send()

    launch()
    return [g[...] for g in got]


def _to_owners(sums, name, collective_id):
    n = len(sums)
    src = [jax.new_ref(s, memory_space=HBM) for s in sums]
    got = [jax.empty_ref(jax.ShapeDtypeStruct(s.shape, s.dtype), memory_space=HBM) for s in sums]

    @_sequencer(name, collective_id, [pltpu.SemaphoreType.DMA((n, 3))] * 2, _nbytes(sums) * 3 // 4)
    def launch(ssem, rsem):
        x, y, c = _place()
        q = 2 * x + y
        chips = _other_chips(x, y)
        _handshake([(px, py, c) for px, py in chips])
        sends = []
        for j, (px, py) in enumerate(chips):
            for s in range(n):
                cp = _remote(src[s].at[2 * px + py], got[s].at[q], ssem.at[s, j], rsem.at[s, j], (px, py, c))
                cp.start()
                sends.append(cp)
        for j, (px, py) in enumerate(chips):
            for s in range(n):
                slot = got[s].at[2 * px + py]
                _remote(slot, slot, ssem.at[s, j], rsem.at[s, j], (px, py, c)).wait_recv()
        for cp in sends:
            cp.wait_send()

    launch()
    return [g[...] for g in got]


def _pair_join(blocks, name, collective_id):
    n = len(blocks)
    out = [jax.new_ref(b, memory_space=HBM) for b in blocks]

    @_sequencer(name, collective_id, [pltpu.SemaphoreType.DMA((n,))] * 2, _nbytes(blocks) // 2)
    def launch(ssem, rsem):
        x, y, c = _place()
        _handshake([(x, y, 1 - c)])
        sends = []
        for s in range(n):
            h = blocks[s].shape[0] // 2
            mine = out[s].at[pl.ds(c * h, h)]
            cp = _remote(mine, mine, ssem.at[s], rsem.at[s], (x, y, 1 - c))
            cp.start()
            sends.append(cp)
        for s in range(n):
            h = blocks[s].shape[0] // 2
            theirs = out[s].at[pl.ds((1 - c) * h, h)]
            _remote(theirs, theirs, ssem.at[s], rsem.at[s], (x, y, 1 - c)).wait_recv()
        for cp in sends:
            cp.wait_send()

    launch()
    return [o[...] for o in out]


WIRE = BF16


def _row_tile(h):
    return _pick(h, (256, 368, 352, 128, 16))


def _add_pair(part, got, c, name, after=None):
    _, h, w = got.shape
    tr = _row_tile(h)
    nt = h // tr

    def body(c_ref, p_ref, g_ref, o_ref):
        o_ref[...] = (p_ref[...] + g_ref[...].astype(F32)).astype(o_ref.dtype)

    in_specs = [pl.BlockSpec((None, tr, w), lambda q, i, c_ref: (q, c_ref[0] * nt + i, 0)),
                pl.BlockSpec((None, tr, w), lambda q, i, c_ref: (q, i, 0))]
    args = [c.reshape(1).astype(jnp.int32), part, got]
    if after is not None:
        body = _skip_ref(body, len(args))
        args.append(_deps(after))
        in_specs.append(_dep_spec(args[-1]))
    return pl.pallas_call(
        body,
        grid_spec=pltpu.PrefetchScalarGridSpec(
            num_scalar_prefetch=1, grid=(got.shape[0], nt), in_specs=in_specs,
            out_specs=pl.BlockSpec((None, tr, w), lambda q, i, c_ref: (q, i, 0))),
        out_shape=jax.ShapeDtypeStruct(got.shape, WIRE), name=name,
        compiler_params=_cparams(("arbitrary", "arbitrary")),
    )(*args)


def _sum_chips(slots, sums, q, c, name, after=None):
    _, h, w = slots.shape
    tr = _row_tile(h)
    nt = h // tr

    def body(s_ref, mine_ref, a_ref, b_ref, d_ref, o_ref):
        o_ref[...] = ((mine_ref[...].astype(F32) + a_ref[...].astype(F32)) + b_ref[...].astype(F32)) + d_ref[...].astype(F32)

    slot = lambda k: pl.BlockSpec((None, tr, w), lambda i, s_ref: (s_ref[1 + k], i, 0))
    scalars = jnp.stack([c, q, (q + 1) % N_CHIPS, (q + 2) % N_CHIPS, (q + 3) % N_CHIPS]).astype(jnp.int32)
    in_specs, args = [slot(0), slot(1), slot(2), slot(3)], [scalars, sums, slots, slots, slots]
    if after is not None:
        body = _skip_ref(body, len(args))
        args.append(_deps(after))
        in_specs.append(_dep_spec(args[-1]))
    return pl.pallas_call(
        body,
        grid_spec=pltpu.PrefetchScalarGridSpec(
            num_scalar_prefetch=1, grid=(nt,), in_specs=in_specs,
            out_specs=pl.BlockSpec((tr, w), lambda i, s_ref: (s_ref[0] * nt + i, 0))),
        out_shape=jax.ShapeDtypeStruct((2 * h, w), F32), name=name,
        compiler_params=_cparams(("arbitrary",)),
    )(*args)


class _Reduce:
    def __init__(self, parts, q, c, tag, first_id, regions=None):
        self.parts, self.q, self.c, self.tag, self.first_id, self.regions = parts, q, c, tag, first_id, regions
        self.got = _pair_swap(parts, f"{tag}_pair_swap", first_id)

    def to_owners(self, after=None):
        self.sums = [_add_pair(p, g, self.c, f"{self.tag}_pair_add{i}", after)
                     for i, (p, g) in enumerate(zip(self.parts, self.got))]
        if self.regions is not None:
            self.sums = self.regions(self.sums)
        self.slots = _to_owners(self.sums, f"{self.tag}_to_owners", self.first_id + 1)
        return self.sums

    def join(self, after=None):
        blocks = [_sum_chips(sl, sm, self.q, self.c, f"{self.tag}_sum_chips{i}", after)
                  for i, (sl, sm) in enumerate(zip(self.slots, self.sums))]
        self.out = _pair_join(blocks, f"{self.tag}_pair_join", self.first_id + 2)
        return blocks


WEIGHTS = ("meta_tokens", "ffn1_norm", "ffn1_w_gu", "ffn1_w_down", "mix_norm", "w_in", "ssd_conv_w", "ssd_conv_b",
           "ssd_dt_bias", "ssd_a_log", "ssd_d", "ssd_norm", "hg_lower_bound", "hg_norm", "w_branch_a", "w_branch_b",
           "w_out", "ffn2_norm", "ffn2_w_gu", "ffn2_w_down", "final_norm")
BIG = ("ffn1_w_gu", "ffn1_w_down", "w_in", "w_branch_a", "w_branch_b", "w_out", "ffn2_w_gu", "ffn2_w_down")
SMALL = tuple(n for n in WEIGHTS if n not in BIG)


def _rows1024(a):
    flat = a.reshape(-1)
    n = -(-flat.shape[0] // 1024) * 1024
    return jnp.pad(flat, (0, n - flat.shape[0])).reshape(-1, 1024)


def kernel(x, meta_tokens, ffn1_norm, ffn1_w_gu, ffn1_w_down, mix_norm, w_in, ssd_conv_w, ssd_conv_b, ssd_dt_bias, ssd_a_log, ssd_d, ssd_norm, hg_lower_bound, hg_norm, w_branch_a, w_branch_b, w_out, ffn2_norm, ffn2_w_gu, ffn2_w_down, final_norm, loss_target, m_meta_tokens, m_ffn1_norm, m_ffn1_w_gu, m_ffn1_w_down, m_mix_norm, m_w_in, m_ssd_conv_w, m_ssd_conv_b, m_ssd_dt_bias, m_ssd_a_log, m_ssd_d, m_ssd_norm, m_hg_lower_bound, m_hg_norm, m_w_branch_a, m_w_branch_b, m_w_out, m_ffn2_norm, m_ffn2_w_gu, m_ffn2_w_down, m_final_norm, v_meta_tokens, v_ffn1_norm, v_ffn1_w_gu, v_ffn1_w_down, v_mix_norm, v_w_in, v_ssd_conv_w, v_ssd_conv_b, v_ssd_dt_bias, v_ssd_a_log, v_ssd_d, v_ssd_norm, v_hg_lower_bound, v_hg_norm, v_w_branch_a, v_w_branch_b, v_w_out, v_ffn2_norm, v_ffn2_w_gu, v_ffn2_w_down, v_final_norm):
    P = dict(zip(WEIGHTS, (meta_tokens, ffn1_norm, ffn1_w_gu, ffn1_w_down, mix_norm, w_in, ssd_conv_w, ssd_conv_b, ssd_dt_bias, ssd_a_log, ssd_d, ssd_norm, hg_lower_bound, hg_norm, w_branch_a, w_branch_b, w_out, ffn2_norm, ffn2_w_gu, ffn2_w_down, final_norm)))
    M = dict(zip(WEIGHTS, (m_meta_tokens, m_ffn1_norm, m_ffn1_w_gu, m_ffn1_w_down, m_mix_norm, m_w_in, m_ssd_conv_w, m_ssd_conv_b, m_ssd_dt_bias, m_ssd_a_log, m_ssd_d, m_ssd_norm, m_hg_lower_bound, m_hg_norm, m_w_branch_a, m_w_branch_b, m_w_out, m_ffn2_norm, m_ffn2_w_gu, m_ffn2_w_down, m_final_norm)))
    V = dict(zip(WEIGHTS, (v_meta_tokens, v_ffn1_norm, v_ffn1_w_gu, v_ffn1_w_down, v_mix_norm, v_w_in, v_ssd_conv_w, v_ssd_conv_b, v_ssd_dt_bias, v_ssd_a_log, v_ssd_d, v_ssd_norm, v_hg_lower_bound, v_hg_norm, v_w_branch_a, v_w_branch_b, v_w_out, v_ffn2_norm, v_ffn2_w_gu, v_ffn2_w_down, v_final_norm)))
    cx, cy, cc = _place()
    q = 2 * cx + cy

    mine = jnp.concatenate([meta_tokens.reshape(4, 1024), ssd_conv_w.reshape(2, 1024), jnp.zeros((2, 1024), F32)], axis=0)
    every = _exchange8(mine, "gather_small")
    meta_full = jnp.concatenate([every[2 * k, 0:4].reshape(N_META, 256) for k in range(N_CHIPS)], axis=1)
    conv_w_full = jnp.concatenate([every[2 * k, 4:6].reshape(SSD_CONV, 512) for k in range(N_CHIPS)], axis=1)

    late = ("ffn2_w_down", "w_branch_a", "w_branch_b", "w_out")
    rows = jnp.concatenate([P[n][0] for n in late], axis=0)
    zero = lambda t, dtype=F32: (t[0:1, 0:1] * 0).astype(dtype)

    def in_slot(s, after=None):
        s = s if after is None else s + zero(after)
        return lax.dynamic_update_slice(lax.empty((N_CHIPS,) + s.shape, BF16), s.astype(BF16)[None], (q, 0, 0))

    gu1, down1 = _gather_seq([in_slot(ffn1_w_gu[0]), in_slot(ffn1_w_down[0])], "gather_ffn1", 1)
    W = {n: P[n] for n in SMALL}
    W["meta_tokens"], W["ssd_conv_w"] = meta_full, conv_w_full
    W["ffn1_w_gu"], W["ffn1_w_down"] = gu1, down1.reshape(-1, D_MODEL)
    flying = {}

    def stage(name, t):
        if name == "ffn1_norm":
            flying["w_in"] = _gather_seq([in_slot(w_in[0], t)], "gather_w_in", 2)
            return {}
        if name == "ffn1_out":
            flying["late"] = _gather_seq([in_slot(ffn2_w_gu[0], t), in_slot(rows, t)], "gather_late", 3)
            (w_in_all,) = flying["w_in"]
            w_in_all = w_in_all + zero(t, BF16)
            return {"w_in": _split_w_in(w_in_all.transpose(1, 0, 2).reshape(D_MODEL, -1))}
        if name == "mixers_out":
            gu2, rows_all = flying["late"]
            out, r = {"ffn2_w_gu": gu2}, 0
            for n in late:
                nr = P[n].shape[1]
                out[n] = (rows_all[:, r:r + nr] + zero(t, BF16)).reshape(N_CHIPS * nr, D_MODEL)
                r += nr
            return out
        if name == "late_grads":
            parts = [t["ffn2_w_gu"]] + [t[n].reshape(N_CHIPS, -1, D_MODEL) for n in late]
            flying["grad_late"] = _Reduce(parts, q, cc, "grad_late", 4)
            return {"_after": [t["ffn2_w_gu"]] + [t[n] for n in late]}
        if name == "after_conv_bwd":
            return {"_after": flying["grad_late"].to_owners(after=t)}
        if name == "w_in_grads":
            order = ("z", "xbc", "dt", "qfig", "gates")
            blocks = flying["grad_late"].join(after=[t[k] for k in order])

            def regions(sums):
                z, xbc, dt, qfig, gates = [s[0] for s in sums]
                h = z.shape[0]
                qfig = qfig.reshape(h, HG_HEADS, 4, 128).transpose(0, 2, 1, 3).reshape(h, 4 * D_MODEL)
                cols = jnp.concatenate([z, xbc, dt[:, :SSD_HEADS], qfig, gates], axis=1)
                return [cols.reshape(h, N_CHIPS, -1).transpose(1, 0, 2)]

            flying["grad_w_in"] = _Reduce([t[k][None] for k in order], q, cc, "grad_w_in", 7, regions)
            return {"_after": blocks}
        if name == "ffn1_dw_down":
            return {"_after": flying["grad_w_in"].to_owners(after=t)}
        if name == "ffn1_dw":
            dw_gu, dw_down = t
            flying["grad_ffn1"] = _Reduce([dw_gu, dw_down.reshape(N_CHIPS, -1, D_MODEL)], q, cc, "grad_ffn1", 10)
            return {}
        if name == "ffn1_d_norm":
            blocks = flying["grad_w_in"].join(after=t)
            return {"_after": flying["grad_ffn1"].to_owners(after=blocks)}
        return {}

    W["_stage"] = stage

    loss8, grad_x, G = _local_step(x, loss_target, W)

    small = jnp.concatenate(
        [G["meta_tokens"]] + [_rows1024(G[n]) for n in SMALL if n != "meta_tokens"] + [_rows1024(loss8[0:1, 0:1])], axis=0)
    small = jnp.pad(small, ((0, 40 - small.shape[0]), (0, 0)))
    small_slots = _share8(small, "share_small", 13)

    grad_ffn1 = flying["grad_ffn1"]
    going = grad_ffn1.sums
    (g_w_in,) = flying["grad_w_in"].out
    Gb = dict(zip(("ffn2_w_gu",) + late, flying["grad_late"].out))
    Gb["w_in"] = g_w_in

    grads, delta, new_m, new_v, done = {}, {}, {}, {}, []
    cols = w_in.shape[2]
    to_tiles = lambda a: a.transpose(2, 0, 1).reshape(cols, 8, 128).reshape(cols * 8, 128)
    from_tiles = lambda a: a.reshape(cols, 1, D_MODEL).transpose(1, 2, 0)
    for n in [n for n in BIG if n in Gb]:
        if n == "w_in":
            g_t = to_tiles(Gb[n][None])
            d_, m_, v_ = _adamw(to_tiles(P[n]), g_t, to_tiles(M[n]), to_tiles(V[n]), f"adamw_{n}", after=going)
            grads[n], delta[n], new_m[n], new_v[n] = from_tiles(g_t), from_tiles(d_), from_tiles(m_), from_tiles(v_)
        else:
            d_, m_, v_ = _adamw(P[n][0], Gb[n], M[n][0], V[n][0], f"adamw_{n}", after=going)
            grads[n], delta[n], new_m[n], new_v[n] = Gb[n][None], d_[None], m_[None], v_[None]
        done.append(d_)

    small = _sum_slots(small_slots, "sum_small", after=done)
    Gs = {"meta_tokens": small[0:N_META]}
    r = N_META
    for n in SMALL:
        if n == "meta_tokens":
            continue
        nr = -(-G[n].size // 1024)
        Gs[n] = small[r:r + nr].reshape(-1)[:G[n].size].reshape(G[n].shape)
        r += nr
    loss = small[r, 0]
    Gs["meta_tokens"] = lax.dynamic_slice(Gs["meta_tokens"], (0, 256 * q), (N_META, 256))
    Gs["ssd_conv_w"] = lax.dynamic_slice(Gs["ssd_conv_w"], (0, 512 * q), (SSD_CONV, 512))[None]
    Gs = {n: Gs[n].reshape(P[n].shape) for n in SMALL}
    grads.update(Gs)
    flat = lambda a: a.reshape(-1, a.shape[-1])
    d_s, m_s, v_s = _adamw_many(*[[flat(D[n]) for n in SMALL] for D in (P, Gs, M, V)], "adamw_small")
    for i, n in enumerate(SMALL):
        delta[n], new_m[n], new_v[n] = d_s[i].reshape(P[n].shape), m_s[i].reshape(P[n].shape), v_s[i].reshape(P[n].shape)
    done.append(d_s[0])
    grad_ffn1.join(after=done)
    Gb["ffn1_w_gu"], Gb["ffn1_w_down"] = grad_ffn1.out
    for n in ("ffn1_w_gu", "ffn1_w_down"):
        d_, m_, v_ = _adamw(P[n][0], Gb[n], M[n][0], V[n][0], f"adamw_{n}")
        grads[n], delta[n], new_m[n], new_v[n] = Gb[n][None], d_[None], m_[None], v_[None]
    return (loss, grad_x, *[grads[n] for n in WEIGHTS], *[delta[n] for n in WEIGHTS],
            *[new_m[n] for n in WEIGHTS], *[new_v[n] for n in WEIGHTS])
```
